```python
import jax, jax.numpy as jnp
from jax import lax
import numpy as np

D_MODEL = 1024
BATCH = 8
SEQ = 2048
DEPTH = 1

N_Q_HEADS = 16
N_KV_HEADS = 4
HEAD_DIM = 64
WINDOW = 128
ATTN_BLOCK = WINDOW
SSD_EXPAND = 2
D_INNER = SSD_EXPAND * D_MODEL
SSD_HEAD_DIM = 64
N_SSD_HEADS = D_INNER // SSD_HEAD_DIM
N_SSD_GROUPS = 4
D_STATE = 128
SSD_CONV = 4
CHUNK = 128
D_FF = 2816
FFN_CONV = 3
EPS = 1e-5
NEG = -1e30

Q_DIM = N_Q_HEADS * HEAD_DIM
KV_DIM = N_KV_HEADS * HEAD_DIM
BC_DIM = N_SSD_GROUPS * D_STATE
XBC_DIM = D_INNER + 2 * BC_DIM
IN_SPLITS = (Q_DIM, KV_DIM, KV_DIM, D_INNER, XBC_DIM, N_SSD_HEADS, D_MODEL, D_MODEL)
IN_DIM = sum(IN_SPLITS)

kernel_name = "hybrid_swa_sink_ssd_convffn"


def _split(t, sizes):
    idx = np.cumsum(np.array(sizes))[:-1].tolist()
    return jnp.split(t, idx, axis=-1)


def rmsnorm(x, w):
    xf = x.astype(jnp.float32)
    y = xf * lax.rsqrt(jnp.mean(xf * xf, axis=-1, keepdims=True) + EPS)
    return (y * w.astype(jnp.float32)).astype(x.dtype)


def causal_dwconv(x, w, b):
    K = w.shape[0]
    S = x.shape[1]
    xp = jnp.pad(x, ((0, 0), (K - 1, 0), (0, 0)))
    y = xp[:, 0:S] * w[0]
    for k in range(1, K):
        y = y + xp[:, k:k + S] * w[k]
    return y + b


def banded_sink_attention(q, k, v, sinks):
    Bsz, S, _ = q.shape
    W = ATTN_BLOCK
    nb = S // W
    G = N_Q_HEADS // N_KV_HEADS
    qb = q.reshape(Bsz, nb, W, N_KV_HEADS, G, HEAD_DIM)

    def band(t):
        t = t.reshape(Bsz, S, N_KV_HEADS, HEAD_DIM)
        tp = jnp.pad(t, ((0, 0), (W, 0), (0, 0), (0, 0)))
        prev = tp[:, :S].reshape(Bsz, nb, W, N_KV_HEADS, HEAD_DIM)
        cur = t.reshape(Bsz, nb, W, N_KV_HEADS, HEAD_DIM)
        return jnp.concatenate([prev, cur], axis=2)

    kb, vb = band(k), band(v)
    scores = jnp.einsum('bnqhgd,bnshd->bnhgqs', qb, kb).astype(jnp.float32) * (HEAD_DIM ** -0.5)
    qi = jnp.arange(W)[:, None]
    si = jnp.arange(2 * W)[None, :]
    dist = W + qi - si
    kpos = jnp.arange(nb)[:, None, None] * W - W + si[None]
    valid = (dist >= 0)[None] & (dist < WINDOW)[None] & (kpos >= 0)
    valid = valid[None, :, None, None]
    scores = jnp.where(valid, scores, NEG)
    sink = sinks.astype(jnp.float32).reshape(N_KV_HEADS, G)[None, None, :, :, None]
    m = jnp.maximum(scores.max(axis=-1), sink)
    p = jnp.where(valid, jnp.exp(scores - m[..., None]), 0.0)
    denom = p.sum(axis=-1) + jnp.exp(sink - m)
    probs = (p / denom[..., None]).astype(v.dtype)
    out = jnp.einsum('bnhgqs,bnshd->bnqhgd', probs, vb)
    return out.reshape(Bsz, S, Q_DIM)


def ssd_chunked(xs, dt, a_log, bmat, cmat, d_skip):
    Bsz, S, H, P = xs.shape
    G, N = bmat.shape[2], bmat.shape[3]
    J = H // G
    L = CHUNK
    nc = S // L
    A = -jnp.exp(a_log.astype(jnp.float32))
    dA = (dt * A).reshape(Bsz, nc, L, G, J)
    xf = xs.astype(jnp.float32)
    xc = (xf * dt[..., None]).reshape(Bsz, nc, L, G, J, P)
    bc = bmat.astype(jnp.float32).reshape(Bsz, nc, L, G, N)
    cc = cmat.astype(jnp.float32).reshape(Bsz, nc, L, G, N)
    a_cs = jnp.cumsum(dA, axis=2)
    seg = a_cs[:, :, :, None] - a_cs[:, :, None, :]
    causal = jnp.tril(jnp.ones((L, L), dtype=bool))[:, :, None, None]
    decay = jnp.where(causal, jnp.exp(jnp.where(causal, seg, 0.0)), 0.0)
    cb = jnp.einsum('bclgn,bcsgn->bclsg', cc, bc)
    y_diag = jnp.einsum('bclsgj,bcsgjp->bclgjp', cb[..., None] * decay, xc)
    decay_states = jnp.exp(a_cs[:, :, -1:] - a_cs)
    states = jnp.einsum('bclgn,bclgjp->bcgjpn', bc, xc * decay_states[..., None])
    chunk_decay = jnp.exp(a_cs[:, :, -1])

    def step(h, inp):
        st, dec = inp
        h_new = h * dec[..., None, None] + st
        return h_new, h

    h0 = jnp.zeros((Bsz, G, J, P, N), jnp.float32)
    _, prev = lax.scan(step, h0, (jnp.swapaxes(states, 0, 1), jnp.swapaxes(chunk_decay, 0, 1)))
    prev = jnp.swapaxes(prev, 0, 1)
    y_off = jnp.einsum('bclgn,bcgjpn->bclgjp', cc, prev) * jnp.exp(a_cs)[..., None]
    y = (y_diag + y_off).reshape(Bsz, S, H, P)
    return y + xf * d_skip.astype(jnp.float32)[:, None]


def _fwd_setup_inputs(seed: int = 0) -> dict:
    key = jax.random.key(seed)
    ks = jax.random.split(key, 24)
    f32 = jnp.float32
    nrm = lambda k, shape, scale: jax.random.normal(k, shape, f32) * scale
    dt0 = jnp.exp(jax.random.uniform(ks[9], (DEPTH, N_SSD_HEADS), f32,
                                     jnp.log(0.001), jnp.log(0.1)))
    return {
        "x": nrm(ks[0], (BATCH, SEQ, D_MODEL), 1.0),
        "norm1_w": 1.0 + nrm(ks[1], (DEPTH, D_MODEL), 0.02),
        "w_in": nrm(ks[2], (DEPTH, D_MODEL, IN_DIM), D_MODEL ** -0.5),
        "b_gate": nrm(ks[3], (DEPTH, 2 * D_MODEL), 0.02),
        "attn_sinks": nrm(ks[4], (DEPTH, N_Q_HEADS), 0.5),
        "w_attn_o": nrm(ks[5], (DEPTH, Q_DIM, D_MODEL), Q_DIM ** -0.5),
        "ssd_conv_w": nrm(ks[6], (DEPTH, SSD_CONV, XBC_DIM), SSD_CONV ** -0.5),
        "ssd_conv_b": nrm(ks[7], (DEPTH, XBC_DIM), 0.02),
        "dt_bias": dt0 + jnp.log(-jnp.expm1(-dt0)),
        "a_log": jnp.log(jax.random.uniform(ks[10], (DEPTH, N_SSD_HEADS), f32, 1.0, 16.0)),
        "d_skip": 1.0 + nrm(ks[11], (DEPTH, N_SSD_HEADS), 0.02),
        "ssd_norm_w": 1.0 + nrm(ks[12], (DEPTH, D_INNER), 0.02),
        "w_ssd_o": nrm(ks[13], (DEPTH, D_INNER, D_MODEL), D_INNER ** -0.5),
        "w_out": nrm(ks[14], (DEPTH, D_MODEL, D_MODEL), D_MODEL ** -0.5),
        "norm2_w": 1.0 + nrm(ks[15], (DEPTH, D_MODEL), 0.02),
        "w_up": nrm(ks[16], (DEPTH, D_MODEL, 2 * D_FF), D_MODEL ** -0.5),
        "ffn_conv_w": nrm(ks[17], (DEPTH, FFN_CONV, 2 * D_FF), FFN_CONV ** -0.5),
        "ffn_conv_b": nrm(ks[18], (DEPTH, 2 * D_FF), 0.02),
        "w_down": nrm(ks[19], (DEPTH, D_FF, D_MODEL), D_FF ** -0.5),
        "final_norm_w": 1.0 + nrm(ks[20], (D_MODEL,), 0.02),
    }


def _fwd_reference(x, norm1_w, w_in, b_gate, attn_sinks, w_attn_o, ssd_conv_w, ssd_conv_b, dt_bias,
              a_log, d_skip, ssd_norm_w, w_ssd_o, w_out, norm2_w, w_up, ffn_conv_w, ffn_conv_b,
              w_down, final_norm_w):
    Bsz, S, _ = x.shape
    h = x
    for layer in range(DEPTH):
        xn = rmsnorm(h, norm1_w[layer])
        proj = xn @ w_in[layer]
        q, k, v, z, xbc, dt_raw, ga_raw, gs_raw = _split(proj, IN_SPLITS)
        ba, bs = _split(b_gate[layer], (D_MODEL, D_MODEL))
        gate_a = jax.nn.sigmoid(ga_raw + ba)
        gate_s = jax.nn.sigmoid(gs_raw + bs)
        attn = banded_sink_attention(q, k, v, attn_sinks[layer]) @ w_attn_o[layer]
        xbc = jax.nn.silu(causal_dwconv(xbc, ssd_conv_w[layer], ssd_conv_b[layer]))
        xs, bm, cm = _split(xbc, (D_INNER, BC_DIM, BC_DIM))
        dt = jax.nn.softplus(dt_raw.astype(jnp.float32) + dt_bias[layer].astype(jnp.float32))
        y = ssd_chunked(xs.reshape(Bsz, S, N_SSD_HEADS, SSD_HEAD_DIM), dt, a_log[layer],
                        bm.reshape(Bsz, S, N_SSD_GROUPS, D_STATE),
                        cm.reshape(Bsz, S, N_SSD_GROUPS, D_STATE), d_skip[layer])
        y = y.reshape(Bsz, S, D_INNER) * jax.nn.silu(z.astype(jnp.float32))
        yg = y.reshape(Bsz, S, N_SSD_GROUPS, D_INNER // N_SSD_GROUPS)
        yg = yg * lax.rsqrt(jnp.mean(yg * yg, axis=-1, keepdims=True) + EPS)
        y = (yg.reshape(Bsz, S, D_INNER) * ssd_norm_w[layer].astype(jnp.float32)).astype(x.dtype)
        ssd_out = y @ w_ssd_o[layer]
        mix = (gate_a * attn + gate_s * ssd_out) @ w_out[layer]
        h = h + mix.astype(h.dtype)
        hn = rmsnorm(h, norm2_w[layer])
        u = causal_dwconv(hn @ w_up[layer], ffn_conv_w[layer], ffn_conv_b[layer])
        val, gt = _split(u, (D_FF, D_FF))
        h = h + ((jax.nn.silu(gt) * val) @ w_down[layer]).astype(h.dtype)
    return rmsnorm(h, final_norm_w)


import jax as _jax
import jax.numpy as _jnp

TWIN_FORMAT = 'train_step'
FWD_PARAMS = ['x', 'norm1_w', 'w_in', 'b_gate', 'attn_sinks', 'w_attn_o', 'ssd_conv_w', 'ssd_conv_b', 'dt_bias', 'a_log', 'd_skip', 'ssd_norm_w', 'w_ssd_o', 'w_out', 'norm2_w', 'w_up', 'ffn_conv_w', 'ffn_conv_b', 'w_down', 'final_norm_w']
TWIN_WEIGHTS = ['norm1_w', 'w_in', 'b_gate', 'attn_sinks', 'w_attn_o', 'ssd_conv_w', 'ssd_conv_b', 'dt_bias', 'a_log', 'd_skip', 'ssd_norm_w', 'w_ssd_o', 'w_out', 'norm2_w', 'w_up', 'ffn_conv_w', 'ffn_conv_b', 'w_down', 'final_norm_w']
TWIN_DIFF_INPUT = 'x'
TWIN_INPUTS = ['x', 'norm1_w', 'w_in', 'b_gate', 'attn_sinks', 'w_attn_o', 'ssd_conv_w', 'ssd_conv_b', 'dt_bias', 'a_log', 'd_skip', 'ssd_norm_w', 'w_ssd_o', 'w_out', 'norm2_w', 'w_up', 'ffn_conv_w', 'ffn_conv_b', 'w_down', 'final_norm_w', 'loss_target', 'm_norm1_w', 'm_w_in', 'm_b_gate', 'm_attn_sinks', 'm_w_attn_o', 'm_ssd_conv_w', 'm_ssd_conv_b', 'm_dt_bias', 'm_a_log', 'm_d_skip', 'm_ssd_norm_w', 'm_w_ssd_o', 'm_w_out', 'm_norm2_w', 'm_w_up', 'm_ffn_conv_w', 'm_ffn_conv_b', 'm_w_down', 'm_final_norm_w', 'v_norm1_w', 'v_w_in', 'v_b_gate', 'v_attn_sinks', 'v_w_attn_o', 'v_ssd_conv_w', 'v_ssd_conv_b', 'v_dt_bias', 'v_a_log', 'v_d_skip', 'v_ssd_norm_w', 'v_w_ssd_o', 'v_w_out', 'v_norm2_w', 'v_w_up', 'v_ffn_conv_w', 'v_ffn_conv_b', 'v_w_down', 'v_final_norm_w']
TWIN_OUTPUTS = ['loss', 'grad_x', 'grad_norm1_w', 'grad_w_in', 'grad_b_gate', 'grad_attn_sinks', 'grad_w_attn_o', 'grad_ssd_conv_w', 'grad_ssd_conv_b', 'grad_dt_bias', 'grad_a_log', 'grad_d_skip', 'grad_ssd_norm_w', 'grad_w_ssd_o', 'grad_w_out', 'grad_norm2_w', 'grad_w_up', 'grad_ffn_conv_w', 'grad_ffn_conv_b', 'grad_w_down', 'grad_final_norm_w', 'delta_norm1_w', 'delta_w_in', 'delta_b_gate', 'delta_attn_sinks', 'delta_w_attn_o', 'delta_ssd_conv_w', 'delta_ssd_conv_b', 'delta_dt_bias', 'delta_a_log', 'delta_d_skip', 'delta_ssd_norm_w', 'delta_w_ssd_o', 'delta_w_out', 'delta_norm2_w', 'delta_w_up', 'delta_ffn_conv_w', 'delta_ffn_conv_b', 'delta_w_down', 'delta_final_norm_w', 'new_m_norm1_w', 'new_m_w_in', 'new_m_b_gate', 'new_m_attn_sinks', 'new_m_w_attn_o', 'new_m_ssd_conv_w', 'new_m_ssd_conv_b', 'new_m_dt_bias', 'new_m_a_log', 'new_m_d_skip', 'new_m_ssd_norm_w', 'new_m_w_ssd_o', 'new_m_w_out', 'new_m_norm2_w', 'new_m_w_up', 'new_m_ffn_conv_w', 'new_m_ffn_conv_b', 'new_m_w_down', 'new_m_final_norm_w', 'new_v_norm1_w', 'new_v_w_in', 'new_v_b_gate', 'new_v_attn_sinks', 'new_v_w_attn_o', 'new_v_ssd_conv_w', 'new_v_ssd_conv_b', 'new_v_dt_bias', 'new_v_a_log', 'new_v_d_skip', 'new_v_ssd_norm_w', 'new_v_w_ssd_o', 'new_v_w_out', 'new_v_norm2_w', 'new_v_w_up', 'new_v_ffn_conv_w', 'new_v_ffn_conv_b', 'new_v_w_down', 'new_v_final_norm_w']
TWIN_LEAF_KINDS = {'loss': 'loss', 'grad_x': 'grad_x', 'grad_norm1_w': 'grad_w', 'grad_w_in': 'grad_w', 'grad_b_gate': 'grad_w', 'grad_attn_sinks': 'grad_w', 'grad_w_attn_o': 'grad_w', 'grad_ssd_conv_w': 'grad_w', 'grad_ssd_conv_b': 'grad_w', 'grad_dt_bias': 'grad_w', 'grad_a_log': 'grad_w', 'grad_d_skip': 'grad_w', 'grad_ssd_norm_w': 'grad_w', 'grad_w_ssd_o': 'grad_w', 'grad_w_out': 'grad_w', 'grad_norm2_w': 'grad_w', 'grad_w_up': 'grad_w', 'grad_ffn_conv_w': 'grad_w', 'grad_ffn_conv_b': 'grad_w', 'grad_w_down': 'grad_w', 'grad_final_norm_w': 'grad_w', 'delta_norm1_w': 'delta_w', 'delta_w_in': 'delta_w', 'delta_b_gate': 'delta_w', 'delta_attn_sinks': 'delta_w', 'delta_w_attn_o': 'delta_w', 'delta_ssd_conv_w': 'delta_w', 'delta_ssd_conv_b': 'delta_w', 'delta_dt_bias': 'delta_w', 'delta_a_log': 'delta_w', 'delta_d_skip': 'delta_w', 'delta_ssd_norm_w': 'delta_w', 'delta_w_ssd_o': 'delta_w', 'delta_w_out': 'delta_w', 'delta_norm2_w': 'delta_w', 'delta_w_up': 'delta_w', 'delta_ffn_conv_w': 'delta_w', 'delta_ffn_conv_b': 'delta_w', 'delta_w_down': 'delta_w', 'delta_final_norm_w': 'delta_w', 'new_m_norm1_w': 'new_m', 'new_m_w_in': 'new_m', 'new_m_b_gate': 'new_m', 'new_m_attn_sinks': 'new_m', 'new_m_w_attn_o': 'new_m', 'new_m_ssd_conv_w': 'new_m', 'new_m_ssd_conv_b': 'new_m', 'new_m_dt_bias': 'new_m', 'new_m_a_log': 'new_m', 'new_m_d_skip': 'new_m', 'new_m_ssd_norm_w': 'new_m', 'new_m_w_ssd_o': 'new_m', 'new_m_w_out': 'new_m', 'new_m_norm2_w': 'new_m', 'new_m_w_up': 'new_m', 'new_m_ffn_conv_w': 'new_m', 'new_m_ffn_conv_b': 'new_m', 'new_m_w_down': 'new_m', 'new_m_final_norm_w': 'new_m', 'new_v_norm1_w': 'new_v', 'new_v_w_in': 'new_v', 'new_v_b_gate': 'new_v', 'new_v_attn_sinks': 'new_v', 'new_v_w_attn_o': 'new_v', 'new_v_ssd_conv_w': 'new_v', 'new_v_ssd_conv_b': 'new_v', 'new_v_dt_bias': 'new_v', 'new_v_a_log': 'new_v', 'new_v_d_skip': 'new_v', 'new_v_ssd_norm_w': 'new_v', 'new_v_w_ssd_o': 'new_v', 'new_v_w_out': 'new_v', 'new_v_norm2_w': 'new_v', 'new_v_w_up': 'new_v', 'new_v_ffn_conv_w': 'new_v', 'new_v_ffn_conv_b': 'new_v', 'new_v_w_down': 'new_v', 'new_v_final_norm_w': 'new_v'}


def _forward(args):
    return _fwd_reference(*[args[k] for k in FWD_PARAMS])


def _output_shape():
    out = _jax.eval_shape(lambda: _forward(_fwd_setup_inputs(0)))
    return out.shape, out.dtype

N_MICROBATCH = 1
ADAM_LR = 0.001
ADAM_B1 = 0.9
ADAM_B2 = 0.999
ADAM_EPS = 1e-08
ADAM_WD = 0.01
ADAM_STEP = 10
PER_EXAMPLE_BATCH_AXIS = {'x': 0, 'loss_target': 0}
SHARED_INPUTS = []
_WEIGHT_DTYPES = {'norm1_w': _jnp.float32, 'w_in': _jnp.float32, 'b_gate': _jnp.float32, 'attn_sinks': _jnp.float32, 'w_attn_o': _jnp.float32, 'ssd_conv_w': _jnp.float32, 'ssd_conv_b': _jnp.float32, 'dt_bias': _jnp.float32, 'a_log': _jnp.float32, 'd_skip': _jnp.float32, 'ssd_norm_w': _jnp.float32, 'w_ssd_o': _jnp.float32, 'w_out': _jnp.float32, 'norm2_w': _jnp.float32, 'w_up': _jnp.float32, 'ffn_conv_w': _jnp.float32, 'ffn_conv_b': _jnp.float32, 'w_down': _jnp.float32, 'final_norm_w': _jnp.float32}
MOMENT_SCALE = {'norm1_w': 1.097840e-01, 'w_in': 3.586788e-02, 'b_gate': 1.916157e-02, 'attn_sinks': 9.844976e-03, 'w_attn_o': 1.366559e-02, 'ssd_conv_w': 4.025473e-02, 'ssd_conv_b': 5.421314e-02, 'dt_bias': 1.078865e-01, 'a_log': 1.275637e-01, 'd_skip': 3.782067e-01, 'ssd_norm_w': 4.689499e-02, 'w_ssd_o': 6.638150e-02, 'w_out': 6.710271e-02, 'norm2_w': 8.499149e-02, 'w_up': 3.586790e-02, 'ffn_conv_w': 3.627830e-02, 'ffn_conv_b': 3.635531e-02, 'w_down': 5.878778e-02, 'final_norm_w': 1.602150e+01}


def _to_microbatches(a, axis):
    t = _jnp.moveaxis(a, axis, 0)
    t = t.reshape((N_MICROBATCH, t.shape[0] // N_MICROBATCH) + t.shape[1:])
    return _jnp.moveaxis(t, 1, axis + 1)


def setup_inputs(seed: int = 0) -> dict:
    inp = _fwd_setup_inputs(seed)
    key = _jax.random.fold_in(_jax.random.key(seed), 7919)
    shape, _ = _output_shape()
    out = dict(inp)
    out["loss_target"] = _jax.random.normal(_jax.random.fold_in(key, 0), shape, _jnp.float32)
    for i, name in enumerate(TWIN_WEIGHTS):
        w = inp[name].astype(_jnp.float32)
        if MOMENT_SCALE is None:
            s = _jnp.sqrt(_jnp.mean(_jnp.square(w)) + 1e-30)
        else:
            s = MOMENT_SCALE[name]
        km, kv = _jax.random.split(_jax.random.fold_in(key, i + 1))
        out[name] = w
        out["m_" + name] = s * _jax.random.normal(km, w.shape, _jnp.float32)
        out["v_" + name] = (s * s) * _jax.random.uniform(kv, w.shape, _jnp.float32, 0.5, 1.5)
    if N_MICROBATCH > 1:
        for name, axis in PER_EXAMPLE_BATCH_AXIS.items():
            out[name] = _to_microbatches(out[name], axis)
    return {'x': out['x'], 'norm1_w': out['norm1_w'], 'w_in': out['w_in'], 'b_gate': out['b_gate'], 'attn_sinks': out['attn_sinks'], 'w_attn_o': out['w_attn_o'], 'ssd_conv_w': out['ssd_conv_w'], 'ssd_conv_b': out['ssd_conv_b'], 'dt_bias': out['dt_bias'], 'a_log': out['a_log'], 'd_skip': out['d_skip'], 'ssd_norm_w': out['ssd_norm_w'], 'w_ssd_o': out['w_ssd_o'], 'w_out': out['w_out'], 'norm2_w': out['norm2_w'], 'w_up': out['w_up'], 'ffn_conv_w': out['ffn_conv_w'], 'ffn_conv_b': out['ffn_conv_b'], 'w_down': out['w_down'], 'final_norm_w': out['final_norm_w'], 'loss_target': out['loss_target'], 'm_norm1_w': out['m_norm1_w'], 'm_w_in': out['m_w_in'], 'm_b_gate': out['m_b_gate'], 'm_attn_sinks': out['m_attn_sinks'], 'm_w_attn_o': out['m_w_attn_o'], 'm_ssd_conv_w': out['m_ssd_conv_w'], 'm_ssd_conv_b': out['m_ssd_conv_b'], 'm_dt_bias': out['m_dt_bias'], 'm_a_log': out['m_a_log'], 'm_d_skip': out['m_d_skip'], 'm_ssd_norm_w': out['m_ssd_norm_w'], 'm_w_ssd_o': out['m_w_ssd_o'], 'm_w_out': out['m_w_out'], 'm_norm2_w': out['m_norm2_w'], 'm_w_up': out['m_w_up'], 'm_ffn_conv_w': out['m_ffn_conv_w'], 'm_ffn_conv_b': out['m_ffn_conv_b'], 'm_w_down': out['m_w_down'], 'm_final_norm_w': out['m_final_norm_w'], 'v_norm1_w': out['v_norm1_w'], 'v_w_in': out['v_w_in'], 'v_b_gate': out['v_b_gate'], 'v_attn_sinks': out['v_attn_sinks'], 'v_w_attn_o': out['v_w_attn_o'], 'v_ssd_conv_w': out['v_ssd_conv_w'], 'v_ssd_conv_b': out['v_ssd_conv_b'], 'v_dt_bias': out['v_dt_bias'], 'v_a_log': out['v_a_log'], 'v_d_skip': out['v_d_skip'], 'v_ssd_norm_w': out['v_ssd_norm_w'], 'v_w_ssd_o': out['v_w_ssd_o'], 'v_w_out': out['v_w_out'], 'v_norm2_w': out['v_norm2_w'], 'v_w_up': out['v_w_up'], 'v_ffn_conv_w': out['v_ffn_conv_w'], 'v_ffn_conv_b': out['v_ffn_conv_b'], 'v_w_down': out['v_w_down'], 'v_final_norm_w': out['v_final_norm_w']}


def _loss(weights, diff, rest, loss_target):
    with _jax.named_scope("forward"):
        args = {**rest, TWIN_DIFF_INPUT: diff, **{k: w.astype(_WEIGHT_DTYPES[k]) for k, w in weights.items()}}
        y = _forward(args)
    with _jax.named_scope("loss_head"):
        err = _jnp.square(y.astype(_jnp.float32) - loss_target)
        return 0.5 * _jnp.sum(_jnp.mean(err, axis=-1)) if err.ndim else 0.5 * err


def _adamw(w, g, m, v):
    m = ADAM_B1 * m + (1.0 - ADAM_B1) * g
    v = ADAM_B2 * v + (1.0 - ADAM_B2) * _jnp.square(g)
    m_hat = m / (1.0 - ADAM_B1 ** ADAM_STEP)
    v_hat = v / (1.0 - ADAM_B2 ** ADAM_STEP)
    delta = -ADAM_LR * (m_hat / (_jnp.sqrt(v_hat) + ADAM_EPS) + ADAM_WD * w)
    return delta, m, v


def reference(x, norm1_w, w_in, b_gate, attn_sinks, w_attn_o, ssd_conv_w, ssd_conv_b, dt_bias, a_log, d_skip, ssd_norm_w, w_ssd_o, w_out, norm2_w, w_up, ffn_conv_w, ffn_conv_b, w_down, final_norm_w, loss_target, m_norm1_w, m_w_in, m_b_gate, m_attn_sinks, m_w_attn_o, m_ssd_conv_w, m_ssd_conv_b, m_dt_bias, m_a_log, m_d_skip, m_ssd_norm_w, m_w_ssd_o, m_w_out, m_norm2_w, m_w_up, m_ffn_conv_w, m_ffn_conv_b, m_w_down, m_final_norm_w, v_norm1_w, v_w_in, v_b_gate, v_attn_sinks, v_w_attn_o, v_ssd_conv_w, v_ssd_conv_b, v_dt_bias, v_a_log, v_d_skip, v_ssd_norm_w, v_w_ssd_o, v_w_out, v_norm2_w, v_w_up, v_ffn_conv_w, v_ffn_conv_b, v_w_down, v_final_norm_w):
    given = dict(x=x, norm1_w=norm1_w, w_in=w_in, b_gate=b_gate, attn_sinks=attn_sinks, w_attn_o=w_attn_o, ssd_conv_w=ssd_conv_w, ssd_conv_b=ssd_conv_b, dt_bias=dt_bias, a_log=a_log, d_skip=d_skip, ssd_norm_w=ssd_norm_w, w_ssd_o=w_ssd_o, w_out=w_out, norm2_w=norm2_w, w_up=w_up, ffn_conv_w=ffn_conv_w, ffn_conv_b=ffn_conv_b, w_down=w_down, final_norm_w=final_norm_w, loss_target=loss_target, m_norm1_w=m_norm1_w, m_w_in=m_w_in, m_b_gate=m_b_gate, m_attn_sinks=m_attn_sinks, m_w_attn_o=m_w_attn_o, m_ssd_conv_w=m_ssd_conv_w, m_ssd_conv_b=m_ssd_conv_b, m_dt_bias=m_dt_bias, m_a_log=m_a_log, m_d_skip=m_d_skip, m_ssd_norm_w=m_ssd_norm_w, m_w_ssd_o=m_w_ssd_o, m_w_out=m_w_out, m_norm2_w=m_norm2_w, m_w_up=m_w_up, m_ffn_conv_w=m_ffn_conv_w, m_ffn_conv_b=m_ffn_conv_b, m_w_down=m_w_down, m_final_norm_w=m_final_norm_w, v_norm1_w=v_norm1_w, v_w_in=v_w_in, v_b_gate=v_b_gate, v_attn_sinks=v_attn_sinks, v_w_attn_o=v_w_attn_o, v_ssd_conv_w=v_ssd_conv_w, v_ssd_conv_b=v_ssd_conv_b, v_dt_bias=v_dt_bias, v_a_log=v_a_log, v_d_skip=v_d_skip, v_ssd_norm_w=v_ssd_norm_w, v_w_ssd_o=v_w_ssd_o, v_w_out=v_w_out, v_norm2_w=v_norm2_w, v_w_up=v_w_up, v_ffn_conv_w=v_ffn_conv_w, v_ffn_conv_b=v_ffn_conv_b, v_w_down=v_w_down, v_final_norm_w=v_final_norm_w)
    weights = {n: given[n] for n in TWIN_WEIGHTS}
    shared = {n: given[n] for n in SHARED_INPUTS}
    per_example = {n: given[n] for n in ['x']}
    grad_fn = _jax.value_and_grad(_loss, argnums=(0, 1))

    def one_microbatch(ex, loss_target):
        ex = dict(ex)
        diff = ex.pop(TWIN_DIFF_INPUT)
        return grad_fn(weights, diff, {**shared, **ex}, loss_target)

    if N_MICROBATCH == 1:
        loss, (grad_w, grad_x) = one_microbatch(per_example, given["loss_target"])
    else:
        def body(carry, xs):
            loss_sum, grad_sum = carry
            l_k, (gw_k, gx_k) = one_microbatch(xs[0], xs[1])
            with _jax.named_scope("update"):
                return (loss_sum + l_k, _jax.tree.map(_jnp.add, grad_sum, gw_k)), gx_k

        init = (_jnp.zeros((), _jnp.float32), _jax.tree.map(_jnp.zeros_like, weights))
        (loss, grad_w), grad_x = _jax.lax.scan(body, init, (per_example, given["loss_target"]))
    with _jax.named_scope("update"):
        delta_w, new_m, new_v = {}, {}, {}
        for n in TWIN_WEIGHTS:
            delta_w[n], new_m[n], new_v[n] = _adamw(weights[n], grad_w[n], given["m_" + n], given["v_" + n])
    return (loss, grad_x, *[grad_w[n] for n in TWIN_WEIGHTS], *[delta_w[n] for n in TWIN_WEIGHTS],
            *[new_m[n] for n in TWIN_WEIGHTS], *[new_v[n] for n in TWIN_WEIGHTS])
```

```python
import functools

import jax
import jax.numpy as jnp
from jax import lax
from jax.experimental import pallas as pl
from jax.experimental.pallas import tpu as pltpu

F32 = jnp.float32
BF16 = jnp.bfloat16
HIGHEST = lax.Precision.HIGHEST

D_MODEL = 1024
N_Q_HEADS = 16
N_KV_HEADS = 4
HEAD_DIM = 64
WINDOW = 128
Q_PER_KV = N_Q_HEADS // N_KV_HEADS
Q_DIM = N_Q_HEADS * HEAD_DIM
KV_DIM = N_KV_HEADS * HEAD_DIM
D_INNER = 2048
SSD_HEAD_DIM = 64
N_SSD_HEADS = 32
N_SSD_GROUPS = 4
HEADS_PER_GROUP = N_SSD_HEADS // N_SSD_GROUPS
D_STATE = 128
BC_DIM = N_SSD_GROUPS * D_STATE
XBC_DIM = D_INNER + 2 * BC_DIM
SSD_CONV = 4
CHUNK = 128
D_FF = 2816
FFN_CONV = 3
EPS = 1e-5
NEG = -1e30
IN_DIM = 8736
N_DEV = 8

OFF_Q = 0
OFF_K = OFF_Q + Q_DIM
OFF_V = OFF_K + KV_DIM
OFF_Z = OFF_V + KV_DIM
OFF_X = OFF_Z + D_INNER
OFF_GA = OFF_X + XBC_DIM
OFF_GS = OFF_GA + D_MODEL
OFF_DT = OFF_GS + D_MODEL
IN_PAD = OFF_DT + 128
ORIG_DT = OFF_X + XBC_DIM
ORIG_GA = ORIG_DT + N_SSD_HEADS
ORIG_GS = ORIG_GA + D_MODEL

ADAM_LR = 0.001
ADAM_B1 = 0.9
ADAM_B2 = 0.999
ADAM_EPS = 1e-08
ADAM_WD = 0.01
ADAM_STEP = 10

VMEM_LIMIT = 48 * 1024 * 1024
MESH = pl.DeviceIdType.MESH


def _cparams(*sem):
    return pltpu.CompilerParams(dimension_semantics=sem, vmem_limit_bytes=VMEM_LIMIT)


def _tile(n, prefs):
    for p in prefs:
        if n % p == 0:
            return p
    return n


def _sigmoid(x):
    return 1.0 / (1.0 + jnp.exp(-x))


def _softplus(x):
    return jnp.maximum(x, 0.0) + jnp.log(1.0 + jnp.exp(-jnp.abs(x)))


def _rowsum(x):
    return jnp.sum(x, axis=1, keepdims=True)


def _colsum(x):
    return jnp.sum(x, axis=0, keepdims=True)


def _dot(a, b):
    return jnp.dot(a, b, preferred_element_type=F32)


def _dot_nt(a, b):
    return lax.dot_general(a, b, (((1,), (1,)), ((), ())), preferred_element_type=F32)


def _dot_tn(a, b):
    return lax.dot_general(a, b, (((0,), (0,)), ((), ())), preferred_element_type=F32)


def _shift_right(x, j):
    if j == 0:
        return x
    lane = lax.broadcasted_iota(jnp.int32, x.shape, 1)
    return jnp.where(lane >= j, pltpu.roll(x, j, 1), 0.0)


def _shift_left(x, j):
    if j == 0:
        return x
    n = x.shape[1]
    lane = lax.broadcasted_iota(jnp.int32, x.shape, 1)
    return jnp.where(lane < n - j, pltpu.roll(x, n - j, 1), 0.0)


def _matmul(a, b, *, nt, out_dtype, name, add=None):
    m, k = a.shape
    n = b.shape[0] if nt else b.shape[1]
    tm = _tile(m, (512, 384, 256, 128))
    tn = _tile(n, (1024, 512, 384, 256, 128))
    tk = _tile(k, (512, 384, 256, 128))
    nk = k // tk

    def body(a_ref, b_ref, *rest):
        if add is None:
            o_ref, acc = rest
        else:
            r_ref, o_ref, acc = rest
        kk = pl.program_id(2)

        @pl.when(kk == 0)
        def _():
            acc[...] = jnp.zeros_like(acc)

        av = a_ref[...].astype(BF16)
        bv = b_ref[...].astype(BF16)
        acc[...] += _dot_nt(av, bv) if nt else _dot(av, bv)

        @pl.when(kk == nk - 1)
        def _():
            r = acc[...]
            if add is not None:
                r = r + r_ref[...]
            o_ref[...] = r.astype(out_dtype)

    in_specs = [
        pl.BlockSpec((tm, tk), lambda i, j, kk: (i, kk)),
        pl.BlockSpec((tn, tk), lambda i, j, kk: (j, kk)) if nt else pl.BlockSpec((tk, tn), lambda i, j, kk: (kk, j)),
    ]
    args = [a, b]
    if add is not None:
        in_specs.append(pl.BlockSpec((tm, tn), lambda i, j, kk: (i, j)))
        args.append(add)
    return pl.pallas_call(
        body,
        name=name,
        grid=(m // tm, n // tn, nk),
        in_specs=in_specs,
        out_specs=pl.BlockSpec((tm, tn), lambda i, j, kk: (i, j)),
        out_shape=jax.ShapeDtypeStruct((m, n), out_dtype),
        scratch_shapes=[pltpu.VMEM((tm, tn), F32)],
        compiler_params=_cparams("parallel", "parallel", "arbitrary"),
    )(*args)


def _norm_fwd(x, w_col, name):
    f, t = x.shape
    tt = _tile(t, (512, 256, 128))

    def body(x_ref, w_ref, o_ref):
        xv = x_ref[...]
        r = lax.rsqrt(jnp.mean(xv * xv, axis=0, keepdims=True) + EPS)
        o_ref[...] = (xv * r * w_ref[...]).astype(BF16)

    return pl.pallas_call(
        body,
        name=name,
        grid=(t // tt,),
        in_specs=[pl.BlockSpec((f, tt), lambda i: (0, i)), pl.BlockSpec((f, 1), lambda i: (0, 0))],
        out_specs=pl.BlockSpec((f, tt), lambda i: (0, i)),
        out_shape=jax.ShapeDtypeStruct((f, t), BF16),
        compiler_params=_cparams("parallel"),
    )(x, w_col)


def _norm_bwd(dy, x, w_col, res, name):
    f, t = x.shape
    tt = _tile(t, (512, 256, 128))

    def body(dy_ref, x_ref, w_ref, res_ref, dx_ref, dw_ref):
        @pl.when(pl.program_id(0) == 0)
        def _():
            dw_ref[...] = jnp.zeros_like(dw_ref)

        xv = x_ref[...]
        r = lax.rsqrt(jnp.mean(xv * xv, axis=0, keepdims=True) + EPS)
        xhat = xv * r
        dyv = dy_ref[...]
        dw_ref[...] += _rowsum(dyv * xhat)
        dxhat = dyv * w_ref[...]
        dx_ref[...] = res_ref[...] + r * (dxhat - xhat * jnp.mean(dxhat * xhat, axis=0, keepdims=True))

    blk = pl.BlockSpec((f, tt), lambda i: (0, i))
    col = pl.BlockSpec((f, 1), lambda i: (0, 0))
    return pl.pallas_call(
        body,
        name=name,
        grid=(t // tt,),
        in_specs=[blk, blk, col, blk],
        out_specs=[blk, col],
        out_shape=[jax.ShapeDtypeStruct((f, t), F32), jax.ShapeDtypeStruct((f, 1), F32)],
        compiler_params=_cparams("arbitrary"),
    )(dy, x, w_col, res)


def _final_norm_loss(h, tgt, w_col):
    f, t = h.shape
    tt = _tile(t, (512, 256, 128))

    def body(h_ref, t_ref, w_ref, dh_ref, loss_ref, dw_ref):
        @pl.when(pl.program_id(0) == 0)
        def _():
            dw_ref[...] = jnp.zeros_like(dw_ref)
            loss_ref[...] = jnp.zeros_like(loss_ref)

        xv = h_ref[...]
        r = lax.rsqrt(jnp.mean(xv * xv, axis=0, keepdims=True) + EPS)
        xhat = xv * r
        wv = w_ref[...]
        err = xhat * wv - t_ref[...]
        loss_ref[...] += 0.5 * _rowsum(jnp.mean(err * err, axis=0, keepdims=True))
        dyv = err * (1.0 / f)
        dw_ref[...] += _rowsum(dyv * xhat)
        dxhat = dyv * wv
        dh_ref[...] = r * (dxhat - xhat * jnp.mean(dxhat * xhat, axis=0, keepdims=True))

    blk = pl.BlockSpec((f, tt), lambda i: (0, i))
    col = pl.BlockSpec((f, 1), lambda i: (0, 0))
    one = pl.BlockSpec((1, 1), lambda i: (0, 0))
    return pl.pallas_call(
        body,
        name="final_norm_loss",
        grid=(t // tt,),
        in_specs=[blk, blk, col],
        out_specs=[blk, one, col],
        out_shape=[jax.ShapeDtypeStruct((f, t), F32), jax.ShapeDtypeStruct((1, 1), F32), jax.ShapeDtypeStruct((f, 1), F32)],
        compiler_params=_cparams("arbitrary"),
    )(h, tgt, w_col)


def _attn_mask(n):
    shape = (2 * WINDOW, Q_PER_KV * WINDOW)
    si = lax.broadcasted_iota(jnp.int32, shape, 0)
    qi = lax.broadcasted_iota(jnp.int32, shape, 1) & (WINDOW - 1)
    dist = WINDOW + qi - si
    return (dist >= 0) & (dist < WINDOW) & ((si >= WINDOW) | (n > 0))


def _lane_cat(ref, row0, rows):
    return jnp.concatenate([ref[row0 + i * rows:row0 + (i + 1) * rows, :] for i in range(Q_PER_KV)], axis=1)


def _attn_fwd(proj, sinks):
    t = proj.shape[1]
    nb = t // WINDOW
    scale = HEAD_DIM ** -0.5

    def body(s_ref, q_ref, kc_ref, kp_ref, vc_ref, vp_ref, o_ref, lse_ref):
        n = pl.program_id(0)
        valid = _attn_mask(n)
        for g in range(N_KV_HEADS):
            rows = slice(g * HEAD_DIM, (g + 1) * HEAD_DIM)
            kt = jnp.concatenate([kp_ref[rows, :], kc_ref[rows, :]], axis=1).astype(BF16)
            vt = jnp.concatenate([vp_ref[rows, :], vc_ref[rows, :]], axis=1).astype(BF16)
            qcat = _lane_cat(q_ref, g * Q_PER_KV * HEAD_DIM, HEAD_DIM).astype(BF16)
            s = jnp.where(valid, _dot_tn(kt, qcat) * scale, NEG)
            sink = jnp.concatenate(
                [jnp.full((1, WINDOW), s_ref[g * Q_PER_KV + i], F32) for i in range(Q_PER_KV)], axis=1)
            m = jnp.maximum(jnp.max(s, axis=0, keepdims=True), sink)
            p = jnp.where(valid, jnp.exp(s - m), 0.0)
            denom = _colsum(p) + jnp.exp(sink - m)
            probs = (p / denom).astype(BF16)
            out = _dot(vt, probs)
            lse = m + jnp.log(denom)
            for i in range(Q_PER_KV):
                h = g * Q_PER_KV + i
                o_ref[h * HEAD_DIM:(h + 1) * HEAD_DIM, :] = out[:, i * WINDOW:(i + 1) * WINDOW]
                lse_ref[h:h + 1, :] = lse[:, i * WINDOW:(i + 1) * WINDOW]

    kb = OFF_K // KV_DIM
    vb = OFF_V // KV_DIM
    prev = lambda n: jnp.maximum(n - 1, 0)
    return pl.pallas_call(
        body,
        name="attn_fwd",
        grid=(nb,),
        in_specs=[
            pl.BlockSpec(memory_space=pltpu.SMEM),
            pl.BlockSpec((Q_DIM, WINDOW), lambda n: (0, n)),
            pl.BlockSpec((KV_DIM, WINDOW), lambda n: (kb, n)),
            pl.BlockSpec((KV_DIM, WINDOW), lambda n: (kb, prev(n))),
            pl.BlockSpec((KV_DIM, WINDOW), lambda n: (vb, n)),
            pl.BlockSpec((KV_DIM, WINDOW), lambda n: (vb, prev(n))),
        ],
        out_specs=[pl.BlockSpec((Q_DIM, WINDOW), lambda n: (0, n)), pl.BlockSpec((N_Q_HEADS, WINDOW), lambda n: (0, n))],
        out_shape=[jax.ShapeDtypeStruct((Q_DIM, t), F32), jax.ShapeDtypeStruct((N_Q_HEADS, t), F32)],
        compiler_params=_cparams("parallel"),
    )(sinks, proj, proj, proj, proj, proj)


def _attn_bwd(proj, sinks, out, lse, dout):
    t = proj.shape[1]
    nb = t // WINDOW
    scale = HEAD_DIM ** -0.5

    def body(s_ref, q_ref, kc_ref, kp_ref, vc_ref, vp_ref, o_ref, lse_ref, do_ref,
             dq_ref, dk_ref, dv_ref, ds_ref, dk_carry, dv_carry):
        step = pl.program_id(0)
        n = nb - 1 - step

        @pl.when(step == 0)
        def _():
            dk_carry[...] = jnp.zeros_like(dk_carry)
            dv_carry[...] = jnp.zeros_like(dv_carry)
            ds_ref[...] = jnp.zeros_like(ds_ref)

        valid = _attn_mask(n)
        for g in range(N_KV_HEADS):
            rows = slice(g * HEAD_DIM, (g + 1) * HEAD_DIM)
            q0 = g * Q_PER_KV * HEAD_DIM
            kt = jnp.concatenate([kp_ref[rows, :], kc_ref[rows, :]], axis=1).astype(BF16)
            vt = jnp.concatenate([vp_ref[rows, :], vc_ref[rows, :]], axis=1).astype(BF16)
            qcat = _lane_cat(q_ref, q0, HEAD_DIM).astype(BF16)
            ocat = _lane_cat(o_ref, q0, HEAD_DIM)
            docat = _lane_cat(do_ref, q0, HEAD_DIM)
            dob = docat.astype(BF16)
            lse_cat = jnp.concatenate(
                [lse_ref[g * Q_PER_KV + i:g * Q_PER_KV + i + 1, :] for i in range(Q_PER_KV)], axis=1)
            sink = jnp.concatenate(
                [jnp.full((1, WINDOW), s_ref[g * Q_PER_KV + i], F32) for i in range(Q_PER_KV)], axis=1)
            s = jnp.where(valid, _dot_tn(kt, qcat) * scale, NEG)
            p = jnp.where(valid, jnp.exp(s - lse_cat), 0.0)
            dp = _dot_tn(vt, dob)
            delta = _colsum(docat * ocat)
            dsc = (p * (dp - delta)).astype(BF16)
            dsink_row = -jnp.exp(sink - lse_cat) * delta
            dq = _dot(kt, dsc) * scale
            dk = _dot_nt(qcat, dsc) * scale
            dv = _dot_nt(dob, p.astype(BF16))
            for i in range(Q_PER_KV):
                h = g * Q_PER_KV + i
                dq_ref[h * HEAD_DIM:(h + 1) * HEAD_DIM, :] = dq[:, i * WINDOW:(i + 1) * WINDOW].astype(BF16)
                ds_ref[h:h + 1, :] += _rowsum(dsink_row[:, i * WINDOW:(i + 1) * WINDOW])
            dk_ref[rows, :] = (dk[:, WINDOW:] + dk_carry[rows, :]).astype(BF16)
            dv_ref[rows, :] = (dv[:, WINDOW:] + dv_carry[rows, :]).astype(BF16)
            dk_carry[rows, :] = dk[:, :WINDOW]
            dv_carry[rows, :] = dv[:, :WINDOW]

    kb = OFF_K // KV_DIM
    vb = OFF_V // KV_DIM
    cur = lambda i: nb - 1 - i
    prev = lambda i: jnp.maximum(nb - 2 - i, 0)
    qspec = pl.BlockSpec((Q_DIM, WINDOW), lambda i: (0, cur(i)))
    kvspec = pl.BlockSpec((KV_DIM, WINDOW), lambda i: (0, cur(i)))
    return pl.pallas_call(
        body,
        name="attn_bwd",
        grid=(nb,),
        in_specs=[
            pl.BlockSpec(memory_space=pltpu.SMEM),
            qspec,
            pl.BlockSpec((KV_DIM, WINDOW), lambda i: (kb, cur(i))),
            pl.BlockSpec((KV_DIM, WINDOW), lambda i: (kb, prev(i))),
            pl.BlockSpec((KV_DIM, WINDOW), lambda i: (vb, cur(i))),
            pl.BlockSpec((KV_DIM, WINDOW), lambda i: (vb, prev(i))),
            qspec,
            pl.BlockSpec((N_Q_HEADS, WINDOW), lambda i: (0, cur(i))),
            qspec,
        ],
        out_specs=[qspec, kvspec, kvspec, pl.BlockSpec((N_Q_HEADS, 1), lambda i: (0, 0))],
        out_shape=[
            jax.ShapeDtypeStruct((Q_DIM, t), BF16),
            jax.ShapeDtypeStruct((KV_DIM, t), BF16),
            jax.ShapeDtypeStruct((KV_DIM, t), BF16),
            jax.ShapeDtypeStruct((N_Q_HEADS, 1), F32),
        ],
        scratch_shapes=[pltpu.VMEM((KV_DIM, WINDOW), F32), pltpu.VMEM((KV_DIM, WINDOW), F32)],
        compiler_params=_cparams("arbitrary"),
    )(sinks, proj, proj, proj, proj, proj, out, lse, dout)


CONV_ROWS = 256


def _conv_silu_fwd(proj, w_col, b_col):
    t = proj.shape[1]
    r0 = OFF_X // CONV_ROWS

    def body(x_ref, w_ref, b_ref, o_ref):
        xv = x_ref[...]
        wv = w_ref[...]
        y = b_ref[...] + wv[:, SSD_CONV - 1:SSD_CONV] * xv
        for k in range(SSD_CONV - 1):
            y = y + wv[:, k:k + 1] * _shift_right(xv, SSD_CONV - 1 - k)
        o_ref[...] = y * _sigmoid(y)

    return pl.pallas_call(
        body,
        name="ssd_conv_fwd",
        grid=(XBC_DIM // CONV_ROWS,),
        in_specs=[
            pl.BlockSpec((CONV_ROWS, t), lambda i: (r0 + i, 0)),
            pl.BlockSpec((CONV_ROWS, SSD_CONV), lambda i: (i, 0)),
            pl.BlockSpec((CONV_ROWS, 1), lambda i: (i, 0)),
        ],
        out_specs=pl.BlockSpec((CONV_ROWS, t), lambda i: (i, 0)),
        out_shape=jax.ShapeDtypeStruct((XBC_DIM, t), F32),
        compiler_params=_cparams("parallel"),
    )(proj, w_col, b_col)


def _conv_silu_bwd(proj, w_col, b_col, dout, row0, name):
    t = proj.shape[1]
    nrows = dout.shape[0]
    p0 = (OFF_X + row0) // CONV_ROWS
    c0 = row0 // CONV_ROWS

    def body(x_ref, w_ref, b_ref, do_ref, dx_ref, dwb_ref):
        xv = x_ref[...]
        wv = w_ref[...]
        y = b_ref[...] + wv[:, SSD_CONV - 1:SSD_CONV] * xv
        for k in range(SSD_CONV - 1):
            y = y + wv[:, k:k + 1] * _shift_right(xv, SSD_CONV - 1 - k)
        sg = _sigmoid(y)
        dy = do_ref[...] * (sg * (1.0 + y * (1.0 - sg)))
        lane = lax.broadcasted_iota(jnp.int32, (CONV_ROWS, 128), 1)
        dwb = jnp.where(lane == SSD_CONV, _rowsum(dy), 0.0)
        dx = wv[:, SSD_CONV - 1:SSD_CONV] * dy
        dwb = jnp.where(lane == SSD_CONV - 1, _rowsum(dy * xv), dwb)
        for k in range(SSD_CONV - 1):
            j = SSD_CONV - 1 - k
            dx = dx + wv[:, k:k + 1] * _shift_left(dy, j)
            dwb = jnp.where(lane == k, _rowsum(dy * _shift_right(xv, j)), dwb)
        dx_ref[...] = dx.astype(BF16)
        dwb_ref[...] = dwb

    return pl.pallas_call(
        body,
        name=name,
        grid=(nrows // CONV_ROWS,),
        in_specs=[
            pl.BlockSpec((CONV_ROWS, t), lambda i: (p0 + i, 0)),
            pl.BlockSpec((CONV_ROWS, SSD_CONV), lambda i: (c0 + i, 0)),
            pl.BlockSpec((CONV_ROWS, 1), lambda i: (c0 + i, 0)),
            pl.BlockSpec((CONV_ROWS, t), lambda i: (i, 0)),
        ],
        out_specs=[pl.BlockSpec((CONV_ROWS, t), lambda i: (i, 0)), pl.BlockSpec((CONV_ROWS, 128), lambda i: (i, 0))],
        out_shape=[jax.ShapeDtypeStruct((nrows, t), BF16), jax.ShapeDtypeStruct((nrows, 128), F32)],
        compiler_params=_cparams("parallel"),
    )(proj, w_col, b_col, dout)


GROUP_ROWS = HEADS_PER_GROUP * SSD_HEAD_DIM


def _ssd_specs(nc, order):
    hb = D_INNER // D_STATE
    dtb = OFF_DT // HEADS_PER_GROUP
    col = pl.BlockSpec((HEADS_PER_GROUP, 1), lambda g, c: (g, 0))
    return [
        pl.BlockSpec((GROUP_ROWS, CHUNK), lambda g, c: (g, order(c))),
        pl.BlockSpec((D_STATE, CHUNK), lambda g, c: (hb + g, order(c))),
        pl.BlockSpec((D_STATE, CHUNK), lambda g, c: (hb + N_SSD_GROUPS + g, order(c))),
        pl.BlockSpec((HEADS_PER_GROUP, CHUNK), lambda g, c: (dtb + g, order(c))),
        col, col, col,
    ]


def _ssd_common(dt_ref, dtb_ref, alog_ref):
    z = dt_ref[...] + dtb_ref[...]
    dt = _softplus(z)
    a_neg = -jnp.exp(alog_ref[...])
    d_a = dt * a_neg
    row = lax.broadcasted_iota(jnp.int32, (CHUNK, CHUNK), 0)
    colm = lax.broadcasted_iota(jnp.int32, (CHUNK, CHUNK), 1)
    upper = (row <= colm).astype(F32)
    a_cs = jnp.dot(d_a, upper, precision=HIGHEST, preferred_element_type=F32)
    a_last = _rowsum(d_a)
    return z, dt, a_neg, a_cs, a_last, row >= colm, row == colm


def _decay(a_row, causal):
    a_s = jnp.broadcast_to(a_row, (CHUNK, CHUNK))
    seg = a_s.T - a_s
    return jnp.where(causal, jnp.exp(jnp.where(causal, seg, 0.0)), 0.0)


def _ssd_fwd(xbc, proj, dtb_col, alog_col, dsk_col):
    t = xbc.shape[1]
    nc = t // CHUNK

    def body(xs_ref, b_ref, c_ref, dt_ref, dtb_ref, alog_ref, dsk_ref, y_ref, hst_ref, h_scr):
        @pl.when(pl.program_id(1) == 0)
        def _():
            h_scr[...] = jnp.zeros_like(h_scr)

        _, dt, _, a_cs, a_last, causal, _ = _ssd_common(dt_ref, dtb_ref, alog_ref)
        bb = b_ref[...].astype(BF16)
        cb_ = c_ref[...].astype(BF16)
        cb = _dot_tn(cb_, bb)
        hst_ref[0, 0] = h_scr[...]
        dsk = dsk_ref[...]
        for j in range(HEADS_PER_GROUP):
            rows = slice(j * SSD_HEAD_DIM, (j + 1) * SSD_HEAD_DIM)
            a = a_cs[j:j + 1, :]
            m = (cb * _decay(a, causal)).astype(BF16)
            xs = xs_ref[rows, :]
            xc = xs * dt[j:j + 1, :]
            hj = h_scr[rows, :]
            y = _dot_nt(xc.astype(BF16), m) + _dot(hj.astype(BF16), cb_) * jnp.exp(a) + dsk[j:j + 1, :] * xs
            y_ref[rows, :] = y
            al = a_last[j:j + 1, :]
            w = jnp.exp(al - a)
            h_scr[rows, :] = jnp.exp(al) * hj + _dot_nt((xc * w).astype(BF16), bb)

    return pl.pallas_call(
        body,
        name="ssd_fwd",
        grid=(N_SSD_GROUPS, nc),
        in_specs=_ssd_specs(nc, lambda c: c),
        out_specs=[
            pl.BlockSpec((GROUP_ROWS, CHUNK), lambda g, c: (g, c)),
            pl.BlockSpec((1, 1, GROUP_ROWS, D_STATE), lambda g, c: (g, c, 0, 0)),
        ],
        out_shape=[
            jax.ShapeDtypeStruct((D_INNER, t), F32),
            jax.ShapeDtypeStruct((N_SSD_GROUPS, nc, GROUP_ROWS, D_STATE), F32),
        ],
        scratch_shapes=[pltpu.VMEM((GROUP_ROWS, D_STATE), F32)],
        compiler_params=_cparams("parallel", "arbitrary"),
    )(xbc, xbc, xbc, proj, dtb_col, alog_col, dsk_col)


def _ssd_bwd(xbc, proj, dtb_col, alog_col, dsk_col, hst, dy):
    t = xbc.shape[1]
    nc = t // CHUNK
    rev = lambda c: nc - 1 - c

    def body(xs_ref, b_ref, c_ref, dt_ref, dtb_ref, alog_ref, dsk_ref, hst_ref, dy_ref,
             dxs_ref, db_ref, dc_ref, ddt_ref, dalog_ref, ddsk_ref, ddtb_ref, dh_scr, da_scr, ddt_scr, dd_scr):
        @pl.when(pl.program_id(1) == 0)
        def _():
            dh_scr[...] = jnp.zeros_like(dh_scr)
            dalog_ref[...] = jnp.zeros_like(dalog_ref)
            ddsk_ref[...] = jnp.zeros_like(ddsk_ref)
            ddtb_ref[...] = jnp.zeros_like(ddtb_ref)

        z, dt, a_neg, a_cs, a_last, causal, eye = _ssd_common(dt_ref, dtb_ref, alog_ref)
        bb = b_ref[...].astype(BF16)
        cb_ = c_ref[...].astype(BF16)
        cb = _dot_tn(cb_, bb)
        dsk = dsk_ref[...]
        last_lane = lax.broadcasted_iota(jnp.int32, (1, CHUNK), 1) == CHUNK - 1
        dcb = jnp.zeros((CHUNK, CHUNK), F32)
        dc_acc = jnp.zeros((D_STATE, CHUNK), F32)
        db_acc = jnp.zeros((D_STATE, CHUNK), F32)
        for j in range(HEADS_PER_GROUP):
            rows = slice(j * SSD_HEAD_DIM, (j + 1) * SSD_HEAD_DIM)
            a = a_cs[j:j + 1, :]
            al = a_last[j:j + 1, :]
            lam = _decay(a, causal)
            mf = cb * lam
            xs = xs_ref[rows, :]
            dtj = dt[j:j + 1, :]
            xc = xs * dtj
            w = jnp.exp(al - a)
            e = jnp.exp(a)
            gam = jnp.exp(al)
            hj = hst_ref[0, 0, rows, :]
            hjb = hj.astype(BF16)
            dyv = dy_ref[rows, :]
            dyb = dyv.astype(BF16)
            dd_scr[j:j + 1, :] = _colsum(dyv * xs)
            gb = (dyv * e).astype(BF16)
            dh_in = _dot_nt(gb, cb_)
            dc_acc = dc_acc + _dot_tn(hjb, gb)
            yoff = _dot(hjb, cb_) * e
            da = _colsum(dyv * yoff)
            dm = _dot_tn(dyb, xc.astype(BF16))
            dxc = _dot(dyb, mf.astype(BF16))
            dcb = dcb + dm * lam
            nmat = dm * mf
            rs = jnp.broadcast_to(_rowsum(nmat), (CHUNK, CHUNK))
            da = da + _colsum(jnp.where(eye, rs, 0.0)) - _colsum(nmat)
            ds = dh_scr[rows, :]
            dsb = ds.astype(BF16)
            t1 = _dot(dsb, bb)
            xcw = xc * w
            dxc = dxc + w * t1
            dww = _colsum(xcw * t1)
            da_l = _rowsum(dww) + _rowsum(_colsum(ds * hj)) * gam
            da = da - dww + jnp.where(last_lane, da_l, 0.0)
            db_acc = db_acc + _dot_tn(dsb, xcw.astype(BF16))
            dh_scr[rows, :] = gam * ds + dh_in
            dxs_ref[rows, :] = dsk[j:j + 1, :] * dyv + dxc * dtj
            da_scr[j:j + 1, :] = da
            ddt_scr[j:j + 1, :] = _colsum(dxc * xs)
        dcbb = dcb.astype(BF16)
        dc_ref[...] = dc_acc + _dot_nt(bb, dcbb)
        db_ref[...] = db_acc + _dot(cb_, dcbb)
        dda = jnp.dot(da_scr[...], causal.astype(F32), precision=HIGHEST, preferred_element_type=F32)
        ddt = ddt_scr[...] + dda * a_neg
        ddt_raw = ddt * _sigmoid(z)
        ddt_ref[...] = ddt_raw
        ddtb_ref[...] += _rowsum(ddt_raw)
        dalog_ref[...] += _rowsum(dda * dt) * a_neg
        ddsk_ref[...] += _rowsum(dd_scr[...])

    col = pl.BlockSpec((HEADS_PER_GROUP, 1), lambda g, c: (g, 0))
    bc = pl.BlockSpec((D_STATE, CHUNK), lambda g, c: (g, rev(c)))
    xs_spec = pl.BlockSpec((GROUP_ROWS, CHUNK), lambda g, c: (g, rev(c)))
    small = pltpu.VMEM((HEADS_PER_GROUP, CHUNK), F32)
    return pl.pallas_call(
        body,
        name="ssd_bwd",
        grid=(N_SSD_GROUPS, nc),
        in_specs=_ssd_specs(nc, rev) + [
            pl.BlockSpec((1, 1, GROUP_ROWS, D_STATE), lambda g, c: (g, rev(c), 0, 0)),
            xs_spec,
        ],
        out_specs=[xs_spec, bc, bc, pl.BlockSpec((HEADS_PER_GROUP, CHUNK), lambda g, c: (g, rev(c))), col, col, col],
        out_shape=[
            jax.ShapeDtypeStruct((D_INNER, t), F32),
            jax.ShapeDtypeStruct((BC_DIM, t), F32),
            jax.ShapeDtypeStruct((BC_DIM, t), F32),
            jax.ShapeDtypeStruct((N_SSD_HEADS, t), F32),
            jax.ShapeDtypeStruct((N_SSD_HEADS, 1), F32),
            jax.ShapeDtypeStruct((N_SSD_HEADS, 1), F32),
            jax.ShapeDtypeStruct((N_SSD_HEADS, 1), F32),
        ],
        scratch_shapes=[pltpu.VMEM((GROUP_ROWS, D_STATE), F32), small, small, small],
        compiler_params=_cparams("parallel", "arbitrary"),
    )(xbc, xbc, xbc, proj, dtb_col, alog_col, dsk_col, hst, dy)


GN_ROWS = D_INNER // N_SSD_GROUPS


def _gnorm_fwd(y, proj, w_col):
    t = y.shape[1]
    tt = _tile(t, (512, 256, 128))
    z0 = OFF_Z // GN_ROWS

    def body(y_ref, z_ref, w_ref, o_ref):
        zv = z_ref[...]
        u = y_ref[...] * (zv * _sigmoid(zv))
        r = lax.rsqrt(jnp.mean(u * u, axis=0, keepdims=True) + EPS)
        o_ref[...] = (u * r * w_ref[...]).astype(BF16)

    blk = pl.BlockSpec((GN_ROWS, tt), lambda g, i: (g, i))
    return pl.pallas_call(
        body,
        name="gnorm_fwd",
        grid=(N_SSD_GROUPS, t // tt),
        in_specs=[blk, pl.BlockSpec((GN_ROWS, tt), lambda g, i: (z0 + g, i)), pl.BlockSpec((GN_ROWS, 1), lambda g, i: (g, 0))],
        out_specs=blk,
        out_shape=jax.ShapeDtypeStruct((D_INNER, t), BF16),
        compiler_params=_cparams("parallel", "parallel"),
    )(y, proj, w_col)


def _gnorm_bwd(dout, y, proj, w_col):
    t = y.shape[1]
    tt = _tile(t, (512, 256, 128))
    z0 = OFF_Z // GN_ROWS

    def body(do_ref, y_ref, z_ref, w_ref, dy_ref, dz_ref, dw_ref):
        @pl.when(pl.program_id(1) == 0)
        def _():
            dw_ref[...] = jnp.zeros_like(dw_ref)

        zv = z_ref[...]
        yv = y_ref[...]
        sg = _sigmoid(zv)
        sz = zv * sg
        u = yv * sz
        r = lax.rsqrt(jnp.mean(u * u, axis=0, keepdims=True) + EPS)
        xhat = u * r
        dov = do_ref[...]
        dw_ref[...] += _rowsum(dov * xhat)
        dxhat = dov * w_ref[...]
        du = r * (dxhat - xhat * jnp.mean(dxhat * xhat, axis=0, keepdims=True))
        dy_ref[...] = du * sz
        dz_ref[...] = (du * yv * (sg * (1.0 + zv * (1.0 - sg)))).astype(BF16)

    blk = pl.BlockSpec((GN_ROWS, tt), lambda g, i: (g, i))
    col = pl.BlockSpec((GN_ROWS, 1), lambda g, i: (g, 0))
    return pl.pallas_call(
        body,
        name="gnorm_bwd",
        grid=(N_SSD_GROUPS, t // tt),
        in_specs=[blk, blk, pl.BlockSpec((GN_ROWS, tt), lambda g, i: (z0 + g, i)), col],
        out_specs=[blk, blk, col],
        out_shape=[jax.ShapeDtypeStruct((D_INNER, t), F32), jax.ShapeDtypeStruct((D_INNER, t), BF16),
                   jax.ShapeDtypeStruct((D_INNER, 1), F32)],
        compiler_params=_cparams("parallel", "arbitrary"),
    )(dout, y, proj, w_col)


GATE_ROWS = 512


def _gate_specs(t, tt):
    ga0 = OFF_GA // GATE_ROWS
    gs0 = OFF_GS // GATE_ROWS
    nr = D_MODEL // GATE_ROWS
    blk = pl.BlockSpec((GATE_ROWS, tt), lambda r, i: (r, i))
    return blk, [
        pl.BlockSpec((GATE_ROWS, tt), lambda r, i: (ga0 + r, i)),
        pl.BlockSpec((GATE_ROWS, tt), lambda r, i: (gs0 + r, i)),
        pl.BlockSpec((GATE_ROWS, 1), lambda r, i: (r, 0)),
        pl.BlockSpec((GATE_ROWS, 1), lambda r, i: (nr + r, 0)),
        blk, blk,
    ]


def _gate_fwd(proj, b_col, attn, ssd):
    t = proj.shape[1]
    tt = _tile(t, (512, 256, 128))
    blk, specs = _gate_specs(t, tt)

    def body(ga_ref, gs_ref, ba_ref, bs_ref, a_ref, s_ref, o_ref):
        o_ref[...] = (_sigmoid(ga_ref[...] + ba_ref[...]) * a_ref[...]
                      + _sigmoid(gs_ref[...] + bs_ref[...]) * s_ref[...]).astype(BF16)

    return pl.pallas_call(
        body,
        name="gate_fwd",
        grid=(D_MODEL // GATE_ROWS, t // tt),
        in_specs=specs,
        out_specs=blk,
        out_shape=jax.ShapeDtypeStruct((D_MODEL, t), BF16),
        compiler_params=_cparams("parallel", "parallel"),
    )(proj, proj, b_col, b_col, attn, ssd)


def _gate_bwd(proj, b_col, attn, ssd, dmix):
    t = proj.shape[1]
    tt = _tile(t, (512, 256, 128))
    blk, specs = _gate_specs(t, tt)
    nr = D_MODEL // GATE_ROWS

    def body(ga_ref, gs_ref, ba_ref, bs_ref, a_ref, s_ref, dm_ref, da_ref, dso_ref, dga_ref, dgs_ref, dba_ref, dbs_ref):
        @pl.when(pl.program_id(1) == 0)
        def _():
            dba_ref[...] = jnp.zeros_like(dba_ref)
            dbs_ref[...] = jnp.zeros_like(dbs_ref)

        dm = dm_ref[...]
        sa = _sigmoid(ga_ref[...] + ba_ref[...])
        ss = _sigmoid(gs_ref[...] + bs_ref[...])
        da_ref[...] = (dm * sa).astype(BF16)
        dso_ref[...] = (dm * ss).astype(BF16)
        dga = dm * a_ref[...] * sa * (1.0 - sa)
        dgs = dm * s_ref[...] * ss * (1.0 - ss)
        dga_ref[...] = dga.astype(BF16)
        dgs_ref[...] = dgs.astype(BF16)
        dba_ref[...] += _rowsum(dga)
        dbs_ref[...] += _rowsum(dgs)

    col = pl.BlockSpec((GATE_ROWS, 1), lambda r, i: (r, 0))
    act = jax.ShapeDtypeStruct((D_MODEL, t), BF16)
    bias = jax.ShapeDtypeStruct((D_MODEL, 1), F32)
    return pl.pallas_call(
        body,
        name="gate_bwd",
        grid=(nr, t // tt),
        in_specs=specs + [blk],
        out_specs=[blk, blk, blk, blk, col, col],
        out_shape=[act, act, act, act, bias, bias],
        compiler_params=_cparams("parallel", "arbitrary"),
    )(proj, proj, b_col, b_col, attn, ssd, dmix)


FFN_ROWS = 256


def _ffn_conv(u_ref, w_ref, b_ref, half):
    xv = u_ref[half]
    wv = w_ref[half]
    y = b_ref[half] + wv[:, FFN_CONV - 1:FFN_CONV] * xv
    for k in range(FFN_CONV - 1):
        y = y + wv[:, k:k + 1] * _shift_right(xv, FFN_CONV - 1 - k)
    return xv, wv, y


def _ffn_fwd(u0, w_col, b_col):
    t = u0.shape[2]

    def body(u_ref, w_ref, b_ref, o_ref):
        _, _, val = _ffn_conv(u_ref, w_ref, b_ref, 0)
        _, _, gt = _ffn_conv(u_ref, w_ref, b_ref, 1)
        o_ref[...] = (gt * _sigmoid(gt) * val).astype(BF16)

    return pl.pallas_call(
        body,
        name="ffn_fwd",
        grid=(D_FF // FFN_ROWS,),
        in_specs=[
            pl.BlockSpec((2, FFN_ROWS, t), lambda i: (0, i, 0)),
            pl.BlockSpec((2, FFN_ROWS, FFN_CONV), lambda i: (0, i, 0)),
            pl.BlockSpec((2, FFN_ROWS, 1), lambda i: (0, i, 0)),
        ],
        out_specs=pl.BlockSpec((FFN_ROWS, t), lambda i: (i, 0)),
        out_shape=jax.ShapeDtypeStruct((D_FF, t), BF16),
        compiler_params=_cparams("parallel"),
    )(u0, w_col, b_col)


def _ffn_bwd(u0, w_col, b_col, dg):
    t = u0.shape[2]

    def body(u_ref, w_ref, b_ref, dg_ref, du_ref, dwb_ref):
        xval, wval, val = _ffn_conv(u_ref, w_ref, b_ref, 0)
        xgt, wgt, gt = _ffn_conv(u_ref, w_ref, b_ref, 1)
        sg = _sigmoid(gt)
        dgv = dg_ref[...]
        dval = dgv * (gt * sg)
        dgt = dgv * val * (sg * (1.0 + gt * (1.0 - sg)))
        lane = lax.broadcasted_iota(jnp.int32, (FFN_ROWS, 128), 1)
        for half, xv, wv, dy in ((0, xval, wval, dval), (1, xgt, wgt, dgt)):
            dwb = jnp.where(lane == FFN_CONV, _rowsum(dy), 0.0)
            dx = wv[:, FFN_CONV - 1:FFN_CONV] * dy
            dwb = jnp.where(lane == FFN_CONV - 1, _rowsum(dy * xv), dwb)
            for k in range(FFN_CONV - 1):
                j = FFN_CONV - 1 - k
                dx = dx + wv[:, k:k + 1] * _shift_left(dy, j)
                dwb = jnp.where(lane == k, _rowsum(dy * _shift_right(xv, j)), dwb)
            du_ref[half] = dx.astype(BF16)
            dwb_ref[half] = dwb

    return pl.pallas_call(
        body,
        name="ffn_bwd",
        grid=(D_FF // FFN_ROWS,),
        in_specs=[
            pl.BlockSpec((2, FFN_ROWS, t), lambda i: (0, i, 0)),
            pl.BlockSpec((2, FFN_ROWS, FFN_CONV), lambda i: (0, i, 0)),
            pl.BlockSpec((2, FFN_ROWS, 1), lambda i: (0, i, 0)),
            pl.BlockSpec((FFN_ROWS, t), lambda i: (i, 0)),
        ],
        out_specs=[pl.BlockSpec((2, FFN_ROWS, t), lambda i: (0, i, 0)), pl.BlockSpec((2, FFN_ROWS, 128), lambda i: (0, i, 0))],
        out_shape=[jax.ShapeDtypeStruct((2, D_FF, t), BF16), jax.ShapeDtypeStruct((2, D_FF, 128), F32)],
        compiler_params=_cparams("parallel"),
    )(u0, w_col, b_col, dg)


def _adamw_math(w, g, m, v):
    m = ADAM_B1 * m + (1.0 - ADAM_B1) * g
    v = ADAM_B2 * v + (1.0 - ADAM_B2) * (g * g)
    m_hat = m / (1.0 - ADAM_B1 ** ADAM_STEP)
    v_hat = v / (1.0 - ADAM_B2 ** ADAM_STEP)
    delta = -ADAM_LR * (m_hat / (jnp.sqrt(v_hat) + ADAM_EPS) + ADAM_WD * w)
    return delta, m, v


def _adamw_sharded(parts, w, m, v, name):
    r, c = w.shape
    tr = _tile(r, (256, 128, 64, 32, 16))

    def body(p_ref, w_ref, m_ref, v_ref, g_ref, d_ref, nm_ref, nv_ref):
        g = p_ref[0].astype(F32)
        for s in range(1, N_DEV):
            g = g + p_ref[s].astype(F32)
        g_ref[...] = g
        d_ref[...], nm_ref[...], nv_ref[...] = _adamw_math(w_ref[...], g, m_ref[...], v_ref[...])

    blk = pl.BlockSpec((tr, c), lambda i: (i, 0))
    out = jax.ShapeDtypeStruct((r, c), F32)
    return pl.pallas_call(
        body,
        name=name,
        grid=(r // tr,),
        in_specs=[pl.BlockSpec((N_DEV, tr, c), lambda i: (0, i, 0)), blk, blk, blk],
        out_specs=[blk, blk, blk, blk],
        out_shape=[out, out, out, out],
        compiler_params=_cparams("parallel"),
    )(parts, w, m, v)


def _sum_slots(parts):
    _, r, c = parts.shape

    def body(p_ref, o_ref):
        g = p_ref[0]
        for s in range(1, N_DEV):
            g = g + p_ref[s]
        o_ref[...] = g

    return pl.pallas_call(body, name="sum_small_grads", out_shape=jax.ShapeDtypeStruct((r, c), F32))(parts)


def _adamw_small(g, w, m, v):
    def body(g_ref, w_ref, m_ref, v_ref, d_ref, nm_ref, nv_ref):
        d_ref[...], nm_ref[...], nv_ref[...] = _adamw_math(w_ref[...], g_ref[...], m_ref[...], v_ref[...])

    out = jax.ShapeDtypeStruct(g.shape, F32)
    return pl.pallas_call(body, name="adamw_small", out_shape=[out, out, out])(g, w, m, v)


ANY = pl.BlockSpec(memory_space=pl.ANY)
FLIPS = [(k >> 2 & 1, k >> 1 & 1, k & 1) for k in range(1, N_DEV)]


def _place():
    return lax.axis_index("x"), lax.axis_index("y"), lax.axis_index("c")


def _all_gather(arrays, name):
    n = len(arrays)

    def body(*refs):
        ins, outs = refs[:n], refs[n:2 * n]
        send_sems, recv_sems, local_sems = refs[2 * n:]
        x, y, c = _place()
        me = 4 * x + 2 * y + c
        local = [pltpu.make_async_copy(ins[i], outs[i].at[me], local_sems.at[i]) for i in range(n)]
        for cp in local:
            cp.start()
        sends = []
        for k, (fx, fy, fc) in enumerate(FLIPS):
            for i in range(n):
                cp = pltpu.make_async_remote_copy(
                    src_ref=ins[i], dst_ref=outs[i].at[me], send_sem=send_sems.at[i, k], recv_sem=recv_sems.at[i, k],
                    device_id=(x ^ fx, y ^ fy, c ^ fc), device_id_type=MESH)
                cp.start()
                sends.append(cp)
        for k, (fx, fy, fc) in enumerate(FLIPS):
            src = 4 * (x ^ fx) + 2 * (y ^ fy) + (c ^ fc)
            for i in range(n):
                pltpu.make_async_remote_copy(
                    src_ref=ins[i], dst_ref=outs[i].at[src], send_sem=send_sems.at[i, k], recv_sem=recv_sems.at[i, k],
                    device_id=(x ^ fx, y ^ fy, c ^ fc), device_id_type=MESH).wait_recv()
        for cp in sends:
            cp.wait_send()
        for cp in local:
            cp.wait()

    return pl.pallas_call(
        body,
        name=name,
        in_specs=[ANY] * n,
        out_specs=[ANY] * n,
        out_shape=[jax.ShapeDtypeStruct((N_DEV,) + a.shape, a.dtype) for a in arrays],
        scratch_shapes=[pltpu.SemaphoreType.DMA((n, N_DEV - 1)), pltpu.SemaphoreType.DMA((n, N_DEV - 1)),
                        pltpu.SemaphoreType.DMA((n,))],
    )(*arrays)


def _all_to_all(arrays, name):
    n = len(arrays)

    def body(*refs):
        ins, outs = refs[:n], refs[n:2 * n]
        send_sems, recv_sems, local_sems = refs[2 * n:]
        x, y, c = _place()
        me = 4 * x + 2 * y + c
        local = [pltpu.make_async_copy(ins[i].at[me], outs[i].at[me], local_sems.at[i]) for i in range(n)]
        for cp in local:
            cp.start()
        sends = []
        for k, (fx, fy, fc) in enumerate(FLIPS):
            peer = 4 * (x ^ fx) + 2 * (y ^ fy) + (c ^ fc)
            for i in range(n):
                cp = pltpu.make_async_remote_copy(
                    src_ref=ins[i].at[peer], dst_ref=outs[i].at[me], send_sem=send_sems.at[i, k],
                    recv_sem=recv_sems.at[i, k], device_id=(x ^ fx, y ^ fy, c ^ fc), device_id_type=MESH)
                cp.start()
                sends.append(cp)
        for k, (fx, fy, fc) in enumerate(FLIPS):
            peer = 4 * (x ^ fx) + 2 * (y ^ fy) + (c ^ fc)
            for i in range(n):
                pltpu.make_async_remote_copy(
                    src_ref=ins[i].at[peer], dst_ref=outs[i].at[peer], send_sem=send_sems.at[i, k],
                    recv_sem=recv_sems.at[i, k], device_id=(x ^ fx, y ^ fy, c ^ fc), device_id_type=MESH).wait_recv()
        for cp in sends:
            cp.wait_send()
        for cp in local:
            cp.wait()

    return pl.pallas_call(
        body,
        name=name,
        in_specs=[ANY] * n,
        out_specs=[ANY] * n,
        out_shape=[jax.ShapeDtypeStruct(a.shape, a.dtype) for a in arrays],
        scratch_shapes=[pltpu.SemaphoreType.DMA((n, N_DEV - 1)), pltpu.SemaphoreType.DMA((n, N_DEV - 1)),
                        pltpu.SemaphoreType.DMA((n,))],
    )(*arrays)


def _col(v):
    return v.reshape(-1, 1).astype(F32)


def _to_internal_rows(w_t):
    pad = jnp.zeros((IN_PAD - IN_DIM, w_t.shape[1]), w_t.dtype)
    return jnp.concatenate([w_t[:ORIG_DT], w_t[ORIG_GA:ORIG_GS], w_t[ORIG_GS:IN_DIM], w_t[ORIG_DT:ORIG_GA], pad], axis=0)


def _from_internal_cols(g):
    return jnp.concatenate([g[:, :OFF_GA], g[:, OFF_DT:OFF_DT + N_SSD_HEADS], g[:, OFF_GA:OFF_GS], g[:, OFF_GS:OFF_DT]], axis=1)


def _local_step(xt, tgt, wts, small):
    t = xt.shape[1]
    n1 = _col(small["norm1_w"])
    n2 = _col(small["norm2_w"])
    nf = _col(small["final_norm_w"])
    bg = _col(small["b_gate"])
    sinks = small["attn_sinks"].reshape(-1).astype(F32)
    cw = small["ssd_conv_w"].T
    cbias = _col(small["ssd_conv_b"])
    dtb = _col(small["dt_bias"])
    alog = _col(small["a_log"])
    dsk = _col(small["d_skip"])
    gnw = _col(small["ssd_norm_w"])
    fw = small["ffn_conv_w"].T.reshape(2, D_FF, FFN_CONV)
    fb = small["ffn_conv_b"].reshape(2, D_FF, 1)

    xn = _norm_fwd(xt, n1, "norm1_fwd")
    proj = _matmul(wts["in_t"], xn, nt=False, out_dtype=F32, name="mm_in")
    ao, lse = _attn_fwd(proj, sinks)
    attn = _matmul(wts["ao_t"], ao, nt=False, out_dtype=F32, name="mm_attn_o")
    xbc = _conv_silu_fwd(proj, cw, cbias)
    y, hst = _ssd_fwd(xbc, proj, dtb, alog, dsk)
    yn = _gnorm_fwd(y, proj, gnw)
    ssd = _matmul(wts["so_t"], yn, nt=False, out_dtype=F32, name="mm_ssd_o")
    mix = _gate_fwd(proj, bg, attn, ssd)
    h1 = _matmul(wts["out_t"], mix, nt=False, out_dtype=F32, name="mm_out", add=xt)
    hn = _norm_fwd(h1, n2, "norm2_fwd")
    u0 = _matmul(wts["up_t"], hn, nt=False, out_dtype=F32, name="mm_up").reshape(2, D_FF, t)
    gl = _ffn_fwd(u0, fw, fb)
    h2 = _matmul(wts["down_t"], gl, nt=False, out_dtype=F32, name="mm_down", add=h1)
    dh2, loss, d_nf = _final_norm_loss(h2, tgt, nf)

    g = {}
    g["w_down"] = _matmul(gl, dh2, nt=True, out_dtype=BF16, name="mm_d_w_down")
    dgl = _matmul(wts["down"], dh2, nt=False, out_dtype=F32, name="mm_d_glu")
    du0, d_fwb = _ffn_bwd(u0, fw, fb, dgl)
    du0 = du0.reshape(2 * D_FF, t)
    g["w_up"] = _matmul(hn, du0, nt=True, out_dtype=BF16, name="mm_d_w_up")
    dhn = _matmul(wts["up"], du0, nt=False, out_dtype=F32, name="mm_d_hn")
    dh1, d_n2 = _norm_bwd(dhn, h1, n2, dh2, "norm2_bwd")
    g["w_out"] = _matmul(mix, dh1, nt=True, out_dtype=BF16, name="mm_d_w_out")
    dmix = _matmul(wts["out"], dh1, nt=False, out_dtype=F32, name="mm_d_mix")
    d_attn, d_ssd, d_ga, d_gs, d_ba, d_bs = _gate_bwd(proj, bg, attn, ssd, dmix)
    g["w_attn_o"] = _matmul(ao, d_attn, nt=True, out_dtype=BF16, name="mm_d_w_attn_o")
    dao = _matmul(wts["ao"], d_attn, nt=False, out_dtype=F32, name="mm_d_ao")
    dq, dk, dv, d_sinks = _attn_bwd(proj, sinks, ao, lse, dao)
    g["w_ssd_o"] = _matmul(yn, d_ssd, nt=True, out_dtype=BF16, name="mm_d_w_ssd_o")
    dyn = _matmul(wts["so"], d_ssd, nt=False, out_dtype=F32, name="mm_d_yn")
    dy, dz, d_gnw = _gnorm_bwd(dyn, y, proj, gnw)
    dxs, dbm, dcm, ddt, d_alog, d_dsk, d_dtb = _ssd_bwd(xbc, proj, dtb, alog, dsk, hst, dy)
    dx_xs, dwb_xs = _conv_silu_bwd(proj, cw, cbias, dxs, 0, "ssd_conv_bwd_x")
    dx_b, dwb_b = _conv_silu_bwd(proj, cw, cbias, dbm, D_INNER, "ssd_conv_bwd_b")
    dx_c, dwb_c = _conv_silu_bwd(proj, cw, cbias, dcm, D_INNER + BC_DIM, "ssd_conv_bwd_c")
    dwb_conv = jnp.concatenate([dwb_xs, dwb_b, dwb_c], axis=0)
    ddt_rows = jnp.concatenate([ddt.astype(BF16), jnp.zeros((IN_PAD - OFF_DT - N_SSD_HEADS, t), BF16)], axis=0)
    dproj = jnp.concatenate([dq, dk, dv, dz, dx_xs, dx_b, dx_c, d_ga, d_gs, ddt_rows], axis=0)
    g["w_in"] = _from_internal_cols(_matmul(xn, dproj, nt=True, out_dtype=BF16, name="mm_d_w_in"))
    dxn = _matmul(wts["in"], dproj, nt=False, out_dtype=F32, name="mm_d_xn")
    dx, d_n1 = _norm_bwd(dxn, xt, n1, dh1, "norm1_bwd")

    g["norm1_w"] = d_n1
    g["b_gate"] = jnp.concatenate([d_ba, d_bs], axis=0)
    g["attn_sinks"] = d_sinks
    g["ssd_conv_w"] = dwb_conv[:, :SSD_CONV].T
    g["ssd_conv_b"] = dwb_conv[:, SSD_CONV]
    g["dt_bias"] = d_dtb
    g["a_log"] = d_alog
    g["d_skip"] = d_dsk
    g["ssd_norm_w"] = d_gnw
    g["norm2_w"] = d_n2
    d_fwb = d_fwb.reshape(2 * D_FF, 128)
    g["ffn_conv_w"] = d_fwb[:, :FFN_CONV].T
    g["ffn_conv_b"] = d_fwb[:, FFN_CONV]
    g["final_norm_w"] = d_nf
    return loss, dx, g


SHARDED = ("w_in", "w_attn_o", "w_ssd_o", "w_out", "w_up", "w_down")
SMALL = ("norm1_w", "b_gate", "attn_sinks", "ssd_conv_w", "ssd_conv_b", "dt_bias", "a_log", "d_skip", "ssd_norm_w",
         "norm2_w", "ffn_conv_w", "ffn_conv_b", "final_norm_w")
SMALL_SHAPES = {"norm1_w": (1, D_MODEL), "b_gate": (1, 2 * D_MODEL), "attn_sinks": (1, N_Q_HEADS),
                "ssd_conv_w": (1, SSD_CONV, XBC_DIM), "ssd_conv_b": (1, XBC_DIM), "dt_bias": (1, N_SSD_HEADS),
                "a_log": (1, N_SSD_HEADS), "d_skip": (1, N_SSD_HEADS), "ssd_norm_w": (1, D_INNER),
                "norm2_w": (1, D_MODEL), "ffn_conv_w": (1, FFN_CONV, 2 * D_FF), "ffn_conv_b": (1, 2 * D_FF),
                "final_norm_w": (D_MODEL,)}
WEIGHT_ORDER = ("norm1_w", "w_in", "b_gate", "attn_sinks", "w_attn_o", "ssd_conv_w", "ssd_conv_b", "dt_bias", "a_log",
                "d_skip", "ssd_norm_w", "w_ssd_o", "w_out", "norm2_w", "w_up", "ffn_conv_w", "ffn_conv_b", "w_down",
                "final_norm_w")


def _pack(parts):
    flat = jnp.concatenate([p.reshape(-1).astype(F32) for p in parts])
    rows = -(-flat.shape[0] // 1024) * 8
    return jnp.pad(flat, (0, rows * 128 - flat.shape[0])).reshape(rows, 128)


def _unpack(packed, shapes):
    flat = packed.reshape(-1)
    out, pos = [], 0
    for shp in shapes:
        size = 1
        for d in shp:
            size *= d
        out.append(flat[pos:pos + size].reshape(shp))
        pos += size
    return out


def kernel(x, norm1_w, w_in, b_gate, attn_sinks, w_attn_o, ssd_conv_w, ssd_conv_b, dt_bias, a_log, d_skip, ssd_norm_w, w_ssd_o, w_out, norm2_w, w_up, ffn_conv_w, ffn_conv_b, w_down, final_norm_w, loss_target, m_norm1_w, m_w_in, m_b_gate, m_attn_sinks, m_w_attn_o, m_ssd_conv_w, m_ssd_conv_b, m_dt_bias, m_a_log, m_d_skip, m_ssd_norm_w, m_w_ssd_o, m_w_out, m_norm2_w, m_w_up, m_ffn_conv_w, m_ffn_conv_b, m_w_down, m_final_norm_w, v_norm1_w, v_w_in, v_b_gate, v_attn_sinks, v_w_attn_o, v_ssd_conv_w, v_ssd_conv_b, v_dt_bias, v_a_log, v_d_skip, v_ssd_norm_w, v_w_ssd_o, v_w_out, v_norm2_w, v_w_up, v_ffn_conv_w, v_ffn_conv_b, v_w_down, v_final_norm_w):
    w = dict(norm1_w=norm1_w, w_in=w_in, b_gate=b_gate, attn_sinks=attn_sinks, w_attn_o=w_attn_o, ssd_conv_w=ssd_conv_w, ssd_conv_b=ssd_conv_b, dt_bias=dt_bias, a_log=a_log, d_skip=d_skip, ssd_norm_w=ssd_norm_w, w_ssd_o=w_ssd_o, w_out=w_out, norm2_w=norm2_w, w_up=w_up, ffn_conv_w=ffn_conv_w, ffn_conv_b=ffn_conv_b, w_down=w_down, final_norm_w=final_norm_w)
    m = dict(norm1_w=m_norm1_w, w_in=m_w_in, b_gate=m_b_gate, attn_sinks=m_attn_sinks, w_attn_o=m_w_attn_o, ssd_conv_w=m_ssd_conv_w, ssd_conv_b=m_ssd_conv_b, dt_bias=m_dt_bias, a_log=m_a_log, d_skip=m_d_skip, ssd_norm_w=m_ssd_norm_w, w_ssd_o=m_w_ssd_o, w_out=m_w_out, norm2_w=m_norm2_w, w_up=m_w_up, ffn_conv_w=m_ffn_conv_w, ffn_conv_b=m_ffn_conv_b, w_down=m_w_down, final_norm_w=m_final_norm_w)
    v = dict(norm1_w=v_norm1_w, w_in=v_w_in, b_gate=v_b_gate, attn_sinks=v_attn_sinks, w_attn_o=v_w_attn_o, ssd_conv_w=v_ssd_conv_w, ssd_conv_b=v_ssd_conv_b, dt_bias=v_dt_bias, a_log=v_a_log, d_skip=v_d_skip, ssd_norm_w=v_ssd_norm_w, w_ssd_o=v_w_ssd_o, w_out=v_w_out, norm2_w=v_norm2_w, w_up=v_w_up, ffn_conv_w=v_ffn_conv_w, ffn_conv_b=v_ffn_conv_b, w_down=v_w_down, final_norm_w=v_final_norm_w)
    me = 4 * lax.axis_index("x") + 2 * lax.axis_index("y") + lax.axis_index("c")
    conv_cols = XBC_DIM // N_DEV
    ffn_cols = 2 * D_FF // N_DEV

    gathered = _all_gather(
        [w_in[0].T.astype(BF16), w_up[0].T.astype(BF16), w_attn_o[0].astype(BF16), w_ssd_o[0].astype(BF16),
         w_out[0].astype(BF16), w_down[0].astype(BF16), ssd_conv_w[0], ffn_conv_w[0]], "gather_weights")
    in_t = _to_internal_rows(gathered[0].reshape(IN_DIM, D_MODEL))
    up_t = gathered[1].reshape(2 * D_FF, D_MODEL)
    ao = gathered[2].reshape(Q_DIM, D_MODEL)
    so = gathered[3].reshape(D_INNER, D_MODEL)
    out = gathered[4].reshape(D_MODEL, D_MODEL)
    down = gathered[5].reshape(D_FF, D_MODEL)
    wts = {"in_t": in_t, "in": in_t.T, "up_t": up_t, "up": up_t.T, "ao": ao, "ao_t": ao.T, "so": so, "so_t": so.T,
           "out": out, "out_t": out.T, "down": down, "down_t": down.T}
    small = {k: w[k][0] if k != "final_norm_w" else w[k] for k in SMALL}
    small["ssd_conv_w"] = jnp.transpose(gathered[6], (1, 0, 2)).reshape(SSD_CONV, XBC_DIM)
    small["ffn_conv_w"] = jnp.transpose(gathered[7], (1, 0, 2)).reshape(FFN_CONV, 2 * D_FF)

    loss, dx, g = _local_step(x[0].T, loss_target[0].T, wts, small)

    t_cols = lambda a, n: jnp.transpose(a.reshape(a.shape[0], N_DEV, n), (1, 0, 2))
    parts = _all_to_all(
        [t_cols(g["w_in"], IN_DIM // N_DEV), g["w_attn_o"].reshape(N_DEV, -1, D_MODEL),
         g["w_ssd_o"].reshape(N_DEV, -1, D_MODEL), g["w_out"].reshape(N_DEV, -1, D_MODEL),
         t_cols(g["w_up"], 2 * D_FF // N_DEV), g["w_down"].reshape(N_DEV, -1, D_MODEL)], "exchange_grads")
    res = {}
    for name, p in zip(SHARDED, parts):
        res[name] = _adamw_sharded(p, w[name][0], m[name][0], v[name][0], "adamw_" + name)

    packed = _pack([loss] + [g[k] for k in SMALL])
    total = _sum_slots(_all_gather([packed], "gather_small_grads")[0])
    tot = _unpack(total, [(1,)] + [SMALL_SHAPES[k] for k in SMALL])
    loss_sum = tot[0].reshape(())
    gs = dict(zip(SMALL, tot[1:]))
    gs["ssd_conv_w"] = lax.dynamic_slice_in_dim(gs["ssd_conv_w"], me * conv_cols, conv_cols, axis=2)
    gs["ffn_conv_w"] = lax.dynamic_slice_in_dim(gs["ffn_conv_w"], me * ffn_cols, ffn_cols, axis=2)
    upd = _adamw_small(_pack([gs[k] for k in SMALL]), _pack([w[k] for k in SMALL]), _pack([m[k] for k in SMALL]),
                       _pack([v[k] for k in SMALL]))
    shapes = [w[k].shape for k in SMALL]
    d_s, m_s, v_s = (dict(zip(SMALL, _unpack(u, shapes))) for u in upd)
    for k in SMALL:
        res[k] = (gs[k], d_s[k], m_s[k], v_s[k])

    grad_x = dx.T[None]
    outs = [loss_sum, grad_x]
    for i in range(4):
        for k in WEIGHT_ORDER:
            r = res[k][i]
            outs.append(r[None] if k in SHARDED else r)
    return tuple(outs)
```

```python
import functools

import jax
import jax.numpy as jnp
from jax import lax
from jax.experimental import pallas as pl
from jax.experimental.pallas import tpu as pltpu

F32 = jnp.float32
BF16 = jnp.bfloat16
HIGHEST = lax.Precision.HIGHEST

D_MODEL = 1024
N_Q_HEADS = 16
N_KV_HEADS = 4
HEAD_DIM = 64
WINDOW = 128
Q_PER_KV = N_Q_HEADS // N_KV_HEADS
Q_DIM = N_Q_HEADS * HEAD_DIM
KV_DIM = N_KV_HEADS * HEAD_DIM
D_INNER = 2048
SSD_HEAD_DIM = 64
N_SSD_HEADS = 32
N_SSD_GROUPS = 4
HEADS_PER_GROUP = N_SSD_HEADS // N_SSD_GROUPS
D_STATE = 128
BC_DIM = N_SSD_GROUPS * D_STATE
XBC_DIM = D_INNER + 2 * BC_DIM
SSD_CONV = 4
CHUNK = 128
D_FF = 2816
FFN_CONV = 3
EPS = 1e-5
NEG = -1e30
IN_DIM = 8736
N_DEV = 8

OFF_Q = 0
OFF_K = OFF_Q + Q_DIM
OFF_V = OFF_K + KV_DIM
OFF_Z = OFF_V + KV_DIM
OFF_X = OFF_Z + D_INNER
OFF_GA = OFF_X + XBC_DIM
OFF_GS = OFF_GA + D_MODEL
OFF_DT = OFF_GS + D_MODEL
IN_PAD = OFF_DT + 128
ORIG_DT = OFF_X + XBC_DIM
ORIG_GA = ORIG_DT + N_SSD_HEADS
ORIG_GS = ORIG_GA + D_MODEL

ADAM_LR = 0.001
ADAM_B1 = 0.9
ADAM_B2 = 0.999
ADAM_EPS = 1e-08
ADAM_WD = 0.01
ADAM_STEP = 10

VMEM_LIMIT = 48 * 1024 * 1024
MESH = pl.DeviceIdType.MESH


def _cparams(*sem):
    return pltpu.CompilerParams(dimension_semantics=sem, vmem_limit_bytes=VMEM_LIMIT)


def _tile(n, prefs):
    for p in prefs:
        if n % p == 0:
            return p
    return n


def _sigmoid(x):
    return 1.0 / (1.0 + jnp.exp(-x))


def _softplus(x):
    return jnp.maximum(x, 0.0) + jnp.log(1.0 + jnp.exp(-jnp.abs(x)))


def _rowsum(x):
    return jnp.sum(x, axis=1, keepdims=True)


def _colsum(x):
    return jnp.sum(x, axis=0, keepdims=True)


def _dot(a, b):
    return jnp.dot(a, b, preferred_element_type=F32)


def _dot_nt(a, b):
    return lax.dot_general(a, b, (((1,), (1,)), ((), ())), preferred_element_type=F32)


def _dot_tn(a, b):
    return lax.dot_general(a, b, (((0,), (0,)), ((), ())), preferred_element_type=F32)


def _shift_right(x, j):
    if j == 0:
        return x
    lane = lax.broadcasted_iota(jnp.int32, x.shape, 1)
    return jnp.where(lane >= j, pltpu.roll(x, j, 1), 0.0)


def _shift_left(x, j):
    if j == 0:
        return x
    n = x.shape[1]
    lane = lax.broadcasted_iota(jnp.int32, x.shape, 1)
    return jnp.where(lane < n - j, pltpu.roll(x, n - j, 1), 0.0)


def _matmul(a, b, *, nt, out_dtype, name, add=None, dep=None):
    m, k = a.shape
    n = b.shape[0] if nt else b.shape[1]
    tm = _tile(m, (512, 384, 256, 128))
    tn = _tile(n, (1024, 512, 384, 256, 128))
    tk = _tile(k, (512, 384, 256, 128))
    nk = k // tk

    def body(a_ref, b_ref, *rest):
        if dep is not None:
            rest = rest[1:]
        if add is None:
            o_ref, acc = rest
        else:
            r_ref, o_ref, acc = rest
        kk = pl.program_id(2)

        @pl.when(kk == 0)
        def _():
            acc[...] = jnp.zeros_like(acc)

        av = a_ref[...].astype(BF16)
        bv = b_ref[...].astype(BF16)
        acc[...] += _dot_nt(av, bv) if nt else _dot(av, bv)

        @pl.when(kk == nk - 1)
        def _():
            r = acc[...]
            if add is not None:
                r = r + r_ref[...]
            o_ref[...] = r.astype(out_dtype)

    in_specs = [
        pl.BlockSpec((tm, tk), lambda i, j, kk: (i, kk)),
        pl.BlockSpec((tn, tk), lambda i, j, kk: (j, kk)) if nt else pl.BlockSpec((tk, tn), lambda i, j, kk: (kk, j)),
    ]
    args = [a, b]
    if dep is not None:
        in_specs.append(pl.BlockSpec(memory_space=pl.ANY))
        args.append(dep)
    if add is not None:
        in_specs.append(pl.BlockSpec((tm, tn), lambda i, j, kk: (i, j)))
        args.append(add)
    return pl.pallas_call(
        body,
        name=name,
        grid=(m // tm, n // tn, nk),
        in_specs=in_specs,
        out_specs=pl.BlockSpec((tm, tn), lambda i, j, kk: (i, j)),
        out_shape=jax.ShapeDtypeStruct((m, n), out_dtype),
        scratch_shapes=[pltpu.VMEM((tm, tn), F32)],
        compiler_params=_cparams("parallel", "parallel", "arbitrary"),
    )(*args)


def _norm_fwd(x, w_col, name):
    f, t = x.shape
    tt = _tile(t, (512, 256, 128))

    def body(x_ref, w_ref, o_ref):
        xv = x_ref[...]
        r = lax.rsqrt(jnp.mean(xv * xv, axis=0, keepdims=True) + EPS)
        o_ref[...] = (xv * r * w_ref[...]).astype(BF16)

    return pl.pallas_call(
        body,
        name=name,
        grid=(t // tt,),
        in_specs=[pl.BlockSpec((f, tt), lambda i: (0, i)), pl.BlockSpec((f, 1), lambda i: (0, 0))],
        out_specs=pl.BlockSpec((f, tt), lambda i: (0, i)),
        out_shape=jax.ShapeDtypeStruct((f, t), BF16),
        compiler_params=_cparams("parallel"),
    )(x, w_col)


def _norm_bwd(dy, x, w_col, res, name):
    f, t = x.shape
    tt = _tile(t, (512, 256, 128))

    def body(dy_ref, x_ref, w_ref, res_ref, dx_ref, dw_ref):
        @pl.when(pl.program_id(0) == 0)
        def _():
            dw_ref[...] = jnp.zeros_like(dw_ref)

        xv = x_ref[...]
        r = lax.rsqrt(jnp.mean(xv * xv, axis=0, keepdims=True) + EPS)
        xhat = xv * r
        dyv = dy_ref[...]
        dw_ref[...] += _rowsum(dyv * xhat)
        dxhat = dyv * w_ref[...]
        dx_ref[...] = res_ref[...] + r * (dxhat - xhat * jnp.mean(dxhat * xhat, axis=0, keepdims=True))

    blk = pl.BlockSpec((f, tt), lambda i: (0, i))
    col = pl.BlockSpec((f, 1), lambda i: (0, 0))
    return pl.pallas_call(
        body,
        name=name,
        grid=(t // tt,),
        in_specs=[blk, blk, col, blk],
        out_specs=[blk, col],
        out_shape=[jax.ShapeDtypeStruct((f, t), F32), jax.ShapeDtypeStruct((f, 1), F32)],
        compiler_params=_cparams("arbitrary"),
    )(dy, x, w_col, res)


def _final_norm_loss(h, tgt, w_col):
    f, t = h.shape
    tt = _tile(t, (512, 256, 128))

    def body(h_ref, t_ref, w_ref, dh_ref, loss_ref, dw_ref):
        @pl.when(pl.program_id(0) == 0)
        def _():
            dw_ref[...] = jnp.zeros_like(dw_ref)
            loss_ref[...] = jnp.zeros_like(loss_ref)

        xv = h_ref[...]
        r = lax.rsqrt(jnp.mean(xv * xv, axis=0, keepdims=True) + EPS)
        xhat = xv * r
        wv = w_ref[...]
        err = xhat * wv - t_ref[...]
        loss_ref[...] += 0.5 * _rowsum(jnp.mean(err * err, axis=0, keepdims=True))
        dyv = err * (1.0 / f)
        dw_ref[...] += _rowsum(dyv * xhat)
        dxhat = dyv * wv
        dh_ref[...] = r * (dxhat - xhat * jnp.mean(dxhat * xhat, axis=0, keepdims=True))

    blk = pl.BlockSpec((f, tt), lambda i: (0, i))
    col = pl.BlockSpec((f, 1), lambda i: (0, 0))
    one = pl.BlockSpec((1, 1), lambda i: (0, 0))
    return pl.pallas_call(
        body,
        name="final_norm_loss",
        grid=(t // tt,),
        in_specs=[blk, blk, col],
        out_specs=[blk, one, col],
        out_shape=[jax.ShapeDtypeStruct((f, t), F32), jax.ShapeDtypeStruct((1, 1), F32), jax.ShapeDtypeStruct((f, 1), F32)],
        compiler_params=_cparams("arbitrary"),
    )(h, tgt, w_col)


def _attn_mask(n):
    shape = (2 * WINDOW, Q_PER_KV * WINDOW)
    si = lax.broadcasted_iota(jnp.int32, shape, 0)
    qi = lax.broadcasted_iota(jnp.int32, shape, 1) & (WINDOW - 1)
    dist = WINDOW + qi - si
    return (dist >= 0) & (dist < WINDOW) & ((si >= WINDOW) | (n > 0))


def _lane_cat(ref, row0, rows):
    return jnp.concatenate([ref[row0 + i * rows:row0 + (i + 1) * rows, :] for i in range(Q_PER_KV)], axis=1)


def _attn_fwd(proj, sinks):
    t = proj.shape[1]
    nb = t // WINDOW
    scale = HEAD_DIM ** -0.5

    def body(s_ref, q_ref, kc_ref, kp_ref, vc_ref, vp_ref, o_ref, lse_ref):
        n = pl.program_id(0)
        valid = _attn_mask(n)
        for g in range(N_KV_HEADS):
            rows = slice(g * HEAD_DIM, (g + 1) * HEAD_DIM)
            kt = jnp.concatenate([kp_ref[rows, :], kc_ref[rows, :]], axis=1).astype(BF16)
            vt = jnp.concatenate([vp_ref[rows, :], vc_ref[rows, :]], axis=1).astype(BF16)
            qcat = _lane_cat(q_ref, g * Q_PER_KV * HEAD_DIM, HEAD_DIM).astype(BF16)
            s = jnp.where(valid, _dot_tn(kt, qcat) * scale, NEG)
            sink = jnp.concatenate(
                [jnp.full((1, WINDOW), s_ref[g * Q_PER_KV + i], F32) for i in range(Q_PER_KV)], axis=1)
            m = jnp.maximum(jnp.max(s, axis=0, keepdims=True), sink)
            p = jnp.where(valid, jnp.exp(s - m), 0.0)
            denom = _colsum(p) + jnp.exp(sink - m)
            probs = (p / denom).astype(BF16)
            out = _dot(vt, probs)
            lse = m + jnp.log(denom)
            for i in range(Q_PER_KV):
                h = g * Q_PER_KV + i
                o_ref[h * HEAD_DIM:(h + 1) * HEAD_DIM, :] = out[:, i * WINDOW:(i + 1) * WINDOW]
                lse_ref[h:h + 1, :] = lse[:, i * WINDOW:(i + 1) * WINDOW]

    kb = OFF_K // KV_DIM
    vb = OFF_V // KV_DIM
    prev = lambda n: jnp.maximum(n - 1, 0)
    return pl.pallas_call(
        body,
        name="attn_fwd",
        grid=(nb,),
        in_specs=[
            pl.BlockSpec(memory_space=pltpu.SMEM),
            pl.BlockSpec((Q_DIM, WINDOW), lambda n: (0, n)),
            pl.BlockSpec((KV_DIM, WINDOW), lambda n: (kb, n)),
            pl.BlockSpec((KV_DIM, WINDOW), lambda n: (kb, prev(n))),
            pl.BlockSpec((KV_DIM, WINDOW), lambda n: (vb, n)),
            pl.BlockSpec((KV_DIM, WINDOW), lambda n: (vb, prev(n))),
        ],
        out_specs=[pl.BlockSpec((Q_DIM, WINDOW), lambda n: (0, n)), pl.BlockSpec((N_Q_HEADS, WINDOW), lambda n: (0, n))],
        out_shape=[jax.ShapeDtypeStruct((Q_DIM, t), F32), jax.ShapeDtypeStruct((N_Q_HEADS, t), F32)],
        compiler_params=_cparams("parallel"),
    )(sinks, proj, proj, proj, proj, proj)


def _attn_bwd(proj, sinks, out, lse, dout):
    t = proj.shape[1]
    nb = t // WINDOW
    scale = HEAD_DIM ** -0.5

    def body(s_ref, q_ref, kc_ref, kp_ref, vc_ref, vp_ref, o_ref, lse_ref, do_ref,
             dq_ref, dk_ref, dv_ref, ds_ref, dk_carry, dv_carry):
        step = pl.program_id(0)
        n = nb - 1 - step

        @pl.when(step == 0)
        def _():
            dk_carry[...] = jnp.zeros_like(dk_carry)
            dv_carry[...] = jnp.zeros_like(dv_carry)
            ds_ref[...] = jnp.zeros_like(ds_ref)

        valid = _attn_mask(n)
        for g in range(N_KV_HEADS):
            rows = slice(g * HEAD_DIM, (g + 1) * HEAD_DIM)
            q0 = g * Q_PER_KV * HEAD_DIM
            kt = jnp.concatenate([kp_ref[rows, :], kc_ref[rows, :]], axis=1).astype(BF16)
            vt = jnp.concatenate([vp_ref[rows, :], vc_ref[rows, :]], axis=1).astype(BF16)
            qcat = _lane_cat(q_ref, q0, HEAD_DIM).astype(BF16)
            ocat = _lane_cat(o_ref, q0, HEAD_DIM)
            docat = _lane_cat(do_ref, q0, HEAD_DIM)
            dob = docat.astype(BF16)
            lse_cat = jnp.concatenate(
                [lse_ref[g * Q_PER_KV + i:g * Q_PER_KV + i + 1, :] for i in range(Q_PER_KV)], axis=1)
            sink = jnp.concatenate(
                [jnp.full((1, WINDOW), s_ref[g * Q_PER_KV + i], F32) for i in range(Q_PER_KV)], axis=1)
            s = jnp.where(valid, _dot_tn(kt, qcat) * scale, NEG)
            p = jnp.where(valid, jnp.exp(s - lse_cat), 0.0)
            dp = _dot_tn(vt, dob)
            delta = _colsum(docat * ocat)
            dsc = (p * (dp - delta)).astype(BF16)
            dsink_row = -jnp.exp(sink - lse_cat) * delta
            dq = _dot(kt, dsc) * scale
            dk = _dot_nt(qcat, dsc) * scale
            dv = _dot_nt(dob, p.astype(BF16))
            for i in range(Q_PER_KV):
                h = g * Q_PER_KV + i
                dq_ref[h * HEAD_DIM:(h + 1) * HEAD_DIM, :] = dq[:, i * WINDOW:(i + 1) * WINDOW].astype(BF16)
                ds_ref[h:h + 1, :] += _rowsum(dsink_row[:, i * WINDOW:(i + 1) * WINDOW])
            dk_ref[rows, :] = (dk[:, WINDOW:] + dk_carry[rows, :]).astype(BF16)
            dv_ref[rows, :] = (dv[:, WINDOW:] + dv_carry[rows, :]).astype(BF16)
            dk_carry[rows, :] = dk[:, :WINDOW]
            dv_carry[rows, :] = dv[:, :WINDOW]

    kb = OFF_K // KV_DIM
    vb = OFF_V // KV_DIM
    cur = lambda i: nb - 1 - i
    prev = lambda i: jnp.maximum(nb - 2 - i, 0)
    qspec = pl.BlockSpec((Q_DIM, WINDOW), lambda i: (0, cur(i)))
    kvspec = pl.BlockSpec((KV_DIM, WINDOW), lambda i: (0, cur(i)))
    return pl.pallas_call(
        body,
        name="attn_bwd",
        grid=(nb,),
        in_specs=[
            pl.BlockSpec(memory_space=pltpu.SMEM),
            qspec,
            pl.BlockSpec((KV_DIM, WINDOW), lambda i: (kb, cur(i))),
            pl.BlockSpec((KV_DIM, WINDOW), lambda i: (kb, prev(i))),
            pl.BlockSpec((KV_DIM, WINDOW), lambda i: (vb, cur(i))),
            pl.BlockSpec((KV_DIM, WINDOW), lambda i: (vb, prev(i))),
            qspec,
            pl.BlockSpec((N_Q_HEADS, WINDOW), lambda i: (0, cur(i))),
            qspec,
        ],
        out_specs=[qspec, kvspec, kvspec, pl.BlockSpec((N_Q_HEADS, 1), lambda i: (0, 0))],
        out_shape=[
            jax.ShapeDtypeStruct((Q_DIM, t), BF16),
            jax.ShapeDtypeStruct((KV_DIM, t), BF16),
            jax.ShapeDtypeStruct((KV_DIM, t), BF16),
            jax.ShapeDtypeStruct((N_Q_HEADS, 1), F32),
        ],
        scratch_shapes=[pltpu.VMEM((KV_DIM, WINDOW), F32), pltpu.VMEM((KV_DIM, WINDOW), F32)],
        compiler_params=_cparams("arbitrary"),
    )(sinks, proj, proj, proj, proj, proj, out, lse, dout)


CONV_ROWS = 256


def _conv_silu_fwd(proj, w_col, b_col):
    t = proj.shape[1]
    r0 = OFF_X // CONV_ROWS

    def body(x_ref, w_ref, b_ref, o_ref):
        xv = x_ref[...]
        wv = w_ref[...]
        y = b_ref[...] + wv[:, SSD_CONV - 1:SSD_CONV] * xv
        for k in range(SSD_CONV - 1):
            y = y + wv[:, k:k + 1] * _shift_right(xv, SSD_CONV - 1 - k)
        o_ref[...] = y * _sigmoid(y)

    return pl.pallas_call(
        body,
        name="ssd_conv_fwd",
        grid=(XBC_DIM // CONV_ROWS,),
        in_specs=[
            pl.BlockSpec((CONV_ROWS, t), lambda i: (r0 + i, 0)),
            pl.BlockSpec((CONV_ROWS, SSD_CONV), lambda i: (i, 0)),
            pl.BlockSpec((CONV_ROWS, 1), lambda i: (i, 0)),
        ],
        out_specs=pl.BlockSpec((CONV_ROWS, t), lambda i: (i, 0)),
        out_shape=jax.ShapeDtypeStruct((XBC_DIM, t), F32),
        compiler_params=_cparams("parallel"),
    )(proj, w_col, b_col)


def _conv_silu_bwd(proj, w_col, b_col, dout, row0, name):
    t = proj.shape[1]
    nrows = dout.shape[0]
    p0 = (OFF_X + row0) // CONV_ROWS
    c0 = row0 // CONV_ROWS

    def body(x_ref, w_ref, b_ref, do_ref, dx_ref, dwb_ref):
        xv = x_ref[...]
        wv = w_ref[...]
        y = b_ref[...] + wv[:, SSD_CONV - 1:SSD_CONV] * xv
        for k in range(SSD_CONV - 1):
            y = y + wv[:, k:k + 1] * _shift_right(xv, SSD_CONV - 1 - k)
        sg = _sigmoid(y)
        dy = do_ref[...] * (sg * (1.0 + y * (1.0 - sg)))
        lane = lax.broadcasted_iota(jnp.int32, (CONV_ROWS, 128), 1)
        dwb = jnp.where(lane == SSD_CONV, _rowsum(dy), 0.0)
        dx = wv[:, SSD_CONV - 1:SSD_CONV] * dy
        dwb = jnp.where(lane == SSD_CONV - 1, _rowsum(dy * xv), dwb)
        for k in range(SSD_CONV - 1):
            j = SSD_CONV - 1 - k
            dx = dx + wv[:, k:k + 1] * _shift_left(dy, j)
            dwb = jnp.where(lane == k, _rowsum(dy * _shift_right(xv, j)), dwb)
        dx_ref[...] = dx.astype(BF16)
        dwb_ref[...] = dwb

    return pl.pallas_call(
        body,
        name=name,
        grid=(nrows // CONV_ROWS,),
        in_specs=[
            pl.BlockSpec((CONV_ROWS, t), lambda i: (p0 + i, 0)),
            pl.BlockSpec((CONV_ROWS, SSD_CONV), lambda i: (c0 + i, 0)),
            pl.BlockSpec((CONV_ROWS, 1), lambda i: (c0 + i, 0)),
            pl.BlockSpec((CONV_ROWS, t), lambda i: (i, 0)),
        ],
        out_specs=[pl.BlockSpec((CONV_ROWS, t), lambda i: (i, 0)), pl.BlockSpec((CONV_ROWS, 128), lambda i: (i, 0))],
        out_shape=[jax.ShapeDtypeStruct((nrows, t), BF16), jax.ShapeDtypeStruct((nrows, 128), F32)],
        compiler_params=_cparams("parallel"),
    )(proj, w_col, b_col, dout)


GROUP_ROWS = HEADS_PER_GROUP * SSD_HEAD_DIM


def _ssd_specs(nc, order):
    hb = D_INNER // D_STATE
    dtb = OFF_DT // HEADS_PER_GROUP
    col = pl.BlockSpec((HEADS_PER_GROUP, 1), lambda g, c: (g, 0))
    return [
        pl.BlockSpec((GROUP_ROWS, CHUNK), lambda g, c: (g, order(c))),
        pl.BlockSpec((D_STATE, CHUNK), lambda g, c: (hb + g, order(c))),
        pl.BlockSpec((D_STATE, CHUNK), lambda g, c: (hb + N_SSD_GROUPS + g, order(c))),
        pl.BlockSpec((HEADS_PER_GROUP, CHUNK), lambda g, c: (dtb + g, order(c))),
        col, col, col,
    ]


def _ssd_common(dt_ref, dtb_ref, alog_ref):
    z = dt_ref[...] + dtb_ref[...]
    dt = _softplus(z)
    a_neg = -jnp.exp(alog_ref[...])
    d_a = dt * a_neg
    row = lax.broadcasted_iota(jnp.int32, (CHUNK, CHUNK), 0)
    colm = lax.broadcasted_iota(jnp.int32, (CHUNK, CHUNK), 1)
    upper = (row <= colm).astype(F32)
    a_cs = jnp.dot(d_a, upper, precision=HIGHEST, preferred_element_type=F32)
    a_last = _rowsum(d_a)
    return z, dt, a_neg, a_cs, a_last, row >= colm, row == colm


def _decay(a_row, causal):
    a_s = jnp.broadcast_to(a_row, (CHUNK, CHUNK))
    seg = a_s.T - a_s
    return jnp.where(causal, jnp.exp(jnp.where(causal, seg, 0.0)), 0.0)


def _ssd_fwd(xbc, proj, dtb_col, alog_col, dsk_col):
    t = xbc.shape[1]
    nc = t // CHUNK

    def body(xs_ref, b_ref, c_ref, dt_ref, dtb_ref, alog_ref, dsk_ref, y_ref, hst_ref, h_scr):
        @pl.when(pl.program_id(1) == 0)
        def _():
            h_scr[...] = jnp.zeros_like(h_scr)

        _, dt, _, a_cs, a_last, causal, _ = _ssd_common(dt_ref, dtb_ref, alog_ref)
        bb = b_ref[...].astype(BF16)
        cb_ = c_ref[...].astype(BF16)
        cb = _dot_tn(cb_, bb)
        hst_ref[0, 0] = h_scr[...]
        dsk = dsk_ref[...]
        for j in range(HEADS_PER_GROUP):
            rows = slice(j * SSD_HEAD_DIM, (j + 1) * SSD_HEAD_DIM)
            a = a_cs[j:j + 1, :]
            m = (cb * _decay(a, causal)).astype(BF16)
            xs = xs_ref[rows, :]
            xc = xs * dt[j:j + 1, :]
            hj = h_scr[rows, :]
            y = _dot_nt(xc.astype(BF16), m) + _dot(hj.astype(BF16), cb_) * jnp.exp(a) + dsk[j:j + 1, :] * xs
            y_ref[rows, :] = y
            al = a_last[j:j + 1, :]
            w = jnp.exp(al - a)
            h_scr[rows, :] = jnp.exp(al) * hj + _dot_nt((xc * w).astype(BF16), bb)

    return pl.pallas_call(
        body,
        name="ssd_fwd",
        grid=(N_SSD_GROUPS, nc),
        in_specs=_ssd_specs(nc, lambda c: c),
        out_specs=[
            pl.BlockSpec((GROUP_ROWS, CHUNK), lambda g, c: (g, c)),
            pl.BlockSpec((1, 1, GROUP_ROWS, D_STATE), lambda g, c: (g, c, 0, 0)),
        ],
        out_shape=[
            jax.ShapeDtypeStruct((D_INNER, t), F32),
            jax.ShapeDtypeStruct((N_SSD_GROUPS, nc, GROUP_ROWS, D_STATE), F32),
        ],
        scratch_shapes=[pltpu.VMEM((GROUP_ROWS, D_STATE), F32)],
        compiler_params=_cparams("parallel", "arbitrary"),
    )(xbc, xbc, xbc, proj, dtb_col, alog_col, dsk_col)


def _ssd_bwd(xbc, proj, dtb_col, alog_col, dsk_col, hst, dy):
    t = xbc.shape[1]
    nc = t // CHUNK
    rev = lambda c: nc - 1 - c

    def body(xs_ref, b_ref, c_ref, dt_ref, dtb_ref, alog_ref, dsk_ref, hst_ref, dy_ref,
             dxs_ref, db_ref, dc_ref, ddt_ref, dalog_ref, ddsk_ref, ddtb_ref, dh_scr, da_scr, ddt_scr, dd_scr):
        @pl.when(pl.program_id(1) == 0)
        def _():
            dh_scr[...] = jnp.zeros_like(dh_scr)
            dalog_ref[...] = jnp.zeros_like(dalog_ref)
            ddsk_ref[...] = jnp.zeros_like(ddsk_ref)
            ddtb_ref[...] = jnp.zeros_like(ddtb_ref)

        z, dt, a_neg, a_cs, a_last, causal, eye = _ssd_common(dt_ref, dtb_ref, alog_ref)
        bb = b_ref[...].astype(BF16)
        cb_ = c_ref[...].astype(BF16)
        cb = _dot_tn(cb_, bb)
        dsk = dsk_ref[...]
        last_lane = lax.broadcasted_iota(jnp.int32, (1, CHUNK), 1) == CHUNK - 1
        dcb = jnp.zeros((CHUNK, CHUNK), F32)
        dc_acc = jnp.zeros((D_STATE, CHUNK), F32)
        db_acc = jnp.zeros((D_STATE, CHUNK), F32)
        for j in range(HEADS_PER_GROUP):
            rows = slice(j * SSD_HEAD_DIM, (j + 1) * SSD_HEAD_DIM)
            a = a_cs[j:j + 1, :]
            al = a_last[j:j + 1, :]
            lam = _decay(a, causal)
            mf = cb * lam
            xs = xs_ref[rows, :]
            dtj = dt[j:j + 1, :]
            xc = xs * dtj
            w = jnp.exp(al - a)
            e = jnp.exp(a)
            gam = jnp.exp(al)
            hj = hst_ref[0, 0, rows, :]
            hjb = hj.astype(BF16)
            dyv = dy_ref[rows, :]
            dyb = dyv.astype(BF16)
            dd_scr[j:j + 1, :] = _colsum(dyv * xs)
            gb = (dyv * e).astype(BF16)
            dh_in = _dot_nt(gb, cb_)
            dc_acc = dc_acc + _dot_tn(hjb, gb)
            yoff = _dot(hjb, cb_) * e
            da = _colsum(dyv * yoff)
            dm = _dot_tn(dyb, xc.astype(BF16))
            dxc = _dot(dyb, mf.astype(BF16))
            dcb = dcb + dm * lam
            nmat = dm * mf
            rs = jnp.broadcast_to(_rowsum(nmat), (CHUNK, CHUNK))
            da = da + _colsum(jnp.where(eye, rs, 0.0)) - _colsum(nmat)
            ds = dh_scr[rows, :]
            dsb = ds.astype(BF16)
            t1 = _dot(dsb, bb)
            xcw = xc * w
            dxc = dxc + w * t1
            dww = _colsum(xcw * t1)
            da_l = _rowsum(dww) + _rowsum(_colsum(ds * hj)) * gam
            da = da - dww + jnp.where(last_lane, da_l, 0.0)
            db_acc = db_acc + _dot_tn(dsb, xcw.astype(BF16))
            dh_scr[rows, :] = gam * ds + dh_in
            dxs_ref[rows, :] = dsk[j:j + 1, :] * dyv + dxc * dtj
            da_scr[j:j + 1, :] = da
            ddt_scr[j:j + 1, :] = _colsum(dxc * xs)
        dcbb = dcb.astype(BF16)
        dc_ref[...] = dc_acc + _dot_nt(bb, dcbb)
        db_ref[...] = db_acc + _dot(cb_, dcbb)
        dda = jnp.dot(da_scr[...], causal.astype(F32), precision=HIGHEST, preferred_element_type=F32)
        ddt = ddt_scr[...] + dda * a_neg
        ddt_raw = ddt * _sigmoid(z)
        ddt_ref[...] = ddt_raw
        ddtb_ref[...] += _rowsum(ddt_raw)
        dalog_ref[...] += _rowsum(dda * dt) * a_neg
        ddsk_ref[...] += _rowsum(dd_scr[...])

    col = pl.BlockSpec((HEADS_PER_GROUP, 1), lambda g, c: (g, 0))
    bc = pl.BlockSpec((D_STATE, CHUNK), lambda g, c: (g, rev(c)))
    xs_spec = pl.BlockSpec((GROUP_ROWS, CHUNK), lambda g, c: (g, rev(c)))
    small = pltpu.VMEM((HEADS_PER_GROUP, CHUNK), F32)
    return pl.pallas_call(
        body,
        name="ssd_bwd",
        grid=(N_SSD_GROUPS, nc),
        in_specs=_ssd_specs(nc, rev) + [
            pl.BlockSpec((1, 1, GROUP_ROWS, D_STATE), lambda g, c: (g, rev(c), 0, 0)),
            xs_spec,
        ],
        out_specs=[xs_spec, bc, bc, pl.BlockSpec((HEADS_PER_GROUP, CHUNK), lambda g, c: (g, rev(c))), col, col, col],
        out_shape=[
            jax.ShapeDtypeStruct((D_INNER, t), F32),
            jax.ShapeDtypeStruct((BC_DIM, t), F32),
            jax.ShapeDtypeStruct((BC_DIM, t), F32),
            jax.ShapeDtypeStruct((N_SSD_HEADS, t), F32),
            jax.ShapeDtypeStruct((N_SSD_HEADS, 1), F32),
            jax.ShapeDtypeStruct((N_SSD_HEADS, 1), F32),
            jax.ShapeDtypeStruct((N_SSD_HEADS, 1), F32),
        ],
        scratch_shapes=[pltpu.VMEM((GROUP_ROWS, D_STATE), F32), small, small, small],
        compiler_params=_cparams("parallel", "arbitrary"),
    )(xbc, xbc, xbc, proj, dtb_col, alog_col, dsk_col, hst, dy)


GN_ROWS = D_INNER // N_SSD_GROUPS


def _gnorm_fwd(y, proj, w_col):
    t = y.shape[1]
    tt = _tile(t, (512, 256, 128))
    z0 = OFF_Z // GN_ROWS

    def body(y_ref, z_ref, w_ref, o_ref):
        zv = z_ref[...]
        u = y_ref[...] * (zv * _sigmoid(zv))
        r = lax.rsqrt(jnp.mean(u * u, axis=0, keepdims=True) + EPS)
        o_ref[...] = (u * r * w_ref[...]).astype(BF16)

    blk = pl.BlockSpec((GN_ROWS, tt), lambda g, i: (g, i))
    return pl.pallas_call(
        body,
        name="gnorm_fwd",
        grid=(N_SSD_GROUPS, t // tt),
        in_specs=[blk, pl.BlockSpec((GN_ROWS, tt), lambda g, i: (z0 + g, i)), pl.BlockSpec((GN_ROWS, 1), lambda g, i: (g, 0))],
        out_specs=blk,
        out_shape=jax.ShapeDtypeStruct((D_INNER, t), BF16),
        compiler_params=_cparams("parallel", "parallel"),
    )(y, proj, w_col)


def _gnorm_bwd(dout, y, proj, w_col):
    t = y.shape[1]
    tt = _tile(t, (512, 256, 128))
    z0 = OFF_Z // GN_ROWS

    def body(do_ref, y_ref, z_ref, w_ref, dy_ref, dz_ref, dw_ref):
        @pl.when(pl.program_id(1) == 0)
        def _():
            dw_ref[...] = jnp.zeros_like(dw_ref)

        zv = z_ref[...]
        yv = y_ref[...]
        sg = _sigmoid(zv)
        sz = zv * sg
        u = yv * sz
        r = lax.rsqrt(jnp.mean(u * u, axis=0, keepdims=True) + EPS)
        xhat = u * r
        dov = do_ref[...]
        dw_ref[...] += _rowsum(dov * xhat)
        dxhat = dov * w_ref[...]
        du = r * (dxhat - xhat * jnp.mean(dxhat * xhat, axis=0, keepdims=True))
        dy_ref[...] = du * sz
        dz_ref[...] = (du * yv * (sg * (1.0 + zv * (1.0 - sg)))).astype(BF16)

    blk = pl.BlockSpec((GN_ROWS, tt), lambda g, i: (g, i))
    col = pl.BlockSpec((GN_ROWS, 1), lambda g, i: (g, 0))
    return pl.pallas_call(
        body,
        name="gnorm_bwd",
        grid=(N_SSD_GROUPS, t // tt),
        in_specs=[blk, blk, pl.BlockSpec((GN_ROWS, tt), lambda g, i: (z0 + g, i)), col],
        out_specs=[blk, blk, col],
        out_shape=[jax.ShapeDtypeStruct((D_INNER, t), F32), jax.ShapeDtypeStruct((D_INNER, t), BF16),
                   jax.ShapeDtypeStruct((D_INNER, 1), F32)],
        compiler_params=_cparams("parallel", "arbitrary"),
    )(dout, y, proj, w_col)


GATE_ROWS = 512


def _gate_specs(t, tt):
    ga0 = OFF_GA // GATE_ROWS
    gs0 = OFF_GS // GATE_ROWS
    nr = D_MODEL // GATE_ROWS
    blk = pl.BlockSpec((GATE_ROWS, tt), lambda r, i: (r, i))
    return blk, [
        pl.BlockSpec((GATE_ROWS, tt), lambda r, i: (ga0 + r, i)),
        pl.BlockSpec((GATE_ROWS, tt), lambda r, i: (gs0 + r, i)),
        pl.BlockSpec((GATE_ROWS, 1), lambda r, i: (r, 0)),
        pl.BlockSpec((GATE_ROWS, 1), lambda r, i: (nr + r, 0)),
        blk, blk,
    ]


def _gate_fwd(proj, b_col, attn, ssd):
    t = proj.shape[1]
    tt = _tile(t, (512, 256, 128))
    blk, specs = _gate_specs(t, tt)

    def body(ga_ref, gs_ref, ba_ref, bs_ref, a_ref, s_ref, o_ref):
        o_ref[...] = (_sigmoid(ga_ref[...] + ba_ref[...]) * a_ref[...]
                      + _sigmoid(gs_ref[...] + bs_ref[...]) * s_ref[...]).astype(BF16)

    return pl.pallas_call(
        body,
        name="gate_fwd",
        grid=(D_MODEL // GATE_ROWS, t // tt),
        in_specs=specs,
        out_specs=blk,
        out_shape=jax.ShapeDtypeStruct((D_MODEL, t), BF16),
        compiler_params=_cparams("parallel", "parallel"),
    )(proj, proj, b_col, b_col, attn, ssd)


def _gate_bwd(proj, b_col, attn, ssd, dmix):
    t = proj.shape[1]
    tt = _tile(t, (512, 256, 128))
    blk, specs = _gate_specs(t, tt)
    nr = D_MODEL // GATE_ROWS

    def body(ga_ref, gs_ref, ba_ref, bs_ref, a_ref, s_ref, dm_ref, da_ref, dso_ref, dga_ref, dgs_ref, dba_ref, dbs_ref):
        @pl.when(pl.program_id(1) == 0)
        def _():
            dba_ref[...] = jnp.zeros_like(dba_ref)
            dbs_ref[...] = jnp.zeros_like(dbs_ref)

        dm = dm_ref[...]
        sa = _sigmoid(ga_ref[...] + ba_ref[...])
        ss = _sigmoid(gs_ref[...] + bs_ref[...])
        da_ref[...] = (dm * sa).astype(BF16)
        dso_ref[...] = (dm * ss).astype(BF16)
        dga = dm * a_ref[...] * sa * (1.0 - sa)
        dgs = dm * s_ref[...] * ss * (1.0 - ss)
        dga_ref[...] = dga.astype(BF16)
        dgs_ref[...] = dgs.astype(BF16)
        dba_ref[...] += _rowsum(dga)
        dbs_ref[...] += _rowsum(dgs)

    col = pl.BlockSpec((GATE_ROWS, 1), lambda r, i: (r, 0))
    act = jax.ShapeDtypeStruct((D_MODEL, t), BF16)
    bias = jax.ShapeDtypeStruct((D_MODEL, 1), F32)
    return pl.pallas_call(
        body,
        name="gate_bwd",
        grid=(nr, t // tt),
        in_specs=specs + [blk],
        out_specs=[blk, blk, blk, blk, col, col],
        out_shape=[act, act, act, act, bias, bias],
        compiler_params=_cparams("parallel", "arbitrary"),
    )(proj, proj, b_col, b_col, attn, ssd, dmix)


FFN_ROWS = 256


def _ffn_conv(u_ref, w_ref, b_ref, half):
    xv = u_ref[half]
    wv = w_ref[half]
    y = b_ref[half] + wv[:, FFN_CONV - 1:FFN_CONV] * xv
    for k in range(FFN_CONV - 1):
        y = y + wv[:, k:k + 1] * _shift_right(xv, FFN_CONV - 1 - k)
    return xv, wv, y


def _ffn_fwd(u0, w_col, b_col):
    t = u0.shape[2]

    def body(u_ref, w_ref, b_ref, o_ref):
        _, _, val = _ffn_conv(u_ref, w_ref, b_ref, 0)
        _, _, gt = _ffn_conv(u_ref, w_ref, b_ref, 1)
        o_ref[...] = (gt * _sigmoid(gt) * val).astype(BF16)

    return pl.pallas_call(
        body,
        name="ffn_fwd",
        grid=(D_FF // FFN_ROWS,),
        in_specs=[
            pl.BlockSpec((2, FFN_ROWS, t), lambda i: (0, i, 0)),
            pl.BlockSpec((2, FFN_ROWS, FFN_CONV), lambda i: (0, i, 0)),
            pl.BlockSpec((2, FFN_ROWS, 1), lambda i: (0, i, 0)),
        ],
        out_specs=pl.BlockSpec((FFN_ROWS, t), lambda i: (i, 0)),
        out_shape=jax.ShapeDtypeStruct((D_FF, t), BF16),
        compiler_params=_cparams("parallel"),
    )(u0, w_col, b_col)


def _ffn_bwd(u0, w_col, b_col, dg):
    t = u0.shape[2]

    def body(u_ref, w_ref, b_ref, dg_ref, du_ref, dwb_ref):
        xval, wval, val = _ffn_conv(u_ref, w_ref, b_ref, 0)
        xgt, wgt, gt = _ffn_conv(u_ref, w_ref, b_ref, 1)
        sg = _sigmoid(gt)
        dgv = dg_ref[...]
        dval = dgv * (gt * sg)
        dgt = dgv * val * (sg * (1.0 + gt * (1.0 - sg)))
        lane = lax.broadcasted_iota(jnp.int32, (FFN_ROWS, 128), 1)
        for half, xv, wv, dy in ((0, xval, wval, dval), (1, xgt, wgt, dgt)):
            dwb = jnp.where(lane == FFN_CONV, _rowsum(dy), 0.0)
            dx = wv[:, FFN_CONV - 1:FFN_CONV] * dy
            dwb = jnp.where(lane == FFN_CONV - 1, _rowsum(dy * xv), dwb)
            for k in range(FFN_CONV - 1):
                j = FFN_CONV - 1 - k
                dx = dx + wv[:, k:k + 1] * _shift_left(dy, j)
                dwb = jnp.where(lane == k, _rowsum(dy * _shift_right(xv, j)), dwb)
            du_ref[half] = dx.astype(BF16)
            dwb_ref[half] = dwb

    return pl.pallas_call(
        body,
        name="ffn_bwd",
        grid=(D_FF // FFN_ROWS,),
        in_specs=[
            pl.BlockSpec((2, FFN_ROWS, t), lambda i: (0, i, 0)),
            pl.BlockSpec((2, FFN_ROWS, FFN_CONV), lambda i: (0, i, 0)),
            pl.BlockSpec((2, FFN_ROWS, 1), lambda i: (0, i, 0)),
            pl.BlockSpec((FFN_ROWS, t), lambda i: (i, 0)),
        ],
        out_specs=[pl.BlockSpec((2, FFN_ROWS, t), lambda i: (0, i, 0)), pl.BlockSpec((2, FFN_ROWS, 128), lambda i: (0, i, 0))],
        out_shape=[jax.ShapeDtypeStruct((2, D_FF, t), BF16), jax.ShapeDtypeStruct((2, D_FF, 128), F32)],
        compiler_params=_cparams("parallel"),
    )(u0, w_col, b_col, dg)


def _adamw_math(w, g, m, v):
    m = ADAM_B1 * m + (1.0 - ADAM_B1) * g
    v = ADAM_B2 * v + (1.0 - ADAM_B2) * (g * g)
    m_hat = m / (1.0 - ADAM_B1 ** ADAM_STEP)
    v_hat = v / (1.0 - ADAM_B2 ** ADAM_STEP)
    delta = -ADAM_LR * (m_hat / (jnp.sqrt(v_hat) + ADAM_EPS) + ADAM_WD * w)
    return delta, m, v


def _adamw_sharded(parts, w, m, v, name):
    r, c = w.shape
    tr = _tile(r, (256, 128, 64, 32, 16))

    def body(p_ref, w_ref, m_ref, v_ref, g_ref, d_ref, nm_ref, nv_ref):
        g = p_ref[0].astype(F32)
        for s in range(1, N_DEV):
            g = g + p_ref[s].astype(F32)
        g_ref[...] = g
        d_ref[...], nm_ref[...], nv_ref[...] = _adamw_math(w_ref[...], g, m_ref[...], v_ref[...])

    blk = pl.BlockSpec((tr, c), lambda i: (i, 0))
    out = jax.ShapeDtypeStruct((r, c), F32)
    return pl.pallas_call(
        body,
        name=name,
        grid=(r // tr,),
        in_specs=[pl.BlockSpec((N_DEV, tr, c), lambda i: (0, i, 0)), blk, blk, blk],
        out_specs=[blk, blk, blk, blk],
        out_shape=[out, out, out, out],
        compiler_params=_cparams("parallel"),
    )(parts, w, m, v)


def _sum_slots(parts):
    _, r, c = parts.shape

    def body(p_ref, o_ref):
        g = p_ref[0]
        for s in range(1, N_DEV):
            g = g + p_ref[s]
        o_ref[...] = g

    return pl.pallas_call(body, name="sum_small_grads", out_shape=jax.ShapeDtypeStruct((r, c), F32))(parts)


def _adamw_small(g, w, m, v):
    def body(g_ref, w_ref, m_ref, v_ref, d_ref, nm_ref, nv_ref):
        d_ref[...], nm_ref[...], nv_ref[...] = _adamw_math(w_ref[...], g_ref[...], m_ref[...], v_ref[...])

    out = jax.ShapeDtypeStruct(g.shape, F32)
    return pl.pallas_call(body, name="adamw_small", out_shape=[out, out, out])(g, w, m, v)


ANY = pl.BlockSpec(memory_space=pl.ANY)
FLIPS = [(k >> 2 & 1, k >> 1 & 1, k & 1) for k in range(1, N_DEV)]


def _place():
    return lax.axis_index("x"), lax.axis_index("y"), lax.axis_index("c")


def _all_gather(arrays, name):
    n = len(arrays)

    def body(*refs):
        ins, outs = refs[:n], refs[n:2 * n]
        send_sems, recv_sems, local_sems = refs[2 * n:]
        x, y, c = _place()
        me = 4 * x + 2 * y + c
        local = [pltpu.make_async_copy(ins[i], outs[i].at[me], local_sems.at[i]) for i in range(n)]
        for cp in local:
            cp.start()
        sends = []
        for k, (fx, fy, fc) in enumerate(FLIPS):
            for i in range(n):
                cp = pltpu.make_async_remote_copy(
                    src_ref=ins[i], dst_ref=outs[i].at[me], send_sem=send_sems.at[i, k], recv_sem=recv_sems.at[i, k],
                    device_id=(x ^ fx, y ^ fy, c ^ fc), device_id_type=MESH)
                cp.start()
                sends.append(cp)
        for k, (fx, fy, fc) in enumerate(FLIPS):
            src = 4 * (x ^ fx) + 2 * (y ^ fy) + (c ^ fc)
            for i in range(n):
                pltpu.make_async_remote_copy(
                    src_ref=ins[i], dst_ref=outs[i].at[src], send_sem=send_sems.at[i, k], recv_sem=recv_sems.at[i, k],
                    device_id=(x ^ fx, y ^ fy, c ^ fc), device_id_type=MESH).wait_recv()
        for cp in sends:
            cp.wait_send()
        for cp in local:
            cp.wait()

    return pl.pallas_call(
        body,
        name=name,
        in_specs=[ANY] * n,
        out_specs=[ANY] * n,
        out_shape=[jax.ShapeDtypeStruct((N_DEV,) + a.shape, a.dtype) for a in arrays],
        scratch_shapes=[pltpu.SemaphoreType.DMA((n, N_DEV - 1)), pltpu.SemaphoreType.DMA((n, N_DEV - 1)),
                        pltpu.SemaphoreType.DMA((n,))],
    )(*arrays)


HBM = pl.BlockSpec(memory_space=pltpu.HBM)
SEM = pl.BlockSpec(memory_space=pltpu.SEMAPHORE)
EFFECT = pltpu.SideEffectType.DATAFLOW_SIDE_EFFECTING


def _peer_copy(gather, src_ref, land_ref, send_sems, recv_sems, k, sending):
    x, y, c = _place()
    fx, fy, fc = FLIPS[k]
    me = 4 * x + 2 * y + c
    peer = 4 * (x ^ fx) + 2 * (y ^ fy) + (c ^ fc)
    return pltpu.make_async_remote_copy(
        src_ref=src_ref if gather else src_ref.at[peer],
        dst_ref=land_ref.at[me if sending else peer],
        send_sem=send_sems.at[k], recv_sem=recv_sems.at[k],
        device_id=(x ^ fx, y ^ fy, c ^ fc), device_id_type=MESH)


def _exchange_start(srcs, gather, name):
    n = len(srcs)
    lands = [lax.empty((N_DEV,) + s.shape if gather else s.shape, s.dtype) for s in srcs]

    def body(*refs):
        src_refs, land_refs = refs[:n], refs[n:2 * n]
        send, recv = refs[2 * n:3 * n], refs[3 * n:4 * n]
        token, local_sems = refs[6 * n], refs[6 * n + 1]
        x, y, c = _place()
        me = 4 * x + 2 * y + c
        local = [pltpu.make_async_copy(src_refs[i] if gather else src_refs[i].at[me], land_refs[i].at[me],
                                       local_sems.at[i]) for i in range(n)]
        for cp in local:
            cp.start()
        for i in range(n):
            for k in range(N_DEV - 1):
                _peer_copy(gather, src_refs[i], land_refs[i], send[i], recv[i], k, True).start()
        for cp in local:
            cp.wait()
        token[...] = jnp.zeros_like(token)

    sem = pltpu.SemaphoreType.DMA((N_DEV - 1,))
    hbm = lambda a: pltpu.HBM(a.shape, a.dtype)
    res = pl.pallas_call(
        body,
        name=name,
        in_specs=[HBM] * (2 * n),
        out_specs=[SEM] * (2 * n) + [HBM] * (2 * n) + [pl.BlockSpec(memory_space=pltpu.VMEM)],
        out_shape=[sem] * (2 * n) + [hbm(s) for s in srcs] + [hbm(a) for a in lands] + [jax.ShapeDtypeStruct((8, 128), F32)],
        input_output_aliases={i: 2 * n + i for i in range(2 * n)},
        scratch_shapes=[pltpu.SemaphoreType.DMA((n,))],
        compiler_params=pltpu.CompilerParams(has_side_effects=EFFECT),
    )(*[pltpu.with_memory_space_constraint(a, pltpu.HBM) for a in list(srcs) + lands])
    return res[:n], res[n:2 * n], res[2 * n:3 * n], res[3 * n:4 * n], res[4 * n]


def _exchange_wait(send_sems, recv_sems, src, land, after, gather, name):
    def body(src_ref, land_ref, send_ref, recv_ref, after_ref, src_out, land_out):
        for k in range(N_DEV - 1):
            cp = _peer_copy(gather, src_ref, land_ref, send_ref, recv_ref, k, False)
            cp.wait_send()
            cp.wait_recv()

    hbm = lambda a: pltpu.HBM(a.shape, a.dtype)
    return pl.pallas_call(
        body,
        name=name,
        in_specs=[HBM, HBM, SEM, SEM, ANY],
        out_specs=[HBM, HBM],
        out_shape=[hbm(src), hbm(land)],
        input_output_aliases={0: 0, 1: 1},
        compiler_params=pltpu.CompilerParams(has_side_effects=EFFECT),
    )(src, land, send_sems, recv_sems, after)[1]


def _col(v):
    return v.reshape(-1, 1).astype(F32)


def _to_internal_rows(w_t):
    pad = jnp.zeros((IN_PAD - IN_DIM, w_t.shape[1]), w_t.dtype)
    return jnp.concatenate([w_t[:ORIG_DT], w_t[ORIG_GA:ORIG_GS], w_t[ORIG_GS:IN_DIM], w_t[ORIG_DT:ORIG_GA], pad], axis=0)


def _from_internal_cols(g):
    return jnp.concatenate([g[:, :OFF_GA], g[:, OFF_DT:OFF_DT + N_SSD_HEADS], g[:, OFF_GA:OFF_GS], g[:, OFF_GS:OFF_DT]], axis=1)


def _local_step(xt, tgt, weight, small, grad_ready):
    t = xt.shape[1]
    n1 = _col(small["norm1_w"])
    n2 = _col(small["norm2_w"])
    nf = _col(small["final_norm_w"])
    bg = _col(small["b_gate"])
    sinks = small["attn_sinks"].reshape(-1).astype(F32)
    cbias = _col(small["ssd_conv_b"])
    dtb = _col(small["dt_bias"])
    alog = _col(small["a_log"])
    dsk = _col(small["d_skip"])
    gnw = _col(small["ssd_norm_w"])
    fb = small["ffn_conv_b"].reshape(2, D_FF, 1)

    xn = _norm_fwd(xt, n1, "norm1_fwd")
    cw = weight("ssd_conv_w", xn).T
    fw = weight("ffn_conv_w", xn).T.reshape(2, D_FF, FFN_CONV)
    w_in, w_in_t = weight("w_in", xn)
    proj = _matmul(w_in_t, xn, nt=False, out_dtype=F32, name="mm_in")
    ao, lse = _attn_fwd(proj, sinks)
    w_ao, w_ao_t = weight("w_attn_o", ao)
    attn = _matmul(w_ao_t, ao, nt=False, out_dtype=F32, name="mm_attn_o")
    xbc = _conv_silu_fwd(proj, cw, cbias)
    y, hst = _ssd_fwd(xbc, proj, dtb, alog, dsk)
    yn = _gnorm_fwd(y, proj, gnw)
    w_so, w_so_t = weight("w_ssd_o", yn)
    ssd = _matmul(w_so_t, yn, nt=False, out_dtype=F32, name="mm_ssd_o")
    mix = _gate_fwd(proj, bg, attn, ssd)
    w_out, w_out_t = weight("w_out", mix)
    h1 = _matmul(w_out_t, mix, nt=False, out_dtype=F32, name="mm_out", add=xt)
    hn = _norm_fwd(h1, n2, "norm2_fwd")
    w_up, w_up_t = weight("w_up", hn)
    u0 = _matmul(w_up_t, hn, nt=False, out_dtype=F32, name="mm_up").reshape(2, D_FF, t)
    gl = _ffn_fwd(u0, fw, fb)
    w_down, w_down_t = weight("w_down", gl)
    h2 = _matmul(w_down_t, gl, nt=False, out_dtype=F32, name="mm_down", add=h1)
    dh2, loss, d_nf = _final_norm_loss(h2, tgt, nf)

    g = {}
    dep = grad_ready("w_down", _matmul(gl, dh2, nt=True, out_dtype=BF16, name="mm_d_w_down"))
    dgl = _matmul(w_down, dh2, nt=False, out_dtype=F32, name="mm_d_glu", dep=dep)
    du0, d_fwb = _ffn_bwd(u0, fw, fb, dgl)
    du0 = du0.reshape(2 * D_FF, t)
    dep = grad_ready("w_up", _matmul(hn, du0, nt=True, out_dtype=BF16, name="mm_d_w_up"))
    dhn = _matmul(w_up, du0, nt=False, out_dtype=F32, name="mm_d_hn", dep=dep)
    dh1, d_n2 = _norm_bwd(dhn, h1, n2, dh2, "norm2_bwd")
    dep = grad_ready("w_out", _matmul(mix, dh1, nt=True, out_dtype=BF16, name="mm_d_w_out"))
    dmix = _matmul(w_out, dh1, nt=False, out_dtype=F32, name="mm_d_mix", dep=dep)
    d_attn, d_ssd, d_ga, d_gs, d_ba, d_bs = _gate_bwd(proj, bg, attn, ssd, dmix)
    dep = grad_ready("w_attn_o", _matmul(ao, d_attn, nt=True, out_dtype=BF16, name="mm_d_w_attn_o"))
    dao = _matmul(w_ao, d_attn, nt=False, out_dtype=F32, name="mm_d_ao", dep=dep)
    dq, dk, dv, d_sinks = _attn_bwd(proj, sinks, ao, lse, dao)
    dep = grad_ready("w_ssd_o", _matmul(yn, d_ssd, nt=True, out_dtype=BF16, name="mm_d_w_ssd_o"))
    dyn = _matmul(w_so, d_ssd, nt=False, out_dtype=F32, name="mm_d_yn", dep=dep)
    dy, dz, d_gnw = _gnorm_bwd(dyn, y, proj, gnw)
    dxs, dbm, dcm, ddt, d_alog, d_dsk, d_dtb = _ssd_bwd(xbc, proj, dtb, alog, dsk, hst, dy)
    dx_xs, dwb_xs = _conv_silu_bwd(proj, cw, cbias, dxs, 0, "ssd_conv_bwd_x")
    dx_b, dwb_b = _conv_silu_bwd(proj, cw, cbias, dbm, D_INNER, "ssd_conv_bwd_b")
    dx_c, dwb_c = _conv_silu_bwd(proj, cw, cbias, dcm, D_INNER + BC_DIM, "ssd_conv_bwd_c")
    dwb_conv = jnp.concatenate([dwb_xs, dwb_b, dwb_c], axis=0)
    ddt_rows = jnp.concatenate([ddt.astype(BF16), jnp.zeros((IN_PAD - OFF_DT - N_SSD_HEADS, t), BF16)], axis=0)
    dproj = jnp.concatenate([dq, dk, dv, dz, dx_xs, dx_b, dx_c, d_ga, d_gs, ddt_rows], axis=0)
    dep = grad_ready("w_in", _from_internal_cols(_matmul(xn, dproj, nt=True, out_dtype=BF16, name="mm_d_w_in")))
    dxn = _matmul(w_in, dproj, nt=False, out_dtype=F32, name="mm_d_xn", dep=dep)
    dx, d_n1 = _norm_bwd(dxn, xt, n1, dh1, "norm1_bwd")

    g["norm1_w"] = d_n1
    g["b_gate"] = jnp.concatenate([d_ba, d_bs], axis=0)
    g["attn_sinks"] = d_sinks
    g["ssd_conv_w"] = dwb_conv[:, :SSD_CONV].T
    g["ssd_conv_b"] = dwb_conv[:, SSD_CONV]
    g["dt_bias"] = d_dtb
    g["a_log"] = d_alog
    g["d_skip"] = d_dsk
    g["ssd_norm_w"] = d_gnw
    g["norm2_w"] = d_n2
    d_fwb = d_fwb.reshape(2 * D_FF, 128)
    g["ffn_conv_w"] = d_fwb[:, :FFN_CONV].T
    g["ffn_conv_b"] = d_fwb[:, FFN_CONV]
    g["final_norm_w"] = d_nf
    return loss, dx, g


SHARDED = ("w_in", "w_attn_o", "w_ssd_o", "w_out", "w_up", "w_down")
SMALL = ("norm1_w", "b_gate", "attn_sinks", "ssd_conv_w", "ssd_conv_b", "dt_bias", "a_log", "d_skip", "ssd_norm_w",
         "norm2_w", "ffn_conv_w", "ffn_conv_b", "final_norm_w")
SMALL_SHAPES = {"norm1_w": (1, D_MODEL), "b_gate": (1, 2 * D_MODEL), "attn_sinks": (1, N_Q_HEADS),
                "ssd_conv_w": (1, SSD_CONV, XBC_DIM), "ssd_conv_b": (1, XBC_DIM), "dt_bias": (1, N_SSD_HEADS),
                "a_log": (1, N_SSD_HEADS), "d_skip": (1, N_SSD_HEADS), "ssd_norm_w": (1, D_INNER),
                "norm2_w": (1, D_MODEL), "ffn_conv_w": (1, FFN_CONV, 2 * D_FF), "ffn_conv_b": (1, 2 * D_FF),
                "final_norm_w": (D_MODEL,)}
WEIGHT_ORDER = ("norm1_w", "w_in", "b_gate", "attn_sinks", "w_attn_o", "ssd_conv_w", "ssd_conv_b", "dt_bias", "a_log",
                "d_skip", "ssd_norm_w", "w_ssd_o", "w_out", "norm2_w", "w_up", "ffn_conv_w", "ffn_conv_b", "w_down",
                "final_norm_w")


def _pack(parts):
    flat = jnp.concatenate([p.reshape(-1).astype(F32) for p in parts])
    rows = -(-flat.shape[0] // 1024) * 8
    return jnp.pad(flat, (0, rows * 128 - flat.shape[0])).reshape(rows, 128)


def _unpack(packed, shapes):
    flat = packed.reshape(-1)
    out, pos = [], 0
    for shp in shapes:
        size = 1
        for d in shp:
            size *= d
        out.append(flat[pos:pos + size].reshape(shp))
        pos += size
    return out


def kernel(x, norm1_w, w_in, b_gate, attn_sinks, w_attn_o, ssd_conv_w, ssd_conv_b, dt_bias, a_log, d_skip, ssd_norm_w, w_ssd_o, w_out, norm2_w, w_up, ffn_conv_w, ffn_conv_b, w_down, final_norm_w, loss_target, m_norm1_w, m_w_in, m_b_gate, m_attn_sinks, m_w_attn_o, m_ssd_conv_w, m_ssd_conv_b, m_dt_bias, m_a_log, m_d_skip, m_ssd_norm_w, m_w_ssd_o, m_w_out, m_norm2_w, m_w_up, m_ffn_conv_w, m_ffn_conv_b, m_w_down, m_final_norm_w, v_norm1_w, v_w_in, v_b_gate, v_attn_sinks, v_w_attn_o, v_ssd_conv_w, v_ssd_conv_b, v_dt_bias, v_a_log, v_d_skip, v_ssd_norm_w, v_w_ssd_o, v_w_out, v_norm2_w, v_w_up, v_ffn_conv_w, v_ffn_conv_b, v_w_down, v_final_norm_w):
    w = dict(norm1_w=norm1_w, w_in=w_in, b_gate=b_gate, attn_sinks=attn_sinks, w_attn_o=w_attn_o, ssd_conv_w=ssd_conv_w, ssd_conv_b=ssd_conv_b, dt_bias=dt_bias, a_log=a_log, d_skip=d_skip, ssd_norm_w=ssd_norm_w, w_ssd_o=w_ssd_o, w_out=w_out, norm2_w=norm2_w, w_up=w_up, ffn_conv_w=ffn_conv_w, ffn_conv_b=ffn_conv_b, w_down=w_down, final_norm_w=final_norm_w)
    m = dict(norm1_w=m_norm1_w, w_in=m_w_in, b_gate=m_b_gate, attn_sinks=m_attn_sinks, w_attn_o=m_w_attn_o, ssd_conv_w=m_ssd_conv_w, ssd_conv_b=m_ssd_conv_b, dt_bias=m_dt_bias, a_log=m_a_log, d_skip=m_d_skip, ssd_norm_w=m_ssd_norm_w, w_ssd_o=m_w_ssd_o, w_out=m_w_out, norm2_w=m_norm2_w, w_up=m_w_up, ffn_conv_w=m_ffn_conv_w, ffn_conv_b=m_ffn_conv_b, w_down=m_w_down, final_norm_w=m_final_norm_w)
    v = dict(norm1_w=v_norm1_w, w_in=v_w_in, b_gate=v_b_gate, attn_sinks=v_attn_sinks, w_attn_o=v_w_attn_o, ssd_conv_w=v_ssd_conv_w, ssd_conv_b=v_ssd_conv_b, dt_bias=v_dt_bias, a_log=v_a_log, d_skip=v_d_skip, ssd_norm_w=v_ssd_norm_w, w_ssd_o=v_w_ssd_o, w_out=v_w_out, norm2_w=v_norm2_w, w_up=v_w_up, ffn_conv_w=v_ffn_conv_w, ffn_conv_b=v_ffn_conv_b, w_down=v_w_down, final_norm_w=v_final_norm_w)
    me = 4 * lax.axis_index("x") + 2 * lax.axis_index("y") + lax.axis_index("c")
    conv_cols = XBC_DIM // N_DEV
    ffn_cols = 2 * D_FF // N_DEV

    shards = {"ssd_conv_w": ssd_conv_w[0], "ffn_conv_w": ffn_conv_w[0], "w_in": w_in[0].T.astype(BF16),
              "w_attn_o": w_attn_o[0].astype(BF16), "w_ssd_o": w_ssd_o[0].astype(BF16), "w_out": w_out[0].astype(BF16),
              "w_up": w_up[0].T.astype(BF16), "w_down": w_down[0].astype(BF16)}
    order = list(shards)
    g_send, g_recv, g_src, g_land, _ = _exchange_start(list(shards.values()), True, "gather_start")

    def weight(name, after):
        i = order.index(name)
        land = _exchange_wait(g_send[i], g_recv[i], g_src[i], g_land[i], after, True, "gather_wait_" + name)
        if name == "ssd_conv_w":
            return jnp.transpose(land, (1, 0, 2)).reshape(SSD_CONV, XBC_DIM)
        if name == "ffn_conv_w":
            return jnp.transpose(land, (1, 0, 2)).reshape(FFN_CONV, 2 * D_FF)
        if name == "w_in":
            w_t = _to_internal_rows(land.reshape(IN_DIM, D_MODEL))
            return w_t.T, w_t
        if name == "w_up":
            w_t = land.reshape(2 * D_FF, D_MODEL)
            return w_t.T, w_t
        full = land.reshape(-1, D_MODEL)
        return full, full.T

    pending = {}

    def grad_ready(name, grad):
        if name in ("w_in", "w_up"):
            chunks = jnp.transpose(grad.reshape(grad.shape[0], N_DEV, -1), (1, 0, 2))
        else:
            chunks = grad.reshape(N_DEV, -1, D_MODEL)
        send, recv, src, land, token = _exchange_start([chunks], False, "grad_start_" + name)
        pending[name] = (send[0], recv[0], src[0], land[0])
        return token

    small = {k: w[k][0] if k != "final_norm_w" else w[k] for k in SMALL}
    loss, dx, g = _local_step(x[0].T, loss_target[0].T, weight, small, grad_ready)

    packed = _pack([loss] + [g[k] for k in SMALL])
    total = _sum_slots(_all_gather([packed], "gather_small_grads")[0])
    tot = _unpack(total, [(1,)] + [SMALL_SHAPES[k] for k in SMALL])
    loss_sum = tot[0].reshape(())
    gs = dict(zip(SMALL, tot[1:]))
    gs["ssd_conv_w"] = lax.dynamic_slice_in_dim(gs["ssd_conv_w"], me * conv_cols, conv_cols, axis=2)
    gs["ffn_conv_w"] = lax.dynamic_slice_in_dim(gs["ffn_conv_w"], me * ffn_cols, ffn_cols, axis=2)
    upd = _adamw_small(_pack([gs[k] for k in SMALL]), _pack([w[k] for k in SMALL]), _pack([m[k] for k in SMALL]),
                       _pack([v[k] for k in SMALL]))
    shapes = [w[k].shape for k in SMALL]
    d_s, m_s, v_s = (dict(zip(SMALL, _unpack(u, shapes))) for u in upd)
    res = {}
    for k in SMALL:
        res[k] = (gs[k], d_s[k], m_s[k], v_s[k])

    after = upd[0]
    for name in ("w_down", "w_up", "w_out", "w_attn_o", "w_ssd_o", "w_in"):
        parts = _exchange_wait(*pending[name], after, False, "grad_wait_" + name)
        res[name] = _adamw_sharded(parts, w[name][0], m[name][0], v[name][0], "adamw_" + name)
        after = res[name][0]

    grad_x = dx.T[None]
    outs = [loss_sum, grad_x]
    for i in range(4):
        for k in WEIGHT_ORDER:
            r = res[k][i]
            outs.append(r[None] if k in SHARDED else r)
    return tuple(outs)
```

```python
import functools

import jax
import jax.numpy as jnp
from jax import lax
from jax.experimental import pallas as pl
from jax.experimental.pallas import tpu as pltpu

F32 = jnp.float32
BF16 = jnp.bfloat16
HIGHEST = lax.Precision.HIGHEST

D_MODEL = 1024
N_Q_HEADS = 16
N_KV_HEADS = 4
HEAD_DIM = 64
WINDOW = 128
Q_PER_KV = N_Q_HEADS // N_KV_HEADS
Q_DIM = N_Q_HEADS * HEAD_DIM
KV_DIM = N_KV_HEADS * HEAD_DIM
D_INNER = 2048
SSD_HEAD_DIM = 64
N_SSD_HEADS = 32
N_SSD_GROUPS = 4
HEADS_PER_GROUP = N_SSD_HEADS // N_SSD_GROUPS
D_STATE = 128
BC_DIM = N_SSD_GROUPS * D_STATE
XBC_DIM = D_INNER + 2 * BC_DIM
SSD_CONV = 4
CHUNK = 128
D_FF = 2816
FFN_CONV = 3
EPS = 1e-5
NEG = -1e30
IN_DIM = 8736
N_DEV = 8

OFF_Q = 0
OFF_K = OFF_Q + Q_DIM
OFF_V = OFF_K + KV_DIM
OFF_Z = OFF_V + KV_DIM
OFF_X = OFF_Z + D_INNER
OFF_GA = OFF_X + XBC_DIM
OFF_GS = OFF_GA + D_MODEL
OFF_DT = OFF_GS + D_MODEL
IN_PAD = OFF_DT + 128
ORIG_DT = OFF_X + XBC_DIM
ORIG_GA = ORIG_DT + N_SSD_HEADS
ORIG_GS = ORIG_GA + D_MODEL

ADAM_LR = 0.001
ADAM_B1 = 0.9
ADAM_B2 = 0.999
ADAM_EPS = 1e-08
ADAM_WD = 0.01
ADAM_STEP = 10

VMEM_LIMIT = 48 * 1024 * 1024
MESH = pl.DeviceIdType.MESH


def _cparams(*sem):
    return pltpu.CompilerParams(dimension_semantics=sem, vmem_limit_bytes=VMEM_LIMIT)


def _tile(n, prefs):
    for p in prefs:
        if n % p == 0:
            return p
    return n


def _sigmoid(x):
    return 1.0 / (1.0 + jnp.exp(-x))


def _softplus(x):
    return jnp.maximum(x, 0.0) + jnp.log(1.0 + jnp.exp(-jnp.abs(x)))


def _rowsum(x):
    return jnp.sum(x, axis=1, keepdims=True)


def _colsum(x):
    return jnp.sum(x, axis=0, keepdims=True)


def _dot(a, b):
    return jnp.dot(a, b, preferred_element_type=F32)


def _dot_nt(a, b):
    return lax.dot_general(a, b, (((1,), (1,)), ((), ())), preferred_element_type=F32)


def _dot_tn(a, b):
    return lax.dot_general(a, b, (((0,), (0,)), ((), ())), preferred_element_type=F32)


def _shift_right(x, j):
    if j == 0:
        return x
    lane = lax.broadcasted_iota(jnp.int32, x.shape, 1)
    return jnp.where(lane >= j, pltpu.roll(x, j, 1), 0.0)


def _shift_left(x, j):
    if j == 0:
        return x
    n = x.shape[1]
    lane = lax.broadcasted_iota(jnp.int32, x.shape, 1)
    return jnp.where(lane < n - j, pltpu.roll(x, n - j, 1), 0.0)


def _matmul(a, b, *, nt, out_dtype, name, add=None, dep=None):
    m, k = a.shape
    n = b.shape[0] if nt else b.shape[1]
    tm = _tile(m, (512, 384, 256, 128))
    tn = _tile(n, (1024, 512, 384, 256, 128))
    tk = _tile(k, (512, 384, 256, 128))
    nk = k // tk

    def body(a_ref, b_ref, *rest):
        if dep is not None:
            rest = rest[1:]
        if add is None:
            o_ref, acc = rest
        else:
            r_ref, o_ref, acc = rest
        kk = pl.program_id(2)

        @pl.when(kk == 0)
        def _():
            acc[...] = jnp.zeros_like(acc)

        av = a_ref[...].astype(BF16)
        bv = b_ref[...].astype(BF16)
        acc[...] += _dot_nt(av, bv) if nt else _dot(av, bv)

        @pl.when(kk == nk - 1)
        def _():
            r = acc[...]
            if add is not None:
                r = r + r_ref[...]
            o_ref[...] = r.astype(out_dtype)

    in_specs = [
        pl.BlockSpec((tm, tk), lambda i, j, kk: (i, kk)),
        pl.BlockSpec((tn, tk), lambda i, j, kk: (j, kk)) if nt else pl.BlockSpec((tk, tn), lambda i, j, kk: (kk, j)),
    ]
    args = [a, b]
    if dep is not None:
        in_specs.append(pl.BlockSpec(memory_space=pl.ANY))
        args.append(dep)
    if add is not None:
        in_specs.append(pl.BlockSpec((tm, tn), lambda i, j, kk: (i, j)))
        args.append(add)
    return pl.pallas_call(
        body,
        name=name,
        grid=(m // tm, n // tn, nk),
        in_specs=in_specs,
        out_specs=pl.BlockSpec((tm, tn), lambda i, j, kk: (i, j)),
        out_shape=jax.ShapeDtypeStruct((m, n), out_dtype),
        scratch_shapes=[pltpu.VMEM((tm, tn), F32)],
        compiler_params=_cparams("parallel", "parallel", "arbitrary"),
    )(*args)


def _norm_fwd(x, w_col, name):
    f, t = x.shape
    tt = _tile(t, (512, 256, 128))

    def body(x_ref, w_ref, o_ref):
        xv = x_ref[...]
        r = lax.rsqrt(jnp.mean(xv * xv, axis=0, keepdims=True) + EPS)
        o_ref[...] = (xv * r * w_ref[...]).astype(BF16)

    return pl.pallas_call(
        body,
        name=name,
        grid=(t // tt,),
        in_specs=[pl.BlockSpec((f, tt), lambda i: (0, i)), pl.BlockSpec((f, 1), lambda i: (0, 0))],
        out_specs=pl.BlockSpec((f, tt), lambda i: (0, i)),
        out_shape=jax.ShapeDtypeStruct((f, t), BF16),
        compiler_params=_cparams("parallel"),
    )(x, w_col)


def _norm_bwd(dy, x, w_col, res, name):
    f, t = x.shape
    tt = _tile(t, (512, 256, 128))

    def body(dy_ref, x_ref, w_ref, res_ref, dx_ref, dw_ref):
        @pl.when(pl.program_id(0) == 0)
        def _():
            dw_ref[...] = jnp.zeros_like(dw_ref)

        xv = x_ref[...]
        r = lax.rsqrt(jnp.mean(xv * xv, axis=0, keepdims=True) + EPS)
        xhat = xv * r
        dyv = dy_ref[...]
        dw_ref[...] += _rowsum(dyv * xhat)
        dxhat = dyv * w_ref[...]
        dx_ref[...] = res_ref[...] + r * (dxhat - xhat * jnp.mean(dxhat * xhat, axis=0, keepdims=True))

    blk = pl.BlockSpec((f, tt), lambda i: (0, i))
    col = pl.BlockSpec((f, 1), lambda i: (0, 0))
    return pl.pallas_call(
        body,
        name=name,
        grid=(t // tt,),
        in_specs=[blk, blk, col, blk],
        out_specs=[blk, col],
        out_shape=[jax.ShapeDtypeStruct((f, t), F32), jax.ShapeDtypeStruct((f, 1), F32)],
        compiler_params=_cparams("arbitrary"),
    )(dy, x, w_col, res)


def _final_norm_loss(h, tgt, w_col):
    f, t = h.shape
    tt = _tile(t, (512, 256, 128))

    def body(h_ref, t_ref, w_ref, dh_ref, loss_ref, dw_ref):
        @pl.when(pl.program_id(0) == 0)
        def _():
            dw_ref[...] = jnp.zeros_like(dw_ref)
            loss_ref[...] = jnp.zeros_like(loss_ref)

        xv = h_ref[...]
        r = lax.rsqrt(jnp.mean(xv * xv, axis=0, keepdims=True) + EPS)
        xhat = xv * r
        wv = w_ref[...]
        err = xhat * wv - t_ref[...]
        loss_ref[...] += 0.5 * _rowsum(jnp.mean(err * err, axis=0, keepdims=True))
        dyv = err * (1.0 / f)
        dw_ref[...] += _rowsum(dyv * xhat)
        dxhat = dyv * wv
        dh_ref[...] = r * (dxhat - xhat * jnp.mean(dxhat * xhat, axis=0, keepdims=True))

    blk = pl.BlockSpec((f, tt), lambda i: (0, i))
    col = pl.BlockSpec((f, 1), lambda i: (0, 0))
    one = pl.BlockSpec((1, 1), lambda i: (0, 0))
    return pl.pallas_call(
        body,
        name="final_norm_loss",
        grid=(t // tt,),
        in_specs=[blk, blk, col],
        out_specs=[blk, one, col],
        out_shape=[jax.ShapeDtypeStruct((f, t), F32), jax.ShapeDtypeStruct((1, 1), F32), jax.ShapeDtypeStruct((f, 1), F32)],
        compiler_params=_cparams("arbitrary"),
    )(h, tgt, w_col)


def _attn_mask(n):
    shape = (2 * WINDOW, Q_PER_KV * WINDOW)
    si = lax.broadcasted_iota(jnp.int32, shape, 0)
    qi = lax.broadcasted_iota(jnp.int32, shape, 1) & (WINDOW - 1)
    dist = WINDOW + qi - si
    return (dist >= 0) & (dist < WINDOW) & ((si >= WINDOW) | (n > 0))


def _lane_cat(ref, row0, rows):
    return jnp.concatenate([ref[row0 + i * rows:row0 + (i + 1) * rows, :] for i in range(Q_PER_KV)], axis=1)


def _attn_fwd(proj, sinks):
    t = proj.shape[1]
    nb = t // WINDOW
    scale = HEAD_DIM ** -0.5

    def body(s_ref, q_ref, kc_ref, kp_ref, vc_ref, vp_ref, o_ref, lse_ref):
        n = pl.program_id(0)
        valid = _attn_mask(n)
        for g in range(N_KV_HEADS):
            rows = slice(g * HEAD_DIM, (g + 1) * HEAD_DIM)
            kt = jnp.concatenate([kp_ref[rows, :], kc_ref[rows, :]], axis=1).astype(BF16)
            vt = jnp.concatenate([vp_ref[rows, :], vc_ref[rows, :]], axis=1).astype(BF16)
            qcat = _lane_cat(q_ref, g * Q_PER_KV * HEAD_DIM, HEAD_DIM).astype(BF16)
            s = jnp.where(valid, _dot_tn(kt, qcat) * scale, NEG)
            sink = jnp.concatenate(
                [jnp.full((1, WINDOW), s_ref[g * Q_PER_KV + i], F32) for i in range(Q_PER_KV)], axis=1)
            m = jnp.maximum(jnp.max(s, axis=0, keepdims=True), sink)
            p = jnp.where(valid, jnp.exp(s - m), 0.0)
            denom = _colsum(p) + jnp.exp(sink - m)
            probs = (p / denom).astype(BF16)
            out = _dot(vt, probs)
            lse = m + jnp.log(denom)
            for i in range(Q_PER_KV):
                h = g * Q_PER_KV + i
                o_ref[h * HEAD_DIM:(h + 1) * HEAD_DIM, :] = out[:, i * WINDOW:(i + 1) * WINDOW]
                lse_ref[h:h + 1, :] = lse[:, i * WINDOW:(i + 1) * WINDOW]

    kb = OFF_K // KV_DIM
    vb = OFF_V // KV_DIM
    prev = lambda n: jnp.maximum(n - 1, 0)
    return pl.pallas_call(
        body,
        name="attn_fwd",
        grid=(nb,),
        in_specs=[
            pl.BlockSpec(memory_space=pltpu.SMEM),
            pl.BlockSpec((Q_DIM, WINDOW), lambda n: (0, n)),
            pl.BlockSpec((KV_DIM, WINDOW), lambda n: (kb, n)),
            pl.BlockSpec((KV_DIM, WINDOW), lambda n: (kb, prev(n))),
            pl.BlockSpec((KV_DIM, WINDOW), lambda n: (vb, n)),
            pl.BlockSpec((KV_DIM, WINDOW), lambda n: (vb, prev(n))),
        ],
        out_specs=[pl.BlockSpec((Q_DIM, WINDOW), lambda n: (0, n)), pl.BlockSpec((N_Q_HEADS, WINDOW), lambda n: (0, n))],
        out_shape=[jax.ShapeDtypeStruct((Q_DIM, t), F32), jax.ShapeDtypeStruct((N_Q_HEADS, t), F32)],
        compiler_params=_cparams("parallel"),
    )(sinks, proj, proj, proj, proj, proj)


def _attn_bwd(proj, sinks, out, lse, dout):
    t = proj.shape[1]
    nb = t // WINDOW
    scale = HEAD_DIM ** -0.5

    def body(s_ref, q_ref, kc_ref, kp_ref, vc_ref, vp_ref, o_ref, lse_ref, do_ref,
             dq_ref, dk_ref, dv_ref, ds_ref, dk_carry, dv_carry):
        step = pl.program_id(0)
        n = nb - 1 - step

        @pl.when(step == 0)
        def _():
            dk_carry[...] = jnp.zeros_like(dk_carry)
            dv_carry[...] = jnp.zeros_like(dv_carry)
            ds_ref[...] = jnp.zeros_like(ds_ref)

        valid = _attn_mask(n)
        for g in range(N_KV_HEADS):
            rows = slice(g * HEAD_DIM, (g + 1) * HEAD_DIM)
            q0 = g * Q_PER_KV * HEAD_DIM
            kt = jnp.concatenate([kp_ref[rows, :], kc_ref[rows, :]], axis=1).astype(BF16)
            vt = jnp.concatenate([vp_ref[rows, :], vc_ref[rows, :]], axis=1).astype(BF16)
            qcat = _lane_cat(q_ref, q0, HEAD_DIM).astype(BF16)
            ocat = _lane_cat(o_ref, q0, HEAD_DIM)
            docat = _lane_cat(do_ref, q0, HEAD_DIM)
            dob = docat.astype(BF16)
            lse_cat = jnp.concatenate(
                [lse_ref[g * Q_PER_KV + i:g * Q_PER_KV + i + 1, :] for i in range(Q_PER_KV)], axis=1)
            sink = jnp.concatenate(
                [jnp.full((1, WINDOW), s_ref[g * Q_PER_KV + i], F32) for i in range(Q_PER_KV)], axis=1)
            s = jnp.where(valid, _dot_tn(kt, qcat) * scale, NEG)
            p = jnp.where(valid, jnp.exp(s - lse_cat), 0.0)
            dp = _dot_tn(vt, dob)
            delta = _colsum(docat * ocat)
            dsc = (p * (dp - delta)).astype(BF16)
            dsink_row = -jnp.exp(sink - lse_cat) * delta
            dq = _dot(kt, dsc) * scale
            dk = _dot_nt(qcat, dsc) * scale
            dv = _dot_nt(dob, p.astype(BF16))
            for i in range(Q_PER_KV):
                h = g * Q_PER_KV + i
                dq_ref[h * HEAD_DIM:(h + 1) * HEAD_DIM, :] = dq[:, i * WINDOW:(i + 1) * WINDOW].astype(BF16)
                ds_ref[h:h + 1, :] += _rowsum(dsink_row[:, i * WINDOW:(i + 1) * WINDOW])
            dk_ref[rows, :] = (dk[:, WINDOW:] + dk_carry[rows, :]).astype(BF16)
            dv_ref[rows, :] = (dv[:, WINDOW:] + dv_carry[rows, :]).astype(BF16)
            dk_carry[rows, :] = dk[:, :WINDOW]
            dv_carry[rows, :] = dv[:, :WINDOW]

    kb = OFF_K // KV_DIM
    vb = OFF_V // KV_DIM
    cur = lambda i: nb - 1 - i
    prev = lambda i: jnp.maximum(nb - 2 - i, 0)
    qspec = pl.BlockSpec((Q_DIM, WINDOW), lambda i: (0, cur(i)))
    kvspec = pl.BlockSpec((KV_DIM, WINDOW), lambda i: (0, cur(i)))
    return pl.pallas_call(
        body,
        name="attn_bwd",
        grid=(nb,),
        in_specs=[
            pl.BlockSpec(memory_space=pltpu.SMEM),
            qspec,
            pl.BlockSpec((KV_DIM, WINDOW), lambda i: (kb, cur(i))),
            pl.BlockSpec((KV_DIM, WINDOW), lambda i: (kb, prev(i))),
            pl.BlockSpec((KV_DIM, WINDOW), lambda i: (vb, cur(i))),
            pl.BlockSpec((KV_DIM, WINDOW), lambda i: (vb, prev(i))),
            qspec,
            pl.BlockSpec((N_Q_HEADS, WINDOW), lambda i: (0, cur(i))),
            qspec,
        ],
        out_specs=[qspec, kvspec, kvspec, pl.BlockSpec((N_Q_HEADS, 1), lambda i: (0, 0))],
        out_shape=[
            jax.ShapeDtypeStruct((Q_DIM, t), BF16),
            jax.ShapeDtypeStruct((KV_DIM, t), BF16),
            jax.ShapeDtypeStruct((KV_DIM, t), BF16),
            jax.ShapeDtypeStruct((N_Q_HEADS, 1), F32),
        ],
        scratch_shapes=[pltpu.VMEM((KV_DIM, WINDOW), F32), pltpu.VMEM((KV_DIM, WINDOW), F32)],
        compiler_params=_cparams("arbitrary"),
    )(sinks, proj, proj, proj, proj, proj, out, lse, dout)


CONV_ROWS = 256


def _conv_silu_fwd(proj, w_col, b_col):
    t = proj.shape[1]
    r0 = OFF_X // CONV_ROWS

    def body(x_ref, w_ref, b_ref, o_ref):
        xv = x_ref[...]
        wv = w_ref[...]
        y = b_ref[...] + wv[:, SSD_CONV - 1:SSD_CONV] * xv
        for k in range(SSD_CONV - 1):
            y = y + wv[:, k:k + 1] * _shift_right(xv, SSD_CONV - 1 - k)
        o_ref[...] = y * _sigmoid(y)

    return pl.pallas_call(
        body,
        name="ssd_conv_fwd",
        grid=(XBC_DIM // CONV_ROWS,),
        in_specs=[
            pl.BlockSpec((CONV_ROWS, t), lambda i: (r0 + i, 0)),
            pl.BlockSpec((CONV_ROWS, SSD_CONV), lambda i: (i, 0)),
            pl.BlockSpec((CONV_ROWS, 1), lambda i: (i, 0)),
        ],
        out_specs=pl.BlockSpec((CONV_ROWS, t), lambda i: (i, 0)),
        out_shape=jax.ShapeDtypeStruct((XBC_DIM, t), F32),
        compiler_params=_cparams("parallel"),
    )(proj, w_col, b_col)


def _conv_silu_bwd(proj, w_col, b_col, dout, row0, name):
    t = proj.shape[1]
    nrows = dout.shape[0]
    p0 = (OFF_X + row0) // CONV_ROWS
    c0 = row0 // CONV_ROWS

    def body(x_ref, w_ref, b_ref, do_ref, dx_ref, dwb_ref):
        xv = x_ref[...]
        wv = w_ref[...]
        y = b_ref[...] + wv[:, SSD_CONV - 1:SSD_CONV] * xv
        for k in range(SSD_CONV - 1):
            y = y + wv[:, k:k + 1] * _shift_right(xv, SSD_CONV - 1 - k)
        sg = _sigmoid(y)
        dy = do_ref[...] * (sg * (1.0 + y * (1.0 - sg)))
        lane = lax.broadcasted_iota(jnp.int32, (CONV_ROWS, 128), 1)
        dwb = jnp.where(lane == SSD_CONV, _rowsum(dy), 0.0)
        dx = wv[:, SSD_CONV - 1:SSD_CONV] * dy
        dwb = jnp.where(lane == SSD_CONV - 1, _rowsum(dy * xv), dwb)
        for k in range(SSD_CONV - 1):
            j = SSD_CONV - 1 - k
            dx = dx + wv[:, k:k + 1] * _shift_left(dy, j)
            dwb = jnp.where(lane == k, _rowsum(dy * _shift_right(xv, j)), dwb)
        dx_ref[...] = dx.astype(BF16)
        dwb_ref[...] = dwb

    return pl.pallas_call(
        body,
        name=name,
        grid=(nrows // CONV_ROWS,),
        in_specs=[
            pl.BlockSpec((CONV_ROWS, t), lambda i: (p0 + i, 0)),
            pl.BlockSpec((CONV_ROWS, SSD_CONV), lambda i: (c0 + i, 0)),
            pl.BlockSpec((CONV_ROWS, 1), lambda i: (c0 + i, 0)),
            pl.BlockSpec((CONV_ROWS, t), lambda i: (i, 0)),
        ],
        out_specs=[pl.BlockSpec((CONV_ROWS, t), lambda i: (i, 0)), pl.BlockSpec((CONV_ROWS, 128), lambda i: (i, 0))],
        out_shape=[jax.ShapeDtypeStruct((nrows, t), BF16), jax.ShapeDtypeStruct((nrows, 128), F32)],
        compiler_params=_cparams("parallel"),
    )(proj, w_col, b_col, dout)


GROUP_ROWS = HEADS_PER_GROUP * SSD_HEAD_DIM


def _ssd_specs(nc, order):
    hb = D_INNER // D_STATE
    dtb = OFF_DT // HEADS_PER_GROUP
    col = pl.BlockSpec((HEADS_PER_GROUP, 1), lambda g, c: (g, 0))
    return [
        pl.BlockSpec((GROUP_ROWS, CHUNK), lambda g, c: (g, order(c))),
        pl.BlockSpec((D_STATE, CHUNK), lambda g, c: (hb + g, order(c))),
        pl.BlockSpec((D_STATE, CHUNK), lambda g, c: (hb + N_SSD_GROUPS + g, order(c))),
        pl.BlockSpec((HEADS_PER_GROUP, CHUNK), lambda g, c: (dtb + g, order(c))),
        col, col, col,
    ]


def _ssd_common(dt_ref, dtb_ref, alog_ref):
    z = dt_ref[...] + dtb_ref[...]
    dt = _softplus(z)
    a_neg = -jnp.exp(alog_ref[...])
    d_a = dt * a_neg
    row = lax.broadcasted_iota(jnp.int32, (CHUNK, CHUNK), 0)
    colm = lax.broadcasted_iota(jnp.int32, (CHUNK, CHUNK), 1)
    upper = (row <= colm).astype(F32)
    a_cs = jnp.dot(d_a, upper, precision=HIGHEST, preferred_element_type=F32)
    a_last = _rowsum(d_a)
    return z, dt, a_neg, a_cs, a_last, row >= colm, row == colm


def _decay(a_row, causal):
    a_s = jnp.broadcast_to(a_row, (CHUNK, CHUNK))
    seg = a_s.T - a_s
    return jnp.where(causal, jnp.exp(jnp.where(causal, seg, 0.0)), 0.0)


def _ssd_fwd(xbc, proj, dtb_col, alog_col, dsk_col):
    t = xbc.shape[1]
    nc = t // CHUNK

    def body(xs_ref, b_ref, c_ref, dt_ref, dtb_ref, alog_ref, dsk_ref, y_ref, hst_ref, h_scr):
        @pl.when(pl.program_id(1) == 0)
        def _():
            h_scr[...] = jnp.zeros_like(h_scr)

        _, dt, _, a_cs, a_last, causal, _ = _ssd_common(dt_ref, dtb_ref, alog_ref)
        bb = b_ref[...].astype(BF16)
        cb_ = c_ref[...].astype(BF16)
        cb = _dot_tn(cb_, bb)
        hst_ref[0, 0] = h_scr[...]
        dsk = dsk_ref[...]
        for j in range(HEADS_PER_GROUP):
            rows = slice(j * SSD_HEAD_DIM, (j + 1) * SSD_HEAD_DIM)
            a = a_cs[j:j + 1, :]
            m = (cb * _decay(a, causal)).astype(BF16)
            xs = xs_ref[rows, :]
            xc = xs * dt[j:j + 1, :]
            hj = h_scr[rows, :]
            y = _dot_nt(xc.astype(BF16), m) + _dot(hj.astype(BF16), cb_) * jnp.exp(a) + dsk[j:j + 1, :] * xs
            y_ref[rows, :] = y
            al = a_last[j:j + 1, :]
            w = jnp.exp(al - a)
            h_scr[rows, :] = jnp.exp(al) * hj + _dot_nt((xc * w).astype(BF16), bb)

    return pl.pallas_call(
        body,
        name="ssd_fwd",
        grid=(N_SSD_GROUPS, nc),
        in_specs=_ssd_specs(nc, lambda c: c),
        out_specs=[
            pl.BlockSpec((GROUP_ROWS, CHUNK), lambda g, c: (g, c)),
            pl.BlockSpec((1, 1, GROUP_ROWS, D_STATE), lambda g, c: (g, c, 0, 0)),
        ],
        out_shape=[
            jax.ShapeDtypeStruct((D_INNER, t), F32),
            jax.ShapeDtypeStruct((N_SSD_GROUPS, nc, GROUP_ROWS, D_STATE), F32),
        ],
        scratch_shapes=[pltpu.VMEM((GROUP_ROWS, D_STATE), F32)],
        compiler_params=_cparams("parallel", "arbitrary"),
    )(xbc, xbc, xbc, proj, dtb_col, alog_col, dsk_col)


def _ssd_bwd(xbc, proj, dtb_col, alog_col, dsk_col, hst, dy):
    t = xbc.shape[1]
    nc = t // CHUNK
    rev = lambda c: nc - 1 - c

    def body(xs_ref, b_ref, c_ref, dt_ref, dtb_ref, alog_ref, dsk_ref, hst_ref, dy_ref,
             dxs_ref, db_ref, dc_ref, ddt_ref, dalog_ref, ddsk_ref, ddtb_ref, dh_scr, da_scr, ddt_scr, dd_scr):
        @pl.when(pl.program_id(1) == 0)
        def _():
            dh_scr[...] = jnp.zeros_like(dh_scr)
            dalog_ref[...] = jnp.zeros_like(dalog_ref)
            ddsk_ref[...] = jnp.zeros_like(ddsk_ref)
            ddtb_ref[...] = jnp.zeros_like(ddtb_ref)

        z, dt, a_neg, a_cs, a_last, causal, eye = _ssd_common(dt_ref, dtb_ref, alog_ref)
        bb = b_ref[...].astype(BF16)
        cb_ = c_ref[...].astype(BF16)
        cb = _dot_tn(cb_, bb)
        dsk = dsk_ref[...]
        last_lane = lax.broadcasted_iota(jnp.int32, (1, CHUNK), 1) == CHUNK - 1
        dcb = jnp.zeros((CHUNK, CHUNK), F32)
        dc_acc = jnp.zeros((D_STATE, CHUNK), F32)
        db_acc = jnp.zeros((D_STATE, CHUNK), F32)
        for j in range(HEADS_PER_GROUP):
            rows = slice(j * SSD_HEAD_DIM, (j + 1) * SSD_HEAD_DIM)
            a = a_cs[j:j + 1, :]
            al = a_last[j:j + 1, :]
            lam = _decay(a, causal)
            mf = cb * lam
            xs = xs_ref[rows, :]
            dtj = dt[j:j + 1, :]
            xc = xs * dtj
            w = jnp.exp(al - a)
            e = jnp.exp(a)
            gam = jnp.exp(al)
            hj = hst_ref[0, 0, rows, :]
            hjb = hj.astype(BF16)
            dyv = dy_ref[rows, :]
            dyb = dyv.astype(BF16)
            dd_scr[j:j + 1, :] = _colsum(dyv * xs)
            gb = (dyv * e).astype(BF16)
            dh_in = _dot_nt(gb, cb_)
            dc_acc = dc_acc + _dot_tn(hjb, gb)
            yoff = _dot(hjb, cb_) * e
            da = _colsum(dyv * yoff)
            dm = _dot_tn(dyb, xc.astype(BF16))
            dxc = _dot(dyb, mf.astype(BF16))
            dcb = dcb + dm * lam
            nmat = dm * mf
            rs = jnp.broadcast_to(_rowsum(nmat), (CHUNK, CHUNK))
            da = da + _colsum(jnp.where(eye, rs, 0.0)) - _colsum(nmat)
            ds = dh_scr[rows, :]
            dsb = ds.astype(BF16)
            t1 = _dot(dsb, bb)
            xcw = xc * w
            dxc = dxc + w * t1
            dww = _colsum(xcw * t1)
            da_l = _rowsum(dww) + _rowsum(_colsum(ds * hj)) * gam
            da = da - dww + jnp.where(last_lane, da_l, 0.0)
            db_acc = db_acc + _dot_tn(dsb, xcw.astype(BF16))
            dh_scr[rows, :] = gam * ds + dh_in
            dxs_ref[rows, :] = dsk[j:j + 1, :] * dyv + dxc * dtj
            da_scr[j:j + 1, :] = da
            ddt_scr[j:j + 1, :] = _colsum(dxc * xs)
        dcbb = dcb.astype(BF16)
        dc_ref[...] = dc_acc + _dot_nt(bb, dcbb)
        db_ref[...] = db_acc + _dot(cb_, dcbb)
        dda = jnp.dot(da_scr[...], causal.astype(F32), precision=HIGHEST, preferred_element_type=F32)
        ddt = ddt_scr[...] + dda * a_neg
        ddt_raw = ddt * _sigmoid(z)
        ddt_ref[...] = ddt_raw
        ddtb_ref[...] += _rowsum(ddt_raw)
        dalog_ref[...] += _rowsum(dda * dt) * a_neg
        ddsk_ref[...] += _rowsum(dd_scr[...])

    col = pl.BlockSpec((HEADS_PER_GROUP, 1), lambda g, c: (g, 0))
    bc = pl.BlockSpec((D_STATE, CHUNK), lambda g, c: (g, rev(c)))
    xs_spec = pl.BlockSpec((GROUP_ROWS, CHUNK), lambda g, c: (g, rev(c)))
    small = pltpu.VMEM((HEADS_PER_GROUP, CHUNK), F32)
    return pl.pallas_call(
        body,
        name="ssd_bwd",
        grid=(N_SSD_GROUPS, nc),
        in_specs=_ssd_specs(nc, rev) + [
            pl.BlockSpec((1, 1, GROUP_ROWS, D_STATE), lambda g, c: (g, rev(c), 0, 0)),
            xs_spec,
        ],
        out_specs=[xs_spec, bc, bc, pl.BlockSpec((HEADS_PER_GROUP, CHUNK), lambda g, c: (g, rev(c))), col, col, col],
        out_shape=[
            jax.ShapeDtypeStruct((D_INNER, t), F32),
            jax.ShapeDtypeStruct((BC_DIM, t), F32),
            jax.ShapeDtypeStruct((BC_DIM, t), F32),
            jax.ShapeDtypeStruct((N_SSD_HEADS, t), F32),
            jax.ShapeDtypeStruct((N_SSD_HEADS, 1), F32),
            jax.ShapeDtypeStruct((N_SSD_HEADS, 1), F32),
            jax.ShapeDtypeStruct((N_SSD_HEADS, 1), F32),
        ],
        scratch_shapes=[pltpu.VMEM((GROUP_ROWS, D_STATE), F32), small, small, small],
        compiler_params=_cparams("parallel", "arbitrary"),
    )(xbc, xbc, xbc, proj, dtb_col, alog_col, dsk_col, hst, dy)


GN_ROWS = D_INNER // N_SSD_GROUPS


def _gnorm_fwd(y, proj, w_col):
    t = y.shape[1]
    tt = _tile(t, (512, 256, 128))
    z0 = OFF_Z // GN_ROWS

    def body(y_ref, z_ref, w_ref, o_ref):
        zv = z_ref[...]
        u = y_ref[...] * (zv * _sigmoid(zv))
        r = lax.rsqrt(jnp.mean(u * u, axis=0, keepdims=True) + EPS)
        o_ref[...] = (u * r * w_ref[...]).astype(BF16)

    blk = pl.BlockSpec((GN_ROWS, tt), lambda g, i: (g, i))
    return pl.pallas_call(
        body,
        name="gnorm_fwd",
        grid=(N_SSD_GROUPS, t // tt),
        in_specs=[blk, pl.BlockSpec((GN_ROWS, tt), lambda g, i: (z0 + g, i)), pl.BlockSpec((GN_ROWS, 1), lambda g, i: (g, 0))],
        out_specs=blk,
        out_shape=jax.ShapeDtypeStruct((D_INNER, t), BF16),
        compiler_params=_cparams("parallel", "parallel"),
    )(y, proj, w_col)


def _gnorm_bwd(dout, y, proj, w_col):
    t = y.shape[1]
    tt = _tile(t, (512, 256, 128))
    z0 = OFF_Z // GN_ROWS

    def body(do_ref, y_ref, z_ref, w_ref, dy_ref, dz_ref, dw_ref):
        @pl.when(pl.program_id(1) == 0)
        def _():
            dw_ref[...] = jnp.zeros_like(dw_ref)

        zv = z_ref[...]
        yv = y_ref[...]
        sg = _sigmoid(zv)
        sz = zv * sg
        u = yv * sz
        r = lax.rsqrt(jnp.mean(u * u, axis=0, keepdims=True) + EPS)
        xhat = u * r
        dov = do_ref[...]
        dw_ref[...] += _rowsum(dov * xhat)
        dxhat = dov * w_ref[...]
        du = r * (dxhat - xhat * jnp.mean(dxhat * xhat, axis=0, keepdims=True))
        dy_ref[...] = du * sz
        dz_ref[...] = (du * yv * (sg * (1.0 + zv * (1.0 - sg)))).astype(BF16)

    blk = pl.BlockSpec((GN_ROWS, tt), lambda g, i: (g, i))
    col = pl.BlockSpec((GN_ROWS, 1), lambda g, i: (g, 0))
    return pl.pallas_call(
        body,
        name="gnorm_bwd",
        grid=(N_SSD_GROUPS, t // tt),
        in_specs=[blk, blk, pl.BlockSpec((GN_ROWS, tt), lambda g, i: (z0 + g, i)), col],
        out_specs=[blk, blk, col],
        out_shape=[jax.ShapeDtypeStruct((D_INNER, t), F32), jax.ShapeDtypeStruct((D_INNER, t), BF16),
                   jax.ShapeDtypeStruct((D_INNER, 1), F32)],
        compiler_params=_cparams("parallel", "arbitrary"),
    )(dout, y, proj, w_col)


GATE_ROWS = 512


def _gate_specs(t, tt):
    ga0 = OFF_GA // GATE_ROWS
    gs0 = OFF_GS // GATE_ROWS
    nr = D_MODEL // GATE_ROWS
    blk = pl.BlockSpec((GATE_ROWS, tt), lambda r, i: (r, i))
    return blk, [
        pl.BlockSpec((GATE_ROWS, tt), lambda r, i: (ga0 + r, i)),
        pl.BlockSpec((GATE_ROWS, tt), lambda r, i: (gs0 + r, i)),
        pl.BlockSpec((GATE_ROWS, 1), lambda r, i: (r, 0)),
        pl.BlockSpec((GATE_ROWS, 1), lambda r, i: (nr + r, 0)),
        blk, blk,
    ]


def _gate_fwd(proj, b_col, attn, ssd):
    t = proj.shape[1]
    tt = _tile(t, (512, 256, 128))
    blk, specs = _gate_specs(t, tt)

    def body(ga_ref, gs_ref, ba_ref, bs_ref, a_ref, s_ref, o_ref):
        o_ref[...] = (_sigmoid(ga_ref[...] + ba_ref[...]) * a_ref[...]
                      + _sigmoid(gs_ref[...] + bs_ref[...]) * s_ref[...]).astype(BF16)

    return pl.pallas_call(
        body,
        name="gate_fwd",
        grid=(D_MODEL // GATE_ROWS, t // tt),
        in_specs=specs,
        out_specs=blk,
        out_shape=jax.ShapeDtypeStruct((D_MODEL, t), BF16),
        compiler_params=_cparams("parallel", "parallel"),
    )(proj, proj, b_col, b_col, attn, ssd)


def _gate_bwd(proj, b_col, attn, ssd, dmix):
    t = proj.shape[1]
    tt = _tile(t, (512, 256, 128))
    blk, specs = _gate_specs(t, tt)
    nr = D_MODEL // GATE_ROWS

    def body(ga_ref, gs_ref, ba_ref, bs_ref, a_ref, s_ref, dm_ref, da_ref, dso_ref, dga_ref, dgs_ref, dba_ref, dbs_ref):
        @pl.when(pl.program_id(1) == 0)
        def _():
            dba_ref[...] = jnp.zeros_like(dba_ref)
            dbs_ref[...] = jnp.zeros_like(dbs_ref)

        dm = dm_ref[...]
        sa = _sigmoid(ga_ref[...] + ba_ref[...])
        ss = _sigmoid(gs_ref[...] + bs_ref[...])
        da_ref[...] = (dm * sa).astype(BF16)
        dso_ref[...] = (dm * ss).astype(BF16)
        dga = dm * a_ref[...] * sa * (1.0 - sa)
        dgs = dm * s_ref[...] * ss * (1.0 - ss)
        dga_ref[...] = dga.astype(BF16)
        dgs_ref[...] = dgs.astype(BF16)
        dba_ref[...] += _rowsum(dga)
        dbs_ref[...] += _rowsum(dgs)

    col = pl.BlockSpec((GATE_ROWS, 1), lambda r, i: (r, 0))
    act = jax.ShapeDtypeStruct((D_MODEL, t), BF16)
    bias = jax.ShapeDtypeStruct((D_MODEL, 1), F32)
    return pl.pallas_call(
        body,
        name="gate_bwd",
        grid=(nr, t // tt),
        in_specs=specs + [blk],
        out_specs=[blk, blk, blk, blk, col, col],
        out_shape=[act, act, act, act, bias, bias],
        compiler_params=_cparams("parallel", "arbitrary"),
    )(proj, proj, b_col, b_col, attn, ssd, dmix)


FFN_ROWS = 256


def _ffn_conv(u_ref, w_ref, b_ref, half):
    xv = u_ref[half]
    wv = w_ref[half]
    y = b_ref[half] + wv[:, FFN_CONV - 1:FFN_CONV] * xv
    for k in range(FFN_CONV - 1):
        y = y + wv[:, k:k + 1] * _shift_right(xv, FFN_CONV - 1 - k)
    return xv, wv, y


def _ffn_fwd(u0, w_col, b_col):
    t = u0.shape[2]

    def body(u_ref, w_ref, b_ref, o_ref):
        _, _, val = _ffn_conv(u_ref, w_ref, b_ref, 0)
        _, _, gt = _ffn_conv(u_ref, w_ref, b_ref, 1)
        o_ref[...] = (gt * _sigmoid(gt) * val).astype(BF16)

    return pl.pallas_call(
        body,
        name="ffn_fwd",
        grid=(D_FF // FFN_ROWS,),
        in_specs=[
            pl.BlockSpec((2, FFN_ROWS, t), lambda i: (0, i, 0)),
            pl.BlockSpec((2, FFN_ROWS, FFN_CONV), lambda i: (0, i, 0)),
            pl.BlockSpec((2, FFN_ROWS, 1), lambda i: (0, i, 0)),
        ],
        out_specs=pl.BlockSpec((FFN_ROWS, t), lambda i: (i, 0)),
        out_shape=jax.ShapeDtypeStruct((D_FF, t), BF16),
        compiler_params=_cparams("parallel"),
    )(u0, w_col, b_col)


def _ffn_bwd(u0, w_col, b_col, dg):
    t = u0.shape[2]

    def body(u_ref, w_ref, b_ref, dg_ref, du_ref, dwb_ref):
        xval, wval, val = _ffn_conv(u_ref, w_ref, b_ref, 0)
        xgt, wgt, gt = _ffn_conv(u_ref, w_ref, b_ref, 1)
        sg = _sigmoid(gt)
        dgv = dg_ref[...]
        dval = dgv * (gt * sg)
        dgt = dgv * val * (sg * (1.0 + gt * (1.0 - sg)))
        lane = lax.broadcasted_iota(jnp.int32, (FFN_ROWS, 128), 1)
        for half, xv, wv, dy in ((0, xval, wval, dval), (1, xgt, wgt, dgt)):
            dwb = jnp.where(lane == FFN_CONV, _rowsum(dy), 0.0)
            dx = wv[:, FFN_CONV - 1:FFN_CONV] * dy
            dwb = jnp.where(lane == FFN_CONV - 1, _rowsum(dy * xv), dwb)
            for k in range(FFN_CONV - 1):
                j = FFN_CONV - 1 - k
                dx = dx + wv[:, k:k + 1] * _shift_left(dy, j)
                dwb = jnp.where(lane == k, _rowsum(dy * _shift_right(xv, j)), dwb)
            du_ref[half] = dx.astype(BF16)
            dwb_ref[half] = dwb

    return pl.pallas_call(
        body,
        name="ffn_bwd",
        grid=(D_FF // FFN_ROWS,),
        in_specs=[
            pl.BlockSpec((2, FFN_ROWS, t), lambda i: (0, i, 0)),
            pl.BlockSpec((2, FFN_ROWS, FFN_CONV), lambda i: (0, i, 0)),
            pl.BlockSpec((2, FFN_ROWS, 1), lambda i: (0, i, 0)),
            pl.BlockSpec((FFN_ROWS, t), lambda i: (i, 0)),
        ],
        out_specs=[pl.BlockSpec((2, FFN_ROWS, t), lambda i: (0, i, 0)), pl.BlockSpec((2, FFN_ROWS, 128), lambda i: (0, i, 0))],
        out_shape=[jax.ShapeDtypeStruct((2, D_FF, t), BF16), jax.ShapeDtypeStruct((2, D_FF, 128), F32)],
        compiler_params=_cparams("parallel"),
    )(u0, w_col, b_col, dg)


def _adamw_math(w, g, m, v):
    m = ADAM_B1 * m + (1.0 - ADAM_B1) * g
    v = ADAM_B2 * v + (1.0 - ADAM_B2) * (g * g)
    m_hat = m / (1.0 - ADAM_B1 ** ADAM_STEP)
    v_hat = v / (1.0 - ADAM_B2 ** ADAM_STEP)
    delta = -ADAM_LR * (m_hat / (jnp.sqrt(v_hat) + ADAM_EPS) + ADAM_WD * w)
    return delta, m, v


def _adamw_sharded(parts, w, m, v, name):
    r, c = w.shape
    tr = _tile(r, (256, 128, 64, 32, 16))

    def body(p_ref, w_ref, m_ref, v_ref, g_ref, d_ref, nm_ref, nv_ref):
        g = p_ref[0].astype(F32)
        for s in range(1, N_DEV):
            g = g + p_ref[s].astype(F32)
        g_ref[...] = g
        d_ref[...], nm_ref[...], nv_ref[...] = _adamw_math(w_ref[...], g, m_ref[...], v_ref[...])

    blk = pl.BlockSpec((tr, c), lambda i: (i, 0))
    out = jax.ShapeDtypeStruct((r, c), F32)
    return pl.pallas_call(
        body,
        name=name,
        grid=(r // tr,),
        in_specs=[pl.BlockSpec((N_DEV, tr, c), lambda i: (0, i, 0)), blk, blk, blk],
        out_specs=[blk, blk, blk, blk],
        out_shape=[out, out, out, out],
        compiler_params=_cparams("parallel"),
    )(parts, w, m, v)


def _sum_slots(parts):
    _, r, c = parts.shape

    def body(p_ref, o_ref):
        g = p_ref[0]
        for s in range(1, N_DEV):
            g = g + p_ref[s]
        o_ref[...] = g

    return pl.pallas_call(body, name="sum_small_grads", out_shape=jax.ShapeDtypeStruct((r, c), F32))(parts)


def _adamw_small(g, w, m, v):
    def body(g_ref, w_ref, m_ref, v_ref, d_ref, nm_ref, nv_ref):
        d_ref[...], nm_ref[...], nv_ref[...] = _adamw_math(w_ref[...], g_ref[...], m_ref[...], v_ref[...])

    out = jax.ShapeDtypeStruct(g.shape, F32)
    return pl.pallas_call(body, name="adamw_small", out_shape=[out, out, out])(g, w, m, v)


ANY = pl.BlockSpec(memory_space=pl.ANY)
FLIPS = [(k >> 2 & 1, k >> 1 & 1, k & 1) for k in range(1, N_DEV)]


def _place():
    return lax.axis_index("x"), lax.axis_index("y"), lax.axis_index("c")


def _all_gather(arrays, name):
    n = len(arrays)

    def body(*refs):
        ins, outs = refs[:n], refs[n:2 * n]
        send_sems, recv_sems, local_sems = refs[2 * n:]
        x, y, c = _place()
        me = 4 * x + 2 * y + c
        local = [pltpu.make_async_copy(ins[i], outs[i].at[me], local_sems.at[i]) for i in range(n)]
        for cp in local:
            cp.start()
        sends = []
        for k, (fx, fy, fc) in enumerate(FLIPS):
            for i in range(n):
                cp = pltpu.make_async_remote_copy(
                    src_ref=ins[i], dst_ref=outs[i].at[me], send_sem=send_sems.at[i, k], recv_sem=recv_sems.at[i, k],
                    device_id=(x ^ fx, y ^ fy, c ^ fc), device_id_type=MESH)
                cp.start()
                sends.append(cp)
        for k, (fx, fy, fc) in enumerate(FLIPS):
            src = 4 * (x ^ fx) + 2 * (y ^ fy) + (c ^ fc)
            for i in range(n):
                pltpu.make_async_remote_copy(
                    src_ref=ins[i], dst_ref=outs[i].at[src], send_sem=send_sems.at[i, k], recv_sem=recv_sems.at[i, k],
                    device_id=(x ^ fx, y ^ fy, c ^ fc), device_id_type=MESH).wait_recv()
        for cp in sends:
            cp.wait_send()
        for cp in local:
            cp.wait()

    return pl.pallas_call(
        body,
        name=name,
        in_specs=[ANY] * n,
        out_specs=[ANY] * n,
        out_shape=[jax.ShapeDtypeStruct((N_DEV,) + a.shape, a.dtype) for a in arrays],
        scratch_shapes=[pltpu.SemaphoreType.DMA((n, N_DEV - 1)), pltpu.SemaphoreType.DMA((n, N_DEV - 1)),
                        pltpu.SemaphoreType.DMA((n,))],
    )(*arrays)


HBM = pl.BlockSpec(memory_space=pltpu.HBM)
SEM = pl.BlockSpec(memory_space=pltpu.SEMAPHORE)
EFFECT = pltpu.SideEffectType.DATAFLOW_SIDE_EFFECTING


def _peer_copy(gather, src_ref, land_ref, send_sems, recv_sems, k, sending):
    x, y, c = _place()
    fx, fy, fc = FLIPS[k]
    me = 4 * x + 2 * y + c
    peer = 4 * (x ^ fx) + 2 * (y ^ fy) + (c ^ fc)
    return pltpu.make_async_remote_copy(
        src_ref=src_ref if gather else src_ref.at[peer],
        dst_ref=land_ref.at[me if sending else peer],
        send_sem=send_sems.at[k], recv_sem=recv_sems.at[k],
        device_id=(x ^ fx, y ^ fy, c ^ fc), device_id_type=MESH)


def _exchange_start(srcs, gather, name):
    n = len(srcs)
    lands = [lax.empty((N_DEV,) + s.shape if gather else s.shape, s.dtype) for s in srcs]

    def body(*refs):
        src_refs, land_refs = refs[:n], refs[n:2 * n]
        send, recv = refs[2 * n:3 * n], refs[3 * n:4 * n]
        token, local_sems = refs[6 * n], refs[6 * n + 1]
        x, y, c = _place()
        me = 4 * x + 2 * y + c
        local = [pltpu.make_async_copy(src_refs[i] if gather else src_refs[i].at[me], land_refs[i].at[me],
                                       local_sems.at[i]) for i in range(n)]
        for cp in local:
            cp.start()
        for cp in local:
            cp.wait()
        for i in range(n):
            for k in range(N_DEV - 1):
                _peer_copy(gather, src_refs[i], land_refs[i], send[i], recv[i], k, True).start()
        token[...] = jnp.zeros_like(token)

    sem = pltpu.SemaphoreType.DMA((N_DEV - 1,))
    hbm = lambda a: pltpu.HBM(a.shape, a.dtype)
    res = pl.pallas_call(
        body,
        name=name,
        in_specs=[HBM] * (2 * n),
        out_specs=[SEM] * (2 * n) + [HBM] * (2 * n) + [pl.BlockSpec(memory_space=pltpu.VMEM)],
        out_shape=[sem] * (2 * n) + [hbm(s) for s in srcs] + [hbm(a) for a in lands] + [jax.ShapeDtypeStruct((8, 128), F32)],
        input_output_aliases={i: 2 * n + i for i in range(2 * n)},
        scratch_shapes=[pltpu.SemaphoreType.DMA((n,))],
        compiler_params=pltpu.CompilerParams(has_side_effects=EFFECT),
    )(*[pltpu.with_memory_space_constraint(a, pltpu.HBM) for a in list(srcs) + lands])
    return res[:n], res[n:2 * n], res[2 * n:3 * n], res[3 * n:4 * n], res[4 * n]


def _exchange_wait(send_sems, recv_sems, src, land, after, gather, name):
    def body(src_ref, land_ref, send_ref, recv_ref, after_ref, src_out, land_out):
        for k in range(N_DEV - 1):
            cp = _peer_copy(gather, src_ref, land_ref, send_ref, recv_ref, k, False)
            cp.wait_send()
            cp.wait_recv()

    hbm = lambda a: pltpu.HBM(a.shape, a.dtype)
    return pl.pallas_call(
        body,
        name=name,
        in_specs=[HBM, HBM, SEM, SEM, ANY],
        out_specs=[HBM, HBM],
        out_shape=[hbm(src), hbm(land)],
        input_output_aliases={0: 0, 1: 1},
        compiler_params=pltpu.CompilerParams(has_side_effects=EFFECT),
    )(src, land, send_sems, recv_sems, after)[1]


def _col(v):
    return v.reshape(-1, 1).astype(F32)


def _to_internal_rows(w_t):
    pad = jnp.zeros((IN_PAD - IN_DIM, w_t.shape[1]), w_t.dtype)
    return jnp.concatenate([w_t[:ORIG_DT], w_t[ORIG_GA:ORIG_GS], w_t[ORIG_GS:IN_DIM], w_t[ORIG_DT:ORIG_GA], pad], axis=0)


def _from_internal_cols(g):
    return jnp.concatenate([g[:, :OFF_GA], g[:, OFF_DT:OFF_DT + N_SSD_HEADS], g[:, OFF_GA:OFF_GS], g[:, OFF_GS:OFF_DT]], axis=1)


def _local_step(xt, tgt, weight, small, grad_ready):
    t = xt.shape[1]
    n1 = _col(small["norm1_w"])
    n2 = _col(small["norm2_w"])
    nf = _col(small["final_norm_w"])
    bg = _col(small["b_gate"])
    sinks = small["attn_sinks"].reshape(-1).astype(F32)
    cbias = _col(small["ssd_conv_b"])
    dtb = _col(small["dt_bias"])
    alog = _col(small["a_log"])
    dsk = _col(small["d_skip"])
    gnw = _col(small["ssd_norm_w"])
    fb = small["ffn_conv_b"].reshape(2, D_FF, 1)

    xn = _norm_fwd(xt, n1, "norm1_fwd")
    cw = weight("ssd_conv_w", xn).T
    fw = weight("ffn_conv_w", xn).T.reshape(2, D_FF, FFN_CONV)
    w_in, w_in_t = weight("w_in", xn)
    proj = _matmul(w_in_t, xn, nt=False, out_dtype=F32, name="mm_in")
    ao, lse = _attn_fwd(proj, sinks)
    w_ao, w_ao_t = weight("w_attn_o", ao)
    attn = _matmul(w_ao_t, ao, nt=False, out_dtype=F32, name="mm_attn_o")
    xbc = _conv_silu_fwd(proj, cw, cbias)
    y, hst = _ssd_fwd(xbc, proj, dtb, alog, dsk)
    yn = _gnorm_fwd(y, proj, gnw)
    w_so, w_so_t = weight("w_ssd_o", yn)
    ssd = _matmul(w_so_t, yn, nt=False, out_dtype=F32, name="mm_ssd_o")
    mix = _gate_fwd(proj, bg, attn, ssd)
    w_out, w_out_t = weight("w_out", mix)
    h1 = _matmul(w_out_t, mix, nt=False, out_dtype=F32, name="mm_out", add=xt)
    hn = _norm_fwd(h1, n2, "norm2_fwd")
    w_up, w_up_t = weight("w_up", hn)
    u0 = _matmul(w_up_t, hn, nt=False, out_dtype=F32, name="mm_up").reshape(2, D_FF, t)
    gl = _ffn_fwd(u0, fw, fb)
    w_down, w_down_t = weight("w_down", gl)
    h2 = _matmul(w_down_t, gl, nt=False, out_dtype=F32, name="mm_down", add=h1)
    dh2, loss, d_nf = _final_norm_loss(h2, tgt, nf)

    g = {}
    dep = grad_ready("w_down", _matmul(gl, dh2, nt=True, out_dtype=BF16, name="mm_d_w_down"))
    dgl = _matmul(w_down, dh2, nt=False, out_dtype=F32, name="mm_d_glu", dep=dep)
    du0, d_fwb = _ffn_bwd(u0, fw, fb, dgl)
    du0 = du0.reshape(2 * D_FF, t)
    dep = grad_ready("w_up", _matmul(hn, du0, nt=True, out_dtype=BF16, name="mm_d_w_up"))
    dhn = _matmul(w_up, du0, nt=False, out_dtype=F32, name="mm_d_hn", dep=dep)
    dh1, d_n2 = _norm_bwd(dhn, h1, n2, dh2, "norm2_bwd")
    dep = grad_ready("w_out", _matmul(mix, dh1, nt=True, out_dtype=BF16, name="mm_d_w_out"))
    dmix = _matmul(w_out, dh1, nt=False, out_dtype=F32, name="mm_d_mix", dep=dep)
    d_attn, d_ssd, d_ga, d_gs, d_ba, d_bs = _gate_bwd(proj, bg, attn, ssd, dmix)
    dep = grad_ready("w_attn_o", _matmul(ao, d_attn, nt=True, out_dtype=BF16, name="mm_d_w_attn_o"))
    dao = _matmul(w_ao, d_attn, nt=False, out_dtype=F32, name="mm_d_ao", dep=dep)
    dq, dk, dv, d_sinks = _attn_bwd(proj, sinks, ao, lse, dao)
    dep = grad_ready("w_ssd_o", _matmul(yn, d_ssd, nt=True, out_dtype=BF16, name="mm_d_w_ssd_o"))
    dyn = _matmul(w_so, d_ssd, nt=False, out_dtype=F32, name="mm_d_yn", dep=dep)
    dy, dz, d_gnw = _gnorm_bwd(dyn, y, proj, gnw)
    dxs, dbm, dcm, ddt, d_alog, d_dsk, d_dtb = _ssd_bwd(xbc, proj, dtb, alog, dsk, hst, dy)
    dx_xs, dwb_xs = _conv_silu_bwd(proj, cw, cbias, dxs, 0, "ssd_conv_bwd_x")
    dx_b, dwb_b = _conv_silu_bwd(proj, cw, cbias, dbm, D_INNER, "ssd_conv_bwd_b")
    dx_c, dwb_c = _conv_silu_bwd(proj, cw, cbias, dcm, D_INNER + BC_DIM, "ssd_conv_bwd_c")
    dwb_conv = jnp.concatenate([dwb_xs, dwb_b, dwb_c], axis=0)
    ddt_rows = jnp.concatenate([ddt.astype(BF16), jnp.zeros((IN_PAD - OFF_DT - N_SSD_HEADS, t), BF16)], axis=0)
    dproj = jnp.concatenate([dq, dk, dv, dz, dx_xs, dx_b, dx_c, d_ga, d_gs, ddt_rows], axis=0)
    dep = grad_ready("w_in", _from_internal_cols(_matmul(xn, dproj, nt=True, out_dtype=BF16, name="mm_d_w_in")))
    dxn = _matmul(w_in, dproj, nt=False, out_dtype=F32, name="mm_d_xn", dep=dep)
    dx, d_n1 = _norm_bwd(dxn, xt, n1, dh1, "norm1_bwd")

    g["norm1_w"] = d_n1
    g["b_gate"] = jnp.concatenate([d_ba, d_bs], axis=0)
    g["attn_sinks"] = d_sinks
    g["ssd_conv_w"] = dwb_conv[:, :SSD_CONV].T
    g["ssd_conv_b"] = dwb_conv[:, SSD_CONV]
    g["dt_bias"] = d_dtb
    g["a_log"] = d_alog
    g["d_skip"] = d_dsk
    g["ssd_norm_w"] = d_gnw
    g["norm2_w"] = d_n2
    d_fwb = d_fwb.reshape(2 * D_FF, 128)
    g["ffn_conv_w"] = d_fwb[:, :FFN_CONV].T
    g["ffn_conv_b"] = d_fwb[:, FFN_CONV]
    g["final_norm_w"] = d_nf
    return loss, dx, g


SHARDED = ("w_in", "w_attn_o", "w_ssd_o", "w_out", "w_up", "w_down")
SMALL = ("norm1_w", "b_gate", "attn_sinks", "ssd_conv_w", "ssd_conv_b", "dt_bias", "a_log", "d_skip", "ssd_norm_w",
         "norm2_w", "ffn_conv_w", "ffn_conv_b", "final_norm_w")
SMALL_SHAPES = {"norm1_w": (1, D_MODEL), "b_gate": (1, 2 * D_MODEL), "attn_sinks": (1, N_Q_HEADS),
                "ssd_conv_w": (1, SSD_CONV, XBC_DIM), "ssd_conv_b": (1, XBC_DIM), "dt_bias": (1, N_SSD_HEADS),
                "a_log": (1, N_SSD_HEADS), "d_skip": (1, N_SSD_HEADS), "ssd_norm_w": (1, D_INNER),
                "norm2_w": (1, D_MODEL), "ffn_conv_w": (1, FFN_CONV, 2 * D_FF), "ffn_conv_b": (1, 2 * D_FF),
                "final_norm_w": (D_MODEL,)}
WEIGHT_ORDER = ("norm1_w", "w_in", "b_gate", "attn_sinks", "w_attn_o", "ssd_conv_w", "ssd_conv_b", "dt_bias", "a_log",
                "d_skip", "ssd_norm_w", "w_ssd_o", "w_out", "norm2_w", "w_up", "ffn_conv_w", "ffn_conv_b", "w_down",
                "final_norm_w")


def _pack(parts):
    flat = jnp.concatenate([p.reshape(-1).astype(F32) for p in parts])
    rows = -(-flat.shape[0] // 1024) * 8
    return jnp.pad(flat, (0, rows * 128 - flat.shape[0])).reshape(rows, 128)


def _unpack(packed, shapes):
    flat = packed.reshape(-1)
    out, pos = [], 0
    for shp in shapes:
        size = 1
        for d in shp:
            size *= d
        out.append(flat[pos:pos + size].reshape(shp))
        pos += size
    return out


def kernel(x, norm1_w, w_in, b_gate, attn_sinks, w_attn_o, ssd_conv_w, ssd_conv_b, dt_bias, a_log, d_skip, ssd_norm_w, w_ssd_o, w_out, norm2_w, w_up, ffn_conv_w, ffn_conv_b, w_down, final_norm_w, loss_target, m_norm1_w, m_w_in, m_b_gate, m_attn_sinks, m_w_attn_o, m_ssd_conv_w, m_ssd_conv_b, m_dt_bias, m_a_log, m_d_skip, m_ssd_norm_w, m_w_ssd_o, m_w_out, m_norm2_w, m_w_up, m_ffn_conv_w, m_ffn_conv_b, m_w_down, m_final_norm_w, v_norm1_w, v_w_in, v_b_gate, v_attn_sinks, v_w_attn_o, v_ssd_conv_w, v_ssd_conv_b, v_dt_bias, v_a_log, v_d_skip, v_ssd_norm_w, v_w_ssd_o, v_w_out, v_norm2_w, v_w_up, v_ffn_conv_w, v_ffn_conv_b, v_w_down, v_final_norm_w):
    w = dict(norm1_w=norm1_w, w_in=w_in, b_gate=b_gate, attn_sinks=attn_sinks, w_attn_o=w_attn_o, ssd_conv_w=ssd_conv_w, ssd_conv_b=ssd_conv_b, dt_bias=dt_bias, a_log=a_log, d_skip=d_skip, ssd_norm_w=ssd_norm_w, w_ssd_o=w_ssd_o, w_out=w_out, norm2_w=norm2_w, w_up=w_up, ffn_conv_w=ffn_conv_w, ffn_conv_b=ffn_conv_b, w_down=w_down, final_norm_w=final_norm_w)
    m = dict(norm1_w=m_norm1_w, w_in=m_w_in, b_gate=m_b_gate, attn_sinks=m_attn_sinks, w_attn_o=m_w_attn_o, ssd_conv_w=m_ssd_conv_w, ssd_conv_b=m_ssd_conv_b, dt_bias=m_dt_bias, a_log=m_a_log, d_skip=m_d_skip, ssd_norm_w=m_ssd_norm_w, w_ssd_o=m_w_ssd_o, w_out=m_w_out, norm2_w=m_norm2_w, w_up=m_w_up, ffn_conv_w=m_ffn_conv_w, ffn_conv_b=m_ffn_conv_b, w_down=m_w_down, final_norm_w=m_final_norm_w)
    v = dict(norm1_w=v_norm1_w, w_in=v_w_in, b_gate=v_b_gate, attn_sinks=v_attn_sinks, w_attn_o=v_w_attn_o, ssd_conv_w=v_ssd_conv_w, ssd_conv_b=v_ssd_conv_b, dt_bias=v_dt_bias, a_log=v_a_log, d_skip=v_d_skip, ssd_norm_w=v_ssd_norm_w, w_ssd_o=v_w_ssd_o, w_out=v_w_out, norm2_w=v_norm2_w, w_up=v_w_up, ffn_conv_w=v_ffn_conv_w, ffn_conv_b=v_ffn_conv_b, w_down=v_w_down, final_norm_w=v_final_norm_w)
    me = 4 * lax.axis_index("x") + 2 * lax.axis_index("y") + lax.axis_index("c")
    conv_cols = XBC_DIM // N_DEV
    ffn_cols = 2 * D_FF // N_DEV

    shards = {"ssd_conv_w": ssd_conv_w[0], "ffn_conv_w": ffn_conv_w[0], "w_in": w_in[0].T.astype(BF16),
              "w_attn_o": w_attn_o[0].astype(BF16), "w_ssd_o": w_ssd_o[0].astype(BF16), "w_out": w_out[0].astype(BF16),
              "w_up": w_up[0].T.astype(BF16), "w_down": w_down[0].astype(BF16)}
    order = list(shards)
    g_send, g_recv, g_src, g_land, _ = _exchange_start(list(shards.values()), True, "gather_start")

    def weight(name, after):
        i = order.index(name)
        land = _exchange_wait(g_send[i], g_recv[i], g_src[i], g_land[i], after, True, "gather_wait_" + name)
        if name == "ssd_conv_w":
            return jnp.transpose(land, (1, 0, 2)).reshape(SSD_CONV, XBC_DIM)
        if name == "ffn_conv_w":
            return jnp.transpose(land, (1, 0, 2)).reshape(FFN_CONV, 2 * D_FF)
        if name == "w_in":
            w_t = _to_internal_rows(land.reshape(IN_DIM, D_MODEL))
            return w_t.T, w_t
        if name == "w_up":
            w_t = land.reshape(2 * D_FF, D_MODEL)
            return w_t.T, w_t
        full = land.reshape(-1, D_MODEL)
        return full, full.T

    pending = {}

    def grad_ready(name, grad):
        if name in ("w_in", "w_up"):
            chunks = jnp.transpose(grad.reshape(grad.shape[0], N_DEV, -1), (1, 0, 2))
        else:
            chunks = grad.reshape(N_DEV, -1, D_MODEL)
        send, recv, src, land, token = _exchange_start([chunks], False, "grad_start_" + name)
        pending[name] = (send[0], recv[0], src[0], land[0])
        return token

    small = {k: w[k][0] if k != "final_norm_w" else w[k] for k in SMALL}
    loss, dx, g = _local_step(x[0].T, loss_target[0].T, weight, small, grad_ready)

    packed = _pack([loss] + [g[k] for k in SMALL])
    total = _sum_slots(_all_gather([packed], "gather_small_grads")[0])
    tot = _unpack(total, [(1,)] + [SMALL_SHAPES[k] for k in SMALL])
    loss_sum = tot[0].reshape(())
    gs = dict(zip(SMALL, tot[1:]))
    gs["ssd_conv_w"] = lax.dynamic_slice_in_dim(gs["ssd_conv_w"], me * conv_cols, conv_cols, axis=2)
    gs["ffn_conv_w"] = lax.dynamic_slice_in_dim(gs["ffn_conv_w"], me * ffn_cols, ffn_cols, axis=2)
    upd = _adamw_small(_pack([gs[k] for k in SMALL]), _pack([w[k] for k in SMALL]), _pack([m[k] for k in SMALL]),
                       _pack([v[k] for k in SMALL]))
    shapes = [w[k].shape for k in SMALL]
    d_s, m_s, v_s = (dict(zip(SMALL, _unpack(u, shapes))) for u in upd)
    res = {}
    for k in SMALL:
        res[k] = (gs[k], d_s[k], m_s[k], v_s[k])

    after = upd[0]
    for name in ("w_down", "w_up", "w_out", "w_attn_o", "w_ssd_o", "w_in"):
        parts = _exchange_wait(*pending[name], after, False, "grad_wait_" + name)
        res[name] = _adamw_sharded(parts, w[name][0], m[name][0], v[name][0], "adamw_" + name)
        after = res[name][0]

    grad_x = dx.T[None]
    outs = [loss_sum, grad_x]
    for i in range(4):
        for k in WEIGHT_ORDER:
            r = res[k][i]
            outs.append(r[None] if k in SHARDED else r)
    return tuple(outs)
```

```python
import functools

import jax
import jax.numpy as jnp
from jax import lax
from jax.experimental import pallas as pl
from jax.experimental.pallas import tpu as pltpu

F32 = jnp.float32
BF16 = jnp.bfloat16
HIGHEST = lax.Precision.HIGHEST

D_MODEL = 1024
N_Q_HEADS = 16
N_KV_HEADS = 4
HEAD_DIM = 64
WINDOW = 128
Q_PER_KV = N_Q_HEADS // N_KV_HEADS
Q_DIM = N_Q_HEADS * HEAD_DIM
KV_DIM = N_KV_HEADS * HEAD_DIM
D_INNER = 2048
SSD_HEAD_DIM = 64
N_SSD_HEADS = 32
N_SSD_GROUPS = 4
HEADS_PER_GROUP = N_SSD_HEADS // N_SSD_GROUPS
D_STATE = 128
BC_DIM = N_SSD_GROUPS * D_STATE
XBC_DIM = D_INNER + 2 * BC_DIM
SSD_CONV = 4
CHUNK = 128
D_FF = 2816
FFN_CONV = 3
EPS = 1e-5
NEG = -1e30
IN_DIM = 8736
N_DEV = 8

OFF_Q = 0
OFF_K = OFF_Q + Q_DIM
OFF_V = OFF_K + KV_DIM
OFF_Z = OFF_V + KV_DIM
OFF_X = OFF_Z + D_INNER
OFF_GA = OFF_X + XBC_DIM
OFF_GS = OFF_GA + D_MODEL
OFF_DT = OFF_GS + D_MODEL
IN_PAD = OFF_DT + 128
ORIG_DT = OFF_X + XBC_DIM
ORIG_GA = ORIG_DT + N_SSD_HEADS
ORIG_GS = ORIG_GA + D_MODEL

ADAM_LR = 0.001
ADAM_B1 = 0.9
ADAM_B2 = 0.999
ADAM_EPS = 1e-08
ADAM_WD = 0.01
ADAM_STEP = 10

VMEM_LIMIT = 48 * 1024 * 1024
MESH = pl.DeviceIdType.MESH


def _cparams(*sem):
    return pltpu.CompilerParams(dimension_semantics=sem, vmem_limit_bytes=VMEM_LIMIT)


def _tile(n, prefs):
    for p in prefs:
        if n % p == 0:
            return p
    return n


def _sigmoid(x):
    return 1.0 / (1.0 + jnp.exp(-x))


def _softplus(x):
    return jnp.maximum(x, 0.0) + jnp.log(1.0 + jnp.exp(-jnp.abs(x)))


def _rowsum(x):
    return jnp.sum(x, axis=1, keepdims=True)


def _colsum(x):
    return jnp.sum(x, axis=0, keepdims=True)


def _dot(a, b):
    return jnp.dot(a, b, preferred_element_type=F32)


def _dot_nt(a, b):
    return lax.dot_general(a, b, (((1,), (1,)), ((), ())), preferred_element_type=F32)


def _dot_tn(a, b):
    return lax.dot_general(a, b, (((0,), (0,)), ((), ())), preferred_element_type=F32)


def _shift_right(x, j):
    if j == 0:
        return x
    lane = lax.broadcasted_iota(jnp.int32, x.shape, 1)
    return jnp.where(lane >= j, pltpu.roll(x, j, 1), 0.0)


def _shift_left(x, j):
    if j == 0:
        return x
    n = x.shape[1]
    lane = lax.broadcasted_iota(jnp.int32, x.shape, 1)
    return jnp.where(lane < n - j, pltpu.roll(x, n - j, 1), 0.0)


MATMUL_VMEM_BUDGET = 36 * 1024 * 1024
MATMUL_MAX_TK = 3072


def _matmul_tiles(m, n, k, a_bytes, b_bytes, out_bytes, has_add):
    tm = _tile(m, (512, 384, 256, 128))
    tk = max(d for d in range(128, min(k, MATMUL_MAX_TK) + 1, 128) if k % d == 0)
    for tn in sorted({d for d in range(128, n + 1, 128) if n % d == 0}, reverse=True):
        need = 2 * (tm * tk * a_bytes + tk * tn * b_bytes) + tm * tn * (2 * out_bytes + (4 if k > tk else 0) + (8 if has_add else 0))
        if tn <= 3072 and need <= MATMUL_VMEM_BUDGET:
            return tm, tn, tk
    return tm, 128, tk


def _matmul(a, b, *, nt, out_dtype, name, add=None, dep=None, tn_a=False):
    if tn_a:
        k, m = a.shape
    else:
        m, k = a.shape
    n = b.shape[0] if nt else b.shape[1]
    tm, tn, tk = _matmul_tiles(m, n, k, a.dtype.itemsize, b.dtype.itemsize, jnp.dtype(out_dtype).itemsize, add is not None)
    nk = k // tk

    def body(a_ref, b_ref, *rest):
        if dep is not None:
            rest = rest[1:]
        r_ref = None
        if add is not None:
            r_ref, rest = rest[0], rest[1:]
        o_ref = rest[0]
        av = a_ref[...].astype(BF16)
        bv = b_ref[...].astype(BF16)
        part = _dot_tn(av, bv) if tn_a else _dot_nt(av, bv) if nt else _dot(av, bv)

        def finish(r):
            if add is not None:
                r = r + r_ref[...]
            o_ref[...] = r.astype(out_dtype)

        if nk == 1:
            finish(part)
            return
        acc = rest[1]
        kk = pl.program_id(2)

        @pl.when(kk == 0)
        def _():
            acc[...] = part

        @pl.when((kk > 0) & (kk < nk - 1))
        def _():
            acc[...] += part

        @pl.when(kk == nk - 1)
        def _():
            finish(acc[...] + part)

    in_specs = [
        pl.BlockSpec((tk, tm), lambda i, j, kk: (kk, i)) if tn_a else pl.BlockSpec((tm, tk), lambda i, j, kk: (i, kk)),
        pl.BlockSpec((tn, tk), lambda i, j, kk: (j, kk)) if nt else pl.BlockSpec((tk, tn), lambda i, j, kk: (kk, j)),
    ]
    args = [a, b]
    if dep is not None:
        in_specs.append(pl.BlockSpec(memory_space=pl.ANY))
        args.append(dep)
    if add is not None:
        in_specs.append(pl.BlockSpec((tm, tn), lambda i, j, kk: (i, j)))
        args.append(add)
    return pl.pallas_call(
        body,
        name=name,
        grid=(m // tm, n // tn, nk),
        in_specs=in_specs,
        out_specs=pl.BlockSpec((tm, tn), lambda i, j, kk: (i, j)),
        out_shape=jax.ShapeDtypeStruct((m, n), out_dtype),
        scratch_shapes=[pltpu.VMEM((tm, tn), F32)] if nk > 1 else [],
        compiler_params=_cparams("parallel", "parallel", "arbitrary"),
    )(*args)


def _norm_fwd(x, w_col, name):
    f, t = x.shape
    tt = _tile(t, (512, 256, 128))

    def body(x_ref, w_ref, o_ref):
        xv = x_ref[...]
        r = lax.rsqrt(jnp.mean(xv * xv, axis=0, keepdims=True) + EPS)
        o_ref[...] = (xv * r * w_ref[...]).astype(BF16)

    return pl.pallas_call(
        body,
        name=name,
        grid=(t // tt,),
        in_specs=[pl.BlockSpec((f, tt), lambda i: (0, i)), pl.BlockSpec((f, 1), lambda i: (0, 0))],
        out_specs=pl.BlockSpec((f, tt), lambda i: (0, i)),
        out_shape=jax.ShapeDtypeStruct((f, t), BF16),
        compiler_params=_cparams("parallel"),
    )(x, w_col)


def _norm_bwd(dy, x, w_col, res, name):
    f, t = x.shape
    tt = _tile(t, (512, 256, 128))

    def body(dy_ref, x_ref, w_ref, res_ref, dx_ref, dw_ref):
        @pl.when(pl.program_id(0) == 0)
        def _():
            dw_ref[...] = jnp.zeros_like(dw_ref)

        xv = x_ref[...]
        r = lax.rsqrt(jnp.mean(xv * xv, axis=0, keepdims=True) + EPS)
        xhat = xv * r
        dyv = dy_ref[...]
        dw_ref[...] += _rowsum(dyv * xhat)
        dxhat = dyv * w_ref[...]
        dx_ref[...] = res_ref[...] + r * (dxhat - xhat * jnp.mean(dxhat * xhat, axis=0, keepdims=True))

    blk = pl.BlockSpec((f, tt), lambda i: (0, i))
    col = pl.BlockSpec((f, 1), lambda i: (0, 0))
    return pl.pallas_call(
        body,
        name=name,
        grid=(t // tt,),
        in_specs=[blk, blk, col, blk],
        out_specs=[blk, col],
        out_shape=[jax.ShapeDtypeStruct((f, t), F32), jax.ShapeDtypeStruct((f, 1), F32)],
        compiler_params=_cparams("arbitrary"),
    )(dy, x, w_col, res)


def _final_norm_loss(h, tgt, w_col):
    f, t = h.shape
    tt = _tile(t, (512, 256, 128))

    def body(h_ref, t_ref, w_ref, dh_ref, loss_ref, dw_ref):
        @pl.when(pl.program_id(0) == 0)
        def _():
            dw_ref[...] = jnp.zeros_like(dw_ref)
            loss_ref[...] = jnp.zeros_like(loss_ref)

        xv = h_ref[...]
        r = lax.rsqrt(jnp.mean(xv * xv, axis=0, keepdims=True) + EPS)
        xhat = xv * r
        wv = w_ref[...]
        err = xhat * wv - t_ref[...]
        loss_ref[...] += 0.5 * _rowsum(jnp.mean(err * err, axis=0, keepdims=True))
        dyv = err * (1.0 / f)
        dw_ref[...] += _rowsum(dyv * xhat)
        dxhat = dyv * wv
        dh_ref[...] = r * (dxhat - xhat * jnp.mean(dxhat * xhat, axis=0, keepdims=True))

    blk = pl.BlockSpec((f, tt), lambda i: (0, i))
    col = pl.BlockSpec((f, 1), lambda i: (0, 0))
    one = pl.BlockSpec((1, 1), lambda i: (0, 0))
    return pl.pallas_call(
        body,
        name="final_norm_loss",
        grid=(t // tt,),
        in_specs=[blk, blk, col],
        out_specs=[blk, one, col],
        out_shape=[jax.ShapeDtypeStruct((f, t), F32), jax.ShapeDtypeStruct((1, 1), F32), jax.ShapeDtypeStruct((f, 1), F32)],
        compiler_params=_cparams("arbitrary"),
    )(h, tgt, w_col)


def _attn_mask(n):
    shape = (2 * WINDOW, Q_PER_KV * WINDOW)
    si = lax.broadcasted_iota(jnp.int32, shape, 0)
    qi = lax.broadcasted_iota(jnp.int32, shape, 1) & (WINDOW - 1)
    dist = WINDOW + qi - si
    return (dist >= 0) & (dist < WINDOW) & ((si >= WINDOW) | (n > 0))


def _lane_cat(ref, row0, rows):
    return jnp.concatenate([ref[row0 + i * rows:row0 + (i + 1) * rows, :] for i in range(Q_PER_KV)], axis=1)


def _attn_fwd(proj, sinks):
    t = proj.shape[1]
    nb = t // WINDOW
    scale = HEAD_DIM ** -0.5

    def body(s_ref, q_ref, kc_ref, kp_ref, vc_ref, vp_ref, o_ref, lse_ref):
        n = pl.program_id(0)
        valid = _attn_mask(n)
        for g in range(N_KV_HEADS):
            rows = slice(g * HEAD_DIM, (g + 1) * HEAD_DIM)
            kt = jnp.concatenate([kp_ref[rows, :], kc_ref[rows, :]], axis=1).astype(BF16)
            vt = jnp.concatenate([vp_ref[rows, :], vc_ref[rows, :]], axis=1).astype(BF16)
            qcat = _lane_cat(q_ref, g * Q_PER_KV * HEAD_DIM, HEAD_DIM).astype(BF16)
            s = jnp.where(valid, _dot_tn(kt, qcat) * scale, NEG)
            sink = jnp.concatenate(
                [jnp.full((1, WINDOW), s_ref[g * Q_PER_KV + i], F32) for i in range(Q_PER_KV)], axis=1)
            m = jnp.maximum(jnp.max(s, axis=0, keepdims=True), sink)
            p = jnp.where(valid, jnp.exp(s - m), 0.0)
            denom = _colsum(p) + jnp.exp(sink - m)
            probs = (p / denom).astype(BF16)
            out = _dot(vt, probs)
            lse = m + jnp.log(denom)
            for i in range(Q_PER_KV):
                h = g * Q_PER_KV + i
                o_ref[h * HEAD_DIM:(h + 1) * HEAD_DIM, :] = out[:, i * WINDOW:(i + 1) * WINDOW]
                lse_ref[h:h + 1, :] = lse[:, i * WINDOW:(i + 1) * WINDOW]

    kb = OFF_K // KV_DIM
    vb = OFF_V // KV_DIM
    prev = lambda n: jnp.maximum(n - 1, 0)
    return pl.pallas_call(
        body,
        name="attn_fwd",
        grid=(nb,),
        in_specs=[
            pl.BlockSpec(memory_space=pltpu.SMEM),
            pl.BlockSpec((Q_DIM, WINDOW), lambda n: (0, n)),
            pl.BlockSpec((KV_DIM, WINDOW), lambda n: (kb, n)),
            pl.BlockSpec((KV_DIM, WINDOW), lambda n: (kb, prev(n))),
            pl.BlockSpec((KV_DIM, WINDOW), lambda n: (vb, n)),
            pl.BlockSpec((KV_DIM, WINDOW), lambda n: (vb, prev(n))),
        ],
        out_specs=[pl.BlockSpec((Q_DIM, WINDOW), lambda n: (0, n)), pl.BlockSpec((N_Q_HEADS, WINDOW), lambda n: (0, n))],
        out_shape=[jax.ShapeDtypeStruct((Q_DIM, t), F32), jax.ShapeDtypeStruct((N_Q_HEADS, t), F32)],
        compiler_params=_cparams("parallel"),
    )(sinks, proj, proj, proj, proj, proj)


def _attn_bwd(proj, sinks, out, lse, dout):
    t = proj.shape[1]
    nb = t // WINDOW
    scale = HEAD_DIM ** -0.5

    def body(s_ref, q_ref, kc_ref, kp_ref, vc_ref, vp_ref, o_ref, lse_ref, do_ref,
             dq_ref, dk_ref, dv_ref, ds_ref, dk_carry, dv_carry):
        step = pl.program_id(0)
        n = nb - 1 - step

        @pl.when(step == 0)
        def _():
            dk_carry[...] = jnp.zeros_like(dk_carry)
            dv_carry[...] = jnp.zeros_like(dv_carry)
            ds_ref[...] = jnp.zeros_like(ds_ref)

        valid = _attn_mask(n)
        for g in range(N_KV_HEADS):
            rows = slice(g * HEAD_DIM, (g + 1) * HEAD_DIM)
            q0 = g * Q_PER_KV * HEAD_DIM
            kt = jnp.concatenate([kp_ref[rows, :], kc_ref[rows, :]], axis=1).astype(BF16)
            vt = jnp.concatenate([vp_ref[rows, :], vc_ref[rows, :]], axis=1).astype(BF16)
            qcat = _lane_cat(q_ref, q0, HEAD_DIM).astype(BF16)
            ocat = _lane_cat(o_ref, q0, HEAD_DIM)
            docat = _lane_cat(do_ref, q0, HEAD_DIM)
            dob = docat.astype(BF16)
            lse_cat = jnp.concatenate(
                [lse_ref[g * Q_PER_KV + i:g * Q_PER_KV + i + 1, :] for i in range(Q_PER_KV)], axis=1)
            sink = jnp.concatenate(
                [jnp.full((1, WINDOW), s_ref[g * Q_PER_KV + i], F32) for i in range(Q_PER_KV)], axis=1)
            s = jnp.where(valid, _dot_tn(kt, qcat) * scale, NEG)
            p = jnp.where(valid, jnp.exp(s - lse_cat), 0.0)
            dp = _dot_tn(vt, dob)
            delta = _colsum(docat * ocat)
            dsc = (p * (dp - delta)).astype(BF16)
            dsink_row = -jnp.exp(sink - lse_cat) * delta
            dq = _dot(kt, dsc) * scale
            dk = _dot_nt(qcat, dsc) * scale
            dv = _dot_nt(dob, p.astype(BF16))
            for i in range(Q_PER_KV):
                h = g * Q_PER_KV + i
                dq_ref[h * HEAD_DIM:(h + 1) * HEAD_DIM, :] = dq[:, i * WINDOW:(i + 1) * WINDOW].astype(BF16)
                ds_ref[h:h + 1, :] += _rowsum(dsink_row[:, i * WINDOW:(i + 1) * WINDOW])
            dk_ref[rows, :] = (dk[:, WINDOW:] + dk_carry[rows, :]).astype(BF16)
            dv_ref[rows, :] = (dv[:, WINDOW:] + dv_carry[rows, :]).astype(BF16)
            dk_carry[rows, :] = dk[:, :WINDOW]
            dv_carry[rows, :] = dv[:, :WINDOW]

    kb = OFF_K // KV_DIM
    vb = OFF_V // KV_DIM
    cur = lambda i: nb - 1 - i
    prev = lambda i: jnp.maximum(nb - 2 - i, 0)
    qspec = pl.BlockSpec((Q_DIM, WINDOW), lambda i: (0, cur(i)))
    kvspec = pl.BlockSpec((KV_DIM, WINDOW), lambda i: (0, cur(i)))
    return pl.pallas_call(
        body,
        name="attn_bwd",
        grid=(nb,),
        in_specs=[
            pl.BlockSpec(memory_space=pltpu.SMEM),
            qspec,
            pl.BlockSpec((KV_DIM, WINDOW), lambda i: (kb, cur(i))),
            pl.BlockSpec((KV_DIM, WINDOW), lambda i: (kb, prev(i))),
            pl.BlockSpec((KV_DIM, WINDOW), lambda i: (vb, cur(i))),
            pl.BlockSpec((KV_DIM, WINDOW), lambda i: (vb, prev(i))),
            qspec,
            pl.BlockSpec((N_Q_HEADS, WINDOW), lambda i: (0, cur(i))),
            qspec,
        ],
        out_specs=[qspec, kvspec, kvspec, pl.BlockSpec((N_Q_HEADS, 1), lambda i: (0, 0))],
        out_shape=[
            jax.ShapeDtypeStruct((Q_DIM, t), BF16),
            jax.ShapeDtypeStruct((KV_DIM, t), BF16),
            jax.ShapeDtypeStruct((KV_DIM, t), BF16),
            jax.ShapeDtypeStruct((N_Q_HEADS, 1), F32),
        ],
        scratch_shapes=[pltpu.VMEM((KV_DIM, WINDOW), F32), pltpu.VMEM((KV_DIM, WINDOW), F32)],
        compiler_params=_cparams("arbitrary"),
    )(sinks, proj, proj, proj, proj, proj, out, lse, dout)


CONV_ROWS = 256


def _conv_silu_fwd(proj, w_col, b_col):
    t = proj.shape[1]
    r0 = OFF_X // CONV_ROWS

    def body(x_ref, w_ref, b_ref, o_ref):
        xv = x_ref[...]
        wv = w_ref[...]
        y = b_ref[...] + wv[:, SSD_CONV - 1:SSD_CONV] * xv
        for k in range(SSD_CONV - 1):
            y = y + wv[:, k:k + 1] * _shift_right(xv, SSD_CONV - 1 - k)
        o_ref[...] = y * _sigmoid(y)

    return pl.pallas_call(
        body,
        name="ssd_conv_fwd",
        grid=(XBC_DIM // CONV_ROWS,),
        in_specs=[
            pl.BlockSpec((CONV_ROWS, t), lambda i: (r0 + i, 0)),
            pl.BlockSpec((CONV_ROWS, SSD_CONV), lambda i: (i, 0)),
            pl.BlockSpec((CONV_ROWS, 1), lambda i: (i, 0)),
        ],
        out_specs=pl.BlockSpec((CONV_ROWS, t), lambda i: (i, 0)),
        out_shape=jax.ShapeDtypeStruct((XBC_DIM, t), F32),
        compiler_params=_cparams("parallel"),
    )(proj, w_col, b_col)


def _conv_silu_bwd(proj, w_col, b_col, dout, row0, name):
    t = proj.shape[1]
    nrows = dout.shape[0]
    p0 = (OFF_X + row0) // CONV_ROWS
    c0 = row0 // CONV_ROWS

    def body(x_ref, w_ref, b_ref, do_ref, dx_ref, dwb_ref):
        xv = x_ref[...]
        wv = w_ref[...]
        y = b_ref[...] + wv[:, SSD_CONV - 1:SSD_CONV] * xv
        for k in range(SSD_CONV - 1):
            y = y + wv[:, k:k + 1] * _shift_right(xv, SSD_CONV - 1 - k)
        sg = _sigmoid(y)
        dy = do_ref[...] * (sg * (1.0 + y * (1.0 - sg)))
        lane = lax.broadcasted_iota(jnp.int32, (CONV_ROWS, 128), 1)
        dwb = jnp.where(lane == SSD_CONV, _rowsum(dy), 0.0)
        dx = wv[:, SSD_CONV - 1:SSD_CONV] * dy
        dwb = jnp.where(lane == SSD_CONV - 1, _rowsum(dy * xv), dwb)
        for k in range(SSD_CONV - 1):
            j = SSD_CONV - 1 - k
            dx = dx + wv[:, k:k + 1] * _shift_left(dy, j)
            dwb = jnp.where(lane == k, _rowsum(dy * _shift_right(xv, j)), dwb)
        dx_ref[...] = dx.astype(BF16)
        dwb_ref[...] = dwb

    return pl.pallas_call(
        body,
        name=name,
        grid=(nrows // CONV_ROWS,),
        in_specs=[
            pl.BlockSpec((CONV_ROWS, t), lambda i: (p0 + i, 0)),
            pl.BlockSpec((CONV_ROWS, SSD_CONV), lambda i: (c0 + i, 0)),
            pl.BlockSpec((CONV_ROWS, 1), lambda i: (c0 + i, 0)),
            pl.BlockSpec((CONV_ROWS, t), lambda i: (i, 0)),
        ],
        out_specs=[pl.BlockSpec((CONV_ROWS, t), lambda i: (i, 0)), pl.BlockSpec((CONV_ROWS, 128), lambda i: (i, 0))],
        out_shape=[jax.ShapeDtypeStruct((nrows, t), BF16), jax.ShapeDtypeStruct((nrows, 128), F32)],
        compiler_params=_cparams("parallel"),
    )(proj, w_col, b_col, dout)


GROUP_ROWS = HEADS_PER_GROUP * SSD_HEAD_DIM


def _ssd_specs(nc, order):
    hb = D_INNER // D_STATE
    dtb = OFF_DT // HEADS_PER_GROUP
    col = pl.BlockSpec((HEADS_PER_GROUP, 1), lambda g, c: (g, 0))
    return [
        pl.BlockSpec((GROUP_ROWS, CHUNK), lambda g, c: (g, order(c))),
        pl.BlockSpec((D_STATE, CHUNK), lambda g, c: (hb + g, order(c))),
        pl.BlockSpec((D_STATE, CHUNK), lambda g, c: (hb + N_SSD_GROUPS + g, order(c))),
        pl.BlockSpec((HEADS_PER_GROUP, CHUNK), lambda g, c: (dtb + g, order(c))),
        col, col, col,
    ]


def _ssd_common(dt_ref, dtb_ref, alog_ref):
    z = dt_ref[...] + dtb_ref[...]
    dt = _softplus(z)
    a_neg = -jnp.exp(alog_ref[...])
    d_a = dt * a_neg
    row = lax.broadcasted_iota(jnp.int32, (CHUNK, CHUNK), 0)
    colm = lax.broadcasted_iota(jnp.int32, (CHUNK, CHUNK), 1)
    upper = (row <= colm).astype(F32)
    a_cs = jnp.dot(d_a, upper, precision=HIGHEST, preferred_element_type=F32)
    a_last = _rowsum(d_a)
    return z, dt, a_neg, a_cs, a_last, row >= colm, row == colm


def _decay(a_row, causal):
    a_s = jnp.broadcast_to(a_row, (CHUNK, CHUNK))
    seg = a_s.T - a_s
    return jnp.where(causal, jnp.exp(jnp.where(causal, seg, 0.0)), 0.0)


def _ssd_fwd(xbc, proj, dtb_col, alog_col, dsk_col):
    t = xbc.shape[1]
    nc = t // CHUNK

    def body(xs_ref, b_ref, c_ref, dt_ref, dtb_ref, alog_ref, dsk_ref, y_ref, hst_ref, h_scr):
        @pl.when(pl.program_id(1) == 0)
        def _():
            h_scr[...] = jnp.zeros_like(h_scr)

        _, dt, _, a_cs, a_last, causal, _ = _ssd_common(dt_ref, dtb_ref, alog_ref)
        bb = b_ref[...].astype(BF16)
        cb_ = c_ref[...].astype(BF16)
        cb = _dot_tn(cb_, bb)
        hst_ref[0, 0] = h_scr[...]
        dsk = dsk_ref[...]
        for j in range(HEADS_PER_GROUP):
            rows = slice(j * SSD_HEAD_DIM, (j + 1) * SSD_HEAD_DIM)
            a = a_cs[j:j + 1, :]
            m = (cb * _decay(a, causal)).astype(BF16)
            xs = xs_ref[rows, :]
            xc = xs * dt[j:j + 1, :]
            hj = h_scr[rows, :]
            y = _dot_nt(xc.astype(BF16), m) + _dot(hj.astype(BF16), cb_) * jnp.exp(a) + dsk[j:j + 1, :] * xs
            y_ref[rows, :] = y
            al = a_last[j:j + 1, :]
            w = jnp.exp(al - a)
            h_scr[rows, :] = jnp.exp(al) * hj + _dot_nt((xc * w).astype(BF16), bb)

    return pl.pallas_call(
        body,
        name="ssd_fwd",
        grid=(N_SSD_GROUPS, nc),
        in_specs=_ssd_specs(nc, lambda c: c),
        out_specs=[
            pl.BlockSpec((GROUP_ROWS, CHUNK), lambda g, c: (g, c)),
            pl.BlockSpec((1, 1, GROUP_ROWS, D_STATE), lambda g, c: (g, c, 0, 0)),
        ],
        out_shape=[
            jax.ShapeDtypeStruct((D_INNER, t), F32),
            jax.ShapeDtypeStruct((N_SSD_GROUPS, nc, GROUP_ROWS, D_STATE), F32),
        ],
        scratch_shapes=[pltpu.VMEM((GROUP_ROWS, D_STATE), F32)],
        compiler_params=_cparams("parallel", "arbitrary"),
    )(xbc, xbc, xbc, proj, dtb_col, alog_col, dsk_col)


def _ssd_bwd(xbc, proj, dtb_col, alog_col, dsk_col, hst, dy):
    t = xbc.shape[1]
    nc = t // CHUNK
    rev = lambda c: nc - 1 - c

    def body(xs_ref, b_ref, c_ref, dt_ref, dtb_ref, alog_ref, dsk_ref, hst_ref, dy_ref,
             dxs_ref, db_ref, dc_ref, ddt_ref, dalog_ref, ddsk_ref, ddtb_ref, dh_scr, da_scr, ddt_scr, dd_scr):
        @pl.when(pl.program_id(1) == 0)
        def _():
            dh_scr[...] = jnp.zeros_like(dh_scr)
            dalog_ref[...] = jnp.zeros_like(dalog_ref)
            ddsk_ref[...] = jnp.zeros_like(ddsk_ref)
            ddtb_ref[...] = jnp.zeros_like(ddtb_ref)

        z, dt, a_neg, a_cs, a_last, causal, eye = _ssd_common(dt_ref, dtb_ref, alog_ref)
        bb = b_ref[...].astype(BF16)
        cb_ = c_ref[...].astype(BF16)
        cb = _dot_tn(cb_, bb)
        dsk = dsk_ref[...]
        last_lane = lax.broadcasted_iota(jnp.int32, (1, CHUNK), 1) == CHUNK - 1
        dcb = jnp.zeros((CHUNK, CHUNK), F32)
        dc_acc = jnp.zeros((D_STATE, CHUNK), F32)
        db_acc = jnp.zeros((D_STATE, CHUNK), F32)
        for j in range(HEADS_PER_GROUP):
            rows = slice(j * SSD_HEAD_DIM, (j + 1) * SSD_HEAD_DIM)
            a = a_cs[j:j + 1, :]
            al = a_last[j:j + 1, :]
            lam = _decay(a, causal)
            mf = cb * lam
            xs = xs_ref[rows, :]
            dtj = dt[j:j + 1, :]
            xc = xs * dtj
            w = jnp.exp(al - a)
            e = jnp.exp(a)
            gam = jnp.exp(al)
            hj = hst_ref[0, 0, rows, :]
            hjb = hj.astype(BF16)
            dyv = dy_ref[rows, :]
            dyb = dyv.astype(BF16)
            dd_scr[j:j + 1, :] = _colsum(dyv * xs)
            gb = (dyv * e).astype(BF16)
            dh_in = _dot_nt(gb, cb_)
            dc_acc = dc_acc + _dot_tn(hjb, gb)
            yoff = _dot(hjb, cb_) * e
            da = _colsum(dyv * yoff)
            dm = _dot_tn(dyb, xc.astype(BF16))
            dxc = _dot(dyb, mf.astype(BF16))
            dcb = dcb + dm * lam
            nmat = dm * mf
            rs = jnp.broadcast_to(_rowsum(nmat), (CHUNK, CHUNK))
            da = da + _colsum(jnp.where(eye, rs, 0.0)) - _colsum(nmat)
            ds = dh_scr[rows, :]
            dsb = ds.astype(BF16)
            t1 = _dot(dsb, bb)
            xcw = xc * w
            dxc = dxc + w * t1
            dww = _colsum(xcw * t1)
            da_l = _rowsum(dww) + _rowsum(_colsum(ds * hj)) * gam
            da = da - dww + jnp.where(last_lane, da_l, 0.0)
            db_acc = db_acc + _dot_tn(dsb, xcw.astype(BF16))
            dh_scr[rows, :] = gam * ds + dh_in
            dxs_ref[rows, :] = dsk[j:j + 1, :] * dyv + dxc * dtj
            da_scr[j:j + 1, :] = da
            ddt_scr[j:j + 1, :] = _colsum(dxc * xs)
        dcbb = dcb.astype(BF16)
        dc_ref[...] = dc_acc + _dot_nt(bb, dcbb)
        db_ref[...] = db_acc + _dot(cb_, dcbb)
        dda = jnp.dot(da_scr[...], causal.astype(F32), precision=HIGHEST, preferred_element_type=F32)
        ddt = ddt_scr[...] + dda * a_neg
        ddt_raw = ddt * _sigmoid(z)
        ddt_ref[...] = ddt_raw
        ddtb_ref[...] += _rowsum(ddt_raw)
        dalog_ref[...] += _rowsum(dda * dt) * a_neg
        ddsk_ref[...] += _rowsum(dd_scr[...])

    col = pl.BlockSpec((HEADS_PER_GROUP, 1), lambda g, c: (g, 0))
    bc = pl.BlockSpec((D_STATE, CHUNK), lambda g, c: (g, rev(c)))
    xs_spec = pl.BlockSpec((GROUP_ROWS, CHUNK), lambda g, c: (g, rev(c)))
    small = pltpu.VMEM((HEADS_PER_GROUP, CHUNK), F32)
    return pl.pallas_call(
        body,
        name="ssd_bwd",
        grid=(N_SSD_GROUPS, nc),
        in_specs=_ssd_specs(nc, rev) + [
            pl.BlockSpec((1, 1, GROUP_ROWS, D_STATE), lambda g, c: (g, rev(c), 0, 0)),
            xs_spec,
        ],
        out_specs=[xs_spec, bc, bc, pl.BlockSpec((HEADS_PER_GROUP, CHUNK), lambda g, c: (g, rev(c))), col, col, col],
        out_shape=[
            jax.ShapeDtypeStruct((D_INNER, t), F32),
            jax.ShapeDtypeStruct((BC_DIM, t), F32),
            jax.ShapeDtypeStruct((BC_DIM, t), F32),
            jax.ShapeDtypeStruct((N_SSD_HEADS, t), F32),
            jax.ShapeDtypeStruct((N_SSD_HEADS, 1), F32),
            jax.ShapeDtypeStruct((N_SSD_HEADS, 1), F32),
            jax.ShapeDtypeStruct((N_SSD_HEADS, 1), F32),
        ],
        scratch_shapes=[pltpu.VMEM((GROUP_ROWS, D_STATE), F32), small, small, small],
        compiler_params=_cparams("parallel", "arbitrary"),
    )(xbc, xbc, xbc, proj, dtb_col, alog_col, dsk_col, hst, dy)


GN_ROWS = D_INNER // N_SSD_GROUPS


def _gnorm_fwd(y, proj, w_col):
    t = y.shape[1]
    tt = _tile(t, (512, 256, 128))
    z0 = OFF_Z // GN_ROWS

    def body(y_ref, z_ref, w_ref, o_ref):
        zv = z_ref[...]
        u = y_ref[...] * (zv * _sigmoid(zv))
        r = lax.rsqrt(jnp.mean(u * u, axis=0, keepdims=True) + EPS)
        o_ref[...] = (u * r * w_ref[...]).astype(BF16)

    blk = pl.BlockSpec((GN_ROWS, tt), lambda g, i: (g, i))
    return pl.pallas_call(
        body,
        name="gnorm_fwd",
        grid=(N_SSD_GROUPS, t // tt),
        in_specs=[blk, pl.BlockSpec((GN_ROWS, tt), lambda g, i: (z0 + g, i)), pl.BlockSpec((GN_ROWS, 1), lambda g, i: (g, 0))],
        out_specs=blk,
        out_shape=jax.ShapeDtypeStruct((D_INNER, t), BF16),
        compiler_params=_cparams("parallel", "parallel"),
    )(y, proj, w_col)


def _gnorm_bwd(dout, y, proj, w_col):
    t = y.shape[1]
    tt = _tile(t, (512, 256, 128))
    z0 = OFF_Z // GN_ROWS

    def body(do_ref, y_ref, z_ref, w_ref, dy_ref, dz_ref, dw_ref):
        @pl.when(pl.program_id(1) == 0)
        def _():
            dw_ref[...] = jnp.zeros_like(dw_ref)

        zv = z_ref[...]
        yv = y_ref[...]
        sg = _sigmoid(zv)
        sz = zv * sg
        u = yv * sz
        r = lax.rsqrt(jnp.mean(u * u, axis=0, keepdims=True) + EPS)
        xhat = u * r
        dov = do_ref[...]
        dw_ref[...] += _rowsum(dov * xhat)
        dxhat = dov * w_ref[...]
        du = r * (dxhat - xhat * jnp.mean(dxhat * xhat, axis=0, keepdims=True))
        dy_ref[...] = du * sz
        dz_ref[...] = (du * yv * (sg * (1.0 + zv * (1.0 - sg)))).astype(BF16)

    blk = pl.BlockSpec((GN_ROWS, tt), lambda g, i: (g, i))
    col = pl.BlockSpec((GN_ROWS, 1), lambda g, i: (g, 0))
    return pl.pallas_call(
        body,
        name="gnorm_bwd",
        grid=(N_SSD_GROUPS, t // tt),
        in_specs=[blk, blk, pl.BlockSpec((GN_ROWS, tt), lambda g, i: (z0 + g, i)), col],
        out_specs=[blk, blk, col],
        out_shape=[jax.ShapeDtypeStruct((D_INNER, t), F32), jax.ShapeDtypeStruct((D_INNER, t), BF16),
                   jax.ShapeDtypeStruct((D_INNER, 1), F32)],
        compiler_params=_cparams("parallel", "arbitrary"),
    )(dout, y, proj, w_col)


GATE_ROWS = 512


def _gate_specs(t, tt):
    ga0 = OFF_GA // GATE_ROWS
    gs0 = OFF_GS // GATE_ROWS
    nr = D_MODEL // GATE_ROWS
    blk = pl.BlockSpec((GATE_ROWS, tt), lambda r, i: (r, i))
    return blk, [
        pl.BlockSpec((GATE_ROWS, tt), lambda r, i: (ga0 + r, i)),
        pl.BlockSpec((GATE_ROWS, tt), lambda r, i: (gs0 + r, i)),
        pl.BlockSpec((GATE_ROWS, 1), lambda r, i: (r, 0)),
        pl.BlockSpec((GATE_ROWS, 1), lambda r, i: (nr + r, 0)),
        blk, blk,
    ]


def _gate_fwd(proj, b_col, attn, ssd):
    t = proj.shape[1]
    tt = _tile(t, (512, 256, 128))
    blk, specs = _gate_specs(t, tt)

    def body(ga_ref, gs_ref, ba_ref, bs_ref, a_ref, s_ref, o_ref):
        o_ref[...] = (_sigmoid(ga_ref[...] + ba_ref[...]) * a_ref[...]
                      + _sigmoid(gs_ref[...] + bs_ref[...]) * s_ref[...]).astype(BF16)

    return pl.pallas_call(
        body,
        name="gate_fwd",
        grid=(D_MODEL // GATE_ROWS, t // tt),
        in_specs=specs,
        out_specs=blk,
        out_shape=jax.ShapeDtypeStruct((D_MODEL, t), BF16),
        compiler_params=_cparams("parallel", "parallel"),
    )(proj, proj, b_col, b_col, attn, ssd)


def _gate_bwd(proj, b_col, attn, ssd, dmix):
    t = proj.shape[1]
    tt = _tile(t, (512, 256, 128))
    blk, specs = _gate_specs(t, tt)
    nr = D_MODEL // GATE_ROWS

    def body(ga_ref, gs_ref, ba_ref, bs_ref, a_ref, s_ref, dm_ref, da_ref, dso_ref, dga_ref, dgs_ref, dba_ref, dbs_ref):
        @pl.when(pl.program_id(1) == 0)
        def _():
            dba_ref[...] = jnp.zeros_like(dba_ref)
            dbs_ref[...] = jnp.zeros_like(dbs_ref)

        dm = dm_ref[...]
        sa = _sigmoid(ga_ref[...] + ba_ref[...])
        ss = _sigmoid(gs_ref[...] + bs_ref[...])
        da_ref[...] = (dm * sa).astype(BF16)
        dso_ref[...] = (dm * ss).astype(BF16)
        dga = dm * a_ref[...] * sa * (1.0 - sa)
        dgs = dm * s_ref[...] * ss * (1.0 - ss)
        dga_ref[...] = dga.astype(BF16)
        dgs_ref[...] = dgs.astype(BF16)
        dba_ref[...] += _rowsum(dga)
        dbs_ref[...] += _rowsum(dgs)

    col = pl.BlockSpec((GATE_ROWS, 1), lambda r, i: (r, 0))
    act = jax.ShapeDtypeStruct((D_MODEL, t), BF16)
    bias = jax.ShapeDtypeStruct((D_MODEL, 1), F32)
    return pl.pallas_call(
        body,
        name="gate_bwd",
        grid=(nr, t // tt),
        in_specs=specs + [blk],
        out_specs=[blk, blk, blk, blk, col, col],
        out_shape=[act, act, act, act, bias, bias],
        compiler_params=_cparams("parallel", "arbitrary"),
    )(proj, proj, b_col, b_col, attn, ssd, dmix)


FFN_ROWS = 256


def _ffn_conv(u_ref, w_ref, b_ref, half):
    xv = u_ref[half]
    wv = w_ref[half]
    y = b_ref[half] + wv[:, FFN_CONV - 1:FFN_CONV] * xv
    for k in range(FFN_CONV - 1):
        y = y + wv[:, k:k + 1] * _shift_right(xv, FFN_CONV - 1 - k)
    return xv, wv, y


def _ffn_fwd(u0, w_col, b_col):
    t = u0.shape[2]

    def body(u_ref, w_ref, b_ref, o_ref):
        _, _, val = _ffn_conv(u_ref, w_ref, b_ref, 0)
        _, _, gt = _ffn_conv(u_ref, w_ref, b_ref, 1)
        o_ref[...] = (gt * _sigmoid(gt) * val).astype(BF16)

    return pl.pallas_call(
        body,
        name="ffn_fwd",
        grid=(D_FF // FFN_ROWS,),
        in_specs=[
            pl.BlockSpec((2, FFN_ROWS, t), lambda i: (0, i, 0)),
            pl.BlockSpec((2, FFN_ROWS, FFN_CONV), lambda i: (0, i, 0)),
            pl.BlockSpec((2, FFN_ROWS, 1), lambda i: (0, i, 0)),
        ],
        out_specs=pl.BlockSpec((FFN_ROWS, t), lambda i: (i, 0)),
        out_shape=jax.ShapeDtypeStruct((D_FF, t), BF16),
        compiler_params=_cparams("parallel"),
    )(u0, w_col, b_col)


def _ffn_bwd(u0, w_col, b_col, dg):
    t = u0.shape[2]

    def body(u_ref, w_ref, b_ref, dg_ref, du_ref, dwb_ref):
        xval, wval, val = _ffn_conv(u_ref, w_ref, b_ref, 0)
        xgt, wgt, gt = _ffn_conv(u_ref, w_ref, b_ref, 1)
        sg = _sigmoid(gt)
        dgv = dg_ref[...]
        dval = dgv * (gt * sg)
        dgt = dgv * val * (sg * (1.0 + gt * (1.0 - sg)))
        lane = lax.broadcasted_iota(jnp.int32, (FFN_ROWS, 128), 1)
        for half, xv, wv, dy in ((0, xval, wval, dval), (1, xgt, wgt, dgt)):
            dwb = jnp.where(lane == FFN_CONV, _rowsum(dy), 0.0)
            dx = wv[:, FFN_CONV - 1:FFN_CONV] * dy
            dwb = jnp.where(lane == FFN_CONV - 1, _rowsum(dy * xv), dwb)
            for k in range(FFN_CONV - 1):
                j = FFN_CONV - 1 - k
                dx = dx + wv[:, k:k + 1] * _shift_left(dy, j)
                dwb = jnp.where(lane == k, _rowsum(dy * _shift_right(xv, j)), dwb)
            du_ref[half] = dx.astype(BF16)
            dwb_ref[half] = dwb

    return pl.pallas_call(
        body,
        name="ffn_bwd",
        grid=(D_FF // FFN_ROWS,),
        in_specs=[
            pl.BlockSpec((2, FFN_ROWS, t), lambda i: (0, i, 0)),
            pl.BlockSpec((2, FFN_ROWS, FFN_CONV), lambda i: (0, i, 0)),
            pl.BlockSpec((2, FFN_ROWS, 1), lambda i: (0, i, 0)),
            pl.BlockSpec((FFN_ROWS, t), lambda i: (i, 0)),
        ],
        out_specs=[pl.BlockSpec((2, FFN_ROWS, t), lambda i: (0, i, 0)), pl.BlockSpec((2, FFN_ROWS, 128), lambda i: (0, i, 0))],
        out_shape=[jax.ShapeDtypeStruct((2, D_FF, t), BF16), jax.ShapeDtypeStruct((2, D_FF, 128), F32)],
        compiler_params=_cparams("parallel"),
    )(u0, w_col, b_col, dg)


def _adamw_math(w, g, m, v):
    m = ADAM_B1 * m + (1.0 - ADAM_B1) * g
    v = ADAM_B2 * v + (1.0 - ADAM_B2) * (g * g)
    m_hat = m / (1.0 - ADAM_B1 ** ADAM_STEP)
    v_hat = v / (1.0 - ADAM_B2 ** ADAM_STEP)
    delta = -ADAM_LR * (m_hat / (jnp.sqrt(v_hat) + ADAM_EPS) + ADAM_WD * w)
    return delta, m, v


def _adamw_sharded(parts, w, m, v, name):
    r, c = w.shape
    tr = _tile(r, (256, 128, 64, 32, 16))

    def body(p_ref, w_ref, m_ref, v_ref, g_ref, d_ref, nm_ref, nv_ref):
        g = p_ref[0].astype(F32)
        for s in range(1, N_DEV):
            g = g + p_ref[s].astype(F32)
        g_ref[...] = g
        d_ref[...], nm_ref[...], nv_ref[...] = _adamw_math(w_ref[...], g, m_ref[...], v_ref[...])

    blk = pl.BlockSpec((tr, c), lambda i: (i, 0))
    out = jax.ShapeDtypeStruct((r, c), F32)
    return pl.pallas_call(
        body,
        name=name,
        grid=(r // tr,),
        in_specs=[pl.BlockSpec((N_DEV, tr, c), lambda i: (0, i, 0)), blk, blk, blk],
        out_specs=[blk, blk, blk, blk],
        out_shape=[out, out, out, out],
        compiler_params=_cparams("parallel"),
    )(parts, w, m, v)


def _sum_slots(parts):
    _, r, c = parts.shape

    def body(p_ref, o_ref):
        g = p_ref[0]
        for s in range(1, N_DEV):
            g = g + p_ref[s]
        o_ref[...] = g

    return pl.pallas_call(body, name="sum_small_grads", out_shape=jax.ShapeDtypeStruct((r, c), F32))(parts)


def _adamw_small(g, w, m, v):
    def body(g_ref, w_ref, m_ref, v_ref, d_ref, nm_ref, nv_ref):
        d_ref[...], nm_ref[...], nv_ref[...] = _adamw_math(w_ref[...], g_ref[...], m_ref[...], v_ref[...])

    out = jax.ShapeDtypeStruct(g.shape, F32)
    return pl.pallas_call(body, name="adamw_small", out_shape=[out, out, out])(g, w, m, v)


ANY = pl.BlockSpec(memory_space=pl.ANY)
FLIPS = [(k >> 2 & 1, k >> 1 & 1, k & 1) for k in range(1, N_DEV)]


def _place():
    return lax.axis_index("x"), lax.axis_index("y"), lax.axis_index("c")


def _all_gather(arrays, name):
    n = len(arrays)

    def body(*refs):
        ins, outs = refs[:n], refs[n:2 * n]
        send_sems, recv_sems, local_sems = refs[2 * n:]
        x, y, c = _place()
        me = 4 * x + 2 * y + c
        local = [pltpu.make_async_copy(ins[i], outs[i].at[me], local_sems.at[i]) for i in range(n)]
        for cp in local:
            cp.start()
        sends = []
        for k, (fx, fy, fc) in enumerate(FLIPS):
            for i in range(n):
                cp = pltpu.make_async_remote_copy(
                    src_ref=ins[i], dst_ref=outs[i].at[me], send_sem=send_sems.at[i, k], recv_sem=recv_sems.at[i, k],
                    device_id=(x ^ fx, y ^ fy, c ^ fc), device_id_type=MESH)
                cp.start()
                sends.append(cp)
        for k, (fx, fy, fc) in enumerate(FLIPS):
            src = 4 * (x ^ fx) + 2 * (y ^ fy) + (c ^ fc)
            for i in range(n):
                pltpu.make_async_remote_copy(
                    src_ref=ins[i], dst_ref=outs[i].at[src], send_sem=send_sems.at[i, k], recv_sem=recv_sems.at[i, k],
                    device_id=(x ^ fx, y ^ fy, c ^ fc), device_id_type=MESH).wait_recv()
        for cp in sends:
            cp.wait_send()
        for cp in local:
            cp.wait()

    return pl.pallas_call(
        body,
        name=name,
        in_specs=[ANY] * n,
        out_specs=[ANY] * n,
        out_shape=[jax.ShapeDtypeStruct((N_DEV,) + a.shape, a.dtype) for a in arrays],
        scratch_shapes=[pltpu.SemaphoreType.DMA((n, N_DEV - 1)), pltpu.SemaphoreType.DMA((n, N_DEV - 1)),
                        pltpu.SemaphoreType.DMA((n,))],
    )(*arrays)


HBM = pl.BlockSpec(memory_space=pltpu.HBM)
SEM = pl.BlockSpec(memory_space=pltpu.SEMAPHORE)
EFFECT = pltpu.SideEffectType.DATAFLOW_SIDE_EFFECTING


def _peer_copy(gather, src_ref, land_ref, send_sems, recv_sems, k, sending):
    x, y, c = _place()
    fx, fy, fc = FLIPS[k]
    me = 4 * x + 2 * y + c
    peer = 4 * (x ^ fx) + 2 * (y ^ fy) + (c ^ fc)
    return pltpu.make_async_remote_copy(
        src_ref=src_ref if gather else src_ref.at[peer],
        dst_ref=land_ref.at[me if sending else peer],
        send_sem=send_sems.at[k], recv_sem=recv_sems.at[k],
        device_id=(x ^ fx, y ^ fy, c ^ fc), device_id_type=MESH)


def _exchange_start(srcs, gather, name):
    n = len(srcs)
    lands = [lax.empty((N_DEV,) + s.shape if gather else s.shape, s.dtype) for s in srcs]

    def body(*refs):
        src_refs, land_refs = refs[:n], refs[n:2 * n]
        send, recv = refs[2 * n:3 * n], refs[3 * n:4 * n]
        token, local_sems = refs[6 * n], refs[6 * n + 1]
        x, y, c = _place()
        me = 4 * x + 2 * y + c
        local = [pltpu.make_async_copy(src_refs[i] if gather else src_refs[i].at[me], land_refs[i].at[me],
                                       local_sems.at[i]) for i in range(n)]
        for cp in local:
            cp.start()
        for i in range(n):
            for k in range(N_DEV - 1):
                _peer_copy(gather, src_refs[i], land_refs[i], send[i], recv[i], k, True).start()
        for cp in local:
            cp.wait()
        token[...] = jnp.zeros_like(token)

    sem = pltpu.SemaphoreType.DMA((N_DEV - 1,))
    hbm = lambda a: pltpu.HBM(a.shape, a.dtype)
    res = pl.pallas_call(
        body,
        name=name,
        in_specs=[HBM] * (2 * n),
        out_specs=[SEM] * (2 * n) + [HBM] * (2 * n) + [pl.BlockSpec(memory_space=pltpu.VMEM)],
        out_shape=[sem] * (2 * n) + [hbm(s) for s in srcs] + [hbm(a) for a in lands] + [jax.ShapeDtypeStruct((8, 128), F32)],
        input_output_aliases={i: 2 * n + i for i in range(2 * n)},
        scratch_shapes=[pltpu.SemaphoreType.DMA((n,))],
        compiler_params=pltpu.CompilerParams(has_side_effects=EFFECT),
    )(*[pltpu.with_memory_space_constraint(a, pltpu.HBM) for a in list(srcs) + lands])
    return res[:n], res[n:2 * n], res[2 * n:3 * n], res[3 * n:4 * n], res[4 * n]


def _exchange_wait(send_sems, recv_sems, src, land, after, gather, name):
    def body(src_ref, land_ref, send_ref, recv_ref, after_ref, src_out, land_out):
        for k in range(N_DEV - 1):
            cp = _peer_copy(gather, src_ref, land_ref, send_ref, recv_ref, k, False)
            cp.wait_send()
            cp.wait_recv()

    hbm = lambda a: pltpu.HBM(a.shape, a.dtype)
    return pl.pallas_call(
        body,
        name=name,
        in_specs=[HBM, HBM, SEM, SEM, ANY],
        out_specs=[HBM, HBM],
        out_shape=[hbm(src), hbm(land)],
        input_output_aliases={0: 0, 1: 1},
        compiler_params=pltpu.CompilerParams(has_side_effects=EFFECT),
    )(src, land, send_sems, recv_sems, after)[1]


def _col(v):
    return v.reshape(-1, 1).astype(F32)


def _to_internal_rows(w_t):
    pad = jnp.zeros((IN_PAD - IN_DIM, w_t.shape[1]), w_t.dtype)
    return jnp.concatenate([w_t[:ORIG_DT], w_t[ORIG_GA:ORIG_GS], w_t[ORIG_GS:IN_DIM], w_t[ORIG_DT:ORIG_GA], pad], axis=0)


def _from_internal_cols(g):
    return jnp.concatenate([g[:, :OFF_GA], g[:, OFF_DT:OFF_DT + N_SSD_HEADS], g[:, OFF_GA:OFF_GS], g[:, OFF_GS:OFF_DT]], axis=1)


def _local_step(xt, tgt, weight, small, grad_ready):
    t = xt.shape[1]
    n1 = _col(small["norm1_w"])
    n2 = _col(small["norm2_w"])
    nf = _col(small["final_norm_w"])
    bg = _col(small["b_gate"])
    sinks = small["attn_sinks"].reshape(-1).astype(F32)
    cbias = _col(small["ssd_conv_b"])
    dtb = _col(small["dt_bias"])
    alog = _col(small["a_log"])
    dsk = _col(small["d_skip"])
    gnw = _col(small["ssd_norm_w"])
    fb = small["ffn_conv_b"].reshape(2, D_FF, 1)

    xn = _norm_fwd(xt, n1, "norm1_fwd")
    cw = weight("ssd_conv_w", xn).T
    fw = weight("ffn_conv_w", xn).T.reshape(2, D_FF, FFN_CONV)
    w_in_t = weight("w_in", xn)
    proj = _matmul(w_in_t, xn, nt=False, out_dtype=F32, name="mm_in")
    ao, lse = _attn_fwd(proj, sinks)
    w_ao = weight("w_attn_o", ao)
    attn = _matmul(w_ao, ao, nt=False, out_dtype=F32, name="mm_attn_o", tn_a=True)
    xbc = _conv_silu_fwd(proj, cw, cbias)
    y, hst = _ssd_fwd(xbc, proj, dtb, alog, dsk)
    yn = _gnorm_fwd(y, proj, gnw)
    w_so = weight("w_ssd_o", yn)
    ssd = _matmul(w_so, yn, nt=False, out_dtype=F32, name="mm_ssd_o", tn_a=True)
    mix = _gate_fwd(proj, bg, attn, ssd)
    w_out = weight("w_out", mix)
    h1 = _matmul(w_out, mix, nt=False, out_dtype=F32, name="mm_out", add=xt, tn_a=True)
    hn = _norm_fwd(h1, n2, "norm2_fwd")
    w_up_t = weight("w_up", hn)
    u0 = _matmul(w_up_t, hn, nt=False, out_dtype=F32, name="mm_up").reshape(2, D_FF, t)
    gl = _ffn_fwd(u0, fw, fb)
    w_down = weight("w_down", gl)
    h2 = _matmul(w_down, gl, nt=False, out_dtype=F32, name="mm_down", add=h1, tn_a=True)
    dh2, loss, d_nf = _final_norm_loss(h2, tgt, nf)

    g = {}
    dep = grad_ready("w_down", _matmul(gl, dh2, nt=True, out_dtype=BF16, name="mm_d_w_down"))
    dgl = _matmul(w_down, dh2, nt=False, out_dtype=F32, name="mm_d_glu", dep=dep)
    du0, d_fwb = _ffn_bwd(u0, fw, fb, dgl)
    du0 = du0.reshape(2 * D_FF, t)
    dep = grad_ready("w_up", _matmul(hn, du0, nt=True, out_dtype=BF16, name="mm_d_w_up"))
    dhn = _matmul(w_up_t, du0, nt=False, out_dtype=F32, name="mm_d_hn", dep=dep, tn_a=True)
    dh1, d_n2 = _norm_bwd(dhn, h1, n2, dh2, "norm2_bwd")
    dep = grad_ready("w_out", _matmul(mix, dh1, nt=True, out_dtype=BF16, name="mm_d_w_out"))
    dmix = _matmul(w_out, dh1, nt=False, out_dtype=F32, name="mm_d_mix", dep=dep)
    d_attn, d_ssd, d_ga, d_gs, d_ba, d_bs = _gate_bwd(proj, bg, attn, ssd, dmix)
    dep = grad_ready("w_attn_o", _matmul(ao, d_attn, nt=True, out_dtype=BF16, name="mm_d_w_attn_o"))
    dao = _matmul(w_ao, d_attn, nt=False, out_dtype=F32, name="mm_d_ao", dep=dep)
    dq, dk, dv, d_sinks = _attn_bwd(proj, sinks, ao, lse, dao)
    dep = grad_ready("w_ssd_o", _matmul(yn, d_ssd, nt=True, out_dtype=BF16, name="mm_d_w_ssd_o"))
    dyn = _matmul(w_so, d_ssd, nt=False, out_dtype=F32, name="mm_d_yn", dep=dep)
    dy, dz, d_gnw = _gnorm_bwd(dyn, y, proj, gnw)
    dxs, dbm, dcm, ddt, d_alog, d_dsk, d_dtb = _ssd_bwd(xbc, proj, dtb, alog, dsk, hst, dy)
    dx_xs, dwb_xs = _conv_silu_bwd(proj, cw, cbias, dxs, 0, "ssd_conv_bwd_x")
    dx_b, dwb_b = _conv_silu_bwd(proj, cw, cbias, dbm, D_INNER, "ssd_conv_bwd_b")
    dx_c, dwb_c = _conv_silu_bwd(proj, cw, cbias, dcm, D_INNER + BC_DIM, "ssd_conv_bwd_c")
    dwb_conv = jnp.concatenate([dwb_xs, dwb_b, dwb_c], axis=0)
    ddt_rows = jnp.concatenate([ddt.astype(BF16), jnp.zeros((IN_PAD - OFF_DT - N_SSD_HEADS, t), BF16)], axis=0)
    dproj = jnp.concatenate([dq, dk, dv, dz, dx_xs, dx_b, dx_c, d_ga, d_gs, ddt_rows], axis=0)
    dep = grad_ready("w_in", _from_internal_cols(_matmul(xn, dproj, nt=True, out_dtype=BF16, name="mm_d_w_in")))
    dxn = _matmul(w_in_t, dproj, nt=False, out_dtype=F32, name="mm_d_xn", dep=dep, tn_a=True)
    dx, d_n1 = _norm_bwd(dxn, xt, n1, dh1, "norm1_bwd")

    g["norm1_w"] = d_n1
    g["b_gate"] = jnp.concatenate([d_ba, d_bs], axis=0)
    g["attn_sinks"] = d_sinks
    g["ssd_conv_w"] = dwb_conv[:, :SSD_CONV].T
    g["ssd_conv_b"] = dwb_conv[:, SSD_CONV]
    g["dt_bias"] = d_dtb
    g["a_log"] = d_alog
    g["d_skip"] = d_dsk
    g["ssd_norm_w"] = d_gnw
    g["norm2_w"] = d_n2
    d_fwb = d_fwb.reshape(2 * D_FF, 128)
    g["ffn_conv_w"] = d_fwb[:, :FFN_CONV].T
    g["ffn_conv_b"] = d_fwb[:, FFN_CONV]
    g["final_norm_w"] = d_nf
    return loss, dx, g


SHARDED = ("w_in", "w_attn_o", "w_ssd_o", "w_out", "w_up", "w_down")
SMALL = ("norm1_w", "b_gate", "attn_sinks", "ssd_conv_w", "ssd_conv_b", "dt_bias", "a_log", "d_skip", "ssd_norm_w",
         "norm2_w", "ffn_conv_w", "ffn_conv_b", "final_norm_w")
SMALL_SHAPES = {"norm1_w": (1, D_MODEL), "b_gate": (1, 2 * D_MODEL), "attn_sinks": (1, N_Q_HEADS),
                "ssd_conv_w": (1, SSD_CONV, XBC_DIM), "ssd_conv_b": (1, XBC_DIM), "dt_bias": (1, N_SSD_HEADS),
                "a_log": (1, N_SSD_HEADS), "d_skip": (1, N_SSD_HEADS), "ssd_norm_w": (1, D_INNER),
                "norm2_w": (1, D_MODEL), "ffn_conv_w": (1, FFN_CONV, 2 * D_FF), "ffn_conv_b": (1, 2 * D_FF),
                "final_norm_w": (D_MODEL,)}
WEIGHT_ORDER = ("norm1_w", "w_in", "b_gate", "attn_sinks", "w_attn_o", "ssd_conv_w", "ssd_conv_b", "dt_bias", "a_log",
                "d_skip", "ssd_norm_w", "w_ssd_o", "w_out", "norm2_w", "w_up", "ffn_conv_w", "ffn_conv_b", "w_down",
                "final_norm_w")


def _pack(parts):
    flat = jnp.concatenate([p.reshape(-1).astype(F32) for p in parts])
    rows = -(-flat.shape[0] // 1024) * 8
    return jnp.pad(flat, (0, rows * 128 - flat.shape[0])).reshape(rows, 128)


def _unpack(packed, shapes):
    flat = packed.reshape(-1)
    out, pos = [], 0
    for shp in shapes:
        size = 1
        for d in shp:
            size *= d
        out.append(flat[pos:pos + size].reshape(shp))
        pos += size
    return out


def kernel(x, norm1_w, w_in, b_gate, attn_sinks, w_attn_o, ssd_conv_w, ssd_conv_b, dt_bias, a_log, d_skip, ssd_norm_w, w_ssd_o, w_out, norm2_w, w_up, ffn_conv_w, ffn_conv_b, w_down, final_norm_w, loss_target, m_norm1_w, m_w_in, m_b_gate, m_attn_sinks, m_w_attn_o, m_ssd_conv_w, m_ssd_conv_b, m_dt_bias, m_a_log, m_d_skip, m_ssd_norm_w, m_w_ssd_o, m_w_out, m_norm2_w, m_w_up, m_ffn_conv_w, m_ffn_conv_b, m_w_down, m_final_norm_w, v_norm1_w, v_w_in, v_b_gate, v_attn_sinks, v_w_attn_o, v_ssd_conv_w, v_ssd_conv_b, v_dt_bias, v_a_log, v_d_skip, v_ssd_norm_w, v_w_ssd_o, v_w_out, v_norm2_w, v_w_up, v_ffn_conv_w, v_ffn_conv_b, v_w_down, v_final_norm_w):
    w = dict(norm1_w=norm1_w, w_in=w_in, b_gate=b_gate, attn_sinks=attn_sinks, w_attn_o=w_attn_o, ssd_conv_w=ssd_conv_w, ssd_conv_b=ssd_conv_b, dt_bias=dt_bias, a_log=a_log, d_skip=d_skip, ssd_norm_w=ssd_norm_w, w_ssd_o=w_ssd_o, w_out=w_out, norm2_w=norm2_w, w_up=w_up, ffn_conv_w=ffn_conv_w, ffn_conv_b=ffn_conv_b, w_down=w_down, final_norm_w=final_norm_w)
    m = dict(norm1_w=m_norm1_w, w_in=m_w_in, b_gate=m_b_gate, attn_sinks=m_attn_sinks, w_attn_o=m_w_attn_o, ssd_conv_w=m_ssd_conv_w, ssd_conv_b=m_ssd_conv_b, dt_bias=m_dt_bias, a_log=m_a_log, d_skip=m_d_skip, ssd_norm_w=m_ssd_norm_w, w_ssd_o=m_w_ssd_o, w_out=m_w_out, norm2_w=m_norm2_w, w_up=m_w_up, ffn_conv_w=m_ffn_conv_w, ffn_conv_b=m_ffn_conv_b, w_down=m_w_down, final_norm_w=m_final_norm_w)
    v = dict(norm1_w=v_norm1_w, w_in=v_w_in, b_gate=v_b_gate, attn_sinks=v_attn_sinks, w_attn_o=v_w_attn_o, ssd_conv_w=v_ssd_conv_w, ssd_conv_b=v_ssd_conv_b, dt_bias=v_dt_bias, a_log=v_a_log, d_skip=v_d_skip, ssd_norm_w=v_ssd_norm_w, w_ssd_o=v_w_ssd_o, w_out=v_w_out, norm2_w=v_norm2_w, w_up=v_w_up, ffn_conv_w=v_ffn_conv_w, ffn_conv_b=v_ffn_conv_b, w_down=v_w_down, final_norm_w=v_final_norm_w)
    me = 4 * lax.axis_index("x") + 2 * lax.axis_index("y") + lax.axis_index("c")
    conv_cols = XBC_DIM // N_DEV
    ffn_cols = 2 * D_FF // N_DEV

    shards = {"ssd_conv_w": ssd_conv_w[0], "ffn_conv_w": ffn_conv_w[0], "w_in": w_in[0].T.astype(BF16),
              "w_attn_o": w_attn_o[0].astype(BF16), "w_ssd_o": w_ssd_o[0].astype(BF16), "w_out": w_out[0].astype(BF16),
              "w_up": w_up[0].T.astype(BF16), "w_down": w_down[0].astype(BF16)}
    order = list(shards)
    g_send, g_recv, g_src, g_land, _ = _exchange_start(list(shards.values()), True, "gather_start")

    def weight(name, after):
        i = order.index(name)
        land = _exchange_wait(g_send[i], g_recv[i], g_src[i], g_land[i], after, True, "gather_wait_" + name)
        if name == "ssd_conv_w":
            return jnp.transpose(land, (1, 0, 2)).reshape(SSD_CONV, XBC_DIM)
        if name == "ffn_conv_w":
            return jnp.transpose(land, (1, 0, 2)).reshape(FFN_CONV, 2 * D_FF)
        if name == "w_in":
            return _to_internal_rows(land.reshape(IN_DIM, D_MODEL))
        return land.reshape(-1, D_MODEL)

    pending = {}

    def grad_ready(name, grad):
        if name in ("w_in", "w_up"):
            chunks = jnp.transpose(grad.reshape(grad.shape[0], N_DEV, -1), (1, 0, 2))
        else:
            chunks = grad.reshape(N_DEV, -1, D_MODEL)
        send, recv, src, land, token = _exchange_start([chunks], False, "grad_start_" + name)
        pending[name] = (send[0], recv[0], src[0], land[0])
        return token

    small = {k: w[k][0] if k != "final_norm_w" else w[k] for k in SMALL}
    loss, dx, g = _local_step(x[0].T, loss_target[0].T, weight, small, grad_ready)

    packed = _pack([loss] + [g[k] for k in SMALL])
    total = _sum_slots(_all_gather([packed], "gather_small_grads")[0])
    tot = _unpack(total, [(1,)] + [SMALL_SHAPES[k] for k in SMALL])
    loss_sum = tot[0].reshape(())
    gs = dict(zip(SMALL, tot[1:]))
    gs["ssd_conv_w"] = lax.dynamic_slice_in_dim(gs["ssd_conv_w"], me * conv_cols, conv_cols, axis=2)
    gs["ffn_conv_w"] = lax.dynamic_slice_in_dim(gs["ffn_conv_w"], me * ffn_cols, ffn_cols, axis=2)
    upd = _adamw_small(_pack([gs[k] for k in SMALL]), _pack([w[k] for k in SMALL]), _pack([m[k] for k in SMALL]),
                       _pack([v[k] for k in SMALL]))
    shapes = [w[k].shape for k in SMALL]
    d_s, m_s, v_s = (dict(zip(SMALL, _unpack(u, shapes))) for u in upd)
    res = {}
    for k in SMALL:
        res[k] = (gs[k], d_s[k], m_s[k], v_s[k])

    after = upd[0]
    for name in ("w_down", "w_up", "w_out", "w_attn_o", "w_ssd_o", "w_in"):
        parts = _exchange_wait(*pending[name], after, False, "grad_wait_" + name)
        res[name] = _adamw_sharded(parts, w[name][0], m[name][0], v[name][0], "adamw_" + name)
        after = res[name][0]

    grad_x = dx.T[None]
    outs = [loss_sum, grad_x]
    for i in range(4):
        for k in WEIGHT_ORDER:
            r = res[k][i]
            outs.append(r[None] if k in SHARDED else r)
    return tuple(outs)
```

```python
import functools

import jax
import jax.numpy as jnp
from jax import lax
from jax.experimental import pallas as pl
from jax.experimental.pallas import tpu as pltpu

F32 = jnp.float32
BF16 = jnp.bfloat16
HIGHEST = lax.Precision.HIGHEST

D_MODEL = 1024
N_Q_HEADS = 16
N_KV_HEADS = 4
HEAD_DIM = 64
WINDOW = 128
Q_PER_KV = N_Q_HEADS // N_KV_HEADS
Q_DIM = N_Q_HEADS * HEAD_DIM
KV_DIM = N_KV_HEADS * HEAD_DIM
D_INNER = 2048
SSD_HEAD_DIM = 64
N_SSD_HEADS = 32
N_SSD_GROUPS = 4
HEADS_PER_GROUP = N_SSD_HEADS // N_SSD_GROUPS
D_STATE = 128
BC_DIM = N_SSD_GROUPS * D_STATE
XBC_DIM = D_INNER + 2 * BC_DIM
SSD_CONV = 4
CHUNK = 128
D_FF = 2816
FFN_CONV = 3
EPS = 1e-5
NEG = -1e30
IN_DIM = 8736
N_DEV = 8

OFF_Q = 0
OFF_K = OFF_Q + Q_DIM
OFF_V = OFF_K + KV_DIM
OFF_Z = OFF_V + KV_DIM
OFF_X = OFF_Z + D_INNER
OFF_GA = OFF_X + XBC_DIM
OFF_GS = OFF_GA + D_MODEL
OFF_DT = OFF_GS + D_MODEL
IN_PAD = OFF_DT + 128
ORIG_DT = OFF_X + XBC_DIM
ORIG_GA = ORIG_DT + N_SSD_HEADS
ORIG_GS = ORIG_GA + D_MODEL

ADAM_LR = 0.001
ADAM_B1 = 0.9
ADAM_B2 = 0.999
ADAM_EPS = 1e-08
ADAM_WD = 0.01
ADAM_STEP = 10

VMEM_LIMIT = 48 * 1024 * 1024
MESH = pl.DeviceIdType.MESH


def _cparams(*sem):
    return pltpu.CompilerParams(dimension_semantics=sem, vmem_limit_bytes=VMEM_LIMIT)


def _tile(n, prefs):
    for p in prefs:
        if n % p == 0:
            return p
    return n


def _sigmoid(x):
    return 1.0 / (1.0 + jnp.exp(-x))


def _softplus(x):
    return jnp.maximum(x, 0.0) + jnp.log(1.0 + jnp.exp(-jnp.abs(x)))


def _rowsum(x):
    return jnp.sum(x, axis=1, keepdims=True)


def _colsum(x):
    return jnp.sum(x, axis=0, keepdims=True)


def _dot(a, b):
    return jnp.dot(a, b, preferred_element_type=F32)


def _dot_nt(a, b):
    return lax.dot_general(a, b, (((1,), (1,)), ((), ())), preferred_element_type=F32)


def _dot_tn(a, b):
    return lax.dot_general(a, b, (((0,), (0,)), ((), ())), preferred_element_type=F32)


def _shift_right(x, j):
    if j == 0:
        return x
    lane = lax.broadcasted_iota(jnp.int32, x.shape, 1)
    return jnp.where(lane >= j, pltpu.roll(x, j, 1), 0.0)


def _shift_left(x, j):
    if j == 0:
        return x
    n = x.shape[1]
    lane = lax.broadcasted_iota(jnp.int32, x.shape, 1)
    return jnp.where(lane < n - j, pltpu.roll(x, n - j, 1), 0.0)


MATMUL_VMEM_BUDGET = 36 * 1024 * 1024
MATMUL_MAX_TK = 3072


def _matmul_tiles(m, n, k, a_bytes, b_bytes, out_bytes, has_add):
    tm = _tile(m, (512, 384, 256, 128))
    tk = max(d for d in range(128, min(k, MATMUL_MAX_TK) + 1, 128) if k % d == 0)
    for tn in sorted({d for d in range(128, n + 1, 128) if n % d == 0}, reverse=True):
        need = 2 * (tm * tk * a_bytes + tk * tn * b_bytes) + tm * tn * (2 * out_bytes + (4 if k > tk else 0) + (8 if has_add else 0))
        if tn <= 3072 and need <= MATMUL_VMEM_BUDGET:
            return tm, tn, tk
    return tm, 128, tk


def _matmul(a, b, *, nt, out_dtype, name, add=None, dep=None, tn_a=False):
    if tn_a:
        k, m = a.shape
    else:
        m, k = a.shape
    n = b.shape[0] if nt else b.shape[1]
    tm, tn, tk = _matmul_tiles(m, n, k, a.dtype.itemsize, b.dtype.itemsize, jnp.dtype(out_dtype).itemsize, add is not None)
    nk = k // tk

    def body(a_ref, b_ref, *rest):
        if dep is not None:
            rest = rest[1:]
        r_ref = None
        if add is not None:
            r_ref, rest = rest[0], rest[1:]
        o_ref = rest[0]
        av = a_ref[...].astype(BF16)
        bv = b_ref[...].astype(BF16)
        part = _dot_tn(av, bv) if tn_a else _dot_nt(av, bv) if nt else _dot(av, bv)

        def finish(r):
            if add is not None:
                r = r + r_ref[...]
            o_ref[...] = r.astype(out_dtype)

        if nk == 1:
            finish(part)
            return
        acc = rest[1]
        kk = pl.program_id(2)

        @pl.when(kk == 0)
        def _():
            acc[...] = part

        @pl.when((kk > 0) & (kk < nk - 1))
        def _():
            acc[...] += part

        @pl.when(kk == nk - 1)
        def _():
            finish(acc[...] + part)

    in_specs = [
        pl.BlockSpec((tk, tm), lambda i, j, kk: (kk, i)) if tn_a else pl.BlockSpec((tm, tk), lambda i, j, kk: (i, kk)),
        pl.BlockSpec((tn, tk), lambda i, j, kk: (j, kk)) if nt else pl.BlockSpec((tk, tn), lambda i, j, kk: (kk, j)),
    ]
    args = [a, b]
    if dep is not None:
        in_specs.append(pl.BlockSpec(memory_space=pl.ANY))
        args.append(dep)
    if add is not None:
        in_specs.append(pl.BlockSpec((tm, tn), lambda i, j, kk: (i, j)))
        args.append(add)
    return pl.pallas_call(
        body,
        name=name,
        grid=(m // tm, n // tn, nk),
        in_specs=in_specs,
        out_specs=pl.BlockSpec((tm, tn), lambda i, j, kk: (i, j)),
        out_shape=jax.ShapeDtypeStruct((m, n), out_dtype),
        scratch_shapes=[pltpu.VMEM((tm, tn), F32)] if nk > 1 else [],
        compiler_params=_cparams("parallel", "parallel", "arbitrary"),
    )(*args)


def _norm_fwd(x, w_col, name):
    f, t = x.shape
    tt = _tile(t, (512, 256, 128))

    def body(x_ref, w_ref, o_ref):
        xv = x_ref[...]
        r = lax.rsqrt(jnp.mean(xv * xv, axis=0, keepdims=True) + EPS)
        o_ref[...] = (xv * r * w_ref[...]).astype(BF16)

    return pl.pallas_call(
        body,
        name=name,
        grid=(t // tt,),
        in_specs=[pl.BlockSpec((f, tt), lambda i: (0, i)), pl.BlockSpec((f, 1), lambda i: (0, 0))],
        out_specs=pl.BlockSpec((f, tt), lambda i: (0, i)),
        out_shape=jax.ShapeDtypeStruct((f, t), BF16),
        compiler_params=_cparams("parallel"),
    )(x, w_col)


def _norm_bwd(dy, x, w_col, res, name):
    f, t = x.shape
    tt = _tile(t, (512, 256, 128))

    def body(dy_ref, x_ref, w_ref, res_ref, dx_ref, dw_ref):
        @pl.when(pl.program_id(0) == 0)
        def _():
            dw_ref[...] = jnp.zeros_like(dw_ref)

        xv = x_ref[...]
        r = lax.rsqrt(jnp.mean(xv * xv, axis=0, keepdims=True) + EPS)
        xhat = xv * r
        dyv = dy_ref[...]
        dw_ref[...] += _rowsum(dyv * xhat)
        dxhat = dyv * w_ref[...]
        dx_ref[...] = res_ref[...] + r * (dxhat - xhat * jnp.mean(dxhat * xhat, axis=0, keepdims=True))

    blk = pl.BlockSpec((f, tt), lambda i: (0, i))
    col = pl.BlockSpec((f, 1), lambda i: (0, 0))
    return pl.pallas_call(
        body,
        name=name,
        grid=(t // tt,),
        in_specs=[blk, blk, col, blk],
        out_specs=[blk, col],
        out_shape=[jax.ShapeDtypeStruct((f, t), F32), jax.ShapeDtypeStruct((f, 1), F32)],
        compiler_params=_cparams("arbitrary"),
    )(dy, x, w_col, res)


def _final_norm_loss(h, tgt, w_col):
    f, t = h.shape
    tt = _tile(t, (512, 256, 128))

    def body(h_ref, t_ref, w_ref, dh_ref, loss_ref, dw_ref):
        @pl.when(pl.program_id(0) == 0)
        def _():
            dw_ref[...] = jnp.zeros_like(dw_ref)
            loss_ref[...] = jnp.zeros_like(loss_ref)

        xv = h_ref[...]
        r = lax.rsqrt(jnp.mean(xv * xv, axis=0, keepdims=True) + EPS)
        xhat = xv * r
        wv = w_ref[...]
        err = xhat * wv - t_ref[...]
        loss_ref[...] += 0.5 * _rowsum(jnp.mean(err * err, axis=0, keepdims=True))
        dyv = err * (1.0 / f)
        dw_ref[...] += _rowsum(dyv * xhat)
        dxhat = dyv * wv
        dh_ref[...] = r * (dxhat - xhat * jnp.mean(dxhat * xhat, axis=0, keepdims=True))

    blk = pl.BlockSpec((f, tt), lambda i: (0, i))
    col = pl.BlockSpec((f, 1), lambda i: (0, 0))
    one = pl.BlockSpec((1, 1), lambda i: (0, 0))
    return pl.pallas_call(
        body,
        name="final_norm_loss",
        grid=(t // tt,),
        in_specs=[blk, blk, col],
        out_specs=[blk, one, col],
        out_shape=[jax.ShapeDtypeStruct((f, t), F32), jax.ShapeDtypeStruct((1, 1), F32), jax.ShapeDtypeStruct((f, 1), F32)],
        compiler_params=_cparams("arbitrary"),
    )(h, tgt, w_col)


def _attn_mask(n):
    shape = (2 * WINDOW, Q_PER_KV * WINDOW)
    si = lax.broadcasted_iota(jnp.int32, shape, 0)
    qi = lax.broadcasted_iota(jnp.int32, shape, 1) & (WINDOW - 1)
    dist = WINDOW + qi - si
    return (dist >= 0) & (dist < WINDOW) & ((si >= WINDOW) | (n > 0))


def _lane_cat(ref, row0, rows):
    return jnp.concatenate([ref[row0 + i * rows:row0 + (i + 1) * rows, :] for i in range(Q_PER_KV)], axis=1)


def _attn_fwd(proj, sinks):
    t = proj.shape[1]
    nb = t // WINDOW
    scale = HEAD_DIM ** -0.5

    def body(s_ref, q_ref, kc_ref, kp_ref, vc_ref, vp_ref, o_ref, lse_ref):
        n = pl.program_id(0)
        valid = _attn_mask(n)
        for g in range(N_KV_HEADS):
            rows = slice(g * HEAD_DIM, (g + 1) * HEAD_DIM)
            kt = jnp.concatenate([kp_ref[rows, :], kc_ref[rows, :]], axis=1).astype(BF16)
            vt = jnp.concatenate([vp_ref[rows, :], vc_ref[rows, :]], axis=1).astype(BF16)
            qcat = _lane_cat(q_ref, g * Q_PER_KV * HEAD_DIM, HEAD_DIM).astype(BF16)
            s = jnp.where(valid, _dot_tn(kt, qcat) * scale, NEG)
            sink = jnp.concatenate(
                [jnp.full((1, WINDOW), s_ref[g * Q_PER_KV + i], F32) for i in range(Q_PER_KV)], axis=1)
            m = jnp.maximum(jnp.max(s, axis=0, keepdims=True), sink)
            p = jnp.where(valid, jnp.exp(s - m), 0.0)
            denom = _colsum(p) + jnp.exp(sink - m)
            probs = (p / denom).astype(BF16)
            out = _dot(vt, probs)
            lse = m + jnp.log(denom)
            for i in range(Q_PER_KV):
                h = g * Q_PER_KV + i
                o_ref[h * HEAD_DIM:(h + 1) * HEAD_DIM, :] = out[:, i * WINDOW:(i + 1) * WINDOW]
                lse_ref[h:h + 1, :] = lse[:, i * WINDOW:(i + 1) * WINDOW]

    kb = OFF_K // KV_DIM
    vb = OFF_V // KV_DIM
    prev = lambda n: jnp.maximum(n - 1, 0)
    return pl.pallas_call(
        body,
        name="attn_fwd",
        grid=(nb,),
        in_specs=[
            pl.BlockSpec(memory_space=pltpu.SMEM),
            pl.BlockSpec((Q_DIM, WINDOW), lambda n: (0, n)),
            pl.BlockSpec((KV_DIM, WINDOW), lambda n: (kb, n)),
            pl.BlockSpec((KV_DIM, WINDOW), lambda n: (kb, prev(n))),
            pl.BlockSpec((KV_DIM, WINDOW), lambda n: (vb, n)),
            pl.BlockSpec((KV_DIM, WINDOW), lambda n: (vb, prev(n))),
        ],
        out_specs=[pl.BlockSpec((Q_DIM, WINDOW), lambda n: (0, n)), pl.BlockSpec((N_Q_HEADS, WINDOW), lambda n: (0, n))],
        out_shape=[jax.ShapeDtypeStruct((Q_DIM, t), F32), jax.ShapeDtypeStruct((N_Q_HEADS, t), F32)],
        compiler_params=_cparams("parallel"),
    )(sinks, proj, proj, proj, proj, proj)


def _attn_bwd(proj, sinks, out, lse, dout):
    t = proj.shape[1]
    nb = t // WINDOW
    scale = HEAD_DIM ** -0.5

    def body(s_ref, q_ref, kc_ref, kp_ref, vc_ref, vp_ref, o_ref, lse_ref, do_ref,
             dq_ref, dk_ref, dv_ref, ds_ref, dk_carry, dv_carry):
        step = pl.program_id(0)
        n = nb - 1 - step

        @pl.when(step == 0)
        def _():
            dk_carry[...] = jnp.zeros_like(dk_carry)
            dv_carry[...] = jnp.zeros_like(dv_carry)
            ds_ref[...] = jnp.zeros_like(ds_ref)

        valid = _attn_mask(n)
        for g in range(N_KV_HEADS):
            rows = slice(g * HEAD_DIM, (g + 1) * HEAD_DIM)
            q0 = g * Q_PER_KV * HEAD_DIM
            kt = jnp.concatenate([kp_ref[rows, :], kc_ref[rows, :]], axis=1).astype(BF16)
            vt = jnp.concatenate([vp_ref[rows, :], vc_ref[rows, :]], axis=1).astype(BF16)
            qcat = _lane_cat(q_ref, q0, HEAD_DIM).astype(BF16)
            ocat = _lane_cat(o_ref, q0, HEAD_DIM)
            docat = _lane_cat(do_ref, q0, HEAD_DIM)
            dob = docat.astype(BF16)
            lse_cat = jnp.concatenate(
                [lse_ref[g * Q_PER_KV + i:g * Q_PER_KV + i + 1, :] for i in range(Q_PER_KV)], axis=1)
            sink = jnp.concatenate(
                [jnp.full((1, WINDOW), s_ref[g * Q_PER_KV + i], F32) for i in range(Q_PER_KV)], axis=1)
            s = jnp.where(valid, _dot_tn(kt, qcat) * scale, NEG)
            p = jnp.where(valid, jnp.exp(s - lse_cat), 0.0)
            dp = _dot_tn(vt, dob)
            delta = _colsum(docat * ocat)
            dsc = (p * (dp - delta)).astype(BF16)
            dsink_row = -jnp.exp(sink - lse_cat) * delta
            dq = _dot(kt, dsc) * scale
            dk = _dot_nt(qcat, dsc) * scale
            dv = _dot_nt(dob, p.astype(BF16))
            for i in range(Q_PER_KV):
                h = g * Q_PER_KV + i
                dq_ref[h * HEAD_DIM:(h + 1) * HEAD_DIM, :] = dq[:, i * WINDOW:(i + 1) * WINDOW].astype(BF16)
                ds_ref[h:h + 1, :] += _rowsum(dsink_row[:, i * WINDOW:(i + 1) * WINDOW])
            dk_ref[rows, :] = (dk[:, WINDOW:] + dk_carry[rows, :]).astype(BF16)
            dv_ref[rows, :] = (dv[:, WINDOW:] + dv_carry[rows, :]).astype(BF16)
            dk_carry[rows, :] = dk[:, :WINDOW]
            dv_carry[rows, :] = dv[:, :WINDOW]

    kb = OFF_K // KV_DIM
    vb = OFF_V // KV_DIM
    cur = lambda i: nb - 1 - i
    prev = lambda i: jnp.maximum(nb - 2 - i, 0)
    qspec = pl.BlockSpec((Q_DIM, WINDOW), lambda i: (0, cur(i)))
    kvspec = pl.BlockSpec((KV_DIM, WINDOW), lambda i: (0, cur(i)))
    return pl.pallas_call(
        body,
        name="attn_bwd",
        grid=(nb,),
        in_specs=[
            pl.BlockSpec(memory_space=pltpu.SMEM),
            qspec,
            pl.BlockSpec((KV_DIM, WINDOW), lambda i: (kb, cur(i))),
            pl.BlockSpec((KV_DIM, WINDOW), lambda i: (kb, prev(i))),
            pl.BlockSpec((KV_DIM, WINDOW), lambda i: (vb, cur(i))),
            pl.BlockSpec((KV_DIM, WINDOW), lambda i: (vb, prev(i))),
            qspec,
            pl.BlockSpec((N_Q_HEADS, WINDOW), lambda i: (0, cur(i))),
            qspec,
        ],
        out_specs=[qspec, kvspec, kvspec, pl.BlockSpec((N_Q_HEADS, 1), lambda i: (0, 0))],
        out_shape=[
            jax.ShapeDtypeStruct((Q_DIM, t), BF16),
            jax.ShapeDtypeStruct((KV_DIM, t), BF16),
            jax.ShapeDtypeStruct((KV_DIM, t), BF16),
            jax.ShapeDtypeStruct((N_Q_HEADS, 1), F32),
        ],
        scratch_shapes=[pltpu.VMEM((KV_DIM, WINDOW), F32), pltpu.VMEM((KV_DIM, WINDOW), F32)],
        compiler_params=_cparams("arbitrary"),
    )(sinks, proj, proj, proj, proj, proj, out, lse, dout)


CONV_ROWS = 256


def _conv_silu_fwd(proj, w_col, b_col):
    t = proj.shape[1]
    r0 = OFF_X // CONV_ROWS

    def body(x_ref, w_ref, b_ref, o_ref):
        xv = x_ref[...]
        wv = w_ref[...]
        y = b_ref[...] + wv[:, SSD_CONV - 1:SSD_CONV] * xv
        for k in range(SSD_CONV - 1):
            y = y + wv[:, k:k + 1] * _shift_right(xv, SSD_CONV - 1 - k)
        o_ref[...] = y * _sigmoid(y)

    return pl.pallas_call(
        body,
        name="ssd_conv_fwd",
        grid=(XBC_DIM // CONV_ROWS,),
        in_specs=[
            pl.BlockSpec((CONV_ROWS, t), lambda i: (r0 + i, 0)),
            pl.BlockSpec((CONV_ROWS, SSD_CONV), lambda i: (i, 0)),
            pl.BlockSpec((CONV_ROWS, 1), lambda i: (i, 0)),
        ],
        out_specs=pl.BlockSpec((CONV_ROWS, t), lambda i: (i, 0)),
        out_shape=jax.ShapeDtypeStruct((XBC_DIM, t), F32),
        compiler_params=_cparams("parallel"),
    )(proj, w_col, b_col)


def _conv_silu_bwd(proj, w_col, b_col, dout, row0, name):
    t = proj.shape[1]
    nrows = dout.shape[0]
    p0 = (OFF_X + row0) // CONV_ROWS
    c0 = row0 // CONV_ROWS

    def body(x_ref, w_ref, b_ref, do_ref, dx_ref, dwb_ref):
        xv = x_ref[...]
        wv = w_ref[...]
        y = b_ref[...] + wv[:, SSD_CONV - 1:SSD_CONV] * xv
        for k in range(SSD_CONV - 1):
            y = y + wv[:, k:k + 1] * _shift_right(xv, SSD_CONV - 1 - k)
        sg = _sigmoid(y)
        dy = do_ref[...] * (sg * (1.0 + y * (1.0 - sg)))
        lane = lax.broadcasted_iota(jnp.int32, (CONV_ROWS, 128), 1)
        dwb = jnp.where(lane == SSD_CONV, _rowsum(dy), 0.0)
        dx = wv[:, SSD_CONV - 1:SSD_CONV] * dy
        dwb = jnp.where(lane == SSD_CONV - 1, _rowsum(dy * xv), dwb)
        for k in range(SSD_CONV - 1):
            j = SSD_CONV - 1 - k
            dx = dx + wv[:, k:k + 1] * _shift_left(dy, j)
            dwb = jnp.where(lane == k, _rowsum(dy * _shift_right(xv, j)), dwb)
        dx_ref[...] = dx.astype(BF16)
        dwb_ref[...] = dwb

    return pl.pallas_call(
        body,
        name=name,
        grid=(nrows // CONV_ROWS,),
        in_specs=[
            pl.BlockSpec((CONV_ROWS, t), lambda i: (p0 + i, 0)),
            pl.BlockSpec((CONV_ROWS, SSD_CONV), lambda i: (c0 + i, 0)),
            pl.BlockSpec((CONV_ROWS, 1), lambda i: (c0 + i, 0)),
            pl.BlockSpec((CONV_ROWS, t), lambda i: (i, 0)),
        ],
        out_specs=[pl.BlockSpec((CONV_ROWS, t), lambda i: (i, 0)), pl.BlockSpec((CONV_ROWS, 128), lambda i: (i, 0))],
        out_shape=[jax.ShapeDtypeStruct((nrows, t), BF16), jax.ShapeDtypeStruct((nrows, 128), F32)],
        compiler_params=_cparams("parallel"),
    )(proj, w_col, b_col, dout)


GROUP_ROWS = HEADS_PER_GROUP * SSD_HEAD_DIM


def _ssd_specs(nc, order):
    hb = D_INNER // D_STATE
    dtb = OFF_DT // HEADS_PER_GROUP
    col = pl.BlockSpec((HEADS_PER_GROUP, 1), lambda g, c: (g, 0))
    return [
        pl.BlockSpec((GROUP_ROWS, CHUNK), lambda g, c: (g, order(c))),
        pl.BlockSpec((D_STATE, CHUNK), lambda g, c: (hb + g, order(c))),
        pl.BlockSpec((D_STATE, CHUNK), lambda g, c: (hb + N_SSD_GROUPS + g, order(c))),
        pl.BlockSpec((HEADS_PER_GROUP, CHUNK), lambda g, c: (dtb + g, order(c))),
        col, col, col,
    ]


def _ssd_common(dt_ref, dtb_ref, alog_ref):
    z = dt_ref[...] + dtb_ref[...]
    dt = _softplus(z)
    a_neg = -jnp.exp(alog_ref[...])
    d_a = dt * a_neg
    row = lax.broadcasted_iota(jnp.int32, (CHUNK, CHUNK), 0)
    colm = lax.broadcasted_iota(jnp.int32, (CHUNK, CHUNK), 1)
    upper = (row <= colm).astype(F32)
    a_cs = jnp.dot(d_a, upper, precision=HIGHEST, preferred_element_type=F32)
    a_last = _rowsum(d_a)
    return z, dt, a_neg, a_cs, a_last, row >= colm, row == colm


def _decay(a_row, causal):
    a_s = jnp.broadcast_to(a_row, (CHUNK, CHUNK))
    seg = a_s.T - a_s
    return jnp.where(causal, jnp.exp(jnp.where(causal, seg, 0.0)), 0.0)


def _ssd_fwd(xbc, proj, dtb_col, alog_col, dsk_col):
    t = xbc.shape[1]
    nc = t // CHUNK

    def body(xs_ref, b_ref, c_ref, dt_ref, dtb_ref, alog_ref, dsk_ref, y_ref, hst_ref, h_scr):
        @pl.when(pl.program_id(1) == 0)
        def _():
            h_scr[...] = jnp.zeros_like(h_scr)

        _, dt, _, a_cs, a_last, causal, _ = _ssd_common(dt_ref, dtb_ref, alog_ref)
        bb = b_ref[...].astype(BF16)
        cb_ = c_ref[...].astype(BF16)
        cb = _dot_tn(cb_, bb)
        hst_ref[0, 0] = h_scr[...]
        dsk = dsk_ref[...]
        for j in range(HEADS_PER_GROUP):
            rows = slice(j * SSD_HEAD_DIM, (j + 1) * SSD_HEAD_DIM)
            a = a_cs[j:j + 1, :]
            m = (cb * _decay(a, causal)).astype(BF16)
            xs = xs_ref[rows, :]
            xc = xs * dt[j:j + 1, :]
            hj = h_scr[rows, :]
            y = _dot_nt(xc.astype(BF16), m) + _dot(hj.astype(BF16), cb_) * jnp.exp(a) + dsk[j:j + 1, :] * xs
            y_ref[rows, :] = y
            al = a_last[j:j + 1, :]
            w = jnp.exp(al - a)
            h_scr[rows, :] = jnp.exp(al) * hj + _dot_nt((xc * w).astype(BF16), bb)

    return pl.pallas_call(
        body,
        name="ssd_fwd",
        grid=(N_SSD_GROUPS, nc),
        in_specs=_ssd_specs(nc, lambda c: c),
        out_specs=[
            pl.BlockSpec((GROUP_ROWS, CHUNK), lambda g, c: (g, c)),
            pl.BlockSpec((1, 1, GROUP_ROWS, D_STATE), lambda g, c: (g, c, 0, 0)),
        ],
        out_shape=[
            jax.ShapeDtypeStruct((D_INNER, t), F32),
            jax.ShapeDtypeStruct((N_SSD_GROUPS, nc, GROUP_ROWS, D_STATE), F32),
        ],
        scratch_shapes=[pltpu.VMEM((GROUP_ROWS, D_STATE), F32)],
        compiler_params=_cparams("parallel", "arbitrary"),
    )(xbc, xbc, xbc, proj, dtb_col, alog_col, dsk_col)


def _ssd_bwd(xbc, proj, dtb_col, alog_col, dsk_col, hst, dy):
    t = xbc.shape[1]
    nc = t // CHUNK
    rev = lambda c: nc - 1 - c

    def body(xs_ref, b_ref, c_ref, dt_ref, dtb_ref, alog_ref, dsk_ref, hst_ref, dy_ref,
             dxs_ref, db_ref, dc_ref, ddt_ref, dalog_ref, ddsk_ref, ddtb_ref, dh_scr, da_scr, ddt_scr, dd_scr):
        @pl.when(pl.program_id(1) == 0)
        def _():
            dh_scr[...] = jnp.zeros_like(dh_scr)
            dalog_ref[...] = jnp.zeros_like(dalog_ref)
            ddsk_ref[...] = jnp.zeros_like(ddsk_ref)
            ddtb_ref[...] = jnp.zeros_like(ddtb_ref)

        z, dt, a_neg, a_cs, a_last, causal, eye = _ssd_common(dt_ref, dtb_ref, alog_ref)
        bb = b_ref[...].astype(BF16)
        cb_ = c_ref[...].astype(BF16)
        cb = _dot_tn(cb_, bb)
        dsk = dsk_ref[...]
        last_lane = lax.broadcasted_iota(jnp.int32, (1, CHUNK), 1) == CHUNK - 1
        dcb = jnp.zeros((CHUNK, CHUNK), F32)
        dc_acc = jnp.zeros((D_STATE, CHUNK), F32)
        db_acc = jnp.zeros((D_STATE, CHUNK), F32)
        for j in range(HEADS_PER_GROUP):
            rows = slice(j * SSD_HEAD_DIM, (j + 1) * SSD_HEAD_DIM)
            a = a_cs[j:j + 1, :]
            al = a_last[j:j + 1, :]
            lam = _decay(a, causal)
            mf = cb * lam
            xs = xs_ref[rows, :]
            dtj = dt[j:j + 1, :]
            xc = xs * dtj
            w = jnp.exp(al - a)
            e = jnp.exp(a)
            gam = jnp.exp(al)
            hj = hst_ref[0, 0, rows, :]
            hjb = hj.astype(BF16)
            dyv = dy_ref[rows, :]
            dyb = dyv.astype(BF16)
            dd_scr[j:j + 1, :] = _colsum(dyv * xs)
            gb = (dyv * e).astype(BF16)
            dh_in = _dot_nt(gb, cb_)
            dc_acc = dc_acc + _dot_tn(hjb, gb)
            yoff = _dot(hjb, cb_) * e
            da = _colsum(dyv * yoff)
            dm = _dot_tn(dyb, xc.astype(BF16))
            dxc = _dot(dyb, mf.astype(BF16))
            dcb = dcb + dm * lam
            nmat = dm * mf
            rs = jnp.broadcast_to(_rowsum(nmat), (CHUNK, CHUNK))
            da = da + _colsum(jnp.where(eye, rs, 0.0)) - _colsum(nmat)
            ds = dh_scr[rows, :]
            dsb = ds.astype(BF16)
            t1 = _dot(dsb, bb)
            xcw = xc * w
            dxc = dxc + w * t1
            dww = _colsum(xcw * t1)
            da_l = _rowsum(dww) + _rowsum(_colsum(ds * hj)) * gam
            da = da - dww + jnp.where(last_lane, da_l, 0.0)
            db_acc = db_acc + _dot_tn(dsb, xcw.astype(BF16))
            dh_scr[rows, :] = gam * ds + dh_in
            dxs_ref[rows, :] = dsk[j:j + 1, :] * dyv + dxc * dtj
            da_scr[j:j + 1, :] = da
            ddt_scr[j:j + 1, :] = _colsum(dxc * xs)
        dcbb = dcb.astype(BF16)
        dc_ref[...] = dc_acc + _dot_nt(bb, dcbb)
        db_ref[...] = db_acc + _dot(cb_, dcbb)
        dda = jnp.dot(da_scr[...], causal.astype(F32), precision=HIGHEST, preferred_element_type=F32)
        ddt = ddt_scr[...] + dda * a_neg
        ddt_raw = ddt * _sigmoid(z)
        ddt_ref[...] = ddt_raw
        ddtb_ref[...] += _rowsum(ddt_raw)
        dalog_ref[...] += _rowsum(dda * dt) * a_neg
        ddsk_ref[...] += _rowsum(dd_scr[...])

    col = pl.BlockSpec((HEADS_PER_GROUP, 1), lambda g, c: (g, 0))
    bc = pl.BlockSpec((D_STATE, CHUNK), lambda g, c: (g, rev(c)))
    xs_spec = pl.BlockSpec((GROUP_ROWS, CHUNK), lambda g, c: (g, rev(c)))
    small = pltpu.VMEM((HEADS_PER_GROUP, CHUNK), F32)
    return pl.pallas_call(
        body,
        name="ssd_bwd",
        grid=(N_SSD_GROUPS, nc),
        in_specs=_ssd_specs(nc, rev) + [
            pl.BlockSpec((1, 1, GROUP_ROWS, D_STATE), lambda g, c: (g, rev(c), 0, 0)),
            xs_spec,
        ],
        out_specs=[xs_spec, bc, bc, pl.BlockSpec((HEADS_PER_GROUP, CHUNK), lambda g, c: (g, rev(c))), col, col, col],
        out_shape=[
            jax.ShapeDtypeStruct((D_INNER, t), F32),
            jax.ShapeDtypeStruct((BC_DIM, t), F32),
            jax.ShapeDtypeStruct((BC_DIM, t), F32),
            jax.ShapeDtypeStruct((N_SSD_HEADS, t), F32),
            jax.ShapeDtypeStruct((N_SSD_HEADS, 1), F32),
            jax.ShapeDtypeStruct((N_SSD_HEADS, 1), F32),
            jax.ShapeDtypeStruct((N_SSD_HEADS, 1), F32),
        ],
        scratch_shapes=[pltpu.VMEM((GROUP_ROWS, D_STATE), F32), small, small, small],
        compiler_params=_cparams("parallel", "arbitrary"),
    )(xbc, xbc, xbc, proj, dtb_col, alog_col, dsk_col, hst, dy)


GN_ROWS = D_INNER // N_SSD_GROUPS


def _gnorm_fwd(y, proj, w_col):
    t = y.shape[1]
    tt = _tile(t, (512, 256, 128))
    z0 = OFF_Z // GN_ROWS

    def body(y_ref, z_ref, w_ref, o_ref):
        zv = z_ref[...]
        u = y_ref[...] * (zv * _sigmoid(zv))
        r = lax.rsqrt(jnp.mean(u * u, axis=0, keepdims=True) + EPS)
        o_ref[...] = (u * r * w_ref[...]).astype(BF16)

    blk = pl.BlockSpec((GN_ROWS, tt), lambda g, i: (g, i))
    return pl.pallas_call(
        body,
        name="gnorm_fwd",
        grid=(N_SSD_GROUPS, t // tt),
        in_specs=[blk, pl.BlockSpec((GN_ROWS, tt), lambda g, i: (z0 + g, i)), pl.BlockSpec((GN_ROWS, 1), lambda g, i: (g, 0))],
        out_specs=blk,
        out_shape=jax.ShapeDtypeStruct((D_INNER, t), BF16),
        compiler_params=_cparams("parallel", "parallel"),
    )(y, proj, w_col)


def _gnorm_bwd(dout, y, proj, w_col):
    t = y.shape[1]
    tt = _tile(t, (512, 256, 128))
    z0 = OFF_Z // GN_ROWS

    def body(do_ref, y_ref, z_ref, w_ref, dy_ref, dz_ref, dw_ref):
        @pl.when(pl.program_id(1) == 0)
        def _():
            dw_ref[...] = jnp.zeros_like(dw_ref)

        zv = z_ref[...]
        yv = y_ref[...]
        sg = _sigmoid(zv)
        sz = zv * sg
        u = yv * sz
        r = lax.rsqrt(jnp.mean(u * u, axis=0, keepdims=True) + EPS)
        xhat = u * r
        dov = do_ref[...]
        dw_ref[...] += _rowsum(dov * xhat)
        dxhat = dov * w_ref[...]
        du = r * (dxhat - xhat * jnp.mean(dxhat * xhat, axis=0, keepdims=True))
        dy_ref[...] = du * sz
        dz_ref[...] = (du * yv * (sg * (1.0 + zv * (1.0 - sg)))).astype(BF16)

    blk = pl.BlockSpec((GN_ROWS, tt), lambda g, i: (g, i))
    col = pl.BlockSpec((GN_ROWS, 1), lambda g, i: (g, 0))
    return pl.pallas_call(
        body,
        name="gnorm_bwd",
        grid=(N_SSD_GROUPS, t // tt),
        in_specs=[blk, blk, pl.BlockSpec((GN_ROWS, tt), lambda g, i: (z0 + g, i)), col],
        out_specs=[blk, blk, col],
        out_shape=[jax.ShapeDtypeStruct((D_INNER, t), F32), jax.ShapeDtypeStruct((D_INNER, t), BF16),
                   jax.ShapeDtypeStruct((D_INNER, 1), F32)],
        compiler_params=_cparams("parallel", "arbitrary"),
    )(dout, y, proj, w_col)


GATE_ROWS = 512


def _gate_specs(t, tt):
    ga0 = OFF_GA // GATE_ROWS
    gs0 = OFF_GS // GATE_ROWS
    nr = D_MODEL // GATE_ROWS
    blk = pl.BlockSpec((GATE_ROWS, tt), lambda r, i: (r, i))
    return blk, [
        pl.BlockSpec((GATE_ROWS, tt), lambda r, i: (ga0 + r, i)),
        pl.BlockSpec((GATE_ROWS, tt), lambda r, i: (gs0 + r, i)),
        pl.BlockSpec((GATE_ROWS, 1), lambda r, i: (r, 0)),
        pl.BlockSpec((GATE_ROWS, 1), lambda r, i: (nr + r, 0)),
        blk, blk,
    ]


def _gate_fwd(proj, b_col, attn, ssd):
    t = proj.shape[1]
    tt = _tile(t, (512, 256, 128))
    blk, specs = _gate_specs(t, tt)

    def body(ga_ref, gs_ref, ba_ref, bs_ref, a_ref, s_ref, o_ref):
        o_ref[...] = (_sigmoid(ga_ref[...] + ba_ref[...]) * a_ref[...]
                      + _sigmoid(gs_ref[...] + bs_ref[...]) * s_ref[...]).astype(BF16)

    return pl.pallas_call(
        body,
        name="gate_fwd",
        grid=(D_MODEL // GATE_ROWS, t // tt),
        in_specs=specs,
        out_specs=blk,
        out_shape=jax.ShapeDtypeStruct((D_MODEL, t), BF16),
        compiler_params=_cparams("parallel", "parallel"),
    )(proj, proj, b_col, b_col, attn, ssd)


def _gate_bwd(proj, b_col, attn, ssd, dmix):
    t = proj.shape[1]
    tt = _tile(t, (512, 256, 128))
    blk, specs = _gate_specs(t, tt)
    nr = D_MODEL // GATE_ROWS

    def body(ga_ref, gs_ref, ba_ref, bs_ref, a_ref, s_ref, dm_ref, da_ref, dso_ref, dga_ref, dgs_ref, dba_ref, dbs_ref):
        @pl.when(pl.program_id(1) == 0)
        def _():
            dba_ref[...] = jnp.zeros_like(dba_ref)
            dbs_ref[...] = jnp.zeros_like(dbs_ref)

        dm = dm_ref[...]
        sa = _sigmoid(ga_ref[...] + ba_ref[...])
        ss = _sigmoid(gs_ref[...] + bs_ref[...])
        da_ref[...] = (dm * sa).astype(BF16)
        dso_ref[...] = (dm * ss).astype(BF16)
        dga = dm * a_ref[...] * sa * (1.0 - sa)
        dgs = dm * s_ref[...] * ss * (1.0 - ss)
        dga_ref[...] = dga.astype(BF16)
        dgs_ref[...] = dgs.astype(BF16)
        dba_ref[...] += _rowsum(dga)
        dbs_ref[...] += _rowsum(dgs)

    col = pl.BlockSpec((GATE_ROWS, 1), lambda r, i: (r, 0))
    act = jax.ShapeDtypeStruct((D_MODEL, t), BF16)
    bias = jax.ShapeDtypeStruct((D_MODEL, 1), F32)
    return pl.pallas_call(
        body,
        name="gate_bwd",
        grid=(nr, t // tt),
        in_specs=specs + [blk],
        out_specs=[blk, blk, blk, blk, col, col],
        out_shape=[act, act, act, act, bias, bias],
        compiler_params=_cparams("parallel", "arbitrary"),
    )(proj, proj, b_col, b_col, attn, ssd, dmix)


FFN_ROWS = 256


def _ffn_conv(u_ref, w_ref, b_ref, half):
    xv = u_ref[half]
    wv = w_ref[half]
    y = b_ref[half] + wv[:, FFN_CONV - 1:FFN_CONV] * xv
    for k in range(FFN_CONV - 1):
        y = y + wv[:, k:k + 1] * _shift_right(xv, FFN_CONV - 1 - k)
    return xv, wv, y


def _ffn_fwd(u0, w_col, b_col):
    t = u0.shape[2]

    def body(u_ref, w_ref, b_ref, o_ref):
        _, _, val = _ffn_conv(u_ref, w_ref, b_ref, 0)
        _, _, gt = _ffn_conv(u_ref, w_ref, b_ref, 1)
        o_ref[...] = (gt * _sigmoid(gt) * val).astype(BF16)

    return pl.pallas_call(
        body,
        name="ffn_fwd",
        grid=(D_FF // FFN_ROWS,),
        in_specs=[
            pl.BlockSpec((2, FFN_ROWS, t), lambda i: (0, i, 0)),
            pl.BlockSpec((2, FFN_ROWS, FFN_CONV), lambda i: (0, i, 0)),
            pl.BlockSpec((2, FFN_ROWS, 1), lambda i: (0, i, 0)),
        ],
        out_specs=pl.BlockSpec((FFN_ROWS, t), lambda i: (i, 0)),
        out_shape=jax.ShapeDtypeStruct((D_FF, t), BF16),
        compiler_params=_cparams("parallel"),
    )(u0, w_col, b_col)


def _ffn_bwd(u0, w_col, b_col, dg):
    t = u0.shape[2]

    def body(u_ref, w_ref, b_ref, dg_ref, du_ref, dwb_ref):
        xval, wval, val = _ffn_conv(u_ref, w_ref, b_ref, 0)
        xgt, wgt, gt = _ffn_conv(u_ref, w_ref, b_ref, 1)
        sg = _sigmoid(gt)
        dgv = dg_ref[...]
        dval = dgv * (gt * sg)
        dgt = dgv * val * (sg * (1.0 + gt * (1.0 - sg)))
        lane = lax.broadcasted_iota(jnp.int32, (FFN_ROWS, 128), 1)
        for half, xv, wv, dy in ((0, xval, wval, dval), (1, xgt, wgt, dgt)):
            dwb = jnp.where(lane == FFN_CONV, _rowsum(dy), 0.0)
            dx = wv[:, FFN_CONV - 1:FFN_CONV] * dy
            dwb = jnp.where(lane == FFN_CONV - 1, _rowsum(dy * xv), dwb)
            for k in range(FFN_CONV - 1):
                j = FFN_CONV - 1 - k
                dx = dx + wv[:, k:k + 1] * _shift_left(dy, j)
                dwb = jnp.where(lane == k, _rowsum(dy * _shift_right(xv, j)), dwb)
            du_ref[half] = dx.astype(BF16)
            dwb_ref[half] = dwb

    return pl.pallas_call(
        body,
        name="ffn_bwd",
        grid=(D_FF // FFN_ROWS,),
        in_specs=[
            pl.BlockSpec((2, FFN_ROWS, t), lambda i: (0, i, 0)),
            pl.BlockSpec((2, FFN_ROWS, FFN_CONV), lambda i: (0, i, 0)),
            pl.BlockSpec((2, FFN_ROWS, 1), lambda i: (0, i, 0)),
            pl.BlockSpec((FFN_ROWS, t), lambda i: (i, 0)),
        ],
        out_specs=[pl.BlockSpec((2, FFN_ROWS, t), lambda i: (0, i, 0)), pl.BlockSpec((2, FFN_ROWS, 128), lambda i: (0, i, 0))],
        out_shape=[jax.ShapeDtypeStruct((2, D_FF, t), BF16), jax.ShapeDtypeStruct((2, D_FF, 128), F32)],
        compiler_params=_cparams("parallel"),
    )(u0, w_col, b_col, dg)


def _adamw_math(w, g, m, v):
    m = ADAM_B1 * m + (1.0 - ADAM_B1) * g
    v = ADAM_B2 * v + (1.0 - ADAM_B2) * (g * g)
    m_hat = m / (1.0 - ADAM_B1 ** ADAM_STEP)
    v_hat = v / (1.0 - ADAM_B2 ** ADAM_STEP)
    delta = -ADAM_LR * (m_hat / (jnp.sqrt(v_hat) + ADAM_EPS) + ADAM_WD * w)
    return delta, m, v


def _adamw_sharded(parts, w, m, v, name):
    r, c = w.shape
    tc = _tile(c, (256, 128))

    def body(p_ref, w_ref, m_ref, v_ref, g_ref, d_ref, nm_ref, nv_ref):
        g = p_ref[0].astype(F32)
        for s in range(1, N_DEV):
            g = g + p_ref[s].astype(F32)
        g_ref[...] = g
        d_ref[...], nm_ref[...], nv_ref[...] = _adamw_math(w_ref[...], g, m_ref[...], v_ref[...])

    blk = pl.BlockSpec((r, tc), lambda i: (0, i))
    out = jax.ShapeDtypeStruct((r, c), F32)
    return pl.pallas_call(
        body,
        name=name,
        grid=(c // tc,),
        in_specs=[pl.BlockSpec((N_DEV, r, tc), lambda i: (0, 0, i)), blk, blk, blk],
        out_specs=[blk, blk, blk, blk],
        out_shape=[out, out, out, out],
        compiler_params=_cparams("parallel"),
    )(parts, w, m, v)


def _sum_slots(parts):
    _, r, c = parts.shape

    def body(p_ref, o_ref):
        g = p_ref[0]
        for s in range(1, N_DEV):
            g = g + p_ref[s]
        o_ref[...] = g

    return pl.pallas_call(body, name="sum_small_grads", out_shape=jax.ShapeDtypeStruct((r, c), F32))(parts)


def _adamw_small(g, w, m, v):
    def body(g_ref, w_ref, m_ref, v_ref, d_ref, nm_ref, nv_ref):
        d_ref[...], nm_ref[...], nv_ref[...] = _adamw_math(w_ref[...], g_ref[...], m_ref[...], v_ref[...])

    out = jax.ShapeDtypeStruct(g.shape, F32)
    return pl.pallas_call(body, name="adamw_small", out_shape=[out, out, out])(g, w, m, v)


ANY = pl.BlockSpec(memory_space=pl.ANY)
FLIPS = [(k >> 2 & 1, k >> 1 & 1, k & 1) for k in range(1, N_DEV)]


def _place():
    return lax.axis_index("x"), lax.axis_index("y"), lax.axis_index("c")


def _all_gather(arrays, name):
    n = len(arrays)

    def body(*refs):
        ins, outs = refs[:n], refs[n:2 * n]
        send_sems, recv_sems, local_sems = refs[2 * n:]
        x, y, c = _place()
        me = 4 * x + 2 * y + c
        local = [pltpu.make_async_copy(ins[i], outs[i].at[me], local_sems.at[i]) for i in range(n)]
        for cp in local:
            cp.start()
        sends = []
        for k, (fx, fy, fc) in enumerate(FLIPS):
            for i in range(n):
                cp = pltpu.make_async_remote_copy(
                    src_ref=ins[i], dst_ref=outs[i].at[me], send_sem=send_sems.at[i, k], recv_sem=recv_sems.at[i, k],
                    device_id=(x ^ fx, y ^ fy, c ^ fc), device_id_type=MESH)
                cp.start()
                sends.append(cp)
        for k, (fx, fy, fc) in enumerate(FLIPS):
            src = 4 * (x ^ fx) + 2 * (y ^ fy) + (c ^ fc)
            for i in range(n):
                pltpu.make_async_remote_copy(
                    src_ref=ins[i], dst_ref=outs[i].at[src], send_sem=send_sems.at[i, k], recv_sem=recv_sems.at[i, k],
                    device_id=(x ^ fx, y ^ fy, c ^ fc), device_id_type=MESH).wait_recv()
        for cp in sends:
            cp.wait_send()
        for cp in local:
            cp.wait()

    return pl.pallas_call(
        body,
        name=name,
        in_specs=[ANY] * n,
        out_specs=[ANY] * n,
        out_shape=[jax.ShapeDtypeStruct((N_DEV,) + a.shape, a.dtype) for a in arrays],
        scratch_shapes=[pltpu.SemaphoreType.DMA((n, N_DEV - 1)), pltpu.SemaphoreType.DMA((n, N_DEV - 1)),
                        pltpu.SemaphoreType.DMA((n,))],
    )(*arrays)


HBM = pl.BlockSpec(memory_space=pltpu.HBM)
SEM = pl.BlockSpec(memory_space=pltpu.SEMAPHORE)
EFFECT = pltpu.SideEffectType.DATAFLOW_SIDE_EFFECTING


def _peer_copy(gather, src_ref, land_ref, send_sems, recv_sems, k, sending):
    x, y, c = _place()
    fx, fy, fc = FLIPS[k]
    me = 4 * x + 2 * y + c
    peer = 4 * (x ^ fx) + 2 * (y ^ fy) + (c ^ fc)
    return pltpu.make_async_remote_copy(
        src_ref=src_ref if gather else src_ref.at[peer],
        dst_ref=land_ref.at[me if sending else peer],
        send_sem=send_sems.at[k], recv_sem=recv_sems.at[k],
        device_id=(x ^ fx, y ^ fy, c ^ fc), device_id_type=MESH)


def _exchange_start(srcs, gather, name):
    n = len(srcs)
    lands = [lax.empty((N_DEV,) + s.shape if gather else s.shape, s.dtype) for s in srcs]

    def body(*refs):
        src_refs, land_refs = refs[:n], refs[n:2 * n]
        send, recv = refs[2 * n:3 * n], refs[3 * n:4 * n]
        token, local_sems = refs[6 * n], refs[6 * n + 1]
        x, y, c = _place()
        me = 4 * x + 2 * y + c
        local = [pltpu.make_async_copy(src_refs[i] if gather else src_refs[i].at[me], land_refs[i].at[me],
                                       local_sems.at[i]) for i in range(n)]
        for cp in local:
            cp.start()
        for i in range(n):
            for k in range(N_DEV - 1):
                _peer_copy(gather, src_refs[i], land_refs[i], send[i], recv[i], k, True).start()
        for cp in local:
            cp.wait()
        token[...] = jnp.zeros_like(token)

    sem = pltpu.SemaphoreType.DMA((N_DEV - 1,))
    hbm = lambda a: pltpu.HBM(a.shape, a.dtype)
    res = pl.pallas_call(
        body,
        name=name,
        in_specs=[HBM] * (2 * n),
        out_specs=[SEM] * (2 * n) + [HBM] * (2 * n) + [pl.BlockSpec(memory_space=pltpu.VMEM)],
        out_shape=[sem] * (2 * n) + [hbm(s) for s in srcs] + [hbm(a) for a in lands] + [jax.ShapeDtypeStruct((8, 128), F32)],
        input_output_aliases={i: 2 * n + i for i in range(2 * n)},
        scratch_shapes=[pltpu.SemaphoreType.DMA((n,))],
        compiler_params=pltpu.CompilerParams(has_side_effects=EFFECT),
    )(*[pltpu.with_memory_space_constraint(a, pltpu.HBM) for a in list(srcs) + lands])
    return res[:n], res[n:2 * n], res[2 * n:3 * n], res[3 * n:4 * n], res[4 * n]


def _exchange_wait(send_sems, recv_sems, src, land, after, gather, name):
    def body(src_ref, land_ref, send_ref, recv_ref, after_ref, src_out, land_out):
        for k in range(N_DEV - 1):
            cp = _peer_copy(gather, src_ref, land_ref, send_ref, recv_ref, k, False)
            cp.wait_send()
            cp.wait_recv()

    hbm = lambda a: pltpu.HBM(a.shape, a.dtype)
    return pl.pallas_call(
        body,
        name=name,
        in_specs=[HBM, HBM, SEM, SEM, ANY],
        out_specs=[HBM, HBM],
        out_shape=[hbm(src), hbm(land)],
        input_output_aliases={0: 0, 1: 1},
        compiler_params=pltpu.CompilerParams(has_side_effects=EFFECT),
    )(src, land, send_sems, recv_sems, after)[1]


def _col(v):
    return v.reshape(-1, 1).astype(F32)


def _to_internal_rows(w_t):
    pad = jnp.zeros((IN_PAD - IN_DIM, w_t.shape[1]), w_t.dtype)
    return jnp.concatenate([w_t[:ORIG_DT], w_t[ORIG_GA:ORIG_GS], w_t[ORIG_GS:IN_DIM], w_t[ORIG_DT:ORIG_GA], pad], axis=0)


def _from_internal_rows(g):
    return jnp.concatenate([g[:OFF_GA], g[OFF_DT:OFF_DT + N_SSD_HEADS], g[OFF_GA:OFF_GS], g[OFF_GS:OFF_DT]], axis=0)


def _local_step(xt, tgt, weight, small, grad_ready):
    t = xt.shape[1]
    n1 = _col(small["norm1_w"])
    n2 = _col(small["norm2_w"])
    nf = _col(small["final_norm_w"])
    bg = _col(small["b_gate"])
    sinks = small["attn_sinks"].reshape(-1).astype(F32)
    cbias = _col(small["ssd_conv_b"])
    dtb = _col(small["dt_bias"])
    alog = _col(small["a_log"])
    dsk = _col(small["d_skip"])
    gnw = _col(small["ssd_norm_w"])
    fb = small["ffn_conv_b"].reshape(2, D_FF, 1)

    xn = _norm_fwd(xt, n1, "norm1_fwd")
    cw = weight("ssd_conv_w", xn).T
    fw = weight("ffn_conv_w", xn).T.reshape(2, D_FF, FFN_CONV)
    w_in_t = weight("w_in", xn)
    proj = _matmul(w_in_t, xn, nt=False, out_dtype=F32, name="mm_in")
    ao, lse = _attn_fwd(proj, sinks)
    w_ao = weight("w_attn_o", ao)
    attn = _matmul(w_ao, ao, nt=False, out_dtype=F32, name="mm_attn_o", tn_a=True)
    xbc = _conv_silu_fwd(proj, cw, cbias)
    y, hst = _ssd_fwd(xbc, proj, dtb, alog, dsk)
    yn = _gnorm_fwd(y, proj, gnw)
    w_so = weight("w_ssd_o", yn)
    ssd = _matmul(w_so, yn, nt=False, out_dtype=F32, name="mm_ssd_o", tn_a=True)
    mix = _gate_fwd(proj, bg, attn, ssd)
    w_out = weight("w_out", mix)
    h1 = _matmul(w_out, mix, nt=False, out_dtype=F32, name="mm_out", add=xt, tn_a=True)
    hn = _norm_fwd(h1, n2, "norm2_fwd")
    w_up_t = weight("w_up", hn)
    u0 = _matmul(w_up_t, hn, nt=False, out_dtype=F32, name="mm_up").reshape(2, D_FF, t)
    gl = _ffn_fwd(u0, fw, fb)
    w_down = weight("w_down", gl)
    h2 = _matmul(w_down, gl, nt=False, out_dtype=F32, name="mm_down", add=h1, tn_a=True)
    dh2, loss, d_nf = _final_norm_loss(h2, tgt, nf)

    g = {}
    dep = grad_ready("w_down", _matmul(gl, dh2, nt=True, out_dtype=BF16, name="mm_d_w_down"))
    dgl = _matmul(w_down, dh2, nt=False, out_dtype=F32, name="mm_d_glu", dep=dep)
    du0, d_fwb = _ffn_bwd(u0, fw, fb, dgl)
    du0 = du0.reshape(2 * D_FF, t)
    dep = grad_ready("w_up", _matmul(du0, hn, nt=True, out_dtype=BF16, name="mm_d_w_up"))
    dhn = _matmul(w_up_t, du0, nt=False, out_dtype=F32, name="mm_d_hn", dep=dep, tn_a=True)
    dh1, d_n2 = _norm_bwd(dhn, h1, n2, dh2, "norm2_bwd")
    dep = grad_ready("w_out", _matmul(mix, dh1, nt=True, out_dtype=BF16, name="mm_d_w_out"))
    dmix = _matmul(w_out, dh1, nt=False, out_dtype=F32, name="mm_d_mix", dep=dep)
    d_attn, d_ssd, d_ga, d_gs, d_ba, d_bs = _gate_bwd(proj, bg, attn, ssd, dmix)
    dep = grad_ready("w_attn_o", _matmul(ao, d_attn, nt=True, out_dtype=BF16, name="mm_d_w_attn_o"))
    dao = _matmul(w_ao, d_attn, nt=False, out_dtype=F32, name="mm_d_ao", dep=dep)
    dq, dk, dv, d_sinks = _attn_bwd(proj, sinks, ao, lse, dao)
    dep = grad_ready("w_ssd_o", _matmul(yn, d_ssd, nt=True, out_dtype=BF16, name="mm_d_w_ssd_o"))
    dyn = _matmul(w_so, d_ssd, nt=False, out_dtype=F32, name="mm_d_yn", dep=dep)
    dy, dz, d_gnw = _gnorm_bwd(dyn, y, proj, gnw)
    dxs, dbm, dcm, ddt, d_alog, d_dsk, d_dtb = _ssd_bwd(xbc, proj, dtb, alog, dsk, hst, dy)
    dx_xs, dwb_xs = _conv_silu_bwd(proj, cw, cbias, dxs, 0, "ssd_conv_bwd_x")
    dx_b, dwb_b = _conv_silu_bwd(proj, cw, cbias, dbm, D_INNER, "ssd_conv_bwd_b")
    dx_c, dwb_c = _conv_silu_bwd(proj, cw, cbias, dcm, D_INNER + BC_DIM, "ssd_conv_bwd_c")
    dwb_conv = jnp.concatenate([dwb_xs, dwb_b, dwb_c], axis=0)
    ddt_rows = jnp.concatenate([ddt.astype(BF16), jnp.zeros((IN_PAD - OFF_DT - N_SSD_HEADS, t), BF16)], axis=0)
    dproj = jnp.concatenate([dq, dk, dv, dz, dx_xs, dx_b, dx_c, d_ga, d_gs, ddt_rows], axis=0)
    dep = grad_ready("w_in", _from_internal_rows(_matmul(dproj, xn, nt=True, out_dtype=BF16, name="mm_d_w_in")))
    dxn = _matmul(w_in_t, dproj, nt=False, out_dtype=F32, name="mm_d_xn", dep=dep, tn_a=True)
    dx, d_n1 = _norm_bwd(dxn, xt, n1, dh1, "norm1_bwd")

    g["norm1_w"] = d_n1
    g["b_gate"] = jnp.concatenate([d_ba, d_bs], axis=0)
    g["attn_sinks"] = d_sinks
    g["ssd_conv_w"] = dwb_conv[:, :SSD_CONV].T
    g["ssd_conv_b"] = dwb_conv[:, SSD_CONV]
    g["dt_bias"] = d_dtb
    g["a_log"] = d_alog
    g["d_skip"] = d_dsk
    g["ssd_norm_w"] = d_gnw
    g["norm2_w"] = d_n2
    d_fwb = d_fwb.reshape(2 * D_FF, 128)
    g["ffn_conv_w"] = d_fwb[:, :FFN_CONV].T
    g["ffn_conv_b"] = d_fwb[:, FFN_CONV]
    g["final_norm_w"] = d_nf
    return loss, dx, g


SHARDED = ("w_in", "w_attn_o", "w_ssd_o", "w_out", "w_up", "w_down")
SMALL = ("norm1_w", "b_gate", "attn_sinks", "ssd_conv_w", "ssd_conv_b", "dt_bias", "a_log", "d_skip", "ssd_norm_w",
         "norm2_w", "ffn_conv_w", "ffn_conv_b", "final_norm_w")
SMALL_SHAPES = {"norm1_w": (1, D_MODEL), "b_gate": (1, 2 * D_MODEL), "attn_sinks": (1, N_Q_HEADS),
                "ssd_conv_w": (1, SSD_CONV, XBC_DIM), "ssd_conv_b": (1, XBC_DIM), "dt_bias": (1, N_SSD_HEADS),
                "a_log": (1, N_SSD_HEADS), "d_skip": (1, N_SSD_HEADS), "ssd_norm_w": (1, D_INNER),
                "norm2_w": (1, D_MODEL), "ffn_conv_w": (1, FFN_CONV, 2 * D_FF), "ffn_conv_b": (1, 2 * D_FF),
                "final_norm_w": (D_MODEL,)}
WEIGHT_ORDER = ("norm1_w", "w_in", "b_gate", "attn_sinks", "w_attn_o", "ssd_conv_w", "ssd_conv_b", "dt_bias", "a_log",
                "d_skip", "ssd_norm_w", "w_ssd_o", "w_out", "norm2_w", "w_up", "ffn_conv_w", "ffn_conv_b", "w_down",
                "final_norm_w")


def _pack(parts):
    flat = jnp.concatenate([p.reshape(-1).astype(F32) for p in parts])
    rows = -(-flat.shape[0] // 1024) * 8
    return jnp.pad(flat, (0, rows * 128 - flat.shape[0])).reshape(rows, 128)


def _unpack(packed, shapes):
    flat = packed.reshape(-1)
    out, pos = [], 0
    for shp in shapes:
        size = 1
        for d in shp:
            size *= d
        out.append(flat[pos:pos + size].reshape(shp))
        pos += size
    return out


def kernel(x, norm1_w, w_in, b_gate, attn_sinks, w_attn_o, ssd_conv_w, ssd_conv_b, dt_bias, a_log, d_skip, ssd_norm_w, w_ssd_o, w_out, norm2_w, w_up, ffn_conv_w, ffn_conv_b, w_down, final_norm_w, loss_target, m_norm1_w, m_w_in, m_b_gate, m_attn_sinks, m_w_attn_o, m_ssd_conv_w, m_ssd_conv_b, m_dt_bias, m_a_log, m_d_skip, m_ssd_norm_w, m_w_ssd_o, m_w_out, m_norm2_w, m_w_up, m_ffn_conv_w, m_ffn_conv_b, m_w_down, m_final_norm_w, v_norm1_w, v_w_in, v_b_gate, v_attn_sinks, v_w_attn_o, v_ssd_conv_w, v_ssd_conv_b, v_dt_bias, v_a_log, v_d_skip, v_ssd_norm_w, v_w_ssd_o, v_w_out, v_norm2_w, v_w_up, v_ffn_conv_w, v_ffn_conv_b, v_w_down, v_final_norm_w):
    w = dict(norm1_w=norm1_w, w_in=w_in, b_gate=b_gate, attn_sinks=attn_sinks, w_attn_o=w_attn_o, ssd_conv_w=ssd_conv_w, ssd_conv_b=ssd_conv_b, dt_bias=dt_bias, a_log=a_log, d_skip=d_skip, ssd_norm_w=ssd_norm_w, w_ssd_o=w_ssd_o, w_out=w_out, norm2_w=norm2_w, w_up=w_up, ffn_conv_w=ffn_conv_w, ffn_conv_b=ffn_conv_b, w_down=w_down, final_norm_w=final_norm_w)
    m = dict(norm1_w=m_norm1_w, w_in=m_w_in, b_gate=m_b_gate, attn_sinks=m_attn_sinks, w_attn_o=m_w_attn_o, ssd_conv_w=m_ssd_conv_w, ssd_conv_b=m_ssd_conv_b, dt_bias=m_dt_bias, a_log=m_a_log, d_skip=m_d_skip, ssd_norm_w=m_ssd_norm_w, w_ssd_o=m_w_ssd_o, w_out=m_w_out, norm2_w=m_norm2_w, w_up=m_w_up, ffn_conv_w=m_ffn_conv_w, ffn_conv_b=m_ffn_conv_b, w_down=m_w_down, final_norm_w=m_final_norm_w)
    v = dict(norm1_w=v_norm1_w, w_in=v_w_in, b_gate=v_b_gate, attn_sinks=v_attn_sinks, w_attn_o=v_w_attn_o, ssd_conv_w=v_ssd_conv_w, ssd_conv_b=v_ssd_conv_b, dt_bias=v_dt_bias, a_log=v_a_log, d_skip=v_d_skip, ssd_norm_w=v_ssd_norm_w, w_ssd_o=v_w_ssd_o, w_out=v_w_out, norm2_w=v_norm2_w, w_up=v_w_up, ffn_conv_w=v_ffn_conv_w, ffn_conv_b=v_ffn_conv_b, w_down=v_w_down, final_norm_w=v_final_norm_w)
    me = 4 * lax.axis_index("x") + 2 * lax.axis_index("y") + lax.axis_index("c")
    conv_cols = XBC_DIM // N_DEV
    ffn_cols = 2 * D_FF // N_DEV

    shards = {"ssd_conv_w": ssd_conv_w[0], "ffn_conv_w": ffn_conv_w[0], "w_in": w_in[0].T.astype(BF16),
              "w_attn_o": w_attn_o[0].astype(BF16), "w_ssd_o": w_ssd_o[0].astype(BF16), "w_out": w_out[0].astype(BF16),
              "w_up": w_up[0].T.astype(BF16), "w_down": w_down[0].astype(BF16)}
    order = list(shards)
    g_send, g_recv, g_src, g_land, _ = _exchange_start(list(shards.values()), True, "gather_start")

    def weight(name, after):
        i = order.index(name)
        land = _exchange_wait(g_send[i], g_recv[i], g_src[i], g_land[i], after, True, "gather_wait_" + name)
        if name == "ssd_conv_w":
            return jnp.transpose(land, (1, 0, 2)).reshape(SSD_CONV, XBC_DIM)
        if name == "ffn_conv_w":
            return jnp.transpose(land, (1, 0, 2)).reshape(FFN_CONV, 2 * D_FF)
        if name == "w_in":
            return _to_internal_rows(land.reshape(IN_DIM, D_MODEL))
        return land.reshape(-1, D_MODEL)

    pending = {}

    def grad_ready(name, grad):
        chunks = grad.reshape(N_DEV, -1, D_MODEL)
        send, recv, src, land, token = _exchange_start([chunks], False, "grad_start_" + name)
        pending[name] = (send[0], recv[0], src[0], land[0])
        return token

    small = {k: w[k][0] if k != "final_norm_w" else w[k] for k in SMALL}
    loss, dx, g = _local_step(x[0].T, loss_target[0].T, weight, small, grad_ready)

    packed = _pack([loss] + [g[k] for k in SMALL])
    total = _sum_slots(_all_gather([packed], "gather_small_grads")[0])
    tot = _unpack(total, [(1,)] + [SMALL_SHAPES[k] for k in SMALL])
    loss_sum = tot[0].reshape(())
    gs = dict(zip(SMALL, tot[1:]))
    gs["ssd_conv_w"] = lax.dynamic_slice_in_dim(gs["ssd_conv_w"], me * conv_cols, conv_cols, axis=2)
    gs["ffn_conv_w"] = lax.dynamic_slice_in_dim(gs["ffn_conv_w"], me * ffn_cols, ffn_cols, axis=2)
    upd = _adamw_small(_pack([gs[k] for k in SMALL]), _pack([w[k] for k in SMALL]), _pack([m[k] for k in SMALL]),
                       _pack([v[k] for k in SMALL]))
    shapes = [w[k].shape for k in SMALL]
    d_s, m_s, v_s = (dict(zip(SMALL, _unpack(u, shapes))) for u in upd)
    res = {}
    for k in SMALL:
        res[k] = (gs[k], d_s[k], m_s[k], v_s[k])

    after = upd[0]
    for name in ("w_down", "w_up", "w_out", "w_attn_o", "w_ssd_o", "w_in"):
        parts = _exchange_wait(*pending[name], after, False, "grad_wait_" + name)
        view = (lambda a: a[0].T) if name in ("w_in", "w_up") else (lambda a: a[0])
        res[name] = _adamw_sharded(parts, view(w[name]), view(m[name]), view(v[name]), "adamw_" + name)
        after = res[name][0]
        if name in ("w_in", "w_up"):
            res[name] = [r.T for r in res[name]]

    grad_x = dx.T[None]
    outs = [loss_sum, grad_x]
    for i in range(4):
        for k in WEIGHT_ORDER:
            r = res[k][i]
            outs.append(r[None] if k in SHARDED else r)
    return tuple(outs)
```

```python
import functools

import jax
import jax.numpy as jnp
from jax import lax
from jax.experimental import pallas as pl
from jax.experimental.pallas import tpu as pltpu

F32 = jnp.float32
BF16 = jnp.bfloat16
HIGHEST = lax.Precision.HIGHEST

D_MODEL = 1024
N_Q_HEADS = 16
N_KV_HEADS = 4
HEAD_DIM = 64
WINDOW = 128
Q_PER_KV = N_Q_HEADS // N_KV_HEADS
Q_DIM = N_Q_HEADS * HEAD_DIM
KV_DIM = N_KV_HEADS * HEAD_DIM
D_INNER = 2048
SSD_HEAD_DIM = 64
N_SSD_HEADS = 32
N_SSD_GROUPS = 4
HEADS_PER_GROUP = N_SSD_HEADS // N_SSD_GROUPS
D_STATE = 128
BC_DIM = N_SSD_GROUPS * D_STATE
XBC_DIM = D_INNER + 2 * BC_DIM
SSD_CONV = 4
CHUNK = 128
D_FF = 2816
FFN_CONV = 3
EPS = 1e-5
NEG = -1e30
IN_DIM = 8736
N_DEV = 8

OFF_Q = 0
OFF_K = OFF_Q + Q_DIM
OFF_V = OFF_K + KV_DIM
OFF_Z = OFF_V + KV_DIM
OFF_X = OFF_Z + D_INNER
OFF_GA = OFF_X + XBC_DIM
OFF_GS = OFF_GA + D_MODEL
OFF_DT = OFF_GS + D_MODEL
IN_PAD = OFF_DT + 128
ORIG_DT = OFF_X + XBC_DIM
ORIG_GA = ORIG_DT + N_SSD_HEADS
ORIG_GS = ORIG_GA + D_MODEL

ADAM_LR = 0.001
ADAM_B1 = 0.9
ADAM_B2 = 0.999
ADAM_EPS = 1e-08
ADAM_WD = 0.01
ADAM_STEP = 10

VMEM_LIMIT = 48 * 1024 * 1024
MESH = pl.DeviceIdType.MESH


def _cparams(*sem):
    return pltpu.CompilerParams(dimension_semantics=sem, vmem_limit_bytes=VMEM_LIMIT)


def _tile(n, prefs):
    for p in prefs:
        if n % p == 0:
            return p
    return n


def _sigmoid(x):
    return 1.0 / (1.0 + jnp.exp(-x))


def _softplus(x):
    return jnp.maximum(x, 0.0) + jnp.log(1.0 + jnp.exp(-jnp.abs(x)))


def _rowsum(x):
    return jnp.sum(x, axis=1, keepdims=True)


def _colsum(x):
    return jnp.sum(x, axis=0, keepdims=True)


def _dot(a, b):
    return jnp.dot(a, b, preferred_element_type=F32)


def _dot_nt(a, b):
    return lax.dot_general(a, b, (((1,), (1,)), ((), ())), preferred_element_type=F32)


def _dot_tn(a, b):
    return lax.dot_general(a, b, (((0,), (0,)), ((), ())), preferred_element_type=F32)


def _shift_right(x, j):
    if j == 0:
        return x
    lane = lax.broadcasted_iota(jnp.int32, x.shape, 1)
    return jnp.where(lane >= j, pltpu.roll(x, j, 1), 0.0)


def _shift_left(x, j):
    if j == 0:
        return x
    n = x.shape[1]
    lane = lax.broadcasted_iota(jnp.int32, x.shape, 1)
    return jnp.where(lane < n - j, pltpu.roll(x, n - j, 1), 0.0)


MATMUL_VMEM_BUDGET = 36 * 1024 * 1024
MATMUL_MAX_TK = 3072


def _matmul_tiles(m, n, k, a_bytes, b_bytes, out_bytes, has_add):
    tm = _tile(m, (512, 384, 256, 128))
    tk = max(d for d in range(128, min(k, MATMUL_MAX_TK) + 1, 128) if k % d == 0)
    for tn in sorted({d for d in range(128, n + 1, 128) if n % d == 0}, reverse=True):
        need = 2 * (tm * tk * a_bytes + tk * tn * b_bytes) + tm * tn * (2 * out_bytes + (4 if k > tk else 0) + (8 if has_add else 0))
        if tn <= 3072 and need <= MATMUL_VMEM_BUDGET:
            return tm, tn, tk
    return tm, 128, tk


def _matmul(a, b, *, nt, out_dtype, name, add=None, tn_a=False, send=None):
    if tn_a:
        k, m = a.shape
    else:
        m, k = a.shape
    n = b.shape[0] if nt else b.shape[1]
    tm, tn, tk = _matmul_tiles(m, n, k, a.dtype.itemsize, b.dtype.itemsize, jnp.dtype(out_dtype).itemsize, add is not None)
    nk = k // tk
    grid = (m // tm, n // tn, nk)

    def body(a_ref, b_ref, *rest):
        r_ref = None
        if add is not None:
            r_ref, rest = rest[0], rest[1:]
        if send is not None:
            src_ref, land_ref, rest = rest[0], rest[1], rest[2:]
            send_sems, recv_sems, local_sem = rest[1], rest[2], rest[-1]
            rest = (rest[0],) + rest[5:-1]
            x, y, c = _place()
            me = 4 * x + 2 * y + c
            local = pltpu.make_async_copy(src_ref.at[me], land_ref.at[me], local_sem)
            step = (pl.program_id(0) * grid[1] + pl.program_id(1)) * grid[2] + pl.program_id(2)

            @pl.when(step == 0)
            def _():
                local.start()
                for peer in range(N_DEV - 1):
                    _peer_copy(False, src_ref, land_ref, send_sems, recv_sems, peer, True).start()

            @pl.when(step == grid[0] * grid[1] * grid[2] - 1)
            def _():
                local.wait()

        o_ref = rest[0]
        av = a_ref[...].astype(BF16)
        bv = b_ref[...].astype(BF16)
        part = _dot_tn(av, bv) if tn_a else _dot_nt(av, bv) if nt else _dot(av, bv)

        def finish(r):
            if add is not None:
                r = r + r_ref[...]
            o_ref[...] = r.astype(out_dtype)

        if nk == 1:
            finish(part)
            return
        acc = rest[1]
        kk = pl.program_id(2)

        @pl.when(kk == 0)
        def _():
            acc[...] = part

        @pl.when((kk > 0) & (kk < nk - 1))
        def _():
            acc[...] += part

        @pl.when(kk == nk - 1)
        def _():
            finish(acc[...] + part)

    in_specs = [
        pl.BlockSpec((tk, tm), lambda i, j, kk: (kk, i)) if tn_a else pl.BlockSpec((tm, tk), lambda i, j, kk: (i, kk)),
        pl.BlockSpec((tn, tk), lambda i, j, kk: (j, kk)) if nt else pl.BlockSpec((tk, tn), lambda i, j, kk: (kk, j)),
    ]
    args = [a, b]
    if add is not None:
        in_specs.append(pl.BlockSpec((tm, tn), lambda i, j, kk: (i, j)))
        args.append(add)
    out_specs = [pl.BlockSpec((tm, tn), lambda i, j, kk: (i, j))]
    out_shape = [jax.ShapeDtypeStruct((m, n), out_dtype)]
    scratch = [pltpu.VMEM((tm, tn), F32)] if nk > 1 else []
    if send is None:
        return pl.pallas_call(
            body, name=name, grid=grid, in_specs=in_specs, out_specs=out_specs[0], out_shape=out_shape[0],
            scratch_shapes=scratch, compiler_params=_cparams("parallel", "parallel", "arbitrary"),
        )(*args)
    sem = pltpu.SemaphoreType.DMA((N_DEV - 1,))
    hbm = pltpu.HBM(send.shape, send.dtype)
    first = len(args)
    res = pl.pallas_call(
        body, name=name, grid=grid,
        in_specs=in_specs + [HBM, HBM],
        out_specs=out_specs + [SEM, SEM, HBM, HBM],
        out_shape=out_shape + [sem, sem, hbm, hbm],
        input_output_aliases={first: 3, first + 1: 4},
        scratch_shapes=scratch + [pltpu.SemaphoreType.DMA(())],
        compiler_params=pltpu.CompilerParams(dimension_semantics=("arbitrary",) * 3, vmem_limit_bytes=VMEM_LIMIT,
                                             has_side_effects=EFFECT),
    )(*args, pltpu.with_memory_space_constraint(send, pltpu.HBM),
      pltpu.with_memory_space_constraint(lax.empty(send.shape, send.dtype), pltpu.HBM))
    return res[0], tuple(res[1:])


def _norm_fwd(x, w_col, name):
    f, t = x.shape
    tt = _tile(t, (512, 256, 128))

    def body(x_ref, w_ref, o_ref):
        xv = x_ref[...]
        r = lax.rsqrt(jnp.mean(xv * xv, axis=0, keepdims=True) + EPS)
        o_ref[...] = (xv * r * w_ref[...]).astype(BF16)

    return pl.pallas_call(
        body,
        name=name,
        grid=(t // tt,),
        in_specs=[pl.BlockSpec((f, tt), lambda i: (0, i)), pl.BlockSpec((f, 1), lambda i: (0, 0))],
        out_specs=pl.BlockSpec((f, tt), lambda i: (0, i)),
        out_shape=jax.ShapeDtypeStruct((f, t), BF16),
        compiler_params=_cparams("parallel"),
    )(x, w_col)


def _norm_bwd(dy, x, w_col, res, name):
    f, t = x.shape
    tt = _tile(t, (512, 256, 128))

    def body(dy_ref, x_ref, w_ref, res_ref, dx_ref, dw_ref):
        @pl.when(pl.program_id(0) == 0)
        def _():
            dw_ref[...] = jnp.zeros_like(dw_ref)

        xv = x_ref[...]
        r = lax.rsqrt(jnp.mean(xv * xv, axis=0, keepdims=True) + EPS)
        xhat = xv * r
        dyv = dy_ref[...]
        dw_ref[...] += _rowsum(dyv * xhat)
        dxhat = dyv * w_ref[...]
        dx_ref[...] = res_ref[...] + r * (dxhat - xhat * jnp.mean(dxhat * xhat, axis=0, keepdims=True))

    blk = pl.BlockSpec((f, tt), lambda i: (0, i))
    col = pl.BlockSpec((f, 1), lambda i: (0, 0))
    return pl.pallas_call(
        body,
        name=name,
        grid=(t // tt,),
        in_specs=[blk, blk, col, blk],
        out_specs=[blk, col],
        out_shape=[jax.ShapeDtypeStruct((f, t), F32), jax.ShapeDtypeStruct((f, 1), F32)],
        compiler_params=_cparams("arbitrary"),
    )(dy, x, w_col, res)


def _final_norm_loss(h, tgt, w_col):
    f, t = h.shape
    tt = _tile(t, (512, 256, 128))

    def body(h_ref, t_ref, w_ref, dh_ref, loss_ref, dw_ref):
        @pl.when(pl.program_id(0) == 0)
        def _():
            dw_ref[...] = jnp.zeros_like(dw_ref)
            loss_ref[...] = jnp.zeros_like(loss_ref)

        xv = h_ref[...]
        r = lax.rsqrt(jnp.mean(xv * xv, axis=0, keepdims=True) + EPS)
        xhat = xv * r
        wv = w_ref[...]
        err = xhat * wv - t_ref[...]
        loss_ref[...] += 0.5 * _rowsum(jnp.mean(err * err, axis=0, keepdims=True))
        dyv = err * (1.0 / f)
        dw_ref[...] += _rowsum(dyv * xhat)
        dxhat = dyv * wv
        dh_ref[...] = r * (dxhat - xhat * jnp.mean(dxhat * xhat, axis=0, keepdims=True))

    blk = pl.BlockSpec((f, tt), lambda i: (0, i))
    col = pl.BlockSpec((f, 1), lambda i: (0, 0))
    one = pl.BlockSpec((1, 1), lambda i: (0, 0))
    return pl.pallas_call(
        body,
        name="final_norm_loss",
        grid=(t // tt,),
        in_specs=[blk, blk, col],
        out_specs=[blk, one, col],
        out_shape=[jax.ShapeDtypeStruct((f, t), F32), jax.ShapeDtypeStruct((1, 1), F32), jax.ShapeDtypeStruct((f, 1), F32)],
        compiler_params=_cparams("arbitrary"),
    )(h, tgt, w_col)


def _attn_mask(n):
    shape = (2 * WINDOW, Q_PER_KV * WINDOW)
    si = lax.broadcasted_iota(jnp.int32, shape, 0)
    qi = lax.broadcasted_iota(jnp.int32, shape, 1) & (WINDOW - 1)
    dist = WINDOW + qi - si
    return (dist >= 0) & (dist < WINDOW) & ((si >= WINDOW) | (n > 0))


def _lane_cat(ref, row0, rows):
    return jnp.concatenate([ref[row0 + i * rows:row0 + (i + 1) * rows, :] for i in range(Q_PER_KV)], axis=1)


def _attn_fwd(proj, sinks):
    t = proj.shape[1]
    nb = t // WINDOW
    scale = HEAD_DIM ** -0.5

    def body(s_ref, q_ref, kc_ref, kp_ref, vc_ref, vp_ref, o_ref, lse_ref):
        n = pl.program_id(0)
        valid = _attn_mask(n)
        for g in range(N_KV_HEADS):
            rows = slice(g * HEAD_DIM, (g + 1) * HEAD_DIM)
            kt = jnp.concatenate([kp_ref[rows, :], kc_ref[rows, :]], axis=1).astype(BF16)
            vt = jnp.concatenate([vp_ref[rows, :], vc_ref[rows, :]], axis=1).astype(BF16)
            qcat = _lane_cat(q_ref, g * Q_PER_KV * HEAD_DIM, HEAD_DIM).astype(BF16)
            s = jnp.where(valid, _dot_tn(kt, qcat) * scale, NEG)
            sink = jnp.concatenate(
                [jnp.full((1, WINDOW), s_ref[g * Q_PER_KV + i], F32) for i in range(Q_PER_KV)], axis=1)
            m = jnp.maximum(jnp.max(s, axis=0, keepdims=True), sink)
            p = jnp.where(valid, jnp.exp(s - m), 0.0)
            denom = _colsum(p) + jnp.exp(sink - m)
            probs = (p / denom).astype(BF16)
            out = _dot(vt, probs)
            lse = m + jnp.log(denom)
            for i in range(Q_PER_KV):
                h = g * Q_PER_KV + i
                o_ref[h * HEAD_DIM:(h + 1) * HEAD_DIM, :] = out[:, i * WINDOW:(i + 1) * WINDOW]
                lse_ref[h:h + 1, :] = lse[:, i * WINDOW:(i + 1) * WINDOW]

    kb = OFF_K // KV_DIM
    vb = OFF_V // KV_DIM
    prev = lambda n: jnp.maximum(n - 1, 0)
    return pl.pallas_call(
        body,
        name="attn_fwd",
        grid=(nb,),
        in_specs=[
            pl.BlockSpec(memory_space=pltpu.SMEM),
            pl.BlockSpec((Q_DIM, WINDOW), lambda n: (0, n)),
            pl.BlockSpec((KV_DIM, WINDOW), lambda n: (kb, n)),
            pl.BlockSpec((KV_DIM, WINDOW), lambda n: (kb, prev(n))),
            pl.BlockSpec((KV_DIM, WINDOW), lambda n: (vb, n)),
            pl.BlockSpec((KV_DIM, WINDOW), lambda n: (vb, prev(n))),
        ],
        out_specs=[pl.BlockSpec((Q_DIM, WINDOW), lambda n: (0, n)), pl.BlockSpec((N_Q_HEADS, WINDOW), lambda n: (0, n))],
        out_shape=[jax.ShapeDtypeStruct((Q_DIM, t), F32), jax.ShapeDtypeStruct((N_Q_HEADS, t), F32)],
        compiler_params=_cparams("parallel"),
    )(sinks, proj, proj, proj, proj, proj)


def _attn_bwd(proj, sinks, out, lse, dout):
    t = proj.shape[1]
    nb = t // WINDOW
    scale = HEAD_DIM ** -0.5

    def body(s_ref, q_ref, kc_ref, kp_ref, vc_ref, vp_ref, o_ref, lse_ref, do_ref,
             dq_ref, dk_ref, dv_ref, ds_ref, dk_carry, dv_carry):
        step = pl.program_id(0)
        n = nb - 1 - step

        @pl.when(step == 0)
        def _():
            dk_carry[...] = jnp.zeros_like(dk_carry)
            dv_carry[...] = jnp.zeros_like(dv_carry)
            ds_ref[...] = jnp.zeros_like(ds_ref)

        valid = _attn_mask(n)
        for g in range(N_KV_HEADS):
            rows = slice(g * HEAD_DIM, (g + 1) * HEAD_DIM)
            q0 = g * Q_PER_KV * HEAD_DIM
            kt = jnp.concatenate([kp_ref[rows, :], kc_ref[rows, :]], axis=1).astype(BF16)
            vt = jnp.concatenate([vp_ref[rows, :], vc_ref[rows, :]], axis=1).astype(BF16)
            qcat = _lane_cat(q_ref, q0, HEAD_DIM).astype(BF16)
            ocat = _lane_cat(o_ref, q0, HEAD_DIM)
            docat = _lane_cat(do_ref, q0, HEAD_DIM)
            dob = docat.astype(BF16)
            lse_cat = jnp.concatenate(
                [lse_ref[g * Q_PER_KV + i:g * Q_PER_KV + i + 1, :] for i in range(Q_PER_KV)], axis=1)
            sink = jnp.concatenate(
                [jnp.full((1, WINDOW), s_ref[g * Q_PER_KV + i], F32) for i in range(Q_PER_KV)], axis=1)
            s = jnp.where(valid, _dot_tn(kt, qcat) * scale, NEG)
            p = jnp.where(valid, jnp.exp(s - lse_cat), 0.0)
            dp = _dot_tn(vt, dob)
            delta = _colsum(docat * ocat)
            dsc = (p * (dp - delta)).astype(BF16)
            dsink_row = -jnp.exp(sink - lse_cat) * delta
            dq = _dot(kt, dsc) * scale
            dk = _dot_nt(qcat, dsc) * scale
            dv = _dot_nt(dob, p.astype(BF16))
            for i in range(Q_PER_KV):
                h = g * Q_PER_KV + i
                dq_ref[h * HEAD_DIM:(h + 1) * HEAD_DIM, :] = dq[:, i * WINDOW:(i + 1) * WINDOW].astype(BF16)
                ds_ref[h:h + 1, :] += _rowsum(dsink_row[:, i * WINDOW:(i + 1) * WINDOW])
            dk_ref[rows, :] = (dk[:, WINDOW:] + dk_carry[rows, :]).astype(BF16)
            dv_ref[rows, :] = (dv[:, WINDOW:] + dv_carry[rows, :]).astype(BF16)
            dk_carry[rows, :] = dk[:, :WINDOW]
            dv_carry[rows, :] = dv[:, :WINDOW]

    kb = OFF_K // KV_DIM
    vb = OFF_V // KV_DIM
    cur = lambda i: nb - 1 - i
    prev = lambda i: jnp.maximum(nb - 2 - i, 0)
    qspec = pl.BlockSpec((Q_DIM, WINDOW), lambda i: (0, cur(i)))
    kvspec = pl.BlockSpec((KV_DIM, WINDOW), lambda i: (0, cur(i)))
    return pl.pallas_call(
        body,
        name="attn_bwd",
        grid=(nb,),
        in_specs=[
            pl.BlockSpec(memory_space=pltpu.SMEM),
            qspec,
            pl.BlockSpec((KV_DIM, WINDOW), lambda i: (kb, cur(i))),
            pl.BlockSpec((KV_DIM, WINDOW), lambda i: (kb, prev(i))),
            pl.BlockSpec((KV_DIM, WINDOW), lambda i: (vb, cur(i))),
            pl.BlockSpec((KV_DIM, WINDOW), lambda i: (vb, prev(i))),
            qspec,
            pl.BlockSpec((N_Q_HEADS, WINDOW), lambda i: (0, cur(i))),
            qspec,
        ],
        out_specs=[qspec, kvspec, kvspec, pl.BlockSpec((N_Q_HEADS, 1), lambda i: (0, 0))],
        out_shape=[
            jax.ShapeDtypeStruct((Q_DIM, t), BF16),
            jax.ShapeDtypeStruct((KV_DIM, t), BF16),
            jax.ShapeDtypeStruct((KV_DIM, t), BF16),
            jax.ShapeDtypeStruct((N_Q_HEADS, 1), F32),
        ],
        scratch_shapes=[pltpu.VMEM((KV_DIM, WINDOW), F32), pltpu.VMEM((KV_DIM, WINDOW), F32)],
        compiler_params=_cparams("arbitrary"),
    )(sinks, proj, proj, proj, proj, proj, out, lse, dout)


CONV_ROWS = 256


def _conv_silu_fwd(proj, w_col, b_col):
    t = proj.shape[1]
    r0 = OFF_X // CONV_ROWS

    def body(x_ref, w_ref, b_ref, o_ref):
        xv = x_ref[...]
        wv = w_ref[...]
        y = b_ref[...] + wv[:, SSD_CONV - 1:SSD_CONV] * xv
        for k in range(SSD_CONV - 1):
            y = y + wv[:, k:k + 1] * _shift_right(xv, SSD_CONV - 1 - k)
        o_ref[...] = y * _sigmoid(y)

    return pl.pallas_call(
        body,
        name="ssd_conv_fwd",
        grid=(XBC_DIM // CONV_ROWS,),
        in_specs=[
            pl.BlockSpec((CONV_ROWS, t), lambda i: (r0 + i, 0)),
            pl.BlockSpec((CONV_ROWS, SSD_CONV), lambda i: (i, 0)),
            pl.BlockSpec((CONV_ROWS, 1), lambda i: (i, 0)),
        ],
        out_specs=pl.BlockSpec((CONV_ROWS, t), lambda i: (i, 0)),
        out_shape=jax.ShapeDtypeStruct((XBC_DIM, t), F32),
        compiler_params=_cparams("parallel"),
    )(proj, w_col, b_col)


def _conv_silu_bwd(proj, w_col, b_col, dout, row0, name):
    t = proj.shape[1]
    nrows = dout.shape[0]
    p0 = (OFF_X + row0) // CONV_ROWS
    c0 = row0 // CONV_ROWS

    def body(x_ref, w_ref, b_ref, do_ref, dx_ref, dwb_ref):
        xv = x_ref[...]
        wv = w_ref[...]
        y = b_ref[...] + wv[:, SSD_CONV - 1:SSD_CONV] * xv
        for k in range(SSD_CONV - 1):
            y = y + wv[:, k:k + 1] * _shift_right(xv, SSD_CONV - 1 - k)
        sg = _sigmoid(y)
        dy = do_ref[...] * (sg * (1.0 + y * (1.0 - sg)))
        lane = lax.broadcasted_iota(jnp.int32, (CONV_ROWS, 128), 1)
        dwb = jnp.where(lane == SSD_CONV, _rowsum(dy), 0.0)
        dx = wv[:, SSD_CONV - 1:SSD_CONV] * dy
        dwb = jnp.where(lane == SSD_CONV - 1, _rowsum(dy * xv), dwb)
        for k in range(SSD_CONV - 1):
            j = SSD_CONV - 1 - k
            dx = dx + wv[:, k:k + 1] * _shift_left(dy, j)
            dwb = jnp.where(lane == k, _rowsum(dy * _shift_right(xv, j)), dwb)
        dx_ref[...] = dx.astype(BF16)
        dwb_ref[...] = dwb

    return pl.pallas_call(
        body,
        name=name,
        grid=(nrows // CONV_ROWS,),
        in_specs=[
            pl.BlockSpec((CONV_ROWS, t), lambda i: (p0 + i, 0)),
            pl.BlockSpec((CONV_ROWS, SSD_CONV), lambda i: (c0 + i, 0)),
            pl.BlockSpec((CONV_ROWS, 1), lambda i: (c0 + i, 0)),
            pl.BlockSpec((CONV_ROWS, t), lambda i: (i, 0)),
        ],
        out_specs=[pl.BlockSpec((CONV_ROWS, t), lambda i: (i, 0)), pl.BlockSpec((CONV_ROWS, 128), lambda i: (i, 0))],
        out_shape=[jax.ShapeDtypeStruct((nrows, t), BF16), jax.ShapeDtypeStruct((nrows, 128), F32)],
        compiler_params=_cparams("parallel"),
    )(proj, w_col, b_col, dout)


GROUP_ROWS = HEADS_PER_GROUP * SSD_HEAD_DIM


def _ssd_specs(nc, order):
    hb = D_INNER // D_STATE
    dtb = OFF_DT // HEADS_PER_GROUP
    col = pl.BlockSpec((HEADS_PER_GROUP, 1), lambda g, c: (g, 0))
    return [
        pl.BlockSpec((GROUP_ROWS, CHUNK), lambda g, c: (g, order(c))),
        pl.BlockSpec((D_STATE, CHUNK), lambda g, c: (hb + g, order(c))),
        pl.BlockSpec((D_STATE, CHUNK), lambda g, c: (hb + N_SSD_GROUPS + g, order(c))),
        pl.BlockSpec((HEADS_PER_GROUP, CHUNK), lambda g, c: (dtb + g, order(c))),
        col, col, col,
    ]


def _ssd_common(dt_ref, dtb_ref, alog_ref):
    z = dt_ref[...] + dtb_ref[...]
    dt = _softplus(z)
    a_neg = -jnp.exp(alog_ref[...])
    d_a = dt * a_neg
    row = lax.broadcasted_iota(jnp.int32, (CHUNK, CHUNK), 0)
    colm = lax.broadcasted_iota(jnp.int32, (CHUNK, CHUNK), 1)
    upper = (row <= colm).astype(F32)
    a_cs = jnp.dot(d_a, upper, precision=HIGHEST, preferred_element_type=F32)
    a_last = _rowsum(d_a)
    return z, dt, a_neg, a_cs, a_last, row >= colm, row == colm


def _decay(a_row, causal):
    a_s = jnp.broadcast_to(a_row, (CHUNK, CHUNK))
    seg = a_s.T - a_s
    return jnp.where(causal, jnp.exp(jnp.where(causal, seg, 0.0)), 0.0)


def _ssd_fwd(xbc, proj, dtb_col, alog_col, dsk_col):
    t = xbc.shape[1]
    nc = t // CHUNK

    def body(xs_ref, b_ref, c_ref, dt_ref, dtb_ref, alog_ref, dsk_ref, y_ref, hst_ref, h_scr):
        @pl.when(pl.program_id(1) == 0)
        def _():
            h_scr[...] = jnp.zeros_like(h_scr)

        _, dt, _, a_cs, a_last, causal, _ = _ssd_common(dt_ref, dtb_ref, alog_ref)
        bb = b_ref[...].astype(BF16)
        cb_ = c_ref[...].astype(BF16)
        cb = _dot_tn(cb_, bb)
        hst_ref[0, 0] = h_scr[...]
        dsk = dsk_ref[...]
        for j in range(HEADS_PER_GROUP):
            rows = slice(j * SSD_HEAD_DIM, (j + 1) * SSD_HEAD_DIM)
            a = a_cs[j:j + 1, :]
            m = (cb * _decay(a, causal)).astype(BF16)
            xs = xs_ref[rows, :]
            xc = xs * dt[j:j + 1, :]
            hj = h_scr[rows, :]
            y = _dot_nt(xc.astype(BF16), m) + _dot(hj.astype(BF16), cb_) * jnp.exp(a) + dsk[j:j + 1, :] * xs
            y_ref[rows, :] = y
            al = a_last[j:j + 1, :]
            w = jnp.exp(al - a)
            h_scr[rows, :] = jnp.exp(al) * hj + _dot_nt((xc * w).astype(BF16), bb)

    return pl.pallas_call(
        body,
        name="ssd_fwd",
        grid=(N_SSD_GROUPS, nc),
        in_specs=_ssd_specs(nc, lambda c: c),
        out_specs=[
            pl.BlockSpec((GROUP_ROWS, CHUNK), lambda g, c: (g, c)),
            pl.BlockSpec((1, 1, GROUP_ROWS, D_STATE), lambda g, c: (g, c, 0, 0)),
        ],
        out_shape=[
            jax.ShapeDtypeStruct((D_INNER, t), F32),
            jax.ShapeDtypeStruct((N_SSD_GROUPS, nc, GROUP_ROWS, D_STATE), F32),
        ],
        scratch_shapes=[pltpu.VMEM((GROUP_ROWS, D_STATE), F32)],
        compiler_params=_cparams("parallel", "arbitrary"),
    )(xbc, xbc, xbc, proj, dtb_col, alog_col, dsk_col)


def _ssd_bwd(xbc, proj, dtb_col, alog_col, dsk_col, hst, dy):
    t = xbc.shape[1]
    nc = t // CHUNK
    rev = lambda c: nc - 1 - c

    def body(xs_ref, b_ref, c_ref, dt_ref, dtb_ref, alog_ref, dsk_ref, hst_ref, dy_ref,
             dxs_ref, db_ref, dc_ref, ddt_ref, dalog_ref, ddsk_ref, ddtb_ref, dh_scr, da_scr, ddt_scr, dd_scr):
        @pl.when(pl.program_id(1) == 0)
        def _():
            dh_scr[...] = jnp.zeros_like(dh_scr)
            dalog_ref[...] = jnp.zeros_like(dalog_ref)
            ddsk_ref[...] = jnp.zeros_like(ddsk_ref)
            ddtb_ref[...] = jnp.zeros_like(ddtb_ref)

        z, dt, a_neg, a_cs, a_last, causal, eye = _ssd_common(dt_ref, dtb_ref, alog_ref)
        bb = b_ref[...].astype(BF16)
        cb_ = c_ref[...].astype(BF16)
        cb = _dot_tn(cb_, bb)
        dsk = dsk_ref[...]
        last_lane = lax.broadcasted_iota(jnp.int32, (1, CHUNK), 1) == CHUNK - 1
        dcb = jnp.zeros((CHUNK, CHUNK), F32)
        dc_acc = jnp.zeros((D_STATE, CHUNK), F32)
        db_acc = jnp.zeros((D_STATE, CHUNK), F32)
        for j in range(HEADS_PER_GROUP):
            rows = slice(j * SSD_HEAD_DIM, (j + 1) * SSD_HEAD_DIM)
            a = a_cs[j:j + 1, :]
            al = a_last[j:j + 1, :]
            lam = _decay(a, causal)
            mf = cb * lam
            xs = xs_ref[rows, :]
            dtj = dt[j:j + 1, :]
            xc = xs * dtj
            w = jnp.exp(al - a)
            e = jnp.exp(a)
            gam = jnp.exp(al)
            hj = hst_ref[0, 0, rows, :]
            hjb = hj.astype(BF16)
            dyv = dy_ref[rows, :]
            dyb = dyv.astype(BF16)
            dd_scr[j:j + 1, :] = _colsum(dyv * xs)
            gb = (dyv * e).astype(BF16)
            dh_in = _dot_nt(gb, cb_)
            dc_acc = dc_acc + _dot_tn(hjb, gb)
            yoff = _dot(hjb, cb_) * e
            da = _colsum(dyv * yoff)
            dm = _dot_tn(dyb, xc.astype(BF16))
            dxc = _dot(dyb, mf.astype(BF16))
            dcb = dcb + dm * lam
            nmat = dm * mf
            rs = jnp.broadcast_to(_rowsum(nmat), (CHUNK, CHUNK))
            da = da + _colsum(jnp.where(eye, rs, 0.0)) - _colsum(nmat)
            ds = dh_scr[rows, :]
            dsb = ds.astype(BF16)
            t1 = _dot(dsb, bb)
            xcw = xc * w
            dxc = dxc + w * t1
            dww = _colsum(xcw * t1)
            da_l = _rowsum(dww) + _rowsum(_colsum(ds * hj)) * gam
            da = da - dww + jnp.where(last_lane, da_l, 0.0)
            db_acc = db_acc + _dot_tn(dsb, xcw.astype(BF16))
            dh_scr[rows, :] = gam * ds + dh_in
            dxs_ref[rows, :] = dsk[j:j + 1, :] * dyv + dxc * dtj
            da_scr[j:j + 1, :] = da
            ddt_scr[j:j + 1, :] = _colsum(dxc * xs)
        dcbb = dcb.astype(BF16)
        dc_ref[...] = dc_acc + _dot_nt(bb, dcbb)
        db_ref[...] = db_acc + _dot(cb_, dcbb)
        dda = jnp.dot(da_scr[...], causal.astype(F32), precision=HIGHEST, preferred_element_type=F32)
        ddt = ddt_scr[...] + dda * a_neg
        ddt_raw = ddt * _sigmoid(z)
        ddt_ref[...] = ddt_raw
        ddtb_ref[...] += _rowsum(ddt_raw)
        dalog_ref[...] += _rowsum(dda * dt) * a_neg
        ddsk_ref[...] += _rowsum(dd_scr[...])

    col = pl.BlockSpec((HEADS_PER_GROUP, 1), lambda g, c: (g, 0))
    bc = pl.BlockSpec((D_STATE, CHUNK), lambda g, c: (g, rev(c)))
    xs_spec = pl.BlockSpec((GROUP_ROWS, CHUNK), lambda g, c: (g, rev(c)))
    small = pltpu.VMEM((HEADS_PER_GROUP, CHUNK), F32)
    return pl.pallas_call(
        body,
        name="ssd_bwd",
        grid=(N_SSD_GROUPS, nc),
        in_specs=_ssd_specs(nc, rev) + [
            pl.BlockSpec((1, 1, GROUP_ROWS, D_STATE), lambda g, c: (g, rev(c), 0, 0)),
            xs_spec,
        ],
        out_specs=[xs_spec, bc, bc, pl.BlockSpec((HEADS_PER_GROUP, CHUNK), lambda g, c: (g, rev(c))), col, col, col],
        out_shape=[
            jax.ShapeDtypeStruct((D_INNER, t), F32),
            jax.ShapeDtypeStruct((BC_DIM, t), F32),
            jax.ShapeDtypeStruct((BC_DIM, t), F32),
            jax.ShapeDtypeStruct((N_SSD_HEADS, t), F32),
            jax.ShapeDtypeStruct((N_SSD_HEADS, 1), F32),
            jax.ShapeDtypeStruct((N_SSD_HEADS, 1), F32),
            jax.ShapeDtypeStruct((N_SSD_HEADS, 1), F32),
        ],
        scratch_shapes=[pltpu.VMEM((GROUP_ROWS, D_STATE), F32), small, small, small],
        compiler_params=_cparams("parallel", "arbitrary"),
    )(xbc, xbc, xbc, proj, dtb_col, alog_col, dsk_col, hst, dy)


GN_ROWS = D_INNER // N_SSD_GROUPS


def _gnorm_fwd(y, proj, w_col):
    t = y.shape[1]
    tt = _tile(t, (512, 256, 128))
    z0 = OFF_Z // GN_ROWS

    def body(y_ref, z_ref, w_ref, o_ref):
        zv = z_ref[...]
        u = y_ref[...] * (zv * _sigmoid(zv))
        r = lax.rsqrt(jnp.mean(u * u, axis=0, keepdims=True) + EPS)
        o_ref[...] = (u * r * w_ref[...]).astype(BF16)

    blk = pl.BlockSpec((GN_ROWS, tt), lambda g, i: (g, i))
    return pl.pallas_call(
        body,
        name="gnorm_fwd",
        grid=(N_SSD_GROUPS, t // tt),
        in_specs=[blk, pl.BlockSpec((GN_ROWS, tt), lambda g, i: (z0 + g, i)), pl.BlockSpec((GN_ROWS, 1), lambda g, i: (g, 0))],
        out_specs=blk,
        out_shape=jax.ShapeDtypeStruct((D_INNER, t), BF16),
        compiler_params=_cparams("parallel", "parallel"),
    )(y, proj, w_col)


def _gnorm_bwd(dout, y, proj, w_col):
    t = y.shape[1]
    tt = _tile(t, (512, 256, 128))
    z0 = OFF_Z // GN_ROWS

    def body(do_ref, y_ref, z_ref, w_ref, dy_ref, dz_ref, dw_ref):
        @pl.when(pl.program_id(1) == 0)
        def _():
            dw_ref[...] = jnp.zeros_like(dw_ref)

        zv = z_ref[...]
        yv = y_ref[...]
        sg = _sigmoid(zv)
        sz = zv * sg
        u = yv * sz
        r = lax.rsqrt(jnp.mean(u * u, axis=0, keepdims=True) + EPS)
        xhat = u * r
        dov = do_ref[...]
        dw_ref[...] += _rowsum(dov * xhat)
        dxhat = dov * w_ref[...]
        du = r * (dxhat - xhat * jnp.mean(dxhat * xhat, axis=0, keepdims=True))
        dy_ref[...] = du * sz
        dz_ref[...] = (du * yv * (sg * (1.0 + zv * (1.0 - sg)))).astype(BF16)

    blk = pl.BlockSpec((GN_ROWS, tt), lambda g, i: (g, i))
    col = pl.BlockSpec((GN_ROWS, 1), lambda g, i: (g, 0))
    return pl.pallas_call(
        body,
        name="gnorm_bwd",
        grid=(N_SSD_GROUPS, t // tt),
        in_specs=[blk, blk, pl.BlockSpec((GN_ROWS, tt), lambda g, i: (z0 + g, i)), col],
        out_specs=[blk, blk, col],
        out_shape=[jax.ShapeDtypeStruct((D_INNER, t), F32), jax.ShapeDtypeStruct((D_INNER, t), BF16),
                   jax.ShapeDtypeStruct((D_INNER, 1), F32)],
        compiler_params=_cparams("parallel", "arbitrary"),
    )(dout, y, proj, w_col)


GATE_ROWS = 512


def _gate_specs(t, tt):
    ga0 = OFF_GA // GATE_ROWS
    gs0 = OFF_GS // GATE_ROWS
    nr = D_MODEL // GATE_ROWS
    blk = pl.BlockSpec((GATE_ROWS, tt), lambda r, i: (r, i))
    return blk, [
        pl.BlockSpec((GATE_ROWS, tt), lambda r, i: (ga0 + r, i)),
        pl.BlockSpec((GATE_ROWS, tt), lambda r, i: (gs0 + r, i)),
        pl.BlockSpec((GATE_ROWS, 1), lambda r, i: (r, 0)),
        pl.BlockSpec((GATE_ROWS, 1), lambda r, i: (nr + r, 0)),
        blk, blk,
    ]


def _gate_fwd(proj, b_col, attn, ssd):
    t = proj.shape[1]
    tt = _tile(t, (512, 256, 128))
    blk, specs = _gate_specs(t, tt)

    def body(ga_ref, gs_ref, ba_ref, bs_ref, a_ref, s_ref, o_ref):
        o_ref[...] = (_sigmoid(ga_ref[...] + ba_ref[...]) * a_ref[...]
                      + _sigmoid(gs_ref[...] + bs_ref[...]) * s_ref[...]).astype(BF16)

    return pl.pallas_call(
        body,
        name="gate_fwd",
        grid=(D_MODEL // GATE_ROWS, t // tt),
        in_specs=specs,
        out_specs=blk,
        out_shape=jax.ShapeDtypeStruct((D_MODEL, t), BF16),
        compiler_params=_cparams("parallel", "parallel"),
    )(proj, proj, b_col, b_col, attn, ssd)


def _gate_bwd(proj, b_col, attn, ssd, dmix):
    t = proj.shape[1]
    tt = _tile(t, (512, 256, 128))
    blk, specs = _gate_specs(t, tt)
    nr = D_MODEL // GATE_ROWS

    def body(ga_ref, gs_ref, ba_ref, bs_ref, a_ref, s_ref, dm_ref, da_ref, dso_ref, dga_ref, dgs_ref, dba_ref, dbs_ref):
        @pl.when(pl.program_id(1) == 0)
        def _():
            dba_ref[...] = jnp.zeros_like(dba_ref)
            dbs_ref[...] = jnp.zeros_like(dbs_ref)

        dm = dm_ref[...]
        sa = _sigmoid(ga_ref[...] + ba_ref[...])
        ss = _sigmoid(gs_ref[...] + bs_ref[...])
        da_ref[...] = (dm * sa).astype(BF16)
        dso_ref[...] = (dm * ss).astype(BF16)
        dga = dm * a_ref[...] * sa * (1.0 - sa)
        dgs = dm * s_ref[...] * ss * (1.0 - ss)
        dga_ref[...] = dga.astype(BF16)
        dgs_ref[...] = dgs.astype(BF16)
        dba_ref[...] += _rowsum(dga)
        dbs_ref[...] += _rowsum(dgs)

    col = pl.BlockSpec((GATE_ROWS, 1), lambda r, i: (r, 0))
    act = jax.ShapeDtypeStruct((D_MODEL, t), BF16)
    bias = jax.ShapeDtypeStruct((D_MODEL, 1), F32)
    return pl.pallas_call(
        body,
        name="gate_bwd",
        grid=(nr, t // tt),
        in_specs=specs + [blk],
        out_specs=[blk, blk, blk, blk, col, col],
        out_shape=[act, act, act, act, bias, bias],
        compiler_params=_cparams("parallel", "arbitrary"),
    )(proj, proj, b_col, b_col, attn, ssd, dmix)


FFN_ROWS = 256


def _ffn_conv(u_ref, w_ref, b_ref, half):
    xv = u_ref[half]
    wv = w_ref[half]
    y = b_ref[half] + wv[:, FFN_CONV - 1:FFN_CONV] * xv
    for k in range(FFN_CONV - 1):
        y = y + wv[:, k:k + 1] * _shift_right(xv, FFN_CONV - 1 - k)
    return xv, wv, y


def _ffn_fwd(u0, w_col, b_col):
    t = u0.shape[2]

    def body(u_ref, w_ref, b_ref, o_ref):
        _, _, val = _ffn_conv(u_ref, w_ref, b_ref, 0)
        _, _, gt = _ffn_conv(u_ref, w_ref, b_ref, 1)
        o_ref[...] = (gt * _sigmoid(gt) * val).astype(BF16)

    return pl.pallas_call(
        body,
        name="ffn_fwd",
        grid=(D_FF // FFN_ROWS,),
        in_specs=[
            pl.BlockSpec((2, FFN_ROWS, t), lambda i: (0, i, 0)),
            pl.BlockSpec((2, FFN_ROWS, FFN_CONV), lambda i: (0, i, 0)),
            pl.BlockSpec((2, FFN_ROWS, 1), lambda i: (0, i, 0)),
        ],
        out_specs=pl.BlockSpec((FFN_ROWS, t), lambda i: (i, 0)),
        out_shape=jax.ShapeDtypeStruct((D_FF, t), BF16),
        compiler_params=_cparams("parallel"),
    )(u0, w_col, b_col)


def _ffn_bwd(u0, w_col, b_col, dg):
    t = u0.shape[2]

    def body(u_ref, w_ref, b_ref, dg_ref, du_ref, dwb_ref):
        xval, wval, val = _ffn_conv(u_ref, w_ref, b_ref, 0)
        xgt, wgt, gt = _ffn_conv(u_ref, w_ref, b_ref, 1)
        sg = _sigmoid(gt)
        dgv = dg_ref[...]
        dval = dgv * (gt * sg)
        dgt = dgv * val * (sg * (1.0 + gt * (1.0 - sg)))
        lane = lax.broadcasted_iota(jnp.int32, (FFN_ROWS, 128), 1)
        for half, xv, wv, dy in ((0, xval, wval, dval), (1, xgt, wgt, dgt)):
            dwb = jnp.where(lane == FFN_CONV, _rowsum(dy), 0.0)
            dx = wv[:, FFN_CONV - 1:FFN_CONV] * dy
            dwb = jnp.where(lane == FFN_CONV - 1, _rowsum(dy * xv), dwb)
            for k in range(FFN_CONV - 1):
                j = FFN_CONV - 1 - k
                dx = dx + wv[:, k:k + 1] * _shift_left(dy, j)
                dwb = jnp.where(lane == k, _rowsum(dy * _shift_right(xv, j)), dwb)
            du_ref[half] = dx.astype(BF16)
            dwb_ref[half] = dwb

    return pl.pallas_call(
        body,
        name="ffn_bwd",
        grid=(D_FF // FFN_ROWS,),
        in_specs=[
            pl.BlockSpec((2, FFN_ROWS, t), lambda i: (0, i, 0)),
            pl.BlockSpec((2, FFN_ROWS, FFN_CONV), lambda i: (0, i, 0)),
            pl.BlockSpec((2, FFN_ROWS, 1), lambda i: (0, i, 0)),
            pl.BlockSpec((FFN_ROWS, t), lambda i: (i, 0)),
        ],
        out_specs=[pl.BlockSpec((2, FFN_ROWS, t), lambda i: (0, i, 0)), pl.BlockSpec((2, FFN_ROWS, 128), lambda i: (0, i, 0))],
        out_shape=[jax.ShapeDtypeStruct((2, D_FF, t), BF16), jax.ShapeDtypeStruct((2, D_FF, 128), F32)],
        compiler_params=_cparams("parallel"),
    )(u0, w_col, b_col, dg)


def _adamw_math(w, g, m, v):
    m = ADAM_B1 * m + (1.0 - ADAM_B1) * g
    v = ADAM_B2 * v + (1.0 - ADAM_B2) * (g * g)
    m_hat = m / (1.0 - ADAM_B1 ** ADAM_STEP)
    v_hat = v / (1.0 - ADAM_B2 ** ADAM_STEP)
    delta = -ADAM_LR * (m_hat / (jnp.sqrt(v_hat) + ADAM_EPS) + ADAM_WD * w)
    return delta, m, v


def _adamw_sharded(parts, w, m, v, name):
    r, c = w.shape
    tc = _tile(c, (256, 128))

    def body(p_ref, w_ref, m_ref, v_ref, g_ref, d_ref, nm_ref, nv_ref):
        g = p_ref[0].astype(F32)
        for s in range(1, N_DEV):
            g = g + p_ref[s].astype(F32)
        g_ref[...] = g
        d_ref[...], nm_ref[...], nv_ref[...] = _adamw_math(w_ref[...], g, m_ref[...], v_ref[...])

    blk = pl.BlockSpec((r, tc), lambda i: (0, i))
    out = jax.ShapeDtypeStruct((r, c), F32)
    return pl.pallas_call(
        body,
        name=name,
        grid=(c // tc,),
        in_specs=[pl.BlockSpec((N_DEV, r, tc), lambda i: (0, 0, i)), blk, blk, blk],
        out_specs=[blk, blk, blk, blk],
        out_shape=[out, out, out, out],
        compiler_params=_cparams("parallel"),
    )(parts, w, m, v)


def _sum_slots(parts):
    _, r, c = parts.shape

    def body(p_ref, o_ref):
        g = p_ref[0]
        for s in range(1, N_DEV):
            g = g + p_ref[s]
        o_ref[...] = g

    return pl.pallas_call(body, name="sum_small_grads", out_shape=jax.ShapeDtypeStruct((r, c), F32))(parts)


def _adamw_small(g, w, m, v):
    def body(g_ref, w_ref, m_ref, v_ref, d_ref, nm_ref, nv_ref):
        d_ref[...], nm_ref[...], nv_ref[...] = _adamw_math(w_ref[...], g_ref[...], m_ref[...], v_ref[...])

    out = jax.ShapeDtypeStruct(g.shape, F32)
    return pl.pallas_call(body, name="adamw_small", out_shape=[out, out, out])(g, w, m, v)


ANY = pl.BlockSpec(memory_space=pl.ANY)
FLIPS = [(k >> 2 & 1, k >> 1 & 1, k & 1) for k in range(1, N_DEV)]


def _place():
    return lax.axis_index("x"), lax.axis_index("y"), lax.axis_index("c")


def _all_gather(arrays, name):
    n = len(arrays)

    def body(*refs):
        ins, outs = refs[:n], refs[n:2 * n]
        send_sems, recv_sems, local_sems = refs[2 * n:]
        x, y, c = _place()
        me = 4 * x + 2 * y + c
        local = [pltpu.make_async_copy(ins[i], outs[i].at[me], local_sems.at[i]) for i in range(n)]
        for cp in local:
            cp.start()
        sends = []
        for k, (fx, fy, fc) in enumerate(FLIPS):
            for i in range(n):
                cp = pltpu.make_async_remote_copy(
                    src_ref=ins[i], dst_ref=outs[i].at[me], send_sem=send_sems.at[i, k], recv_sem=recv_sems.at[i, k],
                    device_id=(x ^ fx, y ^ fy, c ^ fc), device_id_type=MESH)
                cp.start()
                sends.append(cp)
        for k, (fx, fy, fc) in enumerate(FLIPS):
            src = 4 * (x ^ fx) + 2 * (y ^ fy) + (c ^ fc)
            for i in range(n):
                pltpu.make_async_remote_copy(
                    src_ref=ins[i], dst_ref=outs[i].at[src], send_sem=send_sems.at[i, k], recv_sem=recv_sems.at[i, k],
                    device_id=(x ^ fx, y ^ fy, c ^ fc), device_id_type=MESH).wait_recv()
        for cp in sends:
            cp.wait_send()
        for cp in local:
            cp.wait()

    return pl.pallas_call(
        body,
        name=name,
        in_specs=[ANY] * n,
        out_specs=[ANY] * n,
        out_shape=[jax.ShapeDtypeStruct((N_DEV,) + a.shape, a.dtype) for a in arrays],
        scratch_shapes=[pltpu.SemaphoreType.DMA((n, N_DEV - 1)), pltpu.SemaphoreType.DMA((n, N_DEV - 1)),
                        pltpu.SemaphoreType.DMA((n,))],
    )(*arrays)


HBM = pl.BlockSpec(memory_space=pltpu.HBM)
SEM = pl.BlockSpec(memory_space=pltpu.SEMAPHORE)
EFFECT = pltpu.SideEffectType.DATAFLOW_SIDE_EFFECTING


def _peer_copy(gather, src_ref, land_ref, send_sems, recv_sems, k, sending):
    x, y, c = _place()
    fx, fy, fc = FLIPS[k]
    me = 4 * x + 2 * y + c
    peer = 4 * (x ^ fx) + 2 * (y ^ fy) + (c ^ fc)
    return pltpu.make_async_remote_copy(
        src_ref=src_ref if gather else src_ref.at[peer],
        dst_ref=land_ref.at[me if sending else peer],
        send_sem=send_sems.at[k], recv_sem=recv_sems.at[k],
        device_id=(x ^ fx, y ^ fy, c ^ fc), device_id_type=MESH)


def _gather_start(srcs, name):
    n = len(srcs)
    lands = [lax.empty((N_DEV,) + s.shape, s.dtype) for s in srcs]

    def body(*refs):
        src_refs, land_refs = refs[:n], refs[n:2 * n]
        send, recv = refs[2 * n:3 * n], refs[3 * n:4 * n]
        local_sems = refs[6 * n]
        x, y, c = _place()
        me = 4 * x + 2 * y + c
        local = [pltpu.make_async_copy(src_refs[i], land_refs[i].at[me], local_sems.at[i]) for i in range(n)]
        for cp in local:
            cp.start()
        for i in range(n):
            for k in range(N_DEV - 1):
                _peer_copy(True, src_refs[i], land_refs[i], send[i], recv[i], k, True).start()
        for cp in local:
            cp.wait()

    sem = pltpu.SemaphoreType.DMA((N_DEV - 1,))
    hbm = lambda a: pltpu.HBM(a.shape, a.dtype)
    res = pl.pallas_call(
        body,
        name=name,
        in_specs=[HBM] * (2 * n),
        out_specs=[SEM] * (2 * n) + [HBM] * (2 * n),
        out_shape=[sem] * (2 * n) + [hbm(s) for s in srcs] + [hbm(a) for a in lands],
        input_output_aliases={i: 2 * n + i for i in range(2 * n)},
        scratch_shapes=[pltpu.SemaphoreType.DMA((n,))],
        compiler_params=pltpu.CompilerParams(has_side_effects=EFFECT),
    )(*[pltpu.with_memory_space_constraint(a, pltpu.HBM) for a in list(srcs) + lands])
    return res[:n], res[n:2 * n], res[2 * n:3 * n], res[3 * n:4 * n]


def _exchange_wait(send_sems, recv_sems, src, land, after, gather, name):
    def body(src_ref, land_ref, send_ref, recv_ref, after_ref, src_out, land_out):
        for k in range(N_DEV - 1):
            cp = _peer_copy(gather, src_ref, land_ref, send_ref, recv_ref, k, False)
            cp.wait_send()
            cp.wait_recv()

    hbm = lambda a: pltpu.HBM(a.shape, a.dtype)
    return pl.pallas_call(
        body,
        name=name,
        in_specs=[HBM, HBM, SEM, SEM, ANY],
        out_specs=[HBM, HBM],
        out_shape=[hbm(src), hbm(land)],
        input_output_aliases={0: 0, 1: 1},
        compiler_params=pltpu.CompilerParams(has_side_effects=EFFECT),
    )(src, land, send_sems, recv_sems, after)[1]


def _col(v):
    return v.reshape(-1, 1).astype(F32)


def _to_internal_rows(w_t):
    pad = jnp.zeros((IN_PAD - IN_DIM, w_t.shape[1]), w_t.dtype)
    return jnp.concatenate([w_t[:ORIG_DT], w_t[ORIG_GA:ORIG_GS], w_t[ORIG_GS:IN_DIM], w_t[ORIG_DT:ORIG_GA], pad], axis=0)


def _from_internal_rows(g):
    return jnp.concatenate([g[:OFF_GA], g[OFF_DT:OFF_DT + N_SSD_HEADS], g[OFF_GA:OFF_GS], g[OFF_GS:OFF_DT]], axis=0)


def _local_step(xt, tgt, weight, small):
    t = xt.shape[1]
    n1 = _col(small["norm1_w"])
    n2 = _col(small["norm2_w"])
    nf = _col(small["final_norm_w"])
    bg = _col(small["b_gate"])
    sinks = small["attn_sinks"].reshape(-1).astype(F32)
    cbias = _col(small["ssd_conv_b"])
    dtb = _col(small["dt_bias"])
    alog = _col(small["a_log"])
    dsk = _col(small["d_skip"])
    gnw = _col(small["ssd_norm_w"])
    fb = small["ffn_conv_b"].reshape(2, D_FF, 1)

    xn = _norm_fwd(xt, n1, "norm1_fwd")
    cw = weight("ssd_conv_w", xn).T
    fw = weight("ffn_conv_w", xn).T.reshape(2, D_FF, FFN_CONV)
    w_in_t = weight("w_in", xn)
    proj = _matmul(w_in_t, xn, nt=False, out_dtype=F32, name="mm_in")
    ao, lse = _attn_fwd(proj, sinks)
    w_ao = weight("w_attn_o", ao)
    attn = _matmul(w_ao, ao, nt=False, out_dtype=F32, name="mm_attn_o", tn_a=True)
    xbc = _conv_silu_fwd(proj, cw, cbias)
    y, hst = _ssd_fwd(xbc, proj, dtb, alog, dsk)
    yn = _gnorm_fwd(y, proj, gnw)
    w_so = weight("w_ssd_o", yn)
    ssd = _matmul(w_so, yn, nt=False, out_dtype=F32, name="mm_ssd_o", tn_a=True)
    mix = _gate_fwd(proj, bg, attn, ssd)
    w_out = weight("w_out", mix)
    h1 = _matmul(w_out, mix, nt=False, out_dtype=F32, name="mm_out", add=xt, tn_a=True)
    hn = _norm_fwd(h1, n2, "norm2_fwd")
    w_up_t = weight("w_up", hn)
    u0 = _matmul(w_up_t, hn, nt=False, out_dtype=F32, name="mm_up").reshape(2, D_FF, t)
    gl = _ffn_fwd(u0, fw, fb)
    w_down = weight("w_down", gl)
    h2 = _matmul(w_down, gl, nt=False, out_dtype=F32, name="mm_down", add=h1, tn_a=True)
    dh2, loss, d_nf = _final_norm_loss(h2, tgt, nf)

    g = {}
    handles = {}

    def sending(weight_name, grad, *args, **kwargs):
        out, handles[weight_name] = _matmul(*args, send=grad.reshape(N_DEV, -1, D_MODEL), **kwargs)
        return out

    g_down = _matmul(gl, dh2, nt=True, out_dtype=BF16, name="mm_d_w_down")
    dgl = sending("w_down", g_down, w_down, dh2, nt=False, out_dtype=F32, name="mm_d_glu")
    du0, d_fwb = _ffn_bwd(u0, fw, fb, dgl)
    du0 = du0.reshape(2 * D_FF, t)
    g_up = _matmul(du0, hn, nt=True, out_dtype=BF16, name="mm_d_w_up")
    dhn = sending("w_up", g_up, w_up_t, du0, nt=False, out_dtype=F32, name="mm_d_hn", tn_a=True)
    dh1, d_n2 = _norm_bwd(dhn, h1, n2, dh2, "norm2_bwd")
    g_out = _matmul(mix, dh1, nt=True, out_dtype=BF16, name="mm_d_w_out")
    dmix = sending("w_out", g_out, w_out, dh1, nt=False, out_dtype=F32, name="mm_d_mix")
    d_attn, d_ssd, d_ga, d_gs, d_ba, d_bs = _gate_bwd(proj, bg, attn, ssd, dmix)
    g_ao = _matmul(ao, d_attn, nt=True, out_dtype=BF16, name="mm_d_w_attn_o")
    dao = sending("w_attn_o", g_ao, w_ao, d_attn, nt=False, out_dtype=F32, name="mm_d_ao")
    dq, dk, dv, d_sinks = _attn_bwd(proj, sinks, ao, lse, dao)
    g_so = _matmul(yn, d_ssd, nt=True, out_dtype=BF16, name="mm_d_w_ssd_o")
    dyn = sending("w_ssd_o", g_so, w_so, d_ssd, nt=False, out_dtype=F32, name="mm_d_yn")
    dy, dz, d_gnw = _gnorm_bwd(dyn, y, proj, gnw)
    dxs, dbm, dcm, ddt, d_alog, d_dsk, d_dtb = _ssd_bwd(xbc, proj, dtb, alog, dsk, hst, dy)
    dx_xs, dwb_xs = _conv_silu_bwd(proj, cw, cbias, dxs, 0, "ssd_conv_bwd_x")
    dx_b, dwb_b = _conv_silu_bwd(proj, cw, cbias, dbm, D_INNER, "ssd_conv_bwd_b")
    dx_c, dwb_c = _conv_silu_bwd(proj, cw, cbias, dcm, D_INNER + BC_DIM, "ssd_conv_bwd_c")
    dwb_conv = jnp.concatenate([dwb_xs, dwb_b, dwb_c], axis=0)
    ddt_rows = jnp.concatenate([ddt.astype(BF16), jnp.zeros((IN_PAD - OFF_DT - N_SSD_HEADS, t), BF16)], axis=0)
    dproj = jnp.concatenate([dq, dk, dv, dz, dx_xs, dx_b, dx_c, d_ga, d_gs, ddt_rows], axis=0)
    g_in = _from_internal_rows(_matmul(dproj, xn, nt=True, out_dtype=BF16, name="mm_d_w_in"))
    dxn = sending("w_in", g_in, w_in_t, dproj, nt=False, out_dtype=F32, name="mm_d_xn", tn_a=True)
    dx, d_n1 = _norm_bwd(dxn, xt, n1, dh1, "norm1_bwd")

    g["norm1_w"] = d_n1
    g["b_gate"] = jnp.concatenate([d_ba, d_bs], axis=0)
    g["attn_sinks"] = d_sinks
    g["ssd_conv_w"] = dwb_conv[:, :SSD_CONV].T
    g["ssd_conv_b"] = dwb_conv[:, SSD_CONV]
    g["dt_bias"] = d_dtb
    g["a_log"] = d_alog
    g["d_skip"] = d_dsk
    g["ssd_norm_w"] = d_gnw
    g["norm2_w"] = d_n2
    d_fwb = d_fwb.reshape(2 * D_FF, 128)
    g["ffn_conv_w"] = d_fwb[:, :FFN_CONV].T
    g["ffn_conv_b"] = d_fwb[:, FFN_CONV]
    g["final_norm_w"] = d_nf
    return loss, dx, g, handles


SHARDED = ("w_in", "w_attn_o", "w_ssd_o", "w_out", "w_up", "w_down")
SMALL = ("norm1_w", "b_gate", "attn_sinks", "ssd_conv_w", "ssd_conv_b", "dt_bias", "a_log", "d_skip", "ssd_norm_w",
         "norm2_w", "ffn_conv_w", "ffn_conv_b", "final_norm_w")
SMALL_SHAPES = {"norm1_w": (1, D_MODEL), "b_gate": (1, 2 * D_MODEL), "attn_sinks": (1, N_Q_HEADS),
                "ssd_conv_w": (1, SSD_CONV, XBC_DIM), "ssd_conv_b": (1, XBC_DIM), "dt_bias": (1, N_SSD_HEADS),
                "a_log": (1, N_SSD_HEADS), "d_skip": (1, N_SSD_HEADS), "ssd_norm_w": (1, D_INNER),
                "norm2_w": (1, D_MODEL), "ffn_conv_w": (1, FFN_CONV, 2 * D_FF), "ffn_conv_b": (1, 2 * D_FF),
                "final_norm_w": (D_MODEL,)}
WEIGHT_ORDER = ("norm1_w", "w_in", "b_gate", "attn_sinks", "w_attn_o", "ssd_conv_w", "ssd_conv_b", "dt_bias", "a_log",
                "d_skip", "ssd_norm_w", "w_ssd_o", "w_out", "norm2_w", "w_up", "ffn_conv_w", "ffn_conv_b", "w_down",
                "final_norm_w")


def _pack(parts):
    flat = jnp.concatenate([p.reshape(-1).astype(F32) for p in parts])
    rows = -(-flat.shape[0] // 1024) * 8
    return jnp.pad(flat, (0, rows * 128 - flat.shape[0])).reshape(rows, 128)


def _unpack(packed, shapes):
    flat = packed.reshape(-1)
    out, pos = [], 0
    for shp in shapes:
        size = 1
        for d in shp:
            size *= d
        out.append(flat[pos:pos + size].reshape(shp))
        pos += size
    return out


def kernel(x, norm1_w, w_in, b_gate, attn_sinks, w_attn_o, ssd_conv_w, ssd_conv_b, dt_bias, a_log, d_skip, ssd_norm_w, w_ssd_o, w_out, norm2_w, w_up, ffn_conv_w, ffn_conv_b, w_down, final_norm_w, loss_target, m_norm1_w, m_w_in, m_b_gate, m_attn_sinks, m_w_attn_o, m_ssd_conv_w, m_ssd_conv_b, m_dt_bias, m_a_log, m_d_skip, m_ssd_norm_w, m_w_ssd_o, m_w_out, m_norm2_w, m_w_up, m_ffn_conv_w, m_ffn_conv_b, m_w_down, m_final_norm_w, v_norm1_w, v_w_in, v_b_gate, v_attn_sinks, v_w_attn_o, v_ssd_conv_w, v_ssd_conv_b, v_dt_bias, v_a_log, v_d_skip, v_ssd_norm_w, v_w_ssd_o, v_w_out, v_norm2_w, v_w_up, v_ffn_conv_w, v_ffn_conv_b, v_w_down, v_final_norm_w):
    w = dict(norm1_w=norm1_w, w_in=w_in, b_gate=b_gate, attn_sinks=attn_sinks, w_attn_o=w_attn_o, ssd_conv_w=ssd_conv_w, ssd_conv_b=ssd_conv_b, dt_bias=dt_bias, a_log=a_log, d_skip=d_skip, ssd_norm_w=ssd_norm_w, w_ssd_o=w_ssd_o, w_out=w_out, norm2_w=norm2_w, w_up=w_up, ffn_conv_w=ffn_conv_w, ffn_conv_b=ffn_conv_b, w_down=w_down, final_norm_w=final_norm_w)
    m = dict(norm1_w=m_norm1_w, w_in=m_w_in, b_gate=m_b_gate, attn_sinks=m_attn_sinks, w_attn_o=m_w_attn_o, ssd_conv_w=m_ssd_conv_w, ssd_conv_b=m_ssd_conv_b, dt_bias=m_dt_bias, a_log=m_a_log, d_skip=m_d_skip, ssd_norm_w=m_ssd_norm_w, w_ssd_o=m_w_ssd_o, w_out=m_w_out, norm2_w=m_norm2_w, w_up=m_w_up, ffn_conv_w=m_ffn_conv_w, ffn_conv_b=m_ffn_conv_b, w_down=m_w_down, final_norm_w=m_final_norm_w)
    v = dict(norm1_w=v_norm1_w, w_in=v_w_in, b_gate=v_b_gate, attn_sinks=v_attn_sinks, w_attn_o=v_w_attn_o, ssd_conv_w=v_ssd_conv_w, ssd_conv_b=v_ssd_conv_b, dt_bias=v_dt_bias, a_log=v_a_log, d_skip=v_d_skip, ssd_norm_w=v_ssd_norm_w, w_ssd_o=v_w_ssd_o, w_out=v_w_out, norm2_w=v_norm2_w, w_up=v_w_up, ffn_conv_w=v_ffn_conv_w, ffn_conv_b=v_ffn_conv_b, w_down=v_w_down, final_norm_w=v_final_norm_w)
    me = 4 * lax.axis_index("x") + 2 * lax.axis_index("y") + lax.axis_index("c")
    conv_cols = XBC_DIM // N_DEV
    ffn_cols = 2 * D_FF // N_DEV

    shards = {"ssd_conv_w": ssd_conv_w[0], "ffn_conv_w": ffn_conv_w[0], "w_in": w_in[0].T.astype(BF16),
              "w_attn_o": w_attn_o[0].astype(BF16), "w_ssd_o": w_ssd_o[0].astype(BF16), "w_out": w_out[0].astype(BF16),
              "w_up": w_up[0].T.astype(BF16), "w_down": w_down[0].astype(BF16)}
    order = list(shards)
    g_send, g_recv, g_src, g_land = _gather_start(list(shards.values()), "gather_start")

    def weight(name, after):
        i = order.index(name)
        land = _exchange_wait(g_send[i], g_recv[i], g_src[i], g_land[i], after, True, "gather_wait_" + name)
        if name == "ssd_conv_w":
            return jnp.transpose(land, (1, 0, 2)).reshape(SSD_CONV, XBC_DIM)
        if name == "ffn_conv_w":
            return jnp.transpose(land, (1, 0, 2)).reshape(FFN_CONV, 2 * D_FF)
        if name == "w_in":
            return _to_internal_rows(land.reshape(IN_DIM, D_MODEL))
        return land.reshape(-1, D_MODEL)

    small = {k: w[k][0] if k != "final_norm_w" else w[k] for k in SMALL}
    loss, dx, g, pending = _local_step(x[0].T, loss_target[0].T, weight, small)

    packed = _pack([loss] + [g[k] for k in SMALL])
    total = _sum_slots(_all_gather([packed], "gather_small_grads")[0])
    tot = _unpack(total, [(1,)] + [SMALL_SHAPES[k] for k in SMALL])
    loss_sum = tot[0].reshape(())
    gs = dict(zip(SMALL, tot[1:]))
    gs["ssd_conv_w"] = lax.dynamic_slice_in_dim(gs["ssd_conv_w"], me * conv_cols, conv_cols, axis=2)
    gs["ffn_conv_w"] = lax.dynamic_slice_in_dim(gs["ffn_conv_w"], me * ffn_cols, ffn_cols, axis=2)
    upd = _adamw_small(_pack([gs[k] for k in SMALL]), _pack([w[k] for k in SMALL]), _pack([m[k] for k in SMALL]),
                       _pack([v[k] for k in SMALL]))
    shapes = [w[k].shape for k in SMALL]
    d_s, m_s, v_s = (dict(zip(SMALL, _unpack(u, shapes))) for u in upd)
    res = {}
    for k in SMALL:
        res[k] = (gs[k], d_s[k], m_s[k], v_s[k])

    after = upd[0]
    for name in ("w_down", "w_up", "w_out", "w_attn_o", "w_ssd_o", "w_in"):
        parts = _exchange_wait(*pending[name], after, False, "grad_wait_" + name)
        view = (lambda a: a[0].T) if name in ("w_in", "w_up") else (lambda a: a[0])
        res[name] = _adamw_sharded(parts, view(w[name]), view(m[name]), view(v[name]), "adamw_" + name)
        after = res[name][0]
        if name in ("w_in", "w_up"):
            res[name] = [r.T for r in res[name]]

    grad_x = dx.T[None]
    outs = [loss_sum, grad_x]
    for i in range(4):
        for k in WEIGHT_ORDER:
            r = res[k][i]
            outs.append(r[None] if k in SHARDED else r)
    return tuple(outs)
```

```python
import functools

import jax
import jax.numpy as jnp
from jax import lax
from jax.experimental import pallas as pl
from jax.experimental.pallas import tpu as pltpu

F32 = jnp.float32
BF16 = jnp.bfloat16
HIGHEST = lax.Precision.HIGHEST

D_MODEL = 1024
N_Q_HEADS = 16
N_KV_HEADS = 4
HEAD_DIM = 64
WINDOW = 128
Q_PER_KV = N_Q_HEADS // N_KV_HEADS
Q_DIM = N_Q_HEADS * HEAD_DIM
KV_DIM = N_KV_HEADS * HEAD_DIM
D_INNER = 2048
SSD_HEAD_DIM = 64
N_SSD_HEADS = 32
N_SSD_GROUPS = 4
HEADS_PER_GROUP = N_SSD_HEADS // N_SSD_GROUPS
D_STATE = 128
BC_DIM = N_SSD_GROUPS * D_STATE
XBC_DIM = D_INNER + 2 * BC_DIM
SSD_CONV = 4
CHUNK = 128
D_FF = 2816
FFN_CONV = 3
EPS = 1e-5
NEG = -1e30
IN_DIM = 8736
N_DEV = 8

OFF_Q = 0
OFF_K = OFF_Q + Q_DIM
OFF_V = OFF_K + KV_DIM
OFF_Z = OFF_V + KV_DIM
OFF_X = OFF_Z + D_INNER
OFF_DT = OFF_X + XBC_DIM
OFF_GA = OFF_DT + N_SSD_HEADS
OFF_GS = OFF_GA + D_MODEL

ADAM_LR = 0.001
ADAM_B1 = 0.9
ADAM_B2 = 0.999
ADAM_EPS = 1e-08
ADAM_WD = 0.01
ADAM_STEP = 10

VMEM_LIMIT = 48 * 1024 * 1024
MESH = pl.DeviceIdType.MESH


def _cparams(*sem):
    return pltpu.CompilerParams(dimension_semantics=sem, vmem_limit_bytes=VMEM_LIMIT)


def _tile(n, prefs):
    for p in prefs:
        if n % p == 0:
            return p
    return n


def _sigmoid(x):
    return 1.0 / (1.0 + jnp.exp(-x))


def _softplus(x):
    return jnp.maximum(x, 0.0) + jnp.log(1.0 + jnp.exp(-jnp.abs(x)))


def _rowsum(x):
    return jnp.sum(x, axis=1, keepdims=True)


def _colsum(x):
    return jnp.sum(x, axis=0, keepdims=True)


def _dot(a, b):
    return jnp.dot(a, b, preferred_element_type=F32)


def _dot_nt(a, b):
    return lax.dot_general(a, b, (((1,), (1,)), ((), ())), preferred_element_type=F32)


def _dot_tn(a, b):
    return lax.dot_general(a, b, (((0,), (0,)), ((), ())), preferred_element_type=F32)


def _shift_right(x, j):
    if j == 0:
        return x
    r = pltpu.roll(x, j, 1)
    lane = lax.broadcasted_iota(jnp.int32, (x.shape[0], 128), 1)
    return jnp.concatenate([jnp.where(lane >= j, r[:, :128], 0.0), r[:, 128:]], axis=1)


def _shift_left(x, j):
    if j == 0:
        return x
    n = x.shape[1]
    r = pltpu.roll(x, n - j, 1)
    lane = lax.broadcasted_iota(jnp.int32, (x.shape[0], 128), 1)
    return jnp.concatenate([r[:, :n - 128], jnp.where(lane < 128 - j, r[:, n - 128:], 0.0)], axis=1)


def _causal_conv(xv, wv, bv):
    taps = wv.shape[1]
    shifted = [_shift_right(xv, taps - 1 - k) for k in range(taps - 1)]
    y = bv + wv[:, taps - 1:taps] * xv
    for k in range(taps - 1):
        y = y + wv[:, k:k + 1] * shifted[k]
    return y, shifted


def _causal_conv_bwd(dy, xv, shifted, wv):
    taps = wv.shape[1]
    lane = lax.broadcasted_iota(jnp.int32, (dy.shape[0], 128), 1)
    dwb = jnp.where(lane == taps, _rowsum(dy), 0.0)
    dwb = jnp.where(lane == taps - 1, _rowsum(dy * xv), dwb)
    dx = wv[:, taps - 1:taps] * dy
    for k in range(taps - 1):
        dx = dx + wv[:, k:k + 1] * _shift_left(dy, taps - 1 - k)
        dwb = jnp.where(lane == k, _rowsum(dy * shifted[k]), dwb)
    return dx, dwb


MATMUL_VMEM_BUDGET = 36 * 1024 * 1024
MATMUL_MAX_TK = 3072


MATMUL_MAX_TM = 768


def _largest_tile(n, align, cap):
    return max(d for d in range(align, min(n, cap) + 1, align) if n % d == 0)


def _matmul_tiles(m, n, k, a_bytes, b_bytes, out_bytes, has_add, m_align, k_align):
    tm = _largest_tile(m, m_align, MATMUL_MAX_TM)
    tk = _largest_tile(k, k_align, MATMUL_MAX_TK)
    for tn in sorted({d for d in range(128, n + 1, 128) if n % d == 0}, reverse=True):
        need = 2 * (tm * tk * a_bytes + tk * tn * b_bytes) + tm * tn * (2 * out_bytes + (4 if k > tk else 0) + (8 if has_add else 0))
        if tn <= 3072 and need <= MATMUL_VMEM_BUDGET:
            return tm, tn, tk
    return tm, 128, tk


def _matmul(a, b, *, nt, out_dtype, name, add=None, tn_a=False, send=None):
    if tn_a:
        k, m = a.shape
    else:
        m, k = a.shape
    n = b.shape[0] if nt else b.shape[1]
    tm, tn, tk = _matmul_tiles(m, n, k, a.dtype.itemsize, b.dtype.itemsize, jnp.dtype(out_dtype).itemsize, add is not None,
                               128 if tn_a else 16, 16 if tn_a and not nt else 128)
    nk = k // tk
    grid = (m // tm, n // tn, nk)

    def body(a_ref, b_ref, *rest):
        r_ref = None
        if add is not None:
            r_ref, rest = rest[0], rest[1:]
        if send is not None:
            src_ref, land_ref, rest = rest[0], rest[1], rest[2:]
            send_sems, recv_sems, local_sem = rest[1], rest[2], rest[-1]
            rest = (rest[0],) + rest[5:-1]
            x, y, c = _place()
            me = 4 * x + 2 * y + c
            local = pltpu.make_async_copy(src_ref.at[me], land_ref.at[me], local_sem)
            step = (pl.program_id(0) * grid[1] + pl.program_id(1)) * grid[2] + pl.program_id(2)

            @pl.when(step == 0)
            def _():
                local.start()
                for peer in range(N_DEV - 1):
                    _peer_copy(False, src_ref, land_ref, send_sems, recv_sems, peer, True).start()

            @pl.when(step == grid[0] * grid[1] * grid[2] - 1)
            def _():
                local.wait()

        o_ref = rest[0]
        av = a_ref[...].astype(BF16)
        bv = b_ref[...].astype(BF16)
        part = _dot_tn(av, bv) if tn_a else _dot_nt(av, bv) if nt else _dot(av, bv)

        def finish(r):
            if add is not None:
                r = r + r_ref[...]
            o_ref[...] = r.astype(out_dtype)

        if nk == 1:
            finish(part)
            return
        acc = rest[1]
        kk = pl.program_id(2)

        @pl.when(kk == 0)
        def _():
            acc[...] = part

        @pl.when((kk > 0) & (kk < nk - 1))
        def _():
            acc[...] += part

        @pl.when(kk == nk - 1)
        def _():
            finish(acc[...] + part)

    in_specs = [
        pl.BlockSpec((tk, tm), lambda i, j, kk: (kk, i)) if tn_a else pl.BlockSpec((tm, tk), lambda i, j, kk: (i, kk)),
        pl.BlockSpec((tn, tk), lambda i, j, kk: (j, kk)) if nt else pl.BlockSpec((tk, tn), lambda i, j, kk: (kk, j)),
    ]
    args = [a, b]
    if add is not None:
        in_specs.append(pl.BlockSpec((tm, tn), lambda i, j, kk: (i, j)))
        args.append(add)
    out_specs = [pl.BlockSpec((tm, tn), lambda i, j, kk: (i, j))]
    out_shape = [jax.ShapeDtypeStruct((m, n), out_dtype)]
    scratch = [pltpu.VMEM((tm, tn), F32)] if nk > 1 else []
    if send is None:
        return pl.pallas_call(
            body, name=name, grid=grid, in_specs=in_specs, out_specs=out_specs[0], out_shape=out_shape[0],
            scratch_shapes=scratch, compiler_params=_cparams("parallel", "parallel", "arbitrary"),
        )(*args)
    sem = pltpu.SemaphoreType.DMA((N_DEV - 1,))
    hbm = pltpu.HBM(send.shape, send.dtype)
    first = len(args)
    res = pl.pallas_call(
        body, name=name, grid=grid,
        in_specs=in_specs + [HBM, HBM],
        out_specs=out_specs + [SEM, SEM, HBM, HBM],
        out_shape=out_shape + [sem, sem, hbm, hbm],
        input_output_aliases={first: 3, first + 1: 4},
        scratch_shapes=scratch + [pltpu.SemaphoreType.DMA(())],
        compiler_params=pltpu.CompilerParams(dimension_semantics=("arbitrary",) * 3, vmem_limit_bytes=VMEM_LIMIT,
                                             has_side_effects=EFFECT),
    )(*args, pltpu.with_memory_space_constraint(send, pltpu.HBM),
      pltpu.with_memory_space_constraint(lax.empty(send.shape, send.dtype), pltpu.HBM))
    return res[0], tuple(res[1:])


def _norm_fwd(x, w_col, name):
    f, t = x.shape
    tt = _tile(t, (512, 256, 128))

    def body(x_ref, w_ref, o_ref):
        xv = x_ref[...]
        r = lax.rsqrt(jnp.mean(xv * xv, axis=0, keepdims=True) + EPS)
        o_ref[...] = (xv * r * w_ref[...]).astype(BF16)

    return pl.pallas_call(
        body,
        name=name,
        grid=(t // tt,),
        in_specs=[pl.BlockSpec((f, tt), lambda i: (0, i)), pl.BlockSpec((f, 1), lambda i: (0, 0))],
        out_specs=pl.BlockSpec((f, tt), lambda i: (0, i)),
        out_shape=jax.ShapeDtypeStruct((f, t), BF16),
        compiler_params=_cparams("parallel"),
    )(x, w_col)


def _norm_bwd(dy, x, w_col, res, name):
    f, t = x.shape
    tt = _tile(t, (512, 256, 128))

    def body(dy_ref, x_ref, w_ref, res_ref, dx_ref, dw_ref):
        @pl.when(pl.program_id(0) == 0)
        def _():
            dw_ref[...] = jnp.zeros_like(dw_ref)

        xv = x_ref[...]
        r = lax.rsqrt(jnp.mean(xv * xv, axis=0, keepdims=True) + EPS)
        xhat = xv * r
        dyv = dy_ref[...]
        dw_ref[...] += _rowsum(dyv * xhat)
        dxhat = dyv * w_ref[...]
        dx_ref[...] = res_ref[...] + r * (dxhat - xhat * jnp.mean(dxhat * xhat, axis=0, keepdims=True))

    blk = pl.BlockSpec((f, tt), lambda i: (0, i))
    col = pl.BlockSpec((f, 1), lambda i: (0, 0))
    return pl.pallas_call(
        body,
        name=name,
        grid=(t // tt,),
        in_specs=[blk, blk, col, blk],
        out_specs=[blk, col],
        out_shape=[jax.ShapeDtypeStruct((f, t), F32), jax.ShapeDtypeStruct((f, 1), F32)],
        compiler_params=_cparams("arbitrary"),
    )(dy, x, w_col, res)


def _final_norm_loss(h, tgt, w_col):
    f, t = h.shape
    tt = _tile(t, (512, 256, 128))

    def body(h_ref, t_ref, w_ref, dh_ref, loss_ref, dw_ref):
        @pl.when(pl.program_id(0) == 0)
        def _():
            dw_ref[...] = jnp.zeros_like(dw_ref)
            loss_ref[...] = jnp.zeros_like(loss_ref)

        xv = h_ref[...]
        r = lax.rsqrt(jnp.mean(xv * xv, axis=0, keepdims=True) + EPS)
        xhat = xv * r
        wv = w_ref[...]
        err = xhat * wv - t_ref[...]
        loss_ref[...] += 0.5 * _rowsum(jnp.mean(err * err, axis=0, keepdims=True))
        dyv = err * (1.0 / f)
        dw_ref[...] += _rowsum(dyv * xhat)
        dxhat = dyv * wv
        dh_ref[...] = r * (dxhat - xhat * jnp.mean(dxhat * xhat, axis=0, keepdims=True))

    blk = pl.BlockSpec((f, tt), lambda i: (0, i))
    col = pl.BlockSpec((f, 1), lambda i: (0, 0))
    one = pl.BlockSpec((1, 1), lambda i: (0, 0))
    return pl.pallas_call(
        body,
        name="final_norm_loss",
        grid=(t // tt,),
        in_specs=[blk, blk, col],
        out_specs=[blk, one, col],
        out_shape=[jax.ShapeDtypeStruct((f, t), F32), jax.ShapeDtypeStruct((1, 1), F32), jax.ShapeDtypeStruct((f, 1), F32)],
        compiler_params=_cparams("arbitrary"),
    )(h, tgt, w_col)


def _attn_mask(n):
    shape = (2 * WINDOW, Q_PER_KV * WINDOW)
    si = lax.broadcasted_iota(jnp.int32, shape, 0)
    qi = lax.broadcasted_iota(jnp.int32, shape, 1) & (WINDOW - 1)
    dist = WINDOW + qi - si
    return (dist >= 0) & (dist < WINDOW) & ((si >= WINDOW) | (n > 0))


def _lane_cat(ref, row0, rows):
    return jnp.concatenate([ref[row0 + i * rows:row0 + (i + 1) * rows, :] for i in range(Q_PER_KV)], axis=1)


def _attn_fwd(proj, sinks):
    t = proj.shape[1]
    nb = t // WINDOW
    scale = HEAD_DIM ** -0.5

    def body(s_ref, q_ref, kc_ref, kp_ref, vc_ref, vp_ref, o_ref, lse_ref):
        n = pl.program_id(0)
        valid = _attn_mask(n)
        for g in range(N_KV_HEADS):
            rows = slice(g * HEAD_DIM, (g + 1) * HEAD_DIM)
            kt = jnp.concatenate([kp_ref[rows, :], kc_ref[rows, :]], axis=1).astype(BF16)
            vt = jnp.concatenate([vp_ref[rows, :], vc_ref[rows, :]], axis=1).astype(BF16)
            qcat = _lane_cat(q_ref, g * Q_PER_KV * HEAD_DIM, HEAD_DIM).astype(BF16)
            s = jnp.where(valid, _dot_tn(kt, qcat) * scale, NEG)
            sink = jnp.concatenate(
                [jnp.full((1, WINDOW), s_ref[g * Q_PER_KV + i], F32) for i in range(Q_PER_KV)], axis=1)
            m = jnp.maximum(jnp.max(s, axis=0, keepdims=True), sink)
            p = jnp.where(valid, jnp.exp(s - m), 0.0)
            denom = _colsum(p) + jnp.exp(sink - m)
            probs = (p / denom).astype(BF16)
            out = _dot(vt, probs)
            lse = m + jnp.log(denom)
            for i in range(Q_PER_KV):
                h = g * Q_PER_KV + i
                o_ref[h * HEAD_DIM:(h + 1) * HEAD_DIM, :] = out[:, i * WINDOW:(i + 1) * WINDOW]
                lse_ref[h:h + 1, :] = lse[:, i * WINDOW:(i + 1) * WINDOW]

    kb = OFF_K // KV_DIM
    vb = OFF_V // KV_DIM
    prev = lambda n: jnp.maximum(n - 1, 0)
    return pl.pallas_call(
        body,
        name="attn_fwd",
        grid=(nb,),
        in_specs=[
            pl.BlockSpec(memory_space=pltpu.SMEM),
            pl.BlockSpec((Q_DIM, WINDOW), lambda n: (0, n)),
            pl.BlockSpec((KV_DIM, WINDOW), lambda n: (kb, n)),
            pl.BlockSpec((KV_DIM, WINDOW), lambda n: (kb, prev(n))),
            pl.BlockSpec((KV_DIM, WINDOW), lambda n: (vb, n)),
            pl.BlockSpec((KV_DIM, WINDOW), lambda n: (vb, prev(n))),
        ],
        out_specs=[pl.BlockSpec((Q_DIM, WINDOW), lambda n: (0, n)), pl.BlockSpec((N_Q_HEADS, WINDOW), lambda n: (0, n))],
        out_shape=[jax.ShapeDtypeStruct((Q_DIM, t), F32), jax.ShapeDtypeStruct((N_Q_HEADS, t), F32)],
        compiler_params=_cparams("parallel"),
    )(sinks, proj, proj, proj, proj, proj)


def _attn_bwd(proj, sinks, out, lse, dout):
    t = proj.shape[1]
    nb = t // WINDOW
    scale = HEAD_DIM ** -0.5

    def body(s_ref, q_ref, kc_ref, kp_ref, vc_ref, vp_ref, o_ref, lse_ref, do_ref,
             dq_ref, dk_ref, dv_ref, ds_ref, dk_carry, dv_carry):
        step = pl.program_id(0)
        n = nb - 1 - step

        @pl.when(step == 0)
        def _():
            dk_carry[...] = jnp.zeros_like(dk_carry)
            dv_carry[...] = jnp.zeros_like(dv_carry)
            ds_ref[...] = jnp.zeros_like(ds_ref)

        valid = _attn_mask(n)
        for g in range(N_KV_HEADS):
            rows = slice(g * HEAD_DIM, (g + 1) * HEAD_DIM)
            q0 = g * Q_PER_KV * HEAD_DIM
            kt = jnp.concatenate([kp_ref[rows, :], kc_ref[rows, :]], axis=1).astype(BF16)
            vt = jnp.concatenate([vp_ref[rows, :], vc_ref[rows, :]], axis=1).astype(BF16)
            qcat = _lane_cat(q_ref, q0, HEAD_DIM).astype(BF16)
            ocat = _lane_cat(o_ref, q0, HEAD_DIM)
            docat = _lane_cat(do_ref, q0, HEAD_DIM)
            dob = docat.astype(BF16)
            lse_cat = jnp.concatenate(
                [lse_ref[g * Q_PER_KV + i:g * Q_PER_KV + i + 1, :] for i in range(Q_PER_KV)], axis=1)
            sink = jnp.concatenate(
                [jnp.full((1, WINDOW), s_ref[g * Q_PER_KV + i], F32) for i in range(Q_PER_KV)], axis=1)
            s = jnp.where(valid, _dot_tn(kt, qcat) * scale, NEG)
            p = jnp.where(valid, jnp.exp(s - lse_cat), 0.0)
            dp = _dot_tn(vt, dob)
            delta = _colsum(docat * ocat)
            dsc = (p * (dp - delta)).astype(BF16)
            dsink_row = -jnp.exp(sink - lse_cat) * delta
            dq = _dot(kt, dsc) * scale
            dk = _dot_nt(qcat, dsc) * scale
            dv = _dot_nt(dob, p.astype(BF16))
            for i in range(Q_PER_KV):
                h = g * Q_PER_KV + i
                dq_ref[h * HEAD_DIM:(h + 1) * HEAD_DIM, :] = dq[:, i * WINDOW:(i + 1) * WINDOW].astype(BF16)
                ds_ref[h:h + 1, :] += _rowsum(dsink_row[:, i * WINDOW:(i + 1) * WINDOW])
            dk_ref[rows, :] = (dk[:, WINDOW:] + dk_carry[rows, :]).astype(BF16)
            dv_ref[rows, :] = (dv[:, WINDOW:] + dv_carry[rows, :]).astype(BF16)
            dk_carry[rows, :] = dk[:, :WINDOW]
            dv_carry[rows, :] = dv[:, :WINDOW]

    kb = OFF_K // KV_DIM
    vb = OFF_V // KV_DIM
    cur = lambda i: nb - 1 - i
    prev = lambda i: jnp.maximum(nb - 2 - i, 0)
    qspec = pl.BlockSpec((Q_DIM, WINDOW), lambda i: (0, cur(i)))
    kvspec = pl.BlockSpec((KV_DIM, WINDOW), lambda i: (0, cur(i)))
    return pl.pallas_call(
        body,
        name="attn_bwd",
        grid=(nb,),
        in_specs=[
            pl.BlockSpec(memory_space=pltpu.SMEM),
            qspec,
            pl.BlockSpec((KV_DIM, WINDOW), lambda i: (kb, cur(i))),
            pl.BlockSpec((KV_DIM, WINDOW), lambda i: (kb, prev(i))),
            pl.BlockSpec((KV_DIM, WINDOW), lambda i: (vb, cur(i))),
            pl.BlockSpec((KV_DIM, WINDOW), lambda i: (vb, prev(i))),
            qspec,
            pl.BlockSpec((N_Q_HEADS, WINDOW), lambda i: (0, cur(i))),
            qspec,
        ],
        out_specs=[qspec, kvspec, kvspec, pl.BlockSpec((N_Q_HEADS, 1), lambda i: (0, 0))],
        out_shape=[
            jax.ShapeDtypeStruct((Q_DIM, t), BF16),
            jax.ShapeDtypeStruct((KV_DIM, t), BF16),
            jax.ShapeDtypeStruct((KV_DIM, t), BF16),
            jax.ShapeDtypeStruct((N_Q_HEADS, 1), F32),
        ],
        scratch_shapes=[pltpu.VMEM((KV_DIM, WINDOW), F32), pltpu.VMEM((KV_DIM, WINDOW), F32)],
        compiler_params=_cparams("arbitrary"),
    )(sinks, proj, proj, proj, proj, proj, out, lse, dout)


CONV_ROWS = 256


def _conv_silu_fwd(proj, w_col, b_col):
    t = proj.shape[1]
    r0 = OFF_X // CONV_ROWS

    def body(x_ref, w_ref, b_ref, o_ref):
        def strip(rows):
            y, _ = _causal_conv(x_ref[rows, :], w_ref[rows, :], b_ref[rows, :])
            o_ref[rows, :] = y * _sigmoid(y)

        strip(slice(None))

    return pl.pallas_call(
        body,
        name="ssd_conv_fwd",
        grid=(XBC_DIM // CONV_ROWS,),
        in_specs=[
            pl.BlockSpec((CONV_ROWS, t), lambda i: (r0 + i, 0)),
            pl.BlockSpec((CONV_ROWS, SSD_CONV), lambda i: (i, 0)),
            pl.BlockSpec((CONV_ROWS, 1), lambda i: (i, 0)),
        ],
        out_specs=pl.BlockSpec((CONV_ROWS, t), lambda i: (i, 0)),
        out_shape=jax.ShapeDtypeStruct((XBC_DIM, t), F32),
        compiler_params=_cparams("parallel"),
    )(proj, w_col, b_col)


def _conv_silu_bwd(proj, w_col, b_col, dout, row0, name):
    t = proj.shape[1]
    nrows = dout.shape[0]
    p0 = (OFF_X + row0) // CONV_ROWS
    c0 = row0 // CONV_ROWS

    def body(x_ref, w_ref, b_ref, do_ref, dx_ref, dwb_ref):
        def strip(rows):
            xv = x_ref[rows, :]
            wv = w_ref[rows, :]
            y, shifted = _causal_conv(xv, wv, b_ref[rows, :])
            sg = _sigmoid(y)
            dy = do_ref[rows, :] * (sg * (1.0 + y * (1.0 - sg)))
            dx, dwb_ref[rows, :] = _causal_conv_bwd(dy, xv, shifted, wv)
            dx_ref[rows, :] = dx.astype(BF16)

        strip(slice(None))

    return pl.pallas_call(
        body,
        name=name,
        grid=(nrows // CONV_ROWS,),
        in_specs=[
            pl.BlockSpec((CONV_ROWS, t), lambda i: (p0 + i, 0)),
            pl.BlockSpec((CONV_ROWS, SSD_CONV), lambda i: (c0 + i, 0)),
            pl.BlockSpec((CONV_ROWS, 1), lambda i: (c0 + i, 0)),
            pl.BlockSpec((CONV_ROWS, t), lambda i: (i, 0)),
        ],
        out_specs=[pl.BlockSpec((CONV_ROWS, t), lambda i: (i, 0)), pl.BlockSpec((CONV_ROWS, 128), lambda i: (i, 0))],
        out_shape=[jax.ShapeDtypeStruct((nrows, t), BF16), jax.ShapeDtypeStruct((nrows, 128), F32)],
        compiler_params=_cparams("parallel"),
    )(proj, w_col, b_col, dout)


GROUP_ROWS = HEADS_PER_GROUP * SSD_HEAD_DIM


def _ssd_specs(nc, order):
    hb = D_INNER // D_STATE
    dtb = OFF_DT // HEADS_PER_GROUP
    col = pl.BlockSpec((HEADS_PER_GROUP, 1), lambda g, c: (g, 0))
    return [
        pl.BlockSpec((GROUP_ROWS, CHUNK), lambda g, c: (g, order(c))),
        pl.BlockSpec((D_STATE, CHUNK), lambda g, c: (hb + g, order(c))),
        pl.BlockSpec((D_STATE, CHUNK), lambda g, c: (hb + N_SSD_GROUPS + g, order(c))),
        pl.BlockSpec((HEADS_PER_GROUP, CHUNK), lambda g, c: (dtb + g, order(c))),
        col, col, col,
    ]


def _ssd_common(dt_ref, dtb_ref, alog_ref):
    z = dt_ref[...] + dtb_ref[...]
    dt = _softplus(z)
    a_neg = -jnp.exp(alog_ref[...])
    d_a = dt * a_neg
    row = lax.broadcasted_iota(jnp.int32, (CHUNK, CHUNK), 0)
    colm = lax.broadcasted_iota(jnp.int32, (CHUNK, CHUNK), 1)
    upper = (row <= colm).astype(F32)
    a_cs = jnp.dot(d_a, upper, precision=HIGHEST, preferred_element_type=F32)
    a_last = _rowsum(d_a)
    return z, dt, a_neg, a_cs, a_last, row >= colm, row == colm


def _decay(a_row, causal):
    a_s = jnp.broadcast_to(a_row, (CHUNK, CHUNK))
    seg = a_s.T - a_s
    return jnp.where(causal, jnp.exp(jnp.where(causal, seg, 0.0)), 0.0)


def _ssd_fwd(xbc, proj, dtb_col, alog_col, dsk_col):
    t = xbc.shape[1]
    nc = t // CHUNK

    def body(xs_ref, b_ref, c_ref, dt_ref, dtb_ref, alog_ref, dsk_ref, y_ref, hst_ref, h_scr):
        @pl.when(pl.program_id(1) == 0)
        def _():
            h_scr[...] = jnp.zeros_like(h_scr)

        _, dt, _, a_cs, a_last, causal, _ = _ssd_common(dt_ref, dtb_ref, alog_ref)
        bb = b_ref[...].astype(BF16)
        cb_ = c_ref[...].astype(BF16)
        cb = _dot_tn(cb_, bb)
        hst_ref[0, 0] = h_scr[...]
        dsk = dsk_ref[...]
        for j in range(HEADS_PER_GROUP):
            rows = slice(j * SSD_HEAD_DIM, (j + 1) * SSD_HEAD_DIM)
            a = a_cs[j:j + 1, :]
            m = (cb * _decay(a, causal)).astype(BF16)
            xs = xs_ref[rows, :]
            xc = xs * dt[j:j + 1, :]
            hj = h_scr[rows, :]
            y = _dot_nt(xc.astype(BF16), m) + _dot(hj.astype(BF16), cb_) * jnp.exp(a) + dsk[j:j + 1, :] * xs
            y_ref[rows, :] = y
            al = a_last[j:j + 1, :]
            w = jnp.exp(al - a)
            h_scr[rows, :] = jnp.exp(al) * hj + _dot_nt((xc * w).astype(BF16), bb)

    return pl.pallas_call(
        body,
        name="ssd_fwd",
        grid=(N_SSD_GROUPS, nc),
        in_specs=_ssd_specs(nc, lambda c: c),
        out_specs=[
            pl.BlockSpec((GROUP_ROWS, CHUNK), lambda g, c: (g, c)),
            pl.BlockSpec((1, 1, GROUP_ROWS, D_STATE), lambda g, c: (g, c, 0, 0)),
        ],
        out_shape=[
            jax.ShapeDtypeStruct((D_INNER, t), F32),
            jax.ShapeDtypeStruct((N_SSD_GROUPS, nc, GROUP_ROWS, D_STATE), F32),
        ],
        scratch_shapes=[pltpu.VMEM((GROUP_ROWS, D_STATE), F32)],
        compiler_params=_cparams("parallel", "arbitrary"),
    )(xbc, xbc, xbc, proj, dtb_col, alog_col, dsk_col)


def _ssd_bwd(xbc, proj, dtb_col, alog_col, dsk_col, hst, dy):
    t = xbc.shape[1]
    nc = t // CHUNK
    rev = lambda c: nc - 1 - c

    def body(xs_ref, b_ref, c_ref, dt_ref, dtb_ref, alog_ref, dsk_ref, hst_ref, dy_ref,
             dxs_ref, db_ref, dc_ref, ddt_ref, dalog_ref, ddsk_ref, ddtb_ref, dh_scr, da_scr, ddt_scr, dd_scr):
        @pl.when(pl.program_id(1) == 0)
        def _():
            dh_scr[...] = jnp.zeros_like(dh_scr)
            dalog_ref[...] = jnp.zeros_like(dalog_ref)
            ddsk_ref[...] = jnp.zeros_like(ddsk_ref)
            ddtb_ref[...] = jnp.zeros_like(ddtb_ref)

        z, dt, a_neg, a_cs, a_last, causal, eye = _ssd_common(dt_ref, dtb_ref, alog_ref)
        bb = b_ref[...].astype(BF16)
        cb_ = c_ref[...].astype(BF16)
        cb = _dot_tn(cb_, bb)
        dsk = dsk_ref[...]
        last_lane = lax.broadcasted_iota(jnp.int32, (1, CHUNK), 1) == CHUNK - 1
        dcb = jnp.zeros((CHUNK, CHUNK), F32)
        dc_acc = jnp.zeros((D_STATE, CHUNK), F32)
        db_acc = jnp.zeros((D_STATE, CHUNK), F32)
        for j in range(HEADS_PER_GROUP):
            rows = slice(j * SSD_HEAD_DIM, (j + 1) * SSD_HEAD_DIM)
            a = a_cs[j:j + 1, :]
            al = a_last[j:j + 1, :]
            lam = _decay(a, causal)
            mf = cb * lam
            xs = xs_ref[rows, :]
            dtj = dt[j:j + 1, :]
            xc = xs * dtj
            w = jnp.exp(al - a)
            e = jnp.exp(a)
            gam = jnp.exp(al)
            hj = hst_ref[0, 0, rows, :]
            hjb = hj.astype(BF16)
            dyv = dy_ref[rows, :]
            dyb = dyv.astype(BF16)
            dd_scr[j:j + 1, :] = _colsum(dyv * xs)
            gb = (dyv * e).astype(BF16)
            dh_in = _dot_nt(gb, cb_)
            dc_acc = dc_acc + _dot_tn(hjb, gb)
            yoff = _dot(hjb, cb_) * e
            da = _colsum(dyv * yoff)
            dm = _dot_tn(dyb, xc.astype(BF16))
            dxc = _dot(dyb, mf.astype(BF16))
            dcb = dcb + dm * lam
            nmat = dm * mf
            rs = jnp.broadcast_to(_rowsum(nmat), (CHUNK, CHUNK))
            da = da + _colsum(jnp.where(eye, rs, 0.0)) - _colsum(nmat)
            ds = dh_scr[rows, :]
            dsb = ds.astype(BF16)
            t1 = _dot(dsb, bb)
            xcw = xc * w
            dxc = dxc + w * t1
            dww = _colsum(xcw * t1)
            da_l = _rowsum(dww) + _rowsum(_colsum(ds * hj)) * gam
            da = da - dww + jnp.where(last_lane, da_l, 0.0)
            db_acc = db_acc + _dot_tn(dsb, xcw.astype(BF16))
            dh_scr[rows, :] = gam * ds + dh_in
            dxs_ref[rows, :] = dsk[j:j + 1, :] * dyv + dxc * dtj
            da_scr[j:j + 1, :] = da
            ddt_scr[j:j + 1, :] = _colsum(dxc * xs)
        dcbb = dcb.astype(BF16)
        dc_ref[...] = dc_acc + _dot_nt(bb, dcbb)
        db_ref[...] = db_acc + _dot(cb_, dcbb)
        dda = jnp.dot(da_scr[...], causal.astype(F32), precision=HIGHEST, preferred_element_type=F32)
        ddt = ddt_scr[...] + dda * a_neg
        ddt_raw = ddt * _sigmoid(z)
        ddt_ref[...] = ddt_raw
        ddtb_ref[...] += _rowsum(ddt_raw)
        dalog_ref[...] += _rowsum(dda * dt) * a_neg
        ddsk_ref[...] += _rowsum(dd_scr[...])

    col = pl.BlockSpec((HEADS_PER_GROUP, 1), lambda g, c: (g, 0))
    bc = pl.BlockSpec((D_STATE, CHUNK), lambda g, c: (g, rev(c)))
    xs_spec = pl.BlockSpec((GROUP_ROWS, CHUNK), lambda g, c: (g, rev(c)))
    small = pltpu.VMEM((HEADS_PER_GROUP, CHUNK), F32)
    return pl.pallas_call(
        body,
        name="ssd_bwd",
        grid=(N_SSD_GROUPS, nc),
        in_specs=_ssd_specs(nc, rev) + [
            pl.BlockSpec((1, 1, GROUP_ROWS, D_STATE), lambda g, c: (g, rev(c), 0, 0)),
            xs_spec,
        ],
        out_specs=[xs_spec, bc, bc, pl.BlockSpec((HEADS_PER_GROUP, CHUNK), lambda g, c: (g, rev(c))), col, col, col],
        out_shape=[
            jax.ShapeDtypeStruct((D_INNER, t), F32),
            jax.ShapeDtypeStruct((BC_DIM, t), F32),
            jax.ShapeDtypeStruct((BC_DIM, t), F32),
            jax.ShapeDtypeStruct((N_SSD_HEADS, t), F32),
            jax.ShapeDtypeStruct((N_SSD_HEADS, 1), F32),
            jax.ShapeDtypeStruct((N_SSD_HEADS, 1), F32),
            jax.ShapeDtypeStruct((N_SSD_HEADS, 1), F32),
        ],
        scratch_shapes=[pltpu.VMEM((GROUP_ROWS, D_STATE), F32), small, small, small],
        compiler_params=_cparams("parallel", "arbitrary"),
    )(xbc, xbc, xbc, proj, dtb_col, alog_col, dsk_col, hst, dy)


GN_ROWS = D_INNER // N_SSD_GROUPS


def _gnorm_fwd(y, proj, w_col):
    t = y.shape[1]
    tt = _tile(t, (512, 256, 128))
    z0 = OFF_Z // GN_ROWS

    def body(y_ref, z_ref, w_ref, o_ref):
        zv = z_ref[...]
        u = y_ref[...] * (zv * _sigmoid(zv))
        r = lax.rsqrt(jnp.mean(u * u, axis=0, keepdims=True) + EPS)
        o_ref[...] = (u * r * w_ref[...]).astype(BF16)

    blk = pl.BlockSpec((GN_ROWS, tt), lambda g, i: (g, i))
    return pl.pallas_call(
        body,
        name="gnorm_fwd",
        grid=(N_SSD_GROUPS, t // tt),
        in_specs=[blk, pl.BlockSpec((GN_ROWS, tt), lambda g, i: (z0 + g, i)), pl.BlockSpec((GN_ROWS, 1), lambda g, i: (g, 0))],
        out_specs=blk,
        out_shape=jax.ShapeDtypeStruct((D_INNER, t), BF16),
        compiler_params=_cparams("parallel", "parallel"),
    )(y, proj, w_col)


def _gnorm_bwd(dout, y, proj, w_col):
    t = y.shape[1]
    tt = _tile(t, (512, 256, 128))
    z0 = OFF_Z // GN_ROWS

    def body(do_ref, y_ref, z_ref, w_ref, dy_ref, dz_ref, dw_ref):
        @pl.when(pl.program_id(1) == 0)
        def _():
            dw_ref[...] = jnp.zeros_like(dw_ref)

        zv = z_ref[...]
        yv = y_ref[...]
        sg = _sigmoid(zv)
        sz = zv * sg
        u = yv * sz
        r = lax.rsqrt(jnp.mean(u * u, axis=0, keepdims=True) + EPS)
        xhat = u * r
        dov = do_ref[...]
        dw_ref[...] += _rowsum(dov * xhat)
        dxhat = dov * w_ref[...]
        du = r * (dxhat - xhat * jnp.mean(dxhat * xhat, axis=0, keepdims=True))
        dy_ref[...] = du * sz
        dz_ref[...] = (du * yv * (sg * (1.0 + zv * (1.0 - sg)))).astype(BF16)

    blk = pl.BlockSpec((GN_ROWS, tt), lambda g, i: (g, i))
    col = pl.BlockSpec((GN_ROWS, 1), lambda g, i: (g, 0))
    return pl.pallas_call(
        body,
        name="gnorm_bwd",
        grid=(N_SSD_GROUPS, t // tt),
        in_specs=[blk, blk, pl.BlockSpec((GN_ROWS, tt), lambda g, i: (z0 + g, i)), col],
        out_specs=[blk, blk, col],
        out_shape=[jax.ShapeDtypeStruct((D_INNER, t), F32), jax.ShapeDtypeStruct((D_INNER, t), BF16),
                   jax.ShapeDtypeStruct((D_INNER, 1), F32)],
        compiler_params=_cparams("parallel", "arbitrary"),
    )(dout, y, proj, w_col)


GATE_ROWS = 32


def _gate_specs(t):
    ga0 = OFF_GA // GATE_ROWS
    gs0 = OFF_GS // GATE_ROWS
    nr = D_MODEL // GATE_ROWS
    blk = pl.BlockSpec((GATE_ROWS, t), lambda r: (r, 0))
    return blk, [
        pl.BlockSpec((GATE_ROWS, t), lambda r: (ga0 + r, 0)),
        pl.BlockSpec((GATE_ROWS, t), lambda r: (gs0 + r, 0)),
        pl.BlockSpec((GATE_ROWS, 1), lambda r: (r, 0)),
        pl.BlockSpec((GATE_ROWS, 1), lambda r: (nr + r, 0)),
        blk, blk,
    ]


def _gate_fwd(proj, b_col, attn, ssd):
    t = proj.shape[1]
    blk, specs = _gate_specs(t)

    def body(ga_ref, gs_ref, ba_ref, bs_ref, a_ref, s_ref, o_ref):
        o_ref[...] = (_sigmoid(ga_ref[...] + ba_ref[...]) * a_ref[...]
                      + _sigmoid(gs_ref[...] + bs_ref[...]) * s_ref[...]).astype(BF16)

    return pl.pallas_call(
        body,
        name="gate_fwd",
        grid=(D_MODEL // GATE_ROWS,),
        in_specs=specs,
        out_specs=blk,
        out_shape=jax.ShapeDtypeStruct((D_MODEL, t), BF16),
        compiler_params=_cparams("parallel"),
    )(proj, proj, b_col, b_col, attn, ssd)


def _gate_bwd(proj, b_col, attn, ssd, dmix):
    t = proj.shape[1]
    blk, specs = _gate_specs(t)

    def body(ga_ref, gs_ref, ba_ref, bs_ref, a_ref, s_ref, dm_ref, da_ref, dso_ref, dga_ref, dgs_ref, dba_ref, dbs_ref):
        dm = dm_ref[...]
        sa = _sigmoid(ga_ref[...] + ba_ref[...])
        ss = _sigmoid(gs_ref[...] + bs_ref[...])
        da_ref[...] = (dm * sa).astype(BF16)
        dso_ref[...] = (dm * ss).astype(BF16)
        dga = dm * a_ref[...] * sa * (1.0 - sa)
        dgs = dm * s_ref[...] * ss * (1.0 - ss)
        dga_ref[...] = dga.astype(BF16)
        dgs_ref[...] = dgs.astype(BF16)
        dba_ref[...] = _rowsum(dga)
        dbs_ref[...] = _rowsum(dgs)

    col = pl.BlockSpec((GATE_ROWS, 1), lambda r: (r, 0))
    act = jax.ShapeDtypeStruct((D_MODEL, t), BF16)
    bias = jax.ShapeDtypeStruct((D_MODEL, 1), F32)
    return pl.pallas_call(
        body,
        name="gate_bwd",
        grid=(D_MODEL // GATE_ROWS,),
        in_specs=specs + [blk],
        out_specs=[blk, blk, blk, blk, col, col],
        out_shape=[act, act, act, act, bias, bias],
        compiler_params=_cparams("parallel"),
    )(proj, proj, b_col, b_col, attn, ssd, dmix)


FFN_ROWS = 256


def _ffn_fwd(u0, w_col, b_col):
    t = u0.shape[2]

    def body(u_ref, w_ref, b_ref, o_ref):
        def strip(rows):
            val, _ = _causal_conv(u_ref[0, rows, :], w_ref[0, rows, :], b_ref[0, rows, :])
            gt, _ = _causal_conv(u_ref[1, rows, :], w_ref[1, rows, :], b_ref[1, rows, :])
            o_ref[rows, :] = (gt * _sigmoid(gt) * val).astype(BF16)

        strip(slice(None))

    return pl.pallas_call(
        body,
        name="ffn_fwd",
        grid=(D_FF // FFN_ROWS,),
        in_specs=[
            pl.BlockSpec((2, FFN_ROWS, t), lambda i: (0, i, 0)),
            pl.BlockSpec((2, FFN_ROWS, FFN_CONV), lambda i: (0, i, 0)),
            pl.BlockSpec((2, FFN_ROWS, 1), lambda i: (0, i, 0)),
        ],
        out_specs=pl.BlockSpec((FFN_ROWS, t), lambda i: (i, 0)),
        out_shape=jax.ShapeDtypeStruct((D_FF, t), BF16),
        compiler_params=_cparams("parallel"),
    )(u0, w_col, b_col)


def _ffn_bwd(u0, w_col, b_col, dg):
    t = u0.shape[2]

    def body(u_ref, w_ref, b_ref, dg_ref, du_ref, dwb_ref):
        def strip(rows):
            xval, wval = u_ref[0, rows, :], w_ref[0, rows, :]
            xgt, wgt = u_ref[1, rows, :], w_ref[1, rows, :]
            val, sh_val = _causal_conv(xval, wval, b_ref[0, rows, :])
            gt, sh_gt = _causal_conv(xgt, wgt, b_ref[1, rows, :])
            sg = _sigmoid(gt)
            dgv = dg_ref[rows, :]
            dval = dgv * (gt * sg)
            dgt = dgv * val * (sg * (1.0 + gt * (1.0 - sg)))
            dx, dwb_ref[0, rows, :] = _causal_conv_bwd(dval, xval, sh_val, wval)
            du_ref[0, rows, :] = dx.astype(BF16)
            dx, dwb_ref[1, rows, :] = _causal_conv_bwd(dgt, xgt, sh_gt, wgt)
            du_ref[1, rows, :] = dx.astype(BF16)

        strip(slice(None))

    return pl.pallas_call(
        body,
        name="ffn_bwd",
        grid=(D_FF // FFN_ROWS,),
        in_specs=[
            pl.BlockSpec((2, FFN_ROWS, t), lambda i: (0, i, 0)),
            pl.BlockSpec((2, FFN_ROWS, FFN_CONV), lambda i: (0, i, 0)),
            pl.BlockSpec((2, FFN_ROWS, 1), lambda i: (0, i, 0)),
            pl.BlockSpec((FFN_ROWS, t), lambda i: (i, 0)),
        ],
        out_specs=[pl.BlockSpec((2, FFN_ROWS, t), lambda i: (0, i, 0)), pl.BlockSpec((2, FFN_ROWS, 128), lambda i: (0, i, 0))],
        out_shape=[jax.ShapeDtypeStruct((2, D_FF, t), BF16), jax.ShapeDtypeStruct((2, D_FF, 128), F32)],
        compiler_params=_cparams("parallel"),
    )(u0, w_col, b_col, dg)


def _adamw_math(w, g, m, v):
    m = ADAM_B1 * m + (1.0 - ADAM_B1) * g
    v = ADAM_B2 * v + (1.0 - ADAM_B2) * (g * g)
    m_hat = m / (1.0 - ADAM_B1 ** ADAM_STEP)
    v_hat = v / (1.0 - ADAM_B2 ** ADAM_STEP)
    delta = -ADAM_LR * (m_hat / (jnp.sqrt(v_hat) + ADAM_EPS) + ADAM_WD * w)
    return delta, m, v


def _adamw_sharded(parts, w, m, v, name):
    r, c = w.shape
    tc = _tile(c, (256, 128))

    def body(p_ref, w_ref, m_ref, v_ref, g_ref, d_ref, nm_ref, nv_ref):
        g = p_ref[0].astype(F32)
        for s in range(1, N_DEV):
            g = g + p_ref[s].astype(F32)
        g_ref[...] = g
        d_ref[...], nm_ref[...], nv_ref[...] = _adamw_math(w_ref[...], g, m_ref[...], v_ref[...])

    blk = pl.BlockSpec((r, tc), lambda i: (0, i))
    out = jax.ShapeDtypeStruct((r, c), F32)
    return pl.pallas_call(
        body,
        name=name,
        grid=(c // tc,),
        in_specs=[pl.BlockSpec((N_DEV, r, tc), lambda i: (0, 0, i)), blk, blk, blk],
        out_specs=[blk, blk, blk, blk],
        out_shape=[out, out, out, out],
        compiler_params=_cparams("parallel"),
    )(parts, w, m, v)


def _sum_slots(parts):
    _, r, c = parts.shape

    def body(p_ref, o_ref):
        g = p_ref[0]
        for s in range(1, N_DEV):
            g = g + p_ref[s]
        o_ref[...] = g

    return pl.pallas_call(body, name="sum_small_grads", out_shape=jax.ShapeDtypeStruct((r, c), F32))(parts)


def _adamw_small(g, w, m, v):
    def body(g_ref, w_ref, m_ref, v_ref, d_ref, nm_ref, nv_ref):
        d_ref[...], nm_ref[...], nv_ref[...] = _adamw_math(w_ref[...], g_ref[...], m_ref[...], v_ref[...])

    out = jax.ShapeDtypeStruct(g.shape, F32)
    return pl.pallas_call(body, name="adamw_small", out_shape=[out, out, out])(g, w, m, v)


ANY = pl.BlockSpec(memory_space=pl.ANY)
FLIPS = [(k >> 2 & 1, k >> 1 & 1, k & 1) for k in range(1, N_DEV)]


def _place():
    return lax.axis_index("x"), lax.axis_index("y"), lax.axis_index("c")


def _all_gather(arrays, name):
    n = len(arrays)

    def body(*refs):
        ins, outs = refs[:n], refs[n:2 * n]
        send_sems, recv_sems, local_sems = refs[2 * n:]
        x, y, c = _place()
        me = 4 * x + 2 * y + c
        local = [pltpu.make_async_copy(ins[i], outs[i].at[me], local_sems.at[i]) for i in range(n)]
        for cp in local:
            cp.start()
        sends = []
        for k, (fx, fy, fc) in enumerate(FLIPS):
            for i in range(n):
                cp = pltpu.make_async_remote_copy(
                    src_ref=ins[i], dst_ref=outs[i].at[me], send_sem=send_sems.at[i, k], recv_sem=recv_sems.at[i, k],
                    device_id=(x ^ fx, y ^ fy, c ^ fc), device_id_type=MESH)
                cp.start()
                sends.append(cp)
        for k, (fx, fy, fc) in enumerate(FLIPS):
            src = 4 * (x ^ fx) + 2 * (y ^ fy) + (c ^ fc)
            for i in range(n):
                pltpu.make_async_remote_copy(
                    src_ref=ins[i], dst_ref=outs[i].at[src], send_sem=send_sems.at[i, k], recv_sem=recv_sems.at[i, k],
                    device_id=(x ^ fx, y ^ fy, c ^ fc), device_id_type=MESH).wait_recv()
        for cp in sends:
            cp.wait_send()
        for cp in local:
            cp.wait()

    return pl.pallas_call(
        body,
        name=name,
        in_specs=[ANY] * n,
        out_specs=[ANY] * n,
        out_shape=[jax.ShapeDtypeStruct((N_DEV,) + a.shape, a.dtype) for a in arrays],
        scratch_shapes=[pltpu.SemaphoreType.DMA((n, N_DEV - 1)), pltpu.SemaphoreType.DMA((n, N_DEV - 1)),
                        pltpu.SemaphoreType.DMA((n,))],
    )(*arrays)


HBM = pl.BlockSpec(memory_space=pltpu.HBM)
SEM = pl.BlockSpec(memory_space=pltpu.SEMAPHORE)
EFFECT = pltpu.SideEffectType.DATAFLOW_SIDE_EFFECTING


def _peer_copy(gather, src_ref, land_ref, send_sems, recv_sems, k, sending):
    x, y, c = _place()
    fx, fy, fc = FLIPS[k]
    me = 4 * x + 2 * y + c
    peer = 4 * (x ^ fx) + 2 * (y ^ fy) + (c ^ fc)
    return pltpu.make_async_remote_copy(
        src_ref=src_ref if gather else src_ref.at[peer],
        dst_ref=land_ref.at[me if sending else peer],
        send_sem=send_sems.at[k], recv_sem=recv_sems.at[k],
        device_id=(x ^ fx, y ^ fy, c ^ fc), device_id_type=MESH)


def _gather_start(srcs, name):
    n = len(srcs)
    lands = [lax.empty((N_DEV,) + s.shape, s.dtype) for s in srcs]

    def body(*refs):
        src_refs, land_refs = refs[:n], refs[n:2 * n]
        send, recv = refs[2 * n:3 * n], refs[3 * n:4 * n]
        local_sems = refs[6 * n]
        x, y, c = _place()
        me = 4 * x + 2 * y + c
        local = [pltpu.make_async_copy(src_refs[i], land_refs[i].at[me], local_sems.at[i]) for i in range(n)]
        for cp in local:
            cp.start()
        for i in range(n):
            for k in range(N_DEV - 1):
                _peer_copy(True, src_refs[i], land_refs[i], send[i], recv[i], k, True).start()
        for cp in local:
            cp.wait()

    sem = pltpu.SemaphoreType.DMA((N_DEV - 1,))
    hbm = lambda a: pltpu.HBM(a.shape, a.dtype)
    res = pl.pallas_call(
        body,
        name=name,
        in_specs=[HBM] * (2 * n),
        out_specs=[SEM] * (2 * n) + [HBM] * (2 * n),
        out_shape=[sem] * (2 * n) + [hbm(s) for s in srcs] + [hbm(a) for a in lands],
        input_output_aliases={i: 2 * n + i for i in range(2 * n)},
        scratch_shapes=[pltpu.SemaphoreType.DMA((n,))],
        compiler_params=pltpu.CompilerParams(has_side_effects=EFFECT),
    )(*[pltpu.with_memory_space_constraint(a, pltpu.HBM) for a in list(srcs) + lands])
    return res[:n], res[n:2 * n], res[2 * n:3 * n], res[3 * n:4 * n]


def _exchange_wait(send_sems, recv_sems, src, land, after, gather, name):
    def body(src_ref, land_ref, send_ref, recv_ref, after_ref, src_out, land_out):
        for k in range(N_DEV - 1):
            cp = _peer_copy(gather, src_ref, land_ref, send_ref, recv_ref, k, False)
            cp.wait_send()
            cp.wait_recv()

    hbm = lambda a: pltpu.HBM(a.shape, a.dtype)
    return pl.pallas_call(
        body,
        name=name,
        in_specs=[HBM, HBM, SEM, SEM, ANY],
        out_specs=[HBM, HBM],
        out_shape=[hbm(src), hbm(land)],
        input_output_aliases={0: 0, 1: 1},
        compiler_params=pltpu.CompilerParams(has_side_effects=EFFECT),
    )(src, land, send_sems, recv_sems, after)[1]


def _col(v):
    return v.reshape(-1, 1).astype(F32)


def _local_step(xt, tgt, weight, small):
    t = xt.shape[1]
    n1 = _col(small["norm1_w"])
    n2 = _col(small["norm2_w"])
    nf = _col(small["final_norm_w"])
    bg = _col(small["b_gate"])
    sinks = small["attn_sinks"].reshape(-1).astype(F32)
    cbias = _col(small["ssd_conv_b"])
    dtb = _col(small["dt_bias"])
    alog = _col(small["a_log"])
    dsk = _col(small["d_skip"])
    gnw = _col(small["ssd_norm_w"])
    fb = small["ffn_conv_b"].reshape(2, D_FF, 1)

    xn = _norm_fwd(xt, n1, "norm1_fwd")
    cw = weight("ssd_conv_w", xn).T
    fw = weight("ffn_conv_w", xn).T.reshape(2, D_FF, FFN_CONV)
    w_in_t = weight("w_in", xn)
    proj = _matmul(w_in_t, xn, nt=False, out_dtype=F32, name="mm_in")
    ao, lse = _attn_fwd(proj, sinks)
    w_ao = weight("w_attn_o", ao)
    attn = _matmul(w_ao, ao, nt=False, out_dtype=F32, name="mm_attn_o", tn_a=True)
    xbc = _conv_silu_fwd(proj, cw, cbias)
    y, hst = _ssd_fwd(xbc, proj, dtb, alog, dsk)
    yn = _gnorm_fwd(y, proj, gnw)
    w_so = weight("w_ssd_o", yn)
    ssd = _matmul(w_so, yn, nt=False, out_dtype=F32, name="mm_ssd_o", tn_a=True)
    mix = _gate_fwd(proj, bg, attn, ssd)
    w_out = weight("w_out", mix)
    h1 = _matmul(w_out, mix, nt=False, out_dtype=F32, name="mm_out", add=xt, tn_a=True)
    hn = _norm_fwd(h1, n2, "norm2_fwd")
    w_up_t = weight("w_up", hn)
    u0 = _matmul(w_up_t, hn, nt=False, out_dtype=F32, name="mm_up").reshape(2, D_FF, t)
    gl = _ffn_fwd(u0, fw, fb)
    w_down = weight("w_down", gl)
    h2 = _matmul(w_down, gl, nt=False, out_dtype=F32, name="mm_down", add=h1, tn_a=True)
    dh2, loss, d_nf = _final_norm_loss(h2, tgt, nf)

    g = {}
    handles = {}

    def sending(weight_name, grad, *args, **kwargs):
        out, handles[weight_name] = _matmul(*args, send=grad.reshape(N_DEV, -1, D_MODEL), **kwargs)
        return out

    g_down = _matmul(gl, dh2, nt=True, out_dtype=BF16, name="mm_d_w_down")
    dgl = sending("w_down", g_down, w_down, dh2, nt=False, out_dtype=F32, name="mm_d_glu")
    du0, d_fwb = _ffn_bwd(u0, fw, fb, dgl)
    du0 = du0.reshape(2 * D_FF, t)
    g_up = _matmul(du0, hn, nt=True, out_dtype=BF16, name="mm_d_w_up")
    dhn = sending("w_up", g_up, w_up_t, du0, nt=False, out_dtype=F32, name="mm_d_hn", tn_a=True)
    dh1, d_n2 = _norm_bwd(dhn, h1, n2, dh2, "norm2_bwd")
    g_out = _matmul(mix, dh1, nt=True, out_dtype=BF16, name="mm_d_w_out")
    dmix = sending("w_out", g_out, w_out, dh1, nt=False, out_dtype=F32, name="mm_d_mix")
    d_attn, d_ssd, d_ga, d_gs, d_ba, d_bs = _gate_bwd(proj, bg, attn, ssd, dmix)
    g_ao = _matmul(ao, d_attn, nt=True, out_dtype=BF16, name="mm_d_w_attn_o")
    dao = sending("w_attn_o", g_ao, w_ao, d_attn, nt=False, out_dtype=F32, name="mm_d_ao")
    dq, dk, dv, d_sinks = _attn_bwd(proj, sinks, ao, lse, dao)
    g_so = _matmul(yn, d_ssd, nt=True, out_dtype=BF16, name="mm_d_w_ssd_o")
    dyn = sending("w_ssd_o", g_so, w_so, d_ssd, nt=False, out_dtype=F32, name="mm_d_yn")
    dy, dz, d_gnw = _gnorm_bwd(dyn, y, proj, gnw)
    dxs, dbm, dcm, ddt, d_alog, d_dsk, d_dtb = _ssd_bwd(xbc, proj, dtb, alog, dsk, hst, dy)
    dx_xs, dwb_xs = _conv_silu_bwd(proj, cw, cbias, dxs, 0, "ssd_conv_bwd_x")
    dx_b, dwb_b = _conv_silu_bwd(proj, cw, cbias, dbm, D_INNER, "ssd_conv_bwd_b")
    dx_c, dwb_c = _conv_silu_bwd(proj, cw, cbias, dcm, D_INNER + BC_DIM, "ssd_conv_bwd_c")
    dwb_conv = jnp.concatenate([dwb_xs, dwb_b, dwb_c], axis=0)
    dproj = jnp.concatenate([dq, dk, dv, dz, dx_xs, dx_b, dx_c, ddt.astype(BF16), d_ga, d_gs], axis=0)
    g_in = _matmul(dproj, xn, nt=True, out_dtype=BF16, name="mm_d_w_in")
    dxn = sending("w_in", g_in, w_in_t, dproj, nt=False, out_dtype=F32, name="mm_d_xn", tn_a=True)
    dx, d_n1 = _norm_bwd(dxn, xt, n1, dh1, "norm1_bwd")

    g["norm1_w"] = d_n1
    g["b_gate"] = jnp.concatenate([d_ba, d_bs], axis=0)
    g["attn_sinks"] = d_sinks
    g["ssd_conv_w"] = dwb_conv[:, :SSD_CONV].T
    g["ssd_conv_b"] = dwb_conv[:, SSD_CONV]
    g["dt_bias"] = d_dtb
    g["a_log"] = d_alog
    g["d_skip"] = d_dsk
    g["ssd_norm_w"] = d_gnw
    g["norm2_w"] = d_n2
    d_fwb = d_fwb.reshape(2 * D_FF, 128)
    g["ffn_conv_w"] = d_fwb[:, :FFN_CONV].T
    g["ffn_conv_b"] = d_fwb[:, FFN_CONV]
    g["final_norm_w"] = d_nf
    return loss, dx, g, handles


SHARDED = ("w_in", "w_attn_o", "w_ssd_o", "w_out", "w_up", "w_down")
SMALL = ("norm1_w", "b_gate", "attn_sinks", "ssd_conv_w", "ssd_conv_b", "dt_bias", "a_log", "d_skip", "ssd_norm_w",
         "norm2_w", "ffn_conv_w", "ffn_conv_b", "final_norm_w")
SMALL_SHAPES = {"norm1_w": (1, D_MODEL), "b_gate": (1, 2 * D_MODEL), "attn_sinks": (1, N_Q_HEADS),
                "ssd_conv_w": (1, SSD_CONV, XBC_DIM), "ssd_conv_b": (1, XBC_DIM), "dt_bias": (1, N_SSD_HEADS),
                "a_log": (1, N_SSD_HEADS), "d_skip": (1, N_SSD_HEADS), "ssd_norm_w": (1, D_INNER),
                "norm2_w": (1, D_MODEL), "ffn_conv_w": (1, FFN_CONV, 2 * D_FF), "ffn_conv_b": (1, 2 * D_FF),
                "final_norm_w": (D_MODEL,)}
WEIGHT_ORDER = ("norm1_w", "w_in", "b_gate", "attn_sinks", "w_attn_o", "ssd_conv_w", "ssd_conv_b", "dt_bias", "a_log",
                "d_skip", "ssd_norm_w", "w_ssd_o", "w_out", "norm2_w", "w_up", "ffn_conv_w", "ffn_conv_b", "w_down",
                "final_norm_w")


def _pack(parts):
    flat = jnp.concatenate([p.reshape(-1).astype(F32) for p in parts])
    rows = -(-flat.shape[0] // 1024) * 8
    return jnp.pad(flat, (0, rows * 128 - flat.shape[0])).reshape(rows, 128)


def _unpack(packed, shapes):
    flat = packed.reshape(-1)
    out, pos = [], 0
    for shp in shapes:
        size = 1
        for d in shp:
            size *= d
        out.append(flat[pos:pos + size].reshape(shp))
        pos += size
    return out


def kernel(x, norm1_w, w_in, b_gate, attn_sinks, w_attn_o, ssd_conv_w, ssd_conv_b, dt_bias, a_log, d_skip, ssd_norm_w, w_ssd_o, w_out, norm2_w, w_up, ffn_conv_w, ffn_conv_b, w_down, final_norm_w, loss_target, m_norm1_w, m_w_in, m_b_gate, m_attn_sinks, m_w_attn_o, m_ssd_conv_w, m_ssd_conv_b, m_dt_bias, m_a_log, m_d_skip, m_ssd_norm_w, m_w_ssd_o, m_w_out, m_norm2_w, m_w_up, m_ffn_conv_w, m_ffn_conv_b, m_w_down, m_final_norm_w, v_norm1_w, v_w_in, v_b_gate, v_attn_sinks, v_w_attn_o, v_ssd_conv_w, v_ssd_conv_b, v_dt_bias, v_a_log, v_d_skip, v_ssd_norm_w, v_w_ssd_o, v_w_out, v_norm2_w, v_w_up, v_ffn_conv_w, v_ffn_conv_b, v_w_down, v_final_norm_w):
    w = dict(norm1_w=norm1_w, w_in=w_in, b_gate=b_gate, attn_sinks=attn_sinks, w_attn_o=w_attn_o, ssd_conv_w=ssd_conv_w, ssd_conv_b=ssd_conv_b, dt_bias=dt_bias, a_log=a_log, d_skip=d_skip, ssd_norm_w=ssd_norm_w, w_ssd_o=w_ssd_o, w_out=w_out, norm2_w=norm2_w, w_up=w_up, ffn_conv_w=ffn_conv_w, ffn_conv_b=ffn_conv_b, w_down=w_down, final_norm_w=final_norm_w)
    m = dict(norm1_w=m_norm1_w, w_in=m_w_in, b_gate=m_b_gate, attn_sinks=m_attn_sinks, w_attn_o=m_w_attn_o, ssd_conv_w=m_ssd_conv_w, ssd_conv_b=m_ssd_conv_b, dt_bias=m_dt_bias, a_log=m_a_log, d_skip=m_d_skip, ssd_norm_w=m_ssd_norm_w, w_ssd_o=m_w_ssd_o, w_out=m_w_out, norm2_w=m_norm2_w, w_up=m_w_up, ffn_conv_w=m_ffn_conv_w, ffn_conv_b=m_ffn_conv_b, w_down=m_w_down, final_norm_w=m_final_norm_w)
    v = dict(norm1_w=v_norm1_w, w_in=v_w_in, b_gate=v_b_gate, attn_sinks=v_attn_sinks, w_attn_o=v_w_attn_o, ssd_conv_w=v_ssd_conv_w, ssd_conv_b=v_ssd_conv_b, dt_bias=v_dt_bias, a_log=v_a_log, d_skip=v_d_skip, ssd_norm_w=v_ssd_norm_w, w_ssd_o=v_w_ssd_o, w_out=v_w_out, norm2_w=v_norm2_w, w_up=v_w_up, ffn_conv_w=v_ffn_conv_w, ffn_conv_b=v_ffn_conv_b, w_down=v_w_down, final_norm_w=v_final_norm_w)
    me = 4 * lax.axis_index("x") + 2 * lax.axis_index("y") + lax.axis_index("c")
    conv_cols = XBC_DIM // N_DEV
    ffn_cols = 2 * D_FF // N_DEV

    shards = {"ssd_conv_w": ssd_conv_w[0], "ffn_conv_w": ffn_conv_w[0], "w_in": w_in[0].T.astype(BF16),
              "w_attn_o": w_attn_o[0].astype(BF16), "w_ssd_o": w_ssd_o[0].astype(BF16), "w_out": w_out[0].astype(BF16),
              "w_up": w_up[0].T.astype(BF16), "w_down": w_down[0].astype(BF16)}
    order = list(shards)
    g_send, g_recv, g_src, g_land = _gather_start(list(shards.values()), "gather_start")

    def weight(name, after):
        i = order.index(name)
        land = _exchange_wait(g_send[i], g_recv[i], g_src[i], g_land[i], after, True, "gather_wait_" + name)
        if name == "ssd_conv_w":
            return jnp.transpose(land, (1, 0, 2)).reshape(SSD_CONV, XBC_DIM)
        if name == "ffn_conv_w":
            return jnp.transpose(land, (1, 0, 2)).reshape(FFN_CONV, 2 * D_FF)
        return land.reshape(-1, D_MODEL)

    small = {k: w[k][0] if k != "final_norm_w" else w[k] for k in SMALL}
    loss, dx, g, pending = _local_step(x[0].T, loss_target[0].T, weight, small)

    packed = _pack([loss] + [g[k] for k in SMALL])
    total = _sum_slots(_all_gather([packed], "gather_small_grads")[0])
    tot = _unpack(total, [(1,)] + [SMALL_SHAPES[k] for k in SMALL])
    loss_sum = tot[0].reshape(())
    gs = dict(zip(SMALL, tot[1:]))
    gs["ssd_conv_w"] = lax.dynamic_slice_in_dim(gs["ssd_conv_w"], me * conv_cols, conv_cols, axis=2)
    gs["ffn_conv_w"] = lax.dynamic_slice_in_dim(gs["ffn_conv_w"], me * ffn_cols, ffn_cols, axis=2)
    upd = _adamw_small(_pack([gs[k] for k in SMALL]), _pack([w[k] for k in SMALL]), _pack([m[k] for k in SMALL]),
                       _pack([v[k] for k in SMALL]))
    shapes = [w[k].shape for k in SMALL]
    d_s, m_s, v_s = (dict(zip(SMALL, _unpack(u, shapes))) for u in upd)
    res = {}
    for k in SMALL:
        res[k] = (gs[k], d_s[k], m_s[k], v_s[k])

    after = upd[0]
    for name in ("w_down", "w_up", "w_out", "w_attn_o", "w_ssd_o", "w_in"):
        parts = _exchange_wait(*pending[name], after, False, "grad_wait_" + name)
        view = (lambda a: a[0].T) if name in ("w_in", "w_up") else (lambda a: a[0])
        res[name] = _adamw_sharded(parts, view(w[name]), view(m[name]), view(v[name]), "adamw_" + name)
        after = res[name][0]
        if name in ("w_in", "w_up"):
            res[name] = [r.T for r in res[name]]

    grad_x = dx.T[None]
    outs = [loss_sum, grad_x]
    for i in range(4):
        for k in WEIGHT_ORDER:
            r = res[k][i]
            outs.append(r[None] if k in SHARDED else r)
    return tuple(outs)
```

```python
import functools

import jax
import jax.numpy as jnp
from jax import lax
from jax.experimental import pallas as pl
from jax.experimental.pallas import tpu as pltpu

F32 = jnp.float32
BF16 = jnp.bfloat16
HIGHEST = lax.Precision.HIGHEST

D_MODEL = 1024
N_Q_HEADS = 16
N_KV_HEADS = 4
HEAD_DIM = 64
WINDOW = 128
Q_PER_KV = N_Q_HEADS // N_KV_HEADS
Q_DIM = N_Q_HEADS * HEAD_DIM
KV_DIM = N_KV_HEADS * HEAD_DIM
D_INNER = 2048
SSD_HEAD_DIM = 64
N_SSD_HEADS = 32
N_SSD_GROUPS = 4
HEADS_PER_GROUP = N_SSD_HEADS // N_SSD_GROUPS
D_STATE = 128
BC_DIM = N_SSD_GROUPS * D_STATE
XBC_DIM = D_INNER + 2 * BC_DIM
SSD_CONV = 4
CHUNK = 128
D_FF = 2816
FFN_CONV = 3
EPS = 1e-5
NEG = -1e30
IN_DIM = 8736
N_DEV = 8

OFF_Q = 0
OFF_K = OFF_Q + Q_DIM
OFF_V = OFF_K + KV_DIM
OFF_Z = OFF_V + KV_DIM
OFF_X = OFF_Z + D_INNER
OFF_DT = OFF_X + XBC_DIM
OFF_GA = OFF_DT + N_SSD_HEADS
OFF_GS = OFF_GA + D_MODEL

ADAM_LR = 0.001
ADAM_B1 = 0.9
ADAM_B2 = 0.999
ADAM_EPS = 1e-08
ADAM_WD = 0.01
ADAM_STEP = 10

VMEM_LIMIT = 48 * 1024 * 1024
MESH = pl.DeviceIdType.MESH


def _cparams(*sem):
    return pltpu.CompilerParams(dimension_semantics=sem, vmem_limit_bytes=VMEM_LIMIT)


def _tile(n, prefs):
    for p in prefs:
        if n % p == 0:
            return p
    return n


def _sigmoid(x):
    return 1.0 / (1.0 + jnp.exp(-x))


def _softplus(x):
    return jnp.maximum(x, 0.0) + jnp.log(1.0 + jnp.exp(-jnp.abs(x)))


def _rowsum(x):
    return jnp.sum(x, axis=1, keepdims=True)


def _colsum(x):
    return jnp.sum(x, axis=0, keepdims=True)


def _dot(a, b):
    return jnp.dot(a, b, preferred_element_type=F32)


def _dot_nt(a, b):
    return lax.dot_general(a, b, (((1,), (1,)), ((), ())), preferred_element_type=F32)


def _dot_tn(a, b):
    return lax.dot_general(a, b, (((0,), (0,)), ((), ())), preferred_element_type=F32)


def _shift_right(x, j):
    if j == 0:
        return x
    r = pltpu.roll(x, j, 1)
    lane = lax.broadcasted_iota(jnp.int32, (x.shape[0], 128), 1)
    return jnp.concatenate([jnp.where(lane >= j, r[:, :128], 0.0), r[:, 128:]], axis=1)


def _shift_left(x, j):
    if j == 0:
        return x
    n = x.shape[1]
    r = pltpu.roll(x, n - j, 1)
    lane = lax.broadcasted_iota(jnp.int32, (x.shape[0], 128), 1)
    return jnp.concatenate([r[:, :n - 128], jnp.where(lane < 128 - j, r[:, n - 128:], 0.0)], axis=1)


def _causal_conv(xv, wv, bv):
    taps = wv.shape[1]
    shifted = [_shift_right(xv, taps - 1 - k) for k in range(taps - 1)]
    y = bv + wv[:, taps - 1:taps] * xv
    for k in range(taps - 1):
        y = y + wv[:, k:k + 1] * shifted[k]
    return y, shifted


def _causal_conv_bwd(dy, xv, shifted, wv):
    taps = wv.shape[1]
    lane = lax.broadcasted_iota(jnp.int32, (dy.shape[0], 128), 1)
    dwb = jnp.where(lane == taps, _rowsum(dy), 0.0)
    dwb = jnp.where(lane == taps - 1, _rowsum(dy * xv), dwb)
    dx = wv[:, taps - 1:taps] * dy
    for k in range(taps - 1):
        dx = dx + wv[:, k:k + 1] * _shift_left(dy, taps - 1 - k)
        dwb = jnp.where(lane == k, _rowsum(dy * shifted[k]), dwb)
    return dx, dwb


MATMUL_VMEM_BUDGET = 36 * 1024 * 1024
MATMUL_MAX_TK = 3072


MATMUL_MAX_TM = 768


def _largest_tile(n, align, cap):
    return max(d for d in range(align, min(n, cap) + 1, align) if n % d == 0)


def _matmul_tiles(m, n, k, a_bytes, b_bytes, out_bytes, has_add, m_align, k_align):
    tm = _largest_tile(m, m_align, MATMUL_MAX_TM)
    tk = _largest_tile(k, k_align, MATMUL_MAX_TK)
    for tn in sorted({d for d in range(128, n + 1, 128) if n % d == 0}, reverse=True):
        need = 2 * (tm * tk * a_bytes + tk * tn * b_bytes) + tm * tn * (2 * out_bytes + (4 if k > tk else 0) + (8 if has_add else 0))
        if tn <= 3072 and need <= MATMUL_VMEM_BUDGET:
            return tm, tn, tk
    return tm, 128, tk


def _matmul(a, b, *, nt, out_dtype, name, add=None, tn_a=False, send=None):
    if tn_a:
        k, m = a.shape
    else:
        m, k = a.shape
    n = b.shape[0] if nt else b.shape[1]
    tm, tn, tk = _matmul_tiles(m, n, k, a.dtype.itemsize, b.dtype.itemsize, jnp.dtype(out_dtype).itemsize, add is not None,
                               128 if tn_a else 16, 16 if tn_a and not nt else 128)
    nk = k // tk
    grid = (m // tm, n // tn, nk)

    def body(a_ref, b_ref, *rest):
        r_ref = None
        if add is not None:
            r_ref, rest = rest[0], rest[1:]
        if send is not None:
            src_ref, land_ref, rest = rest[0], rest[1], rest[2:]
            send_sems, recv_sems, local_sem = rest[1], rest[2], rest[-1]
            rest = (rest[0],) + rest[5:-1]
            x, y, c = _place()
            me = 4 * x + 2 * y + c
            local = pltpu.make_async_copy(src_ref.at[me], land_ref.at[me], local_sem)
            step = (pl.program_id(0) * grid[1] + pl.program_id(1)) * grid[2] + pl.program_id(2)

            @pl.when(step == 0)
            def _():
                local.start()
                for peer in range(N_DEV - 1):
                    _peer_copy(False, src_ref, land_ref, send_sems, recv_sems, peer, True).start()

            @pl.when(step == grid[0] * grid[1] * grid[2] - 1)
            def _():
                local.wait()

        o_ref = rest[0]
        av = a_ref[...].astype(BF16)
        bv = b_ref[...].astype(BF16)
        part = _dot_tn(av, bv) if tn_a else _dot_nt(av, bv) if nt else _dot(av, bv)

        def finish(r):
            if add is not None:
                r = r + r_ref[...]
            o_ref[...] = r.astype(out_dtype)

        if nk == 1:
            finish(part)
            return
        acc = rest[1]
        kk = pl.program_id(2)

        @pl.when(kk == 0)
        def _():
            acc[...] = part

        @pl.when((kk > 0) & (kk < nk - 1))
        def _():
            acc[...] += part

        @pl.when(kk == nk - 1)
        def _():
            finish(acc[...] + part)

    in_specs = [
        pl.BlockSpec((tk, tm), lambda i, j, kk: (kk, i)) if tn_a else pl.BlockSpec((tm, tk), lambda i, j, kk: (i, kk)),
        pl.BlockSpec((tn, tk), lambda i, j, kk: (j, kk)) if nt else pl.BlockSpec((tk, tn), lambda i, j, kk: (kk, j)),
    ]
    args = [a, b]
    if add is not None:
        in_specs.append(pl.BlockSpec((tm, tn), lambda i, j, kk: (i, j)))
        args.append(add)
    out_specs = [pl.BlockSpec((tm, tn), lambda i, j, kk: (i, j))]
    out_shape = [jax.ShapeDtypeStruct((m, n), out_dtype)]
    scratch = [pltpu.VMEM((tm, tn), F32)] if nk > 1 else []
    if send is None:
        return pl.pallas_call(
            body, name=name, grid=grid, in_specs=in_specs, out_specs=out_specs[0], out_shape=out_shape[0],
            scratch_shapes=scratch, compiler_params=_cparams("parallel", "parallel", "arbitrary"),
        )(*args)
    sem = pltpu.SemaphoreType.DMA((N_DEV - 1,))
    hbm = pltpu.HBM(send.shape, send.dtype)
    first = len(args)
    res = pl.pallas_call(
        body, name=name, grid=grid,
        in_specs=in_specs + [HBM, HBM],
        out_specs=out_specs + [SEM, SEM, HBM, HBM],
        out_shape=out_shape + [sem, sem, hbm, hbm],
        input_output_aliases={first: 3, first + 1: 4},
        scratch_shapes=scratch + [pltpu.SemaphoreType.DMA(())],
        compiler_params=pltpu.CompilerParams(dimension_semantics=("arbitrary",) * 3, vmem_limit_bytes=VMEM_LIMIT,
                                             has_side_effects=EFFECT),
    )(*args, pltpu.with_memory_space_constraint(send, pltpu.HBM),
      pltpu.with_memory_space_constraint(lax.empty(send.shape, send.dtype), pltpu.HBM))
    return res[0], tuple(res[1:])


def _norm_fwd(x, w_col, name):
    f, t = x.shape
    tt = _tile(t, (512, 256, 128))

    def body(x_ref, w_ref, o_ref):
        xv = x_ref[...]
        r = lax.rsqrt(jnp.mean(xv * xv, axis=0, keepdims=True) + EPS)
        o_ref[...] = (xv * r * w_ref[...]).astype(BF16)

    return pl.pallas_call(
        body,
        name=name,
        grid=(t // tt,),
        in_specs=[pl.BlockSpec((f, tt), lambda i: (0, i)), pl.BlockSpec((f, 1), lambda i: (0, 0))],
        out_specs=pl.BlockSpec((f, tt), lambda i: (0, i)),
        out_shape=jax.ShapeDtypeStruct((f, t), BF16),
        compiler_params=_cparams("parallel"),
    )(x, w_col)


def _norm_bwd(dy, x, w_col, res, name):
    f, t = x.shape
    tt = _tile(t, (512, 256, 128))

    def body(dy_ref, x_ref, w_ref, res_ref, dx_ref, dw_ref):
        @pl.when(pl.program_id(0) == 0)
        def _():
            dw_ref[...] = jnp.zeros_like(dw_ref)

        xv = x_ref[...]
        r = lax.rsqrt(jnp.mean(xv * xv, axis=0, keepdims=True) + EPS)
        xhat = xv * r
        dyv = dy_ref[...]
        dw_ref[...] += _rowsum(dyv * xhat)
        dxhat = dyv * w_ref[...]
        dx_ref[...] = res_ref[...] + r * (dxhat - xhat * jnp.mean(dxhat * xhat, axis=0, keepdims=True))

    blk = pl.BlockSpec((f, tt), lambda i: (0, i))
    col = pl.BlockSpec((f, 1), lambda i: (0, 0))
    return pl.pallas_call(
        body,
        name=name,
        grid=(t // tt,),
        in_specs=[blk, blk, col, blk],
        out_specs=[blk, col],
        out_shape=[jax.ShapeDtypeStruct((f, t), F32), jax.ShapeDtypeStruct((f, 1), F32)],
        compiler_params=_cparams("arbitrary"),
    )(dy, x, w_col, res)


def _final_norm_loss(h, tgt, w_col):
    f, t = h.shape
    tt = _tile(t, (512, 256, 128))

    def body(h_ref, t_ref, w_ref, dh_ref, loss_ref, dw_ref):
        @pl.when(pl.program_id(0) == 0)
        def _():
            dw_ref[...] = jnp.zeros_like(dw_ref)
            loss_ref[...] = jnp.zeros_like(loss_ref)

        xv = h_ref[...]
        r = lax.rsqrt(jnp.mean(xv * xv, axis=0, keepdims=True) + EPS)
        xhat = xv * r
        wv = w_ref[...]
        err = xhat * wv - t_ref[...]
        loss_ref[...] += 0.5 * _rowsum(jnp.mean(err * err, axis=0, keepdims=True))
        dyv = err * (1.0 / f)
        dw_ref[...] += _rowsum(dyv * xhat)
        dxhat = dyv * wv
        dh_ref[...] = r * (dxhat - xhat * jnp.mean(dxhat * xhat, axis=0, keepdims=True))

    blk = pl.BlockSpec((f, tt), lambda i: (0, i))
    col = pl.BlockSpec((f, 1), lambda i: (0, 0))
    one = pl.BlockSpec((1, 1), lambda i: (0, 0))
    return pl.pallas_call(
        body,
        name="final_norm_loss",
        grid=(t // tt,),
        in_specs=[blk, blk, col],
        out_specs=[blk, one, col],
        out_shape=[jax.ShapeDtypeStruct((f, t), F32), jax.ShapeDtypeStruct((1, 1), F32), jax.ShapeDtypeStruct((f, 1), F32)],
        compiler_params=_cparams("arbitrary"),
    )(h, tgt, w_col)


def _attn_mask(n):
    shape = (2 * WINDOW, Q_PER_KV * WINDOW)
    si = lax.broadcasted_iota(jnp.int32, shape, 0)
    qi = lax.broadcasted_iota(jnp.int32, shape, 1) & (WINDOW - 1)
    dist = WINDOW + qi - si
    return (dist >= 0) & (dist < WINDOW) & ((si >= WINDOW) | (n > 0))


def _lane_cat(ref, row0, rows):
    return jnp.concatenate([ref[row0 + i * rows:row0 + (i + 1) * rows, :] for i in range(Q_PER_KV)], axis=1)


def _attn_fwd(proj, sinks):
    t = proj.shape[1]
    nb = t // WINDOW
    scale = HEAD_DIM ** -0.5

    def body(s_ref, q_ref, kc_ref, kp_ref, vc_ref, vp_ref, o_ref, lse_ref):
        n = pl.program_id(0)
        valid = _attn_mask(n)
        for g in range(N_KV_HEADS):
            rows = slice(g * HEAD_DIM, (g + 1) * HEAD_DIM)
            kt = jnp.concatenate([kp_ref[rows, :], kc_ref[rows, :]], axis=1).astype(BF16)
            vt = jnp.concatenate([vp_ref[rows, :], vc_ref[rows, :]], axis=1).astype(BF16)
            qcat = (_lane_cat(q_ref, g * Q_PER_KV * HEAD_DIM, HEAD_DIM) * scale).astype(BF16)
            s = jnp.where(valid, _dot_tn(kt, qcat), NEG)
            sink = jnp.concatenate(
                [jnp.full((1, WINDOW), s_ref[g * Q_PER_KV + i], F32) for i in range(Q_PER_KV)], axis=1)
            m = jnp.maximum(jnp.max(s, axis=0, keepdims=True), sink)
            p = jnp.exp(s - m)
            denom = _colsum(p) + jnp.exp(sink - m)
            probs = (p / denom).astype(BF16)
            out = _dot(vt, probs)
            lse = m + jnp.log(denom)
            for i in range(Q_PER_KV):
                h = g * Q_PER_KV + i
                o_ref[h * HEAD_DIM:(h + 1) * HEAD_DIM, :] = out[:, i * WINDOW:(i + 1) * WINDOW]
                lse_ref[h:h + 1, :] = lse[:, i * WINDOW:(i + 1) * WINDOW]

    kb = OFF_K // KV_DIM
    vb = OFF_V // KV_DIM
    prev = lambda n: jnp.maximum(n - 1, 0)
    return pl.pallas_call(
        body,
        name="attn_fwd",
        grid=(nb,),
        in_specs=[
            pl.BlockSpec(memory_space=pltpu.SMEM),
            pl.BlockSpec((Q_DIM, WINDOW), lambda n: (0, n)),
            pl.BlockSpec((KV_DIM, WINDOW), lambda n: (kb, n)),
            pl.BlockSpec((KV_DIM, WINDOW), lambda n: (kb, prev(n))),
            pl.BlockSpec((KV_DIM, WINDOW), lambda n: (vb, n)),
            pl.BlockSpec((KV_DIM, WINDOW), lambda n: (vb, prev(n))),
        ],
        out_specs=[pl.BlockSpec((Q_DIM, WINDOW), lambda n: (0, n)), pl.BlockSpec((N_Q_HEADS, WINDOW), lambda n: (0, n))],
        out_shape=[jax.ShapeDtypeStruct((Q_DIM, t), F32), jax.ShapeDtypeStruct((N_Q_HEADS, t), F32)],
        compiler_params=_cparams("parallel"),
    )(sinks, proj, proj, proj, proj, proj)


def _attn_bwd(proj, sinks, out, lse, dout):
    t = proj.shape[1]
    nb = t // WINDOW
    scale = HEAD_DIM ** -0.5

    def body(s_ref, q_ref, kc_ref, kp_ref, vc_ref, vp_ref, o_ref, lse_ref, do_ref,
             dq_ref, dk_ref, dv_ref, ds_ref, dk_carry, dv_carry):
        step = pl.program_id(0)
        n = nb - 1 - step

        @pl.when(step == 0)
        def _():
            dk_carry[...] = jnp.zeros_like(dk_carry)
            dv_carry[...] = jnp.zeros_like(dv_carry)
            ds_ref[...] = jnp.zeros_like(ds_ref)

        valid = _attn_mask(n)
        for g in range(N_KV_HEADS):
            rows = slice(g * HEAD_DIM, (g + 1) * HEAD_DIM)
            q0 = g * Q_PER_KV * HEAD_DIM
            kt = jnp.concatenate([kp_ref[rows, :], kc_ref[rows, :]], axis=1).astype(BF16)
            vt = jnp.concatenate([vp_ref[rows, :], vc_ref[rows, :]], axis=1).astype(BF16)
            qf = _lane_cat(q_ref, q0, HEAD_DIM)
            qcat = qf.astype(BF16)
            ocat = _lane_cat(o_ref, q0, HEAD_DIM)
            docat = _lane_cat(do_ref, q0, HEAD_DIM)
            dob = docat.astype(BF16)
            lse_cat = jnp.concatenate(
                [lse_ref[g * Q_PER_KV + i:g * Q_PER_KV + i + 1, :] for i in range(Q_PER_KV)], axis=1)
            sink = jnp.concatenate(
                [jnp.full((1, WINDOW), s_ref[g * Q_PER_KV + i], F32) for i in range(Q_PER_KV)], axis=1)
            s = jnp.where(valid, _dot_tn(kt, (qf * scale).astype(BF16)), NEG)
            p = jnp.exp(s - lse_cat)
            dp = _dot_tn(vt, dob)
            delta = _colsum(docat * ocat)
            dsc = (p * (dp - delta)).astype(BF16)
            dsink_row = -jnp.exp(sink - lse_cat) * delta
            dq = _dot(kt, dsc) * scale
            dk = _dot_nt(qcat, dsc) * scale
            dv = _dot_nt(dob, p.astype(BF16))
            for i in range(Q_PER_KV):
                h = g * Q_PER_KV + i
                dq_ref[h * HEAD_DIM:(h + 1) * HEAD_DIM, :] = dq[:, i * WINDOW:(i + 1) * WINDOW].astype(BF16)
                ds_ref[h:h + 1, :] += _rowsum(dsink_row[:, i * WINDOW:(i + 1) * WINDOW])
            dk_ref[rows, :] = (dk[:, WINDOW:] + dk_carry[rows, :]).astype(BF16)
            dv_ref[rows, :] = (dv[:, WINDOW:] + dv_carry[rows, :]).astype(BF16)
            dk_carry[rows, :] = dk[:, :WINDOW]
            dv_carry[rows, :] = dv[:, :WINDOW]

    kb = OFF_K // KV_DIM
    vb = OFF_V // KV_DIM
    cur = lambda i: nb - 1 - i
    prev = lambda i: jnp.maximum(nb - 2 - i, 0)
    qspec = pl.BlockSpec((Q_DIM, WINDOW), lambda i: (0, cur(i)))
    kvspec = pl.BlockSpec((KV_DIM, WINDOW), lambda i: (0, cur(i)))
    return pl.pallas_call(
        body,
        name="attn_bwd",
        grid=(nb,),
        in_specs=[
            pl.BlockSpec(memory_space=pltpu.SMEM),
            qspec,
            pl.BlockSpec((KV_DIM, WINDOW), lambda i: (kb, cur(i))),
            pl.BlockSpec((KV_DIM, WINDOW), lambda i: (kb, prev(i))),
            pl.BlockSpec((KV_DIM, WINDOW), lambda i: (vb, cur(i))),
            pl.BlockSpec((KV_DIM, WINDOW), lambda i: (vb, prev(i))),
            qspec,
            pl.BlockSpec((N_Q_HEADS, WINDOW), lambda i: (0, cur(i))),
            qspec,
        ],
        out_specs=[qspec, kvspec, kvspec, pl.BlockSpec((N_Q_HEADS, 1), lambda i: (0, 0))],
        out_shape=[
            jax.ShapeDtypeStruct((Q_DIM, t), BF16),
            jax.ShapeDtypeStruct((KV_DIM, t), BF16),
            jax.ShapeDtypeStruct((KV_DIM, t), BF16),
            jax.ShapeDtypeStruct((N_Q_HEADS, 1), F32),
        ],
        scratch_shapes=[pltpu.VMEM((KV_DIM, WINDOW), F32), pltpu.VMEM((KV_DIM, WINDOW), F32)],
        compiler_params=_cparams("arbitrary"),
    )(sinks, proj, proj, proj, proj, proj, out, lse, dout)


CONV_ROWS = 256


def _conv_silu_fwd(proj, w_col, b_col):
    t = proj.shape[1]
    r0 = OFF_X // CONV_ROWS

    def body(x_ref, w_ref, b_ref, o_ref):
        def strip(rows):
            y, _ = _causal_conv(x_ref[rows, :], w_ref[rows, :], b_ref[rows, :])
            o_ref[rows, :] = y * _sigmoid(y)

        strip(slice(None))

    return pl.pallas_call(
        body,
        name="ssd_conv_fwd",
        grid=(XBC_DIM // CONV_ROWS,),
        in_specs=[
            pl.BlockSpec((CONV_ROWS, t), lambda i: (r0 + i, 0)),
            pl.BlockSpec((CONV_ROWS, SSD_CONV), lambda i: (i, 0)),
            pl.BlockSpec((CONV_ROWS, 1), lambda i: (i, 0)),
        ],
        out_specs=pl.BlockSpec((CONV_ROWS, t), lambda i: (i, 0)),
        out_shape=jax.ShapeDtypeStruct((XBC_DIM, t), F32),
        compiler_params=_cparams("parallel"),
    )(proj, w_col, b_col)


def _conv_silu_bwd(proj, w_col, b_col, dout, row0, name):
    t = proj.shape[1]
    nrows = dout.shape[0]
    p0 = (OFF_X + row0) // CONV_ROWS
    c0 = row0 // CONV_ROWS

    def body(x_ref, w_ref, b_ref, do_ref, dx_ref, dwb_ref):
        def strip(rows):
            xv = x_ref[rows, :]
            wv = w_ref[rows, :]
            y, shifted = _causal_conv(xv, wv, b_ref[rows, :])
            sg = _sigmoid(y)
            dy = do_ref[rows, :] * (sg * (1.0 + y * (1.0 - sg)))
            dx, dwb_ref[rows, :] = _causal_conv_bwd(dy, xv, shifted, wv)
            dx_ref[rows, :] = dx.astype(BF16)

        strip(slice(None))

    return pl.pallas_call(
        body,
        name=name,
        grid=(nrows // CONV_ROWS,),
        in_specs=[
            pl.BlockSpec((CONV_ROWS, t), lambda i: (p0 + i, 0)),
            pl.BlockSpec((CONV_ROWS, SSD_CONV), lambda i: (c0 + i, 0)),
            pl.BlockSpec((CONV_ROWS, 1), lambda i: (c0 + i, 0)),
            pl.BlockSpec((CONV_ROWS, t), lambda i: (i, 0)),
        ],
        out_specs=[pl.BlockSpec((CONV_ROWS, t), lambda i: (i, 0)), pl.BlockSpec((CONV_ROWS, 128), lambda i: (i, 0))],
        out_shape=[jax.ShapeDtypeStruct((nrows, t), BF16), jax.ShapeDtypeStruct((nrows, 128), F32)],
        compiler_params=_cparams("parallel"),
    )(proj, w_col, b_col, dout)


GROUP_ROWS = HEADS_PER_GROUP * SSD_HEAD_DIM


def _ssd_specs(order):
    xb = D_INNER // BC_DIM
    dtb = OFF_DT // N_SSD_HEADS
    col = pl.BlockSpec((N_SSD_HEADS, 1), lambda c: (0, 0))
    return [
        pl.BlockSpec((D_INNER, CHUNK), lambda c: (0, order(c))),
        pl.BlockSpec((BC_DIM, CHUNK), lambda c: (xb, order(c))),
        pl.BlockSpec((BC_DIM, CHUNK), lambda c: (xb + 1, order(c))),
        pl.BlockSpec((N_SSD_HEADS, CHUNK), lambda c: (dtb, order(c))),
        col, col, col,
    ]


def _ssd_common(dt_ref, dtb_ref, alog_ref):
    z = dt_ref[...] + dtb_ref[...]
    dt = _softplus(z)
    a_neg = -jnp.exp(alog_ref[...])
    d_a = dt * a_neg
    row = lax.broadcasted_iota(jnp.int32, (CHUNK, CHUNK), 0)
    colm = lax.broadcasted_iota(jnp.int32, (CHUNK, CHUNK), 1)
    upper = (row <= colm).astype(F32)
    a_cs = jnp.dot(d_a, upper, precision=HIGHEST, preferred_element_type=F32)
    a_last = _rowsum(d_a)
    return z, dt, a_neg, a_cs, a_last, row >= colm, row == colm


def _decay(a_row, causal):
    a_s = jnp.broadcast_to(a_row, (CHUNK, CHUNK))
    seg = a_s.T - a_s
    return jnp.where(causal, jnp.exp(jnp.where(causal, seg, 0.0)), 0.0)


def _ssd_fwd(xbc, proj, dtb_col, alog_col, dsk_col):
    t = xbc.shape[1]
    nc = t // CHUNK

    def body(xs_ref, b_ref, c_ref, dt_ref, dtb_ref, alog_ref, dsk_ref, y_ref, hst_ref, h_scr):
        @pl.when(pl.program_id(0) == 0)
        def _():
            h_scr[...] = jnp.zeros_like(h_scr)

        _, dt, _, a_cs, a_last, causal, _ = _ssd_common(dt_ref, dtb_ref, alog_ref)
        hst_ref[0] = h_scr[...]
        dsk = dsk_ref[...]
        for g in range(N_SSD_GROUPS):
            grows = slice(g * D_STATE, (g + 1) * D_STATE)
            bb = b_ref[grows, :].astype(BF16)
            cb_ = c_ref[grows, :].astype(BF16)
            cb = _dot_tn(cb_, bb)
            for j in range(g * HEADS_PER_GROUP, (g + 1) * HEADS_PER_GROUP):
                rows = slice(j * SSD_HEAD_DIM, (j + 1) * SSD_HEAD_DIM)
                a = a_cs[j:j + 1, :]
                m = (cb * _decay(a, causal)).astype(BF16)
                xs = xs_ref[rows, :]
                xc = xs * dt[j:j + 1, :]
                hj = h_scr[rows, :]
                y = _dot_nt(xc.astype(BF16), m) + _dot(hj.astype(BF16), cb_) * jnp.exp(a) + dsk[j:j + 1, :] * xs
                y_ref[rows, :] = y
                al = a_last[j:j + 1, :]
                w = jnp.exp(al - a)
                h_scr[rows, :] = jnp.exp(al) * hj + _dot_nt((xc * w).astype(BF16), bb)

    return pl.pallas_call(
        body,
        name="ssd_fwd",
        grid=(nc,),
        in_specs=_ssd_specs(lambda c: c),
        out_specs=[
            pl.BlockSpec((D_INNER, CHUNK), lambda c: (0, c)),
            pl.BlockSpec((1, D_INNER, D_STATE), lambda c: (c, 0, 0)),
        ],
        out_shape=[
            jax.ShapeDtypeStruct((D_INNER, t), F32),
            jax.ShapeDtypeStruct((nc, D_INNER, D_STATE), F32),
        ],
        scratch_shapes=[pltpu.VMEM((D_INNER, D_STATE), F32)],
        compiler_params=_cparams("arbitrary"),
    )(xbc, xbc, xbc, proj, dtb_col, alog_col, dsk_col)


def _ssd_bwd(xbc, proj, dtb_col, alog_col, dsk_col, hst, dy):
    t = xbc.shape[1]
    nc = t // CHUNK
    rev = lambda c: nc - 1 - c

    def body(xs_ref, b_ref, c_ref, dt_ref, dtb_ref, alog_ref, dsk_ref, hst_ref, dy_ref,
             dxs_ref, db_ref, dc_ref, ddt_ref, dalog_ref, ddsk_ref, ddtb_ref, dh_scr, da_scr, ddt_scr, dd_scr):
        @pl.when(pl.program_id(0) == 0)
        def _():
            dh_scr[...] = jnp.zeros_like(dh_scr)
            dalog_ref[...] = jnp.zeros_like(dalog_ref)
            ddsk_ref[...] = jnp.zeros_like(ddsk_ref)
            ddtb_ref[...] = jnp.zeros_like(ddtb_ref)

        z, dt, a_neg, a_cs, a_last, causal, eye = _ssd_common(dt_ref, dtb_ref, alog_ref)
        dsk = dsk_ref[...]
        last_lane = lax.broadcasted_iota(jnp.int32, (1, CHUNK), 1) == CHUNK - 1
        for g in range(N_SSD_GROUPS):
            grows = slice(g * D_STATE, (g + 1) * D_STATE)
            bb = b_ref[grows, :].astype(BF16)
            cb_ = c_ref[grows, :].astype(BF16)
            cb = _dot_tn(cb_, bb)
            dcb = jnp.zeros((CHUNK, CHUNK), F32)
            dc_acc = jnp.zeros((D_STATE, CHUNK), F32)
            db_acc = jnp.zeros((D_STATE, CHUNK), F32)
            for j in range(g * HEADS_PER_GROUP, (g + 1) * HEADS_PER_GROUP):
                rows = slice(j * SSD_HEAD_DIM, (j + 1) * SSD_HEAD_DIM)
                a = a_cs[j:j + 1, :]
                al = a_last[j:j + 1, :]
                lam = _decay(a, causal)
                mf = cb * lam
                xs = xs_ref[rows, :]
                dtj = dt[j:j + 1, :]
                xc = xs * dtj
                w = jnp.exp(al - a)
                e = jnp.exp(a)
                gam = jnp.exp(al)
                hj = hst_ref[0, rows, :]
                hjb = hj.astype(BF16)
                dyv = dy_ref[rows, :]
                dyb = dyv.astype(BF16)
                dd_scr[j:j + 1, :] = _colsum(dyv * xs)
                gb = (dyv * e).astype(BF16)
                dh_in = _dot_nt(gb, cb_)
                dc_acc = dc_acc + _dot_tn(hjb, gb)
                yoff = _dot(hjb, cb_) * e
                da = _colsum(dyv * yoff)
                dm = _dot_tn(dyb, xc.astype(BF16))
                dxc = _dot(dyb, mf.astype(BF16))
                dcb = dcb + dm * lam
                nmat = dm * mf
                rs = jnp.broadcast_to(_rowsum(nmat), (CHUNK, CHUNK))
                da = da + _colsum(jnp.where(eye, rs, 0.0)) - _colsum(nmat)
                ds = dh_scr[rows, :]
                dsb = ds.astype(BF16)
                t1 = _dot(dsb, bb)
                xcw = xc * w
                dxc = dxc + w * t1
                dww = _colsum(xcw * t1)
                da_l = _rowsum(dww) + _rowsum(_colsum(ds * hj)) * gam
                da = da - dww + jnp.where(last_lane, da_l, 0.0)
                db_acc = db_acc + _dot_tn(dsb, xcw.astype(BF16))
                dh_scr[rows, :] = gam * ds + dh_in
                dxs_ref[rows, :] = dsk[j:j + 1, :] * dyv + dxc * dtj
                da_scr[j:j + 1, :] = da
                ddt_scr[j:j + 1, :] = _colsum(dxc * xs)
            dcbb = dcb.astype(BF16)
            dc_ref[grows, :] = dc_acc + _dot_nt(bb, dcbb)
            db_ref[grows, :] = db_acc + _dot(cb_, dcbb)
        dda = jnp.dot(da_scr[...], causal.astype(F32), precision=HIGHEST, preferred_element_type=F32)
        ddt = ddt_scr[...] + dda * a_neg
        ddt_raw = ddt * _sigmoid(z)
        ddt_ref[...] = ddt_raw
        ddtb_ref[...] += _rowsum(ddt_raw)
        dalog_ref[...] += _rowsum(dda * dt) * a_neg
        ddsk_ref[...] += _rowsum(dd_scr[...])

    col = pl.BlockSpec((N_SSD_HEADS, 1), lambda c: (0, 0))
    bc = pl.BlockSpec((BC_DIM, CHUNK), lambda c: (0, rev(c)))
    xs_spec = pl.BlockSpec((D_INNER, CHUNK), lambda c: (0, rev(c)))
    small = pltpu.VMEM((N_SSD_HEADS, CHUNK), F32)
    return pl.pallas_call(
        body,
        name="ssd_bwd",
        grid=(nc,),
        in_specs=_ssd_specs(rev) + [pl.BlockSpec((1, D_INNER, D_STATE), lambda c: (rev(c), 0, 0)), xs_spec],
        out_specs=[xs_spec, bc, bc, pl.BlockSpec((N_SSD_HEADS, CHUNK), lambda c: (0, rev(c))), col, col, col],
        out_shape=[
            jax.ShapeDtypeStruct((D_INNER, t), F32),
            jax.ShapeDtypeStruct((BC_DIM, t), F32),
            jax.ShapeDtypeStruct((BC_DIM, t), F32),
            jax.ShapeDtypeStruct((N_SSD_HEADS, t), F32),
            jax.ShapeDtypeStruct((N_SSD_HEADS, 1), F32),
            jax.ShapeDtypeStruct((N_SSD_HEADS, 1), F32),
            jax.ShapeDtypeStruct((N_SSD_HEADS, 1), F32),
        ],
        scratch_shapes=[pltpu.VMEM((D_INNER, D_STATE), F32), small, small, small],
        compiler_params=_cparams("arbitrary"),
    )(xbc, xbc, xbc, proj, dtb_col, alog_col, dsk_col, hst, dy)


GN_ROWS = D_INNER // N_SSD_GROUPS


def _gnorm_fwd(y, proj, w_col):
    t = y.shape[1]
    tt = _tile(t, (512, 256, 128))
    z0 = OFF_Z // GN_ROWS

    def body(y_ref, z_ref, w_ref, o_ref):
        zv = z_ref[...]
        u = y_ref[...] * (zv * _sigmoid(zv))
        r = lax.rsqrt(jnp.mean(u * u, axis=0, keepdims=True) + EPS)
        o_ref[...] = (u * r * w_ref[...]).astype(BF16)

    blk = pl.BlockSpec((GN_ROWS, tt), lambda g, i: (g, i))
    return pl.pallas_call(
        body,
        name="gnorm_fwd",
        grid=(N_SSD_GROUPS, t // tt),
        in_specs=[blk, pl.BlockSpec((GN_ROWS, tt), lambda g, i: (z0 + g, i)), pl.BlockSpec((GN_ROWS, 1), lambda g, i: (g, 0))],
        out_specs=blk,
        out_shape=jax.ShapeDtypeStruct((D_INNER, t), BF16),
        compiler_params=_cparams("parallel", "parallel"),
    )(y, proj, w_col)


def _gnorm_bwd(dout, y, proj, w_col):
    t = y.shape[1]
    tt = _tile(t, (512, 256, 128))
    z0 = OFF_Z // GN_ROWS

    def body(do_ref, y_ref, z_ref, w_ref, dy_ref, dz_ref, dw_ref):
        @pl.when(pl.program_id(1) == 0)
        def _():
            dw_ref[...] = jnp.zeros_like(dw_ref)

        zv = z_ref[...]
        yv = y_ref[...]
        sg = _sigmoid(zv)
        sz = zv * sg
        u = yv * sz
        r = lax.rsqrt(jnp.mean(u * u, axis=0, keepdims=True) + EPS)
        xhat = u * r
        dov = do_ref[...]
        dw_ref[...] += _rowsum(dov * xhat)
        dxhat = dov * w_ref[...]
        du = r * (dxhat - xhat * jnp.mean(dxhat * xhat, axis=0, keepdims=True))
        dy_ref[...] = du * sz
        dz_ref[...] = (du * yv * (sg * (1.0 + zv * (1.0 - sg)))).astype(BF16)

    blk = pl.BlockSpec((GN_ROWS, tt), lambda g, i: (g, i))
    col = pl.BlockSpec((GN_ROWS, 1), lambda g, i: (g, 0))
    return pl.pallas_call(
        body,
        name="gnorm_bwd",
        grid=(N_SSD_GROUPS, t // tt),
        in_specs=[blk, blk, pl.BlockSpec((GN_ROWS, tt), lambda g, i: (z0 + g, i)), col],
        out_specs=[blk, blk, col],
        out_shape=[jax.ShapeDtypeStruct((D_INNER, t), F32), jax.ShapeDtypeStruct((D_INNER, t), BF16),
                   jax.ShapeDtypeStruct((D_INNER, 1), F32)],
        compiler_params=_cparams("parallel", "arbitrary"),
    )(dout, y, proj, w_col)


GATE_ROWS = 32


def _gate_specs(t):
    ga0 = OFF_GA // GATE_ROWS
    gs0 = OFF_GS // GATE_ROWS
    nr = D_MODEL // GATE_ROWS
    blk = pl.BlockSpec((GATE_ROWS, t), lambda r: (r, 0))
    return blk, [
        pl.BlockSpec((GATE_ROWS, t), lambda r: (ga0 + r, 0)),
        pl.BlockSpec((GATE_ROWS, t), lambda r: (gs0 + r, 0)),
        pl.BlockSpec((GATE_ROWS, 1), lambda r: (r, 0)),
        pl.BlockSpec((GATE_ROWS, 1), lambda r: (nr + r, 0)),
        blk, blk,
    ]


def _gate_fwd(proj, b_col, attn, ssd):
    t = proj.shape[1]
    blk, specs = _gate_specs(t)

    def body(ga_ref, gs_ref, ba_ref, bs_ref, a_ref, s_ref, o_ref):
        o_ref[...] = (_sigmoid(ga_ref[...] + ba_ref[...]) * a_ref[...]
                      + _sigmoid(gs_ref[...] + bs_ref[...]) * s_ref[...]).astype(BF16)

    return pl.pallas_call(
        body,
        name="gate_fwd",
        grid=(D_MODEL // GATE_ROWS,),
        in_specs=specs,
        out_specs=blk,
        out_shape=jax.ShapeDtypeStruct((D_MODEL, t), BF16),
        compiler_params=_cparams("parallel"),
    )(proj, proj, b_col, b_col, attn, ssd)


def _gate_bwd(proj, b_col, attn, ssd, dmix):
    t = proj.shape[1]
    blk, specs = _gate_specs(t)

    def body(ga_ref, gs_ref, ba_ref, bs_ref, a_ref, s_ref, dm_ref, da_ref, dso_ref, dga_ref, dgs_ref, dba_ref, dbs_ref):
        dm = dm_ref[...]
        sa = _sigmoid(ga_ref[...] + ba_ref[...])
        ss = _sigmoid(gs_ref[...] + bs_ref[...])
        da_ref[...] = (dm * sa).astype(BF16)
        dso_ref[...] = (dm * ss).astype(BF16)
        dga = dm * a_ref[...] * sa * (1.0 - sa)
        dgs = dm * s_ref[...] * ss * (1.0 - ss)
        dga_ref[...] = dga.astype(BF16)
        dgs_ref[...] = dgs.astype(BF16)
        dba_ref[...] = _rowsum(dga)
        dbs_ref[...] = _rowsum(dgs)

    col = pl.BlockSpec((GATE_ROWS, 1), lambda r: (r, 0))
    act = jax.ShapeDtypeStruct((D_MODEL, t), BF16)
    bias = jax.ShapeDtypeStruct((D_MODEL, 1), F32)
    return pl.pallas_call(
        body,
        name="gate_bwd",
        grid=(D_MODEL // GATE_ROWS,),
        in_specs=specs + [blk],
        out_specs=[blk, blk, blk, blk, col, col],
        out_shape=[act, act, act, act, bias, bias],
        compiler_params=_cparams("parallel"),
    )(proj, proj, b_col, b_col, attn, ssd, dmix)


FFN_ROWS = 256


def _ffn_fwd(u0, w_col, b_col):
    t = u0.shape[2]

    def body(u_ref, w_ref, b_ref, o_ref):
        def strip(rows):
            val, _ = _causal_conv(u_ref[0, rows, :], w_ref[0, rows, :], b_ref[0, rows, :])
            gt, _ = _causal_conv(u_ref[1, rows, :], w_ref[1, rows, :], b_ref[1, rows, :])
            o_ref[rows, :] = (gt * _sigmoid(gt) * val).astype(BF16)

        strip(slice(None))

    return pl.pallas_call(
        body,
        name="ffn_fwd",
        grid=(D_FF // FFN_ROWS,),
        in_specs=[
            pl.BlockSpec((2, FFN_ROWS, t), lambda i: (0, i, 0)),
            pl.BlockSpec((2, FFN_ROWS, FFN_CONV), lambda i: (0, i, 0)),
            pl.BlockSpec((2, FFN_ROWS, 1), lambda i: (0, i, 0)),
        ],
        out_specs=pl.BlockSpec((FFN_ROWS, t), lambda i: (i, 0)),
        out_shape=jax.ShapeDtypeStruct((D_FF, t), BF16),
        compiler_params=_cparams("parallel"),
    )(u0, w_col, b_col)


def _ffn_bwd(u0, w_col, b_col, dg):
    t = u0.shape[2]

    def body(u_ref, w_ref, b_ref, dg_ref, du_ref, dwb_ref):
        def strip(rows):
            xval, wval = u_ref[0, rows, :], w_ref[0, rows, :]
            xgt, wgt = u_ref[1, rows, :], w_ref[1, rows, :]
            val, sh_val = _causal_conv(xval, wval, b_ref[0, rows, :])
            gt, sh_gt = _causal_conv(xgt, wgt, b_ref[1, rows, :])
            sg = _sigmoid(gt)
            dgv = dg_ref[rows, :]
            dval = dgv * (gt * sg)
            dgt = dgv * val * (sg * (1.0 + gt * (1.0 - sg)))
            dx, dwb_ref[0, rows, :] = _causal_conv_bwd(dval, xval, sh_val, wval)
            du_ref[0, rows, :] = dx.astype(BF16)
            dx, dwb_ref[1, rows, :] = _causal_conv_bwd(dgt, xgt, sh_gt, wgt)
            du_ref[1, rows, :] = dx.astype(BF16)

        strip(slice(None))

    return pl.pallas_call(
        body,
        name="ffn_bwd",
        grid=(D_FF // FFN_ROWS,),
        in_specs=[
            pl.BlockSpec((2, FFN_ROWS, t), lambda i: (0, i, 0)),
            pl.BlockSpec((2, FFN_ROWS, FFN_CONV), lambda i: (0, i, 0)),
            pl.BlockSpec((2, FFN_ROWS, 1), lambda i: (0, i, 0)),
            pl.BlockSpec((FFN_ROWS, t), lambda i: (i, 0)),
        ],
        out_specs=[pl.BlockSpec((2, FFN_ROWS, t), lambda i: (0, i, 0)), pl.BlockSpec((2, FFN_ROWS, 128), lambda i: (0, i, 0))],
        out_shape=[jax.ShapeDtypeStruct((2, D_FF, t), BF16), jax.ShapeDtypeStruct((2, D_FF, 128), F32)],
        compiler_params=_cparams("parallel"),
    )(u0, w_col, b_col, dg)


def _adamw_math(w, g, m, v):
    m = ADAM_B1 * m + (1.0 - ADAM_B1) * g
    v = ADAM_B2 * v + (1.0 - ADAM_B2) * (g * g)
    m_hat = m / (1.0 - ADAM_B1 ** ADAM_STEP)
    v_hat = v / (1.0 - ADAM_B2 ** ADAM_STEP)
    delta = -ADAM_LR * (m_hat / (jnp.sqrt(v_hat) + ADAM_EPS) + ADAM_WD * w)
    return delta, m, v


def _adamw_sharded(parts, w, m, v, name):
    r, c = w.shape
    tc = _tile(c, (256, 128))

    def body(p_ref, w_ref, m_ref, v_ref, g_ref, d_ref, nm_ref, nv_ref):
        g = p_ref[0].astype(F32)
        for s in range(1, N_DEV):
            g = g + p_ref[s].astype(F32)
        g_ref[...] = g
        d_ref[...], nm_ref[...], nv_ref[...] = _adamw_math(w_ref[...], g, m_ref[...], v_ref[...])

    blk = pl.BlockSpec((r, tc), lambda i: (0, i))
    out = jax.ShapeDtypeStruct((r, c), F32)
    return pl.pallas_call(
        body,
        name=name,
        grid=(c // tc,),
        in_specs=[pl.BlockSpec((N_DEV, r, tc), lambda i: (0, 0, i)), blk, blk, blk],
        out_specs=[blk, blk, blk, blk],
        out_shape=[out, out, out, out],
        compiler_params=_cparams("parallel"),
    )(parts, w, m, v)


def _sum_slots(parts):
    _, r, c = parts.shape

    def body(p_ref, o_ref):
        g = p_ref[0]
        for s in range(1, N_DEV):
            g = g + p_ref[s]
        o_ref[...] = g

    return pl.pallas_call(body, name="sum_small_grads", out_shape=jax.ShapeDtypeStruct((r, c), F32))(parts)


def _adamw_small(g, w, m, v):
    def body(g_ref, w_ref, m_ref, v_ref, d_ref, nm_ref, nv_ref):
        d_ref[...], nm_ref[...], nv_ref[...] = _adamw_math(w_ref[...], g_ref[...], m_ref[...], v_ref[...])

    out = jax.ShapeDtypeStruct(g.shape, F32)
    return pl.pallas_call(body, name="adamw_small", out_shape=[out, out, out])(g, w, m, v)


ANY = pl.BlockSpec(memory_space=pl.ANY)
FLIPS = [(k >> 2 & 1, k >> 1 & 1, k & 1) for k in range(1, N_DEV)]


def _place():
    return lax.axis_index("x"), lax.axis_index("y"), lax.axis_index("c")


def _all_gather(arrays, name):
    n = len(arrays)

    def body(*refs):
        ins, outs = refs[:n], refs[n:2 * n]
        send_sems, recv_sems, local_sems = refs[2 * n:]
        x, y, c = _place()
        me = 4 * x + 2 * y + c
        local = [pltpu.make_async_copy(ins[i], outs[i].at[me], local_sems.at[i]) for i in range(n)]
        for cp in local:
            cp.start()
        sends = []
        for k, (fx, fy, fc) in enumerate(FLIPS):
            for i in range(n):
                cp = pltpu.make_async_remote_copy(
                    src_ref=ins[i], dst_ref=outs[i].at[me], send_sem=send_sems.at[i, k], recv_sem=recv_sems.at[i, k],
                    device_id=(x ^ fx, y ^ fy, c ^ fc), device_id_type=MESH)
                cp.start()
                sends.append(cp)
        for k, (fx, fy, fc) in enumerate(FLIPS):
            src = 4 * (x ^ fx) + 2 * (y ^ fy) + (c ^ fc)
            for i in range(n):
                pltpu.make_async_remote_copy(
                    src_ref=ins[i], dst_ref=outs[i].at[src], send_sem=send_sems.at[i, k], recv_sem=recv_sems.at[i, k],
                    device_id=(x ^ fx, y ^ fy, c ^ fc), device_id_type=MESH).wait_recv()
        for cp in sends:
            cp.wait_send()
        for cp in local:
            cp.wait()

    return pl.pallas_call(
        body,
        name=name,
        in_specs=[ANY] * n,
        out_specs=[ANY] * n,
        out_shape=[jax.ShapeDtypeStruct((N_DEV,) + a.shape, a.dtype) for a in arrays],
        scratch_shapes=[pltpu.SemaphoreType.DMA((n, N_DEV - 1)), pltpu.SemaphoreType.DMA((n, N_DEV - 1)),
                        pltpu.SemaphoreType.DMA((n,))],
    )(*arrays)


HBM = pl.BlockSpec(memory_space=pltpu.HBM)
SEM = pl.BlockSpec(memory_space=pltpu.SEMAPHORE)
EFFECT = pltpu.SideEffectType.DATAFLOW_SIDE_EFFECTING


def _peer_copy(gather, src_ref, land_ref, send_sems, recv_sems, k, sending):
    x, y, c = _place()
    fx, fy, fc = FLIPS[k]
    me = 4 * x + 2 * y + c
    peer = 4 * (x ^ fx) + 2 * (y ^ fy) + (c ^ fc)
    return pltpu.make_async_remote_copy(
        src_ref=src_ref if gather else src_ref.at[peer],
        dst_ref=land_ref.at[me if sending else peer],
        send_sem=send_sems.at[k], recv_sem=recv_sems.at[k],
        device_id=(x ^ fx, y ^ fy, c ^ fc), device_id_type=MESH)


def _gather_start(srcs, name):
    n = len(srcs)
    lands = [lax.empty((N_DEV,) + s.shape, s.dtype) for s in srcs]

    def body(*refs):
        src_refs, land_refs = refs[:n], refs[n:2 * n]
        send, recv = refs[2 * n:3 * n], refs[3 * n:4 * n]
        local_sems = refs[6 * n]
        x, y, c = _place()
        me = 4 * x + 2 * y + c
        local = [pltpu.make_async_copy(src_refs[i], land_refs[i].at[me], local_sems.at[i]) for i in range(n)]
        for cp in local:
            cp.start()
        for i in range(n):
            for k in range(N_DEV - 1):
                _peer_copy(True, src_refs[i], land_refs[i], send[i], recv[i], k, True).start()
        for cp in local:
            cp.wait()

    sem = pltpu.SemaphoreType.DMA((N_DEV - 1,))
    hbm = lambda a: pltpu.HBM(a.shape, a.dtype)
    res = pl.pallas_call(
        body,
        name=name,
        in_specs=[HBM] * (2 * n),
        out_specs=[SEM] * (2 * n) + [HBM] * (2 * n),
        out_shape=[sem] * (2 * n) + [hbm(s) for s in srcs] + [hbm(a) for a in lands],
        input_output_aliases={i: 2 * n + i for i in range(2 * n)},
        scratch_shapes=[pltpu.SemaphoreType.DMA((n,))],
        compiler_params=pltpu.CompilerParams(has_side_effects=EFFECT),
    )(*[pltpu.with_memory_space_constraint(a, pltpu.HBM) for a in list(srcs) + lands])
    return res[:n], res[n:2 * n], res[2 * n:3 * n], res[3 * n:4 * n]


def _exchange_wait(send_sems, recv_sems, src, land, after, gather, name):
    def body(src_ref, land_ref, send_ref, recv_ref, after_ref, src_out, land_out):
        for k in range(N_DEV - 1):
            cp = _peer_copy(gather, src_ref, land_ref, send_ref, recv_ref, k, False)
            cp.wait_send()
            cp.wait_recv()

    hbm = lambda a: pltpu.HBM(a.shape, a.dtype)
    return pl.pallas_call(
        body,
        name=name,
        in_specs=[HBM, HBM, SEM, SEM, ANY],
        out_specs=[HBM, HBM],
        out_shape=[hbm(src), hbm(land)],
        input_output_aliases={0: 0, 1: 1},
        compiler_params=pltpu.CompilerParams(has_side_effects=EFFECT),
    )(src, land, send_sems, recv_sems, after)[1]


def _col(v):
    return v.reshape(-1, 1).astype(F32)


def _local_step(xt, tgt, weight, small):
    t = xt.shape[1]
    n1 = _col(small["norm1_w"])
    n2 = _col(small["norm2_w"])
    nf = _col(small["final_norm_w"])
    bg = _col(small["b_gate"])
    sinks = small["attn_sinks"].reshape(-1).astype(F32)
    cbias = _col(small["ssd_conv_b"])
    dtb = _col(small["dt_bias"])
    alog = _col(small["a_log"])
    dsk = _col(small["d_skip"])
    gnw = _col(small["ssd_norm_w"])
    fb = small["ffn_conv_b"].reshape(2, D_FF, 1)

    xn = _norm_fwd(xt, n1, "norm1_fwd")
    cw = weight("ssd_conv_w", xn).T
    fw = weight("ffn_conv_w", xn).T.reshape(2, D_FF, FFN_CONV)
    w_in_t = weight("w_in", xn)
    proj = _matmul(w_in_t, xn, nt=False, out_dtype=F32, name="mm_in")
    ao, lse = _attn_fwd(proj, sinks)
    w_ao = weight("w_attn_o", ao)
    attn = _matmul(w_ao, ao, nt=False, out_dtype=F32, name="mm_attn_o", tn_a=True)
    xbc = _conv_silu_fwd(proj, cw, cbias)
    y, hst = _ssd_fwd(xbc, proj, dtb, alog, dsk)
    yn = _gnorm_fwd(y, proj, gnw)
    w_so = weight("w_ssd_o", yn)
    ssd = _matmul(w_so, yn, nt=False, out_dtype=F32, name="mm_ssd_o", tn_a=True)
    mix = _gate_fwd(proj, bg, attn, ssd)
    w_out = weight("w_out", mix)
    h1 = _matmul(w_out, mix, nt=False, out_dtype=F32, name="mm_out", add=xt, tn_a=True)
    hn = _norm_fwd(h1, n2, "norm2_fwd")
    w_up_t = weight("w_up", hn)
    u0 = _matmul(w_up_t, hn, nt=False, out_dtype=F32, name="mm_up").reshape(2, D_FF, t)
    gl = _ffn_fwd(u0, fw, fb)
    w_down = weight("w_down", gl)
    h2 = _matmul(w_down, gl, nt=False, out_dtype=F32, name="mm_down", add=h1, tn_a=True)
    dh2, loss, d_nf = _final_norm_loss(h2, tgt, nf)

    g = {}
    handles = {}

    def sending(weight_name, grad, *args, **kwargs):
        out, handles[weight_name] = _matmul(*args, send=grad.reshape(N_DEV, -1, D_MODEL), **kwargs)
        return out

    g_down = _matmul(gl, dh2, nt=True, out_dtype=BF16, name="mm_d_w_down")
    dgl = sending("w_down", g_down, w_down, dh2, nt=False, out_dtype=F32, name="mm_d_glu")
    du0, d_fwb = _ffn_bwd(u0, fw, fb, dgl)
    du0 = du0.reshape(2 * D_FF, t)
    g_up = _matmul(du0, hn, nt=True, out_dtype=BF16, name="mm_d_w_up")
    dhn = sending("w_up", g_up, w_up_t, du0, nt=False, out_dtype=F32, name="mm_d_hn", tn_a=True)
    dh1, d_n2 = _norm_bwd(dhn, h1, n2, dh2, "norm2_bwd")
    g_out = _matmul(mix, dh1, nt=True, out_dtype=BF16, name="mm_d_w_out")
    dmix = sending("w_out", g_out, w_out, dh1, nt=False, out_dtype=F32, name="mm_d_mix")
    d_attn, d_ssd, d_ga, d_gs, d_ba, d_bs = _gate_bwd(proj, bg, attn, ssd, dmix)
    g_ao = _matmul(ao, d_attn, nt=True, out_dtype=BF16, name="mm_d_w_attn_o")
    dao = sending("w_attn_o", g_ao, w_ao, d_attn, nt=False, out_dtype=F32, name="mm_d_ao")
    dq, dk, dv, d_sinks = _attn_bwd(proj, sinks, ao, lse, dao)
    g_so = _matmul(yn, d_ssd, nt=True, out_dtype=BF16, name="mm_d_w_ssd_o")
    dyn = sending("w_ssd_o", g_so, w_so, d_ssd, nt=False, out_dtype=F32, name="mm_d_yn")
    dy, dz, d_gnw = _gnorm_bwd(dyn, y, proj, gnw)
    dxs, dbm, dcm, ddt, d_alog, d_dsk, d_dtb = _ssd_bwd(xbc, proj, dtb, alog, dsk, hst, dy)
    dx_xs, dwb_xs = _conv_silu_bwd(proj, cw, cbias, dxs, 0, "ssd_conv_bwd_x")
    dx_b, dwb_b = _conv_silu_bwd(proj, cw, cbias, dbm, D_INNER, "ssd_conv_bwd_b")
    dx_c, dwb_c = _conv_silu_bwd(proj, cw, cbias, dcm, D_INNER + BC_DIM, "ssd_conv_bwd_c")
    dwb_conv = jnp.concatenate([dwb_xs, dwb_b, dwb_c], axis=0)
    dproj = jnp.concatenate([dq, dk, dv, dz, dx_xs, dx_b, dx_c, ddt.astype(BF16), d_ga, d_gs], axis=0)
    g_in = _matmul(dproj, xn, nt=True, out_dtype=BF16, name="mm_d_w_in")
    dxn = sending("w_in", g_in, w_in_t, dproj, nt=False, out_dtype=F32, name="mm_d_xn", tn_a=True)
    dx, d_n1 = _norm_bwd(dxn, xt, n1, dh1, "norm1_bwd")

    g["norm1_w"] = d_n1
    g["b_gate"] = jnp.concatenate([d_ba, d_bs], axis=0)
    g["attn_sinks"] = d_sinks
    g["ssd_conv_w"] = dwb_conv[:, :SSD_CONV].T
    g["ssd_conv_b"] = dwb_conv[:, SSD_CONV]
    g["dt_bias"] = d_dtb
    g["a_log"] = d_alog
    g["d_skip"] = d_dsk
    g["ssd_norm_w"] = d_gnw
    g["norm2_w"] = d_n2
    d_fwb = d_fwb.reshape(2 * D_FF, 128)
    g["ffn_conv_w"] = d_fwb[:, :FFN_CONV].T
    g["ffn_conv_b"] = d_fwb[:, FFN_CONV]
    g["final_norm_w"] = d_nf
    return loss, dx, g, handles


SHARDED = ("w_in", "w_attn_o", "w_ssd_o", "w_out", "w_up", "w_down")
SMALL = ("norm1_w", "b_gate", "attn_sinks", "ssd_conv_w", "ssd_conv_b", "dt_bias", "a_log", "d_skip", "ssd_norm_w",
         "norm2_w", "ffn_conv_w", "ffn_conv_b", "final_norm_w")
SMALL_SHAPES = {"norm1_w": (1, D_MODEL), "b_gate": (1, 2 * D_MODEL), "attn_sinks": (1, N_Q_HEADS),
                "ssd_conv_w": (1, SSD_CONV, XBC_DIM), "ssd_conv_b": (1, XBC_DIM), "dt_bias": (1, N_SSD_HEADS),
                "a_log": (1, N_SSD_HEADS), "d_skip": (1, N_SSD_HEADS), "ssd_norm_w": (1, D_INNER),
                "norm2_w": (1, D_MODEL), "ffn_conv_w": (1, FFN_CONV, 2 * D_FF), "ffn_conv_b": (1, 2 * D_FF),
                "final_norm_w": (D_MODEL,)}
WEIGHT_ORDER = ("norm1_w", "w_in", "b_gate", "attn_sinks", "w_attn_o", "ssd_conv_w", "ssd_conv_b", "dt_bias", "a_log",
                "d_skip", "ssd_norm_w", "w_ssd_o", "w_out", "norm2_w", "w_up", "ffn_conv_w", "ffn_conv_b", "w_down",
                "final_norm_w")


def _pack(parts):
    flat = jnp.concatenate([p.reshape(-1).astype(F32) for p in parts])
    rows = -(-flat.shape[0] // 1024) * 8
    return jnp.pad(flat, (0, rows * 128 - flat.shape[0])).reshape(rows, 128)


def _unpack(packed, shapes):
    flat = packed.reshape(-1)
    out, pos = [], 0
    for shp in shapes:
        size = 1
        for d in shp:
            size *= d
        out.append(flat[pos:pos + size].reshape(shp))
        pos += size
    return out


def kernel(x, norm1_w, w_in, b_gate, attn_sinks, w_attn_o, ssd_conv_w, ssd_conv_b, dt_bias, a_log, d_skip, ssd_norm_w, w_ssd_o, w_out, norm2_w, w_up, ffn_conv_w, ffn_conv_b, w_down, final_norm_w, loss_target, m_norm1_w, m_w_in, m_b_gate, m_attn_sinks, m_w_attn_o, m_ssd_conv_w, m_ssd_conv_b, m_dt_bias, m_a_log, m_d_skip, m_ssd_norm_w, m_w_ssd_o, m_w_out, m_norm2_w, m_w_up, m_ffn_conv_w, m_ffn_conv_b, m_w_down, m_final_norm_w, v_norm1_w, v_w_in, v_b_gate, v_attn_sinks, v_w_attn_o, v_ssd_conv_w, v_ssd_conv_b, v_dt_bias, v_a_log, v_d_skip, v_ssd_norm_w, v_w_ssd_o, v_w_out, v_norm2_w, v_w_up, v_ffn_conv_w, v_ffn_conv_b, v_w_down, v_final_norm_w):
    w = dict(norm1_w=norm1_w, w_in=w_in, b_gate=b_gate, attn_sinks=attn_sinks, w_attn_o=w_attn_o, ssd_conv_w=ssd_conv_w, ssd_conv_b=ssd_conv_b, dt_bias=dt_bias, a_log=a_log, d_skip=d_skip, ssd_norm_w=ssd_norm_w, w_ssd_o=w_ssd_o, w_out=w_out, norm2_w=norm2_w, w_up=w_up, ffn_conv_w=ffn_conv_w, ffn_conv_b=ffn_conv_b, w_down=w_down, final_norm_w=final_norm_w)
    m = dict(norm1_w=m_norm1_w, w_in=m_w_in, b_gate=m_b_gate, attn_sinks=m_attn_sinks, w_attn_o=m_w_attn_o, ssd_conv_w=m_ssd_conv_w, ssd_conv_b=m_ssd_conv_b, dt_bias=m_dt_bias, a_log=m_a_log, d_skip=m_d_skip, ssd_norm_w=m_ssd_norm_w, w_ssd_o=m_w_ssd_o, w_out=m_w_out, norm2_w=m_norm2_w, w_up=m_w_up, ffn_conv_w=m_ffn_conv_w, ffn_conv_b=m_ffn_conv_b, w_down=m_w_down, final_norm_w=m_final_norm_w)
    v = dict(norm1_w=v_norm1_w, w_in=v_w_in, b_gate=v_b_gate, attn_sinks=v_attn_sinks, w_attn_o=v_w_attn_o, ssd_conv_w=v_ssd_conv_w, ssd_conv_b=v_ssd_conv_b, dt_bias=v_dt_bias, a_log=v_a_log, d_skip=v_d_skip, ssd_norm_w=v_ssd_norm_w, w_ssd_o=v_w_ssd_o, w_out=v_w_out, norm2_w=v_norm2_w, w_up=v_w_up, ffn_conv_w=v_ffn_conv_w, ffn_conv_b=v_ffn_conv_b, w_down=v_w_down, final_norm_w=v_final_norm_w)
    me = 4 * lax.axis_index("x") + 2 * lax.axis_index("y") + lax.axis_index("c")
    conv_cols = XBC_DIM // N_DEV
    ffn_cols = 2 * D_FF // N_DEV

    shards = {"ssd_conv_w": ssd_conv_w[0], "ffn_conv_w": ffn_conv_w[0], "w_in": w_in[0].T.astype(BF16),
              "w_attn_o": w_attn_o[0].astype(BF16), "w_ssd_o": w_ssd_o[0].astype(BF16), "w_out": w_out[0].astype(BF16),
              "w_up": w_up[0].T.astype(BF16), "w_down": w_down[0].astype(BF16)}
    order = list(shards)
    g_send, g_recv, g_src, g_land = _gather_start(list(shards.values()), "gather_start")

    def weight(name, after):
        i = order.index(name)
        land = _exchange_wait(g_send[i], g_recv[i], g_src[i], g_land[i], after, True, "gather_wait_" + name)
        if name == "ssd_conv_w":
            return jnp.transpose(land, (1, 0, 2)).reshape(SSD_CONV, XBC_DIM)
        if name == "ffn_conv_w":
            return jnp.transpose(land, (1, 0, 2)).reshape(FFN_CONV, 2 * D_FF)
        return land.reshape(-1, D_MODEL)

    small = {k: w[k][0] if k != "final_norm_w" else w[k] for k in SMALL}
    loss, dx, g, pending = _local_step(x[0].T, loss_target[0].T, weight, small)

    packed = _pack([loss] + [g[k] for k in SMALL])
    total = _sum_slots(_all_gather([packed], "gather_small_grads")[0])
    tot = _unpack(total, [(1,)] + [SMALL_SHAPES[k] for k in SMALL])
    loss_sum = tot[0].reshape(())
    gs = dict(zip(SMALL, tot[1:]))
    gs["ssd_conv_w"] = lax.dynamic_slice_in_dim(gs["ssd_conv_w"], me * conv_cols, conv_cols, axis=2)
    gs["ffn_conv_w"] = lax.dynamic_slice_in_dim(gs["ffn_conv_w"], me * ffn_cols, ffn_cols, axis=2)
    upd = _adamw_small(_pack([gs[k] for k in SMALL]), _pack([w[k] for k in SMALL]), _pack([m[k] for k in SMALL]),
                       _pack([v[k] for k in SMALL]))
    shapes = [w[k].shape for k in SMALL]
    d_s, m_s, v_s = (dict(zip(SMALL, _unpack(u, shapes))) for u in upd)
    res = {}
    for k in SMALL:
        res[k] = (gs[k], d_s[k], m_s[k], v_s[k])

    after = upd[0]
    for name in ("w_down", "w_up", "w_out", "w_attn_o", "w_ssd_o", "w_in"):
        parts = _exchange_wait(*pending[name], after, False, "grad_wait_" + name)
        view = (lambda a: a[0].T) if name in ("w_in", "w_up") else (lambda a: a[0])
        res[name] = _adamw_sharded(parts, view(w[name]), view(m[name]), view(v[name]), "adamw_" + name)
        after = res[name][0]
        if name in ("w_in", "w_up"):
            res[name] = [r.T for r in res[name]]

    grad_x = dx.T[None]
    outs = [loss_sum, grad_x]
    for i in range(4):
        for k in WEIGHT_ORDER:
            r = res[k][i]
            outs.append(r[None] if k in SHARDED else r)
    return tuple(outs)
```

```python
import functools

import jax
import jax.numpy as jnp
from jax import lax
from jax.experimental import pallas as pl
from jax.experimental.pallas import tpu as pltpu

F32 = jnp.float32
BF16 = jnp.bfloat16
HIGHEST = lax.Precision.HIGHEST

D_MODEL = 1024
N_Q_HEADS = 16
N_KV_HEADS = 4
HEAD_DIM = 64
WINDOW = 128
Q_PER_KV = N_Q_HEADS // N_KV_HEADS
Q_DIM = N_Q_HEADS * HEAD_DIM
KV_DIM = N_KV_HEADS * HEAD_DIM
D_INNER = 2048
SSD_HEAD_DIM = 64
N_SSD_HEADS = 32
N_SSD_GROUPS = 4
HEADS_PER_GROUP = N_SSD_HEADS // N_SSD_GROUPS
D_STATE = 128
BC_DIM = N_SSD_GROUPS * D_STATE
XBC_DIM = D_INNER + 2 * BC_DIM
SSD_CONV = 4
CHUNK = 128
D_FF = 2816
FFN_CONV = 3
EPS = 1e-5
NEG = -1e30
IN_DIM = 8736
N_DEV = 8

OFF_Q = 0
OFF_K = OFF_Q + Q_DIM
OFF_V = OFF_K + KV_DIM
OFF_Z = OFF_V + KV_DIM
OFF_X = OFF_Z + D_INNER
OFF_DT = OFF_X + XBC_DIM
OFF_GA = OFF_DT + N_SSD_HEADS
OFF_GS = OFF_GA + D_MODEL

ADAM_LR = 0.001
ADAM_B1 = 0.9
ADAM_B2 = 0.999
ADAM_EPS = 1e-08
ADAM_WD = 0.01
ADAM_STEP = 10

VMEM_LIMIT = 48 * 1024 * 1024
MESH = pl.DeviceIdType.MESH


def _cparams(*sem):
    return pltpu.CompilerParams(dimension_semantics=sem, vmem_limit_bytes=VMEM_LIMIT)


def _tile(n, prefs):
    for p in prefs:
        if n % p == 0:
            return p
    return n


def _sigmoid(x):
    return 1.0 / (1.0 + jnp.exp(-x))


def _softplus(x):
    return jnp.maximum(x, 0.0) + jnp.log(1.0 + jnp.exp(-jnp.abs(x)))


def _rowsum(x):
    return jnp.sum(x, axis=1, keepdims=True)


def _colsum(x):
    return jnp.sum(x, axis=0, keepdims=True)


def _dot(a, b):
    return jnp.dot(a, b, preferred_element_type=F32)


def _dot_nt(a, b):
    return lax.dot_general(a, b, (((1,), (1,)), ((), ())), preferred_element_type=F32)


def _dot_tn(a, b):
    return lax.dot_general(a, b, (((0,), (0,)), ((), ())), preferred_element_type=F32)


def _shift_right(x, j):
    if j == 0:
        return x
    r = pltpu.roll(x, j, 1)
    lane = lax.broadcasted_iota(jnp.int32, (x.shape[0], 128), 1)
    return jnp.concatenate([jnp.where(lane >= j, r[:, :128], 0.0), r[:, 128:]], axis=1)


def _shift_left(x, j):
    if j == 0:
        return x
    n = x.shape[1]
    r = pltpu.roll(x, n - j, 1)
    lane = lax.broadcasted_iota(jnp.int32, (x.shape[0], 128), 1)
    return jnp.concatenate([r[:, :n - 128], jnp.where(lane < 128 - j, r[:, n - 128:], 0.0)], axis=1)


def _causal_conv(xv, wv, bv):
    taps = wv.shape[1]
    shifted = [_shift_right(xv, taps - 1 - k) for k in range(taps - 1)]
    y = bv + wv[:, taps - 1:taps] * xv
    for k in range(taps - 1):
        y = y + wv[:, k:k + 1] * shifted[k]
    return y, shifted


def _causal_conv_bwd(dy, xv, shifted, wv):
    taps = wv.shape[1]
    lane = lax.broadcasted_iota(jnp.int32, (dy.shape[0], 128), 1)
    dwb = jnp.where(lane == taps, _rowsum(dy), 0.0)
    dwb = jnp.where(lane == taps - 1, _rowsum(dy * xv), dwb)
    dx = wv[:, taps - 1:taps] * dy
    for k in range(taps - 1):
        dx = dx + wv[:, k:k + 1] * _shift_left(dy, taps - 1 - k)
        dwb = jnp.where(lane == k, _rowsum(dy * shifted[k]), dwb)
    return dx, dwb


MATMUL_VMEM_BUDGET = 36 * 1024 * 1024
MATMUL_MAX_TK = 3072


MATMUL_MAX_TM = 768


def _largest_tile(n, align, cap):
    return max(d for d in range(align, min(n, cap) + 1, align) if n % d == 0)


def _matmul_tiles(m, n, k, a_bytes, b_bytes, out_bytes, has_add, m_align, k_align):
    tm = _largest_tile(m, m_align, MATMUL_MAX_TM)
    tk = _largest_tile(k, k_align, MATMUL_MAX_TK)
    for tn in sorted({d for d in range(128, n + 1, 128) if n % d == 0}, reverse=True):
        need = 2 * (tm * tk * a_bytes + tk * tn * b_bytes) + tm * tn * (2 * out_bytes + (4 if k > tk else 0) + (8 if has_add else 0))
        if tn <= 3072 and need <= MATMUL_VMEM_BUDGET:
            return tm, tn, tk
    return tm, 128, tk


def _matmul(a, b, *, nt, out_dtype, name, add=None, tn_a=False, send=None):
    if tn_a:
        k, m = a.shape
    else:
        m, k = a.shape
    n = b.shape[0] if nt else b.shape[1]
    tm, tn, tk = _matmul_tiles(m, n, k, a.dtype.itemsize, b.dtype.itemsize, jnp.dtype(out_dtype).itemsize, add is not None,
                               128 if tn_a else 16, 16 if tn_a and not nt else 128)
    nk = k // tk
    grid = (m // tm, n // tn, nk)

    def body(a_ref, b_ref, *rest):
        r_ref = None
        if add is not None:
            r_ref, rest = rest[0], rest[1:]
        if send is not None:
            src_ref, land_ref, rest = rest[0], rest[1], rest[2:]
            send_sems, recv_sems, local_sem = rest[1], rest[2], rest[-1]
            rest = (rest[0],) + rest[5:-1]
            x, y, c = _place()
            me = 4 * x + 2 * y + c
            local = pltpu.make_async_copy(src_ref.at[me], land_ref.at[me], local_sem)
            step = (pl.program_id(0) * grid[1] + pl.program_id(1)) * grid[2] + pl.program_id(2)

            @pl.when(step == 0)
            def _():
                local.start()
                for peer in range(N_DEV - 1):
                    _peer_copy(False, src_ref, land_ref, send_sems, recv_sems, peer, True).start()

            @pl.when(step == grid[0] * grid[1] * grid[2] - 1)
            def _():
                local.wait()

        o_ref = rest[0]
        av = a_ref[...].astype(BF16)
        bv = b_ref[...].astype(BF16)
        part = _dot_tn(av, bv) if tn_a else _dot_nt(av, bv) if nt else _dot(av, bv)

        def finish(r):
            if add is not None:
                r = r + r_ref[...]
            o_ref[...] = r.astype(out_dtype)

        if nk == 1:
            finish(part)
            return
        acc = rest[1]
        kk = pl.program_id(2)

        @pl.when(kk == 0)
        def _():
            acc[...] = part

        @pl.when((kk > 0) & (kk < nk - 1))
        def _():
            acc[...] += part

        @pl.when(kk == nk - 1)
        def _():
            finish(acc[...] + part)

    in_specs = [
        pl.BlockSpec((tk, tm), lambda i, j, kk: (kk, i)) if tn_a else pl.BlockSpec((tm, tk), lambda i, j, kk: (i, kk)),
        pl.BlockSpec((tn, tk), lambda i, j, kk: (j, kk)) if nt else pl.BlockSpec((tk, tn), lambda i, j, kk: (kk, j)),
    ]
    args = [a, b]
    if add is not None:
        in_specs.append(pl.BlockSpec((tm, tn), lambda i, j, kk: (i, j)))
        args.append(add)
    out_specs = [pl.BlockSpec((tm, tn), lambda i, j, kk: (i, j))]
    out_shape = [jax.ShapeDtypeStruct((m, n), out_dtype)]
    scratch = [pltpu.VMEM((tm, tn), F32)] if nk > 1 else []
    if send is None:
        return pl.pallas_call(
            body, name=name, grid=grid, in_specs=in_specs, out_specs=out_specs[0], out_shape=out_shape[0],
            scratch_shapes=scratch, compiler_params=_cparams("parallel", "parallel", "arbitrary"),
        )(*args)
    sem = pltpu.SemaphoreType.DMA((N_DEV - 1,))
    hbm = pltpu.HBM(send.shape, send.dtype)
    first = len(args)
    res = pl.pallas_call(
        body, name=name, grid=grid,
        in_specs=in_specs + [HBM, HBM],
        out_specs=out_specs + [SEM, SEM, HBM, HBM],
        out_shape=out_shape + [sem, sem, hbm, hbm],
        input_output_aliases={first: 3, first + 1: 4},
        scratch_shapes=scratch + [pltpu.SemaphoreType.DMA(())],
        compiler_params=pltpu.CompilerParams(dimension_semantics=("arbitrary",) * 3, vmem_limit_bytes=VMEM_LIMIT,
                                             has_side_effects=EFFECT),
    )(*args, pltpu.with_memory_space_constraint(send, pltpu.HBM),
      pltpu.with_memory_space_constraint(lax.empty(send.shape, send.dtype), pltpu.HBM))
    return res[0], tuple(res[1:])


def _norm_fwd(x, w_col, name):
    f, t = x.shape
    tt = _tile(t, (512, 256, 128))

    def body(x_ref, w_ref, o_ref):
        xv = x_ref[...]
        r = lax.rsqrt(jnp.mean(xv * xv, axis=0, keepdims=True) + EPS)
        o_ref[...] = (xv * r * w_ref[...]).astype(BF16)

    return pl.pallas_call(
        body,
        name=name,
        grid=(t // tt,),
        in_specs=[pl.BlockSpec((f, tt), lambda i: (0, i)), pl.BlockSpec((f, 1), lambda i: (0, 0))],
        out_specs=pl.BlockSpec((f, tt), lambda i: (0, i)),
        out_shape=jax.ShapeDtypeStruct((f, t), BF16),
        compiler_params=_cparams("parallel"),
    )(x, w_col)


def _norm_bwd(dy, x, w_col, res, name):
    f, t = x.shape
    tt = _tile(t, (512, 256, 128))

    def body(dy_ref, x_ref, w_ref, res_ref, dx_ref, dw_ref):
        @pl.when(pl.program_id(0) == 0)
        def _():
            dw_ref[...] = jnp.zeros_like(dw_ref)

        xv = x_ref[...]
        r = lax.rsqrt(jnp.mean(xv * xv, axis=0, keepdims=True) + EPS)
        xhat = xv * r
        dyv = dy_ref[...]
        dw_ref[...] += _rowsum(dyv * xhat)
        dxhat = dyv * w_ref[...]
        dx_ref[...] = res_ref[...] + r * (dxhat - xhat * jnp.mean(dxhat * xhat, axis=0, keepdims=True))

    blk = pl.BlockSpec((f, tt), lambda i: (0, i))
    col = pl.BlockSpec((f, 1), lambda i: (0, 0))
    return pl.pallas_call(
        body,
        name=name,
        grid=(t // tt,),
        in_specs=[blk, blk, col, blk],
        out_specs=[blk, col],
        out_shape=[jax.ShapeDtypeStruct((f, t), F32), jax.ShapeDtypeStruct((f, 1), F32)],
        compiler_params=_cparams("arbitrary"),
    )(dy, x, w_col, res)


def _final_norm_loss(h, tgt, w_col):
    f, t = h.shape
    tt = _tile(t, (512, 256, 128))

    def body(h_ref, t_ref, w_ref, dh_ref, loss_ref, dw_ref):
        @pl.when(pl.program_id(0) == 0)
        def _():
            dw_ref[...] = jnp.zeros_like(dw_ref)
            loss_ref[...] = jnp.zeros_like(loss_ref)

        xv = h_ref[...]
        r = lax.rsqrt(jnp.mean(xv * xv, axis=0, keepdims=True) + EPS)
        xhat = xv * r
        wv = w_ref[...]
        err = xhat * wv - t_ref[...]
        loss_ref[...] += 0.5 * _rowsum(jnp.mean(err * err, axis=0, keepdims=True))
        dyv = err * (1.0 / f)
        dw_ref[...] += _rowsum(dyv * xhat)
        dxhat = dyv * wv
        dh_ref[...] = r * (dxhat - xhat * jnp.mean(dxhat * xhat, axis=0, keepdims=True))

    blk = pl.BlockSpec((f, tt), lambda i: (0, i))
    col = pl.BlockSpec((f, 1), lambda i: (0, 0))
    one = pl.BlockSpec((1, 1), lambda i: (0, 0))
    return pl.pallas_call(
        body,
        name="final_norm_loss",
        grid=(t // tt,),
        in_specs=[blk, blk, col],
        out_specs=[blk, one, col],
        out_shape=[jax.ShapeDtypeStruct((f, t), F32), jax.ShapeDtypeStruct((1, 1), F32), jax.ShapeDtypeStruct((f, 1), F32)],
        compiler_params=_cparams("arbitrary"),
    )(h, tgt, w_col)


def _attn_mask(n):
    shape = (2 * WINDOW, Q_PER_KV * WINDOW)
    si = lax.broadcasted_iota(jnp.int32, shape, 0)
    qi = lax.broadcasted_iota(jnp.int32, shape, 1) & (WINDOW - 1)
    dist = WINDOW + qi - si
    return (dist >= 0) & (dist < WINDOW) & ((si >= WINDOW) | (n > 0))


def _lane_cat(ref, row0, rows):
    return jnp.concatenate([ref[row0 + i * rows:row0 + (i + 1) * rows, :] for i in range(Q_PER_KV)], axis=1)


def _attn_fwd(proj, sinks):
    t = proj.shape[1]
    nb = t // WINDOW
    scale = HEAD_DIM ** -0.5

    def body(s_ref, q_ref, kc_ref, kp_ref, vc_ref, vp_ref, o_ref, lse_ref):
        n = pl.program_id(0)
        valid = _attn_mask(n)
        for g in range(N_KV_HEADS):
            rows = slice(g * HEAD_DIM, (g + 1) * HEAD_DIM)
            kt = jnp.concatenate([kp_ref[rows, :], kc_ref[rows, :]], axis=1).astype(BF16)
            vt = jnp.concatenate([vp_ref[rows, :], vc_ref[rows, :]], axis=1).astype(BF16)
            qcat = (_lane_cat(q_ref, g * Q_PER_KV * HEAD_DIM, HEAD_DIM) * scale).astype(BF16)
            s = jnp.where(valid, _dot_tn(kt, qcat), NEG)
            sink = jnp.concatenate(
                [jnp.full((1, WINDOW), s_ref[g * Q_PER_KV + i], F32) for i in range(Q_PER_KV)], axis=1)
            m = jnp.maximum(jnp.max(s, axis=0, keepdims=True), sink)
            p = jnp.exp(s - m)
            denom = _colsum(p) + jnp.exp(sink - m)
            probs = (p / denom).astype(BF16)
            out = _dot(vt, probs)
            lse = m + jnp.log(denom)
            for i in range(Q_PER_KV):
                h = g * Q_PER_KV + i
                o_ref[h * HEAD_DIM:(h + 1) * HEAD_DIM, :] = out[:, i * WINDOW:(i + 1) * WINDOW]
                lse_ref[h:h + 1, :] = lse[:, i * WINDOW:(i + 1) * WINDOW]

    kb = OFF_K // KV_DIM
    vb = OFF_V // KV_DIM
    prev = lambda n: jnp.maximum(n - 1, 0)
    return pl.pallas_call(
        body,
        name="attn_fwd",
        grid=(nb,),
        in_specs=[
            pl.BlockSpec(memory_space=pltpu.SMEM),
            pl.BlockSpec((Q_DIM, WINDOW), lambda n: (0, n)),
            pl.BlockSpec((KV_DIM, WINDOW), lambda n: (kb, n)),
            pl.BlockSpec((KV_DIM, WINDOW), lambda n: (kb, prev(n))),
            pl.BlockSpec((KV_DIM, WINDOW), lambda n: (vb, n)),
            pl.BlockSpec((KV_DIM, WINDOW), lambda n: (vb, prev(n))),
        ],
        out_specs=[pl.BlockSpec((Q_DIM, WINDOW), lambda n: (0, n)), pl.BlockSpec((N_Q_HEADS, WINDOW), lambda n: (0, n))],
        out_shape=[jax.ShapeDtypeStruct((Q_DIM, t), F32), jax.ShapeDtypeStruct((N_Q_HEADS, t), F32)],
        compiler_params=_cparams("parallel"),
    )(sinks, proj, proj, proj, proj, proj)


def _attn_bwd(proj, sinks, out, lse, dout):
    t = proj.shape[1]
    nb = t // WINDOW
    scale = HEAD_DIM ** -0.5

    def body(s_ref, q_ref, kc_ref, kp_ref, vc_ref, vp_ref, o_ref, lse_ref, do_ref,
             dq_ref, dk_ref, dv_ref, ds_ref, dk_carry, dv_carry):
        step = pl.program_id(0)
        n = nb - 1 - step

        @pl.when(step == 0)
        def _():
            dk_carry[...] = jnp.zeros_like(dk_carry)
            dv_carry[...] = jnp.zeros_like(dv_carry)
            ds_ref[...] = jnp.zeros_like(ds_ref)

        valid = _attn_mask(n)
        for g in range(N_KV_HEADS):
            rows = slice(g * HEAD_DIM, (g + 1) * HEAD_DIM)
            q0 = g * Q_PER_KV * HEAD_DIM
            kt = jnp.concatenate([kp_ref[rows, :], kc_ref[rows, :]], axis=1).astype(BF16)
            vt = jnp.concatenate([vp_ref[rows, :], vc_ref[rows, :]], axis=1).astype(BF16)
            qf = _lane_cat(q_ref, q0, HEAD_DIM)
            qcat = qf.astype(BF16)
            ocat = _lane_cat(o_ref, q0, HEAD_DIM)
            docat = _lane_cat(do_ref, q0, HEAD_DIM)
            dob = docat.astype(BF16)
            lse_cat = jnp.concatenate(
                [lse_ref[g * Q_PER_KV + i:g * Q_PER_KV + i + 1, :] for i in range(Q_PER_KV)], axis=1)
            sink = jnp.concatenate(
                [jnp.full((1, WINDOW), s_ref[g * Q_PER_KV + i], F32) for i in range(Q_PER_KV)], axis=1)
            s = jnp.where(valid, _dot_tn(kt, (qf * scale).astype(BF16)), NEG)
            p = jnp.exp(s - lse_cat)
            dp = _dot_tn(vt, dob)
            delta = _colsum(docat * ocat)
            dsc = (p * (dp - delta)).astype(BF16)
            dsink_row = -jnp.exp(sink - lse_cat) * delta
            dq = _dot(kt, dsc) * scale
            dk = _dot_nt(qcat, dsc) * scale
            dv = _dot_nt(dob, p.astype(BF16))
            for i in range(Q_PER_KV):
                h = g * Q_PER_KV + i
                dq_ref[h * HEAD_DIM:(h + 1) * HEAD_DIM, :] = dq[:, i * WINDOW:(i + 1) * WINDOW].astype(BF16)
                ds_ref[h:h + 1, :] += _rowsum(dsink_row[:, i * WINDOW:(i + 1) * WINDOW])
            dk_ref[rows, :] = (dk[:, WINDOW:] + dk_carry[rows, :]).astype(BF16)
            dv_ref[rows, :] = (dv[:, WINDOW:] + dv_carry[rows, :]).astype(BF16)
            dk_carry[rows, :] = dk[:, :WINDOW]
            dv_carry[rows, :] = dv[:, :WINDOW]

    kb = OFF_K // KV_DIM
    vb = OFF_V // KV_DIM
    cur = lambda i: nb - 1 - i
    prev = lambda i: jnp.maximum(nb - 2 - i, 0)
    qspec = pl.BlockSpec((Q_DIM, WINDOW), lambda i: (0, cur(i)))
    kvspec = pl.BlockSpec((KV_DIM, WINDOW), lambda i: (0, cur(i)))
    return pl.pallas_call(
        body,
        name="attn_bwd",
        grid=(nb,),
        in_specs=[
            pl.BlockSpec(memory_space=pltpu.SMEM),
            qspec,
            pl.BlockSpec((KV_DIM, WINDOW), lambda i: (kb, cur(i))),
            pl.BlockSpec((KV_DIM, WINDOW), lambda i: (kb, prev(i))),
            pl.BlockSpec((KV_DIM, WINDOW), lambda i: (vb, cur(i))),
            pl.BlockSpec((KV_DIM, WINDOW), lambda i: (vb, prev(i))),
            qspec,
            pl.BlockSpec((N_Q_HEADS, WINDOW), lambda i: (0, cur(i))),
            qspec,
        ],
        out_specs=[qspec, kvspec, kvspec, pl.BlockSpec((N_Q_HEADS, 1), lambda i: (0, 0))],
        out_shape=[
            jax.ShapeDtypeStruct((Q_DIM, t), BF16),
            jax.ShapeDtypeStruct((KV_DIM, t), BF16),
            jax.ShapeDtypeStruct((KV_DIM, t), BF16),
            jax.ShapeDtypeStruct((N_Q_HEADS, 1), F32),
        ],
        scratch_shapes=[pltpu.VMEM((KV_DIM, WINDOW), F32), pltpu.VMEM((KV_DIM, WINDOW), F32)],
        compiler_params=_cparams("arbitrary"),
    )(sinks, proj, proj, proj, proj, proj, out, lse, dout)


CONV_ROWS = 256


def _conv_silu_fwd(proj, w_col, b_col):
    t = proj.shape[1]
    r0 = OFF_X // CONV_ROWS

    def body(x_ref, w_ref, b_ref, o_ref):
        def strip(rows):
            y, _ = _causal_conv(x_ref[rows, :], w_ref[rows, :], b_ref[rows, :])
            o_ref[rows, :] = y * _sigmoid(y)

        strip(slice(None))

    return pl.pallas_call(
        body,
        name="ssd_conv_fwd",
        grid=(XBC_DIM // CONV_ROWS,),
        in_specs=[
            pl.BlockSpec((CONV_ROWS, t), lambda i: (r0 + i, 0)),
            pl.BlockSpec((CONV_ROWS, SSD_CONV), lambda i: (i, 0)),
            pl.BlockSpec((CONV_ROWS, 1), lambda i: (i, 0)),
        ],
        out_specs=pl.BlockSpec((CONV_ROWS, t), lambda i: (i, 0)),
        out_shape=jax.ShapeDtypeStruct((XBC_DIM, t), F32),
        compiler_params=_cparams("parallel"),
    )(proj, w_col, b_col)


def _conv_silu_bwd(proj, w_col, b_col, dout, row0, name):
    t = proj.shape[1]
    nrows = dout.shape[0]
    p0 = (OFF_X + row0) // CONV_ROWS
    c0 = row0 // CONV_ROWS

    def body(x_ref, w_ref, b_ref, do_ref, dx_ref, dwb_ref):
        def strip(rows):
            xv = x_ref[rows, :]
            wv = w_ref[rows, :]
            y, shifted = _causal_conv(xv, wv, b_ref[rows, :])
            sg = _sigmoid(y)
            dy = do_ref[rows, :] * (sg * (1.0 + y * (1.0 - sg)))
            dx, dwb_ref[rows, :] = _causal_conv_bwd(dy, xv, shifted, wv)
            dx_ref[rows, :] = dx.astype(BF16)

        strip(slice(None))

    return pl.pallas_call(
        body,
        name=name,
        grid=(nrows // CONV_ROWS,),
        in_specs=[
            pl.BlockSpec((CONV_ROWS, t), lambda i: (p0 + i, 0)),
            pl.BlockSpec((CONV_ROWS, SSD_CONV), lambda i: (c0 + i, 0)),
            pl.BlockSpec((CONV_ROWS, 1), lambda i: (c0 + i, 0)),
            pl.BlockSpec((CONV_ROWS, t), lambda i: (i, 0)),
        ],
        out_specs=[pl.BlockSpec((CONV_ROWS, t), lambda i: (i, 0)), pl.BlockSpec((CONV_ROWS, 128), lambda i: (i, 0))],
        out_shape=[jax.ShapeDtypeStruct((nrows, t), BF16), jax.ShapeDtypeStruct((nrows, 128), F32)],
        compiler_params=_cparams("parallel"),
    )(proj, w_col, b_col, dout)


GROUP_ROWS = HEADS_PER_GROUP * SSD_HEAD_DIM


def _ssd_specs(order):
    xb = D_INNER // BC_DIM
    dtb = OFF_DT // N_SSD_HEADS
    col = pl.BlockSpec((N_SSD_HEADS, 1), lambda c: (0, 0))
    return [
        pl.BlockSpec((D_INNER, CHUNK), lambda c: (0, order(c))),
        pl.BlockSpec((BC_DIM, CHUNK), lambda c: (xb, order(c))),
        pl.BlockSpec((BC_DIM, CHUNK), lambda c: (xb + 1, order(c))),
        pl.BlockSpec((N_SSD_HEADS, CHUNK), lambda c: (dtb, order(c))),
        col, col, col,
    ]


def _ssd_common(dt_ref, dtb_ref, alog_ref):
    z = dt_ref[...] + dtb_ref[...]
    dt = _softplus(z)
    a_neg = -jnp.exp(alog_ref[...])
    d_a = dt * a_neg
    row = lax.broadcasted_iota(jnp.int32, (CHUNK, CHUNK), 0)
    colm = lax.broadcasted_iota(jnp.int32, (CHUNK, CHUNK), 1)
    upper = (row <= colm).astype(F32)
    a_cs = jnp.dot(d_a, upper, precision=HIGHEST, preferred_element_type=F32)
    a_last = _rowsum(d_a)
    return z, dt, a_neg, a_cs, a_last, row >= colm, row == colm


def _decay(a_row, causal):
    a_s = jnp.broadcast_to(a_row, (CHUNK, CHUNK))
    seg = a_s.T - a_s
    return jnp.where(causal, jnp.exp(jnp.where(causal, seg, 0.0)), 0.0)


def _ssd_fwd(xbc, proj, dtb_col, alog_col, dsk_col):
    t = xbc.shape[1]
    nc = t // CHUNK

    def body(xs_ref, b_ref, c_ref, dt_ref, dtb_ref, alog_ref, dsk_ref, y_ref, hst_ref, h_scr):
        @pl.when(pl.program_id(0) == 0)
        def _():
            h_scr[...] = jnp.zeros_like(h_scr)

        _, dt, _, a_cs, a_last, causal, _ = _ssd_common(dt_ref, dtb_ref, alog_ref)
        hst_ref[0] = h_scr[...]
        dsk = dsk_ref[...]
        for g in range(N_SSD_GROUPS):
            grows = slice(g * D_STATE, (g + 1) * D_STATE)
            bb = b_ref[grows, :].astype(BF16)
            cb_ = c_ref[grows, :].astype(BF16)
            cb = _dot_tn(cb_, bb)
            for j in range(g * HEADS_PER_GROUP, (g + 1) * HEADS_PER_GROUP):
                rows = slice(j * SSD_HEAD_DIM, (j + 1) * SSD_HEAD_DIM)
                a = a_cs[j:j + 1, :]
                m = (cb * _decay(a, causal)).astype(BF16)
                xs = xs_ref[rows, :]
                xc = xs * dt[j:j + 1, :]
                hj = h_scr[rows, :]
                y = _dot_nt(xc.astype(BF16), m) + _dot(hj.astype(BF16), cb_) * jnp.exp(a) + dsk[j:j + 1, :] * xs
                y_ref[rows, :] = y
                al = a_last[j:j + 1, :]
                w = jnp.exp(al - a)
                h_scr[rows, :] = jnp.exp(al) * hj + _dot_nt((xc * w).astype(BF16), bb)

    return pl.pallas_call(
        body,
        name="ssd_fwd",
        grid=(nc,),
        in_specs=_ssd_specs(lambda c: c),
        out_specs=[
            pl.BlockSpec((D_INNER, CHUNK), lambda c: (0, c)),
            pl.BlockSpec((1, D_INNER, D_STATE), lambda c: (c, 0, 0)),
        ],
        out_shape=[
            jax.ShapeDtypeStruct((D_INNER, t), F32),
            jax.ShapeDtypeStruct((nc, D_INNER, D_STATE), F32),
        ],
        scratch_shapes=[pltpu.VMEM((D_INNER, D_STATE), F32)],
        compiler_params=_cparams("arbitrary"),
    )(xbc, xbc, xbc, proj, dtb_col, alog_col, dsk_col)


def _ssd_bwd(xbc, proj, dtb_col, alog_col, dsk_col, hst, dy):
    t = xbc.shape[1]
    nc = t // CHUNK
    rev = lambda c: nc - 1 - c

    def body(xs_ref, b_ref, c_ref, dt_ref, dtb_ref, alog_ref, dsk_ref, hst_ref, dy_ref,
             dxs_ref, db_ref, dc_ref, ddt_ref, dalog_ref, ddsk_ref, ddtb_ref, dh_scr, da_scr, ddt_scr, dd_scr):
        @pl.when(pl.program_id(0) == 0)
        def _():
            dh_scr[...] = jnp.zeros_like(dh_scr)
            dalog_ref[...] = jnp.zeros_like(dalog_ref)
            ddsk_ref[...] = jnp.zeros_like(ddsk_ref)
            ddtb_ref[...] = jnp.zeros_like(ddtb_ref)

        z, dt, a_neg, a_cs, a_last, causal, eye = _ssd_common(dt_ref, dtb_ref, alog_ref)
        dsk = dsk_ref[...]
        last_lane = lax.broadcasted_iota(jnp.int32, (1, CHUNK), 1) == CHUNK - 1
        for g in range(N_SSD_GROUPS):
            grows = slice(g * D_STATE, (g + 1) * D_STATE)
            bb = b_ref[grows, :].astype(BF16)
            cb_ = c_ref[grows, :].astype(BF16)
            cb = _dot_tn(cb_, bb)
            dcb = jnp.zeros((CHUNK, CHUNK), F32)
            dc_acc = jnp.zeros((D_STATE, CHUNK), F32)
            db_acc = jnp.zeros((D_STATE, CHUNK), F32)
            for j in range(g * HEADS_PER_GROUP, (g + 1) * HEADS_PER_GROUP):
                rows = slice(j * SSD_HEAD_DIM, (j + 1) * SSD_HEAD_DIM)
                a = a_cs[j:j + 1, :]
                al = a_last[j:j + 1, :]
                lam = _decay(a, causal)
                mf = cb * lam
                xs = xs_ref[rows, :]
                dtj = dt[j:j + 1, :]
                xc = xs * dtj
                w = jnp.exp(al - a)
                e = jnp.exp(a)
                gam = jnp.exp(al)
                hj = hst_ref[0, rows, :]
                hjb = hj.astype(BF16)
                dyv = dy_ref[rows, :]
                dyb = dyv.astype(BF16)
                dd_scr[j:j + 1, :] = _colsum(dyv * xs)
                gb = (dyv * e).astype(BF16)
                dh_in = _dot_nt(gb, cb_)
                dc_acc = dc_acc + _dot_tn(hjb, gb)
                yoff = _dot(hjb, cb_) * e
                da = _colsum(dyv * yoff)
                dm = _dot_tn(dyb, xc.astype(BF16))
                dxc = _dot(dyb, mf.astype(BF16))
                dcb = dcb + dm * lam
                nmat = dm * mf
                rs = jnp.broadcast_to(_rowsum(nmat), (CHUNK, CHUNK))
                da = da + _colsum(jnp.where(eye, rs, 0.0)) - _colsum(nmat)
                ds = dh_scr[rows, :]
                dsb = ds.astype(BF16)
                t1 = _dot(dsb, bb)
                xcw = xc * w
                dxc = dxc + w * t1
                dww = _colsum(xcw * t1)
                da_l = _rowsum(dww) + _rowsum(_colsum(ds * hj)) * gam
                da = da - dww + jnp.where(last_lane, da_l, 0.0)
                db_acc = db_acc + _dot_tn(dsb, xcw.astype(BF16))
                dh_scr[rows, :] = gam * ds + dh_in
                dxs_ref[rows, :] = dsk[j:j + 1, :] * dyv + dxc * dtj
                da_scr[j:j + 1, :] = da
                ddt_scr[j:j + 1, :] = _colsum(dxc * xs)
            dcbb = dcb.astype(BF16)
            dc_ref[grows, :] = dc_acc + _dot_nt(bb, dcbb)
            db_ref[grows, :] = db_acc + _dot(cb_, dcbb)
        dda = jnp.dot(da_scr[...], causal.astype(F32), precision=HIGHEST, preferred_element_type=F32)
        ddt = ddt_scr[...] + dda * a_neg
        ddt_raw = ddt * _sigmoid(z)
        ddt_ref[...] = ddt_raw
        ddtb_ref[...] += _rowsum(ddt_raw)
        dalog_ref[...] += _rowsum(dda * dt) * a_neg
        ddsk_ref[...] += _rowsum(dd_scr[...])

    col = pl.BlockSpec((N_SSD_HEADS, 1), lambda c: (0, 0))
    bc = pl.BlockSpec((BC_DIM, CHUNK), lambda c: (0, rev(c)))
    xs_spec = pl.BlockSpec((D_INNER, CHUNK), lambda c: (0, rev(c)))
    small = pltpu.VMEM((N_SSD_HEADS, CHUNK), F32)
    return pl.pallas_call(
        body,
        name="ssd_bwd",
        grid=(nc,),
        in_specs=_ssd_specs(rev) + [pl.BlockSpec((1, D_INNER, D_STATE), lambda c: (rev(c), 0, 0)), xs_spec],
        out_specs=[xs_spec, bc, bc, pl.BlockSpec((N_SSD_HEADS, CHUNK), lambda c: (0, rev(c))), col, col, col],
        out_shape=[
            jax.ShapeDtypeStruct((D_INNER, t), F32),
            jax.ShapeDtypeStruct((BC_DIM, t), F32),
            jax.ShapeDtypeStruct((BC_DIM, t), F32),
            jax.ShapeDtypeStruct((N_SSD_HEADS, t), F32),
            jax.ShapeDtypeStruct((N_SSD_HEADS, 1), F32),
            jax.ShapeDtypeStruct((N_SSD_HEADS, 1), F32),
            jax.ShapeDtypeStruct((N_SSD_HEADS, 1), F32),
        ],
        scratch_shapes=[pltpu.VMEM((D_INNER, D_STATE), F32), small, small, small],
        compiler_params=_cparams("arbitrary"),
    )(xbc, xbc, xbc, proj, dtb_col, alog_col, dsk_col, hst, dy)


GN_ROWS = D_INNER // N_SSD_GROUPS


def _gnorm_fwd(y, proj, w_col):
    t = y.shape[1]
    tt = _tile(t, (512, 256, 128))
    z0 = OFF_Z // GN_ROWS

    def body(y_ref, z_ref, w_ref, o_ref):
        zv = z_ref[...]
        u = y_ref[...] * (zv * _sigmoid(zv))
        r = lax.rsqrt(jnp.mean(u * u, axis=0, keepdims=True) + EPS)
        o_ref[...] = (u * r * w_ref[...]).astype(BF16)

    blk = pl.BlockSpec((GN_ROWS, tt), lambda g, i: (g, i))
    return pl.pallas_call(
        body,
        name="gnorm_fwd",
        grid=(N_SSD_GROUPS, t // tt),
        in_specs=[blk, pl.BlockSpec((GN_ROWS, tt), lambda g, i: (z0 + g, i)), pl.BlockSpec((GN_ROWS, 1), lambda g, i: (g, 0))],
        out_specs=blk,
        out_shape=jax.ShapeDtypeStruct((D_INNER, t), BF16),
        compiler_params=_cparams("parallel", "parallel"),
    )(y, proj, w_col)


def _gnorm_bwd(dout, y, proj, w_col):
    t = y.shape[1]
    tt = _tile(t, (512, 256, 128))
    z0 = OFF_Z // GN_ROWS

    def body(do_ref, y_ref, z_ref, w_ref, dy_ref, dz_ref, dw_ref):
        @pl.when(pl.program_id(1) == 0)
        def _():
            dw_ref[...] = jnp.zeros_like(dw_ref)

        zv = z_ref[...]
        yv = y_ref[...]
        sg = _sigmoid(zv)
        sz = zv * sg
        u = yv * sz
        r = lax.rsqrt(jnp.mean(u * u, axis=0, keepdims=True) + EPS)
        xhat = u * r
        dov = do_ref[...]
        dw_ref[...] += _rowsum(dov * xhat)
        dxhat = dov * w_ref[...]
        du = r * (dxhat - xhat * jnp.mean(dxhat * xhat, axis=0, keepdims=True))
        dy_ref[...] = du * sz
        dz_ref[...] = (du * yv * (sg * (1.0 + zv * (1.0 - sg)))).astype(BF16)

    blk = pl.BlockSpec((GN_ROWS, tt), lambda g, i: (g, i))
    col = pl.BlockSpec((GN_ROWS, 1), lambda g, i: (g, 0))
    return pl.pallas_call(
        body,
        name="gnorm_bwd",
        grid=(N_SSD_GROUPS, t // tt),
        in_specs=[blk, blk, pl.BlockSpec((GN_ROWS, tt), lambda g, i: (z0 + g, i)), col],
        out_specs=[blk, blk, col],
        out_shape=[jax.ShapeDtypeStruct((D_INNER, t), F32), jax.ShapeDtypeStruct((D_INNER, t), BF16),
                   jax.ShapeDtypeStruct((D_INNER, 1), F32)],
        compiler_params=_cparams("parallel", "arbitrary"),
    )(dout, y, proj, w_col)


GATE_ROWS = 32


def _gate_specs(t):
    ga0 = OFF_GA // GATE_ROWS
    gs0 = OFF_GS // GATE_ROWS
    nr = D_MODEL // GATE_ROWS
    blk = pl.BlockSpec((GATE_ROWS, t), lambda r: (r, 0))
    return blk, [
        pl.BlockSpec((GATE_ROWS, t), lambda r: (ga0 + r, 0)),
        pl.BlockSpec((GATE_ROWS, t), lambda r: (gs0 + r, 0)),
        pl.BlockSpec((GATE_ROWS, 1), lambda r: (r, 0)),
        pl.BlockSpec((GATE_ROWS, 1), lambda r: (nr + r, 0)),
        blk, blk,
    ]


def _gate_fwd(proj, b_col, attn, ssd):
    t = proj.shape[1]
    blk, specs = _gate_specs(t)

    def body(ga_ref, gs_ref, ba_ref, bs_ref, a_ref, s_ref, o_ref):
        o_ref[...] = (_sigmoid(ga_ref[...] + ba_ref[...]) * a_ref[...]
                      + _sigmoid(gs_ref[...] + bs_ref[...]) * s_ref[...]).astype(BF16)

    return pl.pallas_call(
        body,
        name="gate_fwd",
        grid=(D_MODEL // GATE_ROWS,),
        in_specs=specs,
        out_specs=blk,
        out_shape=jax.ShapeDtypeStruct((D_MODEL, t), BF16),
        compiler_params=_cparams("parallel"),
    )(proj, proj, b_col, b_col, attn, ssd)


def _gate_bwd(proj, b_col, attn, ssd, dmix):
    t = proj.shape[1]
    blk, specs = _gate_specs(t)

    def body(ga_ref, gs_ref, ba_ref, bs_ref, a_ref, s_ref, dm_ref, da_ref, dso_ref, dga_ref, dgs_ref, dba_ref, dbs_ref):
        dm = dm_ref[...]
        sa = _sigmoid(ga_ref[...] + ba_ref[...])
        ss = _sigmoid(gs_ref[...] + bs_ref[...])
        da_ref[...] = (dm * sa).astype(BF16)
        dso_ref[...] = (dm * ss).astype(BF16)
        dga = dm * a_ref[...] * sa * (1.0 - sa)
        dgs = dm * s_ref[...] * ss * (1.0 - ss)
        dga_ref[...] = dga.astype(BF16)
        dgs_ref[...] = dgs.astype(BF16)
        dba_ref[...] = _rowsum(dga)
        dbs_ref[...] = _rowsum(dgs)

    col = pl.BlockSpec((GATE_ROWS, 1), lambda r: (r, 0))
    act = jax.ShapeDtypeStruct((D_MODEL, t), BF16)
    bias = jax.ShapeDtypeStruct((D_MODEL, 1), F32)
    return pl.pallas_call(
        body,
        name="gate_bwd",
        grid=(D_MODEL // GATE_ROWS,),
        in_specs=specs + [blk],
        out_specs=[blk, blk, blk, blk, col, col],
        out_shape=[act, act, act, act, bias, bias],
        compiler_params=_cparams("parallel"),
    )(proj, proj, b_col, b_col, attn, ssd, dmix)


FFN_ROWS = 256


def _ffn_fwd(u0, w_col, b_col):
    t = u0.shape[2]

    def body(u_ref, w_ref, b_ref, o_ref):
        def strip(rows):
            val, _ = _causal_conv(u_ref[0, rows, :], w_ref[0, rows, :], b_ref[0, rows, :])
            gt, _ = _causal_conv(u_ref[1, rows, :], w_ref[1, rows, :], b_ref[1, rows, :])
            o_ref[rows, :] = (gt * _sigmoid(gt) * val).astype(BF16)

        strip(slice(None))

    return pl.pallas_call(
        body,
        name="ffn_fwd",
        grid=(D_FF // FFN_ROWS,),
        in_specs=[
            pl.BlockSpec((2, FFN_ROWS, t), lambda i: (0, i, 0)),
            pl.BlockSpec((2, FFN_ROWS, FFN_CONV), lambda i: (0, i, 0)),
            pl.BlockSpec((2, FFN_ROWS, 1), lambda i: (0, i, 0)),
        ],
        out_specs=pl.BlockSpec((FFN_ROWS, t), lambda i: (i, 0)),
        out_shape=jax.ShapeDtypeStruct((D_FF, t), BF16),
        compiler_params=_cparams("parallel"),
    )(u0, w_col, b_col)


def _ffn_bwd(u0, w_col, b_col, dg):
    t = u0.shape[2]

    def body(u_ref, w_ref, b_ref, dg_ref, du_ref, dwb_ref):
        def strip(rows):
            xval, wval = u_ref[0, rows, :], w_ref[0, rows, :]
            xgt, wgt = u_ref[1, rows, :], w_ref[1, rows, :]
            val, sh_val = _causal_conv(xval, wval, b_ref[0, rows, :])
            gt, sh_gt = _causal_conv(xgt, wgt, b_ref[1, rows, :])
            sg = _sigmoid(gt)
            dgv = dg_ref[rows, :]
            dval = dgv * (gt * sg)
            dgt = dgv * val * (sg * (1.0 + gt * (1.0 - sg)))
            dx, dwb_ref[0, rows, :] = _causal_conv_bwd(dval, xval, sh_val, wval)
            du_ref[0, rows, :] = dx.astype(BF16)
            dx, dwb_ref[1, rows, :] = _causal_conv_bwd(dgt, xgt, sh_gt, wgt)
            du_ref[1, rows, :] = dx.astype(BF16)

        strip(slice(None))

    return pl.pallas_call(
        body,
        name="ffn_bwd",
        grid=(D_FF // FFN_ROWS,),
        in_specs=[
            pl.BlockSpec((2, FFN_ROWS, t), lambda i: (0, i, 0)),
            pl.BlockSpec((2, FFN_ROWS, FFN_CONV), lambda i: (0, i, 0)),
            pl.BlockSpec((2, FFN_ROWS, 1), lambda i: (0, i, 0)),
            pl.BlockSpec((FFN_ROWS, t), lambda i: (i, 0)),
        ],
        out_specs=[pl.BlockSpec((2, FFN_ROWS, t), lambda i: (0, i, 0)), pl.BlockSpec((2, FFN_ROWS, 128), lambda i: (0, i, 0))],
        out_shape=[jax.ShapeDtypeStruct((2, D_FF, t), BF16), jax.ShapeDtypeStruct((2, D_FF, 128), F32)],
        compiler_params=_cparams("parallel"),
    )(u0, w_col, b_col, dg)


def _adamw_math(w, g, m, v):
    m = ADAM_B1 * m + (1.0 - ADAM_B1) * g
    v = ADAM_B2 * v + (1.0 - ADAM_B2) * (g * g)
    m_hat = m / (1.0 - ADAM_B1 ** ADAM_STEP)
    v_hat = v / (1.0 - ADAM_B2 ** ADAM_STEP)
    delta = -ADAM_LR * (m_hat / (jnp.sqrt(v_hat) + ADAM_EPS) + ADAM_WD * w)
    return delta, m, v


def _adamw_sharded(parts, w, m, v, name):
    r, c = w.shape
    tc = _tile(c, (256, 128))

    def body(p_ref, w_ref, m_ref, v_ref, g_ref, d_ref, nm_ref, nv_ref):
        g = p_ref[0].astype(F32)
        for s in range(1, N_DEV):
            g = g + p_ref[s].astype(F32)
        g_ref[...] = g
        d_ref[...], nm_ref[...], nv_ref[...] = _adamw_math(w_ref[...], g, m_ref[...], v_ref[...])

    blk = pl.BlockSpec((r, tc), lambda i: (0, i))
    out = jax.ShapeDtypeStruct((r, c), F32)
    return pl.pallas_call(
        body,
        name=name,
        grid=(c // tc,),
        in_specs=[pl.BlockSpec((N_DEV, r, tc), lambda i: (0, 0, i)), blk, blk, blk],
        out_specs=[blk, blk, blk, blk],
        out_shape=[out, out, out, out],
        compiler_params=_cparams("parallel"),
    )(parts, w, m, v)


def _sum_slots(parts):
    _, r, c = parts.shape

    def body(p_ref, o_ref):
        g = p_ref[0]
        for s in range(1, N_DEV):
            g = g + p_ref[s]
        o_ref[...] = g

    return pl.pallas_call(body, name="sum_small_grads", out_shape=jax.ShapeDtypeStruct((r, c), F32))(parts)


def _adamw_small(g, w, m, v):
    def body(g_ref, w_ref, m_ref, v_ref, d_ref, nm_ref, nv_ref):
        d_ref[...], nm_ref[...], nv_ref[...] = _adamw_math(w_ref[...], g_ref[...], m_ref[...], v_ref[...])

    out = jax.ShapeDtypeStruct(g.shape, F32)
    return pl.pallas_call(body, name="adamw_small", out_shape=[out, out, out])(g, w, m, v)


ANY = pl.BlockSpec(memory_space=pl.ANY)
FLIPS = [(k >> 2 & 1, k >> 1 & 1, k & 1) for k in range(1, N_DEV)]


def _place():
    return lax.axis_index("x"), lax.axis_index("y"), lax.axis_index("c")


HBM = pl.BlockSpec(memory_space=pltpu.HBM)
SEM = pl.BlockSpec(memory_space=pltpu.SEMAPHORE)
EFFECT = pltpu.SideEffectType.DATAFLOW_SIDE_EFFECTING


def _peer_copy(gather, src_ref, land_ref, send_sems, recv_sems, k, sending):
    x, y, c = _place()
    fx, fy, fc = FLIPS[k]
    me = 4 * x + 2 * y + c
    peer = 4 * (x ^ fx) + 2 * (y ^ fy) + (c ^ fc)
    return pltpu.make_async_remote_copy(
        src_ref=src_ref if gather else src_ref.at[peer],
        dst_ref=land_ref.at[me if sending else peer],
        send_sem=send_sems.at[k], recv_sem=recv_sems.at[k],
        device_id=(x ^ fx, y ^ fy, c ^ fc), device_id_type=MESH)


def _gather_start(srcs, name):
    n = len(srcs)
    lands = [lax.empty((N_DEV,) + s.shape, s.dtype) for s in srcs]

    def body(*refs):
        src_refs, land_refs = refs[:n], refs[n:2 * n]
        send, recv = refs[2 * n:3 * n], refs[3 * n:4 * n]
        local_sems = refs[6 * n]
        x, y, c = _place()
        me = 4 * x + 2 * y + c
        local = [pltpu.make_async_copy(src_refs[i], land_refs[i].at[me], local_sems.at[i]) for i in range(n)]
        for cp in local:
            cp.start()
        for i in range(n):
            for k in range(N_DEV - 1):
                _peer_copy(True, src_refs[i], land_refs[i], send[i], recv[i], k, True).start()
        for cp in local:
            cp.wait()

    sem = pltpu.SemaphoreType.DMA((N_DEV - 1,))
    hbm = lambda a: pltpu.HBM(a.shape, a.dtype)
    res = pl.pallas_call(
        body,
        name=name,
        in_specs=[HBM] * (2 * n),
        out_specs=[SEM] * (2 * n) + [HBM] * (2 * n),
        out_shape=[sem] * (2 * n) + [hbm(s) for s in srcs] + [hbm(a) for a in lands],
        input_output_aliases={i: 2 * n + i for i in range(2 * n)},
        scratch_shapes=[pltpu.SemaphoreType.DMA((n,))],
        compiler_params=pltpu.CompilerParams(has_side_effects=EFFECT),
    )(*[pltpu.with_memory_space_constraint(a, pltpu.HBM) for a in list(srcs) + lands])
    return res[:n], res[n:2 * n], res[2 * n:3 * n], res[3 * n:4 * n]


def _exchange_wait(send_sems, recv_sems, src, land, after, gather, name):
    def body(src_ref, land_ref, send_ref, recv_ref, after_ref, src_out, land_out):
        for k in range(N_DEV - 1):
            cp = _peer_copy(gather, src_ref, land_ref, send_ref, recv_ref, k, False)
            cp.wait_send()
            cp.wait_recv()

    hbm = lambda a: pltpu.HBM(a.shape, a.dtype)
    return pl.pallas_call(
        body,
        name=name,
        in_specs=[HBM, HBM, SEM, SEM, ANY],
        out_specs=[HBM, HBM],
        out_shape=[hbm(src), hbm(land)],
        input_output_aliases={0: 0, 1: 1},
        compiler_params=pltpu.CompilerParams(has_side_effects=EFFECT),
    )(src, land, send_sems, recv_sems, after)[1]


def _col(v):
    return v.reshape(-1, 1).astype(F32)


def _local_step(xt, tgt, weight, small):
    t = xt.shape[1]
    n1 = _col(small["norm1_w"])
    n2 = _col(small["norm2_w"])
    nf = _col(small["final_norm_w"])
    bg = _col(small["b_gate"])
    sinks = small["attn_sinks"].reshape(-1).astype(F32)
    cbias = _col(small["ssd_conv_b"])
    dtb = _col(small["dt_bias"])
    alog = _col(small["a_log"])
    dsk = _col(small["d_skip"])
    gnw = _col(small["ssd_norm_w"])
    fb = small["ffn_conv_b"].reshape(2, D_FF, 1)

    xn = _norm_fwd(xt, n1, "norm1_fwd")
    cw = weight("ssd_conv_w", xn).T
    fw = weight("ffn_conv_w", xn).T.reshape(2, D_FF, FFN_CONV)
    w_in_t = weight("w_in", xn)
    proj = _matmul(w_in_t, xn, nt=False, out_dtype=F32, name="mm_in")
    ao, lse = _attn_fwd(proj, sinks)
    w_ao = weight("w_attn_o", ao)
    attn = _matmul(w_ao, ao, nt=False, out_dtype=F32, name="mm_attn_o", tn_a=True)
    xbc = _conv_silu_fwd(proj, cw, cbias)
    y, hst = _ssd_fwd(xbc, proj, dtb, alog, dsk)
    yn = _gnorm_fwd(y, proj, gnw)
    w_so = weight("w_ssd_o", yn)
    ssd = _matmul(w_so, yn, nt=False, out_dtype=F32, name="mm_ssd_o", tn_a=True)
    mix = _gate_fwd(proj, bg, attn, ssd)
    w_out = weight("w_out", mix)
    h1 = _matmul(w_out, mix, nt=False, out_dtype=F32, name="mm_out", add=xt, tn_a=True)
    hn = _norm_fwd(h1, n2, "norm2_fwd")
    w_up_t = weight("w_up", hn)
    u0 = _matmul(w_up_t, hn, nt=False, out_dtype=F32, name="mm_up").reshape(2, D_FF, t)
    gl = _ffn_fwd(u0, fw, fb)
    w_down = weight("w_down", gl)
    h2 = _matmul(w_down, gl, nt=False, out_dtype=F32, name="mm_down", add=h1, tn_a=True)
    dh2, loss, d_nf = _final_norm_loss(h2, tgt, nf)

    g = {}
    handles = {}

    def sending(weight_name, grad, *args, **kwargs):
        out, handles[weight_name] = _matmul(*args, send=grad.reshape(N_DEV, -1, D_MODEL), **kwargs)
        return out

    g_down = _matmul(gl, dh2, nt=True, out_dtype=BF16, name="mm_d_w_down")
    dgl = sending("w_down", g_down, w_down, dh2, nt=False, out_dtype=F32, name="mm_d_glu")
    du0, d_fwb = _ffn_bwd(u0, fw, fb, dgl)
    du0 = du0.reshape(2 * D_FF, t)
    g_up = _matmul(du0, hn, nt=True, out_dtype=BF16, name="mm_d_w_up")
    dhn = sending("w_up", g_up, w_up_t, du0, nt=False, out_dtype=F32, name="mm_d_hn", tn_a=True)
    dh1, d_n2 = _norm_bwd(dhn, h1, n2, dh2, "norm2_bwd")
    g_out = _matmul(mix, dh1, nt=True, out_dtype=BF16, name="mm_d_w_out")
    dmix = sending("w_out", g_out, w_out, dh1, nt=False, out_dtype=F32, name="mm_d_mix")
    d_attn, d_ssd, d_ga, d_gs, d_ba, d_bs = _gate_bwd(proj, bg, attn, ssd, dmix)
    g_ao = _matmul(ao, d_attn, nt=True, out_dtype=BF16, name="mm_d_w_attn_o")
    dao = sending("w_attn_o", g_ao, w_ao, d_attn, nt=False, out_dtype=F32, name="mm_d_ao")
    dq, dk, dv, d_sinks = _attn_bwd(proj, sinks, ao, lse, dao)
    g_so = _matmul(yn, d_ssd, nt=True, out_dtype=BF16, name="mm_d_w_ssd_o")
    dyn = sending("w_ssd_o", g_so, w_so, d_ssd, nt=False, out_dtype=F32, name="mm_d_yn")
    dy, dz, d_gnw = _gnorm_bwd(dyn, y, proj, gnw)
    dxs, dbm, dcm, ddt, d_alog, d_dsk, d_dtb = _ssd_bwd(xbc, proj, dtb, alog, dsk, hst, dy)
    dx_xs, dwb_xs = _conv_silu_bwd(proj, cw, cbias, dxs, 0, "ssd_conv_bwd_x")
    dx_b, dwb_b = _conv_silu_bwd(proj, cw, cbias, dbm, D_INNER, "ssd_conv_bwd_b")
    dx_c, dwb_c = _conv_silu_bwd(proj, cw, cbias, dcm, D_INNER + BC_DIM, "ssd_conv_bwd_c")
    dwb_conv = jnp.concatenate([dwb_xs, dwb_b, dwb_c], axis=0)
    dproj = jnp.concatenate([dq, dk, dv, dz, dx_xs, dx_b, dx_c, ddt.astype(BF16), d_ga, d_gs], axis=0)
    g_in = _matmul(dproj, xn, nt=True, out_dtype=BF16, name="mm_d_w_in")
    dxn = sending("w_in", g_in, w_in_t, dproj, nt=False, out_dtype=F32, name="mm_d_xn", tn_a=True)
    dx, d_n1 = _norm_bwd(dxn, xt, n1, dh1, "norm1_bwd")

    g["norm1_w"] = d_n1
    g["b_gate"] = jnp.concatenate([d_ba, d_bs], axis=0)
    g["attn_sinks"] = d_sinks
    g["ssd_conv_w"] = dwb_conv[:, :SSD_CONV].T
    g["ssd_conv_b"] = dwb_conv[:, SSD_CONV]
    g["dt_bias"] = d_dtb
    g["a_log"] = d_alog
    g["d_skip"] = d_dsk
    g["ssd_norm_w"] = d_gnw
    g["norm2_w"] = d_n2
    d_fwb = d_fwb.reshape(2 * D_FF, 128)
    g["ffn_conv_w"] = d_fwb[:, :FFN_CONV].T
    g["ffn_conv_b"] = d_fwb[:, FFN_CONV]
    g["final_norm_w"] = d_nf
    return loss, dx, g, handles


SHARDED = ("w_in", "w_attn_o", "w_ssd_o", "w_out", "w_up", "w_down")
SMALL = ("norm1_w", "b_gate", "attn_sinks", "ssd_conv_w", "ssd_conv_b", "dt_bias", "a_log", "d_skip", "ssd_norm_w",
         "norm2_w", "ffn_conv_w", "ffn_conv_b", "final_norm_w")
SMALL_SHAPES = {"norm1_w": (1, D_MODEL), "b_gate": (1, 2 * D_MODEL), "attn_sinks": (1, N_Q_HEADS),
                "ssd_conv_w": (1, SSD_CONV, XBC_DIM), "ssd_conv_b": (1, XBC_DIM), "dt_bias": (1, N_SSD_HEADS),
                "a_log": (1, N_SSD_HEADS), "d_skip": (1, N_SSD_HEADS), "ssd_norm_w": (1, D_INNER),
                "norm2_w": (1, D_MODEL), "ffn_conv_w": (1, FFN_CONV, 2 * D_FF), "ffn_conv_b": (1, 2 * D_FF),
                "final_norm_w": (D_MODEL,)}
WEIGHT_ORDER = ("norm1_w", "w_in", "b_gate", "attn_sinks", "w_attn_o", "ssd_conv_w", "ssd_conv_b", "dt_bias", "a_log",
                "d_skip", "ssd_norm_w", "w_ssd_o", "w_out", "norm2_w", "w_up", "ffn_conv_w", "ffn_conv_b", "w_down",
                "final_norm_w")


def _pack(parts):
    flat = jnp.concatenate([p.reshape(-1).astype(F32) for p in parts])
    rows = -(-flat.shape[0] // 1024) * 8
    return jnp.pad(flat, (0, rows * 128 - flat.shape[0])).reshape(rows, 128)


def _unpack(packed, shapes):
    flat = packed.reshape(-1)
    out, pos = [], 0
    for shp in shapes:
        size = 1
        for d in shp:
            size *= d
        out.append(flat[pos:pos + size].reshape(shp))
        pos += size
    return out


def kernel(x, norm1_w, w_in, b_gate, attn_sinks, w_attn_o, ssd_conv_w, ssd_conv_b, dt_bias, a_log, d_skip, ssd_norm_w, w_ssd_o, w_out, norm2_w, w_up, ffn_conv_w, ffn_conv_b, w_down, final_norm_w, loss_target, m_norm1_w, m_w_in, m_b_gate, m_attn_sinks, m_w_attn_o, m_ssd_conv_w, m_ssd_conv_b, m_dt_bias, m_a_log, m_d_skip, m_ssd_norm_w, m_w_ssd_o, m_w_out, m_norm2_w, m_w_up, m_ffn_conv_w, m_ffn_conv_b, m_w_down, m_final_norm_w, v_norm1_w, v_w_in, v_b_gate, v_attn_sinks, v_w_attn_o, v_ssd_conv_w, v_ssd_conv_b, v_dt_bias, v_a_log, v_d_skip, v_ssd_norm_w, v_w_ssd_o, v_w_out, v_norm2_w, v_w_up, v_ffn_conv_w, v_ffn_conv_b, v_w_down, v_final_norm_w):
    w = dict(norm1_w=norm1_w, w_in=w_in, b_gate=b_gate, attn_sinks=attn_sinks, w_attn_o=w_attn_o, ssd_conv_w=ssd_conv_w, ssd_conv_b=ssd_conv_b, dt_bias=dt_bias, a_log=a_log, d_skip=d_skip, ssd_norm_w=ssd_norm_w, w_ssd_o=w_ssd_o, w_out=w_out, norm2_w=norm2_w, w_up=w_up, ffn_conv_w=ffn_conv_w, ffn_conv_b=ffn_conv_b, w_down=w_down, final_norm_w=final_norm_w)
    m = dict(norm1_w=m_norm1_w, w_in=m_w_in, b_gate=m_b_gate, attn_sinks=m_attn_sinks, w_attn_o=m_w_attn_o, ssd_conv_w=m_ssd_conv_w, ssd_conv_b=m_ssd_conv_b, dt_bias=m_dt_bias, a_log=m_a_log, d_skip=m_d_skip, ssd_norm_w=m_ssd_norm_w, w_ssd_o=m_w_ssd_o, w_out=m_w_out, norm2_w=m_norm2_w, w_up=m_w_up, ffn_conv_w=m_ffn_conv_w, ffn_conv_b=m_ffn_conv_b, w_down=m_w_down, final_norm_w=m_final_norm_w)
    v = dict(norm1_w=v_norm1_w, w_in=v_w_in, b_gate=v_b_gate, attn_sinks=v_attn_sinks, w_attn_o=v_w_attn_o, ssd_conv_w=v_ssd_conv_w, ssd_conv_b=v_ssd_conv_b, dt_bias=v_dt_bias, a_log=v_a_log, d_skip=v_d_skip, ssd_norm_w=v_ssd_norm_w, w_ssd_o=v_w_ssd_o, w_out=v_w_out, norm2_w=v_norm2_w, w_up=v_w_up, ffn_conv_w=v_ffn_conv_w, ffn_conv_b=v_ffn_conv_b, w_down=v_w_down, final_norm_w=v_final_norm_w)
    me = 4 * lax.axis_index("x") + 2 * lax.axis_index("y") + lax.axis_index("c")
    conv_cols = XBC_DIM // N_DEV
    ffn_cols = 2 * D_FF // N_DEV

    shards = {"ssd_conv_w": ssd_conv_w[0], "ffn_conv_w": ffn_conv_w[0], "w_in": w_in[0].T.astype(BF16),
              "w_attn_o": w_attn_o[0].astype(BF16), "w_ssd_o": w_ssd_o[0].astype(BF16), "w_out": w_out[0].astype(BF16),
              "w_up": w_up[0].T.astype(BF16), "w_down": w_down[0].astype(BF16)}
    order = list(shards)
    g_send, g_recv, g_src, g_land = _gather_start(list(shards.values()), "gather_start")

    def weight(name, after):
        i = order.index(name)
        land = _exchange_wait(g_send[i], g_recv[i], g_src[i], g_land[i], after, True, "gather_wait_" + name)
        if name == "ssd_conv_w":
            return jnp.transpose(land, (1, 0, 2)).reshape(SSD_CONV, XBC_DIM)
        if name == "ffn_conv_w":
            return jnp.transpose(land, (1, 0, 2)).reshape(FFN_CONV, 2 * D_FF)
        return land.reshape(-1, D_MODEL)

    small = {k: w[k][0] if k != "final_norm_w" else w[k] for k in SMALL}
    loss, dx, g, pending = _local_step(x[0].T, loss_target[0].T, weight, small)

    packed = _pack([loss] + [g[k] for k in SMALL])
    s_send, s_recv, s_src, s_land = _gather_start([packed], "small_grads_start")

    res = {}
    after = s_src[0]
    for name in ("w_down", "w_up", "w_out", "w_attn_o", "w_ssd_o", "w_in"):
        parts = _exchange_wait(*pending[name], after, False, "grad_wait_" + name)
        view = (lambda a: a[0].T) if name in ("w_in", "w_up") else (lambda a: a[0])
        res[name] = _adamw_sharded(parts, view(w[name]), view(m[name]), view(v[name]), "adamw_" + name)
        after = res[name][0]
        if name in ("w_in", "w_up"):
            res[name] = [r.T for r in res[name]]

    total = _sum_slots(_exchange_wait(s_send[0], s_recv[0], s_src[0], s_land[0], after, True, "small_grads_wait"))
    tot = _unpack(total, [(1,)] + [SMALL_SHAPES[k] for k in SMALL])
    loss_sum = tot[0].reshape(())
    gs = dict(zip(SMALL, tot[1:]))
    gs["ssd_conv_w"] = lax.dynamic_slice_in_dim(gs["ssd_conv_w"], me * conv_cols, conv_cols, axis=2)
    gs["ffn_conv_w"] = lax.dynamic_slice_in_dim(gs["ffn_conv_w"], me * ffn_cols, ffn_cols, axis=2)
    upd = _adamw_small(_pack([gs[k] for k in SMALL]), _pack([w[k] for k in SMALL]), _pack([m[k] for k in SMALL]),
                       _pack([v[k] for k in SMALL]))
    shapes = [w[k].shape for k in SMALL]
    d_s, m_s, v_s = (dict(zip(SMALL, _unpack(u, shapes))) for u in upd)
    for k in SMALL:
        res[k] = (gs[k], d_s[k], m_s[k], v_s[k])

    grad_x = dx.T[None]
    outs = [loss_sum, grad_x]
    for i in range(4):
        for k in WEIGHT_ORDER:
            r = res[k][i]
            outs.append(r[None] if k in SHARDED else r)
    return tuple(outs)
```

```python
import functools

import jax
import jax.numpy as jnp
from jax import lax
from jax.experimental import pallas as pl
from jax.experimental.pallas import tpu as pltpu

F32 = jnp.float32
BF16 = jnp.bfloat16
HIGHEST = lax.Precision.HIGHEST

D_MODEL = 1024
N_Q_HEADS = 16
N_KV_HEADS = 4
HEAD_DIM = 64
WINDOW = 128
Q_PER_KV = N_Q_HEADS // N_KV_HEADS
Q_DIM = N_Q_HEADS * HEAD_DIM
KV_DIM = N_KV_HEADS * HEAD_DIM
D_INNER = 2048
SSD_HEAD_DIM = 64
N_SSD_HEADS = 32
N_SSD_GROUPS = 4
HEADS_PER_GROUP = N_SSD_HEADS // N_SSD_GROUPS
D_STATE = 128
BC_DIM = N_SSD_GROUPS * D_STATE
XBC_DIM = D_INNER + 2 * BC_DIM
SSD_CONV = 4
CHUNK = 128
D_FF = 2816
FFN_CONV = 3
EPS = 1e-5
NEG = -1e30
IN_DIM = 8736
N_DEV = 8

OFF_Q = 0
OFF_K = OFF_Q + Q_DIM
OFF_V = OFF_K + KV_DIM
OFF_Z = OFF_V + KV_DIM
OFF_X = OFF_Z + D_INNER
OFF_DT = OFF_X + XBC_DIM
OFF_GA = OFF_DT + N_SSD_HEADS
OFF_GS = OFF_GA + D_MODEL

ADAM_LR = 0.001
ADAM_B1 = 0.9
ADAM_B2 = 0.999
ADAM_EPS = 1e-08
ADAM_WD = 0.01
ADAM_STEP = 10

VMEM_LIMIT = 48 * 1024 * 1024
MESH = pl.DeviceIdType.MESH


def _cparams(*sem):
    return pltpu.CompilerParams(dimension_semantics=sem, vmem_limit_bytes=VMEM_LIMIT)


def _tile(n, prefs):
    for p in prefs:
        if n % p == 0:
            return p
    return n


def _sigmoid(x):
    return 1.0 / (1.0 + jnp.exp(-x))


def _softplus(x):
    return jnp.maximum(x, 0.0) + jnp.log(1.0 + jnp.exp(-jnp.abs(x)))


def _rowsum(x):
    return jnp.sum(x, axis=1, keepdims=True)


def _colsum(x):
    return jnp.sum(x, axis=0, keepdims=True)


def _dot(a, b):
    return jnp.dot(a, b, preferred_element_type=F32)


def _dot_nt(a, b):
    return lax.dot_general(a, b, (((1,), (1,)), ((), ())), preferred_element_type=F32)


def _dot_tn(a, b):
    return lax.dot_general(a, b, (((0,), (0,)), ((), ())), preferred_element_type=F32)


def _shift_right(x, j):
    if j == 0:
        return x
    r = pltpu.roll(x, j, 1)
    lane = lax.broadcasted_iota(jnp.int32, (x.shape[0], 128), 1)
    return jnp.concatenate([jnp.where(lane >= j, r[:, :128], 0.0), r[:, 128:]], axis=1)


def _shift_left(x, j):
    if j == 0:
        return x
    n = x.shape[1]
    r = pltpu.roll(x, n - j, 1)
    lane = lax.broadcasted_iota(jnp.int32, (x.shape[0], 128), 1)
    return jnp.concatenate([r[:, :n - 128], jnp.where(lane < 128 - j, r[:, n - 128:], 0.0)], axis=1)


def _causal_conv(xv, wv, bv):
    taps = wv.shape[1]
    shifted = [_shift_right(xv, taps - 1 - k) for k in range(taps - 1)]
    y = bv + wv[:, taps - 1:taps] * xv
    for k in range(taps - 1):
        y = y + wv[:, k:k + 1] * shifted[k]
    return y, shifted


def _causal_conv_bwd(dy, xv, shifted, wv):
    taps = wv.shape[1]
    lane = lax.broadcasted_iota(jnp.int32, (dy.shape[0], 128), 1)
    dwb = jnp.where(lane == taps, _rowsum(dy), 0.0)
    dwb = jnp.where(lane == taps - 1, _rowsum(dy * xv), dwb)
    dx = wv[:, taps - 1:taps] * dy
    for k in range(taps - 1):
        dx = dx + wv[:, k:k + 1] * _shift_left(dy, taps - 1 - k)
        dwb = jnp.where(lane == k, _rowsum(dy * shifted[k]), dwb)
    return dx, dwb


MATMUL_VMEM_BUDGET = 36 * 1024 * 1024
MATMUL_MAX_TK = 3072


MATMUL_MAX_TM = 768


def _largest_tile(n, align, cap):
    return max(d for d in range(align, min(n, cap) + 1, align) if n % d == 0)


def _matmul_tiles(m, n, k, a_bytes, b_bytes, out_bytes, has_add, m_align, k_align):
    tm = _largest_tile(m, m_align, MATMUL_MAX_TM)
    tk = _largest_tile(k, k_align, MATMUL_MAX_TK)
    for tn in sorted({d for d in range(128, n + 1, 128) if n % d == 0}, reverse=True):
        need = 2 * (tm * tk * a_bytes + tk * tn * b_bytes) + tm * tn * (2 * out_bytes + (4 if k > tk else 0) + (8 if has_add else 0))
        if tn <= 3072 and need <= MATMUL_VMEM_BUDGET:
            return tm, tn, tk
    return tm, 128, tk


def _matmul(a, b, *, nt, out_dtype, name, add=None, tn_a=False, send=None):
    if tn_a:
        k, m = a.shape
    else:
        m, k = a.shape
    n = b.shape[0] if nt else b.shape[1]
    tm, tn, tk = _matmul_tiles(m, n, k, a.dtype.itemsize, b.dtype.itemsize, jnp.dtype(out_dtype).itemsize, add is not None,
                               128 if tn_a else 16, 16 if tn_a and not nt else 128)
    nk = k // tk
    grid = (m // tm, n // tn, nk)

    def body(a_ref, b_ref, *rest):
        r_ref = None
        if add is not None:
            r_ref, rest = rest[0], rest[1:]
        if send is not None:
            src_ref, land_ref, rest = rest[0], rest[1], rest[2:]
            send_sems, recv_sems, local_sem = rest[1], rest[2], rest[-1]
            rest = (rest[0],) + rest[5:-1]
            x, y, c = _place()
            me = 4 * x + 2 * y + c
            local = pltpu.make_async_copy(src_ref.at[me], land_ref.at[me], local_sem)
            step = (pl.program_id(0) * grid[1] + pl.program_id(1)) * grid[2] + pl.program_id(2)

            @pl.when(step == 0)
            def _():
                local.start()
                for peer in range(N_DEV - 1):
                    _peer_copy(False, src_ref, land_ref, send_sems, recv_sems, peer, True).start()

            @pl.when(step == grid[0] * grid[1] * grid[2] - 1)
            def _():
                local.wait()

        o_ref = rest[0]
        av = a_ref[...].astype(BF16)
        bv = b_ref[...].astype(BF16)
        part = _dot_tn(av, bv) if tn_a else _dot_nt(av, bv) if nt else _dot(av, bv)

        def finish(r):
            if add is not None:
                r = r + r_ref[...]
            o_ref[...] = r.astype(out_dtype)

        if nk == 1:
            finish(part)
            return
        acc = rest[1]
        kk = pl.program_id(2)

        @pl.when(kk == 0)
        def _():
            acc[...] = part

        @pl.when((kk > 0) & (kk < nk - 1))
        def _():
            acc[...] += part

        @pl.when(kk == nk - 1)
        def _():
            finish(acc[...] + part)

    in_specs = [
        pl.BlockSpec((tk, tm), lambda i, j, kk: (kk, i)) if tn_a else pl.BlockSpec((tm, tk), lambda i, j, kk: (i, kk)),
        pl.BlockSpec((tn, tk), lambda i, j, kk: (j, kk)) if nt else pl.BlockSpec((tk, tn), lambda i, j, kk: (kk, j)),
    ]
    args = [a, b]
    if add is not None:
        in_specs.append(pl.BlockSpec((tm, tn), lambda i, j, kk: (i, j)))
        args.append(add)
    out_specs = [pl.BlockSpec((tm, tn), lambda i, j, kk: (i, j))]
    out_shape = [jax.ShapeDtypeStruct((m, n), out_dtype)]
    scratch = [pltpu.VMEM((tm, tn), F32)] if nk > 1 else []
    if send is None:
        return pl.pallas_call(
            body, name=name, grid=grid, in_specs=in_specs, out_specs=out_specs[0], out_shape=out_shape[0],
            scratch_shapes=scratch, compiler_params=_cparams("parallel", "parallel", "arbitrary"),
        )(*args)
    sem = pltpu.SemaphoreType.DMA((N_DEV - 1,))
    hbm = pltpu.HBM(send.shape, send.dtype)
    first = len(args)
    res = pl.pallas_call(
        body, name=name, grid=grid,
        in_specs=in_specs + [HBM, HBM],
        out_specs=out_specs + [SEM, SEM, HBM, HBM],
        out_shape=out_shape + [sem, sem, hbm, hbm],
        input_output_aliases={first: 3, first + 1: 4},
        scratch_shapes=scratch + [pltpu.SemaphoreType.DMA(())],
        compiler_params=pltpu.CompilerParams(dimension_semantics=("arbitrary",) * 3, vmem_limit_bytes=VMEM_LIMIT,
                                             has_side_effects=EFFECT),
    )(*args, pltpu.with_memory_space_constraint(send, pltpu.HBM),
      pltpu.with_memory_space_constraint(lax.empty(send.shape, send.dtype), pltpu.HBM))
    return res[0], tuple(res[1:])


def _norm_fwd(x, w_col, name):
    f, t = x.shape
    tt = _tile(t, (512, 256, 128))

    def body(x_ref, w_ref, o_ref):
        xv = x_ref[...]
        r = lax.rsqrt(jnp.mean(xv * xv, axis=0, keepdims=True) + EPS)
        o_ref[...] = (xv * r * w_ref[...]).astype(BF16)

    return pl.pallas_call(
        body,
        name=name,
        grid=(t // tt,),
        in_specs=[pl.BlockSpec((f, tt), lambda i: (0, i)), pl.BlockSpec((f, 1), lambda i: (0, 0))],
        out_specs=pl.BlockSpec((f, tt), lambda i: (0, i)),
        out_shape=jax.ShapeDtypeStruct((f, t), BF16),
        compiler_params=_cparams("parallel"),
    )(x, w_col)


def _norm_bwd(dy, x, w_col, res, name):
    f, t = x.shape
    tt = _tile(t, (512, 256, 128))

    def body(dy_ref, x_ref, w_ref, res_ref, dx_ref, dw_ref):
        @pl.when(pl.program_id(0) == 0)
        def _():
            dw_ref[...] = jnp.zeros_like(dw_ref)

        xv = x_ref[...]
        r = lax.rsqrt(jnp.mean(xv * xv, axis=0, keepdims=True) + EPS)
        xhat = xv * r
        dyv = dy_ref[...]
        dw_ref[...] += _rowsum(dyv * xhat)
        dxhat = dyv * w_ref[...]
        dx_ref[...] = res_ref[...] + r * (dxhat - xhat * jnp.mean(dxhat * xhat, axis=0, keepdims=True))

    blk = pl.BlockSpec((f, tt), lambda i: (0, i))
    col = pl.BlockSpec((f, 1), lambda i: (0, 0))
    return pl.pallas_call(
        body,
        name=name,
        grid=(t // tt,),
        in_specs=[blk, blk, col, blk],
        out_specs=[blk, col],
        out_shape=[jax.ShapeDtypeStruct((f, t), F32), jax.ShapeDtypeStruct((f, 1), F32)],
        compiler_params=_cparams("arbitrary"),
    )(dy, x, w_col, res)


def _final_norm_loss(h, tgt, w_col):
    f, t = h.shape
    tt = _tile(t, (512, 256, 128))

    def body(h_ref, t_ref, w_ref, dh_ref, loss_ref, dw_ref):
        @pl.when(pl.program_id(0) == 0)
        def _():
            dw_ref[...] = jnp.zeros_like(dw_ref)
            loss_ref[...] = jnp.zeros_like(loss_ref)

        xv = h_ref[...]
        r = lax.rsqrt(jnp.mean(xv * xv, axis=0, keepdims=True) + EPS)
        xhat = xv * r
        wv = w_ref[...]
        err = xhat * wv - t_ref[...]
        loss_ref[...] += 0.5 * _rowsum(jnp.mean(err * err, axis=0, keepdims=True))
        dyv = err * (1.0 / f)
        dw_ref[...] += _rowsum(dyv * xhat)
        dxhat = dyv * wv
        dh_ref[...] = r * (dxhat - xhat * jnp.mean(dxhat * xhat, axis=0, keepdims=True))

    blk = pl.BlockSpec((f, tt), lambda i: (0, i))
    col = pl.BlockSpec((f, 1), lambda i: (0, 0))
    one = pl.BlockSpec((1, 1), lambda i: (0, 0))
    return pl.pallas_call(
        body,
        name="final_norm_loss",
        grid=(t // tt,),
        in_specs=[blk, blk, col],
        out_specs=[blk, one, col],
        out_shape=[jax.ShapeDtypeStruct((f, t), F32), jax.ShapeDtypeStruct((1, 1), F32), jax.ShapeDtypeStruct((f, 1), F32)],
        compiler_params=_cparams("arbitrary"),
    )(h, tgt, w_col)


def _attn_mask(n):
    shape = (2 * WINDOW, Q_PER_KV * WINDOW)
    si = lax.broadcasted_iota(jnp.int32, shape, 0)
    qi = lax.broadcasted_iota(jnp.int32, shape, 1) & (WINDOW - 1)
    dist = WINDOW + qi - si
    return (dist >= 0) & (dist < WINDOW) & ((si >= WINDOW) | (n > 0))


def _lane_cat(ref, row0, rows):
    return jnp.concatenate([ref[row0 + i * rows:row0 + (i + 1) * rows, :] for i in range(Q_PER_KV)], axis=1)


def _attn_fwd(proj, sinks):
    t = proj.shape[1]
    nb = t // WINDOW
    scale = HEAD_DIM ** -0.5

    def body(s_ref, q_ref, kc_ref, kp_ref, vc_ref, vp_ref, o_ref, lse_ref):
        n = pl.program_id(0)
        valid = _attn_mask(n)
        for g in range(N_KV_HEADS):
            rows = slice(g * HEAD_DIM, (g + 1) * HEAD_DIM)
            kt = jnp.concatenate([kp_ref[rows, :], kc_ref[rows, :]], axis=1).astype(BF16)
            vt = jnp.concatenate([vp_ref[rows, :], vc_ref[rows, :]], axis=1).astype(BF16)
            qcat = (_lane_cat(q_ref, g * Q_PER_KV * HEAD_DIM, HEAD_DIM) * scale).astype(BF16)
            s = jnp.where(valid, _dot_tn(kt, qcat), NEG)
            sink = jnp.concatenate(
                [jnp.full((1, WINDOW), s_ref[g * Q_PER_KV + i], F32) for i in range(Q_PER_KV)], axis=1)
            m = jnp.maximum(jnp.max(s, axis=0, keepdims=True), sink)
            p = jnp.exp(s - m)
            denom = _colsum(p) + jnp.exp(sink - m)
            probs = (p / denom).astype(BF16)
            out = _dot(vt, probs)
            lse = m + jnp.log(denom)
            for i in range(Q_PER_KV):
                h = g * Q_PER_KV + i
                o_ref[h * HEAD_DIM:(h + 1) * HEAD_DIM, :] = out[:, i * WINDOW:(i + 1) * WINDOW]
                lse_ref[h:h + 1, :] = lse[:, i * WINDOW:(i + 1) * WINDOW]

    kb = OFF_K // KV_DIM
    vb = OFF_V // KV_DIM
    prev = lambda n: jnp.maximum(n - 1, 0)
    return pl.pallas_call(
        body,
        name="attn_fwd",
        grid=(nb,),
        in_specs=[
            pl.BlockSpec(memory_space=pltpu.SMEM),
            pl.BlockSpec((Q_DIM, WINDOW), lambda n: (0, n)),
            pl.BlockSpec((KV_DIM, WINDOW), lambda n: (kb, n)),
            pl.BlockSpec((KV_DIM, WINDOW), lambda n: (kb, prev(n))),
            pl.BlockSpec((KV_DIM, WINDOW), lambda n: (vb, n)),
            pl.BlockSpec((KV_DIM, WINDOW), lambda n: (vb, prev(n))),
        ],
        out_specs=[pl.BlockSpec((Q_DIM, WINDOW), lambda n: (0, n)), pl.BlockSpec((N_Q_HEADS, WINDOW), lambda n: (0, n))],
        out_shape=[jax.ShapeDtypeStruct((Q_DIM, t), F32), jax.ShapeDtypeStruct((N_Q_HEADS, t), F32)],
        compiler_params=_cparams("parallel"),
    )(sinks, proj, proj, proj, proj, proj)


def _attn_bwd(proj, sinks, out, lse, dout):
    t = proj.shape[1]
    nb = t // WINDOW
    scale = HEAD_DIM ** -0.5

    def body(s_ref, q_ref, kc_ref, kp_ref, vc_ref, vp_ref, o_ref, lse_ref, do_ref,
             dq_ref, dk_ref, dv_ref, ds_ref, dk_carry, dv_carry):
        step = pl.program_id(0)
        n = nb - 1 - step

        @pl.when(step == 0)
        def _():
            dk_carry[...] = jnp.zeros_like(dk_carry)
            dv_carry[...] = jnp.zeros_like(dv_carry)
            ds_ref[...] = jnp.zeros_like(ds_ref)

        valid = _attn_mask(n)
        for g in range(N_KV_HEADS):
            rows = slice(g * HEAD_DIM, (g + 1) * HEAD_DIM)
            q0 = g * Q_PER_KV * HEAD_DIM
            kt = jnp.concatenate([kp_ref[rows, :], kc_ref[rows, :]], axis=1).astype(BF16)
            vt = jnp.concatenate([vp_ref[rows, :], vc_ref[rows, :]], axis=1).astype(BF16)
            qf = _lane_cat(q_ref, q0, HEAD_DIM)
            qcat = qf.astype(BF16)
            ocat = _lane_cat(o_ref, q0, HEAD_DIM)
            docat = _lane_cat(do_ref, q0, HEAD_DIM)
            dob = docat.astype(BF16)
            lse_cat = jnp.concatenate(
                [lse_ref[g * Q_PER_KV + i:g * Q_PER_KV + i + 1, :] for i in range(Q_PER_KV)], axis=1)
            sink = jnp.concatenate(
                [jnp.full((1, WINDOW), s_ref[g * Q_PER_KV + i], F32) for i in range(Q_PER_KV)], axis=1)
            s = jnp.where(valid, _dot_tn(kt, (qf * scale).astype(BF16)), NEG)
            p = jnp.exp(s - lse_cat)
            dp = _dot_tn(vt, dob)
            delta = _colsum(docat * ocat)
            dsc = (p * (dp - delta)).astype(BF16)
            dsink_row = -jnp.exp(sink - lse_cat) * delta
            dq = _dot(kt, dsc) * scale
            dk = _dot_nt(qcat, dsc) * scale
            dv = _dot_nt(dob, p.astype(BF16))
            for i in range(Q_PER_KV):
                h = g * Q_PER_KV + i
                dq_ref[h * HEAD_DIM:(h + 1) * HEAD_DIM, :] = dq[:, i * WINDOW:(i + 1) * WINDOW].astype(BF16)
                ds_ref[h:h + 1, :] += _rowsum(dsink_row[:, i * WINDOW:(i + 1) * WINDOW])
            dk_ref[rows, :] = (dk[:, WINDOW:] + dk_carry[rows, :]).astype(BF16)
            dv_ref[rows, :] = (dv[:, WINDOW:] + dv_carry[rows, :]).astype(BF16)
            dk_carry[rows, :] = dk[:, :WINDOW]
            dv_carry[rows, :] = dv[:, :WINDOW]

    kb = OFF_K // KV_DIM
    vb = OFF_V // KV_DIM
    cur = lambda i: nb - 1 - i
    prev = lambda i: jnp.maximum(nb - 2 - i, 0)
    qspec = pl.BlockSpec((Q_DIM, WINDOW), lambda i: (0, cur(i)))
    kvspec = pl.BlockSpec((KV_DIM, WINDOW), lambda i: (0, cur(i)))
    return pl.pallas_call(
        body,
        name="attn_bwd",
        grid=(nb,),
        in_specs=[
            pl.BlockSpec(memory_space=pltpu.SMEM),
            qspec,
            pl.BlockSpec((KV_DIM, WINDOW), lambda i: (kb, cur(i))),
            pl.BlockSpec((KV_DIM, WINDOW), lambda i: (kb, prev(i))),
            pl.BlockSpec((KV_DIM, WINDOW), lambda i: (vb, cur(i))),
            pl.BlockSpec((KV_DIM, WINDOW), lambda i: (vb, prev(i))),
            qspec,
            pl.BlockSpec((N_Q_HEADS, WINDOW), lambda i: (0, cur(i))),
            qspec,
        ],
        out_specs=[qspec, kvspec, kvspec, pl.BlockSpec((N_Q_HEADS, 1), lambda i: (0, 0))],
        out_shape=[
            jax.ShapeDtypeStruct((Q_DIM, t), BF16),
            jax.ShapeDtypeStruct((KV_DIM, t), BF16),
            jax.ShapeDtypeStruct((KV_DIM, t), BF16),
            jax.ShapeDtypeStruct((N_Q_HEADS, 1), F32),
        ],
        scratch_shapes=[pltpu.VMEM((KV_DIM, WINDOW), F32), pltpu.VMEM((KV_DIM, WINDOW), F32)],
        compiler_params=_cparams("arbitrary"),
    )(sinks, proj, proj, proj, proj, proj, out, lse, dout)


CONV_ROWS = 256


def _conv_silu_fwd(proj, w_col, b_col):
    t = proj.shape[1]
    r0 = OFF_X // CONV_ROWS

    def body(x_ref, w_ref, b_ref, o_ref):
        def strip(rows):
            y, _ = _causal_conv(x_ref[rows, :], w_ref[rows, :], b_ref[rows, :])
            o_ref[rows, :] = y * _sigmoid(y)

        strip(slice(None))

    return pl.pallas_call(
        body,
        name="ssd_conv_fwd",
        grid=(XBC_DIM // CONV_ROWS,),
        in_specs=[
            pl.BlockSpec((CONV_ROWS, t), lambda i: (r0 + i, 0)),
            pl.BlockSpec((CONV_ROWS, SSD_CONV), lambda i: (i, 0)),
            pl.BlockSpec((CONV_ROWS, 1), lambda i: (i, 0)),
        ],
        out_specs=pl.BlockSpec((CONV_ROWS, t), lambda i: (i, 0)),
        out_shape=jax.ShapeDtypeStruct((XBC_DIM, t), F32),
        compiler_params=_cparams("parallel"),
    )(proj, w_col, b_col)


def _conv_silu_bwd(proj, w_col, b_col, dout, row0, name):
    t = proj.shape[1]
    nrows = dout.shape[0]
    p0 = (OFF_X + row0) // CONV_ROWS
    c0 = row0 // CONV_ROWS

    def body(x_ref, w_ref, b_ref, do_ref, dx_ref, dwb_ref):
        def strip(rows):
            xv = x_ref[rows, :]
            wv = w_ref[rows, :]
            y, shifted = _causal_conv(xv, wv, b_ref[rows, :])
            sg = _sigmoid(y)
            dy = do_ref[rows, :] * (sg * (1.0 + y * (1.0 - sg)))
            dx, dwb_ref[rows, :] = _causal_conv_bwd(dy, xv, shifted, wv)
            dx_ref[rows, :] = dx.astype(BF16)

        strip(slice(None))

    return pl.pallas_call(
        body,
        name=name,
        grid=(nrows // CONV_ROWS,),
        in_specs=[
            pl.BlockSpec((CONV_ROWS, t), lambda i: (p0 + i, 0)),
            pl.BlockSpec((CONV_ROWS, SSD_CONV), lambda i: (c0 + i, 0)),
            pl.BlockSpec((CONV_ROWS, 1), lambda i: (c0 + i, 0)),
            pl.BlockSpec((CONV_ROWS, t), lambda i: (i, 0)),
        ],
        out_specs=[pl.BlockSpec((CONV_ROWS, t), lambda i: (i, 0)), pl.BlockSpec((CONV_ROWS, 128), lambda i: (i, 0))],
        out_shape=[jax.ShapeDtypeStruct((nrows, t), BF16), jax.ShapeDtypeStruct((nrows, 128), F32)],
        compiler_params=_cparams("parallel"),
    )(proj, w_col, b_col, dout)


GROUP_ROWS = HEADS_PER_GROUP * SSD_HEAD_DIM


def _ssd_specs(order):
    xb = D_INNER // BC_DIM
    dtb = OFF_DT // N_SSD_HEADS
    col = pl.BlockSpec((N_SSD_HEADS, 1), lambda c: (0, 0))
    return [
        pl.BlockSpec((D_INNER, CHUNK), lambda c: (0, order(c))),
        pl.BlockSpec((BC_DIM, CHUNK), lambda c: (xb, order(c))),
        pl.BlockSpec((BC_DIM, CHUNK), lambda c: (xb + 1, order(c))),
        pl.BlockSpec((N_SSD_HEADS, CHUNK), lambda c: (dtb, order(c))),
        col, col, col,
    ]


def _ssd_common(dt_ref, dtb_ref, alog_ref):
    z = dt_ref[...] + dtb_ref[...]
    dt = _softplus(z)
    a_neg = -jnp.exp(alog_ref[...])
    d_a = dt * a_neg
    row = lax.broadcasted_iota(jnp.int32, (CHUNK, CHUNK), 0)
    colm = lax.broadcasted_iota(jnp.int32, (CHUNK, CHUNK), 1)
    upper = (row <= colm).astype(F32)
    a_cs = jnp.dot(d_a, upper, precision=HIGHEST, preferred_element_type=F32)
    a_last = _rowsum(d_a)
    return z, dt, a_neg, a_cs, a_last, row >= colm, row == colm


def _decay(a_row, causal):
    a_s = jnp.broadcast_to(a_row, (CHUNK, CHUNK))
    seg = a_s.T - a_s
    return jnp.where(causal, jnp.exp(jnp.where(causal, seg, 0.0)), 0.0)


def _ssd_fwd(xbc, proj, dtb_col, alog_col, dsk_col):
    t = xbc.shape[1]
    nc = t // CHUNK

    def body(xs_ref, b_ref, c_ref, dt_ref, dtb_ref, alog_ref, dsk_ref, y_ref, hst_ref, h_scr):
        @pl.when(pl.program_id(0) == 0)
        def _():
            h_scr[...] = jnp.zeros_like(h_scr)

        _, dt, _, a_cs, a_last, causal, _ = _ssd_common(dt_ref, dtb_ref, alog_ref)
        hst_ref[0] = h_scr[...]
        dsk = dsk_ref[...]
        for g in range(N_SSD_GROUPS):
            grows = slice(g * D_STATE, (g + 1) * D_STATE)
            bb = b_ref[grows, :].astype(BF16)
            cb_ = c_ref[grows, :].astype(BF16)
            cb = _dot_tn(cb_, bb)
            for j in range(g * HEADS_PER_GROUP, (g + 1) * HEADS_PER_GROUP):
                rows = slice(j * SSD_HEAD_DIM, (j + 1) * SSD_HEAD_DIM)
                a = a_cs[j:j + 1, :]
                m = (cb * _decay(a, causal)).astype(BF16)
                xs = xs_ref[rows, :]
                xc = xs * dt[j:j + 1, :]
                hj = h_scr[rows, :]
                y = _dot_nt(xc.astype(BF16), m) + _dot(hj.astype(BF16), cb_) * jnp.exp(a) + dsk[j:j + 1, :] * xs
                y_ref[rows, :] = y
                al = a_last[j:j + 1, :]
                w = jnp.exp(al - a)
                h_scr[rows, :] = jnp.exp(al) * hj + _dot_nt((xc * w).astype(BF16), bb)

    return pl.pallas_call(
        body,
        name="ssd_fwd",
        grid=(nc,),
        in_specs=_ssd_specs(lambda c: c),
        out_specs=[
            pl.BlockSpec((D_INNER, CHUNK), lambda c: (0, c)),
            pl.BlockSpec((1, D_INNER, D_STATE), lambda c: (c, 0, 0)),
        ],
        out_shape=[
            jax.ShapeDtypeStruct((D_INNER, t), F32),
            jax.ShapeDtypeStruct((nc, D_INNER, D_STATE), F32),
        ],
        scratch_shapes=[pltpu.VMEM((D_INNER, D_STATE), F32)],
        compiler_params=_cparams("arbitrary"),
    )(xbc, xbc, xbc, proj, dtb_col, alog_col, dsk_col)


def _ssd_bwd(xbc, proj, dtb_col, alog_col, dsk_col, hst, dy):
    t = xbc.shape[1]
    nc = t // CHUNK
    rev = lambda c: nc - 1 - c

    def body(xs_ref, b_ref, c_ref, dt_ref, dtb_ref, alog_ref, dsk_ref, hst_ref, dy_ref,
             dxs_ref, db_ref, dc_ref, ddt_ref, dalog_ref, ddsk_ref, ddtb_ref, dh_scr, da_scr, ddt_scr, dd_scr):
        @pl.when(pl.program_id(0) == 0)
        def _():
            dh_scr[...] = jnp.zeros_like(dh_scr)
            dalog_ref[...] = jnp.zeros_like(dalog_ref)
            ddsk_ref[...] = jnp.zeros_like(ddsk_ref)
            ddtb_ref[...] = jnp.zeros_like(ddtb_ref)

        z, dt, a_neg, a_cs, a_last, causal, eye = _ssd_common(dt_ref, dtb_ref, alog_ref)
        dsk = dsk_ref[...]
        last_lane = lax.broadcasted_iota(jnp.int32, (1, CHUNK), 1) == CHUNK - 1
        for g in range(N_SSD_GROUPS):
            grows = slice(g * D_STATE, (g + 1) * D_STATE)
            bb = b_ref[grows, :].astype(BF16)
            cb_ = c_ref[grows, :].astype(BF16)
            cb = _dot_tn(cb_, bb)
            dcb = jnp.zeros((CHUNK, CHUNK), F32)
            dc_acc = jnp.zeros((D_STATE, CHUNK), F32)
            db_acc = jnp.zeros((D_STATE, CHUNK), F32)
            for j in range(g * HEADS_PER_GROUP, (g + 1) * HEADS_PER_GROUP):
                rows = slice(j * SSD_HEAD_DIM, (j + 1) * SSD_HEAD_DIM)
                a = a_cs[j:j + 1, :]
                al = a_last[j:j + 1, :]
                lam = _decay(a, causal)
                mf = cb * lam
                xs = xs_ref[rows, :]
                dtj = dt[j:j + 1, :]
                xc = xs * dtj
                w = jnp.exp(al - a)
                e = jnp.exp(a)
                gam = jnp.exp(al)
                hj = hst_ref[0, rows, :]
                hjb = hj.astype(BF16)
                dyv = dy_ref[rows, :]
                dyb = dyv.astype(BF16)
                dd_scr[j:j + 1, :] = _colsum(dyv * xs)
                gb = (dyv * e).astype(BF16)
                dh_in = _dot_nt(gb, cb_)
                dc_acc = dc_acc + _dot_tn(hjb, gb)
                yoff = _dot(hjb, cb_) * e
                da = _colsum(dyv * yoff)
                dm = _dot_tn(dyb, xc.astype(BF16))
                dxc = _dot(dyb, mf.astype(BF16))
                dcb = dcb + dm * lam
                nmat = dm * mf
                rs = jnp.broadcast_to(_rowsum(nmat), (CHUNK, CHUNK))
                da = da + _colsum(jnp.where(eye, rs, 0.0)) - _colsum(nmat)
                ds = dh_scr[rows, :]
                dsb = ds.astype(BF16)
                t1 = _dot(dsb, bb)
                xcw = xc * w
                dxc = dxc + w * t1
                dww = _colsum(xcw * t1)
                da_l = _rowsum(dww) + _rowsum(_colsum(ds * hj)) * gam
                da = da - dww + jnp.where(last_lane, da_l, 0.0)
                db_acc = db_acc + _dot_tn(dsb, xcw.astype(BF16))
                dh_scr[rows, :] = gam * ds + dh_in
                dxs_ref[rows, :] = dsk[j:j + 1, :] * dyv + dxc * dtj
                da_scr[j:j + 1, :] = da
                ddt_scr[j:j + 1, :] = _colsum(dxc * xs)
            dcbb = dcb.astype(BF16)
            dc_ref[grows, :] = dc_acc + _dot_nt(bb, dcbb)
            db_ref[grows, :] = db_acc + _dot(cb_, dcbb)
        dda = jnp.dot(da_scr[...], causal.astype(F32), precision=HIGHEST, preferred_element_type=F32)
        ddt = ddt_scr[...] + dda * a_neg
        ddt_raw = ddt * _sigmoid(z)
        ddt_ref[...] = ddt_raw
        ddtb_ref[...] += _rowsum(ddt_raw)
        dalog_ref[...] += _rowsum(dda * dt) * a_neg
        ddsk_ref[...] += _rowsum(dd_scr[...])

    col = pl.BlockSpec((N_SSD_HEADS, 1), lambda c: (0, 0))
    bc = pl.BlockSpec((BC_DIM, CHUNK), lambda c: (0, rev(c)))
    xs_spec = pl.BlockSpec((D_INNER, CHUNK), lambda c: (0, rev(c)))
    small = pltpu.VMEM((N_SSD_HEADS, CHUNK), F32)
    return pl.pallas_call(
        body,
        name="ssd_bwd",
        grid=(nc,),
        in_specs=_ssd_specs(rev) + [pl.BlockSpec((1, D_INNER, D_STATE), lambda c: (rev(c), 0, 0)), xs_spec],
        out_specs=[xs_spec, bc, bc, pl.BlockSpec((N_SSD_HEADS, CHUNK), lambda c: (0, rev(c))), col, col, col],
        out_shape=[
            jax.ShapeDtypeStruct((D_INNER, t), F32),
            jax.ShapeDtypeStruct((BC_DIM, t), F32),
            jax.ShapeDtypeStruct((BC_DIM, t), F32),
            jax.ShapeDtypeStruct((N_SSD_HEADS, t), F32),
            jax.ShapeDtypeStruct((N_SSD_HEADS, 1), F32),
            jax.ShapeDtypeStruct((N_SSD_HEADS, 1), F32),
            jax.ShapeDtypeStruct((N_SSD_HEADS, 1), F32),
        ],
        scratch_shapes=[pltpu.VMEM((D_INNER, D_STATE), F32), small, small, small],
        compiler_params=_cparams("arbitrary"),
    )(xbc, xbc, xbc, proj, dtb_col, alog_col, dsk_col, hst, dy)


GN_ROWS = D_INNER // N_SSD_GROUPS


def _gnorm_fwd(y, proj, w_col):
    t = y.shape[1]
    tt = _tile(t, (512, 256, 128))
    z0 = OFF_Z // GN_ROWS

    def body(y_ref, z_ref, w_ref, o_ref):
        zv = z_ref[...]
        u = y_ref[...] * (zv * _sigmoid(zv))
        r = lax.rsqrt(jnp.mean(u * u, axis=0, keepdims=True) + EPS)
        o_ref[...] = (u * r * w_ref[...]).astype(BF16)

    blk = pl.BlockSpec((GN_ROWS, tt), lambda g, i: (g, i))
    return pl.pallas_call(
        body,
        name="gnorm_fwd",
        grid=(N_SSD_GROUPS, t // tt),
        in_specs=[blk, pl.BlockSpec((GN_ROWS, tt), lambda g, i: (z0 + g, i)), pl.BlockSpec((GN_ROWS, 1), lambda g, i: (g, 0))],
        out_specs=blk,
        out_shape=jax.ShapeDtypeStruct((D_INNER, t), BF16),
        compiler_params=_cparams("parallel", "parallel"),
    )(y, proj, w_col)


def _gnorm_bwd(dout, y, proj, w_col):
    t = y.shape[1]
    tt = _tile(t, (512, 256, 128))
    z0 = OFF_Z // GN_ROWS

    def body(do_ref, y_ref, z_ref, w_ref, dy_ref, dz_ref, dw_ref):
        @pl.when(pl.program_id(1) == 0)
        def _():
            dw_ref[...] = jnp.zeros_like(dw_ref)

        zv = z_ref[...]
        yv = y_ref[...]
        sg = _sigmoid(zv)
        sz = zv * sg
        u = yv * sz
        r = lax.rsqrt(jnp.mean(u * u, axis=0, keepdims=True) + EPS)
        xhat = u * r
        dov = do_ref[...]
        dw_ref[...] += _rowsum(dov * xhat)
        dxhat = dov * w_ref[...]
        du = r * (dxhat - xhat * jnp.mean(dxhat * xhat, axis=0, keepdims=True))
        dy_ref[...] = du * sz
        dz_ref[...] = (du * yv * (sg * (1.0 + zv * (1.0 - sg)))).astype(BF16)

    blk = pl.BlockSpec((GN_ROWS, tt), lambda g, i: (g, i))
    col = pl.BlockSpec((GN_ROWS, 1), lambda g, i: (g, 0))
    return pl.pallas_call(
        body,
        name="gnorm_bwd",
        grid=(N_SSD_GROUPS, t // tt),
        in_specs=[blk, blk, pl.BlockSpec((GN_ROWS, tt), lambda g, i: (z0 + g, i)), col],
        out_specs=[blk, blk, col],
        out_shape=[jax.ShapeDtypeStruct((D_INNER, t), F32), jax.ShapeDtypeStruct((D_INNER, t), BF16),
                   jax.ShapeDtypeStruct((D_INNER, 1), F32)],
        compiler_params=_cparams("parallel", "arbitrary"),
    )(dout, y, proj, w_col)


GATE_ROWS = 32


def _gate_specs(t):
    ga0 = OFF_GA // GATE_ROWS
    gs0 = OFF_GS // GATE_ROWS
    nr = D_MODEL // GATE_ROWS
    blk = pl.BlockSpec((GATE_ROWS, t), lambda r: (r, 0))
    return blk, [
        pl.BlockSpec((GATE_ROWS, t), lambda r: (ga0 + r, 0)),
        pl.BlockSpec((GATE_ROWS, t), lambda r: (gs0 + r, 0)),
        pl.BlockSpec((GATE_ROWS, 1), lambda r: (r, 0)),
        pl.BlockSpec((GATE_ROWS, 1), lambda r: (nr + r, 0)),
        blk, blk,
    ]


def _gate_fwd(proj, b_col, attn, ssd):
    t = proj.shape[1]
    blk, specs = _gate_specs(t)

    def body(ga_ref, gs_ref, ba_ref, bs_ref, a_ref, s_ref, o_ref):
        o_ref[...] = (_sigmoid(ga_ref[...] + ba_ref[...]) * a_ref[...]
                      + _sigmoid(gs_ref[...] + bs_ref[...]) * s_ref[...]).astype(BF16)

    return pl.pallas_call(
        body,
        name="gate_fwd",
        grid=(D_MODEL // GATE_ROWS,),
        in_specs=specs,
        out_specs=blk,
        out_shape=jax.ShapeDtypeStruct((D_MODEL, t), BF16),
        compiler_params=_cparams("parallel"),
    )(proj, proj, b_col, b_col, attn, ssd)


def _gate_bwd(proj, b_col, attn, ssd, dmix):
    t = proj.shape[1]
    blk, specs = _gate_specs(t)

    def body(ga_ref, gs_ref, ba_ref, bs_ref, a_ref, s_ref, dm_ref, da_ref, dso_ref, dga_ref, dgs_ref, dba_ref, dbs_ref):
        dm = dm_ref[...]
        sa = _sigmoid(ga_ref[...] + ba_ref[...])
        ss = _sigmoid(gs_ref[...] + bs_ref[...])
        da_ref[...] = (dm * sa).astype(BF16)
        dso_ref[...] = (dm * ss).astype(BF16)
        dga = dm * a_ref[...] * sa * (1.0 - sa)
        dgs = dm * s_ref[...] * ss * (1.0 - ss)
        dga_ref[...] = dga.astype(BF16)
        dgs_ref[...] = dgs.astype(BF16)
        dba_ref[...] = _rowsum(dga)
        dbs_ref[...] = _rowsum(dgs)

    col = pl.BlockSpec((GATE_ROWS, 1), lambda r: (r, 0))
    act = jax.ShapeDtypeStruct((D_MODEL, t), BF16)
    bias = jax.ShapeDtypeStruct((D_MODEL, 1), F32)
    return pl.pallas_call(
        body,
        name="gate_bwd",
        grid=(D_MODEL // GATE_ROWS,),
        in_specs=specs + [blk],
        out_specs=[blk, blk, blk, blk, col, col],
        out_shape=[act, act, act, act, bias, bias],
        compiler_params=_cparams("parallel"),
    )(proj, proj, b_col, b_col, attn, ssd, dmix)


FFN_ROWS = 256


def _ffn_fwd(u0, w_col, b_col):
    t = u0.shape[2]

    def body(u_ref, w_ref, b_ref, o_ref):
        def strip(rows):
            val, _ = _causal_conv(u_ref[0, rows, :], w_ref[0, rows, :], b_ref[0, rows, :])
            gt, _ = _causal_conv(u_ref[1, rows, :], w_ref[1, rows, :], b_ref[1, rows, :])
            o_ref[rows, :] = (gt * _sigmoid(gt) * val).astype(BF16)

        strip(slice(None))

    return pl.pallas_call(
        body,
        name="ffn_fwd",
        grid=(D_FF // FFN_ROWS,),
        in_specs=[
            pl.BlockSpec((2, FFN_ROWS, t), lambda i: (0, i, 0)),
            pl.BlockSpec((2, FFN_ROWS, FFN_CONV), lambda i: (0, i, 0)),
            pl.BlockSpec((2, FFN_ROWS, 1), lambda i: (0, i, 0)),
        ],
        out_specs=pl.BlockSpec((FFN_ROWS, t), lambda i: (i, 0)),
        out_shape=jax.ShapeDtypeStruct((D_FF, t), BF16),
        compiler_params=_cparams("parallel"),
    )(u0, w_col, b_col)


def _ffn_bwd(u0, w_col, b_col, dg):
    t = u0.shape[2]

    def body(u_ref, w_ref, b_ref, dg_ref, du_ref, dwb_ref):
        def strip(rows):
            xval, wval = u_ref[0, rows, :], w_ref[0, rows, :]
            xgt, wgt = u_ref[1, rows, :], w_ref[1, rows, :]
            val, sh_val = _causal_conv(xval, wval, b_ref[0, rows, :])
            gt, sh_gt = _causal_conv(xgt, wgt, b_ref[1, rows, :])
            sg = _sigmoid(gt)
            dgv = dg_ref[rows, :]
            dval = dgv * (gt * sg)
            dgt = dgv * val * (sg * (1.0 + gt * (1.0 - sg)))
            dx, dwb_ref[0, rows, :] = _causal_conv_bwd(dval, xval, sh_val, wval)
            du_ref[0, rows, :] = dx.astype(BF16)
            dx, dwb_ref[1, rows, :] = _causal_conv_bwd(dgt, xgt, sh_gt, wgt)
            du_ref[1, rows, :] = dx.astype(BF16)

        strip(slice(None))

    return pl.pallas_call(
        body,
        name="ffn_bwd",
        grid=(D_FF // FFN_ROWS,),
        in_specs=[
            pl.BlockSpec((2, FFN_ROWS, t), lambda i: (0, i, 0)),
            pl.BlockSpec((2, FFN_ROWS, FFN_CONV), lambda i: (0, i, 0)),
            pl.BlockSpec((2, FFN_ROWS, 1), lambda i: (0, i, 0)),
            pl.BlockSpec((FFN_ROWS, t), lambda i: (i, 0)),
        ],
        out_specs=[pl.BlockSpec((2, FFN_ROWS, t), lambda i: (0, i, 0)), pl.BlockSpec((2, FFN_ROWS, 128), lambda i: (0, i, 0))],
        out_shape=[jax.ShapeDtypeStruct((2, D_FF, t), BF16), jax.ShapeDtypeStruct((2, D_FF, 128), F32)],
        compiler_params=_cparams("parallel"),
    )(u0, w_col, b_col, dg)


def _adamw_math(w, g, m, v):
    m = ADAM_B1 * m + (1.0 - ADAM_B1) * g
    v = ADAM_B2 * v + (1.0 - ADAM_B2) * (g * g)
    m_hat = m / (1.0 - ADAM_B1 ** ADAM_STEP)
    v_hat = v / (1.0 - ADAM_B2 ** ADAM_STEP)
    delta = -ADAM_LR * (m_hat / (jnp.sqrt(v_hat) + ADAM_EPS) + ADAM_WD * w)
    return delta, m, v


def _adamw_sharded(parts, w, m, v, name):
    r, c = w.shape[0], w.shape[-1]
    tc = _tile(c, (256, 128))
    blk_shape = (r, tc) if w.ndim == 2 else (r, 1, tc)

    def body(p_ref, w_ref, m_ref, v_ref, g_ref, d_ref, nm_ref, nv_ref):
        g = p_ref[0].astype(F32)
        for s in range(1, N_DEV):
            g = g + p_ref[s].astype(F32)
        flat = lambda ref: ref[...].reshape(r, tc)
        d, nm, nv = _adamw_math(flat(w_ref), g, flat(m_ref), flat(v_ref))
        for ref, val in ((g_ref, g), (d_ref, d), (nm_ref, nm), (nv_ref, nv)):
            ref[...] = val.reshape(blk_shape)

    blk = pl.BlockSpec(blk_shape, (lambda i: (0, i)) if w.ndim == 2 else (lambda i: (0, 0, i)))
    out = jax.ShapeDtypeStruct(w.shape, F32)
    return pl.pallas_call(
        body,
        name=name,
        grid=(c // tc,),
        in_specs=[pl.BlockSpec((N_DEV, r, tc), lambda i: (0, 0, i)), blk, blk, blk],
        out_specs=[blk, blk, blk, blk],
        out_shape=[out, out, out, out],
        compiler_params=_cparams("parallel"),
    )(parts, w, m, v)


def _sum_slots(parts):
    _, r, c = parts.shape

    def body(p_ref, o_ref):
        g = p_ref[0]
        for s in range(1, N_DEV):
            g = g + p_ref[s]
        o_ref[...] = g

    return pl.pallas_call(body, name="sum_small_grads", out_shape=jax.ShapeDtypeStruct((r, c), F32))(parts)


def _adamw_small(g, w, m, v):
    def body(g_ref, w_ref, m_ref, v_ref, d_ref, nm_ref, nv_ref):
        d_ref[...], nm_ref[...], nv_ref[...] = _adamw_math(w_ref[...], g_ref[...], m_ref[...], v_ref[...])

    out = jax.ShapeDtypeStruct(g.shape, F32)
    return pl.pallas_call(body, name="adamw_small", out_shape=[out, out, out])(g, w, m, v)


ANY = pl.BlockSpec(memory_space=pl.ANY)
FLIPS = [(k >> 2 & 1, k >> 1 & 1, k & 1) for k in range(1, N_DEV)]


def _place():
    return lax.axis_index("x"), lax.axis_index("y"), lax.axis_index("c")


HBM = pl.BlockSpec(memory_space=pltpu.HBM)
SEM = pl.BlockSpec(memory_space=pltpu.SEMAPHORE)
EFFECT = pltpu.SideEffectType.DATAFLOW_SIDE_EFFECTING


def _peer_copy(gather, src_ref, land_ref, send_sems, recv_sems, k, sending):
    x, y, c = _place()
    fx, fy, fc = FLIPS[k]
    me = 4 * x + 2 * y + c
    peer = 4 * (x ^ fx) + 2 * (y ^ fy) + (c ^ fc)
    return pltpu.make_async_remote_copy(
        src_ref=src_ref if gather else src_ref.at[peer],
        dst_ref=land_ref.at[me if sending else peer],
        send_sem=send_sems.at[k], recv_sem=recv_sems.at[k],
        device_id=(x ^ fx, y ^ fy, c ^ fc), device_id_type=MESH)


def _gather_start(srcs, name):
    n = len(srcs)
    lands = [lax.empty((N_DEV,) + s.shape, s.dtype) for s in srcs]

    def body(*refs):
        src_refs, land_refs = refs[:n], refs[n:2 * n]
        send, recv = refs[2 * n:3 * n], refs[3 * n:4 * n]
        local_sems = refs[6 * n]
        x, y, c = _place()
        me = 4 * x + 2 * y + c
        local = [pltpu.make_async_copy(src_refs[i], land_refs[i].at[me], local_sems.at[i]) for i in range(n)]
        for cp in local:
            cp.start()
        for i in range(n):
            for k in range(N_DEV - 1):
                _peer_copy(True, src_refs[i], land_refs[i], send[i], recv[i], k, True).start()
        for cp in local:
            cp.wait()

    sem = pltpu.SemaphoreType.DMA((N_DEV - 1,))
    hbm = lambda a: pltpu.HBM(a.shape, a.dtype)
    res = pl.pallas_call(
        body,
        name=name,
        in_specs=[HBM] * (2 * n),
        out_specs=[SEM] * (2 * n) + [HBM] * (2 * n),
        out_shape=[sem] * (2 * n) + [hbm(s) for s in srcs] + [hbm(a) for a in lands],
        input_output_aliases={i: 2 * n + i for i in range(2 * n)},
        scratch_shapes=[pltpu.SemaphoreType.DMA((n,))],
        compiler_params=pltpu.CompilerParams(has_side_effects=EFFECT),
    )(*[pltpu.with_memory_space_constraint(a, pltpu.HBM) for a in list(srcs) + lands])
    return res[:n], res[n:2 * n], res[2 * n:3 * n], res[3 * n:4 * n]


def _exchange_wait(send_sems, recv_sems, src, land, after, gather, name):
    def body(src_ref, land_ref, send_ref, recv_ref, after_ref, src_out, land_out):
        for k in range(N_DEV - 1):
            cp = _peer_copy(gather, src_ref, land_ref, send_ref, recv_ref, k, False)
            cp.wait_send()
            cp.wait_recv()

    hbm = lambda a: pltpu.HBM(a.shape, a.dtype)
    return pl.pallas_call(
        body,
        name=name,
        in_specs=[HBM, HBM, SEM, SEM, ANY],
        out_specs=[HBM, HBM],
        out_shape=[hbm(src), hbm(land)],
        input_output_aliases={0: 0, 1: 1},
        compiler_params=pltpu.CompilerParams(has_side_effects=EFFECT),
    )(src, land, send_sems, recv_sems, after)[1]


def _col(v):
    return v.reshape(-1, 1).astype(F32)


def _local_step(xt, tgt, weight, small):
    t = xt.shape[1]
    n1 = _col(small["norm1_w"])
    n2 = _col(small["norm2_w"])
    nf = _col(small["final_norm_w"])
    bg = _col(small["b_gate"])
    sinks = small["attn_sinks"].reshape(-1).astype(F32)
    cbias = _col(small["ssd_conv_b"])
    dtb = _col(small["dt_bias"])
    alog = _col(small["a_log"])
    dsk = _col(small["d_skip"])
    gnw = _col(small["ssd_norm_w"])
    fb = small["ffn_conv_b"].reshape(2, D_FF, 1)

    xn = _norm_fwd(xt, n1, "norm1_fwd")
    cw = weight("ssd_conv_w", xn).T
    fw = weight("ffn_conv_w", xn).T.reshape(2, D_FF, FFN_CONV)
    w_in_t = weight("w_in", xn)
    proj = _matmul(w_in_t, xn, nt=False, out_dtype=F32, name="mm_in")
    ao, lse = _attn_fwd(proj, sinks)
    w_ao = weight("w_attn_o", ao)
    attn = _matmul(w_ao, ao, nt=False, out_dtype=F32, name="mm_attn_o", tn_a=True)
    xbc = _conv_silu_fwd(proj, cw, cbias)
    y, hst = _ssd_fwd(xbc, proj, dtb, alog, dsk)
    yn = _gnorm_fwd(y, proj, gnw)
    w_so = weight("w_ssd_o", yn)
    ssd = _matmul(w_so, yn, nt=False, out_dtype=F32, name="mm_ssd_o", tn_a=True)
    mix = _gate_fwd(proj, bg, attn, ssd)
    w_out = weight("w_out", mix)
    h1 = _matmul(w_out, mix, nt=False, out_dtype=F32, name="mm_out", add=xt, tn_a=True)
    hn = _norm_fwd(h1, n2, "norm2_fwd")
    w_up_t = weight("w_up", hn)
    u0 = _matmul(w_up_t, hn, nt=False, out_dtype=F32, name="mm_up").reshape(2, D_FF, t)
    gl = _ffn_fwd(u0, fw, fb)
    w_down = weight("w_down", gl)
    h2 = _matmul(w_down, gl, nt=False, out_dtype=F32, name="mm_down", add=h1, tn_a=True)
    dh2, loss, d_nf = _final_norm_loss(h2, tgt, nf)

    g = {}
    handles = {}

    def sending(weight_name, grad, *args, **kwargs):
        out, handles[weight_name] = _matmul(*args, send=grad.reshape(N_DEV, -1, D_MODEL), **kwargs)
        return out

    g_down = _matmul(gl, dh2, nt=True, out_dtype=BF16, name="mm_d_w_down")
    dgl = sending("w_down", g_down, w_down, dh2, nt=False, out_dtype=F32, name="mm_d_glu")
    du0, d_fwb = _ffn_bwd(u0, fw, fb, dgl)
    du0 = du0.reshape(2 * D_FF, t)
    g_up = _matmul(du0, hn, nt=True, out_dtype=BF16, name="mm_d_w_up")
    dhn = sending("w_up", g_up, w_up_t, du0, nt=False, out_dtype=F32, name="mm_d_hn", tn_a=True)
    dh1, d_n2 = _norm_bwd(dhn, h1, n2, dh2, "norm2_bwd")
    g_out = _matmul(mix, dh1, nt=True, out_dtype=BF16, name="mm_d_w_out")
    dmix = sending("w_out", g_out, w_out, dh1, nt=False, out_dtype=F32, name="mm_d_mix")
    d_attn, d_ssd, d_ga, d_gs, d_ba, d_bs = _gate_bwd(proj, bg, attn, ssd, dmix)
    g_ao = _matmul(ao, d_attn, nt=True, out_dtype=BF16, name="mm_d_w_attn_o")
    dao = sending("w_attn_o", g_ao, w_ao, d_attn, nt=False, out_dtype=F32, name="mm_d_ao")
    dq, dk, dv, d_sinks = _attn_bwd(proj, sinks, ao, lse, dao)
    g_so = _matmul(yn, d_ssd, nt=True, out_dtype=BF16, name="mm_d_w_ssd_o")
    dyn = sending("w_ssd_o", g_so, w_so, d_ssd, nt=False, out_dtype=F32, name="mm_d_yn")
    dy, dz, d_gnw = _gnorm_bwd(dyn, y, proj, gnw)
    dxs, dbm, dcm, ddt, d_alog, d_dsk, d_dtb = _ssd_bwd(xbc, proj, dtb, alog, dsk, hst, dy)
    dx_xs, dwb_xs = _conv_silu_bwd(proj, cw, cbias, dxs, 0, "ssd_conv_bwd_x")
    dx_b, dwb_b = _conv_silu_bwd(proj, cw, cbias, dbm, D_INNER, "ssd_conv_bwd_b")
    dx_c, dwb_c = _conv_silu_bwd(proj, cw, cbias, dcm, D_INNER + BC_DIM, "ssd_conv_bwd_c")
    dwb_conv = jnp.concatenate([dwb_xs, dwb_b, dwb_c], axis=0)
    dproj = jnp.concatenate([dq, dk, dv, dz, dx_xs, dx_b, dx_c, ddt.astype(BF16), d_ga, d_gs], axis=0)
    g_in = _matmul(dproj, xn, nt=True, out_dtype=BF16, name="mm_d_w_in")
    dxn = sending("w_in", g_in, w_in_t, dproj, nt=False, out_dtype=F32, name="mm_d_xn", tn_a=True)
    dx, d_n1 = _norm_bwd(dxn, xt, n1, dh1, "norm1_bwd")

    g["norm1_w"] = d_n1
    g["b_gate"] = jnp.concatenate([d_ba, d_bs], axis=0)
    g["attn_sinks"] = d_sinks
    g["ssd_conv_w"] = dwb_conv[:, :SSD_CONV].T
    g["ssd_conv_b"] = dwb_conv[:, SSD_CONV]
    g["dt_bias"] = d_dtb
    g["a_log"] = d_alog
    g["d_skip"] = d_dsk
    g["ssd_norm_w"] = d_gnw
    g["norm2_w"] = d_n2
    d_fwb = d_fwb.reshape(2 * D_FF, 128)
    g["ffn_conv_w"] = d_fwb[:, :FFN_CONV].T
    g["ffn_conv_b"] = d_fwb[:, FFN_CONV]
    g["final_norm_w"] = d_nf
    return loss, dx, g, handles


SMALL = ("norm1_w", "b_gate", "attn_sinks", "ssd_conv_w", "ssd_conv_b", "dt_bias", "a_log", "d_skip", "ssd_norm_w",
         "norm2_w", "ffn_conv_w", "ffn_conv_b", "final_norm_w")
SMALL_SHAPES = {"norm1_w": (1, D_MODEL), "b_gate": (1, 2 * D_MODEL), "attn_sinks": (1, N_Q_HEADS),
                "ssd_conv_w": (1, SSD_CONV, XBC_DIM), "ssd_conv_b": (1, XBC_DIM), "dt_bias": (1, N_SSD_HEADS),
                "a_log": (1, N_SSD_HEADS), "d_skip": (1, N_SSD_HEADS), "ssd_norm_w": (1, D_INNER),
                "norm2_w": (1, D_MODEL), "ffn_conv_w": (1, FFN_CONV, 2 * D_FF), "ffn_conv_b": (1, 2 * D_FF),
                "final_norm_w": (D_MODEL,)}
WEIGHT_ORDER = ("norm1_w", "w_in", "b_gate", "attn_sinks", "w_attn_o", "ssd_conv_w", "ssd_conv_b", "dt_bias", "a_log",
                "d_skip", "ssd_norm_w", "w_ssd_o", "w_out", "norm2_w", "w_up", "ffn_conv_w", "ffn_conv_b", "w_down",
                "final_norm_w")


def _pack(parts):
    flat = jnp.concatenate([p.reshape(-1).astype(F32) for p in parts])
    rows = -(-flat.shape[0] // 1024) * 8
    return jnp.pad(flat, (0, rows * 128 - flat.shape[0])).reshape(rows, 128)


def _unpack(packed, shapes):
    flat = packed.reshape(-1)
    out, pos = [], 0
    for shp in shapes:
        size = 1
        for d in shp:
            size *= d
        out.append(flat[pos:pos + size].reshape(shp))
        pos += size
    return out


def kernel(x, norm1_w, w_in, b_gate, attn_sinks, w_attn_o, ssd_conv_w, ssd_conv_b, dt_bias, a_log, d_skip, ssd_norm_w, w_ssd_o, w_out, norm2_w, w_up, ffn_conv_w, ffn_conv_b, w_down, final_norm_w, loss_target, m_norm1_w, m_w_in, m_b_gate, m_attn_sinks, m_w_attn_o, m_ssd_conv_w, m_ssd_conv_b, m_dt_bias, m_a_log, m_d_skip, m_ssd_norm_w, m_w_ssd_o, m_w_out, m_norm2_w, m_w_up, m_ffn_conv_w, m_ffn_conv_b, m_w_down, m_final_norm_w, v_norm1_w, v_w_in, v_b_gate, v_attn_sinks, v_w_attn_o, v_ssd_conv_w, v_ssd_conv_b, v_dt_bias, v_a_log, v_d_skip, v_ssd_norm_w, v_w_ssd_o, v_w_out, v_norm2_w, v_w_up, v_ffn_conv_w, v_ffn_conv_b, v_w_down, v_final_norm_w):
    w = dict(norm1_w=norm1_w, w_in=w_in, b_gate=b_gate, attn_sinks=attn_sinks, w_attn_o=w_attn_o, ssd_conv_w=ssd_conv_w, ssd_conv_b=ssd_conv_b, dt_bias=dt_bias, a_log=a_log, d_skip=d_skip, ssd_norm_w=ssd_norm_w, w_ssd_o=w_ssd_o, w_out=w_out, norm2_w=norm2_w, w_up=w_up, ffn_conv_w=ffn_conv_w, ffn_conv_b=ffn_conv_b, w_down=w_down, final_norm_w=final_norm_w)
    m = dict(norm1_w=m_norm1_w, w_in=m_w_in, b_gate=m_b_gate, attn_sinks=m_attn_sinks, w_attn_o=m_w_attn_o, ssd_conv_w=m_ssd_conv_w, ssd_conv_b=m_ssd_conv_b, dt_bias=m_dt_bias, a_log=m_a_log, d_skip=m_d_skip, ssd_norm_w=m_ssd_norm_w, w_ssd_o=m_w_ssd_o, w_out=m_w_out, norm2_w=m_norm2_w, w_up=m_w_up, ffn_conv_w=m_ffn_conv_w, ffn_conv_b=m_ffn_conv_b, w_down=m_w_down, final_norm_w=m_final_norm_w)
    v = dict(norm1_w=v_norm1_w, w_in=v_w_in, b_gate=v_b_gate, attn_sinks=v_attn_sinks, w_attn_o=v_w_attn_o, ssd_conv_w=v_ssd_conv_w, ssd_conv_b=v_ssd_conv_b, dt_bias=v_dt_bias, a_log=v_a_log, d_skip=v_d_skip, ssd_norm_w=v_ssd_norm_w, w_ssd_o=v_w_ssd_o, w_out=v_w_out, norm2_w=v_norm2_w, w_up=v_w_up, ffn_conv_w=v_ffn_conv_w, ffn_conv_b=v_ffn_conv_b, w_down=v_w_down, final_norm_w=v_final_norm_w)
    me = 4 * lax.axis_index("x") + 2 * lax.axis_index("y") + lax.axis_index("c")
    conv_cols = XBC_DIM // N_DEV
    ffn_cols = 2 * D_FF // N_DEV

    shards = {"ssd_conv_w": ssd_conv_w[0], "ffn_conv_w": ffn_conv_w[0], "w_in": w_in[0].T.astype(BF16),
              "w_attn_o": w_attn_o[0].astype(BF16), "w_ssd_o": w_ssd_o[0].astype(BF16), "w_out": w_out[0].astype(BF16),
              "w_up": w_up[0].T.astype(BF16), "w_down": w_down[0].astype(BF16)}
    order = list(shards)
    g_send, g_recv, g_src, g_land = _gather_start(list(shards.values()), "gather_start")

    def weight(name, after):
        i = order.index(name)
        land = _exchange_wait(g_send[i], g_recv[i], g_src[i], g_land[i], after, True, "gather_wait_" + name)
        if name == "ssd_conv_w":
            return jnp.transpose(land, (1, 0, 2)).reshape(SSD_CONV, XBC_DIM)
        if name == "ffn_conv_w":
            return jnp.transpose(land, (1, 0, 2)).reshape(FFN_CONV, 2 * D_FF)
        return land.reshape(-1, D_MODEL)

    small = {k: w[k][0] if k != "final_norm_w" else w[k] for k in SMALL}
    loss, dx, g, pending = _local_step(x[0].T, loss_target[0].T, weight, small)

    packed = _pack([loss] + [g[k] for k in SMALL])
    s_send, s_recv, s_src, s_land = _gather_start([packed], "small_grads_start")

    res = {}
    after = s_src[0]
    for name in ("w_down", "w_up", "w_out", "w_attn_o", "w_ssd_o", "w_in"):
        parts = _exchange_wait(*pending[name], after, False, "grad_wait_" + name)
        view, back = {
            "w_in": (lambda a: jnp.transpose(a, (2, 0, 1)), lambda r: jnp.transpose(r, (1, 2, 0))),
            "w_up": (lambda a: a[0].T, lambda r: r.T[None]),
        }.get(name, (lambda a: a[0], lambda r: r[None]))
        res[name] = _adamw_sharded(parts, view(w[name]), view(m[name]), view(v[name]), "adamw_" + name)
        after = res[name][0]
        res[name] = [back(r) for r in res[name]]

    total = _sum_slots(_exchange_wait(s_send[0], s_recv[0], s_src[0], s_land[0], after, True, "small_grads_wait"))
    tot = _unpack(total, [(1,)] + [SMALL_SHAPES[k] for k in SMALL])
    loss_sum = tot[0].reshape(())
    gs = dict(zip(SMALL, tot[1:]))
    gs["ssd_conv_w"] = lax.dynamic_slice_in_dim(gs["ssd_conv_w"], me * conv_cols, conv_cols, axis=2)
    gs["ffn_conv_w"] = lax.dynamic_slice_in_dim(gs["ffn_conv_w"], me * ffn_cols, ffn_cols, axis=2)
    upd = _adamw_small(_pack([gs[k] for k in SMALL]), _pack([w[k] for k in SMALL]), _pack([m[k] for k in SMALL]),
                       _pack([v[k] for k in SMALL]))
    shapes = [w[k].shape for k in SMALL]
    d_s, m_s, v_s = (dict(zip(SMALL, _unpack(u, shapes))) for u in upd)
    for k in SMALL:
        res[k] = (gs[k], d_s[k], m_s[k], v_s[k])

    grad_x = dx.T[None]
    outs = [loss_sum, grad_x]
    for i in range(4):
        outs.extend(res[k][i] for k in WEIGHT_ORDER)
    return tuple(outs)
```

```python
import functools

import jax
import jax.numpy as jnp
from jax import lax
from jax.experimental import pallas as pl
from jax.experimental.pallas import tpu as pltpu

F32 = jnp.float32
BF16 = jnp.bfloat16
HIGHEST = lax.Precision.HIGHEST

D_MODEL = 1024
N_Q_HEADS = 16
N_KV_HEADS = 4
HEAD_DIM = 64
WINDOW = 128
Q_PER_KV = N_Q_HEADS // N_KV_HEADS
Q_DIM = N_Q_HEADS * HEAD_DIM
KV_DIM = N_KV_HEADS * HEAD_DIM
D_INNER = 2048
SSD_HEAD_DIM = 64
N_SSD_HEADS = 32
N_SSD_GROUPS = 4
HEADS_PER_GROUP = N_SSD_HEADS // N_SSD_GROUPS
D_STATE = 128
BC_DIM = N_SSD_GROUPS * D_STATE
XBC_DIM = D_INNER + 2 * BC_DIM
SSD_CONV = 4
CHUNK = 128
D_FF = 2816
FFN_CONV = 3
EPS = 1e-5
NEG = -1e30
IN_DIM = 8736
N_DEV = 8

OFF_Q = 0
OFF_K = OFF_Q + Q_DIM
OFF_V = OFF_K + KV_DIM
OFF_Z = OFF_V + KV_DIM
OFF_X = OFF_Z + D_INNER
OFF_DT = OFF_X + XBC_DIM
OFF_GA = OFF_DT + N_SSD_HEADS
OFF_GS = OFF_GA + D_MODEL

ADAM_LR = 0.001
ADAM_B1 = 0.9
ADAM_B2 = 0.999
ADAM_EPS = 1e-08
ADAM_WD = 0.01
ADAM_STEP = 10

VMEM_LIMIT = 48 * 1024 * 1024
MESH = pl.DeviceIdType.MESH


def _cparams(*sem):
    return pltpu.CompilerParams(dimension_semantics=sem, vmem_limit_bytes=VMEM_LIMIT)


def _tile(n, prefs):
    for p in prefs:
        if n % p == 0:
            return p
    return n


def _sigmoid(x):
    return 1.0 / (1.0 + jnp.exp(-x))


def _softplus(x):
    return jnp.maximum(x, 0.0) + jnp.log(1.0 + jnp.exp(-jnp.abs(x)))


def _rowsum(x):
    return jnp.sum(x, axis=1, keepdims=True)


def _colsum(x):
    return jnp.sum(x, axis=0, keepdims=True)


def _dot(a, b):
    return jnp.dot(a, b, preferred_element_type=F32)


def _dot_nt(a, b):
    return lax.dot_general(a, b, (((1,), (1,)), ((), ())), preferred_element_type=F32)


def _dot_tn(a, b):
    return lax.dot_general(a, b, (((0,), (0,)), ((), ())), preferred_element_type=F32)


def _shift_right(x, j):
    if j == 0:
        return x
    r = pltpu.roll(x, j, 1)
    lane = lax.broadcasted_iota(jnp.int32, (x.shape[0], 128), 1)
    return jnp.concatenate([jnp.where(lane >= j, r[:, :128], 0.0), r[:, 128:]], axis=1)


def _shift_left(x, j):
    if j == 0:
        return x
    n = x.shape[1]
    r = pltpu.roll(x, n - j, 1)
    lane = lax.broadcasted_iota(jnp.int32, (x.shape[0], 128), 1)
    return jnp.concatenate([r[:, :n - 128], jnp.where(lane < 128 - j, r[:, n - 128:], 0.0)], axis=1)


def _causal_conv(xv, wv, bv):
    taps = wv.shape[1]
    shifted = [_shift_right(xv, taps - 1 - k) for k in range(taps - 1)]
    y = bv + wv[:, taps - 1:taps] * xv
    for k in range(taps - 1):
        y = y + wv[:, k:k + 1] * shifted[k]
    return y, shifted


def _causal_conv_bwd(dy, xv, shifted, wv):
    taps = wv.shape[1]
    lane = lax.broadcasted_iota(jnp.int32, (dy.shape[0], 128), 1)
    dwb = jnp.where(lane == taps, _rowsum(dy), 0.0)
    dwb = jnp.where(lane == taps - 1, _rowsum(dy * xv), dwb)
    dx = wv[:, taps - 1:taps] * dy
    for k in range(taps - 1):
        dx = dx + wv[:, k:k + 1] * _shift_left(dy, taps - 1 - k)
        dwb = jnp.where(lane == k, _rowsum(dy * shifted[k]), dwb)
    return dx, dwb


def _call(body, *, name, grid, in_specs, out_specs, out_shape, args, semantics, scratch_shapes=(), send=None):
    if send is None:
        return pl.pallas_call(body, name=name, grid=grid, in_specs=in_specs, out_specs=out_specs, out_shape=out_shape,
                              scratch_shapes=list(scratch_shapes), compiler_params=_cparams(*semantics))(*args)
    single = not isinstance(out_specs, (list, tuple))
    out_specs, out_shape = ([out_specs], [out_shape]) if single else (list(out_specs), list(out_shape))
    n_in, n_out, n_scr = len(in_specs), len(out_specs), len(scratch_shapes)
    steps = 1
    for size in grid:
        steps *= size

    def sending(*refs):
        ins, (src_ref, land_ref) = refs[:n_in], refs[n_in:n_in + 2]
        outs = refs[n_in + 2:n_in + 2 + n_out]
        send_sems, recv_sems = refs[n_in + 2 + n_out:n_in + 4 + n_out]
        scratch, local_sem = refs[n_in + 6 + n_out:n_in + 6 + n_out + n_scr], refs[-1]
        x, y, c = _place()
        me = 4 * x + 2 * y + c
        local = pltpu.make_async_copy(src_ref.at[me], land_ref.at[me], local_sem)
        step = 0
        for axis, size in enumerate(grid):
            step = step * size + pl.program_id(axis)

        @pl.when(step == 0)
        def _():
            local.start()
            for peer in range(N_DEV - 1):
                _peer_copy(False, src_ref, land_ref, send_sems, recv_sems, peer, True).start()

        body(*ins, *outs, *scratch)

        @pl.when(step == steps - 1)
        def _():
            local.wait()

    sem = pltpu.SemaphoreType.DMA((N_DEV - 1,))
    hbm = pltpu.HBM(send.shape, send.dtype)
    res = pl.pallas_call(
        sending, name=name, grid=grid,
        in_specs=list(in_specs) + [HBM, HBM],
        out_specs=out_specs + [SEM, SEM, HBM, HBM],
        out_shape=out_shape + [sem, sem, hbm, hbm],
        input_output_aliases={n_in: n_out + 2, n_in + 1: n_out + 3},
        scratch_shapes=list(scratch_shapes) + [pltpu.SemaphoreType.DMA(())],
        compiler_params=pltpu.CompilerParams(dimension_semantics=("arbitrary",) * len(grid), vmem_limit_bytes=VMEM_LIMIT,
                                             has_side_effects=EFFECT),
    )(*args, pltpu.with_memory_space_constraint(send, pltpu.HBM),
      pltpu.with_memory_space_constraint(lax.empty(send.shape, send.dtype), pltpu.HBM))
    return (res[0] if single else list(res[:n_out])), tuple(res[n_out:])


MATMUL_VMEM_BUDGET = 36 * 1024 * 1024
MATMUL_MAX_TK = 3072


MATMUL_MAX_TM = 768


def _largest_tile(n, align, cap):
    return max(d for d in range(align, min(n, cap) + 1, align) if n % d == 0)


def _matmul_tiles(m, n, k, a_bytes, b_bytes, out_bytes, has_add, m_align, k_align):
    tm = _largest_tile(m, m_align, MATMUL_MAX_TM)
    tk = _largest_tile(k, k_align, MATMUL_MAX_TK)
    for tn in sorted({d for d in range(128, n + 1, 128) if n % d == 0}, reverse=True):
        need = 2 * (tm * tk * a_bytes + tk * tn * b_bytes) + tm * tn * (2 * out_bytes + (4 if k > tk else 0) + (8 if has_add else 0))
        if tn <= 3072 and need <= MATMUL_VMEM_BUDGET:
            return tm, tn, tk
    return tm, 128, tk


def _matmul(a, b, *, nt, out_dtype, name, add=None, tn_a=False, send=None):
    if tn_a:
        k, m = a.shape
    else:
        m, k = a.shape
    n = b.shape[0] if nt else b.shape[1]
    tm, tn, tk = _matmul_tiles(m, n, k, a.dtype.itemsize, b.dtype.itemsize, jnp.dtype(out_dtype).itemsize, add is not None,
                               128 if tn_a else 16, 16 if tn_a and not nt else 128)
    nk = k // tk
    grid = (m // tm, n // tn, nk)

    def body(a_ref, b_ref, *rest):
        r_ref = None
        if add is not None:
            r_ref, rest = rest[0], rest[1:]
        o_ref = rest[0]
        av = a_ref[...].astype(BF16)
        bv = b_ref[...].astype(BF16)
        part = _dot_tn(av, bv) if tn_a else _dot_nt(av, bv) if nt else _dot(av, bv)

        def finish(r):
            if add is not None:
                r = r + r_ref[...]
            o_ref[...] = r.astype(out_dtype)

        if nk == 1:
            finish(part)
            return
        acc = rest[1]
        kk = pl.program_id(2)

        @pl.when(kk == 0)
        def _():
            acc[...] = part

        @pl.when((kk > 0) & (kk < nk - 1))
        def _():
            acc[...] += part

        @pl.when(kk == nk - 1)
        def _():
            finish(acc[...] + part)

    in_specs = [
        pl.BlockSpec((tk, tm), lambda i, j, kk: (kk, i)) if tn_a else pl.BlockSpec((tm, tk), lambda i, j, kk: (i, kk)),
        pl.BlockSpec((tn, tk), lambda i, j, kk: (j, kk)) if nt else pl.BlockSpec((tk, tn), lambda i, j, kk: (kk, j)),
    ]
    args = [a, b]
    if add is not None:
        in_specs.append(pl.BlockSpec((tm, tn), lambda i, j, kk: (i, j)))
        args.append(add)
    return _call(
        body, name=name, grid=grid, in_specs=in_specs, args=args,
        out_specs=pl.BlockSpec((tm, tn), lambda i, j, kk: (i, j)),
        out_shape=jax.ShapeDtypeStruct((m, n), out_dtype),
        scratch_shapes=[pltpu.VMEM((tm, tn), F32)] if nk > 1 else [],
        semantics=("parallel", "parallel", "arbitrary"), send=send)


def _norm_fwd(x, w_col, name):
    f, t = x.shape
    tt = _tile(t, (512, 256, 128))

    def body(x_ref, w_ref, o_ref):
        xv = x_ref[...]
        r = lax.rsqrt(jnp.mean(xv * xv, axis=0, keepdims=True) + EPS)
        o_ref[...] = (xv * r * w_ref[...]).astype(BF16)

    return pl.pallas_call(
        body,
        name=name,
        grid=(t // tt,),
        in_specs=[pl.BlockSpec((f, tt), lambda i: (0, i)), pl.BlockSpec((f, 1), lambda i: (0, 0))],
        out_specs=pl.BlockSpec((f, tt), lambda i: (0, i)),
        out_shape=jax.ShapeDtypeStruct((f, t), BF16),
        compiler_params=_cparams("parallel"),
    )(x, w_col)


def _norm_bwd(dy, x, w_col, res, name):
    f, t = x.shape
    tt = _tile(t, (512, 256, 128))

    def body(dy_ref, x_ref, w_ref, res_ref, dx_ref, dw_ref):
        @pl.when(pl.program_id(0) == 0)
        def _():
            dw_ref[...] = jnp.zeros_like(dw_ref)

        xv = x_ref[...]
        r = lax.rsqrt(jnp.mean(xv * xv, axis=0, keepdims=True) + EPS)
        xhat = xv * r
        dyv = dy_ref[...]
        dw_ref[...] += _rowsum(dyv * xhat)
        dxhat = dyv * w_ref[...]
        dx_ref[...] = res_ref[...] + r * (dxhat - xhat * jnp.mean(dxhat * xhat, axis=0, keepdims=True))

    blk = pl.BlockSpec((f, tt), lambda i: (0, i))
    col = pl.BlockSpec((f, 1), lambda i: (0, 0))
    return pl.pallas_call(
        body,
        name=name,
        grid=(t // tt,),
        in_specs=[blk, blk, col, blk],
        out_specs=[blk, col],
        out_shape=[jax.ShapeDtypeStruct((f, t), F32), jax.ShapeDtypeStruct((f, 1), F32)],
        compiler_params=_cparams("arbitrary"),
    )(dy, x, w_col, res)


def _final_norm_loss(h, tgt, w_col):
    f, t = h.shape
    tt = _tile(t, (512, 256, 128))

    def body(h_ref, t_ref, w_ref, dh_ref, loss_ref, dw_ref):
        @pl.when(pl.program_id(0) == 0)
        def _():
            dw_ref[...] = jnp.zeros_like(dw_ref)
            loss_ref[...] = jnp.zeros_like(loss_ref)

        xv = h_ref[...]
        r = lax.rsqrt(jnp.mean(xv * xv, axis=0, keepdims=True) + EPS)
        xhat = xv * r
        wv = w_ref[...]
        err = xhat * wv - t_ref[...]
        loss_ref[...] += 0.5 * _rowsum(jnp.mean(err * err, axis=0, keepdims=True))
        dyv = err * (1.0 / f)
        dw_ref[...] += _rowsum(dyv * xhat)
        dxhat = dyv * wv
        dh_ref[...] = r * (dxhat - xhat * jnp.mean(dxhat * xhat, axis=0, keepdims=True))

    blk = pl.BlockSpec((f, tt), lambda i: (0, i))
    col = pl.BlockSpec((f, 1), lambda i: (0, 0))
    one = pl.BlockSpec((1, 1), lambda i: (0, 0))
    return pl.pallas_call(
        body,
        name="final_norm_loss",
        grid=(t // tt,),
        in_specs=[blk, blk, col],
        out_specs=[blk, one, col],
        out_shape=[jax.ShapeDtypeStruct((f, t), F32), jax.ShapeDtypeStruct((1, 1), F32), jax.ShapeDtypeStruct((f, 1), F32)],
        compiler_params=_cparams("arbitrary"),
    )(h, tgt, w_col)


def _attn_mask(n):
    shape = (2 * WINDOW, Q_PER_KV * WINDOW)
    si = lax.broadcasted_iota(jnp.int32, shape, 0)
    qi = lax.broadcasted_iota(jnp.int32, shape, 1) & (WINDOW - 1)
    dist = WINDOW + qi - si
    return (dist >= 0) & (dist < WINDOW) & ((si >= WINDOW) | (n > 0))


def _lane_cat(ref, row0, rows):
    return jnp.concatenate([ref[row0 + i * rows:row0 + (i + 1) * rows, :] for i in range(Q_PER_KV)], axis=1)


def _attn_fwd(proj, sinks):
    t = proj.shape[1]
    nb = t // WINDOW
    scale = HEAD_DIM ** -0.5

    def body(s_ref, q_ref, kc_ref, kp_ref, vc_ref, vp_ref, o_ref, lse_ref):
        n = pl.program_id(0)
        valid = _attn_mask(n)
        for g in range(N_KV_HEADS):
            rows = slice(g * HEAD_DIM, (g + 1) * HEAD_DIM)
            kt = jnp.concatenate([kp_ref[rows, :], kc_ref[rows, :]], axis=1).astype(BF16)
            vt = jnp.concatenate([vp_ref[rows, :], vc_ref[rows, :]], axis=1).astype(BF16)
            qcat = (_lane_cat(q_ref, g * Q_PER_KV * HEAD_DIM, HEAD_DIM) * scale).astype(BF16)
            s = jnp.where(valid, _dot_tn(kt, qcat), NEG)
            sink = jnp.concatenate(
                [jnp.full((1, WINDOW), s_ref[g * Q_PER_KV + i], F32) for i in range(Q_PER_KV)], axis=1)
            m = jnp.maximum(jnp.max(s, axis=0, keepdims=True), sink)
            p = jnp.exp(s - m)
            denom = _colsum(p) + jnp.exp(sink - m)
            probs = (p / denom).astype(BF16)
            out = _dot(vt, probs)
            lse = m + jnp.log(denom)
            for i in range(Q_PER_KV):
                h = g * Q_PER_KV + i
                o_ref[h * HEAD_DIM:(h + 1) * HEAD_DIM, :] = out[:, i * WINDOW:(i + 1) * WINDOW]
                lse_ref[h:h + 1, :] = lse[:, i * WINDOW:(i + 1) * WINDOW]

    kb = OFF_K // KV_DIM
    vb = OFF_V // KV_DIM
    prev = lambda n: jnp.maximum(n - 1, 0)
    return pl.pallas_call(
        body,
        name="attn_fwd",
        grid=(nb,),
        in_specs=[
            pl.BlockSpec(memory_space=pltpu.SMEM),
            pl.BlockSpec((Q_DIM, WINDOW), lambda n: (0, n)),
            pl.BlockSpec((KV_DIM, WINDOW), lambda n: (kb, n)),
            pl.BlockSpec((KV_DIM, WINDOW), lambda n: (kb, prev(n))),
            pl.BlockSpec((KV_DIM, WINDOW), lambda n: (vb, n)),
            pl.BlockSpec((KV_DIM, WINDOW), lambda n: (vb, prev(n))),
        ],
        out_specs=[pl.BlockSpec((Q_DIM, WINDOW), lambda n: (0, n)), pl.BlockSpec((N_Q_HEADS, WINDOW), lambda n: (0, n))],
        out_shape=[jax.ShapeDtypeStruct((Q_DIM, t), F32), jax.ShapeDtypeStruct((N_Q_HEADS, t), F32)],
        compiler_params=_cparams("parallel"),
    )(sinks, proj, proj, proj, proj, proj)


def _attn_bwd(proj, sinks, out, lse, dout, send=None):
    t = proj.shape[1]
    nb = t // WINDOW
    scale = HEAD_DIM ** -0.5

    def body(s_ref, q_ref, kc_ref, kp_ref, vc_ref, vp_ref, o_ref, lse_ref, do_ref,
             dq_ref, dk_ref, dv_ref, ds_ref, dk_carry, dv_carry):
        step = pl.program_id(0)
        n = nb - 1 - step

        @pl.when(step == 0)
        def _():
            dk_carry[...] = jnp.zeros_like(dk_carry)
            dv_carry[...] = jnp.zeros_like(dv_carry)
            ds_ref[...] = jnp.zeros_like(ds_ref)

        valid = _attn_mask(n)
        for g in range(N_KV_HEADS):
            rows = slice(g * HEAD_DIM, (g + 1) * HEAD_DIM)
            q0 = g * Q_PER_KV * HEAD_DIM
            kt = jnp.concatenate([kp_ref[rows, :], kc_ref[rows, :]], axis=1).astype(BF16)
            vt = jnp.concatenate([vp_ref[rows, :], vc_ref[rows, :]], axis=1).astype(BF16)
            qf = _lane_cat(q_ref, q0, HEAD_DIM)
            qcat = qf.astype(BF16)
            ocat = _lane_cat(o_ref, q0, HEAD_DIM)
            docat = _lane_cat(do_ref, q0, HEAD_DIM)
            dob = docat.astype(BF16)
            lse_cat = jnp.concatenate(
                [lse_ref[g * Q_PER_KV + i:g * Q_PER_KV + i + 1, :] for i in range(Q_PER_KV)], axis=1)
            sink = jnp.concatenate(
                [jnp.full((1, WINDOW), s_ref[g * Q_PER_KV + i], F32) for i in range(Q_PER_KV)], axis=1)
            s = jnp.where(valid, _dot_tn(kt, (qf * scale).astype(BF16)), NEG)
            p = jnp.exp(s - lse_cat)
            dp = _dot_tn(vt, dob)
            delta = _colsum(docat * ocat)
            dsc = (p * (dp - delta)).astype(BF16)
            dsink_row = -jnp.exp(sink - lse_cat) * delta
            dq = _dot(kt, dsc) * scale
            dk = _dot_nt(qcat, dsc) * scale
            dv = _dot_nt(dob, p.astype(BF16))
            for i in range(Q_PER_KV):
                h = g * Q_PER_KV + i
                dq_ref[h * HEAD_DIM:(h + 1) * HEAD_DIM, :] = dq[:, i * WINDOW:(i + 1) * WINDOW].astype(BF16)
                ds_ref[h:h + 1, :] += _rowsum(dsink_row[:, i * WINDOW:(i + 1) * WINDOW])
            dk_ref[rows, :] = (dk[:, WINDOW:] + dk_carry[rows, :]).astype(BF16)
            dv_ref[rows, :] = (dv[:, WINDOW:] + dv_carry[rows, :]).astype(BF16)
            dk_carry[rows, :] = dk[:, :WINDOW]
            dv_carry[rows, :] = dv[:, :WINDOW]

    kb = OFF_K // KV_DIM
    vb = OFF_V // KV_DIM
    cur = lambda i: nb - 1 - i
    prev = lambda i: jnp.maximum(nb - 2 - i, 0)
    qspec = pl.BlockSpec((Q_DIM, WINDOW), lambda i: (0, cur(i)))
    kvspec = pl.BlockSpec((KV_DIM, WINDOW), lambda i: (0, cur(i)))
    return _call(
        body,
        name="attn_bwd",
        grid=(nb,),
        in_specs=[
            pl.BlockSpec(memory_space=pltpu.SMEM),
            qspec,
            pl.BlockSpec((KV_DIM, WINDOW), lambda i: (kb, cur(i))),
            pl.BlockSpec((KV_DIM, WINDOW), lambda i: (kb, prev(i))),
            pl.BlockSpec((KV_DIM, WINDOW), lambda i: (vb, cur(i))),
            pl.BlockSpec((KV_DIM, WINDOW), lambda i: (vb, prev(i))),
            qspec,
            pl.BlockSpec((N_Q_HEADS, WINDOW), lambda i: (0, cur(i))),
            qspec,
        ],
        out_specs=[qspec, kvspec, kvspec, pl.BlockSpec((N_Q_HEADS, 1), lambda i: (0, 0))],
        out_shape=[
            jax.ShapeDtypeStruct((Q_DIM, t), BF16),
            jax.ShapeDtypeStruct((KV_DIM, t), BF16),
            jax.ShapeDtypeStruct((KV_DIM, t), BF16),
            jax.ShapeDtypeStruct((N_Q_HEADS, 1), F32),
        ],
        scratch_shapes=[pltpu.VMEM((KV_DIM, WINDOW), F32), pltpu.VMEM((KV_DIM, WINDOW), F32)],
        semantics=("arbitrary",), args=(sinks, proj, proj, proj, proj, proj, out, lse, dout), send=send)


CONV_ROWS = 256


def _conv_silu_fwd(proj, w_col, b_col):
    t = proj.shape[1]
    r0 = OFF_X // CONV_ROWS

    def body(x_ref, w_ref, b_ref, o_ref):
        def strip(rows):
            y, _ = _causal_conv(x_ref[rows, :], w_ref[rows, :], b_ref[rows, :])
            o_ref[rows, :] = y * _sigmoid(y)

        strip(slice(None))

    return pl.pallas_call(
        body,
        name="ssd_conv_fwd",
        grid=(XBC_DIM // CONV_ROWS,),
        in_specs=[
            pl.BlockSpec((CONV_ROWS, t), lambda i: (r0 + i, 0)),
            pl.BlockSpec((CONV_ROWS, SSD_CONV), lambda i: (i, 0)),
            pl.BlockSpec((CONV_ROWS, 1), lambda i: (i, 0)),
        ],
        out_specs=pl.BlockSpec((CONV_ROWS, t), lambda i: (i, 0)),
        out_shape=jax.ShapeDtypeStruct((XBC_DIM, t), F32),
        compiler_params=_cparams("parallel"),
    )(proj, w_col, b_col)


def _conv_silu_bwd(proj, w_col, b_col, dout, row0, name):
    t = proj.shape[1]
    nrows = dout.shape[0]
    p0 = (OFF_X + row0) // CONV_ROWS
    c0 = row0 // CONV_ROWS

    def body(x_ref, w_ref, b_ref, do_ref, dx_ref, dwb_ref):
        def strip(rows):
            xv = x_ref[rows, :]
            wv = w_ref[rows, :]
            y, shifted = _causal_conv(xv, wv, b_ref[rows, :])
            sg = _sigmoid(y)
            dy = do_ref[rows, :] * (sg * (1.0 + y * (1.0 - sg)))
            dx, dwb_ref[rows, :] = _causal_conv_bwd(dy, xv, shifted, wv)
            dx_ref[rows, :] = dx.astype(BF16)

        strip(slice(None))

    return pl.pallas_call(
        body,
        name=name,
        grid=(nrows // CONV_ROWS,),
        in_specs=[
            pl.BlockSpec((CONV_ROWS, t), lambda i: (p0 + i, 0)),
            pl.BlockSpec((CONV_ROWS, SSD_CONV), lambda i: (c0 + i, 0)),
            pl.BlockSpec((CONV_ROWS, 1), lambda i: (c0 + i, 0)),
            pl.BlockSpec((CONV_ROWS, t), lambda i: (i, 0)),
        ],
        out_specs=[pl.BlockSpec((CONV_ROWS, t), lambda i: (i, 0)), pl.BlockSpec((CONV_ROWS, 128), lambda i: (i, 0))],
        out_shape=[jax.ShapeDtypeStruct((nrows, t), BF16), jax.ShapeDtypeStruct((nrows, 128), F32)],
        compiler_params=_cparams("parallel"),
    )(proj, w_col, b_col, dout)


GROUP_ROWS = HEADS_PER_GROUP * SSD_HEAD_DIM


def _ssd_specs(order):
    xb = D_INNER // BC_DIM
    dtb = OFF_DT // N_SSD_HEADS
    col = pl.BlockSpec((N_SSD_HEADS, 1), lambda c: (0, 0))
    return [
        pl.BlockSpec((D_INNER, CHUNK), lambda c: (0, order(c))),
        pl.BlockSpec((BC_DIM, CHUNK), lambda c: (xb, order(c))),
        pl.BlockSpec((BC_DIM, CHUNK), lambda c: (xb + 1, order(c))),
        pl.BlockSpec((N_SSD_HEADS, CHUNK), lambda c: (dtb, order(c))),
        col, col, col,
    ]


def _ssd_common(dt_ref, dtb_ref, alog_ref):
    z = dt_ref[...] + dtb_ref[...]
    dt = _softplus(z)
    a_neg = -jnp.exp(alog_ref[...])
    d_a = dt * a_neg
    row = lax.broadcasted_iota(jnp.int32, (CHUNK, CHUNK), 0)
    colm = lax.broadcasted_iota(jnp.int32, (CHUNK, CHUNK), 1)
    upper = (row <= colm).astype(F32)
    a_cs = jnp.dot(d_a, upper, precision=HIGHEST, preferred_element_type=F32)
    a_last = _rowsum(d_a)
    return z, dt, a_neg, a_cs, a_last, row >= colm, row == colm


def _decay(a_row, causal):
    a_s = jnp.broadcast_to(a_row, (CHUNK, CHUNK))
    seg = a_s.T - a_s
    return jnp.where(causal, jnp.exp(jnp.where(causal, seg, 0.0)), 0.0)


def _ssd_fwd(xbc, proj, dtb_col, alog_col, dsk_col):
    t = xbc.shape[1]
    nc = t // CHUNK

    def body(xs_ref, b_ref, c_ref, dt_ref, dtb_ref, alog_ref, dsk_ref, y_ref, hst_ref, h_scr):
        @pl.when(pl.program_id(0) == 0)
        def _():
            h_scr[...] = jnp.zeros_like(h_scr)

        _, dt, _, a_cs, a_last, causal, _ = _ssd_common(dt_ref, dtb_ref, alog_ref)
        hst_ref[0] = h_scr[...]
        dsk = dsk_ref[...]
        for g in range(N_SSD_GROUPS):
            grows = slice(g * D_STATE, (g + 1) * D_STATE)
            bb = b_ref[grows, :].astype(BF16)
            cb_ = c_ref[grows, :].astype(BF16)
            cb = _dot_tn(cb_, bb)
            for j in range(g * HEADS_PER_GROUP, (g + 1) * HEADS_PER_GROUP):
                rows = slice(j * SSD_HEAD_DIM, (j + 1) * SSD_HEAD_DIM)
                a = a_cs[j:j + 1, :]
                m = (cb * _decay(a, causal)).astype(BF16)
                xs = xs_ref[rows, :]
                xc = xs * dt[j:j + 1, :]
                hj = h_scr[rows, :]
                y = _dot_nt(xc.astype(BF16), m) + _dot(hj.astype(BF16), cb_) * jnp.exp(a) + dsk[j:j + 1, :] * xs
                y_ref[rows, :] = y
                al = a_last[j:j + 1, :]
                w = jnp.exp(al - a)
                h_scr[rows, :] = jnp.exp(al) * hj + _dot_nt((xc * w).astype(BF16), bb)

    return pl.pallas_call(
        body,
        name="ssd_fwd",
        grid=(nc,),
        in_specs=_ssd_specs(lambda c: c),
        out_specs=[
            pl.BlockSpec((D_INNER, CHUNK), lambda c: (0, c)),
            pl.BlockSpec((1, D_INNER, D_STATE), lambda c: (c, 0, 0)),
        ],
        out_shape=[
            jax.ShapeDtypeStruct((D_INNER, t), F32),
            jax.ShapeDtypeStruct((nc, D_INNER, D_STATE), F32),
        ],
        scratch_shapes=[pltpu.VMEM((D_INNER, D_STATE), F32)],
        compiler_params=_cparams("arbitrary"),
    )(xbc, xbc, xbc, proj, dtb_col, alog_col, dsk_col)


def _ssd_bwd(xbc, proj, dtb_col, alog_col, dsk_col, hst, dy):
    t = xbc.shape[1]
    nc = t // CHUNK
    rev = lambda c: nc - 1 - c

    def body(xs_ref, b_ref, c_ref, dt_ref, dtb_ref, alog_ref, dsk_ref, hst_ref, dy_ref,
             dxs_ref, db_ref, dc_ref, ddt_ref, dalog_ref, ddsk_ref, ddtb_ref, dh_scr, da_scr, ddt_scr, dd_scr):
        @pl.when(pl.program_id(0) == 0)
        def _():
            dh_scr[...] = jnp.zeros_like(dh_scr)
            dalog_ref[...] = jnp.zeros_like(dalog_ref)
            ddsk_ref[...] = jnp.zeros_like(ddsk_ref)
            ddtb_ref[...] = jnp.zeros_like(ddtb_ref)

        z, dt, a_neg, a_cs, a_last, causal, eye = _ssd_common(dt_ref, dtb_ref, alog_ref)
        dsk = dsk_ref[...]
        last_lane = lax.broadcasted_iota(jnp.int32, (1, CHUNK), 1) == CHUNK - 1
        for g in range(N_SSD_GROUPS):
            grows = slice(g * D_STATE, (g + 1) * D_STATE)
            bb = b_ref[grows, :].astype(BF16)
            cb_ = c_ref[grows, :].astype(BF16)
            cb = _dot_tn(cb_, bb)
            dcb = jnp.zeros((CHUNK, CHUNK), F32)
            dc_acc = jnp.zeros((D_STATE, CHUNK), F32)
            db_acc = jnp.zeros((D_STATE, CHUNK), F32)
            for j in range(g * HEADS_PER_GROUP, (g + 1) * HEADS_PER_GROUP):
                rows = slice(j * SSD_HEAD_DIM, (j + 1) * SSD_HEAD_DIM)
                a = a_cs[j:j + 1, :]
                al = a_last[j:j + 1, :]
                lam = _decay(a, causal)
                mf = cb * lam
                xs = xs_ref[rows, :]
                dtj = dt[j:j + 1, :]
                xc = xs * dtj
                w = jnp.exp(al - a)
                e = jnp.exp(a)
                gam = jnp.exp(al)
                hj = hst_ref[0, rows, :]
                hjb = hj.astype(BF16)
                dyv = dy_ref[rows, :]
                dyb = dyv.astype(BF16)
                dd_scr[j:j + 1, :] = _colsum(dyv * xs)
                gb = (dyv * e).astype(BF16)
                dh_in = _dot_nt(gb, cb_)
                dc_acc = dc_acc + _dot_tn(hjb, gb)
                yoff = _dot(hjb, cb_) * e
                da = _colsum(dyv * yoff)
                dm = _dot_tn(dyb, xc.astype(BF16))
                dxc = _dot(dyb, mf.astype(BF16))
                dcb = dcb + dm * lam
                nmat = dm * mf
                rs = jnp.broadcast_to(_rowsum(nmat), (CHUNK, CHUNK))
                da = da + _colsum(jnp.where(eye, rs, 0.0)) - _colsum(nmat)
                ds = dh_scr[rows, :]
                dsb = ds.astype(BF16)
                t1 = _dot(dsb, bb)
                xcw = xc * w
                dxc = dxc + w * t1
                dww = _colsum(xcw * t1)
                da_l = _rowsum(dww) + _rowsum(_colsum(ds * hj)) * gam
                da = da - dww + jnp.where(last_lane, da_l, 0.0)
                db_acc = db_acc + _dot_tn(dsb, xcw.astype(BF16))
                dh_scr[rows, :] = gam * ds + dh_in
                dxs_ref[rows, :] = dsk[j:j + 1, :] * dyv + dxc * dtj
                da_scr[j:j + 1, :] = da
                ddt_scr[j:j + 1, :] = _colsum(dxc * xs)
            dcbb = dcb.astype(BF16)
            dc_ref[grows, :] = dc_acc + _dot_nt(bb, dcbb)
            db_ref[grows, :] = db_acc + _dot(cb_, dcbb)
        dda = jnp.dot(da_scr[...], causal.astype(F32), precision=HIGHEST, preferred_element_type=F32)
        ddt = ddt_scr[...] + dda * a_neg
        ddt_raw = ddt * _sigmoid(z)
        ddt_ref[...] = ddt_raw
        ddtb_ref[...] += _rowsum(ddt_raw)
        dalog_ref[...] += _rowsum(dda * dt) * a_neg
        ddsk_ref[...] += _rowsum(dd_scr[...])

    col = pl.BlockSpec((N_SSD_HEADS, 1), lambda c: (0, 0))
    bc = pl.BlockSpec((BC_DIM, CHUNK), lambda c: (0, rev(c)))
    xs_spec = pl.BlockSpec((D_INNER, CHUNK), lambda c: (0, rev(c)))
    small = pltpu.VMEM((N_SSD_HEADS, CHUNK), F32)
    return pl.pallas_call(
        body,
        name="ssd_bwd",
        grid=(nc,),
        in_specs=_ssd_specs(rev) + [pl.BlockSpec((1, D_INNER, D_STATE), lambda c: (rev(c), 0, 0)), xs_spec],
        out_specs=[xs_spec, bc, bc, pl.BlockSpec((N_SSD_HEADS, CHUNK), lambda c: (0, rev(c))), col, col, col],
        out_shape=[
            jax.ShapeDtypeStruct((D_INNER, t), F32),
            jax.ShapeDtypeStruct((BC_DIM, t), F32),
            jax.ShapeDtypeStruct((BC_DIM, t), F32),
            jax.ShapeDtypeStruct((N_SSD_HEADS, t), F32),
            jax.ShapeDtypeStruct((N_SSD_HEADS, 1), F32),
            jax.ShapeDtypeStruct((N_SSD_HEADS, 1), F32),
            jax.ShapeDtypeStruct((N_SSD_HEADS, 1), F32),
        ],
        scratch_shapes=[pltpu.VMEM((D_INNER, D_STATE), F32), small, small, small],
        compiler_params=_cparams("arbitrary"),
    )(xbc, xbc, xbc, proj, dtb_col, alog_col, dsk_col, hst, dy)


GN_ROWS = D_INNER // N_SSD_GROUPS


def _gnorm_fwd(y, proj, w_col):
    t = y.shape[1]
    tt = _tile(t, (512, 256, 128))
    z0 = OFF_Z // GN_ROWS

    def body(y_ref, z_ref, w_ref, o_ref):
        zv = z_ref[...]
        u = y_ref[...] * (zv * _sigmoid(zv))
        r = lax.rsqrt(jnp.mean(u * u, axis=0, keepdims=True) + EPS)
        o_ref[...] = (u * r * w_ref[...]).astype(BF16)

    blk = pl.BlockSpec((GN_ROWS, tt), lambda g, i: (g, i))
    return pl.pallas_call(
        body,
        name="gnorm_fwd",
        grid=(N_SSD_GROUPS, t // tt),
        in_specs=[blk, pl.BlockSpec((GN_ROWS, tt), lambda g, i: (z0 + g, i)), pl.BlockSpec((GN_ROWS, 1), lambda g, i: (g, 0))],
        out_specs=blk,
        out_shape=jax.ShapeDtypeStruct((D_INNER, t), BF16),
        compiler_params=_cparams("parallel", "parallel"),
    )(y, proj, w_col)


def _gnorm_bwd(dout, y, proj, w_col, send=None):
    t = y.shape[1]
    tt = _tile(t, (512, 256, 128))
    z0 = OFF_Z // GN_ROWS

    def body(do_ref, y_ref, z_ref, w_ref, dy_ref, dz_ref, dw_ref):
        @pl.when(pl.program_id(1) == 0)
        def _():
            dw_ref[...] = jnp.zeros_like(dw_ref)

        zv = z_ref[...]
        yv = y_ref[...]
        sg = _sigmoid(zv)
        sz = zv * sg
        u = yv * sz
        r = lax.rsqrt(jnp.mean(u * u, axis=0, keepdims=True) + EPS)
        xhat = u * r
        dov = do_ref[...]
        dw_ref[...] += _rowsum(dov * xhat)
        dxhat = dov * w_ref[...]
        du = r * (dxhat - xhat * jnp.mean(dxhat * xhat, axis=0, keepdims=True))
        dy_ref[...] = du * sz
        dz_ref[...] = (du * yv * (sg * (1.0 + zv * (1.0 - sg)))).astype(BF16)

    blk = pl.BlockSpec((GN_ROWS, tt), lambda g, i: (g, i))
    col = pl.BlockSpec((GN_ROWS, 1), lambda g, i: (g, 0))
    return _call(
        body,
        name="gnorm_bwd",
        grid=(N_SSD_GROUPS, t // tt),
        in_specs=[blk, blk, pl.BlockSpec((GN_ROWS, tt), lambda g, i: (z0 + g, i)), col],
        out_specs=[blk, blk, col],
        out_shape=[jax.ShapeDtypeStruct((D_INNER, t), F32), jax.ShapeDtypeStruct((D_INNER, t), BF16),
                   jax.ShapeDtypeStruct((D_INNER, 1), F32)],
        semantics=("parallel", "arbitrary"), args=(dout, y, proj, w_col), send=send)


GATE_ROWS = 32


def _gate_specs(t):
    ga0 = OFF_GA // GATE_ROWS
    gs0 = OFF_GS // GATE_ROWS
    nr = D_MODEL // GATE_ROWS
    blk = pl.BlockSpec((GATE_ROWS, t), lambda r: (r, 0))
    return blk, [
        pl.BlockSpec((GATE_ROWS, t), lambda r: (ga0 + r, 0)),
        pl.BlockSpec((GATE_ROWS, t), lambda r: (gs0 + r, 0)),
        pl.BlockSpec((GATE_ROWS, 1), lambda r: (r, 0)),
        pl.BlockSpec((GATE_ROWS, 1), lambda r: (nr + r, 0)),
        blk, blk,
    ]


def _gate_fwd(proj, b_col, attn, ssd):
    t = proj.shape[1]
    blk, specs = _gate_specs(t)

    def body(ga_ref, gs_ref, ba_ref, bs_ref, a_ref, s_ref, o_ref):
        o_ref[...] = (_sigmoid(ga_ref[...] + ba_ref[...]) * a_ref[...]
                      + _sigmoid(gs_ref[...] + bs_ref[...]) * s_ref[...]).astype(BF16)

    return pl.pallas_call(
        body,
        name="gate_fwd",
        grid=(D_MODEL // GATE_ROWS,),
        in_specs=specs,
        out_specs=blk,
        out_shape=jax.ShapeDtypeStruct((D_MODEL, t), BF16),
        compiler_params=_cparams("parallel"),
    )(proj, proj, b_col, b_col, attn, ssd)


def _gate_bwd(proj, b_col, attn, ssd, dmix, send=None):
    t = proj.shape[1]
    blk, specs = _gate_specs(t)

    def body(ga_ref, gs_ref, ba_ref, bs_ref, a_ref, s_ref, dm_ref, da_ref, dso_ref, dga_ref, dgs_ref, dba_ref, dbs_ref):
        dm = dm_ref[...]
        sa = _sigmoid(ga_ref[...] + ba_ref[...])
        ss = _sigmoid(gs_ref[...] + bs_ref[...])
        da_ref[...] = (dm * sa).astype(BF16)
        dso_ref[...] = (dm * ss).astype(BF16)
        dga = dm * a_ref[...] * sa * (1.0 - sa)
        dgs = dm * s_ref[...] * ss * (1.0 - ss)
        dga_ref[...] = dga.astype(BF16)
        dgs_ref[...] = dgs.astype(BF16)
        dba_ref[...] = _rowsum(dga)
        dbs_ref[...] = _rowsum(dgs)

    col = pl.BlockSpec((GATE_ROWS, 1), lambda r: (r, 0))
    act = jax.ShapeDtypeStruct((D_MODEL, t), BF16)
    bias = jax.ShapeDtypeStruct((D_MODEL, 1), F32)
    return _call(
        body,
        name="gate_bwd",
        grid=(D_MODEL // GATE_ROWS,),
        in_specs=specs + [blk],
        out_specs=[blk, blk, blk, blk, col, col],
        out_shape=[act, act, act, act, bias, bias],
        semantics=("parallel",), args=(proj, proj, b_col, b_col, attn, ssd, dmix), send=send)


FFN_ROWS = 256


def _ffn_fwd(u0, w_col, b_col):
    t = u0.shape[2]

    def body(u_ref, w_ref, b_ref, o_ref):
        def strip(rows):
            val, _ = _causal_conv(u_ref[0, rows, :], w_ref[0, rows, :], b_ref[0, rows, :])
            gt, _ = _causal_conv(u_ref[1, rows, :], w_ref[1, rows, :], b_ref[1, rows, :])
            o_ref[rows, :] = (gt * _sigmoid(gt) * val).astype(BF16)

        strip(slice(None))

    return pl.pallas_call(
        body,
        name="ffn_fwd",
        grid=(D_FF // FFN_ROWS,),
        in_specs=[
            pl.BlockSpec((2, FFN_ROWS, t), lambda i: (0, i, 0)),
            pl.BlockSpec((2, FFN_ROWS, FFN_CONV), lambda i: (0, i, 0)),
            pl.BlockSpec((2, FFN_ROWS, 1), lambda i: (0, i, 0)),
        ],
        out_specs=pl.BlockSpec((FFN_ROWS, t), lambda i: (i, 0)),
        out_shape=jax.ShapeDtypeStruct((D_FF, t), BF16),
        compiler_params=_cparams("parallel"),
    )(u0, w_col, b_col)


def _ffn_bwd(u0, w_col, b_col, dg, send=None):
    t = u0.shape[2]

    def body(u_ref, w_ref, b_ref, dg_ref, du_ref, dwb_ref):
        def strip(rows):
            xval, wval = u_ref[0, rows, :], w_ref[0, rows, :]
            xgt, wgt = u_ref[1, rows, :], w_ref[1, rows, :]
            val, sh_val = _causal_conv(xval, wval, b_ref[0, rows, :])
            gt, sh_gt = _causal_conv(xgt, wgt, b_ref[1, rows, :])
            sg = _sigmoid(gt)
            dgv = dg_ref[rows, :]
            dval = dgv * (gt * sg)
            dgt = dgv * val * (sg * (1.0 + gt * (1.0 - sg)))
            dx, dwb_ref[0, rows, :] = _causal_conv_bwd(dval, xval, sh_val, wval)
            du_ref[0, rows, :] = dx.astype(BF16)
            dx, dwb_ref[1, rows, :] = _causal_conv_bwd(dgt, xgt, sh_gt, wgt)
            du_ref[1, rows, :] = dx.astype(BF16)

        strip(slice(None))

    return _call(
        body,
        name="ffn_bwd",
        grid=(D_FF // FFN_ROWS,),
        in_specs=[
            pl.BlockSpec((2, FFN_ROWS, t), lambda i: (0, i, 0)),
            pl.BlockSpec((2, FFN_ROWS, FFN_CONV), lambda i: (0, i, 0)),
            pl.BlockSpec((2, FFN_ROWS, 1), lambda i: (0, i, 0)),
            pl.BlockSpec((FFN_ROWS, t), lambda i: (i, 0)),
        ],
        out_specs=[pl.BlockSpec((2, FFN_ROWS, t), lambda i: (0, i, 0)), pl.BlockSpec((2, FFN_ROWS, 128), lambda i: (0, i, 0))],
        out_shape=[jax.ShapeDtypeStruct((2, D_FF, t), BF16), jax.ShapeDtypeStruct((2, D_FF, 128), F32)],
        semantics=("parallel",), args=(u0, w_col, b_col, dg), send=send)


def _adamw_math(w, g, m, v):
    m = ADAM_B1 * m + (1.0 - ADAM_B1) * g
    v = ADAM_B2 * v + (1.0 - ADAM_B2) * (g * g)
    m_hat = m / (1.0 - ADAM_B1 ** ADAM_STEP)
    v_hat = v / (1.0 - ADAM_B2 ** ADAM_STEP)
    delta = -ADAM_LR * (m_hat / (jnp.sqrt(v_hat) + ADAM_EPS) + ADAM_WD * w)
    return delta, m, v


def _adamw_sharded(parts, w, m, v, name):
    r, c = w.shape[0], w.shape[-1]
    tc = _tile(c, (256, 128))
    blk_shape = (r, tc) if w.ndim == 2 else (r, 1, tc)

    def body(p_ref, w_ref, m_ref, v_ref, g_ref, d_ref, nm_ref, nv_ref):
        g = p_ref[0].astype(F32)
        for s in range(1, N_DEV):
            g = g + p_ref[s].astype(F32)
        flat = lambda ref: ref[...].reshape(r, tc)
        d, nm, nv = _adamw_math(flat(w_ref), g, flat(m_ref), flat(v_ref))
        for ref, val in ((g_ref, g), (d_ref, d), (nm_ref, nm), (nv_ref, nv)):
            ref[...] = val.reshape(blk_shape)

    blk = pl.BlockSpec(blk_shape, (lambda i: (0, i)) if w.ndim == 2 else (lambda i: (0, 0, i)))
    out = jax.ShapeDtypeStruct(w.shape, F32)
    return pl.pallas_call(
        body,
        name=name,
        grid=(c // tc,),
        in_specs=[pl.BlockSpec((N_DEV, r, tc), lambda i: (0, 0, i)), blk, blk, blk],
        out_specs=[blk, blk, blk, blk],
        out_shape=[out, out, out, out],
        compiler_params=_cparams("parallel"),
    )(parts, w, m, v)


def _sum_slots(parts):
    _, r, c = parts.shape

    def body(p_ref, o_ref):
        g = p_ref[0]
        for s in range(1, N_DEV):
            g = g + p_ref[s]
        o_ref[...] = g

    return pl.pallas_call(body, name="sum_small_grads", out_shape=jax.ShapeDtypeStruct((r, c), F32))(parts)


def _adamw_small(g, w, m, v):
    def body(g_ref, w_ref, m_ref, v_ref, d_ref, nm_ref, nv_ref):
        d_ref[...], nm_ref[...], nv_ref[...] = _adamw_math(w_ref[...], g_ref[...], m_ref[...], v_ref[...])

    out = jax.ShapeDtypeStruct(g.shape, F32)
    return pl.pallas_call(body, name="adamw_small", out_shape=[out, out, out])(g, w, m, v)


ANY = pl.BlockSpec(memory_space=pl.ANY)
FLIPS = [(k >> 2 & 1, k >> 1 & 1, k & 1) for k in range(1, N_DEV)]


def _place():
    return lax.axis_index("x"), lax.axis_index("y"), lax.axis_index("c")


HBM = pl.BlockSpec(memory_space=pltpu.HBM)
SEM = pl.BlockSpec(memory_space=pltpu.SEMAPHORE)
EFFECT = pltpu.SideEffectType.DATAFLOW_SIDE_EFFECTING


def _peer_copy(gather, src_ref, land_ref, send_sems, recv_sems, k, sending):
    x, y, c = _place()
    fx, fy, fc = FLIPS[k]
    me = 4 * x + 2 * y + c
    peer = 4 * (x ^ fx) + 2 * (y ^ fy) + (c ^ fc)
    return pltpu.make_async_remote_copy(
        src_ref=src_ref if gather else src_ref.at[peer],
        dst_ref=land_ref.at[me if sending else peer],
        send_sem=send_sems.at[k], recv_sem=recv_sems.at[k],
        device_id=(x ^ fx, y ^ fy, c ^ fc), device_id_type=MESH)


def _gather_start(srcs, name):
    n = len(srcs)
    lands = [lax.empty((N_DEV,) + s.shape, s.dtype) for s in srcs]

    def body(*refs):
        src_refs, land_refs = refs[:n], refs[n:2 * n]
        send, recv = refs[2 * n:3 * n], refs[3 * n:4 * n]
        local_sems = refs[6 * n]
        x, y, c = _place()
        me = 4 * x + 2 * y + c
        local = [pltpu.make_async_copy(src_refs[i], land_refs[i].at[me], local_sems.at[i]) for i in range(n)]
        for cp in local:
            cp.start()
        for i in range(n):
            for k in range(N_DEV - 1):
                _peer_copy(True, src_refs[i], land_refs[i], send[i], recv[i], k, True).start()
        for cp in local:
            cp.wait()

    sem = pltpu.SemaphoreType.DMA((N_DEV - 1,))
    hbm = lambda a: pltpu.HBM(a.shape, a.dtype)
    res = pl.pallas_call(
        body,
        name=name,
        in_specs=[HBM] * (2 * n),
        out_specs=[SEM] * (2 * n) + [HBM] * (2 * n),
        out_shape=[sem] * (2 * n) + [hbm(s) for s in srcs] + [hbm(a) for a in lands],
        input_output_aliases={i: 2 * n + i for i in range(2 * n)},
        scratch_shapes=[pltpu.SemaphoreType.DMA((n,))],
        compiler_params=pltpu.CompilerParams(has_side_effects=EFFECT),
    )(*[pltpu.with_memory_space_constraint(a, pltpu.HBM) for a in list(srcs) + lands])
    return res[:n], res[n:2 * n], res[2 * n:3 * n], res[3 * n:4 * n]


def _exchange_wait(send_sems, recv_sems, src, land, after, gather, name):
    def body(src_ref, land_ref, send_ref, recv_ref, after_ref, src_out, land_out):
        for k in range(N_DEV - 1):
            cp = _peer_copy(gather, src_ref, land_ref, send_ref, recv_ref, k, False)
            cp.wait_send()
            cp.wait_recv()

    hbm = lambda a: pltpu.HBM(a.shape, a.dtype)
    return pl.pallas_call(
        body,
        name=name,
        in_specs=[HBM, HBM, SEM, SEM, ANY],
        out_specs=[HBM, HBM],
        out_shape=[hbm(src), hbm(land)],
        input_output_aliases={0: 0, 1: 1},
        compiler_params=pltpu.CompilerParams(has_side_effects=EFFECT),
    )(src, land, send_sems, recv_sems, after)[1]


def _col(v):
    return v.reshape(-1, 1).astype(F32)


def _local_step(xt, tgt, weight, small):
    t = xt.shape[1]
    n1 = _col(small["norm1_w"])
    n2 = _col(small["norm2_w"])
    nf = _col(small["final_norm_w"])
    bg = _col(small["b_gate"])
    sinks = small["attn_sinks"].reshape(-1).astype(F32)
    cbias = _col(small["ssd_conv_b"])
    dtb = _col(small["dt_bias"])
    alog = _col(small["a_log"])
    dsk = _col(small["d_skip"])
    gnw = _col(small["ssd_norm_w"])
    fb = small["ffn_conv_b"].reshape(2, D_FF, 1)

    xn = _norm_fwd(xt, n1, "norm1_fwd")
    cw = weight("ssd_conv_w", xn).T
    fw = weight("ffn_conv_w", xn).T.reshape(2, D_FF, FFN_CONV)
    w_in_t = weight("w_in", xn)
    proj = _matmul(w_in_t, xn, nt=False, out_dtype=F32, name="mm_in")
    ao, lse = _attn_fwd(proj, sinks)
    w_ao = weight("w_attn_o", ao)
    attn = _matmul(w_ao, ao, nt=False, out_dtype=F32, name="mm_attn_o", tn_a=True)
    xbc = _conv_silu_fwd(proj, cw, cbias)
    y, hst = _ssd_fwd(xbc, proj, dtb, alog, dsk)
    yn = _gnorm_fwd(y, proj, gnw)
    w_so = weight("w_ssd_o", yn)
    ssd = _matmul(w_so, yn, nt=False, out_dtype=F32, name="mm_ssd_o", tn_a=True)
    mix = _gate_fwd(proj, bg, attn, ssd)
    w_out = weight("w_out", mix)
    h1 = _matmul(w_out, mix, nt=False, out_dtype=F32, name="mm_out", add=xt, tn_a=True)
    hn = _norm_fwd(h1, n2, "norm2_fwd")
    w_up_t = weight("w_up", hn)
    u0 = _matmul(w_up_t, hn, nt=False, out_dtype=F32, name="mm_up").reshape(2, D_FF, t)
    gl = _ffn_fwd(u0, fw, fb)
    w_down = weight("w_down", gl)
    h2 = _matmul(w_down, gl, nt=False, out_dtype=F32, name="mm_down", add=h1, tn_a=True)
    dh2, loss, d_nf = _final_norm_loss(h2, tgt, nf)

    g = {}
    handles = {}

    def sending(weight_name, grad, fn, *args, **kwargs):
        out, handles[weight_name] = fn(*args, send=grad.reshape(N_DEV, -1, D_MODEL), **kwargs)
        return out

    g_down = _matmul(gl, dh2, nt=True, out_dtype=BF16, name="mm_d_w_down")
    dgl = _matmul(w_down, dh2, nt=False, out_dtype=F32, name="mm_d_glu")
    du0, d_fwb = sending("w_down", g_down, _ffn_bwd, u0, fw, fb, dgl)
    du0 = du0.reshape(2 * D_FF, t)
    g_up = _matmul(du0, hn, nt=True, out_dtype=BF16, name="mm_d_w_up")
    dhn = sending("w_up", g_up, _matmul, w_up_t, du0, nt=False, out_dtype=F32, name="mm_d_hn", tn_a=True)
    dh1, d_n2 = _norm_bwd(dhn, h1, n2, dh2, "norm2_bwd")
    g_out = _matmul(mix, dh1, nt=True, out_dtype=BF16, name="mm_d_w_out")
    dmix = _matmul(w_out, dh1, nt=False, out_dtype=F32, name="mm_d_mix")
    d_attn, d_ssd, d_ga, d_gs, d_ba, d_bs = sending("w_out", g_out, _gate_bwd, proj, bg, attn, ssd, dmix)
    g_ao = _matmul(ao, d_attn, nt=True, out_dtype=BF16, name="mm_d_w_attn_o")
    dao = _matmul(w_ao, d_attn, nt=False, out_dtype=F32, name="mm_d_ao")
    dq, dk, dv, d_sinks = sending("w_attn_o", g_ao, _attn_bwd, proj, sinks, ao, lse, dao)
    g_so = _matmul(yn, d_ssd, nt=True, out_dtype=BF16, name="mm_d_w_ssd_o")
    dyn = _matmul(w_so, d_ssd, nt=False, out_dtype=F32, name="mm_d_yn")
    dy, dz, d_gnw = sending("w_ssd_o", g_so, _gnorm_bwd, dyn, y, proj, gnw)
    dxs, dbm, dcm, ddt, d_alog, d_dsk, d_dtb = _ssd_bwd(xbc, proj, dtb, alog, dsk, hst, dy)
    dx_xs, dwb_xs = _conv_silu_bwd(proj, cw, cbias, dxs, 0, "ssd_conv_bwd_x")
    dx_b, dwb_b = _conv_silu_bwd(proj, cw, cbias, dbm, D_INNER, "ssd_conv_bwd_b")
    dx_c, dwb_c = _conv_silu_bwd(proj, cw, cbias, dcm, D_INNER + BC_DIM, "ssd_conv_bwd_c")
    dwb_conv = jnp.concatenate([dwb_xs, dwb_b, dwb_c], axis=0)
    dproj = jnp.concatenate([dq, dk, dv, dz, dx_xs, dx_b, dx_c, ddt.astype(BF16), d_ga, d_gs], axis=0)
    g_in = _matmul(dproj, xn, nt=True, out_dtype=BF16, name="mm_d_w_in")
    dxn = sending("w_in", g_in, _matmul, w_in_t, dproj, nt=False, out_dtype=F32, name="mm_d_xn", tn_a=True)
    dx, d_n1 = _norm_bwd(dxn, xt, n1, dh1, "norm1_bwd")

    g["norm1_w"] = d_n1
    g["b_gate"] = jnp.concatenate([d_ba, d_bs], axis=0)
    g["attn_sinks"] = d_sinks
    g["ssd_conv_w"] = dwb_conv[:, :SSD_CONV].T
    g["ssd_conv_b"] = dwb_conv[:, SSD_CONV]
    g["dt_bias"] = d_dtb
    g["a_log"] = d_alog
    g["d_skip"] = d_dsk
    g["ssd_norm_w"] = d_gnw
    g["norm2_w"] = d_n2
    d_fwb = d_fwb.reshape(2 * D_FF, 128)
    g["ffn_conv_w"] = d_fwb[:, :FFN_CONV].T
    g["ffn_conv_b"] = d_fwb[:, FFN_CONV]
    g["final_norm_w"] = d_nf
    return loss, dx, g, handles


SMALL = ("norm1_w", "b_gate", "attn_sinks", "ssd_conv_w", "ssd_conv_b", "dt_bias", "a_log", "d_skip", "ssd_norm_w",
         "norm2_w", "ffn_conv_w", "ffn_conv_b", "final_norm_w")
SMALL_SHAPES = {"norm1_w": (1, D_MODEL), "b_gate": (1, 2 * D_MODEL), "attn_sinks": (1, N_Q_HEADS),
                "ssd_conv_w": (1, SSD_CONV, XBC_DIM), "ssd_conv_b": (1, XBC_DIM), "dt_bias": (1, N_SSD_HEADS),
                "a_log": (1, N_SSD_HEADS), "d_skip": (1, N_SSD_HEADS), "ssd_norm_w": (1, D_INNER),
                "norm2_w": (1, D_MODEL), "ffn_conv_w": (1, FFN_CONV, 2 * D_FF), "ffn_conv_b": (1, 2 * D_FF),
                "final_norm_w": (D_MODEL,)}
WEIGHT_ORDER = ("norm1_w", "w_in", "b_gate", "attn_sinks", "w_attn_o", "ssd_conv_w", "ssd_conv_b", "dt_bias", "a_log",
                "d_skip", "ssd_norm_w", "w_ssd_o", "w_out", "norm2_w", "w_up", "ffn_conv_w", "ffn_conv_b", "w_down",
                "final_norm_w")


def _pack(parts):
    flat = jnp.concatenate([p.reshape(-1).astype(F32) for p in parts])
    rows = -(-flat.shape[0] // 1024) * 8
    return jnp.pad(flat, (0, rows * 128 - flat.shape[0])).reshape(rows, 128)


def _unpack(packed, shapes):
    flat = packed.reshape(-1)
    out, pos = [], 0
    for shp in shapes:
        size = 1
        for d in shp:
            size *= d
        out.append(flat[pos:pos + size].reshape(shp))
        pos += size
    return out


def kernel(x, norm1_w, w_in, b_gate, attn_sinks, w_attn_o, ssd_conv_w, ssd_conv_b, dt_bias, a_log, d_skip, ssd_norm_w, w_ssd_o, w_out, norm2_w, w_up, ffn_conv_w, ffn_conv_b, w_down, final_norm_w, loss_target, m_norm1_w, m_w_in, m_b_gate, m_attn_sinks, m_w_attn_o, m_ssd_conv_w, m_ssd_conv_b, m_dt_bias, m_a_log, m_d_skip, m_ssd_norm_w, m_w_ssd_o, m_w_out, m_norm2_w, m_w_up, m_ffn_conv_w, m_ffn_conv_b, m_w_down, m_final_norm_w, v_norm1_w, v_w_in, v_b_gate, v_attn_sinks, v_w_attn_o, v_ssd_conv_w, v_ssd_conv_b, v_dt_bias, v_a_log, v_d_skip, v_ssd_norm_w, v_w_ssd_o, v_w_out, v_norm2_w, v_w_up, v_ffn_conv_w, v_ffn_conv_b, v_w_down, v_final_norm_w):
    w = dict(norm1_w=norm1_w, w_in=w_in, b_gate=b_gate, attn_sinks=attn_sinks, w_attn_o=w_attn_o, ssd_conv_w=ssd_conv_w, ssd_conv_b=ssd_conv_b, dt_bias=dt_bias, a_log=a_log, d_skip=d_skip, ssd_norm_w=ssd_norm_w, w_ssd_o=w_ssd_o, w_out=w_out, norm2_w=norm2_w, w_up=w_up, ffn_conv_w=ffn_conv_w, ffn_conv_b=ffn_conv_b, w_down=w_down, final_norm_w=final_norm_w)
    m = dict(norm1_w=m_norm1_w, w_in=m_w_in, b_gate=m_b_gate, attn_sinks=m_attn_sinks, w_attn_o=m_w_attn_o, ssd_conv_w=m_ssd_conv_w, ssd_conv_b=m_ssd_conv_b, dt_bias=m_dt_bias, a_log=m_a_log, d_skip=m_d_skip, ssd_norm_w=m_ssd_norm_w, w_ssd_o=m_w_ssd_o, w_out=m_w_out, norm2_w=m_norm2_w, w_up=m_w_up, ffn_conv_w=m_ffn_conv_w, ffn_conv_b=m_ffn_conv_b, w_down=m_w_down, final_norm_w=m_final_norm_w)
    v = dict(norm1_w=v_norm1_w, w_in=v_w_in, b_gate=v_b_gate, attn_sinks=v_attn_sinks, w_attn_o=v_w_attn_o, ssd_conv_w=v_ssd_conv_w, ssd_conv_b=v_ssd_conv_b, dt_bias=v_dt_bias, a_log=v_a_log, d_skip=v_d_skip, ssd_norm_w=v_ssd_norm_w, w_ssd_o=v_w_ssd_o, w_out=v_w_out, norm2_w=v_norm2_w, w_up=v_w_up, ffn_conv_w=v_ffn_conv_w, ffn_conv_b=v_ffn_conv_b, w_down=v_w_down, final_norm_w=v_final_norm_w)
    me = 4 * lax.axis_index("x") + 2 * lax.axis_index("y") + lax.axis_index("c")
    conv_cols = XBC_DIM // N_DEV
    ffn_cols = 2 * D_FF // N_DEV

    shards = {"ssd_conv_w": ssd_conv_w[0], "ffn_conv_w": ffn_conv_w[0], "w_in": w_in[0].T.astype(BF16),
              "w_attn_o": w_attn_o[0].astype(BF16), "w_ssd_o": w_ssd_o[0].astype(BF16), "w_out": w_out[0].astype(BF16),
              "w_up": w_up[0].T.astype(BF16), "w_down": w_down[0].astype(BF16)}
    order = list(shards)
    g_send, g_recv, g_src, g_land = _gather_start(list(shards.values()), "gather_start")

    def weight(name, after):
        i = order.index(name)
        land = _exchange_wait(g_send[i], g_recv[i], g_src[i], g_land[i], after, True, "gather_wait_" + name)
        if name == "ssd_conv_w":
            return jnp.transpose(land, (1, 0, 2)).reshape(SSD_CONV, XBC_DIM)
        if name == "ffn_conv_w":
            return jnp.transpose(land, (1, 0, 2)).reshape(FFN_CONV, 2 * D_FF)
        return land.reshape(-1, D_MODEL)

    small = {k: w[k][0] if k != "final_norm_w" else w[k] for k in SMALL}
    loss, dx, g, pending = _local_step(x[0].T, loss_target[0].T, weight, small)

    packed = _pack([loss] + [g[k] for k in SMALL])
    s_send, s_recv, s_src, s_land = _gather_start([packed], "small_grads_start")

    res = {}
    after = s_src[0]
    for name in ("w_down", "w_up", "w_out", "w_attn_o", "w_ssd_o", "w_in"):
        parts = _exchange_wait(*pending[name], after, False, "grad_wait_" + name)
        view, back = {
            "w_in": (lambda a: jnp.transpose(a, (2, 0, 1)), lambda r: jnp.transpose(r, (1, 2, 0))),
            "w_up": (lambda a: a[0].T, lambda r: r.T[None]),
        }.get(name, (lambda a: a[0], lambda r: r[None]))
        res[name] = _adamw_sharded(parts, view(w[name]), view(m[name]), view(v[name]), "adamw_" + name)
        after = res[name][0]
        res[name] = [back(r) for r in res[name]]

    total = _sum_slots(_exchange_wait(s_send[0], s_recv[0], s_src[0], s_land[0], after, True, "small_grads_wait"))
    tot = _unpack(total, [(1,)] + [SMALL_SHAPES[k] for k in SMALL])
    loss_sum = tot[0].reshape(())
    gs = dict(zip(SMALL, tot[1:]))
    gs["ssd_conv_w"] = lax.dynamic_slice_in_dim(gs["ssd_conv_w"], me * conv_cols, conv_cols, axis=2)
    gs["ffn_conv_w"] = lax.dynamic_slice_in_dim(gs["ffn_conv_w"], me * ffn_cols, ffn_cols, axis=2)
    upd = _adamw_small(_pack([gs[k] for k in SMALL]), _pack([w[k] for k in SMALL]), _pack([m[k] for k in SMALL]),
                       _pack([v[k] for k in SMALL]))
    shapes = [w[k].shape for k in SMALL]
    d_s, m_s, v_s = (dict(zip(SMALL, _unpack(u, shapes))) for u in upd)
    for k in SMALL:
        res[k] = (gs[k], d_s[k], m_s[k], v_s[k])

    grad_x = dx.T[None]
    outs = [loss_sum, grad_x]
    for i in range(4):
        outs.extend(res[k][i] for k in WEIGHT_ORDER)
    return tuple(outs)
```

```python
import functools

import jax
import jax.numpy as jnp
from jax import lax
from jax.experimental import pallas as pl
from jax.experimental.pallas import tpu as pltpu

F32 = jnp.float32
BF16 = jnp.bfloat16
HIGHEST = lax.Precision.HIGHEST

D_MODEL = 1024
N_Q_HEADS = 16
N_KV_HEADS = 4
HEAD_DIM = 64
WINDOW = 128
Q_PER_KV = N_Q_HEADS // N_KV_HEADS
Q_DIM = N_Q_HEADS * HEAD_DIM
KV_DIM = N_KV_HEADS * HEAD_DIM
D_INNER = 2048
SSD_HEAD_DIM = 64
N_SSD_HEADS = 32
N_SSD_GROUPS = 4
HEADS_PER_GROUP = N_SSD_HEADS // N_SSD_GROUPS
D_STATE = 128
BC_DIM = N_SSD_GROUPS * D_STATE
XBC_DIM = D_INNER + 2 * BC_DIM
SSD_CONV = 4
CHUNK = 128
D_FF = 2816
FFN_CONV = 3
EPS = 1e-5
NEG = -1e30
IN_DIM = 8736
N_DEV = 8

OFF_Q = 0
OFF_K = OFF_Q + Q_DIM
OFF_V = OFF_K + KV_DIM
OFF_Z = OFF_V + KV_DIM
OFF_X = OFF_Z + D_INNER
OFF_DT = OFF_X + XBC_DIM
OFF_GA = OFF_DT + N_SSD_HEADS
OFF_GS = OFF_GA + D_MODEL

ADAM_LR = 0.001
ADAM_B1 = 0.9
ADAM_B2 = 0.999
ADAM_EPS = 1e-08
ADAM_WD = 0.01
ADAM_STEP = 10

VMEM_LIMIT = 48 * 1024 * 1024
MESH = pl.DeviceIdType.MESH


def _cparams(*sem):
    return pltpu.CompilerParams(dimension_semantics=sem, vmem_limit_bytes=VMEM_LIMIT)


def _tile(n, prefs):
    for p in prefs:
        if n % p == 0:
            return p
    return n


def _sigmoid(x):
    return 1.0 / (1.0 + jnp.exp(-x))


def _softplus(x):
    return jnp.maximum(x, 0.0) + jnp.log(1.0 + jnp.exp(-jnp.abs(x)))


def _rowsum(x):
    return jnp.sum(x, axis=1, keepdims=True)


def _colsum(x):
    return jnp.sum(x, axis=0, keepdims=True)


def _dot(a, b):
    return jnp.dot(a, b, preferred_element_type=F32)


def _dot_nt(a, b):
    return lax.dot_general(a, b, (((1,), (1,)), ((), ())), preferred_element_type=F32)


def _dot_tn(a, b):
    return lax.dot_general(a, b, (((0,), (0,)), ((), ())), preferred_element_type=F32)


def _shift_right(x, j):
    if j == 0:
        return x
    r = pltpu.roll(x, j, 1)
    lane = lax.broadcasted_iota(jnp.int32, (x.shape[0], 128), 1)
    return jnp.concatenate([jnp.where(lane >= j, r[:, :128], 0.0), r[:, 128:]], axis=1)


def _shift_left(x, j):
    if j == 0:
        return x
    n = x.shape[1]
    r = pltpu.roll(x, n - j, 1)
    lane = lax.broadcasted_iota(jnp.int32, (x.shape[0], 128), 1)
    return jnp.concatenate([r[:, :n - 128], jnp.where(lane < 128 - j, r[:, n - 128:], 0.0)], axis=1)


def _causal_conv(xv, wv, bv):
    taps = wv.shape[1]
    shifted = [_shift_right(xv, taps - 1 - k) for k in range(taps - 1)]
    y = bv + wv[:, taps - 1:taps] * xv
    for k in range(taps - 1):
        y = y + wv[:, k:k + 1] * shifted[k]
    return y, shifted


def _causal_conv_bwd(dy, xv, shifted, wv):
    taps = wv.shape[1]
    lane = lax.broadcasted_iota(jnp.int32, (dy.shape[0], 128), 1)
    dwb = jnp.where(lane == taps, _rowsum(dy), 0.0)
    dwb = jnp.where(lane == taps - 1, _rowsum(dy * xv), dwb)
    dx = wv[:, taps - 1:taps] * dy
    for k in range(taps - 1):
        dx = dx + wv[:, k:k + 1] * _shift_left(dy, taps - 1 - k)
        dwb = jnp.where(lane == k, _rowsum(dy * shifted[k]), dwb)
    return dx, dwb


def _call(body, *, name, grid, in_specs, out_specs, out_shape, args, semantics, scratch_shapes=(), send=None):
    if send is None:
        return pl.pallas_call(body, name=name, grid=grid, in_specs=in_specs, out_specs=out_specs, out_shape=out_shape,
                              scratch_shapes=list(scratch_shapes), compiler_params=_cparams(*semantics))(*args)
    single = not isinstance(out_specs, (list, tuple))
    out_specs, out_shape = ([out_specs], [out_shape]) if single else (list(out_specs), list(out_shape))
    n_in, n_out, n_scr = len(in_specs), len(out_specs), len(scratch_shapes)
    steps = 1
    for size in grid:
        steps *= size

    def sending(*refs):
        ins, (src_ref, land_ref) = refs[:n_in], refs[n_in:n_in + 2]
        outs = refs[n_in + 2:n_in + 2 + n_out]
        send_sems, recv_sems = refs[n_in + 2 + n_out:n_in + 4 + n_out]
        scratch, local_sem = refs[n_in + 6 + n_out:n_in + 6 + n_out + n_scr], refs[-1]
        x, y, c = _place()
        me = 4 * x + 2 * y + c
        local = pltpu.make_async_copy(src_ref.at[me], land_ref.at[me], local_sem)
        step = 0
        for axis, size in enumerate(grid):
            step = step * size + pl.program_id(axis)

        @pl.when(step == 0)
        def _():
            local.start()
            for peer in range(N_DEV - 1):
                _peer_copy(False, src_ref, land_ref, send_sems, recv_sems, peer, True).start()

        body(*ins, *outs, *scratch)

        @pl.when(step == steps - 1)
        def _():
            local.wait()

    sem = pltpu.SemaphoreType.DMA((N_DEV - 1,))
    hbm = pltpu.HBM(send.shape, send.dtype)
    res = pl.pallas_call(
        sending, name=name, grid=grid,
        in_specs=list(in_specs) + [HBM, HBM],
        out_specs=out_specs + [SEM, SEM, HBM, HBM],
        out_shape=out_shape + [sem, sem, hbm, hbm],
        input_output_aliases={n_in: n_out + 2, n_in + 1: n_out + 3},
        scratch_shapes=list(scratch_shapes) + [pltpu.SemaphoreType.DMA(())],
        compiler_params=pltpu.CompilerParams(dimension_semantics=("arbitrary",) * len(grid), vmem_limit_bytes=VMEM_LIMIT,
                                             has_side_effects=EFFECT),
    )(*args, pltpu.with_memory_space_constraint(send, pltpu.HBM),
      pltpu.with_memory_space_constraint(lax.empty(send.shape, send.dtype), pltpu.HBM))
    return (res[0] if single else list(res[:n_out])), tuple(res[n_out:])


MATMUL_VMEM_BUDGET = 36 * 1024 * 1024
MATMUL_MAX_TK = 3072


MATMUL_MAX_TM = 768


def _largest_tile(n, align, cap):
    return max(d for d in range(align, min(n, cap) + 1, align) if n % d == 0)


def _matmul_tiles(m, n, k, a_bytes, b_bytes, out_bytes, has_add, m_align, k_align):
    tm = _largest_tile(m, m_align, MATMUL_MAX_TM)
    tk = _largest_tile(k, k_align, MATMUL_MAX_TK)
    for tn in sorted({d for d in range(128, n + 1, 128) if n % d == 0}, reverse=True):
        need = 2 * (tm * tk * a_bytes + tk * tn * b_bytes) + tm * tn * (2 * out_bytes + (4 if k > tk else 0) + (8 if has_add else 0))
        if tn <= 3072 and need <= MATMUL_VMEM_BUDGET:
            return tm, tn, tk
    return tm, 128, tk


def _matmul(a, b, *, nt, out_dtype, name, add=None, tn_a=False, send=None):
    if tn_a:
        k, m = a.shape
    else:
        m, k = a.shape
    n = b.shape[0] if nt else b.shape[1]
    tm, tn, tk = _matmul_tiles(m, n, k, a.dtype.itemsize, b.dtype.itemsize, jnp.dtype(out_dtype).itemsize, add is not None,
                               128 if tn_a else 16, 16 if tn_a and not nt else 128)
    nk = k // tk
    grid = (m // tm, n // tn, nk)

    def body(a_ref, b_ref, *rest):
        r_ref = None
        if add is not None:
            r_ref, rest = rest[0], rest[1:]
        o_ref = rest[0]
        av = a_ref[...].astype(BF16)
        bv = b_ref[...].astype(BF16)
        part = _dot_tn(av, bv) if tn_a else _dot_nt(av, bv) if nt else _dot(av, bv)

        def finish(r):
            if add is not None:
                r = r + r_ref[...]
            o_ref[...] = r.astype(out_dtype)

        if nk == 1:
            finish(part)
            return
        acc = rest[1]
        kk = pl.program_id(2)

        @pl.when(kk == 0)
        def _():
            acc[...] = part

        @pl.when((kk > 0) & (kk < nk - 1))
        def _():
            acc[...] += part

        @pl.when(kk == nk - 1)
        def _():
            finish(acc[...] + part)

    in_specs = [
        pl.BlockSpec((tk, tm), lambda i, j, kk: (kk, i)) if tn_a else pl.BlockSpec((tm, tk), lambda i, j, kk: (i, kk)),
        pl.BlockSpec((tn, tk), lambda i, j, kk: (j, kk)) if nt else pl.BlockSpec((tk, tn), lambda i, j, kk: (kk, j)),
    ]
    args = [a, b]
    if add is not None:
        in_specs.append(pl.BlockSpec((tm, tn), lambda i, j, kk: (i, j)))
        args.append(add)
    return _call(
        body, name=name, grid=grid, in_specs=in_specs, args=args,
        out_specs=pl.BlockSpec((tm, tn), lambda i, j, kk: (i, j)),
        out_shape=jax.ShapeDtypeStruct((m, n), out_dtype),
        scratch_shapes=[pltpu.VMEM((tm, tn), F32)] if nk > 1 else [],
        semantics=("parallel", "parallel", "arbitrary"), send=send)


def _norm_fwd(x, w_col, name):
    f, t = x.shape
    tt = _tile(t, (512, 256, 128))

    def body(x_ref, w_ref, o_ref):
        xv = x_ref[...]
        r = lax.rsqrt(jnp.mean(xv * xv, axis=0, keepdims=True) + EPS)
        o_ref[...] = (xv * r * w_ref[...]).astype(BF16)

    return pl.pallas_call(
        body,
        name=name,
        grid=(t // tt,),
        in_specs=[pl.BlockSpec((f, tt), lambda i: (0, i)), pl.BlockSpec((f, 1), lambda i: (0, 0))],
        out_specs=pl.BlockSpec((f, tt), lambda i: (0, i)),
        out_shape=jax.ShapeDtypeStruct((f, t), BF16),
        compiler_params=_cparams("parallel"),
    )(x, w_col)


def _norm_bwd(dy, x, w_col, res, name):
    f, t = x.shape
    tt = _tile(t, (512, 256, 128))

    def body(dy_ref, x_ref, w_ref, res_ref, dx_ref, dw_ref):
        @pl.when(pl.program_id(0) == 0)
        def _():
            dw_ref[...] = jnp.zeros_like(dw_ref)

        xv = x_ref[...]
        r = lax.rsqrt(jnp.mean(xv * xv, axis=0, keepdims=True) + EPS)
        xhat = xv * r
        dyv = dy_ref[...]
        dw_ref[...] += _rowsum(dyv * xhat)
        dxhat = dyv * w_ref[...]
        dx_ref[...] = res_ref[...] + r * (dxhat - xhat * jnp.mean(dxhat * xhat, axis=0, keepdims=True))

    blk = pl.BlockSpec((f, tt), lambda i: (0, i))
    col = pl.BlockSpec((f, 1), lambda i: (0, 0))
    return pl.pallas_call(
        body,
        name=name,
        grid=(t // tt,),
        in_specs=[blk, blk, col, blk],
        out_specs=[blk, col],
        out_shape=[jax.ShapeDtypeStruct((f, t), F32), jax.ShapeDtypeStruct((f, 1), F32)],
        compiler_params=_cparams("arbitrary"),
    )(dy, x, w_col, res)


def _final_norm_loss(h, tgt, w_col):
    f, t = h.shape
    tt = _tile(t, (512, 256, 128))

    def body(h_ref, t_ref, w_ref, dh_ref, loss_ref, dw_ref):
        @pl.when(pl.program_id(0) == 0)
        def _():
            dw_ref[...] = jnp.zeros_like(dw_ref)
            loss_ref[...] = jnp.zeros_like(loss_ref)

        xv = h_ref[...]
        r = lax.rsqrt(jnp.mean(xv * xv, axis=0, keepdims=True) + EPS)
        xhat = xv * r
        wv = w_ref[...]
        err = xhat * wv - t_ref[...]
        loss_ref[...] += 0.5 * _rowsum(jnp.mean(err * err, axis=0, keepdims=True))
        dyv = err * (1.0 / f)
        dw_ref[...] += _rowsum(dyv * xhat)
        dxhat = dyv * wv
        dh_ref[...] = r * (dxhat - xhat * jnp.mean(dxhat * xhat, axis=0, keepdims=True))

    blk = pl.BlockSpec((f, tt), lambda i: (0, i))
    col = pl.BlockSpec((f, 1), lambda i: (0, 0))
    one = pl.BlockSpec((1, 1), lambda i: (0, 0))
    return pl.pallas_call(
        body,
        name="final_norm_loss",
        grid=(t // tt,),
        in_specs=[blk, blk, col],
        out_specs=[blk, one, col],
        out_shape=[jax.ShapeDtypeStruct((f, t), F32), jax.ShapeDtypeStruct((1, 1), F32), jax.ShapeDtypeStruct((f, 1), F32)],
        compiler_params=_cparams("arbitrary"),
    )(h, tgt, w_col)


def _attn_mask(n):
    shape = (2 * WINDOW, Q_PER_KV * WINDOW)
    si = lax.broadcasted_iota(jnp.int32, shape, 0)
    qi = lax.broadcasted_iota(jnp.int32, shape, 1) & (WINDOW - 1)
    dist = WINDOW + qi - si
    return (dist >= 0) & (dist < WINDOW) & ((si >= WINDOW) | (n > 0))


def _lane_cat(ref, row0, rows):
    return jnp.concatenate([ref[row0 + i * rows:row0 + (i + 1) * rows, :] for i in range(Q_PER_KV)], axis=1)


def _attn_fwd(proj, sinks):
    t = proj.shape[1]
    nb = t // WINDOW
    scale = HEAD_DIM ** -0.5

    def body(s_ref, q_ref, kc_ref, kp_ref, vc_ref, vp_ref, o_ref, lse_ref):
        n = pl.program_id(0)
        valid = _attn_mask(n)
        for g in range(N_KV_HEADS):
            rows = slice(g * HEAD_DIM, (g + 1) * HEAD_DIM)
            kt = jnp.concatenate([kp_ref[rows, :], kc_ref[rows, :]], axis=1).astype(BF16)
            vt = jnp.concatenate([vp_ref[rows, :], vc_ref[rows, :]], axis=1).astype(BF16)
            qcat = (_lane_cat(q_ref, g * Q_PER_KV * HEAD_DIM, HEAD_DIM) * scale).astype(BF16)
            s = jnp.where(valid, _dot_tn(kt, qcat), NEG)
            sink = jnp.concatenate(
                [jnp.full((1, WINDOW), s_ref[g * Q_PER_KV + i], F32) for i in range(Q_PER_KV)], axis=1)
            m = jnp.maximum(jnp.max(s, axis=0, keepdims=True), sink)
            p = jnp.exp(s - m)
            denom = _colsum(p) + jnp.exp(sink - m)
            probs = (p / denom).astype(BF16)
            out = _dot(vt, probs)
            lse = m + jnp.log(denom)
            for i in range(Q_PER_KV):
                h = g * Q_PER_KV + i
                o_ref[h * HEAD_DIM:(h + 1) * HEAD_DIM, :] = out[:, i * WINDOW:(i + 1) * WINDOW]
                lse_ref[h:h + 1, :] = lse[:, i * WINDOW:(i + 1) * WINDOW]

    kb = OFF_K // KV_DIM
    vb = OFF_V // KV_DIM
    prev = lambda n: jnp.maximum(n - 1, 0)
    return pl.pallas_call(
        body,
        name="attn_fwd",
        grid=(nb,),
        in_specs=[
            pl.BlockSpec(memory_space=pltpu.SMEM),
            pl.BlockSpec((Q_DIM, WINDOW), lambda n: (0, n)),
            pl.BlockSpec((KV_DIM, WINDOW), lambda n: (kb, n)),
            pl.BlockSpec((KV_DIM, WINDOW), lambda n: (kb, prev(n))),
            pl.BlockSpec((KV_DIM, WINDOW), lambda n: (vb, n)),
            pl.BlockSpec((KV_DIM, WINDOW), lambda n: (vb, prev(n))),
        ],
        out_specs=[pl.BlockSpec((Q_DIM, WINDOW), lambda n: (0, n)), pl.BlockSpec((N_Q_HEADS, WINDOW), lambda n: (0, n))],
        out_shape=[jax.ShapeDtypeStruct((Q_DIM, t), F32), jax.ShapeDtypeStruct((N_Q_HEADS, t), F32)],
        compiler_params=_cparams("parallel"),
    )(sinks, proj, proj, proj, proj, proj)


def _attn_bwd(proj, sinks, out, lse, dout, send=None):
    t = proj.shape[1]
    nb = t // WINDOW
    scale = HEAD_DIM ** -0.5

    def body(s_ref, q_ref, kc_ref, kp_ref, vc_ref, vp_ref, o_ref, lse_ref, do_ref,
             dq_ref, dk_ref, dv_ref, ds_ref, dk_carry, dv_carry):
        step = pl.program_id(0)
        n = nb - 1 - step

        @pl.when(step == 0)
        def _():
            dk_carry[...] = jnp.zeros_like(dk_carry)
            dv_carry[...] = jnp.zeros_like(dv_carry)
            ds_ref[...] = jnp.zeros_like(ds_ref)

        valid = _attn_mask(n)
        for g in range(N_KV_HEADS):
            rows = slice(g * HEAD_DIM, (g + 1) * HEAD_DIM)
            q0 = g * Q_PER_KV * HEAD_DIM
            kt = jnp.concatenate([kp_ref[rows, :], kc_ref[rows, :]], axis=1).astype(BF16)
            vt = jnp.concatenate([vp_ref[rows, :], vc_ref[rows, :]], axis=1).astype(BF16)
            qf = _lane_cat(q_ref, q0, HEAD_DIM)
            qcat = qf.astype(BF16)
            ocat = _lane_cat(o_ref, q0, HEAD_DIM)
            docat = _lane_cat(do_ref, q0, HEAD_DIM)
            dob = docat.astype(BF16)
            lse_cat = jnp.concatenate(
                [lse_ref[g * Q_PER_KV + i:g * Q_PER_KV + i + 1, :] for i in range(Q_PER_KV)], axis=1)
            sink = jnp.concatenate(
                [jnp.full((1, WINDOW), s_ref[g * Q_PER_KV + i], F32) for i in range(Q_PER_KV)], axis=1)
            s = jnp.where(valid, _dot_tn(kt, (qf * scale).astype(BF16)), NEG)
            p = jnp.exp(s - lse_cat)
            dp = _dot_tn(vt, dob)
            delta = _colsum(docat * ocat)
            dsc = (p * (dp - delta)).astype(BF16)
            dsink_row = -jnp.exp(sink - lse_cat) * delta
            dq = _dot(kt, dsc) * scale
            dk = _dot_nt(qcat, dsc) * scale
            dv = _dot_nt(dob, p.astype(BF16))
            for i in range(Q_PER_KV):
                h = g * Q_PER_KV + i
                dq_ref[h * HEAD_DIM:(h + 1) * HEAD_DIM, :] = dq[:, i * WINDOW:(i + 1) * WINDOW].astype(BF16)
                ds_ref[h:h + 1, :] += _rowsum(dsink_row[:, i * WINDOW:(i + 1) * WINDOW])
            dk_ref[rows, :] = (dk[:, WINDOW:] + dk_carry[rows, :]).astype(BF16)
            dv_ref[rows, :] = (dv[:, WINDOW:] + dv_carry[rows, :]).astype(BF16)
            dk_carry[rows, :] = dk[:, :WINDOW]
            dv_carry[rows, :] = dv[:, :WINDOW]

    kb = OFF_K // KV_DIM
    vb = OFF_V // KV_DIM
    cur = lambda i: nb - 1 - i
    prev = lambda i: jnp.maximum(nb - 2 - i, 0)
    qspec = pl.BlockSpec((Q_DIM, WINDOW), lambda i: (0, cur(i)))
    kvspec = pl.BlockSpec((KV_DIM, WINDOW), lambda i: (0, cur(i)))
    return _call(
        body,
        name="attn_bwd",
        grid=(nb,),
        in_specs=[
            pl.BlockSpec(memory_space=pltpu.SMEM),
            qspec,
            pl.BlockSpec((KV_DIM, WINDOW), lambda i: (kb, cur(i))),
            pl.BlockSpec((KV_DIM, WINDOW), lambda i: (kb, prev(i))),
            pl.BlockSpec((KV_DIM, WINDOW), lambda i: (vb, cur(i))),
            pl.BlockSpec((KV_DIM, WINDOW), lambda i: (vb, prev(i))),
            qspec,
            pl.BlockSpec((N_Q_HEADS, WINDOW), lambda i: (0, cur(i))),
            qspec,
        ],
        out_specs=[qspec, kvspec, kvspec, pl.BlockSpec((N_Q_HEADS, 1), lambda i: (0, 0))],
        out_shape=[
            jax.ShapeDtypeStruct((Q_DIM, t), BF16),
            jax.ShapeDtypeStruct((KV_DIM, t), BF16),
            jax.ShapeDtypeStruct((KV_DIM, t), BF16),
            jax.ShapeDtypeStruct((N_Q_HEADS, 1), F32),
        ],
        scratch_shapes=[pltpu.VMEM((KV_DIM, WINDOW), F32), pltpu.VMEM((KV_DIM, WINDOW), F32)],
        semantics=("arbitrary",), args=(sinks, proj, proj, proj, proj, proj, out, lse, dout), send=send)


CONV_ROWS = 256


def _conv_silu_fwd(proj, w_col, b_col):
    t = proj.shape[1]
    r0 = OFF_X // CONV_ROWS

    def body(x_ref, w_ref, b_ref, o_ref):
        def strip(rows):
            y, _ = _causal_conv(x_ref[rows, :], w_ref[rows, :], b_ref[rows, :])
            o_ref[rows, :] = y * _sigmoid(y)

        strip(slice(None))

    return pl.pallas_call(
        body,
        name="ssd_conv_fwd",
        grid=(XBC_DIM // CONV_ROWS,),
        in_specs=[
            pl.BlockSpec((CONV_ROWS, t), lambda i: (r0 + i, 0)),
            pl.BlockSpec((CONV_ROWS, SSD_CONV), lambda i: (i, 0)),
            pl.BlockSpec((CONV_ROWS, 1), lambda i: (i, 0)),
        ],
        out_specs=pl.BlockSpec((CONV_ROWS, t), lambda i: (i, 0)),
        out_shape=jax.ShapeDtypeStruct((XBC_DIM, t), F32),
        compiler_params=_cparams("parallel"),
    )(proj, w_col, b_col)


def _conv_silu_bwd(proj, w_col, b_col, dout, row0, name):
    t = proj.shape[1]
    nrows = dout.shape[0]
    p0 = (OFF_X + row0) // CONV_ROWS
    c0 = row0 // CONV_ROWS

    def body(x_ref, w_ref, b_ref, do_ref, dx_ref, dwb_ref):
        def strip(rows):
            xv = x_ref[rows, :]
            wv = w_ref[rows, :]
            y, shifted = _causal_conv(xv, wv, b_ref[rows, :])
            sg = _sigmoid(y)
            dy = do_ref[rows, :] * (sg * (1.0 + y * (1.0 - sg)))
            dx, dwb_ref[rows, :] = _causal_conv_bwd(dy, xv, shifted, wv)
            dx_ref[rows, :] = dx.astype(BF16)

        strip(slice(None))

    return pl.pallas_call(
        body,
        name=name,
        grid=(nrows // CONV_ROWS,),
        in_specs=[
            pl.BlockSpec((CONV_ROWS, t), lambda i: (p0 + i, 0)),
            pl.BlockSpec((CONV_ROWS, SSD_CONV), lambda i: (c0 + i, 0)),
            pl.BlockSpec((CONV_ROWS, 1), lambda i: (c0 + i, 0)),
            pl.BlockSpec((CONV_ROWS, t), lambda i: (i, 0)),
        ],
        out_specs=[pl.BlockSpec((CONV_ROWS, t), lambda i: (i, 0)), pl.BlockSpec((CONV_ROWS, 128), lambda i: (i, 0))],
        out_shape=[jax.ShapeDtypeStruct((nrows, t), BF16), jax.ShapeDtypeStruct((nrows, 128), F32)],
        compiler_params=_cparams("parallel"),
    )(proj, w_col, b_col, dout)


GROUP_ROWS = HEADS_PER_GROUP * SSD_HEAD_DIM


def _ssd_specs(order):
    xb = D_INNER // BC_DIM
    dtb = OFF_DT // N_SSD_HEADS
    col = pl.BlockSpec((N_SSD_HEADS, 1), lambda c: (0, 0))
    return [
        pl.BlockSpec((D_INNER, CHUNK), lambda c: (0, order(c))),
        pl.BlockSpec((BC_DIM, CHUNK), lambda c: (xb, order(c))),
        pl.BlockSpec((BC_DIM, CHUNK), lambda c: (xb + 1, order(c))),
        pl.BlockSpec((N_SSD_HEADS, CHUNK), lambda c: (dtb, order(c))),
        col, col, col,
    ]


def _ssd_common(dt_ref, dtb_ref, alog_ref):
    z = dt_ref[...] + dtb_ref[...]
    dt = _softplus(z)
    a_neg = -jnp.exp(alog_ref[...])
    d_a = dt * a_neg
    row = lax.broadcasted_iota(jnp.int32, (CHUNK, CHUNK), 0)
    colm = lax.broadcasted_iota(jnp.int32, (CHUNK, CHUNK), 1)
    upper = (row <= colm).astype(F32)
    a_cs = jnp.dot(d_a, upper, precision=HIGHEST, preferred_element_type=F32)
    a_last = _rowsum(d_a)
    return z, dt, a_neg, a_cs, a_last, row >= colm, row == colm


def _decay(a_row, causal):
    a_s = jnp.broadcast_to(a_row, (CHUNK, CHUNK))
    seg = a_s.T - a_s
    return jnp.where(causal, jnp.exp(jnp.where(causal, seg, 0.0)), 0.0)


def _ssd_fwd(xbc, proj, dtb_col, alog_col, dsk_col):
    t = xbc.shape[1]
    nc = t // CHUNK

    def body(xs_ref, b_ref, c_ref, dt_ref, dtb_ref, alog_ref, dsk_ref, y_ref, hst_ref, h_scr):
        @pl.when(pl.program_id(0) == 0)
        def _():
            h_scr[...] = jnp.zeros_like(h_scr)

        _, dt, _, a_cs, a_last, causal, _ = _ssd_common(dt_ref, dtb_ref, alog_ref)
        hst_ref[0] = h_scr[...]
        dsk = dsk_ref[...]
        for g in range(N_SSD_GROUPS):
            grows = slice(g * D_STATE, (g + 1) * D_STATE)
            bb = b_ref[grows, :].astype(BF16)
            cb_ = c_ref[grows, :].astype(BF16)
            cb = _dot_tn(cb_, bb)
            for j in range(g * HEADS_PER_GROUP, (g + 1) * HEADS_PER_GROUP):
                rows = slice(j * SSD_HEAD_DIM, (j + 1) * SSD_HEAD_DIM)
                a = a_cs[j:j + 1, :]
                m = (cb * _decay(a, causal)).astype(BF16)
                xs = xs_ref[rows, :]
                xc = xs * dt[j:j + 1, :]
                hj = h_scr[rows, :]
                y = _dot_nt(xc.astype(BF16), m) + _dot(hj.astype(BF16), cb_) * jnp.exp(a) + dsk[j:j + 1, :] * xs
                y_ref[rows, :] = y
                al = a_last[j:j + 1, :]
                w = jnp.exp(al - a)
                h_scr[rows, :] = jnp.exp(al) * hj + _dot_nt((xc * w).astype(BF16), bb)

    return pl.pallas_call(
        body,
        name="ssd_fwd",
        grid=(nc,),
        in_specs=_ssd_specs(lambda c: c),
        out_specs=[
            pl.BlockSpec((D_INNER, CHUNK), lambda c: (0, c)),
            pl.BlockSpec((1, D_INNER, D_STATE), lambda c: (c, 0, 0)),
        ],
        out_shape=[
            jax.ShapeDtypeStruct((D_INNER, t), F32),
            jax.ShapeDtypeStruct((nc, D_INNER, D_STATE), F32),
        ],
        scratch_shapes=[pltpu.VMEM((D_INNER, D_STATE), F32)],
        compiler_params=_cparams("arbitrary"),
    )(xbc, xbc, xbc, proj, dtb_col, alog_col, dsk_col)


def _ssd_bwd(xbc, proj, dtb_col, alog_col, dsk_col, hst, dy):
    t = xbc.shape[1]
    nc = t // CHUNK
    rev = lambda c: nc - 1 - c

    def body(xs_ref, b_ref, c_ref, dt_ref, dtb_ref, alog_ref, dsk_ref, hst_ref, dy_ref,
             dxs_ref, db_ref, dc_ref, ddt_ref, dalog_ref, ddsk_ref, ddtb_ref, dh_scr, da_scr, ddt_scr, dd_scr):
        @pl.when(pl.program_id(0) == 0)
        def _():
            dh_scr[...] = jnp.zeros_like(dh_scr)
            dalog_ref[...] = jnp.zeros_like(dalog_ref)
            ddsk_ref[...] = jnp.zeros_like(ddsk_ref)
            ddtb_ref[...] = jnp.zeros_like(ddtb_ref)

        z, dt, a_neg, a_cs, a_last, causal, eye = _ssd_common(dt_ref, dtb_ref, alog_ref)
        dsk = dsk_ref[...]
        last_lane = lax.broadcasted_iota(jnp.int32, (1, CHUNK), 1) == CHUNK - 1
        for g in range(N_SSD_GROUPS):
            grows = slice(g * D_STATE, (g + 1) * D_STATE)
            bb = b_ref[grows, :].astype(BF16)
            cb_ = c_ref[grows, :].astype(BF16)
            cb = _dot_tn(cb_, bb)
            dcb = jnp.zeros((CHUNK, CHUNK), F32)
            dc_acc = jnp.zeros((D_STATE, CHUNK), F32)
            db_acc = jnp.zeros((D_STATE, CHUNK), F32)
            for j in range(g * HEADS_PER_GROUP, (g + 1) * HEADS_PER_GROUP):
                rows = slice(j * SSD_HEAD_DIM, (j + 1) * SSD_HEAD_DIM)
                a = a_cs[j:j + 1, :]
                al = a_last[j:j + 1, :]
                lam = _decay(a, causal)
                mf = cb * lam
                xs = xs_ref[rows, :]
                dtj = dt[j:j + 1, :]
                xc = xs * dtj
                w = jnp.exp(al - a)
                e = jnp.exp(a)
                gam = jnp.exp(al)
                hj = hst_ref[0, rows, :]
                hjb = hj.astype(BF16)
                dyv = dy_ref[rows, :]
                dyb = dyv.astype(BF16)
                dd_scr[j:j + 1, :] = _colsum(dyv * xs)
                gb = (dyv * e).astype(BF16)
                dh_in = _dot_nt(gb, cb_)
                dc_acc = dc_acc + _dot_tn(hjb, gb)
                yoff = _dot(hjb, cb_) * e
                da = _colsum(dyv * yoff)
                dm = _dot_tn(dyb, xc.astype(BF16))
                dxc = _dot(dyb, mf.astype(BF16))
                dcb = dcb + dm * lam
                nmat = dm * mf
                rs = jnp.broadcast_to(_rowsum(nmat), (CHUNK, CHUNK))
                da = da + _colsum(jnp.where(eye, rs, 0.0)) - _colsum(nmat)
                ds = dh_scr[rows, :]
                dsb = ds.astype(BF16)
                t1 = _dot(dsb, bb)
                xcw = xc * w
                dxc = dxc + w * t1
                dww = _colsum(xcw * t1)
                da_l = _rowsum(dww) + _rowsum(_colsum(ds * hj)) * gam
                da = da - dww + jnp.where(last_lane, da_l, 0.0)
                db_acc = db_acc + _dot_tn(dsb, xcw.astype(BF16))
                dh_scr[rows, :] = gam * ds + dh_in
                dxs_ref[rows, :] = dsk[j:j + 1, :] * dyv + dxc * dtj
                da_scr[j:j + 1, :] = da
                ddt_scr[j:j + 1, :] = _colsum(dxc * xs)
            dcbb = dcb.astype(BF16)
            dc_ref[grows, :] = dc_acc + _dot_nt(bb, dcbb)
            db_ref[grows, :] = db_acc + _dot(cb_, dcbb)
        dda = jnp.dot(da_scr[...], causal.astype(F32), precision=HIGHEST, preferred_element_type=F32)
        ddt = ddt_scr[...] + dda * a_neg
        ddt_raw = ddt * _sigmoid(z)
        ddt_ref[...] = ddt_raw
        ddtb_ref[...] += _rowsum(ddt_raw)
        dalog_ref[...] += _rowsum(dda * dt) * a_neg
        ddsk_ref[...] += _rowsum(dd_scr[...])

    col = pl.BlockSpec((N_SSD_HEADS, 1), lambda c: (0, 0))
    bc = pl.BlockSpec((BC_DIM, CHUNK), lambda c: (0, rev(c)))
    xs_spec = pl.BlockSpec((D_INNER, CHUNK), lambda c: (0, rev(c)))
    small = pltpu.VMEM((N_SSD_HEADS, CHUNK), F32)
    return pl.pallas_call(
        body,
        name="ssd_bwd",
        grid=(nc,),
        in_specs=_ssd_specs(rev) + [pl.BlockSpec((1, D_INNER, D_STATE), lambda c: (rev(c), 0, 0)), xs_spec],
        out_specs=[xs_spec, bc, bc, pl.BlockSpec((N_SSD_HEADS, CHUNK), lambda c: (0, rev(c))), col, col, col],
        out_shape=[
            jax.ShapeDtypeStruct((D_INNER, t), F32),
            jax.ShapeDtypeStruct((BC_DIM, t), F32),
            jax.ShapeDtypeStruct((BC_DIM, t), F32),
            jax.ShapeDtypeStruct((N_SSD_HEADS, t), F32),
            jax.ShapeDtypeStruct((N_SSD_HEADS, 1), F32),
            jax.ShapeDtypeStruct((N_SSD_HEADS, 1), F32),
            jax.ShapeDtypeStruct((N_SSD_HEADS, 1), F32),
        ],
        scratch_shapes=[pltpu.VMEM((D_INNER, D_STATE), F32), small, small, small],
        compiler_params=_cparams("arbitrary"),
    )(xbc, xbc, xbc, proj, dtb_col, alog_col, dsk_col, hst, dy)


GN_ROWS = D_INNER // N_SSD_GROUPS


def _gnorm_fwd(y, proj, w_col):
    t = y.shape[1]
    tt = _tile(t, (512, 256, 128))
    z0 = OFF_Z // GN_ROWS

    def body(y_ref, z_ref, w_ref, o_ref):
        zv = z_ref[...]
        u = y_ref[...] * (zv * _sigmoid(zv))
        r = lax.rsqrt(jnp.mean(u * u, axis=0, keepdims=True) + EPS)
        o_ref[...] = (u * r * w_ref[...]).astype(BF16)

    blk = pl.BlockSpec((GN_ROWS, tt), lambda g, i: (g, i))
    return pl.pallas_call(
        body,
        name="gnorm_fwd",
        grid=(N_SSD_GROUPS, t // tt),
        in_specs=[blk, pl.BlockSpec((GN_ROWS, tt), lambda g, i: (z0 + g, i)), pl.BlockSpec((GN_ROWS, 1), lambda g, i: (g, 0))],
        out_specs=blk,
        out_shape=jax.ShapeDtypeStruct((D_INNER, t), BF16),
        compiler_params=_cparams("parallel", "parallel"),
    )(y, proj, w_col)


def _gnorm_bwd(dout, y, proj, w_col, send=None):
    t = y.shape[1]
    tt = _tile(t, (512, 256, 128))
    z0 = OFF_Z // GN_ROWS

    def body(do_ref, y_ref, z_ref, w_ref, dy_ref, dz_ref, dw_ref):
        @pl.when(pl.program_id(1) == 0)
        def _():
            dw_ref[...] = jnp.zeros_like(dw_ref)

        zv = z_ref[...]
        yv = y_ref[...]
        sg = _sigmoid(zv)
        sz = zv * sg
        u = yv * sz
        r = lax.rsqrt(jnp.mean(u * u, axis=0, keepdims=True) + EPS)
        xhat = u * r
        dov = do_ref[...]
        dw_ref[...] += _rowsum(dov * xhat)
        dxhat = dov * w_ref[...]
        du = r * (dxhat - xhat * jnp.mean(dxhat * xhat, axis=0, keepdims=True))
        dy_ref[...] = du * sz
        dz_ref[...] = (du * yv * (sg * (1.0 + zv * (1.0 - sg)))).astype(BF16)

    blk = pl.BlockSpec((GN_ROWS, tt), lambda g, i: (g, i))
    col = pl.BlockSpec((GN_ROWS, 1), lambda g, i: (g, 0))
    return _call(
        body,
        name="gnorm_bwd",
        grid=(N_SSD_GROUPS, t // tt),
        in_specs=[blk, blk, pl.BlockSpec((GN_ROWS, tt), lambda g, i: (z0 + g, i)), col],
        out_specs=[blk, blk, col],
        out_shape=[jax.ShapeDtypeStruct((D_INNER, t), F32), jax.ShapeDtypeStruct((D_INNER, t), BF16),
                   jax.ShapeDtypeStruct((D_INNER, 1), F32)],
        semantics=("parallel", "arbitrary"), args=(dout, y, proj, w_col), send=send)


GATE_ROWS = 32


def _gate_specs(t):
    ga0 = OFF_GA // GATE_ROWS
    gs0 = OFF_GS // GATE_ROWS
    nr = D_MODEL // GATE_ROWS
    blk = pl.BlockSpec((GATE_ROWS, t), lambda r: (r, 0))
    return blk, [
        pl.BlockSpec((GATE_ROWS, t), lambda r: (ga0 + r, 0)),
        pl.BlockSpec((GATE_ROWS, t), lambda r: (gs0 + r, 0)),
        pl.BlockSpec((GATE_ROWS, 1), lambda r: (r, 0)),
        pl.BlockSpec((GATE_ROWS, 1), lambda r: (nr + r, 0)),
        blk, blk,
    ]


def _gate_fwd(proj, b_col, attn, ssd):
    t = proj.shape[1]
    blk, specs = _gate_specs(t)

    def body(ga_ref, gs_ref, ba_ref, bs_ref, a_ref, s_ref, o_ref):
        o_ref[...] = (_sigmoid(ga_ref[...] + ba_ref[...]) * a_ref[...]
                      + _sigmoid(gs_ref[...] + bs_ref[...]) * s_ref[...]).astype(BF16)

    return pl.pallas_call(
        body,
        name="gate_fwd",
        grid=(D_MODEL // GATE_ROWS,),
        in_specs=specs,
        out_specs=blk,
        out_shape=jax.ShapeDtypeStruct((D_MODEL, t), BF16),
        compiler_params=_cparams("parallel"),
    )(proj, proj, b_col, b_col, attn, ssd)


def _gate_bwd(proj, b_col, attn, ssd, dmix, send=None):
    t = proj.shape[1]
    blk, specs = _gate_specs(t)

    def body(ga_ref, gs_ref, ba_ref, bs_ref, a_ref, s_ref, dm_ref, da_ref, dso_ref, dga_ref, dgs_ref, dba_ref, dbs_ref):
        dm = dm_ref[...]
        sa = _sigmoid(ga_ref[...] + ba_ref[...])
        ss = _sigmoid(gs_ref[...] + bs_ref[...])
        da_ref[...] = (dm * sa).astype(BF16)
        dso_ref[...] = (dm * ss).astype(BF16)
        dga = dm * a_ref[...] * sa * (1.0 - sa)
        dgs = dm * s_ref[...] * ss * (1.0 - ss)
        dga_ref[...] = dga.astype(BF16)
        dgs_ref[...] = dgs.astype(BF16)
        dba_ref[...] = _rowsum(dga)
        dbs_ref[...] = _rowsum(dgs)

    col = pl.BlockSpec((GATE_ROWS, 1), lambda r: (r, 0))
    act = jax.ShapeDtypeStruct((D_MODEL, t), BF16)
    bias = jax.ShapeDtypeStruct((D_MODEL, 1), F32)
    return _call(
        body,
        name="gate_bwd",
        grid=(D_MODEL // GATE_ROWS,),
        in_specs=specs + [blk],
        out_specs=[blk, blk, blk, blk, col, col],
        out_shape=[act, act, act, act, bias, bias],
        semantics=("parallel",), args=(proj, proj, b_col, b_col, attn, ssd, dmix), send=send)


FFN_ROWS = 256


def _ffn_fwd(u0, w_col, b_col):
    t = u0.shape[2]

    def body(u_ref, w_ref, b_ref, o_ref):
        def strip(rows):
            val, _ = _causal_conv(u_ref[0, rows, :], w_ref[0, rows, :], b_ref[0, rows, :])
            gt, _ = _causal_conv(u_ref[1, rows, :], w_ref[1, rows, :], b_ref[1, rows, :])
            o_ref[rows, :] = (gt * _sigmoid(gt) * val).astype(BF16)

        strip(slice(None))

    return pl.pallas_call(
        body,
        name="ffn_fwd",
        grid=(D_FF // FFN_ROWS,),
        in_specs=[
            pl.BlockSpec((2, FFN_ROWS, t), lambda i: (0, i, 0)),
            pl.BlockSpec((2, FFN_ROWS, FFN_CONV), lambda i: (0, i, 0)),
            pl.BlockSpec((2, FFN_ROWS, 1), lambda i: (0, i, 0)),
        ],
        out_specs=pl.BlockSpec((FFN_ROWS, t), lambda i: (i, 0)),
        out_shape=jax.ShapeDtypeStruct((D_FF, t), BF16),
        compiler_params=_cparams("parallel"),
    )(u0, w_col, b_col)


def _ffn_bwd(u0, w_col, b_col, dg, send=None):
    t = u0.shape[2]

    def body(u_ref, w_ref, b_ref, dg_ref, du_ref, dwb_ref):
        def strip(rows):
            xval, wval = u_ref[0, rows, :], w_ref[0, rows, :]
            xgt, wgt = u_ref[1, rows, :], w_ref[1, rows, :]
            val, sh_val = _causal_conv(xval, wval, b_ref[0, rows, :])
            gt, sh_gt = _causal_conv(xgt, wgt, b_ref[1, rows, :])
            sg = _sigmoid(gt)
            dgv = dg_ref[rows, :]
            dval = dgv * (gt * sg)
            dgt = dgv * val * (sg * (1.0 + gt * (1.0 - sg)))
            dx, dwb_ref[0, rows, :] = _causal_conv_bwd(dval, xval, sh_val, wval)
            du_ref[0, rows, :] = dx.astype(BF16)
            dx, dwb_ref[1, rows, :] = _causal_conv_bwd(dgt, xgt, sh_gt, wgt)
            du_ref[1, rows, :] = dx.astype(BF16)

        strip(slice(None))

    return _call(
        body,
        name="ffn_bwd",
        grid=(D_FF // FFN_ROWS,),
        in_specs=[
            pl.BlockSpec((2, FFN_ROWS, t), lambda i: (0, i, 0)),
            pl.BlockSpec((2, FFN_ROWS, FFN_CONV), lambda i: (0, i, 0)),
            pl.BlockSpec((2, FFN_ROWS, 1), lambda i: (0, i, 0)),
            pl.BlockSpec((FFN_ROWS, t), lambda i: (i, 0)),
        ],
        out_specs=[pl.BlockSpec((2, FFN_ROWS, t), lambda i: (0, i, 0)), pl.BlockSpec((2, FFN_ROWS, 128), lambda i: (0, i, 0))],
        out_shape=[jax.ShapeDtypeStruct((2, D_FF, t), BF16), jax.ShapeDtypeStruct((2, D_FF, 128), F32)],
        semantics=("parallel",), args=(u0, w_col, b_col, dg), send=send)


def _adamw_math(w, g, m, v):
    m = ADAM_B1 * m + (1.0 - ADAM_B1) * g
    v = ADAM_B2 * v + (1.0 - ADAM_B2) * (g * g)
    m_hat = m / (1.0 - ADAM_B1 ** ADAM_STEP)
    v_hat = v / (1.0 - ADAM_B2 ** ADAM_STEP)
    delta = -ADAM_LR * (m_hat / (jnp.sqrt(v_hat) + ADAM_EPS) + ADAM_WD * w)
    return delta, m, v


def _adamw_sharded(parts, w, m, v, name):
    r, c = w.shape[0], w.shape[-1]
    tc = _tile(c, (256, 128))
    blk_shape = (r, tc) if w.ndim == 2 else (r, 1, tc)

    def body(p_ref, w_ref, m_ref, v_ref, g_ref, d_ref, nm_ref, nv_ref):
        g = p_ref[0].astype(F32)
        for s in range(1, N_DEV):
            g = g + p_ref[s].astype(F32)
        flat = lambda ref: ref[...].reshape(r, tc)
        d, nm, nv = _adamw_math(flat(w_ref), g, flat(m_ref), flat(v_ref))
        for ref, val in ((g_ref, g), (d_ref, d), (nm_ref, nm), (nv_ref, nv)):
            ref[...] = val.reshape(blk_shape)

    blk = pl.BlockSpec(blk_shape, (lambda i: (0, i)) if w.ndim == 2 else (lambda i: (0, 0, i)))
    out = jax.ShapeDtypeStruct(w.shape, F32)
    return pl.pallas_call(
        body,
        name=name,
        grid=(c // tc,),
        in_specs=[pl.BlockSpec((N_DEV, r, tc), lambda i: (0, 0, i)), blk, blk, blk],
        out_specs=[blk, blk, blk, blk],
        out_shape=[out, out, out, out],
        compiler_params=_cparams("parallel"),
    )(parts, w, m, v)


def _lane_offsets(sizes):
    offsets, pos = [], 0
    for n in sizes:
        offsets.append(pos)
        pos += -(-n // 128) * 128
    return offsets, pos


def _pack_row(parts):
    rows = [p.reshape(1, -1).astype(F32) for p in parts]
    return jnp.concatenate([jnp.pad(r, ((0, 0), (0, -r.shape[1] % 128))) for r in rows], axis=1)


def _small_update(parts, me, full_sizes, ws, ms, vs):
    n = len(ws)
    offsets, _ = _lane_offsets([1] + list(full_sizes))

    def body(me_ref, p_ref, *refs):
        w_refs, m_refs, v_refs = refs[:n], refs[n:2 * n], refs[2 * n:3 * n]
        scalar_ref, out_refs = refs[3 * n], refs[3 * n + 1:]
        tot = p_ref[0]
        for s in range(1, N_DEV):
            tot = tot + p_ref[s]
        scalar_ref[...] = tot[:, 0:1]
        for k in range(n):
            g_ref, d_ref, nm_ref, nv_ref = out_refs[4 * k:4 * k + 4]
            taps, cols = w_refs[k].shape
            if taps == 1:
                g_ref[...] = tot[:, offsets[k + 1]:offsets[k + 1] + cols]
            else:
                full = full_sizes[k] // taps
                for tap in range(taps):
                    mine = jnp.zeros((1, cols), F32)
                    for d in range(N_DEV):
                        lo = offsets[k + 1] + tap * full + d * cols
                        mine = jnp.where(me_ref[0] == d, tot[:, lo:lo + cols], mine)
                    g_ref[tap:tap + 1, :] = mine
            d_ref[...], nm_ref[...], nv_ref[...] = _adamw_math(w_refs[k][...], g_ref[...], m_refs[k][...], v_refs[k][...])

    vmem = pl.BlockSpec(memory_space=pltpu.VMEM)
    out_shape = [jax.ShapeDtypeStruct((1, 1), F32)]
    for wk in ws:
        out_shape += [jax.ShapeDtypeStruct(wk.shape, F32)] * 4
    res = pl.pallas_call(
        body,
        name="small_update",
        in_specs=[pl.BlockSpec(memory_space=pltpu.SMEM)] + [vmem] * (1 + 3 * n),
        out_specs=[vmem] * len(out_shape),
        out_shape=out_shape,
    )(me, parts, *ws, *ms, *vs)
    return res[0], [res[1 + 4 * k:5 + 4 * k] for k in range(n)]


ANY = pl.BlockSpec(memory_space=pl.ANY)
FLIPS = [(k >> 2 & 1, k >> 1 & 1, k & 1) for k in range(1, N_DEV)]


def _place():
    return lax.axis_index("x"), lax.axis_index("y"), lax.axis_index("c")


HBM = pl.BlockSpec(memory_space=pltpu.HBM)
SEM = pl.BlockSpec(memory_space=pltpu.SEMAPHORE)
EFFECT = pltpu.SideEffectType.DATAFLOW_SIDE_EFFECTING


def _peer_copy(gather, src_ref, land_ref, send_sems, recv_sems, k, sending):
    x, y, c = _place()
    fx, fy, fc = FLIPS[k]
    me = 4 * x + 2 * y + c
    peer = 4 * (x ^ fx) + 2 * (y ^ fy) + (c ^ fc)
    return pltpu.make_async_remote_copy(
        src_ref=src_ref if gather else src_ref.at[peer],
        dst_ref=land_ref.at[me if sending else peer],
        send_sem=send_sems.at[k], recv_sem=recv_sems.at[k],
        device_id=(x ^ fx, y ^ fy, c ^ fc), device_id_type=MESH)


def _gather_start(srcs, name):
    n = len(srcs)
    lands = [lax.empty((N_DEV,) + s.shape, s.dtype) for s in srcs]

    def body(*refs):
        src_refs, land_refs = refs[:n], refs[n:2 * n]
        send, recv = refs[2 * n:3 * n], refs[3 * n:4 * n]
        local_sems = refs[6 * n]
        x, y, c = _place()
        me = 4 * x + 2 * y + c
        local = [pltpu.make_async_copy(src_refs[i], land_refs[i].at[me], local_sems.at[i]) for i in range(n)]
        for cp in local:
            cp.start()
        for i in range(n):
            for k in range(N_DEV - 1):
                _peer_copy(True, src_refs[i], land_refs[i], send[i], recv[i], k, True).start()
        for cp in local:
            cp.wait()

    sem = pltpu.SemaphoreType.DMA((N_DEV - 1,))
    hbm = lambda a: pltpu.HBM(a.shape, a.dtype)
    res = pl.pallas_call(
        body,
        name=name,
        in_specs=[HBM] * (2 * n),
        out_specs=[SEM] * (2 * n) + [HBM] * (2 * n),
        out_shape=[sem] * (2 * n) + [hbm(s) for s in srcs] + [hbm(a) for a in lands],
        input_output_aliases={i: 2 * n + i for i in range(2 * n)},
        scratch_shapes=[pltpu.SemaphoreType.DMA((n,))],
        compiler_params=pltpu.CompilerParams(has_side_effects=EFFECT),
    )(*[pltpu.with_memory_space_constraint(a, pltpu.HBM) for a in list(srcs) + lands])
    return res[:n], res[n:2 * n], res[2 * n:3 * n], res[3 * n:4 * n]


def _exchange_wait(send_sems, recv_sems, src, land, after, gather, name):
    def body(src_ref, land_ref, send_ref, recv_ref, after_ref, src_out, land_out):
        for k in range(N_DEV - 1):
            cp = _peer_copy(gather, src_ref, land_ref, send_ref, recv_ref, k, False)
            cp.wait_send()
            cp.wait_recv()

    hbm = lambda a: pltpu.HBM(a.shape, a.dtype)
    return pl.pallas_call(
        body,
        name=name,
        in_specs=[HBM, HBM, SEM, SEM, ANY],
        out_specs=[HBM, HBM],
        out_shape=[hbm(src), hbm(land)],
        input_output_aliases={0: 0, 1: 1},
        compiler_params=pltpu.CompilerParams(has_side_effects=EFFECT),
    )(src, land, send_sems, recv_sems, after)[1]


def _col(v):
    return v.reshape(-1, 1).astype(F32)


def _local_step(xt, tgt, weight, small):
    t = xt.shape[1]
    n1 = _col(small["norm1_w"])
    n2 = _col(small["norm2_w"])
    nf = _col(small["final_norm_w"])
    bg = _col(small["b_gate"])
    sinks = small["attn_sinks"].reshape(-1).astype(F32)
    cbias = _col(small["ssd_conv_b"])
    dtb = _col(small["dt_bias"])
    alog = _col(small["a_log"])
    dsk = _col(small["d_skip"])
    gnw = _col(small["ssd_norm_w"])
    fb = small["ffn_conv_b"].reshape(2, D_FF, 1)

    xn = _norm_fwd(xt, n1, "norm1_fwd")
    cw = weight("ssd_conv_w", xn).T
    fw = weight("ffn_conv_w", xn).T.reshape(2, D_FF, FFN_CONV)
    w_in_t = weight("w_in", xn)
    proj = _matmul(w_in_t, xn, nt=False, out_dtype=F32, name="mm_in")
    ao, lse = _attn_fwd(proj, sinks)
    w_ao = weight("w_attn_o", ao)
    attn = _matmul(w_ao, ao, nt=False, out_dtype=F32, name="mm_attn_o", tn_a=True)
    xbc = _conv_silu_fwd(proj, cw, cbias)
    y, hst = _ssd_fwd(xbc, proj, dtb, alog, dsk)
    yn = _gnorm_fwd(y, proj, gnw)
    w_so = weight("w_ssd_o", yn)
    ssd = _matmul(w_so, yn, nt=False, out_dtype=F32, name="mm_ssd_o", tn_a=True)
    mix = _gate_fwd(proj, bg, attn, ssd)
    w_out = weight("w_out", mix)
    h1 = _matmul(w_out, mix, nt=False, out_dtype=F32, name="mm_out", add=xt, tn_a=True)
    hn = _norm_fwd(h1, n2, "norm2_fwd")
    w_up_t = weight("w_up", hn)
    u0 = _matmul(w_up_t, hn, nt=False, out_dtype=F32, name="mm_up").reshape(2, D_FF, t)
    gl = _ffn_fwd(u0, fw, fb)
    w_down = weight("w_down", gl)
    h2 = _matmul(w_down, gl, nt=False, out_dtype=F32, name="mm_down", add=h1, tn_a=True)
    dh2, loss, d_nf = _final_norm_loss(h2, tgt, nf)

    g = {}
    handles = {}

    def sending(weight_name, grad, fn, *args, **kwargs):
        out, handles[weight_name] = fn(*args, send=grad.reshape(N_DEV, -1, D_MODEL), **kwargs)
        return out

    g_down = _matmul(gl, dh2, nt=True, out_dtype=BF16, name="mm_d_w_down")
    dgl = _matmul(w_down, dh2, nt=False, out_dtype=F32, name="mm_d_glu")
    du0, d_fwb = sending("w_down", g_down, _ffn_bwd, u0, fw, fb, dgl)
    du0 = du0.reshape(2 * D_FF, t)
    g_up = _matmul(du0, hn, nt=True, out_dtype=BF16, name="mm_d_w_up")
    dhn = sending("w_up", g_up, _matmul, w_up_t, du0, nt=False, out_dtype=F32, name="mm_d_hn", tn_a=True)
    dh1, d_n2 = _norm_bwd(dhn, h1, n2, dh2, "norm2_bwd")
    g_out = _matmul(mix, dh1, nt=True, out_dtype=BF16, name="mm_d_w_out")
    dmix = _matmul(w_out, dh1, nt=False, out_dtype=F32, name="mm_d_mix")
    d_attn, d_ssd, d_ga, d_gs, d_ba, d_bs = sending("w_out", g_out, _gate_bwd, proj, bg, attn, ssd, dmix)
    g_ao = _matmul(ao, d_attn, nt=True, out_dtype=BF16, name="mm_d_w_attn_o")
    dao = _matmul(w_ao, d_attn, nt=False, out_dtype=F32, name="mm_d_ao")
    dq, dk, dv, d_sinks = sending("w_attn_o", g_ao, _attn_bwd, proj, sinks, ao, lse, dao)
    g_so = _matmul(yn, d_ssd, nt=True, out_dtype=BF16, name="mm_d_w_ssd_o")
    dyn = _matmul(w_so, d_ssd, nt=False, out_dtype=F32, name="mm_d_yn")
    dy, dz, d_gnw = sending("w_ssd_o", g_so, _gnorm_bwd, dyn, y, proj, gnw)
    dxs, dbm, dcm, ddt, d_alog, d_dsk, d_dtb = _ssd_bwd(xbc, proj, dtb, alog, dsk, hst, dy)
    dx_xs, dwb_xs = _conv_silu_bwd(proj, cw, cbias, dxs, 0, "ssd_conv_bwd_x")
    dx_b, dwb_b = _conv_silu_bwd(proj, cw, cbias, dbm, D_INNER, "ssd_conv_bwd_b")
    dx_c, dwb_c = _conv_silu_bwd(proj, cw, cbias, dcm, D_INNER + BC_DIM, "ssd_conv_bwd_c")
    dwb_conv = jnp.concatenate([dwb_xs, dwb_b, dwb_c], axis=0)
    dproj = jnp.concatenate([dq, dk, dv, dz, dx_xs, dx_b, dx_c, ddt.astype(BF16), d_ga, d_gs], axis=0)
    g_in = _matmul(dproj, xn, nt=True, out_dtype=BF16, name="mm_d_w_in")
    dxn = sending("w_in", g_in, _matmul, w_in_t, dproj, nt=False, out_dtype=F32, name="mm_d_xn", tn_a=True)
    dx, d_n1 = _norm_bwd(dxn, xt, n1, dh1, "norm1_bwd")

    g["norm1_w"] = d_n1
    g["b_gate"] = jnp.concatenate([d_ba, d_bs], axis=0)
    g["attn_sinks"] = d_sinks
    g["ssd_conv_w"] = dwb_conv[:, :SSD_CONV].T
    g["ssd_conv_b"] = dwb_conv[:, SSD_CONV]
    g["dt_bias"] = d_dtb
    g["a_log"] = d_alog
    g["d_skip"] = d_dsk
    g["ssd_norm_w"] = d_gnw
    g["norm2_w"] = d_n2
    d_fwb = d_fwb.reshape(2 * D_FF, 128)
    g["ffn_conv_w"] = d_fwb[:, :FFN_CONV].T
    g["ffn_conv_b"] = d_fwb[:, FFN_CONV]
    g["final_norm_w"] = d_nf
    return loss, dx, g, handles


SMALL = ("norm1_w", "b_gate", "attn_sinks", "ssd_conv_w", "ssd_conv_b", "dt_bias", "a_log", "d_skip", "ssd_norm_w",
         "norm2_w", "ffn_conv_w", "ffn_conv_b", "final_norm_w")
WEIGHT_ORDER = ("norm1_w", "w_in", "b_gate", "attn_sinks", "w_attn_o", "ssd_conv_w", "ssd_conv_b", "dt_bias", "a_log",
                "d_skip", "ssd_norm_w", "w_ssd_o", "w_out", "norm2_w", "w_up", "ffn_conv_w", "ffn_conv_b", "w_down",
                "final_norm_w")


def kernel(x, norm1_w, w_in, b_gate, attn_sinks, w_attn_o, ssd_conv_w, ssd_conv_b, dt_bias, a_log, d_skip, ssd_norm_w, w_ssd_o, w_out, norm2_w, w_up, ffn_conv_w, ffn_conv_b, w_down, final_norm_w, loss_target, m_norm1_w, m_w_in, m_b_gate, m_attn_sinks, m_w_attn_o, m_ssd_conv_w, m_ssd_conv_b, m_dt_bias, m_a_log, m_d_skip, m_ssd_norm_w, m_w_ssd_o, m_w_out, m_norm2_w, m_w_up, m_ffn_conv_w, m_ffn_conv_b, m_w_down, m_final_norm_w, v_norm1_w, v_w_in, v_b_gate, v_attn_sinks, v_w_attn_o, v_ssd_conv_w, v_ssd_conv_b, v_dt_bias, v_a_log, v_d_skip, v_ssd_norm_w, v_w_ssd_o, v_w_out, v_norm2_w, v_w_up, v_ffn_conv_w, v_ffn_conv_b, v_w_down, v_final_norm_w):
    w = dict(norm1_w=norm1_w, w_in=w_in, b_gate=b_gate, attn_sinks=attn_sinks, w_attn_o=w_attn_o, ssd_conv_w=ssd_conv_w, ssd_conv_b=ssd_conv_b, dt_bias=dt_bias, a_log=a_log, d_skip=d_skip, ssd_norm_w=ssd_norm_w, w_ssd_o=w_ssd_o, w_out=w_out, norm2_w=norm2_w, w_up=w_up, ffn_conv_w=ffn_conv_w, ffn_conv_b=ffn_conv_b, w_down=w_down, final_norm_w=final_norm_w)
    m = dict(norm1_w=m_norm1_w, w_in=m_w_in, b_gate=m_b_gate, attn_sinks=m_attn_sinks, w_attn_o=m_w_attn_o, ssd_conv_w=m_ssd_conv_w, ssd_conv_b=m_ssd_conv_b, dt_bias=m_dt_bias, a_log=m_a_log, d_skip=m_d_skip, ssd_norm_w=m_ssd_norm_w, w_ssd_o=m_w_ssd_o, w_out=m_w_out, norm2_w=m_norm2_w, w_up=m_w_up, ffn_conv_w=m_ffn_conv_w, ffn_conv_b=m_ffn_conv_b, w_down=m_w_down, final_norm_w=m_final_norm_w)
    v = dict(norm1_w=v_norm1_w, w_in=v_w_in, b_gate=v_b_gate, attn_sinks=v_attn_sinks, w_attn_o=v_w_attn_o, ssd_conv_w=v_ssd_conv_w, ssd_conv_b=v_ssd_conv_b, dt_bias=v_dt_bias, a_log=v_a_log, d_skip=v_d_skip, ssd_norm_w=v_ssd_norm_w, w_ssd_o=v_w_ssd_o, w_out=v_w_out, norm2_w=v_norm2_w, w_up=v_w_up, ffn_conv_w=v_ffn_conv_w, ffn_conv_b=v_ffn_conv_b, w_down=v_w_down, final_norm_w=v_final_norm_w)
    me = 4 * lax.axis_index("x") + 2 * lax.axis_index("y") + lax.axis_index("c")

    shards = {"ssd_conv_w": ssd_conv_w[0], "ffn_conv_w": ffn_conv_w[0], "w_in": w_in[0].T.astype(BF16),
              "w_attn_o": w_attn_o[0].astype(BF16), "w_ssd_o": w_ssd_o[0].astype(BF16), "w_out": w_out[0].astype(BF16),
              "w_up": w_up[0].T.astype(BF16), "w_down": w_down[0].astype(BF16)}
    order = list(shards)
    g_send, g_recv, g_src, g_land = _gather_start(list(shards.values()), "gather_start")

    def weight(name, after):
        i = order.index(name)
        land = _exchange_wait(g_send[i], g_recv[i], g_src[i], g_land[i], after, True, "gather_wait_" + name)
        if name == "ssd_conv_w":
            return jnp.transpose(land, (1, 0, 2)).reshape(SSD_CONV, XBC_DIM)
        if name == "ffn_conv_w":
            return jnp.transpose(land, (1, 0, 2)).reshape(FFN_CONV, 2 * D_FF)
        return land.reshape(-1, D_MODEL)

    small = {k: w[k][0] if k != "final_norm_w" else w[k] for k in SMALL}
    loss, dx, g, pending = _local_step(x[0].T, loss_target[0].T, weight, small)

    packed = _pack_row([loss] + [g[k] for k in SMALL])
    s_send, s_recv, s_src, s_land = _gather_start([packed], "small_grads_start")

    res = {}
    after = s_src[0]
    for name in ("w_down", "w_up", "w_out", "w_attn_o", "w_ssd_o", "w_in"):
        parts = _exchange_wait(*pending[name], after, False, "grad_wait_" + name)
        view, back = {
            "w_in": (lambda a: jnp.transpose(a, (2, 0, 1)), lambda r: jnp.transpose(r, (1, 2, 0))),
            "w_up": (lambda a: a[0].T, lambda r: r.T[None]),
        }.get(name, (lambda a: a[0], lambda r: r[None]))
        res[name] = _adamw_sharded(parts, view(w[name]), view(m[name]), view(v[name]), "adamw_" + name)
        after = res[name][0]
        res[name] = [back(r) for r in res[name]]

    rows = _exchange_wait(s_send[0], s_recv[0], s_src[0], s_land[0], after, True, "small_grads_wait")
    flat = lambda a: a.reshape(-1, a.shape[-1])
    loss_sum, updates = _small_update(
        rows, me.reshape(1), [g[k].size for k in SMALL],
        [flat(w[k]) for k in SMALL], [flat(m[k]) for k in SMALL], [flat(v[k]) for k in SMALL])
    for k, upd in zip(SMALL, updates):
        res[k] = [u.reshape(w[k].shape) for u in upd]

    grad_x = dx.T[None]
    outs = [loss_sum.reshape(()), grad_x]
    for i in range(4):
        outs.extend(res[k][i] for k in WEIGHT_ORDER)
    return tuple(outs)
```

```python
import functools

import jax
import jax.numpy as jnp
from jax import lax
from jax.experimental import pallas as pl
from jax.experimental.pallas import tpu as pltpu

F32 = jnp.float32
BF16 = jnp.bfloat16
HIGHEST = lax.Precision.HIGHEST

D_MODEL = 1024
N_Q_HEADS = 16
N_KV_HEADS = 4
HEAD_DIM = 64
WINDOW = 128
Q_PER_KV = N_Q_HEADS // N_KV_HEADS
Q_DIM = N_Q_HEADS * HEAD_DIM
KV_DIM = N_KV_HEADS * HEAD_DIM
D_INNER = 2048
SSD_HEAD_DIM = 64
N_SSD_HEADS = 32
N_SSD_GROUPS = 4
HEADS_PER_GROUP = N_SSD_HEADS // N_SSD_GROUPS
D_STATE = 128
BC_DIM = N_SSD_GROUPS * D_STATE
XBC_DIM = D_INNER + 2 * BC_DIM
SSD_CONV = 4
CHUNK = 128
D_FF = 2816
FFN_CONV = 3
EPS = 1e-5
NEG = -1e30
IN_DIM = 8736
N_DEV = 8

OFF_Q = 0
OFF_K = OFF_Q + Q_DIM
OFF_V = OFF_K + KV_DIM
OFF_Z = OFF_V + KV_DIM
OFF_X = OFF_Z + D_INNER
OFF_DT = OFF_X + XBC_DIM
OFF_GA = OFF_DT + N_SSD_HEADS
OFF_GS = OFF_GA + D_MODEL

ADAM_LR = 0.001
ADAM_B1 = 0.9
ADAM_B2 = 0.999
ADAM_EPS = 1e-08
ADAM_WD = 0.01
ADAM_STEP = 10

VMEM_LIMIT = 48 * 1024 * 1024
MESH = pl.DeviceIdType.MESH


def _cparams(*sem):
    return pltpu.CompilerParams(dimension_semantics=sem, vmem_limit_bytes=VMEM_LIMIT)


def _tile(n, prefs):
    for p in prefs:
        if n % p == 0:
            return p
    return n


def _sigmoid(x):
    return 1.0 / (1.0 + jnp.exp(-x))


def _softplus(x):
    return jnp.maximum(x, 0.0) + jnp.log(1.0 + jnp.exp(-jnp.abs(x)))


def _rowsum(x):
    return jnp.sum(x, axis=1, keepdims=True)


def _colsum(x):
    return jnp.sum(x, axis=0, keepdims=True)


def _dot(a, b):
    return jnp.dot(a, b, preferred_element_type=F32)


def _dot_nt(a, b):
    return lax.dot_general(a, b, (((1,), (1,)), ((), ())), preferred_element_type=F32)


def _dot_tn(a, b):
    return lax.dot_general(a, b, (((0,), (0,)), ((), ())), preferred_element_type=F32)


def _shift_right(x, j):
    if j == 0:
        return x
    r = pltpu.roll(x, j, 1)
    lane = lax.broadcasted_iota(jnp.int32, (x.shape[0], 128), 1)
    return jnp.concatenate([jnp.where(lane >= j, r[:, :128], 0.0), r[:, 128:]], axis=1)


def _shift_left(x, j):
    if j == 0:
        return x
    n = x.shape[1]
    r = pltpu.roll(x, n - j, 1)
    lane = lax.broadcasted_iota(jnp.int32, (x.shape[0], 128), 1)
    return jnp.concatenate([r[:, :n - 128], jnp.where(lane < 128 - j, r[:, n - 128:], 0.0)], axis=1)


def _causal_conv(xv, wv, bv):
    taps = wv.shape[1]
    shifted = [_shift_right(xv, taps - 1 - k) for k in range(taps - 1)]
    y = bv + wv[:, taps - 1:taps] * xv
    for k in range(taps - 1):
        y = y + wv[:, k:k + 1] * shifted[k]
    return y, shifted


def _causal_conv_bwd(dy, xv, shifted, wv):
    taps = wv.shape[1]
    lane = lax.broadcasted_iota(jnp.int32, (dy.shape[0], 128), 1)
    dwb = jnp.where(lane == taps, _rowsum(dy), 0.0)
    dwb = jnp.where(lane == taps - 1, _rowsum(dy * xv), dwb)
    dx = wv[:, taps - 1:taps] * dy
    for k in range(taps - 1):
        dx = dx + wv[:, k:k + 1] * _shift_left(dy, taps - 1 - k)
        dwb = jnp.where(lane == k, _rowsum(dy * shifted[k]), dwb)
    return dx, dwb


def _call(body, *, name, grid, in_specs, out_specs, out_shape, args, semantics, scratch_shapes=(), send=None):
    if send is None:
        return pl.pallas_call(body, name=name, grid=grid, in_specs=in_specs, out_specs=out_specs, out_shape=out_shape,
                              scratch_shapes=list(scratch_shapes), compiler_params=_cparams(*semantics))(*args)
    single = not isinstance(out_specs, (list, tuple))
    out_specs, out_shape = ([out_specs], [out_shape]) if single else (list(out_specs), list(out_shape))
    n_in, n_out, n_scr = len(in_specs), len(out_specs), len(scratch_shapes)
    steps = 1
    for size in grid:
        steps *= size

    def sending(*refs):
        ins, (src_ref, land_ref) = refs[:n_in], refs[n_in:n_in + 2]
        outs = refs[n_in + 2:n_in + 2 + n_out]
        send_sems, recv_sems = refs[n_in + 2 + n_out:n_in + 4 + n_out]
        scratch, local_sem = refs[n_in + 6 + n_out:n_in + 6 + n_out + n_scr], refs[-1]
        x, y, c = _place()
        me = 4 * x + 2 * y + c
        local = pltpu.make_async_copy(src_ref.at[me], land_ref.at[me], local_sem)
        step = 0
        for axis, size in enumerate(grid):
            step = step * size + pl.program_id(axis)

        @pl.when(step == 0)
        def _():
            local.start()
            for peer in range(N_DEV - 1):
                _peer_copy(False, src_ref, land_ref, send_sems, recv_sems, peer, True).start()

        body(*ins, *outs, *scratch)

        @pl.when(step == steps - 1)
        def _():
            local.wait()

    sem = pltpu.SemaphoreType.DMA((N_DEV - 1,))
    hbm = pltpu.HBM(send.shape, send.dtype)
    res = pl.pallas_call(
        sending, name=name, grid=grid,
        in_specs=list(in_specs) + [HBM, HBM],
        out_specs=out_specs + [SEM, SEM, HBM, HBM],
        out_shape=out_shape + [sem, sem, hbm, hbm],
        input_output_aliases={n_in: n_out + 2, n_in + 1: n_out + 3},
        scratch_shapes=list(scratch_shapes) + [pltpu.SemaphoreType.DMA(())],
        compiler_params=pltpu.CompilerParams(dimension_semantics=("arbitrary",) * len(grid), vmem_limit_bytes=VMEM_LIMIT,
                                             has_side_effects=EFFECT),
    )(*args, pltpu.with_memory_space_constraint(send, pltpu.HBM),
      pltpu.with_memory_space_constraint(lax.empty(send.shape, send.dtype), pltpu.HBM))
    return (res[0] if single else list(res[:n_out])), tuple(res[n_out:])


MATMUL_VMEM_BUDGET = 36 * 1024 * 1024
MATMUL_MAX_TK = 3072


MATMUL_MAX_TM = 768


def _largest_tile(n, align, cap):
    return max(d for d in range(align, min(n, cap) + 1, align) if n % d == 0)


def _matmul_tiles(m, n, k, a_bytes, b_bytes, out_bytes, has_add, m_align, k_align):
    tm = _largest_tile(m, m_align, MATMUL_MAX_TM)
    tk = _largest_tile(k, k_align, MATMUL_MAX_TK)
    for tn in sorted({d for d in range(128, n + 1, 128) if n % d == 0}, reverse=True):
        need = 2 * (tm * tk * a_bytes + tk * tn * b_bytes) + tm * tn * (2 * out_bytes + (4 if k > tk else 0) + (8 if has_add else 0))
        if tn <= 3072 and need <= MATMUL_VMEM_BUDGET:
            return tm, tn, tk
    return tm, 128, tk


def _matmul(a, b, *, nt, out_dtype, name, add=None, tn_a=False, send=None):
    if tn_a:
        k, m = a.shape
    else:
        m, k = a.shape
    n = b.shape[0] if nt else b.shape[1]
    tm, tn, tk = _matmul_tiles(m, n, k, a.dtype.itemsize, b.dtype.itemsize, jnp.dtype(out_dtype).itemsize, add is not None,
                               128 if tn_a else 16, 16 if tn_a and not nt else 128)
    nk = k // tk
    grid = (m // tm, n // tn, nk)

    def body(a_ref, b_ref, *rest):
        r_ref = None
        if add is not None:
            r_ref, rest = rest[0], rest[1:]
        o_ref = rest[0]
        av = a_ref[...].astype(BF16)
        bv = b_ref[...].astype(BF16)
        part = _dot_tn(av, bv) if tn_a else _dot_nt(av, bv) if nt else _dot(av, bv)

        def finish(r):
            if add is not None:
                r = r + r_ref[...]
            o_ref[...] = r.astype(out_dtype)

        if nk == 1:
            finish(part)
            return
        acc = rest[1]
        kk = pl.program_id(2)

        @pl.when(kk == 0)
        def _():
            acc[...] = part

        @pl.when((kk > 0) & (kk < nk - 1))
        def _():
            acc[...] += part

        @pl.when(kk == nk - 1)
        def _():
            finish(acc[...] + part)

    in_specs = [
        pl.BlockSpec((tk, tm), lambda i, j, kk: (kk, i)) if tn_a else pl.BlockSpec((tm, tk), lambda i, j, kk: (i, kk)),
        pl.BlockSpec((tn, tk), lambda i, j, kk: (j, kk)) if nt else pl.BlockSpec((tk, tn), lambda i, j, kk: (kk, j)),
    ]
    args = [a, b]
    if add is not None:
        in_specs.append(pl.BlockSpec((tm, tn), lambda i, j, kk: (i, j)))
        args.append(add)
    return _call(
        body, name=name, grid=grid, in_specs=in_specs, args=args,
        out_specs=pl.BlockSpec((tm, tn), lambda i, j, kk: (i, j)),
        out_shape=jax.ShapeDtypeStruct((m, n), out_dtype),
        scratch_shapes=[pltpu.VMEM((tm, tn), F32)] if nk > 1 else [],
        semantics=("parallel", "parallel", "arbitrary"), send=send)


def _norm_fwd(x, w_col, name):
    f, t = x.shape
    tt = _tile(t, (512, 256, 128))

    def body(x_ref, w_ref, o_ref):
        xv = x_ref[...]
        r = lax.rsqrt(jnp.mean(xv * xv, axis=0, keepdims=True) + EPS)
        o_ref[...] = (xv * r * w_ref[...]).astype(BF16)

    return pl.pallas_call(
        body,
        name=name,
        grid=(t // tt,),
        in_specs=[pl.BlockSpec((f, tt), lambda i: (0, i)), pl.BlockSpec((f, 1), lambda i: (0, 0))],
        out_specs=pl.BlockSpec((f, tt), lambda i: (0, i)),
        out_shape=jax.ShapeDtypeStruct((f, t), BF16),
        compiler_params=_cparams("parallel"),
    )(x, w_col)


def _norm_bwd(dy, x, w_col, res, name):
    f, t = x.shape
    tt = _tile(t, (512, 256, 128))

    def body(dy_ref, x_ref, w_ref, res_ref, dx_ref, dw_ref):
        @pl.when(pl.program_id(0) == 0)
        def _():
            dw_ref[...] = jnp.zeros_like(dw_ref)

        xv = x_ref[...]
        r = lax.rsqrt(jnp.mean(xv * xv, axis=0, keepdims=True) + EPS)
        xhat = xv * r
        dyv = dy_ref[...]
        dw_ref[...] += _rowsum(dyv * xhat)
        dxhat = dyv * w_ref[...]
        dx_ref[...] = res_ref[...] + r * (dxhat - xhat * jnp.mean(dxhat * xhat, axis=0, keepdims=True))

    blk = pl.BlockSpec((f, tt), lambda i: (0, i))
    col = pl.BlockSpec((f, 1), lambda i: (0, 0))
    return pl.pallas_call(
        body,
        name=name,
        grid=(t // tt,),
        in_specs=[blk, blk, col, blk],
        out_specs=[blk, col],
        out_shape=[jax.ShapeDtypeStruct((f, t), F32), jax.ShapeDtypeStruct((f, 1), F32)],
        compiler_params=_cparams("arbitrary"),
    )(dy, x, w_col, res)


def _final_norm_loss(h, tgt, w_col):
    f, t = h.shape
    tt = _tile(t, (512, 256, 128))

    def body(h_ref, t_ref, w_ref, dh_ref, loss_ref, dw_ref):
        @pl.when(pl.program_id(0) == 0)
        def _():
            dw_ref[...] = jnp.zeros_like(dw_ref)
            loss_ref[...] = jnp.zeros_like(loss_ref)

        xv = h_ref[...]
        r = lax.rsqrt(jnp.mean(xv * xv, axis=0, keepdims=True) + EPS)
        xhat = xv * r
        wv = w_ref[...]
        err = xhat * wv - t_ref[...]
        loss_ref[...] += 0.5 * _rowsum(jnp.mean(err * err, axis=0, keepdims=True))
        dyv = err * (1.0 / f)
        dw_ref[...] += _rowsum(dyv * xhat)
        dxhat = dyv * wv
        dh_ref[...] = r * (dxhat - xhat * jnp.mean(dxhat * xhat, axis=0, keepdims=True))

    blk = pl.BlockSpec((f, tt), lambda i: (0, i))
    col = pl.BlockSpec((f, 1), lambda i: (0, 0))
    one = pl.BlockSpec((1, 1), lambda i: (0, 0))
    return pl.pallas_call(
        body,
        name="final_norm_loss",
        grid=(t // tt,),
        in_specs=[blk, blk, col],
        out_specs=[blk, one, col],
        out_shape=[jax.ShapeDtypeStruct((f, t), F32), jax.ShapeDtypeStruct((1, 1), F32), jax.ShapeDtypeStruct((f, 1), F32)],
        compiler_params=_cparams("arbitrary"),
    )(h, tgt, w_col)


def _attn_mask(n):
    shape = (2 * WINDOW, Q_PER_KV * WINDOW)
    si = lax.broadcasted_iota(jnp.int32, shape, 0)
    qi = lax.broadcasted_iota(jnp.int32, shape, 1) & (WINDOW - 1)
    dist = WINDOW + qi - si
    return (dist >= 0) & (dist < WINDOW) & ((si >= WINDOW) | (n > 0))


def _lane_cat(ref, row0, rows):
    return jnp.concatenate([ref[row0 + i * rows:row0 + (i + 1) * rows, :] for i in range(Q_PER_KV)], axis=1)


def _attn_fwd(proj, sinks):
    t = proj.shape[1]
    nb = t // WINDOW
    scale = HEAD_DIM ** -0.5

    def body(s_ref, q_ref, kc_ref, kp_ref, vc_ref, vp_ref, o_ref, lse_ref):
        n = pl.program_id(0)
        valid = _attn_mask(n)
        for g in range(N_KV_HEADS):
            rows = slice(g * HEAD_DIM, (g + 1) * HEAD_DIM)
            kt = jnp.concatenate([kp_ref[rows, :], kc_ref[rows, :]], axis=1).astype(BF16)
            vt = jnp.concatenate([vp_ref[rows, :], vc_ref[rows, :]], axis=1).astype(BF16)
            qcat = (_lane_cat(q_ref, g * Q_PER_KV * HEAD_DIM, HEAD_DIM) * scale).astype(BF16)
            s = jnp.where(valid, _dot_tn(kt, qcat), NEG)
            sink = jnp.concatenate(
                [jnp.full((1, WINDOW), s_ref[g * Q_PER_KV + i], F32) for i in range(Q_PER_KV)], axis=1)
            m = jnp.maximum(jnp.max(s, axis=0, keepdims=True), sink)
            p = jnp.exp(s - m)
            denom = _colsum(p) + jnp.exp(sink - m)
            probs = (p / denom).astype(BF16)
            out = _dot(vt, probs)
            lse = m + jnp.log(denom)
            for i in range(Q_PER_KV):
                h = g * Q_PER_KV + i
                o_ref[h * HEAD_DIM:(h + 1) * HEAD_DIM, :] = out[:, i * WINDOW:(i + 1) * WINDOW]
                lse_ref[h:h + 1, :] = lse[:, i * WINDOW:(i + 1) * WINDOW]

    kb = OFF_K // KV_DIM
    vb = OFF_V // KV_DIM
    prev = lambda n: jnp.maximum(n - 1, 0)
    return pl.pallas_call(
        body,
        name="attn_fwd",
        grid=(nb,),
        in_specs=[
            pl.BlockSpec(memory_space=pltpu.SMEM),
            pl.BlockSpec((Q_DIM, WINDOW), lambda n: (0, n)),
            pl.BlockSpec((KV_DIM, WINDOW), lambda n: (kb, n)),
            pl.BlockSpec((KV_DIM, WINDOW), lambda n: (kb, prev(n))),
            pl.BlockSpec((KV_DIM, WINDOW), lambda n: (vb, n)),
            pl.BlockSpec((KV_DIM, WINDOW), lambda n: (vb, prev(n))),
        ],
        out_specs=[pl.BlockSpec((Q_DIM, WINDOW), lambda n: (0, n)), pl.BlockSpec((N_Q_HEADS, WINDOW), lambda n: (0, n))],
        out_shape=[jax.ShapeDtypeStruct((Q_DIM, t), F32), jax.ShapeDtypeStruct((N_Q_HEADS, t), F32)],
        compiler_params=_cparams("parallel"),
    )(sinks, proj, proj, proj, proj, proj)


def _attn_bwd(proj, sinks, out, lse, dout, send=None):
    t = proj.shape[1]
    nb = t // WINDOW
    scale = HEAD_DIM ** -0.5

    def body(s_ref, q_ref, kc_ref, kp_ref, vc_ref, vp_ref, o_ref, lse_ref, do_ref,
             dq_ref, dk_ref, dv_ref, ds_ref, dk_carry, dv_carry):
        step = pl.program_id(0)
        n = nb - 1 - step

        @pl.when(step == 0)
        def _():
            dk_carry[...] = jnp.zeros_like(dk_carry)
            dv_carry[...] = jnp.zeros_like(dv_carry)
            ds_ref[...] = jnp.zeros_like(ds_ref)

        valid = _attn_mask(n)
        for g in range(N_KV_HEADS):
            rows = slice(g * HEAD_DIM, (g + 1) * HEAD_DIM)
            q0 = g * Q_PER_KV * HEAD_DIM
            kt = jnp.concatenate([kp_ref[rows, :], kc_ref[rows, :]], axis=1).astype(BF16)
            vt = jnp.concatenate([vp_ref[rows, :], vc_ref[rows, :]], axis=1).astype(BF16)
            qf = _lane_cat(q_ref, q0, HEAD_DIM)
            qcat = qf.astype(BF16)
            ocat = _lane_cat(o_ref, q0, HEAD_DIM)
            docat = _lane_cat(do_ref, q0, HEAD_DIM)
            dob = docat.astype(BF16)
            lse_cat = jnp.concatenate(
                [lse_ref[g * Q_PER_KV + i:g * Q_PER_KV + i + 1, :] for i in range(Q_PER_KV)], axis=1)
            sink = jnp.concatenate(
                [jnp.full((1, WINDOW), s_ref[g * Q_PER_KV + i], F32) for i in range(Q_PER_KV)], axis=1)
            s = jnp.where(valid, _dot_tn(kt, (qf * scale).astype(BF16)), NEG)
            p = jnp.exp(s - lse_cat)
            dp = _dot_tn(vt, dob)
            delta = _colsum(docat * ocat)
            dsc = (p * (dp - delta)).astype(BF16)
            dsink_row = -jnp.exp(sink - lse_cat) * delta
            dq = _dot(kt, dsc) * scale
            dk = _dot_nt(qcat, dsc) * scale
            dv = _dot_nt(dob, p.astype(BF16))
            for i in range(Q_PER_KV):
                h = g * Q_PER_KV + i
                dq_ref[h * HEAD_DIM:(h + 1) * HEAD_DIM, :] = dq[:, i * WINDOW:(i + 1) * WINDOW].astype(BF16)
                ds_ref[h:h + 1, :] += _rowsum(dsink_row[:, i * WINDOW:(i + 1) * WINDOW])
            dk_ref[rows, :] = (dk[:, WINDOW:] + dk_carry[rows, :]).astype(BF16)
            dv_ref[rows, :] = (dv[:, WINDOW:] + dv_carry[rows, :]).astype(BF16)
            dk_carry[rows, :] = dk[:, :WINDOW]
            dv_carry[rows, :] = dv[:, :WINDOW]

    kb = OFF_K // KV_DIM
    vb = OFF_V // KV_DIM
    cur = lambda i: nb - 1 - i
    prev = lambda i: jnp.maximum(nb - 2 - i, 0)
    qspec = pl.BlockSpec((Q_DIM, WINDOW), lambda i: (0, cur(i)))
    kvspec = pl.BlockSpec((KV_DIM, WINDOW), lambda i: (0, cur(i)))
    return _call(
        body,
        name="attn_bwd",
        grid=(nb,),
        in_specs=[
            pl.BlockSpec(memory_space=pltpu.SMEM),
            qspec,
            pl.BlockSpec((KV_DIM, WINDOW), lambda i: (kb, cur(i))),
            pl.BlockSpec((KV_DIM, WINDOW), lambda i: (kb, prev(i))),
            pl.BlockSpec((KV_DIM, WINDOW), lambda i: (vb, cur(i))),
            pl.BlockSpec((KV_DIM, WINDOW), lambda i: (vb, prev(i))),
            qspec,
            pl.BlockSpec((N_Q_HEADS, WINDOW), lambda i: (0, cur(i))),
            qspec,
        ],
        out_specs=[qspec, kvspec, kvspec, pl.BlockSpec((N_Q_HEADS, 1), lambda i: (0, 0))],
        out_shape=[
            jax.ShapeDtypeStruct((Q_DIM, t), BF16),
            jax.ShapeDtypeStruct((KV_DIM, t), BF16),
            jax.ShapeDtypeStruct((KV_DIM, t), BF16),
            jax.ShapeDtypeStruct((N_Q_HEADS, 1), F32),
        ],
        scratch_shapes=[pltpu.VMEM((KV_DIM, WINDOW), F32), pltpu.VMEM((KV_DIM, WINDOW), F32)],
        semantics=("arbitrary",), args=(sinks, proj, proj, proj, proj, proj, out, lse, dout), send=send)


CONV_ROWS = 256


def _conv_silu_fwd(proj, w_col, b_col):
    t = proj.shape[1]
    r0 = OFF_X // CONV_ROWS

    def body(x_ref, w_ref, b_ref, o_ref):
        def strip(rows):
            y, _ = _causal_conv(x_ref[rows, :], w_ref[rows, :], b_ref[rows, :])
            o_ref[rows, :] = y * _sigmoid(y)

        strip(slice(None))

    return pl.pallas_call(
        body,
        name="ssd_conv_fwd",
        grid=(XBC_DIM // CONV_ROWS,),
        in_specs=[
            pl.BlockSpec((CONV_ROWS, t), lambda i: (r0 + i, 0)),
            pl.BlockSpec((CONV_ROWS, SSD_CONV), lambda i: (i, 0)),
            pl.BlockSpec((CONV_ROWS, 1), lambda i: (i, 0)),
        ],
        out_specs=pl.BlockSpec((CONV_ROWS, t), lambda i: (i, 0)),
        out_shape=jax.ShapeDtypeStruct((XBC_DIM, t), F32),
        compiler_params=_cparams("parallel"),
    )(proj, w_col, b_col)


def _conv_silu_bwd(proj, w_col, b_col, dout, row0, name):
    t = proj.shape[1]
    nrows = dout.shape[0]
    p0 = (OFF_X + row0) // CONV_ROWS
    c0 = row0 // CONV_ROWS

    def body(x_ref, w_ref, b_ref, do_ref, dx_ref, dwb_ref):
        def strip(rows):
            xv = x_ref[rows, :]
            wv = w_ref[rows, :]
            y, shifted = _causal_conv(xv, wv, b_ref[rows, :])
            sg = _sigmoid(y)
            dy = do_ref[rows, :] * (sg * (1.0 + y * (1.0 - sg)))
            dx, dwb_ref[rows, :] = _causal_conv_bwd(dy, xv, shifted, wv)
            dx_ref[rows, :] = dx.astype(BF16)

        strip(slice(None))

    return pl.pallas_call(
        body,
        name=name,
        grid=(nrows // CONV_ROWS,),
        in_specs=[
            pl.BlockSpec((CONV_ROWS, t), lambda i: (p0 + i, 0)),
            pl.BlockSpec((CONV_ROWS, SSD_CONV), lambda i: (c0 + i, 0)),
            pl.BlockSpec((CONV_ROWS, 1), lambda i: (c0 + i, 0)),
            pl.BlockSpec((CONV_ROWS, t), lambda i: (i, 0)),
        ],
        out_specs=[pl.BlockSpec((CONV_ROWS, t), lambda i: (i, 0)), pl.BlockSpec((CONV_ROWS, 128), lambda i: (i, 0))],
        out_shape=[jax.ShapeDtypeStruct((nrows, t), BF16), jax.ShapeDtypeStruct((nrows, 128), F32)],
        compiler_params=_cparams("parallel"),
    )(proj, w_col, b_col, dout)


GROUP_ROWS = HEADS_PER_GROUP * SSD_HEAD_DIM


def _ssd_specs(order):
    xb = D_INNER // BC_DIM
    dtb = OFF_DT // N_SSD_HEADS
    col = pl.BlockSpec((N_SSD_HEADS, 1), lambda c: (0, 0))
    return [
        pl.BlockSpec((D_INNER, CHUNK), lambda c: (0, order(c))),
        pl.BlockSpec((BC_DIM, CHUNK), lambda c: (xb, order(c))),
        pl.BlockSpec((BC_DIM, CHUNK), lambda c: (xb + 1, order(c))),
        pl.BlockSpec((N_SSD_HEADS, CHUNK), lambda c: (dtb, order(c))),
        col, col, col,
    ]


def _ssd_common(dt_ref, dtb_ref, alog_ref):
    z = dt_ref[...] + dtb_ref[...]
    dt = _softplus(z)
    a_neg = -jnp.exp(alog_ref[...])
    d_a = dt * a_neg
    row = lax.broadcasted_iota(jnp.int32, (CHUNK, CHUNK), 0)
    colm = lax.broadcasted_iota(jnp.int32, (CHUNK, CHUNK), 1)
    upper = (row <= colm).astype(F32)
    a_cs = jnp.dot(d_a, upper, precision=HIGHEST, preferred_element_type=F32)
    a_last = _rowsum(d_a)
    return z, dt, a_neg, a_cs, a_last, row >= colm, row == colm


def _decay(a_row, causal):
    a_s = jnp.broadcast_to(a_row, (CHUNK, CHUNK))
    seg = a_s.T - a_s
    return jnp.where(causal, jnp.exp(jnp.where(causal, seg, 0.0)), 0.0)


def _ssd_fwd(xbc, proj, dtb_col, alog_col, dsk_col):
    t = xbc.shape[1]
    nc = t // CHUNK

    def body(xs_ref, b_ref, c_ref, dt_ref, dtb_ref, alog_ref, dsk_ref, y_ref, hst_ref, h_scr):
        @pl.when(pl.program_id(0) == 0)
        def _():
            h_scr[...] = jnp.zeros_like(h_scr)

        _, dt, _, a_cs, a_last, causal, _ = _ssd_common(dt_ref, dtb_ref, alog_ref)
        hst_ref[0] = h_scr[...]
        dsk = dsk_ref[...]
        for g in range(N_SSD_GROUPS):
            grows = slice(g * D_STATE, (g + 1) * D_STATE)
            bb = b_ref[grows, :].astype(BF16)
            cb_ = c_ref[grows, :].astype(BF16)
            cb = _dot_tn(cb_, bb)
            for j in range(g * HEADS_PER_GROUP, (g + 1) * HEADS_PER_GROUP):
                rows = slice(j * SSD_HEAD_DIM, (j + 1) * SSD_HEAD_DIM)
                a = a_cs[j:j + 1, :]
                m = (cb * _decay(a, causal)).astype(BF16)
                xs = xs_ref[rows, :]
                xc = xs * dt[j:j + 1, :]
                hj = h_scr[rows, :]
                y = _dot_nt(xc.astype(BF16), m) + _dot(hj.astype(BF16), cb_) * jnp.exp(a) + dsk[j:j + 1, :] * xs
                y_ref[rows, :] = y
                al = a_last[j:j + 1, :]
                w = jnp.exp(al - a)
                h_scr[rows, :] = jnp.exp(al) * hj + _dot_nt((xc * w).astype(BF16), bb)

    return pl.pallas_call(
        body,
        name="ssd_fwd",
        grid=(nc,),
        in_specs=_ssd_specs(lambda c: c),
        out_specs=[
            pl.BlockSpec((D_INNER, CHUNK), lambda c: (0, c)),
            pl.BlockSpec((1, D_INNER, D_STATE), lambda c: (c, 0, 0)),
        ],
        out_shape=[
            jax.ShapeDtypeStruct((D_INNER, t), F32),
            jax.ShapeDtypeStruct((nc, D_INNER, D_STATE), F32),
        ],
        scratch_shapes=[pltpu.VMEM((D_INNER, D_STATE), F32)],
        compiler_params=_cparams("arbitrary"),
    )(xbc, xbc, xbc, proj, dtb_col, alog_col, dsk_col)


def _ssd_bwd(xbc, proj, dtb_col, alog_col, dsk_col, hst, dy):
    t = xbc.shape[1]
    nc = t // CHUNK
    rev = lambda c: nc - 1 - c

    def body(xs_ref, b_ref, c_ref, dt_ref, dtb_ref, alog_ref, dsk_ref, hst_ref, dy_ref,
             dxs_ref, db_ref, dc_ref, ddt_ref, dalog_ref, ddsk_ref, ddtb_ref, dh_scr, da_scr, ddt_scr, dd_scr):
        @pl.when(pl.program_id(0) == 0)
        def _():
            dh_scr[...] = jnp.zeros_like(dh_scr)
            dalog_ref[...] = jnp.zeros_like(dalog_ref)
            ddsk_ref[...] = jnp.zeros_like(ddsk_ref)
            ddtb_ref[...] = jnp.zeros_like(ddtb_ref)

        z, dt, a_neg, a_cs, a_last, causal, eye = _ssd_common(dt_ref, dtb_ref, alog_ref)
        dsk = dsk_ref[...]
        last_lane = lax.broadcasted_iota(jnp.int32, (1, CHUNK), 1) == CHUNK - 1
        for g in range(N_SSD_GROUPS):
            grows = slice(g * D_STATE, (g + 1) * D_STATE)
            bb = b_ref[grows, :].astype(BF16)
            cb_ = c_ref[grows, :].astype(BF16)
            cb = _dot_tn(cb_, bb)
            dcb = jnp.zeros((CHUNK, CHUNK), F32)
            dc_acc = jnp.zeros((D_STATE, CHUNK), F32)
            db_acc = jnp.zeros((D_STATE, CHUNK), F32)
            for j in range(g * HEADS_PER_GROUP, (g + 1) * HEADS_PER_GROUP):
                rows = slice(j * SSD_HEAD_DIM, (j + 1) * SSD_HEAD_DIM)
                a = a_cs[j:j + 1, :]
                al = a_last[j:j + 1, :]
                lam = _decay(a, causal)
                mf = cb * lam
                xs = xs_ref[rows, :]
                dtj = dt[j:j + 1, :]
                xc = xs * dtj
                w = jnp.exp(al - a)
                e = jnp.exp(a)
                gam = jnp.exp(al)
                hj = hst_ref[0, rows, :]
                hjb = hj.astype(BF16)
                dyv = dy_ref[rows, :]
                dyb = dyv.astype(BF16)
                dd_scr[j:j + 1, :] = _colsum(dyv * xs)
                gb = (dyv * e).astype(BF16)
                dh_in = _dot_nt(gb, cb_)
                dc_acc = dc_acc + _dot_tn(hjb, gb)
                yoff = _dot(hjb, cb_) * e
                da = _colsum(dyv * yoff)
                dm = _dot_tn(dyb, xc.astype(BF16))
                dxc = _dot(dyb, mf.astype(BF16))
                dcb = dcb + dm * lam
                nmat = dm * mf
                rs = jnp.broadcast_to(_rowsum(nmat), (CHUNK, CHUNK))
                da = da + _colsum(jnp.where(eye, rs, 0.0)) - _colsum(nmat)
                ds = dh_scr[rows, :]
                dsb = ds.astype(BF16)
                t1 = _dot(dsb, bb)
                xcw = xc * w
                dxc = dxc + w * t1
                dww = _colsum(xcw * t1)
                da_l = _rowsum(dww) + _rowsum(_colsum(ds * hj)) * gam
                da = da - dww + jnp.where(last_lane, da_l, 0.0)
                db_acc = db_acc + _dot_tn(dsb, xcw.astype(BF16))
                dh_scr[rows, :] = gam * ds + dh_in
                dxs_ref[rows, :] = dsk[j:j + 1, :] * dyv + dxc * dtj
                da_scr[j:j + 1, :] = da
                ddt_scr[j:j + 1, :] = _colsum(dxc * xs)
            dcbb = dcb.astype(BF16)
            dc_ref[grows, :] = dc_acc + _dot_nt(bb, dcbb)
            db_ref[grows, :] = db_acc + _dot(cb_, dcbb)
        dda = jnp.dot(da_scr[...], causal.astype(F32), precision=HIGHEST, preferred_element_type=F32)
        ddt = ddt_scr[...] + dda * a_neg
        ddt_raw = ddt * _sigmoid(z)
        ddt_ref[...] = ddt_raw
        ddtb_ref[...] += _rowsum(ddt_raw)
        dalog_ref[...] += _rowsum(dda * dt) * a_neg
        ddsk_ref[...] += _rowsum(dd_scr[...])

    col = pl.BlockSpec((N_SSD_HEADS, 1), lambda c: (0, 0))
    bc = pl.BlockSpec((BC_DIM, CHUNK), lambda c: (0, rev(c)))
    xs_spec = pl.BlockSpec((D_INNER, CHUNK), lambda c: (0, rev(c)))
    small = pltpu.VMEM((N_SSD_HEADS, CHUNK), F32)
    return pl.pallas_call(
        body,
        name="ssd_bwd",
        grid=(nc,),
        in_specs=_ssd_specs(rev) + [pl.BlockSpec((1, D_INNER, D_STATE), lambda c: (rev(c), 0, 0)), xs_spec],
        out_specs=[xs_spec, bc, bc, pl.BlockSpec((N_SSD_HEADS, CHUNK), lambda c: (0, rev(c))), col, col, col],
        out_shape=[
            jax.ShapeDtypeStruct((D_INNER, t), F32),
            jax.ShapeDtypeStruct((BC_DIM, t), F32),
            jax.ShapeDtypeStruct((BC_DIM, t), F32),
            jax.ShapeDtypeStruct((N_SSD_HEADS, t), F32),
            jax.ShapeDtypeStruct((N_SSD_HEADS, 1), F32),
            jax.ShapeDtypeStruct((N_SSD_HEADS, 1), F32),
            jax.ShapeDtypeStruct((N_SSD_HEADS, 1), F32),
        ],
        scratch_shapes=[pltpu.VMEM((D_INNER, D_STATE), F32), small, small, small],
        compiler_params=_cparams("arbitrary"),
    )(xbc, xbc, xbc, proj, dtb_col, alog_col, dsk_col, hst, dy)


GN_ROWS = D_INNER // N_SSD_GROUPS


def _gnorm_fwd(y, proj, w_col):
    t = y.shape[1]
    tt = _tile(t, (512, 256, 128))
    z0 = OFF_Z // GN_ROWS

    def body(y_ref, z_ref, w_ref, o_ref):
        zv = z_ref[...]
        u = y_ref[...] * (zv * _sigmoid(zv))
        r = lax.rsqrt(jnp.mean(u * u, axis=0, keepdims=True) + EPS)
        o_ref[...] = (u * r * w_ref[...]).astype(BF16)

    blk = pl.BlockSpec((GN_ROWS, tt), lambda g, i: (g, i))
    return pl.pallas_call(
        body,
        name="gnorm_fwd",
        grid=(N_SSD_GROUPS, t // tt),
        in_specs=[blk, pl.BlockSpec((GN_ROWS, tt), lambda g, i: (z0 + g, i)), pl.BlockSpec((GN_ROWS, 1), lambda g, i: (g, 0))],
        out_specs=blk,
        out_shape=jax.ShapeDtypeStruct((D_INNER, t), BF16),
        compiler_params=_cparams("parallel", "parallel"),
    )(y, proj, w_col)


def _gnorm_bwd(dout, y, proj, w_col, send=None):
    t = y.shape[1]
    tt = _tile(t, (512, 256, 128))
    z0 = OFF_Z // GN_ROWS

    def body(do_ref, y_ref, z_ref, w_ref, dy_ref, dz_ref, dw_ref):
        @pl.when(pl.program_id(1) == 0)
        def _():
            dw_ref[...] = jnp.zeros_like(dw_ref)

        zv = z_ref[...]
        yv = y_ref[...]
        sg = _sigmoid(zv)
        sz = zv * sg
        u = yv * sz
        r = lax.rsqrt(jnp.mean(u * u, axis=0, keepdims=True) + EPS)
        xhat = u * r
        dov = do_ref[...]
        dw_ref[...] += _rowsum(dov * xhat)
        dxhat = dov * w_ref[...]
        du = r * (dxhat - xhat * jnp.mean(dxhat * xhat, axis=0, keepdims=True))
        dy_ref[...] = du * sz
        dz_ref[...] = (du * yv * (sg * (1.0 + zv * (1.0 - sg)))).astype(BF16)

    blk = pl.BlockSpec((GN_ROWS, tt), lambda g, i: (g, i))
    col = pl.BlockSpec((GN_ROWS, 1), lambda g, i: (g, 0))
    return _call(
        body,
        name="gnorm_bwd",
        grid=(N_SSD_GROUPS, t // tt),
        in_specs=[blk, blk, pl.BlockSpec((GN_ROWS, tt), lambda g, i: (z0 + g, i)), col],
        out_specs=[blk, blk, col],
        out_shape=[jax.ShapeDtypeStruct((D_INNER, t), F32), jax.ShapeDtypeStruct((D_INNER, t), BF16),
                   jax.ShapeDtypeStruct((D_INNER, 1), F32)],
        semantics=("parallel", "arbitrary"), args=(dout, y, proj, w_col), send=send)


GATE_ROWS = 128


def _gate_specs(t):
    nr = D_MODEL // GATE_ROWS
    blk = pl.BlockSpec((GATE_ROWS, t), lambda r: (r, 0))
    rows_from = lambda first: pl.BlockSpec(
        (pl.Element(GATE_ROWS), pl.Element(t)), lambda r: (pl.multiple_of(first + GATE_ROWS * r, N_SSD_HEADS), 0))
    return blk, [
        rows_from(OFF_GA),
        rows_from(OFF_GS),
        pl.BlockSpec((GATE_ROWS, 1), lambda r: (r, 0)),
        pl.BlockSpec((GATE_ROWS, 1), lambda r: (nr + r, 0)),
        blk, blk,
    ]


def _gate_fwd(proj, b_col, attn, ssd):
    t = proj.shape[1]
    blk, specs = _gate_specs(t)

    def body(ga_ref, gs_ref, ba_ref, bs_ref, a_ref, s_ref, o_ref):
        o_ref[...] = (_sigmoid(ga_ref[...] + ba_ref[...]) * a_ref[...]
                      + _sigmoid(gs_ref[...] + bs_ref[...]) * s_ref[...]).astype(BF16)

    return pl.pallas_call(
        body,
        name="gate_fwd",
        grid=(D_MODEL // GATE_ROWS,),
        in_specs=specs,
        out_specs=blk,
        out_shape=jax.ShapeDtypeStruct((D_MODEL, t), BF16),
        compiler_params=_cparams("parallel"),
    )(proj, proj, b_col, b_col, attn, ssd)


def _gate_bwd(proj, b_col, attn, ssd, dmix, send=None):
    t = proj.shape[1]
    blk, specs = _gate_specs(t)

    def body(ga_ref, gs_ref, ba_ref, bs_ref, a_ref, s_ref, dm_ref, da_ref, dso_ref, dga_ref, dgs_ref, dba_ref, dbs_ref):
        dm = dm_ref[...]
        sa = _sigmoid(ga_ref[...] + ba_ref[...])
        ss = _sigmoid(gs_ref[...] + bs_ref[...])
        da_ref[...] = (dm * sa).astype(BF16)
        dso_ref[...] = (dm * ss).astype(BF16)
        dga = dm * a_ref[...] * sa * (1.0 - sa)
        dgs = dm * s_ref[...] * ss * (1.0 - ss)
        dga_ref[...] = dga.astype(BF16)
        dgs_ref[...] = dgs.astype(BF16)
        dba_ref[...] = _rowsum(dga)
        dbs_ref[...] = _rowsum(dgs)

    col = pl.BlockSpec((GATE_ROWS, 1), lambda r: (r, 0))
    act = jax.ShapeDtypeStruct((D_MODEL, t), BF16)
    bias = jax.ShapeDtypeStruct((D_MODEL, 1), F32)
    return _call(
        body,
        name="gate_bwd",
        grid=(D_MODEL // GATE_ROWS,),
        in_specs=specs + [blk],
        out_specs=[blk, blk, blk, blk, col, col],
        out_shape=[act, act, act, act, bias, bias],
        semantics=("parallel",), args=(proj, proj, b_col, b_col, attn, ssd, dmix), send=send)


FFN_ROWS = 256


def _ffn_fwd(u0, w_col, b_col):
    t = u0.shape[2]

    def body(u_ref, w_ref, b_ref, o_ref):
        def strip(rows):
            val, _ = _causal_conv(u_ref[0, rows, :], w_ref[0, rows, :], b_ref[0, rows, :])
            gt, _ = _causal_conv(u_ref[1, rows, :], w_ref[1, rows, :], b_ref[1, rows, :])
            o_ref[rows, :] = (gt * _sigmoid(gt) * val).astype(BF16)

        strip(slice(None))

    return pl.pallas_call(
        body,
        name="ffn_fwd",
        grid=(D_FF // FFN_ROWS,),
        in_specs=[
            pl.BlockSpec((2, FFN_ROWS, t), lambda i: (0, i, 0)),
            pl.BlockSpec((2, FFN_ROWS, FFN_CONV), lambda i: (0, i, 0)),
            pl.BlockSpec((2, FFN_ROWS, 1), lambda i: (0, i, 0)),
        ],
        out_specs=pl.BlockSpec((FFN_ROWS, t), lambda i: (i, 0)),
        out_shape=jax.ShapeDtypeStruct((D_FF, t), BF16),
        compiler_params=_cparams("parallel"),
    )(u0, w_col, b_col)


def _ffn_bwd(u0, w_col, b_col, dg, send=None):
    t = u0.shape[2]

    def body(u_ref, w_ref, b_ref, dg_ref, du_ref, dwb_ref):
        def strip(rows):
            xval, wval = u_ref[0, rows, :], w_ref[0, rows, :]
            xgt, wgt = u_ref[1, rows, :], w_ref[1, rows, :]
            val, sh_val = _causal_conv(xval, wval, b_ref[0, rows, :])
            gt, sh_gt = _causal_conv(xgt, wgt, b_ref[1, rows, :])
            sg = _sigmoid(gt)
            dgv = dg_ref[rows, :]
            dval = dgv * (gt * sg)
            dgt = dgv * val * (sg * (1.0 + gt * (1.0 - sg)))
            dx, dwb_ref[0, rows, :] = _causal_conv_bwd(dval, xval, sh_val, wval)
            du_ref[0, rows, :] = dx.astype(BF16)
            dx, dwb_ref[1, rows, :] = _causal_conv_bwd(dgt, xgt, sh_gt, wgt)
            du_ref[1, rows, :] = dx.astype(BF16)

        strip(slice(None))

    return _call(
        body,
        name="ffn_bwd",
        grid=(D_FF // FFN_ROWS,),
        in_specs=[
            pl.BlockSpec((2, FFN_ROWS, t), lambda i: (0, i, 0)),
            pl.BlockSpec((2, FFN_ROWS, FFN_CONV), lambda i: (0, i, 0)),
            pl.BlockSpec((2, FFN_ROWS, 1), lambda i: (0, i, 0)),
            pl.BlockSpec((FFN_ROWS, t), lambda i: (i, 0)),
        ],
        out_specs=[pl.BlockSpec((2, FFN_ROWS, t), lambda i: (0, i, 0)), pl.BlockSpec((2, FFN_ROWS, 128), lambda i: (0, i, 0))],
        out_shape=[jax.ShapeDtypeStruct((2, D_FF, t), BF16), jax.ShapeDtypeStruct((2, D_FF, 128), F32)],
        semantics=("parallel",), args=(u0, w_col, b_col, dg), send=send)


def _adamw_math(w, g, m, v):
    m = ADAM_B1 * m + (1.0 - ADAM_B1) * g
    v = ADAM_B2 * v + (1.0 - ADAM_B2) * (g * g)
    m_hat = m / (1.0 - ADAM_B1 ** ADAM_STEP)
    v_hat = v / (1.0 - ADAM_B2 ** ADAM_STEP)
    delta = -ADAM_LR * (m_hat / (jnp.sqrt(v_hat) + ADAM_EPS) + ADAM_WD * w)
    return delta, m, v


def _adamw_sharded(parts, w, m, v, name):
    r, c = w.shape[0], w.shape[-1]
    tc = _tile(c, (256, 128))
    blk_shape = (r, tc) if w.ndim == 2 else (r, 1, tc)

    def body(p_ref, w_ref, m_ref, v_ref, g_ref, d_ref, nm_ref, nv_ref):
        g = p_ref[0].astype(F32)
        for s in range(1, N_DEV):
            g = g + p_ref[s].astype(F32)
        flat = lambda ref: ref[...].reshape(r, tc)
        d, nm, nv = _adamw_math(flat(w_ref), g, flat(m_ref), flat(v_ref))
        for ref, val in ((g_ref, g), (d_ref, d), (nm_ref, nm), (nv_ref, nv)):
            ref[...] = val.reshape(blk_shape)

    blk = pl.BlockSpec(blk_shape, (lambda i: (0, i)) if w.ndim == 2 else (lambda i: (0, 0, i)))
    out = jax.ShapeDtypeStruct(w.shape, F32)
    return pl.pallas_call(
        body,
        name=name,
        grid=(c // tc,),
        in_specs=[pl.BlockSpec((N_DEV, r, tc), lambda i: (0, 0, i)), blk, blk, blk],
        out_specs=[blk, blk, blk, blk],
        out_shape=[out, out, out, out],
        compiler_params=_cparams("parallel"),
    )(parts, w, m, v)


def _lane_offsets(sizes):
    offsets, pos = [], 0
    for n in sizes:
        offsets.append(pos)
        pos += -(-n // 128) * 128
    return offsets, pos


def _pack_row(parts):
    rows = [p.reshape(1, -1).astype(F32) for p in parts]
    return jnp.concatenate([jnp.pad(r, ((0, 0), (0, -r.shape[1] % 128))) for r in rows], axis=1)


def _small_update(parts, me, full_sizes, ws, ms, vs):
    n = len(ws)
    offsets, _ = _lane_offsets([1] + list(full_sizes))

    def body(me_ref, p_ref, *refs):
        w_refs, m_refs, v_refs = refs[:n], refs[n:2 * n], refs[2 * n:3 * n]
        scalar_ref, out_refs = refs[3 * n], refs[3 * n + 1:]
        tot = p_ref[0]
        for s in range(1, N_DEV):
            tot = tot + p_ref[s]
        scalar_ref[...] = tot[:, 0:1]
        for k in range(n):
            g_ref, d_ref, nm_ref, nv_ref = out_refs[4 * k:4 * k + 4]
            taps, cols = w_refs[k].shape
            if taps == 1:
                g_ref[...] = tot[:, offsets[k + 1]:offsets[k + 1] + cols]
            else:
                full = full_sizes[k] // taps
                for tap in range(taps):
                    mine = jnp.zeros((1, cols), F32)
                    for d in range(N_DEV):
                        lo = offsets[k + 1] + tap * full + d * cols
                        mine = jnp.where(me_ref[0] == d, tot[:, lo:lo + cols], mine)
                    g_ref[tap:tap + 1, :] = mine
            d_ref[...], nm_ref[...], nv_ref[...] = _adamw_math(w_refs[k][...], g_ref[...], m_refs[k][...], v_refs[k][...])

    vmem = pl.BlockSpec(memory_space=pltpu.VMEM)
    out_shape = [jax.ShapeDtypeStruct((1, 1), F32)]
    for wk in ws:
        out_shape += [jax.ShapeDtypeStruct(wk.shape, F32)] * 4
    res = pl.pallas_call(
        body,
        name="small_update",
        in_specs=[pl.BlockSpec(memory_space=pltpu.SMEM)] + [vmem] * (1 + 3 * n),
        out_specs=[vmem] * len(out_shape),
        out_shape=out_shape,
    )(me, parts, *ws, *ms, *vs)
    return res[0], [res[1 + 4 * k:5 + 4 * k] for k in range(n)]


ANY = pl.BlockSpec(memory_space=pl.ANY)
FLIPS = [(k >> 2 & 1, k >> 1 & 1, k & 1) for k in range(1, N_DEV)]


def _place():
    return lax.axis_index("x"), lax.axis_index("y"), lax.axis_index("c")


HBM = pl.BlockSpec(memory_space=pltpu.HBM)
SEM = pl.BlockSpec(memory_space=pltpu.SEMAPHORE)
EFFECT = pltpu.SideEffectType.DATAFLOW_SIDE_EFFECTING


def _peer_copy(gather, src_ref, land_ref, send_sems, recv_sems, k, sending):
    x, y, c = _place()
    fx, fy, fc = FLIPS[k]
    me = 4 * x + 2 * y + c
    peer = 4 * (x ^ fx) + 2 * (y ^ fy) + (c ^ fc)
    return pltpu.make_async_remote_copy(
        src_ref=src_ref if gather else src_ref.at[peer],
        dst_ref=land_ref.at[me if sending else peer],
        send_sem=send_sems.at[k], recv_sem=recv_sems.at[k],
        device_id=(x ^ fx, y ^ fy, c ^ fc), device_id_type=MESH)


def _gather_start(srcs, name):
    n = len(srcs)
    lands = [lax.empty((N_DEV,) + s.shape, s.dtype) for s in srcs]

    def body(*refs):
        src_refs, land_refs = refs[:n], refs[n:2 * n]
        send, recv = refs[2 * n:3 * n], refs[3 * n:4 * n]
        local_sems = refs[6 * n]
        x, y, c = _place()
        me = 4 * x + 2 * y + c
        local = [pltpu.make_async_copy(src_refs[i], land_refs[i].at[me], local_sems.at[i]) for i in range(n)]
        for cp in local:
            cp.start()
        for i in range(n):
            for k in range(N_DEV - 1):
                _peer_copy(True, src_refs[i], land_refs[i], send[i], recv[i], k, True).start()
        for cp in local:
            cp.wait()

    sem = pltpu.SemaphoreType.DMA((N_DEV - 1,))
    hbm = lambda a: pltpu.HBM(a.shape, a.dtype)
    res = pl.pallas_call(
        body,
        name=name,
        in_specs=[HBM] * (2 * n),
        out_specs=[SEM] * (2 * n) + [HBM] * (2 * n),
        out_shape=[sem] * (2 * n) + [hbm(s) for s in srcs] + [hbm(a) for a in lands],
        input_output_aliases={i: 2 * n + i for i in range(2 * n)},
        scratch_shapes=[pltpu.SemaphoreType.DMA((n,))],
        compiler_params=pltpu.CompilerParams(has_side_effects=EFFECT),
    )(*[pltpu.with_memory_space_constraint(a, pltpu.HBM) for a in list(srcs) + lands])
    return res[:n], res[n:2 * n], res[2 * n:3 * n], res[3 * n:4 * n]


def _exchange_wait(send_sems, recv_sems, src, land, after, gather, name):
    def body(src_ref, land_ref, send_ref, recv_ref, after_ref, src_out, land_out):
        for k in range(N_DEV - 1):
            cp = _peer_copy(gather, src_ref, land_ref, send_ref, recv_ref, k, False)
            cp.wait_send()
            cp.wait_recv()

    hbm = lambda a: pltpu.HBM(a.shape, a.dtype)
    return pl.pallas_call(
        body,
        name=name,
        in_specs=[HBM, HBM, SEM, SEM, ANY],
        out_specs=[HBM, HBM],
        out_shape=[hbm(src), hbm(land)],
        input_output_aliases={0: 0, 1: 1},
        compiler_params=pltpu.CompilerParams(has_side_effects=EFFECT),
    )(src, land, send_sems, recv_sems, after)[1]


def _col(v):
    return v.reshape(-1, 1).astype(F32)


def _local_step(xt, tgt, weight, small):
    t = xt.shape[1]
    n1 = _col(small["norm1_w"])
    n2 = _col(small["norm2_w"])
    nf = _col(small["final_norm_w"])
    bg = _col(small["b_gate"])
    sinks = small["attn_sinks"].reshape(-1).astype(F32)
    cbias = _col(small["ssd_conv_b"])
    dtb = _col(small["dt_bias"])
    alog = _col(small["a_log"])
    dsk = _col(small["d_skip"])
    gnw = _col(small["ssd_norm_w"])
    fb = small["ffn_conv_b"].reshape(2, D_FF, 1)

    xn = _norm_fwd(xt, n1, "norm1_fwd")
    cw = weight("ssd_conv_w", xn).T
    fw = weight("ffn_conv_w", xn).T.reshape(2, D_FF, FFN_CONV)
    w_in_t = weight("w_in", xn)
    proj = _matmul(w_in_t, xn, nt=False, out_dtype=F32, name="mm_in")
    ao, lse = _attn_fwd(proj, sinks)
    w_ao = weight("w_attn_o", ao)
    attn = _matmul(w_ao, ao, nt=False, out_dtype=F32, name="mm_attn_o", tn_a=True)
    xbc = _conv_silu_fwd(proj, cw, cbias)
    y, hst = _ssd_fwd(xbc, proj, dtb, alog, dsk)
    yn = _gnorm_fwd(y, proj, gnw)
    w_so = weight("w_ssd_o", yn)
    ssd = _matmul(w_so, yn, nt=False, out_dtype=F32, name="mm_ssd_o", tn_a=True)
    mix = _gate_fwd(proj, bg, attn, ssd)
    w_out = weight("w_out", mix)
    h1 = _matmul(w_out, mix, nt=False, out_dtype=F32, name="mm_out", add=xt, tn_a=True)
    hn = _norm_fwd(h1, n2, "norm2_fwd")
    w_up_t = weight("w_up", hn)
    u0 = _matmul(w_up_t, hn, nt=False, out_dtype=F32, name="mm_up").reshape(2, D_FF, t)
    gl = _ffn_fwd(u0, fw, fb)
    w_down = weight("w_down", gl)
    h2 = _matmul(w_down, gl, nt=False, out_dtype=F32, name="mm_down", add=h1, tn_a=True)
    dh2, loss, d_nf = _final_norm_loss(h2, tgt, nf)

    g = {}
    handles = {}

    def sending(weight_name, grad, fn, *args, **kwargs):
        out, handles[weight_name] = fn(*args, send=grad.reshape(N_DEV, -1, D_MODEL), **kwargs)
        return out

    g_down = _matmul(gl, dh2, nt=True, out_dtype=BF16, name="mm_d_w_down")
    dgl = _matmul(w_down, dh2, nt=False, out_dtype=F32, name="mm_d_glu")
    du0, d_fwb = sending("w_down", g_down, _ffn_bwd, u0, fw, fb, dgl)
    du0 = du0.reshape(2 * D_FF, t)
    g_up = _matmul(du0, hn, nt=True, out_dtype=BF16, name="mm_d_w_up")
    dhn = sending("w_up", g_up, _matmul, w_up_t, du0, nt=False, out_dtype=F32, name="mm_d_hn", tn_a=True)
    dh1, d_n2 = _norm_bwd(dhn, h1, n2, dh2, "norm2_bwd")
    g_out = _matmul(mix, dh1, nt=True, out_dtype=BF16, name="mm_d_w_out")
    dmix = _matmul(w_out, dh1, nt=False, out_dtype=F32, name="mm_d_mix")
    d_attn, d_ssd, d_ga, d_gs, d_ba, d_bs = sending("w_out", g_out, _gate_bwd, proj, bg, attn, ssd, dmix)
    g_ao = _matmul(ao, d_attn, nt=True, out_dtype=BF16, name="mm_d_w_attn_o")
    dao = _matmul(w_ao, d_attn, nt=False, out_dtype=F32, name="mm_d_ao")
    dq, dk, dv, d_sinks = sending("w_attn_o", g_ao, _attn_bwd, proj, sinks, ao, lse, dao)
    g_so = _matmul(yn, d_ssd, nt=True, out_dtype=BF16, name="mm_d_w_ssd_o")
    dyn = _matmul(w_so, d_ssd, nt=False, out_dtype=F32, name="mm_d_yn")
    dy, dz, d_gnw = sending("w_ssd_o", g_so, _gnorm_bwd, dyn, y, proj, gnw)
    dxs, dbm, dcm, ddt, d_alog, d_dsk, d_dtb = _ssd_bwd(xbc, proj, dtb, alog, dsk, hst, dy)
    dx_xs, dwb_xs = _conv_silu_bwd(proj, cw, cbias, dxs, 0, "ssd_conv_bwd_x")
    dx_b, dwb_b = _conv_silu_bwd(proj, cw, cbias, dbm, D_INNER, "ssd_conv_bwd_b")
    dx_c, dwb_c = _conv_silu_bwd(proj, cw, cbias, dcm, D_INNER + BC_DIM, "ssd_conv_bwd_c")
    dwb_conv = jnp.concatenate([dwb_xs, dwb_b, dwb_c], axis=0)
    dproj = jnp.concatenate([dq, dk, dv, dz, dx_xs, dx_b, dx_c, ddt.astype(BF16), d_ga, d_gs], axis=0)
    g_in = _matmul(dproj, xn, nt=True, out_dtype=BF16, name="mm_d_w_in")
    dxn = sending("w_in", g_in, _matmul, w_in_t, dproj, nt=False, out_dtype=F32, name="mm_d_xn", tn_a=True)
    dx, d_n1 = _norm_bwd(dxn, xt, n1, dh1, "norm1_bwd")

    g["norm1_w"] = d_n1
    g["b_gate"] = jnp.concatenate([d_ba, d_bs], axis=0)
    g["attn_sinks"] = d_sinks
    g["ssd_conv_w"] = dwb_conv[:, :SSD_CONV].T
    g["ssd_conv_b"] = dwb_conv[:, SSD_CONV]
    g["dt_bias"] = d_dtb
    g["a_log"] = d_alog
    g["d_skip"] = d_dsk
    g["ssd_norm_w"] = d_gnw
    g["norm2_w"] = d_n2
    d_fwb = d_fwb.reshape(2 * D_FF, 128)
    g["ffn_conv_w"] = d_fwb[:, :FFN_CONV].T
    g["ffn_conv_b"] = d_fwb[:, FFN_CONV]
    g["final_norm_w"] = d_nf
    return loss, dx, g, handles


SMALL = ("norm1_w", "b_gate", "attn_sinks", "ssd_conv_w", "ssd_conv_b", "dt_bias", "a_log", "d_skip", "ssd_norm_w",
         "norm2_w", "ffn_conv_w", "ffn_conv_b", "final_norm_w")
WEIGHT_ORDER = ("norm1_w", "w_in", "b_gate", "attn_sinks", "w_attn_o", "ssd_conv_w", "ssd_conv_b", "dt_bias", "a_log",
                "d_skip", "ssd_norm_w", "w_ssd_o", "w_out", "norm2_w", "w_up", "ffn_conv_w", "ffn_conv_b", "w_down",
                "final_norm_w")


def kernel(x, norm1_w, w_in, b_gate, attn_sinks, w_attn_o, ssd_conv_w, ssd_conv_b, dt_bias, a_log, d_skip, ssd_norm_w, w_ssd_o, w_out, norm2_w, w_up, ffn_conv_w, ffn_conv_b, w_down, final_norm_w, loss_target, m_norm1_w, m_w_in, m_b_gate, m_attn_sinks, m_w_attn_o, m_ssd_conv_w, m_ssd_conv_b, m_dt_bias, m_a_log, m_d_skip, m_ssd_norm_w, m_w_ssd_o, m_w_out, m_norm2_w, m_w_up, m_ffn_conv_w, m_ffn_conv_b, m_w_down, m_final_norm_w, v_norm1_w, v_w_in, v_b_gate, v_attn_sinks, v_w_attn_o, v_ssd_conv_w, v_ssd_conv_b, v_dt_bias, v_a_log, v_d_skip, v_ssd_norm_w, v_w_ssd_o, v_w_out, v_norm2_w, v_w_up, v_ffn_conv_w, v_ffn_conv_b, v_w_down, v_final_norm_w):
    w = dict(norm1_w=norm1_w, w_in=w_in, b_gate=b_gate, attn_sinks=attn_sinks, w_attn_o=w_attn_o, ssd_conv_w=ssd_conv_w, ssd_conv_b=ssd_conv_b, dt_bias=dt_bias, a_log=a_log, d_skip=d_skip, ssd_norm_w=ssd_norm_w, w_ssd_o=w_ssd_o, w_out=w_out, norm2_w=norm2_w, w_up=w_up, ffn_conv_w=ffn_conv_w, ffn_conv_b=ffn_conv_b, w_down=w_down, final_norm_w=final_norm_w)
    m = dict(norm1_w=m_norm1_w, w_in=m_w_in, b_gate=m_b_gate, attn_sinks=m_attn_sinks, w_attn_o=m_w_attn_o, ssd_conv_w=m_ssd_conv_w, ssd_conv_b=m_ssd_conv_b, dt_bias=m_dt_bias, a_log=m_a_log, d_skip=m_d_skip, ssd_norm_w=m_ssd_norm_w, w_ssd_o=m_w_ssd_o, w_out=m_w_out, norm2_w=m_norm2_w, w_up=m_w_up, ffn_conv_w=m_ffn_conv_w, ffn_conv_b=m_ffn_conv_b, w_down=m_w_down, final_norm_w=m_final_norm_w)
    v = dict(norm1_w=v_norm1_w, w_in=v_w_in, b_gate=v_b_gate, attn_sinks=v_attn_sinks, w_attn_o=v_w_attn_o, ssd_conv_w=v_ssd_conv_w, ssd_conv_b=v_ssd_conv_b, dt_bias=v_dt_bias, a_log=v_a_log, d_skip=v_d_skip, ssd_norm_w=v_ssd_norm_w, w_ssd_o=v_w_ssd_o, w_out=v_w_out, norm2_w=v_norm2_w, w_up=v_w_up, ffn_conv_w=v_ffn_conv_w, ffn_conv_b=v_ffn_conv_b, w_down=v_w_down, final_norm_w=v_final_norm_w)
    me = 4 * lax.axis_index("x") + 2 * lax.axis_index("y") + lax.axis_index("c")

    shards = {"ssd_conv_w": ssd_conv_w[0], "ffn_conv_w": ffn_conv_w[0], "w_in": w_in[0].T.astype(BF16),
              "w_attn_o": w_attn_o[0].astype(BF16), "w_ssd_o": w_ssd_o[0].astype(BF16), "w_out": w_out[0].astype(BF16),
              "w_up": w_up[0].T.astype(BF16), "w_down": w_down[0].astype(BF16)}
    order = list(shards)
    g_send, g_recv, g_src, g_land = _gather_start(list(shards.values()), "gather_start")

    def weight(name, after):
        i = order.index(name)
        land = _exchange_wait(g_send[i], g_recv[i], g_src[i], g_land[i], after, True, "gather_wait_" + name)
        if name == "ssd_conv_w":
            return jnp.transpose(land, (1, 0, 2)).reshape(SSD_CONV, XBC_DIM)
        if name == "ffn_conv_w":
            return jnp.transpose(land, (1, 0, 2)).reshape(FFN_CONV, 2 * D_FF)
        return land.reshape(-1, D_MODEL)

    small = {k: w[k][0] if k != "final_norm_w" else w[k] for k in SMALL}
    loss, dx, g, pending = _local_step(x[0].T, loss_target[0].T, weight, small)

    packed = _pack_row([loss] + [g[k] for k in SMALL])
    s_send, s_recv, s_src, s_land = _gather_start([packed], "small_grads_start")

    res = {}
    after = s_src[0]
    for name in ("w_down", "w_up", "w_out", "w_attn_o", "w_ssd_o", "w_in"):
        parts = _exchange_wait(*pending[name], after, False, "grad_wait_" + name)
        view, back = {
            "w_in": (lambda a: jnp.transpose(a, (2, 0, 1)), lambda r: jnp.transpose(r, (1, 2, 0))),
            "w_up": (lambda a: a[0].T, lambda r: r.T[None]),
        }.get(name, (lambda a: a[0], lambda r: r[None]))
        res[name] = _adamw_sharded(parts, view(w[name]), view(m[name]), view(v[name]), "adamw_" + name)
        after = res[name][0]
        res[name] = [back(r) for r in res[name]]

    rows = _exchange_wait(s_send[0], s_recv[0], s_src[0], s_land[0], after, True, "small_grads_wait")
    flat = lambda a: a.reshape(-1, a.shape[-1])
    loss_sum, updates = _small_update(
        rows, me.reshape(1), [g[k].size for k in SMALL],
        [flat(w[k]) for k in SMALL], [flat(m[k]) for k in SMALL], [flat(v[k]) for k in SMALL])
    for k, upd in zip(SMALL, updates):
        res[k] = [u.reshape(w[k].shape) for u in upd]

    grad_x = dx.T[None]
    outs = [loss_sum.reshape(()), grad_x]
    for i in range(4):
        outs.extend(res[k][i] for k in WEIGHT_ORDER)
    return tuple(outs)
```

```python
import functools

import jax
import jax.numpy as jnp
from jax import lax
from jax.experimental import pallas as pl
from jax.experimental.pallas import tpu as pltpu

F32 = jnp.float32
BF16 = jnp.bfloat16
HIGHEST = lax.Precision.HIGHEST

D_MODEL = 1024
N_Q_HEADS = 16
N_KV_HEADS = 4
HEAD_DIM = 64
WINDOW = 128
Q_PER_KV = N_Q_HEADS // N_KV_HEADS
Q_DIM = N_Q_HEADS * HEAD_DIM
KV_DIM = N_KV_HEADS * HEAD_DIM
D_INNER = 2048
SSD_HEAD_DIM = 64
N_SSD_HEADS = 32
N_SSD_GROUPS = 4
HEADS_PER_GROUP = N_SSD_HEADS // N_SSD_GROUPS
D_STATE = 128
BC_DIM = N_SSD_GROUPS * D_STATE
XBC_DIM = D_INNER + 2 * BC_DIM
SSD_CONV = 4
CHUNK = 128
D_FF = 2816
FFN_CONV = 3
EPS = 1e-5
NEG = -1e30
IN_DIM = 8736
N_DEV = 8

OFF_Q = 0
OFF_K = OFF_Q + Q_DIM
OFF_V = OFF_K + KV_DIM
OFF_Z = OFF_V + KV_DIM
OFF_X = OFF_Z + D_INNER
OFF_DT = OFF_X + XBC_DIM
OFF_GA = OFF_DT + N_SSD_HEADS
OFF_GS = OFF_GA + D_MODEL

ADAM_LR = 0.001
ADAM_B1 = 0.9
ADAM_B2 = 0.999
ADAM_EPS = 1e-08
ADAM_WD = 0.01
ADAM_STEP = 10

VMEM_LIMIT = 48 * 1024 * 1024
MESH = pl.DeviceIdType.MESH


def _cparams(*sem):
    return pltpu.CompilerParams(dimension_semantics=sem, vmem_limit_bytes=VMEM_LIMIT)


def _tile(n, prefs):
    for p in prefs:
        if n % p == 0:
            return p
    return n


def _sigmoid(x):
    return 1.0 / (1.0 + jnp.exp(-x))


def _softplus(x):
    return jnp.maximum(x, 0.0) + jnp.log(1.0 + jnp.exp(-jnp.abs(x)))


def _rowsum(x):
    return jnp.sum(x, axis=1, keepdims=True)


def _colsum(x):
    return jnp.sum(x, axis=0, keepdims=True)


def _dot(a, b):
    return jnp.dot(a, b, preferred_element_type=F32)


def _dot_nt(a, b):
    return lax.dot_general(a, b, (((1,), (1,)), ((), ())), preferred_element_type=F32)


def _dot_tn(a, b):
    return lax.dot_general(a, b, (((0,), (0,)), ((), ())), preferred_element_type=F32)


def _shift_right(x, j):
    if j == 0:
        return x
    r = pltpu.roll(x, j, 1)
    lane = lax.broadcasted_iota(jnp.int32, (x.shape[0], 128), 1)
    return jnp.concatenate([jnp.where(lane >= j, r[:, :128], 0.0), r[:, 128:]], axis=1)


def _shift_left(x, j):
    if j == 0:
        return x
    n = x.shape[1]
    r = pltpu.roll(x, n - j, 1)
    lane = lax.broadcasted_iota(jnp.int32, (x.shape[0], 128), 1)
    return jnp.concatenate([r[:, :n - 128], jnp.where(lane < 128 - j, r[:, n - 128:], 0.0)], axis=1)


def _causal_conv(xv, wv, bv):
    taps = wv.shape[1]
    shifted = [_shift_right(xv, taps - 1 - k) for k in range(taps - 1)]
    y = bv + wv[:, taps - 1:taps] * xv
    for k in range(taps - 1):
        y = y + wv[:, k:k + 1] * shifted[k]
    return y, shifted


def _causal_conv_bwd(dy, xv, shifted, wv):
    taps = wv.shape[1]
    lane = lax.broadcasted_iota(jnp.int32, (dy.shape[0], 128), 1)
    dwb = jnp.where(lane == taps, _rowsum(dy), 0.0)
    dwb = jnp.where(lane == taps - 1, _rowsum(dy * xv), dwb)
    dx = wv[:, taps - 1:taps] * dy
    for k in range(taps - 1):
        dx = dx + wv[:, k:k + 1] * _shift_left(dy, taps - 1 - k)
        dwb = jnp.where(lane == k, _rowsum(dy * shifted[k]), dwb)
    return dx, dwb


def _call(body, *, name, grid, in_specs, out_specs, out_shape, args, semantics, scratch_shapes=(), send=None):
    if send is None:
        return pl.pallas_call(body, name=name, grid=grid, in_specs=in_specs, out_specs=out_specs, out_shape=out_shape,
                              scratch_shapes=list(scratch_shapes), compiler_params=_cparams(*semantics))(*args)
    single = not isinstance(out_specs, (list, tuple))
    out_specs, out_shape = ([out_specs], [out_shape]) if single else (list(out_specs), list(out_shape))
    n_in, n_out, n_scr = len(in_specs), len(out_specs), len(scratch_shapes)
    steps = 1
    for size in grid:
        steps *= size

    def sending(*refs):
        ins, (src_ref, land_ref) = refs[:n_in], refs[n_in:n_in + 2]
        outs = refs[n_in + 2:n_in + 2 + n_out]
        send_sems, recv_sems = refs[n_in + 2 + n_out:n_in + 4 + n_out]
        scratch, local_sem = refs[n_in + 6 + n_out:n_in + 6 + n_out + n_scr], refs[-1]
        x, y, c = _place()
        me = 4 * x + 2 * y + c
        local = pltpu.make_async_copy(src_ref.at[me], land_ref.at[me], local_sem)
        step = 0
        for axis, size in enumerate(grid):
            step = step * size + pl.program_id(axis)

        @pl.when(step == 0)
        def _():
            local.start()
            for peer in range(N_DEV - 1):
                _peer_copy(False, src_ref, land_ref, send_sems, recv_sems, peer, True).start()

        body(*ins, *outs, *scratch)

        @pl.when(step == steps - 1)
        def _():
            local.wait()

    sem = pltpu.SemaphoreType.DMA((N_DEV - 1,))
    hbm = pltpu.HBM(send.shape, send.dtype)
    res = pl.pallas_call(
        sending, name=name, grid=grid,
        in_specs=list(in_specs) + [HBM, HBM],
        out_specs=out_specs + [SEM, SEM, HBM, HBM],
        out_shape=out_shape + [sem, sem, hbm, hbm],
        input_output_aliases={n_in: n_out + 2, n_in + 1: n_out + 3},
        scratch_shapes=list(scratch_shapes) + [pltpu.SemaphoreType.DMA(())],
        compiler_params=pltpu.CompilerParams(dimension_semantics=("arbitrary",) * len(grid), vmem_limit_bytes=VMEM_LIMIT,
                                             has_side_effects=EFFECT),
    )(*args, pltpu.with_memory_space_constraint(send, pltpu.HBM),
      pltpu.with_memory_space_constraint(lax.empty(send.shape, send.dtype), pltpu.HBM))
    return (res[0] if single else list(res[:n_out])), tuple(res[n_out:])


MATMUL_VMEM_BUDGET = 36 * 1024 * 1024
MATMUL_MAX_TK = 3072


MATMUL_MAX_TM = 768


def _largest_tile(n, align, cap):
    return max(d for d in range(align, min(n, cap) + 1, align) if n % d == 0)


def _matmul_tiles(m, n, k, a_bytes, b_bytes, out_bytes, has_add, m_align, k_align):
    tm = _largest_tile(m, m_align, MATMUL_MAX_TM)
    tk = _largest_tile(k, k_align, MATMUL_MAX_TK)
    for tn in sorted({d for d in range(128, n + 1, 128) if n % d == 0}, reverse=True):
        need = 2 * (tm * tk * a_bytes + tk * tn * b_bytes) + tm * tn * (2 * out_bytes + (4 if k > tk else 0) + (8 if has_add else 0))
        if tn <= 3072 and need <= MATMUL_VMEM_BUDGET:
            return tm, tn, tk
    return tm, 128, tk


def _matmul(a, b, *, nt, out_dtype, name, add=None, tn_a=False, send=None):
    if tn_a:
        k, m = a.shape
    else:
        m, k = a.shape
    n = b.shape[0] if nt else b.shape[1]
    tm, tn, tk = _matmul_tiles(m, n, k, a.dtype.itemsize, b.dtype.itemsize, jnp.dtype(out_dtype).itemsize, add is not None,
                               128 if tn_a else 16, 16 if tn_a and not nt else 128)
    nk = k // tk
    grid = (m // tm, n // tn, nk)

    def body(a_ref, b_ref, *rest):
        r_ref = None
        if add is not None:
            r_ref, rest = rest[0], rest[1:]
        o_ref = rest[0]
        av = a_ref[...].astype(BF16)
        bv = b_ref[...].astype(BF16)
        part = _dot_tn(av, bv) if tn_a else _dot_nt(av, bv) if nt else _dot(av, bv)

        def finish(r):
            if add is not None:
                r = r + r_ref[...]
            o_ref[...] = r.astype(out_dtype)

        if nk == 1:
            finish(part)
            return
        acc = rest[1]
        kk = pl.program_id(2)

        @pl.when(kk == 0)
        def _():
            acc[...] = part

        @pl.when((kk > 0) & (kk < nk - 1))
        def _():
            acc[...] += part

        @pl.when(kk == nk - 1)
        def _():
            finish(acc[...] + part)

    in_specs = [
        pl.BlockSpec((tk, tm), lambda i, j, kk: (kk, i)) if tn_a else pl.BlockSpec((tm, tk), lambda i, j, kk: (i, kk)),
        pl.BlockSpec((tn, tk), lambda i, j, kk: (j, kk)) if nt else pl.BlockSpec((tk, tn), lambda i, j, kk: (kk, j)),
    ]
    args = [a, b]
    if add is not None:
        in_specs.append(pl.BlockSpec((tm, tn), lambda i, j, kk: (i, j)))
        args.append(add)
    return _call(
        body, name=name, grid=grid, in_specs=in_specs, args=args,
        out_specs=pl.BlockSpec((tm, tn), lambda i, j, kk: (i, j)),
        out_shape=jax.ShapeDtypeStruct((m, n), out_dtype),
        scratch_shapes=[pltpu.VMEM((tm, tn), F32)] if nk > 1 else [],
        semantics=("parallel", "parallel", "arbitrary"), send=send)


def _norm_fwd(x, w_col, name):
    f, t = x.shape
    tt = _tile(t, (512, 256, 128))

    def body(x_ref, w_ref, o_ref):
        xv = x_ref[...]
        r = lax.rsqrt(jnp.mean(xv * xv, axis=0, keepdims=True) + EPS)
        o_ref[...] = (xv * r * w_ref[...]).astype(BF16)

    return pl.pallas_call(
        body,
        name=name,
        grid=(t // tt,),
        in_specs=[pl.BlockSpec((f, tt), lambda i: (0, i)), pl.BlockSpec((f, 1), lambda i: (0, 0))],
        out_specs=pl.BlockSpec((f, tt), lambda i: (0, i)),
        out_shape=jax.ShapeDtypeStruct((f, t), BF16),
        compiler_params=_cparams("parallel"),
    )(x, w_col)


def _norm_bwd(dy, x, w_col, res, name):
    f, t = x.shape
    tt = _tile(t, (512, 256, 128))

    def body(dy_ref, x_ref, w_ref, res_ref, dx_ref, dw_ref):
        @pl.when(pl.program_id(0) == 0)
        def _():
            dw_ref[...] = jnp.zeros_like(dw_ref)

        xv = x_ref[...]
        r = lax.rsqrt(jnp.mean(xv * xv, axis=0, keepdims=True) + EPS)
        xhat = xv * r
        dyv = dy_ref[...]
        dw_ref[...] += _rowsum(dyv * xhat)
        dxhat = dyv * w_ref[...]
        dx_ref[...] = res_ref[...] + r * (dxhat - xhat * jnp.mean(dxhat * xhat, axis=0, keepdims=True))

    blk = pl.BlockSpec((f, tt), lambda i: (0, i))
    col = pl.BlockSpec((f, 1), lambda i: (0, 0))
    return pl.pallas_call(
        body,
        name=name,
        grid=(t // tt,),
        in_specs=[blk, blk, col, blk],
        out_specs=[blk, col],
        out_shape=[jax.ShapeDtypeStruct((f, t), F32), jax.ShapeDtypeStruct((f, 1), F32)],
        compiler_params=_cparams("arbitrary"),
    )(dy, x, w_col, res)


def _final_norm_loss(h, tgt, w_col):
    f, t = h.shape
    tt = _tile(t, (512, 256, 128))

    def body(h_ref, t_ref, w_ref, dh_ref, loss_ref, dw_ref):
        @pl.when(pl.program_id(0) == 0)
        def _():
            dw_ref[...] = jnp.zeros_like(dw_ref)
            loss_ref[...] = jnp.zeros_like(loss_ref)

        xv = h_ref[...]
        r = lax.rsqrt(jnp.mean(xv * xv, axis=0, keepdims=True) + EPS)
        xhat = xv * r
        wv = w_ref[...]
        err = xhat * wv - t_ref[...]
        loss_ref[...] += 0.5 * _rowsum(jnp.mean(err * err, axis=0, keepdims=True))
        dyv = err * (1.0 / f)
        dw_ref[...] += _rowsum(dyv * xhat)
        dxhat = dyv * wv
        dh_ref[...] = r * (dxhat - xhat * jnp.mean(dxhat * xhat, axis=0, keepdims=True))

    blk = pl.BlockSpec((f, tt), lambda i: (0, i))
    col = pl.BlockSpec((f, 1), lambda i: (0, 0))
    one = pl.BlockSpec((1, 1), lambda i: (0, 0))
    return pl.pallas_call(
        body,
        name="final_norm_loss",
        grid=(t // tt,),
        in_specs=[blk, blk, col],
        out_specs=[blk, one, col],
        out_shape=[jax.ShapeDtypeStruct((f, t), F32), jax.ShapeDtypeStruct((1, 1), F32), jax.ShapeDtypeStruct((f, 1), F32)],
        compiler_params=_cparams("arbitrary"),
    )(h, tgt, w_col)


def _attn_mask(n):
    shape = (2 * WINDOW, Q_PER_KV * WINDOW)
    si = lax.broadcasted_iota(jnp.int32, shape, 0)
    qi = lax.broadcasted_iota(jnp.int32, shape, 1) & (WINDOW - 1)
    dist = WINDOW + qi - si
    return (dist >= 0) & (dist < WINDOW) & ((si >= WINDOW) | (n > 0))


def _lane_cat(ref, row0, rows):
    return jnp.concatenate([ref[row0 + i * rows:row0 + (i + 1) * rows, :] for i in range(Q_PER_KV)], axis=1)


def _attn_fwd(proj, sinks):
    t = proj.shape[1]
    nb = t // WINDOW
    scale = HEAD_DIM ** -0.5

    def body(s_ref, q_ref, kc_ref, kp_ref, vc_ref, vp_ref, o_ref, lse_ref):
        n = pl.program_id(0)
        valid = _attn_mask(n)
        for g in range(N_KV_HEADS):
            rows = slice(g * HEAD_DIM, (g + 1) * HEAD_DIM)
            kt = jnp.concatenate([kp_ref[rows, :], kc_ref[rows, :]], axis=1).astype(BF16)
            vt = jnp.concatenate([vp_ref[rows, :], vc_ref[rows, :]], axis=1).astype(BF16)
            qcat = (_lane_cat(q_ref, g * Q_PER_KV * HEAD_DIM, HEAD_DIM) * scale).astype(BF16)
            s = jnp.where(valid, _dot_tn(kt, qcat), NEG)
            sink = jnp.concatenate(
                [jnp.full((1, WINDOW), s_ref[g * Q_PER_KV + i], F32) for i in range(Q_PER_KV)], axis=1)
            m = jnp.maximum(jnp.max(s, axis=0, keepdims=True), sink)
            p = jnp.exp(s - m)
            denom = _colsum(p) + jnp.exp(sink - m)
            probs = (p / denom).astype(BF16)
            out = _dot(vt, probs)
            lse = m + jnp.log(denom)
            for i in range(Q_PER_KV):
                h = g * Q_PER_KV + i
                o_ref[h * HEAD_DIM:(h + 1) * HEAD_DIM, :] = out[:, i * WINDOW:(i + 1) * WINDOW]
                lse_ref[h:h + 1, :] = lse[:, i * WINDOW:(i + 1) * WINDOW]

    kb = OFF_K // KV_DIM
    vb = OFF_V // KV_DIM
    prev = lambda n: jnp.maximum(n - 1, 0)
    return pl.pallas_call(
        body,
        name="attn_fwd",
        grid=(nb,),
        in_specs=[
            pl.BlockSpec(memory_space=pltpu.SMEM),
            pl.BlockSpec((Q_DIM, WINDOW), lambda n: (0, n)),
            pl.BlockSpec((KV_DIM, WINDOW), lambda n: (kb, n)),
            pl.BlockSpec((KV_DIM, WINDOW), lambda n: (kb, prev(n))),
            pl.BlockSpec((KV_DIM, WINDOW), lambda n: (vb, n)),
            pl.BlockSpec((KV_DIM, WINDOW), lambda n: (vb, prev(n))),
        ],
        out_specs=[pl.BlockSpec((Q_DIM, WINDOW), lambda n: (0, n)), pl.BlockSpec((N_Q_HEADS, WINDOW), lambda n: (0, n))],
        out_shape=[jax.ShapeDtypeStruct((Q_DIM, t), F32), jax.ShapeDtypeStruct((N_Q_HEADS, t), F32)],
        compiler_params=_cparams("parallel"),
    )(sinks, proj, proj, proj, proj, proj)


def _attn_bwd(proj, sinks, out, lse, dout, send=None):
    t = proj.shape[1]
    nb = t // WINDOW
    scale = HEAD_DIM ** -0.5

    def body(s_ref, q_ref, kc_ref, kp_ref, vc_ref, vp_ref, o_ref, lse_ref, do_ref,
             dq_ref, dk_ref, dv_ref, ds_ref, dk_carry, dv_carry):
        step = pl.program_id(0)
        n = nb - 1 - step

        @pl.when(step == 0)
        def _():
            dk_carry[...] = jnp.zeros_like(dk_carry)
            dv_carry[...] = jnp.zeros_like(dv_carry)
            ds_ref[...] = jnp.zeros_like(ds_ref)

        valid = _attn_mask(n)
        for g in range(N_KV_HEADS):
            rows = slice(g * HEAD_DIM, (g + 1) * HEAD_DIM)
            q0 = g * Q_PER_KV * HEAD_DIM
            kt = jnp.concatenate([kp_ref[rows, :], kc_ref[rows, :]], axis=1).astype(BF16)
            vt = jnp.concatenate([vp_ref[rows, :], vc_ref[rows, :]], axis=1).astype(BF16)
            qf = _lane_cat(q_ref, q0, HEAD_DIM)
            qcat = qf.astype(BF16)
            ocat = _lane_cat(o_ref, q0, HEAD_DIM)
            docat = _lane_cat(do_ref, q0, HEAD_DIM)
            dob = docat.astype(BF16)
            lse_cat = jnp.concatenate(
                [lse_ref[g * Q_PER_KV + i:g * Q_PER_KV + i + 1, :] for i in range(Q_PER_KV)], axis=1)
            sink = jnp.concatenate(
                [jnp.full((1, WINDOW), s_ref[g * Q_PER_KV + i], F32) for i in range(Q_PER_KV)], axis=1)
            s = jnp.where(valid, _dot_tn(kt, (qf * scale).astype(BF16)), NEG)
            p = jnp.exp(s - lse_cat)
            dp = _dot_tn(vt, dob)
            delta = _colsum(docat * ocat)
            dsc = (p * (dp - delta)).astype(BF16)
            dsink_row = -jnp.exp(sink - lse_cat) * delta
            dq = _dot(kt, dsc) * scale
            dk = _dot_nt(qcat, dsc) * scale
            dv = _dot_nt(dob, p.astype(BF16))
            for i in range(Q_PER_KV):
                h = g * Q_PER_KV + i
                dq_ref[h * HEAD_DIM:(h + 1) * HEAD_DIM, :] = dq[:, i * WINDOW:(i + 1) * WINDOW].astype(BF16)
                ds_ref[h:h + 1, :] += _rowsum(dsink_row[:, i * WINDOW:(i + 1) * WINDOW])
            dk_ref[rows, :] = (dk[:, WINDOW:] + dk_carry[rows, :]).astype(BF16)
            dv_ref[rows, :] = (dv[:, WINDOW:] + dv_carry[rows, :]).astype(BF16)
            dk_carry[rows, :] = dk[:, :WINDOW]
            dv_carry[rows, :] = dv[:, :WINDOW]

    kb = OFF_K // KV_DIM
    vb = OFF_V // KV_DIM
    cur = lambda i: nb - 1 - i
    prev = lambda i: jnp.maximum(nb - 2 - i, 0)
    qspec = pl.BlockSpec((Q_DIM, WINDOW), lambda i: (0, cur(i)))
    kvspec = pl.BlockSpec((KV_DIM, WINDOW), lambda i: (0, cur(i)))
    return _call(
        body,
        name="attn_bwd",
        grid=(nb,),
        in_specs=[
            pl.BlockSpec(memory_space=pltpu.SMEM),
            qspec,
            pl.BlockSpec((KV_DIM, WINDOW), lambda i: (kb, cur(i))),
            pl.BlockSpec((KV_DIM, WINDOW), lambda i: (kb, prev(i))),
            pl.BlockSpec((KV_DIM, WINDOW), lambda i: (vb, cur(i))),
            pl.BlockSpec((KV_DIM, WINDOW), lambda i: (vb, prev(i))),
            qspec,
            pl.BlockSpec((N_Q_HEADS, WINDOW), lambda i: (0, cur(i))),
            qspec,
        ],
        out_specs=[qspec, kvspec, kvspec, pl.BlockSpec((N_Q_HEADS, 1), lambda i: (0, 0))],
        out_shape=[
            jax.ShapeDtypeStruct((Q_DIM, t), BF16),
            jax.ShapeDtypeStruct((KV_DIM, t), BF16),
            jax.ShapeDtypeStruct((KV_DIM, t), BF16),
            jax.ShapeDtypeStruct((N_Q_HEADS, 1), F32),
        ],
        scratch_shapes=[pltpu.VMEM((KV_DIM, WINDOW), F32), pltpu.VMEM((KV_DIM, WINDOW), F32)],
        semantics=("arbitrary",), args=(sinks, proj, proj, proj, proj, proj, out, lse, dout), send=send)


CONV_ROWS = 256


def _conv_silu_fwd(proj, w_col, b_col):
    t = proj.shape[1]
    r0 = OFF_X // CONV_ROWS

    def body(x_ref, w_ref, b_ref, o_ref):
        def strip(rows):
            y, _ = _causal_conv(x_ref[rows, :], w_ref[rows, :], b_ref[rows, :])
            o_ref[rows, :] = y * _sigmoid(y)

        strip(slice(None))

    return pl.pallas_call(
        body,
        name="ssd_conv_fwd",
        grid=(XBC_DIM // CONV_ROWS,),
        in_specs=[
            pl.BlockSpec((CONV_ROWS, t), lambda i: (r0 + i, 0)),
            pl.BlockSpec((CONV_ROWS, SSD_CONV), lambda i: (i, 0)),
            pl.BlockSpec((CONV_ROWS, 1), lambda i: (i, 0)),
        ],
        out_specs=pl.BlockSpec((CONV_ROWS, t), lambda i: (i, 0)),
        out_shape=jax.ShapeDtypeStruct((XBC_DIM, t), F32),
        compiler_params=_cparams("parallel"),
    )(proj, w_col, b_col)


def _conv_silu_bwd(proj, w_col, b_col, dout, row0, name):
    t = proj.shape[1]
    nrows = dout.shape[0]
    p0 = (OFF_X + row0) // CONV_ROWS
    c0 = row0 // CONV_ROWS

    def body(x_ref, w_ref, b_ref, do_ref, dx_ref, dwb_ref):
        def strip(rows):
            xv = x_ref[rows, :]
            wv = w_ref[rows, :]
            y, shifted = _causal_conv(xv, wv, b_ref[rows, :])
            sg = _sigmoid(y)
            dy = do_ref[rows, :] * (sg * (1.0 + y * (1.0 - sg)))
            dx, dwb_ref[rows, :] = _causal_conv_bwd(dy, xv, shifted, wv)
            dx_ref[rows, :] = dx.astype(BF16)

        strip(slice(None))

    return pl.pallas_call(
        body,
        name=name,
        grid=(nrows // CONV_ROWS,),
        in_specs=[
            pl.BlockSpec((CONV_ROWS, t), lambda i: (p0 + i, 0)),
            pl.BlockSpec((CONV_ROWS, SSD_CONV), lambda i: (c0 + i, 0)),
            pl.BlockSpec((CONV_ROWS, 1), lambda i: (c0 + i, 0)),
            pl.BlockSpec((CONV_ROWS, t), lambda i: (i, 0)),
        ],
        out_specs=[pl.BlockSpec((CONV_ROWS, t), lambda i: (i, 0)), pl.BlockSpec((CONV_ROWS, 128), lambda i: (i, 0))],
        out_shape=[jax.ShapeDtypeStruct((nrows, t), BF16), jax.ShapeDtypeStruct((nrows, 128), F32)],
        compiler_params=_cparams("parallel"),
    )(proj, w_col, b_col, dout)


GROUP_ROWS = HEADS_PER_GROUP * SSD_HEAD_DIM


def _ssd_specs(order):
    xb = D_INNER // BC_DIM
    dtb = OFF_DT // N_SSD_HEADS
    col = pl.BlockSpec((N_SSD_HEADS, 1), lambda c: (0, 0))
    return [
        pl.BlockSpec((D_INNER, CHUNK), lambda c: (0, order(c))),
        pl.BlockSpec((BC_DIM, CHUNK), lambda c: (xb, order(c))),
        pl.BlockSpec((BC_DIM, CHUNK), lambda c: (xb + 1, order(c))),
        pl.BlockSpec((N_SSD_HEADS, CHUNK), lambda c: (dtb, order(c))),
        col, col, col,
    ]


def _ssd_common(dt_ref, dtb_ref, alog_ref):
    z = dt_ref[...] + dtb_ref[...]
    dt = _softplus(z)
    a_neg = -jnp.exp(alog_ref[...])
    d_a = dt * a_neg
    row = lax.broadcasted_iota(jnp.int32, (CHUNK, CHUNK), 0)
    colm = lax.broadcasted_iota(jnp.int32, (CHUNK, CHUNK), 1)
    upper = (row <= colm).astype(F32)
    a_cs = jnp.dot(d_a, upper, precision=HIGHEST, preferred_element_type=F32)
    a_last = _rowsum(d_a)
    return z, dt, a_neg, a_cs, a_last, row >= colm, row == colm


def _decay(a_row, causal):
    a_s = jnp.broadcast_to(a_row, (CHUNK, CHUNK))
    seg = a_s.T - a_s
    return jnp.where(causal, jnp.exp(jnp.where(causal, seg, 0.0)), 0.0)


def _ssd_fwd(xbc, proj, dtb_col, alog_col, dsk_col):
    t = xbc.shape[1]
    nc = t // CHUNK

    def body(xs_ref, b_ref, c_ref, dt_ref, dtb_ref, alog_ref, dsk_ref, y_ref, hst_ref, h_scr):
        @pl.when(pl.program_id(0) == 0)
        def _():
            h_scr[...] = jnp.zeros_like(h_scr)

        _, dt, _, a_cs, a_last, causal, _ = _ssd_common(dt_ref, dtb_ref, alog_ref)
        hst_ref[0] = h_scr[...]
        dsk = dsk_ref[...]
        for g in range(N_SSD_GROUPS):
            grows = slice(g * D_STATE, (g + 1) * D_STATE)
            bb = b_ref[grows, :].astype(BF16)
            cb_ = c_ref[grows, :].astype(BF16)
            cb = _dot_tn(cb_, bb)
            for j in range(g * HEADS_PER_GROUP, (g + 1) * HEADS_PER_GROUP):
                rows = slice(j * SSD_HEAD_DIM, (j + 1) * SSD_HEAD_DIM)
                a = a_cs[j:j + 1, :]
                m = (cb * _decay(a, causal)).astype(BF16)
                xs = xs_ref[rows, :]
                xc = xs * dt[j:j + 1, :]
                hj = h_scr[rows, :]
                y = _dot_nt(xc.astype(BF16), m) + _dot(hj.astype(BF16), cb_) * jnp.exp(a) + dsk[j:j + 1, :] * xs
                y_ref[rows, :] = y
                al = a_last[j:j + 1, :]
                w = jnp.exp(al - a)
                h_scr[rows, :] = jnp.exp(al) * hj + _dot_nt((xc * w).astype(BF16), bb)

    return pl.pallas_call(
        body,
        name="ssd_fwd",
        grid=(nc,),
        in_specs=_ssd_specs(lambda c: c),
        out_specs=[
            pl.BlockSpec((D_INNER, CHUNK), lambda c: (0, c)),
            pl.BlockSpec((1, D_INNER, D_STATE), lambda c: (c, 0, 0)),
        ],
        out_shape=[
            jax.ShapeDtypeStruct((D_INNER, t), F32),
            jax.ShapeDtypeStruct((nc, D_INNER, D_STATE), F32),
        ],
        scratch_shapes=[pltpu.VMEM((D_INNER, D_STATE), F32)],
        compiler_params=_cparams("arbitrary"),
    )(xbc, xbc, xbc, proj, dtb_col, alog_col, dsk_col)


def _ssd_bwd(xbc, proj, dtb_col, alog_col, dsk_col, hst, dy):
    t = xbc.shape[1]
    nc = t // CHUNK
    rev = lambda c: nc - 1 - c

    def body(xs_ref, b_ref, c_ref, dt_ref, dtb_ref, alog_ref, dsk_ref, hst_ref, dy_ref,
             dxs_ref, db_ref, dc_ref, ddt_ref, dalog_ref, ddsk_ref, ddtb_ref, dh_scr, da_scr, ddt_scr, dd_scr):
        @pl.when(pl.program_id(0) == 0)
        def _():
            dh_scr[...] = jnp.zeros_like(dh_scr)
            dalog_ref[...] = jnp.zeros_like(dalog_ref)
            ddsk_ref[...] = jnp.zeros_like(ddsk_ref)
            ddtb_ref[...] = jnp.zeros_like(ddtb_ref)

        z, dt, a_neg, a_cs, a_last, causal, eye = _ssd_common(dt_ref, dtb_ref, alog_ref)
        dsk = dsk_ref[...]
        last_lane = lax.broadcasted_iota(jnp.int32, (1, CHUNK), 1) == CHUNK - 1
        for g in range(N_SSD_GROUPS):
            grows = slice(g * D_STATE, (g + 1) * D_STATE)
            bb = b_ref[grows, :].astype(BF16)
            cb_ = c_ref[grows, :].astype(BF16)
            cb = _dot_tn(cb_, bb)
            dcb = jnp.zeros((CHUNK, CHUNK), F32)
            dc_acc = jnp.zeros((D_STATE, CHUNK), F32)
            db_acc = jnp.zeros((D_STATE, CHUNK), F32)
            for j in range(g * HEADS_PER_GROUP, (g + 1) * HEADS_PER_GROUP):
                rows = slice(j * SSD_HEAD_DIM, (j + 1) * SSD_HEAD_DIM)
                a = a_cs[j:j + 1, :]
                al = a_last[j:j + 1, :]
                lam = _decay(a, causal)
                mf = cb * lam
                xs = xs_ref[rows, :]
                dtj = dt[j:j + 1, :]
                xc = xs * dtj
                w = jnp.exp(al - a)
                e = jnp.exp(a)
                gam = jnp.exp(al)
                hj = hst_ref[0, rows, :]
                hjb = hj.astype(BF16)
                dyv = dy_ref[rows, :]
                dyb = dyv.astype(BF16)
                dd_scr[j:j + 1, :] = _colsum(dyv * xs)
                gb = (dyv * e).astype(BF16)
                dh_in = _dot_nt(gb, cb_)
                dc_acc = dc_acc + _dot_tn(hjb, gb)
                yoff = _dot(hjb, cb_) * e
                da = _colsum(dyv * yoff)
                dm = _dot_tn(dyb, xc.astype(BF16))
                dxc = _dot(dyb, mf.astype(BF16))
                dcb = dcb + dm * lam
                nmat = dm * mf
                rs = jnp.broadcast_to(_rowsum(nmat), (CHUNK, CHUNK))
                da = da + _colsum(jnp.where(eye, rs, 0.0)) - _colsum(nmat)
                ds = dh_scr[rows, :]
                dsb = ds.astype(BF16)
                t1 = _dot(dsb, bb)
                xcw = xc * w
                dxc = dxc + w * t1
                dww = _colsum(xcw * t1)
                da_l = _rowsum(dww) + _rowsum(_colsum(ds * hj)) * gam
                da = da - dww + jnp.where(last_lane, da_l, 0.0)
                db_acc = db_acc + _dot_tn(dsb, xcw.astype(BF16))
                dh_scr[rows, :] = gam * ds + dh_in
                dxs_ref[rows, :] = dsk[j:j + 1, :] * dyv + dxc * dtj
                da_scr[j:j + 1, :] = da
                ddt_scr[j:j + 1, :] = _colsum(dxc * xs)
            dcbb = dcb.astype(BF16)
            dc_ref[grows, :] = dc_acc + _dot_nt(bb, dcbb)
            db_ref[grows, :] = db_acc + _dot(cb_, dcbb)
        dda = jnp.dot(da_scr[...], causal.astype(F32), precision=HIGHEST, preferred_element_type=F32)
        ddt = ddt_scr[...] + dda * a_neg
        ddt_raw = ddt * _sigmoid(z)
        ddt_ref[...] = ddt_raw
        ddtb_ref[...] += _rowsum(ddt_raw)
        dalog_ref[...] += _rowsum(dda * dt) * a_neg
        ddsk_ref[...] += _rowsum(dd_scr[...])

    col = pl.BlockSpec((N_SSD_HEADS, 1), lambda c: (0, 0))
    bc = pl.BlockSpec((BC_DIM, CHUNK), lambda c: (0, rev(c)))
    xs_spec = pl.BlockSpec((D_INNER, CHUNK), lambda c: (0, rev(c)))
    small = pltpu.VMEM((N_SSD_HEADS, CHUNK), F32)
    return pl.pallas_call(
        body,
        name="ssd_bwd",
        grid=(nc,),
        in_specs=_ssd_specs(rev) + [pl.BlockSpec((1, D_INNER, D_STATE), lambda c: (rev(c), 0, 0)), xs_spec],
        out_specs=[xs_spec, bc, bc, pl.BlockSpec((N_SSD_HEADS, CHUNK), lambda c: (0, rev(c))), col, col, col],
        out_shape=[
            jax.ShapeDtypeStruct((D_INNER, t), F32),
            jax.ShapeDtypeStruct((BC_DIM, t), F32),
            jax.ShapeDtypeStruct((BC_DIM, t), F32),
            jax.ShapeDtypeStruct((N_SSD_HEADS, t), F32),
            jax.ShapeDtypeStruct((N_SSD_HEADS, 1), F32),
            jax.ShapeDtypeStruct((N_SSD_HEADS, 1), F32),
            jax.ShapeDtypeStruct((N_SSD_HEADS, 1), F32),
        ],
        scratch_shapes=[pltpu.VMEM((D_INNER, D_STATE), F32), small, small, small],
        compiler_params=_cparams("arbitrary"),
    )(xbc, xbc, xbc, proj, dtb_col, alog_col, dsk_col, hst, dy)


GN_ROWS = D_INNER // N_SSD_GROUPS


def _gnorm_fwd(y, proj, w_col):
    t = y.shape[1]
    tt = _tile(t, (512, 256, 128))
    z0 = OFF_Z // GN_ROWS

    def body(y_ref, z_ref, w_ref, o_ref):
        zv = z_ref[...]
        u = y_ref[...] * (zv * _sigmoid(zv))
        r = lax.rsqrt(jnp.mean(u * u, axis=0, keepdims=True) + EPS)
        o_ref[...] = (u * r * w_ref[...]).astype(BF16)

    blk = pl.BlockSpec((GN_ROWS, tt), lambda g, i: (g, i))
    return pl.pallas_call(
        body,
        name="gnorm_fwd",
        grid=(N_SSD_GROUPS, t // tt),
        in_specs=[blk, pl.BlockSpec((GN_ROWS, tt), lambda g, i: (z0 + g, i)), pl.BlockSpec((GN_ROWS, 1), lambda g, i: (g, 0))],
        out_specs=blk,
        out_shape=jax.ShapeDtypeStruct((D_INNER, t), BF16),
        compiler_params=_cparams("parallel", "parallel"),
    )(y, proj, w_col)


def _gnorm_bwd(dout, y, proj, w_col, send=None):
    t = y.shape[1]
    tt = _tile(t, (512, 256, 128))
    z0 = OFF_Z // GN_ROWS

    def body(do_ref, y_ref, z_ref, w_ref, dy_ref, dz_ref, dw_ref):
        @pl.when(pl.program_id(1) == 0)
        def _():
            dw_ref[...] = jnp.zeros_like(dw_ref)

        zv = z_ref[...]
        yv = y_ref[...]
        sg = _sigmoid(zv)
        sz = zv * sg
        u = yv * sz
        r = lax.rsqrt(jnp.mean(u * u, axis=0, keepdims=True) + EPS)
        xhat = u * r
        dov = do_ref[...]
        dw_ref[...] += _rowsum(dov * xhat)
        dxhat = dov * w_ref[...]
        du = r * (dxhat - xhat * jnp.mean(dxhat * xhat, axis=0, keepdims=True))
        dy_ref[...] = du * sz
        dz_ref[...] = (du * yv * (sg * (1.0 + zv * (1.0 - sg)))).astype(BF16)

    blk = pl.BlockSpec((GN_ROWS, tt), lambda g, i: (g, i))
    col = pl.BlockSpec((GN_ROWS, 1), lambda g, i: (g, 0))
    return _call(
        body,
        name="gnorm_bwd",
        grid=(N_SSD_GROUPS, t // tt),
        in_specs=[blk, blk, pl.BlockSpec((GN_ROWS, tt), lambda g, i: (z0 + g, i)), col],
        out_specs=[blk, blk, col],
        out_shape=[jax.ShapeDtypeStruct((D_INNER, t), F32), jax.ShapeDtypeStruct((D_INNER, t), BF16),
                   jax.ShapeDtypeStruct((D_INNER, 1), F32)],
        semantics=("parallel", "arbitrary"), args=(dout, y, proj, w_col), send=send)


GATE_ROWS = 128


def _gate_specs(t):
    nr = D_MODEL // GATE_ROWS
    blk = pl.BlockSpec((GATE_ROWS, t), lambda r: (r, 0))
    rows_from = lambda first: pl.BlockSpec(
        (pl.Element(GATE_ROWS), pl.Element(t)), lambda r: (pl.multiple_of(first + GATE_ROWS * r, N_SSD_HEADS), 0))
    return blk, [
        rows_from(OFF_GA),
        rows_from(OFF_GS),
        pl.BlockSpec((GATE_ROWS, 1), lambda r: (r, 0)),
        pl.BlockSpec((GATE_ROWS, 1), lambda r: (nr + r, 0)),
        blk, blk,
    ]


def _gate_fwd(proj, b_col, attn, ssd):
    t = proj.shape[1]
    blk, specs = _gate_specs(t)

    def body(ga_ref, gs_ref, ba_ref, bs_ref, a_ref, s_ref, o_ref):
        o_ref[...] = (_sigmoid(ga_ref[...] + ba_ref[...]) * a_ref[...]
                      + _sigmoid(gs_ref[...] + bs_ref[...]) * s_ref[...]).astype(BF16)

    return pl.pallas_call(
        body,
        name="gate_fwd",
        grid=(D_MODEL // GATE_ROWS,),
        in_specs=specs,
        out_specs=blk,
        out_shape=jax.ShapeDtypeStruct((D_MODEL, t), BF16),
        compiler_params=_cparams("parallel"),
    )(proj, proj, b_col, b_col, attn, ssd)


def _gate_bwd(proj, b_col, attn, ssd, dmix, send=None):
    t = proj.shape[1]
    blk, specs = _gate_specs(t)

    def body(ga_ref, gs_ref, ba_ref, bs_ref, a_ref, s_ref, dm_ref, da_ref, dso_ref, dga_ref, dgs_ref, dba_ref, dbs_ref):
        dm = dm_ref[...]
        sa = _sigmoid(ga_ref[...] + ba_ref[...])
        ss = _sigmoid(gs_ref[...] + bs_ref[...])
        da_ref[...] = (dm * sa).astype(BF16)
        dso_ref[...] = (dm * ss).astype(BF16)
        dga = dm * a_ref[...] * sa * (1.0 - sa)
        dgs = dm * s_ref[...] * ss * (1.0 - ss)
        dga_ref[...] = dga.astype(BF16)
        dgs_ref[...] = dgs.astype(BF16)
        dba_ref[...] = _rowsum(dga)
        dbs_ref[...] = _rowsum(dgs)

    col = pl.BlockSpec((GATE_ROWS, 1), lambda r: (r, 0))
    act = jax.ShapeDtypeStruct((D_MODEL, t), BF16)
    bias = jax.ShapeDtypeStruct((D_MODEL, 1), F32)
    return _call(
        body,
        name="gate_bwd",
        grid=(D_MODEL // GATE_ROWS,),
        in_specs=specs + [blk],
        out_specs=[blk, blk, blk, blk, col, col],
        out_shape=[act, act, act, act, bias, bias],
        semantics=("parallel",), args=(proj, proj, b_col, b_col, attn, ssd, dmix), send=send)


FFN_ROWS = 256


def _ffn_fwd(u0, w_col, b_col):
    t = u0.shape[2]

    def body(u_ref, w_ref, b_ref, o_ref):
        def strip(rows):
            val, _ = _causal_conv(u_ref[0, rows, :], w_ref[0, rows, :], b_ref[0, rows, :])
            gt, _ = _causal_conv(u_ref[1, rows, :], w_ref[1, rows, :], b_ref[1, rows, :])
            o_ref[rows, :] = (gt * _sigmoid(gt) * val).astype(BF16)

        strip(slice(None))

    return pl.pallas_call(
        body,
        name="ffn_fwd",
        grid=(D_FF // FFN_ROWS,),
        in_specs=[
            pl.BlockSpec((2, FFN_ROWS, t), lambda i: (0, i, 0)),
            pl.BlockSpec((2, FFN_ROWS, FFN_CONV), lambda i: (0, i, 0)),
            pl.BlockSpec((2, FFN_ROWS, 1), lambda i: (0, i, 0)),
        ],
        out_specs=pl.BlockSpec((FFN_ROWS, t), lambda i: (i, 0)),
        out_shape=jax.ShapeDtypeStruct((D_FF, t), BF16),
        compiler_params=_cparams("parallel"),
    )(u0, w_col, b_col)


def _ffn_bwd(u0, w_col, b_col, dg, send=None):
    t = u0.shape[2]

    def body(u_ref, w_ref, b_ref, dg_ref, du_ref, dwb_ref):
        def strip(rows):
            xval, wval = u_ref[0, rows, :], w_ref[0, rows, :]
            xgt, wgt = u_ref[1, rows, :], w_ref[1, rows, :]
            val, sh_val = _causal_conv(xval, wval, b_ref[0, rows, :])
            gt, sh_gt = _causal_conv(xgt, wgt, b_ref[1, rows, :])
            sg = _sigmoid(gt)
            dgv = dg_ref[rows, :]
            dval = dgv * (gt * sg)
            dgt = dgv * val * (sg * (1.0 + gt * (1.0 - sg)))
            dx, dwb_ref[0, rows, :] = _causal_conv_bwd(dval, xval, sh_val, wval)
            du_ref[0, rows, :] = dx.astype(BF16)
            dx, dwb_ref[1, rows, :] = _causal_conv_bwd(dgt, xgt, sh_gt, wgt)
            du_ref[1, rows, :] = dx.astype(BF16)

        strip(slice(None))

    return _call(
        body,
        name="ffn_bwd",
        grid=(D_FF // FFN_ROWS,),
        in_specs=[
            pl.BlockSpec((2, FFN_ROWS, t), lambda i: (0, i, 0)),
            pl.BlockSpec((2, FFN_ROWS, FFN_CONV), lambda i: (0, i, 0)),
            pl.BlockSpec((2, FFN_ROWS, 1), lambda i: (0, i, 0)),
            pl.BlockSpec((FFN_ROWS, t), lambda i: (i, 0)),
        ],
        out_specs=[pl.BlockSpec((2, FFN_ROWS, t), lambda i: (0, i, 0)), pl.BlockSpec((2, FFN_ROWS, 128), lambda i: (0, i, 0))],
        out_shape=[jax.ShapeDtypeStruct((2, D_FF, t), BF16), jax.ShapeDtypeStruct((2, D_FF, 128), F32)],
        semantics=("parallel",), args=(u0, w_col, b_col, dg), send=send)


def _adamw_math(w, g, m, v):
    m = ADAM_B1 * m + (1.0 - ADAM_B1) * g
    v = ADAM_B2 * v + (1.0 - ADAM_B2) * (g * g)
    m_hat = m / (1.0 - ADAM_B1 ** ADAM_STEP)
    v_hat = v / (1.0 - ADAM_B2 ** ADAM_STEP)
    delta = -ADAM_LR * (m_hat / (jnp.sqrt(v_hat) + ADAM_EPS) + ADAM_WD * w)
    return delta, m, v


def _adamw_sharded(parts, w, m, v, name):
    r, c = w.shape[0], w.shape[-1]
    tc = _tile(c, (256, 128))
    blk_shape = (r, tc) if w.ndim == 2 else (r, 1, tc)

    def body(p_ref, w_ref, m_ref, v_ref, g_ref, d_ref, nm_ref, nv_ref):
        g = p_ref[0].astype(F32)
        for s in range(1, N_DEV):
            g = g + p_ref[s].astype(F32)
        flat = lambda ref: ref[...].reshape(r, tc)
        d, nm, nv = _adamw_math(flat(w_ref), g, flat(m_ref), flat(v_ref))
        for ref, val in ((g_ref, g), (d_ref, d), (nm_ref, nm), (nv_ref, nv)):
            ref[...] = val.reshape(blk_shape)

    blk = pl.BlockSpec(blk_shape, (lambda i: (0, i)) if w.ndim == 2 else (lambda i: (0, 0, i)))
    out = jax.ShapeDtypeStruct(w.shape, F32)
    return pl.pallas_call(
        body,
        name=name,
        grid=(c // tc,),
        in_specs=[pl.BlockSpec((N_DEV, r, tc), lambda i: (0, 0, i)), blk, blk, blk],
        out_specs=[blk, blk, blk, blk],
        out_shape=[out, out, out, out],
        compiler_params=_cparams("parallel"),
    )(parts, w, m, v)


def _lane_offsets(sizes):
    offsets, pos = [], 0
    for n in sizes:
        offsets.append(pos)
        pos += -(-n // 128) * 128
    return offsets, pos


def _pack_row(parts):
    rows = [p.reshape(1, -1).astype(F32) for p in parts]
    return jnp.concatenate([jnp.pad(r, ((0, 0), (0, -r.shape[1] % 128))) for r in rows], axis=1)


def _small_update(parts, me, full_sizes, ws, ms, vs):
    n = len(ws)
    offsets, _ = _lane_offsets([1] + list(full_sizes))

    def body(me_ref, p_ref, *refs):
        w_refs, m_refs, v_refs = refs[:n], refs[n:2 * n], refs[2 * n:3 * n]
        scalar_ref, out_refs = refs[3 * n], refs[3 * n + 1:]
        tot = p_ref[0]
        for s in range(1, N_DEV):
            tot = tot + p_ref[s]
        scalar_ref[...] = tot[:, 0:1]
        for k in range(n):
            g_ref, d_ref, nm_ref, nv_ref = out_refs[4 * k:4 * k + 4]
            taps, cols = w_refs[k].shape
            if taps == 1:
                g_ref[...] = tot[:, offsets[k + 1]:offsets[k + 1] + cols]
            else:
                full = full_sizes[k] // taps
                for tap in range(taps):
                    mine = jnp.zeros((1, cols), F32)
                    for d in range(N_DEV):
                        lo = offsets[k + 1] + tap * full + d * cols
                        mine = jnp.where(me_ref[0] == d, tot[:, lo:lo + cols], mine)
                    g_ref[tap:tap + 1, :] = mine
            d_ref[...], nm_ref[...], nv_ref[...] = _adamw_math(w_refs[k][...], g_ref[...], m_refs[k][...], v_refs[k][...])

    vmem = pl.BlockSpec(memory_space=pltpu.VMEM)
    out_shape = [jax.ShapeDtypeStruct((1, 1), F32)]
    for wk in ws:
        out_shape += [jax.ShapeDtypeStruct(wk.shape, F32)] * 4
    res = pl.pallas_call(
        body,
        name="small_update",
        in_specs=[pl.BlockSpec(memory_space=pltpu.SMEM)] + [vmem] * (1 + 3 * n),
        out_specs=[vmem] * len(out_shape),
        out_shape=out_shape,
    )(me, parts, *ws, *ms, *vs)
    return res[0], [res[1 + 4 * k:5 + 4 * k] for k in range(n)]


ANY = pl.BlockSpec(memory_space=pl.ANY)
FLIPS = [(k >> 2 & 1, k >> 1 & 1, k & 1) for k in range(1, N_DEV)]


def _place():
    return lax.axis_index("x"), lax.axis_index("y"), lax.axis_index("c")


HBM = pl.BlockSpec(memory_space=pltpu.HBM)
SEM = pl.BlockSpec(memory_space=pltpu.SEMAPHORE)
EFFECT = pltpu.SideEffectType.DATAFLOW_SIDE_EFFECTING


def _peer_copy(gather, src_ref, land_ref, send_sems, recv_sems, k, sending):
    x, y, c = _place()
    fx, fy, fc = FLIPS[k]
    me = 4 * x + 2 * y + c
    peer = 4 * (x ^ fx) + 2 * (y ^ fy) + (c ^ fc)
    return pltpu.make_async_remote_copy(
        src_ref=src_ref if gather else src_ref.at[peer],
        dst_ref=land_ref.at[me if sending else peer],
        send_sem=send_sems.at[k], recv_sem=recv_sems.at[k],
        device_id=(x ^ fx, y ^ fy, c ^ fc), device_id_type=MESH)


SIBLING = 0
OTHER_CHIPS = (1, 3, 5)


def _gather_start(srcs, name, via_sibling=()):
    n = len(srcs)
    lands = [lax.empty((N_DEV,) + s.shape, s.dtype) for s in srcs]

    def body(*refs):
        src_refs, land_refs = refs[:n], refs[n:2 * n]
        send, recv = refs[2 * n:3 * n], refs[3 * n:4 * n]
        local_sems = refs[6 * n]
        x, y, c = _place()
        me = 4 * x + 2 * y + c
        local = [pltpu.make_async_copy(src_refs[i], land_refs[i].at[me], local_sems.at[i]) for i in range(n)]
        for cp in local:
            cp.start()
        for i in range(n):
            for k in (SIBLING,) + OTHER_CHIPS if i in via_sibling else range(N_DEV - 1):
                _peer_copy(True, src_refs[i], land_refs[i], send[i], recv[i], k, True).start()
        for cp in local:
            cp.wait()

    sem = pltpu.SemaphoreType.DMA((N_DEV - 1,))
    hbm = lambda a: pltpu.HBM(a.shape, a.dtype)
    res = pl.pallas_call(
        body,
        name=name,
        in_specs=[HBM] * (2 * n),
        out_specs=[SEM] * (2 * n) + [HBM] * (2 * n),
        out_shape=[sem] * (2 * n) + [hbm(s) for s in srcs] + [hbm(a) for a in lands],
        input_output_aliases={i: 2 * n + i for i in range(2 * n)},
        scratch_shapes=[pltpu.SemaphoreType.DMA((n,))],
        compiler_params=pltpu.CompilerParams(has_side_effects=EFFECT),
    )(*[pltpu.with_memory_space_constraint(a, pltpu.HBM) for a in list(srcs) + lands])
    return res[:n], res[n:2 * n], res[2 * n:3 * n], res[3 * n:4 * n]


def _exchange_wait(send_sems, recv_sems, src, land, after, gather, name):
    def body(src_ref, land_ref, send_ref, recv_ref, after_ref, src_out, land_out):
        for k in range(N_DEV - 1):
            cp = _peer_copy(gather, src_ref, land_ref, send_ref, recv_ref, k, False)
            cp.wait_send()
            cp.wait_recv()

    hbm = lambda a: pltpu.HBM(a.shape, a.dtype)
    return pl.pallas_call(
        body,
        name=name,
        in_specs=[HBM, HBM, SEM, SEM, ANY],
        out_specs=[HBM, HBM],
        out_shape=[hbm(src), hbm(land)],
        input_output_aliases={0: 0, 1: 1},
        compiler_params=pltpu.CompilerParams(has_side_effects=EFFECT),
    )(src, land, send_sems, recv_sems, after)[1]


def _forwarded_copy(land_ref, send_sems, recv_sems, j, sending):
    x, y, c = _place()
    fx, fy, _ = FLIPS[OTHER_CHIPS[j]]
    slot = 4 * (x ^ fx) + 2 * (y ^ fy) + (c if sending else 1 - c)
    return pltpu.make_async_remote_copy(
        src_ref=land_ref.at[slot], dst_ref=land_ref.at[slot], send_sem=send_sems.at[j], recv_sem=recv_sems.at[j],
        device_id=(x, y, 1 - c), device_id_type=MESH)


def _gather_forward(send_sems, recv_sems, src, land, name):
    def body(src_ref, land_ref, send_ref, recv_ref, fwd_send, fwd_recv, src_out, land_out):
        for j, k in enumerate(OTHER_CHIPS):
            _peer_copy(True, src_ref, land_ref, send_ref, recv_ref, k, False).wait_recv()
            _forwarded_copy(land_ref, fwd_send, fwd_recv, j, True).start()

    sem = pltpu.SemaphoreType.DMA((len(OTHER_CHIPS),))
    hbm = lambda a: pltpu.HBM(a.shape, a.dtype)
    return pl.pallas_call(
        body,
        name=name,
        in_specs=[HBM, HBM, SEM, SEM],
        out_specs=[SEM, SEM, HBM, HBM],
        out_shape=[sem, sem, hbm(src), hbm(land)],
        input_output_aliases={0: 2, 1: 3},
        compiler_params=pltpu.CompilerParams(has_side_effects=EFFECT),
    )(src, land, send_sems, recv_sems)


def _gather_wait_forwarded(send_sems, recv_sems, fwd_send, fwd_recv, src, land, after, name):
    def body(src_ref, land_ref, send_ref, recv_ref, fwd_send_ref, fwd_recv_ref, after_ref, src_out, land_out):
        for k in (SIBLING,) + OTHER_CHIPS:
            _peer_copy(True, src_ref, land_ref, send_ref, recv_ref, k, False).wait_send()
        _peer_copy(True, src_ref, land_ref, send_ref, recv_ref, SIBLING, False).wait_recv()
        for j in range(len(OTHER_CHIPS)):
            _forwarded_copy(land_ref, fwd_send_ref, fwd_recv_ref, j, True).wait_send()
            _forwarded_copy(land_ref, fwd_send_ref, fwd_recv_ref, j, False).wait_recv()

    hbm = lambda a: pltpu.HBM(a.shape, a.dtype)
    return pl.pallas_call(
        body,
        name=name,
        in_specs=[HBM, HBM, SEM, SEM, SEM, SEM, ANY],
        out_specs=[HBM, HBM],
        out_shape=[hbm(src), hbm(land)],
        input_output_aliases={0: 0, 1: 1},
        compiler_params=pltpu.CompilerParams(has_side_effects=EFFECT),
    )(src, land, send_sems, recv_sems, fwd_send, fwd_recv, after)[1]


def _col(v):
    return v.reshape(-1, 1).astype(F32)


def _local_step(xt, tgt, weight, small):
    t = xt.shape[1]
    n1 = _col(small["norm1_w"])
    n2 = _col(small["norm2_w"])
    nf = _col(small["final_norm_w"])
    bg = _col(small["b_gate"])
    sinks = small["attn_sinks"].reshape(-1).astype(F32)
    cbias = _col(small["ssd_conv_b"])
    dtb = _col(small["dt_bias"])
    alog = _col(small["a_log"])
    dsk = _col(small["d_skip"])
    gnw = _col(small["ssd_norm_w"])
    fb = small["ffn_conv_b"].reshape(2, D_FF, 1)

    xn = _norm_fwd(xt, n1, "norm1_fwd")
    cw = weight("ssd_conv_w", xn).T
    fw = weight("ffn_conv_w", xn).T.reshape(2, D_FF, FFN_CONV)
    w_in_t = weight("w_in", xn)
    proj = _matmul(w_in_t, xn, nt=False, out_dtype=F32, name="mm_in")
    ao, lse = _attn_fwd(proj, sinks)
    w_ao = weight("w_attn_o", ao)
    attn = _matmul(w_ao, ao, nt=False, out_dtype=F32, name="mm_attn_o", tn_a=True)
    xbc = _conv_silu_fwd(proj, cw, cbias)
    y, hst = _ssd_fwd(xbc, proj, dtb, alog, dsk)
    yn = _gnorm_fwd(y, proj, gnw)
    w_so = weight("w_ssd_o", yn)
    ssd = _matmul(w_so, yn, nt=False, out_dtype=F32, name="mm_ssd_o", tn_a=True)
    mix = _gate_fwd(proj, bg, attn, ssd)
    w_out = weight("w_out", mix)
    h1 = _matmul(w_out, mix, nt=False, out_dtype=F32, name="mm_out", add=xt, tn_a=True)
    hn = _norm_fwd(h1, n2, "norm2_fwd")
    w_up_t = weight("w_up", hn)
    u0 = _matmul(w_up_t, hn, nt=False, out_dtype=F32, name="mm_up").reshape(2, D_FF, t)
    gl = _ffn_fwd(u0, fw, fb)
    w_down = weight("w_down", gl)
    h2 = _matmul(w_down, gl, nt=False, out_dtype=F32, name="mm_down", add=h1, tn_a=True)
    dh2, loss, d_nf = _final_norm_loss(h2, tgt, nf)

    g = {}
    handles = {}

    def sending(weight_name, grad, fn, *args, **kwargs):
        out, handles[weight_name] = fn(*args, send=grad.reshape(N_DEV, -1, D_MODEL), **kwargs)
        return out

    g_down = _matmul(gl, dh2, nt=True, out_dtype=BF16, name="mm_d_w_down")
    dgl = _matmul(w_down, dh2, nt=False, out_dtype=F32, name="mm_d_glu")
    du0, d_fwb = sending("w_down", g_down, _ffn_bwd, u0, fw, fb, dgl)
    du0 = du0.reshape(2 * D_FF, t)
    g_up = _matmul(du0, hn, nt=True, out_dtype=BF16, name="mm_d_w_up")
    dhn = sending("w_up", g_up, _matmul, w_up_t, du0, nt=False, out_dtype=F32, name="mm_d_hn", tn_a=True)
    dh1, d_n2 = _norm_bwd(dhn, h1, n2, dh2, "norm2_bwd")
    g_out = _matmul(mix, dh1, nt=True, out_dtype=BF16, name="mm_d_w_out")
    dmix = _matmul(w_out, dh1, nt=False, out_dtype=F32, name="mm_d_mix")
    d_attn, d_ssd, d_ga, d_gs, d_ba, d_bs = sending("w_out", g_out, _gate_bwd, proj, bg, attn, ssd, dmix)
    g_ao = _matmul(ao, d_attn, nt=True, out_dtype=BF16, name="mm_d_w_attn_o")
    dao = _matmul(w_ao, d_attn, nt=False, out_dtype=F32, name="mm_d_ao")
    dq, dk, dv, d_sinks = sending("w_attn_o", g_ao, _attn_bwd, proj, sinks, ao, lse, dao)
    g_so = _matmul(yn, d_ssd, nt=True, out_dtype=BF16, name="mm_d_w_ssd_o")
    dyn = _matmul(w_so, d_ssd, nt=False, out_dtype=F32, name="mm_d_yn")
    dy, dz, d_gnw = sending("w_ssd_o", g_so, _gnorm_bwd, dyn, y, proj, gnw)
    dxs, dbm, dcm, ddt, d_alog, d_dsk, d_dtb = _ssd_bwd(xbc, proj, dtb, alog, dsk, hst, dy)
    dx_xs, dwb_xs = _conv_silu_bwd(proj, cw, cbias, dxs, 0, "ssd_conv_bwd_x")
    dx_b, dwb_b = _conv_silu_bwd(proj, cw, cbias, dbm, D_INNER, "ssd_conv_bwd_b")
    dx_c, dwb_c = _conv_silu_bwd(proj, cw, cbias, dcm, D_INNER + BC_DIM, "ssd_conv_bwd_c")
    dwb_conv = jnp.concatenate([dwb_xs, dwb_b, dwb_c], axis=0)
    dproj = jnp.concatenate([dq, dk, dv, dz, dx_xs, dx_b, dx_c, ddt.astype(BF16), d_ga, d_gs], axis=0)
    g_in = _matmul(dproj, xn, nt=True, out_dtype=BF16, name="mm_d_w_in")
    dxn = sending("w_in", g_in, _matmul, w_in_t, dproj, nt=False, out_dtype=F32, name="mm_d_xn", tn_a=True)
    dx, d_n1 = _norm_bwd(dxn, xt, n1, dh1, "norm1_bwd")

    g["norm1_w"] = d_n1
    g["b_gate"] = jnp.concatenate([d_ba, d_bs], axis=0)
    g["attn_sinks"] = d_sinks
    g["ssd_conv_w"] = dwb_conv[:, :SSD_CONV].T
    g["ssd_conv_b"] = dwb_conv[:, SSD_CONV]
    g["dt_bias"] = d_dtb
    g["a_log"] = d_alog
    g["d_skip"] = d_dsk
    g["ssd_norm_w"] = d_gnw
    g["norm2_w"] = d_n2
    d_fwb = d_fwb.reshape(2 * D_FF, 128)
    g["ffn_conv_w"] = d_fwb[:, :FFN_CONV].T
    g["ffn_conv_b"] = d_fwb[:, FFN_CONV]
    g["final_norm_w"] = d_nf
    return loss, dx, g, handles


SMALL = ("norm1_w", "b_gate", "attn_sinks", "ssd_conv_w", "ssd_conv_b", "dt_bias", "a_log", "d_skip", "ssd_norm_w",
         "norm2_w", "ffn_conv_w", "ffn_conv_b", "final_norm_w")
WEIGHT_ORDER = ("norm1_w", "w_in", "b_gate", "attn_sinks", "w_attn_o", "ssd_conv_w", "ssd_conv_b", "dt_bias", "a_log",
                "d_skip", "ssd_norm_w", "w_ssd_o", "w_out", "norm2_w", "w_up", "ffn_conv_w", "ffn_conv_b", "w_down",
                "final_norm_w")


def kernel(x, norm1_w, w_in, b_gate, attn_sinks, w_attn_o, ssd_conv_w, ssd_conv_b, dt_bias, a_log, d_skip, ssd_norm_w, w_ssd_o, w_out, norm2_w, w_up, ffn_conv_w, ffn_conv_b, w_down, final_norm_w, loss_target, m_norm1_w, m_w_in, m_b_gate, m_attn_sinks, m_w_attn_o, m_ssd_conv_w, m_ssd_conv_b, m_dt_bias, m_a_log, m_d_skip, m_ssd_norm_w, m_w_ssd_o, m_w_out, m_norm2_w, m_w_up, m_ffn_conv_w, m_ffn_conv_b, m_w_down, m_final_norm_w, v_norm1_w, v_w_in, v_b_gate, v_attn_sinks, v_w_attn_o, v_ssd_conv_w, v_ssd_conv_b, v_dt_bias, v_a_log, v_d_skip, v_ssd_norm_w, v_w_ssd_o, v_w_out, v_norm2_w, v_w_up, v_ffn_conv_w, v_ffn_conv_b, v_w_down, v_final_norm_w):
    w = dict(norm1_w=norm1_w, w_in=w_in, b_gate=b_gate, attn_sinks=attn_sinks, w_attn_o=w_attn_o, ssd_conv_w=ssd_conv_w, ssd_conv_b=ssd_conv_b, dt_bias=dt_bias, a_log=a_log, d_skip=d_skip, ssd_norm_w=ssd_norm_w, w_ssd_o=w_ssd_o, w_out=w_out, norm2_w=norm2_w, w_up=w_up, ffn_conv_w=ffn_conv_w, ffn_conv_b=ffn_conv_b, w_down=w_down, final_norm_w=final_norm_w)
    m = dict(norm1_w=m_norm1_w, w_in=m_w_in, b_gate=m_b_gate, attn_sinks=m_attn_sinks, w_attn_o=m_w_attn_o, ssd_conv_w=m_ssd_conv_w, ssd_conv_b=m_ssd_conv_b, dt_bias=m_dt_bias, a_log=m_a_log, d_skip=m_d_skip, ssd_norm_w=m_ssd_norm_w, w_ssd_o=m_w_ssd_o, w_out=m_w_out, norm2_w=m_norm2_w, w_up=m_w_up, ffn_conv_w=m_ffn_conv_w, ffn_conv_b=m_ffn_conv_b, w_down=m_w_down, final_norm_w=m_final_norm_w)
    v = dict(norm1_w=v_norm1_w, w_in=v_w_in, b_gate=v_b_gate, attn_sinks=v_attn_sinks, w_attn_o=v_w_attn_o, ssd_conv_w=v_ssd_conv_w, ssd_conv_b=v_ssd_conv_b, dt_bias=v_dt_bias, a_log=v_a_log, d_skip=v_d_skip, ssd_norm_w=v_ssd_norm_w, w_ssd_o=v_w_ssd_o, w_out=v_w_out, norm2_w=v_norm2_w, w_up=v_w_up, ffn_conv_w=v_ffn_conv_w, ffn_conv_b=v_ffn_conv_b, w_down=v_w_down, final_norm_w=v_final_norm_w)
    me = 4 * lax.axis_index("x") + 2 * lax.axis_index("y") + lax.axis_index("c")

    shards = {"ssd_conv_w": ssd_conv_w[0], "ffn_conv_w": ffn_conv_w[0], "w_in": w_in[0].T.astype(BF16),
              "w_attn_o": w_attn_o[0].astype(BF16), "w_ssd_o": w_ssd_o[0].astype(BF16), "w_out": w_out[0].astype(BF16),
              "w_up": w_up[0].T.astype(BF16), "w_down": w_down[0].astype(BF16)}
    order = list(shards)
    i_in = order.index("w_in")
    g_send, g_recv, g_src, g_land = _gather_start(list(shards.values()), "gather_start", via_sibling=(i_in,))
    f_send, f_recv, in_src, in_land = _gather_forward(g_send[i_in], g_recv[i_in], g_src[i_in], g_land[i_in],
                                                      "gather_forward_w_in")

    def weight(name, after):
        i = order.index(name)
        if i == i_in:
            land = _gather_wait_forwarded(g_send[i], g_recv[i], f_send, f_recv, in_src, in_land, after,
                                          "gather_wait_" + name)
        else:
            land = _exchange_wait(g_send[i], g_recv[i], g_src[i], g_land[i], after, True, "gather_wait_" + name)
        if name == "ssd_conv_w":
            return jnp.transpose(land, (1, 0, 2)).reshape(SSD_CONV, XBC_DIM)
        if name == "ffn_conv_w":
            return jnp.transpose(land, (1, 0, 2)).reshape(FFN_CONV, 2 * D_FF)
        return land.reshape(-1, D_MODEL)

    small = {k: w[k][0] if k != "final_norm_w" else w[k] for k in SMALL}
    loss, dx, g, pending = _local_step(x[0].T, loss_target[0].T, weight, small)

    packed = _pack_row([loss] + [g[k] for k in SMALL])
    s_send, s_recv, s_src, s_land = _gather_start([packed], "small_grads_start")

    res = {}
    after = s_src[0]
    for name in ("w_down", "w_up", "w_out", "w_attn_o", "w_ssd_o", "w_in"):
        parts = _exchange_wait(*pending[name], after, False, "grad_wait_" + name)
        view, back = {
            "w_in": (lambda a: jnp.transpose(a, (2, 0, 1)), lambda r: jnp.transpose(r, (1, 2, 0))),
            "w_up": (lambda a: a[0].T, lambda r: r.T[None]),
        }.get(name, (lambda a: a[0], lambda r: r[None]))
        res[name] = _adamw_sharded(parts, view(w[name]), view(m[name]), view(v[name]), "adamw_" + name)
        after = res[name][0]
        res[name] = [back(r) for r in res[name]]

    rows = _exchange_wait(s_send[0], s_recv[0], s_src[0], s_land[0], after, True, "small_grads_wait")
    flat = lambda a: a.reshape(-1, a.shape[-1])
    loss_sum, updates = _small_update(
        rows, me.reshape(1), [g[k].size for k in SMALL],
        [flat(w[k]) for k in SMALL], [flat(m[k]) for k in SMALL], [flat(v[k]) for k in SMALL])
    for k, upd in zip(SMALL, updates):
        res[k] = [u.reshape(w[k].shape) for u in upd]

    grad_x = dx.T[None]
    outs = [loss_sum.reshape(()), grad_x]
    for i in range(4):
        outs.extend(res[k][i] for k in WEIGHT_ORDER)
    return tuple(outs)
```

```python
import functools

import jax
import jax.numpy as jnp
from jax import lax
from jax.experimental import pallas as pl
from jax.experimental.pallas import tpu as pltpu

F32 = jnp.float32
BF16 = jnp.bfloat16
HIGHEST = lax.Precision.HIGHEST

D_MODEL = 1024
N_Q_HEADS = 16
N_KV_HEADS = 4
HEAD_DIM = 64
WINDOW = 128
Q_PER_KV = N_Q_HEADS // N_KV_HEADS
Q_DIM = N_Q_HEADS * HEAD_DIM
KV_DIM = N_KV_HEADS * HEAD_DIM
D_INNER = 2048
SSD_HEAD_DIM = 64
N_SSD_HEADS = 32
N_SSD_GROUPS = 4
HEADS_PER_GROUP = N_SSD_HEADS // N_SSD_GROUPS
D_STATE = 128
BC_DIM = N_SSD_GROUPS * D_STATE
XBC_DIM = D_INNER + 2 * BC_DIM
SSD_CONV = 4
CHUNK = 128
D_FF = 2816
FFN_CONV = 3
EPS = 1e-5
NEG = -1e30
IN_DIM = 8736
N_DEV = 8

OFF_Q = 0
OFF_K = OFF_Q + Q_DIM
OFF_V = OFF_K + KV_DIM
OFF_Z = OFF_V + KV_DIM
OFF_X = OFF_Z + D_INNER
OFF_DT = OFF_X + XBC_DIM
OFF_GA = OFF_DT + N_SSD_HEADS
OFF_GS = OFF_GA + D_MODEL

ADAM_LR = 0.001
ADAM_B1 = 0.9
ADAM_B2 = 0.999
ADAM_EPS = 1e-08
ADAM_WD = 0.01
ADAM_STEP = 10

VMEM_LIMIT = 48 * 1024 * 1024
MESH = pl.DeviceIdType.MESH


def _cparams(*sem):
    return pltpu.CompilerParams(dimension_semantics=sem, vmem_limit_bytes=VMEM_LIMIT)


def _tile(n, prefs):
    for p in prefs:
        if n % p == 0:
            return p
    return n


def _sigmoid(x):
    return 1.0 / (1.0 + jnp.exp(-x))


def _softplus(x):
    return jnp.maximum(x, 0.0) + jnp.log(1.0 + jnp.exp(-jnp.abs(x)))


def _rowsum(x):
    return jnp.sum(x, axis=1, keepdims=True)


def _colsum(x):
    return jnp.sum(x, axis=0, keepdims=True)


def _dot(a, b):
    return jnp.dot(a, b, preferred_element_type=F32)


def _dot_nt(a, b):
    return lax.dot_general(a, b, (((1,), (1,)), ((), ())), preferred_element_type=F32)


def _dot_tn(a, b):
    return lax.dot_general(a, b, (((0,), (0,)), ((), ())), preferred_element_type=F32)


def _shift_right(x, j):
    if j == 0:
        return x
    r = pltpu.roll(x, j, 1)
    lane = lax.broadcasted_iota(jnp.int32, (x.shape[0], 128), 1)
    return jnp.concatenate([jnp.where(lane >= j, r[:, :128], 0.0), r[:, 128:]], axis=1)


def _shift_left(x, j):
    if j == 0:
        return x
    n = x.shape[1]
    r = pltpu.roll(x, n - j, 1)
    lane = lax.broadcasted_iota(jnp.int32, (x.shape[0], 128), 1)
    return jnp.concatenate([r[:, :n - 128], jnp.where(lane < 128 - j, r[:, n - 128:], 0.0)], axis=1)


def _causal_conv(xv, wv, bv):
    taps = wv.shape[1]
    shifted = [_shift_right(xv, taps - 1 - k) for k in range(taps - 1)]
    y = bv + wv[:, taps - 1:taps] * xv
    for k in range(taps - 1):
        y = y + wv[:, k:k + 1] * shifted[k]
    return y, shifted


def _causal_conv_bwd(dy, xv, shifted, wv):
    taps = wv.shape[1]
    lane = lax.broadcasted_iota(jnp.int32, (dy.shape[0], 128), 1)
    dwb = jnp.where(lane == taps, _rowsum(dy), 0.0)
    dwb = jnp.where(lane == taps - 1, _rowsum(dy * xv), dwb)
    dx = wv[:, taps - 1:taps] * dy
    for k in range(taps - 1):
        dx = dx + wv[:, k:k + 1] * _shift_left(dy, taps - 1 - k)
        dwb = jnp.where(lane == k, _rowsum(dy * shifted[k]), dwb)
    return dx, dwb


def _call(body, *, name, grid, in_specs, out_specs, out_shape, args, semantics, scratch_shapes=(), send=None):
    if send is None:
        return pl.pallas_call(body, name=name, grid=grid, in_specs=in_specs, out_specs=out_specs, out_shape=out_shape,
                              scratch_shapes=list(scratch_shapes), compiler_params=_cparams(*semantics))(*args)
    single = not isinstance(out_specs, (list, tuple))
    out_specs, out_shape = ([out_specs], [out_shape]) if single else (list(out_specs), list(out_shape))
    n_in, n_out, n_scr = len(in_specs), len(out_specs), len(scratch_shapes)
    steps = 1
    for size in grid:
        steps *= size

    def sending(*refs):
        ins, (src_ref, land_ref) = refs[:n_in], refs[n_in:n_in + 2]
        outs = refs[n_in + 2:n_in + 2 + n_out]
        send_sems, recv_sems = refs[n_in + 2 + n_out:n_in + 4 + n_out]
        scratch, local_sem = refs[n_in + 6 + n_out:n_in + 6 + n_out + n_scr], refs[-1]
        x, y, c = _place()
        me = 4 * x + 2 * y + c
        local = pltpu.make_async_copy(src_ref.at[me], land_ref.at[me], local_sem)
        step = 0
        for axis, size in enumerate(grid):
            step = step * size + pl.program_id(axis)

        @pl.when(step == 0)
        def _():
            local.start()
            for peer in range(N_DEV - 1):
                _peer_copy(False, src_ref, land_ref, send_sems, recv_sems, peer, True).start()

        body(*ins, *outs, *scratch)

        @pl.when(step == steps - 1)
        def _():
            local.wait()

    sem = pltpu.SemaphoreType.DMA((N_DEV - 1,))
    hbm = pltpu.HBM(send.shape, send.dtype)
    res = pl.pallas_call(
        sending, name=name, grid=grid,
        in_specs=list(in_specs) + [HBM, HBM],
        out_specs=out_specs + [SEM, SEM, HBM, HBM],
        out_shape=out_shape + [sem, sem, hbm, hbm],
        input_output_aliases={n_in: n_out + 2, n_in + 1: n_out + 3},
        scratch_shapes=list(scratch_shapes) + [pltpu.SemaphoreType.DMA(())],
        compiler_params=pltpu.CompilerParams(dimension_semantics=("arbitrary",) * len(grid), vmem_limit_bytes=VMEM_LIMIT,
                                             has_side_effects=EFFECT),
    )(*args, pltpu.with_memory_space_constraint(send, pltpu.HBM),
      pltpu.with_memory_space_constraint(lax.empty(send.shape, send.dtype), pltpu.HBM))
    return (res[0] if single else list(res[:n_out])), tuple(res[n_out:])


MATMUL_VMEM_BUDGET = 36 * 1024 * 1024
MATMUL_MAX_TK = 3072


MATMUL_MAX_TM = 768


def _largest_tile(n, align, cap):
    return max(d for d in range(align, min(n, cap) + 1, align) if n % d == 0)


def _matmul_tiles(m, n, k, a_bytes, b_bytes, out_bytes, has_add, m_align, k_align):
    tm = _largest_tile(m, m_align, MATMUL_MAX_TM)
    tk = _largest_tile(k, k_align, MATMUL_MAX_TK)
    for tn in sorted({d for d in range(128, n + 1, 128) if n % d == 0}, reverse=True):
        need = 2 * (tm * tk * a_bytes + tk * tn * b_bytes) + tm * tn * (2 * out_bytes + (4 if k > tk else 0) + (8 if has_add else 0))
        if tn <= 3072 and need <= MATMUL_VMEM_BUDGET:
            return tm, tn, tk
    return tm, 128, tk


def _matmul(a, b, *, nt, out_dtype, name, add=None, tn_a=False, send=None):
    if tn_a:
        k, m = a.shape
    else:
        m, k = a.shape
    n = b.shape[0] if nt else b.shape[1]
    tm, tn, tk = _matmul_tiles(m, n, k, a.dtype.itemsize, b.dtype.itemsize, jnp.dtype(out_dtype).itemsize, add is not None,
                               128 if tn_a else 16, 16 if tn_a and not nt else 128)
    nk = k // tk
    grid = (m // tm, n // tn, nk)

    def body(a_ref, b_ref, *rest):
        r_ref = None
        if add is not None:
            r_ref, rest = rest[0], rest[1:]
        o_ref = rest[0]
        av = a_ref[...].astype(BF16)
        bv = b_ref[...].astype(BF16)
        part = _dot_tn(av, bv) if tn_a else _dot_nt(av, bv) if nt else _dot(av, bv)

        def finish(r):
            if add is not None:
                r = r + r_ref[...]
            o_ref[...] = r.astype(out_dtype)

        if nk == 1:
            finish(part)
            return
        acc = rest[1]
        kk = pl.program_id(2)

        @pl.when(kk == 0)
        def _():
            acc[...] = part

        @pl.when((kk > 0) & (kk < nk - 1))
        def _():
            acc[...] += part

        @pl.when(kk == nk - 1)
        def _():
            finish(acc[...] + part)

    in_specs = [
        pl.BlockSpec((tk, tm), lambda i, j, kk: (kk, i)) if tn_a else pl.BlockSpec((tm, tk), lambda i, j, kk: (i, kk)),
        pl.BlockSpec((tn, tk), lambda i, j, kk: (j, kk)) if nt else pl.BlockSpec((tk, tn), lambda i, j, kk: (kk, j)),
    ]
    args = [a, b]
    if add is not None:
        in_specs.append(pl.BlockSpec((tm, tn), lambda i, j, kk: (i, j)))
        args.append(add)
    return _call(
        body, name=name, grid=grid, in_specs=in_specs, args=args,
        out_specs=pl.BlockSpec((tm, tn), lambda i, j, kk: (i, j)),
        out_shape=jax.ShapeDtypeStruct((m, n), out_dtype),
        scratch_shapes=[pltpu.VMEM((tm, tn), F32)] if nk > 1 else [],
        semantics=("parallel", "parallel", "arbitrary"), send=send)


def _norm_fwd(x, w_col, name):
    f, t = x.shape
    tt = _tile(t, (512, 256, 128))

    def body(x_ref, w_ref, o_ref):
        xv = x_ref[...]
        r = lax.rsqrt(jnp.mean(xv * xv, axis=0, keepdims=True) + EPS)
        o_ref[...] = (xv * r * w_ref[...]).astype(BF16)

    return pl.pallas_call(
        body,
        name=name,
        grid=(t // tt,),
        in_specs=[pl.BlockSpec((f, tt), lambda i: (0, i)), pl.BlockSpec((f, 1), lambda i: (0, 0))],
        out_specs=pl.BlockSpec((f, tt), lambda i: (0, i)),
        out_shape=jax.ShapeDtypeStruct((f, t), BF16),
        compiler_params=_cparams("parallel"),
    )(x, w_col)


def _norm_bwd(dy, x, w_col, res, name):
    f, t = x.shape
    tt = _tile(t, (512, 256, 128))

    def body(dy_ref, x_ref, w_ref, res_ref, dx_ref, dw_ref):
        @pl.when(pl.program_id(0) == 0)
        def _():
            dw_ref[...] = jnp.zeros_like(dw_ref)

        xv = x_ref[...]
        r = lax.rsqrt(jnp.mean(xv * xv, axis=0, keepdims=True) + EPS)
        xhat = xv * r
        dyv = dy_ref[...]
        dw_ref[...] += _rowsum(dyv * xhat)
        dxhat = dyv * w_ref[...]
        dx_ref[...] = res_ref[...] + r * (dxhat - xhat * jnp.mean(dxhat * xhat, axis=0, keepdims=True))

    blk = pl.BlockSpec((f, tt), lambda i: (0, i))
    col = pl.BlockSpec((f, 1), lambda i: (0, 0))
    return pl.pallas_call(
        body,
        name=name,
        grid=(t // tt,),
        in_specs=[blk, blk, col, blk],
        out_specs=[blk, col],
        out_shape=[jax.ShapeDtypeStruct((f, t), F32), jax.ShapeDtypeStruct((f, 1), F32)],
        compiler_params=_cparams("arbitrary"),
    )(dy, x, w_col, res)


def _final_norm_loss(h, tgt, w_col):
    f, t = h.shape
    tt = _tile(t, (512, 256, 128))

    def body(h_ref, t_ref, w_ref, dh_ref, loss_ref, dw_ref):
        @pl.when(pl.program_id(0) == 0)
        def _():
            dw_ref[...] = jnp.zeros_like(dw_ref)
            loss_ref[...] = jnp.zeros_like(loss_ref)

        xv = h_ref[...]
        r = lax.rsqrt(jnp.mean(xv * xv, axis=0, keepdims=True) + EPS)
        xhat = xv * r
        wv = w_ref[...]
        err = xhat * wv - t_ref[...]
        loss_ref[...] += 0.5 * _rowsum(jnp.mean(err * err, axis=0, keepdims=True))
        dyv = err * (1.0 / f)
        dw_ref[...] += _rowsum(dyv * xhat)
        dxhat = dyv * wv
        dh_ref[...] = r * (dxhat - xhat * jnp.mean(dxhat * xhat, axis=0, keepdims=True))

    blk = pl.BlockSpec((f, tt), lambda i: (0, i))
    col = pl.BlockSpec((f, 1), lambda i: (0, 0))
    one = pl.BlockSpec((1, 1), lambda i: (0, 0))
    return pl.pallas_call(
        body,
        name="final_norm_loss",
        grid=(t // tt,),
        in_specs=[blk, blk, col],
        out_specs=[blk, one, col],
        out_shape=[jax.ShapeDtypeStruct((f, t), F32), jax.ShapeDtypeStruct((1, 1), F32), jax.ShapeDtypeStruct((f, 1), F32)],
        compiler_params=_cparams("arbitrary"),
    )(h, tgt, w_col)


def _attn_mask(n):
    shape = (2 * WINDOW, Q_PER_KV * WINDOW)
    si = lax.broadcasted_iota(jnp.int32, shape, 0)
    qi = lax.broadcasted_iota(jnp.int32, shape, 1) & (WINDOW - 1)
    dist = WINDOW + qi - si
    return (dist >= 0) & (dist < WINDOW) & ((si >= WINDOW) | (n > 0))


def _lane_cat(ref, row0, rows):
    return jnp.concatenate([ref[row0 + i * rows:row0 + (i + 1) * rows, :] for i in range(Q_PER_KV)], axis=1)


def _attn_fwd(proj, sinks):
    t = proj.shape[1]
    nb = t // WINDOW
    scale = HEAD_DIM ** -0.5

    def body(s_ref, q_ref, kc_ref, kp_ref, vc_ref, vp_ref, o_ref, lse_ref):
        n = pl.program_id(0)
        valid = _attn_mask(n)
        for g in range(N_KV_HEADS):
            rows = slice(g * HEAD_DIM, (g + 1) * HEAD_DIM)
            kt = jnp.concatenate([kp_ref[rows, :], kc_ref[rows, :]], axis=1).astype(BF16)
            vt = jnp.concatenate([vp_ref[rows, :], vc_ref[rows, :]], axis=1).astype(BF16)
            qcat = (_lane_cat(q_ref, g * Q_PER_KV * HEAD_DIM, HEAD_DIM) * scale).astype(BF16)
            s = jnp.where(valid, _dot_tn(kt, qcat), NEG)
            sink = jnp.concatenate(
                [jnp.full((1, WINDOW), s_ref[g * Q_PER_KV + i], F32) for i in range(Q_PER_KV)], axis=1)
            m = jnp.maximum(jnp.max(s, axis=0, keepdims=True), sink)
            p = jnp.exp(s - m)
            denom = _colsum(p) + jnp.exp(sink - m)
            probs = (p / denom).astype(BF16)
            out = _dot(vt, probs)
            lse = m + jnp.log(denom)
            for i in range(Q_PER_KV):
                h = g * Q_PER_KV + i
                o_ref[h * HEAD_DIM:(h + 1) * HEAD_DIM, :] = out[:, i * WINDOW:(i + 1) * WINDOW]
                lse_ref[h:h + 1, :] = lse[:, i * WINDOW:(i + 1) * WINDOW]

    kb = OFF_K // KV_DIM
    vb = OFF_V // KV_DIM
    prev = lambda n: jnp.maximum(n - 1, 0)
    return pl.pallas_call(
        body,
        name="attn_fwd",
        grid=(nb,),
        in_specs=[
            pl.BlockSpec(memory_space=pltpu.SMEM),
            pl.BlockSpec((Q_DIM, WINDOW), lambda n: (0, n)),
            pl.BlockSpec((KV_DIM, WINDOW), lambda n: (kb, n)),
            pl.BlockSpec((KV_DIM, WINDOW), lambda n: (kb, prev(n))),
            pl.BlockSpec((KV_DIM, WINDOW), lambda n: (vb, n)),
            pl.BlockSpec((KV_DIM, WINDOW), lambda n: (vb, prev(n))),
        ],
        out_specs=[pl.BlockSpec((Q_DIM, WINDOW), lambda n: (0, n)), pl.BlockSpec((N_Q_HEADS, WINDOW), lambda n: (0, n))],
        out_shape=[jax.ShapeDtypeStruct((Q_DIM, t), F32), jax.ShapeDtypeStruct((N_Q_HEADS, t), F32)],
        compiler_params=_cparams("parallel"),
    )(sinks, proj, proj, proj, proj, proj)


def _attn_bwd(proj, sinks, out, lse, dout, send=None):
    t = proj.shape[1]
    nb = t // WINDOW
    scale = HEAD_DIM ** -0.5

    def body(s_ref, q_ref, kc_ref, kp_ref, vc_ref, vp_ref, o_ref, lse_ref, do_ref,
             dq_ref, dk_ref, dv_ref, ds_ref, dk_carry, dv_carry):
        step = pl.program_id(0)
        n = nb - 1 - step

        @pl.when(step == 0)
        def _():
            dk_carry[...] = jnp.zeros_like(dk_carry)
            dv_carry[...] = jnp.zeros_like(dv_carry)
            ds_ref[...] = jnp.zeros_like(ds_ref)

        valid = _attn_mask(n)
        for g in range(N_KV_HEADS):
            rows = slice(g * HEAD_DIM, (g + 1) * HEAD_DIM)
            q0 = g * Q_PER_KV * HEAD_DIM
            kt = jnp.concatenate([kp_ref[rows, :], kc_ref[rows, :]], axis=1).astype(BF16)
            vt = jnp.concatenate([vp_ref[rows, :], vc_ref[rows, :]], axis=1).astype(BF16)
            qf = _lane_cat(q_ref, q0, HEAD_DIM)
            qcat = qf.astype(BF16)
            ocat = _lane_cat(o_ref, q0, HEAD_DIM)
            docat = _lane_cat(do_ref, q0, HEAD_DIM)
            dob = docat.astype(BF16)
            lse_cat = jnp.concatenate(
                [lse_ref[g * Q_PER_KV + i:g * Q_PER_KV + i + 1, :] for i in range(Q_PER_KV)], axis=1)
            sink = jnp.concatenate(
                [jnp.full((1, WINDOW), s_ref[g * Q_PER_KV + i], F32) for i in range(Q_PER_KV)], axis=1)
            s = jnp.where(valid, _dot_tn(kt, (qf * scale).astype(BF16)), NEG)
            p = jnp.exp(s - lse_cat)
            dp = _dot_tn(vt, dob)
            delta = _colsum(docat * ocat)
            dsc = (p * (dp - delta)).astype(BF16)
            dsink_row = -jnp.exp(sink - lse_cat) * delta
            dq = _dot(kt, dsc) * scale
            dk = _dot_nt(qcat, dsc) * scale
            dv = _dot_nt(dob, p.astype(BF16))
            for i in range(Q_PER_KV):
                h = g * Q_PER_KV + i
                dq_ref[h * HEAD_DIM:(h + 1) * HEAD_DIM, :] = dq[:, i * WINDOW:(i + 1) * WINDOW].astype(BF16)
                ds_ref[h:h + 1, :] += _rowsum(dsink_row[:, i * WINDOW:(i + 1) * WINDOW])
            dk_ref[rows, :] = (dk[:, WINDOW:] + dk_carry[rows, :]).astype(BF16)
            dv_ref[rows, :] = (dv[:, WINDOW:] + dv_carry[rows, :]).astype(BF16)
            dk_carry[rows, :] = dk[:, :WINDOW]
            dv_carry[rows, :] = dv[:, :WINDOW]

    kb = OFF_K // KV_DIM
    vb = OFF_V // KV_DIM
    cur = lambda i: nb - 1 - i
    prev = lambda i: jnp.maximum(nb - 2 - i, 0)
    qspec = pl.BlockSpec((Q_DIM, WINDOW), lambda i: (0, cur(i)))
    kvspec = pl.BlockSpec((KV_DIM, WINDOW), lambda i: (0, cur(i)))
    return _call(
        body,
        name="attn_bwd",
        grid=(nb,),
        in_specs=[
            pl.BlockSpec(memory_space=pltpu.SMEM),
            qspec,
            pl.BlockSpec((KV_DIM, WINDOW), lambda i: (kb, cur(i))),
            pl.BlockSpec((KV_DIM, WINDOW), lambda i: (kb, prev(i))),
            pl.BlockSpec((KV_DIM, WINDOW), lambda i: (vb, cur(i))),
            pl.BlockSpec((KV_DIM, WINDOW), lambda i: (vb, prev(i))),
            qspec,
            pl.BlockSpec((N_Q_HEADS, WINDOW), lambda i: (0, cur(i))),
            qspec,
        ],
        out_specs=[qspec, kvspec, kvspec, pl.BlockSpec((N_Q_HEADS, 1), lambda i: (0, 0))],
        out_shape=[
            jax.ShapeDtypeStruct((Q_DIM, t), BF16),
            jax.ShapeDtypeStruct((KV_DIM, t), BF16),
            jax.ShapeDtypeStruct((KV_DIM, t), BF16),
            jax.ShapeDtypeStruct((N_Q_HEADS, 1), F32),
        ],
        scratch_shapes=[pltpu.VMEM((KV_DIM, WINDOW), F32), pltpu.VMEM((KV_DIM, WINDOW), F32)],
        semantics=("arbitrary",), args=(sinks, proj, proj, proj, proj, proj, out, lse, dout), send=send)


CONV_ROWS = 256


def _conv_silu_fwd(proj, w_col, b_col):
    t = proj.shape[1]
    r0 = OFF_X // CONV_ROWS

    def body(x_ref, w_ref, b_ref, o_ref):
        def strip(rows):
            y, _ = _causal_conv(x_ref[rows, :], w_ref[rows, :], b_ref[rows, :])
            o_ref[rows, :] = y * _sigmoid(y)

        strip(slice(None))

    return pl.pallas_call(
        body,
        name="ssd_conv_fwd",
        grid=(XBC_DIM // CONV_ROWS,),
        in_specs=[
            pl.BlockSpec((CONV_ROWS, t), lambda i: (r0 + i, 0)),
            pl.BlockSpec((CONV_ROWS, SSD_CONV), lambda i: (i, 0)),
            pl.BlockSpec((CONV_ROWS, 1), lambda i: (i, 0)),
        ],
        out_specs=pl.BlockSpec((CONV_ROWS, t), lambda i: (i, 0)),
        out_shape=jax.ShapeDtypeStruct((XBC_DIM, t), F32),
        compiler_params=_cparams("parallel"),
    )(proj, w_col, b_col)


def _conv_silu_bwd(proj, w_col, b_col, dout, row0, name):
    t = proj.shape[1]
    nrows = dout.shape[0]
    p0 = (OFF_X + row0) // CONV_ROWS
    c0 = row0 // CONV_ROWS

    def body(x_ref, w_ref, b_ref, do_ref, dx_ref, dwb_ref):
        def strip(rows):
            xv = x_ref[rows, :]
            wv = w_ref[rows, :]
            y, shifted = _causal_conv(xv, wv, b_ref[rows, :])
            sg = _sigmoid(y)
            dy = do_ref[rows, :] * (sg * (1.0 + y * (1.0 - sg)))
            dx, dwb_ref[rows, :] = _causal_conv_bwd(dy, xv, shifted, wv)
            dx_ref[rows, :] = dx.astype(BF16)

        strip(slice(None))

    return pl.pallas_call(
        body,
        name=name,
        grid=(nrows // CONV_ROWS,),
        in_specs=[
            pl.BlockSpec((CONV_ROWS, t), lambda i: (p0 + i, 0)),
            pl.BlockSpec((CONV_ROWS, SSD_CONV), lambda i: (c0 + i, 0)),
            pl.BlockSpec((CONV_ROWS, 1), lambda i: (c0 + i, 0)),
            pl.BlockSpec((CONV_ROWS, t), lambda i: (i, 0)),
        ],
        out_specs=[pl.BlockSpec((CONV_ROWS, t), lambda i: (i, 0)), pl.BlockSpec((CONV_ROWS, 128), lambda i: (i, 0))],
        out_shape=[jax.ShapeDtypeStruct((nrows, t), BF16), jax.ShapeDtypeStruct((nrows, 128), F32)],
        compiler_params=_cparams("parallel"),
    )(proj, w_col, b_col, dout)


GROUP_ROWS = HEADS_PER_GROUP * SSD_HEAD_DIM


def _ssd_specs(order):
    xb = D_INNER // BC_DIM
    dtb = OFF_DT // N_SSD_HEADS
    col = pl.BlockSpec((N_SSD_HEADS, 1), lambda c: (0, 0))
    return [
        pl.BlockSpec((D_INNER, CHUNK), lambda c: (0, order(c))),
        pl.BlockSpec((BC_DIM, CHUNK), lambda c: (xb, order(c))),
        pl.BlockSpec((BC_DIM, CHUNK), lambda c: (xb + 1, order(c))),
        pl.BlockSpec((N_SSD_HEADS, CHUNK), lambda c: (dtb, order(c))),
        col, col, col,
    ]


def _ssd_common(dt_ref, dtb_ref, alog_ref):
    z = dt_ref[...] + dtb_ref[...]
    dt = _softplus(z)
    a_neg = -jnp.exp(alog_ref[...])
    d_a = dt * a_neg
    row = lax.broadcasted_iota(jnp.int32, (CHUNK, CHUNK), 0)
    colm = lax.broadcasted_iota(jnp.int32, (CHUNK, CHUNK), 1)
    upper = (row <= colm).astype(F32)
    a_cs = jnp.dot(d_a, upper, precision=HIGHEST, preferred_element_type=F32)
    a_last = _rowsum(d_a)
    return z, dt, a_neg, a_cs, a_last, row >= colm, row == colm


def _decay(a_row, causal):
    a_s = jnp.broadcast_to(a_row, (CHUNK, CHUNK))
    seg = a_s.T - a_s
    return jnp.where(causal, jnp.exp(jnp.where(causal, seg, 0.0)), 0.0)


def _ssd_fwd(xbc, proj, dtb_col, alog_col, dsk_col):
    t = xbc.shape[1]
    nc = t // CHUNK

    def body(xs_ref, b_ref, c_ref, dt_ref, dtb_ref, alog_ref, dsk_ref, y_ref, hst_ref, h_scr):
        @pl.when(pl.program_id(0) == 0)
        def _():
            h_scr[...] = jnp.zeros_like(h_scr)

        _, dt, _, a_cs, a_last, causal, _ = _ssd_common(dt_ref, dtb_ref, alog_ref)
        hst_ref[0] = h_scr[...]
        dsk = dsk_ref[...]
        for g in range(N_SSD_GROUPS):
            grows = slice(g * D_STATE, (g + 1) * D_STATE)
            bb = b_ref[grows, :].astype(BF16)
            cb_ = c_ref[grows, :].astype(BF16)
            cb = _dot_tn(cb_, bb)
            for j in range(g * HEADS_PER_GROUP, (g + 1) * HEADS_PER_GROUP):
                rows = slice(j * SSD_HEAD_DIM, (j + 1) * SSD_HEAD_DIM)
                a = a_cs[j:j + 1, :]
                m = (cb * _decay(a, causal)).astype(BF16)
                xs = xs_ref[rows, :]
                xc = xs * dt[j:j + 1, :]
                hj = h_scr[rows, :]
                y = _dot_nt(xc.astype(BF16), m) + _dot(hj.astype(BF16), cb_) * jnp.exp(a) + dsk[j:j + 1, :] * xs
                y_ref[rows, :] = y
                al = a_last[j:j + 1, :]
                w = jnp.exp(al - a)
                h_scr[rows, :] = jnp.exp(al) * hj + _dot_nt((xc * w).astype(BF16), bb)

    return pl.pallas_call(
        body,
        name="ssd_fwd",
        grid=(nc,),
        in_specs=_ssd_specs(lambda c: c),
        out_specs=[
            pl.BlockSpec((D_INNER, CHUNK), lambda c: (0, c)),
            pl.BlockSpec((1, D_INNER, D_STATE), lambda c: (c, 0, 0)),
        ],
        out_shape=[
            jax.ShapeDtypeStruct((D_INNER, t), F32),
            jax.ShapeDtypeStruct((nc, D_INNER, D_STATE), F32),
        ],
        scratch_shapes=[pltpu.VMEM((D_INNER, D_STATE), F32)],
        compiler_params=_cparams("arbitrary"),
    )(xbc, xbc, xbc, proj, dtb_col, alog_col, dsk_col)


def _ssd_bwd(xbc, proj, dtb_col, alog_col, dsk_col, hst, dy):
    t = xbc.shape[1]
    nc = t // CHUNK
    rev = lambda c: nc - 1 - c

    def body(xs_ref, b_ref, c_ref, dt_ref, dtb_ref, alog_ref, dsk_ref, hst_ref, dy_ref,
             dxs_ref, db_ref, dc_ref, ddt_ref, dalog_ref, ddsk_ref, ddtb_ref, dh_scr, da_scr, ddt_scr, dd_scr):
        @pl.when(pl.program_id(0) == 0)
        def _():
            dh_scr[...] = jnp.zeros_like(dh_scr)
            dalog_ref[...] = jnp.zeros_like(dalog_ref)
            ddsk_ref[...] = jnp.zeros_like(ddsk_ref)
            ddtb_ref[...] = jnp.zeros_like(ddtb_ref)

        z, dt, a_neg, a_cs, a_last, causal, eye = _ssd_common(dt_ref, dtb_ref, alog_ref)
        dsk = dsk_ref[...]
        last_lane = lax.broadcasted_iota(jnp.int32, (1, CHUNK), 1) == CHUNK - 1
        for g in range(N_SSD_GROUPS):
            grows = slice(g * D_STATE, (g + 1) * D_STATE)
            bb = b_ref[grows, :].astype(BF16)
            cb_ = c_ref[grows, :].astype(BF16)
            cb = _dot_tn(cb_, bb)
            dcb = jnp.zeros((CHUNK, CHUNK), F32)
            dc_acc = jnp.zeros((D_STATE, CHUNK), F32)
            db_acc = jnp.zeros((D_STATE, CHUNK), F32)
            for j in range(g * HEADS_PER_GROUP, (g + 1) * HEADS_PER_GROUP):
                rows = slice(j * SSD_HEAD_DIM, (j + 1) * SSD_HEAD_DIM)
                a = a_cs[j:j + 1, :]
                al = a_last[j:j + 1, :]
                lam = _decay(a, causal)
                mf = cb * lam
                xs = xs_ref[rows, :]
                dtj = dt[j:j + 1, :]
                xc = xs * dtj
                w = jnp.exp(al - a)
                e = jnp.exp(a)
                gam = jnp.exp(al)
                hj = hst_ref[0, rows, :]
                hjb = hj.astype(BF16)
                dyv = dy_ref[rows, :]
                dyb = dyv.astype(BF16)
                dd_scr[j:j + 1, :] = _colsum(dyv * xs)
                gb = (dyv * e).astype(BF16)
                dh_in = _dot_nt(gb, cb_)
                dc_acc = dc_acc + _dot_tn(hjb, gb)
                yoff = _dot(hjb, cb_) * e
                da = _colsum(dyv * yoff)
                dm = _dot_tn(dyb, xc.astype(BF16))
                dxc = _dot(dyb, mf.astype(BF16))
                dcb = dcb + dm * lam
                nmat = dm * mf
                rs = jnp.broadcast_to(_rowsum(nmat), (CHUNK, CHUNK))
                da = da + _colsum(jnp.where(eye, rs, 0.0)) - _colsum(nmat)
                ds = dh_scr[rows, :]
                dsb = ds.astype(BF16)
                t1 = _dot(dsb, bb)
                xcw = xc * w
                dxc = dxc + w * t1
                dww = _colsum(xcw * t1)
                da_l = _rowsum(dww) + _rowsum(_colsum(ds * hj)) * gam
                da = da - dww + jnp.where(last_lane, da_l, 0.0)
                db_acc = db_acc + _dot_tn(dsb, xcw.astype(BF16))
                dh_scr[rows, :] = gam * ds + dh_in
                dxs_ref[rows, :] = dsk[j:j + 1, :] * dyv + dxc * dtj
                da_scr[j:j + 1, :] = da
                ddt_scr[j:j + 1, :] = _colsum(dxc * xs)
            dcbb = dcb.astype(BF16)
            dc_ref[grows, :] = dc_acc + _dot_nt(bb, dcbb)
            db_ref[grows, :] = db_acc + _dot(cb_, dcbb)
        dda = jnp.dot(da_scr[...], causal.astype(F32), precision=HIGHEST, preferred_element_type=F32)
        ddt = ddt_scr[...] + dda * a_neg
        ddt_raw = ddt * _sigmoid(z)
        ddt_ref[...] = ddt_raw
        ddtb_ref[...] += _rowsum(ddt_raw)
        dalog_ref[...] += _rowsum(dda * dt) * a_neg
        ddsk_ref[...] += _rowsum(dd_scr[...])

    col = pl.BlockSpec((N_SSD_HEADS, 1), lambda c: (0, 0))
    bc = pl.BlockSpec((BC_DIM, CHUNK), lambda c: (0, rev(c)))
    xs_spec = pl.BlockSpec((D_INNER, CHUNK), lambda c: (0, rev(c)))
    small = pltpu.VMEM((N_SSD_HEADS, CHUNK), F32)
    return pl.pallas_call(
        body,
        name="ssd_bwd",
        grid=(nc,),
        in_specs=_ssd_specs(rev) + [pl.BlockSpec((1, D_INNER, D_STATE), lambda c: (rev(c), 0, 0)), xs_spec],
        out_specs=[xs_spec, bc, bc, pl.BlockSpec((N_SSD_HEADS, CHUNK), lambda c: (0, rev(c))), col, col, col],
        out_shape=[
            jax.ShapeDtypeStruct((D_INNER, t), F32),
            jax.ShapeDtypeStruct((BC_DIM, t), F32),
            jax.ShapeDtypeStruct((BC_DIM, t), F32),
            jax.ShapeDtypeStruct((N_SSD_HEADS, t), F32),
            jax.ShapeDtypeStruct((N_SSD_HEADS, 1), F32),
            jax.ShapeDtypeStruct((N_SSD_HEADS, 1), F32),
            jax.ShapeDtypeStruct((N_SSD_HEADS, 1), F32),
        ],
        scratch_shapes=[pltpu.VMEM((D_INNER, D_STATE), F32), small, small, small],
        compiler_params=_cparams("arbitrary"),
    )(xbc, xbc, xbc, proj, dtb_col, alog_col, dsk_col, hst, dy)


GN_ROWS = D_INNER // N_SSD_GROUPS


def _gnorm_fwd(y, proj, w_col):
    t = y.shape[1]
    tt = _tile(t, (512, 256, 128))
    z0 = OFF_Z // GN_ROWS

    def body(y_ref, z_ref, w_ref, o_ref):
        zv = z_ref[...]
        u = y_ref[...] * (zv * _sigmoid(zv))
        r = lax.rsqrt(jnp.mean(u * u, axis=0, keepdims=True) + EPS)
        o_ref[...] = (u * r * w_ref[...]).astype(BF16)

    blk = pl.BlockSpec((GN_ROWS, tt), lambda g, i: (g, i))
    return pl.pallas_call(
        body,
        name="gnorm_fwd",
        grid=(N_SSD_GROUPS, t // tt),
        in_specs=[blk, pl.BlockSpec((GN_ROWS, tt), lambda g, i: (z0 + g, i)), pl.BlockSpec((GN_ROWS, 1), lambda g, i: (g, 0))],
        out_specs=blk,
        out_shape=jax.ShapeDtypeStruct((D_INNER, t), BF16),
        compiler_params=_cparams("parallel", "parallel"),
    )(y, proj, w_col)


def _gnorm_bwd(dout, y, proj, w_col, send=None):
    t = y.shape[1]
    tt = _tile(t, (512, 256, 128))
    z0 = OFF_Z // GN_ROWS

    def body(do_ref, y_ref, z_ref, w_ref, dy_ref, dz_ref, dw_ref):
        @pl.when(pl.program_id(1) == 0)
        def _():
            dw_ref[...] = jnp.zeros_like(dw_ref)

        zv = z_ref[...]
        yv = y_ref[...]
        sg = _sigmoid(zv)
        sz = zv * sg
        u = yv * sz
        r = lax.rsqrt(jnp.mean(u * u, axis=0, keepdims=True) + EPS)
        xhat = u * r
        dov = do_ref[...]
        dw_ref[...] += _rowsum(dov * xhat)
        dxhat = dov * w_ref[...]
        du = r * (dxhat - xhat * jnp.mean(dxhat * xhat, axis=0, keepdims=True))
        dy_ref[...] = du * sz
        dz_ref[...] = (du * yv * (sg * (1.0 + zv * (1.0 - sg)))).astype(BF16)

    blk = pl.BlockSpec((GN_ROWS, tt), lambda g, i: (g, i))
    col = pl.BlockSpec((GN_ROWS, 1), lambda g, i: (g, 0))
    return _call(
        body,
        name="gnorm_bwd",
        grid=(N_SSD_GROUPS, t // tt),
        in_specs=[blk, blk, pl.BlockSpec((GN_ROWS, tt), lambda g, i: (z0 + g, i)), col],
        out_specs=[blk, blk, col],
        out_shape=[jax.ShapeDtypeStruct((D_INNER, t), F32), jax.ShapeDtypeStruct((D_INNER, t), BF16),
                   jax.ShapeDtypeStruct((D_INNER, 1), F32)],
        semantics=("parallel", "arbitrary"), args=(dout, y, proj, w_col), send=send)


GATE_ROWS = 128


def _gate_specs(t):
    nr = D_MODEL // GATE_ROWS
    blk = pl.BlockSpec((GATE_ROWS, t), lambda r: (r, 0))
    rows_from = lambda first: pl.BlockSpec(
        (pl.Element(GATE_ROWS), pl.Element(t)), lambda r: (pl.multiple_of(first + GATE_ROWS * r, N_SSD_HEADS), 0))
    return blk, [
        rows_from(OFF_GA),
        rows_from(OFF_GS),
        pl.BlockSpec((GATE_ROWS, 1), lambda r: (r, 0)),
        pl.BlockSpec((GATE_ROWS, 1), lambda r: (nr + r, 0)),
        blk, blk,
    ]


def _gate_fwd(proj, b_col, attn, ssd):
    t = proj.shape[1]
    blk, specs = _gate_specs(t)

    def body(ga_ref, gs_ref, ba_ref, bs_ref, a_ref, s_ref, o_ref):
        o_ref[...] = (_sigmoid(ga_ref[...] + ba_ref[...]) * a_ref[...]
                      + _sigmoid(gs_ref[...] + bs_ref[...]) * s_ref[...]).astype(BF16)

    return pl.pallas_call(
        body,
        name="gate_fwd",
        grid=(D_MODEL // GATE_ROWS,),
        in_specs=specs,
        out_specs=blk,
        out_shape=jax.ShapeDtypeStruct((D_MODEL, t), BF16),
        compiler_params=_cparams("parallel"),
    )(proj, proj, b_col, b_col, attn, ssd)


def _gate_bwd(proj, b_col, attn, ssd, dmix, send=None):
    t = proj.shape[1]
    blk, specs = _gate_specs(t)

    def body(ga_ref, gs_ref, ba_ref, bs_ref, a_ref, s_ref, dm_ref, da_ref, dso_ref, dga_ref, dgs_ref, dba_ref, dbs_ref):
        dm = dm_ref[...]
        sa = _sigmoid(ga_ref[...] + ba_ref[...])
        ss = _sigmoid(gs_ref[...] + bs_ref[...])
        da_ref[...] = (dm * sa).astype(BF16)
        dso_ref[...] = (dm * ss).astype(BF16)
        dga = dm * a_ref[...] * sa * (1.0 - sa)
        dgs = dm * s_ref[...] * ss * (1.0 - ss)
        dga_ref[...] = dga.astype(BF16)
        dgs_ref[...] = dgs.astype(BF16)
        dba_ref[...] = _rowsum(dga)
        dbs_ref[...] = _rowsum(dgs)

    col = pl.BlockSpec((GATE_ROWS, 1), lambda r: (r, 0))
    act = jax.ShapeDtypeStruct((D_MODEL, t), BF16)
    bias = jax.ShapeDtypeStruct((D_MODEL, 1), F32)
    return _call(
        body,
        name="gate_bwd",
        grid=(D_MODEL // GATE_ROWS,),
        in_specs=specs + [blk],
        out_specs=[blk, blk, blk, blk, col, col],
        out_shape=[act, act, act, act, bias, bias],
        semantics=("parallel",), args=(proj, proj, b_col, b_col, attn, ssd, dmix), send=send)


FFN_ROWS = 256


def _ffn_fwd(u0, w_col, b_col):
    t = u0.shape[2]

    def body(u_ref, w_ref, b_ref, o_ref):
        def strip(rows):
            val, _ = _causal_conv(u_ref[0, rows, :], w_ref[0, rows, :], b_ref[0, rows, :])
            gt, _ = _causal_conv(u_ref[1, rows, :], w_ref[1, rows, :], b_ref[1, rows, :])
            o_ref[rows, :] = (gt * _sigmoid(gt) * val).astype(BF16)

        strip(slice(None))

    return pl.pallas_call(
        body,
        name="ffn_fwd",
        grid=(D_FF // FFN_ROWS,),
        in_specs=[
            pl.BlockSpec((2, FFN_ROWS, t), lambda i: (0, i, 0)),
            pl.BlockSpec((2, FFN_ROWS, FFN_CONV), lambda i: (0, i, 0)),
            pl.BlockSpec((2, FFN_ROWS, 1), lambda i: (0, i, 0)),
        ],
        out_specs=pl.BlockSpec((FFN_ROWS, t), lambda i: (i, 0)),
        out_shape=jax.ShapeDtypeStruct((D_FF, t), BF16),
        compiler_params=_cparams("parallel"),
    )(u0, w_col, b_col)


def _ffn_bwd(u0, w_col, b_col, dg, send=None):
    t = u0.shape[2]

    def body(u_ref, w_ref, b_ref, dg_ref, du_ref, dwb_ref):
        def strip(rows):
            xval, wval = u_ref[0, rows, :], w_ref[0, rows, :]
            xgt, wgt = u_ref[1, rows, :], w_ref[1, rows, :]
            val, sh_val = _causal_conv(xval, wval, b_ref[0, rows, :])
            gt, sh_gt = _causal_conv(xgt, wgt, b_ref[1, rows, :])
            sg = _sigmoid(gt)
            dgv = dg_ref[rows, :]
            dval = dgv * (gt * sg)
            dgt = dgv * val * (sg * (1.0 + gt * (1.0 - sg)))
            dx, dwb_ref[0, rows, :] = _causal_conv_bwd(dval, xval, sh_val, wval)
            du_ref[0, rows, :] = dx.astype(BF16)
            dx, dwb_ref[1, rows, :] = _causal_conv_bwd(dgt, xgt, sh_gt, wgt)
            du_ref[1, rows, :] = dx.astype(BF16)

        strip(slice(None))

    return _call(
        body,
        name="ffn_bwd",
        grid=(D_FF // FFN_ROWS,),
        in_specs=[
            pl.BlockSpec((2, FFN_ROWS, t), lambda i: (0, i, 0)),
            pl.BlockSpec((2, FFN_ROWS, FFN_CONV), lambda i: (0, i, 0)),
            pl.BlockSpec((2, FFN_ROWS, 1), lambda i: (0, i, 0)),
            pl.BlockSpec((FFN_ROWS, t), lambda i: (i, 0)),
        ],
        out_specs=[pl.BlockSpec((2, FFN_ROWS, t), lambda i: (0, i, 0)), pl.BlockSpec((2, FFN_ROWS, 128), lambda i: (0, i, 0))],
        out_shape=[jax.ShapeDtypeStruct((2, D_FF, t), BF16), jax.ShapeDtypeStruct((2, D_FF, 128), F32)],
        semantics=("parallel",), args=(u0, w_col, b_col, dg), send=send)


def _adamw_math(w, g, m, v):
    m = ADAM_B1 * m + (1.0 - ADAM_B1) * g
    v = ADAM_B2 * v + (1.0 - ADAM_B2) * (g * g)
    m_hat = m / (1.0 - ADAM_B1 ** ADAM_STEP)
    v_hat = v / (1.0 - ADAM_B2 ** ADAM_STEP)
    delta = -ADAM_LR * (m_hat / (jnp.sqrt(v_hat) + ADAM_EPS) + ADAM_WD * w)
    return delta, m, v


def _adamw_sharded(parts, w, m, v, name):
    r, c = w.shape[0], w.shape[-1]
    tc = _tile(c, (256, 128))
    blk_shape = (r, tc) if w.ndim == 2 else (r, 1, tc)

    def body(p_ref, w_ref, m_ref, v_ref, g_ref, d_ref, nm_ref, nv_ref):
        g = p_ref[0].astype(F32)
        for s in range(1, N_DEV):
            g = g + p_ref[s].astype(F32)
        flat = lambda ref: ref[...].reshape(r, tc)
        d, nm, nv = _adamw_math(flat(w_ref), g, flat(m_ref), flat(v_ref))
        for ref, val in ((g_ref, g), (d_ref, d), (nm_ref, nm), (nv_ref, nv)):
            ref[...] = val.reshape(blk_shape)

    blk = pl.BlockSpec(blk_shape, (lambda i: (0, i)) if w.ndim == 2 else (lambda i: (0, 0, i)))
    out = jax.ShapeDtypeStruct(w.shape, F32)
    return pl.pallas_call(
        body,
        name=name,
        grid=(c // tc,),
        in_specs=[pl.BlockSpec((N_DEV, r, tc), lambda i: (0, 0, i)), blk, blk, blk],
        out_specs=[blk, blk, blk, blk],
        out_shape=[out, out, out, out],
        compiler_params=_cparams("parallel"),
    )(parts, w, m, v)


def _lane_offsets(sizes):
    offsets, pos = [], 0
    for n in sizes:
        offsets.append(pos)
        pos += -(-n // 128) * 128
    return offsets, pos


def _pack_row(parts):
    rows = [p.reshape(1, -1).astype(F32) for p in parts]
    return jnp.concatenate([jnp.pad(r, ((0, 0), (0, -r.shape[1] % 128))) for r in rows], axis=1)


def _small_update(parts, me, full_sizes, ws, ms, vs):
    n = len(ws)
    offsets, _ = _lane_offsets([1] + list(full_sizes))

    def body(me_ref, p_ref, *refs):
        w_refs, m_refs, v_refs = refs[:n], refs[n:2 * n], refs[2 * n:3 * n]
        scalar_ref, out_refs = refs[3 * n], refs[3 * n + 1:]
        tot = p_ref[0]
        for s in range(1, N_DEV):
            tot = tot + p_ref[s]
        scalar_ref[...] = tot[:, 0:1]
        for k in range(n):
            g_ref, d_ref, nm_ref, nv_ref = out_refs[4 * k:4 * k + 4]
            taps, cols = w_refs[k].shape
            if taps == 1:
                g_ref[...] = tot[:, offsets[k + 1]:offsets[k + 1] + cols]
            else:
                full = full_sizes[k] // taps
                for tap in range(taps):
                    mine = jnp.zeros((1, cols), F32)
                    for d in range(N_DEV):
                        lo = offsets[k + 1] + tap * full + d * cols
                        mine = jnp.where(me_ref[0] == d, tot[:, lo:lo + cols], mine)
                    g_ref[tap:tap + 1, :] = mine
            d_ref[...], nm_ref[...], nv_ref[...] = _adamw_math(w_refs[k][...], g_ref[...], m_refs[k][...], v_refs[k][...])

    vmem = pl.BlockSpec(memory_space=pltpu.VMEM)
    out_shape = [jax.ShapeDtypeStruct((1, 1), F32)]
    for wk in ws:
        out_shape += [jax.ShapeDtypeStruct(wk.shape, F32)] * 4
    res = pl.pallas_call(
        body,
        name="small_update",
        in_specs=[pl.BlockSpec(memory_space=pltpu.SMEM)] + [vmem] * (1 + 3 * n),
        out_specs=[vmem] * len(out_shape),
        out_shape=out_shape,
    )(me, parts, *ws, *ms, *vs)
    return res[0], [res[1 + 4 * k:5 + 4 * k] for k in range(n)]


ANY = pl.BlockSpec(memory_space=pl.ANY)
FLIPS = [(k >> 2 & 1, k >> 1 & 1, k & 1) for k in range(1, N_DEV)]


def _place():
    return lax.axis_index("x"), lax.axis_index("y"), lax.axis_index("c")


HBM = pl.BlockSpec(memory_space=pltpu.HBM)
SEM = pl.BlockSpec(memory_space=pltpu.SEMAPHORE)
EFFECT = pltpu.SideEffectType.DATAFLOW_SIDE_EFFECTING


def _peer_copy(gather, src_ref, land_ref, send_sems, recv_sems, k, sending):
    x, y, c = _place()
    fx, fy, fc = FLIPS[k]
    me = 4 * x + 2 * y + c
    peer = 4 * (x ^ fx) + 2 * (y ^ fy) + (c ^ fc)
    return pltpu.make_async_remote_copy(
        src_ref=src_ref if gather else src_ref.at[peer],
        dst_ref=land_ref.at[me if sending else peer],
        send_sem=send_sems.at[k], recv_sem=recv_sems.at[k],
        device_id=(x ^ fx, y ^ fy, c ^ fc), device_id_type=MESH)


SIBLING = 0
OTHER_CHIPS = (1, 3, 5)


def _gather_start(srcs, name, via_sibling):
    n = len(srcs)
    lands = [lax.empty((N_DEV,) + s.shape, s.dtype) for s in srcs]

    def body(*refs):
        src_refs, land_refs = refs[:n], refs[n:2 * n]
        send, recv = refs[2 * n:3 * n], refs[3 * n:4 * n]
        local_sems = refs[6 * n]
        x, y, c = _place()
        me = 4 * x + 2 * y + c
        local = [pltpu.make_async_copy(src_refs[i], land_refs[i].at[me], local_sems.at[i]) for i in range(n)]
        for cp in local:
            cp.start()
        for i in range(n):
            for k in (SIBLING,) + OTHER_CHIPS if via_sibling else range(N_DEV - 1):
                _peer_copy(True, src_refs[i], land_refs[i], send[i], recv[i], k, True).start()
        for cp in local:
            cp.wait()

    sem = pltpu.SemaphoreType.DMA((N_DEV - 1,))
    hbm = lambda a: pltpu.HBM(a.shape, a.dtype)
    res = pl.pallas_call(
        body,
        name=name,
        in_specs=[HBM] * (2 * n),
        out_specs=[SEM] * (2 * n) + [HBM] * (2 * n),
        out_shape=[sem] * (2 * n) + [hbm(s) for s in srcs] + [hbm(a) for a in lands],
        input_output_aliases={i: 2 * n + i for i in range(2 * n)},
        scratch_shapes=[pltpu.SemaphoreType.DMA((n,))],
        compiler_params=pltpu.CompilerParams(has_side_effects=EFFECT),
    )(*[pltpu.with_memory_space_constraint(a, pltpu.HBM) for a in list(srcs) + lands])
    return res[:n], res[n:2 * n], res[2 * n:3 * n], res[3 * n:4 * n]


def _exchange_wait(send_sems, recv_sems, src, land, after, gather, name):
    def body(src_ref, land_ref, send_ref, recv_ref, after_ref, src_out, land_out):
        for k in range(N_DEV - 1):
            cp = _peer_copy(gather, src_ref, land_ref, send_ref, recv_ref, k, False)
            cp.wait_send()
            cp.wait_recv()

    hbm = lambda a: pltpu.HBM(a.shape, a.dtype)
    return pl.pallas_call(
        body,
        name=name,
        in_specs=[HBM, HBM, SEM, SEM, ANY],
        out_specs=[HBM, HBM],
        out_shape=[hbm(src), hbm(land)],
        input_output_aliases={0: 0, 1: 1},
        compiler_params=pltpu.CompilerParams(has_side_effects=EFFECT),
    )(src, land, send_sems, recv_sems, after)[1]


def _forwarded_copy(land_ref, send_sems, recv_sems, j, sending):
    x, y, c = _place()
    fx, fy, _ = FLIPS[OTHER_CHIPS[j]]
    slot = 4 * (x ^ fx) + 2 * (y ^ fy) + (c if sending else 1 - c)
    return pltpu.make_async_remote_copy(
        src_ref=land_ref.at[slot], dst_ref=land_ref.at[slot], send_sem=send_sems.at[j], recv_sem=recv_sems.at[j],
        device_id=(x, y, 1 - c), device_id_type=MESH)


def _gather_forward(send_sems, recv_sems, srcs, lands, name):
    n = len(srcs)

    def body(*refs):
        src_refs, land_refs = refs[:n], refs[n:2 * n]
        send, recv = refs[2 * n:3 * n], refs[3 * n:4 * n]
        fwd_send, fwd_recv = refs[4 * n:5 * n], refs[5 * n:6 * n]
        for i in range(n):
            for j, k in enumerate(OTHER_CHIPS):
                _peer_copy(True, src_refs[i], land_refs[i], send[i], recv[i], k, False).wait_recv()
                _forwarded_copy(land_refs[i], fwd_send[i], fwd_recv[i], j, True).start()

    sem = pltpu.SemaphoreType.DMA((len(OTHER_CHIPS),))
    hbm = lambda a: pltpu.HBM(a.shape, a.dtype)
    res = pl.pallas_call(
        body,
        name=name,
        in_specs=[HBM] * (2 * n) + [SEM] * (2 * n),
        out_specs=[SEM] * (2 * n) + [HBM] * (2 * n),
        out_shape=[sem] * (2 * n) + [hbm(a) for a in srcs] + [hbm(a) for a in lands],
        input_output_aliases={i: 2 * n + i for i in range(2 * n)},
        compiler_params=pltpu.CompilerParams(has_side_effects=EFFECT),
    )(*srcs, *lands, *send_sems, *recv_sems)
    return res[:n], res[n:2 * n], res[2 * n:3 * n], res[3 * n:4 * n]


def _gather_wait_forwarded(send_sems, recv_sems, fwd_send, fwd_recv, src, land, after, name):
    def body(src_ref, land_ref, send_ref, recv_ref, fwd_send_ref, fwd_recv_ref, after_ref, src_out, land_out):
        for k in (SIBLING,) + OTHER_CHIPS:
            _peer_copy(True, src_ref, land_ref, send_ref, recv_ref, k, False).wait_send()
        _peer_copy(True, src_ref, land_ref, send_ref, recv_ref, SIBLING, False).wait_recv()
        for j in range(len(OTHER_CHIPS)):
            _forwarded_copy(land_ref, fwd_send_ref, fwd_recv_ref, j, True).wait_send()
            _forwarded_copy(land_ref, fwd_send_ref, fwd_recv_ref, j, False).wait_recv()

    hbm = lambda a: pltpu.HBM(a.shape, a.dtype)
    return pl.pallas_call(
        body,
        name=name,
        in_specs=[HBM, HBM, SEM, SEM, SEM, SEM, ANY],
        out_specs=[HBM, HBM],
        out_shape=[hbm(src), hbm(land)],
        input_output_aliases={0: 0, 1: 1},
        compiler_params=pltpu.CompilerParams(has_side_effects=EFFECT),
    )(src, land, send_sems, recv_sems, fwd_send, fwd_recv, after)[1]


def _col(v):
    return v.reshape(-1, 1).astype(F32)


def _local_step(xt, tgt, weight, small):
    t = xt.shape[1]
    n1 = _col(small["norm1_w"])
    n2 = _col(small["norm2_w"])
    nf = _col(small["final_norm_w"])
    bg = _col(small["b_gate"])
    sinks = small["attn_sinks"].reshape(-1).astype(F32)
    cbias = _col(small["ssd_conv_b"])
    dtb = _col(small["dt_bias"])
    alog = _col(small["a_log"])
    dsk = _col(small["d_skip"])
    gnw = _col(small["ssd_norm_w"])
    fb = small["ffn_conv_b"].reshape(2, D_FF, 1)

    xn = _norm_fwd(xt, n1, "norm1_fwd")
    cw = weight("ssd_conv_w", xn).T
    fw = weight("ffn_conv_w", xn).T.reshape(2, D_FF, FFN_CONV)
    w_in_t = weight("w_in", xn)
    proj = _matmul(w_in_t, xn, nt=False, out_dtype=F32, name="mm_in")
    ao, lse = _attn_fwd(proj, sinks)
    w_ao = weight("w_attn_o", ao)
    attn = _matmul(w_ao, ao, nt=False, out_dtype=F32, name="mm_attn_o", tn_a=True)
    xbc = _conv_silu_fwd(proj, cw, cbias)
    y, hst = _ssd_fwd(xbc, proj, dtb, alog, dsk)
    yn = _gnorm_fwd(y, proj, gnw)
    w_so = weight("w_ssd_o", yn)
    ssd = _matmul(w_so, yn, nt=False, out_dtype=F32, name="mm_ssd_o", tn_a=True)
    mix = _gate_fwd(proj, bg, attn, ssd)
    w_out = weight("w_out", mix)
    h1 = _matmul(w_out, mix, nt=False, out_dtype=F32, name="mm_out", add=xt, tn_a=True)
    hn = _norm_fwd(h1, n2, "norm2_fwd")
    w_up_t = weight("w_up", hn)
    u0 = _matmul(w_up_t, hn, nt=False, out_dtype=F32, name="mm_up").reshape(2, D_FF, t)
    gl = _ffn_fwd(u0, fw, fb)
    w_down = weight("w_down", gl)
    h2 = _matmul(w_down, gl, nt=False, out_dtype=F32, name="mm_down", add=h1, tn_a=True)
    dh2, loss, d_nf = _final_norm_loss(h2, tgt, nf)

    g = {}
    handles = {}

    def sending(weight_name, grad, fn, *args, **kwargs):
        out, handles[weight_name] = fn(*args, send=grad.reshape(N_DEV, -1, D_MODEL), **kwargs)
        return out

    g_down = _matmul(gl, dh2, nt=True, out_dtype=BF16, name="mm_d_w_down")
    dgl = _matmul(w_down, dh2, nt=False, out_dtype=F32, name="mm_d_glu")
    du0, d_fwb = sending("w_down", g_down, _ffn_bwd, u0, fw, fb, dgl)
    du0 = du0.reshape(2 * D_FF, t)
    g_up = _matmul(du0, hn, nt=True, out_dtype=BF16, name="mm_d_w_up")
    dhn = sending("w_up", g_up, _matmul, w_up_t, du0, nt=False, out_dtype=F32, name="mm_d_hn", tn_a=True)
    dh1, d_n2 = _norm_bwd(dhn, h1, n2, dh2, "norm2_bwd")
    g_out = _matmul(mix, dh1, nt=True, out_dtype=BF16, name="mm_d_w_out")
    dmix = _matmul(w_out, dh1, nt=False, out_dtype=F32, name="mm_d_mix")
    d_attn, d_ssd, d_ga, d_gs, d_ba, d_bs = sending("w_out", g_out, _gate_bwd, proj, bg, attn, ssd, dmix)
    g_ao = _matmul(ao, d_attn, nt=True, out_dtype=BF16, name="mm_d_w_attn_o")
    dao = _matmul(w_ao, d_attn, nt=False, out_dtype=F32, name="mm_d_ao")
    dq, dk, dv, d_sinks = sending("w_attn_o", g_ao, _attn_bwd, proj, sinks, ao, lse, dao)
    g_so = _matmul(yn, d_ssd, nt=True, out_dtype=BF16, name="mm_d_w_ssd_o")
    dyn = _matmul(w_so, d_ssd, nt=False, out_dtype=F32, name="mm_d_yn")
    dy, dz, d_gnw = sending("w_ssd_o", g_so, _gnorm_bwd, dyn, y, proj, gnw)
    dxs, dbm, dcm, ddt, d_alog, d_dsk, d_dtb = _ssd_bwd(xbc, proj, dtb, alog, dsk, hst, dy)
    dx_xs, dwb_xs = _conv_silu_bwd(proj, cw, cbias, dxs, 0, "ssd_conv_bwd_x")
    dx_b, dwb_b = _conv_silu_bwd(proj, cw, cbias, dbm, D_INNER, "ssd_conv_bwd_b")
    dx_c, dwb_c = _conv_silu_bwd(proj, cw, cbias, dcm, D_INNER + BC_DIM, "ssd_conv_bwd_c")
    dwb_conv = jnp.concatenate([dwb_xs, dwb_b, dwb_c], axis=0)
    dproj = jnp.concatenate([dq, dk, dv, dz, dx_xs, dx_b, dx_c, ddt.astype(BF16), d_ga, d_gs], axis=0)
    g_in = _matmul(dproj, xn, nt=True, out_dtype=BF16, name="mm_d_w_in")
    dxn = sending("w_in", g_in, _matmul, w_in_t, dproj, nt=False, out_dtype=F32, name="mm_d_xn", tn_a=True)
    dx, d_n1 = _norm_bwd(dxn, xt, n1, dh1, "norm1_bwd")

    g["norm1_w"] = d_n1
    g["b_gate"] = jnp.concatenate([d_ba, d_bs], axis=0)
    g["attn_sinks"] = d_sinks
    g["ssd_conv_w"] = dwb_conv[:, :SSD_CONV].T
    g["ssd_conv_b"] = dwb_conv[:, SSD_CONV]
    g["dt_bias"] = d_dtb
    g["a_log"] = d_alog
    g["d_skip"] = d_dsk
    g["ssd_norm_w"] = d_gnw
    g["norm2_w"] = d_n2
    d_fwb = d_fwb.reshape(2 * D_FF, 128)
    g["ffn_conv_w"] = d_fwb[:, :FFN_CONV].T
    g["ffn_conv_b"] = d_fwb[:, FFN_CONV]
    g["final_norm_w"] = d_nf
    return loss, dx, g, handles


SMALL = ("norm1_w", "b_gate", "attn_sinks", "ssd_conv_w", "ssd_conv_b", "dt_bias", "a_log", "d_skip", "ssd_norm_w",
         "norm2_w", "ffn_conv_w", "ffn_conv_b", "final_norm_w")
WEIGHT_ORDER = ("norm1_w", "w_in", "b_gate", "attn_sinks", "w_attn_o", "ssd_conv_w", "ssd_conv_b", "dt_bias", "a_log",
                "d_skip", "ssd_norm_w", "w_ssd_o", "w_out", "norm2_w", "w_up", "ffn_conv_w", "ffn_conv_b", "w_down",
                "final_norm_w")


def kernel(x, norm1_w, w_in, b_gate, attn_sinks, w_attn_o, ssd_conv_w, ssd_conv_b, dt_bias, a_log, d_skip, ssd_norm_w, w_ssd_o, w_out, norm2_w, w_up, ffn_conv_w, ffn_conv_b, w_down, final_norm_w, loss_target, m_norm1_w, m_w_in, m_b_gate, m_attn_sinks, m_w_attn_o, m_ssd_conv_w, m_ssd_conv_b, m_dt_bias, m_a_log, m_d_skip, m_ssd_norm_w, m_w_ssd_o, m_w_out, m_norm2_w, m_w_up, m_ffn_conv_w, m_ffn_conv_b, m_w_down, m_final_norm_w, v_norm1_w, v_w_in, v_b_gate, v_attn_sinks, v_w_attn_o, v_ssd_conv_w, v_ssd_conv_b, v_dt_bias, v_a_log, v_d_skip, v_ssd_norm_w, v_w_ssd_o, v_w_out, v_norm2_w, v_w_up, v_ffn_conv_w, v_ffn_conv_b, v_w_down, v_final_norm_w):
    w = dict(norm1_w=norm1_w, w_in=w_in, b_gate=b_gate, attn_sinks=attn_sinks, w_attn_o=w_attn_o, ssd_conv_w=ssd_conv_w, ssd_conv_b=ssd_conv_b, dt_bias=dt_bias, a_log=a_log, d_skip=d_skip, ssd_norm_w=ssd_norm_w, w_ssd_o=w_ssd_o, w_out=w_out, norm2_w=norm2_w, w_up=w_up, ffn_conv_w=ffn_conv_w, ffn_conv_b=ffn_conv_b, w_down=w_down, final_norm_w=final_norm_w)
    m = dict(norm1_w=m_norm1_w, w_in=m_w_in, b_gate=m_b_gate, attn_sinks=m_attn_sinks, w_attn_o=m_w_attn_o, ssd_conv_w=m_ssd_conv_w, ssd_conv_b=m_ssd_conv_b, dt_bias=m_dt_bias, a_log=m_a_log, d_skip=m_d_skip, ssd_norm_w=m_ssd_norm_w, w_ssd_o=m_w_ssd_o, w_out=m_w_out, norm2_w=m_norm2_w, w_up=m_w_up, ffn_conv_w=m_ffn_conv_w, ffn_conv_b=m_ffn_conv_b, w_down=m_w_down, final_norm_w=m_final_norm_w)
    v = dict(norm1_w=v_norm1_w, w_in=v_w_in, b_gate=v_b_gate, attn_sinks=v_attn_sinks, w_attn_o=v_w_attn_o, ssd_conv_w=v_ssd_conv_w, ssd_conv_b=v_ssd_conv_b, dt_bias=v_dt_bias, a_log=v_a_log, d_skip=v_d_skip, ssd_norm_w=v_ssd_norm_w, w_ssd_o=v_w_ssd_o, w_out=v_w_out, norm2_w=v_norm2_w, w_up=v_w_up, ffn_conv_w=v_ffn_conv_w, ffn_conv_b=v_ffn_conv_b, w_down=v_w_down, final_norm_w=v_final_norm_w)
    me = 4 * lax.axis_index("x") + 2 * lax.axis_index("y") + lax.axis_index("c")

    shards = {"ssd_conv_w": ssd_conv_w[0], "ffn_conv_w": ffn_conv_w[0], "w_in": w_in[0].T.astype(BF16),
              "w_attn_o": w_attn_o[0].astype(BF16), "w_ssd_o": w_ssd_o[0].astype(BF16), "w_out": w_out[0].astype(BF16),
              "w_up": w_up[0].T.astype(BF16), "w_down": w_down[0].astype(BF16)}
    order = list(shards)
    g_send, g_recv, g_src, g_land = _gather_start(list(shards.values()), "gather_start", True)
    f_send, f_recv, g_src, g_land = _gather_forward(g_send, g_recv, g_src, g_land, "gather_forward")

    def weight(name, after):
        i = order.index(name)
        land = _gather_wait_forwarded(g_send[i], g_recv[i], f_send[i], f_recv[i], g_src[i], g_land[i], after,
                                      "gather_wait_" + name)
        if name == "ssd_conv_w":
            return jnp.transpose(land, (1, 0, 2)).reshape(SSD_CONV, XBC_DIM)
        if name == "ffn_conv_w":
            return jnp.transpose(land, (1, 0, 2)).reshape(FFN_CONV, 2 * D_FF)
        return land.reshape(-1, D_MODEL)

    small = {k: w[k][0] if k != "final_norm_w" else w[k] for k in SMALL}
    loss, dx, g, pending = _local_step(x[0].T, loss_target[0].T, weight, small)

    packed = _pack_row([loss] + [g[k] for k in SMALL])
    s_send, s_recv, s_src, s_land = _gather_start([packed], "small_grads_start", False)

    res = {}
    after = s_src[0]
    for name in ("w_down", "w_up", "w_out", "w_attn_o", "w_ssd_o", "w_in"):
        parts = _exchange_wait(*pending[name], after, False, "grad_wait_" + name)
        view, back = {
            "w_in": (lambda a: jnp.transpose(a, (2, 0, 1)), lambda r: jnp.transpose(r, (1, 2, 0))),
            "w_up": (lambda a: a[0].T, lambda r: r.T[None]),
        }.get(name, (lambda a: a[0], lambda r: r[None]))
        res[name] = _adamw_sharded(parts, view(w[name]), view(m[name]), view(v[name]), "adamw_" + name)
        after = res[name][0]
        res[name] = [back(r) for r in res[name]]

    rows = _exchange_wait(s_send[0], s_recv[0], s_src[0], s_land[0], after, True, "small_grads_wait")
    flat = lambda a: a.reshape(-1, a.shape[-1])
    loss_sum, updates = _small_update(
        rows, me.reshape(1), [g[k].size for k in SMALL],
        [flat(w[k]) for k in SMALL], [flat(m[k]) for k in SMALL], [flat(v[k]) for k in SMALL])
    for k, upd in zip(SMALL, updates):
        res[k] = [u.reshape(w[k].shape) for u in upd]

    grad_x = dx.T[None]
    outs = [loss_sum.reshape(()), grad_x]
    for i in range(4):
        outs.extend(res[k][i] for k in WEIGHT_ORDER)
    return tuple(outs)
```

```python
import functools

import jax
import jax.numpy as jnp
from jax import lax
from jax.experimental import pallas as pl
from jax.experimental.pallas import tpu as pltpu

F32 = jnp.float32
BF16 = jnp.bfloat16
HIGHEST = lax.Precision.HIGHEST

D_MODEL = 1024
N_Q_HEADS = 16
N_KV_HEADS = 4
HEAD_DIM = 64
WINDOW = 128
Q_PER_KV = N_Q_HEADS // N_KV_HEADS
Q_DIM = N_Q_HEADS * HEAD_DIM
KV_DIM = N_KV_HEADS * HEAD_DIM
D_INNER = 2048
SSD_HEAD_DIM = 64
N_SSD_HEADS = 32
N_SSD_GROUPS = 4
HEADS_PER_GROUP = N_SSD_HEADS // N_SSD_GROUPS
D_STATE = 128
BC_DIM = N_SSD_GROUPS * D_STATE
XBC_DIM = D_INNER + 2 * BC_DIM
SSD_CONV = 4
CHUNK = 128
D_FF = 2816
FFN_CONV = 3
EPS = 1e-5
NEG = -1e30
IN_DIM = 8736
N_DEV = 8

OFF_Q = 0
OFF_K = OFF_Q + Q_DIM
OFF_V = OFF_K + KV_DIM
OFF_Z = OFF_V + KV_DIM
OFF_X = OFF_Z + D_INNER
OFF_DT = OFF_X + XBC_DIM
OFF_GA = OFF_DT + N_SSD_HEADS
OFF_GS = OFF_GA + D_MODEL

ADAM_LR = 0.001
ADAM_B1 = 0.9
ADAM_B2 = 0.999
ADAM_EPS = 1e-08
ADAM_WD = 0.01
ADAM_STEP = 10

VMEM_LIMIT = 48 * 1024 * 1024
MESH = pl.DeviceIdType.MESH


def _cparams(*sem):
    return pltpu.CompilerParams(dimension_semantics=sem, vmem_limit_bytes=VMEM_LIMIT)


def _tile(n, prefs):
    for p in prefs:
        if n % p == 0:
            return p
    return n


def _sigmoid(x):
    return 1.0 / (1.0 + jnp.exp(-x))


def _softplus(x):
    return jnp.maximum(x, 0.0) + jnp.log(1.0 + jnp.exp(-jnp.abs(x)))


def _rowsum(x):
    return jnp.sum(x, axis=1, keepdims=True)


def _colsum(x):
    return jnp.sum(x, axis=0, keepdims=True)


def _dot(a, b):
    return jnp.dot(a, b, preferred_element_type=F32)


def _dot_nt(a, b):
    return lax.dot_general(a, b, (((1,), (1,)), ((), ())), preferred_element_type=F32)


def _dot_tn(a, b):
    return lax.dot_general(a, b, (((0,), (0,)), ((), ())), preferred_element_type=F32)


def _shift_right(x, j):
    if j == 0:
        return x
    r = pltpu.roll(x, j, 1)
    lane = lax.broadcasted_iota(jnp.int32, (x.shape[0], 128), 1)
    return jnp.concatenate([jnp.where(lane >= j, r[:, :128], 0.0), r[:, 128:]], axis=1)


def _shift_left(x, j):
    if j == 0:
        return x
    n = x.shape[1]
    r = pltpu.roll(x, n - j, 1)
    lane = lax.broadcasted_iota(jnp.int32, (x.shape[0], 128), 1)
    return jnp.concatenate([r[:, :n - 128], jnp.where(lane < 128 - j, r[:, n - 128:], 0.0)], axis=1)


def _causal_conv(xv, wv, bv):
    taps = wv.shape[1]
    shifted = [_shift_right(xv, taps - 1 - k) for k in range(taps - 1)]
    y = bv + wv[:, taps - 1:taps] * xv
    for k in range(taps - 1):
        y = y + wv[:, k:k + 1] * shifted[k]
    return y, shifted


def _causal_conv_bwd(dy, xv, shifted, wv):
    taps = wv.shape[1]
    lane = lax.broadcasted_iota(jnp.int32, (dy.shape[0], 128), 1)
    dwb = jnp.where(lane == taps, _rowsum(dy), 0.0)
    dwb = jnp.where(lane == taps - 1, _rowsum(dy * xv), dwb)
    dx = wv[:, taps - 1:taps] * dy
    for k in range(taps - 1):
        dx = dx + wv[:, k:k + 1] * _shift_left(dy, taps - 1 - k)
        dwb = jnp.where(lane == k, _rowsum(dy * shifted[k]), dwb)
    return dx, dwb


def _call(body, *, name, grid, in_specs, out_specs, out_shape, args, semantics, scratch_shapes=(), send=None):
    if send is None:
        return pl.pallas_call(body, name=name, grid=grid, in_specs=in_specs, out_specs=out_specs, out_shape=out_shape,
                              scratch_shapes=list(scratch_shapes), compiler_params=_cparams(*semantics))(*args)
    single = not isinstance(out_specs, (list, tuple))
    out_specs, out_shape = ([out_specs], [out_shape]) if single else (list(out_specs), list(out_shape))
    n_in, n_out = len(in_specs), len(out_specs)

    def sending(*refs):
        ins, (src_ref, land_ref) = refs[:n_in], refs[n_in:n_in + 2]
        outs = refs[n_in + 2:n_in + 2 + n_out]
        send_sems, recv_sems = refs[n_in + 2 + n_out:n_in + 4 + n_out]
        scratch = refs[n_in + 6 + n_out:]
        step = 0
        for axis, size in enumerate(grid):
            step = step * size + pl.program_id(axis)

        @pl.when(step == 0)
        def _():
            for peer in range(N_DEV - 1):
                _peer_copy(False, src_ref, land_ref, send_sems, recv_sems, peer, True).start()

        body(*ins, *outs, *scratch)

    sem = pltpu.SemaphoreType.DMA((N_DEV - 1,))
    hbm = pltpu.HBM(send.shape, send.dtype)
    res = pl.pallas_call(
        sending, name=name, grid=grid,
        in_specs=list(in_specs) + [HBM, HBM],
        out_specs=out_specs + [SEM, SEM, HBM, HBM],
        out_shape=out_shape + [sem, sem, hbm, hbm],
        input_output_aliases={n_in: n_out + 2, n_in + 1: n_out + 3},
        scratch_shapes=list(scratch_shapes),
        compiler_params=pltpu.CompilerParams(dimension_semantics=("arbitrary",) * len(grid), vmem_limit_bytes=VMEM_LIMIT,
                                             has_side_effects=EFFECT),
    )(*args, pltpu.with_memory_space_constraint(send, pltpu.HBM),
      pltpu.with_memory_space_constraint(lax.empty(send.shape, send.dtype), pltpu.HBM))
    return (res[0] if single else list(res[:n_out])), tuple(res[n_out:])


MATMUL_VMEM_BUDGET = 36 * 1024 * 1024
MATMUL_MAX_TK = 3072


MATMUL_MAX_TM = 768


def _largest_tile(n, align, cap):
    return max(d for d in range(align, min(n, cap) + 1, align) if n % d == 0)


def _matmul_tiles(m, n, k, a_bytes, b_bytes, out_bytes, has_add, m_align, k_align):
    tm = _largest_tile(m, m_align, MATMUL_MAX_TM)
    tk = _largest_tile(k, k_align, MATMUL_MAX_TK)
    for tn in sorted({d for d in range(128, n + 1, 128) if n % d == 0}, reverse=True):
        need = 2 * (tm * tk * a_bytes + tk * tn * b_bytes) + tm * tn * (2 * out_bytes + (4 if k > tk else 0) + (8 if has_add else 0))
        if tn <= 3072 and need <= MATMUL_VMEM_BUDGET:
            return tm, tn, tk
    return tm, 128, tk


def _matmul(a, b, *, nt, out_dtype, name, add=None, tn_a=False, send=None):
    if tn_a:
        k, m = a.shape
    else:
        m, k = a.shape
    n = b.shape[0] if nt else b.shape[1]
    tm, tn, tk = _matmul_tiles(m, n, k, a.dtype.itemsize, b.dtype.itemsize, jnp.dtype(out_dtype).itemsize, add is not None,
                               128 if tn_a else 16, 16 if tn_a and not nt else 128)
    nk = k // tk
    grid = (m // tm, n // tn, nk)

    def body(a_ref, b_ref, *rest):
        r_ref = None
        if add is not None:
            r_ref, rest = rest[0], rest[1:]
        o_ref = rest[0]
        av = a_ref[...].astype(BF16)
        bv = b_ref[...].astype(BF16)
        part = _dot_tn(av, bv) if tn_a else _dot_nt(av, bv) if nt else _dot(av, bv)

        def finish(r):
            if add is not None:
                r = r + r_ref[...]
            o_ref[...] = r.astype(out_dtype)

        if nk == 1:
            finish(part)
            return
        acc = rest[1]
        kk = pl.program_id(2)

        @pl.when(kk == 0)
        def _():
            acc[...] = part

        @pl.when((kk > 0) & (kk < nk - 1))
        def _():
            acc[...] += part

        @pl.when(kk == nk - 1)
        def _():
            finish(acc[...] + part)

    in_specs = [
        pl.BlockSpec((tk, tm), lambda i, j, kk: (kk, i)) if tn_a else pl.BlockSpec((tm, tk), lambda i, j, kk: (i, kk)),
        pl.BlockSpec((tn, tk), lambda i, j, kk: (j, kk)) if nt else pl.BlockSpec((tk, tn), lambda i, j, kk: (kk, j)),
    ]
    args = [a, b]
    if add is not None:
        in_specs.append(pl.BlockSpec((tm, tn), lambda i, j, kk: (i, j)))
        args.append(add)
    return _call(
        body, name=name, grid=grid, in_specs=in_specs, args=args,
        out_specs=pl.BlockSpec((tm, tn), lambda i, j, kk: (i, j)),
        out_shape=jax.ShapeDtypeStruct((m, n), out_dtype),
        scratch_shapes=[pltpu.VMEM((tm, tn), F32)] if nk > 1 else [],
        semantics=("parallel", "parallel", "arbitrary"), send=send)


def _norm_fwd(x, w_col, name):
    f, t = x.shape
    tt = _tile(t, (512, 256, 128))

    def body(x_ref, w_ref, o_ref):
        xv = x_ref[...]
        r = lax.rsqrt(jnp.mean(xv * xv, axis=0, keepdims=True) + EPS)
        o_ref[...] = (xv * r * w_ref[...]).astype(BF16)

    return pl.pallas_call(
        body,
        name=name,
        grid=(t // tt,),
        in_specs=[pl.BlockSpec((f, tt), lambda i: (0, i)), pl.BlockSpec((f, 1), lambda i: (0, 0))],
        out_specs=pl.BlockSpec((f, tt), lambda i: (0, i)),
        out_shape=jax.ShapeDtypeStruct((f, t), BF16),
        compiler_params=_cparams("parallel"),
    )(x, w_col)


def _norm_bwd(dy, x, w_col, res, name):
    f, t = x.shape
    tt = _tile(t, (512, 256, 128))

    def body(dy_ref, x_ref, w_ref, res_ref, dx_ref, dw_ref):
        @pl.when(pl.program_id(0) == 0)
        def _():
            dw_ref[...] = jnp.zeros_like(dw_ref)

        xv = x_ref[...]
        r = lax.rsqrt(jnp.mean(xv * xv, axis=0, keepdims=True) + EPS)
        xhat = xv * r
        dyv = dy_ref[...]
        dw_ref[...] += _rowsum(dyv * xhat)
        dxhat = dyv * w_ref[...]
        dx_ref[...] = res_ref[...] + r * (dxhat - xhat * jnp.mean(dxhat * xhat, axis=0, keepdims=True))

    blk = pl.BlockSpec((f, tt), lambda i: (0, i))
    col = pl.BlockSpec((f, 1), lambda i: (0, 0))
    return pl.pallas_call(
        body,
        name=name,
        grid=(t // tt,),
        in_specs=[blk, blk, col, blk],
        out_specs=[blk, col],
        out_shape=[jax.ShapeDtypeStruct((f, t), F32), jax.ShapeDtypeStruct((f, 1), F32)],
        compiler_params=_cparams("arbitrary"),
    )(dy, x, w_col, res)


def _final_norm_loss(h, tgt, w_col):
    f, t = h.shape
    tt = _tile(t, (512, 256, 128))

    def body(h_ref, t_ref, w_ref, dh_ref, loss_ref, dw_ref):
        @pl.when(pl.program_id(0) == 0)
        def _():
            dw_ref[...] = jnp.zeros_like(dw_ref)
            loss_ref[...] = jnp.zeros_like(loss_ref)

        xv = h_ref[...]
        r = lax.rsqrt(jnp.mean(xv * xv, axis=0, keepdims=True) + EPS)
        xhat = xv * r
        wv = w_ref[...]
        err = xhat * wv - t_ref[...]
        loss_ref[...] += 0.5 * _rowsum(jnp.mean(err * err, axis=0, keepdims=True))
        dyv = err * (1.0 / f)
        dw_ref[...] += _rowsum(dyv * xhat)
        dxhat = dyv * wv
        dh_ref[...] = r * (dxhat - xhat * jnp.mean(dxhat * xhat, axis=0, keepdims=True))

    blk = pl.BlockSpec((f, tt), lambda i: (0, i))
    col = pl.BlockSpec((f, 1), lambda i: (0, 0))
    one = pl.BlockSpec((1, 1), lambda i: (0, 0))
    return pl.pallas_call(
        body,
        name="final_norm_loss",
        grid=(t // tt,),
        in_specs=[blk, blk, col],
        out_specs=[blk, one, col],
        out_shape=[jax.ShapeDtypeStruct((f, t), F32), jax.ShapeDtypeStruct((1, 1), F32), jax.ShapeDtypeStruct((f, 1), F32)],
        compiler_params=_cparams("arbitrary"),
    )(h, tgt, w_col)


def _attn_mask(n):
    shape = (2 * WINDOW, Q_PER_KV * WINDOW)
    si = lax.broadcasted_iota(jnp.int32, shape, 0)
    qi = lax.broadcasted_iota(jnp.int32, shape, 1) & (WINDOW - 1)
    dist = WINDOW + qi - si
    return (dist >= 0) & (dist < WINDOW) & ((si >= WINDOW) | (n > 0))


def _lane_cat(ref, row0, rows):
    return jnp.concatenate([ref[row0 + i * rows:row0 + (i + 1) * rows, :] for i in range(Q_PER_KV)], axis=1)


def _attn_fwd(proj, sinks):
    t = proj.shape[1]
    nb = t // WINDOW
    scale = HEAD_DIM ** -0.5

    def body(s_ref, q_ref, kc_ref, kp_ref, vc_ref, vp_ref, o_ref, lse_ref):
        n = pl.program_id(0)
        valid = _attn_mask(n)
        for g in range(N_KV_HEADS):
            rows = slice(g * HEAD_DIM, (g + 1) * HEAD_DIM)
            kt = jnp.concatenate([kp_ref[rows, :], kc_ref[rows, :]], axis=1).astype(BF16)
            vt = jnp.concatenate([vp_ref[rows, :], vc_ref[rows, :]], axis=1).astype(BF16)
            qcat = (_lane_cat(q_ref, g * Q_PER_KV * HEAD_DIM, HEAD_DIM) * scale).astype(BF16)
            s = jnp.where(valid, _dot_tn(kt, qcat), NEG)
            sink = jnp.concatenate(
                [jnp.full((1, WINDOW), s_ref[g * Q_PER_KV + i], F32) for i in range(Q_PER_KV)], axis=1)
            m = jnp.maximum(jnp.max(s, axis=0, keepdims=True), sink)
            p = jnp.exp(s - m)
            denom = _colsum(p) + jnp.exp(sink - m)
            probs = (p / denom).astype(BF16)
            out = _dot(vt, probs)
            lse = m + jnp.log(denom)
            for i in range(Q_PER_KV):
                h = g * Q_PER_KV + i
                o_ref[h * HEAD_DIM:(h + 1) * HEAD_DIM, :] = out[:, i * WINDOW:(i + 1) * WINDOW]
                lse_ref[h:h + 1, :] = lse[:, i * WINDOW:(i + 1) * WINDOW]

    kb = OFF_K // KV_DIM
    vb = OFF_V // KV_DIM
    prev = lambda n: jnp.maximum(n - 1, 0)
    return pl.pallas_call(
        body,
        name="attn_fwd",
        grid=(nb,),
        in_specs=[
            pl.BlockSpec(memory_space=pltpu.SMEM),
            pl.BlockSpec((Q_DIM, WINDOW), lambda n: (0, n)),
            pl.BlockSpec((KV_DIM, WINDOW), lambda n: (kb, n)),
            pl.BlockSpec((KV_DIM, WINDOW), lambda n: (kb, prev(n))),
            pl.BlockSpec((KV_DIM, WINDOW), lambda n: (vb, n)),
            pl.BlockSpec((KV_DIM, WINDOW), lambda n: (vb, prev(n))),
        ],
        out_specs=[pl.BlockSpec((Q_DIM, WINDOW), lambda n: (0, n)), pl.BlockSpec((N_Q_HEADS, WINDOW), lambda n: (0, n))],
        out_shape=[jax.ShapeDtypeStruct((Q_DIM, t), F32), jax.ShapeDtypeStruct((N_Q_HEADS, t), F32)],
        compiler_params=_cparams("parallel"),
    )(sinks, proj, proj, proj, proj, proj)


def _attn_bwd(proj, sinks, out, lse, dout, send=None):
    t = proj.shape[1]
    nb = t // WINDOW
    scale = HEAD_DIM ** -0.5

    def body(s_ref, q_ref, kc_ref, kp_ref, vc_ref, vp_ref, o_ref, lse_ref, do_ref,
             dq_ref, dk_ref, dv_ref, ds_ref, dk_carry, dv_carry):
        step = pl.program_id(0)
        n = nb - 1 - step

        @pl.when(step == 0)
        def _():
            dk_carry[...] = jnp.zeros_like(dk_carry)
            dv_carry[...] = jnp.zeros_like(dv_carry)
            ds_ref[...] = jnp.zeros_like(ds_ref)

        valid = _attn_mask(n)
        for g in range(N_KV_HEADS):
            rows = slice(g * HEAD_DIM, (g + 1) * HEAD_DIM)
            q0 = g * Q_PER_KV * HEAD_DIM
            kt = jnp.concatenate([kp_ref[rows, :], kc_ref[rows, :]], axis=1).astype(BF16)
            vt = jnp.concatenate([vp_ref[rows, :], vc_ref[rows, :]], axis=1).astype(BF16)
            qf = _lane_cat(q_ref, q0, HEAD_DIM)
            qcat = qf.astype(BF16)
            ocat = _lane_cat(o_ref, q0, HEAD_DIM)
            docat = _lane_cat(do_ref, q0, HEAD_DIM)
            dob = docat.astype(BF16)
            lse_cat = jnp.concatenate(
                [lse_ref[g * Q_PER_KV + i:g * Q_PER_KV + i + 1, :] for i in range(Q_PER_KV)], axis=1)
            sink = jnp.concatenate(
                [jnp.full((1, WINDOW), s_ref[g * Q_PER_KV + i], F32) for i in range(Q_PER_KV)], axis=1)
            s = jnp.where(valid, _dot_tn(kt, (qf * scale).astype(BF16)), NEG)
            p = jnp.exp(s - lse_cat)
            dp = _dot_tn(vt, dob)
            delta = _colsum(docat * ocat)
            dsc = (p * (dp - delta)).astype(BF16)
            dsink_row = -jnp.exp(sink - lse_cat) * delta
            dq = _dot(kt, dsc) * scale
            dk = _dot_nt(qcat, dsc) * scale
            dv = _dot_nt(dob, p.astype(BF16))
            for i in range(Q_PER_KV):
                h = g * Q_PER_KV + i
                dq_ref[h * HEAD_DIM:(h + 1) * HEAD_DIM, :] = dq[:, i * WINDOW:(i + 1) * WINDOW].astype(BF16)
                ds_ref[h:h + 1, :] += _rowsum(dsink_row[:, i * WINDOW:(i + 1) * WINDOW])
            dk_ref[rows, :] = (dk[:, WINDOW:] + dk_carry[rows, :]).astype(BF16)
            dv_ref[rows, :] = (dv[:, WINDOW:] + dv_carry[rows, :]).astype(BF16)
            dk_carry[rows, :] = dk[:, :WINDOW]
            dv_carry[rows, :] = dv[:, :WINDOW]

    kb = OFF_K // KV_DIM
    vb = OFF_V // KV_DIM
    cur = lambda i: nb - 1 - i
    prev = lambda i: jnp.maximum(nb - 2 - i, 0)
    qspec = pl.BlockSpec((Q_DIM, WINDOW), lambda i: (0, cur(i)))
    kvspec = pl.BlockSpec((KV_DIM, WINDOW), lambda i: (0, cur(i)))
    return _call(
        body,
        name="attn_bwd",
        grid=(nb,),
        in_specs=[
            pl.BlockSpec(memory_space=pltpu.SMEM),
            qspec,
            pl.BlockSpec((KV_DIM, WINDOW), lambda i: (kb, cur(i))),
            pl.BlockSpec((KV_DIM, WINDOW), lambda i: (kb, prev(i))),
            pl.BlockSpec((KV_DIM, WINDOW), lambda i: (vb, cur(i))),
            pl.BlockSpec((KV_DIM, WINDOW), lambda i: (vb, prev(i))),
            qspec,
            pl.BlockSpec((N_Q_HEADS, WINDOW), lambda i: (0, cur(i))),
            qspec,
        ],
        out_specs=[qspec, kvspec, kvspec, pl.BlockSpec((N_Q_HEADS, 1), lambda i: (0, 0))],
        out_shape=[
            jax.ShapeDtypeStruct((Q_DIM, t), BF16),
            jax.ShapeDtypeStruct((KV_DIM, t), BF16),
            jax.ShapeDtypeStruct((KV_DIM, t), BF16),
            jax.ShapeDtypeStruct((N_Q_HEADS, 1), F32),
        ],
        scratch_shapes=[pltpu.VMEM((KV_DIM, WINDOW), F32), pltpu.VMEM((KV_DIM, WINDOW), F32)],
        semantics=("arbitrary",), args=(sinks, proj, proj, proj, proj, proj, out, lse, dout), send=send)


CONV_ROWS = 256


def _conv_silu_fwd(proj, w_col, b_col):
    t = proj.shape[1]
    r0 = OFF_X // CONV_ROWS

    def body(x_ref, w_ref, b_ref, o_ref):
        def strip(rows):
            y, _ = _causal_conv(x_ref[rows, :], w_ref[rows, :], b_ref[rows, :])
            o_ref[rows, :] = y * _sigmoid(y)

        strip(slice(None))

    return pl.pallas_call(
        body,
        name="ssd_conv_fwd",
        grid=(XBC_DIM // CONV_ROWS,),
        in_specs=[
            pl.BlockSpec((CONV_ROWS, t), lambda i: (r0 + i, 0)),
            pl.BlockSpec((CONV_ROWS, SSD_CONV), lambda i: (i, 0)),
            pl.BlockSpec((CONV_ROWS, 1), lambda i: (i, 0)),
        ],
        out_specs=pl.BlockSpec((CONV_ROWS, t), lambda i: (i, 0)),
        out_shape=jax.ShapeDtypeStruct((XBC_DIM, t), F32),
        compiler_params=_cparams("parallel"),
    )(proj, w_col, b_col)


def _conv_silu_bwd(proj, w_col, b_col, dout, row0, name):
    t = proj.shape[1]
    nrows = dout.shape[0]
    p0 = (OFF_X + row0) // CONV_ROWS
    c0 = row0 // CONV_ROWS

    def body(x_ref, w_ref, b_ref, do_ref, dx_ref, dwb_ref):
        def strip(rows):
            xv = x_ref[rows, :]
            wv = w_ref[rows, :]
            y, shifted = _causal_conv(xv, wv, b_ref[rows, :])
            sg = _sigmoid(y)
            dy = do_ref[rows, :] * (sg * (1.0 + y * (1.0 - sg)))
            dx, dwb_ref[rows, :] = _causal_conv_bwd(dy, xv, shifted, wv)
            dx_ref[rows, :] = dx.astype(BF16)

        strip(slice(None))

    return pl.pallas_call(
        body,
        name=name,
        grid=(nrows // CONV_ROWS,),
        in_specs=[
            pl.BlockSpec((CONV_ROWS, t), lambda i: (p0 + i, 0)),
            pl.BlockSpec((CONV_ROWS, SSD_CONV), lambda i: (c0 + i, 0)),
            pl.BlockSpec((CONV_ROWS, 1), lambda i: (c0 + i, 0)),
            pl.BlockSpec((CONV_ROWS, t), lambda i: (i, 0)),
        ],
        out_specs=[pl.BlockSpec((CONV_ROWS, t), lambda i: (i, 0)), pl.BlockSpec((CONV_ROWS, 128), lambda i: (i, 0))],
        out_shape=[jax.ShapeDtypeStruct((nrows, t), BF16), jax.ShapeDtypeStruct((nrows, 128), F32)],
        compiler_params=_cparams("parallel"),
    )(proj, w_col, b_col, dout)


GROUP_ROWS = HEADS_PER_GROUP * SSD_HEAD_DIM


def _ssd_specs(order):
    xb = D_INNER // BC_DIM
    dtb = OFF_DT // N_SSD_HEADS
    col = pl.BlockSpec((N_SSD_HEADS, 1), lambda c: (0, 0))
    return [
        pl.BlockSpec((D_INNER, CHUNK), lambda c: (0, order(c))),
        pl.BlockSpec((BC_DIM, CHUNK), lambda c: (xb, order(c))),
        pl.BlockSpec((BC_DIM, CHUNK), lambda c: (xb + 1, order(c))),
        pl.BlockSpec((N_SSD_HEADS, CHUNK), lambda c: (dtb, order(c))),
        col, col, col,
    ]


def _ssd_common(dt_ref, dtb_ref, alog_ref):
    z = dt_ref[...] + dtb_ref[...]
    dt = _softplus(z)
    a_neg = -jnp.exp(alog_ref[...])
    d_a = dt * a_neg
    row = lax.broadcasted_iota(jnp.int32, (CHUNK, CHUNK), 0)
    colm = lax.broadcasted_iota(jnp.int32, (CHUNK, CHUNK), 1)
    upper = (row <= colm).astype(F32)
    a_cs = jnp.dot(d_a, upper, precision=HIGHEST, preferred_element_type=F32)
    a_last = _rowsum(d_a)
    return z, dt, a_neg, a_cs, a_last, row >= colm, row == colm


def _decay(a_row, causal):
    a_s = jnp.broadcast_to(a_row, (CHUNK, CHUNK))
    seg = a_s.T - a_s
    return jnp.where(causal, jnp.exp(jnp.where(causal, seg, 0.0)), 0.0)


def _ssd_fwd(xbc, proj, dtb_col, alog_col, dsk_col):
    t = xbc.shape[1]
    nc = t // CHUNK

    def body(xs_ref, b_ref, c_ref, dt_ref, dtb_ref, alog_ref, dsk_ref, y_ref, hst_ref, h_scr):
        @pl.when(pl.program_id(0) == 0)
        def _():
            h_scr[...] = jnp.zeros_like(h_scr)

        _, dt, _, a_cs, a_last, causal, _ = _ssd_common(dt_ref, dtb_ref, alog_ref)
        hst_ref[0] = h_scr[...]
        dsk = dsk_ref[...]
        for g in range(N_SSD_GROUPS):
            grows = slice(g * D_STATE, (g + 1) * D_STATE)
            bb = b_ref[grows, :].astype(BF16)
            cb_ = c_ref[grows, :].astype(BF16)
            cb = _dot_tn(cb_, bb)
            for j in range(g * HEADS_PER_GROUP, (g + 1) * HEADS_PER_GROUP):
                rows = slice(j * SSD_HEAD_DIM, (j + 1) * SSD_HEAD_DIM)
                a = a_cs[j:j + 1, :]
                m = (cb * _decay(a, causal)).astype(BF16)
                xs = xs_ref[rows, :]
                xc = xs * dt[j:j + 1, :]
                hj = h_scr[rows, :]
                y = _dot_nt(xc.astype(BF16), m) + _dot(hj.astype(BF16), cb_) * jnp.exp(a) + dsk[j:j + 1, :] * xs
                y_ref[rows, :] = y
                al = a_last[j:j + 1, :]
                w = jnp.exp(al - a)
                h_scr[rows, :] = jnp.exp(al) * hj + _dot_nt((xc * w).astype(BF16), bb)

    return pl.pallas_call(
        body,
        name="ssd_fwd",
        grid=(nc,),
        in_specs=_ssd_specs(lambda c: c),
        out_specs=[
            pl.BlockSpec((D_INNER, CHUNK), lambda c: (0, c)),
            pl.BlockSpec((1, D_INNER, D_STATE), lambda c: (c, 0, 0)),
        ],
        out_shape=[
            jax.ShapeDtypeStruct((D_INNER, t), F32),
            jax.ShapeDtypeStruct((nc, D_INNER, D_STATE), F32),
        ],
        scratch_shapes=[pltpu.VMEM((D_INNER, D_STATE), F32)],
        compiler_params=_cparams("arbitrary"),
    )(xbc, xbc, xbc, proj, dtb_col, alog_col, dsk_col)


def _ssd_bwd(xbc, proj, dtb_col, alog_col, dsk_col, hst, dy):
    t = xbc.shape[1]
    nc = t // CHUNK
    rev = lambda c: nc - 1 - c

    def body(xs_ref, b_ref, c_ref, dt_ref, dtb_ref, alog_ref, dsk_ref, hst_ref, dy_ref,
             dxs_ref, db_ref, dc_ref, ddt_ref, dalog_ref, ddsk_ref, ddtb_ref, dh_scr, da_scr, ddt_scr, dd_scr):
        @pl.when(pl.program_id(0) == 0)
        def _():
            dh_scr[...] = jnp.zeros_like(dh_scr)
            dalog_ref[...] = jnp.zeros_like(dalog_ref)
            ddsk_ref[...] = jnp.zeros_like(ddsk_ref)
            ddtb_ref[...] = jnp.zeros_like(ddtb_ref)

        z, dt, a_neg, a_cs, a_last, causal, eye = _ssd_common(dt_ref, dtb_ref, alog_ref)
        dsk = dsk_ref[...]
        last_lane = lax.broadcasted_iota(jnp.int32, (1, CHUNK), 1) == CHUNK - 1
        for g in range(N_SSD_GROUPS):
            grows = slice(g * D_STATE, (g + 1) * D_STATE)
            bb = b_ref[grows, :].astype(BF16)
            cb_ = c_ref[grows, :].astype(BF16)
            cb = _dot_tn(cb_, bb)
            dcb = jnp.zeros((CHUNK, CHUNK), F32)
            dc_acc = jnp.zeros((D_STATE, CHUNK), F32)
            db_acc = jnp.zeros((D_STATE, CHUNK), F32)
            for j in range(g * HEADS_PER_GROUP, (g + 1) * HEADS_PER_GROUP):
                rows = slice(j * SSD_HEAD_DIM, (j + 1) * SSD_HEAD_DIM)
                a = a_cs[j:j + 1, :]
                al = a_last[j:j + 1, :]
                lam = _decay(a, causal)
                mf = cb * lam
                xs = xs_ref[rows, :]
                dtj = dt[j:j + 1, :]
                xc = xs * dtj
                w = jnp.exp(al - a)
                e = jnp.exp(a)
                gam = jnp.exp(al)
                hj = hst_ref[0, rows, :]
                hjb = hj.astype(BF16)
                dyv = dy_ref[rows, :]
                dyb = dyv.astype(BF16)
                dd_scr[j:j + 1, :] = _colsum(dyv * xs)
                gb = (dyv * e).astype(BF16)
                dh_in = _dot_nt(gb, cb_)
                dc_acc = dc_acc + _dot_tn(hjb, gb)
                yoff = _dot(hjb, cb_) * e
                da = _colsum(dyv * yoff)
                dm = _dot_tn(dyb, xc.astype(BF16))
                dxc = _dot(dyb, mf.astype(BF16))
                dcb = dcb + dm * lam
                nmat = dm * mf
                rs = jnp.broadcast_to(_rowsum(nmat), (CHUNK, CHUNK))
                da = da + _colsum(jnp.where(eye, rs, 0.0)) - _colsum(nmat)
                ds = dh_scr[rows, :]
                dsb = ds.astype(BF16)
                t1 = _dot(dsb, bb)
                xcw = xc * w
                dxc = dxc + w * t1
                dww = _colsum(xcw * t1)
                da_l = _rowsum(dww) + _rowsum(_colsum(ds * hj)) * gam
                da = da - dww + jnp.where(last_lane, da_l, 0.0)
                db_acc = db_acc + _dot_tn(dsb, xcw.astype(BF16))
                dh_scr[rows, :] = gam * ds + dh_in
                dxs_ref[rows, :] = dsk[j:j + 1, :] * dyv + dxc * dtj
                da_scr[j:j + 1, :] = da
                ddt_scr[j:j + 1, :] = _colsum(dxc * xs)
            dcbb = dcb.astype(BF16)
            dc_ref[grows, :] = dc_acc + _dot_nt(bb, dcbb)
            db_ref[grows, :] = db_acc + _dot(cb_, dcbb)
        dda = jnp.dot(da_scr[...], causal.astype(F32), precision=HIGHEST, preferred_element_type=F32)
        ddt = ddt_scr[...] + dda * a_neg
        ddt_raw = ddt * _sigmoid(z)
        ddt_ref[...] = ddt_raw
        ddtb_ref[...] += _rowsum(ddt_raw)
        dalog_ref[...] += _rowsum(dda * dt) * a_neg
        ddsk_ref[...] += _rowsum(dd_scr[...])

    col = pl.BlockSpec((N_SSD_HEADS, 1), lambda c: (0, 0))
    bc = pl.BlockSpec((BC_DIM, CHUNK), lambda c: (0, rev(c)))
    xs_spec = pl.BlockSpec((D_INNER, CHUNK), lambda c: (0, rev(c)))
    small = pltpu.VMEM((N_SSD_HEADS, CHUNK), F32)
    return pl.pallas_call(
        body,
        name="ssd_bwd",
        grid=(nc,),
        in_specs=_ssd_specs(rev) + [pl.BlockSpec((1, D_INNER, D_STATE), lambda c: (rev(c), 0, 0)), xs_spec],
        out_specs=[xs_spec, bc, bc, pl.BlockSpec((N_SSD_HEADS, CHUNK), lambda c: (0, rev(c))), col, col, col],
        out_shape=[
            jax.ShapeDtypeStruct((D_INNER, t), F32),
            jax.ShapeDtypeStruct((BC_DIM, t), F32),
            jax.ShapeDtypeStruct((BC_DIM, t), F32),
            jax.ShapeDtypeStruct((N_SSD_HEADS, t), F32),
            jax.ShapeDtypeStruct((N_SSD_HEADS, 1), F32),
            jax.ShapeDtypeStruct((N_SSD_HEADS, 1), F32),
            jax.ShapeDtypeStruct((N_SSD_HEADS, 1), F32),
        ],
        scratch_shapes=[pltpu.VMEM((D_INNER, D_STATE), F32), small, small, small],
        compiler_params=_cparams("arbitrary"),
    )(xbc, xbc, xbc, proj, dtb_col, alog_col, dsk_col, hst, dy)


GN_ROWS = D_INNER // N_SSD_GROUPS


def _gnorm_fwd(y, proj, w_col):
    t = y.shape[1]
    tt = _tile(t, (512, 256, 128))
    z0 = OFF_Z // GN_ROWS

    def body(y_ref, z_ref, w_ref, o_ref):
        zv = z_ref[...]
        u = y_ref[...] * (zv * _sigmoid(zv))
        r = lax.rsqrt(jnp.mean(u * u, axis=0, keepdims=True) + EPS)
        o_ref[...] = (u * r * w_ref[...]).astype(BF16)

    blk = pl.BlockSpec((GN_ROWS, tt), lambda g, i: (g, i))
    return pl.pallas_call(
        body,
        name="gnorm_fwd",
        grid=(N_SSD_GROUPS, t // tt),
        in_specs=[blk, pl.BlockSpec((GN_ROWS, tt), lambda g, i: (z0 + g, i)), pl.BlockSpec((GN_ROWS, 1), lambda g, i: (g, 0))],
        out_specs=blk,
        out_shape=jax.ShapeDtypeStruct((D_INNER, t), BF16),
        compiler_params=_cparams("parallel", "parallel"),
    )(y, proj, w_col)


def _gnorm_bwd(dout, y, proj, w_col, send=None):
    t = y.shape[1]
    tt = _tile(t, (512, 256, 128))
    z0 = OFF_Z // GN_ROWS

    def body(do_ref, y_ref, z_ref, w_ref, dy_ref, dz_ref, dw_ref):
        @pl.when(pl.program_id(1) == 0)
        def _():
            dw_ref[...] = jnp.zeros_like(dw_ref)

        zv = z_ref[...]
        yv = y_ref[...]
        sg = _sigmoid(zv)
        sz = zv * sg
        u = yv * sz
        r = lax.rsqrt(jnp.mean(u * u, axis=0, keepdims=True) + EPS)
        xhat = u * r
        dov = do_ref[...]
        dw_ref[...] += _rowsum(dov * xhat)
        dxhat = dov * w_ref[...]
        du = r * (dxhat - xhat * jnp.mean(dxhat * xhat, axis=0, keepdims=True))
        dy_ref[...] = du * sz
        dz_ref[...] = (du * yv * (sg * (1.0 + zv * (1.0 - sg)))).astype(BF16)

    blk = pl.BlockSpec((GN_ROWS, tt), lambda g, i: (g, i))
    col = pl.BlockSpec((GN_ROWS, 1), lambda g, i: (g, 0))
    return _call(
        body,
        name="gnorm_bwd",
        grid=(N_SSD_GROUPS, t // tt),
        in_specs=[blk, blk, pl.BlockSpec((GN_ROWS, tt), lambda g, i: (z0 + g, i)), col],
        out_specs=[blk, blk, col],
        out_shape=[jax.ShapeDtypeStruct((D_INNER, t), F32), jax.ShapeDtypeStruct((D_INNER, t), BF16),
                   jax.ShapeDtypeStruct((D_INNER, 1), F32)],
        semantics=("parallel", "arbitrary"), args=(dout, y, proj, w_col), send=send)


GATE_ROWS = 128


def _gate_specs(t):
    nr = D_MODEL // GATE_ROWS
    blk = pl.BlockSpec((GATE_ROWS, t), lambda r: (r, 0))
    rows_from = lambda first: pl.BlockSpec(
        (pl.Element(GATE_ROWS), pl.Element(t)), lambda r: (pl.multiple_of(first + GATE_ROWS * r, N_SSD_HEADS), 0))
    return blk, [
        rows_from(OFF_GA),
        rows_from(OFF_GS),
        pl.BlockSpec((GATE_ROWS, 1), lambda r: (r, 0)),
        pl.BlockSpec((GATE_ROWS, 1), lambda r: (nr + r, 0)),
        blk, blk,
    ]


def _gate_fwd(proj, b_col, attn, ssd):
    t = proj.shape[1]
    blk, specs = _gate_specs(t)

    def body(ga_ref, gs_ref, ba_ref, bs_ref, a_ref, s_ref, o_ref):
        o_ref[...] = (_sigmoid(ga_ref[...] + ba_ref[...]) * a_ref[...]
                      + _sigmoid(gs_ref[...] + bs_ref[...]) * s_ref[...]).astype(BF16)

    return pl.pallas_call(
        body,
        name="gate_fwd",
        grid=(D_MODEL // GATE_ROWS,),
        in_specs=specs,
        out_specs=blk,
        out_shape=jax.ShapeDtypeStruct((D_MODEL, t), BF16),
        compiler_params=_cparams("parallel"),
    )(proj, proj, b_col, b_col, attn, ssd)


def _gate_bwd(proj, b_col, attn, ssd, dmix, send=None):
    t = proj.shape[1]
    blk, specs = _gate_specs(t)

    def body(ga_ref, gs_ref, ba_ref, bs_ref, a_ref, s_ref, dm_ref, da_ref, dso_ref, dga_ref, dgs_ref, dba_ref, dbs_ref):
        dm = dm_ref[...]
        sa = _sigmoid(ga_ref[...] + ba_ref[...])
        ss = _sigmoid(gs_ref[...] + bs_ref[...])
        da_ref[...] = (dm * sa).astype(BF16)
        dso_ref[...] = (dm * ss).astype(BF16)
        dga = dm * a_ref[...] * sa * (1.0 - sa)
        dgs = dm * s_ref[...] * ss * (1.0 - ss)
        dga_ref[...] = dga.astype(BF16)
        dgs_ref[...] = dgs.astype(BF16)
        dba_ref[...] = _rowsum(dga)
        dbs_ref[...] = _rowsum(dgs)

    col = pl.BlockSpec((GATE_ROWS, 1), lambda r: (r, 0))
    act = jax.ShapeDtypeStruct((D_MODEL, t), BF16)
    bias = jax.ShapeDtypeStruct((D_MODEL, 1), F32)
    return _call(
        body,
        name="gate_bwd",
        grid=(D_MODEL // GATE_ROWS,),
        in_specs=specs + [blk],
        out_specs=[blk, blk, blk, blk, col, col],
        out_shape=[act, act, act, act, bias, bias],
        semantics=("parallel",), args=(proj, proj, b_col, b_col, attn, ssd, dmix), send=send)


FFN_ROWS = 256


def _ffn_fwd(u0, w_col, b_col):
    t = u0.shape[2]

    def body(u_ref, w_ref, b_ref, o_ref):
        def strip(rows):
            val, _ = _causal_conv(u_ref[0, rows, :], w_ref[0, rows, :], b_ref[0, rows, :])
            gt, _ = _causal_conv(u_ref[1, rows, :], w_ref[1, rows, :], b_ref[1, rows, :])
            o_ref[rows, :] = (gt * _sigmoid(gt) * val).astype(BF16)

        strip(slice(None))

    return pl.pallas_call(
        body,
        name="ffn_fwd",
        grid=(D_FF // FFN_ROWS,),
        in_specs=[
            pl.BlockSpec((2, FFN_ROWS, t), lambda i: (0, i, 0)),
            pl.BlockSpec((2, FFN_ROWS, FFN_CONV), lambda i: (0, i, 0)),
            pl.BlockSpec((2, FFN_ROWS, 1), lambda i: (0, i, 0)),
        ],
        out_specs=pl.BlockSpec((FFN_ROWS, t), lambda i: (i, 0)),
        out_shape=jax.ShapeDtypeStruct((D_FF, t), BF16),
        compiler_params=_cparams("parallel"),
    )(u0, w_col, b_col)


def _ffn_bwd(u0, w_col, b_col, dg, send=None):
    t = u0.shape[2]

    def body(u_ref, w_ref, b_ref, dg_ref, du_ref, dwb_ref):
        def strip(rows):
            xval, wval = u_ref[0, rows, :], w_ref[0, rows, :]
            xgt, wgt = u_ref[1, rows, :], w_ref[1, rows, :]
            val, sh_val = _causal_conv(xval, wval, b_ref[0, rows, :])
            gt, sh_gt = _causal_conv(xgt, wgt, b_ref[1, rows, :])
            sg = _sigmoid(gt)
            dgv = dg_ref[rows, :]
            dval = dgv * (gt * sg)
            dgt = dgv * val * (sg * (1.0 + gt * (1.0 - sg)))
            dx, dwb_ref[0, rows, :] = _causal_conv_bwd(dval, xval, sh_val, wval)
            du_ref[0, rows, :] = dx.astype(BF16)
            dx, dwb_ref[1, rows, :] = _causal_conv_bwd(dgt, xgt, sh_gt, wgt)
            du_ref[1, rows, :] = dx.astype(BF16)

        strip(slice(None))

    return _call(
        body,
        name="ffn_bwd",
        grid=(D_FF // FFN_ROWS,),
        in_specs=[
            pl.BlockSpec((2, FFN_ROWS, t), lambda i: (0, i, 0)),
            pl.BlockSpec((2, FFN_ROWS, FFN_CONV), lambda i: (0, i, 0)),
            pl.BlockSpec((2, FFN_ROWS, 1), lambda i: (0, i, 0)),
            pl.BlockSpec((FFN_ROWS, t), lambda i: (i, 0)),
        ],
        out_specs=[pl.BlockSpec((2, FFN_ROWS, t), lambda i: (0, i, 0)), pl.BlockSpec((2, FFN_ROWS, 128), lambda i: (0, i, 0))],
        out_shape=[jax.ShapeDtypeStruct((2, D_FF, t), BF16), jax.ShapeDtypeStruct((2, D_FF, 128), F32)],
        semantics=("parallel",), args=(u0, w_col, b_col, dg), send=send)


def _adamw_math(w, g, m, v):
    m = ADAM_B1 * m + (1.0 - ADAM_B1) * g
    v = ADAM_B2 * v + (1.0 - ADAM_B2) * (g * g)
    m_hat = m / (1.0 - ADAM_B1 ** ADAM_STEP)
    v_hat = v / (1.0 - ADAM_B2 ** ADAM_STEP)
    delta = -ADAM_LR * (m_hat / (jnp.sqrt(v_hat) + ADAM_EPS) + ADAM_WD * w)
    return delta, m, v


def _adamw_sharded(parts, w, m, v, name):
    r, c = w.shape[0], w.shape[-1]
    tc = _tile(c, (256, 128))
    blk_shape = (r, tc) if w.ndim == 2 else (r, 1, tc)

    def body(p_ref, w_ref, m_ref, v_ref, g_ref, d_ref, nm_ref, nv_ref):
        g = p_ref[0].astype(F32)
        for s in range(1, N_DEV):
            g = g + p_ref[s].astype(F32)
        flat = lambda ref: ref[...].reshape(r, tc)
        d, nm, nv = _adamw_math(flat(w_ref), g, flat(m_ref), flat(v_ref))
        for ref, val in ((g_ref, g), (d_ref, d), (nm_ref, nm), (nv_ref, nv)):
            ref[...] = val.reshape(blk_shape)

    blk = pl.BlockSpec(blk_shape, (lambda i: (0, i)) if w.ndim == 2 else (lambda i: (0, 0, i)))
    out = jax.ShapeDtypeStruct(w.shape, F32)
    return pl.pallas_call(
        body,
        name=name,
        grid=(c // tc,),
        in_specs=[pl.BlockSpec((N_DEV, r, tc), lambda i: (0, 0, i)), blk, blk, blk],
        out_specs=[blk, blk, blk, blk],
        out_shape=[out, out, out, out],
        compiler_params=_cparams("parallel"),
    )(parts, w, m, v)


def _lane_offsets(sizes):
    offsets, pos = [], 0
    for n in sizes:
        offsets.append(pos)
        pos += -(-n // 128) * 128
    return offsets, pos


def _pack_row(parts):
    rows = [p.reshape(1, -1).astype(F32) for p in parts]
    return jnp.concatenate([jnp.pad(r, ((0, 0), (0, -r.shape[1] % 128))) for r in rows], axis=1)


def _small_update(parts, me, full_sizes, ws, ms, vs):
    n = len(ws)
    offsets, _ = _lane_offsets([1] + list(full_sizes))

    def body(me_ref, p_ref, *refs):
        w_refs, m_refs, v_refs = refs[:n], refs[n:2 * n], refs[2 * n:3 * n]
        scalar_ref, out_refs = refs[3 * n], refs[3 * n + 1:]
        tot = p_ref[0]
        for s in range(1, N_DEV):
            tot = tot + p_ref[s]
        scalar_ref[...] = tot[:, 0:1]
        for k in range(n):
            g_ref, d_ref, nm_ref, nv_ref = out_refs[4 * k:4 * k + 4]
            taps, cols = w_refs[k].shape
            if taps == 1:
                g_ref[...] = tot[:, offsets[k + 1]:offsets[k + 1] + cols]
            else:
                full = full_sizes[k] // taps
                for tap in range(taps):
                    mine = jnp.zeros((1, cols), F32)
                    for d in range(N_DEV):
                        lo = offsets[k + 1] + tap * full + d * cols
                        mine = jnp.where(me_ref[0] == d, tot[:, lo:lo + cols], mine)
                    g_ref[tap:tap + 1, :] = mine
            d_ref[...], nm_ref[...], nv_ref[...] = _adamw_math(w_refs[k][...], g_ref[...], m_refs[k][...], v_refs[k][...])

    vmem = pl.BlockSpec(memory_space=pltpu.VMEM)
    out_shape = [jax.ShapeDtypeStruct((1, 1), F32)]
    for wk in ws:
        out_shape += [jax.ShapeDtypeStruct(wk.shape, F32)] * 4
    res = pl.pallas_call(
        body,
        name="small_update",
        in_specs=[pl.BlockSpec(memory_space=pltpu.SMEM)] + [vmem] * (1 + 3 * n),
        out_specs=[vmem] * len(out_shape),
        out_shape=out_shape,
    )(me, parts, *ws, *ms, *vs)
    return res[0], [res[1 + 4 * k:5 + 4 * k] for k in range(n)]


ANY = pl.BlockSpec(memory_space=pl.ANY)
FLIPS = [(k >> 2 & 1, k >> 1 & 1, k & 1) for k in range(1, N_DEV)]


def _place():
    return lax.axis_index("x"), lax.axis_index("y"), lax.axis_index("c")


HBM = pl.BlockSpec(memory_space=pltpu.HBM)
SEM = pl.BlockSpec(memory_space=pltpu.SEMAPHORE)
EFFECT = pltpu.SideEffectType.DATAFLOW_SIDE_EFFECTING


def _peer_copy(gather, src_ref, land_ref, send_sems, recv_sems, k, sending):
    x, y, c = _place()
    fx, fy, fc = FLIPS[k]
    me = 4 * x + 2 * y + c
    peer = 4 * (x ^ fx) + 2 * (y ^ fy) + (c ^ fc)
    return pltpu.make_async_remote_copy(
        src_ref=src_ref if gather else src_ref.at[peer],
        dst_ref=land_ref.at[me if sending else peer],
        send_sem=send_sems.at[k], recv_sem=recv_sems.at[k],
        device_id=(x ^ fx, y ^ fy, c ^ fc), device_id_type=MESH)


SIBLING = 0
OTHER_CHIPS = (1, 3, 5)


def _gather_start(srcs, name, via_sibling):
    n = len(srcs)
    lands = [lax.empty((N_DEV,) + s.shape, s.dtype) for s in srcs]

    def body(*refs):
        src_refs, land_refs = refs[:n], refs[n:2 * n]
        send, recv = refs[2 * n:3 * n], refs[3 * n:4 * n]
        for i in range(n):
            for k in (SIBLING,) + OTHER_CHIPS if via_sibling else range(N_DEV - 1):
                _peer_copy(True, src_refs[i], land_refs[i], send[i], recv[i], k, True).start()

    sem = pltpu.SemaphoreType.DMA((N_DEV - 1,))
    hbm = lambda a: pltpu.HBM(a.shape, a.dtype)
    res = pl.pallas_call(
        body,
        name=name,
        in_specs=[HBM] * (2 * n),
        out_specs=[SEM] * (2 * n) + [HBM] * (2 * n),
        out_shape=[sem] * (2 * n) + [hbm(s) for s in srcs] + [hbm(a) for a in lands],
        input_output_aliases={i: 2 * n + i for i in range(2 * n)},
        compiler_params=pltpu.CompilerParams(has_side_effects=EFFECT),
    )(*[pltpu.with_memory_space_constraint(a, pltpu.HBM) for a in list(srcs) + lands])
    return res[:n], res[n:2 * n], res[2 * n:3 * n], res[3 * n:4 * n]


def _exchange_wait(send_sems, recv_sems, src, land, after, gather, name):
    def body(src_ref, land_ref, send_ref, recv_ref, after_ref, src_out, land_out):
        for k in range(N_DEV - 1):
            cp = _peer_copy(gather, src_ref, land_ref, send_ref, recv_ref, k, False)
            cp.wait_send()
            cp.wait_recv()

    hbm = lambda a: pltpu.HBM(a.shape, a.dtype)
    return pl.pallas_call(
        body,
        name=name,
        in_specs=[HBM, HBM, SEM, SEM, ANY],
        out_specs=[HBM, HBM],
        out_shape=[hbm(src), hbm(land)],
        input_output_aliases={0: 0, 1: 1},
        compiler_params=pltpu.CompilerParams(has_side_effects=EFFECT),
    )(src, land, send_sems, recv_sems, after)


def _own_slot(src, land, me, gather):
    own = src[None] if gather else lax.dynamic_slice_in_dim(src, me, 1, axis=0)
    return lax.dynamic_update_slice_in_dim(land, own, me, axis=0)


def _forwarded_copy(land_ref, send_sems, recv_sems, j, sending):
    x, y, c = _place()
    fx, fy, _ = FLIPS[OTHER_CHIPS[j]]
    slot = 4 * (x ^ fx) + 2 * (y ^ fy) + (c if sending else 1 - c)
    return pltpu.make_async_remote_copy(
        src_ref=land_ref.at[slot], dst_ref=land_ref.at[slot], send_sem=send_sems.at[j], recv_sem=recv_sems.at[j],
        device_id=(x, y, 1 - c), device_id_type=MESH)


def _gather_forward(send_sems, recv_sems, srcs, lands, after, name):
    n = len(srcs)

    def body(*refs):
        src_refs, land_refs = refs[:n], refs[n:2 * n]
        send, recv = refs[2 * n:3 * n], refs[3 * n:4 * n]
        fwd_send, fwd_recv = refs[4 * n + 1:5 * n + 1], refs[5 * n + 1:6 * n + 1]
        for i in range(n):
            for j, k in enumerate(OTHER_CHIPS):
                _peer_copy(True, src_refs[i], land_refs[i], send[i], recv[i], k, False).wait_recv()
                _forwarded_copy(land_refs[i], fwd_send[i], fwd_recv[i], j, True).start()

    sem = pltpu.SemaphoreType.DMA((len(OTHER_CHIPS),))
    hbm = lambda a: pltpu.HBM(a.shape, a.dtype)
    res = pl.pallas_call(
        body,
        name=name,
        in_specs=[HBM] * (2 * n) + [SEM] * (2 * n) + [ANY],
        out_specs=[SEM] * (2 * n) + [HBM] * (2 * n),
        out_shape=[sem] * (2 * n) + [hbm(a) for a in srcs] + [hbm(a) for a in lands],
        input_output_aliases={i: 2 * n + i for i in range(2 * n)},
        compiler_params=pltpu.CompilerParams(has_side_effects=EFFECT),
    )(*srcs, *lands, *send_sems, *recv_sems, after)
    return res[:n], res[n:2 * n], res[2 * n:3 * n], res[3 * n:4 * n]


def _gather_wait_forwarded(send_sems, recv_sems, fwd_send, fwd_recv, src, land, after, name):
    def body(src_ref, land_ref, send_ref, recv_ref, fwd_send_ref, fwd_recv_ref, after_ref, src_out, land_out):
        for k in (SIBLING,) + OTHER_CHIPS:
            _peer_copy(True, src_ref, land_ref, send_ref, recv_ref, k, False).wait_send()
        _peer_copy(True, src_ref, land_ref, send_ref, recv_ref, SIBLING, False).wait_recv()
        for j in range(len(OTHER_CHIPS)):
            _forwarded_copy(land_ref, fwd_send_ref, fwd_recv_ref, j, True).wait_send()
            _forwarded_copy(land_ref, fwd_send_ref, fwd_recv_ref, j, False).wait_recv()

    hbm = lambda a: pltpu.HBM(a.shape, a.dtype)
    return pl.pallas_call(
        body,
        name=name,
        in_specs=[HBM, HBM, SEM, SEM, SEM, SEM, ANY],
        out_specs=[HBM, HBM],
        out_shape=[hbm(src), hbm(land)],
        input_output_aliases={0: 0, 1: 1},
        compiler_params=pltpu.CompilerParams(has_side_effects=EFFECT),
    )(src, land, send_sems, recv_sems, fwd_send, fwd_recv, after)


def _col(v):
    return v.reshape(-1, 1).astype(F32)


def _local_step(xt, tgt, weight, small):
    t = xt.shape[1]
    n1 = _col(small["norm1_w"])
    n2 = _col(small["norm2_w"])
    nf = _col(small["final_norm_w"])
    bg = _col(small["b_gate"])
    sinks = small["attn_sinks"].reshape(-1).astype(F32)
    cbias = _col(small["ssd_conv_b"])
    dtb = _col(small["dt_bias"])
    alog = _col(small["a_log"])
    dsk = _col(small["d_skip"])
    gnw = _col(small["ssd_norm_w"])
    fb = small["ffn_conv_b"].reshape(2, D_FF, 1)

    xn = _norm_fwd(xt, n1, "norm1_fwd")
    cw = weight("ssd_conv_w", xn).T
    fw = weight("ffn_conv_w", xn).T.reshape(2, D_FF, FFN_CONV)
    w_in_t = weight("w_in", xn)
    proj = _matmul(w_in_t, xn, nt=False, out_dtype=F32, name="mm_in")
    ao, lse = _attn_fwd(proj, sinks)
    w_ao = weight("w_attn_o", ao)
    attn = _matmul(w_ao, ao, nt=False, out_dtype=F32, name="mm_attn_o", tn_a=True)
    xbc = _conv_silu_fwd(proj, cw, cbias)
    y, hst = _ssd_fwd(xbc, proj, dtb, alog, dsk)
    yn = _gnorm_fwd(y, proj, gnw)
    w_so = weight("w_ssd_o", yn)
    ssd = _matmul(w_so, yn, nt=False, out_dtype=F32, name="mm_ssd_o", tn_a=True)
    mix = _gate_fwd(proj, bg, attn, ssd)
    w_out = weight("w_out", mix)
    h1 = _matmul(w_out, mix, nt=False, out_dtype=F32, name="mm_out", add=xt, tn_a=True)
    hn = _norm_fwd(h1, n2, "norm2_fwd")
    w_up_t = weight("w_up", hn)
    u0 = _matmul(w_up_t, hn, nt=False, out_dtype=F32, name="mm_up").reshape(2, D_FF, t)
    gl = _ffn_fwd(u0, fw, fb)
    w_down = weight("w_down", gl)
    h2 = _matmul(w_down, gl, nt=False, out_dtype=F32, name="mm_down", add=h1, tn_a=True)
    dh2, loss, d_nf = _final_norm_loss(h2, tgt, nf)

    g = {}
    handles = {}

    def sending(weight_name, grad, fn, *args, **kwargs):
        out, handles[weight_name] = fn(*args, send=grad.reshape(N_DEV, -1, D_MODEL), **kwargs)
        return out

    g_down = _matmul(gl, dh2, nt=True, out_dtype=BF16, name="mm_d_w_down")
    dgl = _matmul(w_down, dh2, nt=False, out_dtype=F32, name="mm_d_glu")
    du0, d_fwb = sending("w_down", g_down, _ffn_bwd, u0, fw, fb, dgl)
    du0 = du0.reshape(2 * D_FF, t)
    g_up = _matmul(du0, hn, nt=True, out_dtype=BF16, name="mm_d_w_up")
    dhn = sending("w_up", g_up, _matmul, w_up_t, du0, nt=False, out_dtype=F32, name="mm_d_hn", tn_a=True)
    dh1, d_n2 = _norm_bwd(dhn, h1, n2, dh2, "norm2_bwd")
    g_out = _matmul(mix, dh1, nt=True, out_dtype=BF16, name="mm_d_w_out")
    dmix = _matmul(w_out, dh1, nt=False, out_dtype=F32, name="mm_d_mix")
    d_attn, d_ssd, d_ga, d_gs, d_ba, d_bs = sending("w_out", g_out, _gate_bwd, proj, bg, attn, ssd, dmix)
    g_ao = _matmul(ao, d_attn, nt=True, out_dtype=BF16, name="mm_d_w_attn_o")
    dao = _matmul(w_ao, d_attn, nt=False, out_dtype=F32, name="mm_d_ao")
    dq, dk, dv, d_sinks = sending("w_attn_o", g_ao, _attn_bwd, proj, sinks, ao, lse, dao)
    g_so = _matmul(yn, d_ssd, nt=True, out_dtype=BF16, name="mm_d_w_ssd_o")
    dyn = _matmul(w_so, d_ssd, nt=False, out_dtype=F32, name="mm_d_yn")
    dy, dz, d_gnw = sending("w_ssd_o", g_so, _gnorm_bwd, dyn, y, proj, gnw)
    dxs, dbm, dcm, ddt, d_alog, d_dsk, d_dtb = _ssd_bwd(xbc, proj, dtb, alog, dsk, hst, dy)
    dx_xs, dwb_xs = _conv_silu_bwd(proj, cw, cbias, dxs, 0, "ssd_conv_bwd_x")
    dx_b, dwb_b = _conv_silu_bwd(proj, cw, cbias, dbm, D_INNER, "ssd_conv_bwd_b")
    dx_c, dwb_c = _conv_silu_bwd(proj, cw, cbias, dcm, D_INNER + BC_DIM, "ssd_conv_bwd_c")
    dwb_conv = jnp.concatenate([dwb_xs, dwb_b, dwb_c], axis=0)
    dproj = jnp.concatenate([dq, dk, dv, dz, dx_xs, dx_b, dx_c, ddt.astype(BF16), d_ga, d_gs], axis=0)
    g_in = _matmul(dproj, xn, nt=True, out_dtype=BF16, name="mm_d_w_in")
    dxn = sending("w_in", g_in, _matmul, w_in_t, dproj, nt=False, out_dtype=F32, name="mm_d_xn", tn_a=True)
    dx, d_n1 = _norm_bwd(dxn, xt, n1, dh1, "norm1_bwd")

    g["norm1_w"] = d_n1
    g["b_gate"] = jnp.concatenate([d_ba, d_bs], axis=0)
    g["attn_sinks"] = d_sinks
    g["ssd_conv_w"] = dwb_conv[:, :SSD_CONV].T
    g["ssd_conv_b"] = dwb_conv[:, SSD_CONV]
    g["dt_bias"] = d_dtb
    g["a_log"] = d_alog
    g["d_skip"] = d_dsk
    g["ssd_norm_w"] = d_gnw
    g["norm2_w"] = d_n2
    d_fwb = d_fwb.reshape(2 * D_FF, 128)
    g["ffn_conv_w"] = d_fwb[:, :FFN_CONV].T
    g["ffn_conv_b"] = d_fwb[:, FFN_CONV]
    g["final_norm_w"] = d_nf
    return loss, dx, g, handles


SMALL = ("norm1_w", "b_gate", "attn_sinks", "ssd_conv_w", "ssd_conv_b", "dt_bias", "a_log", "d_skip", "ssd_norm_w",
         "norm2_w", "ffn_conv_w", "ffn_conv_b", "final_norm_w")
WEIGHT_ORDER = ("norm1_w", "w_in", "b_gate", "attn_sinks", "w_attn_o", "ssd_conv_w", "ssd_conv_b", "dt_bias", "a_log",
                "d_skip", "ssd_norm_w", "w_ssd_o", "w_out", "norm2_w", "w_up", "ffn_conv_w", "ffn_conv_b", "w_down",
                "final_norm_w")


def kernel(x, norm1_w, w_in, b_gate, attn_sinks, w_attn_o, ssd_conv_w, ssd_conv_b, dt_bias, a_log, d_skip, ssd_norm_w, w_ssd_o, w_out, norm2_w, w_up, ffn_conv_w, ffn_conv_b, w_down, final_norm_w, loss_target, m_norm1_w, m_w_in, m_b_gate, m_attn_sinks, m_w_attn_o, m_ssd_conv_w, m_ssd_conv_b, m_dt_bias, m_a_log, m_d_skip, m_ssd_norm_w, m_w_ssd_o, m_w_out, m_norm2_w, m_w_up, m_ffn_conv_w, m_ffn_conv_b, m_w_down, m_final_norm_w, v_norm1_w, v_w_in, v_b_gate, v_attn_sinks, v_w_attn_o, v_ssd_conv_w, v_ssd_conv_b, v_dt_bias, v_a_log, v_d_skip, v_ssd_norm_w, v_w_ssd_o, v_w_out, v_norm2_w, v_w_up, v_ffn_conv_w, v_ffn_conv_b, v_w_down, v_final_norm_w):
    w = dict(norm1_w=norm1_w, w_in=w_in, b_gate=b_gate, attn_sinks=attn_sinks, w_attn_o=w_attn_o, ssd_conv_w=ssd_conv_w, ssd_conv_b=ssd_conv_b, dt_bias=dt_bias, a_log=a_log, d_skip=d_skip, ssd_norm_w=ssd_norm_w, w_ssd_o=w_ssd_o, w_out=w_out, norm2_w=norm2_w, w_up=w_up, ffn_conv_w=ffn_conv_w, ffn_conv_b=ffn_conv_b, w_down=w_down, final_norm_w=final_norm_w)
    m = dict(norm1_w=m_norm1_w, w_in=m_w_in, b_gate=m_b_gate, attn_sinks=m_attn_sinks, w_attn_o=m_w_attn_o, ssd_conv_w=m_ssd_conv_w, ssd_conv_b=m_ssd_conv_b, dt_bias=m_dt_bias, a_log=m_a_log, d_skip=m_d_skip, ssd_norm_w=m_ssd_norm_w, w_ssd_o=m_w_ssd_o, w_out=m_w_out, norm2_w=m_norm2_w, w_up=m_w_up, ffn_conv_w=m_ffn_conv_w, ffn_conv_b=m_ffn_conv_b, w_down=m_w_down, final_norm_w=m_final_norm_w)
    v = dict(norm1_w=v_norm1_w, w_in=v_w_in, b_gate=v_b_gate, attn_sinks=v_attn_sinks, w_attn_o=v_w_attn_o, ssd_conv_w=v_ssd_conv_w, ssd_conv_b=v_ssd_conv_b, dt_bias=v_dt_bias, a_log=v_a_log, d_skip=v_d_skip, ssd_norm_w=v_ssd_norm_w, w_ssd_o=v_w_ssd_o, w_out=v_w_out, norm2_w=v_norm2_w, w_up=v_w_up, ffn_conv_w=v_ffn_conv_w, ffn_conv_b=v_ffn_conv_b, w_down=v_w_down, final_norm_w=v_final_norm_w)
    me = 4 * lax.axis_index("x") + 2 * lax.axis_index("y") + lax.axis_index("c")

    shards = {"ssd_conv_w": ssd_conv_w[0], "ffn_conv_w": ffn_conv_w[0], "w_in": w_in[0].T.astype(BF16),
              "w_attn_o": w_attn_o[0].astype(BF16), "w_ssd_o": w_ssd_o[0].astype(BF16), "w_out": w_out[0].astype(BF16),
              "w_up": w_up[0].T.astype(BF16), "w_down": w_down[0].astype(BF16)}
    order = list(shards)
    g_send, g_recv, g_src, g_land = _gather_start(list(shards.values()), "gather_start", True)
    first = ("ssd_conv_w", "ffn_conv_w", "w_in")
    forwarded = {}

    def weight(name, after):
        if name not in forwarded:
            group = [k for k in order if (k in first) == (name in first)]
            idx = [order.index(k) for k in group]
            handles = _gather_forward([g_send[i] for i in idx], [g_recv[i] for i in idx], [g_src[i] for i in idx],
                                      [g_land[i] for i in idx], after, "gather_forward_for_" + name)
            forwarded.update(zip(group, zip(*handles)))
        i = order.index(name)
        src, land = _gather_wait_forwarded(g_send[i], g_recv[i], *forwarded[name], after, "gather_wait_" + name)
        land = _own_slot(src, land, me, True)
        if name == "ssd_conv_w":
            return jnp.transpose(land, (1, 0, 2)).reshape(SSD_CONV, XBC_DIM)
        if name == "ffn_conv_w":
            return jnp.transpose(land, (1, 0, 2)).reshape(FFN_CONV, 2 * D_FF)
        return land.reshape(-1, D_MODEL)

    small = {k: w[k][0] if k != "final_norm_w" else w[k] for k in SMALL}
    loss, dx, g, pending = _local_step(x[0].T, loss_target[0].T, weight, small)

    packed = _pack_row([loss] + [g[k] for k in SMALL])
    s_send, s_recv, s_src, s_land = _gather_start([packed], "small_grads_start", False)

    res = {}
    after = s_src[0]
    for name in ("w_down", "w_up", "w_out", "w_attn_o", "w_ssd_o", "w_in"):
        parts = _own_slot(*_exchange_wait(*pending[name], after, False, "grad_wait_" + name), me, False)
        view, back = {
            "w_in": (lambda a: jnp.transpose(a, (2, 0, 1)), lambda r: jnp.transpose(r, (1, 2, 0))),
            "w_up": (lambda a: a[0].T, lambda r: r.T[None]),
        }.get(name, (lambda a: a[0], lambda r: r[None]))
        res[name] = _adamw_sharded(parts, view(w[name]), view(m[name]), view(v[name]), "adamw_" + name)
        after = res[name][0]
        res[name] = [back(r) for r in res[name]]

    rows = _own_slot(*_exchange_wait(s_send[0], s_recv[0], s_src[0], s_land[0], after, True, "small_grads_wait"),
                     me, True)
    flat = lambda a: a.reshape(-1, a.shape[-1])
    loss_sum, updates = _small_update(
        rows, me.reshape(1), [g[k].size for k in SMALL],
        [flat(w[k]) for k in SMALL], [flat(m[k]) for k in SMALL], [flat(v[k]) for k in SMALL])
    for k, upd in zip(SMALL, updates):
        res[k] = [u.reshape(w[k].shape) for u in upd]

    grad_x = dx.T[None]
    outs = [loss_sum.reshape(()), grad_x]
    for i in range(4):
        outs.extend(res[k][i] for k in WEIGHT_ORDER)
    return tuple(outs)
```

```python
import functools

import jax
import jax.numpy as jnp
from jax import lax
from jax.experimental import pallas as pl
from jax.experimental.pallas import tpu as pltpu

F32 = jnp.float32
BF16 = jnp.bfloat16
HIGHEST = lax.Precision.HIGHEST

D_MODEL = 1024
N_Q_HEADS = 16
N_KV_HEADS = 4
HEAD_DIM = 64
WINDOW = 128
Q_PER_KV = N_Q_HEADS // N_KV_HEADS
Q_DIM = N_Q_HEADS * HEAD_DIM
KV_DIM = N_KV_HEADS * HEAD_DIM
D_INNER = 2048
SSD_HEAD_DIM = 64
N_SSD_HEADS = 32
N_SSD_GROUPS = 4
HEADS_PER_GROUP = N_SSD_HEADS // N_SSD_GROUPS
D_STATE = 128
BC_DIM = N_SSD_GROUPS * D_STATE
XBC_DIM = D_INNER + 2 * BC_DIM
SSD_CONV = 4
CHUNK = 128
D_FF = 2816
FFN_CONV = 3
EPS = 1e-5
NEG = -1e30
IN_DIM = 8736
N_DEV = 8

OFF_Q = 0
OFF_K = OFF_Q + Q_DIM
OFF_V = OFF_K + KV_DIM
OFF_Z = OFF_V + KV_DIM
OFF_X = OFF_Z + D_INNER
OFF_DT = OFF_X + XBC_DIM
OFF_GA = OFF_DT + N_SSD_HEADS
OFF_GS = OFF_GA + D_MODEL

ADAM_LR = 0.001
ADAM_B1 = 0.9
ADAM_B2 = 0.999
ADAM_EPS = 1e-08
ADAM_WD = 0.01
ADAM_STEP = 10

VMEM_LIMIT = 48 * 1024 * 1024
MESH = pl.DeviceIdType.MESH


def _cparams(*sem):
    return pltpu.CompilerParams(dimension_semantics=sem, vmem_limit_bytes=VMEM_LIMIT)


def _tile(n, prefs):
    for p in prefs:
        if n % p == 0:
            return p
    return n


def _sigmoid(x):
    return 1.0 / (1.0 + jnp.exp(-x))


def _softplus(x):
    return jnp.maximum(x, 0.0) + jnp.log(1.0 + jnp.exp(-jnp.abs(x)))


def _rowsum(x):
    return jnp.sum(x, axis=1, keepdims=True)


def _colsum(x):
    return jnp.sum(x, axis=0, keepdims=True)


def _dot(a, b):
    return jnp.dot(a, b, preferred_element_type=F32)


def _dot_nt(a, b):
    return lax.dot_general(a, b, (((1,), (1,)), ((), ())), preferred_element_type=F32)


def _dot_tn(a, b):
    return lax.dot_general(a, b, (((0,), (0,)), ((), ())), preferred_element_type=F32)


def _shift_right(x, j):
    if j == 0:
        return x
    r = pltpu.roll(x, j, 1)
    lane = lax.broadcasted_iota(jnp.int32, (x.shape[0], 128), 1)
    return jnp.concatenate([jnp.where(lane >= j, r[:, :128], 0.0), r[:, 128:]], axis=1)


def _shift_left(x, j):
    if j == 0:
        return x
    n = x.shape[1]
    r = pltpu.roll(x, n - j, 1)
    lane = lax.broadcasted_iota(jnp.int32, (x.shape[0], 128), 1)
    return jnp.concatenate([r[:, :n - 128], jnp.where(lane < 128 - j, r[:, n - 128:], 0.0)], axis=1)


def _causal_conv(xv, wv, bv):
    taps = wv.shape[1]
    shifted = [_shift_right(xv, taps - 1 - k) for k in range(taps - 1)]
    y = bv + wv[:, taps - 1:taps] * xv
    for k in range(taps - 1):
        y = y + wv[:, k:k + 1] * shifted[k]
    return y, shifted


def _causal_conv_bwd(dy, xv, shifted, wv):
    taps = wv.shape[1]
    lane = lax.broadcasted_iota(jnp.int32, (dy.shape[0], 128), 1)
    dwb = jnp.where(lane == taps, _rowsum(dy), 0.0)
    dwb = jnp.where(lane == taps - 1, _rowsum(dy * xv), dwb)
    dx = wv[:, taps - 1:taps] * dy
    for k in range(taps - 1):
        dx = dx + wv[:, k:k + 1] * _shift_left(dy, taps - 1 - k)
        dwb = jnp.where(lane == k, _rowsum(dy * shifted[k]), dwb)
    return dx, dwb


def _call(body, *, name, grid, in_specs, out_specs, out_shape, args, semantics, scratch_shapes=(), send=None):
    if send is None:
        return pl.pallas_call(body, name=name, grid=grid, in_specs=in_specs, out_specs=out_specs, out_shape=out_shape,
                              scratch_shapes=list(scratch_shapes), compiler_params=_cparams(*semantics))(*args)
    single = not isinstance(out_specs, (list, tuple))
    out_specs, out_shape = ([out_specs], [out_shape]) if single else (list(out_specs), list(out_shape))
    n_in, n_out = len(in_specs), len(out_specs)
    chips = send.shape[0] == N_DEV // 2
    n_copies = len(OTHER_CHIPS) if chips else N_DEV - 1

    def sending(*refs):
        ins, (src_ref, land_ref) = refs[:n_in], refs[n_in:n_in + 2]
        outs = refs[n_in + 2:n_in + 2 + n_out]
        send_sems, recv_sems = refs[n_in + 2 + n_out:n_in + 4 + n_out]
        scratch = refs[n_in + 6 + n_out:]
        step = 0
        for axis, size in enumerate(grid):
            step = step * size + pl.program_id(axis)

        @pl.when(step == 0)
        def _():
            for k in range(n_copies):
                if chips:
                    _chip_copy(src_ref, land_ref, send_sems, recv_sems, k, True).start()
                else:
                    _peer_copy(False, src_ref, land_ref, send_sems, recv_sems, k, True).start()

        body(*ins, *outs, *scratch)

    sem = pltpu.SemaphoreType.DMA((n_copies,))
    hbm = pltpu.HBM(send.shape, send.dtype)
    res = pl.pallas_call(
        sending, name=name, grid=grid,
        in_specs=list(in_specs) + [HBM, HBM],
        out_specs=out_specs + [SEM, SEM, HBM, HBM],
        out_shape=out_shape + [sem, sem, hbm, hbm],
        input_output_aliases={n_in: n_out + 2, n_in + 1: n_out + 3},
        scratch_shapes=list(scratch_shapes),
        compiler_params=pltpu.CompilerParams(dimension_semantics=("arbitrary",) * len(grid), vmem_limit_bytes=VMEM_LIMIT,
                                             has_side_effects=EFFECT),
    )(*args, pltpu.with_memory_space_constraint(send, pltpu.HBM),
      pltpu.with_memory_space_constraint(lax.empty(send.shape, send.dtype), pltpu.HBM))
    return (res[0] if single else list(res[:n_out])), tuple(res[n_out:])


MATMUL_VMEM_BUDGET = 36 * 1024 * 1024
MATMUL_MAX_TK = 3072


MATMUL_MAX_TM = 768


def _largest_tile(n, align, cap):
    return max(d for d in range(align, min(n, cap) + 1, align) if n % d == 0)


def _matmul_tiles(m, n, k, a_bytes, b_bytes, out_bytes, has_add, m_align, k_align):
    tm = _largest_tile(m, m_align, MATMUL_MAX_TM)
    tk = _largest_tile(k, k_align, MATMUL_MAX_TK)
    for tn in sorted({d for d in range(128, n + 1, 128) if n % d == 0}, reverse=True):
        need = 2 * (tm * tk * a_bytes + tk * tn * b_bytes) + tm * tn * (2 * out_bytes + (4 if k > tk else 0) + (8 if has_add else 0))
        if tn <= 3072 and need <= MATMUL_VMEM_BUDGET:
            return tm, tn, tk
    return tm, 128, tk


def _matmul(a, b, *, nt, out_dtype, name, add=None, tn_a=False, send=None):
    if tn_a:
        k, m = a.shape
    else:
        m, k = a.shape
    n = b.shape[0] if nt else b.shape[1]
    tm, tn, tk = _matmul_tiles(m, n, k, a.dtype.itemsize, b.dtype.itemsize, jnp.dtype(out_dtype).itemsize, add is not None,
                               128 if tn_a else 16, 16 if tn_a and not nt else 128)
    nk = k // tk
    grid = (m // tm, n // tn, nk)

    def body(a_ref, b_ref, *rest):
        r_ref = None
        if add is not None:
            r_ref, rest = rest[0], rest[1:]
        o_ref = rest[0]
        av = a_ref[...].astype(BF16)
        bv = b_ref[...].astype(BF16)
        part = _dot_tn(av, bv) if tn_a else _dot_nt(av, bv) if nt else _dot(av, bv)

        def finish(r):
            if add is not None:
                r = r + r_ref[...]
            o_ref[...] = r.astype(out_dtype)

        if nk == 1:
            finish(part)
            return
        acc = rest[1]
        kk = pl.program_id(2)

        @pl.when(kk == 0)
        def _():
            acc[...] = part

        @pl.when((kk > 0) & (kk < nk - 1))
        def _():
            acc[...] += part

        @pl.when(kk == nk - 1)
        def _():
            finish(acc[...] + part)

    in_specs = [
        pl.BlockSpec((tk, tm), lambda i, j, kk: (kk, i)) if tn_a else pl.BlockSpec((tm, tk), lambda i, j, kk: (i, kk)),
        pl.BlockSpec((tn, tk), lambda i, j, kk: (j, kk)) if nt else pl.BlockSpec((tk, tn), lambda i, j, kk: (kk, j)),
    ]
    args = [a, b]
    if add is not None:
        in_specs.append(pl.BlockSpec((tm, tn), lambda i, j, kk: (i, j)))
        args.append(add)
    return _call(
        body, name=name, grid=grid, in_specs=in_specs, args=args,
        out_specs=pl.BlockSpec((tm, tn), lambda i, j, kk: (i, j)),
        out_shape=jax.ShapeDtypeStruct((m, n), out_dtype),
        scratch_shapes=[pltpu.VMEM((tm, tn), F32)] if nk > 1 else [],
        semantics=("parallel", "parallel", "arbitrary"), send=send)


def _norm_fwd(x, w_col, name):
    f, t = x.shape
    tt = _tile(t, (512, 256, 128))

    def body(x_ref, w_ref, o_ref):
        xv = x_ref[...]
        r = lax.rsqrt(jnp.mean(xv * xv, axis=0, keepdims=True) + EPS)
        o_ref[...] = (xv * r * w_ref[...]).astype(BF16)

    return pl.pallas_call(
        body,
        name=name,
        grid=(t // tt,),
        in_specs=[pl.BlockSpec((f, tt), lambda i: (0, i)), pl.BlockSpec((f, 1), lambda i: (0, 0))],
        out_specs=pl.BlockSpec((f, tt), lambda i: (0, i)),
        out_shape=jax.ShapeDtypeStruct((f, t), BF16),
        compiler_params=_cparams("parallel"),
    )(x, w_col)


def _norm_bwd(dy, x, w_col, res, name):
    f, t = x.shape
    tt = _tile(t, (512, 256, 128))

    def body(dy_ref, x_ref, w_ref, res_ref, dx_ref, dw_ref):
        @pl.when(pl.program_id(0) == 0)
        def _():
            dw_ref[...] = jnp.zeros_like(dw_ref)

        xv = x_ref[...]
        r = lax.rsqrt(jnp.mean(xv * xv, axis=0, keepdims=True) + EPS)
        xhat = xv * r
        dyv = dy_ref[...]
        dw_ref[...] += _rowsum(dyv * xhat)
        dxhat = dyv * w_ref[...]
        dx_ref[...] = res_ref[...] + r * (dxhat - xhat * jnp.mean(dxhat * xhat, axis=0, keepdims=True))

    blk = pl.BlockSpec((f, tt), lambda i: (0, i))
    col = pl.BlockSpec((f, 1), lambda i: (0, 0))
    return pl.pallas_call(
        body,
        name=name,
        grid=(t // tt,),
        in_specs=[blk, blk, col, blk],
        out_specs=[blk, col],
        out_shape=[jax.ShapeDtypeStruct((f, t), F32), jax.ShapeDtypeStruct((f, 1), F32)],
        compiler_params=_cparams("arbitrary"),
    )(dy, x, w_col, res)


def _final_norm_loss(h, tgt, w_col):
    f, t = h.shape
    tt = _tile(t, (512, 256, 128))

    def body(h_ref, t_ref, w_ref, dh_ref, loss_ref, dw_ref):
        @pl.when(pl.program_id(0) == 0)
        def _():
            dw_ref[...] = jnp.zeros_like(dw_ref)
            loss_ref[...] = jnp.zeros_like(loss_ref)

        xv = h_ref[...]
        r = lax.rsqrt(jnp.mean(xv * xv, axis=0, keepdims=True) + EPS)
        xhat = xv * r
        wv = w_ref[...]
        err = xhat * wv - t_ref[...]
        loss_ref[...] += 0.5 * _rowsum(jnp.mean(err * err, axis=0, keepdims=True))
        dyv = err * (1.0 / f)
        dw_ref[...] += _rowsum(dyv * xhat)
        dxhat = dyv * wv
        dh_ref[...] = r * (dxhat - xhat * jnp.mean(dxhat * xhat, axis=0, keepdims=True))

    blk = pl.BlockSpec((f, tt), lambda i: (0, i))
    col = pl.BlockSpec((f, 1), lambda i: (0, 0))
    one = pl.BlockSpec((1, 1), lambda i: (0, 0))
    return pl.pallas_call(
        body,
        name="final_norm_loss",
        grid=(t // tt,),
        in_specs=[blk, blk, col],
        out_specs=[blk, one, col],
        out_shape=[jax.ShapeDtypeStruct((f, t), F32), jax.ShapeDtypeStruct((1, 1), F32), jax.ShapeDtypeStruct((f, 1), F32)],
        compiler_params=_cparams("arbitrary"),
    )(h, tgt, w_col)


def _attn_mask(n):
    shape = (2 * WINDOW, Q_PER_KV * WINDOW)
    si = lax.broadcasted_iota(jnp.int32, shape, 0)
    qi = lax.broadcasted_iota(jnp.int32, shape, 1) & (WINDOW - 1)
    dist = WINDOW + qi - si
    return (dist >= 0) & (dist < WINDOW) & ((si >= WINDOW) | (n > 0))


def _lane_cat(ref, row0, rows):
    return jnp.concatenate([ref[row0 + i * rows:row0 + (i + 1) * rows, :] for i in range(Q_PER_KV)], axis=1)


def _attn_fwd(proj, sinks):
    t = proj.shape[1]
    nb = t // WINDOW
    scale = HEAD_DIM ** -0.5

    def body(s_ref, q_ref, kc_ref, kp_ref, vc_ref, vp_ref, o_ref, lse_ref):
        n = pl.program_id(0)
        valid = _attn_mask(n)
        for g in range(N_KV_HEADS):
            rows = slice(g * HEAD_DIM, (g + 1) * HEAD_DIM)
            kt = jnp.concatenate([kp_ref[rows, :], kc_ref[rows, :]], axis=1).astype(BF16)
            vt = jnp.concatenate([vp_ref[rows, :], vc_ref[rows, :]], axis=1).astype(BF16)
            qcat = (_lane_cat(q_ref, g * Q_PER_KV * HEAD_DIM, HEAD_DIM) * scale).astype(BF16)
            s = jnp.where(valid, _dot_tn(kt, qcat), NEG)
            sink = jnp.concatenate(
                [jnp.full((1, WINDOW), s_ref[g * Q_PER_KV + i], F32) for i in range(Q_PER_KV)], axis=1)
            m = jnp.maximum(jnp.max(s, axis=0, keepdims=True), sink)
            p = jnp.exp(s - m)
            denom = _colsum(p) + jnp.exp(sink - m)
            probs = (p / denom).astype(BF16)
            out = _dot(vt, probs)
            lse = m + jnp.log(denom)
            for i in range(Q_PER_KV):
                h = g * Q_PER_KV + i
                o_ref[h * HEAD_DIM:(h + 1) * HEAD_DIM, :] = out[:, i * WINDOW:(i + 1) * WINDOW]
                lse_ref[h:h + 1, :] = lse[:, i * WINDOW:(i + 1) * WINDOW]

    kb = OFF_K // KV_DIM
    vb = OFF_V // KV_DIM
    prev = lambda n: jnp.maximum(n - 1, 0)
    return pl.pallas_call(
        body,
        name="attn_fwd",
        grid=(nb,),
        in_specs=[
            pl.BlockSpec(memory_space=pltpu.SMEM),
            pl.BlockSpec((Q_DIM, WINDOW), lambda n: (0, n)),
            pl.BlockSpec((KV_DIM, WINDOW), lambda n: (kb, n)),
            pl.BlockSpec((KV_DIM, WINDOW), lambda n: (kb, prev(n))),
            pl.BlockSpec((KV_DIM, WINDOW), lambda n: (vb, n)),
            pl.BlockSpec((KV_DIM, WINDOW), lambda n: (vb, prev(n))),
        ],
        out_specs=[pl.BlockSpec((Q_DIM, WINDOW), lambda n: (0, n)), pl.BlockSpec((N_Q_HEADS, WINDOW), lambda n: (0, n))],
        out_shape=[jax.ShapeDtypeStruct((Q_DIM, t), F32), jax.ShapeDtypeStruct((N_Q_HEADS, t), F32)],
        compiler_params=_cparams("parallel"),
    )(sinks, proj, proj, proj, proj, proj)


def _attn_bwd(proj, sinks, out, lse, dout, send=None):
    t = proj.shape[1]
    nb = t // WINDOW
    scale = HEAD_DIM ** -0.5

    def body(s_ref, q_ref, kc_ref, kp_ref, vc_ref, vp_ref, o_ref, lse_ref, do_ref,
             dq_ref, dk_ref, dv_ref, ds_ref, dk_carry, dv_carry):
        step = pl.program_id(0)
        n = nb - 1 - step

        @pl.when(step == 0)
        def _():
            dk_carry[...] = jnp.zeros_like(dk_carry)
            dv_carry[...] = jnp.zeros_like(dv_carry)
            ds_ref[...] = jnp.zeros_like(ds_ref)

        valid = _attn_mask(n)
        for g in range(N_KV_HEADS):
            rows = slice(g * HEAD_DIM, (g + 1) * HEAD_DIM)
            q0 = g * Q_PER_KV * HEAD_DIM
            kt = jnp.concatenate([kp_ref[rows, :], kc_ref[rows, :]], axis=1).astype(BF16)
            vt = jnp.concatenate([vp_ref[rows, :], vc_ref[rows, :]], axis=1).astype(BF16)
            qf = _lane_cat(q_ref, q0, HEAD_DIM)
            qcat = qf.astype(BF16)
            ocat = _lane_cat(o_ref, q0, HEAD_DIM)
            docat = _lane_cat(do_ref, q0, HEAD_DIM)
            dob = docat.astype(BF16)
            lse_cat = jnp.concatenate(
                [lse_ref[g * Q_PER_KV + i:g * Q_PER_KV + i + 1, :] for i in range(Q_PER_KV)], axis=1)
            sink = jnp.concatenate(
                [jnp.full((1, WINDOW), s_ref[g * Q_PER_KV + i], F32) for i in range(Q_PER_KV)], axis=1)
            s = jnp.where(valid, _dot_tn(kt, (qf * scale).astype(BF16)), NEG)
            p = jnp.exp(s - lse_cat)
            dp = _dot_tn(vt, dob)
            delta = _colsum(docat * ocat)
            dsc = (p * (dp - delta)).astype(BF16)
            dsink_row = -jnp.exp(sink - lse_cat) * delta
            dq = _dot(kt, dsc) * scale
            dk = _dot_nt(qcat, dsc) * scale
            dv = _dot_nt(dob, p.astype(BF16))
            for i in range(Q_PER_KV):
                h = g * Q_PER_KV + i
                dq_ref[h * HEAD_DIM:(h + 1) * HEAD_DIM, :] = dq[:, i * WINDOW:(i + 1) * WINDOW].astype(BF16)
                ds_ref[h:h + 1, :] += _rowsum(dsink_row[:, i * WINDOW:(i + 1) * WINDOW])
            dk_ref[rows, :] = (dk[:, WINDOW:] + dk_carry[rows, :]).astype(BF16)
            dv_ref[rows, :] = (dv[:, WINDOW:] + dv_carry[rows, :]).astype(BF16)
            dk_carry[rows, :] = dk[:, :WINDOW]
            dv_carry[rows, :] = dv[:, :WINDOW]

    kb = OFF_K // KV_DIM
    vb = OFF_V // KV_DIM
    cur = lambda i: nb - 1 - i
    prev = lambda i: jnp.maximum(nb - 2 - i, 0)
    qspec = pl.BlockSpec((Q_DIM, WINDOW), lambda i: (0, cur(i)))
    kvspec = pl.BlockSpec((KV_DIM, WINDOW), lambda i: (0, cur(i)))
    return _call(
        body,
        name="attn_bwd",
        grid=(nb,),
        in_specs=[
            pl.BlockSpec(memory_space=pltpu.SMEM),
            qspec,
            pl.BlockSpec((KV_DIM, WINDOW), lambda i: (kb, cur(i))),
            pl.BlockSpec((KV_DIM, WINDOW), lambda i: (kb, prev(i))),
            pl.BlockSpec((KV_DIM, WINDOW), lambda i: (vb, cur(i))),
            pl.BlockSpec((KV_DIM, WINDOW), lambda i: (vb, prev(i))),
            qspec,
            pl.BlockSpec((N_Q_HEADS, WINDOW), lambda i: (0, cur(i))),
            qspec,
        ],
        out_specs=[qspec, kvspec, kvspec, pl.BlockSpec((N_Q_HEADS, 1), lambda i: (0, 0))],
        out_shape=[
            jax.ShapeDtypeStruct((Q_DIM, t), BF16),
            jax.ShapeDtypeStruct((KV_DIM, t), BF16),
            jax.ShapeDtypeStruct((KV_DIM, t), BF16),
            jax.ShapeDtypeStruct((N_Q_HEADS, 1), F32),
        ],
        scratch_shapes=[pltpu.VMEM((KV_DIM, WINDOW), F32), pltpu.VMEM((KV_DIM, WINDOW), F32)],
        semantics=("arbitrary",), args=(sinks, proj, proj, proj, proj, proj, out, lse, dout), send=send)


CONV_ROWS = 256


def _conv_silu_fwd(proj, w_col, b_col):
    t = proj.shape[1]
    r0 = OFF_X // CONV_ROWS

    def body(x_ref, w_ref, b_ref, o_ref):
        def strip(rows):
            y, _ = _causal_conv(x_ref[rows, :], w_ref[rows, :], b_ref[rows, :])
            o_ref[rows, :] = y * _sigmoid(y)

        strip(slice(None))

    return pl.pallas_call(
        body,
        name="ssd_conv_fwd",
        grid=(XBC_DIM // CONV_ROWS,),
        in_specs=[
            pl.BlockSpec((CONV_ROWS, t), lambda i: (r0 + i, 0)),
            pl.BlockSpec((CONV_ROWS, SSD_CONV), lambda i: (i, 0)),
            pl.BlockSpec((CONV_ROWS, 1), lambda i: (i, 0)),
        ],
        out_specs=pl.BlockSpec((CONV_ROWS, t), lambda i: (i, 0)),
        out_shape=jax.ShapeDtypeStruct((XBC_DIM, t), F32),
        compiler_params=_cparams("parallel"),
    )(proj, w_col, b_col)


def _conv_silu_bwd(proj, w_col, b_col, dout, row0, name):
    t = proj.shape[1]
    nrows = dout.shape[0]
    p0 = (OFF_X + row0) // CONV_ROWS
    c0 = row0 // CONV_ROWS

    def body(x_ref, w_ref, b_ref, do_ref, dx_ref, dwb_ref):
        def strip(rows):
            xv = x_ref[rows, :]
            wv = w_ref[rows, :]
            y, shifted = _causal_conv(xv, wv, b_ref[rows, :])
            sg = _sigmoid(y)
            dy = do_ref[rows, :] * (sg * (1.0 + y * (1.0 - sg)))
            dx, dwb_ref[rows, :] = _causal_conv_bwd(dy, xv, shifted, wv)
            dx_ref[rows, :] = dx.astype(BF16)

        strip(slice(None))

    return pl.pallas_call(
        body,
        name=name,
        grid=(nrows // CONV_ROWS,),
        in_specs=[
            pl.BlockSpec((CONV_ROWS, t), lambda i: (p0 + i, 0)),
            pl.BlockSpec((CONV_ROWS, SSD_CONV), lambda i: (c0 + i, 0)),
            pl.BlockSpec((CONV_ROWS, 1), lambda i: (c0 + i, 0)),
            pl.BlockSpec((CONV_ROWS, t), lambda i: (i, 0)),
        ],
        out_specs=[pl.BlockSpec((CONV_ROWS, t), lambda i: (i, 0)), pl.BlockSpec((CONV_ROWS, 128), lambda i: (i, 0))],
        out_shape=[jax.ShapeDtypeStruct((nrows, t), BF16), jax.ShapeDtypeStruct((nrows, 128), F32)],
        compiler_params=_cparams("parallel"),
    )(proj, w_col, b_col, dout)


GROUP_ROWS = HEADS_PER_GROUP * SSD_HEAD_DIM


def _ssd_specs(order):
    xb = D_INNER // BC_DIM
    dtb = OFF_DT // N_SSD_HEADS
    col = pl.BlockSpec((N_SSD_HEADS, 1), lambda c: (0, 0))
    return [
        pl.BlockSpec((D_INNER, CHUNK), lambda c: (0, order(c))),
        pl.BlockSpec((BC_DIM, CHUNK), lambda c: (xb, order(c))),
        pl.BlockSpec((BC_DIM, CHUNK), lambda c: (xb + 1, order(c))),
        pl.BlockSpec((N_SSD_HEADS, CHUNK), lambda c: (dtb, order(c))),
        col, col, col,
    ]


def _ssd_common(dt_ref, dtb_ref, alog_ref):
    z = dt_ref[...] + dtb_ref[...]
    dt = _softplus(z)
    a_neg = -jnp.exp(alog_ref[...])
    d_a = dt * a_neg
    row = lax.broadcasted_iota(jnp.int32, (CHUNK, CHUNK), 0)
    colm = lax.broadcasted_iota(jnp.int32, (CHUNK, CHUNK), 1)
    upper = (row <= colm).astype(F32)
    a_cs = jnp.dot(d_a, upper, precision=HIGHEST, preferred_element_type=F32)
    a_last = _rowsum(d_a)
    return z, dt, a_neg, a_cs, a_last, row >= colm, row == colm


def _decay(a_row, causal):
    a_s = jnp.broadcast_to(a_row, (CHUNK, CHUNK))
    seg = a_s.T - a_s
    return jnp.where(causal, jnp.exp(jnp.where(causal, seg, 0.0)), 0.0)


def _ssd_fwd(xbc, proj, dtb_col, alog_col, dsk_col):
    t = xbc.shape[1]
    nc = t // CHUNK

    def body(xs_ref, b_ref, c_ref, dt_ref, dtb_ref, alog_ref, dsk_ref, y_ref, hst_ref, h_scr):
        @pl.when(pl.program_id(0) == 0)
        def _():
            h_scr[...] = jnp.zeros_like(h_scr)

        _, dt, _, a_cs, a_last, causal, _ = _ssd_common(dt_ref, dtb_ref, alog_ref)
        hst_ref[0] = h_scr[...]
        dsk = dsk_ref[...]
        for g in range(N_SSD_GROUPS):
            grows = slice(g * D_STATE, (g + 1) * D_STATE)
            bb = b_ref[grows, :].astype(BF16)
            cb_ = c_ref[grows, :].astype(BF16)
            cb = _dot_tn(cb_, bb)
            for j in range(g * HEADS_PER_GROUP, (g + 1) * HEADS_PER_GROUP):
                rows = slice(j * SSD_HEAD_DIM, (j + 1) * SSD_HEAD_DIM)
                a = a_cs[j:j + 1, :]
                m = (cb * _decay(a, causal)).astype(BF16)
                xs = xs_ref[rows, :]
                xc = xs * dt[j:j + 1, :]
                hj = h_scr[rows, :]
                y = _dot_nt(xc.astype(BF16), m) + _dot(hj.astype(BF16), cb_) * jnp.exp(a) + dsk[j:j + 1, :] * xs
                y_ref[rows, :] = y
                al = a_last[j:j + 1, :]
                w = jnp.exp(al - a)
                h_scr[rows, :] = jnp.exp(al) * hj + _dot_nt((xc * w).astype(BF16), bb)

    return pl.pallas_call(
        body,
        name="ssd_fwd",
        grid=(nc,),
        in_specs=_ssd_specs(lambda c: c),
        out_specs=[
            pl.BlockSpec((D_INNER, CHUNK), lambda c: (0, c)),
            pl.BlockSpec((1, D_INNER, D_STATE), lambda c: (c, 0, 0)),
        ],
        out_shape=[
            jax.ShapeDtypeStruct((D_INNER, t), F32),
            jax.ShapeDtypeStruct((nc, D_INNER, D_STATE), F32),
        ],
        scratch_shapes=[pltpu.VMEM((D_INNER, D_STATE), F32)],
        compiler_params=_cparams("arbitrary"),
    )(xbc, xbc, xbc, proj, dtb_col, alog_col, dsk_col)


def _ssd_bwd(xbc, proj, dtb_col, alog_col, dsk_col, hst, dy):
    t = xbc.shape[1]
    nc = t // CHUNK
    rev = lambda c: nc - 1 - c

    def body(xs_ref, b_ref, c_ref, dt_ref, dtb_ref, alog_ref, dsk_ref, hst_ref, dy_ref,
             dxs_ref, db_ref, dc_ref, ddt_ref, dalog_ref, ddsk_ref, ddtb_ref, dh_scr, da_scr, ddt_scr, dd_scr):
        @pl.when(pl.program_id(0) == 0)
        def _():
            dh_scr[...] = jnp.zeros_like(dh_scr)
            dalog_ref[...] = jnp.zeros_like(dalog_ref)
            ddsk_ref[...] = jnp.zeros_like(ddsk_ref)
            ddtb_ref[...] = jnp.zeros_like(ddtb_ref)

        z, dt, a_neg, a_cs, a_last, causal, eye = _ssd_common(dt_ref, dtb_ref, alog_ref)
        dsk = dsk_ref[...]
        last_lane = lax.broadcasted_iota(jnp.int32, (1, CHUNK), 1) == CHUNK - 1
        for g in range(N_SSD_GROUPS):
            grows = slice(g * D_STATE, (g + 1) * D_STATE)
            bb = b_ref[grows, :].astype(BF16)
            cb_ = c_ref[grows, :].astype(BF16)
            cb = _dot_tn(cb_, bb)
            dcb = jnp.zeros((CHUNK, CHUNK), F32)
            dc_acc = jnp.zeros((D_STATE, CHUNK), F32)
            db_acc = jnp.zeros((D_STATE, CHUNK), F32)
            for j in range(g * HEADS_PER_GROUP, (g + 1) * HEADS_PER_GROUP):
                rows = slice(j * SSD_HEAD_DIM, (j + 1) * SSD_HEAD_DIM)
                a = a_cs[j:j + 1, :]
                al = a_last[j:j + 1, :]
                lam = _decay(a, causal)
                mf = cb * lam
                xs = xs_ref[rows, :]
                dtj = dt[j:j + 1, :]
                xc = xs * dtj
                w = jnp.exp(al - a)
                e = jnp.exp(a)
                gam = jnp.exp(al)
                hj = hst_ref[0, rows, :]
                hjb = hj.astype(BF16)
                dyv = dy_ref[rows, :]
                dyb = dyv.astype(BF16)
                dd_scr[j:j + 1, :] = _colsum(dyv * xs)
                gb = (dyv * e).astype(BF16)
                dh_in = _dot_nt(gb, cb_)
                dc_acc = dc_acc + _dot_tn(hjb, gb)
                yoff = _dot(hjb, cb_) * e
                da = _colsum(dyv * yoff)
                dm = _dot_tn(dyb, xc.astype(BF16))
                dxc = _dot(dyb, mf.astype(BF16))
                dcb = dcb + dm * lam
                nmat = dm * mf
                rs = jnp.broadcast_to(_rowsum(nmat), (CHUNK, CHUNK))
                da = da + _colsum(jnp.where(eye, rs, 0.0)) - _colsum(nmat)
                ds = dh_scr[rows, :]
                dsb = ds.astype(BF16)
                t1 = _dot(dsb, bb)
                xcw = xc * w
                dxc = dxc + w * t1
                dww = _colsum(xcw * t1)
                da_l = _rowsum(dww) + _rowsum(_colsum(ds * hj)) * gam
                da = da - dww + jnp.where(last_lane, da_l, 0.0)
                db_acc = db_acc + _dot_tn(dsb, xcw.astype(BF16))
                dh_scr[rows, :] = gam * ds + dh_in
                dxs_ref[rows, :] = dsk[j:j + 1, :] * dyv + dxc * dtj
                da_scr[j:j + 1, :] = da
                ddt_scr[j:j + 1, :] = _colsum(dxc * xs)
            dcbb = dcb.astype(BF16)
            dc_ref[grows, :] = dc_acc + _dot_nt(bb, dcbb)
            db_ref[grows, :] = db_acc + _dot(cb_, dcbb)
        dda = jnp.dot(da_scr[...], causal.astype(F32), precision=HIGHEST, preferred_element_type=F32)
        ddt = ddt_scr[...] + dda * a_neg
        ddt_raw = ddt * _sigmoid(z)
        ddt_ref[...] = ddt_raw
        ddtb_ref[...] += _rowsum(ddt_raw)
        dalog_ref[...] += _rowsum(dda * dt) * a_neg
        ddsk_ref[...] += _rowsum(dd_scr[...])

    col = pl.BlockSpec((N_SSD_HEADS, 1), lambda c: (0, 0))
    bc = pl.BlockSpec((BC_DIM, CHUNK), lambda c: (0, rev(c)))
    xs_spec = pl.BlockSpec((D_INNER, CHUNK), lambda c: (0, rev(c)))
    small = pltpu.VMEM((N_SSD_HEADS, CHUNK), F32)
    return pl.pallas_call(
        body,
        name="ssd_bwd",
        grid=(nc,),
        in_specs=_ssd_specs(rev) + [pl.BlockSpec((1, D_INNER, D_STATE), lambda c: (rev(c), 0, 0)), xs_spec],
        out_specs=[xs_spec, bc, bc, pl.BlockSpec((N_SSD_HEADS, CHUNK), lambda c: (0, rev(c))), col, col, col],
        out_shape=[
            jax.ShapeDtypeStruct((D_INNER, t), F32),
            jax.ShapeDtypeStruct((BC_DIM, t), F32),
            jax.ShapeDtypeStruct((BC_DIM, t), F32),
            jax.ShapeDtypeStruct((N_SSD_HEADS, t), F32),
            jax.ShapeDtypeStruct((N_SSD_HEADS, 1), F32),
            jax.ShapeDtypeStruct((N_SSD_HEADS, 1), F32),
            jax.ShapeDtypeStruct((N_SSD_HEADS, 1), F32),
        ],
        scratch_shapes=[pltpu.VMEM((D_INNER, D_STATE), F32), small, small, small],
        compiler_params=_cparams("arbitrary"),
    )(xbc, xbc, xbc, proj, dtb_col, alog_col, dsk_col, hst, dy)


GN_ROWS = D_INNER // N_SSD_GROUPS


def _gnorm_fwd(y, proj, w_col):
    t = y.shape[1]
    tt = _tile(t, (512, 256, 128))
    z0 = OFF_Z // GN_ROWS

    def body(y_ref, z_ref, w_ref, o_ref):
        zv = z_ref[...]
        u = y_ref[...] * (zv * _sigmoid(zv))
        r = lax.rsqrt(jnp.mean(u * u, axis=0, keepdims=True) + EPS)
        o_ref[...] = (u * r * w_ref[...]).astype(BF16)

    blk = pl.BlockSpec((GN_ROWS, tt), lambda g, i: (g, i))
    return pl.pallas_call(
        body,
        name="gnorm_fwd",
        grid=(N_SSD_GROUPS, t // tt),
        in_specs=[blk, pl.BlockSpec((GN_ROWS, tt), lambda g, i: (z0 + g, i)), pl.BlockSpec((GN_ROWS, 1), lambda g, i: (g, 0))],
        out_specs=blk,
        out_shape=jax.ShapeDtypeStruct((D_INNER, t), BF16),
        compiler_params=_cparams("parallel", "parallel"),
    )(y, proj, w_col)


def _gnorm_bwd(dout, y, proj, w_col, send=None):
    t = y.shape[1]
    tt = _tile(t, (512, 256, 128))
    z0 = OFF_Z // GN_ROWS

    def body(do_ref, y_ref, z_ref, w_ref, dy_ref, dz_ref, dw_ref):
        @pl.when(pl.program_id(1) == 0)
        def _():
            dw_ref[...] = jnp.zeros_like(dw_ref)

        zv = z_ref[...]
        yv = y_ref[...]
        sg = _sigmoid(zv)
        sz = zv * sg
        u = yv * sz
        r = lax.rsqrt(jnp.mean(u * u, axis=0, keepdims=True) + EPS)
        xhat = u * r
        dov = do_ref[...]
        dw_ref[...] += _rowsum(dov * xhat)
        dxhat = dov * w_ref[...]
        du = r * (dxhat - xhat * jnp.mean(dxhat * xhat, axis=0, keepdims=True))
        dy_ref[...] = du * sz
        dz_ref[...] = (du * yv * (sg * (1.0 + zv * (1.0 - sg)))).astype(BF16)

    blk = pl.BlockSpec((GN_ROWS, tt), lambda g, i: (g, i))
    col = pl.BlockSpec((GN_ROWS, 1), lambda g, i: (g, 0))
    return _call(
        body,
        name="gnorm_bwd",
        grid=(N_SSD_GROUPS, t // tt),
        in_specs=[blk, blk, pl.BlockSpec((GN_ROWS, tt), lambda g, i: (z0 + g, i)), col],
        out_specs=[blk, blk, col],
        out_shape=[jax.ShapeDtypeStruct((D_INNER, t), F32), jax.ShapeDtypeStruct((D_INNER, t), BF16),
                   jax.ShapeDtypeStruct((D_INNER, 1), F32)],
        semantics=("parallel", "arbitrary"), args=(dout, y, proj, w_col), send=send)


GATE_ROWS = 128


def _gate_specs(t):
    nr = D_MODEL // GATE_ROWS
    blk = pl.BlockSpec((GATE_ROWS, t), lambda r: (r, 0))
    rows_from = lambda first: pl.BlockSpec(
        (pl.Element(GATE_ROWS), pl.Element(t)), lambda r: (pl.multiple_of(first + GATE_ROWS * r, N_SSD_HEADS), 0))
    return blk, [
        rows_from(OFF_GA),
        rows_from(OFF_GS),
        pl.BlockSpec((GATE_ROWS, 1), lambda r: (r, 0)),
        pl.BlockSpec((GATE_ROWS, 1), lambda r: (nr + r, 0)),
        blk, blk,
    ]


def _gate_fwd(proj, b_col, attn, ssd):
    t = proj.shape[1]
    blk, specs = _gate_specs(t)

    def body(ga_ref, gs_ref, ba_ref, bs_ref, a_ref, s_ref, o_ref):
        o_ref[...] = (_sigmoid(ga_ref[...] + ba_ref[...]) * a_ref[...]
                      + _sigmoid(gs_ref[...] + bs_ref[...]) * s_ref[...]).astype(BF16)

    return pl.pallas_call(
        body,
        name="gate_fwd",
        grid=(D_MODEL // GATE_ROWS,),
        in_specs=specs,
        out_specs=blk,
        out_shape=jax.ShapeDtypeStruct((D_MODEL, t), BF16),
        compiler_params=_cparams("parallel"),
    )(proj, proj, b_col, b_col, attn, ssd)


def _gate_bwd(proj, b_col, attn, ssd, dmix, send=None):
    t = proj.shape[1]
    blk, specs = _gate_specs(t)

    def body(ga_ref, gs_ref, ba_ref, bs_ref, a_ref, s_ref, dm_ref, da_ref, dso_ref, dga_ref, dgs_ref, dba_ref, dbs_ref):
        dm = dm_ref[...]
        sa = _sigmoid(ga_ref[...] + ba_ref[...])
        ss = _sigmoid(gs_ref[...] + bs_ref[...])
        da_ref[...] = (dm * sa).astype(BF16)
        dso_ref[...] = (dm * ss).astype(BF16)
        dga = dm * a_ref[...] * sa * (1.0 - sa)
        dgs = dm * s_ref[...] * ss * (1.0 - ss)
        dga_ref[...] = dga.astype(BF16)
        dgs_ref[...] = dgs.astype(BF16)
        dba_ref[...] = _rowsum(dga)
        dbs_ref[...] = _rowsum(dgs)

    col = pl.BlockSpec((GATE_ROWS, 1), lambda r: (r, 0))
    act = jax.ShapeDtypeStruct((D_MODEL, t), BF16)
    bias = jax.ShapeDtypeStruct((D_MODEL, 1), F32)
    return _call(
        body,
        name="gate_bwd",
        grid=(D_MODEL // GATE_ROWS,),
        in_specs=specs + [blk],
        out_specs=[blk, blk, blk, blk, col, col],
        out_shape=[act, act, act, act, bias, bias],
        semantics=("parallel",), args=(proj, proj, b_col, b_col, attn, ssd, dmix), send=send)


FFN_ROWS = 256


def _ffn_fwd(u0, w_col, b_col):
    t = u0.shape[2]

    def body(u_ref, w_ref, b_ref, o_ref):
        def strip(rows):
            val, _ = _causal_conv(u_ref[0, rows, :], w_ref[0, rows, :], b_ref[0, rows, :])
            gt, _ = _causal_conv(u_ref[1, rows, :], w_ref[1, rows, :], b_ref[1, rows, :])
            o_ref[rows, :] = (gt * _sigmoid(gt) * val).astype(BF16)

        strip(slice(None))

    return pl.pallas_call(
        body,
        name="ffn_fwd",
        grid=(D_FF // FFN_ROWS,),
        in_specs=[
            pl.BlockSpec((2, FFN_ROWS, t), lambda i: (0, i, 0)),
            pl.BlockSpec((2, FFN_ROWS, FFN_CONV), lambda i: (0, i, 0)),
            pl.BlockSpec((2, FFN_ROWS, 1), lambda i: (0, i, 0)),
        ],
        out_specs=pl.BlockSpec((FFN_ROWS, t), lambda i: (i, 0)),
        out_shape=jax.ShapeDtypeStruct((D_FF, t), BF16),
        compiler_params=_cparams("parallel"),
    )(u0, w_col, b_col)


def _ffn_bwd(u0, w_col, b_col, dg, send=None):
    t = u0.shape[2]

    def body(u_ref, w_ref, b_ref, dg_ref, du_ref, dwb_ref):
        def strip(rows):
            xval, wval = u_ref[0, rows, :], w_ref[0, rows, :]
            xgt, wgt = u_ref[1, rows, :], w_ref[1, rows, :]
            val, sh_val = _causal_conv(xval, wval, b_ref[0, rows, :])
            gt, sh_gt = _causal_conv(xgt, wgt, b_ref[1, rows, :])
            sg = _sigmoid(gt)
            dgv = dg_ref[rows, :]
            dval = dgv * (gt * sg)
            dgt = dgv * val * (sg * (1.0 + gt * (1.0 - sg)))
            dx, dwb_ref[0, rows, :] = _causal_conv_bwd(dval, xval, sh_val, wval)
            du_ref[0, rows, :] = dx.astype(BF16)
            dx, dwb_ref[1, rows, :] = _causal_conv_bwd(dgt, xgt, sh_gt, wgt)
            du_ref[1, rows, :] = dx.astype(BF16)

        strip(slice(None))

    return _call(
        body,
        name="ffn_bwd",
        grid=(D_FF // FFN_ROWS,),
        in_specs=[
            pl.BlockSpec((2, FFN_ROWS, t), lambda i: (0, i, 0)),
            pl.BlockSpec((2, FFN_ROWS, FFN_CONV), lambda i: (0, i, 0)),
            pl.BlockSpec((2, FFN_ROWS, 1), lambda i: (0, i, 0)),
            pl.BlockSpec((FFN_ROWS, t), lambda i: (i, 0)),
        ],
        out_specs=[pl.BlockSpec((2, FFN_ROWS, t), lambda i: (0, i, 0)), pl.BlockSpec((2, FFN_ROWS, 128), lambda i: (0, i, 0))],
        out_shape=[jax.ShapeDtypeStruct((2, D_FF, t), BF16), jax.ShapeDtypeStruct((2, D_FF, 128), F32)],
        semantics=("parallel",), args=(u0, w_col, b_col, dg), send=send)


def _adamw_math(w, g, m, v):
    m = ADAM_B1 * m + (1.0 - ADAM_B1) * g
    v = ADAM_B2 * v + (1.0 - ADAM_B2) * (g * g)
    m_hat = m / (1.0 - ADAM_B1 ** ADAM_STEP)
    v_hat = v / (1.0 - ADAM_B2 ** ADAM_STEP)
    delta = -ADAM_LR * (m_hat / (jnp.sqrt(v_hat) + ADAM_EPS) + ADAM_WD * w)
    return delta, m, v


def _adamw_sharded(parts, w, m, v, name):
    r, c = w.shape[0], w.shape[-1]
    tc = _tile(c, (256, 128))
    blk_shape = (r, tc) if w.ndim == 2 else (r, 1, tc)
    slots = parts.shape[0]

    def body(p_ref, w_ref, m_ref, v_ref, g_ref, d_ref, nm_ref, nv_ref):
        g = p_ref[0].astype(F32)
        for s in range(1, slots):
            g = g + p_ref[s].astype(F32)
        flat = lambda ref: ref[...].reshape(r, tc)
        d, nm, nv = _adamw_math(flat(w_ref), g, flat(m_ref), flat(v_ref))
        for ref, val in ((g_ref, g), (d_ref, d), (nm_ref, nm), (nv_ref, nv)):
            ref[...] = val.reshape(blk_shape)

    blk = pl.BlockSpec(blk_shape, (lambda i: (0, i)) if w.ndim == 2 else (lambda i: (0, 0, i)))
    out = jax.ShapeDtypeStruct(w.shape, F32)
    return pl.pallas_call(
        body,
        name=name,
        grid=(c // tc,),
        in_specs=[pl.BlockSpec((slots, r, tc), lambda i: (0, 0, i)), blk, blk, blk],
        out_specs=[blk, blk, blk, blk],
        out_shape=[out, out, out, out],
        compiler_params=_cparams("parallel"),
    )(parts, w, m, v)


def _lane_offsets(sizes):
    offsets, pos = [], 0
    for n in sizes:
        offsets.append(pos)
        pos += -(-n // 128) * 128
    return offsets, pos


def _pack_row(parts):
    rows = [p.reshape(1, -1).astype(F32) for p in parts]
    return jnp.concatenate([jnp.pad(r, ((0, 0), (0, -r.shape[1] % 128))) for r in rows], axis=1)


def _small_update(parts, me, full_sizes, ws, ms, vs):
    n = len(ws)
    offsets, _ = _lane_offsets([1] + list(full_sizes))

    def body(me_ref, p_ref, *refs):
        w_refs, m_refs, v_refs = refs[:n], refs[n:2 * n], refs[2 * n:3 * n]
        scalar_ref, out_refs = refs[3 * n], refs[3 * n + 1:]
        tot = p_ref[0]
        for s in range(1, N_DEV):
            tot = tot + p_ref[s]
        scalar_ref[...] = tot[:, 0:1]
        for k in range(n):
            g_ref, d_ref, nm_ref, nv_ref = out_refs[4 * k:4 * k + 4]
            taps, cols = w_refs[k].shape
            if taps == 1:
                g_ref[...] = tot[:, offsets[k + 1]:offsets[k + 1] + cols]
            else:
                full = full_sizes[k] // taps
                for tap in range(taps):
                    mine = jnp.zeros((1, cols), F32)
                    for d in range(N_DEV):
                        lo = offsets[k + 1] + tap * full + d * cols
                        mine = jnp.where(me_ref[0] == d, tot[:, lo:lo + cols], mine)
                    g_ref[tap:tap + 1, :] = mine
            d_ref[...], nm_ref[...], nv_ref[...] = _adamw_math(w_refs[k][...], g_ref[...], m_refs[k][...], v_refs[k][...])

    vmem = pl.BlockSpec(memory_space=pltpu.VMEM)
    out_shape = [jax.ShapeDtypeStruct((1, 1), F32)]
    for wk in ws:
        out_shape += [jax.ShapeDtypeStruct(wk.shape, F32)] * 4
    res = pl.pallas_call(
        body,
        name="small_update",
        in_specs=[pl.BlockSpec(memory_space=pltpu.SMEM)] + [vmem] * (1 + 3 * n),
        out_specs=[vmem] * len(out_shape),
        out_shape=out_shape,
    )(me, parts, *ws, *ms, *vs)
    return res[0], [res[1 + 4 * k:5 + 4 * k] for k in range(n)]


ANY = pl.BlockSpec(memory_space=pl.ANY)
FLIPS = [(k >> 2 & 1, k >> 1 & 1, k & 1) for k in range(1, N_DEV)]


def _place():
    return lax.axis_index("x"), lax.axis_index("y"), lax.axis_index("c")


HBM = pl.BlockSpec(memory_space=pltpu.HBM)
SEM = pl.BlockSpec(memory_space=pltpu.SEMAPHORE)
EFFECT = pltpu.SideEffectType.DATAFLOW_SIDE_EFFECTING


def _peer_copy(gather, src_ref, land_ref, send_sems, recv_sems, k, sending):
    x, y, c = _place()
    fx, fy, fc = FLIPS[k]
    me = 4 * x + 2 * y + c
    peer = 4 * (x ^ fx) + 2 * (y ^ fy) + (c ^ fc)
    return pltpu.make_async_remote_copy(
        src_ref=src_ref if gather else src_ref.at[peer],
        dst_ref=land_ref.at[me if sending else peer],
        send_sem=send_sems.at[k], recv_sem=recv_sems.at[k],
        device_id=(x ^ fx, y ^ fy, c ^ fc), device_id_type=MESH)


SIBLING = 0
OTHER_CHIPS = (1, 3, 5)


def _gather_start(srcs, name, via_sibling):
    n = len(srcs)
    lands = [lax.empty((N_DEV,) + s.shape, s.dtype) for s in srcs]

    def body(*refs):
        src_refs, land_refs = refs[:n], refs[n:2 * n]
        send, recv = refs[2 * n:3 * n], refs[3 * n:4 * n]
        for i in range(n):
            for k in (SIBLING,) + OTHER_CHIPS if via_sibling else range(N_DEV - 1):
                _peer_copy(True, src_refs[i], land_refs[i], send[i], recv[i], k, True).start()

    sem = pltpu.SemaphoreType.DMA((N_DEV - 1,))
    hbm = lambda a: pltpu.HBM(a.shape, a.dtype)
    res = pl.pallas_call(
        body,
        name=name,
        in_specs=[HBM] * (2 * n),
        out_specs=[SEM] * (2 * n) + [HBM] * (2 * n),
        out_shape=[sem] * (2 * n) + [hbm(s) for s in srcs] + [hbm(a) for a in lands],
        input_output_aliases={i: 2 * n + i for i in range(2 * n)},
        compiler_params=pltpu.CompilerParams(has_side_effects=EFFECT),
    )(*[pltpu.with_memory_space_constraint(a, pltpu.HBM) for a in list(srcs) + lands])
    return res[:n], res[n:2 * n], res[2 * n:3 * n], res[3 * n:4 * n]


def _exchange_wait(send_sems, recv_sems, src, land, after, gather, name):
    def body(src_ref, land_ref, send_ref, recv_ref, after_ref, src_out, land_out):
        for k in range(N_DEV - 1):
            cp = _peer_copy(gather, src_ref, land_ref, send_ref, recv_ref, k, False)
            cp.wait_send()
            cp.wait_recv()

    hbm = lambda a: pltpu.HBM(a.shape, a.dtype)
    return pl.pallas_call(
        body,
        name=name,
        in_specs=[HBM, HBM, SEM, SEM, ANY],
        out_specs=[HBM, HBM],
        out_shape=[hbm(src), hbm(land)],
        input_output_aliases={0: 0, 1: 1},
        compiler_params=pltpu.CompilerParams(has_side_effects=EFFECT),
    )(src, land, send_sems, recv_sems, after)


def _own_slot(src, land, me, gather):
    own = src[None] if gather else lax.dynamic_slice_in_dim(src, me, 1, axis=0)
    return lax.dynamic_update_slice_in_dim(land, own, me, axis=0)


def _forwarded_copy(land_ref, send_sems, recv_sems, j, sending):
    x, y, c = _place()
    fx, fy, _ = FLIPS[OTHER_CHIPS[j]]
    slot = 4 * (x ^ fx) + 2 * (y ^ fy) + (c if sending else 1 - c)
    return pltpu.make_async_remote_copy(
        src_ref=land_ref.at[slot], dst_ref=land_ref.at[slot], send_sem=send_sems.at[j], recv_sem=recv_sems.at[j],
        device_id=(x, y, 1 - c), device_id_type=MESH)


def _gather_forward(send_sems, recv_sems, srcs, lands, after, name):
    n = len(srcs)

    def body(*refs):
        src_refs, land_refs = refs[:n], refs[n:2 * n]
        send, recv = refs[2 * n:3 * n], refs[3 * n:4 * n]
        fwd_send, fwd_recv = refs[4 * n + 1:5 * n + 1], refs[5 * n + 1:6 * n + 1]
        for i in range(n):
            for j, k in enumerate(OTHER_CHIPS):
                _peer_copy(True, src_refs[i], land_refs[i], send[i], recv[i], k, False).wait_recv()
                _forwarded_copy(land_refs[i], fwd_send[i], fwd_recv[i], j, True).start()

    sem = pltpu.SemaphoreType.DMA((len(OTHER_CHIPS),))
    hbm = lambda a: pltpu.HBM(a.shape, a.dtype)
    res = pl.pallas_call(
        body,
        name=name,
        in_specs=[HBM] * (2 * n) + [SEM] * (2 * n) + [ANY],
        out_specs=[SEM] * (2 * n) + [HBM] * (2 * n),
        out_shape=[sem] * (2 * n) + [hbm(a) for a in srcs] + [hbm(a) for a in lands],
        input_output_aliases={i: 2 * n + i for i in range(2 * n)},
        compiler_params=pltpu.CompilerParams(has_side_effects=EFFECT),
    )(*srcs, *lands, *send_sems, *recv_sems, after)
    return res[:n], res[n:2 * n], res[2 * n:3 * n], res[3 * n:4 * n]


def _gather_wait_forwarded(send_sems, recv_sems, fwd_send, fwd_recv, src, land, after, name):
    def body(src_ref, land_ref, send_ref, recv_ref, fwd_send_ref, fwd_recv_ref, after_ref, src_out, land_out):
        for k in (SIBLING,) + OTHER_CHIPS:
            _peer_copy(True, src_ref, land_ref, send_ref, recv_ref, k, False).wait_send()
        _peer_copy(True, src_ref, land_ref, send_ref, recv_ref, SIBLING, False).wait_recv()
        for j in range(len(OTHER_CHIPS)):
            _forwarded_copy(land_ref, fwd_send_ref, fwd_recv_ref, j, True).wait_send()
            _forwarded_copy(land_ref, fwd_send_ref, fwd_recv_ref, j, False).wait_recv()

    hbm = lambda a: pltpu.HBM(a.shape, a.dtype)
    return pl.pallas_call(
        body,
        name=name,
        in_specs=[HBM, HBM, SEM, SEM, SEM, SEM, ANY],
        out_specs=[HBM, HBM],
        out_shape=[hbm(src), hbm(land)],
        input_output_aliases={0: 0, 1: 1},
        compiler_params=pltpu.CompilerParams(has_side_effects=EFFECT),
    )(src, land, send_sems, recv_sems, fwd_send, fwd_recv, after)


N_CHIPS = N_DEV // 2


def _pair_exchange(theirs, name):
    def start(src_ref, land_ref, send_sems, recv_sems, src_out, land_out):
        x, y, c = _place()
        for q in range(N_CHIPS):
            pltpu.make_async_remote_copy(
                src_ref=src_ref.at[q], dst_ref=land_ref.at[q], send_sem=send_sems.at[q], recv_sem=recv_sems.at[q],
                device_id=(x, y, 1 - c), device_id_type=MESH).start()

    def wait(src_ref, land_ref, send_sems, recv_sems, src_out, land_out):
        x, y, c = _place()
        for q in range(N_CHIPS):
            cp = pltpu.make_async_remote_copy(
                src_ref=src_ref.at[q], dst_ref=land_ref.at[q], send_sem=send_sems.at[q], recv_sem=recv_sems.at[q],
                device_id=(x, y, 1 - c), device_id_type=MESH)
            cp.wait_send()
            cp.wait_recv()

    sem = pltpu.SemaphoreType.DMA((N_CHIPS,))
    hbm = pltpu.HBM(theirs.shape, theirs.dtype)
    params = pltpu.CompilerParams(has_side_effects=EFFECT)
    send_sems, recv_sems, src, land = pl.pallas_call(
        start, name=name + "_start", in_specs=[HBM, HBM], out_specs=[SEM, SEM, HBM, HBM],
        out_shape=[sem, sem, hbm, hbm], input_output_aliases={0: 2, 1: 3}, compiler_params=params,
    )(pltpu.with_memory_space_constraint(theirs, pltpu.HBM),
      pltpu.with_memory_space_constraint(lax.empty(theirs.shape, theirs.dtype), pltpu.HBM))
    return pl.pallas_call(
        wait, name=name + "_wait", in_specs=[HBM, HBM, SEM, SEM], out_specs=[HBM, HBM], out_shape=[hbm, hbm],
        input_output_aliases={0: 0, 1: 1}, compiler_params=params,
    )(src, land, send_sems, recv_sems)[1]


def _pair_add(mine, landed, name):
    q, r, c = mine.shape
    tc = _tile(c, (256, 128))

    def body(a_ref, b_ref, o_ref):
        o_ref[...] = (a_ref[...].astype(F32) + b_ref[...].astype(F32)).astype(BF16)

    blk = pl.BlockSpec((1, r, tc), lambda i, j: (i, 0, j))
    return pl.pallas_call(
        body, name=name, grid=(q, c // tc), in_specs=[blk, blk], out_specs=blk,
        out_shape=jax.ShapeDtypeStruct(mine.shape, BF16), compiler_params=_cparams("parallel", "parallel"),
    )(mine, landed)


def _chip_copy(src_ref, land_ref, send_sems, recv_sems, j, sending):
    x, y, c = _place()
    fx, fy, _ = FLIPS[OTHER_CHIPS[j]]
    here, there = 2 * x + y, 2 * (x ^ fx) + (y ^ fy)
    return pltpu.make_async_remote_copy(
        src_ref=src_ref.at[there], dst_ref=land_ref.at[here if sending else there],
        send_sem=send_sems.at[j], recv_sem=recv_sems.at[j],
        device_id=(x ^ fx, y ^ fy, c), device_id_type=MESH)


def _chip_wait(send_sems, recv_sems, src, land, after, name):
    def body(src_ref, land_ref, send_ref, recv_ref, after_ref, src_out, land_out):
        for j in range(len(OTHER_CHIPS)):
            cp = _chip_copy(src_ref, land_ref, send_ref, recv_ref, j, False)
            cp.wait_send()
            cp.wait_recv()

    hbm = lambda a: pltpu.HBM(a.shape, a.dtype)
    return pl.pallas_call(
        body,
        name=name,
        in_specs=[HBM, HBM, SEM, SEM, ANY],
        out_specs=[HBM, HBM],
        out_shape=[hbm(src), hbm(land)],
        input_output_aliases={0: 0, 1: 1},
        compiler_params=pltpu.CompilerParams(has_side_effects=EFFECT),
    )(src, land, send_sems, recv_sems, after)


def _col(v):
    return v.reshape(-1, 1).astype(F32)


def _local_step(xt, tgt, weight, small, pair_sums):
    t = xt.shape[1]
    n1 = _col(small["norm1_w"])
    n2 = _col(small["norm2_w"])
    nf = _col(small["final_norm_w"])
    bg = _col(small["b_gate"])
    sinks = small["attn_sinks"].reshape(-1).astype(F32)
    cbias = _col(small["ssd_conv_b"])
    dtb = _col(small["dt_bias"])
    alog = _col(small["a_log"])
    dsk = _col(small["d_skip"])
    gnw = _col(small["ssd_norm_w"])
    fb = small["ffn_conv_b"].reshape(2, D_FF, 1)

    xn = _norm_fwd(xt, n1, "norm1_fwd")
    cw = weight("ssd_conv_w", xn).T
    fw = weight("ffn_conv_w", xn).T.reshape(2, D_FF, FFN_CONV)
    w_in_t = weight("w_in", xn)
    proj = _matmul(w_in_t, xn, nt=False, out_dtype=F32, name="mm_in")
    ao, lse = _attn_fwd(proj, sinks)
    w_ao = weight("w_attn_o", ao)
    attn = _matmul(w_ao, ao, nt=False, out_dtype=F32, name="mm_attn_o", tn_a=True)
    xbc = _conv_silu_fwd(proj, cw, cbias)
    y, hst = _ssd_fwd(xbc, proj, dtb, alog, dsk)
    yn = _gnorm_fwd(y, proj, gnw)
    w_so = weight("w_ssd_o", yn)
    ssd = _matmul(w_so, yn, nt=False, out_dtype=F32, name="mm_ssd_o", tn_a=True)
    mix = _gate_fwd(proj, bg, attn, ssd)
    w_out = weight("w_out", mix)
    h1 = _matmul(w_out, mix, nt=False, out_dtype=F32, name="mm_out", add=xt, tn_a=True)
    hn = _norm_fwd(h1, n2, "norm2_fwd")
    w_up_t = weight("w_up", hn)
    u0 = _matmul(w_up_t, hn, nt=False, out_dtype=F32, name="mm_up").reshape(2, D_FF, t)
    gl = _ffn_fwd(u0, fw, fb)
    w_down = weight("w_down", gl)
    h2 = _matmul(w_down, gl, nt=False, out_dtype=F32, name="mm_down", add=h1, tn_a=True)
    dh2, loss, d_nf = _final_norm_loss(h2, tgt, nf)

    g = {}
    handles = {}

    def sending(weight_name, grad, fn, *args, **kwargs):
        chunks = grad if grad.ndim == 3 else grad.reshape(N_DEV, -1, D_MODEL)
        out, handles[weight_name] = fn(*args, send=chunks, **kwargs)
        return out

    g_down = _matmul(gl, dh2, nt=True, out_dtype=BF16, name="mm_d_w_down")
    dgl = _matmul(w_down, dh2, nt=False, out_dtype=F32, name="mm_d_glu")
    du0, d_fwb = sending("w_down", g_down, _ffn_bwd, u0, fw, fb, dgl)
    du0 = du0.reshape(2 * D_FF, t)
    g_up = _matmul(du0, hn, nt=True, out_dtype=BF16, name="mm_d_w_up")
    dhn = sending("w_up", g_up, _matmul, w_up_t, du0, nt=False, out_dtype=F32, name="mm_d_hn", tn_a=True)
    dh1, d_n2 = _norm_bwd(dhn, h1, n2, dh2, "norm2_bwd")
    g_out = _matmul(mix, dh1, nt=True, out_dtype=BF16, name="mm_d_w_out")
    dmix = _matmul(w_out, dh1, nt=False, out_dtype=F32, name="mm_d_mix")
    d_attn, d_ssd, d_ga, d_gs, d_ba, d_bs = sending("w_out", g_out, _gate_bwd, proj, bg, attn, ssd, dmix)
    g_ao = _matmul(ao, d_attn, nt=True, out_dtype=BF16, name="mm_d_w_attn_o")
    dao = _matmul(w_ao, d_attn, nt=False, out_dtype=F32, name="mm_d_ao")
    dq, dk, dv, d_sinks = sending("w_attn_o", g_ao, _attn_bwd, proj, sinks, ao, lse, dao)
    g_so = _matmul(yn, d_ssd, nt=True, out_dtype=BF16, name="mm_d_w_ssd_o")
    dyn = _matmul(w_so, d_ssd, nt=False, out_dtype=F32, name="mm_d_yn")
    dy, dz, d_gnw = sending("w_ssd_o", g_so, _gnorm_bwd, dyn, y, proj, gnw)
    dxs, dbm, dcm, ddt, d_alog, d_dsk, d_dtb = _ssd_bwd(xbc, proj, dtb, alog, dsk, hst, dy)
    dx_xs, dwb_xs = _conv_silu_bwd(proj, cw, cbias, dxs, 0, "ssd_conv_bwd_x")
    dx_b, dwb_b = _conv_silu_bwd(proj, cw, cbias, dbm, D_INNER, "ssd_conv_bwd_b")
    dx_c, dwb_c = _conv_silu_bwd(proj, cw, cbias, dcm, D_INNER + BC_DIM, "ssd_conv_bwd_c")
    dwb_conv = jnp.concatenate([dwb_xs, dwb_b, dwb_c], axis=0)
    dproj = jnp.concatenate([dq, dk, dv, dz, dx_xs, dx_b, dx_c, ddt.astype(BF16), d_ga, d_gs], axis=0)
    g_in = pair_sums(_matmul(dproj, xn, nt=True, out_dtype=BF16, name="mm_d_w_in"))
    dxn = sending("w_in", g_in, _matmul, w_in_t, dproj, nt=False, out_dtype=F32, name="mm_d_xn", tn_a=True)
    dx, d_n1 = _norm_bwd(dxn, xt, n1, dh1, "norm1_bwd")

    g["norm1_w"] = d_n1
    g["b_gate"] = jnp.concatenate([d_ba, d_bs], axis=0)
    g["attn_sinks"] = d_sinks
    g["ssd_conv_w"] = dwb_conv[:, :SSD_CONV].T
    g["ssd_conv_b"] = dwb_conv[:, SSD_CONV]
    g["dt_bias"] = d_dtb
    g["a_log"] = d_alog
    g["d_skip"] = d_dsk
    g["ssd_norm_w"] = d_gnw
    g["norm2_w"] = d_n2
    d_fwb = d_fwb.reshape(2 * D_FF, 128)
    g["ffn_conv_w"] = d_fwb[:, :FFN_CONV].T
    g["ffn_conv_b"] = d_fwb[:, FFN_CONV]
    g["final_norm_w"] = d_nf
    return loss, dx, g, handles


SMALL = ("norm1_w", "b_gate", "attn_sinks", "ssd_conv_w", "ssd_conv_b", "dt_bias", "a_log", "d_skip", "ssd_norm_w",
         "norm2_w", "ffn_conv_w", "ffn_conv_b", "final_norm_w")
WEIGHT_ORDER = ("norm1_w", "w_in", "b_gate", "attn_sinks", "w_attn_o", "ssd_conv_w", "ssd_conv_b", "dt_bias", "a_log",
                "d_skip", "ssd_norm_w", "w_ssd_o", "w_out", "norm2_w", "w_up", "ffn_conv_w", "ffn_conv_b", "w_down",
                "final_norm_w")


def kernel(x, norm1_w, w_in, b_gate, attn_sinks, w_attn_o, ssd_conv_w, ssd_conv_b, dt_bias, a_log, d_skip, ssd_norm_w, w_ssd_o, w_out, norm2_w, w_up, ffn_conv_w, ffn_conv_b, w_down, final_norm_w, loss_target, m_norm1_w, m_w_in, m_b_gate, m_attn_sinks, m_w_attn_o, m_ssd_conv_w, m_ssd_conv_b, m_dt_bias, m_a_log, m_d_skip, m_ssd_norm_w, m_w_ssd_o, m_w_out, m_norm2_w, m_w_up, m_ffn_conv_w, m_ffn_conv_b, m_w_down, m_final_norm_w, v_norm1_w, v_w_in, v_b_gate, v_attn_sinks, v_w_attn_o, v_ssd_conv_w, v_ssd_conv_b, v_dt_bias, v_a_log, v_d_skip, v_ssd_norm_w, v_w_ssd_o, v_w_out, v_norm2_w, v_w_up, v_ffn_conv_w, v_ffn_conv_b, v_w_down, v_final_norm_w):
    w = dict(norm1_w=norm1_w, w_in=w_in, b_gate=b_gate, attn_sinks=attn_sinks, w_attn_o=w_attn_o, ssd_conv_w=ssd_conv_w, ssd_conv_b=ssd_conv_b, dt_bias=dt_bias, a_log=a_log, d_skip=d_skip, ssd_norm_w=ssd_norm_w, w_ssd_o=w_ssd_o, w_out=w_out, norm2_w=norm2_w, w_up=w_up, ffn_conv_w=ffn_conv_w, ffn_conv_b=ffn_conv_b, w_down=w_down, final_norm_w=final_norm_w)
    m = dict(norm1_w=m_norm1_w, w_in=m_w_in, b_gate=m_b_gate, attn_sinks=m_attn_sinks, w_attn_o=m_w_attn_o, ssd_conv_w=m_ssd_conv_w, ssd_conv_b=m_ssd_conv_b, dt_bias=m_dt_bias, a_log=m_a_log, d_skip=m_d_skip, ssd_norm_w=m_ssd_norm_w, w_ssd_o=m_w_ssd_o, w_out=m_w_out, norm2_w=m_norm2_w, w_up=m_w_up, ffn_conv_w=m_ffn_conv_w, ffn_conv_b=m_ffn_conv_b, w_down=m_w_down, final_norm_w=m_final_norm_w)
    v = dict(norm1_w=v_norm1_w, w_in=v_w_in, b_gate=v_b_gate, attn_sinks=v_attn_sinks, w_attn_o=v_w_attn_o, ssd_conv_w=v_ssd_conv_w, ssd_conv_b=v_ssd_conv_b, dt_bias=v_dt_bias, a_log=v_a_log, d_skip=v_d_skip, ssd_norm_w=v_ssd_norm_w, w_ssd_o=v_w_ssd_o, w_out=v_w_out, norm2_w=v_norm2_w, w_up=v_w_up, ffn_conv_w=v_ffn_conv_w, ffn_conv_b=v_ffn_conv_b, w_down=v_w_down, final_norm_w=v_final_norm_w)
    me = 4 * lax.axis_index("x") + 2 * lax.axis_index("y") + lax.axis_index("c")

    shards = {"ssd_conv_w": ssd_conv_w[0], "ffn_conv_w": ffn_conv_w[0], "w_in": w_in[0].T.astype(BF16),
              "w_attn_o": w_attn_o[0].astype(BF16), "w_ssd_o": w_ssd_o[0].astype(BF16), "w_out": w_out[0].astype(BF16),
              "w_up": w_up[0].T.astype(BF16), "w_down": w_down[0].astype(BF16)}
    order = list(shards)
    g_send, g_recv, g_src, g_land = _gather_start(list(shards.values()), "gather_start", True)
    first = ("ssd_conv_w", "ffn_conv_w", "w_in")
    forwarded = {}

    def weight(name, after):
        if name not in forwarded:
            group = [k for k in order if (k in first) == (name in first)]
            idx = [order.index(k) for k in group]
            handles = _gather_forward([g_send[i] for i in idx], [g_recv[i] for i in idx], [g_src[i] for i in idx],
                                      [g_land[i] for i in idx], after, "gather_forward_for_" + name)
            forwarded.update(zip(group, zip(*handles)))
        i = order.index(name)
        src, land = _gather_wait_forwarded(g_send[i], g_recv[i], *forwarded[name], after, "gather_wait_" + name)
        land = _own_slot(src, land, me, True)
        if name == "ssd_conv_w":
            return jnp.transpose(land, (1, 0, 2)).reshape(SSD_CONV, XBC_DIM)
        if name == "ffn_conv_w":
            return jnp.transpose(land, (1, 0, 2)).reshape(FFN_CONV, 2 * D_FF)
        return land.reshape(-1, D_MODEL)

    def pair_sums(grad):
        by_core = grad.reshape(N_CHIPS, 2, -1, D_MODEL)
        core = lax.axis_index("c")
        mine = lax.dynamic_index_in_dim(by_core, core, axis=1, keepdims=False)
        theirs = lax.dynamic_index_in_dim(by_core, 1 - core, axis=1, keepdims=False)
        return _pair_add(mine, _pair_exchange(theirs, "grad_pair_w_in"), "grad_pair_add_w_in")

    small = {k: w[k][0] if k != "final_norm_w" else w[k] for k in SMALL}
    loss, dx, g, pending = _local_step(x[0].T, loss_target[0].T, weight, small, pair_sums)

    packed = _pack_row([loss] + [g[k] for k in SMALL])
    s_send, s_recv, s_src, s_land = _gather_start([packed], "small_grads_start", False)

    res = {}
    after = s_src[0]
    for name in ("w_down", "w_up", "w_out", "w_attn_o", "w_ssd_o", "w_in"):
        if name == "w_in":
            parts = _own_slot(*_chip_wait(*pending[name], after, "grad_wait_" + name), me // 2, False)
        else:
            parts = _own_slot(*_exchange_wait(*pending[name], after, False, "grad_wait_" + name), me, False)
        view, back = {
            "w_in": (lambda a: jnp.transpose(a, (2, 0, 1)), lambda r: jnp.transpose(r, (1, 2, 0))),
            "w_up": (lambda a: a[0].T, lambda r: r.T[None]),
        }.get(name, (lambda a: a[0], lambda r: r[None]))
        res[name] = _adamw_sharded(parts, view(w[name]), view(m[name]), view(v[name]), "adamw_" + name)
        after = res[name][0]
        res[name] = [back(r) for r in res[name]]

    rows = _own_slot(*_exchange_wait(s_send[0], s_recv[0], s_src[0], s_land[0], after, True, "small_grads_wait"),
                     me, True)
    flat = lambda a: a.reshape(-1, a.shape[-1])
    loss_sum, updates = _small_update(
        rows, me.reshape(1), [g[k].size for k in SMALL],
        [flat(w[k]) for k in SMALL], [flat(m[k]) for k in SMALL], [flat(v[k]) for k in SMALL])
    for k, upd in zip(SMALL, updates):
        res[k] = [u.reshape(w[k].shape) for u in upd]

    grad_x = dx.T[None]
    outs = [loss_sum.reshape(()), grad_x]
    for i in range(4):
        outs.extend(res[k][i] for k in WEIGHT_ORDER)
    return tuple(outs)
```

```python
import functools

import jax
import jax.numpy as jnp
from jax import lax
from jax.experimental import pallas as pl
from jax.experimental.pallas import tpu as pltpu

F32 = jnp.float32
BF16 = jnp.bfloat16
HIGHEST = lax.Precision.HIGHEST

D_MODEL = 1024
N_Q_HEADS = 16
N_KV_HEADS = 4
HEAD_DIM = 64
WINDOW = 128
Q_PER_KV = N_Q_HEADS // N_KV_HEADS
Q_DIM = N_Q_HEADS * HEAD_DIM
KV_DIM = N_KV_HEADS * HEAD_DIM
D_INNER = 2048
SSD_HEAD_DIM = 64
N_SSD_HEADS = 32
N_SSD_GROUPS = 4
HEADS_PER_GROUP = N_SSD_HEADS // N_SSD_GROUPS
D_STATE = 128
BC_DIM = N_SSD_GROUPS * D_STATE
XBC_DIM = D_INNER + 2 * BC_DIM
SSD_CONV = 4
CHUNK = 128
D_FF = 2816
FFN_CONV = 3
EPS = 1e-5
NEG = -1e30
IN_DIM = 8736
N_DEV = 8

OFF_Q = 0
OFF_K = OFF_Q + Q_DIM
OFF_V = OFF_K + KV_DIM
OFF_Z = OFF_V + KV_DIM
OFF_X = OFF_Z + D_INNER
OFF_DT = OFF_X + XBC_DIM
OFF_GA = OFF_DT + N_SSD_HEADS
OFF_GS = OFF_GA + D_MODEL

ADAM_LR = 0.001
ADAM_B1 = 0.9
ADAM_B2 = 0.999
ADAM_EPS = 1e-08
ADAM_WD = 0.01
ADAM_STEP = 10

VMEM_LIMIT = 48 * 1024 * 1024
MESH = pl.DeviceIdType.MESH


def _cparams(*sem):
    return pltpu.CompilerParams(dimension_semantics=sem, vmem_limit_bytes=VMEM_LIMIT)


def _tile(n, prefs):
    for p in prefs:
        if n % p == 0:
            return p
    return n


def _sigmoid(x):
    return 1.0 / (1.0 + jnp.exp(-x))


def _softplus(x):
    return jnp.maximum(x, 0.0) + jnp.log(1.0 + jnp.exp(-jnp.abs(x)))


def _rowsum(x):
    return jnp.sum(x, axis=1, keepdims=True)


def _colsum(x):
    return jnp.sum(x, axis=0, keepdims=True)


def _dot(a, b):
    return jnp.dot(a, b, preferred_element_type=F32)


def _dot_nt(a, b):
    return lax.dot_general(a, b, (((1,), (1,)), ((), ())), preferred_element_type=F32)


def _dot_tn(a, b):
    return lax.dot_general(a, b, (((0,), (0,)), ((), ())), preferred_element_type=F32)


def _shift_right(x, j):
    if j == 0:
        return x
    r = pltpu.roll(x, j, 1)
    lane = lax.broadcasted_iota(jnp.int32, (x.shape[0], 128), 1)
    return jnp.concatenate([jnp.where(lane >= j, r[:, :128], 0.0), r[:, 128:]], axis=1)


def _shift_left(x, j):
    if j == 0:
        return x
    n = x.shape[1]
    r = pltpu.roll(x, n - j, 1)
    lane = lax.broadcasted_iota(jnp.int32, (x.shape[0], 128), 1)
    return jnp.concatenate([r[:, :n - 128], jnp.where(lane < 128 - j, r[:, n - 128:], 0.0)], axis=1)


def _causal_conv(xv, wv, bv):
    taps = wv.shape[1]
    shifted = [_shift_right(xv, taps - 1 - k) for k in range(taps - 1)]
    y = bv + wv[:, taps - 1:taps] * xv
    for k in range(taps - 1):
        y = y + wv[:, k:k + 1] * shifted[k]
    return y, shifted


def _causal_conv_bwd(dy, xv, shifted, wv):
    taps = wv.shape[1]
    lane = lax.broadcasted_iota(jnp.int32, (dy.shape[0], 128), 1)
    dwb = jnp.where(lane == taps, _rowsum(dy), 0.0)
    dwb = jnp.where(lane == taps - 1, _rowsum(dy * xv), dwb)
    dx = wv[:, taps - 1:taps] * dy
    for k in range(taps - 1):
        dx = dx + wv[:, k:k + 1] * _shift_left(dy, taps - 1 - k)
        dwb = jnp.where(lane == k, _rowsum(dy * shifted[k]), dwb)
    return dx, dwb


def _call(body, *, name, grid, in_specs, out_specs, out_shape, args, semantics, scratch_shapes=(), send=None):
    if send is None:
        return pl.pallas_call(body, name=name, grid=grid, in_specs=in_specs, out_specs=out_specs, out_shape=out_shape,
                              scratch_shapes=list(scratch_shapes), compiler_params=_cparams(*semantics))(*args)
    single = not isinstance(out_specs, (list, tuple))
    out_specs, out_shape = ([out_specs], [out_shape]) if single else (list(out_specs), list(out_shape))
    n_in, n_out = len(in_specs), len(out_specs)
    chips = send.shape[0] == N_DEV // 2
    n_copies = len(OTHER_CHIPS) if chips else N_DEV - 1

    def sending(*refs):
        ins, (src_ref, land_ref) = refs[:n_in], refs[n_in:n_in + 2]
        outs = refs[n_in + 2:n_in + 2 + n_out]
        send_sems, recv_sems = refs[n_in + 2 + n_out:n_in + 4 + n_out]
        scratch = refs[n_in + 6 + n_out:]
        step = 0
        for axis, size in enumerate(grid):
            step = step * size + pl.program_id(axis)

        @pl.when(step == 0)
        def _():
            for k in range(n_copies):
                if chips:
                    _chip_copy(src_ref, land_ref, send_sems, recv_sems, k, True).start()
                else:
                    _peer_copy(False, src_ref, land_ref, send_sems, recv_sems, k, True).start()

        body(*ins, *outs, *scratch)

    sem = pltpu.SemaphoreType.DMA((n_copies,))
    hbm = pltpu.HBM(send.shape, send.dtype)
    res = pl.pallas_call(
        sending, name=name, grid=grid,
        in_specs=list(in_specs) + [HBM, HBM],
        out_specs=out_specs + [SEM, SEM, HBM, HBM],
        out_shape=out_shape + [sem, sem, hbm, hbm],
        input_output_aliases={n_in: n_out + 2, n_in + 1: n_out + 3},
        scratch_shapes=list(scratch_shapes),
        compiler_params=pltpu.CompilerParams(dimension_semantics=("arbitrary",) * len(grid), vmem_limit_bytes=VMEM_LIMIT,
                                             has_side_effects=EFFECT),
    )(*args, pltpu.with_memory_space_constraint(send, pltpu.HBM),
      pltpu.with_memory_space_constraint(lax.empty(send.shape, send.dtype), pltpu.HBM))
    return (res[0] if single else list(res[:n_out])), tuple(res[n_out:])


MATMUL_VMEM_BUDGET = 36 * 1024 * 1024
MATMUL_MAX_TK = 3072


MATMUL_MAX_TM = 768


def _largest_tile(n, align, cap):
    return max(d for d in range(align, min(n, cap) + 1, align) if n % d == 0)


def _matmul_tiles(m, n, k, a_bytes, b_bytes, out_bytes, has_add, m_align, k_align):
    tm = _largest_tile(m, m_align, MATMUL_MAX_TM)
    tk = _largest_tile(k, k_align, MATMUL_MAX_TK)
    for tn in sorted({d for d in range(128, n + 1, 128) if n % d == 0}, reverse=True):
        need = 2 * (tm * tk * a_bytes + tk * tn * b_bytes) + tm * tn * (2 * out_bytes + (4 if k > tk else 0) + (8 if has_add else 0))
        if tn <= 3072 and need <= MATMUL_VMEM_BUDGET:
            return tm, tn, tk
    return tm, 128, tk


def _matmul(a, b, *, nt, out_dtype, name, add=None, tn_a=False, send=None):
    if tn_a:
        k, m = a.shape
    else:
        m, k = a.shape
    n = b.shape[0] if nt else b.shape[1]
    tm, tn, tk = _matmul_tiles(m, n, k, a.dtype.itemsize, b.dtype.itemsize, jnp.dtype(out_dtype).itemsize, add is not None,
                               128 if tn_a else 16, 16 if tn_a and not nt else 128)
    nk = k // tk
    grid = (m // tm, n // tn, nk)

    def body(a_ref, b_ref, *rest):
        r_ref = None
        if add is not None:
            r_ref, rest = rest[0], rest[1:]
        o_ref = rest[0]
        av = a_ref[...].astype(BF16)
        bv = b_ref[...].astype(BF16)
        part = _dot_tn(av, bv) if tn_a else _dot_nt(av, bv) if nt else _dot(av, bv)

        def finish(r):
            if add is not None:
                r = r + r_ref[...]
            o_ref[...] = r.astype(out_dtype)

        if nk == 1:
            finish(part)
            return
        acc = rest[1]
        kk = pl.program_id(2)

        @pl.when(kk == 0)
        def _():
            acc[...] = part

        @pl.when((kk > 0) & (kk < nk - 1))
        def _():
            acc[...] += part

        @pl.when(kk == nk - 1)
        def _():
            finish(acc[...] + part)

    in_specs = [
        pl.BlockSpec((tk, tm), lambda i, j, kk: (kk, i)) if tn_a else pl.BlockSpec((tm, tk), lambda i, j, kk: (i, kk)),
        pl.BlockSpec((tn, tk), lambda i, j, kk: (j, kk)) if nt else pl.BlockSpec((tk, tn), lambda i, j, kk: (kk, j)),
    ]
    args = [a, b]
    if add is not None:
        in_specs.append(pl.BlockSpec((tm, tn), lambda i, j, kk: (i, j)))
        args.append(add)
    return _call(
        body, name=name, grid=grid, in_specs=in_specs, args=args,
        out_specs=pl.BlockSpec((tm, tn), lambda i, j, kk: (i, j)),
        out_shape=jax.ShapeDtypeStruct((m, n), out_dtype),
        scratch_shapes=[pltpu.VMEM((tm, tn), F32)] if nk > 1 else [],
        semantics=("parallel", "parallel", "arbitrary"), send=send)


def _norm_fwd(x, w_col, name):
    f, t = x.shape
    tt = _tile(t, (512, 256, 128))

    def body(x_ref, w_ref, o_ref):
        xv = x_ref[...]
        r = lax.rsqrt(jnp.mean(xv * xv, axis=0, keepdims=True) + EPS)
        o_ref[...] = (xv * r * w_ref[...]).astype(BF16)

    return pl.pallas_call(
        body,
        name=name,
        grid=(t // tt,),
        in_specs=[pl.BlockSpec((f, tt), lambda i: (0, i)), pl.BlockSpec((f, 1), lambda i: (0, 0))],
        out_specs=pl.BlockSpec((f, tt), lambda i: (0, i)),
        out_shape=jax.ShapeDtypeStruct((f, t), BF16),
        compiler_params=_cparams("parallel"),
    )(x, w_col)


def _norm_bwd(dy, x, w_col, res, name):
    f, t = x.shape
    tt = _tile(t, (512, 256, 128))

    def body(dy_ref, x_ref, w_ref, res_ref, dx_ref, dw_ref):
        @pl.when(pl.program_id(0) == 0)
        def _():
            dw_ref[...] = jnp.zeros_like(dw_ref)

        xv = x_ref[...]
        r = lax.rsqrt(jnp.mean(xv * xv, axis=0, keepdims=True) + EPS)
        xhat = xv * r
        dyv = dy_ref[...]
        dw_ref[...] += _rowsum(dyv * xhat)
        dxhat = dyv * w_ref[...]
        dx_ref[...] = res_ref[...] + r * (dxhat - xhat * jnp.mean(dxhat * xhat, axis=0, keepdims=True))

    blk = pl.BlockSpec((f, tt), lambda i: (0, i))
    col = pl.BlockSpec((f, 1), lambda i: (0, 0))
    return pl.pallas_call(
        body,
        name=name,
        grid=(t // tt,),
        in_specs=[blk, blk, col, blk],
        out_specs=[blk, col],
        out_shape=[jax.ShapeDtypeStruct((f, t), F32), jax.ShapeDtypeStruct((f, 1), F32)],
        compiler_params=_cparams("arbitrary"),
    )(dy, x, w_col, res)


def _final_norm_loss(h, tgt, w_col):
    f, t = h.shape
    tt = _tile(t, (512, 256, 128))

    def body(h_ref, t_ref, w_ref, dh_ref, loss_ref, dw_ref):
        @pl.when(pl.program_id(0) == 0)
        def _():
            dw_ref[...] = jnp.zeros_like(dw_ref)
            loss_ref[...] = jnp.zeros_like(loss_ref)

        xv = h_ref[...]
        r = lax.rsqrt(jnp.mean(xv * xv, axis=0, keepdims=True) + EPS)
        xhat = xv * r
        wv = w_ref[...]
        err = xhat * wv - t_ref[...]
        loss_ref[...] += 0.5 * _rowsum(jnp.mean(err * err, axis=0, keepdims=True))
        dyv = err * (1.0 / f)
        dw_ref[...] += _rowsum(dyv * xhat)
        dxhat = dyv * wv
        dh_ref[...] = r * (dxhat - xhat * jnp.mean(dxhat * xhat, axis=0, keepdims=True))

    blk = pl.BlockSpec((f, tt), lambda i: (0, i))
    col = pl.BlockSpec((f, 1), lambda i: (0, 0))
    one = pl.BlockSpec((1, 1), lambda i: (0, 0))
    return pl.pallas_call(
        body,
        name="final_norm_loss",
        grid=(t // tt,),
        in_specs=[blk, blk, col],
        out_specs=[blk, one, col],
        out_shape=[jax.ShapeDtypeStruct((f, t), F32), jax.ShapeDtypeStruct((1, 1), F32), jax.ShapeDtypeStruct((f, 1), F32)],
        compiler_params=_cparams("arbitrary"),
    )(h, tgt, w_col)


def _attn_mask(n):
    shape = (2 * WINDOW, Q_PER_KV * WINDOW)
    si = lax.broadcasted_iota(jnp.int32, shape, 0)
    qi = lax.broadcasted_iota(jnp.int32, shape, 1) & (WINDOW - 1)
    dist = WINDOW + qi - si
    return (dist >= 0) & (dist < WINDOW) & ((si >= WINDOW) | (n > 0))


def _lane_cat(ref, row0, rows):
    return jnp.concatenate([ref[row0 + i * rows:row0 + (i + 1) * rows, :] for i in range(Q_PER_KV)], axis=1)


def _attn_fwd(proj, sinks):
    t = proj.shape[1]
    nb = t // WINDOW
    scale = HEAD_DIM ** -0.5

    def body(s_ref, q_ref, kc_ref, kp_ref, vc_ref, vp_ref, o_ref, lse_ref):
        n = pl.program_id(0)
        valid = _attn_mask(n)
        for g in range(N_KV_HEADS):
            rows = slice(g * HEAD_DIM, (g + 1) * HEAD_DIM)
            kt = jnp.concatenate([kp_ref[rows, :], kc_ref[rows, :]], axis=1).astype(BF16)
            vt = jnp.concatenate([vp_ref[rows, :], vc_ref[rows, :]], axis=1).astype(BF16)
            qcat = (_lane_cat(q_ref, g * Q_PER_KV * HEAD_DIM, HEAD_DIM) * scale).astype(BF16)
            s = jnp.where(valid, _dot_tn(kt, qcat), NEG)
            sink = jnp.concatenate(
                [jnp.full((1, WINDOW), s_ref[g * Q_PER_KV + i], F32) for i in range(Q_PER_KV)], axis=1)
            m = jnp.maximum(jnp.max(s, axis=0, keepdims=True), sink)
            p = jnp.exp(s - m)
            denom = _colsum(p) + jnp.exp(sink - m)
            probs = (p / denom).astype(BF16)
            out = _dot(vt, probs)
            lse = m + jnp.log(denom)
            for i in range(Q_PER_KV):
                h = g * Q_PER_KV + i
                o_ref[h * HEAD_DIM:(h + 1) * HEAD_DIM, :] = out[:, i * WINDOW:(i + 1) * WINDOW]
                lse_ref[h:h + 1, :] = lse[:, i * WINDOW:(i + 1) * WINDOW]

    kb = OFF_K // KV_DIM
    vb = OFF_V // KV_DIM
    prev = lambda n: jnp.maximum(n - 1, 0)
    return pl.pallas_call(
        body,
        name="attn_fwd",
        grid=(nb,),
        in_specs=[
            pl.BlockSpec(memory_space=pltpu.SMEM),
            pl.BlockSpec((Q_DIM, WINDOW), lambda n: (0, n)),
            pl.BlockSpec((KV_DIM, WINDOW), lambda n: (kb, n)),
            pl.BlockSpec((KV_DIM, WINDOW), lambda n: (kb, prev(n))),
            pl.BlockSpec((KV_DIM, WINDOW), lambda n: (vb, n)),
            pl.BlockSpec((KV_DIM, WINDOW), lambda n: (vb, prev(n))),
        ],
        out_specs=[pl.BlockSpec((Q_DIM, WINDOW), lambda n: (0, n)), pl.BlockSpec((N_Q_HEADS, WINDOW), lambda n: (0, n))],
        out_shape=[jax.ShapeDtypeStruct((Q_DIM, t), F32), jax.ShapeDtypeStruct((N_Q_HEADS, t), F32)],
        compiler_params=_cparams("parallel"),
    )(sinks, proj, proj, proj, proj, proj)


def _attn_bwd(proj, sinks, out, lse, dout, send=None):
    t = proj.shape[1]
    nb = t // WINDOW
    scale = HEAD_DIM ** -0.5

    def body(s_ref, q_ref, kc_ref, kp_ref, vc_ref, vp_ref, o_ref, lse_ref, do_ref,
             dq_ref, dk_ref, dv_ref, ds_ref, dk_carry, dv_carry):
        step = pl.program_id(0)
        n = nb - 1 - step

        @pl.when(step == 0)
        def _():
            dk_carry[...] = jnp.zeros_like(dk_carry)
            dv_carry[...] = jnp.zeros_like(dv_carry)
            ds_ref[...] = jnp.zeros_like(ds_ref)

        valid = _attn_mask(n)
        for g in range(N_KV_HEADS):
            rows = slice(g * HEAD_DIM, (g + 1) * HEAD_DIM)
            q0 = g * Q_PER_KV * HEAD_DIM
            kt = jnp.concatenate([kp_ref[rows, :], kc_ref[rows, :]], axis=1).astype(BF16)
            vt = jnp.concatenate([vp_ref[rows, :], vc_ref[rows, :]], axis=1).astype(BF16)
            qf = _lane_cat(q_ref, q0, HEAD_DIM)
            qcat = qf.astype(BF16)
            ocat = _lane_cat(o_ref, q0, HEAD_DIM)
            docat = _lane_cat(do_ref, q0, HEAD_DIM)
            dob = docat.astype(BF16)
            lse_cat = jnp.concatenate(
                [lse_ref[g * Q_PER_KV + i:g * Q_PER_KV + i + 1, :] for i in range(Q_PER_KV)], axis=1)
            sink = jnp.concatenate(
                [jnp.full((1, WINDOW), s_ref[g * Q_PER_KV + i], F32) for i in range(Q_PER_KV)], axis=1)
            s = jnp.where(valid, _dot_tn(kt, (qf * scale).astype(BF16)), NEG)
            p = jnp.exp(s - lse_cat)
            dp = _dot_tn(vt, dob)
            delta = _colsum(docat * ocat)
            dsc = (p * (dp - delta)).astype(BF16)
            dsink_row = -jnp.exp(sink - lse_cat) * delta
            dq = _dot(kt, dsc) * scale
            dk = _dot_nt(qcat, dsc) * scale
            dv = _dot_nt(dob, p.astype(BF16))
            for i in range(Q_PER_KV):
                h = g * Q_PER_KV + i
                dq_ref[h * HEAD_DIM:(h + 1) * HEAD_DIM, :] = dq[:, i * WINDOW:(i + 1) * WINDOW].astype(BF16)
                ds_ref[h:h + 1, :] += _rowsum(dsink_row[:, i * WINDOW:(i + 1) * WINDOW])
            dk_ref[rows, :] = (dk[:, WINDOW:] + dk_carry[rows, :]).astype(BF16)
            dv_ref[rows, :] = (dv[:, WINDOW:] + dv_carry[rows, :]).astype(BF16)
            dk_carry[rows, :] = dk[:, :WINDOW]
            dv_carry[rows, :] = dv[:, :WINDOW]

    kb = OFF_K // KV_DIM
    vb = OFF_V // KV_DIM
    cur = lambda i: nb - 1 - i
    prev = lambda i: jnp.maximum(nb - 2 - i, 0)
    qspec = pl.BlockSpec((Q_DIM, WINDOW), lambda i: (0, cur(i)))
    kvspec = pl.BlockSpec((KV_DIM, WINDOW), lambda i: (0, cur(i)))
    return _call(
        body,
        name="attn_bwd",
        grid=(nb,),
        in_specs=[
            pl.BlockSpec(memory_space=pltpu.SMEM),
            qspec,
            pl.BlockSpec((KV_DIM, WINDOW), lambda i: (kb, cur(i))),
            pl.BlockSpec((KV_DIM, WINDOW), lambda i: (kb, prev(i))),
            pl.BlockSpec((KV_DIM, WINDOW), lambda i: (vb, cur(i))),
            pl.BlockSpec((KV_DIM, WINDOW), lambda i: (vb, prev(i))),
            qspec,
            pl.BlockSpec((N_Q_HEADS, WINDOW), lambda i: (0, cur(i))),
            qspec,
        ],
        out_specs=[qspec, kvspec, kvspec, pl.BlockSpec((N_Q_HEADS, 1), lambda i: (0, 0))],
        out_shape=[
            jax.ShapeDtypeStruct((Q_DIM, t), BF16),
            jax.ShapeDtypeStruct((KV_DIM, t), BF16),
            jax.ShapeDtypeStruct((KV_DIM, t), BF16),
            jax.ShapeDtypeStruct((N_Q_HEADS, 1), F32),
        ],
        scratch_shapes=[pltpu.VMEM((KV_DIM, WINDOW), F32), pltpu.VMEM((KV_DIM, WINDOW), F32)],
        semantics=("arbitrary",), args=(sinks, proj, proj, proj, proj, proj, out, lse, dout), send=send)


CONV_ROWS = 256


def _conv_silu_fwd(proj, w_col, b_col):
    t = proj.shape[1]
    r0 = OFF_X // CONV_ROWS

    def body(x_ref, w_ref, b_ref, o_ref):
        def strip(rows):
            y, _ = _causal_conv(x_ref[rows, :], w_ref[rows, :], b_ref[rows, :])
            o_ref[rows, :] = y * _sigmoid(y)

        strip(slice(None))

    return pl.pallas_call(
        body,
        name="ssd_conv_fwd",
        grid=(XBC_DIM // CONV_ROWS,),
        in_specs=[
            pl.BlockSpec((CONV_ROWS, t), lambda i: (r0 + i, 0)),
            pl.BlockSpec((CONV_ROWS, SSD_CONV), lambda i: (i, 0)),
            pl.BlockSpec((CONV_ROWS, 1), lambda i: (i, 0)),
        ],
        out_specs=pl.BlockSpec((CONV_ROWS, t), lambda i: (i, 0)),
        out_shape=jax.ShapeDtypeStruct((XBC_DIM, t), F32),
        compiler_params=_cparams("parallel"),
    )(proj, w_col, b_col)


def _conv_silu_bwd(proj, w_col, b_col, dout, row0, name):
    t = proj.shape[1]
    nrows = dout.shape[0]
    p0 = (OFF_X + row0) // CONV_ROWS
    c0 = row0 // CONV_ROWS

    def body(x_ref, w_ref, b_ref, do_ref, dx_ref, dwb_ref):
        def strip(rows):
            xv = x_ref[rows, :]
            wv = w_ref[rows, :]
            y, shifted = _causal_conv(xv, wv, b_ref[rows, :])
            sg = _sigmoid(y)
            dy = do_ref[rows, :] * (sg * (1.0 + y * (1.0 - sg)))
            dx, dwb_ref[rows, :] = _causal_conv_bwd(dy, xv, shifted, wv)
            dx_ref[rows, :] = dx.astype(BF16)

        strip(slice(None))

    return pl.pallas_call(
        body,
        name=name,
        grid=(nrows // CONV_ROWS,),
        in_specs=[
            pl.BlockSpec((CONV_ROWS, t), lambda i: (p0 + i, 0)),
            pl.BlockSpec((CONV_ROWS, SSD_CONV), lambda i: (c0 + i, 0)),
            pl.BlockSpec((CONV_ROWS, 1), lambda i: (c0 + i, 0)),
            pl.BlockSpec((CONV_ROWS, t), lambda i: (i, 0)),
        ],
        out_specs=[pl.BlockSpec((CONV_ROWS, t), lambda i: (i, 0)), pl.BlockSpec((CONV_ROWS, 128), lambda i: (i, 0))],
        out_shape=[jax.ShapeDtypeStruct((nrows, t), BF16), jax.ShapeDtypeStruct((nrows, 128), F32)],
        compiler_params=_cparams("parallel"),
    )(proj, w_col, b_col, dout)


GROUP_ROWS = HEADS_PER_GROUP * SSD_HEAD_DIM


def _ssd_specs(order):
    xb = D_INNER // BC_DIM
    dtb = OFF_DT // N_SSD_HEADS
    col = pl.BlockSpec((N_SSD_HEADS, 1), lambda c: (0, 0))
    return [
        pl.BlockSpec((D_INNER, CHUNK), lambda c: (0, order(c))),
        pl.BlockSpec((BC_DIM, CHUNK), lambda c: (xb, order(c))),
        pl.BlockSpec((BC_DIM, CHUNK), lambda c: (xb + 1, order(c))),
        pl.BlockSpec((N_SSD_HEADS, CHUNK), lambda c: (dtb, order(c))),
        col, col, col,
    ]


def _ssd_common(dt_ref, dtb_ref, alog_ref):
    z = dt_ref[...] + dtb_ref[...]
    dt = _softplus(z)
    a_neg = -jnp.exp(alog_ref[...])
    d_a = dt * a_neg
    row = lax.broadcasted_iota(jnp.int32, (CHUNK, CHUNK), 0)
    colm = lax.broadcasted_iota(jnp.int32, (CHUNK, CHUNK), 1)
    upper = (row <= colm).astype(F32)
    a_cs = jnp.dot(d_a, upper, precision=HIGHEST, preferred_element_type=F32)
    a_last = _rowsum(d_a)
    return z, dt, a_neg, a_cs, a_last, row >= colm, row == colm


def _decay(a_row, causal):
    a_s = jnp.broadcast_to(a_row, (CHUNK, CHUNK))
    seg = a_s.T - a_s
    return jnp.where(causal, jnp.exp(jnp.where(causal, seg, 0.0)), 0.0)


def _ssd_fwd(xbc, proj, dtb_col, alog_col, dsk_col):
    t = xbc.shape[1]
    nc = t // CHUNK

    def body(xs_ref, b_ref, c_ref, dt_ref, dtb_ref, alog_ref, dsk_ref, y_ref, hst_ref, h_scr):
        @pl.when(pl.program_id(0) == 0)
        def _():
            h_scr[...] = jnp.zeros_like(h_scr)

        _, dt, _, a_cs, a_last, causal, _ = _ssd_common(dt_ref, dtb_ref, alog_ref)
        hst_ref[0] = h_scr[...]
        dsk = dsk_ref[...]
        for g in range(N_SSD_GROUPS):
            grows = slice(g * D_STATE, (g + 1) * D_STATE)
            bb = b_ref[grows, :].astype(BF16)
            cb_ = c_ref[grows, :].astype(BF16)
            cb = _dot_tn(cb_, bb)
            for j in range(g * HEADS_PER_GROUP, (g + 1) * HEADS_PER_GROUP):
                rows = slice(j * SSD_HEAD_DIM, (j + 1) * SSD_HEAD_DIM)
                a = a_cs[j:j + 1, :]
                m = (cb * _decay(a, causal)).astype(BF16)
                xs = xs_ref[rows, :]
                xc = xs * dt[j:j + 1, :]
                hj = h_scr[rows, :]
                y = _dot_nt(xc.astype(BF16), m) + _dot(hj.astype(BF16), cb_) * jnp.exp(a) + dsk[j:j + 1, :] * xs
                y_ref[rows, :] = y
                al = a_last[j:j + 1, :]
                w = jnp.exp(al - a)
                h_scr[rows, :] = jnp.exp(al) * hj + _dot_nt((xc * w).astype(BF16), bb)

    return pl.pallas_call(
        body,
        name="ssd_fwd",
        grid=(nc,),
        in_specs=_ssd_specs(lambda c: c),
        out_specs=[
            pl.BlockSpec((D_INNER, CHUNK), lambda c: (0, c)),
            pl.BlockSpec((1, D_INNER, D_STATE), lambda c: (c, 0, 0)),
        ],
        out_shape=[
            jax.ShapeDtypeStruct((D_INNER, t), F32),
            jax.ShapeDtypeStruct((nc, D_INNER, D_STATE), F32),
        ],
        scratch_shapes=[pltpu.VMEM((D_INNER, D_STATE), F32)],
        compiler_params=_cparams("arbitrary"),
    )(xbc, xbc, xbc, proj, dtb_col, alog_col, dsk_col)


def _ssd_bwd(xbc, proj, dtb_col, alog_col, dsk_col, hst, dy):
    t = xbc.shape[1]
    nc = t // CHUNK
    rev = lambda c: nc - 1 - c

    def body(xs_ref, b_ref, c_ref, dt_ref, dtb_ref, alog_ref, dsk_ref, hst_ref, dy_ref,
             dxs_ref, db_ref, dc_ref, ddt_ref, dalog_ref, ddsk_ref, ddtb_ref, dh_scr, da_scr, ddt_scr, dd_scr):
        @pl.when(pl.program_id(0) == 0)
        def _():
            dh_scr[...] = jnp.zeros_like(dh_scr)
            dalog_ref[...] = jnp.zeros_like(dalog_ref)
            ddsk_ref[...] = jnp.zeros_like(ddsk_ref)
            ddtb_ref[...] = jnp.zeros_like(ddtb_ref)

        z, dt, a_neg, a_cs, a_last, causal, eye = _ssd_common(dt_ref, dtb_ref, alog_ref)
        dsk = dsk_ref[...]
        last_lane = lax.broadcasted_iota(jnp.int32, (1, CHUNK), 1) == CHUNK - 1
        for g in range(N_SSD_GROUPS):
            grows = slice(g * D_STATE, (g + 1) * D_STATE)
            bb = b_ref[grows, :].astype(BF16)
            cb_ = c_ref[grows, :].astype(BF16)
            cb = _dot_tn(cb_, bb)
            dcb = jnp.zeros((CHUNK, CHUNK), F32)
            dc_acc = jnp.zeros((D_STATE, CHUNK), F32)
            db_acc = jnp.zeros((D_STATE, CHUNK), F32)
            for j in range(g * HEADS_PER_GROUP, (g + 1) * HEADS_PER_GROUP):
                rows = slice(j * SSD_HEAD_DIM, (j + 1) * SSD_HEAD_DIM)
                a = a_cs[j:j + 1, :]
                al = a_last[j:j + 1, :]
                lam = _decay(a, causal)
                mf = cb * lam
                xs = xs_ref[rows, :]
                dtj = dt[j:j + 1, :]
                xc = xs * dtj
                w = jnp.exp(al - a)
                e = jnp.exp(a)
                gam = jnp.exp(al)
                hj = hst_ref[0, rows, :]
                hjb = hj.astype(BF16)
                dyv = dy_ref[rows, :]
                dyb = dyv.astype(BF16)
                dd_scr[j:j + 1, :] = _colsum(dyv * xs)
                gb = (dyv * e).astype(BF16)
                dh_in = _dot_nt(gb, cb_)
                dc_acc = dc_acc + _dot_tn(hjb, gb)
                yoff = _dot(hjb, cb_) * e
                da = _colsum(dyv * yoff)
                dm = _dot_tn(dyb, xc.astype(BF16))
                dxc = _dot(dyb, mf.astype(BF16))
                dcb = dcb + dm * lam
                nmat = dm * mf
                rs = jnp.broadcast_to(_rowsum(nmat), (CHUNK, CHUNK))
                da = da + _colsum(jnp.where(eye, rs, 0.0)) - _colsum(nmat)
                ds = dh_scr[rows, :]
                dsb = ds.astype(BF16)
                t1 = _dot(dsb, bb)
                xcw = xc * w
                dxc = dxc + w * t1
                dww = _colsum(xcw * t1)
                da_l = _rowsum(dww) + _rowsum(_colsum(ds * hj)) * gam
                da = da - dww + jnp.where(last_lane, da_l, 0.0)
                db_acc = db_acc + _dot_tn(dsb, xcw.astype(BF16))
                dh_scr[rows, :] = gam * ds + dh_in
                dxs_ref[rows, :] = dsk[j:j + 1, :] * dyv + dxc * dtj
                da_scr[j:j + 1, :] = da
                ddt_scr[j:j + 1, :] = _colsum(dxc * xs)
            dcbb = dcb.astype(BF16)
            dc_ref[grows, :] = dc_acc + _dot_nt(bb, dcbb)
            db_ref[grows, :] = db_acc + _dot(cb_, dcbb)
        dda = jnp.dot(da_scr[...], causal.astype(F32), precision=HIGHEST, preferred_element_type=F32)
        ddt = ddt_scr[...] + dda * a_neg
        ddt_raw = ddt * _sigmoid(z)
        ddt_ref[...] = ddt_raw
        ddtb_ref[...] += _rowsum(ddt_raw)
        dalog_ref[...] += _rowsum(dda * dt) * a_neg
        ddsk_ref[...] += _rowsum(dd_scr[...])

    col = pl.BlockSpec((N_SSD_HEADS, 1), lambda c: (0, 0))
    bc = pl.BlockSpec((BC_DIM, CHUNK), lambda c: (0, rev(c)))
    xs_spec = pl.BlockSpec((D_INNER, CHUNK), lambda c: (0, rev(c)))
    small = pltpu.VMEM((N_SSD_HEADS, CHUNK), F32)
    return pl.pallas_call(
        body,
        name="ssd_bwd",
        grid=(nc,),
        in_specs=_ssd_specs(rev) + [pl.BlockSpec((1, D_INNER, D_STATE), lambda c: (rev(c), 0, 0)), xs_spec],
        out_specs=[xs_spec, bc, bc, pl.BlockSpec((N_SSD_HEADS, CHUNK), lambda c: (0, rev(c))), col, col, col],
        out_shape=[
            jax.ShapeDtypeStruct((D_INNER, t), F32),
            jax.ShapeDtypeStruct((BC_DIM, t), F32),
            jax.ShapeDtypeStruct((BC_DIM, t), F32),
            jax.ShapeDtypeStruct((N_SSD_HEADS, t), F32),
            jax.ShapeDtypeStruct((N_SSD_HEADS, 1), F32),
            jax.ShapeDtypeStruct((N_SSD_HEADS, 1), F32),
            jax.ShapeDtypeStruct((N_SSD_HEADS, 1), F32),
        ],
        scratch_shapes=[pltpu.VMEM((D_INNER, D_STATE), F32), small, small, small],
        compiler_params=_cparams("arbitrary"),
    )(xbc, xbc, xbc, proj, dtb_col, alog_col, dsk_col, hst, dy)


GN_ROWS = D_INNER // N_SSD_GROUPS


def _gnorm_fwd(y, proj, w_col):
    t = y.shape[1]
    tt = _tile(t, (512, 256, 128))
    z0 = OFF_Z // GN_ROWS

    def body(y_ref, z_ref, w_ref, o_ref):
        zv = z_ref[...]
        u = y_ref[...] * (zv * _sigmoid(zv))
        r = lax.rsqrt(jnp.mean(u * u, axis=0, keepdims=True) + EPS)
        o_ref[...] = (u * r * w_ref[...]).astype(BF16)

    blk = pl.BlockSpec((GN_ROWS, tt), lambda g, i: (g, i))
    return pl.pallas_call(
        body,
        name="gnorm_fwd",
        grid=(N_SSD_GROUPS, t // tt),
        in_specs=[blk, pl.BlockSpec((GN_ROWS, tt), lambda g, i: (z0 + g, i)), pl.BlockSpec((GN_ROWS, 1), lambda g, i: (g, 0))],
        out_specs=blk,
        out_shape=jax.ShapeDtypeStruct((D_INNER, t), BF16),
        compiler_params=_cparams("parallel", "parallel"),
    )(y, proj, w_col)


def _gnorm_bwd(dout, y, proj, w_col, send=None):
    t = y.shape[1]
    tt = _tile(t, (512, 256, 128))
    z0 = OFF_Z // GN_ROWS

    def body(do_ref, y_ref, z_ref, w_ref, dy_ref, dz_ref, dw_ref):
        @pl.when(pl.program_id(1) == 0)
        def _():
            dw_ref[...] = jnp.zeros_like(dw_ref)

        zv = z_ref[...]
        yv = y_ref[...]
        sg = _sigmoid(zv)
        sz = zv * sg
        u = yv * sz
        r = lax.rsqrt(jnp.mean(u * u, axis=0, keepdims=True) + EPS)
        xhat = u * r
        dov = do_ref[...]
        dw_ref[...] += _rowsum(dov * xhat)
        dxhat = dov * w_ref[...]
        du = r * (dxhat - xhat * jnp.mean(dxhat * xhat, axis=0, keepdims=True))
        dy_ref[...] = du * sz
        dz_ref[...] = (du * yv * (sg * (1.0 + zv * (1.0 - sg)))).astype(BF16)

    blk = pl.BlockSpec((GN_ROWS, tt), lambda g, i: (g, i))
    col = pl.BlockSpec((GN_ROWS, 1), lambda g, i: (g, 0))
    return _call(
        body,
        name="gnorm_bwd",
        grid=(N_SSD_GROUPS, t // tt),
        in_specs=[blk, blk, pl.BlockSpec((GN_ROWS, tt), lambda g, i: (z0 + g, i)), col],
        out_specs=[blk, blk, col],
        out_shape=[jax.ShapeDtypeStruct((D_INNER, t), F32), jax.ShapeDtypeStruct((D_INNER, t), BF16),
                   jax.ShapeDtypeStruct((D_INNER, 1), F32)],
        semantics=("parallel", "arbitrary"), args=(dout, y, proj, w_col), send=send)


GATE_ROWS = 128


def _gate_specs(t):
    nr = D_MODEL // GATE_ROWS
    blk = pl.BlockSpec((GATE_ROWS, t), lambda r: (r, 0))
    rows_from = lambda first: pl.BlockSpec(
        (pl.Element(GATE_ROWS), pl.Element(t)), lambda r: (pl.multiple_of(first + GATE_ROWS * r, N_SSD_HEADS), 0))
    return blk, [
        rows_from(OFF_GA),
        rows_from(OFF_GS),
        pl.BlockSpec((GATE_ROWS, 1), lambda r: (r, 0)),
        pl.BlockSpec((GATE_ROWS, 1), lambda r: (nr + r, 0)),
        blk, blk,
    ]


def _gate_fwd(proj, b_col, attn, ssd):
    t = proj.shape[1]
    blk, specs = _gate_specs(t)

    def body(ga_ref, gs_ref, ba_ref, bs_ref, a_ref, s_ref, o_ref):
        o_ref[...] = (_sigmoid(ga_ref[...] + ba_ref[...]) * a_ref[...]
                      + _sigmoid(gs_ref[...] + bs_ref[...]) * s_ref[...]).astype(BF16)

    return pl.pallas_call(
        body,
        name="gate_fwd",
        grid=(D_MODEL // GATE_ROWS,),
        in_specs=specs,
        out_specs=blk,
        out_shape=jax.ShapeDtypeStruct((D_MODEL, t), BF16),
        compiler_params=_cparams("parallel"),
    )(proj, proj, b_col, b_col, attn, ssd)


def _gate_bwd(proj, b_col, attn, ssd, dmix, send=None):
    t = proj.shape[1]
    blk, specs = _gate_specs(t)

    def body(ga_ref, gs_ref, ba_ref, bs_ref, a_ref, s_ref, dm_ref, da_ref, dso_ref, dga_ref, dgs_ref, dba_ref, dbs_ref):
        dm = dm_ref[...]
        sa = _sigmoid(ga_ref[...] + ba_ref[...])
        ss = _sigmoid(gs_ref[...] + bs_ref[...])
        da_ref[...] = (dm * sa).astype(BF16)
        dso_ref[...] = (dm * ss).astype(BF16)
        dga = dm * a_ref[...] * sa * (1.0 - sa)
        dgs = dm * s_ref[...] * ss * (1.0 - ss)
        dga_ref[...] = dga.astype(BF16)
        dgs_ref[...] = dgs.astype(BF16)
        dba_ref[...] = _rowsum(dga)
        dbs_ref[...] = _rowsum(dgs)

    col = pl.BlockSpec((GATE_ROWS, 1), lambda r: (r, 0))
    act = jax.ShapeDtypeStruct((D_MODEL, t), BF16)
    bias = jax.ShapeDtypeStruct((D_MODEL, 1), F32)
    return _call(
        body,
        name="gate_bwd",
        grid=(D_MODEL // GATE_ROWS,),
        in_specs=specs + [blk],
        out_specs=[blk, blk, blk, blk, col, col],
        out_shape=[act, act, act, act, bias, bias],
        semantics=("parallel",), args=(proj, proj, b_col, b_col, attn, ssd, dmix), send=send)


FFN_ROWS = 256


def _ffn_fwd(u0, w_col, b_col):
    t = u0.shape[2]

    def body(u_ref, w_ref, b_ref, o_ref):
        def strip(rows):
            val, _ = _causal_conv(u_ref[0, rows, :], w_ref[0, rows, :], b_ref[0, rows, :])
            gt, _ = _causal_conv(u_ref[1, rows, :], w_ref[1, rows, :], b_ref[1, rows, :])
            o_ref[rows, :] = (gt * _sigmoid(gt) * val).astype(BF16)

        strip(slice(None))

    return pl.pallas_call(
        body,
        name="ffn_fwd",
        grid=(D_FF // FFN_ROWS,),
        in_specs=[
            pl.BlockSpec((2, FFN_ROWS, t), lambda i: (0, i, 0)),
            pl.BlockSpec((2, FFN_ROWS, FFN_CONV), lambda i: (0, i, 0)),
            pl.BlockSpec((2, FFN_ROWS, 1), lambda i: (0, i, 0)),
        ],
        out_specs=pl.BlockSpec((FFN_ROWS, t), lambda i: (i, 0)),
        out_shape=jax.ShapeDtypeStruct((D_FF, t), BF16),
        compiler_params=_cparams("parallel"),
    )(u0, w_col, b_col)


def _ffn_bwd(u0, w_col, b_col, dg, send=None):
    t = u0.shape[2]

    def body(u_ref, w_ref, b_ref, dg_ref, du_ref, dwb_ref):
        def strip(rows):
            xval, wval = u_ref[0, rows, :], w_ref[0, rows, :]
            xgt, wgt = u_ref[1, rows, :], w_ref[1, rows, :]
            val, sh_val = _causal_conv(xval, wval, b_ref[0, rows, :])
            gt, sh_gt = _causal_conv(xgt, wgt, b_ref[1, rows, :])
            sg = _sigmoid(gt)
            dgv = dg_ref[rows, :]
            dval = dgv * (gt * sg)
            dgt = dgv * val * (sg * (1.0 + gt * (1.0 - sg)))
            dx, dwb_ref[0, rows, :] = _causal_conv_bwd(dval, xval, sh_val, wval)
            du_ref[0, rows, :] = dx.astype(BF16)
            dx, dwb_ref[1, rows, :] = _causal_conv_bwd(dgt, xgt, sh_gt, wgt)
            du_ref[1, rows, :] = dx.astype(BF16)

        strip(slice(None))

    return _call(
        body,
        name="ffn_bwd",
        grid=(D_FF // FFN_ROWS,),
        in_specs=[
            pl.BlockSpec((2, FFN_ROWS, t), lambda i: (0, i, 0)),
            pl.BlockSpec((2, FFN_ROWS, FFN_CONV), lambda i: (0, i, 0)),
            pl.BlockSpec((2, FFN_ROWS, 1), lambda i: (0, i, 0)),
            pl.BlockSpec((FFN_ROWS, t), lambda i: (i, 0)),
        ],
        out_specs=[pl.BlockSpec((2, FFN_ROWS, t), lambda i: (0, i, 0)), pl.BlockSpec((2, FFN_ROWS, 128), lambda i: (0, i, 0))],
        out_shape=[jax.ShapeDtypeStruct((2, D_FF, t), BF16), jax.ShapeDtypeStruct((2, D_FF, 128), F32)],
        semantics=("parallel",), args=(u0, w_col, b_col, dg), send=send)


def _adamw_math(w, g, m, v):
    m = ADAM_B1 * m + (1.0 - ADAM_B1) * g
    v = ADAM_B2 * v + (1.0 - ADAM_B2) * (g * g)
    m_hat = m / (1.0 - ADAM_B1 ** ADAM_STEP)
    v_hat = v / (1.0 - ADAM_B2 ** ADAM_STEP)
    delta = -ADAM_LR * (m_hat / (jnp.sqrt(v_hat) + ADAM_EPS) + ADAM_WD * w)
    return delta, m, v


def _adamw_sharded(parts, w, m, v, name):
    r, c = w.shape[0], w.shape[-1]
    tc = _tile(c, (256, 128))
    blk_shape = (r, tc) if w.ndim == 2 else (r, 1, tc)
    slots = parts.shape[0]

    def body(p_ref, w_ref, m_ref, v_ref, g_ref, d_ref, nm_ref, nv_ref):
        g = p_ref[0].astype(F32)
        for s in range(1, slots):
            g = g + p_ref[s].astype(F32)
        flat = lambda ref: ref[...].reshape(r, tc)
        d, nm, nv = _adamw_math(flat(w_ref), g, flat(m_ref), flat(v_ref))
        for ref, val in ((g_ref, g), (d_ref, d), (nm_ref, nm), (nv_ref, nv)):
            ref[...] = val.reshape(blk_shape)

    blk = pl.BlockSpec(blk_shape, (lambda i: (0, i)) if w.ndim == 2 else (lambda i: (0, 0, i)))
    out = jax.ShapeDtypeStruct(w.shape, F32)
    return pl.pallas_call(
        body,
        name=name,
        grid=(c // tc,),
        in_specs=[pl.BlockSpec((slots, r, tc), lambda i: (0, 0, i)), blk, blk, blk],
        out_specs=[blk, blk, blk, blk],
        out_shape=[out, out, out, out],
        compiler_params=_cparams("parallel"),
    )(parts, w, m, v)


def _lane_offsets(sizes):
    offsets, pos = [], 0
    for n in sizes:
        offsets.append(pos)
        pos += -(-n // 128) * 128
    return offsets, pos


def _pack_row(parts):
    rows = [p.reshape(1, -1).astype(F32) for p in parts]
    return jnp.concatenate([jnp.pad(r, ((0, 0), (0, -r.shape[1] % 128))) for r in rows], axis=1)


def _small_update(parts, me, full_sizes, ws, ms, vs):
    n = len(ws)
    offsets, _ = _lane_offsets([1] + list(full_sizes))

    def body(me_ref, p_ref, *refs):
        w_refs, m_refs, v_refs = refs[:n], refs[n:2 * n], refs[2 * n:3 * n]
        scalar_ref, out_refs = refs[3 * n], refs[3 * n + 1:]
        tot = p_ref[0]
        for s in range(1, N_DEV):
            tot = tot + p_ref[s]
        scalar_ref[...] = tot[:, 0:1]
        for k in range(n):
            g_ref, d_ref, nm_ref, nv_ref = out_refs[4 * k:4 * k + 4]
            taps, cols = w_refs[k].shape
            if taps == 1:
                g_ref[...] = tot[:, offsets[k + 1]:offsets[k + 1] + cols]
            else:
                full = full_sizes[k] // taps
                for tap in range(taps):
                    mine = jnp.zeros((1, cols), F32)
                    for d in range(N_DEV):
                        lo = offsets[k + 1] + tap * full + d * cols
                        mine = jnp.where(me_ref[0] == d, tot[:, lo:lo + cols], mine)
                    g_ref[tap:tap + 1, :] = mine
            d_ref[...], nm_ref[...], nv_ref[...] = _adamw_math(w_refs[k][...], g_ref[...], m_refs[k][...], v_refs[k][...])

    vmem = pl.BlockSpec(memory_space=pltpu.VMEM)
    out_shape = [jax.ShapeDtypeStruct((1, 1), F32)]
    for wk in ws:
        out_shape += [jax.ShapeDtypeStruct(wk.shape, F32)] * 4
    res = pl.pallas_call(
        body,
        name="small_update",
        in_specs=[pl.BlockSpec(memory_space=pltpu.SMEM)] + [vmem] * (1 + 3 * n),
        out_specs=[vmem] * len(out_shape),
        out_shape=out_shape,
    )(me, parts, *ws, *ms, *vs)
    return res[0], [res[1 + 4 * k:5 + 4 * k] for k in range(n)]


ANY = pl.BlockSpec(memory_space=pl.ANY)
FLIPS = [(k >> 2 & 1, k >> 1 & 1, k & 1) for k in range(1, N_DEV)]


def _place():
    return lax.axis_index("x"), lax.axis_index("y"), lax.axis_index("c")


HBM = pl.BlockSpec(memory_space=pltpu.HBM)
SEM = pl.BlockSpec(memory_space=pltpu.SEMAPHORE)
EFFECT = pltpu.SideEffectType.DATAFLOW_SIDE_EFFECTING


def _peer_copy(gather, src_ref, land_ref, send_sems, recv_sems, k, sending):
    x, y, c = _place()
    fx, fy, fc = FLIPS[k]
    me = 4 * x + 2 * y + c
    peer = 4 * (x ^ fx) + 2 * (y ^ fy) + (c ^ fc)
    return pltpu.make_async_remote_copy(
        src_ref=src_ref if gather else src_ref.at[peer],
        dst_ref=land_ref.at[me if sending else peer],
        send_sem=send_sems.at[k], recv_sem=recv_sems.at[k],
        device_id=(x ^ fx, y ^ fy, c ^ fc), device_id_type=MESH)


SIBLING = 0
OTHER_CHIPS = (1, 3, 5)


def _gather_start(srcs, name, via_sibling):
    n = len(srcs)
    lands = [lax.empty((N_DEV,) + s.shape, s.dtype) for s in srcs]

    def body(*refs):
        src_refs, land_refs = refs[:n], refs[n:2 * n]
        send, recv = refs[2 * n:3 * n], refs[3 * n:4 * n]
        for i in range(n):
            for k in (SIBLING,) + OTHER_CHIPS if via_sibling else range(N_DEV - 1):
                _peer_copy(True, src_refs[i], land_refs[i], send[i], recv[i], k, True).start()

    sem = pltpu.SemaphoreType.DMA((N_DEV - 1,))
    hbm = lambda a: pltpu.HBM(a.shape, a.dtype)
    res = pl.pallas_call(
        body,
        name=name,
        in_specs=[HBM] * (2 * n),
        out_specs=[SEM] * (2 * n) + [HBM] * (2 * n),
        out_shape=[sem] * (2 * n) + [hbm(s) for s in srcs] + [hbm(a) for a in lands],
        input_output_aliases={i: 2 * n + i for i in range(2 * n)},
        compiler_params=pltpu.CompilerParams(has_side_effects=EFFECT),
    )(*[pltpu.with_memory_space_constraint(a, pltpu.HBM) for a in list(srcs) + lands])
    return res[:n], res[n:2 * n], res[2 * n:3 * n], res[3 * n:4 * n]


def _exchange_wait(send_sems, recv_sems, src, land, after, gather, name):
    def body(src_ref, land_ref, send_ref, recv_ref, after_ref, src_out, land_out):
        for k in range(N_DEV - 1):
            cp = _peer_copy(gather, src_ref, land_ref, send_ref, recv_ref, k, False)
            cp.wait_send()
            cp.wait_recv()

    hbm = lambda a: pltpu.HBM(a.shape, a.dtype)
    return pl.pallas_call(
        body,
        name=name,
        in_specs=[HBM, HBM, SEM, SEM, ANY],
        out_specs=[HBM, HBM],
        out_shape=[hbm(src), hbm(land)],
        input_output_aliases={0: 0, 1: 1},
        compiler_params=pltpu.CompilerParams(has_side_effects=EFFECT),
    )(src, land, send_sems, recv_sems, after)


def _own_slot(src, land, me, gather):
    own = src[None] if gather else lax.dynamic_slice_in_dim(src, me, 1, axis=0)
    return lax.dynamic_update_slice_in_dim(land, own, me, axis=0)


def _forwarded_copy(land_ref, send_sems, recv_sems, j, sending):
    x, y, c = _place()
    fx, fy, _ = FLIPS[OTHER_CHIPS[j]]
    slot = 4 * (x ^ fx) + 2 * (y ^ fy) + (c if sending else 1 - c)
    return pltpu.make_async_remote_copy(
        src_ref=land_ref.at[slot], dst_ref=land_ref.at[slot], send_sem=send_sems.at[j], recv_sem=recv_sems.at[j],
        device_id=(x, y, 1 - c), device_id_type=MESH)


def _gather_forward(send_sems, recv_sems, srcs, lands, after, name):
    n = len(srcs)

    def body(*refs):
        src_refs, land_refs = refs[:n], refs[n:2 * n]
        send, recv = refs[2 * n:3 * n], refs[3 * n:4 * n]
        fwd_send, fwd_recv = refs[4 * n + 1:5 * n + 1], refs[5 * n + 1:6 * n + 1]
        for i in range(n):
            for j, k in enumerate(OTHER_CHIPS):
                _peer_copy(True, src_refs[i], land_refs[i], send[i], recv[i], k, False).wait_recv()
                _forwarded_copy(land_refs[i], fwd_send[i], fwd_recv[i], j, True).start()

    sem = pltpu.SemaphoreType.DMA((len(OTHER_CHIPS),))
    hbm = lambda a: pltpu.HBM(a.shape, a.dtype)
    res = pl.pallas_call(
        body,
        name=name,
        in_specs=[HBM] * (2 * n) + [SEM] * (2 * n) + [ANY],
        out_specs=[SEM] * (2 * n) + [HBM] * (2 * n),
        out_shape=[sem] * (2 * n) + [hbm(a) for a in srcs] + [hbm(a) for a in lands],
        input_output_aliases={i: 2 * n + i for i in range(2 * n)},
        compiler_params=pltpu.CompilerParams(has_side_effects=EFFECT),
    )(*srcs, *lands, *send_sems, *recv_sems, after)
    return res[:n], res[n:2 * n], res[2 * n:3 * n], res[3 * n:4 * n]


def _gather_wait_forwarded(send_sems, recv_sems, fwd_send, fwd_recv, src, land, after, name):
    def body(src_ref, land_ref, send_ref, recv_ref, fwd_send_ref, fwd_recv_ref, after_ref, src_out, land_out):
        for k in (SIBLING,) + OTHER_CHIPS:
            _peer_copy(True, src_ref, land_ref, send_ref, recv_ref, k, False).wait_send()
        _peer_copy(True, src_ref, land_ref, send_ref, recv_ref, SIBLING, False).wait_recv()
        for j in range(len(OTHER_CHIPS)):
            _forwarded_copy(land_ref, fwd_send_ref, fwd_recv_ref, j, True).wait_send()
            _forwarded_copy(land_ref, fwd_send_ref, fwd_recv_ref, j, False).wait_recv()

    hbm = lambda a: pltpu.HBM(a.shape, a.dtype)
    return pl.pallas_call(
        body,
        name=name,
        in_specs=[HBM, HBM, SEM, SEM, SEM, SEM, ANY],
        out_specs=[HBM, HBM],
        out_shape=[hbm(src), hbm(land)],
        input_output_aliases={0: 0, 1: 1},
        compiler_params=pltpu.CompilerParams(has_side_effects=EFFECT),
    )(src, land, send_sems, recv_sems, fwd_send, fwd_recv, after)


N_CHIPS = N_DEV // 2


def _pair_exchange(by_core, name):
    def copy(src_ref, land_ref, send_sems, recv_sems, q):
        x, y, c = _place()
        return pltpu.make_async_remote_copy(
            src_ref=src_ref.at[q, 1 - c], dst_ref=land_ref.at[q], send_sem=send_sems.at[q], recv_sem=recv_sems.at[q],
            device_id=(x, y, 1 - c), device_id_type=MESH)

    def start(src_ref, land_ref, send_sems, recv_sems, src_out, land_out):
        for q in range(N_CHIPS):
            copy(src_ref, land_ref, send_sems, recv_sems, q).start()

    def wait(src_ref, land_ref, send_sems, recv_sems, src_out, land_out):
        for q in range(N_CHIPS):
            cp = copy(src_ref, land_ref, send_sems, recv_sems, q)
            cp.wait_send()
            cp.wait_recv()

    sem = pltpu.SemaphoreType.DMA((N_CHIPS,))
    hbm_src = pltpu.HBM(by_core.shape, by_core.dtype)
    hbm_land = pltpu.HBM(by_core.shape[:1] + by_core.shape[2:], by_core.dtype)
    params = pltpu.CompilerParams(has_side_effects=EFFECT)
    send_sems, recv_sems, src, land = pl.pallas_call(
        start, name=name + "_start", in_specs=[HBM, HBM], out_specs=[SEM, SEM, HBM, HBM],
        out_shape=[sem, sem, hbm_src, hbm_land], input_output_aliases={0: 2, 1: 3}, compiler_params=params,
    )(pltpu.with_memory_space_constraint(by_core, pltpu.HBM),
      pltpu.with_memory_space_constraint(lax.empty(hbm_land.shape, by_core.dtype), pltpu.HBM))
    return pl.pallas_call(
        wait, name=name + "_wait", in_specs=[HBM, HBM, SEM, SEM], out_specs=[HBM, HBM], out_shape=[hbm_src, hbm_land],
        input_output_aliases={0: 0, 1: 1}, compiler_params=params,
    )(src, land, send_sems, recv_sems)


def _pair_add(by_core, landed, name):
    q, _, r, c = by_core.shape
    tc = _tile(c, (512, 256, 128))

    def body(a_ref, b_ref, o_ref):
        mine = a_ref[0, lax.axis_index("c")]
        o_ref[0] = (mine.astype(F32) + b_ref[0].astype(F32)).astype(BF16)

    blk = pl.BlockSpec((1, r, tc), lambda i, j: (i, 0, j))
    return pl.pallas_call(
        body, name=name, grid=(q, c // tc),
        in_specs=[pl.BlockSpec((1, 2, r, tc), lambda i, j: (i, 0, 0, j)), blk], out_specs=blk,
        out_shape=jax.ShapeDtypeStruct(landed.shape, BF16), compiler_params=_cparams("parallel", "parallel"),
    )(by_core, landed)


def _chip_copy(src_ref, land_ref, send_sems, recv_sems, j, sending):
    x, y, c = _place()
    fx, fy, _ = FLIPS[OTHER_CHIPS[j]]
    here, there = 2 * x + y, 2 * (x ^ fx) + (y ^ fy)
    return pltpu.make_async_remote_copy(
        src_ref=src_ref.at[there], dst_ref=land_ref.at[here if sending else there],
        send_sem=send_sems.at[j], recv_sem=recv_sems.at[j],
        device_id=(x ^ fx, y ^ fy, c), device_id_type=MESH)


def _chip_wait(send_sems, recv_sems, src, land, after, name):
    def body(src_ref, land_ref, send_ref, recv_ref, after_ref, src_out, land_out):
        for j in range(len(OTHER_CHIPS)):
            cp = _chip_copy(src_ref, land_ref, send_ref, recv_ref, j, False)
            cp.wait_send()
            cp.wait_recv()

    hbm = lambda a: pltpu.HBM(a.shape, a.dtype)
    return pl.pallas_call(
        body,
        name=name,
        in_specs=[HBM, HBM, SEM, SEM, ANY],
        out_specs=[HBM, HBM],
        out_shape=[hbm(src), hbm(land)],
        input_output_aliases={0: 0, 1: 1},
        compiler_params=pltpu.CompilerParams(has_side_effects=EFFECT),
    )(src, land, send_sems, recv_sems, after)


def _col(v):
    return v.reshape(-1, 1).astype(F32)


def _local_step(xt, tgt, weight, small, pair_sums):
    t = xt.shape[1]
    n1 = _col(small["norm1_w"])
    n2 = _col(small["norm2_w"])
    nf = _col(small["final_norm_w"])
    bg = _col(small["b_gate"])
    sinks = small["attn_sinks"].reshape(-1).astype(F32)
    cbias = _col(small["ssd_conv_b"])
    dtb = _col(small["dt_bias"])
    alog = _col(small["a_log"])
    dsk = _col(small["d_skip"])
    gnw = _col(small["ssd_norm_w"])
    fb = small["ffn_conv_b"].reshape(2, D_FF, 1)

    xn = _norm_fwd(xt, n1, "norm1_fwd")
    cw = weight("ssd_conv_w", xn).T
    fw = weight("ffn_conv_w", xn).T.reshape(2, D_FF, FFN_CONV)
    w_in_t = weight("w_in", xn)
    proj = _matmul(w_in_t, xn, nt=False, out_dtype=F32, name="mm_in")
    ao, lse = _attn_fwd(proj, sinks)
    w_ao = weight("w_attn_o", ao)
    attn = _matmul(w_ao, ao, nt=False, out_dtype=F32, name="mm_attn_o", tn_a=True)
    xbc = _conv_silu_fwd(proj, cw, cbias)
    y, hst = _ssd_fwd(xbc, proj, dtb, alog, dsk)
    yn = _gnorm_fwd(y, proj, gnw)
    w_so = weight("w_ssd_o", yn)
    ssd = _matmul(w_so, yn, nt=False, out_dtype=F32, name="mm_ssd_o", tn_a=True)
    mix = _gate_fwd(proj, bg, attn, ssd)
    w_out = weight("w_out", mix)
    h1 = _matmul(w_out, mix, nt=False, out_dtype=F32, name="mm_out", add=xt, tn_a=True)
    hn = _norm_fwd(h1, n2, "norm2_fwd")
    w_up_t = weight("w_up", hn)
    u0 = _matmul(w_up_t, hn, nt=False, out_dtype=F32, name="mm_up").reshape(2, D_FF, t)
    gl = _ffn_fwd(u0, fw, fb)
    w_down = weight("w_down", gl)
    h2 = _matmul(w_down, gl, nt=False, out_dtype=F32, name="mm_down", add=h1, tn_a=True)
    dh2, loss, d_nf = _final_norm_loss(h2, tgt, nf)

    g = {}
    handles = {}

    def sending(weight_name, grad, fn, *args, **kwargs):
        chunks = grad if grad.ndim == 3 else grad.reshape(N_DEV, -1, D_MODEL)
        out, handles[weight_name] = fn(*args, send=chunks, **kwargs)
        return out

    g_down = _matmul(gl, dh2, nt=True, out_dtype=BF16, name="mm_d_w_down")
    dgl = _matmul(w_down, dh2, nt=False, out_dtype=F32, name="mm_d_glu")
    du0, d_fwb = sending("w_down", g_down, _ffn_bwd, u0, fw, fb, dgl)
    du0 = du0.reshape(2 * D_FF, t)
    g_up = _matmul(du0, hn, nt=True, out_dtype=BF16, name="mm_d_w_up")
    dhn = sending("w_up", g_up, _matmul, w_up_t, du0, nt=False, out_dtype=F32, name="mm_d_hn", tn_a=True)
    dh1, d_n2 = _norm_bwd(dhn, h1, n2, dh2, "norm2_bwd")
    g_out = _matmul(mix, dh1, nt=True, out_dtype=BF16, name="mm_d_w_out")
    dmix = _matmul(w_out, dh1, nt=False, out_dtype=F32, name="mm_d_mix")
    d_attn, d_ssd, d_ga, d_gs, d_ba, d_bs = sending("w_out", g_out, _gate_bwd, proj, bg, attn, ssd, dmix)
    g_ao = _matmul(ao, d_attn, nt=True, out_dtype=BF16, name="mm_d_w_attn_o")
    dao = _matmul(w_ao, d_attn, nt=False, out_dtype=F32, name="mm_d_ao")
    dq, dk, dv, d_sinks = sending("w_attn_o", g_ao, _attn_bwd, proj, sinks, ao, lse, dao)
    g_so = _matmul(yn, d_ssd, nt=True, out_dtype=BF16, name="mm_d_w_ssd_o")
    dyn = _matmul(w_so, d_ssd, nt=False, out_dtype=F32, name="mm_d_yn")
    dy, dz, d_gnw = sending("w_ssd_o", g_so, _gnorm_bwd, dyn, y, proj, gnw)
    dxs, dbm, dcm, ddt, d_alog, d_dsk, d_dtb = _ssd_bwd(xbc, proj, dtb, alog, dsk, hst, dy)
    dx_xs, dwb_xs = _conv_silu_bwd(proj, cw, cbias, dxs, 0, "ssd_conv_bwd_x")
    dx_b, dwb_b = _conv_silu_bwd(proj, cw, cbias, dbm, D_INNER, "ssd_conv_bwd_b")
    dx_c, dwb_c = _conv_silu_bwd(proj, cw, cbias, dcm, D_INNER + BC_DIM, "ssd_conv_bwd_c")
    dwb_conv = jnp.concatenate([dwb_xs, dwb_b, dwb_c], axis=0)
    dproj = jnp.concatenate([dq, dk, dv, dz, dx_xs, dx_b, dx_c, ddt.astype(BF16), d_ga, d_gs], axis=0)
    g_in = pair_sums(_matmul(dproj, xn, nt=True, out_dtype=BF16, name="mm_d_w_in"))
    dxn = sending("w_in", g_in, _matmul, w_in_t, dproj, nt=False, out_dtype=F32, name="mm_d_xn", tn_a=True)
    dx, d_n1 = _norm_bwd(dxn, xt, n1, dh1, "norm1_bwd")

    g["norm1_w"] = d_n1
    g["b_gate"] = jnp.concatenate([d_ba, d_bs], axis=0)
    g["attn_sinks"] = d_sinks
    g["ssd_conv_w"] = dwb_conv[:, :SSD_CONV].T
    g["ssd_conv_b"] = dwb_conv[:, SSD_CONV]
    g["dt_bias"] = d_dtb
    g["a_log"] = d_alog
    g["d_skip"] = d_dsk
    g["ssd_norm_w"] = d_gnw
    g["norm2_w"] = d_n2
    d_fwb = d_fwb.reshape(2 * D_FF, 128)
    g["ffn_conv_w"] = d_fwb[:, :FFN_CONV].T
    g["ffn_conv_b"] = d_fwb[:, FFN_CONV]
    g["final_norm_w"] = d_nf
    return loss, dx, g, handles


SMALL = ("norm1_w", "b_gate", "attn_sinks", "ssd_conv_w", "ssd_conv_b", "dt_bias", "a_log", "d_skip", "ssd_norm_w",
         "norm2_w", "ffn_conv_w", "ffn_conv_b", "final_norm_w")
WEIGHT_ORDER = ("norm1_w", "w_in", "b_gate", "attn_sinks", "w_attn_o", "ssd_conv_w", "ssd_conv_b", "dt_bias", "a_log",
                "d_skip", "ssd_norm_w", "w_ssd_o", "w_out", "norm2_w", "w_up", "ffn_conv_w", "ffn_conv_b", "w_down",
                "final_norm_w")


def kernel(x, norm1_w, w_in, b_gate, attn_sinks, w_attn_o, ssd_conv_w, ssd_conv_b, dt_bias, a_log, d_skip, ssd_norm_w, w_ssd_o, w_out, norm2_w, w_up, ffn_conv_w, ffn_conv_b, w_down, final_norm_w, loss_target, m_norm1_w, m_w_in, m_b_gate, m_attn_sinks, m_w_attn_o, m_ssd_conv_w, m_ssd_conv_b, m_dt_bias, m_a_log, m_d_skip, m_ssd_norm_w, m_w_ssd_o, m_w_out, m_norm2_w, m_w_up, m_ffn_conv_w, m_ffn_conv_b, m_w_down, m_final_norm_w, v_norm1_w, v_w_in, v_b_gate, v_attn_sinks, v_w_attn_o, v_ssd_conv_w, v_ssd_conv_b, v_dt_bias, v_a_log, v_d_skip, v_ssd_norm_w, v_w_ssd_o, v_w_out, v_norm2_w, v_w_up, v_ffn_conv_w, v_ffn_conv_b, v_w_down, v_final_norm_w):
    w = dict(norm1_w=norm1_w, w_in=w_in, b_gate=b_gate, attn_sinks=attn_sinks, w_attn_o=w_attn_o, ssd_conv_w=ssd_conv_w, ssd_conv_b=ssd_conv_b, dt_bias=dt_bias, a_log=a_log, d_skip=d_skip, ssd_norm_w=ssd_norm_w, w_ssd_o=w_ssd_o, w_out=w_out, norm2_w=norm2_w, w_up=w_up, ffn_conv_w=ffn_conv_w, ffn_conv_b=ffn_conv_b, w_down=w_down, final_norm_w=final_norm_w)
    m = dict(norm1_w=m_norm1_w, w_in=m_w_in, b_gate=m_b_gate, attn_sinks=m_attn_sinks, w_attn_o=m_w_attn_o, ssd_conv_w=m_ssd_conv_w, ssd_conv_b=m_ssd_conv_b, dt_bias=m_dt_bias, a_log=m_a_log, d_skip=m_d_skip, ssd_norm_w=m_ssd_norm_w, w_ssd_o=m_w_ssd_o, w_out=m_w_out, norm2_w=m_norm2_w, w_up=m_w_up, ffn_conv_w=m_ffn_conv_w, ffn_conv_b=m_ffn_conv_b, w_down=m_w_down, final_norm_w=m_final_norm_w)
    v = dict(norm1_w=v_norm1_w, w_in=v_w_in, b_gate=v_b_gate, attn_sinks=v_attn_sinks, w_attn_o=v_w_attn_o, ssd_conv_w=v_ssd_conv_w, ssd_conv_b=v_ssd_conv_b, dt_bias=v_dt_bias, a_log=v_a_log, d_skip=v_d_skip, ssd_norm_w=v_ssd_norm_w, w_ssd_o=v_w_ssd_o, w_out=v_w_out, norm2_w=v_norm2_w, w_up=v_w_up, ffn_conv_w=v_ffn_conv_w, ffn_conv_b=v_ffn_conv_b, w_down=v_w_down, final_norm_w=v_final_norm_w)
    me = 4 * lax.axis_index("x") + 2 * lax.axis_index("y") + lax.axis_index("c")

    shards = {"ssd_conv_w": ssd_conv_w[0], "ffn_conv_w": ffn_conv_w[0], "w_in": w_in[0].T.astype(BF16),
              "w_attn_o": w_attn_o[0].astype(BF16), "w_ssd_o": w_ssd_o[0].astype(BF16), "w_out": w_out[0].astype(BF16),
              "w_up": w_up[0].T.astype(BF16), "w_down": w_down[0].astype(BF16)}
    order = list(shards)
    g_send, g_recv, g_src, g_land = _gather_start(list(shards.values()), "gather_start", True)
    first = ("ssd_conv_w", "ffn_conv_w", "w_in")
    forwarded = {}

    def weight(name, after):
        if name not in forwarded:
            group = [k for k in order if (k in first) == (name in first)]
            idx = [order.index(k) for k in group]
            handles = _gather_forward([g_send[i] for i in idx], [g_recv[i] for i in idx], [g_src[i] for i in idx],
                                      [g_land[i] for i in idx], after, "gather_forward_for_" + name)
            forwarded.update(zip(group, zip(*handles)))
        i = order.index(name)
        src, land = _gather_wait_forwarded(g_send[i], g_recv[i], *forwarded[name], after, "gather_wait_" + name)
        land = _own_slot(src, land, me, True)
        if name == "ssd_conv_w":
            return jnp.transpose(land, (1, 0, 2)).reshape(SSD_CONV, XBC_DIM)
        if name == "ffn_conv_w":
            return jnp.transpose(land, (1, 0, 2)).reshape(FFN_CONV, 2 * D_FF)
        return land.reshape(-1, D_MODEL)

    def pair_sums(grad):
        by_core, landed = _pair_exchange(grad.reshape(N_CHIPS, 2, -1, D_MODEL), "grad_pair_w_in")
        return _pair_add(by_core, landed, "grad_pair_add_w_in")

    small = {k: w[k][0] if k != "final_norm_w" else w[k] for k in SMALL}
    loss, dx, g, pending = _local_step(x[0].T, loss_target[0].T, weight, small, pair_sums)

    packed = _pack_row([loss] + [g[k] for k in SMALL])
    s_send, s_recv, s_src, s_land = _gather_start([packed], "small_grads_start", False)

    res = {}
    after = s_src[0]
    for name in ("w_down", "w_up", "w_out", "w_attn_o", "w_ssd_o", "w_in"):
        if name == "w_in":
            parts = _own_slot(*_chip_wait(*pending[name], after, "grad_wait_" + name), me // 2, False)
        else:
            parts = _own_slot(*_exchange_wait(*pending[name], after, False, "grad_wait_" + name), me, False)
        view, back = {
            "w_in": (lambda a: jnp.transpose(a, (2, 0, 1)), lambda r: jnp.transpose(r, (1, 2, 0))),
            "w_up": (lambda a: a[0].T, lambda r: r.T[None]),
        }.get(name, (lambda a: a[0], lambda r: r[None]))
        res[name] = _adamw_sharded(parts, view(w[name]), view(m[name]), view(v[name]), "adamw_" + name)
        after = res[name][0]
        res[name] = [back(r) for r in res[name]]

    rows = _own_slot(*_exchange_wait(s_send[0], s_recv[0], s_src[0], s_land[0], after, True, "small_grads_wait"),
                     me, True)
    flat = lambda a: a.reshape(-1, a.shape[-1])
    loss_sum, updates = _small_update(
        rows, me.reshape(1), [g[k].size for k in SMALL],
        [flat(w[k]) for k in SMALL], [flat(m[k]) for k in SMALL], [flat(v[k]) for k in SMALL])
    for k, upd in zip(SMALL, updates):
        res[k] = [u.reshape(w[k].shape) for u in upd]

    grad_x = dx.T[None]
    outs = [loss_sum.reshape(()), grad_x]
    for i in range(4):
        outs.extend(res[k][i] for k in WEIGHT_ORDER)
    return tuple(outs)
```

```python
import functools

import jax
import jax.numpy as jnp
from jax import lax
from jax.experimental import pallas as pl
from jax.experimental.pallas import tpu as pltpu

F32 = jnp.float32
BF16 = jnp.bfloat16
HIGHEST = lax.Precision.HIGHEST

D_MODEL = 1024
N_Q_HEADS = 16
N_KV_HEADS = 4
HEAD_DIM = 64
WINDOW = 128
Q_PER_KV = N_Q_HEADS // N_KV_HEADS
Q_DIM = N_Q_HEADS * HEAD_DIM
KV_DIM = N_KV_HEADS * HEAD_DIM
D_INNER = 2048
SSD_HEAD_DIM = 64
N_SSD_HEADS = 32
N_SSD_GROUPS = 4
HEADS_PER_GROUP = N_SSD_HEADS // N_SSD_GROUPS
D_STATE = 128
BC_DIM = N_SSD_GROUPS * D_STATE
XBC_DIM = D_INNER + 2 * BC_DIM
SSD_CONV = 4
CHUNK = 128
D_FF = 2816
FFN_CONV = 3
EPS = 1e-5
NEG = -1e30
IN_DIM = 8736
N_DEV = 8

OFF_Q = 0
OFF_K = OFF_Q + Q_DIM
OFF_V = OFF_K + KV_DIM
OFF_Z = OFF_V + KV_DIM
OFF_X = OFF_Z + D_INNER
OFF_DT = OFF_X + XBC_DIM
OFF_GA = OFF_DT + N_SSD_HEADS
OFF_GS = OFF_GA + D_MODEL

ADAM_LR = 0.001
ADAM_B1 = 0.9
ADAM_B2 = 0.999
ADAM_EPS = 1e-08
ADAM_WD = 0.01
ADAM_STEP = 10

VMEM_LIMIT = 48 * 1024 * 1024
MESH = pl.DeviceIdType.MESH


def _cparams(*sem):
    return pltpu.CompilerParams(dimension_semantics=sem, vmem_limit_bytes=VMEM_LIMIT)


def _tile(n, prefs):
    for p in prefs:
        if n % p == 0:
            return p
    return n


def _sigmoid(x):
    return 1.0 / (1.0 + jnp.exp(-x))


def _softplus(x):
    return jnp.maximum(x, 0.0) + jnp.log(1.0 + jnp.exp(-jnp.abs(x)))


def _rowsum(x):
    return jnp.sum(x, axis=1, keepdims=True)


def _colsum(x):
    return jnp.sum(x, axis=0, keepdims=True)


def _dot(a, b):
    return jnp.dot(a, b, preferred_element_type=F32)


def _dot_nt(a, b):
    return lax.dot_general(a, b, (((1,), (1,)), ((), ())), preferred_element_type=F32)


def _dot_tn(a, b):
    return lax.dot_general(a, b, (((0,), (0,)), ((), ())), preferred_element_type=F32)


def _shift_right(x, j):
    if j == 0:
        return x
    r = pltpu.roll(x, j, 1)
    lane = lax.broadcasted_iota(jnp.int32, (x.shape[0], 128), 1)
    return jnp.concatenate([jnp.where(lane >= j, r[:, :128], 0.0), r[:, 128:]], axis=1)


def _shift_left(x, j):
    if j == 0:
        return x
    n = x.shape[1]
    r = pltpu.roll(x, n - j, 1)
    lane = lax.broadcasted_iota(jnp.int32, (x.shape[0], 128), 1)
    return jnp.concatenate([r[:, :n - 128], jnp.where(lane < 128 - j, r[:, n - 128:], 0.0)], axis=1)


def _causal_conv(xv, wv, bv):
    taps = wv.shape[1]
    shifted = [_shift_right(xv, taps - 1 - k) for k in range(taps - 1)]
    y = bv + wv[:, taps - 1:taps] * xv
    for k in range(taps - 1):
        y = y + wv[:, k:k + 1] * shifted[k]
    return y, shifted


def _causal_conv_bwd(dy, xv, shifted, wv):
    taps = wv.shape[1]
    lane = lax.broadcasted_iota(jnp.int32, (dy.shape[0], 128), 1)
    dwb = jnp.where(lane == taps, _rowsum(dy), 0.0)
    dwb = jnp.where(lane == taps - 1, _rowsum(dy * xv), dwb)
    dx = wv[:, taps - 1:taps] * dy
    for k in range(taps - 1):
        dx = dx + wv[:, k:k + 1] * _shift_left(dy, taps - 1 - k)
        dwb = jnp.where(lane == k, _rowsum(dy * shifted[k]), dwb)
    return dx, dwb


def _call(body, *, name, grid, in_specs, out_specs, out_shape, args, semantics, scratch_shapes=(), send=None):
    if send is None:
        return pl.pallas_call(body, name=name, grid=grid, in_specs=in_specs, out_specs=out_specs, out_shape=out_shape,
                              scratch_shapes=list(scratch_shapes), compiler_params=_cparams(*semantics))(*args)
    single = not isinstance(out_specs, (list, tuple))
    out_specs, out_shape = ([out_specs], [out_shape]) if single else (list(out_specs), list(out_shape))
    n_in, n_out = len(in_specs), len(out_specs)
    chips = send.shape[0] == N_DEV // 2
    n_copies = len(OTHER_CHIPS) if chips else N_DEV - 1

    def sending(*refs):
        ins, (src_ref, land_ref) = refs[:n_in], refs[n_in:n_in + 2]
        outs = refs[n_in + 2:n_in + 2 + n_out]
        send_sems, recv_sems = refs[n_in + 2 + n_out:n_in + 4 + n_out]
        scratch = refs[n_in + 6 + n_out:]
        step = 0
        for axis, size in enumerate(grid):
            step = step * size + pl.program_id(axis)

        @pl.when(step == 0)
        def _():
            for k in range(n_copies):
                if chips:
                    _chip_copy(src_ref, land_ref, send_sems, recv_sems, k, True).start()
                else:
                    _peer_copy(False, src_ref, land_ref, send_sems, recv_sems, k, True).start()

        body(*ins, *outs, *scratch)

    sem = pltpu.SemaphoreType.DMA((n_copies,))
    hbm = pltpu.HBM(send.shape, send.dtype)
    res = pl.pallas_call(
        sending, name=name, grid=grid,
        in_specs=list(in_specs) + [HBM, HBM],
        out_specs=out_specs + [SEM, SEM, HBM, HBM],
        out_shape=out_shape + [sem, sem, hbm, hbm],
        input_output_aliases={n_in: n_out + 2, n_in + 1: n_out + 3},
        scratch_shapes=list(scratch_shapes),
        compiler_params=pltpu.CompilerParams(dimension_semantics=("arbitrary",) * len(grid), vmem_limit_bytes=VMEM_LIMIT,
                                             has_side_effects=EFFECT),
    )(*args, pltpu.with_memory_space_constraint(send, pltpu.HBM),
      pltpu.with_memory_space_constraint(lax.empty(send.shape, send.dtype), pltpu.HBM))
    return (res[0] if single else list(res[:n_out])), tuple(res[n_out:])


MATMUL_VMEM_BUDGET = 36 * 1024 * 1024
MATMUL_MAX_TK = 3072


MATMUL_MAX_TM = 768


def _largest_tile(n, align, cap):
    return max(d for d in range(align, min(n, cap) + 1, align) if n % d == 0)


def _matmul_tiles(m, n, k, a_bytes, b_bytes, out_bytes, has_add, m_align, k_align):
    tm = _largest_tile(m, m_align, MATMUL_MAX_TM)
    tk = _largest_tile(k, k_align, MATMUL_MAX_TK)
    for tn in sorted({d for d in range(128, n + 1, 128) if n % d == 0}, reverse=True):
        need = 2 * (tm * tk * a_bytes + tk * tn * b_bytes) + tm * tn * (2 * out_bytes + (4 if k > tk else 0) + (8 if has_add else 0))
        if tn <= 3072 and need <= MATMUL_VMEM_BUDGET:
            return tm, tn, tk
    return tm, 128, tk


def _matmul(a, b, *, nt, out_dtype, name, add=None, tn_a=False, send=None):
    if tn_a:
        k, m = a.shape
    else:
        m, k = a.shape
    n = b.shape[0] if nt else b.shape[1]
    tm, tn, tk = _matmul_tiles(m, n, k, a.dtype.itemsize, b.dtype.itemsize, jnp.dtype(out_dtype).itemsize, add is not None,
                               128 if tn_a else 16, 16 if tn_a and not nt else 128)
    nk = k // tk
    grid = (m // tm, n // tn, nk)

    def body(a_ref, b_ref, *rest):
        r_ref = None
        if add is not None:
            r_ref, rest = rest[0], rest[1:]
        o_ref = rest[0]
        av = a_ref[...].astype(BF16)
        bv = b_ref[...].astype(BF16)
        part = _dot_tn(av, bv) if tn_a else _dot_nt(av, bv) if nt else _dot(av, bv)

        def finish(r):
            if add is not None:
                r = r + r_ref[...]
            o_ref[...] = r.astype(out_dtype)

        if nk == 1:
            finish(part)
            return
        acc = rest[1]
        kk = pl.program_id(2)

        @pl.when(kk == 0)
        def _():
            acc[...] = part

        @pl.when((kk > 0) & (kk < nk - 1))
        def _():
            acc[...] += part

        @pl.when(kk == nk - 1)
        def _():
            finish(acc[...] + part)

    in_specs = [
        pl.BlockSpec((tk, tm), lambda i, j, kk: (kk, i)) if tn_a else pl.BlockSpec((tm, tk), lambda i, j, kk: (i, kk)),
        pl.BlockSpec((tn, tk), lambda i, j, kk: (j, kk)) if nt else pl.BlockSpec((tk, tn), lambda i, j, kk: (kk, j)),
    ]
    args = [a, b]
    if add is not None:
        in_specs.append(pl.BlockSpec((tm, tn), lambda i, j, kk: (i, j)))
        args.append(add)
    return _call(
        body, name=name, grid=grid, in_specs=in_specs, args=args,
        out_specs=pl.BlockSpec((tm, tn), lambda i, j, kk: (i, j)),
        out_shape=jax.ShapeDtypeStruct((m, n), out_dtype),
        scratch_shapes=[pltpu.VMEM((tm, tn), F32)] if nk > 1 else [],
        semantics=("parallel", "parallel", "arbitrary"), send=send)


def _norm_fwd(x, w_col, name):
    f, t = x.shape
    tt = _tile(t, (512, 256, 128))

    def body(x_ref, w_ref, o_ref):
        xv = x_ref[...]
        r = lax.rsqrt(jnp.mean(xv * xv, axis=0, keepdims=True) + EPS)
        o_ref[...] = (xv * r * w_ref[...]).astype(BF16)

    return pl.pallas_call(
        body,
        name=name,
        grid=(t // tt,),
        in_specs=[pl.BlockSpec((f, tt), lambda i: (0, i)), pl.BlockSpec((f, 1), lambda i: (0, 0))],
        out_specs=pl.BlockSpec((f, tt), lambda i: (0, i)),
        out_shape=jax.ShapeDtypeStruct((f, t), BF16),
        compiler_params=_cparams("parallel"),
    )(x, w_col)


def _norm_fwd_tokens(x, w_col, after, name):
    t, f = x.shape
    tt = _tile(t, (512, 256, 128))

    def body(x_ref, w_ref, after_ref, xt_ref, o_ref):
        xv = x_ref[...].T
        xt_ref[...] = xv
        r = lax.rsqrt(jnp.mean(xv * xv, axis=0, keepdims=True) + EPS)
        o_ref[...] = (xv * r * w_ref[...]).astype(BF16)

    blk = pl.BlockSpec((f, tt), lambda i: (0, i))
    return pl.pallas_call(
        body,
        name=name,
        grid=(t // tt,),
        in_specs=[pl.BlockSpec((tt, f), lambda i: (i, 0)), pl.BlockSpec((f, 1), lambda i: (0, 0)), ANY],
        out_specs=[blk, blk],
        out_shape=[jax.ShapeDtypeStruct((f, t), F32), jax.ShapeDtypeStruct((f, t), BF16)],
        compiler_params=_cparams("parallel"),
    )(x, w_col, after)


def _norm_bwd(dy, x, w_col, res, name, tokens_out=False):
    f, t = x.shape
    tt = _tile(t, (512, 256, 128))

    def body(dy_ref, x_ref, w_ref, res_ref, dx_ref, dw_ref):
        @pl.when(pl.program_id(0) == 0)
        def _():
            dw_ref[...] = jnp.zeros_like(dw_ref)

        xv = x_ref[...]
        r = lax.rsqrt(jnp.mean(xv * xv, axis=0, keepdims=True) + EPS)
        xhat = xv * r
        dyv = dy_ref[...]
        dw_ref[...] += _rowsum(dyv * xhat)
        dxhat = dyv * w_ref[...]
        dx = res_ref[...] + r * (dxhat - xhat * jnp.mean(dxhat * xhat, axis=0, keepdims=True))
        dx_ref[...] = dx.T if tokens_out else dx

    blk = pl.BlockSpec((f, tt), lambda i: (0, i))
    col = pl.BlockSpec((f, 1), lambda i: (0, 0))
    return pl.pallas_call(
        body,
        name=name,
        grid=(t // tt,),
        in_specs=[blk, blk, col, blk],
        out_specs=[pl.BlockSpec((tt, f), lambda i: (i, 0)) if tokens_out else blk, col],
        out_shape=[jax.ShapeDtypeStruct((t, f) if tokens_out else (f, t), F32), jax.ShapeDtypeStruct((f, 1), F32)],
        compiler_params=_cparams("arbitrary"),
    )(dy, x, w_col, res)


def _final_norm_loss(h, tgt, w_col):
    f, t = h.shape
    tt = _tile(t, (512, 256, 128))

    def body(h_ref, t_ref, w_ref, dh_ref, loss_ref, dw_ref):
        @pl.when(pl.program_id(0) == 0)
        def _():
            dw_ref[...] = jnp.zeros_like(dw_ref)
            loss_ref[...] = jnp.zeros_like(loss_ref)

        xv = h_ref[...]
        r = lax.rsqrt(jnp.mean(xv * xv, axis=0, keepdims=True) + EPS)
        xhat = xv * r
        wv = w_ref[...]
        err = xhat * wv - t_ref[...].T
        loss_ref[...] += 0.5 * _rowsum(jnp.mean(err * err, axis=0, keepdims=True))
        dyv = err * (1.0 / f)
        dw_ref[...] += _rowsum(dyv * xhat)
        dxhat = dyv * wv
        dh_ref[...] = r * (dxhat - xhat * jnp.mean(dxhat * xhat, axis=0, keepdims=True))

    blk = pl.BlockSpec((f, tt), lambda i: (0, i))
    col = pl.BlockSpec((f, 1), lambda i: (0, 0))
    one = pl.BlockSpec((1, 1), lambda i: (0, 0))
    return pl.pallas_call(
        body,
        name="final_norm_loss",
        grid=(t // tt,),
        in_specs=[blk, pl.BlockSpec((tt, f), lambda i: (i, 0)), col],
        out_specs=[blk, one, col],
        out_shape=[jax.ShapeDtypeStruct((f, t), F32), jax.ShapeDtypeStruct((1, 1), F32), jax.ShapeDtypeStruct((f, 1), F32)],
        compiler_params=_cparams("arbitrary"),
    )(h, tgt, w_col)


def _attn_mask(n):
    shape = (2 * WINDOW, Q_PER_KV * WINDOW)
    si = lax.broadcasted_iota(jnp.int32, shape, 0)
    qi = lax.broadcasted_iota(jnp.int32, shape, 1) & (WINDOW - 1)
    dist = WINDOW + qi - si
    return (dist >= 0) & (dist < WINDOW) & ((si >= WINDOW) | (n > 0))


def _lane_cat(ref, row0, rows):
    return jnp.concatenate([ref[row0 + i * rows:row0 + (i + 1) * rows, :] for i in range(Q_PER_KV)], axis=1)


def _attn_fwd(proj, sinks):
    t = proj.shape[1]
    nb = t // WINDOW
    scale = HEAD_DIM ** -0.5

    def body(s_ref, q_ref, kc_ref, kp_ref, vc_ref, vp_ref, o_ref, lse_ref):
        n = pl.program_id(0)
        valid = _attn_mask(n)
        for g in range(N_KV_HEADS):
            rows = slice(g * HEAD_DIM, (g + 1) * HEAD_DIM)
            kt = jnp.concatenate([kp_ref[rows, :], kc_ref[rows, :]], axis=1).astype(BF16)
            vt = jnp.concatenate([vp_ref[rows, :], vc_ref[rows, :]], axis=1).astype(BF16)
            qcat = (_lane_cat(q_ref, g * Q_PER_KV * HEAD_DIM, HEAD_DIM) * scale).astype(BF16)
            s = jnp.where(valid, _dot_tn(kt, qcat), NEG)
            sink = jnp.concatenate(
                [jnp.full((1, WINDOW), s_ref[g * Q_PER_KV + i], F32) for i in range(Q_PER_KV)], axis=1)
            m = jnp.maximum(jnp.max(s, axis=0, keepdims=True), sink)
            p = jnp.exp(s - m)
            denom = _colsum(p) + jnp.exp(sink - m)
            probs = (p / denom).astype(BF16)
            out = _dot(vt, probs)
            lse = m + jnp.log(denom)
            for i in range(Q_PER_KV):
                h = g * Q_PER_KV + i
                o_ref[h * HEAD_DIM:(h + 1) * HEAD_DIM, :] = out[:, i * WINDOW:(i + 1) * WINDOW]
                lse_ref[h:h + 1, :] = lse[:, i * WINDOW:(i + 1) * WINDOW]

    kb = OFF_K // KV_DIM
    vb = OFF_V // KV_DIM
    prev = lambda n: jnp.maximum(n - 1, 0)
    return pl.pallas_call(
        body,
        name="attn_fwd",
        grid=(nb,),
        in_specs=[
            pl.BlockSpec(memory_space=pltpu.SMEM),
            pl.BlockSpec((Q_DIM, WINDOW), lambda n: (0, n)),
            pl.BlockSpec((KV_DIM, WINDOW), lambda n: (kb, n)),
            pl.BlockSpec((KV_DIM, WINDOW), lambda n: (kb, prev(n))),
            pl.BlockSpec((KV_DIM, WINDOW), lambda n: (vb, n)),
            pl.BlockSpec((KV_DIM, WINDOW), lambda n: (vb, prev(n))),
        ],
        out_specs=[pl.BlockSpec((Q_DIM, WINDOW), lambda n: (0, n)), pl.BlockSpec((N_Q_HEADS, WINDOW), lambda n: (0, n))],
        out_shape=[jax.ShapeDtypeStruct((Q_DIM, t), F32), jax.ShapeDtypeStruct((N_Q_HEADS, t), F32)],
        compiler_params=_cparams("parallel"),
    )(sinks, proj, proj, proj, proj, proj)


def _attn_bwd(proj, sinks, out, lse, dout, send=None):
    t = proj.shape[1]
    nb = t // WINDOW
    scale = HEAD_DIM ** -0.5

    def body(s_ref, q_ref, kc_ref, kp_ref, vc_ref, vp_ref, o_ref, lse_ref, do_ref,
             dq_ref, dk_ref, dv_ref, ds_ref, dk_carry, dv_carry):
        step = pl.program_id(0)
        n = nb - 1 - step

        @pl.when(step == 0)
        def _():
            dk_carry[...] = jnp.zeros_like(dk_carry)
            dv_carry[...] = jnp.zeros_like(dv_carry)
            ds_ref[...] = jnp.zeros_like(ds_ref)

        valid = _attn_mask(n)
        for g in range(N_KV_HEADS):
            rows = slice(g * HEAD_DIM, (g + 1) * HEAD_DIM)
            q0 = g * Q_PER_KV * HEAD_DIM
            kt = jnp.concatenate([kp_ref[rows, :], kc_ref[rows, :]], axis=1).astype(BF16)
            vt = jnp.concatenate([vp_ref[rows, :], vc_ref[rows, :]], axis=1).astype(BF16)
            qf = _lane_cat(q_ref, q0, HEAD_DIM)
            qcat = qf.astype(BF16)
            ocat = _lane_cat(o_ref, q0, HEAD_DIM)
            docat = _lane_cat(do_ref, q0, HEAD_DIM)
            dob = docat.astype(BF16)
            lse_cat = jnp.concatenate(
                [lse_ref[g * Q_PER_KV + i:g * Q_PER_KV + i + 1, :] for i in range(Q_PER_KV)], axis=1)
            sink = jnp.concatenate(
                [jnp.full((1, WINDOW), s_ref[g * Q_PER_KV + i], F32) for i in range(Q_PER_KV)], axis=1)
            s = jnp.where(valid, _dot_tn(kt, (qf * scale).astype(BF16)), NEG)
            p = jnp.exp(s - lse_cat)
            dp = _dot_tn(vt, dob)
            delta = _colsum(docat * ocat)
            dsc = (p * (dp - delta)).astype(BF16)
            dsink_row = -jnp.exp(sink - lse_cat) * delta
            dq = _dot(kt, dsc) * scale
            dk = _dot_nt(qcat, dsc) * scale
            dv = _dot_nt(dob, p.astype(BF16))
            for i in range(Q_PER_KV):
                h = g * Q_PER_KV + i
                dq_ref[h * HEAD_DIM:(h + 1) * HEAD_DIM, :] = dq[:, i * WINDOW:(i + 1) * WINDOW].astype(BF16)
                ds_ref[h:h + 1, :] += _rowsum(dsink_row[:, i * WINDOW:(i + 1) * WINDOW])
            dk_ref[rows, :] = (dk[:, WINDOW:] + dk_carry[rows, :]).astype(BF16)
            dv_ref[rows, :] = (dv[:, WINDOW:] + dv_carry[rows, :]).astype(BF16)
            dk_carry[rows, :] = dk[:, :WINDOW]
            dv_carry[rows, :] = dv[:, :WINDOW]

    kb = OFF_K // KV_DIM
    vb = OFF_V // KV_DIM
    cur = lambda i: nb - 1 - i
    prev = lambda i: jnp.maximum(nb - 2 - i, 0)
    qspec = pl.BlockSpec((Q_DIM, WINDOW), lambda i: (0, cur(i)))
    kvspec = pl.BlockSpec((KV_DIM, WINDOW), lambda i: (0, cur(i)))
    return _call(
        body,
        name="attn_bwd",
        grid=(nb,),
        in_specs=[
            pl.BlockSpec(memory_space=pltpu.SMEM),
            qspec,
            pl.BlockSpec((KV_DIM, WINDOW), lambda i: (kb, cur(i))),
            pl.BlockSpec((KV_DIM, WINDOW), lambda i: (kb, prev(i))),
            pl.BlockSpec((KV_DIM, WINDOW), lambda i: (vb, cur(i))),
            pl.BlockSpec((KV_DIM, WINDOW), lambda i: (vb, prev(i))),
            qspec,
            pl.BlockSpec((N_Q_HEADS, WINDOW), lambda i: (0, cur(i))),
            qspec,
        ],
        out_specs=[qspec, kvspec, kvspec, pl.BlockSpec((N_Q_HEADS, 1), lambda i: (0, 0))],
        out_shape=[
            jax.ShapeDtypeStruct((Q_DIM, t), BF16),
            jax.ShapeDtypeStruct((KV_DIM, t), BF16),
            jax.ShapeDtypeStruct((KV_DIM, t), BF16),
            jax.ShapeDtypeStruct((N_Q_HEADS, 1), F32),
        ],
        scratch_shapes=[pltpu.VMEM((KV_DIM, WINDOW), F32), pltpu.VMEM((KV_DIM, WINDOW), F32)],
        semantics=("arbitrary",), args=(sinks, proj, proj, proj, proj, proj, out, lse, dout), send=send)


CONV_ROWS = 256


def _conv_silu_fwd(proj, w_col, b_col):
    t = proj.shape[1]
    r0 = OFF_X // CONV_ROWS

    def body(x_ref, w_ref, b_ref, o_ref):
        def strip(rows):
            y, _ = _causal_conv(x_ref[rows, :], w_ref[rows, :], b_ref[rows, :])
            o_ref[rows, :] = y * _sigmoid(y)

        strip(slice(None))

    return pl.pallas_call(
        body,
        name="ssd_conv_fwd",
        grid=(XBC_DIM // CONV_ROWS,),
        in_specs=[
            pl.BlockSpec((CONV_ROWS, t), lambda i: (r0 + i, 0)),
            pl.BlockSpec((CONV_ROWS, SSD_CONV), lambda i: (i, 0)),
            pl.BlockSpec((CONV_ROWS, 1), lambda i: (i, 0)),
        ],
        out_specs=pl.BlockSpec((CONV_ROWS, t), lambda i: (i, 0)),
        out_shape=jax.ShapeDtypeStruct((XBC_DIM, t), F32),
        compiler_params=_cparams("parallel"),
    )(proj, w_col, b_col)


def _conv_silu_bwd(proj, w_col, b_col, dout, row0, name):
    t = proj.shape[1]
    nrows = dout.shape[0]
    p0 = (OFF_X + row0) // CONV_ROWS
    c0 = row0 // CONV_ROWS

    def body(x_ref, w_ref, b_ref, do_ref, dx_ref, dwb_ref):
        def strip(rows):
            xv = x_ref[rows, :]
            wv = w_ref[rows, :]
            y, shifted = _causal_conv(xv, wv, b_ref[rows, :])
            sg = _sigmoid(y)
            dy = do_ref[rows, :] * (sg * (1.0 + y * (1.0 - sg)))
            dx, dwb_ref[rows, :] = _causal_conv_bwd(dy, xv, shifted, wv)
            dx_ref[rows, :] = dx.astype(BF16)

        strip(slice(None))

    return pl.pallas_call(
        body,
        name=name,
        grid=(nrows // CONV_ROWS,),
        in_specs=[
            pl.BlockSpec((CONV_ROWS, t), lambda i: (p0 + i, 0)),
            pl.BlockSpec((CONV_ROWS, SSD_CONV), lambda i: (c0 + i, 0)),
            pl.BlockSpec((CONV_ROWS, 1), lambda i: (c0 + i, 0)),
            pl.BlockSpec((CONV_ROWS, t), lambda i: (i, 0)),
        ],
        out_specs=[pl.BlockSpec((CONV_ROWS, t), lambda i: (i, 0)), pl.BlockSpec((CONV_ROWS, 128), lambda i: (i, 0))],
        out_shape=[jax.ShapeDtypeStruct((nrows, t), BF16), jax.ShapeDtypeStruct((nrows, 128), F32)],
        compiler_params=_cparams("parallel"),
    )(proj, w_col, b_col, dout)


GROUP_ROWS = HEADS_PER_GROUP * SSD_HEAD_DIM


def _ssd_specs(order):
    xb = D_INNER // BC_DIM
    dtb = OFF_DT // N_SSD_HEADS
    col = pl.BlockSpec((N_SSD_HEADS, 1), lambda c: (0, 0))
    return [
        pl.BlockSpec((D_INNER, CHUNK), lambda c: (0, order(c))),
        pl.BlockSpec((BC_DIM, CHUNK), lambda c: (xb, order(c))),
        pl.BlockSpec((BC_DIM, CHUNK), lambda c: (xb + 1, order(c))),
        pl.BlockSpec((N_SSD_HEADS, CHUNK), lambda c: (dtb, order(c))),
        col, col, col,
    ]


def _ssd_common(dt_ref, dtb_ref, alog_ref):
    z = dt_ref[...] + dtb_ref[...]
    dt = _softplus(z)
    a_neg = -jnp.exp(alog_ref[...])
    d_a = dt * a_neg
    row = lax.broadcasted_iota(jnp.int32, (CHUNK, CHUNK), 0)
    colm = lax.broadcasted_iota(jnp.int32, (CHUNK, CHUNK), 1)
    upper = (row <= colm).astype(F32)
    a_cs = jnp.dot(d_a, upper, precision=HIGHEST, preferred_element_type=F32)
    a_last = _rowsum(d_a)
    return z, dt, a_neg, a_cs, a_last, row >= colm, row == colm


def _decay(a_row, causal):
    a_s = jnp.broadcast_to(a_row, (CHUNK, CHUNK))
    seg = a_s.T - a_s
    return jnp.where(causal, jnp.exp(jnp.where(causal, seg, 0.0)), 0.0)


def _ssd_fwd(xbc, proj, dtb_col, alog_col, dsk_col):
    t = xbc.shape[1]
    nc = t // CHUNK

    def body(xs_ref, b_ref, c_ref, dt_ref, dtb_ref, alog_ref, dsk_ref, y_ref, hst_ref, h_scr):
        @pl.when(pl.program_id(0) == 0)
        def _():
            h_scr[...] = jnp.zeros_like(h_scr)

        _, dt, _, a_cs, a_last, causal, _ = _ssd_common(dt_ref, dtb_ref, alog_ref)
        hst_ref[0] = h_scr[...]
        dsk = dsk_ref[...]
        for g in range(N_SSD_GROUPS):
            grows = slice(g * D_STATE, (g + 1) * D_STATE)
            bb = b_ref[grows, :].astype(BF16)
            cb_ = c_ref[grows, :].astype(BF16)
            cb = _dot_tn(cb_, bb)
            for j in range(g * HEADS_PER_GROUP, (g + 1) * HEADS_PER_GROUP):
                rows = slice(j * SSD_HEAD_DIM, (j + 1) * SSD_HEAD_DIM)
                a = a_cs[j:j + 1, :]
                m = (cb * _decay(a, causal)).astype(BF16)
                xs = xs_ref[rows, :]
                xc = xs * dt[j:j + 1, :]
                hj = h_scr[rows, :]
                y = _dot_nt(xc.astype(BF16), m) + _dot(hj.astype(BF16), cb_) * jnp.exp(a) + dsk[j:j + 1, :] * xs
                y_ref[rows, :] = y
                al = a_last[j:j + 1, :]
                w = jnp.exp(al - a)
                h_scr[rows, :] = jnp.exp(al) * hj + _dot_nt((xc * w).astype(BF16), bb)

    return pl.pallas_call(
        body,
        name="ssd_fwd",
        grid=(nc,),
        in_specs=_ssd_specs(lambda c: c),
        out_specs=[
            pl.BlockSpec((D_INNER, CHUNK), lambda c: (0, c)),
            pl.BlockSpec((1, D_INNER, D_STATE), lambda c: (c, 0, 0)),
        ],
        out_shape=[
            jax.ShapeDtypeStruct((D_INNER, t), F32),
            jax.ShapeDtypeStruct((nc, D_INNER, D_STATE), F32),
        ],
        scratch_shapes=[pltpu.VMEM((D_INNER, D_STATE), F32)],
        compiler_params=_cparams("arbitrary"),
    )(xbc, xbc, xbc, proj, dtb_col, alog_col, dsk_col)


def _ssd_bwd(xbc, proj, dtb_col, alog_col, dsk_col, hst, dy):
    t = xbc.shape[1]
    nc = t // CHUNK
    rev = lambda c: nc - 1 - c

    def body(xs_ref, b_ref, c_ref, dt_ref, dtb_ref, alog_ref, dsk_ref, hst_ref, dy_ref,
             dxs_ref, db_ref, dc_ref, ddt_ref, dalog_ref, ddsk_ref, ddtb_ref, dh_scr, da_scr, ddt_scr, dd_scr):
        @pl.when(pl.program_id(0) == 0)
        def _():
            dh_scr[...] = jnp.zeros_like(dh_scr)
            dalog_ref[...] = jnp.zeros_like(dalog_ref)
            ddsk_ref[...] = jnp.zeros_like(ddsk_ref)
            ddtb_ref[...] = jnp.zeros_like(ddtb_ref)

        z, dt, a_neg, a_cs, a_last, causal, eye = _ssd_common(dt_ref, dtb_ref, alog_ref)
        dsk = dsk_ref[...]
        last_lane = lax.broadcasted_iota(jnp.int32, (1, CHUNK), 1) == CHUNK - 1
        for g in range(N_SSD_GROUPS):
            grows = slice(g * D_STATE, (g + 1) * D_STATE)
            bb = b_ref[grows, :].astype(BF16)
            cb_ = c_ref[grows, :].astype(BF16)
            cb = _dot_tn(cb_, bb)
            dcb = jnp.zeros((CHUNK, CHUNK), F32)
            dc_acc = jnp.zeros((D_STATE, CHUNK), F32)
            db_acc = jnp.zeros((D_STATE, CHUNK), F32)
            for j in range(g * HEADS_PER_GROUP, (g + 1) * HEADS_PER_GROUP):
                rows = slice(j * SSD_HEAD_DIM, (j + 1) * SSD_HEAD_DIM)
                a = a_cs[j:j + 1, :]
                al = a_last[j:j + 1, :]
                lam = _decay(a, causal)
                mf = cb * lam
                xs = xs_ref[rows, :]
                dtj = dt[j:j + 1, :]
                xc = xs * dtj
                w = jnp.exp(al - a)
                e = jnp.exp(a)
                gam = jnp.exp(al)
                hj = hst_ref[0, rows, :]
                hjb = hj.astype(BF16)
                dyv = dy_ref[rows, :]
                dyb = dyv.astype(BF16)
                dd_scr[j:j + 1, :] = _colsum(dyv * xs)
                gb = (dyv * e).astype(BF16)
                dh_in = _dot_nt(gb, cb_)
                dc_acc = dc_acc + _dot_tn(hjb, gb)
                yoff = _dot(hjb, cb_) * e
                da = _colsum(dyv * yoff)
                dm = _dot_tn(dyb, xc.astype(BF16))
                dxc = _dot(dyb, mf.astype(BF16))
                dcb = dcb + dm * lam
                nmat = dm * mf
                rs = jnp.broadcast_to(_rowsum(nmat), (CHUNK, CHUNK))
                da = da + _colsum(jnp.where(eye, rs, 0.0)) - _colsum(nmat)
                ds = dh_scr[rows, :]
                dsb = ds.astype(BF16)
                t1 = _dot(dsb, bb)
                xcw = xc * w
                dxc = dxc + w * t1
                dww = _colsum(xcw * t1)
                da_l = _rowsum(dww) + _rowsum(_colsum(ds * hj)) * gam
                da = da - dww + jnp.where(last_lane, da_l, 0.0)
                db_acc = db_acc + _dot_tn(dsb, xcw.astype(BF16))
                dh_scr[rows, :] = gam * ds + dh_in
                dxs_ref[rows, :] = dsk[j:j + 1, :] * dyv + dxc * dtj
                da_scr[j:j + 1, :] = da
                ddt_scr[j:j + 1, :] = _colsum(dxc * xs)
            dcbb = dcb.astype(BF16)
            dc_ref[grows, :] = dc_acc + _dot_nt(bb, dcbb)
            db_ref[grows, :] = db_acc + _dot(cb_, dcbb)
        dda = jnp.dot(da_scr[...], causal.astype(F32), precision=HIGHEST, preferred_element_type=F32)
        ddt = ddt_scr[...] + dda * a_neg
        ddt_raw = ddt * _sigmoid(z)
        ddt_ref[...] = ddt_raw
        ddtb_ref[...] += _rowsum(ddt_raw)
        dalog_ref[...] += _rowsum(dda * dt) * a_neg
        ddsk_ref[...] += _rowsum(dd_scr[...])

    col = pl.BlockSpec((N_SSD_HEADS, 1), lambda c: (0, 0))
    bc = pl.BlockSpec((BC_DIM, CHUNK), lambda c: (0, rev(c)))
    xs_spec = pl.BlockSpec((D_INNER, CHUNK), lambda c: (0, rev(c)))
    small = pltpu.VMEM((N_SSD_HEADS, CHUNK), F32)
    return pl.pallas_call(
        body,
        name="ssd_bwd",
        grid=(nc,),
        in_specs=_ssd_specs(rev) + [pl.BlockSpec((1, D_INNER, D_STATE), lambda c: (rev(c), 0, 0)), xs_spec],
        out_specs=[xs_spec, bc, bc, pl.BlockSpec((N_SSD_HEADS, CHUNK), lambda c: (0, rev(c))), col, col, col],
        out_shape=[
            jax.ShapeDtypeStruct((D_INNER, t), F32),
            jax.ShapeDtypeStruct((BC_DIM, t), F32),
            jax.ShapeDtypeStruct((BC_DIM, t), F32),
            jax.ShapeDtypeStruct((N_SSD_HEADS, t), F32),
            jax.ShapeDtypeStruct((N_SSD_HEADS, 1), F32),
            jax.ShapeDtypeStruct((N_SSD_HEADS, 1), F32),
            jax.ShapeDtypeStruct((N_SSD_HEADS, 1), F32),
        ],
        scratch_shapes=[pltpu.VMEM((D_INNER, D_STATE), F32), small, small, small],
        compiler_params=_cparams("arbitrary"),
    )(xbc, xbc, xbc, proj, dtb_col, alog_col, dsk_col, hst, dy)


GN_ROWS = D_INNER // N_SSD_GROUPS


def _gnorm_fwd(y, proj, w_col):
    t = y.shape[1]
    tt = _tile(t, (512, 256, 128))
    z0 = OFF_Z // GN_ROWS

    def body(y_ref, z_ref, w_ref, o_ref):
        zv = z_ref[...]
        u = y_ref[...] * (zv * _sigmoid(zv))
        r = lax.rsqrt(jnp.mean(u * u, axis=0, keepdims=True) + EPS)
        o_ref[...] = (u * r * w_ref[...]).astype(BF16)

    blk = pl.BlockSpec((GN_ROWS, tt), lambda g, i: (g, i))
    return pl.pallas_call(
        body,
        name="gnorm_fwd",
        grid=(N_SSD_GROUPS, t // tt),
        in_specs=[blk, pl.BlockSpec((GN_ROWS, tt), lambda g, i: (z0 + g, i)), pl.BlockSpec((GN_ROWS, 1), lambda g, i: (g, 0))],
        out_specs=blk,
        out_shape=jax.ShapeDtypeStruct((D_INNER, t), BF16),
        compiler_params=_cparams("parallel", "parallel"),
    )(y, proj, w_col)


def _gnorm_bwd(dout, y, proj, w_col, send=None):
    t = y.shape[1]
    tt = _tile(t, (512, 256, 128))
    z0 = OFF_Z // GN_ROWS

    def body(do_ref, y_ref, z_ref, w_ref, dy_ref, dz_ref, dw_ref):
        @pl.when(pl.program_id(1) == 0)
        def _():
            dw_ref[...] = jnp.zeros_like(dw_ref)

        zv = z_ref[...]
        yv = y_ref[...]
        sg = _sigmoid(zv)
        sz = zv * sg
        u = yv * sz
        r = lax.rsqrt(jnp.mean(u * u, axis=0, keepdims=True) + EPS)
        xhat = u * r
        dov = do_ref[...]
        dw_ref[...] += _rowsum(dov * xhat)
        dxhat = dov * w_ref[...]
        du = r * (dxhat - xhat * jnp.mean(dxhat * xhat, axis=0, keepdims=True))
        dy_ref[...] = du * sz
        dz_ref[...] = (du * yv * (sg * (1.0 + zv * (1.0 - sg)))).astype(BF16)

    blk = pl.BlockSpec((GN_ROWS, tt), lambda g, i: (g, i))
    col = pl.BlockSpec((GN_ROWS, 1), lambda g, i: (g, 0))
    return _call(
        body,
        name="gnorm_bwd",
        grid=(N_SSD_GROUPS, t // tt),
        in_specs=[blk, blk, pl.BlockSpec((GN_ROWS, tt), lambda g, i: (z0 + g, i)), col],
        out_specs=[blk, blk, col],
        out_shape=[jax.ShapeDtypeStruct((D_INNER, t), F32), jax.ShapeDtypeStruct((D_INNER, t), BF16),
                   jax.ShapeDtypeStruct((D_INNER, 1), F32)],
        semantics=("parallel", "arbitrary"), args=(dout, y, proj, w_col), send=send)


GATE_ROWS = 128


def _gate_specs(t):
    nr = D_MODEL // GATE_ROWS
    blk = pl.BlockSpec((GATE_ROWS, t), lambda r: (r, 0))
    rows_from = lambda first: pl.BlockSpec(
        (pl.Element(GATE_ROWS), pl.Element(t)), lambda r: (pl.multiple_of(first + GATE_ROWS * r, N_SSD_HEADS), 0))
    return blk, [
        rows_from(OFF_GA),
        rows_from(OFF_GS),
        pl.BlockSpec((GATE_ROWS, 1), lambda r: (r, 0)),
        pl.BlockSpec((GATE_ROWS, 1), lambda r: (nr + r, 0)),
        blk, blk,
    ]


def _gate_fwd(proj, b_col, attn, ssd):
    t = proj.shape[1]
    blk, specs = _gate_specs(t)

    def body(ga_ref, gs_ref, ba_ref, bs_ref, a_ref, s_ref, o_ref):
        o_ref[...] = (_sigmoid(ga_ref[...] + ba_ref[...]) * a_ref[...]
                      + _sigmoid(gs_ref[...] + bs_ref[...]) * s_ref[...]).astype(BF16)

    return pl.pallas_call(
        body,
        name="gate_fwd",
        grid=(D_MODEL // GATE_ROWS,),
        in_specs=specs,
        out_specs=blk,
        out_shape=jax.ShapeDtypeStruct((D_MODEL, t), BF16),
        compiler_params=_cparams("parallel"),
    )(proj, proj, b_col, b_col, attn, ssd)


def _gate_bwd(proj, b_col, attn, ssd, dmix, send=None):
    t = proj.shape[1]
    blk, specs = _gate_specs(t)

    def body(ga_ref, gs_ref, ba_ref, bs_ref, a_ref, s_ref, dm_ref, da_ref, dso_ref, dga_ref, dgs_ref, dba_ref, dbs_ref):
        dm = dm_ref[...]
        sa = _sigmoid(ga_ref[...] + ba_ref[...])
        ss = _sigmoid(gs_ref[...] + bs_ref[...])
        da_ref[...] = (dm * sa).astype(BF16)
        dso_ref[...] = (dm * ss).astype(BF16)
        dga = dm * a_ref[...] * sa * (1.0 - sa)
        dgs = dm * s_ref[...] * ss * (1.0 - ss)
        dga_ref[...] = dga.astype(BF16)
        dgs_ref[...] = dgs.astype(BF16)
        dba_ref[...] = _rowsum(dga)
        dbs_ref[...] = _rowsum(dgs)

    col = pl.BlockSpec((GATE_ROWS, 1), lambda r: (r, 0))
    act = jax.ShapeDtypeStruct((D_MODEL, t), BF16)
    bias = jax.ShapeDtypeStruct((D_MODEL, 1), F32)
    return _call(
        body,
        name="gate_bwd",
        grid=(D_MODEL // GATE_ROWS,),
        in_specs=specs + [blk],
        out_specs=[blk, blk, blk, blk, col, col],
        out_shape=[act, act, act, act, bias, bias],
        semantics=("parallel",), args=(proj, proj, b_col, b_col, attn, ssd, dmix), send=send)


FFN_ROWS = 256


def _ffn_fwd(u0, w_col, b_col):
    t = u0.shape[2]

    def body(u_ref, w_ref, b_ref, o_ref):
        def strip(rows):
            val, _ = _causal_conv(u_ref[0, rows, :], w_ref[0, rows, :], b_ref[0, rows, :])
            gt, _ = _causal_conv(u_ref[1, rows, :], w_ref[1, rows, :], b_ref[1, rows, :])
            o_ref[rows, :] = (gt * _sigmoid(gt) * val).astype(BF16)

        strip(slice(None))

    return pl.pallas_call(
        body,
        name="ffn_fwd",
        grid=(D_FF // FFN_ROWS,),
        in_specs=[
            pl.BlockSpec((2, FFN_ROWS, t), lambda i: (0, i, 0)),
            pl.BlockSpec((2, FFN_ROWS, FFN_CONV), lambda i: (0, i, 0)),
            pl.BlockSpec((2, FFN_ROWS, 1), lambda i: (0, i, 0)),
        ],
        out_specs=pl.BlockSpec((FFN_ROWS, t), lambda i: (i, 0)),
        out_shape=jax.ShapeDtypeStruct((D_FF, t), BF16),
        compiler_params=_cparams("parallel"),
    )(u0, w_col, b_col)


def _ffn_bwd(u0, w_col, b_col, dg, send=None):
    t = u0.shape[2]

    def body(u_ref, w_ref, b_ref, dg_ref, du_ref, dwb_ref):
        def strip(rows):
            xval, wval = u_ref[0, rows, :], w_ref[0, rows, :]
            xgt, wgt = u_ref[1, rows, :], w_ref[1, rows, :]
            val, sh_val = _causal_conv(xval, wval, b_ref[0, rows, :])
            gt, sh_gt = _causal_conv(xgt, wgt, b_ref[1, rows, :])
            sg = _sigmoid(gt)
            dgv = dg_ref[rows, :]
            dval = dgv * (gt * sg)
            dgt = dgv * val * (sg * (1.0 + gt * (1.0 - sg)))
            dx, dwb_ref[0, rows, :] = _causal_conv_bwd(dval, xval, sh_val, wval)
            du_ref[0, rows, :] = dx.astype(BF16)
            dx, dwb_ref[1, rows, :] = _causal_conv_bwd(dgt, xgt, sh_gt, wgt)
            du_ref[1, rows, :] = dx.astype(BF16)

        strip(slice(None))

    return _call(
        body,
        name="ffn_bwd",
        grid=(D_FF // FFN_ROWS,),
        in_specs=[
            pl.BlockSpec((2, FFN_ROWS, t), lambda i: (0, i, 0)),
            pl.BlockSpec((2, FFN_ROWS, FFN_CONV), lambda i: (0, i, 0)),
            pl.BlockSpec((2, FFN_ROWS, 1), lambda i: (0, i, 0)),
            pl.BlockSpec((FFN_ROWS, t), lambda i: (i, 0)),
        ],
        out_specs=[pl.BlockSpec((2, FFN_ROWS, t), lambda i: (0, i, 0)), pl.BlockSpec((2, FFN_ROWS, 128), lambda i: (0, i, 0))],
        out_shape=[jax.ShapeDtypeStruct((2, D_FF, t), BF16), jax.ShapeDtypeStruct((2, D_FF, 128), F32)],
        semantics=("parallel",), args=(u0, w_col, b_col, dg), send=send)


def _adamw_math(w, g, m, v):
    m = ADAM_B1 * m + (1.0 - ADAM_B1) * g
    v = ADAM_B2 * v + (1.0 - ADAM_B2) * (g * g)
    m_hat = m / (1.0 - ADAM_B1 ** ADAM_STEP)
    v_hat = v / (1.0 - ADAM_B2 ** ADAM_STEP)
    delta = -ADAM_LR * (m_hat / (jnp.sqrt(v_hat) + ADAM_EPS) + ADAM_WD * w)
    return delta, m, v


def _adamw_sharded(parts, w, m, v, name):
    r, c = w.shape[0], w.shape[-1]
    tc = _tile(c, (256, 128))
    blk_shape = (r, tc) if w.ndim == 2 else (r, 1, tc)
    slots = parts.shape[0]

    def body(p_ref, w_ref, m_ref, v_ref, g_ref, d_ref, nm_ref, nv_ref):
        g = p_ref[0].astype(F32)
        for s in range(1, slots):
            g = g + p_ref[s].astype(F32)
        flat = lambda ref: ref[...].reshape(r, tc)
        d, nm, nv = _adamw_math(flat(w_ref), g, flat(m_ref), flat(v_ref))
        for ref, val in ((g_ref, g), (d_ref, d), (nm_ref, nm), (nv_ref, nv)):
            ref[...] = val.reshape(blk_shape)

    blk = pl.BlockSpec(blk_shape, (lambda i: (0, i)) if w.ndim == 2 else (lambda i: (0, 0, i)))
    out = jax.ShapeDtypeStruct(w.shape, F32)
    return pl.pallas_call(
        body,
        name=name,
        grid=(c // tc,),
        in_specs=[pl.BlockSpec((slots, r, tc), lambda i: (0, 0, i)), blk, blk, blk],
        out_specs=[blk, blk, blk, blk],
        out_shape=[out, out, out, out],
        compiler_params=_cparams("parallel"),
    )(parts, w, m, v)


def _lane_offsets(sizes):
    offsets, pos = [], 0
    for n in sizes:
        offsets.append(pos)
        pos += -(-n // 128) * 128
    return offsets, pos


def _pack_row(parts):
    rows = [p.reshape(1, -1).astype(F32) for p in parts]
    return jnp.concatenate([jnp.pad(r, ((0, 0), (0, -r.shape[1] % 128))) for r in rows], axis=1)


def _small_update(parts, me, full_sizes, ws, ms, vs):
    n = len(ws)
    offsets, _ = _lane_offsets([1] + list(full_sizes))

    def body(me_ref, p_ref, *refs):
        w_refs, m_refs, v_refs = refs[:n], refs[n:2 * n], refs[2 * n:3 * n]
        scalar_ref, out_refs = refs[3 * n], refs[3 * n + 1:]
        tot = p_ref[0]
        for s in range(1, N_DEV):
            tot = tot + p_ref[s]
        scalar_ref[...] = tot[:, 0:1]
        for k in range(n):
            g_ref, d_ref, nm_ref, nv_ref = out_refs[4 * k:4 * k + 4]
            taps, cols = w_refs[k].shape
            if taps == 1:
                g_ref[...] = tot[:, offsets[k + 1]:offsets[k + 1] + cols]
            else:
                full = full_sizes[k] // taps
                for tap in range(taps):
                    mine = jnp.zeros((1, cols), F32)
                    for d in range(N_DEV):
                        lo = offsets[k + 1] + tap * full + d * cols
                        mine = jnp.where(me_ref[0] == d, tot[:, lo:lo + cols], mine)
                    g_ref[tap:tap + 1, :] = mine
            d_ref[...], nm_ref[...], nv_ref[...] = _adamw_math(w_refs[k][...], g_ref[...], m_refs[k][...], v_refs[k][...])

    vmem = pl.BlockSpec(memory_space=pltpu.VMEM)
    out_shape = [jax.ShapeDtypeStruct((1, 1), F32)]
    for wk in ws:
        out_shape += [jax.ShapeDtypeStruct(wk.shape, F32)] * 4
    res = pl.pallas_call(
        body,
        name="small_update",
        in_specs=[pl.BlockSpec(memory_space=pltpu.SMEM)] + [vmem] * (1 + 3 * n),
        out_specs=[vmem] * len(out_shape),
        out_shape=out_shape,
    )(me, parts, *ws, *ms, *vs)
    return res[0], [res[1 + 4 * k:5 + 4 * k] for k in range(n)]


ANY = pl.BlockSpec(memory_space=pl.ANY)
FLIPS = [(k >> 2 & 1, k >> 1 & 1, k & 1) for k in range(1, N_DEV)]


def _place():
    return lax.axis_index("x"), lax.axis_index("y"), lax.axis_index("c")


HBM = pl.BlockSpec(memory_space=pltpu.HBM)
SEM = pl.BlockSpec(memory_space=pltpu.SEMAPHORE)
EFFECT = pltpu.SideEffectType.DATAFLOW_SIDE_EFFECTING


def _peer_copy(gather, src_ref, land_ref, send_sems, recv_sems, k, sending):
    x, y, c = _place()
    fx, fy, fc = FLIPS[k]
    me = 4 * x + 2 * y + c
    peer = 4 * (x ^ fx) + 2 * (y ^ fy) + (c ^ fc)
    return pltpu.make_async_remote_copy(
        src_ref=src_ref if gather else src_ref.at[peer],
        dst_ref=land_ref.at[me if sending else peer],
        send_sem=send_sems.at[k], recv_sem=recv_sems.at[k],
        device_id=(x ^ fx, y ^ fy, c ^ fc), device_id_type=MESH)


SIBLING = 0
OTHER_CHIPS = (1, 3, 5)


def _gather_start(srcs, name, via_sibling):
    n = len(srcs)
    lands = [lax.empty((N_DEV,) + s.shape, s.dtype) for s in srcs]

    def body(*refs):
        src_refs, land_refs = refs[:n], refs[n:2 * n]
        send, recv = refs[2 * n:3 * n], refs[3 * n:4 * n]
        for i in range(n):
            for k in (SIBLING,) + OTHER_CHIPS if via_sibling else range(N_DEV - 1):
                _peer_copy(True, src_refs[i], land_refs[i], send[i], recv[i], k, True).start()

    sem = pltpu.SemaphoreType.DMA((N_DEV - 1,))
    hbm = lambda a: pltpu.HBM(a.shape, a.dtype)
    res = pl.pallas_call(
        body,
        name=name,
        in_specs=[HBM] * (2 * n),
        out_specs=[SEM] * (2 * n) + [HBM] * (2 * n),
        out_shape=[sem] * (2 * n) + [hbm(s) for s in srcs] + [hbm(a) for a in lands],
        input_output_aliases={i: 2 * n + i for i in range(2 * n)},
        compiler_params=pltpu.CompilerParams(has_side_effects=EFFECT),
    )(*[pltpu.with_memory_space_constraint(a, pltpu.HBM) for a in list(srcs) + lands])
    return res[:n], res[n:2 * n], res[2 * n:3 * n], res[3 * n:4 * n]


def _exchange_wait(send_sems, recv_sems, src, land, after, gather, name):
    def body(src_ref, land_ref, send_ref, recv_ref, after_ref, src_out, land_out):
        for k in range(N_DEV - 1):
            cp = _peer_copy(gather, src_ref, land_ref, send_ref, recv_ref, k, False)
            cp.wait_send()
            cp.wait_recv()

    hbm = lambda a: pltpu.HBM(a.shape, a.dtype)
    return pl.pallas_call(
        body,
        name=name,
        in_specs=[HBM, HBM, SEM, SEM, ANY],
        out_specs=[HBM, HBM],
        out_shape=[hbm(src), hbm(land)],
        input_output_aliases={0: 0, 1: 1},
        compiler_params=pltpu.CompilerParams(has_side_effects=EFFECT),
    )(src, land, send_sems, recv_sems, after)


def _own_slot(src, land, me, gather):
    own = src[None] if gather else lax.dynamic_slice_in_dim(src, me, 1, axis=0)
    return lax.dynamic_update_slice_in_dim(land, own, me, axis=0)


def _forwarded_copy(land_ref, send_sems, recv_sems, j, sending):
    x, y, c = _place()
    fx, fy, _ = FLIPS[OTHER_CHIPS[j]]
    slot = 4 * (x ^ fx) + 2 * (y ^ fy) + (c if sending else 1 - c)
    return pltpu.make_async_remote_copy(
        src_ref=land_ref.at[slot], dst_ref=land_ref.at[slot], send_sem=send_sems.at[j], recv_sem=recv_sems.at[j],
        device_id=(x, y, 1 - c), device_id_type=MESH)


def _gather_forward(send_sems, recv_sems, srcs, lands, after, name):
    n = len(srcs)

    def body(*refs):
        src_refs, land_refs = refs[:n], refs[n:2 * n]
        send, recv = refs[2 * n:3 * n], refs[3 * n:4 * n]
        fwd_send, fwd_recv = refs[4 * n + 1:5 * n + 1], refs[5 * n + 1:6 * n + 1]
        for i in range(n):
            for j, k in enumerate(OTHER_CHIPS):
                _peer_copy(True, src_refs[i], land_refs[i], send[i], recv[i], k, False).wait_recv()
                _forwarded_copy(land_refs[i], fwd_send[i], fwd_recv[i], j, True).start()

    sem = pltpu.SemaphoreType.DMA((len(OTHER_CHIPS),))
    hbm = lambda a: pltpu.HBM(a.shape, a.dtype)
    res = pl.pallas_call(
        body,
        name=name,
        in_specs=[HBM] * (2 * n) + [SEM] * (2 * n) + [ANY],
        out_specs=[SEM] * (2 * n) + [HBM] * (2 * n),
        out_shape=[sem] * (2 * n) + [hbm(a) for a in srcs] + [hbm(a) for a in lands],
        input_output_aliases={i: 2 * n + i for i in range(2 * n)},
        compiler_params=pltpu.CompilerParams(has_side_effects=EFFECT),
    )(*srcs, *lands, *send_sems, *recv_sems, after)
    return res[:n], res[n:2 * n], res[2 * n:3 * n], res[3 * n:4 * n]


def _gather_wait_forwarded(send_sems, recv_sems, fwd_send, fwd_recv, src, land, after, name):
    def body(src_ref, land_ref, send_ref, recv_ref, fwd_send_ref, fwd_recv_ref, after_ref, src_out, land_out):
        for k in (SIBLING,) + OTHER_CHIPS:
            _peer_copy(True, src_ref, land_ref, send_ref, recv_ref, k, False).wait_send()
        _peer_copy(True, src_ref, land_ref, send_ref, recv_ref, SIBLING, False).wait_recv()
        for j in range(len(OTHER_CHIPS)):
            _forwarded_copy(land_ref, fwd_send_ref, fwd_recv_ref, j, True).wait_send()
            _forwarded_copy(land_ref, fwd_send_ref, fwd_recv_ref, j, False).wait_recv()

    hbm = lambda a: pltpu.HBM(a.shape, a.dtype)
    return pl.pallas_call(
        body,
        name=name,
        in_specs=[HBM, HBM, SEM, SEM, SEM, SEM, ANY],
        out_specs=[HBM, HBM],
        out_shape=[hbm(src), hbm(land)],
        input_output_aliases={0: 0, 1: 1},
        compiler_params=pltpu.CompilerParams(has_side_effects=EFFECT),
    )(src, land, send_sems, recv_sems, fwd_send, fwd_recv, after)


N_CHIPS = N_DEV // 2


def _pair_exchange(by_core, name):
    def copy(src_ref, land_ref, send_sems, recv_sems, q):
        x, y, c = _place()
        return pltpu.make_async_remote_copy(
            src_ref=src_ref.at[q, 1 - c], dst_ref=land_ref.at[q], send_sem=send_sems.at[q], recv_sem=recv_sems.at[q],
            device_id=(x, y, 1 - c), device_id_type=MESH)

    def start(src_ref, land_ref, send_sems, recv_sems, src_out, land_out):
        for q in range(N_CHIPS):
            copy(src_ref, land_ref, send_sems, recv_sems, q).start()

    def wait(src_ref, land_ref, send_sems, recv_sems, src_out, land_out):
        for q in range(N_CHIPS):
            cp = copy(src_ref, land_ref, send_sems, recv_sems, q)
            cp.wait_send()
            cp.wait_recv()

    sem = pltpu.SemaphoreType.DMA((N_CHIPS,))
    hbm_src = pltpu.HBM(by_core.shape, by_core.dtype)
    hbm_land = pltpu.HBM(by_core.shape[:1] + by_core.shape[2:], by_core.dtype)
    params = pltpu.CompilerParams(has_side_effects=EFFECT)
    send_sems, recv_sems, src, land = pl.pallas_call(
        start, name=name + "_start", in_specs=[HBM, HBM], out_specs=[SEM, SEM, HBM, HBM],
        out_shape=[sem, sem, hbm_src, hbm_land], input_output_aliases={0: 2, 1: 3}, compiler_params=params,
    )(pltpu.with_memory_space_constraint(by_core, pltpu.HBM),
      pltpu.with_memory_space_constraint(lax.empty(hbm_land.shape, by_core.dtype), pltpu.HBM))
    return pl.pallas_call(
        wait, name=name + "_wait", in_specs=[HBM, HBM, SEM, SEM], out_specs=[HBM, HBM], out_shape=[hbm_src, hbm_land],
        input_output_aliases={0: 0, 1: 1}, compiler_params=params,
    )(src, land, send_sems, recv_sems)


def _pair_add(by_core, landed, name):
    q, _, r, c = by_core.shape
    tc = _tile(c, (512, 256, 128))

    def body(a_ref, b_ref, o_ref):
        mine = a_ref[0, lax.axis_index("c")]
        o_ref[0] = (mine.astype(F32) + b_ref[0].astype(F32)).astype(BF16)

    blk = pl.BlockSpec((1, r, tc), lambda i, j: (i, 0, j))
    return pl.pallas_call(
        body, name=name, grid=(q, c // tc),
        in_specs=[pl.BlockSpec((1, 2, r, tc), lambda i, j: (i, 0, 0, j)), blk], out_specs=blk,
        out_shape=jax.ShapeDtypeStruct(landed.shape, BF16), compiler_params=_cparams("parallel", "parallel"),
    )(by_core, landed)


def _chip_copy(src_ref, land_ref, send_sems, recv_sems, j, sending):
    x, y, c = _place()
    fx, fy, _ = FLIPS[OTHER_CHIPS[j]]
    here, there = 2 * x + y, 2 * (x ^ fx) + (y ^ fy)
    return pltpu.make_async_remote_copy(
        src_ref=src_ref.at[there], dst_ref=land_ref.at[here if sending else there],
        send_sem=send_sems.at[j], recv_sem=recv_sems.at[j],
        device_id=(x ^ fx, y ^ fy, c), device_id_type=MESH)


def _chip_wait(send_sems, recv_sems, src, land, after, name):
    def body(src_ref, land_ref, send_ref, recv_ref, after_ref, src_out, land_out):
        for j in range(len(OTHER_CHIPS)):
            cp = _chip_copy(src_ref, land_ref, send_ref, recv_ref, j, False)
            cp.wait_send()
            cp.wait_recv()

    hbm = lambda a: pltpu.HBM(a.shape, a.dtype)
    return pl.pallas_call(
        body,
        name=name,
        in_specs=[HBM, HBM, SEM, SEM, ANY],
        out_specs=[HBM, HBM],
        out_shape=[hbm(src), hbm(land)],
        input_output_aliases={0: 0, 1: 1},
        compiler_params=pltpu.CompilerParams(has_side_effects=EFFECT),
    )(src, land, send_sems, recv_sems, after)


def _col(v):
    return v.reshape(-1, 1).astype(F32)


def _local_step(x, tgt, started, weight, small, pair_sums):
    t = x.shape[0]
    n1 = _col(small["norm1_w"])
    n2 = _col(small["norm2_w"])
    nf = _col(small["final_norm_w"])
    bg = _col(small["b_gate"])
    sinks = small["attn_sinks"].reshape(-1).astype(F32)
    cbias = _col(small["ssd_conv_b"])
    dtb = _col(small["dt_bias"])
    alog = _col(small["a_log"])
    dsk = _col(small["d_skip"])
    gnw = _col(small["ssd_norm_w"])
    fb = small["ffn_conv_b"].reshape(2, D_FF, 1)

    xt, xn = _norm_fwd_tokens(x, n1, started, "norm1_fwd")
    cw = weight("ssd_conv_w", xn).T
    fw = weight("ffn_conv_w", xn).T.reshape(2, D_FF, FFN_CONV)
    w_in_t = weight("w_in", xn)
    proj = _matmul(w_in_t, xn, nt=False, out_dtype=F32, name="mm_in")
    ao, lse = _attn_fwd(proj, sinks)
    w_ao = weight("w_attn_o", ao)
    attn = _matmul(w_ao, ao, nt=False, out_dtype=F32, name="mm_attn_o", tn_a=True)
    xbc = _conv_silu_fwd(proj, cw, cbias)
    y, hst = _ssd_fwd(xbc, proj, dtb, alog, dsk)
    yn = _gnorm_fwd(y, proj, gnw)
    w_so = weight("w_ssd_o", yn)
    ssd = _matmul(w_so, yn, nt=False, out_dtype=F32, name="mm_ssd_o", tn_a=True)
    mix = _gate_fwd(proj, bg, attn, ssd)
    w_out = weight("w_out", mix)
    h1 = _matmul(w_out, mix, nt=False, out_dtype=F32, name="mm_out", add=xt, tn_a=True)
    hn = _norm_fwd(h1, n2, "norm2_fwd")
    w_up_t = weight("w_up", hn)
    u0 = _matmul(w_up_t, hn, nt=False, out_dtype=F32, name="mm_up").reshape(2, D_FF, t)
    gl = _ffn_fwd(u0, fw, fb)
    w_down = weight("w_down", gl)
    h2 = _matmul(w_down, gl, nt=False, out_dtype=F32, name="mm_down", add=h1, tn_a=True)
    dh2, loss, d_nf = _final_norm_loss(h2, tgt, nf)

    g = {}
    handles = {}

    def sending(weight_name, grad, fn, *args, **kwargs):
        chunks = grad if grad.ndim == 3 else grad.reshape(N_DEV, -1, D_MODEL)
        out, handles[weight_name] = fn(*args, send=chunks, **kwargs)
        return out

    g_down = _matmul(gl, dh2, nt=True, out_dtype=BF16, name="mm_d_w_down")
    dgl = _matmul(w_down, dh2, nt=False, out_dtype=F32, name="mm_d_glu")
    du0, d_fwb = sending("w_down", g_down, _ffn_bwd, u0, fw, fb, dgl)
    du0 = du0.reshape(2 * D_FF, t)
    g_up = _matmul(du0, hn, nt=True, out_dtype=BF16, name="mm_d_w_up")
    dhn = sending("w_up", g_up, _matmul, w_up_t, du0, nt=False, out_dtype=F32, name="mm_d_hn", tn_a=True)
    dh1, d_n2 = _norm_bwd(dhn, h1, n2, dh2, "norm2_bwd")
    g_out = _matmul(mix, dh1, nt=True, out_dtype=BF16, name="mm_d_w_out")
    dmix = _matmul(w_out, dh1, nt=False, out_dtype=F32, name="mm_d_mix")
    d_attn, d_ssd, d_ga, d_gs, d_ba, d_bs = sending("w_out", g_out, _gate_bwd, proj, bg, attn, ssd, dmix)
    g_ao = _matmul(ao, d_attn, nt=True, out_dtype=BF16, name="mm_d_w_attn_o")
    dao = _matmul(w_ao, d_attn, nt=False, out_dtype=F32, name="mm_d_ao")
    dq, dk, dv, d_sinks = sending("w_attn_o", g_ao, _attn_bwd, proj, sinks, ao, lse, dao)
    g_so = _matmul(yn, d_ssd, nt=True, out_dtype=BF16, name="mm_d_w_ssd_o")
    dyn = _matmul(w_so, d_ssd, nt=False, out_dtype=F32, name="mm_d_yn")
    dy, dz, d_gnw = sending("w_ssd_o", g_so, _gnorm_bwd, dyn, y, proj, gnw)
    dxs, dbm, dcm, ddt, d_alog, d_dsk, d_dtb = _ssd_bwd(xbc, proj, dtb, alog, dsk, hst, dy)
    dx_xs, dwb_xs = _conv_silu_bwd(proj, cw, cbias, dxs, 0, "ssd_conv_bwd_x")
    dx_b, dwb_b = _conv_silu_bwd(proj, cw, cbias, dbm, D_INNER, "ssd_conv_bwd_b")
    dx_c, dwb_c = _conv_silu_bwd(proj, cw, cbias, dcm, D_INNER + BC_DIM, "ssd_conv_bwd_c")
    dwb_conv = jnp.concatenate([dwb_xs, dwb_b, dwb_c], axis=0)
    dproj = jnp.concatenate([dq, dk, dv, dz, dx_xs, dx_b, dx_c, ddt.astype(BF16), d_ga, d_gs], axis=0)
    g_in = pair_sums(_matmul(dproj, xn, nt=True, out_dtype=BF16, name="mm_d_w_in"))
    dxn = sending("w_in", g_in, _matmul, w_in_t, dproj, nt=False, out_dtype=F32, name="mm_d_xn", tn_a=True)
    dx, d_n1 = _norm_bwd(dxn, xt, n1, dh1, "norm1_bwd", tokens_out=True)

    g["norm1_w"] = d_n1
    g["b_gate"] = jnp.concatenate([d_ba, d_bs], axis=0)
    g["attn_sinks"] = d_sinks
    g["ssd_conv_w"] = dwb_conv[:, :SSD_CONV].T
    g["ssd_conv_b"] = dwb_conv[:, SSD_CONV]
    g["dt_bias"] = d_dtb
    g["a_log"] = d_alog
    g["d_skip"] = d_dsk
    g["ssd_norm_w"] = d_gnw
    g["norm2_w"] = d_n2
    d_fwb = d_fwb.reshape(2 * D_FF, 128)
    g["ffn_conv_w"] = d_fwb[:, :FFN_CONV].T
    g["ffn_conv_b"] = d_fwb[:, FFN_CONV]
    g["final_norm_w"] = d_nf
    return loss, dx, g, handles


SMALL = ("norm1_w", "b_gate", "attn_sinks", "ssd_conv_w", "ssd_conv_b", "dt_bias", "a_log", "d_skip", "ssd_norm_w",
         "norm2_w", "ffn_conv_w", "ffn_conv_b", "final_norm_w")
WEIGHT_ORDER = ("norm1_w", "w_in", "b_gate", "attn_sinks", "w_attn_o", "ssd_conv_w", "ssd_conv_b", "dt_bias", "a_log",
                "d_skip", "ssd_norm_w", "w_ssd_o", "w_out", "norm2_w", "w_up", "ffn_conv_w", "ffn_conv_b", "w_down",
                "final_norm_w")


def kernel(x, norm1_w, w_in, b_gate, attn_sinks, w_attn_o, ssd_conv_w, ssd_conv_b, dt_bias, a_log, d_skip, ssd_norm_w, w_ssd_o, w_out, norm2_w, w_up, ffn_conv_w, ffn_conv_b, w_down, final_norm_w, loss_target, m_norm1_w, m_w_in, m_b_gate, m_attn_sinks, m_w_attn_o, m_ssd_conv_w, m_ssd_conv_b, m_dt_bias, m_a_log, m_d_skip, m_ssd_norm_w, m_w_ssd_o, m_w_out, m_norm2_w, m_w_up, m_ffn_conv_w, m_ffn_conv_b, m_w_down, m_final_norm_w, v_norm1_w, v_w_in, v_b_gate, v_attn_sinks, v_w_attn_o, v_ssd_conv_w, v_ssd_conv_b, v_dt_bias, v_a_log, v_d_skip, v_ssd_norm_w, v_w_ssd_o, v_w_out, v_norm2_w, v_w_up, v_ffn_conv_w, v_ffn_conv_b, v_w_down, v_final_norm_w):
    w = dict(norm1_w=norm1_w, w_in=w_in, b_gate=b_gate, attn_sinks=attn_sinks, w_attn_o=w_attn_o, ssd_conv_w=ssd_conv_w, ssd_conv_b=ssd_conv_b, dt_bias=dt_bias, a_log=a_log, d_skip=d_skip, ssd_norm_w=ssd_norm_w, w_ssd_o=w_ssd_o, w_out=w_out, norm2_w=norm2_w, w_up=w_up, ffn_conv_w=ffn_conv_w, ffn_conv_b=ffn_conv_b, w_down=w_down, final_norm_w=final_norm_w)
    m = dict(norm1_w=m_norm1_w, w_in=m_w_in, b_gate=m_b_gate, attn_sinks=m_attn_sinks, w_attn_o=m_w_attn_o, ssd_conv_w=m_ssd_conv_w, ssd_conv_b=m_ssd_conv_b, dt_bias=m_dt_bias, a_log=m_a_log, d_skip=m_d_skip, ssd_norm_w=m_ssd_norm_w, w_ssd_o=m_w_ssd_o, w_out=m_w_out, norm2_w=m_norm2_w, w_up=m_w_up, ffn_conv_w=m_ffn_conv_w, ffn_conv_b=m_ffn_conv_b, w_down=m_w_down, final_norm_w=m_final_norm_w)
    v = dict(norm1_w=v_norm1_w, w_in=v_w_in, b_gate=v_b_gate, attn_sinks=v_attn_sinks, w_attn_o=v_w_attn_o, ssd_conv_w=v_ssd_conv_w, ssd_conv_b=v_ssd_conv_b, dt_bias=v_dt_bias, a_log=v_a_log, d_skip=v_d_skip, ssd_norm_w=v_ssd_norm_w, w_ssd_o=v_w_ssd_o, w_out=v_w_out, norm2_w=v_norm2_w, w_up=v_w_up, ffn_conv_w=v_ffn_conv_w, ffn_conv_b=v_ffn_conv_b, w_down=v_w_down, final_norm_w=v_final_norm_w)
    me = 4 * lax.axis_index("x") + 2 * lax.axis_index("y") + lax.axis_index("c")

    shards = {"ssd_conv_w": ssd_conv_w[0], "ffn_conv_w": ffn_conv_w[0], "w_in": w_in[0].T.astype(BF16),
              "w_attn_o": w_attn_o[0].astype(BF16), "w_ssd_o": w_ssd_o[0].astype(BF16), "w_out": w_out[0].astype(BF16),
              "w_up": w_up[0].T.astype(BF16), "w_down": w_down[0].astype(BF16)}
    order = list(shards)
    g_send, g_recv, g_src, g_land = _gather_start(list(shards.values()), "gather_start", True)
    first = ("ssd_conv_w", "ffn_conv_w", "w_in")
    forwarded = {}

    def weight(name, after):
        if name not in forwarded:
            group = [k for k in order if (k in first) == (name in first)]
            idx = [order.index(k) for k in group]
            handles = _gather_forward([g_send[i] for i in idx], [g_recv[i] for i in idx], [g_src[i] for i in idx],
                                      [g_land[i] for i in idx], after, "gather_forward_for_" + name)
            forwarded.update(zip(group, zip(*handles)))
        i = order.index(name)
        src, land = _gather_wait_forwarded(g_send[i], g_recv[i], *forwarded[name], after, "gather_wait_" + name)
        land = _own_slot(src, land, me, True)
        if name == "ssd_conv_w":
            return jnp.transpose(land, (1, 0, 2)).reshape(SSD_CONV, XBC_DIM)
        if name == "ffn_conv_w":
            return jnp.transpose(land, (1, 0, 2)).reshape(FFN_CONV, 2 * D_FF)
        return land.reshape(-1, D_MODEL)

    def pair_sums(grad):
        by_core, landed = _pair_exchange(grad.reshape(N_CHIPS, 2, -1, D_MODEL), "grad_pair_w_in")
        return _pair_add(by_core, landed, "grad_pair_add_w_in")

    small = {k: w[k][0] if k != "final_norm_w" else w[k] for k in SMALL}
    loss, dx, g, pending = _local_step(x[0], loss_target[0], g_src[0], weight, small, pair_sums)

    packed = _pack_row([loss] + [g[k] for k in SMALL])
    s_send, s_recv, s_src, s_land = _gather_start([packed], "small_grads_start", False)

    res = {}
    after = s_src[0]
    for name in ("w_down", "w_up", "w_out", "w_attn_o", "w_ssd_o", "w_in"):
        if name == "w_in":
            parts = _own_slot(*_chip_wait(*pending[name], after, "grad_wait_" + name), me // 2, False)
        else:
            parts = _own_slot(*_exchange_wait(*pending[name], after, False, "grad_wait_" + name), me, False)
        view, back = {
            "w_in": (lambda a: jnp.transpose(a, (2, 0, 1)), lambda r: jnp.transpose(r, (1, 2, 0))),
            "w_up": (lambda a: a[0].T, lambda r: r.T[None]),
        }.get(name, (lambda a: a[0], lambda r: r[None]))
        res[name] = _adamw_sharded(parts, view(w[name]), view(m[name]), view(v[name]), "adamw_" + name)
        after = res[name][0]
        res[name] = [back(r) for r in res[name]]

    rows = _own_slot(*_exchange_wait(s_send[0], s_recv[0], s_src[0], s_land[0], after, True, "small_grads_wait"),
                     me, True)
    flat = lambda a: a.reshape(-1, a.shape[-1])
    loss_sum, updates = _small_update(
        rows, me.reshape(1), [g[k].size for k in SMALL],
        [flat(w[k]) for k in SMALL], [flat(m[k]) for k in SMALL], [flat(v[k]) for k in SMALL])
    for k, upd in zip(SMALL, updates):
        res[k] = [u.reshape(w[k].shape) for u in upd]

    grad_x = dx[None]
    outs = [loss_sum.reshape(()), grad_x]
    for i in range(4):
        outs.extend(res[k][i] for k in WEIGHT_ORDER)
    return tuple(outs)
```

```python
import functools

import jax
import jax.numpy as jnp
from jax import lax
from jax.experimental import pallas as pl
from jax.experimental.pallas import tpu as pltpu

F32 = jnp.float32
BF16 = jnp.bfloat16
HIGHEST = lax.Precision.HIGHEST

D_MODEL = 1024
N_Q_HEADS = 16
N_KV_HEADS = 4
HEAD_DIM = 64
WINDOW = 128
Q_PER_KV = N_Q_HEADS // N_KV_HEADS
Q_DIM = N_Q_HEADS * HEAD_DIM
KV_DIM = N_KV_HEADS * HEAD_DIM
D_INNER = 2048
SSD_HEAD_DIM = 64
N_SSD_HEADS = 32
N_SSD_GROUPS = 4
HEADS_PER_GROUP = N_SSD_HEADS // N_SSD_GROUPS
D_STATE = 128
BC_DIM = N_SSD_GROUPS * D_STATE
XBC_DIM = D_INNER + 2 * BC_DIM
SSD_CONV = 4
CHUNK = 128
D_FF = 2816
FFN_CONV = 3
EPS = 1e-5
NEG = -1e30
IN_DIM = 8736
N_DEV = 8

OFF_Q = 0
OFF_K = OFF_Q + Q_DIM
OFF_V = OFF_K + KV_DIM
OFF_Z = OFF_V + KV_DIM
OFF_X = OFF_Z + D_INNER
OFF_DT = OFF_X + XBC_DIM
OFF_GA = OFF_DT + N_SSD_HEADS
OFF_GS = OFF_GA + D_MODEL

ADAM_LR = 0.001
ADAM_B1 = 0.9
ADAM_B2 = 0.999
ADAM_EPS = 1e-08
ADAM_WD = 0.01
ADAM_STEP = 10

VMEM_LIMIT = 48 * 1024 * 1024
MESH = pl.DeviceIdType.MESH


def _cparams(*sem):
    return pltpu.CompilerParams(dimension_semantics=sem, vmem_limit_bytes=VMEM_LIMIT)


def _tile(n, prefs):
    for p in prefs:
        if n % p == 0:
            return p
    return n


def _sigmoid(x):
    return 1.0 / (1.0 + jnp.exp(-x))


def _softplus(x):
    return jnp.maximum(x, 0.0) + jnp.log(1.0 + jnp.exp(-jnp.abs(x)))


def _rowsum(x):
    return jnp.sum(x, axis=1, keepdims=True)


def _colsum(x):
    return jnp.sum(x, axis=0, keepdims=True)


def _dot(a, b):
    return jnp.dot(a, b, preferred_element_type=F32)


def _dot_nt(a, b):
    return lax.dot_general(a, b, (((1,), (1,)), ((), ())), preferred_element_type=F32)


def _dot_tn(a, b):
    return lax.dot_general(a, b, (((0,), (0,)), ((), ())), preferred_element_type=F32)


def _shift_right(x, j):
    if j == 0:
        return x
    r = pltpu.roll(x, j, 1)
    lane = lax.broadcasted_iota(jnp.int32, (x.shape[0], 128), 1)
    return jnp.concatenate([jnp.where(lane >= j, r[:, :128], 0.0), r[:, 128:]], axis=1)


def _shift_left(x, j):
    if j == 0:
        return x
    n = x.shape[1]
    r = pltpu.roll(x, n - j, 1)
    lane = lax.broadcasted_iota(jnp.int32, (x.shape[0], 128), 1)
    return jnp.concatenate([r[:, :n - 128], jnp.where(lane < 128 - j, r[:, n - 128:], 0.0)], axis=1)


def _causal_conv(xv, wv, bv):
    taps = wv.shape[1]
    shifted = [_shift_right(xv, taps - 1 - k) for k in range(taps - 1)]
    y = bv + wv[:, taps - 1:taps] * xv
    for k in range(taps - 1):
        y = y + wv[:, k:k + 1] * shifted[k]
    return y, shifted


def _causal_conv_bwd(dy, xv, shifted, wv):
    taps = wv.shape[1]
    lane = lax.broadcasted_iota(jnp.int32, (dy.shape[0], 128), 1)
    dwb = jnp.where(lane == taps, _rowsum(dy), 0.0)
    dwb = jnp.where(lane == taps - 1, _rowsum(dy * xv), dwb)
    dx = wv[:, taps - 1:taps] * dy
    for k in range(taps - 1):
        dx = dx + wv[:, k:k + 1] * _shift_left(dy, taps - 1 - k)
        dwb = jnp.where(lane == k, _rowsum(dy * shifted[k]), dwb)
    return dx, dwb


def _call(body, *, name, grid, in_specs, out_specs, out_shape, args, semantics, scratch_shapes=(), send=None):
    if send is None:
        return pl.pallas_call(body, name=name, grid=grid, in_specs=in_specs, out_specs=out_specs, out_shape=out_shape,
                              scratch_shapes=list(scratch_shapes), compiler_params=_cparams(*semantics))(*args)
    single = not isinstance(out_specs, (list, tuple))
    out_specs, out_shape = ([out_specs], [out_shape]) if single else (list(out_specs), list(out_shape))
    n_in, n_out = len(in_specs), len(out_specs)
    chips = send.shape[0] == N_DEV // 2
    n_copies = len(OTHER_CHIPS) if chips else N_DEV - 1

    def sending(*refs):
        ins, (src_ref, land_ref) = refs[:n_in], refs[n_in:n_in + 2]
        outs = refs[n_in + 2:n_in + 2 + n_out]
        send_sems, recv_sems = refs[n_in + 2 + n_out:n_in + 4 + n_out]
        scratch = refs[n_in + 6 + n_out:]
        step = 0
        for axis, size in enumerate(grid):
            step = step * size + pl.program_id(axis)

        @pl.when(step == 0)
        def _():
            for k in range(n_copies):
                if chips:
                    _chip_copy(src_ref, land_ref, send_sems, recv_sems, k, True).start()
                else:
                    _peer_copy(False, src_ref, land_ref, send_sems, recv_sems, k, True).start()

        body(*ins, *outs, *scratch)

    sem = pltpu.SemaphoreType.DMA((n_copies,))
    hbm = pltpu.HBM(send.shape, send.dtype)
    res = pl.pallas_call(
        sending, name=name, grid=grid,
        in_specs=list(in_specs) + [HBM, HBM],
        out_specs=out_specs + [SEM, SEM, HBM, HBM],
        out_shape=out_shape + [sem, sem, hbm, hbm],
        input_output_aliases={n_in: n_out + 2, n_in + 1: n_out + 3},
        scratch_shapes=list(scratch_shapes),
        compiler_params=pltpu.CompilerParams(dimension_semantics=("arbitrary",) * len(grid), vmem_limit_bytes=VMEM_LIMIT,
                                             has_side_effects=EFFECT),
    )(*args, pltpu.with_memory_space_constraint(send, pltpu.HBM),
      pltpu.with_memory_space_constraint(lax.empty(send.shape, send.dtype), pltpu.HBM))
    return (res[0] if single else list(res[:n_out])), tuple(res[n_out:])


MATMUL_VMEM_BUDGET = 36 * 1024 * 1024
MATMUL_MAX_TK = 3072


MATMUL_MAX_TM = 768


def _largest_tile(n, align, cap):
    return max(d for d in range(align, min(n, cap) + 1, align) if n % d == 0)


def _matmul_tiles(m, n, k, a_bytes, b_bytes, out_bytes, has_add, m_align, k_align):
    tm = _largest_tile(m, m_align, MATMUL_MAX_TM)
    tk = _largest_tile(k, k_align, MATMUL_MAX_TK)
    for tn in sorted({d for d in range(128, n + 1, 128) if n % d == 0}, reverse=True):
        need = 2 * (tm * tk * a_bytes + tk * tn * b_bytes) + tm * tn * (2 * out_bytes + (4 if k > tk else 0) + (8 if has_add else 0))
        if tn <= 3072 and need <= MATMUL_VMEM_BUDGET:
            return tm, tn, tk
    return tm, 128, tk


def _matmul(a, b, *, nt, out_dtype, name, add=None, tn_a=False, send=None):
    if tn_a:
        k, m = a.shape
    else:
        m, k = a.shape
    n = b.shape[0] if nt else b.shape[1]
    tm, tn, tk = _matmul_tiles(m, n, k, a.dtype.itemsize, b.dtype.itemsize, jnp.dtype(out_dtype).itemsize, add is not None,
                               128 if tn_a else 16, 16 if tn_a and not nt else 128)
    nk = k // tk
    grid = (m // tm, n // tn, nk)

    def body(a_ref, b_ref, *rest):
        r_ref = None
        if add is not None:
            r_ref, rest = rest[0], rest[1:]
        o_ref = rest[0]
        av = a_ref[...].astype(BF16)
        bv = b_ref[...].astype(BF16)
        part = _dot_tn(av, bv) if tn_a else _dot_nt(av, bv) if nt else _dot(av, bv)

        def finish(r):
            if add is not None:
                r = r + r_ref[...]
            o_ref[...] = r.astype(out_dtype)

        if nk == 1:
            finish(part)
            return
        acc = rest[1]
        kk = pl.program_id(2)

        @pl.when(kk == 0)
        def _():
            acc[...] = part

        @pl.when((kk > 0) & (kk < nk - 1))
        def _():
            acc[...] += part

        @pl.when(kk == nk - 1)
        def _():
            finish(acc[...] + part)

    in_specs = [
        pl.BlockSpec((tk, tm), lambda i, j, kk: (kk, i)) if tn_a else pl.BlockSpec((tm, tk), lambda i, j, kk: (i, kk)),
        pl.BlockSpec((tn, tk), lambda i, j, kk: (j, kk)) if nt else pl.BlockSpec((tk, tn), lambda i, j, kk: (kk, j)),
    ]
    args = [a, b]
    if add is not None:
        in_specs.append(pl.BlockSpec((tm, tn), lambda i, j, kk: (i, j)))
        args.append(add)
    return _call(
        body, name=name, grid=grid, in_specs=in_specs, args=args,
        out_specs=pl.BlockSpec((tm, tn), lambda i, j, kk: (i, j)),
        out_shape=jax.ShapeDtypeStruct((m, n), out_dtype),
        scratch_shapes=[pltpu.VMEM((tm, tn), F32)] if nk > 1 else [],
        semantics=("parallel", "parallel", "arbitrary"), send=send)


def _norm_fwd(x, w_col, name):
    f, t = x.shape
    tt = _tile(t, (512, 256, 128))

    def body(x_ref, w_ref, o_ref):
        xv = x_ref[...]
        r = lax.rsqrt(jnp.mean(xv * xv, axis=0, keepdims=True) + EPS)
        o_ref[...] = (xv * r * w_ref[...]).astype(BF16)

    return pl.pallas_call(
        body,
        name=name,
        grid=(t // tt,),
        in_specs=[pl.BlockSpec((f, tt), lambda i: (0, i)), pl.BlockSpec((f, 1), lambda i: (0, 0))],
        out_specs=pl.BlockSpec((f, tt), lambda i: (0, i)),
        out_shape=jax.ShapeDtypeStruct((f, t), BF16),
        compiler_params=_cparams("parallel"),
    )(x, w_col)


def _norm_fwd_tokens(x, w_col, after, name):
    t, f = x.shape
    tt = _tile(t, (512, 256, 128))

    def body(x_ref, w_ref, after_ref, xt_ref, o_ref):
        xv = x_ref[...].T
        xt_ref[...] = xv
        r = lax.rsqrt(jnp.mean(xv * xv, axis=0, keepdims=True) + EPS)
        o_ref[...] = (xv * r * w_ref[...]).astype(BF16)

    blk = pl.BlockSpec((f, tt), lambda i: (0, i))
    return pl.pallas_call(
        body,
        name=name,
        grid=(t // tt,),
        in_specs=[pl.BlockSpec((tt, f), lambda i: (i, 0)), pl.BlockSpec((f, 1), lambda i: (0, 0)), ANY],
        out_specs=[blk, blk],
        out_shape=[jax.ShapeDtypeStruct((f, t), F32), jax.ShapeDtypeStruct((f, t), BF16)],
        compiler_params=_cparams("parallel"),
    )(x, w_col, after)


def _norm_bwd(dy, x, w_col, res, name, tokens_out=False):
    f, t = x.shape
    tt = _tile(t, (512, 256, 128))

    def body(dy_ref, x_ref, w_ref, res_ref, dx_ref, dw_ref):
        @pl.when(pl.program_id(0) == 0)
        def _():
            dw_ref[...] = jnp.zeros_like(dw_ref)

        xv = x_ref[...]
        r = lax.rsqrt(jnp.mean(xv * xv, axis=0, keepdims=True) + EPS)
        xhat = xv * r
        dyv = dy_ref[...]
        dw_ref[...] += _rowsum(dyv * xhat)
        dxhat = dyv * w_ref[...]
        dx = res_ref[...] + r * (dxhat - xhat * jnp.mean(dxhat * xhat, axis=0, keepdims=True))
        dx_ref[...] = dx.T if tokens_out else dx

    blk = pl.BlockSpec((f, tt), lambda i: (0, i))
    col = pl.BlockSpec((f, 1), lambda i: (0, 0))
    return pl.pallas_call(
        body,
        name=name,
        grid=(t // tt,),
        in_specs=[blk, blk, col, blk],
        out_specs=[pl.BlockSpec((tt, f), lambda i: (i, 0)) if tokens_out else blk, col],
        out_shape=[jax.ShapeDtypeStruct((t, f) if tokens_out else (f, t), F32), jax.ShapeDtypeStruct((f, 1), F32)],
        compiler_params=_cparams("arbitrary"),
    )(dy, x, w_col, res)


def _final_norm_loss(h, tgt, w_col):
    f, t = h.shape
    tt = _tile(t, (512, 256, 128))

    def body(h_ref, t_ref, w_ref, dh_ref, loss_ref, dw_ref):
        @pl.when(pl.program_id(0) == 0)
        def _():
            dw_ref[...] = jnp.zeros_like(dw_ref)
            loss_ref[...] = jnp.zeros_like(loss_ref)

        xv = h_ref[...]
        r = lax.rsqrt(jnp.mean(xv * xv, axis=0, keepdims=True) + EPS)
        xhat = xv * r
        wv = w_ref[...]
        err = xhat * wv - t_ref[...].T
        loss_ref[...] += 0.5 * _rowsum(jnp.mean(err * err, axis=0, keepdims=True))
        dyv = err * (1.0 / f)
        dw_ref[...] += _rowsum(dyv * xhat)
        dxhat = dyv * wv
        dh_ref[...] = r * (dxhat - xhat * jnp.mean(dxhat * xhat, axis=0, keepdims=True))

    blk = pl.BlockSpec((f, tt), lambda i: (0, i))
    col = pl.BlockSpec((f, 1), lambda i: (0, 0))
    one = pl.BlockSpec((1, 1), lambda i: (0, 0))
    return pl.pallas_call(
        body,
        name="final_norm_loss",
        grid=(t // tt,),
        in_specs=[blk, pl.BlockSpec((tt, f), lambda i: (i, 0)), col],
        out_specs=[blk, one, col],
        out_shape=[jax.ShapeDtypeStruct((f, t), F32), jax.ShapeDtypeStruct((1, 1), F32), jax.ShapeDtypeStruct((f, 1), F32)],
        compiler_params=_cparams("arbitrary"),
    )(h, tgt, w_col)


def _attn_mask(n):
    shape = (2 * WINDOW, Q_PER_KV * WINDOW)
    si = lax.broadcasted_iota(jnp.int32, shape, 0)
    qi = lax.broadcasted_iota(jnp.int32, shape, 1) & (WINDOW - 1)
    dist = WINDOW + qi - si
    return (dist >= 0) & (dist < WINDOW) & ((si >= WINDOW) | (n > 0))


def _lane_cat(ref, row0, rows):
    return jnp.concatenate([ref[row0 + i * rows:row0 + (i + 1) * rows, :] for i in range(Q_PER_KV)], axis=1)


def _attn_fwd(proj, sinks):
    t = proj.shape[1]
    nb = t // WINDOW
    scale = HEAD_DIM ** -0.5

    def body(s_ref, q_ref, kc_ref, kp_ref, vc_ref, vp_ref, o_ref, lse_ref):
        n = pl.program_id(0)
        valid = _attn_mask(n)
        for g in range(N_KV_HEADS):
            rows = slice(g * HEAD_DIM, (g + 1) * HEAD_DIM)
            kt = jnp.concatenate([kp_ref[rows, :], kc_ref[rows, :]], axis=1).astype(BF16)
            vt = jnp.concatenate([vp_ref[rows, :], vc_ref[rows, :]], axis=1).astype(BF16)
            qcat = (_lane_cat(q_ref, g * Q_PER_KV * HEAD_DIM, HEAD_DIM) * scale).astype(BF16)
            s = jnp.where(valid, _dot_tn(kt, qcat), NEG)
            sink = jnp.concatenate(
                [jnp.full((1, WINDOW), s_ref[g * Q_PER_KV + i], F32) for i in range(Q_PER_KV)], axis=1)
            m = jnp.maximum(jnp.max(s, axis=0, keepdims=True), sink)
            p = jnp.exp(s - m)
            denom = _colsum(p) + jnp.exp(sink - m)
            probs = (p / denom).astype(BF16)
            out = _dot(vt, probs)
            lse = m + jnp.log(denom)
            for i in range(Q_PER_KV):
                h = g * Q_PER_KV + i
                o_ref[h * HEAD_DIM:(h + 1) * HEAD_DIM, :] = out[:, i * WINDOW:(i + 1) * WINDOW]
                lse_ref[h:h + 1, :] = lse[:, i * WINDOW:(i + 1) * WINDOW]

    kb = OFF_K // KV_DIM
    vb = OFF_V // KV_DIM
    prev = lambda n: jnp.maximum(n - 1, 0)
    return pl.pallas_call(
        body,
        name="attn_fwd",
        grid=(nb,),
        in_specs=[
            pl.BlockSpec(memory_space=pltpu.SMEM),
            pl.BlockSpec((Q_DIM, WINDOW), lambda n: (0, n)),
            pl.BlockSpec((KV_DIM, WINDOW), lambda n: (kb, n)),
            pl.BlockSpec((KV_DIM, WINDOW), lambda n: (kb, prev(n))),
            pl.BlockSpec((KV_DIM, WINDOW), lambda n: (vb, n)),
            pl.BlockSpec((KV_DIM, WINDOW), lambda n: (vb, prev(n))),
        ],
        out_specs=[pl.BlockSpec((Q_DIM, WINDOW), lambda n: (0, n)), pl.BlockSpec((N_Q_HEADS, WINDOW), lambda n: (0, n))],
        out_shape=[jax.ShapeDtypeStruct((Q_DIM, t), F32), jax.ShapeDtypeStruct((N_Q_HEADS, t), F32)],
        compiler_params=_cparams("parallel"),
    )(sinks, proj, proj, proj, proj, proj)


def _attn_bwd(proj, sinks, out, lse, dout, send=None):
    t = proj.shape[1]
    nb = t // WINDOW
    scale = HEAD_DIM ** -0.5

    def body(s_ref, q_ref, kc_ref, kp_ref, vc_ref, vp_ref, o_ref, lse_ref, do_ref,
             dq_ref, dk_ref, dv_ref, ds_ref, dk_carry, dv_carry):
        step = pl.program_id(0)
        n = nb - 1 - step

        @pl.when(step == 0)
        def _():
            dk_carry[...] = jnp.zeros_like(dk_carry)
            dv_carry[...] = jnp.zeros_like(dv_carry)
            ds_ref[...] = jnp.zeros_like(ds_ref)

        valid = _attn_mask(n)
        for g in range(N_KV_HEADS):
            rows = slice(g * HEAD_DIM, (g + 1) * HEAD_DIM)
            q0 = g * Q_PER_KV * HEAD_DIM
            kt = jnp.concatenate([kp_ref[rows, :], kc_ref[rows, :]], axis=1).astype(BF16)
            vt = jnp.concatenate([vp_ref[rows, :], vc_ref[rows, :]], axis=1).astype(BF16)
            qf = _lane_cat(q_ref, q0, HEAD_DIM)
            qcat = qf.astype(BF16)
            ocat = _lane_cat(o_ref, q0, HEAD_DIM)
            docat = _lane_cat(do_ref, q0, HEAD_DIM)
            dob = docat.astype(BF16)
            lse_cat = jnp.concatenate(
                [lse_ref[g * Q_PER_KV + i:g * Q_PER_KV + i + 1, :] for i in range(Q_PER_KV)], axis=1)
            sink = jnp.concatenate(
                [jnp.full((1, WINDOW), s_ref[g * Q_PER_KV + i], F32) for i in range(Q_PER_KV)], axis=1)
            s = jnp.where(valid, _dot_tn(kt, (qf * scale).astype(BF16)), NEG)
            p = jnp.exp(s - lse_cat)
            dp = _dot_tn(vt, dob)
            delta = _colsum(docat * ocat)
            dsc = (p * (dp - delta)).astype(BF16)
            dsink_row = -jnp.exp(sink - lse_cat) * delta
            dq = _dot(kt, dsc) * scale
            dk = _dot_nt(qcat, dsc) * scale
            dv = _dot_nt(dob, p.astype(BF16))
            for i in range(Q_PER_KV):
                h = g * Q_PER_KV + i
                dq_ref[h * HEAD_DIM:(h + 1) * HEAD_DIM, :] = dq[:, i * WINDOW:(i + 1) * WINDOW].astype(BF16)
                ds_ref[h:h + 1, :] += _rowsum(dsink_row[:, i * WINDOW:(i + 1) * WINDOW])
            dk_ref[rows, :] = (dk[:, WINDOW:] + dk_carry[rows, :]).astype(BF16)
            dv_ref[rows, :] = (dv[:, WINDOW:] + dv_carry[rows, :]).astype(BF16)
            dk_carry[rows, :] = dk[:, :WINDOW]
            dv_carry[rows, :] = dv[:, :WINDOW]

    kb = OFF_K // KV_DIM
    vb = OFF_V // KV_DIM
    cur = lambda i: nb - 1 - i
    prev = lambda i: jnp.maximum(nb - 2 - i, 0)
    qspec = pl.BlockSpec((Q_DIM, WINDOW), lambda i: (0, cur(i)))
    kvspec = pl.BlockSpec((KV_DIM, WINDOW), lambda i: (0, cur(i)))
    return _call(
        body,
        name="attn_bwd",
        grid=(nb,),
        in_specs=[
            pl.BlockSpec(memory_space=pltpu.SMEM),
            qspec,
            pl.BlockSpec((KV_DIM, WINDOW), lambda i: (kb, cur(i))),
            pl.BlockSpec((KV_DIM, WINDOW), lambda i: (kb, prev(i))),
            pl.BlockSpec((KV_DIM, WINDOW), lambda i: (vb, cur(i))),
            pl.BlockSpec((KV_DIM, WINDOW), lambda i: (vb, prev(i))),
            qspec,
            pl.BlockSpec((N_Q_HEADS, WINDOW), lambda i: (0, cur(i))),
            qspec,
        ],
        out_specs=[qspec, kvspec, kvspec, pl.BlockSpec((N_Q_HEADS, 1), lambda i: (0, 0))],
        out_shape=[
            jax.ShapeDtypeStruct((Q_DIM, t), BF16),
            jax.ShapeDtypeStruct((KV_DIM, t), BF16),
            jax.ShapeDtypeStruct((KV_DIM, t), BF16),
            jax.ShapeDtypeStruct((N_Q_HEADS, 1), F32),
        ],
        scratch_shapes=[pltpu.VMEM((KV_DIM, WINDOW), F32), pltpu.VMEM((KV_DIM, WINDOW), F32)],
        semantics=("arbitrary",), args=(sinks, proj, proj, proj, proj, proj, out, lse, dout), send=send)


CONV_ROWS = 256


def _conv_silu_fwd(proj, w_col, b_col):
    t = proj.shape[1]
    r0 = OFF_X // CONV_ROWS

    def body(x_ref, w_ref, b_ref, o_ref):
        def strip(rows):
            y, _ = _causal_conv(x_ref[rows, :], w_ref[rows, :], b_ref[rows, :])
            o_ref[rows, :] = y * _sigmoid(y)

        strip(slice(None))

    return pl.pallas_call(
        body,
        name="ssd_conv_fwd",
        grid=(XBC_DIM // CONV_ROWS,),
        in_specs=[
            pl.BlockSpec((CONV_ROWS, t), lambda i: (r0 + i, 0)),
            pl.BlockSpec((CONV_ROWS, SSD_CONV), lambda i: (i, 0)),
            pl.BlockSpec((CONV_ROWS, 1), lambda i: (i, 0)),
        ],
        out_specs=pl.BlockSpec((CONV_ROWS, t), lambda i: (i, 0)),
        out_shape=jax.ShapeDtypeStruct((XBC_DIM, t), F32),
        compiler_params=_cparams("parallel"),
    )(proj, w_col, b_col)


def _conv_silu_bwd(proj, w_col, b_col, dout, row0, dproj, name):
    t = proj.shape[1]
    nrows = dout.shape[0]
    p0 = (OFF_X + row0) // CONV_ROWS
    c0 = row0 // CONV_ROWS

    def body(x_ref, w_ref, b_ref, do_ref, dproj_ref, dx_ref, dwb_ref):
        def strip(rows):
            xv = x_ref[rows, :]
            wv = w_ref[rows, :]
            y, shifted = _causal_conv(xv, wv, b_ref[rows, :])
            sg = _sigmoid(y)
            dy = do_ref[rows, :] * (sg * (1.0 + y * (1.0 - sg)))
            dx, dwb_ref[rows, :] = _causal_conv_bwd(dy, xv, shifted, wv)
            dx_ref[rows, :] = dx.astype(BF16)

        strip(slice(None))

    return pl.pallas_call(
        body,
        name=name,
        grid=(nrows // CONV_ROWS,),
        in_specs=[
            pl.BlockSpec((CONV_ROWS, t), lambda i: (p0 + i, 0)),
            pl.BlockSpec((CONV_ROWS, SSD_CONV), lambda i: (c0 + i, 0)),
            pl.BlockSpec((CONV_ROWS, 1), lambda i: (c0 + i, 0)),
            pl.BlockSpec((CONV_ROWS, t), lambda i: (i, 0)),
            pl.BlockSpec(memory_space=pl.ANY),
        ],
        out_specs=[pl.BlockSpec((CONV_ROWS, t), lambda i: (p0 + i, 0)), pl.BlockSpec((CONV_ROWS, 128), lambda i: (i, 0))],
        out_shape=[jax.ShapeDtypeStruct(dproj.shape, BF16), jax.ShapeDtypeStruct((nrows, 128), F32)],
        input_output_aliases={4: 0},
        compiler_params=_cparams("parallel"),
    )(proj, w_col, b_col, dout, dproj)


GROUP_ROWS = HEADS_PER_GROUP * SSD_HEAD_DIM


def _ssd_specs(order):
    xb = D_INNER // BC_DIM
    dtb = OFF_DT // N_SSD_HEADS
    col = pl.BlockSpec((N_SSD_HEADS, 1), lambda c: (0, 0))
    return [
        pl.BlockSpec((D_INNER, CHUNK), lambda c: (0, order(c))),
        pl.BlockSpec((BC_DIM, CHUNK), lambda c: (xb, order(c))),
        pl.BlockSpec((BC_DIM, CHUNK), lambda c: (xb + 1, order(c))),
        pl.BlockSpec((N_SSD_HEADS, CHUNK), lambda c: (dtb, order(c))),
        col, col, col,
    ]


def _ssd_common(dt_ref, dtb_ref, alog_ref):
    z = dt_ref[...] + dtb_ref[...]
    dt = _softplus(z)
    a_neg = -jnp.exp(alog_ref[...])
    d_a = dt * a_neg
    row = lax.broadcasted_iota(jnp.int32, (CHUNK, CHUNK), 0)
    colm = lax.broadcasted_iota(jnp.int32, (CHUNK, CHUNK), 1)
    upper = (row <= colm).astype(F32)
    a_cs = jnp.dot(d_a, upper, precision=HIGHEST, preferred_element_type=F32)
    a_last = _rowsum(d_a)
    return z, dt, a_neg, a_cs, a_last, row >= colm, row == colm


def _decay(a_row, causal):
    a_s = jnp.broadcast_to(a_row, (CHUNK, CHUNK))
    seg = a_s.T - a_s
    return jnp.where(causal, jnp.exp(jnp.where(causal, seg, 0.0)), 0.0)


def _ssd_fwd(xbc, proj, dtb_col, alog_col, dsk_col):
    t = xbc.shape[1]
    nc = t // CHUNK

    def body(xs_ref, b_ref, c_ref, dt_ref, dtb_ref, alog_ref, dsk_ref, y_ref, hst_ref, h_scr):
        @pl.when(pl.program_id(0) == 0)
        def _():
            h_scr[...] = jnp.zeros_like(h_scr)

        _, dt, _, a_cs, a_last, causal, _ = _ssd_common(dt_ref, dtb_ref, alog_ref)
        hst_ref[0] = h_scr[...]
        dsk = dsk_ref[...]
        for g in range(N_SSD_GROUPS):
            grows = slice(g * D_STATE, (g + 1) * D_STATE)
            bb = b_ref[grows, :].astype(BF16)
            cb_ = c_ref[grows, :].astype(BF16)
            cb = _dot_tn(cb_, bb)
            for j in range(g * HEADS_PER_GROUP, (g + 1) * HEADS_PER_GROUP):
                rows = slice(j * SSD_HEAD_DIM, (j + 1) * SSD_HEAD_DIM)
                a = a_cs[j:j + 1, :]
                m = (cb * _decay(a, causal)).astype(BF16)
                xs = xs_ref[rows, :]
                xc = xs * dt[j:j + 1, :]
                hj = h_scr[rows, :]
                y = _dot_nt(xc.astype(BF16), m) + _dot(hj.astype(BF16), cb_) * jnp.exp(a) + dsk[j:j + 1, :] * xs
                y_ref[rows, :] = y
                al = a_last[j:j + 1, :]
                w = jnp.exp(al - a)
                h_scr[rows, :] = jnp.exp(al) * hj + _dot_nt((xc * w).astype(BF16), bb)

    return pl.pallas_call(
        body,
        name="ssd_fwd",
        grid=(nc,),
        in_specs=_ssd_specs(lambda c: c),
        out_specs=[
            pl.BlockSpec((D_INNER, CHUNK), lambda c: (0, c)),
            pl.BlockSpec((1, D_INNER, D_STATE), lambda c: (c, 0, 0)),
        ],
        out_shape=[
            jax.ShapeDtypeStruct((D_INNER, t), F32),
            jax.ShapeDtypeStruct((nc, D_INNER, D_STATE), F32),
        ],
        scratch_shapes=[pltpu.VMEM((D_INNER, D_STATE), F32)],
        compiler_params=_cparams("arbitrary"),
    )(xbc, xbc, xbc, proj, dtb_col, alog_col, dsk_col)


def _ssd_bwd(xbc, proj, dtb_col, alog_col, dsk_col, hst, dy):
    t = xbc.shape[1]
    nc = t // CHUNK
    rev = lambda c: nc - 1 - c

    def body(xs_ref, b_ref, c_ref, dt_ref, dtb_ref, alog_ref, dsk_ref, hst_ref, dy_ref,
             dxs_ref, db_ref, dc_ref, ddt_ref, dalog_ref, ddsk_ref, ddtb_ref, dh_scr, da_scr, ddt_scr, dd_scr):
        @pl.when(pl.program_id(0) == 0)
        def _():
            dh_scr[...] = jnp.zeros_like(dh_scr)
            dalog_ref[...] = jnp.zeros_like(dalog_ref)
            ddsk_ref[...] = jnp.zeros_like(ddsk_ref)
            ddtb_ref[...] = jnp.zeros_like(ddtb_ref)

        z, dt, a_neg, a_cs, a_last, causal, eye = _ssd_common(dt_ref, dtb_ref, alog_ref)
        dsk = dsk_ref[...]
        last_lane = lax.broadcasted_iota(jnp.int32, (1, CHUNK), 1) == CHUNK - 1
        for g in range(N_SSD_GROUPS):
            grows = slice(g * D_STATE, (g + 1) * D_STATE)
            bb = b_ref[grows, :].astype(BF16)
            cb_ = c_ref[grows, :].astype(BF16)
            cb = _dot_tn(cb_, bb)
            dcb = jnp.zeros((CHUNK, CHUNK), F32)
            dc_acc = jnp.zeros((D_STATE, CHUNK), F32)
            db_acc = jnp.zeros((D_STATE, CHUNK), F32)
            for j in range(g * HEADS_PER_GROUP, (g + 1) * HEADS_PER_GROUP):
                rows = slice(j * SSD_HEAD_DIM, (j + 1) * SSD_HEAD_DIM)
                a = a_cs[j:j + 1, :]
                al = a_last[j:j + 1, :]
                lam = _decay(a, causal)
                mf = cb * lam
                xs = xs_ref[rows, :]
                dtj = dt[j:j + 1, :]
                xc = xs * dtj
                w = jnp.exp(al - a)
                e = jnp.exp(a)
                gam = jnp.exp(al)
                hj = hst_ref[0, rows, :]
                hjb = hj.astype(BF16)
                dyv = dy_ref[rows, :]
                dyb = dyv.astype(BF16)
                dd_scr[j:j + 1, :] = _colsum(dyv * xs)
                gb = (dyv * e).astype(BF16)
                dh_in = _dot_nt(gb, cb_)
                dc_acc = dc_acc + _dot_tn(hjb, gb)
                yoff = _dot(hjb, cb_) * e
                da = _colsum(dyv * yoff)
                dm = _dot_tn(dyb, xc.astype(BF16))
                dxc = _dot(dyb, mf.astype(BF16))
                dcb = dcb + dm * lam
                nmat = dm * mf
                rs = jnp.broadcast_to(_rowsum(nmat), (CHUNK, CHUNK))
                da = da + _colsum(jnp.where(eye, rs, 0.0)) - _colsum(nmat)
                ds = dh_scr[rows, :]
                dsb = ds.astype(BF16)
                t1 = _dot(dsb, bb)
                xcw = xc * w
                dxc = dxc + w * t1
                dww = _colsum(xcw * t1)
                da_l = _rowsum(dww) + _rowsum(_colsum(ds * hj)) * gam
                da = da - dww + jnp.where(last_lane, da_l, 0.0)
                db_acc = db_acc + _dot_tn(dsb, xcw.astype(BF16))
                dh_scr[rows, :] = gam * ds + dh_in
                dxs_ref[rows, :] = dsk[j:j + 1, :] * dyv + dxc * dtj
                da_scr[j:j + 1, :] = da
                ddt_scr[j:j + 1, :] = _colsum(dxc * xs)
            dcbb = dcb.astype(BF16)
            dc_ref[grows, :] = dc_acc + _dot_nt(bb, dcbb)
            db_ref[grows, :] = db_acc + _dot(cb_, dcbb)
        dda = jnp.dot(da_scr[...], causal.astype(F32), precision=HIGHEST, preferred_element_type=F32)
        ddt = ddt_scr[...] + dda * a_neg
        ddt_raw = ddt * _sigmoid(z)
        ddt_ref[...] = ddt_raw
        ddtb_ref[...] += _rowsum(ddt_raw)
        dalog_ref[...] += _rowsum(dda * dt) * a_neg
        ddsk_ref[...] += _rowsum(dd_scr[...])

    col = pl.BlockSpec((N_SSD_HEADS, 1), lambda c: (0, 0))
    bc = pl.BlockSpec((BC_DIM, CHUNK), lambda c: (0, rev(c)))
    xs_spec = pl.BlockSpec((D_INNER, CHUNK), lambda c: (0, rev(c)))
    small = pltpu.VMEM((N_SSD_HEADS, CHUNK), F32)
    return pl.pallas_call(
        body,
        name="ssd_bwd",
        grid=(nc,),
        in_specs=_ssd_specs(rev) + [pl.BlockSpec((1, D_INNER, D_STATE), lambda c: (rev(c), 0, 0)), xs_spec],
        out_specs=[xs_spec, bc, bc, pl.BlockSpec((N_SSD_HEADS, CHUNK), lambda c: (0, rev(c))), col, col, col],
        out_shape=[
            jax.ShapeDtypeStruct((D_INNER, t), F32),
            jax.ShapeDtypeStruct((BC_DIM, t), F32),
            jax.ShapeDtypeStruct((BC_DIM, t), F32),
            jax.ShapeDtypeStruct((N_SSD_HEADS, t), F32),
            jax.ShapeDtypeStruct((N_SSD_HEADS, 1), F32),
            jax.ShapeDtypeStruct((N_SSD_HEADS, 1), F32),
            jax.ShapeDtypeStruct((N_SSD_HEADS, 1), F32),
        ],
        scratch_shapes=[pltpu.VMEM((D_INNER, D_STATE), F32), small, small, small],
        compiler_params=_cparams("arbitrary"),
    )(xbc, xbc, xbc, proj, dtb_col, alog_col, dsk_col, hst, dy)


GN_ROWS = D_INNER // N_SSD_GROUPS


def _gnorm_fwd(y, proj, w_col):
    t = y.shape[1]
    tt = _tile(t, (512, 256, 128))
    z0 = OFF_Z // GN_ROWS

    def body(y_ref, z_ref, w_ref, o_ref):
        zv = z_ref[...]
        u = y_ref[...] * (zv * _sigmoid(zv))
        r = lax.rsqrt(jnp.mean(u * u, axis=0, keepdims=True) + EPS)
        o_ref[...] = (u * r * w_ref[...]).astype(BF16)

    blk = pl.BlockSpec((GN_ROWS, tt), lambda g, i: (g, i))
    return pl.pallas_call(
        body,
        name="gnorm_fwd",
        grid=(N_SSD_GROUPS, t // tt),
        in_specs=[blk, pl.BlockSpec((GN_ROWS, tt), lambda g, i: (z0 + g, i)), pl.BlockSpec((GN_ROWS, 1), lambda g, i: (g, 0))],
        out_specs=blk,
        out_shape=jax.ShapeDtypeStruct((D_INNER, t), BF16),
        compiler_params=_cparams("parallel", "parallel"),
    )(y, proj, w_col)


def _gnorm_bwd(dout, y, proj, w_col, send=None):
    t = y.shape[1]
    tt = _tile(t, (512, 256, 128))
    z0 = OFF_Z // GN_ROWS

    def body(do_ref, y_ref, z_ref, w_ref, dy_ref, dz_ref, dw_ref):
        @pl.when(pl.program_id(1) == 0)
        def _():
            dw_ref[...] = jnp.zeros_like(dw_ref)

        zv = z_ref[...]
        yv = y_ref[...]
        sg = _sigmoid(zv)
        sz = zv * sg
        u = yv * sz
        r = lax.rsqrt(jnp.mean(u * u, axis=0, keepdims=True) + EPS)
        xhat = u * r
        dov = do_ref[...]
        dw_ref[...] += _rowsum(dov * xhat)
        dxhat = dov * w_ref[...]
        du = r * (dxhat - xhat * jnp.mean(dxhat * xhat, axis=0, keepdims=True))
        dy_ref[...] = du * sz
        dz_ref[...] = (du * yv * (sg * (1.0 + zv * (1.0 - sg)))).astype(BF16)

    blk = pl.BlockSpec((GN_ROWS, tt), lambda g, i: (g, i))
    col = pl.BlockSpec((GN_ROWS, 1), lambda g, i: (g, 0))
    return _call(
        body,
        name="gnorm_bwd",
        grid=(N_SSD_GROUPS, t // tt),
        in_specs=[blk, blk, pl.BlockSpec((GN_ROWS, tt), lambda g, i: (z0 + g, i)), col],
        out_specs=[blk, pl.BlockSpec((GN_ROWS, tt), lambda g, i: (z0 + g, i)), col],
        out_shape=[jax.ShapeDtypeStruct((D_INNER, t), F32), jax.ShapeDtypeStruct((IN_DIM, t), BF16),
                   jax.ShapeDtypeStruct((D_INNER, 1), F32)],
        semantics=("parallel", "arbitrary"), args=(dout, y, proj, w_col), send=send)


GATE_ROWS = 128


def _gate_specs(t):
    nr = D_MODEL // GATE_ROWS
    blk = pl.BlockSpec((GATE_ROWS, t), lambda r: (r, 0))
    rows_from = lambda first: pl.BlockSpec(
        (pl.Element(GATE_ROWS), pl.Element(t)), lambda r: (pl.multiple_of(first + GATE_ROWS * r, N_SSD_HEADS), 0))
    return blk, [
        rows_from(OFF_GA),
        rows_from(OFF_GS),
        pl.BlockSpec((GATE_ROWS, 1), lambda r: (r, 0)),
        pl.BlockSpec((GATE_ROWS, 1), lambda r: (nr + r, 0)),
        blk, blk,
    ]


def _gate_fwd(proj, b_col, attn, ssd):
    t = proj.shape[1]
    blk, specs = _gate_specs(t)

    def body(ga_ref, gs_ref, ba_ref, bs_ref, a_ref, s_ref, o_ref):
        o_ref[...] = (_sigmoid(ga_ref[...] + ba_ref[...]) * a_ref[...]
                      + _sigmoid(gs_ref[...] + bs_ref[...]) * s_ref[...]).astype(BF16)

    return pl.pallas_call(
        body,
        name="gate_fwd",
        grid=(D_MODEL // GATE_ROWS,),
        in_specs=specs,
        out_specs=blk,
        out_shape=jax.ShapeDtypeStruct((D_MODEL, t), BF16),
        compiler_params=_cparams("parallel"),
    )(proj, proj, b_col, b_col, attn, ssd)


def _gate_bwd(proj, b_col, attn, ssd, dmix, send=None):
    t = proj.shape[1]
    blk, specs = _gate_specs(t)

    def body(ga_ref, gs_ref, ba_ref, bs_ref, a_ref, s_ref, dm_ref, da_ref, dso_ref, dga_ref, dgs_ref, dba_ref, dbs_ref):
        dm = dm_ref[...]
        sa = _sigmoid(ga_ref[...] + ba_ref[...])
        ss = _sigmoid(gs_ref[...] + bs_ref[...])
        da_ref[...] = (dm * sa).astype(BF16)
        dso_ref[...] = (dm * ss).astype(BF16)
        dga = dm * a_ref[...] * sa * (1.0 - sa)
        dgs = dm * s_ref[...] * ss * (1.0 - ss)
        dga_ref[...] = dga.astype(BF16)
        dgs_ref[...] = dgs.astype(BF16)
        dba_ref[...] = _rowsum(dga)
        dbs_ref[...] = _rowsum(dgs)

    col = pl.BlockSpec((GATE_ROWS, 1), lambda r: (r, 0))
    act = jax.ShapeDtypeStruct((D_MODEL, t), BF16)
    bias = jax.ShapeDtypeStruct((D_MODEL, 1), F32)
    return _call(
        body,
        name="gate_bwd",
        grid=(D_MODEL // GATE_ROWS,),
        in_specs=specs + [blk],
        out_specs=[blk, blk, blk, blk, col, col],
        out_shape=[act, act, act, act, bias, bias],
        semantics=("parallel",), args=(proj, proj, b_col, b_col, attn, ssd, dmix), send=send)


FFN_ROWS = 256


def _ffn_fwd(u0, w_col, b_col):
    t = u0.shape[2]

    def body(u_ref, w_ref, b_ref, o_ref):
        def strip(rows):
            val, _ = _causal_conv(u_ref[0, rows, :], w_ref[0, rows, :], b_ref[0, rows, :])
            gt, _ = _causal_conv(u_ref[1, rows, :], w_ref[1, rows, :], b_ref[1, rows, :])
            o_ref[rows, :] = (gt * _sigmoid(gt) * val).astype(BF16)

        strip(slice(None))

    return pl.pallas_call(
        body,
        name="ffn_fwd",
        grid=(D_FF // FFN_ROWS,),
        in_specs=[
            pl.BlockSpec((2, FFN_ROWS, t), lambda i: (0, i, 0)),
            pl.BlockSpec((2, FFN_ROWS, FFN_CONV), lambda i: (0, i, 0)),
            pl.BlockSpec((2, FFN_ROWS, 1), lambda i: (0, i, 0)),
        ],
        out_specs=pl.BlockSpec((FFN_ROWS, t), lambda i: (i, 0)),
        out_shape=jax.ShapeDtypeStruct((D_FF, t), BF16),
        compiler_params=_cparams("parallel"),
    )(u0, w_col, b_col)


def _ffn_bwd(u0, w_col, b_col, dg, send=None):
    t = u0.shape[2]

    def body(u_ref, w_ref, b_ref, dg_ref, du_ref, dwb_ref):
        def strip(rows):
            xval, wval = u_ref[0, rows, :], w_ref[0, rows, :]
            xgt, wgt = u_ref[1, rows, :], w_ref[1, rows, :]
            val, sh_val = _causal_conv(xval, wval, b_ref[0, rows, :])
            gt, sh_gt = _causal_conv(xgt, wgt, b_ref[1, rows, :])
            sg = _sigmoid(gt)
            dgv = dg_ref[rows, :]
            dval = dgv * (gt * sg)
            dgt = dgv * val * (sg * (1.0 + gt * (1.0 - sg)))
            dx, dwb_ref[0, rows, :] = _causal_conv_bwd(dval, xval, sh_val, wval)
            du_ref[0, rows, :] = dx.astype(BF16)
            dx, dwb_ref[1, rows, :] = _causal_conv_bwd(dgt, xgt, sh_gt, wgt)
            du_ref[1, rows, :] = dx.astype(BF16)

        strip(slice(None))

    return _call(
        body,
        name="ffn_bwd",
        grid=(D_FF // FFN_ROWS,),
        in_specs=[
            pl.BlockSpec((2, FFN_ROWS, t), lambda i: (0, i, 0)),
            pl.BlockSpec((2, FFN_ROWS, FFN_CONV), lambda i: (0, i, 0)),
            pl.BlockSpec((2, FFN_ROWS, 1), lambda i: (0, i, 0)),
            pl.BlockSpec((FFN_ROWS, t), lambda i: (i, 0)),
        ],
        out_specs=[pl.BlockSpec((2, FFN_ROWS, t), lambda i: (0, i, 0)), pl.BlockSpec((2, FFN_ROWS, 128), lambda i: (0, i, 0))],
        out_shape=[jax.ShapeDtypeStruct((2, D_FF, t), BF16), jax.ShapeDtypeStruct((2, D_FF, 128), F32)],
        semantics=("parallel",), args=(u0, w_col, b_col, dg), send=send)


def _adamw_math(w, g, m, v):
    m = ADAM_B1 * m + (1.0 - ADAM_B1) * g
    v = ADAM_B2 * v + (1.0 - ADAM_B2) * (g * g)
    m_hat = m / (1.0 - ADAM_B1 ** ADAM_STEP)
    v_hat = v / (1.0 - ADAM_B2 ** ADAM_STEP)
    delta = -ADAM_LR * (m_hat / (jnp.sqrt(v_hat) + ADAM_EPS) + ADAM_WD * w)
    return delta, m, v


def _adamw_sharded(parts, w, m, v, name):
    r, c = w.shape[0], w.shape[-1]
    tc = _tile(c, (256, 128))
    blk_shape = (r, tc) if w.ndim == 2 else (r, 1, tc)
    slots = parts.shape[0]

    def body(p_ref, w_ref, m_ref, v_ref, g_ref, d_ref, nm_ref, nv_ref):
        g = p_ref[0].astype(F32)
        for s in range(1, slots):
            g = g + p_ref[s].astype(F32)
        flat = lambda ref: ref[...].reshape(r, tc)
        d, nm, nv = _adamw_math(flat(w_ref), g, flat(m_ref), flat(v_ref))
        for ref, val in ((g_ref, g), (d_ref, d), (nm_ref, nm), (nv_ref, nv)):
            ref[...] = val.reshape(blk_shape)

    blk = pl.BlockSpec(blk_shape, (lambda i: (0, i)) if w.ndim == 2 else (lambda i: (0, 0, i)))
    out = jax.ShapeDtypeStruct(w.shape, F32)
    return pl.pallas_call(
        body,
        name=name,
        grid=(c // tc,),
        in_specs=[pl.BlockSpec((slots, r, tc), lambda i: (0, 0, i)), blk, blk, blk],
        out_specs=[blk, blk, blk, blk],
        out_shape=[out, out, out, out],
        compiler_params=_cparams("parallel"),
    )(parts, w, m, v)


def _lane_offsets(sizes):
    offsets, pos = [], 0
    for n in sizes:
        offsets.append(pos)
        pos += -(-n // 128) * 128
    return offsets, pos


def _pack_row(parts):
    rows = [p.reshape(1, -1).astype(F32) for p in parts]
    return jnp.concatenate([jnp.pad(r, ((0, 0), (0, -r.shape[1] % 128))) for r in rows], axis=1)


def _small_update(parts, me, full_sizes, ws, ms, vs):
    n = len(ws)
    offsets, _ = _lane_offsets([1] + list(full_sizes))

    def body(me_ref, p_ref, *refs):
        w_refs, m_refs, v_refs = refs[:n], refs[n:2 * n], refs[2 * n:3 * n]
        scalar_ref, out_refs = refs[3 * n], refs[3 * n + 1:]
        tot = p_ref[0]
        for s in range(1, N_DEV):
            tot = tot + p_ref[s]
        scalar_ref[...] = tot[:, 0:1]
        for k in range(n):
            g_ref, d_ref, nm_ref, nv_ref = out_refs[4 * k:4 * k + 4]
            taps, cols = w_refs[k].shape
            if taps == 1:
                g_ref[...] = tot[:, offsets[k + 1]:offsets[k + 1] + cols]
            else:
                full = full_sizes[k] // taps
                for tap in range(taps):
                    mine = jnp.zeros((1, cols), F32)
                    for d in range(N_DEV):
                        lo = offsets[k + 1] + tap * full + d * cols
                        mine = jnp.where(me_ref[0] == d, tot[:, lo:lo + cols], mine)
                    g_ref[tap:tap + 1, :] = mine
            d_ref[...], nm_ref[...], nv_ref[...] = _adamw_math(w_refs[k][...], g_ref[...], m_refs[k][...], v_refs[k][...])

    vmem = pl.BlockSpec(memory_space=pltpu.VMEM)
    out_shape = [jax.ShapeDtypeStruct((1, 1), F32)]
    for wk in ws:
        out_shape += [jax.ShapeDtypeStruct(wk.shape, F32)] * 4
    res = pl.pallas_call(
        body,
        name="small_update",
        in_specs=[pl.BlockSpec(memory_space=pltpu.SMEM)] + [vmem] * (1 + 3 * n),
        out_specs=[vmem] * len(out_shape),
        out_shape=out_shape,
    )(me, parts, *ws, *ms, *vs)
    return res[0], [res[1 + 4 * k:5 + 4 * k] for k in range(n)]


ANY = pl.BlockSpec(memory_space=pl.ANY)
FLIPS = [(k >> 2 & 1, k >> 1 & 1, k & 1) for k in range(1, N_DEV)]


def _place():
    return lax.axis_index("x"), lax.axis_index("y"), lax.axis_index("c")


HBM = pl.BlockSpec(memory_space=pltpu.HBM)
SEM = pl.BlockSpec(memory_space=pltpu.SEMAPHORE)
EFFECT = pltpu.SideEffectType.DATAFLOW_SIDE_EFFECTING


def _peer_copy(gather, src_ref, land_ref, send_sems, recv_sems, k, sending):
    x, y, c = _place()
    fx, fy, fc = FLIPS[k]
    me = 4 * x + 2 * y + c
    peer = 4 * (x ^ fx) + 2 * (y ^ fy) + (c ^ fc)
    return pltpu.make_async_remote_copy(
        src_ref=src_ref if gather else src_ref.at[peer],
        dst_ref=land_ref.at[me if sending else peer],
        send_sem=send_sems.at[k], recv_sem=recv_sems.at[k],
        device_id=(x ^ fx, y ^ fy, c ^ fc), device_id_type=MESH)


SIBLING = 0
OTHER_CHIPS = (1, 3, 5)


def _gather_start(srcs, name, via_sibling):
    n = len(srcs)
    lands = [lax.empty((N_DEV,) + s.shape, s.dtype) for s in srcs]

    def body(*refs):
        src_refs, land_refs = refs[:n], refs[n:2 * n]
        send, recv = refs[2 * n:3 * n], refs[3 * n:4 * n]
        for i in range(n):
            for k in (SIBLING,) + OTHER_CHIPS if via_sibling else range(N_DEV - 1):
                _peer_copy(True, src_refs[i], land_refs[i], send[i], recv[i], k, True).start()

    sem = pltpu.SemaphoreType.DMA((N_DEV - 1,))
    hbm = lambda a: pltpu.HBM(a.shape, a.dtype)
    res = pl.pallas_call(
        body,
        name=name,
        in_specs=[HBM] * (2 * n),
        out_specs=[SEM] * (2 * n) + [HBM] * (2 * n),
        out_shape=[sem] * (2 * n) + [hbm(s) for s in srcs] + [hbm(a) for a in lands],
        input_output_aliases={i: 2 * n + i for i in range(2 * n)},
        compiler_params=pltpu.CompilerParams(has_side_effects=EFFECT),
    )(*[pltpu.with_memory_space_constraint(a, pltpu.HBM) for a in list(srcs) + lands])
    return res[:n], res[n:2 * n], res[2 * n:3 * n], res[3 * n:4 * n]


def _exchange_wait(send_sems, recv_sems, src, land, after, gather, name):
    def body(src_ref, land_ref, send_ref, recv_ref, after_ref, src_out, land_out):
        for k in range(N_DEV - 1):
            cp = _peer_copy(gather, src_ref, land_ref, send_ref, recv_ref, k, False)
            cp.wait_send()
            cp.wait_recv()

    hbm = lambda a: pltpu.HBM(a.shape, a.dtype)
    return pl.pallas_call(
        body,
        name=name,
        in_specs=[HBM, HBM, SEM, SEM, ANY],
        out_specs=[HBM, HBM],
        out_shape=[hbm(src), hbm(land)],
        input_output_aliases={0: 0, 1: 1},
        compiler_params=pltpu.CompilerParams(has_side_effects=EFFECT),
    )(src, land, send_sems, recv_sems, after)


def _own_slot(src, land, me, gather):
    own = src[None] if gather else lax.dynamic_slice_in_dim(src, me, 1, axis=0)
    return lax.dynamic_update_slice_in_dim(land, own, me, axis=0)


def _forwarded_copy(land_ref, send_sems, recv_sems, j, sending):
    x, y, c = _place()
    fx, fy, _ = FLIPS[OTHER_CHIPS[j]]
    slot = 4 * (x ^ fx) + 2 * (y ^ fy) + (c if sending else 1 - c)
    return pltpu.make_async_remote_copy(
        src_ref=land_ref.at[slot], dst_ref=land_ref.at[slot], send_sem=send_sems.at[j], recv_sem=recv_sems.at[j],
        device_id=(x, y, 1 - c), device_id_type=MESH)


def _gather_forward(send_sems, recv_sems, srcs, lands, after, name):
    n = len(srcs)

    def body(*refs):
        src_refs, land_refs = refs[:n], refs[n:2 * n]
        send, recv = refs[2 * n:3 * n], refs[3 * n:4 * n]
        fwd_send, fwd_recv = refs[4 * n + 1:5 * n + 1], refs[5 * n + 1:6 * n + 1]
        for i in range(n):
            for j, k in enumerate(OTHER_CHIPS):
                _peer_copy(True, src_refs[i], land_refs[i], send[i], recv[i], k, False).wait_recv()
                _forwarded_copy(land_refs[i], fwd_send[i], fwd_recv[i], j, True).start()

    sem = pltpu.SemaphoreType.DMA((len(OTHER_CHIPS),))
    hbm = lambda a: pltpu.HBM(a.shape, a.dtype)
    res = pl.pallas_call(
        body,
        name=name,
        in_specs=[HBM] * (2 * n) + [SEM] * (2 * n) + [ANY],
        out_specs=[SEM] * (2 * n) + [HBM] * (2 * n),
        out_shape=[sem] * (2 * n) + [hbm(a) for a in srcs] + [hbm(a) for a in lands],
        input_output_aliases={i: 2 * n + i for i in range(2 * n)},
        compiler_params=pltpu.CompilerParams(has_side_effects=EFFECT),
    )(*srcs, *lands, *send_sems, *recv_sems, after)
    return res[:n], res[n:2 * n], res[2 * n:3 * n], res[3 * n:4 * n]


def _gather_wait_forwarded(send_sems, recv_sems, fwd_send, fwd_recv, src, land, after, name):
    def body(src_ref, land_ref, send_ref, recv_ref, fwd_send_ref, fwd_recv_ref, after_ref, src_out, land_out):
        for k in (SIBLING,) + OTHER_CHIPS:
            _peer_copy(True, src_ref, land_ref, send_ref, recv_ref, k, False).wait_send()
        _peer_copy(True, src_ref, land_ref, send_ref, recv_ref, SIBLING, False).wait_recv()
        for j in range(len(OTHER_CHIPS)):
            _forwarded_copy(land_ref, fwd_send_ref, fwd_recv_ref, j, True).wait_send()
            _forwarded_copy(land_ref, fwd_send_ref, fwd_recv_ref, j, False).wait_recv()

    hbm = lambda a: pltpu.HBM(a.shape, a.dtype)
    return pl.pallas_call(
        body,
        name=name,
        in_specs=[HBM, HBM, SEM, SEM, SEM, SEM, ANY],
        out_specs=[HBM, HBM],
        out_shape=[hbm(src), hbm(land)],
        input_output_aliases={0: 0, 1: 1},
        compiler_params=pltpu.CompilerParams(has_side_effects=EFFECT),
    )(src, land, send_sems, recv_sems, fwd_send, fwd_recv, after)


N_CHIPS = N_DEV // 2


def _pair_exchange(by_core, name):
    def copy(src_ref, land_ref, send_sems, recv_sems, q):
        x, y, c = _place()
        return pltpu.make_async_remote_copy(
            src_ref=src_ref.at[q, 1 - c], dst_ref=land_ref.at[q], send_sem=send_sems.at[q], recv_sem=recv_sems.at[q],
            device_id=(x, y, 1 - c), device_id_type=MESH)

    def start(src_ref, land_ref, send_sems, recv_sems, src_out, land_out):
        for q in range(N_CHIPS):
            copy(src_ref, land_ref, send_sems, recv_sems, q).start()

    def wait(src_ref, land_ref, send_sems, recv_sems, src_out, land_out):
        for q in range(N_CHIPS):
            cp = copy(src_ref, land_ref, send_sems, recv_sems, q)
            cp.wait_send()
            cp.wait_recv()

    sem = pltpu.SemaphoreType.DMA((N_CHIPS,))
    hbm_src = pltpu.HBM(by_core.shape, by_core.dtype)
    hbm_land = pltpu.HBM(by_core.shape[:1] + by_core.shape[2:], by_core.dtype)
    params = pltpu.CompilerParams(has_side_effects=EFFECT)
    send_sems, recv_sems, src, land = pl.pallas_call(
        start, name=name + "_start", in_specs=[HBM, HBM], out_specs=[SEM, SEM, HBM, HBM],
        out_shape=[sem, sem, hbm_src, hbm_land], input_output_aliases={0: 2, 1: 3}, compiler_params=params,
    )(pltpu.with_memory_space_constraint(by_core, pltpu.HBM),
      pltpu.with_memory_space_constraint(lax.empty(hbm_land.shape, by_core.dtype), pltpu.HBM))
    return pl.pallas_call(
        wait, name=name + "_wait", in_specs=[HBM, HBM, SEM, SEM], out_specs=[HBM, HBM], out_shape=[hbm_src, hbm_land],
        input_output_aliases={0: 0, 1: 1}, compiler_params=params,
    )(src, land, send_sems, recv_sems)


def _pair_add(by_core, landed, name):
    q, _, r, c = by_core.shape
    tc = _tile(c, (512, 256, 128))

    def body(a_ref, b_ref, o_ref):
        mine = a_ref[0, lax.axis_index("c")]
        o_ref[0] = (mine.astype(F32) + b_ref[0].astype(F32)).astype(BF16)

    blk = pl.BlockSpec((1, r, tc), lambda i, j: (i, 0, j))
    return pl.pallas_call(
        body, name=name, grid=(q, c // tc),
        in_specs=[pl.BlockSpec((1, 2, r, tc), lambda i, j: (i, 0, 0, j)), blk], out_specs=blk,
        out_shape=jax.ShapeDtypeStruct(landed.shape, BF16), compiler_params=_cparams("parallel", "parallel"),
    )(by_core, landed)


def _chip_copy(src_ref, land_ref, send_sems, recv_sems, j, sending):
    x, y, c = _place()
    fx, fy, _ = FLIPS[OTHER_CHIPS[j]]
    here, there = 2 * x + y, 2 * (x ^ fx) + (y ^ fy)
    return pltpu.make_async_remote_copy(
        src_ref=src_ref.at[there], dst_ref=land_ref.at[here if sending else there],
        send_sem=send_sems.at[j], recv_sem=recv_sems.at[j],
        device_id=(x ^ fx, y ^ fy, c), device_id_type=MESH)


def _chip_wait(send_sems, recv_sems, src, land, after, name):
    def body(src_ref, land_ref, send_ref, recv_ref, after_ref, src_out, land_out):
        for j in range(len(OTHER_CHIPS)):
            cp = _chip_copy(src_ref, land_ref, send_ref, recv_ref, j, False)
            cp.wait_send()
            cp.wait_recv()

    hbm = lambda a: pltpu.HBM(a.shape, a.dtype)
    return pl.pallas_call(
        body,
        name=name,
        in_specs=[HBM, HBM, SEM, SEM, ANY],
        out_specs=[HBM, HBM],
        out_shape=[hbm(src), hbm(land)],
        input_output_aliases={0: 0, 1: 1},
        compiler_params=pltpu.CompilerParams(has_side_effects=EFFECT),
    )(src, land, send_sems, recv_sems, after)


def _col(v):
    return v.reshape(-1, 1).astype(F32)


def _local_step(x, tgt, started, weight, small, pair_sums):
    t = x.shape[0]
    n1 = _col(small["norm1_w"])
    n2 = _col(small["norm2_w"])
    nf = _col(small["final_norm_w"])
    bg = _col(small["b_gate"])
    sinks = small["attn_sinks"].reshape(-1).astype(F32)
    cbias = _col(small["ssd_conv_b"])
    dtb = _col(small["dt_bias"])
    alog = _col(small["a_log"])
    dsk = _col(small["d_skip"])
    gnw = _col(small["ssd_norm_w"])
    fb = small["ffn_conv_b"].reshape(2, D_FF, 1)

    xt, xn = _norm_fwd_tokens(x, n1, started, "norm1_fwd")
    cw = weight("ssd_conv_w", xn).T
    fw = weight("ffn_conv_w", xn).T.reshape(2, D_FF, FFN_CONV)
    w_in_t = weight("w_in", xn)
    proj = _matmul(w_in_t, xn, nt=False, out_dtype=F32, name="mm_in")
    ao, lse = _attn_fwd(proj, sinks)
    w_ao = weight("w_attn_o", ao)
    attn = _matmul(w_ao, ao, nt=False, out_dtype=F32, name="mm_attn_o", tn_a=True)
    xbc = _conv_silu_fwd(proj, cw, cbias)
    y, hst = _ssd_fwd(xbc, proj, dtb, alog, dsk)
    yn = _gnorm_fwd(y, proj, gnw)
    w_so = weight("w_ssd_o", yn)
    ssd = _matmul(w_so, yn, nt=False, out_dtype=F32, name="mm_ssd_o", tn_a=True)
    mix = _gate_fwd(proj, bg, attn, ssd)
    w_out = weight("w_out", mix)
    h1 = _matmul(w_out, mix, nt=False, out_dtype=F32, name="mm_out", add=xt, tn_a=True)
    hn = _norm_fwd(h1, n2, "norm2_fwd")
    w_up_t = weight("w_up", hn)
    u0 = _matmul(w_up_t, hn, nt=False, out_dtype=F32, name="mm_up").reshape(2, D_FF, t)
    gl = _ffn_fwd(u0, fw, fb)
    w_down = weight("w_down", gl)
    h2 = _matmul(w_down, gl, nt=False, out_dtype=F32, name="mm_down", add=h1, tn_a=True)
    dh2, loss, d_nf = _final_norm_loss(h2, tgt, nf)

    g = {}
    handles = {}

    def sending(weight_name, grad, fn, *args, **kwargs):
        chunks = grad if grad.ndim == 3 else grad.reshape(N_DEV, -1, D_MODEL)
        out, handles[weight_name] = fn(*args, send=chunks, **kwargs)
        return out

    g_down = _matmul(gl, dh2, nt=True, out_dtype=BF16, name="mm_d_w_down")
    dgl = _matmul(w_down, dh2, nt=False, out_dtype=F32, name="mm_d_glu")
    du0, d_fwb = sending("w_down", g_down, _ffn_bwd, u0, fw, fb, dgl)
    du0 = du0.reshape(2 * D_FF, t)
    g_up = _matmul(du0, hn, nt=True, out_dtype=BF16, name="mm_d_w_up")
    dhn = sending("w_up", g_up, _matmul, w_up_t, du0, nt=False, out_dtype=F32, name="mm_d_hn", tn_a=True)
    dh1, d_n2 = _norm_bwd(dhn, h1, n2, dh2, "norm2_bwd")
    g_out = _matmul(mix, dh1, nt=True, out_dtype=BF16, name="mm_d_w_out")
    dmix = _matmul(w_out, dh1, nt=False, out_dtype=F32, name="mm_d_mix")
    d_attn, d_ssd, d_ga, d_gs, d_ba, d_bs = sending("w_out", g_out, _gate_bwd, proj, bg, attn, ssd, dmix)
    g_ao = _matmul(ao, d_attn, nt=True, out_dtype=BF16, name="mm_d_w_attn_o")
    dao = _matmul(w_ao, d_attn, nt=False, out_dtype=F32, name="mm_d_ao")
    dq, dk, dv, d_sinks = sending("w_attn_o", g_ao, _attn_bwd, proj, sinks, ao, lse, dao)
    g_so = _matmul(yn, d_ssd, nt=True, out_dtype=BF16, name="mm_d_w_ssd_o")
    dyn = _matmul(w_so, d_ssd, nt=False, out_dtype=F32, name="mm_d_yn")
    dy, dproj, d_gnw = sending("w_ssd_o", g_so, _gnorm_bwd, dyn, y, proj, gnw)
    dxs, dbm, dcm, ddt, d_alog, d_dsk, d_dtb = _ssd_bwd(xbc, proj, dtb, alog, dsk, hst, dy)
    dproj, dwb_xs = _conv_silu_bwd(proj, cw, cbias, dxs, 0, dproj, "ssd_conv_bwd_x")
    dproj, dwb_b = _conv_silu_bwd(proj, cw, cbias, dbm, D_INNER, dproj, "ssd_conv_bwd_b")
    dproj, dwb_c = _conv_silu_bwd(proj, cw, cbias, dcm, D_INNER + BC_DIM, dproj, "ssd_conv_bwd_c")
    dwb_conv = jnp.concatenate([dwb_xs, dwb_b, dwb_c], axis=0)
    for rows, part in ((OFF_Q, dq), (OFF_K, dk), (OFF_V, dv), (OFF_DT, ddt.astype(BF16)), (OFF_GA, d_ga), (OFF_GS, d_gs)):
        dproj = lax.dynamic_update_slice(dproj, part, (rows, 0))
    g_in = pair_sums(_matmul(dproj, xn, nt=True, out_dtype=BF16, name="mm_d_w_in"))
    dxn = sending("w_in", g_in, _matmul, w_in_t, dproj, nt=False, out_dtype=F32, name="mm_d_xn", tn_a=True)
    dx, d_n1 = _norm_bwd(dxn, xt, n1, dh1, "norm1_bwd", tokens_out=True)

    g["norm1_w"] = d_n1
    g["b_gate"] = jnp.concatenate([d_ba, d_bs], axis=0)
    g["attn_sinks"] = d_sinks
    g["ssd_conv_w"] = dwb_conv[:, :SSD_CONV].T
    g["ssd_conv_b"] = dwb_conv[:, SSD_CONV]
    g["dt_bias"] = d_dtb
    g["a_log"] = d_alog
    g["d_skip"] = d_dsk
    g["ssd_norm_w"] = d_gnw
    g["norm2_w"] = d_n2
    d_fwb = d_fwb.reshape(2 * D_FF, 128)
    g["ffn_conv_w"] = d_fwb[:, :FFN_CONV].T
    g["ffn_conv_b"] = d_fwb[:, FFN_CONV]
    g["final_norm_w"] = d_nf
    return loss, dx, g, handles


SMALL = ("norm1_w", "b_gate", "attn_sinks", "ssd_conv_w", "ssd_conv_b", "dt_bias", "a_log", "d_skip", "ssd_norm_w",
         "norm2_w", "ffn_conv_w", "ffn_conv_b", "final_norm_w")
WEIGHT_ORDER = ("norm1_w", "w_in", "b_gate", "attn_sinks", "w_attn_o", "ssd_conv_w", "ssd_conv_b", "dt_bias", "a_log",
                "d_skip", "ssd_norm_w", "w_ssd_o", "w_out", "norm2_w", "w_up", "ffn_conv_w", "ffn_conv_b", "w_down",
                "final_norm_w")


def kernel(x, norm1_w, w_in, b_gate, attn_sinks, w_attn_o, ssd_conv_w, ssd_conv_b, dt_bias, a_log, d_skip, ssd_norm_w, w_ssd_o, w_out, norm2_w, w_up, ffn_conv_w, ffn_conv_b, w_down, final_norm_w, loss_target, m_norm1_w, m_w_in, m_b_gate, m_attn_sinks, m_w_attn_o, m_ssd_conv_w, m_ssd_conv_b, m_dt_bias, m_a_log, m_d_skip, m_ssd_norm_w, m_w_ssd_o, m_w_out, m_norm2_w, m_w_up, m_ffn_conv_w, m_ffn_conv_b, m_w_down, m_final_norm_w, v_norm1_w, v_w_in, v_b_gate, v_attn_sinks, v_w_attn_o, v_ssd_conv_w, v_ssd_conv_b, v_dt_bias, v_a_log, v_d_skip, v_ssd_norm_w, v_w_ssd_o, v_w_out, v_norm2_w, v_w_up, v_ffn_conv_w, v_ffn_conv_b, v_w_down, v_final_norm_w):
    w = dict(norm1_w=norm1_w, w_in=w_in, b_gate=b_gate, attn_sinks=attn_sinks, w_attn_o=w_attn_o, ssd_conv_w=ssd_conv_w, ssd_conv_b=ssd_conv_b, dt_bias=dt_bias, a_log=a_log, d_skip=d_skip, ssd_norm_w=ssd_norm_w, w_ssd_o=w_ssd_o, w_out=w_out, norm2_w=norm2_w, w_up=w_up, ffn_conv_w=ffn_conv_w, ffn_conv_b=ffn_conv_b, w_down=w_down, final_norm_w=final_norm_w)
    m = dict(norm1_w=m_norm1_w, w_in=m_w_in, b_gate=m_b_gate, attn_sinks=m_attn_sinks, w_attn_o=m_w_attn_o, ssd_conv_w=m_ssd_conv_w, ssd_conv_b=m_ssd_conv_b, dt_bias=m_dt_bias, a_log=m_a_log, d_skip=m_d_skip, ssd_norm_w=m_ssd_norm_w, w_ssd_o=m_w_ssd_o, w_out=m_w_out, norm2_w=m_norm2_w, w_up=m_w_up, ffn_conv_w=m_ffn_conv_w, ffn_conv_b=m_ffn_conv_b, w_down=m_w_down, final_norm_w=m_final_norm_w)
    v = dict(norm1_w=v_norm1_w, w_in=v_w_in, b_gate=v_b_gate, attn_sinks=v_attn_sinks, w_attn_o=v_w_attn_o, ssd_conv_w=v_ssd_conv_w, ssd_conv_b=v_ssd_conv_b, dt_bias=v_dt_bias, a_log=v_a_log, d_skip=v_d_skip, ssd_norm_w=v_ssd_norm_w, w_ssd_o=v_w_ssd_o, w_out=v_w_out, norm2_w=v_norm2_w, w_up=v_w_up, ffn_conv_w=v_ffn_conv_w, ffn_conv_b=v_ffn_conv_b, w_down=v_w_down, final_norm_w=v_final_norm_w)
    me = 4 * lax.axis_index("x") + 2 * lax.axis_index("y") + lax.axis_index("c")

    shards = {"ssd_conv_w": ssd_conv_w[0], "ffn_conv_w": ffn_conv_w[0], "w_in": w_in[0].T.astype(BF16),
              "w_attn_o": w_attn_o[0].astype(BF16), "w_ssd_o": w_ssd_o[0].astype(BF16), "w_out": w_out[0].astype(BF16),
              "w_up": w_up[0].T.astype(BF16), "w_down": w_down[0].astype(BF16)}
    order = list(shards)
    g_send, g_recv, g_src, g_land = _gather_start(list(shards.values()), "gather_start", True)
    first = ("ssd_conv_w", "ffn_conv_w", "w_in")
    forwarded = {}

    def weight(name, after):
        if name not in forwarded:
            group = [k for k in order if (k in first) == (name in first)]
            idx = [order.index(k) for k in group]
            handles = _gather_forward([g_send[i] for i in idx], [g_recv[i] for i in idx], [g_src[i] for i in idx],
                                      [g_land[i] for i in idx], after, "gather_forward_for_" + name)
            forwarded.update(zip(group, zip(*handles)))
        i = order.index(name)
        src, land = _gather_wait_forwarded(g_send[i], g_recv[i], *forwarded[name], after, "gather_wait_" + name)
        land = _own_slot(src, land, me, True)
        if name == "ssd_conv_w":
            return jnp.transpose(land, (1, 0, 2)).reshape(SSD_CONV, XBC_DIM)
        if name == "ffn_conv_w":
            return jnp.transpose(land, (1, 0, 2)).reshape(FFN_CONV, 2 * D_FF)
        return land.reshape(-1, D_MODEL)

    def pair_sums(grad):
        by_core, landed = _pair_exchange(grad.reshape(N_CHIPS, 2, -1, D_MODEL), "grad_pair_w_in")
        return _pair_add(by_core, landed, "grad_pair_add_w_in")

    small = {k: w[k][0] if k != "final_norm_w" else w[k] for k in SMALL}
    loss, dx, g, pending = _local_step(x[0], loss_target[0], g_src[0], weight, small, pair_sums)

    packed = _pack_row([loss] + [g[k] for k in SMALL])
    s_send, s_recv, s_src, s_land = _gather_start([packed], "small_grads_start", False)

    res = {}
    after = s_src[0]
    for name in ("w_down", "w_up", "w_out", "w_attn_o", "w_ssd_o", "w_in"):
        if name == "w_in":
            parts = _own_slot(*_chip_wait(*pending[name], after, "grad_wait_" + name), me // 2, False)
        else:
            parts = _own_slot(*_exchange_wait(*pending[name], after, False, "grad_wait_" + name), me, False)
        view, back = {
            "w_in": (lambda a: jnp.transpose(a, (2, 0, 1)), lambda r: jnp.transpose(r, (1, 2, 0))),
            "w_up": (lambda a: a[0].T, lambda r: r.T[None]),
        }.get(name, (lambda a: a[0], lambda r: r[None]))
        res[name] = _adamw_sharded(parts, view(w[name]), view(m[name]), view(v[name]), "adamw_" + name)
        after = res[name][0]
        res[name] = [back(r) for r in res[name]]

    rows = _own_slot(*_exchange_wait(s_send[0], s_recv[0], s_src[0], s_land[0], after, True, "small_grads_wait"),
                     me, True)
    flat = lambda a: a.reshape(-1, a.shape[-1])
    loss_sum, updates = _small_update(
        rows, me.reshape(1), [g[k].size for k in SMALL],
        [flat(w[k]) for k in SMALL], [flat(m[k]) for k in SMALL], [flat(v[k]) for k in SMALL])
    for k, upd in zip(SMALL, updates):
        res[k] = [u.reshape(w[k].shape) for u in upd]

    grad_x = dx[None]
    outs = [loss_sum.reshape(()), grad_x]
    for i in range(4):
        outs.extend(res[k][i] for k in WEIGHT_ORDER)
    return tuple(outs)
```

```python
import functools

import jax
import jax.numpy as jnp
from jax import lax
from jax.experimental import pallas as pl
from jax.experimental.pallas import tpu as pltpu

F32 = jnp.float32
BF16 = jnp.bfloat16
HIGHEST = lax.Precision.HIGHEST

D_MODEL = 1024
N_Q_HEADS = 16
N_KV_HEADS = 4
HEAD_DIM = 64
WINDOW = 128
Q_PER_KV = N_Q_HEADS // N_KV_HEADS
Q_DIM = N_Q_HEADS * HEAD_DIM
KV_DIM = N_KV_HEADS * HEAD_DIM
D_INNER = 2048
SSD_HEAD_DIM = 64
N_SSD_HEADS = 32
N_SSD_GROUPS = 4
HEADS_PER_GROUP = N_SSD_HEADS // N_SSD_GROUPS
D_STATE = 128
BC_DIM = N_SSD_GROUPS * D_STATE
XBC_DIM = D_INNER + 2 * BC_DIM
SSD_CONV = 4
CHUNK = 128
D_FF = 2816
FFN_CONV = 3
EPS = 1e-5
NEG = -1e30
IN_DIM = 8736
N_DEV = 8

OFF_Q = 0
OFF_K = OFF_Q + Q_DIM
OFF_V = OFF_K + KV_DIM
OFF_Z = OFF_V + KV_DIM
OFF_X = OFF_Z + D_INNER
OFF_DT = OFF_X + XBC_DIM
OFF_GA = OFF_DT + N_SSD_HEADS
OFF_GS = OFF_GA + D_MODEL

ADAM_LR = 0.001
ADAM_B1 = 0.9
ADAM_B2 = 0.999
ADAM_EPS = 1e-08
ADAM_WD = 0.01
ADAM_STEP = 10

VMEM_LIMIT = 48 * 1024 * 1024
MESH = pl.DeviceIdType.MESH


def _cparams(*sem):
    return pltpu.CompilerParams(dimension_semantics=sem, vmem_limit_bytes=VMEM_LIMIT)


def _tile(n, prefs):
    for p in prefs:
        if n % p == 0:
            return p
    return n


def _sigmoid(x):
    return 1.0 / (1.0 + jnp.exp(-x))


def _softplus(x):
    return jnp.maximum(x, 0.0) + jnp.log(1.0 + jnp.exp(-jnp.abs(x)))


def _rowsum(x):
    return jnp.sum(x, axis=1, keepdims=True)


def _colsum(x):
    return jnp.sum(x, axis=0, keepdims=True)


def _dot(a, b):
    return jnp.dot(a, b, preferred_element_type=F32)


def _dot_nt(a, b):
    return lax.dot_general(a, b, (((1,), (1,)), ((), ())), preferred_element_type=F32)


def _dot_tn(a, b):
    return lax.dot_general(a, b, (((0,), (0,)), ((), ())), preferred_element_type=F32)


def _shift_right(x, j):
    if j == 0:
        return x
    r = pltpu.roll(x, j, 1)
    lane = lax.broadcasted_iota(jnp.int32, (x.shape[0], 128), 1)
    return jnp.concatenate([jnp.where(lane >= j, r[:, :128], 0.0), r[:, 128:]], axis=1)


def _shift_left(x, j):
    if j == 0:
        return x
    n = x.shape[1]
    r = pltpu.roll(x, n - j, 1)
    lane = lax.broadcasted_iota(jnp.int32, (x.shape[0], 128), 1)
    return jnp.concatenate([r[:, :n - 128], jnp.where(lane < 128 - j, r[:, n - 128:], 0.0)], axis=1)


def _causal_conv(xv, wv, bv):
    taps = wv.shape[1]
    shifted = [_shift_right(xv, taps - 1 - k) for k in range(taps - 1)]
    y = bv + wv[:, taps - 1:taps] * xv
    for k in range(taps - 1):
        y = y + wv[:, k:k + 1] * shifted[k]
    return y, shifted


def _causal_conv_bwd(dy, xv, shifted, wv):
    taps = wv.shape[1]
    lane = lax.broadcasted_iota(jnp.int32, (dy.shape[0], 128), 1)
    dwb = jnp.where(lane == taps, _rowsum(dy), 0.0)
    dwb = jnp.where(lane == taps - 1, _rowsum(dy * xv), dwb)
    dx = wv[:, taps - 1:taps] * dy
    for k in range(taps - 1):
        dx = dx + wv[:, k:k + 1] * _shift_left(dy, taps - 1 - k)
        dwb = jnp.where(lane == k, _rowsum(dy * shifted[k]), dwb)
    return dx, dwb


def _call(body, *, name, grid, in_specs, out_specs, out_shape, args, semantics, scratch_shapes=(), send=None):
    if send is None:
        return pl.pallas_call(body, name=name, grid=grid, in_specs=in_specs, out_specs=out_specs, out_shape=out_shape,
                              scratch_shapes=list(scratch_shapes), compiler_params=_cparams(*semantics))(*args)
    single = not isinstance(out_specs, (list, tuple))
    out_specs, out_shape = ([out_specs], [out_shape]) if single else (list(out_specs), list(out_shape))
    n_in, n_out = len(in_specs), len(out_specs)
    chips = send.shape[0] == N_DEV // 2
    n_copies = len(OTHER_CHIPS) if chips else N_DEV - 1

    def sending(*refs):
        ins, (src_ref, land_ref) = refs[:n_in], refs[n_in:n_in + 2]
        outs = refs[n_in + 2:n_in + 2 + n_out]
        send_sems, recv_sems = refs[n_in + 2 + n_out:n_in + 4 + n_out]
        scratch = refs[n_in + 6 + n_out:]
        step = 0
        for axis, size in enumerate(grid):
            step = step * size + pl.program_id(axis)

        @pl.when(step == 0)
        def _():
            for k in range(n_copies):
                if chips:
                    _chip_copy(src_ref, land_ref, send_sems, recv_sems, k, True).start()
                else:
                    _peer_copy(False, src_ref, land_ref, send_sems, recv_sems, k, True).start()

        body(*ins, *outs, *scratch)

    sem = pltpu.SemaphoreType.DMA((n_copies,))
    hbm = pltpu.HBM(send.shape, send.dtype)
    res = pl.pallas_call(
        sending, name=name, grid=grid,
        in_specs=list(in_specs) + [HBM, HBM],
        out_specs=out_specs + [SEM, SEM, HBM, HBM],
        out_shape=out_shape + [sem, sem, hbm, hbm],
        input_output_aliases={n_in: n_out + 2, n_in + 1: n_out + 3},
        scratch_shapes=list(scratch_shapes),
        compiler_params=pltpu.CompilerParams(dimension_semantics=("arbitrary",) * len(grid), vmem_limit_bytes=VMEM_LIMIT,
                                             has_side_effects=EFFECT),
    )(*args, pltpu.with_memory_space_constraint(send, pltpu.HBM),
      pltpu.with_memory_space_constraint(lax.empty(send.shape, send.dtype), pltpu.HBM))
    return (res[0] if single else list(res[:n_out])), tuple(res[n_out:])


MATMUL_VMEM_BUDGET = 36 * 1024 * 1024
MATMUL_MAX_TK = 3072


MATMUL_MAX_TM = 768


def _largest_tile(n, align, cap):
    return max(d for d in range(align, min(n, cap) + 1, align) if n % d == 0)


def _matmul_tiles(m, n, k, a_bytes, b_bytes, out_bytes, has_add, m_align, k_align):
    tm = _largest_tile(m, m_align, MATMUL_MAX_TM)
    tk = _largest_tile(k, k_align, MATMUL_MAX_TK)
    for tn in sorted({d for d in range(128, n + 1, 128) if n % d == 0}, reverse=True):
        need = 2 * (tm * tk * a_bytes + tk * tn * b_bytes) + tm * tn * (2 * out_bytes + (4 if k > tk else 0) + (8 if has_add else 0))
        if tn <= 3072 and need <= MATMUL_VMEM_BUDGET:
            return tm, tn, tk
    return tm, 128, tk


def _matmul(a, b, *, nt, out_dtype, name, add=None, tn_a=False, send=None):
    if tn_a:
        k, m = a.shape
    else:
        m, k = a.shape
    n = b.shape[0] if nt else b.shape[1]
    tm, tn, tk = _matmul_tiles(m, n, k, a.dtype.itemsize, b.dtype.itemsize, jnp.dtype(out_dtype).itemsize, add is not None,
                               128 if tn_a else 16, 16 if tn_a and not nt else 128)
    nk = k // tk
    grid = (m // tm, n // tn, nk)

    def body(a_ref, b_ref, *rest):
        r_ref = None
        if add is not None:
            r_ref, rest = rest[0], rest[1:]
        o_ref = rest[0]
        av = a_ref[...].astype(BF16)
        bv = b_ref[...].astype(BF16)
        part = _dot_tn(av, bv) if tn_a else _dot_nt(av, bv) if nt else _dot(av, bv)

        def finish(r):
            if add is not None:
                r = r + r_ref[...]
            o_ref[...] = r.astype(out_dtype)

        if nk == 1:
            finish(part)
            return
        acc = rest[1]
        kk = pl.program_id(2)

        @pl.when(kk == 0)
        def _():
            acc[...] = part

        @pl.when((kk > 0) & (kk < nk - 1))
        def _():
            acc[...] += part

        @pl.when(kk == nk - 1)
        def _():
            finish(acc[...] + part)

    in_specs = [
        pl.BlockSpec((tk, tm), lambda i, j, kk: (kk, i)) if tn_a else pl.BlockSpec((tm, tk), lambda i, j, kk: (i, kk)),
        pl.BlockSpec((tn, tk), lambda i, j, kk: (j, kk)) if nt else pl.BlockSpec((tk, tn), lambda i, j, kk: (kk, j)),
    ]
    args = [a, b]
    if add is not None:
        in_specs.append(pl.BlockSpec((tm, tn), lambda i, j, kk: (i, j)))
        args.append(add)
    return _call(
        body, name=name, grid=grid, in_specs=in_specs, args=args,
        out_specs=pl.BlockSpec((tm, tn), lambda i, j, kk: (i, j)),
        out_shape=jax.ShapeDtypeStruct((m, n), out_dtype),
        scratch_shapes=[pltpu.VMEM((tm, tn), F32)] if nk > 1 else [],
        semantics=("parallel", "parallel", "arbitrary"), send=send)


def _norm_fwd(x, w_col, name):
    f, t = x.shape
    tt = _tile(t, (512, 256, 128))

    def body(x_ref, w_ref, o_ref):
        xv = x_ref[...]
        r = lax.rsqrt(jnp.mean(xv * xv, axis=0, keepdims=True) + EPS)
        o_ref[...] = (xv * r * w_ref[...]).astype(BF16)

    return pl.pallas_call(
        body,
        name=name,
        grid=(t // tt,),
        in_specs=[pl.BlockSpec((f, tt), lambda i: (0, i)), pl.BlockSpec((f, 1), lambda i: (0, 0))],
        out_specs=pl.BlockSpec((f, tt), lambda i: (0, i)),
        out_shape=jax.ShapeDtypeStruct((f, t), BF16),
        compiler_params=_cparams("parallel"),
    )(x, w_col)


def _norm_fwd_tokens(x, w_col, after, name):
    t, f = x.shape
    tt = _tile(t, (512, 256, 128))

    def body(x_ref, w_ref, after_ref, xt_ref, o_ref):
        xv = x_ref[...].T
        xt_ref[...] = xv
        r = lax.rsqrt(jnp.mean(xv * xv, axis=0, keepdims=True) + EPS)
        o_ref[...] = (xv * r * w_ref[...]).astype(BF16)

    blk = pl.BlockSpec((f, tt), lambda i: (0, i))
    return pl.pallas_call(
        body,
        name=name,
        grid=(t // tt,),
        in_specs=[pl.BlockSpec((tt, f), lambda i: (i, 0)), pl.BlockSpec((f, 1), lambda i: (0, 0)), ANY],
        out_specs=[blk, blk],
        out_shape=[jax.ShapeDtypeStruct((f, t), F32), jax.ShapeDtypeStruct((f, t), BF16)],
        compiler_params=_cparams("parallel"),
    )(x, w_col, after)


def _norm_bwd(dy, x, w_col, res, name, tokens_out=False):
    f, t = x.shape
    tt = _tile(t, (512, 256, 128))

    def body(dy_ref, x_ref, w_ref, res_ref, dx_ref, dw_ref):
        @pl.when(pl.program_id(0) == 0)
        def _():
            dw_ref[...] = jnp.zeros_like(dw_ref)

        xv = x_ref[...]
        r = lax.rsqrt(jnp.mean(xv * xv, axis=0, keepdims=True) + EPS)
        xhat = xv * r
        dyv = dy_ref[...]
        dw_ref[...] += _rowsum(dyv * xhat)
        dxhat = dyv * w_ref[...]
        dx = res_ref[...] + r * (dxhat - xhat * jnp.mean(dxhat * xhat, axis=0, keepdims=True))
        dx_ref[...] = dx.T if tokens_out else dx

    blk = pl.BlockSpec((f, tt), lambda i: (0, i))
    col = pl.BlockSpec((f, 1), lambda i: (0, 0))
    return pl.pallas_call(
        body,
        name=name,
        grid=(t // tt,),
        in_specs=[blk, blk, col, blk],
        out_specs=[pl.BlockSpec((tt, f), lambda i: (i, 0)) if tokens_out else blk, col],
        out_shape=[jax.ShapeDtypeStruct((t, f) if tokens_out else (f, t), F32), jax.ShapeDtypeStruct((f, 1), F32)],
        compiler_params=_cparams("arbitrary"),
    )(dy, x, w_col, res)


def _final_norm_loss(h, tgt, w_col):
    f, t = h.shape
    tt = _tile(t, (512, 256, 128))

    def body(h_ref, t_ref, w_ref, dh_ref, loss_ref, dw_ref):
        @pl.when(pl.program_id(0) == 0)
        def _():
            dw_ref[...] = jnp.zeros_like(dw_ref)
            loss_ref[...] = jnp.zeros_like(loss_ref)

        xv = h_ref[...]
        r = lax.rsqrt(jnp.mean(xv * xv, axis=0, keepdims=True) + EPS)
        xhat = xv * r
        wv = w_ref[...]
        err = xhat * wv - t_ref[...].T
        loss_ref[...] += 0.5 * _rowsum(jnp.mean(err * err, axis=0, keepdims=True))
        dyv = err * (1.0 / f)
        dw_ref[...] += _rowsum(dyv * xhat)
        dxhat = dyv * wv
        dh_ref[...] = r * (dxhat - xhat * jnp.mean(dxhat * xhat, axis=0, keepdims=True))

    blk = pl.BlockSpec((f, tt), lambda i: (0, i))
    col = pl.BlockSpec((f, 1), lambda i: (0, 0))
    one = pl.BlockSpec((1, 1), lambda i: (0, 0))
    return pl.pallas_call(
        body,
        name="final_norm_loss",
        grid=(t // tt,),
        in_specs=[blk, pl.BlockSpec((tt, f), lambda i: (i, 0)), col],
        out_specs=[blk, one, col],
        out_shape=[jax.ShapeDtypeStruct((f, t), F32), jax.ShapeDtypeStruct((1, 1), F32), jax.ShapeDtypeStruct((f, 1), F32)],
        compiler_params=_cparams("arbitrary"),
    )(h, tgt, w_col)


def _attn_mask(n):
    shape = (2 * WINDOW, Q_PER_KV * WINDOW)
    si = lax.broadcasted_iota(jnp.int32, shape, 0)
    qi = lax.broadcasted_iota(jnp.int32, shape, 1) & (WINDOW - 1)
    dist = WINDOW + qi - si
    return (dist >= 0) & (dist < WINDOW) & ((si >= WINDOW) | (n > 0))


def _lane_cat(ref, row0, rows):
    return jnp.concatenate([ref[row0 + i * rows:row0 + (i + 1) * rows, :] for i in range(Q_PER_KV)], axis=1)


def _attn_fwd(proj, sinks):
    t = proj.shape[1]
    nb = t // WINDOW
    scale = HEAD_DIM ** -0.5

    def body(s_ref, q_ref, kc_ref, kp_ref, vc_ref, vp_ref, o_ref, lse_ref):
        n = pl.program_id(0)
        valid = _attn_mask(n)
        for g in range(N_KV_HEADS):
            rows = slice(g * HEAD_DIM, (g + 1) * HEAD_DIM)
            kt = jnp.concatenate([kp_ref[rows, :], kc_ref[rows, :]], axis=1).astype(BF16)
            vt = jnp.concatenate([vp_ref[rows, :], vc_ref[rows, :]], axis=1).astype(BF16)
            qcat = (_lane_cat(q_ref, g * Q_PER_KV * HEAD_DIM, HEAD_DIM) * scale).astype(BF16)
            s = jnp.where(valid, _dot_tn(kt, qcat), NEG)
            sink = jnp.concatenate(
                [jnp.full((1, WINDOW), s_ref[g * Q_PER_KV + i], F32) for i in range(Q_PER_KV)], axis=1)
            m = jnp.maximum(jnp.max(s, axis=0, keepdims=True), sink)
            p = jnp.exp(s - m)
            denom = _colsum(p) + jnp.exp(sink - m)
            probs = (p / denom).astype(BF16)
            out = _dot(vt, probs)
            lse = m + jnp.log(denom)
            for i in range(Q_PER_KV):
                h = g * Q_PER_KV + i
                o_ref[h * HEAD_DIM:(h + 1) * HEAD_DIM, :] = out[:, i * WINDOW:(i + 1) * WINDOW]
                lse_ref[h:h + 1, :] = lse[:, i * WINDOW:(i + 1) * WINDOW]

    kb = OFF_K // KV_DIM
    vb = OFF_V // KV_DIM
    prev = lambda n: jnp.maximum(n - 1, 0)
    return pl.pallas_call(
        body,
        name="attn_fwd",
        grid=(nb,),
        in_specs=[
            pl.BlockSpec(memory_space=pltpu.SMEM),
            pl.BlockSpec((Q_DIM, WINDOW), lambda n: (0, n)),
            pl.BlockSpec((KV_DIM, WINDOW), lambda n: (kb, n)),
            pl.BlockSpec((KV_DIM, WINDOW), lambda n: (kb, prev(n))),
            pl.BlockSpec((KV_DIM, WINDOW), lambda n: (vb, n)),
            pl.BlockSpec((KV_DIM, WINDOW), lambda n: (vb, prev(n))),
        ],
        out_specs=[pl.BlockSpec((Q_DIM, WINDOW), lambda n: (0, n)), pl.BlockSpec((N_Q_HEADS, WINDOW), lambda n: (0, n))],
        out_shape=[jax.ShapeDtypeStruct((Q_DIM, t), F32), jax.ShapeDtypeStruct((N_Q_HEADS, t), F32)],
        compiler_params=_cparams("parallel"),
    )(sinks, proj, proj, proj, proj, proj)


def _attn_bwd(proj, sinks, out, lse, dout, send=None):
    t = proj.shape[1]
    nb = t // WINDOW
    scale = HEAD_DIM ** -0.5

    def body(s_ref, q_ref, kc_ref, kp_ref, vc_ref, vp_ref, o_ref, lse_ref, do_ref,
             dq_ref, dk_ref, dv_ref, ds_ref, dk_carry, dv_carry):
        step = pl.program_id(0)
        n = nb - 1 - step

        @pl.when(step == 0)
        def _():
            dk_carry[...] = jnp.zeros_like(dk_carry)
            dv_carry[...] = jnp.zeros_like(dv_carry)
            ds_ref[...] = jnp.zeros_like(ds_ref)

        valid = _attn_mask(n)
        for g in range(N_KV_HEADS):
            rows = slice(g * HEAD_DIM, (g + 1) * HEAD_DIM)
            q0 = g * Q_PER_KV * HEAD_DIM
            kt = jnp.concatenate([kp_ref[rows, :], kc_ref[rows, :]], axis=1).astype(BF16)
            vt = jnp.concatenate([vp_ref[rows, :], vc_ref[rows, :]], axis=1).astype(BF16)
            qf = _lane_cat(q_ref, q0, HEAD_DIM)
            qcat = qf.astype(BF16)
            ocat = _lane_cat(o_ref, q0, HEAD_DIM)
            docat = _lane_cat(do_ref, q0, HEAD_DIM)
            dob = docat.astype(BF16)
            lse_cat = jnp.concatenate(
                [lse_ref[g * Q_PER_KV + i:g * Q_PER_KV + i + 1, :] for i in range(Q_PER_KV)], axis=1)
            sink = jnp.concatenate(
                [jnp.full((1, WINDOW), s_ref[g * Q_PER_KV + i], F32) for i in range(Q_PER_KV)], axis=1)
            s = jnp.where(valid, _dot_tn(kt, (qf * scale).astype(BF16)), NEG)
            p = jnp.exp(s - lse_cat)
            dp = _dot_tn(vt, dob)
            delta = _colsum(docat * ocat)
            dsc = (p * (dp - delta)).astype(BF16)
            dsink_row = -jnp.exp(sink - lse_cat) * delta
            dq = _dot(kt, dsc) * scale
            dk = _dot_nt(qcat, dsc) * scale
            dv = _dot_nt(dob, p.astype(BF16))
            for i in range(Q_PER_KV):
                h = g * Q_PER_KV + i
                dq_ref[h * HEAD_DIM:(h + 1) * HEAD_DIM, :] = dq[:, i * WINDOW:(i + 1) * WINDOW].astype(BF16)
                ds_ref[h:h + 1, :] += _rowsum(dsink_row[:, i * WINDOW:(i + 1) * WINDOW])
            dk_ref[rows, :] = (dk[:, WINDOW:] + dk_carry[rows, :]).astype(BF16)
            dv_ref[rows, :] = (dv[:, WINDOW:] + dv_carry[rows, :]).astype(BF16)
            dk_carry[rows, :] = dk[:, :WINDOW]
            dv_carry[rows, :] = dv[:, :WINDOW]

    kb = OFF_K // KV_DIM
    vb = OFF_V // KV_DIM
    cur = lambda i: nb - 1 - i
    prev = lambda i: jnp.maximum(nb - 2 - i, 0)
    qspec = pl.BlockSpec((Q_DIM, WINDOW), lambda i: (0, cur(i)))
    kvspec = pl.BlockSpec((KV_DIM, WINDOW), lambda i: (0, cur(i)))
    return _call(
        body,
        name="attn_bwd",
        grid=(nb,),
        in_specs=[
            pl.BlockSpec(memory_space=pltpu.SMEM),
            qspec,
            pl.BlockSpec((KV_DIM, WINDOW), lambda i: (kb, cur(i))),
            pl.BlockSpec((KV_DIM, WINDOW), lambda i: (kb, prev(i))),
            pl.BlockSpec((KV_DIM, WINDOW), lambda i: (vb, cur(i))),
            pl.BlockSpec((KV_DIM, WINDOW), lambda i: (vb, prev(i))),
            qspec,
            pl.BlockSpec((N_Q_HEADS, WINDOW), lambda i: (0, cur(i))),
            qspec,
        ],
        out_specs=[qspec, kvspec, kvspec, pl.BlockSpec((N_Q_HEADS, 1), lambda i: (0, 0))],
        out_shape=[
            jax.ShapeDtypeStruct((Q_DIM, t), BF16),
            jax.ShapeDtypeStruct((KV_DIM, t), BF16),
            jax.ShapeDtypeStruct((KV_DIM, t), BF16),
            jax.ShapeDtypeStruct((N_Q_HEADS, 1), F32),
        ],
        scratch_shapes=[pltpu.VMEM((KV_DIM, WINDOW), F32), pltpu.VMEM((KV_DIM, WINDOW), F32)],
        semantics=("arbitrary",), args=(sinks, proj, proj, proj, proj, proj, out, lse, dout), send=send)


CONV_ROWS = 256


def _conv_silu_fwd(proj, w_col, b_col):
    t = proj.shape[1]
    r0 = OFF_X // CONV_ROWS

    def body(x_ref, w_ref, b_ref, o_ref):
        def strip(rows):
            y, _ = _causal_conv(x_ref[rows, :], w_ref[rows, :], b_ref[rows, :])
            o_ref[rows, :] = y * _sigmoid(y)

        strip(slice(None))

    return pl.pallas_call(
        body,
        name="ssd_conv_fwd",
        grid=(XBC_DIM // CONV_ROWS,),
        in_specs=[
            pl.BlockSpec((CONV_ROWS, t), lambda i: (r0 + i, 0)),
            pl.BlockSpec((CONV_ROWS, SSD_CONV), lambda i: (i, 0)),
            pl.BlockSpec((CONV_ROWS, 1), lambda i: (i, 0)),
        ],
        out_specs=pl.BlockSpec((CONV_ROWS, t), lambda i: (i, 0)),
        out_shape=jax.ShapeDtypeStruct((XBC_DIM, t), F32),
        compiler_params=_cparams("parallel"),
    )(proj, w_col, b_col)


def _conv_silu_bwd(proj, w_col, b_col, dout, row0, dproj, name):
    t = proj.shape[1]
    nrows = dout.shape[0]
    p0 = (OFF_X + row0) // CONV_ROWS
    c0 = row0 // CONV_ROWS

    def body(x_ref, w_ref, b_ref, do_ref, dproj_ref, dx_ref, dwb_ref):
        def strip(rows):
            xv = x_ref[rows, :]
            wv = w_ref[rows, :]
            y, shifted = _causal_conv(xv, wv, b_ref[rows, :])
            sg = _sigmoid(y)
            dy = do_ref[rows, :] * (sg * (1.0 + y * (1.0 - sg)))
            dx, dwb_ref[rows, :] = _causal_conv_bwd(dy, xv, shifted, wv)
            dx_ref[rows, :] = dx.astype(BF16)

        strip(slice(None))

    return pl.pallas_call(
        body,
        name=name,
        grid=(nrows // CONV_ROWS,),
        in_specs=[
            pl.BlockSpec((CONV_ROWS, t), lambda i: (p0 + i, 0)),
            pl.BlockSpec((CONV_ROWS, SSD_CONV), lambda i: (c0 + i, 0)),
            pl.BlockSpec((CONV_ROWS, 1), lambda i: (c0 + i, 0)),
            pl.BlockSpec((CONV_ROWS, t), lambda i: (i, 0)),
            pl.BlockSpec(memory_space=pl.ANY),
        ],
        out_specs=[pl.BlockSpec((CONV_ROWS, t), lambda i: (p0 + i, 0)), pl.BlockSpec((CONV_ROWS, 128), lambda i: (i, 0))],
        out_shape=[jax.ShapeDtypeStruct(dproj.shape, BF16), jax.ShapeDtypeStruct((nrows, 128), F32)],
        input_output_aliases={4: 0},
        compiler_params=_cparams("parallel"),
    )(proj, w_col, b_col, dout, dproj)


GROUP_ROWS = HEADS_PER_GROUP * SSD_HEAD_DIM


def _ssd_specs(order):
    xb = D_INNER // BC_DIM
    dtb = OFF_DT // N_SSD_HEADS
    col = pl.BlockSpec((N_SSD_HEADS, 1), lambda c: (0, 0))
    return [
        pl.BlockSpec((D_INNER, CHUNK), lambda c: (0, order(c))),
        pl.BlockSpec((BC_DIM, CHUNK), lambda c: (xb, order(c))),
        pl.BlockSpec((BC_DIM, CHUNK), lambda c: (xb + 1, order(c))),
        pl.BlockSpec((N_SSD_HEADS, CHUNK), lambda c: (dtb, order(c))),
        col, col, col,
    ]


def _ssd_common(dt_ref, dtb_ref, alog_ref):
    z = dt_ref[...] + dtb_ref[...]
    dt = _softplus(z)
    a_neg = -jnp.exp(alog_ref[...])
    d_a = dt * a_neg
    row = lax.broadcasted_iota(jnp.int32, (CHUNK, CHUNK), 0)
    colm = lax.broadcasted_iota(jnp.int32, (CHUNK, CHUNK), 1)
    upper = (row <= colm).astype(F32)
    a_cs = jnp.dot(d_a, upper, precision=HIGHEST, preferred_element_type=F32)
    a_last = _rowsum(d_a)
    return z, dt, a_neg, a_cs, a_last, row >= colm, row == colm


def _decay(a_row, causal):
    a_s = jnp.broadcast_to(a_row, (CHUNK, CHUNK))
    seg = a_s.T - a_s
    return jnp.where(causal, jnp.exp(jnp.where(causal, seg, 0.0)), 0.0)


def _ssd_fwd(xbc, proj, dtb_col, alog_col, dsk_col):
    t = xbc.shape[1]
    nc = t // CHUNK

    def body(xs_ref, b_ref, c_ref, dt_ref, dtb_ref, alog_ref, dsk_ref, y_ref, hst_ref, h_scr):
        @pl.when(pl.program_id(0) == 0)
        def _():
            h_scr[...] = jnp.zeros_like(h_scr)

        _, dt, _, a_cs, a_last, causal, _ = _ssd_common(dt_ref, dtb_ref, alog_ref)
        hst_ref[0] = h_scr[...]
        dsk = dsk_ref[...]
        for g in range(N_SSD_GROUPS):
            grows = slice(g * D_STATE, (g + 1) * D_STATE)
            bb = b_ref[grows, :].astype(BF16)
            cb_ = c_ref[grows, :].astype(BF16)
            cb = _dot_tn(cb_, bb)
            for j in range(g * HEADS_PER_GROUP, (g + 1) * HEADS_PER_GROUP):
                rows = slice(j * SSD_HEAD_DIM, (j + 1) * SSD_HEAD_DIM)
                a = a_cs[j:j + 1, :]
                m = (cb * _decay(a, causal)).astype(BF16)
                xs = xs_ref[rows, :]
                xc = xs * dt[j:j + 1, :]
                hj = h_scr[rows, :]
                y = _dot_nt(xc.astype(BF16), m) + _dot(hj.astype(BF16), cb_) * jnp.exp(a) + dsk[j:j + 1, :] * xs
                y_ref[rows, :] = y
                al = a_last[j:j + 1, :]
                w = jnp.exp(al - a)
                h_scr[rows, :] = jnp.exp(al) * hj + _dot_nt((xc * w).astype(BF16), bb)

    return pl.pallas_call(
        body,
        name="ssd_fwd",
        grid=(nc,),
        in_specs=_ssd_specs(lambda c: c),
        out_specs=[
            pl.BlockSpec((D_INNER, CHUNK), lambda c: (0, c)),
            pl.BlockSpec((1, D_INNER, D_STATE), lambda c: (c, 0, 0)),
        ],
        out_shape=[
            jax.ShapeDtypeStruct((D_INNER, t), F32),
            jax.ShapeDtypeStruct((nc, D_INNER, D_STATE), F32),
        ],
        scratch_shapes=[pltpu.VMEM((D_INNER, D_STATE), F32)],
        compiler_params=_cparams("arbitrary"),
    )(xbc, xbc, xbc, proj, dtb_col, alog_col, dsk_col)


def _ssd_bwd(xbc, proj, dtb_col, alog_col, dsk_col, hst, dy):
    t = xbc.shape[1]
    nc = t // CHUNK
    rev = lambda c: nc - 1 - c

    def body(xs_ref, b_ref, c_ref, dt_ref, dtb_ref, alog_ref, dsk_ref, hst_ref, dy_ref,
             dxs_ref, db_ref, dc_ref, ddt_ref, dalog_ref, ddsk_ref, ddtb_ref, dh_scr, da_scr, ddt_scr, dd_scr):
        @pl.when(pl.program_id(0) == 0)
        def _():
            dh_scr[...] = jnp.zeros_like(dh_scr)
            dalog_ref[...] = jnp.zeros_like(dalog_ref)
            ddsk_ref[...] = jnp.zeros_like(ddsk_ref)
            ddtb_ref[...] = jnp.zeros_like(ddtb_ref)

        z, dt, a_neg, a_cs, a_last, causal, eye = _ssd_common(dt_ref, dtb_ref, alog_ref)
        dsk = dsk_ref[...]
        last_lane = lax.broadcasted_iota(jnp.int32, (1, CHUNK), 1) == CHUNK - 1
        for g in range(N_SSD_GROUPS):
            grows = slice(g * D_STATE, (g + 1) * D_STATE)
            bb = b_ref[grows, :].astype(BF16)
            cb_ = c_ref[grows, :].astype(BF16)
            cb = _dot_tn(cb_, bb)
            dcb = jnp.zeros((CHUNK, CHUNK), F32)
            dc_acc = jnp.zeros((D_STATE, CHUNK), F32)
            db_acc = jnp.zeros((D_STATE, CHUNK), F32)
            for j in range(g * HEADS_PER_GROUP, (g + 1) * HEADS_PER_GROUP):
                rows = slice(j * SSD_HEAD_DIM, (j + 1) * SSD_HEAD_DIM)
                a = a_cs[j:j + 1, :]
                al = a_last[j:j + 1, :]
                lam = _decay(a, causal)
                mf = cb * lam
                xs = xs_ref[rows, :]
                dtj = dt[j:j + 1, :]
                xc = xs * dtj
                w = jnp.exp(al - a)
                e = jnp.exp(a)
                gam = jnp.exp(al)
                hj = hst_ref[0, rows, :]
                hjb = hj.astype(BF16)
                dyv = dy_ref[rows, :]
                dyb = dyv.astype(BF16)
                dd_scr[j:j + 1, :] = _colsum(dyv * xs)
                gb = (dyv * e).astype(BF16)
                dh_in = _dot_nt(gb, cb_)
                dc_acc = dc_acc + _dot_tn(hjb, gb)
                yoff = _dot(hjb, cb_) * e
                da = _colsum(dyv * yoff)
                dm = _dot_tn(dyb, xc.astype(BF16))
                dxc = _dot(dyb, mf.astype(BF16))
                dcb = dcb + dm * lam
                nmat = dm * mf
                rs = jnp.broadcast_to(_rowsum(nmat), (CHUNK, CHUNK))
                da = da + _colsum(jnp.where(eye, rs, 0.0)) - _colsum(nmat)
                ds = dh_scr[rows, :]
                dsb = ds.astype(BF16)
                t1 = _dot(dsb, bb)
                xcw = xc * w
                dxc = dxc + w * t1
                dww = _colsum(xcw * t1)
                da_l = _rowsum(dww) + _rowsum(_colsum(ds * hj)) * gam
                da = da - dww + jnp.where(last_lane, da_l, 0.0)
                db_acc = db_acc + _dot_tn(dsb, xcw.astype(BF16))
                dh_scr[rows, :] = gam * ds + dh_in
                dxs_ref[rows, :] = dsk[j:j + 1, :] * dyv + dxc * dtj
                da_scr[j:j + 1, :] = da
                ddt_scr[j:j + 1, :] = _colsum(dxc * xs)
            dcbb = dcb.astype(BF16)
            dc_ref[grows, :] = dc_acc + _dot_nt(bb, dcbb)
            db_ref[grows, :] = db_acc + _dot(cb_, dcbb)
        dda = jnp.dot(da_scr[...], causal.astype(F32), precision=HIGHEST, preferred_element_type=F32)
        ddt = ddt_scr[...] + dda * a_neg
        ddt_raw = ddt * _sigmoid(z)
        ddt_ref[...] = ddt_raw
        ddtb_ref[...] += _rowsum(ddt_raw)
        dalog_ref[...] += _rowsum(dda * dt) * a_neg
        ddsk_ref[...] += _rowsum(dd_scr[...])

    col = pl.BlockSpec((N_SSD_HEADS, 1), lambda c: (0, 0))
    bc = pl.BlockSpec((BC_DIM, CHUNK), lambda c: (0, rev(c)))
    xs_spec = pl.BlockSpec((D_INNER, CHUNK), lambda c: (0, rev(c)))
    small = pltpu.VMEM((N_SSD_HEADS, CHUNK), F32)
    return pl.pallas_call(
        body,
        name="ssd_bwd",
        grid=(nc,),
        in_specs=_ssd_specs(rev) + [pl.BlockSpec((1, D_INNER, D_STATE), lambda c: (rev(c), 0, 0)), xs_spec],
        out_specs=[xs_spec, bc, bc, pl.BlockSpec((N_SSD_HEADS, CHUNK), lambda c: (0, rev(c))), col, col, col],
        out_shape=[
            jax.ShapeDtypeStruct((D_INNER, t), F32),
            jax.ShapeDtypeStruct((BC_DIM, t), F32),
            jax.ShapeDtypeStruct((BC_DIM, t), F32),
            jax.ShapeDtypeStruct((N_SSD_HEADS, t), F32),
            jax.ShapeDtypeStruct((N_SSD_HEADS, 1), F32),
            jax.ShapeDtypeStruct((N_SSD_HEADS, 1), F32),
            jax.ShapeDtypeStruct((N_SSD_HEADS, 1), F32),
        ],
        scratch_shapes=[pltpu.VMEM((D_INNER, D_STATE), F32), small, small, small],
        compiler_params=_cparams("arbitrary"),
    )(xbc, xbc, xbc, proj, dtb_col, alog_col, dsk_col, hst, dy)


GN_ROWS = D_INNER // N_SSD_GROUPS


def _gnorm_fwd(y, proj, w_col):
    t = y.shape[1]
    tt = _tile(t, (512, 256, 128))
    z0 = OFF_Z // GN_ROWS

    def body(y_ref, z_ref, w_ref, o_ref):
        zv = z_ref[...]
        u = y_ref[...] * (zv * _sigmoid(zv))
        r = lax.rsqrt(jnp.mean(u * u, axis=0, keepdims=True) + EPS)
        o_ref[...] = (u * r * w_ref[...]).astype(BF16)

    blk = pl.BlockSpec((GN_ROWS, tt), lambda g, i: (g, i))
    return pl.pallas_call(
        body,
        name="gnorm_fwd",
        grid=(N_SSD_GROUPS, t // tt),
        in_specs=[blk, pl.BlockSpec((GN_ROWS, tt), lambda g, i: (z0 + g, i)), pl.BlockSpec((GN_ROWS, 1), lambda g, i: (g, 0))],
        out_specs=blk,
        out_shape=jax.ShapeDtypeStruct((D_INNER, t), BF16),
        compiler_params=_cparams("parallel", "parallel"),
    )(y, proj, w_col)


def _gnorm_bwd(dout, y, proj, w_col, send=None):
    t = y.shape[1]
    tt = _tile(t, (512, 256, 128))
    z0 = OFF_Z // GN_ROWS

    def body(do_ref, y_ref, z_ref, w_ref, dy_ref, dz_ref, dw_ref):
        @pl.when(pl.program_id(1) == 0)
        def _():
            dw_ref[...] = jnp.zeros_like(dw_ref)

        zv = z_ref[...]
        yv = y_ref[...]
        sg = _sigmoid(zv)
        sz = zv * sg
        u = yv * sz
        r = lax.rsqrt(jnp.mean(u * u, axis=0, keepdims=True) + EPS)
        xhat = u * r
        dov = do_ref[...]
        dw_ref[...] += _rowsum(dov * xhat)
        dxhat = dov * w_ref[...]
        du = r * (dxhat - xhat * jnp.mean(dxhat * xhat, axis=0, keepdims=True))
        dy_ref[...] = du * sz
        dz_ref[...] = (du * yv * (sg * (1.0 + zv * (1.0 - sg)))).astype(BF16)

    blk = pl.BlockSpec((GN_ROWS, tt), lambda g, i: (g, i))
    col = pl.BlockSpec((GN_ROWS, 1), lambda g, i: (g, 0))
    return _call(
        body,
        name="gnorm_bwd",
        grid=(N_SSD_GROUPS, t // tt),
        in_specs=[blk, blk, pl.BlockSpec((GN_ROWS, tt), lambda g, i: (z0 + g, i)), col],
        out_specs=[blk, pl.BlockSpec((GN_ROWS, tt), lambda g, i: (z0 + g, i)), col],
        out_shape=[jax.ShapeDtypeStruct((D_INNER, t), F32), jax.ShapeDtypeStruct((IN_DIM, t), BF16),
                   jax.ShapeDtypeStruct((D_INNER, 1), F32)],
        semantics=("parallel", "arbitrary"), args=(dout, y, proj, w_col), send=send)


GATE_ROWS = 128


def _gate_specs(t):
    nr = D_MODEL // GATE_ROWS
    blk = pl.BlockSpec((GATE_ROWS, t), lambda r: (r, 0))
    rows_from = lambda first: pl.BlockSpec(
        (pl.Element(GATE_ROWS), pl.Element(t)), lambda r: (pl.multiple_of(first + GATE_ROWS * r, N_SSD_HEADS), 0))
    return blk, [
        rows_from(OFF_GA),
        rows_from(OFF_GS),
        pl.BlockSpec((GATE_ROWS, 1), lambda r: (r, 0)),
        pl.BlockSpec((GATE_ROWS, 1), lambda r: (nr + r, 0)),
        blk, blk,
    ]


def _gate_fwd(proj, b_col, attn, ssd):
    t = proj.shape[1]
    blk, specs = _gate_specs(t)

    def body(ga_ref, gs_ref, ba_ref, bs_ref, a_ref, s_ref, o_ref):
        o_ref[...] = (_sigmoid(ga_ref[...] + ba_ref[...]) * a_ref[...]
                      + _sigmoid(gs_ref[...] + bs_ref[...]) * s_ref[...]).astype(BF16)

    return pl.pallas_call(
        body,
        name="gate_fwd",
        grid=(D_MODEL // GATE_ROWS,),
        in_specs=specs,
        out_specs=blk,
        out_shape=jax.ShapeDtypeStruct((D_MODEL, t), BF16),
        compiler_params=_cparams("parallel"),
    )(proj, proj, b_col, b_col, attn, ssd)


def _gate_bwd(proj, b_col, attn, ssd, dmix, send=None):
    t = proj.shape[1]
    blk, specs = _gate_specs(t)

    def body(ga_ref, gs_ref, ba_ref, bs_ref, a_ref, s_ref, dm_ref, da_ref, dso_ref, dga_ref, dgs_ref, dba_ref, dbs_ref):
        dm = dm_ref[...]
        sa = _sigmoid(ga_ref[...] + ba_ref[...])
        ss = _sigmoid(gs_ref[...] + bs_ref[...])
        da_ref[...] = (dm * sa).astype(BF16)
        dso_ref[...] = (dm * ss).astype(BF16)
        dga = dm * a_ref[...] * sa * (1.0 - sa)
        dgs = dm * s_ref[...] * ss * (1.0 - ss)
        dga_ref[...] = dga.astype(BF16)
        dgs_ref[...] = dgs.astype(BF16)
        dba_ref[...] = _rowsum(dga)
        dbs_ref[...] = _rowsum(dgs)

    col = pl.BlockSpec((GATE_ROWS, 1), lambda r: (r, 0))
    act = jax.ShapeDtypeStruct((D_MODEL, t), BF16)
    bias = jax.ShapeDtypeStruct((D_MODEL, 1), F32)
    return _call(
        body,
        name="gate_bwd",
        grid=(D_MODEL // GATE_ROWS,),
        in_specs=specs + [blk],
        out_specs=[blk, blk, blk, blk, col, col],
        out_shape=[act, act, act, act, bias, bias],
        semantics=("parallel",), args=(proj, proj, b_col, b_col, attn, ssd, dmix), send=send)


FFN_ROWS = 256


def _ffn_fwd(u0, w_col, b_col):
    t = u0.shape[2]

    def body(u_ref, w_ref, b_ref, o_ref):
        def strip(rows):
            val, _ = _causal_conv(u_ref[0, rows, :], w_ref[0, rows, :], b_ref[0, rows, :])
            gt, _ = _causal_conv(u_ref[1, rows, :], w_ref[1, rows, :], b_ref[1, rows, :])
            o_ref[rows, :] = (gt * _sigmoid(gt) * val).astype(BF16)

        strip(slice(None))

    return pl.pallas_call(
        body,
        name="ffn_fwd",
        grid=(D_FF // FFN_ROWS,),
        in_specs=[
            pl.BlockSpec((2, FFN_ROWS, t), lambda i: (0, i, 0)),
            pl.BlockSpec((2, FFN_ROWS, FFN_CONV), lambda i: (0, i, 0)),
            pl.BlockSpec((2, FFN_ROWS, 1), lambda i: (0, i, 0)),
        ],
        out_specs=pl.BlockSpec((FFN_ROWS, t), lambda i: (i, 0)),
        out_shape=jax.ShapeDtypeStruct((D_FF, t), BF16),
        compiler_params=_cparams("parallel"),
    )(u0, w_col, b_col)


def _ffn_bwd(u0, w_col, b_col, dg, send=None):
    t = u0.shape[2]

    def body(u_ref, w_ref, b_ref, dg_ref, du_ref, dwb_ref):
        def strip(rows):
            xval, wval = u_ref[0, rows, :], w_ref[0, rows, :]
            xgt, wgt = u_ref[1, rows, :], w_ref[1, rows, :]
            val, sh_val = _causal_conv(xval, wval, b_ref[0, rows, :])
            gt, sh_gt = _causal_conv(xgt, wgt, b_ref[1, rows, :])
            sg = _sigmoid(gt)
            dgv = dg_ref[rows, :]
            dval = dgv * (gt * sg)
            dgt = dgv * val * (sg * (1.0 + gt * (1.0 - sg)))
            dx, dwb_ref[0, rows, :] = _causal_conv_bwd(dval, xval, sh_val, wval)
            du_ref[0, rows, :] = dx.astype(BF16)
            dx, dwb_ref[1, rows, :] = _causal_conv_bwd(dgt, xgt, sh_gt, wgt)
            du_ref[1, rows, :] = dx.astype(BF16)

        strip(slice(None))

    return _call(
        body,
        name="ffn_bwd",
        grid=(D_FF // FFN_ROWS,),
        in_specs=[
            pl.BlockSpec((2, FFN_ROWS, t), lambda i: (0, i, 0)),
            pl.BlockSpec((2, FFN_ROWS, FFN_CONV), lambda i: (0, i, 0)),
            pl.BlockSpec((2, FFN_ROWS, 1), lambda i: (0, i, 0)),
            pl.BlockSpec((FFN_ROWS, t), lambda i: (i, 0)),
        ],
        out_specs=[pl.BlockSpec((2, FFN_ROWS, t), lambda i: (0, i, 0)), pl.BlockSpec((2, FFN_ROWS, 128), lambda i: (0, i, 0))],
        out_shape=[jax.ShapeDtypeStruct((2, D_FF, t), BF16), jax.ShapeDtypeStruct((2, D_FF, 128), F32)],
        semantics=("parallel",), args=(u0, w_col, b_col, dg), send=send)


def _adamw_math(w, g, m, v):
    m = ADAM_B1 * m + (1.0 - ADAM_B1) * g
    v = ADAM_B2 * v + (1.0 - ADAM_B2) * (g * g)
    m_hat = m / (1.0 - ADAM_B1 ** ADAM_STEP)
    v_hat = v / (1.0 - ADAM_B2 ** ADAM_STEP)
    delta = -ADAM_LR * (m_hat / (jnp.sqrt(v_hat) + ADAM_EPS) + ADAM_WD * w)
    return delta, m, v


def _adamw_sharded(parts, w, m, v, name):
    r, c = w.shape[0], w.shape[-1]
    tc = _tile(c, (256, 128))
    blk_shape = (r, tc) if w.ndim == 2 else (r, 1, tc)
    slots = parts.shape[0]

    def body(p_ref, w_ref, m_ref, v_ref, g_ref, d_ref, nm_ref, nv_ref):
        g = p_ref[0].astype(F32)
        for s in range(1, slots):
            g = g + p_ref[s].astype(F32)
        flat = lambda ref: ref[...].reshape(r, tc)
        d, nm, nv = _adamw_math(flat(w_ref), g, flat(m_ref), flat(v_ref))
        for ref, val in ((g_ref, g), (d_ref, d), (nm_ref, nm), (nv_ref, nv)):
            ref[...] = val.reshape(blk_shape)

    blk = pl.BlockSpec(blk_shape, (lambda i: (0, i)) if w.ndim == 2 else (lambda i: (0, 0, i)))
    out = jax.ShapeDtypeStruct(w.shape, F32)
    return pl.pallas_call(
        body,
        name=name,
        grid=(c // tc,),
        in_specs=[pl.BlockSpec((slots, r, tc), lambda i: (0, 0, i)), blk, blk, blk],
        out_specs=[blk, blk, blk, blk],
        out_shape=[out, out, out, out],
        compiler_params=_cparams("parallel"),
    )(parts, w, m, v)


def _lane_offsets(sizes):
    offsets, pos = [], 0
    for n in sizes:
        offsets.append(pos)
        pos += -(-n // 128) * 128
    return offsets, pos


def _pack_row(parts):
    rows = [p.reshape(1, -1).astype(F32) for p in parts]
    return jnp.concatenate([jnp.pad(r, ((0, 0), (0, -r.shape[1] % 128))) for r in rows], axis=1)


def _small_update(parts, me, full_sizes, ws, ms, vs):
    n = len(ws)
    offsets, _ = _lane_offsets([1] + list(full_sizes))

    def body(me_ref, p_ref, *refs):
        w_refs, m_refs, v_refs = refs[:n], refs[n:2 * n], refs[2 * n:3 * n]
        scalar_ref, out_refs = refs[3 * n], refs[3 * n + 1:]
        tot = p_ref[0]
        for s in range(1, N_DEV):
            tot = tot + p_ref[s]
        scalar_ref[...] = tot[:, 0:1]
        for k in range(n):
            g_ref, d_ref, nm_ref, nv_ref = out_refs[4 * k:4 * k + 4]
            taps, cols = w_refs[k].shape
            if taps == 1:
                g_ref[...] = tot[:, offsets[k + 1]:offsets[k + 1] + cols]
            else:
                full = full_sizes[k] // taps
                for tap in range(taps):
                    mine = jnp.zeros((1, cols), F32)
                    for d in range(N_DEV):
                        lo = offsets[k + 1] + tap * full + d * cols
                        mine = jnp.where(me_ref[0] == d, tot[:, lo:lo + cols], mine)
                    g_ref[tap:tap + 1, :] = mine
            d_ref[...], nm_ref[...], nv_ref[...] = _adamw_math(w_refs[k][...], g_ref[...], m_refs[k][...], v_refs[k][...])

    vmem = pl.BlockSpec(memory_space=pltpu.VMEM)
    out_shape = [jax.ShapeDtypeStruct((1, 1), F32)]
    for wk in ws:
        out_shape += [jax.ShapeDtypeStruct(wk.shape, F32)] * 4
    res = pl.pallas_call(
        body,
        name="small_update",
        in_specs=[pl.BlockSpec(memory_space=pltpu.SMEM)] + [vmem] * (1 + 3 * n),
        out_specs=[vmem] * len(out_shape),
        out_shape=out_shape,
    )(me, parts, *ws, *ms, *vs)
    return res[0], [res[1 + 4 * k:5 + 4 * k] for k in range(n)]


ANY = pl.BlockSpec(memory_space=pl.ANY)
FLIPS = [(k >> 2 & 1, k >> 1 & 1, k & 1) for k in range(1, N_DEV)]


def _place():
    return lax.axis_index("x"), lax.axis_index("y"), lax.axis_index("c")


HBM = pl.BlockSpec(memory_space=pltpu.HBM)
SEM = pl.BlockSpec(memory_space=pltpu.SEMAPHORE)
EFFECT = pltpu.SideEffectType.DATAFLOW_SIDE_EFFECTING


def _peer_copy(gather, src_ref, land_ref, send_sems, recv_sems, k, sending):
    x, y, c = _place()
    fx, fy, fc = FLIPS[k]
    me = 4 * x + 2 * y + c
    peer = 4 * (x ^ fx) + 2 * (y ^ fy) + (c ^ fc)
    return pltpu.make_async_remote_copy(
        src_ref=src_ref if gather else src_ref.at[peer],
        dst_ref=land_ref.at[me if sending else peer],
        send_sem=send_sems.at[k], recv_sem=recv_sems.at[k],
        device_id=(x ^ fx, y ^ fy, c ^ fc), device_id_type=MESH)


SIBLING = 0
OTHER_CHIPS = (1, 3, 5)


def _gather_start(srcs, name, via_sibling):
    n = len(srcs)
    lands = [lax.empty((N_DEV,) + s.shape, s.dtype) for s in srcs]

    def body(*refs):
        src_refs, land_refs = refs[:n], refs[n:2 * n]
        send, recv = refs[2 * n:3 * n], refs[3 * n:4 * n]
        for i in range(n):
            for k in (SIBLING,) + OTHER_CHIPS if via_sibling else range(N_DEV - 1):
                _peer_copy(True, src_refs[i], land_refs[i], send[i], recv[i], k, True).start()

    sem = pltpu.SemaphoreType.DMA((N_DEV - 1,))
    hbm = lambda a: pltpu.HBM(a.shape, a.dtype)
    res = pl.pallas_call(
        body,
        name=name,
        in_specs=[HBM] * (2 * n),
        out_specs=[SEM] * (2 * n) + [HBM] * (2 * n),
        out_shape=[sem] * (2 * n) + [hbm(s) for s in srcs] + [hbm(a) for a in lands],
        input_output_aliases={i: 2 * n + i for i in range(2 * n)},
        compiler_params=pltpu.CompilerParams(has_side_effects=EFFECT),
    )(*[pltpu.with_memory_space_constraint(a, pltpu.HBM) for a in list(srcs) + lands])
    return res[:n], res[n:2 * n], res[2 * n:3 * n], res[3 * n:4 * n]


def _exchange_wait(send_sems, recv_sems, src, land, after, gather, name):
    def body(src_ref, land_ref, send_ref, recv_ref, after_ref, src_out, land_out):
        for k in range(N_DEV - 1):
            cp = _peer_copy(gather, src_ref, land_ref, send_ref, recv_ref, k, False)
            cp.wait_send()
            cp.wait_recv()

    hbm = lambda a: pltpu.HBM(a.shape, a.dtype)
    return pl.pallas_call(
        body,
        name=name,
        in_specs=[HBM, HBM, SEM, SEM, ANY],
        out_specs=[HBM, HBM],
        out_shape=[hbm(src), hbm(land)],
        input_output_aliases={0: 0, 1: 1},
        compiler_params=pltpu.CompilerParams(has_side_effects=EFFECT),
    )(src, land, send_sems, recv_sems, after)


def _own_slot(src, land, me, gather):
    own = src[None] if gather else lax.dynamic_slice_in_dim(src, me, 1, axis=0)
    return lax.dynamic_update_slice_in_dim(land, own, me, axis=0)


def _forwarded_copy(land_ref, send_sems, recv_sems, j, sending):
    x, y, c = _place()
    fx, fy, _ = FLIPS[OTHER_CHIPS[j]]
    slot = 4 * (x ^ fx) + 2 * (y ^ fy) + (c if sending else 1 - c)
    return pltpu.make_async_remote_copy(
        src_ref=land_ref.at[slot], dst_ref=land_ref.at[slot], send_sem=send_sems.at[j], recv_sem=recv_sems.at[j],
        device_id=(x, y, 1 - c), device_id_type=MESH)


def _gather_forward(send_sems, recv_sems, srcs, lands, after, name):
    n = len(srcs)

    def body(*refs):
        src_refs, land_refs = refs[:n], refs[n:2 * n]
        send, recv = refs[2 * n:3 * n], refs[3 * n:4 * n]
        fwd_send, fwd_recv = refs[4 * n + 1:5 * n + 1], refs[5 * n + 1:6 * n + 1]
        for i in range(n):
            for j, k in enumerate(OTHER_CHIPS):
                _peer_copy(True, src_refs[i], land_refs[i], send[i], recv[i], k, False).wait_recv()
                _forwarded_copy(land_refs[i], fwd_send[i], fwd_recv[i], j, True).start()

    sem = pltpu.SemaphoreType.DMA((len(OTHER_CHIPS),))
    hbm = lambda a: pltpu.HBM(a.shape, a.dtype)
    res = pl.pallas_call(
        body,
        name=name,
        in_specs=[HBM] * (2 * n) + [SEM] * (2 * n) + [ANY],
        out_specs=[SEM] * (2 * n) + [HBM] * (2 * n),
        out_shape=[sem] * (2 * n) + [hbm(a) for a in srcs] + [hbm(a) for a in lands],
        input_output_aliases={i: 2 * n + i for i in range(2 * n)},
        compiler_params=pltpu.CompilerParams(has_side_effects=EFFECT),
    )(*srcs, *lands, *send_sems, *recv_sems, after)
    return res[:n], res[n:2 * n], res[2 * n:3 * n], res[3 * n:4 * n]


def _gather_wait_forwarded(send_sems, recv_sems, fwd_send, fwd_recv, src, land, after, name):
    def body(src_ref, land_ref, send_ref, recv_ref, fwd_send_ref, fwd_recv_ref, after_ref, src_out, land_out):
        for k in (SIBLING,) + OTHER_CHIPS:
            _peer_copy(True, src_ref, land_ref, send_ref, recv_ref, k, False).wait_send()
        _peer_copy(True, src_ref, land_ref, send_ref, recv_ref, SIBLING, False).wait_recv()
        for j in range(len(OTHER_CHIPS)):
            _forwarded_copy(land_ref, fwd_send_ref, fwd_recv_ref, j, True).wait_send()
            _forwarded_copy(land_ref, fwd_send_ref, fwd_recv_ref, j, False).wait_recv()

    hbm = lambda a: pltpu.HBM(a.shape, a.dtype)
    return pl.pallas_call(
        body,
        name=name,
        in_specs=[HBM, HBM, SEM, SEM, SEM, SEM, ANY],
        out_specs=[HBM, HBM],
        out_shape=[hbm(src), hbm(land)],
        input_output_aliases={0: 0, 1: 1},
        compiler_params=pltpu.CompilerParams(has_side_effects=EFFECT),
    )(src, land, send_sems, recv_sems, fwd_send, fwd_recv, after)


N_CHIPS = N_DEV // 2


def _pair_exchange(by_core, meanwhile, name):
    def copy(src_ref, land_ref, send_sems, recv_sems, q):
        x, y, c = _place()
        return pltpu.make_async_remote_copy(
            src_ref=src_ref.at[q, 1 - c], dst_ref=land_ref.at[q], send_sem=send_sems.at[q], recv_sem=recv_sems.at[q],
            device_id=(x, y, 1 - c), device_id_type=MESH)

    def start(src_ref, land_ref, send_sems, recv_sems, src_out, land_out):
        for q in range(N_CHIPS):
            copy(src_ref, land_ref, send_sems, recv_sems, q).start()

    def wait(src_ref, land_ref, send_sems, recv_sems, after_ref, src_out, land_out):
        for q in range(N_CHIPS):
            cp = copy(src_ref, land_ref, send_sems, recv_sems, q)
            cp.wait_send()
            cp.wait_recv()

    sem = pltpu.SemaphoreType.DMA((N_CHIPS,))
    hbm_src = pltpu.HBM(by_core.shape, by_core.dtype)
    hbm_land = pltpu.HBM(by_core.shape[:1] + by_core.shape[2:], by_core.dtype)
    params = pltpu.CompilerParams(has_side_effects=EFFECT)
    send_sems, recv_sems, src, land = pl.pallas_call(
        start, name=name + "_start", in_specs=[HBM, HBM], out_specs=[SEM, SEM, HBM, HBM],
        out_shape=[sem, sem, hbm_src, hbm_land], input_output_aliases={0: 2, 1: 3}, compiler_params=params,
    )(pltpu.with_memory_space_constraint(by_core, pltpu.HBM),
      pltpu.with_memory_space_constraint(lax.empty(hbm_land.shape, by_core.dtype), pltpu.HBM))
    return pl.pallas_call(
        wait, name=name + "_wait", in_specs=[HBM, HBM, SEM, SEM, ANY], out_specs=[HBM, HBM],
        out_shape=[hbm_src, hbm_land], input_output_aliases={0: 0, 1: 1}, compiler_params=params,
    )(src, land, send_sems, recv_sems, meanwhile(src))


def _pair_add(by_core, landed, name):
    q, _, r, c = by_core.shape
    tc = _tile(c, (512, 256, 128))

    def body(a_ref, b_ref, o_ref):
        mine = a_ref[0, lax.axis_index("c")]
        o_ref[0] = (mine.astype(F32) + b_ref[0].astype(F32)).astype(BF16)

    blk = pl.BlockSpec((1, r, tc), lambda i, j: (i, 0, j))
    return pl.pallas_call(
        body, name=name, grid=(q, c // tc),
        in_specs=[pl.BlockSpec((1, 2, r, tc), lambda i, j: (i, 0, 0, j)), blk], out_specs=blk,
        out_shape=jax.ShapeDtypeStruct(landed.shape, BF16), compiler_params=_cparams("parallel", "parallel"),
    )(by_core, landed)


def _chip_copy(src_ref, land_ref, send_sems, recv_sems, j, sending):
    x, y, c = _place()
    fx, fy, _ = FLIPS[OTHER_CHIPS[j]]
    here, there = 2 * x + y, 2 * (x ^ fx) + (y ^ fy)
    return pltpu.make_async_remote_copy(
        src_ref=src_ref.at[there], dst_ref=land_ref.at[here if sending else there],
        send_sem=send_sems.at[j], recv_sem=recv_sems.at[j],
        device_id=(x ^ fx, y ^ fy, c), device_id_type=MESH)


def _chip_wait(send_sems, recv_sems, src, land, after, name):
    def body(src_ref, land_ref, send_ref, recv_ref, after_ref, src_out, land_out):
        for j in range(len(OTHER_CHIPS)):
            cp = _chip_copy(src_ref, land_ref, send_ref, recv_ref, j, False)
            cp.wait_send()
            cp.wait_recv()

    hbm = lambda a: pltpu.HBM(a.shape, a.dtype)
    return pl.pallas_call(
        body,
        name=name,
        in_specs=[HBM, HBM, SEM, SEM, ANY],
        out_specs=[HBM, HBM],
        out_shape=[hbm(src), hbm(land)],
        input_output_aliases={0: 0, 1: 1},
        compiler_params=pltpu.CompilerParams(has_side_effects=EFFECT),
    )(src, land, send_sems, recv_sems, after)


def _col(v):
    return v.reshape(-1, 1).astype(F32)


def _local_step(x, tgt, started, weight, small, pair_sums, handles):
    t = x.shape[0]
    n1 = _col(small["norm1_w"])
    n2 = _col(small["norm2_w"])
    nf = _col(small["final_norm_w"])
    bg = _col(small["b_gate"])
    sinks = small["attn_sinks"].reshape(-1).astype(F32)
    cbias = _col(small["ssd_conv_b"])
    dtb = _col(small["dt_bias"])
    alog = _col(small["a_log"])
    dsk = _col(small["d_skip"])
    gnw = _col(small["ssd_norm_w"])
    fb = small["ffn_conv_b"].reshape(2, D_FF, 1)

    xt, xn = _norm_fwd_tokens(x, n1, started, "norm1_fwd")
    cw = weight("ssd_conv_w", xn).T
    fw = weight("ffn_conv_w", xn).T.reshape(2, D_FF, FFN_CONV)
    w_in_t = weight("w_in", xn)
    proj = _matmul(w_in_t, xn, nt=False, out_dtype=F32, name="mm_in")
    ao, lse = _attn_fwd(proj, sinks)
    w_ao = weight("w_attn_o", ao)
    attn = _matmul(w_ao, ao, nt=False, out_dtype=F32, name="mm_attn_o", tn_a=True)
    xbc = _conv_silu_fwd(proj, cw, cbias)
    y, hst = _ssd_fwd(xbc, proj, dtb, alog, dsk)
    yn = _gnorm_fwd(y, proj, gnw)
    w_so = weight("w_ssd_o", yn)
    ssd = _matmul(w_so, yn, nt=False, out_dtype=F32, name="mm_ssd_o", tn_a=True)
    mix = _gate_fwd(proj, bg, attn, ssd)
    w_out = weight("w_out", mix)
    h1 = _matmul(w_out, mix, nt=False, out_dtype=F32, name="mm_out", add=xt, tn_a=True)
    hn = _norm_fwd(h1, n2, "norm2_fwd")
    w_up_t = weight("w_up", hn)
    u0 = _matmul(w_up_t, hn, nt=False, out_dtype=F32, name="mm_up").reshape(2, D_FF, t)
    gl = _ffn_fwd(u0, fw, fb)
    w_down = weight("w_down", gl)
    h2 = _matmul(w_down, gl, nt=False, out_dtype=F32, name="mm_down", add=h1, tn_a=True)
    dh2, loss, d_nf = _final_norm_loss(h2, tgt, nf)

    g = {}

    def sending(weight_name, grad, fn, *args, **kwargs):
        chunks = grad if grad.ndim == 3 else grad.reshape(N_DEV, -1, D_MODEL)
        out, handles[weight_name] = fn(*args, send=chunks, **kwargs)
        return out

    g_down = _matmul(gl, dh2, nt=True, out_dtype=BF16, name="mm_d_w_down")
    dgl = _matmul(w_down, dh2, nt=False, out_dtype=F32, name="mm_d_glu")
    du0, d_fwb = sending("w_down", g_down, _ffn_bwd, u0, fw, fb, dgl)
    du0 = du0.reshape(2 * D_FF, t)
    g_up = _matmul(du0, hn, nt=True, out_dtype=BF16, name="mm_d_w_up")
    dhn = sending("w_up", g_up, _matmul, w_up_t, du0, nt=False, out_dtype=F32, name="mm_d_hn", tn_a=True)
    dh1, d_n2 = _norm_bwd(dhn, h1, n2, dh2, "norm2_bwd")
    g_out = _matmul(mix, dh1, nt=True, out_dtype=BF16, name="mm_d_w_out")
    dmix = _matmul(w_out, dh1, nt=False, out_dtype=F32, name="mm_d_mix")
    d_attn, d_ssd, d_ga, d_gs, d_ba, d_bs = sending("w_out", g_out, _gate_bwd, proj, bg, attn, ssd, dmix)
    g_ao = _matmul(ao, d_attn, nt=True, out_dtype=BF16, name="mm_d_w_attn_o")
    dao = _matmul(w_ao, d_attn, nt=False, out_dtype=F32, name="mm_d_ao")
    dq, dk, dv, d_sinks = sending("w_attn_o", g_ao, _attn_bwd, proj, sinks, ao, lse, dao)
    g_so = _matmul(yn, d_ssd, nt=True, out_dtype=BF16, name="mm_d_w_ssd_o")
    dyn = _matmul(w_so, d_ssd, nt=False, out_dtype=F32, name="mm_d_yn")
    dy, dproj, d_gnw = sending("w_ssd_o", g_so, _gnorm_bwd, dyn, y, proj, gnw)
    dxs, dbm, dcm, ddt, d_alog, d_dsk, d_dtb = _ssd_bwd(xbc, proj, dtb, alog, dsk, hst, dy)
    dproj, dwb_xs = _conv_silu_bwd(proj, cw, cbias, dxs, 0, dproj, "ssd_conv_bwd_x")
    dproj, dwb_b = _conv_silu_bwd(proj, cw, cbias, dbm, D_INNER, dproj, "ssd_conv_bwd_b")
    dproj, dwb_c = _conv_silu_bwd(proj, cw, cbias, dcm, D_INNER + BC_DIM, dproj, "ssd_conv_bwd_c")
    dwb_conv = jnp.concatenate([dwb_xs, dwb_b, dwb_c], axis=0)
    for rows, part in ((OFF_Q, dq), (OFF_K, dk), (OFF_V, dv), (OFF_DT, ddt.astype(BF16)), (OFF_GA, d_ga), (OFF_GS, d_gs)):
        dproj = lax.dynamic_update_slice(dproj, part, (rows, 0))
    g_in = pair_sums(_matmul(dproj, xn, nt=True, out_dtype=BF16, name="mm_d_w_in"))
    dxn = sending("w_in", g_in, _matmul, w_in_t, dproj, nt=False, out_dtype=F32, name="mm_d_xn", tn_a=True)
    dx, d_n1 = _norm_bwd(dxn, xt, n1, dh1, "norm1_bwd", tokens_out=True)

    g["norm1_w"] = d_n1
    g["b_gate"] = jnp.concatenate([d_ba, d_bs], axis=0)
    g["attn_sinks"] = d_sinks
    g["ssd_conv_w"] = dwb_conv[:, :SSD_CONV].T
    g["ssd_conv_b"] = dwb_conv[:, SSD_CONV]
    g["dt_bias"] = d_dtb
    g["a_log"] = d_alog
    g["d_skip"] = d_dsk
    g["ssd_norm_w"] = d_gnw
    g["norm2_w"] = d_n2
    d_fwb = d_fwb.reshape(2 * D_FF, 128)
    g["ffn_conv_w"] = d_fwb[:, :FFN_CONV].T
    g["ffn_conv_b"] = d_fwb[:, FFN_CONV]
    g["final_norm_w"] = d_nf
    return loss, dx, g


SMALL = ("norm1_w", "b_gate", "attn_sinks", "ssd_conv_w", "ssd_conv_b", "dt_bias", "a_log", "d_skip", "ssd_norm_w",
         "norm2_w", "ffn_conv_w", "ffn_conv_b", "final_norm_w")
WEIGHT_ORDER = ("norm1_w", "w_in", "b_gate", "attn_sinks", "w_attn_o", "ssd_conv_w", "ssd_conv_b", "dt_bias", "a_log",
                "d_skip", "ssd_norm_w", "w_ssd_o", "w_out", "norm2_w", "w_up", "ffn_conv_w", "ffn_conv_b", "w_down",
                "final_norm_w")


def kernel(x, norm1_w, w_in, b_gate, attn_sinks, w_attn_o, ssd_conv_w, ssd_conv_b, dt_bias, a_log, d_skip, ssd_norm_w, w_ssd_o, w_out, norm2_w, w_up, ffn_conv_w, ffn_conv_b, w_down, final_norm_w, loss_target, m_norm1_w, m_w_in, m_b_gate, m_attn_sinks, m_w_attn_o, m_ssd_conv_w, m_ssd_conv_b, m_dt_bias, m_a_log, m_d_skip, m_ssd_norm_w, m_w_ssd_o, m_w_out, m_norm2_w, m_w_up, m_ffn_conv_w, m_ffn_conv_b, m_w_down, m_final_norm_w, v_norm1_w, v_w_in, v_b_gate, v_attn_sinks, v_w_attn_o, v_ssd_conv_w, v_ssd_conv_b, v_dt_bias, v_a_log, v_d_skip, v_ssd_norm_w, v_w_ssd_o, v_w_out, v_norm2_w, v_w_up, v_ffn_conv_w, v_ffn_conv_b, v_w_down, v_final_norm_w):
    w = dict(norm1_w=norm1_w, w_in=w_in, b_gate=b_gate, attn_sinks=attn_sinks, w_attn_o=w_attn_o, ssd_conv_w=ssd_conv_w, ssd_conv_b=ssd_conv_b, dt_bias=dt_bias, a_log=a_log, d_skip=d_skip, ssd_norm_w=ssd_norm_w, w_ssd_o=w_ssd_o, w_out=w_out, norm2_w=norm2_w, w_up=w_up, ffn_conv_w=ffn_conv_w, ffn_conv_b=ffn_conv_b, w_down=w_down, final_norm_w=final_norm_w)
    m = dict(norm1_w=m_norm1_w, w_in=m_w_in, b_gate=m_b_gate, attn_sinks=m_attn_sinks, w_attn_o=m_w_attn_o, ssd_conv_w=m_ssd_conv_w, ssd_conv_b=m_ssd_conv_b, dt_bias=m_dt_bias, a_log=m_a_log, d_skip=m_d_skip, ssd_norm_w=m_ssd_norm_w, w_ssd_o=m_w_ssd_o, w_out=m_w_out, norm2_w=m_norm2_w, w_up=m_w_up, ffn_conv_w=m_ffn_conv_w, ffn_conv_b=m_ffn_conv_b, w_down=m_w_down, final_norm_w=m_final_norm_w)
    v = dict(norm1_w=v_norm1_w, w_in=v_w_in, b_gate=v_b_gate, attn_sinks=v_attn_sinks, w_attn_o=v_w_attn_o, ssd_conv_w=v_ssd_conv_w, ssd_conv_b=v_ssd_conv_b, dt_bias=v_dt_bias, a_log=v_a_log, d_skip=v_d_skip, ssd_norm_w=v_ssd_norm_w, w_ssd_o=v_w_ssd_o, w_out=v_w_out, norm2_w=v_norm2_w, w_up=v_w_up, ffn_conv_w=v_ffn_conv_w, ffn_conv_b=v_ffn_conv_b, w_down=v_w_down, final_norm_w=v_final_norm_w)
    me = 4 * lax.axis_index("x") + 2 * lax.axis_index("y") + lax.axis_index("c")

    shards = {"ssd_conv_w": ssd_conv_w[0], "ffn_conv_w": ffn_conv_w[0], "w_in": w_in[0].T.astype(BF16),
              "w_attn_o": w_attn_o[0].astype(BF16), "w_ssd_o": w_ssd_o[0].astype(BF16), "w_out": w_out[0].astype(BF16),
              "w_up": w_up[0].T.astype(BF16), "w_down": w_down[0].astype(BF16)}
    order = list(shards)
    g_send, g_recv, g_src, g_land = _gather_start(list(shards.values()), "gather_start", True)
    first = ("ssd_conv_w", "ffn_conv_w", "w_in")
    forwarded = {}

    def weight(name, after):
        if name not in forwarded:
            group = [k for k in order if (k in first) == (name in first)]
            idx = [order.index(k) for k in group]
            handles = _gather_forward([g_send[i] for i in idx], [g_recv[i] for i in idx], [g_src[i] for i in idx],
                                      [g_land[i] for i in idx], after, "gather_forward_for_" + name)
            forwarded.update(zip(group, zip(*handles)))
        i = order.index(name)
        src, land = _gather_wait_forwarded(g_send[i], g_recv[i], *forwarded[name], after, "gather_wait_" + name)
        land = _own_slot(src, land, me, True)
        if name == "ssd_conv_w":
            return jnp.transpose(land, (1, 0, 2)).reshape(SSD_CONV, XBC_DIM)
        if name == "ffn_conv_w":
            return jnp.transpose(land, (1, 0, 2)).reshape(FFN_CONV, 2 * D_FF)
        return land.reshape(-1, D_MODEL)

    res, pending = {}, {}

    def update(name, after):
        if name == "w_in":
            parts = _own_slot(*_chip_wait(*pending[name], after, "grad_wait_" + name), me // 2, False)
        else:
            parts = _own_slot(*_exchange_wait(*pending[name], after, False, "grad_wait_" + name), me, False)
        view, back = {
            "w_in": (lambda a: jnp.transpose(a, (2, 0, 1)), lambda r: jnp.transpose(r, (1, 2, 0))),
            "w_up": (lambda a: a[0].T, lambda r: r.T[None]),
        }.get(name, (lambda a: a[0], lambda r: r[None]))
        done = _adamw_sharded(parts, view(w[name]), view(m[name]), view(v[name]), "adamw_" + name)
        res[name] = [back(r) for r in done]
        return done[0]

    def pair_sums(grad):
        by_core, landed = _pair_exchange(grad.reshape(N_CHIPS, 2, -1, D_MODEL),
                                         lambda started: update("w_up", update("w_down", started)), "grad_pair_w_in")
        return _pair_add(by_core, landed, "grad_pair_add_w_in")

    small = {k: w[k][0] if k != "final_norm_w" else w[k] for k in SMALL}
    loss, dx, g = _local_step(x[0], loss_target[0], g_src[0], weight, small, pair_sums, pending)

    packed = _pack_row([loss] + [g[k] for k in SMALL])
    s_send, s_recv, s_src, s_land = _gather_start([packed], "small_grads_start", False)
    after = s_src[0]
    for name in ("w_out", "w_attn_o", "w_ssd_o", "w_in"):
        after = update(name, after)

    rows = _own_slot(*_exchange_wait(s_send[0], s_recv[0], s_src[0], s_land[0], after, True, "small_grads_wait"),
                     me, True)
    flat = lambda a: a.reshape(-1, a.shape[-1])
    loss_sum, updates = _small_update(
        rows, me.reshape(1), [g[k].size for k in SMALL],
        [flat(w[k]) for k in SMALL], [flat(m[k]) for k in SMALL], [flat(v[k]) for k in SMALL])
    for k, upd in zip(SMALL, updates):
        res[k] = [u.reshape(w[k].shape) for u in upd]

    grad_x = dx[None]
    outs = [loss_sum.reshape(()), grad_x]
    for i in range(4):
        outs.extend(res[k][i] for k in WEIGHT_ORDER)
    return tuple(outs)
```

```python
import jax
import jax.numpy as jnp
from jax import lax
from jax.experimental import pallas as pl
from jax.experimental.pallas import tpu as pltpu

F32 = jnp.float32
BF16 = jnp.bfloat16
HIGHEST = lax.Precision.HIGHEST

D_MODEL = 1024
N_Q_HEADS = 16
N_KV_HEADS = 4
HEAD_DIM = 64
WINDOW = 128
Q_PER_KV = N_Q_HEADS // N_KV_HEADS
Q_DIM = N_Q_HEADS * HEAD_DIM
KV_DIM = N_KV_HEADS * HEAD_DIM
D_INNER = 2048
SSD_HEAD_DIM = 64
N_SSD_HEADS = 32
N_SSD_GROUPS = 4
HEADS_PER_GROUP = N_SSD_HEADS // N_SSD_GROUPS
D_STATE = 128
BC_DIM = N_SSD_GROUPS * D_STATE
XBC_DIM = D_INNER + 2 * BC_DIM
SSD_CONV = 4
CHUNK = 128
D_FF = 2816
FFN_CONV = 3
EPS = 1e-5
NEG = -1e30
IN_DIM = 8736
N_DEV = 8

OFF_Q = 0
OFF_K = OFF_Q + Q_DIM
OFF_V = OFF_K + KV_DIM
OFF_Z = OFF_V + KV_DIM
OFF_X = OFF_Z + D_INNER
OFF_DT = OFF_X + XBC_DIM
OFF_GA = OFF_DT + N_SSD_HEADS
OFF_GS = OFF_GA + D_MODEL

ADAM_LR = 0.001
ADAM_B1 = 0.9
ADAM_B2 = 0.999
ADAM_EPS = 1e-08
ADAM_WD = 0.01
ADAM_STEP = 10

VMEM_LIMIT = 48 * 1024 * 1024
MESH = pl.DeviceIdType.MESH


def _cparams(*sem):
    return pltpu.CompilerParams(dimension_semantics=sem, vmem_limit_bytes=VMEM_LIMIT)


def _tile(n, prefs):
    for p in prefs:
        if n % p == 0:
            return p
    return n


def _sigmoid(x):
    return 1.0 / (1.0 + jnp.exp(-x))


def _softplus(x):
    return jnp.maximum(x, 0.0) + jnp.log(1.0 + jnp.exp(-jnp.abs(x)))


def _rowsum(x):
    return jnp.sum(x, axis=1, keepdims=True)


def _colsum(x):
    return jnp.sum(x, axis=0, keepdims=True)


def _dot(a, b):
    return jnp.dot(a, b, preferred_element_type=F32)


def _dot_nt(a, b):
    return lax.dot_general(a, b, (((1,), (1,)), ((), ())), preferred_element_type=F32)


def _dot_tn(a, b):
    return lax.dot_general(a, b, (((0,), (0,)), ((), ())), preferred_element_type=F32)


def _shift_right(x, j):
    if j == 0:
        return x
    r = pltpu.roll(x, j, 1)
    lane = lax.broadcasted_iota(jnp.int32, (x.shape[0], 128), 1)
    return jnp.concatenate([jnp.where(lane >= j, r[:, :128], 0.0), r[:, 128:]], axis=1)


def _shift_left(x, j):
    if j == 0:
        return x
    n = x.shape[1]
    r = pltpu.roll(x, n - j, 1)
    lane = lax.broadcasted_iota(jnp.int32, (x.shape[0], 128), 1)
    return jnp.concatenate([r[:, :n - 128], jnp.where(lane < 128 - j, r[:, n - 128:], 0.0)], axis=1)


def _causal_conv(xv, wv, bv):
    taps = wv.shape[1]
    shifted = [_shift_right(xv, taps - 1 - k) for k in range(taps - 1)]
    y = bv + wv[:, taps - 1:taps] * xv
    for k in range(taps - 1):
        y = y + wv[:, k:k + 1] * shifted[k]
    return y, shifted


def _causal_conv_bwd(dy, xv, shifted, wv):
    taps = wv.shape[1]
    lane = lax.broadcasted_iota(jnp.int32, (dy.shape[0], 128), 1)
    dwb = jnp.where(lane == taps, _rowsum(dy), 0.0)
    dwb = jnp.where(lane == taps - 1, _rowsum(dy * xv), dwb)
    dx = wv[:, taps - 1:taps] * dy
    for k in range(taps - 1):
        dx = dx + wv[:, k:k + 1] * _shift_left(dy, taps - 1 - k)
        dwb = jnp.where(lane == k, _rowsum(dy * shifted[k]), dwb)
    return dx, dwb


def _call(body, *, name, grid, in_specs, out_specs, out_shape, args, semantics, scratch_shapes=(), aliases=None,
          send=None):
    aliases = dict(aliases or {})
    if send is None:
        return pl.pallas_call(body, name=name, grid=grid, in_specs=in_specs, out_specs=out_specs, out_shape=out_shape,
                              scratch_shapes=list(scratch_shapes), input_output_aliases=aliases,
                              compiler_params=_cparams(*semantics))(*args)
    single = not isinstance(out_specs, (list, tuple))
    out_specs, out_shape = ([out_specs], [out_shape]) if single else (list(out_specs), list(out_shape))
    n_in, n_out = len(in_specs), len(out_specs)
    chips = send.shape[0] == N_DEV // 2
    n_copies = len(OTHER_CHIPS) if chips else N_DEV - 1

    def sending(*refs):
        ins, (src_ref, land_ref) = refs[:n_in], refs[n_in:n_in + 2]
        outs = refs[n_in + 2:n_in + 2 + n_out]
        send_sems, recv_sems = refs[n_in + 2 + n_out:n_in + 4 + n_out]
        scratch = refs[n_in + 6 + n_out:]
        step = 0
        for axis, size in enumerate(grid):
            step = step * size + pl.program_id(axis)

        @pl.when(step == 0)
        def _():
            for k in range(n_copies):
                if chips:
                    _chip_copy(src_ref, land_ref, send_sems, recv_sems, k, True).start()
                else:
                    _peer_copy(False, src_ref, land_ref, send_sems, recv_sems, k, True).start()

        body(*ins, *outs, *scratch)

    sem = pltpu.SemaphoreType.DMA((n_copies,))
    hbm = pltpu.HBM(send.shape, send.dtype)
    res = pl.pallas_call(
        sending, name=name, grid=grid,
        in_specs=list(in_specs) + [HBM, HBM],
        out_specs=out_specs + [SEM, SEM, HBM, HBM],
        out_shape=out_shape + [sem, sem, hbm, hbm],
        input_output_aliases={**aliases, n_in: n_out + 2, n_in + 1: n_out + 3},
        scratch_shapes=list(scratch_shapes),
        compiler_params=pltpu.CompilerParams(dimension_semantics=("arbitrary",) * len(grid), vmem_limit_bytes=VMEM_LIMIT,
                                             has_side_effects=EFFECT),
    )(*args, pltpu.with_memory_space_constraint(send, pltpu.HBM),
      pltpu.with_memory_space_constraint(lax.empty(send.shape, send.dtype), pltpu.HBM))
    return (res[0] if single else list(res[:n_out])), tuple(res[n_out:])


MATMUL_VMEM_BUDGET = 36 * 1024 * 1024
MATMUL_MAX_TK = 3072


MATMUL_MAX_TM = 768


def _largest_tile(n, align, cap):
    return max(d for d in range(align, min(n, cap) + 1, align) if n % d == 0)


def _matmul_tiles(m, n, k, a_bytes, b_bytes, out_bytes, has_add, m_align, k_align):
    tm = _largest_tile(m, m_align, MATMUL_MAX_TM)
    tk = _largest_tile(k, k_align, MATMUL_MAX_TK)
    for tn in sorted({d for d in range(128, n + 1, 128) if n % d == 0}, reverse=True):
        need = 2 * (tm * tk * a_bytes + tk * tn * b_bytes) + tm * tn * (2 * out_bytes + (4 if k > tk else 0) + (8 if has_add else 0))
        if tn <= 3072 and need <= MATMUL_VMEM_BUDGET:
            return tm, tn, tk
    return tm, 128, tk


def _matmul(a, b, *, nt, out_dtype, name, add=None, tn_a=False, send=None):
    if tn_a:
        k, m = a.shape
    else:
        m, k = a.shape
    n = b.shape[0] if nt else b.shape[1]
    tm, tn, tk = _matmul_tiles(m, n, k, a.dtype.itemsize, b.dtype.itemsize, jnp.dtype(out_dtype).itemsize, add is not None,
                               128 if tn_a else 16, 16 if tn_a and not nt else 128)
    nk = k // tk
    grid = (m // tm, n // tn, nk)

    def body(a_ref, b_ref, *rest):
        r_ref = None
        if add is not None:
            r_ref, rest = rest[0], rest[1:]
        o_ref = rest[0]
        av = a_ref[...].astype(BF16)
        bv = b_ref[...].astype(BF16)
        part = _dot_tn(av, bv) if tn_a else _dot_nt(av, bv) if nt else _dot(av, bv)

        def finish(r):
            if add is not None:
                r = r + r_ref[...]
            o_ref[...] = r.astype(out_dtype)

        if nk == 1:
            finish(part)
            return
        acc = rest[1]
        kk = pl.program_id(2)

        @pl.when(kk == 0)
        def _():
            acc[...] = part

        @pl.when((kk > 0) & (kk < nk - 1))
        def _():
            acc[...] += part

        @pl.when(kk == nk - 1)
        def _():
            finish(acc[...] + part)

    in_specs = [
        pl.BlockSpec((tk, tm), lambda i, j, kk: (kk, i)) if tn_a else pl.BlockSpec((tm, tk), lambda i, j, kk: (i, kk)),
        pl.BlockSpec((tn, tk), lambda i, j, kk: (j, kk)) if nt else pl.BlockSpec((tk, tn), lambda i, j, kk: (kk, j)),
    ]
    args = [a, b]
    if add is not None:
        in_specs.append(pl.BlockSpec((tm, tn), lambda i, j, kk: (i, j)))
        args.append(add)
    return _call(
        body, name=name, grid=grid, in_specs=in_specs, args=args,
        out_specs=pl.BlockSpec((tm, tn), lambda i, j, kk: (i, j)),
        out_shape=jax.ShapeDtypeStruct((m, n), out_dtype),
        scratch_shapes=[pltpu.VMEM((tm, tn), F32)] if nk > 1 else [],
        semantics=("parallel", "parallel", "arbitrary"), send=send)


def _norm_fwd(x, w_col, name):
    f, t = x.shape
    tt = _tile(t, (512, 256, 128))

    def body(x_ref, w_ref, o_ref):
        xv = x_ref[...]
        r = lax.rsqrt(jnp.mean(xv * xv, axis=0, keepdims=True) + EPS)
        o_ref[...] = (xv * r * w_ref[...]).astype(BF16)

    return pl.pallas_call(
        body,
        name=name,
        grid=(t // tt,),
        in_specs=[pl.BlockSpec((f, tt), lambda i: (0, i)), pl.BlockSpec((f, 1), lambda i: (0, 0))],
        out_specs=pl.BlockSpec((f, tt), lambda i: (0, i)),
        out_shape=jax.ShapeDtypeStruct((f, t), BF16),
        compiler_params=_cparams("parallel"),
    )(x, w_col)


def _norm_fwd_tokens(x, w_col, after, name):
    t, f = x.shape
    tt = _tile(t, (512, 256, 128))

    def body(x_ref, w_ref, after_ref, xt_ref, o_ref):
        xv = x_ref[...].T
        xt_ref[...] = xv
        r = lax.rsqrt(jnp.mean(xv * xv, axis=0, keepdims=True) + EPS)
        o_ref[...] = (xv * r * w_ref[...]).astype(BF16)

    blk = pl.BlockSpec((f, tt), lambda i: (0, i))
    return pl.pallas_call(
        body,
        name=name,
        grid=(t // tt,),
        in_specs=[pl.BlockSpec((tt, f), lambda i: (i, 0)), pl.BlockSpec((f, 1), lambda i: (0, 0)), ANY],
        out_specs=[blk, blk],
        out_shape=[jax.ShapeDtypeStruct((f, t), F32), jax.ShapeDtypeStruct((f, t), BF16)],
        compiler_params=_cparams("parallel"),
    )(x, w_col, after)


def _norm_bwd(dy, x, w_col, res, name, tokens_out=False):
    f, t = x.shape
    tt = _tile(t, (512, 256, 128))

    def body(dy_ref, x_ref, w_ref, res_ref, dx_ref, dw_ref):
        @pl.when(pl.program_id(0) == 0)
        def _():
            dw_ref[...] = jnp.zeros_like(dw_ref)

        xv = x_ref[...]
        r = lax.rsqrt(jnp.mean(xv * xv, axis=0, keepdims=True) + EPS)
        xhat = xv * r
        dyv = dy_ref[...]
        dw_ref[...] += _rowsum(dyv * xhat)
        dxhat = dyv * w_ref[...]
        dx = res_ref[...] + r * (dxhat - xhat * jnp.mean(dxhat * xhat, axis=0, keepdims=True))
        dx_ref[...] = dx.T if tokens_out else dx

    blk = pl.BlockSpec((f, tt), lambda i: (0, i))
    col = pl.BlockSpec((f, 1), lambda i: (0, 0))
    return pl.pallas_call(
        body,
        name=name,
        grid=(t // tt,),
        in_specs=[blk, blk, col, blk],
        out_specs=[pl.BlockSpec((tt, f), lambda i: (i, 0)) if tokens_out else blk, col],
        out_shape=[jax.ShapeDtypeStruct((t, f) if tokens_out else (f, t), F32), jax.ShapeDtypeStruct((f, 1), F32)],
        compiler_params=_cparams("arbitrary"),
    )(dy, x, w_col, res)


def _final_norm_loss(h, tgt, w_col):
    f, t = h.shape
    tt = _tile(t, (512, 256, 128))

    def body(h_ref, t_ref, w_ref, dh_ref, loss_ref, dw_ref):
        @pl.when(pl.program_id(0) == 0)
        def _():
            dw_ref[...] = jnp.zeros_like(dw_ref)
            loss_ref[...] = jnp.zeros_like(loss_ref)

        xv = h_ref[...]
        r = lax.rsqrt(jnp.mean(xv * xv, axis=0, keepdims=True) + EPS)
        xhat = xv * r
        wv = w_ref[...]
        err = xhat * wv - t_ref[...].T
        loss_ref[...] += 0.5 * _rowsum(jnp.mean(err * err, axis=0, keepdims=True))
        dyv = err * (1.0 / f)
        dw_ref[...] += _rowsum(dyv * xhat)
        dxhat = dyv * wv
        dh_ref[...] = r * (dxhat - xhat * jnp.mean(dxhat * xhat, axis=0, keepdims=True))

    blk = pl.BlockSpec((f, tt), lambda i: (0, i))
    col = pl.BlockSpec((f, 1), lambda i: (0, 0))
    one = pl.BlockSpec((1, 1), lambda i: (0, 0))
    return pl.pallas_call(
        body,
        name="final_norm_loss",
        grid=(t // tt,),
        in_specs=[blk, pl.BlockSpec((tt, f), lambda i: (i, 0)), col],
        out_specs=[blk, one, col],
        out_shape=[jax.ShapeDtypeStruct((f, t), F32), jax.ShapeDtypeStruct((1, 1), F32), jax.ShapeDtypeStruct((f, 1), F32)],
        compiler_params=_cparams("arbitrary"),
    )(h, tgt, w_col)


def _attn_mask(n):
    shape = (2 * WINDOW, Q_PER_KV * WINDOW)
    si = lax.broadcasted_iota(jnp.int32, shape, 0)
    qi = lax.broadcasted_iota(jnp.int32, shape, 1) & (WINDOW - 1)
    dist = WINDOW + qi - si
    return (dist >= 0) & (dist < WINDOW) & ((si >= WINDOW) | (n > 0))


def _lane_cat(ref, row0, rows):
    return jnp.concatenate([ref[row0 + i * rows:row0 + (i + 1) * rows, :] for i in range(Q_PER_KV)], axis=1)


def _attn_fwd(proj, sinks):
    t = proj.shape[1]
    nb = t // WINDOW
    scale = HEAD_DIM ** -0.5

    def body(s_ref, q_ref, kc_ref, kp_ref, vc_ref, vp_ref, o_ref, lse_ref):
        n = pl.program_id(0)
        valid = _attn_mask(n)
        for g in range(N_KV_HEADS):
            rows = slice(g * HEAD_DIM, (g + 1) * HEAD_DIM)
            kt = jnp.concatenate([kp_ref[rows, :], kc_ref[rows, :]], axis=1).astype(BF16)
            vt = jnp.concatenate([vp_ref[rows, :], vc_ref[rows, :]], axis=1).astype(BF16)
            qcat = (_lane_cat(q_ref, g * Q_PER_KV * HEAD_DIM, HEAD_DIM) * scale).astype(BF16)
            s = jnp.where(valid, _dot_tn(kt, qcat), NEG)
            sink = jnp.concatenate(
                [jnp.full((1, WINDOW), s_ref[g * Q_PER_KV + i], F32) for i in range(Q_PER_KV)], axis=1)
            m = jnp.maximum(jnp.max(s, axis=0, keepdims=True), sink)
            p = jnp.exp(s - m)
            denom = _colsum(p) + jnp.exp(sink - m)
            probs = (p / denom).astype(BF16)
            out = _dot(vt, probs)
            lse = m + jnp.log(denom)
            for i in range(Q_PER_KV):
                h = g * Q_PER_KV + i
                o_ref[h * HEAD_DIM:(h + 1) * HEAD_DIM, :] = out[:, i * WINDOW:(i + 1) * WINDOW]
                lse_ref[h:h + 1, :] = lse[:, i * WINDOW:(i + 1) * WINDOW]

    kb = OFF_K // KV_DIM
    vb = OFF_V // KV_DIM
    prev = lambda n: jnp.maximum(n - 1, 0)
    return pl.pallas_call(
        body,
        name="attn_fwd",
        grid=(nb,),
        in_specs=[
            pl.BlockSpec(memory_space=pltpu.SMEM),
            pl.BlockSpec((Q_DIM, WINDOW), lambda n: (0, n)),
            pl.BlockSpec((KV_DIM, WINDOW), lambda n: (kb, n)),
            pl.BlockSpec((KV_DIM, WINDOW), lambda n: (kb, prev(n))),
            pl.BlockSpec((KV_DIM, WINDOW), lambda n: (vb, n)),
            pl.BlockSpec((KV_DIM, WINDOW), lambda n: (vb, prev(n))),
        ],
        out_specs=[pl.BlockSpec((Q_DIM, WINDOW), lambda n: (0, n)), pl.BlockSpec((N_Q_HEADS, WINDOW), lambda n: (0, n))],
        out_shape=[jax.ShapeDtypeStruct((Q_DIM, t), F32), jax.ShapeDtypeStruct((N_Q_HEADS, t), F32)],
        compiler_params=_cparams("parallel"),
    )(sinks, proj, proj, proj, proj, proj)


def _attn_bwd(proj, sinks, out, lse, dout, dproj, send=None):
    t = proj.shape[1]
    nb = t // WINDOW
    scale = HEAD_DIM ** -0.5

    def body(s_ref, q_ref, kc_ref, kp_ref, vc_ref, vp_ref, o_ref, lse_ref, do_ref, dproj_ref,
             dqkv_ref, ds_ref, dk_carry, dv_carry):
        dq_ref = dqkv_ref.at[pl.ds(OFF_Q, Q_DIM)]
        dk_ref = dqkv_ref.at[pl.ds(OFF_K, KV_DIM)]
        dv_ref = dqkv_ref.at[pl.ds(OFF_V, KV_DIM)]
        step = pl.program_id(0)
        n = nb - 1 - step

        @pl.when(step == 0)
        def _():
            dk_carry[...] = jnp.zeros_like(dk_carry)
            dv_carry[...] = jnp.zeros_like(dv_carry)
            ds_ref[...] = jnp.zeros_like(ds_ref)

        valid = _attn_mask(n)
        for g in range(N_KV_HEADS):
            rows = slice(g * HEAD_DIM, (g + 1) * HEAD_DIM)
            q0 = g * Q_PER_KV * HEAD_DIM
            kt = jnp.concatenate([kp_ref[rows, :], kc_ref[rows, :]], axis=1).astype(BF16)
            vt = jnp.concatenate([vp_ref[rows, :], vc_ref[rows, :]], axis=1).astype(BF16)
            qf = _lane_cat(q_ref, q0, HEAD_DIM)
            qcat = qf.astype(BF16)
            ocat = _lane_cat(o_ref, q0, HEAD_DIM)
            docat = _lane_cat(do_ref, q0, HEAD_DIM)
            dob = docat.astype(BF16)
            lse_cat = jnp.concatenate(
                [lse_ref[g * Q_PER_KV + i:g * Q_PER_KV + i + 1, :] for i in range(Q_PER_KV)], axis=1)
            sink = jnp.concatenate(
                [jnp.full((1, WINDOW), s_ref[g * Q_PER_KV + i], F32) for i in range(Q_PER_KV)], axis=1)
            s = jnp.where(valid, _dot_tn(kt, (qf * scale).astype(BF16)), NEG)
            p = jnp.exp(s - lse_cat)
            dp = _dot_tn(vt, dob)
            delta = _colsum(docat * ocat)
            dsc = (p * (dp - delta)).astype(BF16)
            dsink_row = -jnp.exp(sink - lse_cat) * delta
            dq = _dot(kt, dsc) * scale
            dk = _dot_nt(qcat, dsc) * scale
            dv = _dot_nt(dob, p.astype(BF16))
            for i in range(Q_PER_KV):
                h = g * Q_PER_KV + i
                dq_ref[h * HEAD_DIM:(h + 1) * HEAD_DIM, :] = dq[:, i * WINDOW:(i + 1) * WINDOW].astype(BF16)
                ds_ref[h:h + 1, :] += _rowsum(dsink_row[:, i * WINDOW:(i + 1) * WINDOW])
            dk_ref[rows, :] = (dk[:, WINDOW:] + dk_carry[rows, :]).astype(BF16)
            dv_ref[rows, :] = (dv[:, WINDOW:] + dv_carry[rows, :]).astype(BF16)
            dk_carry[rows, :] = dk[:, :WINDOW]
            dv_carry[rows, :] = dv[:, :WINDOW]

    kb = OFF_K // KV_DIM
    vb = OFF_V // KV_DIM
    cur = lambda i: nb - 1 - i
    prev = lambda i: jnp.maximum(nb - 2 - i, 0)
    qspec = pl.BlockSpec((Q_DIM, WINDOW), lambda i: (0, cur(i)))
    return _call(
        body,
        name="attn_bwd",
        grid=(nb,),
        in_specs=[
            pl.BlockSpec(memory_space=pltpu.SMEM),
            qspec,
            pl.BlockSpec((KV_DIM, WINDOW), lambda i: (kb, cur(i))),
            pl.BlockSpec((KV_DIM, WINDOW), lambda i: (kb, prev(i))),
            pl.BlockSpec((KV_DIM, WINDOW), lambda i: (vb, cur(i))),
            pl.BlockSpec((KV_DIM, WINDOW), lambda i: (vb, prev(i))),
            qspec,
            pl.BlockSpec((N_Q_HEADS, WINDOW), lambda i: (0, cur(i))),
            qspec,
            pl.BlockSpec(memory_space=pl.ANY),
        ],
        out_specs=[pl.BlockSpec((OFF_Z, WINDOW), lambda i: (0, cur(i))), pl.BlockSpec((N_Q_HEADS, 1), lambda i: (0, 0))],
        out_shape=[jax.ShapeDtypeStruct(dproj.shape, BF16), jax.ShapeDtypeStruct((N_Q_HEADS, 1), F32)],
        scratch_shapes=[pltpu.VMEM((KV_DIM, WINDOW), F32), pltpu.VMEM((KV_DIM, WINDOW), F32)],
        aliases={9: 0},
        semantics=("arbitrary",), args=(sinks, proj, proj, proj, proj, proj, out, lse, dout, dproj), send=send)


CONV_ROWS = 256


def _conv_silu_fwd(proj, w_col, b_col):
    t = proj.shape[1]
    r0 = OFF_X // CONV_ROWS

    def body(x_ref, w_ref, b_ref, o_ref):
        y, _ = _causal_conv(x_ref[...], w_ref[...], b_ref[...])
        o_ref[...] = y * _sigmoid(y)

    return pl.pallas_call(
        body,
        name="ssd_conv_fwd",
        grid=(XBC_DIM // CONV_ROWS,),
        in_specs=[
            pl.BlockSpec((CONV_ROWS, t), lambda i: (r0 + i, 0)),
            pl.BlockSpec((CONV_ROWS, SSD_CONV), lambda i: (i, 0)),
            pl.BlockSpec((CONV_ROWS, 1), lambda i: (i, 0)),
        ],
        out_specs=pl.BlockSpec((CONV_ROWS, t), lambda i: (i, 0)),
        out_shape=jax.ShapeDtypeStruct((XBC_DIM, t), F32),
        compiler_params=_cparams("parallel"),
    )(proj, w_col, b_col)


def _conv_silu_bwd(proj, w_col, b_col, dout, row0, dproj, name):
    t = proj.shape[1]
    nrows = dout.shape[0]
    p0 = (OFF_X + row0) // CONV_ROWS
    c0 = row0 // CONV_ROWS

    def body(x_ref, w_ref, b_ref, do_ref, dproj_ref, dx_ref, dwb_ref):
        xv = x_ref[...]
        wv = w_ref[...]
        y, shifted = _causal_conv(xv, wv, b_ref[...])
        sg = _sigmoid(y)
        dy = do_ref[...] * (sg * (1.0 + y * (1.0 - sg)))
        dx, dwb_ref[...] = _causal_conv_bwd(dy, xv, shifted, wv)
        dx_ref[...] = dx.astype(BF16)

    return pl.pallas_call(
        body,
        name=name,
        grid=(nrows // CONV_ROWS,),
        in_specs=[
            pl.BlockSpec((CONV_ROWS, t), lambda i: (p0 + i, 0)),
            pl.BlockSpec((CONV_ROWS, SSD_CONV), lambda i: (c0 + i, 0)),
            pl.BlockSpec((CONV_ROWS, 1), lambda i: (c0 + i, 0)),
            pl.BlockSpec((CONV_ROWS, t), lambda i: (i, 0)),
            pl.BlockSpec(memory_space=pl.ANY),
        ],
        out_specs=[pl.BlockSpec((CONV_ROWS, t), lambda i: (p0 + i, 0)), pl.BlockSpec((CONV_ROWS, 128), lambda i: (i, 0))],
        out_shape=[jax.ShapeDtypeStruct(dproj.shape, BF16), jax.ShapeDtypeStruct((nrows, 128), F32)],
        input_output_aliases={4: 0},
        compiler_params=_cparams("parallel"),
    )(proj, w_col, b_col, dout, dproj)


def _ssd_specs(order):
    xb = D_INNER // BC_DIM
    dtb = OFF_DT // N_SSD_HEADS
    col = pl.BlockSpec((N_SSD_HEADS, 1), lambda c: (0, 0))
    return [
        pl.BlockSpec((D_INNER, CHUNK), lambda c: (0, order(c))),
        pl.BlockSpec((BC_DIM, CHUNK), lambda c: (xb, order(c))),
        pl.BlockSpec((BC_DIM, CHUNK), lambda c: (xb + 1, order(c))),
        pl.BlockSpec((N_SSD_HEADS, CHUNK), lambda c: (dtb, order(c))),
        col, col, col,
    ]


def _ssd_common(dt_ref, dtb_ref, alog_ref):
    z = dt_ref[...] + dtb_ref[...]
    dt = _softplus(z)
    a_neg = -jnp.exp(alog_ref[...])
    d_a = dt * a_neg
    row = lax.broadcasted_iota(jnp.int32, (CHUNK, CHUNK), 0)
    colm = lax.broadcasted_iota(jnp.int32, (CHUNK, CHUNK), 1)
    upper = (row <= colm).astype(F32)
    a_cs = jnp.dot(d_a, upper, precision=HIGHEST, preferred_element_type=F32)
    a_last = _rowsum(d_a)
    return z, dt, a_neg, a_cs, a_last, row >= colm, row == colm


def _decay(a_row, causal):
    a_s = jnp.broadcast_to(a_row, (CHUNK, CHUNK))
    seg = a_s.T - a_s
    return jnp.where(causal, jnp.exp(jnp.where(causal, seg, 0.0)), 0.0)


def _ssd_fwd(xbc, proj, dtb_col, alog_col, dsk_col):
    t = xbc.shape[1]
    nc = t // CHUNK

    def body(xs_ref, b_ref, c_ref, dt_ref, dtb_ref, alog_ref, dsk_ref, y_ref, hst_ref, h_scr):
        @pl.when(pl.program_id(0) == 0)
        def _():
            h_scr[...] = jnp.zeros_like(h_scr)

        _, dt, _, a_cs, a_last, causal, _ = _ssd_common(dt_ref, dtb_ref, alog_ref)
        hst_ref[0] = h_scr[...]
        dsk = dsk_ref[...]
        for g in range(N_SSD_GROUPS):
            grows = slice(g * D_STATE, (g + 1) * D_STATE)
            bb = b_ref[grows, :].astype(BF16)
            cb_ = c_ref[grows, :].astype(BF16)
            cb = _dot_tn(cb_, bb)
            for j in range(g * HEADS_PER_GROUP, (g + 1) * HEADS_PER_GROUP):
                rows = slice(j * SSD_HEAD_DIM, (j + 1) * SSD_HEAD_DIM)
                a = a_cs[j:j + 1, :]
                m = (cb * _decay(a, causal)).astype(BF16)
                xs = xs_ref[rows, :]
                xc = xs * dt[j:j + 1, :]
                hj = h_scr[rows, :]
                y = _dot_nt(xc.astype(BF16), m) + _dot(hj.astype(BF16), cb_) * jnp.exp(a) + dsk[j:j + 1, :] * xs
                y_ref[rows, :] = y
                al = a_last[j:j + 1, :]
                w = jnp.exp(al - a)
                h_scr[rows, :] = jnp.exp(al) * hj + _dot_nt((xc * w).astype(BF16), bb)

    return pl.pallas_call(
        body,
        name="ssd_fwd",
        grid=(nc,),
        in_specs=_ssd_specs(lambda c: c),
        out_specs=[
            pl.BlockSpec((D_INNER, CHUNK), lambda c: (0, c)),
            pl.BlockSpec((1, D_INNER, D_STATE), lambda c: (c, 0, 0)),
        ],
        out_shape=[
            jax.ShapeDtypeStruct((D_INNER, t), F32),
            jax.ShapeDtypeStruct((nc, D_INNER, D_STATE), F32),
        ],
        scratch_shapes=[pltpu.VMEM((D_INNER, D_STATE), F32)],
        compiler_params=_cparams("arbitrary"),
    )(xbc, xbc, xbc, proj, dtb_col, alog_col, dsk_col)


def _ssd_bwd(xbc, proj, dtb_col, alog_col, dsk_col, hst, dy):
    t = xbc.shape[1]
    nc = t // CHUNK
    rev = lambda c: nc - 1 - c

    def body(xs_ref, b_ref, c_ref, dt_ref, dtb_ref, alog_ref, dsk_ref, hst_ref, dy_ref,
             dxs_ref, db_ref, dc_ref, ddt_ref, dalog_ref, ddsk_ref, ddtb_ref, dh_scr, da_scr, ddt_scr, dd_scr):
        @pl.when(pl.program_id(0) == 0)
        def _():
            dh_scr[...] = jnp.zeros_like(dh_scr)
            dalog_ref[...] = jnp.zeros_like(dalog_ref)
            ddsk_ref[...] = jnp.zeros_like(ddsk_ref)
            ddtb_ref[...] = jnp.zeros_like(ddtb_ref)

        z, dt, a_neg, a_cs, a_last, causal, eye = _ssd_common(dt_ref, dtb_ref, alog_ref)
        dsk = dsk_ref[...]
        last_lane = lax.broadcasted_iota(jnp.int32, (1, CHUNK), 1) == CHUNK - 1
        for g in range(N_SSD_GROUPS):
            grows = slice(g * D_STATE, (g + 1) * D_STATE)
            bb = b_ref[grows, :].astype(BF16)
            cb_ = c_ref[grows, :].astype(BF16)
            cb = _dot_tn(cb_, bb)
            dcb = jnp.zeros((CHUNK, CHUNK), F32)
            dc_acc = jnp.zeros((D_STATE, CHUNK), F32)
            db_acc = jnp.zeros((D_STATE, CHUNK), F32)
            for j in range(g * HEADS_PER_GROUP, (g + 1) * HEADS_PER_GROUP):
                rows = slice(j * SSD_HEAD_DIM, (j + 1) * SSD_HEAD_DIM)
                a = a_cs[j:j + 1, :]
                al = a_last[j:j + 1, :]
                lam = _decay(a, causal)
                mf = cb * lam
                xs = xs_ref[rows, :]
                dtj = dt[j:j + 1, :]
                xc = xs * dtj
                w = jnp.exp(al - a)
                e = jnp.exp(a)
                gam = jnp.exp(al)
                hj = hst_ref[0, rows, :]
                hjb = hj.astype(BF16)
                dyv = dy_ref[rows, :]
                dyb = dyv.astype(BF16)
                dd_scr[j:j + 1, :] = _colsum(dyv * xs)
                gb = (dyv * e).astype(BF16)
                dh_in = _dot_nt(gb, cb_)
                dc_acc = dc_acc + _dot_tn(hjb, gb)
                yoff = _dot(hjb, cb_) * e
                da = _colsum(dyv * yoff)
                dm = _dot_tn(dyb, xc.astype(BF16))
                dxc = _dot(dyb, mf.astype(BF16))
                dcb = dcb + dm * lam
                nmat = dm * mf
                rs = jnp.broadcast_to(_rowsum(nmat), (CHUNK, CHUNK))
                da = da + _colsum(jnp.where(eye, rs, 0.0)) - _colsum(nmat)
                ds = dh_scr[rows, :]
                dsb = ds.astype(BF16)
                t1 = _dot(dsb, bb)
                xcw = xc * w
                dxc = dxc + w * t1
                dww = _colsum(xcw * t1)
                da_l = _rowsum(dww) + _rowsum(_colsum(ds * hj)) * gam
                da = da - dww + jnp.where(last_lane, da_l, 0.0)
                db_acc = db_acc + _dot_tn(dsb, xcw.astype(BF16))
                dh_scr[rows, :] = gam * ds + dh_in
                dxs_ref[rows, :] = dsk[j:j + 1, :] * dyv + dxc * dtj
                da_scr[j:j + 1, :] = da
                ddt_scr[j:j + 1, :] = _colsum(dxc * xs)
            dcbb = dcb.astype(BF16)
            dc_ref[grows, :] = dc_acc + _dot_nt(bb, dcbb)
            db_ref[grows, :] = db_acc + _dot(cb_, dcbb)
        dda = jnp.dot(da_scr[...], causal.astype(F32), precision=HIGHEST, preferred_element_type=F32)
        ddt = ddt_scr[...] + dda * a_neg
        ddt_raw = ddt * _sigmoid(z)
        ddt_ref[...] = ddt_raw
        ddtb_ref[...] += _rowsum(ddt_raw)
        dalog_ref[...] += _rowsum(dda * dt) * a_neg
        ddsk_ref[...] += _rowsum(dd_scr[...])

    col = pl.BlockSpec((N_SSD_HEADS, 1), lambda c: (0, 0))
    bc = pl.BlockSpec((BC_DIM, CHUNK), lambda c: (0, rev(c)))
    xs_spec = pl.BlockSpec((D_INNER, CHUNK), lambda c: (0, rev(c)))
    small = pltpu.VMEM((N_SSD_HEADS, CHUNK), F32)
    return pl.pallas_call(
        body,
        name="ssd_bwd",
        grid=(nc,),
        in_specs=_ssd_specs(rev) + [pl.BlockSpec((1, D_INNER, D_STATE), lambda c: (rev(c), 0, 0)), xs_spec],
        out_specs=[xs_spec, bc, bc, pl.BlockSpec((N_SSD_HEADS, CHUNK), lambda c: (0, rev(c))), col, col, col],
        out_shape=[
            jax.ShapeDtypeStruct((D_INNER, t), F32),
            jax.ShapeDtypeStruct((BC_DIM, t), F32),
            jax.ShapeDtypeStruct((BC_DIM, t), F32),
            jax.ShapeDtypeStruct((N_SSD_HEADS, t), F32),
            jax.ShapeDtypeStruct((N_SSD_HEADS, 1), F32),
            jax.ShapeDtypeStruct((N_SSD_HEADS, 1), F32),
            jax.ShapeDtypeStruct((N_SSD_HEADS, 1), F32),
        ],
        scratch_shapes=[pltpu.VMEM((D_INNER, D_STATE), F32), small, small, small],
        compiler_params=_cparams("arbitrary"),
    )(xbc, xbc, xbc, proj, dtb_col, alog_col, dsk_col, hst, dy)


GN_ROWS = D_INNER // N_SSD_GROUPS


def _gnorm_fwd(y, proj, w_col):
    t = y.shape[1]
    tt = _tile(t, (512, 256, 128))
    z0 = OFF_Z // GN_ROWS

    def body(y_ref, z_ref, w_ref, o_ref):
        zv = z_ref[...]
        u = y_ref[...] * (zv * _sigmoid(zv))
        r = lax.rsqrt(jnp.mean(u * u, axis=0, keepdims=True) + EPS)
        o_ref[...] = (u * r * w_ref[...]).astype(BF16)

    blk = pl.BlockSpec((GN_ROWS, tt), lambda g, i: (g, i))
    return pl.pallas_call(
        body,
        name="gnorm_fwd",
        grid=(N_SSD_GROUPS, t // tt),
        in_specs=[blk, pl.BlockSpec((GN_ROWS, tt), lambda g, i: (z0 + g, i)), pl.BlockSpec((GN_ROWS, 1), lambda g, i: (g, 0))],
        out_specs=blk,
        out_shape=jax.ShapeDtypeStruct((D_INNER, t), BF16),
        compiler_params=_cparams("parallel", "parallel"),
    )(y, proj, w_col)


def _gnorm_bwd(dout, y, proj, w_col, send=None):
    t = y.shape[1]
    tt = _tile(t, (512, 256, 128))
    z0 = OFF_Z // GN_ROWS

    def body(do_ref, y_ref, z_ref, w_ref, dy_ref, dz_ref, dw_ref):
        @pl.when(pl.program_id(1) == 0)
        def _():
            dw_ref[...] = jnp.zeros_like(dw_ref)

        zv = z_ref[...]
        yv = y_ref[...]
        sg = _sigmoid(zv)
        sz = zv * sg
        u = yv * sz
        r = lax.rsqrt(jnp.mean(u * u, axis=0, keepdims=True) + EPS)
        xhat = u * r
        dov = do_ref[...]
        dw_ref[...] += _rowsum(dov * xhat)
        dxhat = dov * w_ref[...]
        du = r * (dxhat - xhat * jnp.mean(dxhat * xhat, axis=0, keepdims=True))
        dy_ref[...] = du * sz
        dz_ref[...] = (du * yv * (sg * (1.0 + zv * (1.0 - sg)))).astype(BF16)

    blk = pl.BlockSpec((GN_ROWS, tt), lambda g, i: (g, i))
    col = pl.BlockSpec((GN_ROWS, 1), lambda g, i: (g, 0))
    return _call(
        body,
        name="gnorm_bwd",
        grid=(N_SSD_GROUPS, t // tt),
        in_specs=[blk, blk, pl.BlockSpec((GN_ROWS, tt), lambda g, i: (z0 + g, i)), col],
        out_specs=[blk, pl.BlockSpec((GN_ROWS, tt), lambda g, i: (z0 + g, i)), col],
        out_shape=[jax.ShapeDtypeStruct((D_INNER, t), F32), jax.ShapeDtypeStruct((IN_DIM, t), BF16),
                   jax.ShapeDtypeStruct((D_INNER, 1), F32)],
        semantics=("parallel", "arbitrary"), args=(dout, y, proj, w_col), send=send)


GATE_ROWS = 128


def _gate_specs(t):
    nr = D_MODEL // GATE_ROWS
    blk = pl.BlockSpec((GATE_ROWS, t), lambda r: (r, 0))
    rows_from = lambda first: pl.BlockSpec(
        (pl.Element(GATE_ROWS), pl.Element(t)), lambda r: (pl.multiple_of(first + GATE_ROWS * r, N_SSD_HEADS), 0))
    return blk, [
        rows_from(OFF_GA),
        rows_from(OFF_GS),
        pl.BlockSpec((GATE_ROWS, 1), lambda r: (r, 0)),
        pl.BlockSpec((GATE_ROWS, 1), lambda r: (nr + r, 0)),
        blk, blk,
    ]


def _gate_fwd(proj, b_col, attn, ssd):
    t = proj.shape[1]
    blk, specs = _gate_specs(t)

    def body(ga_ref, gs_ref, ba_ref, bs_ref, a_ref, s_ref, o_ref):
        o_ref[...] = (_sigmoid(ga_ref[...] + ba_ref[...]) * a_ref[...]
                      + _sigmoid(gs_ref[...] + bs_ref[...]) * s_ref[...]).astype(BF16)

    return pl.pallas_call(
        body,
        name="gate_fwd",
        grid=(D_MODEL // GATE_ROWS,),
        in_specs=specs,
        out_specs=blk,
        out_shape=jax.ShapeDtypeStruct((D_MODEL, t), BF16),
        compiler_params=_cparams("parallel"),
    )(proj, proj, b_col, b_col, attn, ssd)


def _gate_bwd(proj, b_col, attn, ssd, dmix, send=None):
    t = proj.shape[1]
    blk, specs = _gate_specs(t)

    def body(ga_ref, gs_ref, ba_ref, bs_ref, a_ref, s_ref, dm_ref, da_ref, dso_ref, dga_ref, dgs_ref, dba_ref, dbs_ref):
        dm = dm_ref[...]
        sa = _sigmoid(ga_ref[...] + ba_ref[...])
        ss = _sigmoid(gs_ref[...] + bs_ref[...])
        da_ref[...] = (dm * sa).astype(BF16)
        dso_ref[...] = (dm * ss).astype(BF16)
        dga = dm * a_ref[...] * sa * (1.0 - sa)
        dgs = dm * s_ref[...] * ss * (1.0 - ss)
        dga_ref[...] = dga.astype(BF16)
        dgs_ref[...] = dgs.astype(BF16)
        dba_ref[...] = _rowsum(dga)
        dbs_ref[...] = _rowsum(dgs)

    col = pl.BlockSpec((GATE_ROWS, 1), lambda r: (r, 0))
    act = jax.ShapeDtypeStruct((D_MODEL, t), BF16)
    bias = jax.ShapeDtypeStruct((D_MODEL, 1), F32)
    return _call(
        body,
        name="gate_bwd",
        grid=(D_MODEL // GATE_ROWS,),
        in_specs=specs + [blk],
        out_specs=[blk, blk, blk, blk, col, col],
        out_shape=[act, act, act, act, bias, bias],
        semantics=("parallel",), args=(proj, proj, b_col, b_col, attn, ssd, dmix), send=send)


FFN_ROWS = 256


def _ffn_fwd(u0, w_col, b_col):
    t = u0.shape[2]

    def body(u_ref, w_ref, b_ref, o_ref):
        val, _ = _causal_conv(u_ref[0], w_ref[0], b_ref[0])
        gt, _ = _causal_conv(u_ref[1], w_ref[1], b_ref[1])
        o_ref[...] = (gt * _sigmoid(gt) * val).astype(BF16)

    return pl.pallas_call(
        body,
        name="ffn_fwd",
        grid=(D_FF // FFN_ROWS,),
        in_specs=[
            pl.BlockSpec((2, FFN_ROWS, t), lambda i: (0, i, 0)),
            pl.BlockSpec((2, FFN_ROWS, FFN_CONV), lambda i: (0, i, 0)),
            pl.BlockSpec((2, FFN_ROWS, 1), lambda i: (0, i, 0)),
        ],
        out_specs=pl.BlockSpec((FFN_ROWS, t), lambda i: (i, 0)),
        out_shape=jax.ShapeDtypeStruct((D_FF, t), BF16),
        compiler_params=_cparams("parallel"),
    )(u0, w_col, b_col)


def _ffn_bwd(u0, w_col, b_col, dg, send=None):
    t = u0.shape[2]

    def body(u_ref, w_ref, b_ref, dg_ref, du_ref, dwb_ref):
        xval, wval = u_ref[0], w_ref[0]
        xgt, wgt = u_ref[1], w_ref[1]
        val, sh_val = _causal_conv(xval, wval, b_ref[0])
        gt, sh_gt = _causal_conv(xgt, wgt, b_ref[1])
        sg = _sigmoid(gt)
        dgv = dg_ref[...]
        dval = dgv * (gt * sg)
        dgt = dgv * val * (sg * (1.0 + gt * (1.0 - sg)))
        dx, dwb_ref[0] = _causal_conv_bwd(dval, xval, sh_val, wval)
        du_ref[0] = dx.astype(BF16)
        dx, dwb_ref[1] = _causal_conv_bwd(dgt, xgt, sh_gt, wgt)
        du_ref[1] = dx.astype(BF16)

    return _call(
        body,
        name="ffn_bwd",
        grid=(D_FF // FFN_ROWS,),
        in_specs=[
            pl.BlockSpec((2, FFN_ROWS, t), lambda i: (0, i, 0)),
            pl.BlockSpec((2, FFN_ROWS, FFN_CONV), lambda i: (0, i, 0)),
            pl.BlockSpec((2, FFN_ROWS, 1), lambda i: (0, i, 0)),
            pl.BlockSpec((FFN_ROWS, t), lambda i: (i, 0)),
        ],
        out_specs=[pl.BlockSpec((2, FFN_ROWS, t), lambda i: (0, i, 0)), pl.BlockSpec((2, FFN_ROWS, 128), lambda i: (0, i, 0))],
        out_shape=[jax.ShapeDtypeStruct((2, D_FF, t), BF16), jax.ShapeDtypeStruct((2, D_FF, 128), F32)],
        semantics=("parallel",), args=(u0, w_col, b_col, dg), send=send)


def _adamw_math(w, g, m, v):
    m = ADAM_B1 * m + (1.0 - ADAM_B1) * g
    v = ADAM_B2 * v + (1.0 - ADAM_B2) * (g * g)
    m_hat = m / (1.0 - ADAM_B1 ** ADAM_STEP)
    v_hat = v / (1.0 - ADAM_B2 ** ADAM_STEP)
    delta = -ADAM_LR * (m_hat / (jnp.sqrt(v_hat) + ADAM_EPS) + ADAM_WD * w)
    return delta, m, v


def _adamw_sharded(parts, w, m, v, name):
    r, c = w.shape[0], w.shape[-1]
    tc = _tile(c, (256, 128))
    blk_shape = (r, tc) if w.ndim == 2 else (r, 1, tc)
    slots = parts.shape[0]

    def body(p_ref, w_ref, m_ref, v_ref, g_ref, d_ref, nm_ref, nv_ref):
        g = p_ref[0].astype(F32)
        for s in range(1, slots):
            g = g + p_ref[s].astype(F32)
        flat = lambda ref: ref[...].reshape(r, tc)
        d, nm, nv = _adamw_math(flat(w_ref), g, flat(m_ref), flat(v_ref))
        for ref, val in ((g_ref, g), (d_ref, d), (nm_ref, nm), (nv_ref, nv)):
            ref[...] = val.reshape(blk_shape)

    blk = pl.BlockSpec(blk_shape, (lambda i: (0, i)) if w.ndim == 2 else (lambda i: (0, 0, i)))
    out = jax.ShapeDtypeStruct(w.shape, F32)
    return pl.pallas_call(
        body,
        name=name,
        grid=(c // tc,),
        in_specs=[pl.BlockSpec((slots, r, tc), lambda i: (0, 0, i)), blk, blk, blk],
        out_specs=[blk, blk, blk, blk],
        out_shape=[out, out, out, out],
        compiler_params=_cparams("parallel"),
    )(parts, w, m, v)


def _lane_offsets(sizes):
    offsets, pos = [], 0
    for n in sizes:
        offsets.append(pos)
        pos += -(-n // 128) * 128
    return offsets, pos


def _pack_row(parts):
    rows = [p.reshape(1, -1).astype(F32) for p in parts]
    return jnp.concatenate([jnp.pad(r, ((0, 0), (0, -r.shape[1] % 128))) for r in rows], axis=1)


def _small_update(parts, me, full_sizes, ws, ms, vs):
    n = len(ws)
    offsets, _ = _lane_offsets([1] + list(full_sizes))

    def body(me_ref, p_ref, *refs):
        w_refs, m_refs, v_refs = refs[:n], refs[n:2 * n], refs[2 * n:3 * n]
        scalar_ref, out_refs = refs[3 * n], refs[3 * n + 1:]
        tot = p_ref[0]
        for s in range(1, N_DEV):
            tot = tot + p_ref[s]
        scalar_ref[...] = tot[:, 0:1]
        for k in range(n):
            g_ref, d_ref, nm_ref, nv_ref = out_refs[4 * k:4 * k + 4]
            taps, cols = w_refs[k].shape
            if taps == 1:
                g_ref[...] = tot[:, offsets[k + 1]:offsets[k + 1] + cols]
            else:
                full = full_sizes[k] // taps
                for tap in range(taps):
                    mine = jnp.zeros((1, cols), F32)
                    for d in range(N_DEV):
                        lo = offsets[k + 1] + tap * full + d * cols
                        mine = jnp.where(me_ref[0] == d, tot[:, lo:lo + cols], mine)
                    g_ref[tap:tap + 1, :] = mine
            d_ref[...], nm_ref[...], nv_ref[...] = _adamw_math(w_refs[k][...], g_ref[...], m_refs[k][...], v_refs[k][...])

    vmem = pl.BlockSpec(memory_space=pltpu.VMEM)
    out_shape = [jax.ShapeDtypeStruct((1, 1), F32)]
    for wk in ws:
        out_shape += [jax.ShapeDtypeStruct(wk.shape, F32)] * 4
    res = pl.pallas_call(
        body,
        name="small_update",
        in_specs=[pl.BlockSpec(memory_space=pltpu.SMEM)] + [vmem] * (1 + 3 * n),
        out_specs=[vmem] * len(out_shape),
        out_shape=out_shape,
    )(me, parts, *ws, *ms, *vs)
    return res[0], [res[1 + 4 * k:5 + 4 * k] for k in range(n)]


ANY = pl.BlockSpec(memory_space=pl.ANY)
FLIPS = [(k >> 2 & 1, k >> 1 & 1, k & 1) for k in range(1, N_DEV)]


def _place():
    return lax.axis_index("x"), lax.axis_index("y"), lax.axis_index("c")


HBM = pl.BlockSpec(memory_space=pltpu.HBM)
SEM = pl.BlockSpec(memory_space=pltpu.SEMAPHORE)
EFFECT = pltpu.SideEffectType.DATAFLOW_SIDE_EFFECTING


def _peer_copy(gather, src_ref, land_ref, send_sems, recv_sems, k, sending):
    x, y, c = _place()
    fx, fy, fc = FLIPS[k]
    me = 4 * x + 2 * y + c
    peer = 4 * (x ^ fx) + 2 * (y ^ fy) + (c ^ fc)
    return pltpu.make_async_remote_copy(
        src_ref=src_ref if gather else src_ref.at[peer],
        dst_ref=land_ref.at[me if sending else peer],
        send_sem=send_sems.at[k], recv_sem=recv_sems.at[k],
        device_id=(x ^ fx, y ^ fy, c ^ fc), device_id_type=MESH)


SIBLING = 0
OTHER_CHIPS = (1, 3, 5)


def _gather_start(srcs, name, via_sibling):
    n = len(srcs)
    lands = [lax.empty((N_DEV,) + s.shape, s.dtype) for s in srcs]

    def body(*refs):
        src_refs, land_refs = refs[:n], refs[n:2 * n]
        send, recv = refs[2 * n:3 * n], refs[3 * n:4 * n]
        for i in range(n):
            for k in (SIBLING,) + OTHER_CHIPS if via_sibling else range(N_DEV - 1):
                _peer_copy(True, src_refs[i], land_refs[i], send[i], recv[i], k, True).start()

    sem = pltpu.SemaphoreType.DMA((N_DEV - 1,))
    hbm = lambda a: pltpu.HBM(a.shape, a.dtype)
    res = pl.pallas_call(
        body,
        name=name,
        in_specs=[HBM] * (2 * n),
        out_specs=[SEM] * (2 * n) + [HBM] * (2 * n),
        out_shape=[sem] * (2 * n) + [hbm(s) for s in srcs] + [hbm(a) for a in lands],
        input_output_aliases={i: 2 * n + i for i in range(2 * n)},
        compiler_params=pltpu.CompilerParams(has_side_effects=EFFECT),
    )(*[pltpu.with_memory_space_constraint(a, pltpu.HBM) for a in list(srcs) + lands])
    return res[:n], res[n:2 * n], res[2 * n:3 * n], res[3 * n:4 * n]


def _exchange_wait(send_sems, recv_sems, src, land, after, gather, name):
    def body(src_ref, land_ref, send_ref, recv_ref, after_ref, src_out, land_out):
        for k in range(N_DEV - 1):
            cp = _peer_copy(gather, src_ref, land_ref, send_ref, recv_ref, k, False)
            cp.wait_send()
            cp.wait_recv()

    hbm = lambda a: pltpu.HBM(a.shape, a.dtype)
    return pl.pallas_call(
        body,
        name=name,
        in_specs=[HBM, HBM, SEM, SEM, ANY],
        out_specs=[HBM, HBM],
        out_shape=[hbm(src), hbm(land)],
        input_output_aliases={0: 0, 1: 1},
        compiler_params=pltpu.CompilerParams(has_side_effects=EFFECT),
    )(src, land, send_sems, recv_sems, after)


def _own_slot(src, land, me, gather):
    own = src[None] if gather else lax.dynamic_slice_in_dim(src, me, 1, axis=0)
    return lax.dynamic_update_slice_in_dim(land, own, me, axis=0)


def _forwarded_copy(land_ref, send_sems, recv_sems, j, sending):
    x, y, c = _place()
    fx, fy, _ = FLIPS[OTHER_CHIPS[j]]
    slot = 4 * (x ^ fx) + 2 * (y ^ fy) + (c if sending else 1 - c)
    return pltpu.make_async_remote_copy(
        src_ref=land_ref.at[slot], dst_ref=land_ref.at[slot], send_sem=send_sems.at[j], recv_sem=recv_sems.at[j],
        device_id=(x, y, 1 - c), device_id_type=MESH)


def _gather_forward(send_sems, recv_sems, srcs, lands, after, name):
    n = len(srcs)

    def body(*refs):
        src_refs, land_refs = refs[:n], refs[n:2 * n]
        send, recv = refs[2 * n:3 * n], refs[3 * n:4 * n]
        fwd_send, fwd_recv = refs[4 * n + 1:5 * n + 1], refs[5 * n + 1:6 * n + 1]
        for i in range(n):
            for j, k in enumerate(OTHER_CHIPS):
                _peer_copy(True, src_refs[i], land_refs[i], send[i], recv[i], k, False).wait_recv()
                _forwarded_copy(land_refs[i], fwd_send[i], fwd_recv[i], j, True).start()

    sem = pltpu.SemaphoreType.DMA((len(OTHER_CHIPS),))
    hbm = lambda a: pltpu.HBM(a.shape, a.dtype)
    res = pl.pallas_call(
        body,
        name=name,
        in_specs=[HBM] * (2 * n) + [SEM] * (2 * n) + [ANY],
        out_specs=[SEM] * (2 * n) + [HBM] * (2 * n),
        out_shape=[sem] * (2 * n) + [hbm(a) for a in srcs] + [hbm(a) for a in lands],
        input_output_aliases={i: 2 * n + i for i in range(2 * n)},
        compiler_params=pltpu.CompilerParams(has_side_effects=EFFECT),
    )(*srcs, *lands, *send_sems, *recv_sems, after)
    return res[:n], res[n:2 * n], res[2 * n:3 * n], res[3 * n:4 * n]


def _gather_wait_forwarded(send_sems, recv_sems, fwd_send, fwd_recv, src, land, after, name):
    def body(src_ref, land_ref, send_ref, recv_ref, fwd_send_ref, fwd_recv_ref, after_ref, src_out, land_out):
        for k in (SIBLING,) + OTHER_CHIPS:
            _peer_copy(True, src_ref, land_ref, send_ref, recv_ref, k, False).wait_send()
        _peer_copy(True, src_ref, land_ref, send_ref, recv_ref, SIBLING, False).wait_recv()
        for j in range(len(OTHER_CHIPS)):
            _forwarded_copy(land_ref, fwd_send_ref, fwd_recv_ref, j, True).wait_send()
            _forwarded_copy(land_ref, fwd_send_ref, fwd_recv_ref, j, False).wait_recv()

    hbm = lambda a: pltpu.HBM(a.shape, a.dtype)
    return pl.pallas_call(
        body,
        name=name,
        in_specs=[HBM, HBM, SEM, SEM, SEM, SEM, ANY],
        out_specs=[HBM, HBM],
        out_shape=[hbm(src), hbm(land)],
        input_output_aliases={0: 0, 1: 1},
        compiler_params=pltpu.CompilerParams(has_side_effects=EFFECT),
    )(src, land, send_sems, recv_sems, fwd_send, fwd_recv, after)


N_CHIPS = N_DEV // 2


def _pair_exchange(by_core, meanwhile, name):
    def copy(src_ref, land_ref, send_sems, recv_sems, q):
        x, y, c = _place()
        return pltpu.make_async_remote_copy(
            src_ref=src_ref.at[q, 1 - c], dst_ref=land_ref.at[q], send_sem=send_sems.at[q], recv_sem=recv_sems.at[q],
            device_id=(x, y, 1 - c), device_id_type=MESH)

    def start(src_ref, land_ref, send_sems, recv_sems, src_out, land_out):
        for q in range(N_CHIPS):
            copy(src_ref, land_ref, send_sems, recv_sems, q).start()

    def wait(src_ref, land_ref, send_sems, recv_sems, after_ref, src_out, land_out):
        for q in range(N_CHIPS):
            cp = copy(src_ref, land_ref, send_sems, recv_sems, q)
            cp.wait_send()
            cp.wait_recv()

    sem = pltpu.SemaphoreType.DMA((N_CHIPS,))
    hbm_src = pltpu.HBM(by_core.shape, by_core.dtype)
    hbm_land = pltpu.HBM(by_core.shape[:1] + by_core.shape[2:], by_core.dtype)
    params = pltpu.CompilerParams(has_side_effects=EFFECT)
    send_sems, recv_sems, src, land = pl.pallas_call(
        start, name=name + "_start", in_specs=[HBM, HBM], out_specs=[SEM, SEM, HBM, HBM],
        out_shape=[sem, sem, hbm_src, hbm_land], input_output_aliases={0: 2, 1: 3}, compiler_params=params,
    )(pltpu.with_memory_space_constraint(by_core, pltpu.HBM),
      pltpu.with_memory_space_constraint(lax.empty(hbm_land.shape, by_core.dtype), pltpu.HBM))
    return pl.pallas_call(
        wait, name=name + "_wait", in_specs=[HBM, HBM, SEM, SEM, ANY], out_specs=[HBM, HBM],
        out_shape=[hbm_src, hbm_land], input_output_aliases={0: 0, 1: 1}, compiler_params=params,
    )(src, land, send_sems, recv_sems, meanwhile(src))


def _pair_add(by_core, landed, name):
    q, _, r, c = by_core.shape
    tc = _tile(c, (512, 256, 128))

    def body(a_ref, b_ref, o_ref):
        mine = a_ref[0, lax.axis_index("c")]
        o_ref[0] = (mine.astype(F32) + b_ref[0].astype(F32)).astype(BF16)

    blk = pl.BlockSpec((1, r, tc), lambda i, j: (i, 0, j))
    return pl.pallas_call(
        body, name=name, grid=(q, c // tc),
        in_specs=[pl.BlockSpec((1, 2, r, tc), lambda i, j: (i, 0, 0, j)), blk], out_specs=blk,
        out_shape=jax.ShapeDtypeStruct(landed.shape, BF16), compiler_params=_cparams("parallel", "parallel"),
    )(by_core, landed)


def _chip_copy(src_ref, land_ref, send_sems, recv_sems, j, sending):
    x, y, c = _place()
    fx, fy, _ = FLIPS[OTHER_CHIPS[j]]
    here, there = 2 * x + y, 2 * (x ^ fx) + (y ^ fy)
    return pltpu.make_async_remote_copy(
        src_ref=src_ref.at[there], dst_ref=land_ref.at[here if sending else there],
        send_sem=send_sems.at[j], recv_sem=recv_sems.at[j],
        device_id=(x ^ fx, y ^ fy, c), device_id_type=MESH)


def _chip_wait(send_sems, recv_sems, src, land, after, name):
    def body(src_ref, land_ref, send_ref, recv_ref, after_ref, src_out, land_out):
        for j in range(len(OTHER_CHIPS)):
            cp = _chip_copy(src_ref, land_ref, send_ref, recv_ref, j, False)
            cp.wait_send()
            cp.wait_recv()

    hbm = lambda a: pltpu.HBM(a.shape, a.dtype)
    return pl.pallas_call(
        body,
        name=name,
        in_specs=[HBM, HBM, SEM, SEM, ANY],
        out_specs=[HBM, HBM],
        out_shape=[hbm(src), hbm(land)],
        input_output_aliases={0: 0, 1: 1},
        compiler_params=pltpu.CompilerParams(has_side_effects=EFFECT),
    )(src, land, send_sems, recv_sems, after)


def _col(v):
    return v.reshape(-1, 1).astype(F32)


def _local_step(x, tgt, started, weight, small, pair_sums, handles):
    t = x.shape[0]
    n1 = _col(small["norm1_w"])
    n2 = _col(small["norm2_w"])
    nf = _col(small["final_norm_w"])
    bg = _col(small["b_gate"])
    sinks = small["attn_sinks"].reshape(-1).astype(F32)
    cbias = _col(small["ssd_conv_b"])
    dtb = _col(small["dt_bias"])
    alog = _col(small["a_log"])
    dsk = _col(small["d_skip"])
    gnw = _col(small["ssd_norm_w"])
    fb = small["ffn_conv_b"].reshape(2, D_FF, 1)

    xt, xn = _norm_fwd_tokens(x, n1, started, "norm1_fwd")
    cw = weight("ssd_conv_w", xn).T
    fw = weight("ffn_conv_w", xn).T.reshape(2, D_FF, FFN_CONV)
    w_in_t = weight("w_in", xn)
    proj = _matmul(w_in_t, xn, nt=False, out_dtype=F32, name="mm_in")
    ao, lse = _attn_fwd(proj, sinks)
    w_ao = weight("w_attn_o", ao)
    attn = _matmul(w_ao, ao, nt=False, out_dtype=F32, name="mm_attn_o", tn_a=True)
    xbc = _conv_silu_fwd(proj, cw, cbias)
    y, hst = _ssd_fwd(xbc, proj, dtb, alog, dsk)
    yn = _gnorm_fwd(y, proj, gnw)
    w_so = weight("w_ssd_o", yn)
    ssd = _matmul(w_so, yn, nt=False, out_dtype=F32, name="mm_ssd_o", tn_a=True)
    mix = _gate_fwd(proj, bg, attn, ssd)
    w_out = weight("w_out", mix)
    h1 = _matmul(w_out, mix, nt=False, out_dtype=F32, name="mm_out", add=xt, tn_a=True)
    hn = _norm_fwd(h1, n2, "norm2_fwd")
    w_up_t = weight("w_up", hn)
    u0 = _matmul(w_up_t, hn, nt=False, out_dtype=F32, name="mm_up").reshape(2, D_FF, t)
    gl = _ffn_fwd(u0, fw, fb)
    w_down = weight("w_down", gl)
    h2 = _matmul(w_down, gl, nt=False, out_dtype=F32, name="mm_down", add=h1, tn_a=True)
    dh2, loss, d_nf = _final_norm_loss(h2, tgt, nf)

    g = {}

    def sending(weight_name, grad, fn, *args, **kwargs):
        chunks = grad if grad.ndim == 3 else grad.reshape(N_DEV, -1, D_MODEL)
        out, handles[weight_name] = fn(*args, send=chunks, **kwargs)
        return out

    g_down = _matmul(gl, dh2, nt=True, out_dtype=BF16, name="mm_d_w_down")
    dgl = _matmul(w_down, dh2, nt=False, out_dtype=F32, name="mm_d_glu")
    du0, d_fwb = sending("w_down", g_down, _ffn_bwd, u0, fw, fb, dgl)
    du0 = du0.reshape(2 * D_FF, t)
    g_up = _matmul(du0, hn, nt=True, out_dtype=BF16, name="mm_d_w_up")
    dhn = sending("w_up", g_up, _matmul, w_up_t, du0, nt=False, out_dtype=F32, name="mm_d_hn", tn_a=True)
    dh1, d_n2 = _norm_bwd(dhn, h1, n2, dh2, "norm2_bwd")
    g_out = _matmul(mix, dh1, nt=True, out_dtype=BF16, name="mm_d_w_out")
    dmix = _matmul(w_out, dh1, nt=False, out_dtype=F32, name="mm_d_mix")
    d_attn, d_ssd, d_ga, d_gs, d_ba, d_bs = sending("w_out", g_out, _gate_bwd, proj, bg, attn, ssd, dmix)
    g_ao = _matmul(ao, d_attn, nt=True, out_dtype=BF16, name="mm_d_w_attn_o")
    dao = _matmul(w_ao, d_attn, nt=False, out_dtype=F32, name="mm_d_ao")
    g_so = _matmul(yn, d_ssd, nt=True, out_dtype=BF16, name="mm_d_w_ssd_o")
    dyn = _matmul(w_so, d_ssd, nt=False, out_dtype=F32, name="mm_d_yn")
    dy, dproj, d_gnw = sending("w_ssd_o", g_so, _gnorm_bwd, dyn, y, proj, gnw)
    dxs, dbm, dcm, ddt, d_alog, d_dsk, d_dtb = _ssd_bwd(xbc, proj, dtb, alog, dsk, hst, dy)
    dproj, dwb_xs = _conv_silu_bwd(proj, cw, cbias, dxs, 0, dproj, "ssd_conv_bwd_x")
    dproj, dwb_b = _conv_silu_bwd(proj, cw, cbias, dbm, D_INNER, dproj, "ssd_conv_bwd_b")
    dproj, dwb_c = _conv_silu_bwd(proj, cw, cbias, dcm, D_INNER + BC_DIM, dproj, "ssd_conv_bwd_c")
    dwb_conv = jnp.concatenate([dwb_xs, dwb_b, dwb_c], axis=0)
    dproj, d_sinks = sending("w_attn_o", g_ao, _attn_bwd, proj, sinks, ao, lse, dao, dproj)
    for rows, part in ((OFF_DT, ddt.astype(BF16)), (OFF_GA, d_ga), (OFF_GS, d_gs)):
        dproj = lax.dynamic_update_slice(dproj, part, (rows, 0))
    g_in = pair_sums(_matmul(dproj, xn, nt=True, out_dtype=BF16, name="mm_d_w_in"))
    dxn = sending("w_in", g_in, _matmul, w_in_t, dproj, nt=False, out_dtype=F32, name="mm_d_xn", tn_a=True)
    dx, d_n1 = _norm_bwd(dxn, xt, n1, dh1, "norm1_bwd", tokens_out=True)

    g["norm1_w"] = d_n1
    g["b_gate"] = jnp.concatenate([d_ba, d_bs], axis=0)
    g["attn_sinks"] = d_sinks
    g["ssd_conv_w"] = dwb_conv[:, :SSD_CONV].T
    g["ssd_conv_b"] = dwb_conv[:, SSD_CONV]
    g["dt_bias"] = d_dtb
    g["a_log"] = d_alog
    g["d_skip"] = d_dsk
    g["ssd_norm_w"] = d_gnw
    g["norm2_w"] = d_n2
    d_fwb = d_fwb.reshape(2 * D_FF, 128)
    g["ffn_conv_w"] = d_fwb[:, :FFN_CONV].T
    g["ffn_conv_b"] = d_fwb[:, FFN_CONV]
    g["final_norm_w"] = d_nf
    return loss, dx, g


SMALL = ("norm1_w", "b_gate", "attn_sinks", "ssd_conv_w", "ssd_conv_b", "dt_bias", "a_log", "d_skip", "ssd_norm_w",
         "norm2_w", "ffn_conv_w", "ffn_conv_b", "final_norm_w")
WEIGHT_ORDER = ("norm1_w", "w_in", "b_gate", "attn_sinks", "w_attn_o", "ssd_conv_w", "ssd_conv_b", "dt_bias", "a_log",
                "d_skip", "ssd_norm_w", "w_ssd_o", "w_out", "norm2_w", "w_up", "ffn_conv_w", "ffn_conv_b", "w_down",
                "final_norm_w")


def kernel(x, norm1_w, w_in, b_gate, attn_sinks, w_attn_o, ssd_conv_w, ssd_conv_b, dt_bias, a_log, d_skip, ssd_norm_w, w_ssd_o, w_out, norm2_w, w_up, ffn_conv_w, ffn_conv_b, w_down, final_norm_w, loss_target, m_norm1_w, m_w_in, m_b_gate, m_attn_sinks, m_w_attn_o, m_ssd_conv_w, m_ssd_conv_b, m_dt_bias, m_a_log, m_d_skip, m_ssd_norm_w, m_w_ssd_o, m_w_out, m_norm2_w, m_w_up, m_ffn_conv_w, m_ffn_conv_b, m_w_down, m_final_norm_w, v_norm1_w, v_w_in, v_b_gate, v_attn_sinks, v_w_attn_o, v_ssd_conv_w, v_ssd_conv_b, v_dt_bias, v_a_log, v_d_skip, v_ssd_norm_w, v_w_ssd_o, v_w_out, v_norm2_w, v_w_up, v_ffn_conv_w, v_ffn_conv_b, v_w_down, v_final_norm_w):
    w = dict(norm1_w=norm1_w, w_in=w_in, b_gate=b_gate, attn_sinks=attn_sinks, w_attn_o=w_attn_o, ssd_conv_w=ssd_conv_w, ssd_conv_b=ssd_conv_b, dt_bias=dt_bias, a_log=a_log, d_skip=d_skip, ssd_norm_w=ssd_norm_w, w_ssd_o=w_ssd_o, w_out=w_out, norm2_w=norm2_w, w_up=w_up, ffn_conv_w=ffn_conv_w, ffn_conv_b=ffn_conv_b, w_down=w_down, final_norm_w=final_norm_w)
    m = dict(norm1_w=m_norm1_w, w_in=m_w_in, b_gate=m_b_gate, attn_sinks=m_attn_sinks, w_attn_o=m_w_attn_o, ssd_conv_w=m_ssd_conv_w, ssd_conv_b=m_ssd_conv_b, dt_bias=m_dt_bias, a_log=m_a_log, d_skip=m_d_skip, ssd_norm_w=m_ssd_norm_w, w_ssd_o=m_w_ssd_o, w_out=m_w_out, norm2_w=m_norm2_w, w_up=m_w_up, ffn_conv_w=m_ffn_conv_w, ffn_conv_b=m_ffn_conv_b, w_down=m_w_down, final_norm_w=m_final_norm_w)
    v = dict(norm1_w=v_norm1_w, w_in=v_w_in, b_gate=v_b_gate, attn_sinks=v_attn_sinks, w_attn_o=v_w_attn_o, ssd_conv_w=v_ssd_conv_w, ssd_conv_b=v_ssd_conv_b, dt_bias=v_dt_bias, a_log=v_a_log, d_skip=v_d_skip, ssd_norm_w=v_ssd_norm_w, w_ssd_o=v_w_ssd_o, w_out=v_w_out, norm2_w=v_norm2_w, w_up=v_w_up, ffn_conv_w=v_ffn_conv_w, ffn_conv_b=v_ffn_conv_b, w_down=v_w_down, final_norm_w=v_final_norm_w)
    me = 4 * lax.axis_index("x") + 2 * lax.axis_index("y") + lax.axis_index("c")

    shards = {"ssd_conv_w": ssd_conv_w[0], "ffn_conv_w": ffn_conv_w[0], "w_in": w_in[0].T.astype(BF16),
              "w_attn_o": w_attn_o[0].astype(BF16), "w_ssd_o": w_ssd_o[0].astype(BF16), "w_out": w_out[0].astype(BF16),
              "w_up": w_up[0].T.astype(BF16), "w_down": w_down[0].astype(BF16)}
    order = list(shards)
    g_send, g_recv, g_src, g_land = _gather_start(list(shards.values()), "gather_start", True)
    first = ("ssd_conv_w", "ffn_conv_w", "w_in")
    forwarded = {}

    def weight(name, after):
        if name not in forwarded:
            group = [k for k in order if (k in first) == (name in first)]
            idx = [order.index(k) for k in group]
            handles = _gather_forward([g_send[i] for i in idx], [g_recv[i] for i in idx], [g_src[i] for i in idx],
                                      [g_land[i] for i in idx], after, "gather_forward_for_" + name)
            forwarded.update(zip(group, zip(*handles)))
        i = order.index(name)
        src, land = _gather_wait_forwarded(g_send[i], g_recv[i], *forwarded[name], after, "gather_wait_" + name)
        land = _own_slot(src, land, me, True)
        if name == "ssd_conv_w":
            return jnp.transpose(land, (1, 0, 2)).reshape(SSD_CONV, XBC_DIM)
        if name == "ffn_conv_w":
            return jnp.transpose(land, (1, 0, 2)).reshape(FFN_CONV, 2 * D_FF)
        return land.reshape(-1, D_MODEL)

    res, pending = {}, {}

    def update(name, after):
        if name == "w_in":
            parts = _own_slot(*_chip_wait(*pending[name], after, "grad_wait_" + name), me // 2, False)
        else:
            parts = _own_slot(*_exchange_wait(*pending[name], after, False, "grad_wait_" + name), me, False)
        view, back = {
            "w_in": (lambda a: jnp.transpose(a, (2, 0, 1)), lambda r: jnp.transpose(r, (1, 2, 0))),
            "w_up": (lambda a: a[0].T, lambda r: r.T[None]),
        }.get(name, (lambda a: a[0], lambda r: r[None]))
        done = _adamw_sharded(parts, view(w[name]), view(m[name]), view(v[name]), "adamw_" + name)
        res[name] = [back(r) for r in done]
        return done[0]

    def pair_sums(grad):
        by_core, landed = _pair_exchange(grad.reshape(N_CHIPS, 2, -1, D_MODEL),
                                         lambda started: update("w_up", update("w_down", started)), "grad_pair_w_in")
        return _pair_add(by_core, landed, "grad_pair_add_w_in")

    small = {k: w[k][0] if k != "final_norm_w" else w[k] for k in SMALL}
    loss, dx, g = _local_step(x[0], loss_target[0], g_src[0], weight, small, pair_sums, pending)

    packed = _pack_row([loss] + [g[k] for k in SMALL])
    s_send, s_recv, s_src, s_land = _gather_start([packed], "small_grads_start", False)
    after = s_src[0]
    for name in ("w_out", "w_attn_o", "w_ssd_o", "w_in"):
        after = update(name, after)

    rows = _own_slot(*_exchange_wait(s_send[0], s_recv[0], s_src[0], s_land[0], after, True, "small_grads_wait"),
                     me, True)
    flat = lambda a: a.reshape(-1, a.shape[-1])
    loss_sum, updates = _small_update(
        rows, me.reshape(1), [g[k].size for k in SMALL],
        [flat(w[k]) for k in SMALL], [flat(m[k]) for k in SMALL], [flat(v[k]) for k in SMALL])
    for k, upd in zip(SMALL, updates):
        res[k] = [u.reshape(w[k].shape) for u in upd]

    grad_x = dx[None]
    outs = [loss_sum.reshape(()), grad_x]
    for i in range(4):
        outs.extend(res[k][i] for k in WEIGHT_ORDER)
    return tuple(outs)
```

```python
import jax
import jax.numpy as jnp
from jax import lax
from jax.experimental import pallas as pl
from jax.experimental.pallas import tpu as pltpu

F32 = jnp.float32
BF16 = jnp.bfloat16
HIGHEST = lax.Precision.HIGHEST

D_MODEL = 1024
N_Q_HEADS = 16
N_KV_HEADS = 4
HEAD_DIM = 64
WINDOW = 128
Q_PER_KV = N_Q_HEADS // N_KV_HEADS
Q_DIM = N_Q_HEADS * HEAD_DIM
KV_DIM = N_KV_HEADS * HEAD_DIM
D_INNER = 2048
SSD_HEAD_DIM = 64
N_SSD_HEADS = 32
N_SSD_GROUPS = 4
HEADS_PER_GROUP = N_SSD_HEADS // N_SSD_GROUPS
D_STATE = 128
BC_DIM = N_SSD_GROUPS * D_STATE
XBC_DIM = D_INNER + 2 * BC_DIM
SSD_CONV = 4
CHUNK = 128
D_FF = 2816
FFN_CONV = 3
EPS = 1e-5
NEG = -1e30
IN_DIM = 8736
N_DEV = 8

OFF_Q = 0
OFF_K = OFF_Q + Q_DIM
OFF_V = OFF_K + KV_DIM
OFF_Z = OFF_V + KV_DIM
OFF_X = OFF_Z + D_INNER
OFF_DT = OFF_X + XBC_DIM
OFF_GA = OFF_DT + N_SSD_HEADS
OFF_GS = OFF_GA + D_MODEL

ADAM_LR = 0.001
ADAM_B1 = 0.9
ADAM_B2 = 0.999
ADAM_EPS = 1e-08
ADAM_WD = 0.01
ADAM_STEP = 10

LANES = 128
BF16_TILE_ROWS = 16
VMEM_BYTES = 64 * 1024 * 1024
VMEM_LIMIT = VMEM_BYTES * 3 // 4
MESH = pl.DeviceIdType.MESH


def _cparams(*sem):
    return pltpu.CompilerParams(dimension_semantics=sem, vmem_limit_bytes=VMEM_LIMIT)


def _tile(n, prefs):
    for p in prefs:
        if n % p == 0:
            return p
    return n


def _sigmoid(x):
    return 1.0 / (1.0 + jnp.exp(-x))


def _softplus(x):
    return jnp.maximum(x, 0.0) + jnp.log(1.0 + jnp.exp(-jnp.abs(x)))


def _rowsum(x):
    return jnp.sum(x, axis=1, keepdims=True)


def _colsum(x):
    return jnp.sum(x, axis=0, keepdims=True)


def _dot(a, b):
    return jnp.dot(a, b, preferred_element_type=F32)


def _dot_nt(a, b):
    return lax.dot_general(a, b, (((1,), (1,)), ((), ())), preferred_element_type=F32)


def _dot_tn(a, b):
    return lax.dot_general(a, b, (((0,), (0,)), ((), ())), preferred_element_type=F32)


def _shift_right(x, j):
    if j == 0:
        return x
    r = pltpu.roll(x, j, 1)
    lane = lax.broadcasted_iota(jnp.int32, (x.shape[0], 128), 1)
    return jnp.concatenate([jnp.where(lane >= j, r[:, :128], 0.0), r[:, 128:]], axis=1)


def _shift_left(x, j):
    if j == 0:
        return x
    n = x.shape[1]
    r = pltpu.roll(x, n - j, 1)
    lane = lax.broadcasted_iota(jnp.int32, (x.shape[0], 128), 1)
    return jnp.concatenate([r[:, :n - 128], jnp.where(lane < 128 - j, r[:, n - 128:], 0.0)], axis=1)


def _causal_conv(xv, wv, bv):
    taps = wv.shape[1]
    shifted = [_shift_right(xv, taps - 1 - k) for k in range(taps - 1)]
    y = bv + wv[:, taps - 1:taps] * xv
    for k in range(taps - 1):
        y = y + wv[:, k:k + 1] * shifted[k]
    return y, shifted


def _causal_conv_bwd(dy, xv, shifted, wv):
    taps = wv.shape[1]
    lane = lax.broadcasted_iota(jnp.int32, (dy.shape[0], 128), 1)
    dwb = jnp.where(lane == taps, _rowsum(dy), 0.0)
    dwb = jnp.where(lane == taps - 1, _rowsum(dy * xv), dwb)
    dx = wv[:, taps - 1:taps] * dy
    for k in range(taps - 1):
        dx = dx + wv[:, k:k + 1] * _shift_left(dy, taps - 1 - k)
        dwb = jnp.where(lane == k, _rowsum(dy * shifted[k]), dwb)
    return dx, dwb


def _call(body, *, name, grid, in_specs, out_specs, out_shape, args, semantics, scratch_shapes=(), aliases=None,
          send=None):
    aliases = dict(aliases or {})
    if send is None:
        return pl.pallas_call(body, name=name, grid=grid, in_specs=in_specs, out_specs=out_specs, out_shape=out_shape,
                              scratch_shapes=list(scratch_shapes), input_output_aliases=aliases,
                              compiler_params=_cparams(*semantics))(*args)
    single = not isinstance(out_specs, (list, tuple))
    out_specs, out_shape = ([out_specs], [out_shape]) if single else (list(out_specs), list(out_shape))
    n_in, n_out = len(in_specs), len(out_specs)
    chips = send.shape[0] == N_DEV // 2
    n_copies = len(OTHER_CHIPS) if chips else N_DEV - 1

    def sending(*refs):
        ins, (src_ref, land_ref) = refs[:n_in], refs[n_in:n_in + 2]
        outs = refs[n_in + 2:n_in + 2 + n_out]
        send_sems, recv_sems = refs[n_in + 2 + n_out:n_in + 4 + n_out]
        scratch = refs[n_in + 6 + n_out:]
        step = 0
        for axis, size in enumerate(grid):
            step = step * size + pl.program_id(axis)

        @pl.when(step == 0)
        def _():
            for k in range(n_copies):
                if chips:
                    _chip_copy(src_ref, land_ref, send_sems, recv_sems, k, True).start()
                else:
                    _peer_copy(False, src_ref, land_ref, send_sems, recv_sems, k, True).start()

        body(*ins, *outs, *scratch)

    sem = pltpu.SemaphoreType.DMA((n_copies,))
    hbm = pltpu.HBM(send.shape, send.dtype)
    res = pl.pallas_call(
        sending, name=name, grid=grid,
        in_specs=list(in_specs) + [HBM, HBM],
        out_specs=out_specs + [SEM, SEM, HBM, HBM],
        out_shape=out_shape + [sem, sem, hbm, hbm],
        input_output_aliases={**aliases, n_in: n_out + 2, n_in + 1: n_out + 3},
        scratch_shapes=list(scratch_shapes),
        compiler_params=pltpu.CompilerParams(dimension_semantics=("arbitrary",) * len(grid), vmem_limit_bytes=VMEM_LIMIT,
                                             has_side_effects=EFFECT),
    )(*args, pltpu.with_memory_space_constraint(send, pltpu.HBM),
      pltpu.with_memory_space_constraint(lax.empty(send.shape, send.dtype), pltpu.HBM))
    return (res[0] if single else list(res[:n_out])), tuple(res[n_out:])


BLOCK_VMEM_BUDGET = VMEM_LIMIT * 3 // 4
MATMUL_MAX_TM = 768
MATMUL_MAX_TN = 3072
MATMUL_MAX_TK = 3072


def _largest_tile(n, align, cap):
    return max(d for d in range(align, min(n, cap) + 1, align) if n % d == 0)


def _matmul_tiles(m, n, k, a_bytes, b_bytes, out_bytes, has_add, m_align, k_align):
    tm = _largest_tile(m, m_align, MATMUL_MAX_TM)
    tk = _largest_tile(k, k_align, MATMUL_MAX_TK)
    for tn in sorted({d for d in range(LANES, min(n, MATMUL_MAX_TN) + 1, LANES) if n % d == 0}, reverse=True):
        need = 2 * (tm * tk * a_bytes + tk * tn * b_bytes) + tm * tn * (2 * out_bytes + (4 if k > tk else 0) + (8 if has_add else 0))
        if need <= BLOCK_VMEM_BUDGET:
            return tm, tn, tk
    return tm, LANES, tk


def _matmul(a, b, *, nt, out_dtype, name, add=None, tn_a=False, send=None):
    if tn_a:
        k, m = a.shape
    else:
        m, k = a.shape
    n = b.shape[0] if nt else b.shape[1]
    tm, tn, tk = _matmul_tiles(m, n, k, a.dtype.itemsize, b.dtype.itemsize, jnp.dtype(out_dtype).itemsize, add is not None,
                               LANES if tn_a else BF16_TILE_ROWS, BF16_TILE_ROWS if tn_a and not nt else LANES)
    nk = k // tk
    grid = (m // tm, n // tn, nk)

    def body(a_ref, b_ref, *rest):
        r_ref = None
        if add is not None:
            r_ref, rest = rest[0], rest[1:]
        o_ref = rest[0]
        av = a_ref[...].astype(BF16)
        bv = b_ref[...].astype(BF16)
        part = _dot_tn(av, bv) if tn_a else _dot_nt(av, bv) if nt else _dot(av, bv)

        def finish(r):
            if add is not None:
                r = r + r_ref[...]
            o_ref[...] = r.astype(out_dtype)

        if nk == 1:
            finish(part)
            return
        acc = rest[1]
        kk = pl.program_id(2)

        @pl.when(kk == 0)
        def _():
            acc[...] = part

        @pl.when((kk > 0) & (kk < nk - 1))
        def _():
            acc[...] += part

        @pl.when(kk == nk - 1)
        def _():
            finish(acc[...] + part)

    in_specs = [
        pl.BlockSpec((tk, tm), lambda i, j, kk: (kk, i)) if tn_a else pl.BlockSpec((tm, tk), lambda i, j, kk: (i, kk)),
        pl.BlockSpec((tn, tk), lambda i, j, kk: (j, kk)) if nt else pl.BlockSpec((tk, tn), lambda i, j, kk: (kk, j)),
    ]
    args = [a, b]
    if add is not None:
        in_specs.append(pl.BlockSpec((tm, tn), lambda i, j, kk: (i, j)))
        args.append(add)
    return _call(
        body, name=name, grid=grid, in_specs=in_specs, args=args,
        out_specs=pl.BlockSpec((tm, tn), lambda i, j, kk: (i, j)),
        out_shape=jax.ShapeDtypeStruct((m, n), out_dtype),
        scratch_shapes=[pltpu.VMEM((tm, tn), F32)] if nk > 1 else [],
        semantics=("parallel", "parallel", "arbitrary"), send=send)


def _norm_fwd(x, w_col, name):
    f, t = x.shape
    tt = _tile(t, (512, 256, 128))

    def body(x_ref, w_ref, o_ref):
        xv = x_ref[...]
        r = lax.rsqrt(jnp.mean(xv * xv, axis=0, keepdims=True) + EPS)
        o_ref[...] = (xv * r * w_ref[...]).astype(BF16)

    return pl.pallas_call(
        body,
        name=name,
        grid=(t // tt,),
        in_specs=[pl.BlockSpec((f, tt), lambda i: (0, i)), pl.BlockSpec((f, 1), lambda i: (0, 0))],
        out_specs=pl.BlockSpec((f, tt), lambda i: (0, i)),
        out_shape=jax.ShapeDtypeStruct((f, t), BF16),
        compiler_params=_cparams("parallel"),
    )(x, w_col)


def _norm_fwd_tokens(x, w_col, after, name):
    t, f = x.shape
    tt = _tile(t, (512, 256, 128))

    def body(x_ref, w_ref, after_ref, xt_ref, o_ref):
        xv = x_ref[...].T
        xt_ref[...] = xv
        r = lax.rsqrt(jnp.mean(xv * xv, axis=0, keepdims=True) + EPS)
        o_ref[...] = (xv * r * w_ref[...]).astype(BF16)

    blk = pl.BlockSpec((f, tt), lambda i: (0, i))
    return pl.pallas_call(
        body,
        name=name,
        grid=(t // tt,),
        in_specs=[pl.BlockSpec((tt, f), lambda i: (i, 0)), pl.BlockSpec((f, 1), lambda i: (0, 0)), ANY],
        out_specs=[blk, blk],
        out_shape=[jax.ShapeDtypeStruct((f, t), F32), jax.ShapeDtypeStruct((f, t), BF16)],
        compiler_params=_cparams("parallel"),
    )(x, w_col, after)


def _norm_bwd(dy, x, w_col, res, name, tokens_out=False):
    f, t = x.shape
    tt = _tile(t, (512, 256, 128))

    def body(dy_ref, x_ref, w_ref, res_ref, dx_ref, dw_ref):
        @pl.when(pl.program_id(0) == 0)
        def _():
            dw_ref[...] = jnp.zeros_like(dw_ref)

        xv = x_ref[...]
        r = lax.rsqrt(jnp.mean(xv * xv, axis=0, keepdims=True) + EPS)
        xhat = xv * r
        dyv = dy_ref[...]
        dw_ref[...] += _rowsum(dyv * xhat)
        dxhat = dyv * w_ref[...]
        dx = res_ref[...] + r * (dxhat - xhat * jnp.mean(dxhat * xhat, axis=0, keepdims=True))
        dx_ref[...] = dx.T if tokens_out else dx

    blk = pl.BlockSpec((f, tt), lambda i: (0, i))
    col = pl.BlockSpec((f, 1), lambda i: (0, 0))
    return pl.pallas_call(
        body,
        name=name,
        grid=(t // tt,),
        in_specs=[blk, blk, col, blk],
        out_specs=[pl.BlockSpec((tt, f), lambda i: (i, 0)) if tokens_out else blk, col],
        out_shape=[jax.ShapeDtypeStruct((t, f) if tokens_out else (f, t), F32), jax.ShapeDtypeStruct((f, 1), F32)],
        compiler_params=_cparams("arbitrary"),
    )(dy, x, w_col, res)


def _final_norm_loss(h, tgt, w_col):
    f, t = h.shape
    tt = _tile(t, (512, 256, 128))

    def body(h_ref, t_ref, w_ref, dh_ref, loss_ref, dw_ref):
        @pl.when(pl.program_id(0) == 0)
        def _():
            dw_ref[...] = jnp.zeros_like(dw_ref)
            loss_ref[...] = jnp.zeros_like(loss_ref)

        xv = h_ref[...]
        r = lax.rsqrt(jnp.mean(xv * xv, axis=0, keepdims=True) + EPS)
        xhat = xv * r
        wv = w_ref[...]
        err = xhat * wv - t_ref[...].T
        loss_ref[...] += 0.5 * _rowsum(jnp.mean(err * err, axis=0, keepdims=True))
        dyv = err * (1.0 / f)
        dw_ref[...] += _rowsum(dyv * xhat)
        dxhat = dyv * wv
        dh_ref[...] = r * (dxhat - xhat * jnp.mean(dxhat * xhat, axis=0, keepdims=True))

    blk = pl.BlockSpec((f, tt), lambda i: (0, i))
    col = pl.BlockSpec((f, 1), lambda i: (0, 0))
    one = pl.BlockSpec((1, 1), lambda i: (0, 0))
    return pl.pallas_call(
        body,
        name="final_norm_loss",
        grid=(t // tt,),
        in_specs=[blk, pl.BlockSpec((tt, f), lambda i: (i, 0)), col],
        out_specs=[blk, one, col],
        out_shape=[jax.ShapeDtypeStruct((f, t), F32), jax.ShapeDtypeStruct((1, 1), F32), jax.ShapeDtypeStruct((f, 1), F32)],
        compiler_params=_cparams("arbitrary"),
    )(h, tgt, w_col)


def _attn_mask(n):
    shape = (2 * WINDOW, Q_PER_KV * WINDOW)
    si = lax.broadcasted_iota(jnp.int32, shape, 0)
    qi = lax.broadcasted_iota(jnp.int32, shape, 1) & (WINDOW - 1)
    dist = WINDOW + qi - si
    return (dist >= 0) & (dist < WINDOW) & ((si >= WINDOW) | (n > 0))


def _lane_cat(ref, row0, rows):
    return jnp.concatenate([ref[row0 + i * rows:row0 + (i + 1) * rows, :] for i in range(Q_PER_KV)], axis=1)


def _attn_fwd(proj, sinks):
    t = proj.shape[1]
    nb = t // WINDOW
    scale = HEAD_DIM ** -0.5

    def body(s_ref, q_ref, kc_ref, kp_ref, vc_ref, vp_ref, o_ref, lse_ref):
        n = pl.program_id(0)
        valid = _attn_mask(n)
        for g in range(N_KV_HEADS):
            rows = slice(g * HEAD_DIM, (g + 1) * HEAD_DIM)
            kt = jnp.concatenate([kp_ref[rows, :], kc_ref[rows, :]], axis=1).astype(BF16)
            vt = jnp.concatenate([vp_ref[rows, :], vc_ref[rows, :]], axis=1).astype(BF16)
            qcat = (_lane_cat(q_ref, g * Q_PER_KV * HEAD_DIM, HEAD_DIM) * scale).astype(BF16)
            s = jnp.where(valid, _dot_tn(kt, qcat), NEG)
            sink = jnp.concatenate(
                [jnp.full((1, WINDOW), s_ref[g * Q_PER_KV + i], F32) for i in range(Q_PER_KV)], axis=1)
            m = jnp.maximum(jnp.max(s, axis=0, keepdims=True), sink)
            p = jnp.exp(s - m)
            denom = _colsum(p) + jnp.exp(sink - m)
            probs = (p / denom).astype(BF16)
            out = _dot(vt, probs)
            lse = m + jnp.log(denom)
            for i in range(Q_PER_KV):
                h = g * Q_PER_KV + i
                o_ref[h * HEAD_DIM:(h + 1) * HEAD_DIM, :] = out[:, i * WINDOW:(i + 1) * WINDOW]
                lse_ref[h:h + 1, :] = lse[:, i * WINDOW:(i + 1) * WINDOW]

    kb = OFF_K // KV_DIM
    vb = OFF_V // KV_DIM
    prev = lambda n: jnp.maximum(n - 1, 0)
    return pl.pallas_call(
        body,
        name="attn_fwd",
        grid=(nb,),
        in_specs=[
            pl.BlockSpec(memory_space=pltpu.SMEM),
            pl.BlockSpec((Q_DIM, WINDOW), lambda n: (0, n)),
            pl.BlockSpec((KV_DIM, WINDOW), lambda n: (kb, n)),
            pl.BlockSpec((KV_DIM, WINDOW), lambda n: (kb, prev(n))),
            pl.BlockSpec((KV_DIM, WINDOW), lambda n: (vb, n)),
            pl.BlockSpec((KV_DIM, WINDOW), lambda n: (vb, prev(n))),
        ],
        out_specs=[pl.BlockSpec((Q_DIM, WINDOW), lambda n: (0, n)), pl.BlockSpec((N_Q_HEADS, WINDOW), lambda n: (0, n))],
        out_shape=[jax.ShapeDtypeStruct((Q_DIM, t), F32), jax.ShapeDtypeStruct((N_Q_HEADS, t), F32)],
        compiler_params=_cparams("parallel"),
    )(sinks, proj, proj, proj, proj, proj)


def _attn_bwd(proj, sinks, out, lse, dout, dproj, send=None):
    t = proj.shape[1]
    nb = t // WINDOW
    scale = HEAD_DIM ** -0.5

    def body(s_ref, q_ref, kc_ref, kp_ref, vc_ref, vp_ref, o_ref, lse_ref, do_ref, dproj_ref,
             dqkv_ref, ds_ref, dk_carry, dv_carry):
        dq_ref = dqkv_ref.at[pl.ds(OFF_Q, Q_DIM)]
        dk_ref = dqkv_ref.at[pl.ds(OFF_K, KV_DIM)]
        dv_ref = dqkv_ref.at[pl.ds(OFF_V, KV_DIM)]
        step = pl.program_id(0)
        n = nb - 1 - step

        @pl.when(step == 0)
        def _():
            dk_carry[...] = jnp.zeros_like(dk_carry)
            dv_carry[...] = jnp.zeros_like(dv_carry)
            ds_ref[...] = jnp.zeros_like(ds_ref)

        valid = _attn_mask(n)
        for g in range(N_KV_HEADS):
            rows = slice(g * HEAD_DIM, (g + 1) * HEAD_DIM)
            q0 = g * Q_PER_KV * HEAD_DIM
            kt = jnp.concatenate([kp_ref[rows, :], kc_ref[rows, :]], axis=1).astype(BF16)
            vt = jnp.concatenate([vp_ref[rows, :], vc_ref[rows, :]], axis=1).astype(BF16)
            qf = _lane_cat(q_ref, q0, HEAD_DIM)
            qcat = qf.astype(BF16)
            ocat = _lane_cat(o_ref, q0, HEAD_DIM)
            docat = _lane_cat(do_ref, q0, HEAD_DIM)
            dob = docat.astype(BF16)
            lse_cat = jnp.concatenate(
                [lse_ref[g * Q_PER_KV + i:g * Q_PER_KV + i + 1, :] for i in range(Q_PER_KV)], axis=1)
            sink = jnp.concatenate(
                [jnp.full((1, WINDOW), s_ref[g * Q_PER_KV + i], F32) for i in range(Q_PER_KV)], axis=1)
            s = jnp.where(valid, _dot_tn(kt, (qf * scale).astype(BF16)), NEG)
            p = jnp.exp(s - lse_cat)
            dp = _dot_tn(vt, dob)
            delta = _colsum(docat * ocat)
            dsc = (p * (dp - delta)).astype(BF16)
            dsink_row = -jnp.exp(sink - lse_cat) * delta
            dq = _dot(kt, dsc) * scale
            dk = _dot_nt(qcat, dsc) * scale
            dv = _dot_nt(dob, p.astype(BF16))
            for i in range(Q_PER_KV):
                h = g * Q_PER_KV + i
                dq_ref[h * HEAD_DIM:(h + 1) * HEAD_DIM, :] = dq[:, i * WINDOW:(i + 1) * WINDOW].astype(BF16)
                ds_ref[h:h + 1, :] += _rowsum(dsink_row[:, i * WINDOW:(i + 1) * WINDOW])
            dk_ref[rows, :] = (dk[:, WINDOW:] + dk_carry[rows, :]).astype(BF16)
            dv_ref[rows, :] = (dv[:, WINDOW:] + dv_carry[rows, :]).astype(BF16)
            dk_carry[rows, :] = dk[:, :WINDOW]
            dv_carry[rows, :] = dv[:, :WINDOW]

    kb = OFF_K // KV_DIM
    vb = OFF_V // KV_DIM
    cur = lambda i: nb - 1 - i
    prev = lambda i: jnp.maximum(nb - 2 - i, 0)
    qspec = pl.BlockSpec((Q_DIM, WINDOW), lambda i: (0, cur(i)))
    return _call(
        body,
        name="attn_bwd",
        grid=(nb,),
        in_specs=[
            pl.BlockSpec(memory_space=pltpu.SMEM),
            qspec,
            pl.BlockSpec((KV_DIM, WINDOW), lambda i: (kb, cur(i))),
            pl.BlockSpec((KV_DIM, WINDOW), lambda i: (kb, prev(i))),
            pl.BlockSpec((KV_DIM, WINDOW), lambda i: (vb, cur(i))),
            pl.BlockSpec((KV_DIM, WINDOW), lambda i: (vb, prev(i))),
            qspec,
            pl.BlockSpec((N_Q_HEADS, WINDOW), lambda i: (0, cur(i))),
            qspec,
            pl.BlockSpec(memory_space=pl.ANY),
        ],
        out_specs=[pl.BlockSpec((OFF_Z, WINDOW), lambda i: (0, cur(i))), pl.BlockSpec((N_Q_HEADS, 1), lambda i: (0, 0))],
        out_shape=[jax.ShapeDtypeStruct(dproj.shape, BF16), jax.ShapeDtypeStruct((N_Q_HEADS, 1), F32)],
        scratch_shapes=[pltpu.VMEM((KV_DIM, WINDOW), F32), pltpu.VMEM((KV_DIM, WINDOW), F32)],
        aliases={9: 0},
        semantics=("arbitrary",), args=(sinks, proj, proj, proj, proj, proj, out, lse, dout, dproj), send=send)


CONV_ROWS = 256


def _conv_silu_fwd(proj, w_col, b_col):
    t = proj.shape[1]
    r0 = OFF_X // CONV_ROWS

    def body(x_ref, w_ref, b_ref, o_ref):
        y, _ = _causal_conv(x_ref[...], w_ref[...], b_ref[...])
        o_ref[...] = y * _sigmoid(y)

    return pl.pallas_call(
        body,
        name="ssd_conv_fwd",
        grid=(XBC_DIM // CONV_ROWS,),
        in_specs=[
            pl.BlockSpec((CONV_ROWS, t), lambda i: (r0 + i, 0)),
            pl.BlockSpec((CONV_ROWS, SSD_CONV), lambda i: (i, 0)),
            pl.BlockSpec((CONV_ROWS, 1), lambda i: (i, 0)),
        ],
        out_specs=pl.BlockSpec((CONV_ROWS, t), lambda i: (i, 0)),
        out_shape=jax.ShapeDtypeStruct((XBC_DIM, t), F32),
        compiler_params=_cparams("parallel"),
    )(proj, w_col, b_col)


def _conv_silu_bwd(proj, w_col, b_col, dout, row0, dproj, name):
    t = proj.shape[1]
    nrows = dout.shape[0]
    p0 = (OFF_X + row0) // CONV_ROWS
    c0 = row0 // CONV_ROWS

    def body(x_ref, w_ref, b_ref, do_ref, dproj_ref, dx_ref, dwb_ref):
        xv = x_ref[...]
        wv = w_ref[...]
        y, shifted = _causal_conv(xv, wv, b_ref[...])
        sg = _sigmoid(y)
        dy = do_ref[...] * (sg * (1.0 + y * (1.0 - sg)))
        dx, dwb_ref[...] = _causal_conv_bwd(dy, xv, shifted, wv)
        dx_ref[...] = dx.astype(BF16)

    return pl.pallas_call(
        body,
        name=name,
        grid=(nrows // CONV_ROWS,),
        in_specs=[
            pl.BlockSpec((CONV_ROWS, t), lambda i: (p0 + i, 0)),
            pl.BlockSpec((CONV_ROWS, SSD_CONV), lambda i: (c0 + i, 0)),
            pl.BlockSpec((CONV_ROWS, 1), lambda i: (c0 + i, 0)),
            pl.BlockSpec((CONV_ROWS, t), lambda i: (i, 0)),
            pl.BlockSpec(memory_space=pl.ANY),
        ],
        out_specs=[pl.BlockSpec((CONV_ROWS, t), lambda i: (p0 + i, 0)), pl.BlockSpec((CONV_ROWS, 128), lambda i: (i, 0))],
        out_shape=[jax.ShapeDtypeStruct(dproj.shape, BF16), jax.ShapeDtypeStruct((nrows, 128), F32)],
        input_output_aliases={4: 0},
        compiler_params=_cparams("parallel"),
    )(proj, w_col, b_col, dout, dproj)


def _ssd_specs(order):
    xb = D_INNER // BC_DIM
    dtb = OFF_DT // N_SSD_HEADS
    col = pl.BlockSpec((N_SSD_HEADS, 1), lambda c: (0, 0))
    return [
        pl.BlockSpec((D_INNER, CHUNK), lambda c: (0, order(c))),
        pl.BlockSpec((BC_DIM, CHUNK), lambda c: (xb, order(c))),
        pl.BlockSpec((BC_DIM, CHUNK), lambda c: (xb + 1, order(c))),
        pl.BlockSpec((N_SSD_HEADS, CHUNK), lambda c: (dtb, order(c))),
        col, col, col,
    ]


def _ssd_common(dt_ref, dtb_ref, alog_ref):
    z = dt_ref[...] + dtb_ref[...]
    dt = _softplus(z)
    a_neg = -jnp.exp(alog_ref[...])
    d_a = dt * a_neg
    row = lax.broadcasted_iota(jnp.int32, (CHUNK, CHUNK), 0)
    colm = lax.broadcasted_iota(jnp.int32, (CHUNK, CHUNK), 1)
    upper = (row <= colm).astype(F32)
    a_cs = jnp.dot(d_a, upper, precision=HIGHEST, preferred_element_type=F32)
    a_last = _rowsum(d_a)
    return z, dt, a_neg, a_cs, a_last, row >= colm, row == colm


def _decay(a_row, causal):
    a_s = jnp.broadcast_to(a_row, (CHUNK, CHUNK))
    seg = a_s.T - a_s
    return jnp.where(causal, jnp.exp(jnp.where(causal, seg, 0.0)), 0.0)


def _ssd_fwd(xbc, proj, dtb_col, alog_col, dsk_col):
    t = xbc.shape[1]
    nc = t // CHUNK

    def body(xs_ref, b_ref, c_ref, dt_ref, dtb_ref, alog_ref, dsk_ref, y_ref, hst_ref, h_scr):
        @pl.when(pl.program_id(0) == 0)
        def _():
            h_scr[...] = jnp.zeros_like(h_scr)

        _, dt, _, a_cs, a_last, causal, _ = _ssd_common(dt_ref, dtb_ref, alog_ref)
        hst_ref[0] = h_scr[...]
        dsk = dsk_ref[...]
        for g in range(N_SSD_GROUPS):
            grows = slice(g * D_STATE, (g + 1) * D_STATE)
            bb = b_ref[grows, :].astype(BF16)
            cb_ = c_ref[grows, :].astype(BF16)
            cb = _dot_tn(cb_, bb)
            for j in range(g * HEADS_PER_GROUP, (g + 1) * HEADS_PER_GROUP):
                rows = slice(j * SSD_HEAD_DIM, (j + 1) * SSD_HEAD_DIM)
                a = a_cs[j:j + 1, :]
                m = (cb * _decay(a, causal)).astype(BF16)
                xs = xs_ref[rows, :]
                xc = xs * dt[j:j + 1, :]
                hj = h_scr[rows, :]
                y = _dot_nt(xc.astype(BF16), m) + _dot(hj.astype(BF16), cb_) * jnp.exp(a) + dsk[j:j + 1, :] * xs
                y_ref[rows, :] = y
                al = a_last[j:j + 1, :]
                w = jnp.exp(al - a)
                h_scr[rows, :] = jnp.exp(al) * hj + _dot_nt((xc * w).astype(BF16), bb)

    return pl.pallas_call(
        body,
        name="ssd_fwd",
        grid=(nc,),
        in_specs=_ssd_specs(lambda c: c),
        out_specs=[
            pl.BlockSpec((D_INNER, CHUNK), lambda c: (0, c)),
            pl.BlockSpec((1, D_INNER, D_STATE), lambda c: (c, 0, 0)),
        ],
        out_shape=[
            jax.ShapeDtypeStruct((D_INNER, t), F32),
            jax.ShapeDtypeStruct((nc, D_INNER, D_STATE), F32),
        ],
        scratch_shapes=[pltpu.VMEM((D_INNER, D_STATE), F32)],
        compiler_params=_cparams("arbitrary"),
    )(xbc, xbc, xbc, proj, dtb_col, alog_col, dsk_col)


def _ssd_bwd(xbc, proj, dtb_col, alog_col, dsk_col, hst, dy):
    t = xbc.shape[1]
    nc = t // CHUNK
    rev = lambda c: nc - 1 - c

    def body(xs_ref, b_ref, c_ref, dt_ref, dtb_ref, alog_ref, dsk_ref, hst_ref, dy_ref,
             dxs_ref, db_ref, dc_ref, ddt_ref, dalog_ref, ddsk_ref, ddtb_ref, dh_scr, da_scr, ddt_scr, dd_scr):
        @pl.when(pl.program_id(0) == 0)
        def _():
            dh_scr[...] = jnp.zeros_like(dh_scr)
            dalog_ref[...] = jnp.zeros_like(dalog_ref)
            ddsk_ref[...] = jnp.zeros_like(ddsk_ref)
            ddtb_ref[...] = jnp.zeros_like(ddtb_ref)

        z, dt, a_neg, a_cs, a_last, causal, eye = _ssd_common(dt_ref, dtb_ref, alog_ref)
        dsk = dsk_ref[...]
        last_lane = lax.broadcasted_iota(jnp.int32, (1, CHUNK), 1) == CHUNK - 1
        for g in range(N_SSD_GROUPS):
            grows = slice(g * D_STATE, (g + 1) * D_STATE)
            bb = b_ref[grows, :].astype(BF16)
            cb_ = c_ref[grows, :].astype(BF16)
            cb = _dot_tn(cb_, bb)
            dcb = jnp.zeros((CHUNK, CHUNK), F32)
            dc_acc = jnp.zeros((D_STATE, CHUNK), F32)
            db_acc = jnp.zeros((D_STATE, CHUNK), F32)
            for j in range(g * HEADS_PER_GROUP, (g + 1) * HEADS_PER_GROUP):
                rows = slice(j * SSD_HEAD_DIM, (j + 1) * SSD_HEAD_DIM)
                a = a_cs[j:j + 1, :]
                al = a_last[j:j + 1, :]
                lam = _decay(a, causal)
                mf = cb * lam
                xs = xs_ref[rows, :]
                dtj = dt[j:j + 1, :]
                xc = xs * dtj
                w = jnp.exp(al - a)
                e = jnp.exp(a)
                gam = jnp.exp(al)
                hj = hst_ref[0, rows, :]
                hjb = hj.astype(BF16)
                dyv = dy_ref[rows, :]
                dyb = dyv.astype(BF16)
                dd_scr[j:j + 1, :] = _colsum(dyv * xs)
                gb = (dyv * e).astype(BF16)
                dh_in = _dot_nt(gb, cb_)
                dc_acc = dc_acc + _dot_tn(hjb, gb)
                yoff = _dot(hjb, cb_) * e
                da = _colsum(dyv * yoff)
                dm = _dot_tn(dyb, xc.astype(BF16))
                dxc = _dot(dyb, mf.astype(BF16))
                dcb = dcb + dm * lam
                nmat = dm * mf
                rs = jnp.broadcast_to(_rowsum(nmat), (CHUNK, CHUNK))
                da = da + _colsum(jnp.where(eye, rs, 0.0)) - _colsum(nmat)
                ds = dh_scr[rows, :]
                dsb = ds.astype(BF16)
                t1 = _dot(dsb, bb)
                xcw = xc * w
                dxc = dxc + w * t1
                dww = _colsum(xcw * t1)
                da_l = _rowsum(dww) + _rowsum(_colsum(ds * hj)) * gam
                da = da - dww + jnp.where(last_lane, da_l, 0.0)
                db_acc = db_acc + _dot_tn(dsb, xcw.astype(BF16))
                dh_scr[rows, :] = gam * ds + dh_in
                dxs_ref[rows, :] = dsk[j:j + 1, :] * dyv + dxc * dtj
                da_scr[j:j + 1, :] = da
                ddt_scr[j:j + 1, :] = _colsum(dxc * xs)
            dcbb = dcb.astype(BF16)
            dc_ref[grows, :] = dc_acc + _dot_nt(bb, dcbb)
            db_ref[grows, :] = db_acc + _dot(cb_, dcbb)
        dda = jnp.dot(da_scr[...], causal.astype(F32), precision=HIGHEST, preferred_element_type=F32)
        ddt = ddt_scr[...] + dda * a_neg
        ddt_raw = ddt * _sigmoid(z)
        ddt_ref[...] = ddt_raw
        ddtb_ref[...] += _rowsum(ddt_raw)
        dalog_ref[...] += _rowsum(dda * dt) * a_neg
        ddsk_ref[...] += _rowsum(dd_scr[...])

    col = pl.BlockSpec((N_SSD_HEADS, 1), lambda c: (0, 0))
    bc = pl.BlockSpec((BC_DIM, CHUNK), lambda c: (0, rev(c)))
    xs_spec = pl.BlockSpec((D_INNER, CHUNK), lambda c: (0, rev(c)))
    small = pltpu.VMEM((N_SSD_HEADS, CHUNK), F32)
    return pl.pallas_call(
        body,
        name="ssd_bwd",
        grid=(nc,),
        in_specs=_ssd_specs(rev) + [pl.BlockSpec((1, D_INNER, D_STATE), lambda c: (rev(c), 0, 0)), xs_spec],
        out_specs=[xs_spec, bc, bc, pl.BlockSpec((N_SSD_HEADS, CHUNK), lambda c: (0, rev(c))), col, col, col],
        out_shape=[
            jax.ShapeDtypeStruct((D_INNER, t), F32),
            jax.ShapeDtypeStruct((BC_DIM, t), F32),
            jax.ShapeDtypeStruct((BC_DIM, t), F32),
            jax.ShapeDtypeStruct((N_SSD_HEADS, t), F32),
            jax.ShapeDtypeStruct((N_SSD_HEADS, 1), F32),
            jax.ShapeDtypeStruct((N_SSD_HEADS, 1), F32),
            jax.ShapeDtypeStruct((N_SSD_HEADS, 1), F32),
        ],
        scratch_shapes=[pltpu.VMEM((D_INNER, D_STATE), F32), small, small, small],
        compiler_params=_cparams("arbitrary"),
    )(xbc, xbc, xbc, proj, dtb_col, alog_col, dsk_col, hst, dy)


GN_ROWS = D_INNER // N_SSD_GROUPS


def _gnorm_fwd(y, proj, w_col):
    t = y.shape[1]
    tt = _tile(t, (512, 256, 128))
    z0 = OFF_Z // GN_ROWS

    def body(y_ref, z_ref, w_ref, o_ref):
        zv = z_ref[...]
        u = y_ref[...] * (zv * _sigmoid(zv))
        r = lax.rsqrt(jnp.mean(u * u, axis=0, keepdims=True) + EPS)
        o_ref[...] = (u * r * w_ref[...]).astype(BF16)

    blk = pl.BlockSpec((GN_ROWS, tt), lambda g, i: (g, i))
    return pl.pallas_call(
        body,
        name="gnorm_fwd",
        grid=(N_SSD_GROUPS, t // tt),
        in_specs=[blk, pl.BlockSpec((GN_ROWS, tt), lambda g, i: (z0 + g, i)), pl.BlockSpec((GN_ROWS, 1), lambda g, i: (g, 0))],
        out_specs=blk,
        out_shape=jax.ShapeDtypeStruct((D_INNER, t), BF16),
        compiler_params=_cparams("parallel", "parallel"),
    )(y, proj, w_col)


def _gnorm_bwd(dout, y, proj, w_col, send=None):
    t = y.shape[1]
    tt = _tile(t, (512, 256, 128))
    z0 = OFF_Z // GN_ROWS

    def body(do_ref, y_ref, z_ref, w_ref, dy_ref, dz_ref, dw_ref):
        @pl.when(pl.program_id(1) == 0)
        def _():
            dw_ref[...] = jnp.zeros_like(dw_ref)

        zv = z_ref[...]
        yv = y_ref[...]
        sg = _sigmoid(zv)
        sz = zv * sg
        u = yv * sz
        r = lax.rsqrt(jnp.mean(u * u, axis=0, keepdims=True) + EPS)
        xhat = u * r
        dov = do_ref[...]
        dw_ref[...] += _rowsum(dov * xhat)
        dxhat = dov * w_ref[...]
        du = r * (dxhat - xhat * jnp.mean(dxhat * xhat, axis=0, keepdims=True))
        dy_ref[...] = du * sz
        dz_ref[...] = (du * yv * (sg * (1.0 + zv * (1.0 - sg)))).astype(BF16)

    blk = pl.BlockSpec((GN_ROWS, tt), lambda g, i: (g, i))
    col = pl.BlockSpec((GN_ROWS, 1), lambda g, i: (g, 0))
    return _call(
        body,
        name="gnorm_bwd",
        grid=(N_SSD_GROUPS, t // tt),
        in_specs=[blk, blk, pl.BlockSpec((GN_ROWS, tt), lambda g, i: (z0 + g, i)), col],
        out_specs=[blk, pl.BlockSpec((GN_ROWS, tt), lambda g, i: (z0 + g, i)), col],
        out_shape=[jax.ShapeDtypeStruct((D_INNER, t), F32), jax.ShapeDtypeStruct((IN_DIM, t), BF16),
                   jax.ShapeDtypeStruct((D_INNER, 1), F32)],
        semantics=("parallel", "arbitrary"), args=(dout, y, proj, w_col), send=send)


GATE_ROWS = 128


def _gate_specs(t):
    nr = D_MODEL // GATE_ROWS
    blk = pl.BlockSpec((GATE_ROWS, t), lambda r: (r, 0))
    rows_from = lambda first: pl.BlockSpec(
        (pl.Element(GATE_ROWS), pl.Element(t)), lambda r: (pl.multiple_of(first + GATE_ROWS * r, N_SSD_HEADS), 0))
    return blk, [
        rows_from(OFF_GA),
        rows_from(OFF_GS),
        pl.BlockSpec((GATE_ROWS, 1), lambda r: (r, 0)),
        pl.BlockSpec((GATE_ROWS, 1), lambda r: (nr + r, 0)),
        blk, blk,
    ]


def _gate_fwd(proj, b_col, attn, ssd):
    t = proj.shape[1]
    blk, specs = _gate_specs(t)

    def body(ga_ref, gs_ref, ba_ref, bs_ref, a_ref, s_ref, o_ref):
        o_ref[...] = (_sigmoid(ga_ref[...] + ba_ref[...]) * a_ref[...]
                      + _sigmoid(gs_ref[...] + bs_ref[...]) * s_ref[...]).astype(BF16)

    return pl.pallas_call(
        body,
        name="gate_fwd",
        grid=(D_MODEL // GATE_ROWS,),
        in_specs=specs,
        out_specs=blk,
        out_shape=jax.ShapeDtypeStruct((D_MODEL, t), BF16),
        compiler_params=_cparams("parallel"),
    )(proj, proj, b_col, b_col, attn, ssd)


def _gate_bwd(proj, b_col, attn, ssd, dmix, send=None):
    t = proj.shape[1]
    blk, specs = _gate_specs(t)

    def body(ga_ref, gs_ref, ba_ref, bs_ref, a_ref, s_ref, dm_ref, da_ref, dso_ref, dga_ref, dgs_ref, dba_ref, dbs_ref):
        dm = dm_ref[...]
        sa = _sigmoid(ga_ref[...] + ba_ref[...])
        ss = _sigmoid(gs_ref[...] + bs_ref[...])
        da_ref[...] = (dm * sa).astype(BF16)
        dso_ref[...] = (dm * ss).astype(BF16)
        dga = dm * a_ref[...] * sa * (1.0 - sa)
        dgs = dm * s_ref[...] * ss * (1.0 - ss)
        dga_ref[...] = dga.astype(BF16)
        dgs_ref[...] = dgs.astype(BF16)
        dba_ref[...] = _rowsum(dga)
        dbs_ref[...] = _rowsum(dgs)

    col = pl.BlockSpec((GATE_ROWS, 1), lambda r: (r, 0))
    act = jax.ShapeDtypeStruct((D_MODEL, t), BF16)
    bias = jax.ShapeDtypeStruct((D_MODEL, 1), F32)
    return _call(
        body,
        name="gate_bwd",
        grid=(D_MODEL // GATE_ROWS,),
        in_specs=specs + [blk],
        out_specs=[blk, blk, blk, blk, col, col],
        out_shape=[act, act, act, act, bias, bias],
        semantics=("parallel",), args=(proj, proj, b_col, b_col, attn, ssd, dmix), send=send)


FFN_ROWS = 256


def _ffn_fwd(u0, w_col, b_col):
    t = u0.shape[2]

    def body(u_ref, w_ref, b_ref, o_ref):
        val, _ = _causal_conv(u_ref[0], w_ref[0], b_ref[0])
        gt, _ = _causal_conv(u_ref[1], w_ref[1], b_ref[1])
        o_ref[...] = (gt * _sigmoid(gt) * val).astype(BF16)

    return pl.pallas_call(
        body,
        name="ffn_fwd",
        grid=(D_FF // FFN_ROWS,),
        in_specs=[
            pl.BlockSpec((2, FFN_ROWS, t), lambda i: (0, i, 0)),
            pl.BlockSpec((2, FFN_ROWS, FFN_CONV), lambda i: (0, i, 0)),
            pl.BlockSpec((2, FFN_ROWS, 1), lambda i: (0, i, 0)),
        ],
        out_specs=pl.BlockSpec((FFN_ROWS, t), lambda i: (i, 0)),
        out_shape=jax.ShapeDtypeStruct((D_FF, t), BF16),
        compiler_params=_cparams("parallel"),
    )(u0, w_col, b_col)


def _ffn_bwd(u0, w_col, b_col, dg, send=None):
    t = u0.shape[2]

    def body(u_ref, w_ref, b_ref, dg_ref, du_ref, dwb_ref):
        xval, wval = u_ref[0], w_ref[0]
        xgt, wgt = u_ref[1], w_ref[1]
        val, sh_val = _causal_conv(xval, wval, b_ref[0])
        gt, sh_gt = _causal_conv(xgt, wgt, b_ref[1])
        sg = _sigmoid(gt)
        dgv = dg_ref[...]
        dval = dgv * (gt * sg)
        dgt = dgv * val * (sg * (1.0 + gt * (1.0 - sg)))
        dx, dwb_ref[0] = _causal_conv_bwd(dval, xval, sh_val, wval)
        du_ref[0] = dx.astype(BF16)
        dx, dwb_ref[1] = _causal_conv_bwd(dgt, xgt, sh_gt, wgt)
        du_ref[1] = dx.astype(BF16)

    return _call(
        body,
        name="ffn_bwd",
        grid=(D_FF // FFN_ROWS,),
        in_specs=[
            pl.BlockSpec((2, FFN_ROWS, t), lambda i: (0, i, 0)),
            pl.BlockSpec((2, FFN_ROWS, FFN_CONV), lambda i: (0, i, 0)),
            pl.BlockSpec((2, FFN_ROWS, 1), lambda i: (0, i, 0)),
            pl.BlockSpec((FFN_ROWS, t), lambda i: (i, 0)),
        ],
        out_specs=[pl.BlockSpec((2, FFN_ROWS, t), lambda i: (0, i, 0)), pl.BlockSpec((2, FFN_ROWS, 128), lambda i: (0, i, 0))],
        out_shape=[jax.ShapeDtypeStruct((2, D_FF, t), BF16), jax.ShapeDtypeStruct((2, D_FF, 128), F32)],
        semantics=("parallel",), args=(u0, w_col, b_col, dg), send=send)


def _adamw_math(w, g, m, v):
    m = ADAM_B1 * m + (1.0 - ADAM_B1) * g
    v = ADAM_B2 * v + (1.0 - ADAM_B2) * (g * g)
    m_hat = m / (1.0 - ADAM_B1 ** ADAM_STEP)
    v_hat = v / (1.0 - ADAM_B2 ** ADAM_STEP)
    delta = -ADAM_LR * (m_hat / (jnp.sqrt(v_hat) + ADAM_EPS) + ADAM_WD * w)
    return delta, m, v


def _adamw_sharded(parts, w, m, v, name):
    r, c = w.shape[0], w.shape[-1]
    slots = parts.shape[0]
    per_lane = 2 * r * (slots * parts.dtype.itemsize + 7 * w.dtype.itemsize)
    tc = max(d for d in range(LANES, c + 1, LANES) if c % d == 0 and (d * per_lane <= BLOCK_VMEM_BUDGET or d == LANES))
    blk_shape = (r, tc) if w.ndim == 2 else (r, 1, tc)

    def body(p_ref, w_ref, m_ref, v_ref, g_ref, d_ref, nm_ref, nv_ref):
        g = p_ref[0].astype(F32)
        for s in range(1, slots):
            g = g + p_ref[s].astype(F32)
        flat = lambda ref: ref[...].reshape(r, tc)
        d, nm, nv = _adamw_math(flat(w_ref), g, flat(m_ref), flat(v_ref))
        for ref, val in ((g_ref, g), (d_ref, d), (nm_ref, nm), (nv_ref, nv)):
            ref[...] = val.reshape(blk_shape)

    blk = pl.BlockSpec(blk_shape, (lambda i: (0, i)) if w.ndim == 2 else (lambda i: (0, 0, i)))
    out = jax.ShapeDtypeStruct(w.shape, F32)
    return pl.pallas_call(
        body,
        name=name,
        grid=(c // tc,),
        in_specs=[pl.BlockSpec((slots, r, tc), lambda i: (0, 0, i)), blk, blk, blk],
        out_specs=[blk, blk, blk, blk],
        out_shape=[out, out, out, out],
        compiler_params=_cparams("parallel"),
    )(parts, w, m, v)


def _lane_offsets(sizes):
    offsets, pos = [], 0
    for n in sizes:
        offsets.append(pos)
        pos += -(-n // 128) * 128
    return offsets, pos


def _pack_row(parts):
    rows = [p.reshape(1, -1).astype(F32) for p in parts]
    return jnp.concatenate([jnp.pad(r, ((0, 0), (0, -r.shape[1] % 128))) for r in rows], axis=1)


def _small_update(parts, me, full_sizes, ws, ms, vs):
    n = len(ws)
    offsets, _ = _lane_offsets([1] + list(full_sizes))

    def body(me_ref, p_ref, *refs):
        w_refs, m_refs, v_refs = refs[:n], refs[n:2 * n], refs[2 * n:3 * n]
        scalar_ref, out_refs = refs[3 * n], refs[3 * n + 1:]
        tot = p_ref[0]
        for s in range(1, N_DEV):
            tot = tot + p_ref[s]
        scalar_ref[...] = tot[:, 0:1]
        for k in range(n):
            g_ref, d_ref, nm_ref, nv_ref = out_refs[4 * k:4 * k + 4]
            taps, cols = w_refs[k].shape
            if taps == 1:
                g_ref[...] = tot[:, offsets[k + 1]:offsets[k + 1] + cols]
            else:
                full = full_sizes[k] // taps
                for tap in range(taps):
                    mine = jnp.zeros((1, cols), F32)
                    for d in range(N_DEV):
                        lo = offsets[k + 1] + tap * full + d * cols
                        mine = jnp.where(me_ref[0] == d, tot[:, lo:lo + cols], mine)
                    g_ref[tap:tap + 1, :] = mine
            d_ref[...], nm_ref[...], nv_ref[...] = _adamw_math(w_refs[k][...], g_ref[...], m_refs[k][...], v_refs[k][...])

    vmem = pl.BlockSpec(memory_space=pltpu.VMEM)
    out_shape = [jax.ShapeDtypeStruct((1, 1), F32)]
    for wk in ws:
        out_shape += [jax.ShapeDtypeStruct(wk.shape, F32)] * 4
    res = pl.pallas_call(
        body,
        name="small_update",
        in_specs=[pl.BlockSpec(memory_space=pltpu.SMEM)] + [vmem] * (1 + 3 * n),
        out_specs=[vmem] * len(out_shape),
        out_shape=out_shape,
    )(me, parts, *ws, *ms, *vs)
    return res[0], [res[1 + 4 * k:5 + 4 * k] for k in range(n)]


ANY = pl.BlockSpec(memory_space=pl.ANY)
FLIPS = [(k >> 2 & 1, k >> 1 & 1, k & 1) for k in range(1, N_DEV)]


def _place():
    return lax.axis_index("x"), lax.axis_index("y"), lax.axis_index("c")


HBM = pl.BlockSpec(memory_space=pltpu.HBM)
SEM = pl.BlockSpec(memory_space=pltpu.SEMAPHORE)
EFFECT = pltpu.SideEffectType.DATAFLOW_SIDE_EFFECTING


def _peer_copy(gather, src_ref, land_ref, send_sems, recv_sems, k, sending):
    x, y, c = _place()
    fx, fy, fc = FLIPS[k]
    me = 4 * x + 2 * y + c
    peer = 4 * (x ^ fx) + 2 * (y ^ fy) + (c ^ fc)
    return pltpu.make_async_remote_copy(
        src_ref=src_ref if gather else src_ref.at[peer],
        dst_ref=land_ref.at[me if sending else peer],
        send_sem=send_sems.at[k], recv_sem=recv_sems.at[k],
        device_id=(x ^ fx, y ^ fy, c ^ fc), device_id_type=MESH)


SIBLING = 0
OTHER_CHIPS = (1, 3, 5)


def _gather_start(srcs, name, via_sibling):
    n = len(srcs)
    lands = [lax.empty((N_DEV,) + s.shape, s.dtype) for s in srcs]

    def body(*refs):
        src_refs, land_refs = refs[:n], refs[n:2 * n]
        send, recv = refs[2 * n:3 * n], refs[3 * n:4 * n]
        for i in range(n):
            for k in (SIBLING,) + OTHER_CHIPS if via_sibling else range(N_DEV - 1):
                _peer_copy(True, src_refs[i], land_refs[i], send[i], recv[i], k, True).start()

    sem = pltpu.SemaphoreType.DMA((N_DEV - 1,))
    hbm = lambda a: pltpu.HBM(a.shape, a.dtype)
    res = pl.pallas_call(
        body,
        name=name,
        in_specs=[HBM] * (2 * n),
        out_specs=[SEM] * (2 * n) + [HBM] * (2 * n),
        out_shape=[sem] * (2 * n) + [hbm(s) for s in srcs] + [hbm(a) for a in lands],
        input_output_aliases={i: 2 * n + i for i in range(2 * n)},
        compiler_params=pltpu.CompilerParams(has_side_effects=EFFECT),
    )(*[pltpu.with_memory_space_constraint(a, pltpu.HBM) for a in list(srcs) + lands])
    return res[:n], res[n:2 * n], res[2 * n:3 * n], res[3 * n:4 * n]


def _exchange_wait(send_sems, recv_sems, src, land, after, gather, name):
    def body(src_ref, land_ref, send_ref, recv_ref, after_ref, src_out, land_out):
        for k in range(N_DEV - 1):
            cp = _peer_copy(gather, src_ref, land_ref, send_ref, recv_ref, k, False)
            cp.wait_send()
            cp.wait_recv()

    hbm = lambda a: pltpu.HBM(a.shape, a.dtype)
    return pl.pallas_call(
        body,
        name=name,
        in_specs=[HBM, HBM, SEM, SEM, ANY],
        out_specs=[HBM, HBM],
        out_shape=[hbm(src), hbm(land)],
        input_output_aliases={0: 0, 1: 1},
        compiler_params=pltpu.CompilerParams(has_side_effects=EFFECT),
    )(src, land, send_sems, recv_sems, after)


def _own_slot(src, land, me, gather):
    own = src[None] if gather else lax.dynamic_slice_in_dim(src, me, 1, axis=0)
    return lax.dynamic_update_slice_in_dim(land, own, me, axis=0)


def _forwarded_copy(land_ref, send_sems, recv_sems, j, sending):
    x, y, c = _place()
    fx, fy, _ = FLIPS[OTHER_CHIPS[j]]
    slot = 4 * (x ^ fx) + 2 * (y ^ fy) + (c if sending else 1 - c)
    return pltpu.make_async_remote_copy(
        src_ref=land_ref.at[slot], dst_ref=land_ref.at[slot], send_sem=send_sems.at[j], recv_sem=recv_sems.at[j],
        device_id=(x, y, 1 - c), device_id_type=MESH)


def _gather_forward(send_sems, recv_sems, srcs, lands, after, name):
    n = len(srcs)

    def body(*refs):
        src_refs, land_refs = refs[:n], refs[n:2 * n]
        send, recv = refs[2 * n:3 * n], refs[3 * n:4 * n]
        fwd_send, fwd_recv = refs[4 * n + 1:5 * n + 1], refs[5 * n + 1:6 * n + 1]
        for i in range(n):
            for j, k in enumerate(OTHER_CHIPS):
                _peer_copy(True, src_refs[i], land_refs[i], send[i], recv[i], k, False).wait_recv()
                _forwarded_copy(land_refs[i], fwd_send[i], fwd_recv[i], j, True).start()

    sem = pltpu.SemaphoreType.DMA((len(OTHER_CHIPS),))
    hbm = lambda a: pltpu.HBM(a.shape, a.dtype)
    res = pl.pallas_call(
        body,
        name=name,
        in_specs=[HBM] * (2 * n) + [SEM] * (2 * n) + [ANY],
        out_specs=[SEM] * (2 * n) + [HBM] * (2 * n),
        out_shape=[sem] * (2 * n) + [hbm(a) for a in srcs] + [hbm(a) for a in lands],
        input_output_aliases={i: 2 * n + i for i in range(2 * n)},
        compiler_params=pltpu.CompilerParams(has_side_effects=EFFECT),
    )(*srcs, *lands, *send_sems, *recv_sems, after)
    return res[:n], res[n:2 * n], res[2 * n:3 * n], res[3 * n:4 * n]


def _gather_wait_forwarded(send_sems, recv_sems, fwd_send, fwd_recv, src, land, after, name):
    def body(src_ref, land_ref, send_ref, recv_ref, fwd_send_ref, fwd_recv_ref, after_ref, src_out, land_out):
        for k in (SIBLING,) + OTHER_CHIPS:
            _peer_copy(True, src_ref, land_ref, send_ref, recv_ref, k, False).wait_send()
        _peer_copy(True, src_ref, land_ref, send_ref, recv_ref, SIBLING, False).wait_recv()
        for j in range(len(OTHER_CHIPS)):
            _forwarded_copy(land_ref, fwd_send_ref, fwd_recv_ref, j, True).wait_send()
            _forwarded_copy(land_ref, fwd_send_ref, fwd_recv_ref, j, False).wait_recv()

    hbm = lambda a: pltpu.HBM(a.shape, a.dtype)
    return pl.pallas_call(
        body,
        name=name,
        in_specs=[HBM, HBM, SEM, SEM, SEM, SEM, ANY],
        out_specs=[HBM, HBM],
        out_shape=[hbm(src), hbm(land)],
        input_output_aliases={0: 0, 1: 1},
        compiler_params=pltpu.CompilerParams(has_side_effects=EFFECT),
    )(src, land, send_sems, recv_sems, fwd_send, fwd_recv, after)


N_CHIPS = N_DEV // 2


def _pair_exchange(by_core, meanwhile, name):
    def copy(src_ref, land_ref, send_sems, recv_sems, q):
        x, y, c = _place()
        return pltpu.make_async_remote_copy(
            src_ref=src_ref.at[q, 1 - c], dst_ref=land_ref.at[q], send_sem=send_sems.at[q], recv_sem=recv_sems.at[q],
            device_id=(x, y, 1 - c), device_id_type=MESH)

    def start(src_ref, land_ref, send_sems, recv_sems, src_out, land_out):
        for q in range(N_CHIPS):
            copy(src_ref, land_ref, send_sems, recv_sems, q).start()

    def wait(src_ref, land_ref, send_sems, recv_sems, after_ref, src_out, land_out):
        for q in range(N_CHIPS):
            cp = copy(src_ref, land_ref, send_sems, recv_sems, q)
            cp.wait_send()
            cp.wait_recv()

    sem = pltpu.SemaphoreType.DMA((N_CHIPS,))
    hbm_src = pltpu.HBM(by_core.shape, by_core.dtype)
    hbm_land = pltpu.HBM(by_core.shape[:1] + by_core.shape[2:], by_core.dtype)
    params = pltpu.CompilerParams(has_side_effects=EFFECT)
    send_sems, recv_sems, src, land = pl.pallas_call(
        start, name=name + "_start", in_specs=[HBM, HBM], out_specs=[SEM, SEM, HBM, HBM],
        out_shape=[sem, sem, hbm_src, hbm_land], input_output_aliases={0: 2, 1: 3}, compiler_params=params,
    )(pltpu.with_memory_space_constraint(by_core, pltpu.HBM),
      pltpu.with_memory_space_constraint(lax.empty(hbm_land.shape, by_core.dtype), pltpu.HBM))
    return pl.pallas_call(
        wait, name=name + "_wait", in_specs=[HBM, HBM, SEM, SEM, ANY], out_specs=[HBM, HBM],
        out_shape=[hbm_src, hbm_land], input_output_aliases={0: 0, 1: 1}, compiler_params=params,
    )(src, land, send_sems, recv_sems, meanwhile(src))


def _pair_add(by_core, landed, name):
    q, _, r, c = by_core.shape
    tc = _tile(c, (512, 256, 128))

    def body(a_ref, b_ref, o_ref):
        mine = a_ref[0, lax.axis_index("c")]
        o_ref[0] = (mine.astype(F32) + b_ref[0].astype(F32)).astype(BF16)

    blk = pl.BlockSpec((1, r, tc), lambda i, j: (i, 0, j))
    return pl.pallas_call(
        body, name=name, grid=(q, c // tc),
        in_specs=[pl.BlockSpec((1, 2, r, tc), lambda i, j: (i, 0, 0, j)), blk], out_specs=blk,
        out_shape=jax.ShapeDtypeStruct(landed.shape, BF16), compiler_params=_cparams("parallel", "parallel"),
    )(by_core, landed)


def _chip_copy(src_ref, land_ref, send_sems, recv_sems, j, sending):
    x, y, c = _place()
    fx, fy, _ = FLIPS[OTHER_CHIPS[j]]
    here, there = 2 * x + y, 2 * (x ^ fx) + (y ^ fy)
    return pltpu.make_async_remote_copy(
        src_ref=src_ref.at[there], dst_ref=land_ref.at[here if sending else there],
        send_sem=send_sems.at[j], recv_sem=recv_sems.at[j],
        device_id=(x ^ fx, y ^ fy, c), device_id_type=MESH)


def _chip_wait(send_sems, recv_sems, src, land, after, name):
    def body(src_ref, land_ref, send_ref, recv_ref, after_ref, src_out, land_out):
        for j in range(len(OTHER_CHIPS)):
            cp = _chip_copy(src_ref, land_ref, send_ref, recv_ref, j, False)
            cp.wait_send()
            cp.wait_recv()

    hbm = lambda a: pltpu.HBM(a.shape, a.dtype)
    return pl.pallas_call(
        body,
        name=name,
        in_specs=[HBM, HBM, SEM, SEM, ANY],
        out_specs=[HBM, HBM],
        out_shape=[hbm(src), hbm(land)],
        input_output_aliases={0: 0, 1: 1},
        compiler_params=pltpu.CompilerParams(has_side_effects=EFFECT),
    )(src, land, send_sems, recv_sems, after)


def _col(v):
    return v.reshape(-1, 1).astype(F32)


def _local_step(x, tgt, started, weight, small, pair_sums, handles):
    t = x.shape[0]
    n1 = _col(small["norm1_w"])
    n2 = _col(small["norm2_w"])
    nf = _col(small["final_norm_w"])
    bg = _col(small["b_gate"])
    sinks = small["attn_sinks"].reshape(-1).astype(F32)
    cbias = _col(small["ssd_conv_b"])
    dtb = _col(small["dt_bias"])
    alog = _col(small["a_log"])
    dsk = _col(small["d_skip"])
    gnw = _col(small["ssd_norm_w"])
    fb = small["ffn_conv_b"].reshape(2, D_FF, 1)

    xt, xn = _norm_fwd_tokens(x, n1, started, "norm1_fwd")
    cw = weight("ssd_conv_w", xn).T
    fw = weight("ffn_conv_w", xn).T.reshape(2, D_FF, FFN_CONV)
    w_in_t = weight("w_in", xn)
    proj = _matmul(w_in_t, xn, nt=False, out_dtype=F32, name="mm_in")
    ao, lse = _attn_fwd(proj, sinks)
    w_ao = weight("w_attn_o", ao)
    attn = _matmul(w_ao, ao, nt=False, out_dtype=F32, name="mm_attn_o", tn_a=True)
    xbc = _conv_silu_fwd(proj, cw, cbias)
    y, hst = _ssd_fwd(xbc, proj, dtb, alog, dsk)
    yn = _gnorm_fwd(y, proj, gnw)
    w_so = weight("w_ssd_o", yn)
    ssd = _matmul(w_so, yn, nt=False, out_dtype=F32, name="mm_ssd_o", tn_a=True)
    mix = _gate_fwd(proj, bg, attn, ssd)
    w_out = weight("w_out", mix)
    h1 = _matmul(w_out, mix, nt=False, out_dtype=F32, name="mm_out", add=xt, tn_a=True)
    hn = _norm_fwd(h1, n2, "norm2_fwd")
    w_up_t = weight("w_up", hn)
    u0 = _matmul(w_up_t, hn, nt=False, out_dtype=F32, name="mm_up").reshape(2, D_FF, t)
    gl = _ffn_fwd(u0, fw, fb)
    w_down = weight("w_down", gl)
    h2 = _matmul(w_down, gl, nt=False, out_dtype=F32, name="mm_down", add=h1, tn_a=True)
    dh2, loss, d_nf = _final_norm_loss(h2, tgt, nf)

    g = {}

    def sending(weight_name, grad, fn, *args, **kwargs):
        chunks = grad if grad.ndim == 3 else grad.reshape(N_DEV, -1, D_MODEL)
        out, handles[weight_name] = fn(*args, send=chunks, **kwargs)
        return out

    g_down = _matmul(gl, dh2, nt=True, out_dtype=BF16, name="mm_d_w_down")
    dgl = _matmul(w_down, dh2, nt=False, out_dtype=F32, name="mm_d_glu")
    du0, d_fwb = sending("w_down", g_down, _ffn_bwd, u0, fw, fb, dgl)
    du0 = du0.reshape(2 * D_FF, t)
    g_up = _matmul(du0, hn, nt=True, out_dtype=BF16, name="mm_d_w_up")
    dhn = sending("w_up", g_up, _matmul, w_up_t, du0, nt=False, out_dtype=F32, name="mm_d_hn", tn_a=True)
    dh1, d_n2 = _norm_bwd(dhn, h1, n2, dh2, "norm2_bwd")
    g_out = _matmul(mix, dh1, nt=True, out_dtype=BF16, name="mm_d_w_out")
    dmix = _matmul(w_out, dh1, nt=False, out_dtype=F32, name="mm_d_mix")
    d_attn, d_ssd, d_ga, d_gs, d_ba, d_bs = sending("w_out", g_out, _gate_bwd, proj, bg, attn, ssd, dmix)
    g_ao = _matmul(ao, d_attn, nt=True, out_dtype=BF16, name="mm_d_w_attn_o")
    dao = _matmul(w_ao, d_attn, nt=False, out_dtype=F32, name="mm_d_ao")
    g_so = _matmul(yn, d_ssd, nt=True, out_dtype=BF16, name="mm_d_w_ssd_o")
    dyn = _matmul(w_so, d_ssd, nt=False, out_dtype=F32, name="mm_d_yn")
    dy, dproj, d_gnw = sending("w_ssd_o", g_so, _gnorm_bwd, dyn, y, proj, gnw)
    dxs, dbm, dcm, ddt, d_alog, d_dsk, d_dtb = _ssd_bwd(xbc, proj, dtb, alog, dsk, hst, dy)
    dproj, dwb_xs = _conv_silu_bwd(proj, cw, cbias, dxs, 0, dproj, "ssd_conv_bwd_x")
    dproj, dwb_b = _conv_silu_bwd(proj, cw, cbias, dbm, D_INNER, dproj, "ssd_conv_bwd_b")
    dproj, dwb_c = _conv_silu_bwd(proj, cw, cbias, dcm, D_INNER + BC_DIM, dproj, "ssd_conv_bwd_c")
    dwb_conv = jnp.concatenate([dwb_xs, dwb_b, dwb_c], axis=0)
    dproj, d_sinks = sending("w_attn_o", g_ao, _attn_bwd, proj, sinks, ao, lse, dao, dproj)
    for rows, part in ((OFF_DT, ddt.astype(BF16)), (OFF_GA, d_ga), (OFF_GS, d_gs)):
        dproj = lax.dynamic_update_slice(dproj, part, (rows, 0))
    g_in = pair_sums(_matmul(dproj, xn, nt=True, out_dtype=BF16, name="mm_d_w_in"))
    dxn = sending("w_in", g_in, _matmul, w_in_t, dproj, nt=False, out_dtype=F32, name="mm_d_xn", tn_a=True)
    dx, d_n1 = _norm_bwd(dxn, xt, n1, dh1, "norm1_bwd", tokens_out=True)

    g["norm1_w"] = d_n1
    g["b_gate"] = jnp.concatenate([d_ba, d_bs], axis=0)
    g["attn_sinks"] = d_sinks
    g["ssd_conv_w"] = dwb_conv[:, :SSD_CONV].T
    g["ssd_conv_b"] = dwb_conv[:, SSD_CONV]
    g["dt_bias"] = d_dtb
    g["a_log"] = d_alog
    g["d_skip"] = d_dsk
    g["ssd_norm_w"] = d_gnw
    g["norm2_w"] = d_n2
    d_fwb = d_fwb.reshape(2 * D_FF, 128)
    g["ffn_conv_w"] = d_fwb[:, :FFN_CONV].T
    g["ffn_conv_b"] = d_fwb[:, FFN_CONV]
    g["final_norm_w"] = d_nf
    return loss, dx, g


SMALL = ("norm1_w", "b_gate", "attn_sinks", "ssd_conv_w", "ssd_conv_b", "dt_bias", "a_log", "d_skip", "ssd_norm_w",
         "norm2_w", "ffn_conv_w", "ffn_conv_b", "final_norm_w")
WEIGHT_ORDER = ("norm1_w", "w_in", "b_gate", "attn_sinks", "w_attn_o", "ssd_conv_w", "ssd_conv_b", "dt_bias", "a_log",
                "d_skip", "ssd_norm_w", "w_ssd_o", "w_out", "norm2_w", "w_up", "ffn_conv_w", "ffn_conv_b", "w_down",
                "final_norm_w")


def kernel(x, norm1_w, w_in, b_gate, attn_sinks, w_attn_o, ssd_conv_w, ssd_conv_b, dt_bias, a_log, d_skip, ssd_norm_w, w_ssd_o, w_out, norm2_w, w_up, ffn_conv_w, ffn_conv_b, w_down, final_norm_w, loss_target, m_norm1_w, m_w_in, m_b_gate, m_attn_sinks, m_w_attn_o, m_ssd_conv_w, m_ssd_conv_b, m_dt_bias, m_a_log, m_d_skip, m_ssd_norm_w, m_w_ssd_o, m_w_out, m_norm2_w, m_w_up, m_ffn_conv_w, m_ffn_conv_b, m_w_down, m_final_norm_w, v_norm1_w, v_w_in, v_b_gate, v_attn_sinks, v_w_attn_o, v_ssd_conv_w, v_ssd_conv_b, v_dt_bias, v_a_log, v_d_skip, v_ssd_norm_w, v_w_ssd_o, v_w_out, v_norm2_w, v_w_up, v_ffn_conv_w, v_ffn_conv_b, v_w_down, v_final_norm_w):
    w = dict(norm1_w=norm1_w, w_in=w_in, b_gate=b_gate, attn_sinks=attn_sinks, w_attn_o=w_attn_o, ssd_conv_w=ssd_conv_w, ssd_conv_b=ssd_conv_b, dt_bias=dt_bias, a_log=a_log, d_skip=d_skip, ssd_norm_w=ssd_norm_w, w_ssd_o=w_ssd_o, w_out=w_out, norm2_w=norm2_w, w_up=w_up, ffn_conv_w=ffn_conv_w, ffn_conv_b=ffn_conv_b, w_down=w_down, final_norm_w=final_norm_w)
    m = dict(norm1_w=m_norm1_w, w_in=m_w_in, b_gate=m_b_gate, attn_sinks=m_attn_sinks, w_attn_o=m_w_attn_o, ssd_conv_w=m_ssd_conv_w, ssd_conv_b=m_ssd_conv_b, dt_bias=m_dt_bias, a_log=m_a_log, d_skip=m_d_skip, ssd_norm_w=m_ssd_norm_w, w_ssd_o=m_w_ssd_o, w_out=m_w_out, norm2_w=m_norm2_w, w_up=m_w_up, ffn_conv_w=m_ffn_conv_w, ffn_conv_b=m_ffn_conv_b, w_down=m_w_down, final_norm_w=m_final_norm_w)
    v = dict(norm1_w=v_norm1_w, w_in=v_w_in, b_gate=v_b_gate, attn_sinks=v_attn_sinks, w_attn_o=v_w_attn_o, ssd_conv_w=v_ssd_conv_w, ssd_conv_b=v_ssd_conv_b, dt_bias=v_dt_bias, a_log=v_a_log, d_skip=v_d_skip, ssd_norm_w=v_ssd_norm_w, w_ssd_o=v_w_ssd_o, w_out=v_w_out, norm2_w=v_norm2_w, w_up=v_w_up, ffn_conv_w=v_ffn_conv_w, ffn_conv_b=v_ffn_conv_b, w_down=v_w_down, final_norm_w=v_final_norm_w)
    me = 4 * lax.axis_index("x") + 2 * lax.axis_index("y") + lax.axis_index("c")

    shards = {"ssd_conv_w": ssd_conv_w[0], "ffn_conv_w": ffn_conv_w[0], "w_in": w_in[0].T.astype(BF16),
              "w_attn_o": w_attn_o[0].astype(BF16), "w_ssd_o": w_ssd_o[0].astype(BF16), "w_out": w_out[0].astype(BF16),
              "w_up": w_up[0].T.astype(BF16), "w_down": w_down[0].astype(BF16)}
    order = list(shards)
    g_send, g_recv, g_src, g_land = _gather_start(list(shards.values()), "gather_start", True)
    first = ("ssd_conv_w", "ffn_conv_w", "w_in")
    forwarded = {}

    def weight(name, after):
        if name not in forwarded:
            group = [k for k in order if (k in first) == (name in first)]
            idx = [order.index(k) for k in group]
            handles = _gather_forward([g_send[i] for i in idx], [g_recv[i] for i in idx], [g_src[i] for i in idx],
                                      [g_land[i] for i in idx], after, "gather_forward_for_" + name)
            forwarded.update(zip(group, zip(*handles)))
        i = order.index(name)
        src, land = _gather_wait_forwarded(g_send[i], g_recv[i], *forwarded[name], after, "gather_wait_" + name)
        land = _own_slot(src, land, me, True)
        if name == "ssd_conv_w":
            return jnp.transpose(land, (1, 0, 2)).reshape(SSD_CONV, XBC_DIM)
        if name == "ffn_conv_w":
            return jnp.transpose(land, (1, 0, 2)).reshape(FFN_CONV, 2 * D_FF)
        return land.reshape(-1, D_MODEL)

    res, pending = {}, {}

    def update(name, after):
        if name == "w_in":
            parts = _own_slot(*_chip_wait(*pending[name], after, "grad_wait_" + name), me // 2, False)
        else:
            parts = _own_slot(*_exchange_wait(*pending[name], after, False, "grad_wait_" + name), me, False)
        view, back = {
            "w_in": (lambda a: jnp.transpose(a, (2, 0, 1)), lambda r: jnp.transpose(r, (1, 2, 0))),
            "w_up": (lambda a: a[0].T, lambda r: r.T[None]),
        }.get(name, (lambda a: a[0], lambda r: r[None]))
        done = _adamw_sharded(parts, view(w[name]), view(m[name]), view(v[name]), "adamw_" + name)
        res[name] = [back(r) for r in done]
        return done[0]

    def pair_sums(grad):
        by_core, landed = _pair_exchange(grad.reshape(N_CHIPS, 2, -1, D_MODEL),
                                         lambda started: update("w_up", update("w_down", started)), "grad_pair_w_in")
        return _pair_add(by_core, landed, "grad_pair_add_w_in")

    small = {k: w[k][0] if k != "final_norm_w" else w[k] for k in SMALL}
    loss, dx, g = _local_step(x[0], loss_target[0], g_src[0], weight, small, pair_sums, pending)

    packed = _pack_row([loss] + [g[k] for k in SMALL])
    s_send, s_recv, s_src, s_land = _gather_start([packed], "small_grads_start", False)
    after = s_src[0]
    for name in ("w_out", "w_attn_o", "w_ssd_o", "w_in"):
        after = update(name, after)

    rows = _own_slot(*_exchange_wait(s_send[0], s_recv[0], s_src[0], s_land[0], after, True, "small_grads_wait"),
                     me, True)
    flat = lambda a: a.reshape(-1, a.shape[-1])
    loss_sum, updates = _small_update(
        rows, me.reshape(1), [g[k].size for k in SMALL],
        [flat(w[k]) for k in SMALL], [flat(m[k]) for k in SMALL], [flat(v[k]) for k in SMALL])
    for k, upd in zip(SMALL, updates):
        res[k] = [u.reshape(w[k].shape) for u in upd]

    grad_x = dx[None]
    outs = [loss_sum.reshape(()), grad_x]
    for i in range(4):
        outs.extend(res[k][i] for k in WEIGHT_ORDER)
    return tuple(outs)
```

```python
import jax
import jax.numpy as jnp
from jax import lax
from jax.experimental import pallas as pl
from jax.experimental.pallas import tpu as pltpu

F32 = jnp.float32
BF16 = jnp.bfloat16
HIGHEST = lax.Precision.HIGHEST

D_MODEL = 1024
N_Q_HEADS = 16
N_KV_HEADS = 4
HEAD_DIM = 64
WINDOW = 128
Q_PER_KV = N_Q_HEADS // N_KV_HEADS
Q_DIM = N_Q_HEADS * HEAD_DIM
KV_DIM = N_KV_HEADS * HEAD_DIM
D_INNER = 2048
SSD_HEAD_DIM = 64
N_SSD_HEADS = 32
N_SSD_GROUPS = 4
HEADS_PER_GROUP = N_SSD_HEADS // N_SSD_GROUPS
D_STATE = 128
BC_DIM = N_SSD_GROUPS * D_STATE
XBC_DIM = D_INNER + 2 * BC_DIM
SSD_CONV = 4
CHUNK = 128
D_FF = 2816
FFN_CONV = 3
EPS = 1e-5
NEG = -1e30
IN_DIM = 8736
N_DEV = 8

OFF_Q = 0
OFF_K = OFF_Q + Q_DIM
OFF_V = OFF_K + KV_DIM
OFF_Z = OFF_V + KV_DIM
OFF_X = OFF_Z + D_INNER
OFF_DT = OFF_X + XBC_DIM
OFF_GA = OFF_DT + N_SSD_HEADS
OFF_GS = OFF_GA + D_MODEL

ADAM_LR = 0.001
ADAM_B1 = 0.9
ADAM_B2 = 0.999
ADAM_EPS = 1e-08
ADAM_WD = 0.01
ADAM_STEP = 10

LANES = 128
BF16_TILE_ROWS = 16
VMEM_BYTES = 64 * 1024 * 1024
VMEM_LIMIT = VMEM_BYTES * 3 // 4
MESH = pl.DeviceIdType.MESH


def _cparams(*sem):
    return pltpu.CompilerParams(dimension_semantics=sem, vmem_limit_bytes=VMEM_LIMIT)


def _tile(n, prefs):
    for p in prefs:
        if n % p == 0:
            return p
    return n


def _sigmoid(x):
    return 1.0 / (1.0 + jnp.exp(-x))


def _softplus(x):
    return jnp.maximum(x, 0.0) + jnp.log(1.0 + jnp.exp(-jnp.abs(x)))


def _rowsum(x):
    return jnp.sum(x, axis=1, keepdims=True)


def _colsum(x):
    return jnp.sum(x, axis=0, keepdims=True)


def _dot(a, b):
    return jnp.dot(a, b, preferred_element_type=F32)


def _dot_nt(a, b):
    return lax.dot_general(a, b, (((1,), (1,)), ((), ())), preferred_element_type=F32)


def _dot_tn(a, b):
    return lax.dot_general(a, b, (((0,), (0,)), ((), ())), preferred_element_type=F32)


def _shift_right(x, j):
    if j == 0:
        return x
    r = pltpu.roll(x, j, 1)
    lane = lax.broadcasted_iota(jnp.int32, (x.shape[0], 128), 1)
    return jnp.concatenate([jnp.where(lane >= j, r[:, :128], 0.0), r[:, 128:]], axis=1)


def _shift_left(x, j):
    if j == 0:
        return x
    n = x.shape[1]
    r = pltpu.roll(x, n - j, 1)
    lane = lax.broadcasted_iota(jnp.int32, (x.shape[0], 128), 1)
    return jnp.concatenate([r[:, :n - 128], jnp.where(lane < 128 - j, r[:, n - 128:], 0.0)], axis=1)


def _causal_conv(xv, wv, bv):
    taps = wv.shape[1]
    shifted = [_shift_right(xv, taps - 1 - k) for k in range(taps - 1)]
    y = bv + wv[:, taps - 1:taps] * xv
    for k in range(taps - 1):
        y = y + wv[:, k:k + 1] * shifted[k]
    return y, shifted


def _causal_conv_bwd(dy, xv, shifted, wv):
    taps = wv.shape[1]
    lane = lax.broadcasted_iota(jnp.int32, (dy.shape[0], 128), 1)
    dwb = jnp.where(lane == taps, _rowsum(dy), 0.0)
    dwb = jnp.where(lane == taps - 1, _rowsum(dy * xv), dwb)
    dx = wv[:, taps - 1:taps] * dy
    for k in range(taps - 1):
        dx = dx + wv[:, k:k + 1] * _shift_left(dy, taps - 1 - k)
        dwb = jnp.where(lane == k, _rowsum(dy * shifted[k]), dwb)
    return dx, dwb


def _call(body, *, name, grid, in_specs, out_specs, out_shape, args, semantics, scratch_shapes=(), aliases=None,
          send=None):
    aliases = dict(aliases or {})
    if send is None:
        return pl.pallas_call(body, name=name, grid=grid, in_specs=in_specs, out_specs=out_specs, out_shape=out_shape,
                              scratch_shapes=list(scratch_shapes), input_output_aliases=aliases,
                              compiler_params=_cparams(*semantics))(*args)
    single = not isinstance(out_specs, (list, tuple))
    out_specs, out_shape = ([out_specs], [out_shape]) if single else (list(out_specs), list(out_shape))
    n_in, n_out = len(in_specs), len(out_specs)
    chips = send.shape[0] == N_DEV // 2
    n_copies = len(OTHER_CHIPS) if chips else N_DEV - 1

    def sending(*refs):
        ins, (src_ref, land_ref) = refs[:n_in], refs[n_in:n_in + 2]
        outs = refs[n_in + 2:n_in + 2 + n_out]
        send_sems, recv_sems = refs[n_in + 2 + n_out:n_in + 4 + n_out]
        scratch = refs[n_in + 6 + n_out:]
        step = 0
        for axis, size in enumerate(grid):
            step = step * size + pl.program_id(axis)

        @pl.when(step == 0)
        def _():
            for k in range(n_copies):
                if chips:
                    _chip_copy(src_ref, land_ref, send_sems, recv_sems, k, True).start()
                else:
                    _peer_copy(False, src_ref, land_ref, send_sems, recv_sems, k, True).start()

        body(*ins, *outs, *scratch)

    sem = pltpu.SemaphoreType.DMA((n_copies,))
    hbm = pltpu.HBM(send.shape, send.dtype)
    res = pl.pallas_call(
        sending, name=name, grid=grid,
        in_specs=list(in_specs) + [HBM, HBM],
        out_specs=out_specs + [SEM, SEM, HBM, HBM],
        out_shape=out_shape + [sem, sem, hbm, hbm],
        input_output_aliases={**aliases, n_in: n_out + 2, n_in + 1: n_out + 3},
        scratch_shapes=list(scratch_shapes),
        compiler_params=pltpu.CompilerParams(dimension_semantics=("arbitrary",) * len(grid), vmem_limit_bytes=VMEM_LIMIT,
                                             has_side_effects=EFFECT),
    )(*args, pltpu.with_memory_space_constraint(send, pltpu.HBM),
      pltpu.with_memory_space_constraint(lax.empty(send.shape, send.dtype), pltpu.HBM))
    return (res[0] if single else list(res[:n_out])), tuple(res[n_out:])


BLOCK_VMEM_BUDGET = VMEM_LIMIT * 3 // 4
MATMUL_MAX_TM = 768
MATMUL_MAX_TN = 3072
MATMUL_MAX_TK = 3072
MATMUL_MIN_STEPS = 4


def _largest_tile(n, align, cap):
    return max(d for d in range(align, min(n, cap) + 1, align) if n % d == 0)


def _matmul_tiles(m, n, k, a_bytes, b_bytes, out_bytes, has_add, m_align, k_align):
    tm = _largest_tile(m, m_align, MATMUL_MAX_TM)
    tk = _largest_tile(k, k_align, MATMUL_MAX_TK)
    for tn in sorted({d for d in range(LANES, min(n, MATMUL_MAX_TN) + 1, LANES) if n % d == 0}, reverse=True):
        need = 2 * (tm * tk * a_bytes + tk * tn * b_bytes) + tm * tn * (2 * out_bytes + (4 if k > tk else 0) + (8 if has_add else 0))
        enough_steps = (m // tm) * (n // tn) >= MATMUL_MIN_STEPS or tn == LANES
        if need <= BLOCK_VMEM_BUDGET and enough_steps:
            return tm, tn, tk
    return tm, LANES, tk


def _matmul(a, b, *, nt, out_dtype, name, add=None, tn_a=False, send=None):
    if tn_a:
        k, m = a.shape
    else:
        m, k = a.shape
    n = b.shape[0] if nt else b.shape[1]
    tm, tn, tk = _matmul_tiles(m, n, k, a.dtype.itemsize, b.dtype.itemsize, jnp.dtype(out_dtype).itemsize, add is not None,
                               LANES if tn_a else BF16_TILE_ROWS, BF16_TILE_ROWS if tn_a and not nt else LANES)
    nk = k // tk
    grid = (m // tm, n // tn, nk)

    def body(a_ref, b_ref, *rest):
        r_ref = None
        if add is not None:
            r_ref, rest = rest[0], rest[1:]
        o_ref = rest[0]
        av = a_ref[...].astype(BF16)
        bv = b_ref[...].astype(BF16)
        part = _dot_tn(av, bv) if tn_a else _dot_nt(av, bv) if nt else _dot(av, bv)

        def finish(r):
            if add is not None:
                r = r + r_ref[...]
            o_ref[...] = r.astype(out_dtype)

        if nk == 1:
            finish(part)
            return
        acc = rest[1]
        kk = pl.program_id(2)

        @pl.when(kk == 0)
        def _():
            acc[...] = part

        @pl.when((kk > 0) & (kk < nk - 1))
        def _():
            acc[...] += part

        @pl.when(kk == nk - 1)
        def _():
            finish(acc[...] + part)

    in_specs = [
        pl.BlockSpec((tk, tm), lambda i, j, kk: (kk, i)) if tn_a else pl.BlockSpec((tm, tk), lambda i, j, kk: (i, kk)),
        pl.BlockSpec((tn, tk), lambda i, j, kk: (j, kk)) if nt else pl.BlockSpec((tk, tn), lambda i, j, kk: (kk, j)),
    ]
    args = [a, b]
    if add is not None:
        in_specs.append(pl.BlockSpec((tm, tn), lambda i, j, kk: (i, j)))
        args.append(add)
    return _call(
        body, name=name, grid=grid, in_specs=in_specs, args=args,
        out_specs=pl.BlockSpec((tm, tn), lambda i, j, kk: (i, j)),
        out_shape=jax.ShapeDtypeStruct((m, n), out_dtype),
        scratch_shapes=[pltpu.VMEM((tm, tn), F32)] if nk > 1 else [],
        semantics=("parallel", "parallel", "arbitrary"), send=send)


def _norm_fwd(x, w_col, name):
    f, t = x.shape
    tt = _tile(t, (512, 256, 128))

    def body(x_ref, w_ref, o_ref):
        xv = x_ref[...]
        r = lax.rsqrt(jnp.mean(xv * xv, axis=0, keepdims=True) + EPS)
        o_ref[...] = (xv * r * w_ref[...]).astype(BF16)

    return pl.pallas_call(
        body,
        name=name,
        grid=(t // tt,),
        in_specs=[pl.BlockSpec((f, tt), lambda i: (0, i)), pl.BlockSpec((f, 1), lambda i: (0, 0))],
        out_specs=pl.BlockSpec((f, tt), lambda i: (0, i)),
        out_shape=jax.ShapeDtypeStruct((f, t), BF16),
        compiler_params=_cparams("parallel"),
    )(x, w_col)


def _norm_fwd_tokens(x, w_col, after, name):
    t, f = x.shape
    tt = _tile(t, (512, 256, 128))

    def body(x_ref, w_ref, after_ref, xt_ref, o_ref):
        xv = x_ref[...].T
        xt_ref[...] = xv
        r = lax.rsqrt(jnp.mean(xv * xv, axis=0, keepdims=True) + EPS)
        o_ref[...] = (xv * r * w_ref[...]).astype(BF16)

    blk = pl.BlockSpec((f, tt), lambda i: (0, i))
    return pl.pallas_call(
        body,
        name=name,
        grid=(t // tt,),
        in_specs=[pl.BlockSpec((tt, f), lambda i: (i, 0)), pl.BlockSpec((f, 1), lambda i: (0, 0)), ANY],
        out_specs=[blk, blk],
        out_shape=[jax.ShapeDtypeStruct((f, t), F32), jax.ShapeDtypeStruct((f, t), BF16)],
        compiler_params=_cparams("parallel"),
    )(x, w_col, after)


def _norm_bwd(dy, x, w_col, res, name, tokens_out=False):
    f, t = x.shape
    tt = _tile(t, (512, 256, 128))

    def body(dy_ref, x_ref, w_ref, res_ref, dx_ref, dw_ref):
        @pl.when(pl.program_id(0) == 0)
        def _():
            dw_ref[...] = jnp.zeros_like(dw_ref)

        xv = x_ref[...]
        r = lax.rsqrt(jnp.mean(xv * xv, axis=0, keepdims=True) + EPS)
        xhat = xv * r
        dyv = dy_ref[...]
        dw_ref[...] += _rowsum(dyv * xhat)
        dxhat = dyv * w_ref[...]
        dx = res_ref[...] + r * (dxhat - xhat * jnp.mean(dxhat * xhat, axis=0, keepdims=True))
        dx_ref[...] = dx.T if tokens_out else dx

    blk = pl.BlockSpec((f, tt), lambda i: (0, i))
    col = pl.BlockSpec((f, 1), lambda i: (0, 0))
    return pl.pallas_call(
        body,
        name=name,
        grid=(t // tt,),
        in_specs=[blk, blk, col, blk],
        out_specs=[pl.BlockSpec((tt, f), lambda i: (i, 0)) if tokens_out else blk, col],
        out_shape=[jax.ShapeDtypeStruct((t, f) if tokens_out else (f, t), F32), jax.ShapeDtypeStruct((f, 1), F32)],
        compiler_params=_cparams("arbitrary"),
    )(dy, x, w_col, res)


def _final_norm_loss(h, tgt, w_col):
    f, t = h.shape
    tt = _tile(t, (512, 256, 128))

    def body(h_ref, t_ref, w_ref, dh_ref, loss_ref, dw_ref):
        @pl.when(pl.program_id(0) == 0)
        def _():
            dw_ref[...] = jnp.zeros_like(dw_ref)
            loss_ref[...] = jnp.zeros_like(loss_ref)

        xv = h_ref[...]
        r = lax.rsqrt(jnp.mean(xv * xv, axis=0, keepdims=True) + EPS)
        xhat = xv * r
        wv = w_ref[...]
        err = xhat * wv - t_ref[...].T
        loss_ref[...] += 0.5 * _rowsum(jnp.mean(err * err, axis=0, keepdims=True))
        dyv = err * (1.0 / f)
        dw_ref[...] += _rowsum(dyv * xhat)
        dxhat = dyv * wv
        dh_ref[...] = r * (dxhat - xhat * jnp.mean(dxhat * xhat, axis=0, keepdims=True))

    blk = pl.BlockSpec((f, tt), lambda i: (0, i))
    col = pl.BlockSpec((f, 1), lambda i: (0, 0))
    one = pl.BlockSpec((1, 1), lambda i: (0, 0))
    return pl.pallas_call(
        body,
        name="final_norm_loss",
        grid=(t // tt,),
        in_specs=[blk, pl.BlockSpec((tt, f), lambda i: (i, 0)), col],
        out_specs=[blk, one, col],
        out_shape=[jax.ShapeDtypeStruct((f, t), F32), jax.ShapeDtypeStruct((1, 1), F32), jax.ShapeDtypeStruct((f, 1), F32)],
        compiler_params=_cparams("arbitrary"),
    )(h, tgt, w_col)


def _attn_mask(n):
    shape = (2 * WINDOW, Q_PER_KV * WINDOW)
    si = lax.broadcasted_iota(jnp.int32, shape, 0)
    qi = lax.broadcasted_iota(jnp.int32, shape, 1) & (WINDOW - 1)
    dist = WINDOW + qi - si
    return (dist >= 0) & (dist < WINDOW) & ((si >= WINDOW) | (n > 0))


def _lane_cat(ref, row0, rows):
    return jnp.concatenate([ref[row0 + i * rows:row0 + (i + 1) * rows, :] for i in range(Q_PER_KV)], axis=1)


def _attn_fwd(proj, sinks):
    t = proj.shape[1]
    nb = t // WINDOW
    scale = HEAD_DIM ** -0.5

    def body(s_ref, q_ref, kc_ref, kp_ref, vc_ref, vp_ref, o_ref, lse_ref):
        n = pl.program_id(0)
        valid = _attn_mask(n)
        for g in range(N_KV_HEADS):
            rows = slice(g * HEAD_DIM, (g + 1) * HEAD_DIM)
            kt = jnp.concatenate([kp_ref[rows, :], kc_ref[rows, :]], axis=1).astype(BF16)
            vt = jnp.concatenate([vp_ref[rows, :], vc_ref[rows, :]], axis=1).astype(BF16)
            qcat = (_lane_cat(q_ref, g * Q_PER_KV * HEAD_DIM, HEAD_DIM) * scale).astype(BF16)
            s = jnp.where(valid, _dot_tn(kt, qcat), NEG)
            sink = jnp.concatenate(
                [jnp.full((1, WINDOW), s_ref[g * Q_PER_KV + i], F32) for i in range(Q_PER_KV)], axis=1)
            m = jnp.maximum(jnp.max(s, axis=0, keepdims=True), sink)
            p = jnp.exp(s - m)
            denom = _colsum(p) + jnp.exp(sink - m)
            probs = (p / denom).astype(BF16)
            out = _dot(vt, probs)
            lse = m + jnp.log(denom)
            for i in range(Q_PER_KV):
                h = g * Q_PER_KV + i
                o_ref[h * HEAD_DIM:(h + 1) * HEAD_DIM, :] = out[:, i * WINDOW:(i + 1) * WINDOW]
                lse_ref[h:h + 1, :] = lse[:, i * WINDOW:(i + 1) * WINDOW]

    kb = OFF_K // KV_DIM
    vb = OFF_V // KV_DIM
    prev = lambda n: jnp.maximum(n - 1, 0)
    return pl.pallas_call(
        body,
        name="attn_fwd",
        grid=(nb,),
        in_specs=[
            pl.BlockSpec(memory_space=pltpu.SMEM),
            pl.BlockSpec((Q_DIM, WINDOW), lambda n: (0, n)),
            pl.BlockSpec((KV_DIM, WINDOW), lambda n: (kb, n)),
            pl.BlockSpec((KV_DIM, WINDOW), lambda n: (kb, prev(n))),
            pl.BlockSpec((KV_DIM, WINDOW), lambda n: (vb, n)),
            pl.BlockSpec((KV_DIM, WINDOW), lambda n: (vb, prev(n))),
        ],
        out_specs=[pl.BlockSpec((Q_DIM, WINDOW), lambda n: (0, n)), pl.BlockSpec((N_Q_HEADS, WINDOW), lambda n: (0, n))],
        out_shape=[jax.ShapeDtypeStruct((Q_DIM, t), F32), jax.ShapeDtypeStruct((N_Q_HEADS, t), F32)],
        compiler_params=_cparams("parallel"),
    )(sinks, proj, proj, proj, proj, proj)


def _attn_bwd(proj, sinks, out, lse, dout, dproj, send=None):
    t = proj.shape[1]
    nb = t // WINDOW
    scale = HEAD_DIM ** -0.5

    def body(s_ref, q_ref, kc_ref, kp_ref, vc_ref, vp_ref, o_ref, lse_ref, do_ref, dproj_ref,
             dqkv_ref, ds_ref, dk_carry, dv_carry):
        dq_ref = dqkv_ref.at[pl.ds(OFF_Q, Q_DIM)]
        dk_ref = dqkv_ref.at[pl.ds(OFF_K, KV_DIM)]
        dv_ref = dqkv_ref.at[pl.ds(OFF_V, KV_DIM)]
        step = pl.program_id(0)
        n = nb - 1 - step

        @pl.when(step == 0)
        def _():
            dk_carry[...] = jnp.zeros_like(dk_carry)
            dv_carry[...] = jnp.zeros_like(dv_carry)
            ds_ref[...] = jnp.zeros_like(ds_ref)

        valid = _attn_mask(n)
        for g in range(N_KV_HEADS):
            rows = slice(g * HEAD_DIM, (g + 1) * HEAD_DIM)
            q0 = g * Q_PER_KV * HEAD_DIM
            kt = jnp.concatenate([kp_ref[rows, :], kc_ref[rows, :]], axis=1).astype(BF16)
            vt = jnp.concatenate([vp_ref[rows, :], vc_ref[rows, :]], axis=1).astype(BF16)
            qf = _lane_cat(q_ref, q0, HEAD_DIM)
            qcat = qf.astype(BF16)
            ocat = _lane_cat(o_ref, q0, HEAD_DIM)
            docat = _lane_cat(do_ref, q0, HEAD_DIM)
            dob = docat.astype(BF16)
            lse_cat = jnp.concatenate(
                [lse_ref[g * Q_PER_KV + i:g * Q_PER_KV + i + 1, :] for i in range(Q_PER_KV)], axis=1)
            sink = jnp.concatenate(
                [jnp.full((1, WINDOW), s_ref[g * Q_PER_KV + i], F32) for i in range(Q_PER_KV)], axis=1)
            s = jnp.where(valid, _dot_tn(kt, (qf * scale).astype(BF16)), NEG)
            p = jnp.exp(s - lse_cat)
            dp = _dot_tn(vt, dob)
            delta = _colsum(docat * ocat)
            dsc = (p * (dp - delta)).astype(BF16)
            dsink_row = -jnp.exp(sink - lse_cat) * delta
            dq = _dot(kt, dsc) * scale
            dk = _dot_nt(qcat, dsc) * scale
            dv = _dot_nt(dob, p.astype(BF16))
            for i in range(Q_PER_KV):
                h = g * Q_PER_KV + i
                dq_ref[h * HEAD_DIM:(h + 1) * HEAD_DIM, :] = dq[:, i * WINDOW:(i + 1) * WINDOW].astype(BF16)
                ds_ref[h:h + 1, :] += _rowsum(dsink_row[:, i * WINDOW:(i + 1) * WINDOW])
            dk_ref[rows, :] = (dk[:, WINDOW:] + dk_carry[rows, :]).astype(BF16)
            dv_ref[rows, :] = (dv[:, WINDOW:] + dv_carry[rows, :]).astype(BF16)
            dk_carry[rows, :] = dk[:, :WINDOW]
            dv_carry[rows, :] = dv[:, :WINDOW]

    kb = OFF_K // KV_DIM
    vb = OFF_V // KV_DIM
    cur = lambda i: nb - 1 - i
    prev = lambda i: jnp.maximum(nb - 2 - i, 0)
    qspec = pl.BlockSpec((Q_DIM, WINDOW), lambda i: (0, cur(i)))
    return _call(
        body,
        name="attn_bwd",
        grid=(nb,),
        in_specs=[
            pl.BlockSpec(memory_space=pltpu.SMEM),
            qspec,
            pl.BlockSpec((KV_DIM, WINDOW), lambda i: (kb, cur(i))),
            pl.BlockSpec((KV_DIM, WINDOW), lambda i: (kb, prev(i))),
            pl.BlockSpec((KV_DIM, WINDOW), lambda i: (vb, cur(i))),
            pl.BlockSpec((KV_DIM, WINDOW), lambda i: (vb, prev(i))),
            qspec,
            pl.BlockSpec((N_Q_HEADS, WINDOW), lambda i: (0, cur(i))),
            qspec,
            pl.BlockSpec(memory_space=pl.ANY),
        ],
        out_specs=[pl.BlockSpec((OFF_Z, WINDOW), lambda i: (0, cur(i))), pl.BlockSpec((N_Q_HEADS, 1), lambda i: (0, 0))],
        out_shape=[jax.ShapeDtypeStruct(dproj.shape, BF16), jax.ShapeDtypeStruct((N_Q_HEADS, 1), F32)],
        scratch_shapes=[pltpu.VMEM((KV_DIM, WINDOW), F32), pltpu.VMEM((KV_DIM, WINDOW), F32)],
        aliases={9: 0},
        semantics=("arbitrary",), args=(sinks, proj, proj, proj, proj, proj, out, lse, dout, dproj), send=send)


CONV_ROWS = 256


def _conv_silu_fwd(proj, w_col, b_col):
    t = proj.shape[1]
    r0 = OFF_X // CONV_ROWS

    def body(x_ref, w_ref, b_ref, o_ref):
        y, _ = _causal_conv(x_ref[...], w_ref[...], b_ref[...])
        o_ref[...] = y * _sigmoid(y)

    return pl.pallas_call(
        body,
        name="ssd_conv_fwd",
        grid=(XBC_DIM // CONV_ROWS,),
        in_specs=[
            pl.BlockSpec((CONV_ROWS, t), lambda i: (r0 + i, 0)),
            pl.BlockSpec((CONV_ROWS, SSD_CONV), lambda i: (i, 0)),
            pl.BlockSpec((CONV_ROWS, 1), lambda i: (i, 0)),
        ],
        out_specs=pl.BlockSpec((CONV_ROWS, t), lambda i: (i, 0)),
        out_shape=jax.ShapeDtypeStruct((XBC_DIM, t), F32),
        compiler_params=_cparams("parallel"),
    )(proj, w_col, b_col)


def _conv_silu_bwd(proj, w_col, b_col, dout, row0, dproj, name):
    t = proj.shape[1]
    nrows = dout.shape[0]
    p0 = (OFF_X + row0) // CONV_ROWS
    c0 = row0 // CONV_ROWS

    def body(x_ref, w_ref, b_ref, do_ref, dproj_ref, dx_ref, dwb_ref):
        xv = x_ref[...]
        wv = w_ref[...]
        y, shifted = _causal_conv(xv, wv, b_ref[...])
        sg = _sigmoid(y)
        dy = do_ref[...] * (sg * (1.0 + y * (1.0 - sg)))
        dx, dwb_ref[...] = _causal_conv_bwd(dy, xv, shifted, wv)
        dx_ref[...] = dx.astype(BF16)

    return pl.pallas_call(
        body,
        name=name,
        grid=(nrows // CONV_ROWS,),
        in_specs=[
            pl.BlockSpec((CONV_ROWS, t), lambda i: (p0 + i, 0)),
            pl.BlockSpec((CONV_ROWS, SSD_CONV), lambda i: (c0 + i, 0)),
            pl.BlockSpec((CONV_ROWS, 1), lambda i: (c0 + i, 0)),
            pl.BlockSpec((CONV_ROWS, t), lambda i: (i, 0)),
            pl.BlockSpec(memory_space=pl.ANY),
        ],
        out_specs=[pl.BlockSpec((CONV_ROWS, t), lambda i: (p0 + i, 0)), pl.BlockSpec((CONV_ROWS, 128), lambda i: (i, 0))],
        out_shape=[jax.ShapeDtypeStruct(dproj.shape, BF16), jax.ShapeDtypeStruct((nrows, 128), F32)],
        input_output_aliases={4: 0},
        compiler_params=_cparams("parallel"),
    )(proj, w_col, b_col, dout, dproj)


def _ssd_specs(order):
    xb = D_INNER // BC_DIM
    dtb = OFF_DT // N_SSD_HEADS
    col = pl.BlockSpec((N_SSD_HEADS, 1), lambda c: (0, 0))
    return [
        pl.BlockSpec((D_INNER, CHUNK), lambda c: (0, order(c))),
        pl.BlockSpec((BC_DIM, CHUNK), lambda c: (xb, order(c))),
        pl.BlockSpec((BC_DIM, CHUNK), lambda c: (xb + 1, order(c))),
        pl.BlockSpec((N_SSD_HEADS, CHUNK), lambda c: (dtb, order(c))),
        col, col, col,
    ]


def _ssd_common(dt_ref, dtb_ref, alog_ref):
    z = dt_ref[...] + dtb_ref[...]
    dt = _softplus(z)
    a_neg = -jnp.exp(alog_ref[...])
    d_a = dt * a_neg
    row = lax.broadcasted_iota(jnp.int32, (CHUNK, CHUNK), 0)
    colm = lax.broadcasted_iota(jnp.int32, (CHUNK, CHUNK), 1)
    upper = (row <= colm).astype(F32)
    a_cs = jnp.dot(d_a, upper, precision=HIGHEST, preferred_element_type=F32)
    a_last = _rowsum(d_a)
    return z, dt, a_neg, a_cs, a_last, row >= colm, row == colm


def _decay(a_row, causal):
    a_s = jnp.broadcast_to(a_row, (CHUNK, CHUNK))
    seg = a_s.T - a_s
    return jnp.where(causal, jnp.exp(jnp.where(causal, seg, 0.0)), 0.0)


def _ssd_fwd(xbc, proj, dtb_col, alog_col, dsk_col):
    t = xbc.shape[1]
    nc = t // CHUNK

    def body(xs_ref, b_ref, c_ref, dt_ref, dtb_ref, alog_ref, dsk_ref, y_ref, hst_ref, h_scr):
        @pl.when(pl.program_id(0) == 0)
        def _():
            h_scr[...] = jnp.zeros_like(h_scr)

        _, dt, _, a_cs, a_last, causal, _ = _ssd_common(dt_ref, dtb_ref, alog_ref)
        hst_ref[0] = h_scr[...]
        dsk = dsk_ref[...]
        for g in range(N_SSD_GROUPS):
            grows = slice(g * D_STATE, (g + 1) * D_STATE)
            bb = b_ref[grows, :].astype(BF16)
            cb_ = c_ref[grows, :].astype(BF16)
            cb = _dot_tn(cb_, bb)
            for j in range(g * HEADS_PER_GROUP, (g + 1) * HEADS_PER_GROUP):
                rows = slice(j * SSD_HEAD_DIM, (j + 1) * SSD_HEAD_DIM)
                a = a_cs[j:j + 1, :]
                m = (cb * _decay(a, causal)).astype(BF16)
                xs = xs_ref[rows, :]
                xc = xs * dt[j:j + 1, :]
                hj = h_scr[rows, :]
                y = _dot_nt(xc.astype(BF16), m) + _dot(hj.astype(BF16), cb_) * jnp.exp(a) + dsk[j:j + 1, :] * xs
                y_ref[rows, :] = y
                al = a_last[j:j + 1, :]
                w = jnp.exp(al - a)
                h_scr[rows, :] = jnp.exp(al) * hj + _dot_nt((xc * w).astype(BF16), bb)

    return pl.pallas_call(
        body,
        name="ssd_fwd",
        grid=(nc,),
        in_specs=_ssd_specs(lambda c: c),
        out_specs=[
            pl.BlockSpec((D_INNER, CHUNK), lambda c: (0, c)),
            pl.BlockSpec((1, D_INNER, D_STATE), lambda c: (c, 0, 0)),
        ],
        out_shape=[
            jax.ShapeDtypeStruct((D_INNER, t), F32),
            jax.ShapeDtypeStruct((nc, D_INNER, D_STATE), F32),
        ],
        scratch_shapes=[pltpu.VMEM((D_INNER, D_STATE), F32)],
        compiler_params=_cparams("arbitrary"),
    )(xbc, xbc, xbc, proj, dtb_col, alog_col, dsk_col)


def _ssd_bwd(xbc, proj, dtb_col, alog_col, dsk_col, hst, dy):
    t = xbc.shape[1]
    nc = t // CHUNK
    rev = lambda c: nc - 1 - c

    def body(xs_ref, b_ref, c_ref, dt_ref, dtb_ref, alog_ref, dsk_ref, hst_ref, dy_ref,
             dxs_ref, db_ref, dc_ref, ddt_ref, dalog_ref, ddsk_ref, ddtb_ref, dh_scr, da_scr, ddt_scr, dd_scr):
        @pl.when(pl.program_id(0) == 0)
        def _():
            dh_scr[...] = jnp.zeros_like(dh_scr)
            dalog_ref[...] = jnp.zeros_like(dalog_ref)
            ddsk_ref[...] = jnp.zeros_like(ddsk_ref)
            ddtb_ref[...] = jnp.zeros_like(ddtb_ref)

        z, dt, a_neg, a_cs, a_last, causal, eye = _ssd_common(dt_ref, dtb_ref, alog_ref)
        dsk = dsk_ref[...]
        last_lane = lax.broadcasted_iota(jnp.int32, (1, CHUNK), 1) == CHUNK - 1
        for g in range(N_SSD_GROUPS):
            grows = slice(g * D_STATE, (g + 1) * D_STATE)
            bb = b_ref[grows, :].astype(BF16)
            cb_ = c_ref[grows, :].astype(BF16)
            cb = _dot_tn(cb_, bb)
            dcb = jnp.zeros((CHUNK, CHUNK), F32)
            dc_acc = jnp.zeros((D_STATE, CHUNK), F32)
            db_acc = jnp.zeros((D_STATE, CHUNK), F32)
            for j in range(g * HEADS_PER_GROUP, (g + 1) * HEADS_PER_GROUP):
                rows = slice(j * SSD_HEAD_DIM, (j + 1) * SSD_HEAD_DIM)
                a = a_cs[j:j + 1, :]
                al = a_last[j:j + 1, :]
                lam = _decay(a, causal)
                mf = cb * lam
                xs = xs_ref[rows, :]
                dtj = dt[j:j + 1, :]
                xc = xs * dtj
                w = jnp.exp(al - a)
                e = jnp.exp(a)
                gam = jnp.exp(al)
                hj = hst_ref[0, rows, :]
                hjb = hj.astype(BF16)
                dyv = dy_ref[rows, :]
                dyb = dyv.astype(BF16)
                dd_scr[j:j + 1, :] = _colsum(dyv * xs)
                gb = (dyv * e).astype(BF16)
                dh_in = _dot_nt(gb, cb_)
                dc_acc = dc_acc + _dot_tn(hjb, gb)
                yoff = _dot(hjb, cb_) * e
                da = _colsum(dyv * yoff)
                dm = _dot_tn(dyb, xc.astype(BF16))
                dxc = _dot(dyb, mf.astype(BF16))
                dcb = dcb + dm * lam
                nmat = dm * mf
                rs = jnp.broadcast_to(_rowsum(nmat), (CHUNK, CHUNK))
                da = da + _colsum(jnp.where(eye, rs, 0.0)) - _colsum(nmat)
                ds = dh_scr[rows, :]
                dsb = ds.astype(BF16)
                t1 = _dot(dsb, bb)
                xcw = xc * w
                dxc = dxc + w * t1
                dww = _colsum(xcw * t1)
                da_l = _rowsum(dww) + _rowsum(_colsum(ds * hj)) * gam
                da = da - dww + jnp.where(last_lane, da_l, 0.0)
                db_acc = db_acc + _dot_tn(dsb, xcw.astype(BF16))
                dh_scr[rows, :] = gam * ds + dh_in
                dxs_ref[rows, :] = dsk[j:j + 1, :] * dyv + dxc * dtj
                da_scr[j:j + 1, :] = da
                ddt_scr[j:j + 1, :] = _colsum(dxc * xs)
            dcbb = dcb.astype(BF16)
            dc_ref[grows, :] = dc_acc + _dot_nt(bb, dcbb)
            db_ref[grows, :] = db_acc + _dot(cb_, dcbb)
        dda = jnp.dot(da_scr[...], causal.astype(F32), precision=HIGHEST, preferred_element_type=F32)
        ddt = ddt_scr[...] + dda * a_neg
        ddt_raw = ddt * _sigmoid(z)
        ddt_ref[...] = ddt_raw
        ddtb_ref[...] += _rowsum(ddt_raw)
        dalog_ref[...] += _rowsum(dda * dt) * a_neg
        ddsk_ref[...] += _rowsum(dd_scr[...])

    col = pl.BlockSpec((N_SSD_HEADS, 1), lambda c: (0, 0))
    bc = pl.BlockSpec((BC_DIM, CHUNK), lambda c: (0, rev(c)))
    xs_spec = pl.BlockSpec((D_INNER, CHUNK), lambda c: (0, rev(c)))
    small = pltpu.VMEM((N_SSD_HEADS, CHUNK), F32)
    return pl.pallas_call(
        body,
        name="ssd_bwd",
        grid=(nc,),
        in_specs=_ssd_specs(rev) + [pl.BlockSpec((1, D_INNER, D_STATE), lambda c: (rev(c), 0, 0)), xs_spec],
        out_specs=[xs_spec, bc, bc, pl.BlockSpec((N_SSD_HEADS, CHUNK), lambda c: (0, rev(c))), col, col, col],
        out_shape=[
            jax.ShapeDtypeStruct((D_INNER, t), F32),
            jax.ShapeDtypeStruct((BC_DIM, t), F32),
            jax.ShapeDtypeStruct((BC_DIM, t), F32),
            jax.ShapeDtypeStruct((N_SSD_HEADS, t), F32),
            jax.ShapeDtypeStruct((N_SSD_HEADS, 1), F32),
            jax.ShapeDtypeStruct((N_SSD_HEADS, 1), F32),
            jax.ShapeDtypeStruct((N_SSD_HEADS, 1), F32),
        ],
        scratch_shapes=[pltpu.VMEM((D_INNER, D_STATE), F32), small, small, small],
        compiler_params=_cparams("arbitrary"),
    )(xbc, xbc, xbc, proj, dtb_col, alog_col, dsk_col, hst, dy)


GN_ROWS = D_INNER // N_SSD_GROUPS


def _gnorm_fwd(y, proj, w_col):
    t = y.shape[1]
    tt = _tile(t, (512, 256, 128))
    z0 = OFF_Z // GN_ROWS

    def body(y_ref, z_ref, w_ref, o_ref):
        zv = z_ref[...]
        u = y_ref[...] * (zv * _sigmoid(zv))
        r = lax.rsqrt(jnp.mean(u * u, axis=0, keepdims=True) + EPS)
        o_ref[...] = (u * r * w_ref[...]).astype(BF16)

    blk = pl.BlockSpec((GN_ROWS, tt), lambda g, i: (g, i))
    return pl.pallas_call(
        body,
        name="gnorm_fwd",
        grid=(N_SSD_GROUPS, t // tt),
        in_specs=[blk, pl.BlockSpec((GN_ROWS, tt), lambda g, i: (z0 + g, i)), pl.BlockSpec((GN_ROWS, 1), lambda g, i: (g, 0))],
        out_specs=blk,
        out_shape=jax.ShapeDtypeStruct((D_INNER, t), BF16),
        compiler_params=_cparams("parallel", "parallel"),
    )(y, proj, w_col)


def _gnorm_bwd(dout, y, proj, w_col, send=None):
    t = y.shape[1]
    tt = _tile(t, (512, 256, 128))
    z0 = OFF_Z // GN_ROWS

    def body(do_ref, y_ref, z_ref, w_ref, dy_ref, dz_ref, dw_ref):
        @pl.when(pl.program_id(1) == 0)
        def _():
            dw_ref[...] = jnp.zeros_like(dw_ref)

        zv = z_ref[...]
        yv = y_ref[...]
        sg = _sigmoid(zv)
        sz = zv * sg
        u = yv * sz
        r = lax.rsqrt(jnp.mean(u * u, axis=0, keepdims=True) + EPS)
        xhat = u * r
        dov = do_ref[...]
        dw_ref[...] += _rowsum(dov * xhat)
        dxhat = dov * w_ref[...]
        du = r * (dxhat - xhat * jnp.mean(dxhat * xhat, axis=0, keepdims=True))
        dy_ref[...] = du * sz
        dz_ref[...] = (du * yv * (sg * (1.0 + zv * (1.0 - sg)))).astype(BF16)

    blk = pl.BlockSpec((GN_ROWS, tt), lambda g, i: (g, i))
    col = pl.BlockSpec((GN_ROWS, 1), lambda g, i: (g, 0))
    return _call(
        body,
        name="gnorm_bwd",
        grid=(N_SSD_GROUPS, t // tt),
        in_specs=[blk, blk, pl.BlockSpec((GN_ROWS, tt), lambda g, i: (z0 + g, i)), col],
        out_specs=[blk, pl.BlockSpec((GN_ROWS, tt), lambda g, i: (z0 + g, i)), col],
        out_shape=[jax.ShapeDtypeStruct((D_INNER, t), F32), jax.ShapeDtypeStruct((IN_DIM, t), BF16),
                   jax.ShapeDtypeStruct((D_INNER, 1), F32)],
        semantics=("parallel", "arbitrary"), args=(dout, y, proj, w_col), send=send)


GATE_ROWS = 128


def _gate_specs(t):
    nr = D_MODEL // GATE_ROWS
    blk = pl.BlockSpec((GATE_ROWS, t), lambda r: (r, 0))
    rows_from = lambda first: pl.BlockSpec(
        (pl.Element(GATE_ROWS), pl.Element(t)), lambda r: (pl.multiple_of(first + GATE_ROWS * r, N_SSD_HEADS), 0))
    return blk, [
        rows_from(OFF_GA),
        rows_from(OFF_GS),
        pl.BlockSpec((GATE_ROWS, 1), lambda r: (r, 0)),
        pl.BlockSpec((GATE_ROWS, 1), lambda r: (nr + r, 0)),
        blk, blk,
    ]


def _gate_fwd(proj, b_col, attn, ssd):
    t = proj.shape[1]
    blk, specs = _gate_specs(t)

    def body(ga_ref, gs_ref, ba_ref, bs_ref, a_ref, s_ref, o_ref):
        o_ref[...] = (_sigmoid(ga_ref[...] + ba_ref[...]) * a_ref[...]
                      + _sigmoid(gs_ref[...] + bs_ref[...]) * s_ref[...]).astype(BF16)

    return pl.pallas_call(
        body,
        name="gate_fwd",
        grid=(D_MODEL // GATE_ROWS,),
        in_specs=specs,
        out_specs=blk,
        out_shape=jax.ShapeDtypeStruct((D_MODEL, t), BF16),
        compiler_params=_cparams("parallel"),
    )(proj, proj, b_col, b_col, attn, ssd)


def _gate_bwd(proj, b_col, attn, ssd, dmix, send=None):
    t = proj.shape[1]
    blk, specs = _gate_specs(t)

    def body(ga_ref, gs_ref, ba_ref, bs_ref, a_ref, s_ref, dm_ref, da_ref, dso_ref, dga_ref, dgs_ref, dba_ref, dbs_ref):
        dm = dm_ref[...]
        sa = _sigmoid(ga_ref[...] + ba_ref[...])
        ss = _sigmoid(gs_ref[...] + bs_ref[...])
        da_ref[...] = (dm * sa).astype(BF16)
        dso_ref[...] = (dm * ss).astype(BF16)
        dga = dm * a_ref[...] * sa * (1.0 - sa)
        dgs = dm * s_ref[...] * ss * (1.0 - ss)
        dga_ref[...] = dga.astype(BF16)
        dgs_ref[...] = dgs.astype(BF16)
        dba_ref[...] = _rowsum(dga)
        dbs_ref[...] = _rowsum(dgs)

    col = pl.BlockSpec((GATE_ROWS, 1), lambda r: (r, 0))
    act = jax.ShapeDtypeStruct((D_MODEL, t), BF16)
    bias = jax.ShapeDtypeStruct((D_MODEL, 1), F32)
    return _call(
        body,
        name="gate_bwd",
        grid=(D_MODEL // GATE_ROWS,),
        in_specs=specs + [blk],
        out_specs=[blk, blk, blk, blk, col, col],
        out_shape=[act, act, act, act, bias, bias],
        semantics=("parallel",), args=(proj, proj, b_col, b_col, attn, ssd, dmix), send=send)


FFN_ROWS = 256


def _ffn_fwd(u0, w_col, b_col):
    t = u0.shape[2]

    def body(u_ref, w_ref, b_ref, o_ref):
        val, _ = _causal_conv(u_ref[0], w_ref[0], b_ref[0])
        gt, _ = _causal_conv(u_ref[1], w_ref[1], b_ref[1])
        o_ref[...] = (gt * _sigmoid(gt) * val).astype(BF16)

    return pl.pallas_call(
        body,
        name="ffn_fwd",
        grid=(D_FF // FFN_ROWS,),
        in_specs=[
            pl.BlockSpec((2, FFN_ROWS, t), lambda i: (0, i, 0)),
            pl.BlockSpec((2, FFN_ROWS, FFN_CONV), lambda i: (0, i, 0)),
            pl.BlockSpec((2, FFN_ROWS, 1), lambda i: (0, i, 0)),
        ],
        out_specs=pl.BlockSpec((FFN_ROWS, t), lambda i: (i, 0)),
        out_shape=jax.ShapeDtypeStruct((D_FF, t), BF16),
        compiler_params=_cparams("parallel"),
    )(u0, w_col, b_col)


def _ffn_bwd(u0, w_col, b_col, dg, send=None):
    t = u0.shape[2]

    def body(u_ref, w_ref, b_ref, dg_ref, du_ref, dwb_ref):
        xval, wval = u_ref[0], w_ref[0]
        xgt, wgt = u_ref[1], w_ref[1]
        val, sh_val = _causal_conv(xval, wval, b_ref[0])
        gt, sh_gt = _causal_conv(xgt, wgt, b_ref[1])
        sg = _sigmoid(gt)
        dgv = dg_ref[...]
        dval = dgv * (gt * sg)
        dgt = dgv * val * (sg * (1.0 + gt * (1.0 - sg)))
        dx, dwb_ref[0] = _causal_conv_bwd(dval, xval, sh_val, wval)
        du_ref[0] = dx.astype(BF16)
        dx, dwb_ref[1] = _causal_conv_bwd(dgt, xgt, sh_gt, wgt)
        du_ref[1] = dx.astype(BF16)

    return _call(
        body,
        name="ffn_bwd",
        grid=(D_FF // FFN_ROWS,),
        in_specs=[
            pl.BlockSpec((2, FFN_ROWS, t), lambda i: (0, i, 0)),
            pl.BlockSpec((2, FFN_ROWS, FFN_CONV), lambda i: (0, i, 0)),
            pl.BlockSpec((2, FFN_ROWS, 1), lambda i: (0, i, 0)),
            pl.BlockSpec((FFN_ROWS, t), lambda i: (i, 0)),
        ],
        out_specs=[pl.BlockSpec((2, FFN_ROWS, t), lambda i: (0, i, 0)), pl.BlockSpec((2, FFN_ROWS, 128), lambda i: (0, i, 0))],
        out_shape=[jax.ShapeDtypeStruct((2, D_FF, t), BF16), jax.ShapeDtypeStruct((2, D_FF, 128), F32)],
        semantics=("parallel",), args=(u0, w_col, b_col, dg), send=send)


def _adamw_math(w, g, m, v):
    m = ADAM_B1 * m + (1.0 - ADAM_B1) * g
    v = ADAM_B2 * v + (1.0 - ADAM_B2) * (g * g)
    m_hat = m / (1.0 - ADAM_B1 ** ADAM_STEP)
    v_hat = v / (1.0 - ADAM_B2 ** ADAM_STEP)
    delta = -ADAM_LR * (m_hat / (jnp.sqrt(v_hat) + ADAM_EPS) + ADAM_WD * w)
    return delta, m, v


def _adamw_sharded(parts, w, m, v, name):
    r, c = w.shape[0], w.shape[-1]
    slots = parts.shape[0]
    per_lane = 2 * r * (slots * parts.dtype.itemsize + 7 * w.dtype.itemsize)
    tc = max(d for d in range(LANES, c + 1, LANES) if c % d == 0 and (d * per_lane <= BLOCK_VMEM_BUDGET or d == LANES))
    blk_shape = (r, tc) if w.ndim == 2 else (r, 1, tc)

    def body(p_ref, w_ref, m_ref, v_ref, g_ref, d_ref, nm_ref, nv_ref):
        g = p_ref[0].astype(F32)
        for s in range(1, slots):
            g = g + p_ref[s].astype(F32)
        flat = lambda ref: ref[...].reshape(r, tc)
        d, nm, nv = _adamw_math(flat(w_ref), g, flat(m_ref), flat(v_ref))
        for ref, val in ((g_ref, g), (d_ref, d), (nm_ref, nm), (nv_ref, nv)):
            ref[...] = val.reshape(blk_shape)

    blk = pl.BlockSpec(blk_shape, (lambda i: (0, i)) if w.ndim == 2 else (lambda i: (0, 0, i)))
    out = jax.ShapeDtypeStruct(w.shape, F32)
    return pl.pallas_call(
        body,
        name=name,
        grid=(c // tc,),
        in_specs=[pl.BlockSpec((slots, r, tc), lambda i: (0, 0, i)), blk, blk, blk],
        out_specs=[blk, blk, blk, blk],
        out_shape=[out, out, out, out],
        compiler_params=_cparams("parallel"),
    )(parts, w, m, v)


def _lane_offsets(sizes):
    offsets, pos = [], 0
    for n in sizes:
        offsets.append(pos)
        pos += -(-n // 128) * 128
    return offsets, pos


def _pack_row(parts):
    rows = [p.reshape(1, -1).astype(F32) for p in parts]
    return jnp.concatenate([jnp.pad(r, ((0, 0), (0, -r.shape[1] % 128))) for r in rows], axis=1)


def _small_update(parts, me, full_sizes, ws, ms, vs):
    n = len(ws)
    offsets, _ = _lane_offsets([1] + list(full_sizes))

    def body(me_ref, p_ref, *refs):
        w_refs, m_refs, v_refs = refs[:n], refs[n:2 * n], refs[2 * n:3 * n]
        scalar_ref, out_refs = refs[3 * n], refs[3 * n + 1:]
        tot = p_ref[0]
        for s in range(1, N_DEV):
            tot = tot + p_ref[s]
        scalar_ref[...] = tot[:, 0:1]
        for k in range(n):
            g_ref, d_ref, nm_ref, nv_ref = out_refs[4 * k:4 * k + 4]
            taps, cols = w_refs[k].shape
            if taps == 1:
                g_ref[...] = tot[:, offsets[k + 1]:offsets[k + 1] + cols]
            else:
                full = full_sizes[k] // taps
                for tap in range(taps):
                    mine = jnp.zeros((1, cols), F32)
                    for d in range(N_DEV):
                        lo = offsets[k + 1] + tap * full + d * cols
                        mine = jnp.where(me_ref[0] == d, tot[:, lo:lo + cols], mine)
                    g_ref[tap:tap + 1, :] = mine
            d_ref[...], nm_ref[...], nv_ref[...] = _adamw_math(w_refs[k][...], g_ref[...], m_refs[k][...], v_refs[k][...])

    vmem = pl.BlockSpec(memory_space=pltpu.VMEM)
    out_shape = [jax.ShapeDtypeStruct((1, 1), F32)]
    for wk in ws:
        out_shape += [jax.ShapeDtypeStruct(wk.shape, F32)] * 4
    res = pl.pallas_call(
        body,
        name="small_update",
        in_specs=[pl.BlockSpec(memory_space=pltpu.SMEM)] + [vmem] * (1 + 3 * n),
        out_specs=[vmem] * len(out_shape),
        out_shape=out_shape,
    )(me, parts, *ws, *ms, *vs)
    return res[0], [res[1 + 4 * k:5 + 4 * k] for k in range(n)]


ANY = pl.BlockSpec(memory_space=pl.ANY)
FLIPS = [(k >> 2 & 1, k >> 1 & 1, k & 1) for k in range(1, N_DEV)]


def _place():
    return lax.axis_index("x"), lax.axis_index("y"), lax.axis_index("c")


HBM = pl.BlockSpec(memory_space=pltpu.HBM)
SEM = pl.BlockSpec(memory_space=pltpu.SEMAPHORE)
EFFECT = pltpu.SideEffectType.DATAFLOW_SIDE_EFFECTING


def _peer_copy(gather, src_ref, land_ref, send_sems, recv_sems, k, sending):
    x, y, c = _place()
    fx, fy, fc = FLIPS[k]
    me = 4 * x + 2 * y + c
    peer = 4 * (x ^ fx) + 2 * (y ^ fy) + (c ^ fc)
    return pltpu.make_async_remote_copy(
        src_ref=src_ref if gather else src_ref.at[peer],
        dst_ref=land_ref.at[me if sending else peer],
        send_sem=send_sems.at[k], recv_sem=recv_sems.at[k],
        device_id=(x ^ fx, y ^ fy, c ^ fc), device_id_type=MESH)


SIBLING = 0
OTHER_CHIPS = (1, 3, 5)


def _gather_start(srcs, name, via_sibling):
    n = len(srcs)
    lands = [lax.empty((N_DEV,) + s.shape, s.dtype) for s in srcs]

    def body(*refs):
        src_refs, land_refs = refs[:n], refs[n:2 * n]
        send, recv = refs[2 * n:3 * n], refs[3 * n:4 * n]
        for i in range(n):
            for k in (SIBLING,) + OTHER_CHIPS if via_sibling else range(N_DEV - 1):
                _peer_copy(True, src_refs[i], land_refs[i], send[i], recv[i], k, True).start()

    sem = pltpu.SemaphoreType.DMA((N_DEV - 1,))
    hbm = lambda a: pltpu.HBM(a.shape, a.dtype)
    res = pl.pallas_call(
        body,
        name=name,
        in_specs=[HBM] * (2 * n),
        out_specs=[SEM] * (2 * n) + [HBM] * (2 * n),
        out_shape=[sem] * (2 * n) + [hbm(s) for s in srcs] + [hbm(a) for a in lands],
        input_output_aliases={i: 2 * n + i for i in range(2 * n)},
        compiler_params=pltpu.CompilerParams(has_side_effects=EFFECT),
    )(*[pltpu.with_memory_space_constraint(a, pltpu.HBM) for a in list(srcs) + lands])
    return res[:n], res[n:2 * n], res[2 * n:3 * n], res[3 * n:4 * n]


def _exchange_wait(send_sems, recv_sems, src, land, after, gather, name):
    def body(src_ref, land_ref, send_ref, recv_ref, after_ref, src_out, land_out):
        for k in range(N_DEV - 1):
            cp = _peer_copy(gather, src_ref, land_ref, send_ref, recv_ref, k, False)
            cp.wait_send()
            cp.wait_recv()

    hbm = lambda a: pltpu.HBM(a.shape, a.dtype)
    return pl.pallas_call(
        body,
        name=name,
        in_specs=[HBM, HBM, SEM, SEM, ANY],
        out_specs=[HBM, HBM],
        out_shape=[hbm(src), hbm(land)],
        input_output_aliases={0: 0, 1: 1},
        compiler_params=pltpu.CompilerParams(has_side_effects=EFFECT),
    )(src, land, send_sems, recv_sems, after)


def _own_slot(src, land, me, gather):
    own = src[None] if gather else lax.dynamic_slice_in_dim(src, me, 1, axis=0)
    return lax.dynamic_update_slice_in_dim(land, own, me, axis=0)


def _forwarded_copy(land_ref, send_sems, recv_sems, j, sending):
    x, y, c = _place()
    fx, fy, _ = FLIPS[OTHER_CHIPS[j]]
    slot = 4 * (x ^ fx) + 2 * (y ^ fy) + (c if sending else 1 - c)
    return pltpu.make_async_remote_copy(
        src_ref=land_ref.at[slot], dst_ref=land_ref.at[slot], send_sem=send_sems.at[j], recv_sem=recv_sems.at[j],
        device_id=(x, y, 1 - c), device_id_type=MESH)


def _gather_forward(send_sems, recv_sems, srcs, lands, after, name):
    n = len(srcs)

    def body(*refs):
        src_refs, land_refs = refs[:n], refs[n:2 * n]
        send, recv = refs[2 * n:3 * n], refs[3 * n:4 * n]
        fwd_send, fwd_recv = refs[4 * n + 1:5 * n + 1], refs[5 * n + 1:6 * n + 1]
        for i in range(n):
            for j, k in enumerate(OTHER_CHIPS):
                _peer_copy(True, src_refs[i], land_refs[i], send[i], recv[i], k, False).wait_recv()
                _forwarded_copy(land_refs[i], fwd_send[i], fwd_recv[i], j, True).start()

    sem = pltpu.SemaphoreType.DMA((len(OTHER_CHIPS),))
    hbm = lambda a: pltpu.HBM(a.shape, a.dtype)
    res = pl.pallas_call(
        body,
        name=name,
        in_specs=[HBM] * (2 * n) + [SEM] * (2 * n) + [ANY],
        out_specs=[SEM] * (2 * n) + [HBM] * (2 * n),
        out_shape=[sem] * (2 * n) + [hbm(a) for a in srcs] + [hbm(a) for a in lands],
        input_output_aliases={i: 2 * n + i for i in range(2 * n)},
        compiler_params=pltpu.CompilerParams(has_side_effects=EFFECT),
    )(*srcs, *lands, *send_sems, *recv_sems, after)
    return res[:n], res[n:2 * n], res[2 * n:3 * n], res[3 * n:4 * n]


def _gather_wait_forwarded(send_sems, recv_sems, fwd_send, fwd_recv, src, land, after, name):
    def body(src_ref, land_ref, send_ref, recv_ref, fwd_send_ref, fwd_recv_ref, after_ref, src_out, land_out):
        for k in (SIBLING,) + OTHER_CHIPS:
            _peer_copy(True, src_ref, land_ref, send_ref, recv_ref, k, False).wait_send()
        _peer_copy(True, src_ref, land_ref, send_ref, recv_ref, SIBLING, False).wait_recv()
        for j in range(len(OTHER_CHIPS)):
            _forwarded_copy(land_ref, fwd_send_ref, fwd_recv_ref, j, True).wait_send()
            _forwarded_copy(land_ref, fwd_send_ref, fwd_recv_ref, j, False).wait_recv()

    hbm = lambda a: pltpu.HBM(a.shape, a.dtype)
    return pl.pallas_call(
        body,
        name=name,
        in_specs=[HBM, HBM, SEM, SEM, SEM, SEM, ANY],
        out_specs=[HBM, HBM],
        out_shape=[hbm(src), hbm(land)],
        input_output_aliases={0: 0, 1: 1},
        compiler_params=pltpu.CompilerParams(has_side_effects=EFFECT),
    )(src, land, send_sems, recv_sems, fwd_send, fwd_recv, after)


N_CHIPS = N_DEV // 2


def _pair_exchange(by_core, meanwhile, name):
    def copy(src_ref, land_ref, send_sems, recv_sems, q):
        x, y, c = _place()
        return pltpu.make_async_remote_copy(
            src_ref=src_ref.at[q, 1 - c], dst_ref=land_ref.at[q], send_sem=send_sems.at[q], recv_sem=recv_sems.at[q],
            device_id=(x, y, 1 - c), device_id_type=MESH)

    def start(src_ref, land_ref, send_sems, recv_sems, src_out, land_out):
        for q in range(N_CHIPS):
            copy(src_ref, land_ref, send_sems, recv_sems, q).start()

    def wait(src_ref, land_ref, send_sems, recv_sems, after_ref, src_out, land_out):
        for q in range(N_CHIPS):
            cp = copy(src_ref, land_ref, send_sems, recv_sems, q)
            cp.wait_send()
            cp.wait_recv()

    sem = pltpu.SemaphoreType.DMA((N_CHIPS,))
    hbm_src = pltpu.HBM(by_core.shape, by_core.dtype)
    hbm_land = pltpu.HBM(by_core.shape[:1] + by_core.shape[2:], by_core.dtype)
    params = pltpu.CompilerParams(has_side_effects=EFFECT)
    send_sems, recv_sems, src, land = pl.pallas_call(
        start, name=name + "_start", in_specs=[HBM, HBM], out_specs=[SEM, SEM, HBM, HBM],
        out_shape=[sem, sem, hbm_src, hbm_land], input_output_aliases={0: 2, 1: 3}, compiler_params=params,
    )(pltpu.with_memory_space_constraint(by_core, pltpu.HBM),
      pltpu.with_memory_space_constraint(lax.empty(hbm_land.shape, by_core.dtype), pltpu.HBM))
    return pl.pallas_call(
        wait, name=name + "_wait", in_specs=[HBM, HBM, SEM, SEM, ANY], out_specs=[HBM, HBM],
        out_shape=[hbm_src, hbm_land], input_output_aliases={0: 0, 1: 1}, compiler_params=params,
    )(src, land, send_sems, recv_sems, meanwhile(src))


def _pair_add(by_core, landed, name):
    q, _, r, c = by_core.shape
    tc = _tile(c, (512, 256, 128))

    def body(a_ref, b_ref, o_ref):
        mine = a_ref[0, lax.axis_index("c")]
        o_ref[0] = (mine.astype(F32) + b_ref[0].astype(F32)).astype(BF16)

    blk = pl.BlockSpec((1, r, tc), lambda i, j: (i, 0, j))
    return pl.pallas_call(
        body, name=name, grid=(q, c // tc),
        in_specs=[pl.BlockSpec((1, 2, r, tc), lambda i, j: (i, 0, 0, j)), blk], out_specs=blk,
        out_shape=jax.ShapeDtypeStruct(landed.shape, BF16), compiler_params=_cparams("parallel", "parallel"),
    )(by_core, landed)


def _chip_copy(src_ref, land_ref, send_sems, recv_sems, j, sending):
    x, y, c = _place()
    fx, fy, _ = FLIPS[OTHER_CHIPS[j]]
    here, there = 2 * x + y, 2 * (x ^ fx) + (y ^ fy)
    return pltpu.make_async_remote_copy(
        src_ref=src_ref.at[there], dst_ref=land_ref.at[here if sending else there],
        send_sem=send_sems.at[j], recv_sem=recv_sems.at[j],
        device_id=(x ^ fx, y ^ fy, c), device_id_type=MESH)


def _chip_wait(send_sems, recv_sems, src, land, after, name):
    def body(src_ref, land_ref, send_ref, recv_ref, after_ref, src_out, land_out):
        for j in range(len(OTHER_CHIPS)):
            cp = _chip_copy(src_ref, land_ref, send_ref, recv_ref, j, False)
            cp.wait_send()
            cp.wait_recv()

    hbm = lambda a: pltpu.HBM(a.shape, a.dtype)
    return pl.pallas_call(
        body,
        name=name,
        in_specs=[HBM, HBM, SEM, SEM, ANY],
        out_specs=[HBM, HBM],
        out_shape=[hbm(src), hbm(land)],
        input_output_aliases={0: 0, 1: 1},
        compiler_params=pltpu.CompilerParams(has_side_effects=EFFECT),
    )(src, land, send_sems, recv_sems, after)


def _col(v):
    return v.reshape(-1, 1).astype(F32)


def _local_step(x, tgt, started, weight, small, pair_sums, handles):
    t = x.shape[0]
    n1 = _col(small["norm1_w"])
    n2 = _col(small["norm2_w"])
    nf = _col(small["final_norm_w"])
    bg = _col(small["b_gate"])
    sinks = small["attn_sinks"].reshape(-1).astype(F32)
    cbias = _col(small["ssd_conv_b"])
    dtb = _col(small["dt_bias"])
    alog = _col(small["a_log"])
    dsk = _col(small["d_skip"])
    gnw = _col(small["ssd_norm_w"])
    fb = small["ffn_conv_b"].reshape(2, D_FF, 1)

    xt, xn = _norm_fwd_tokens(x, n1, started, "norm1_fwd")
    cw = weight("ssd_conv_w", xn).T
    fw = weight("ffn_conv_w", xn).T.reshape(2, D_FF, FFN_CONV)
    w_in_t = weight("w_in", xn)
    proj = _matmul(w_in_t, xn, nt=False, out_dtype=F32, name="mm_in")
    ao, lse = _attn_fwd(proj, sinks)
    w_ao = weight("w_attn_o", ao)
    attn = _matmul(w_ao, ao, nt=False, out_dtype=F32, name="mm_attn_o", tn_a=True)
    xbc = _conv_silu_fwd(proj, cw, cbias)
    y, hst = _ssd_fwd(xbc, proj, dtb, alog, dsk)
    yn = _gnorm_fwd(y, proj, gnw)
    w_so = weight("w_ssd_o", yn)
    ssd = _matmul(w_so, yn, nt=False, out_dtype=F32, name="mm_ssd_o", tn_a=True)
    mix = _gate_fwd(proj, bg, attn, ssd)
    w_out = weight("w_out", mix)
    h1 = _matmul(w_out, mix, nt=False, out_dtype=F32, name="mm_out", add=xt, tn_a=True)
    hn = _norm_fwd(h1, n2, "norm2_fwd")
    w_up_t = weight("w_up", hn)
    u0 = _matmul(w_up_t, hn, nt=False, out_dtype=F32, name="mm_up").reshape(2, D_FF, t)
    gl = _ffn_fwd(u0, fw, fb)
    w_down = weight("w_down", gl)
    h2 = _matmul(w_down, gl, nt=False, out_dtype=F32, name="mm_down", add=h1, tn_a=True)
    dh2, loss, d_nf = _final_norm_loss(h2, tgt, nf)

    g = {}

    def sending(weight_name, grad, fn, *args, **kwargs):
        chunks = grad if grad.ndim == 3 else grad.reshape(N_DEV, -1, D_MODEL)
        out, handles[weight_name] = fn(*args, send=chunks, **kwargs)
        return out

    g_down = _matmul(gl, dh2, nt=True, out_dtype=BF16, name="mm_d_w_down")
    dgl = _matmul(w_down, dh2, nt=False, out_dtype=F32, name="mm_d_glu")
    du0, d_fwb = sending("w_down", g_down, _ffn_bwd, u0, fw, fb, dgl)
    du0 = du0.reshape(2 * D_FF, t)
    g_up = _matmul(du0, hn, nt=True, out_dtype=BF16, name="mm_d_w_up")
    dhn = sending("w_up", g_up, _matmul, w_up_t, du0, nt=False, out_dtype=F32, name="mm_d_hn", tn_a=True)
    dh1, d_n2 = _norm_bwd(dhn, h1, n2, dh2, "norm2_bwd")
    g_out = _matmul(mix, dh1, nt=True, out_dtype=BF16, name="mm_d_w_out")
    dmix = _matmul(w_out, dh1, nt=False, out_dtype=F32, name="mm_d_mix")
    d_attn, d_ssd, d_ga, d_gs, d_ba, d_bs = sending("w_out", g_out, _gate_bwd, proj, bg, attn, ssd, dmix)
    g_ao = _matmul(ao, d_attn, nt=True, out_dtype=BF16, name="mm_d_w_attn_o")
    dao = _matmul(w_ao, d_attn, nt=False, out_dtype=F32, name="mm_d_ao")
    g_so = _matmul(yn, d_ssd, nt=True, out_dtype=BF16, name="mm_d_w_ssd_o")
    dyn = _matmul(w_so, d_ssd, nt=False, out_dtype=F32, name="mm_d_yn")
    dy, dproj, d_gnw = sending("w_ssd_o", g_so, _gnorm_bwd, dyn, y, proj, gnw)
    dxs, dbm, dcm, ddt, d_alog, d_dsk, d_dtb = _ssd_bwd(xbc, proj, dtb, alog, dsk, hst, dy)
    dproj, dwb_xs = _conv_silu_bwd(proj, cw, cbias, dxs, 0, dproj, "ssd_conv_bwd_x")
    dproj, dwb_b = _conv_silu_bwd(proj, cw, cbias, dbm, D_INNER, dproj, "ssd_conv_bwd_b")
    dproj, dwb_c = _conv_silu_bwd(proj, cw, cbias, dcm, D_INNER + BC_DIM, dproj, "ssd_conv_bwd_c")
    dwb_conv = jnp.concatenate([dwb_xs, dwb_b, dwb_c], axis=0)
    dproj, d_sinks = sending("w_attn_o", g_ao, _attn_bwd, proj, sinks, ao, lse, dao, dproj)
    for rows, part in ((OFF_DT, ddt.astype(BF16)), (OFF_GA, d_ga), (OFF_GS, d_gs)):
        dproj = lax.dynamic_update_slice(dproj, part, (rows, 0))
    g_in = pair_sums(_matmul(dproj, xn, nt=True, out_dtype=BF16, name="mm_d_w_in"))
    dxn = sending("w_in", g_in, _matmul, w_in_t, dproj, nt=False, out_dtype=F32, name="mm_d_xn", tn_a=True)
    dx, d_n1 = _norm_bwd(dxn, xt, n1, dh1, "norm1_bwd", tokens_out=True)

    g["norm1_w"] = d_n1
    g["b_gate"] = jnp.concatenate([d_ba, d_bs], axis=0)
    g["attn_sinks"] = d_sinks
    g["ssd_conv_w"] = dwb_conv[:, :SSD_CONV].T
    g["ssd_conv_b"] = dwb_conv[:, SSD_CONV]
    g["dt_bias"] = d_dtb
    g["a_log"] = d_alog
    g["d_skip"] = d_dsk
    g["ssd_norm_w"] = d_gnw
    g["norm2_w"] = d_n2
    d_fwb = d_fwb.reshape(2 * D_FF, 128)
    g["ffn_conv_w"] = d_fwb[:, :FFN_CONV].T
    g["ffn_conv_b"] = d_fwb[:, FFN_CONV]
    g["final_norm_w"] = d_nf
    return loss, dx, g


SMALL = ("norm1_w", "b_gate", "attn_sinks", "ssd_conv_w", "ssd_conv_b", "dt_bias", "a_log", "d_skip", "ssd_norm_w",
         "norm2_w", "ffn_conv_w", "ffn_conv_b", "final_norm_w")
WEIGHT_ORDER = ("norm1_w", "w_in", "b_gate", "attn_sinks", "w_attn_o", "ssd_conv_w", "ssd_conv_b", "dt_bias", "a_log",
                "d_skip", "ssd_norm_w", "w_ssd_o", "w_out", "norm2_w", "w_up", "ffn_conv_w", "ffn_conv_b", "w_down",
                "final_norm_w")


def kernel(x, norm1_w, w_in, b_gate, attn_sinks, w_attn_o, ssd_conv_w, ssd_conv_b, dt_bias, a_log, d_skip, ssd_norm_w, w_ssd_o, w_out, norm2_w, w_up, ffn_conv_w, ffn_conv_b, w_down, final_norm_w, loss_target, m_norm1_w, m_w_in, m_b_gate, m_attn_sinks, m_w_attn_o, m_ssd_conv_w, m_ssd_conv_b, m_dt_bias, m_a_log, m_d_skip, m_ssd_norm_w, m_w_ssd_o, m_w_out, m_norm2_w, m_w_up, m_ffn_conv_w, m_ffn_conv_b, m_w_down, m_final_norm_w, v_norm1_w, v_w_in, v_b_gate, v_attn_sinks, v_w_attn_o, v_ssd_conv_w, v_ssd_conv_b, v_dt_bias, v_a_log, v_d_skip, v_ssd_norm_w, v_w_ssd_o, v_w_out, v_norm2_w, v_w_up, v_ffn_conv_w, v_ffn_conv_b, v_w_down, v_final_norm_w):
    w = dict(norm1_w=norm1_w, w_in=w_in, b_gate=b_gate, attn_sinks=attn_sinks, w_attn_o=w_attn_o, ssd_conv_w=ssd_conv_w, ssd_conv_b=ssd_conv_b, dt_bias=dt_bias, a_log=a_log, d_skip=d_skip, ssd_norm_w=ssd_norm_w, w_ssd_o=w_ssd_o, w_out=w_out, norm2_w=norm2_w, w_up=w_up, ffn_conv_w=ffn_conv_w, ffn_conv_b=ffn_conv_b, w_down=w_down, final_norm_w=final_norm_w)
    m = dict(norm1_w=m_norm1_w, w_in=m_w_in, b_gate=m_b_gate, attn_sinks=m_attn_sinks, w_attn_o=m_w_attn_o, ssd_conv_w=m_ssd_conv_w, ssd_conv_b=m_ssd_conv_b, dt_bias=m_dt_bias, a_log=m_a_log, d_skip=m_d_skip, ssd_norm_w=m_ssd_norm_w, w_ssd_o=m_w_ssd_o, w_out=m_w_out, norm2_w=m_norm2_w, w_up=m_w_up, ffn_conv_w=m_ffn_conv_w, ffn_conv_b=m_ffn_conv_b, w_down=m_w_down, final_norm_w=m_final_norm_w)
    v = dict(norm1_w=v_norm1_w, w_in=v_w_in, b_gate=v_b_gate, attn_sinks=v_attn_sinks, w_attn_o=v_w_attn_o, ssd_conv_w=v_ssd_conv_w, ssd_conv_b=v_ssd_conv_b, dt_bias=v_dt_bias, a_log=v_a_log, d_skip=v_d_skip, ssd_norm_w=v_ssd_norm_w, w_ssd_o=v_w_ssd_o, w_out=v_w_out, norm2_w=v_norm2_w, w_up=v_w_up, ffn_conv_w=v_ffn_conv_w, ffn_conv_b=v_ffn_conv_b, w_down=v_w_down, final_norm_w=v_final_norm_w)
    me = 4 * lax.axis_index("x") + 2 * lax.axis_index("y") + lax.axis_index("c")

    shards = {"ssd_conv_w": ssd_conv_w[0], "ffn_conv_w": ffn_conv_w[0], "w_in": w_in[0].T.astype(BF16),
              "w_attn_o": w_attn_o[0].astype(BF16), "w_ssd_o": w_ssd_o[0].astype(BF16), "w_out": w_out[0].astype(BF16),
              "w_up": w_up[0].T.astype(BF16), "w_down": w_down[0].astype(BF16)}
    order = list(shards)
    g_send, g_recv, g_src, g_land = _gather_start(list(shards.values()), "gather_start", True)
    first = ("ssd_conv_w", "ffn_conv_w", "w_in")
    forwarded = {}

    def weight(name, after):
        if name not in forwarded:
            group = [k for k in order if (k in first) == (name in first)]
            idx = [order.index(k) for k in group]
            handles = _gather_forward([g_send[i] for i in idx], [g_recv[i] for i in idx], [g_src[i] for i in idx],
                                      [g_land[i] for i in idx], after, "gather_forward_for_" + name)
            forwarded.update(zip(group, zip(*handles)))
        i = order.index(name)
        src, land = _gather_wait_forwarded(g_send[i], g_recv[i], *forwarded[name], after, "gather_wait_" + name)
        land = _own_slot(src, land, me, True)
        if name == "ssd_conv_w":
            return jnp.transpose(land, (1, 0, 2)).reshape(SSD_CONV, XBC_DIM)
        if name == "ffn_conv_w":
            return jnp.transpose(land, (1, 0, 2)).reshape(FFN_CONV, 2 * D_FF)
        return land.reshape(-1, D_MODEL)

    res, pending = {}, {}

    def update(name, after):
        if name == "w_in":
            parts = _own_slot(*_chip_wait(*pending[name], after, "grad_wait_" + name), me // 2, False)
        else:
            parts = _own_slot(*_exchange_wait(*pending[name], after, False, "grad_wait_" + name), me, False)
        view, back = {
            "w_in": (lambda a: jnp.transpose(a, (2, 0, 1)), lambda r: jnp.transpose(r, (1, 2, 0))),
            "w_up": (lambda a: a[0].T, lambda r: r.T[None]),
        }.get(name, (lambda a: a[0], lambda r: r[None]))
        done = _adamw_sharded(parts, view(w[name]), view(m[name]), view(v[name]), "adamw_" + name)
        res[name] = [back(r) for r in done]
        return done[0]

    def pair_sums(grad):
        by_core, landed = _pair_exchange(grad.reshape(N_CHIPS, 2, -1, D_MODEL),
                                         lambda started: update("w_up", update("w_down", started)), "grad_pair_w_in")
        return _pair_add(by_core, landed, "grad_pair_add_w_in")

    small = {k: w[k][0] if k != "final_norm_w" else w[k] for k in SMALL}
    loss, dx, g = _local_step(x[0], loss_target[0], g_src[0], weight, small, pair_sums, pending)

    packed = _pack_row([loss] + [g[k] for k in SMALL])
    s_send, s_recv, s_src, s_land = _gather_start([packed], "small_grads_start", False)
    after = s_src[0]
    for name in ("w_out", "w_attn_o", "w_ssd_o", "w_in"):
        after = update(name, after)

    rows = _own_slot(*_exchange_wait(s_send[0], s_recv[0], s_src[0], s_land[0], after, True, "small_grads_wait"),
                     me, True)
    flat = lambda a: a.reshape(-1, a.shape[-1])
    loss_sum, updates = _small_update(
        rows, me.reshape(1), [g[k].size for k in SMALL],
        [flat(w[k]) for k in SMALL], [flat(m[k]) for k in SMALL], [flat(v[k]) for k in SMALL])
    for k, upd in zip(SMALL, updates):
        res[k] = [u.reshape(w[k].shape) for u in upd]

    grad_x = dx[None]
    outs = [loss_sum.reshape(()), grad_x]
    for i in range(4):
        outs.extend(res[k][i] for k in WEIGHT_ORDER)
    return tuple(outs)
```

```python
import jax
import jax.numpy as jnp
from jax import lax
from jax.experimental import pallas as pl
from jax.experimental.pallas import tpu as pltpu

F32 = jnp.float32
BF16 = jnp.bfloat16
HIGHEST = lax.Precision.HIGHEST

D_MODEL = 1024
N_Q_HEADS = 16
N_KV_HEADS = 4
HEAD_DIM = 64
WINDOW = 128
Q_PER_KV = N_Q_HEADS // N_KV_HEADS
Q_DIM = N_Q_HEADS * HEAD_DIM
KV_DIM = N_KV_HEADS * HEAD_DIM
D_INNER = 2048
SSD_HEAD_DIM = 64
N_SSD_HEADS = 32
N_SSD_GROUPS = 4
HEADS_PER_GROUP = N_SSD_HEADS // N_SSD_GROUPS
D_STATE = 128
BC_DIM = N_SSD_GROUPS * D_STATE
XBC_DIM = D_INNER + 2 * BC_DIM
SSD_CONV = 4
CHUNK = 128
D_FF = 2816
FFN_CONV = 3
EPS = 1e-5
NEG = -1e30
IN_DIM = 8736
N_DEV = 8

OFF_Q = 0
OFF_K = OFF_Q + Q_DIM
OFF_V = OFF_K + KV_DIM
OFF_Z = OFF_V + KV_DIM
OFF_X = OFF_Z + D_INNER
OFF_DT = OFF_X + XBC_DIM
OFF_GA = OFF_DT + N_SSD_HEADS
OFF_GS = OFF_GA + D_MODEL

ADAM_LR = 0.001
ADAM_B1 = 0.9
ADAM_B2 = 0.999
ADAM_EPS = 1e-08
ADAM_WD = 0.01
ADAM_STEP = 10

LANES = 128
BF16_TILE_ROWS = 16
VMEM_BYTES = 64 * 1024 * 1024
VMEM_LIMIT = VMEM_BYTES * 3 // 4
MESH = pl.DeviceIdType.MESH


def _cparams(*sem):
    return pltpu.CompilerParams(dimension_semantics=sem, vmem_limit_bytes=VMEM_LIMIT)


def _tile(n, prefs):
    for p in prefs:
        if n % p == 0:
            return p
    return n


def _sigmoid(x):
    return 1.0 / (1.0 + jnp.exp(-x))


def _softplus(x):
    return jnp.maximum(x, 0.0) + jnp.log(1.0 + jnp.exp(-jnp.abs(x)))


def _rowsum(x):
    return jnp.sum(x, axis=1, keepdims=True)


def _colsum(x):
    return jnp.sum(x, axis=0, keepdims=True)


def _dot(a, b):
    return jnp.dot(a, b, preferred_element_type=F32)


def _dot_nt(a, b):
    return lax.dot_general(a, b, (((1,), (1,)), ((), ())), preferred_element_type=F32)


def _dot_tn(a, b):
    return lax.dot_general(a, b, (((0,), (0,)), ((), ())), preferred_element_type=F32)


def _shift_right(x, j):
    if j == 0:
        return x
    r = pltpu.roll(x, j, 1)
    lane = lax.broadcasted_iota(jnp.int32, (x.shape[0], 128), 1)
    return jnp.concatenate([jnp.where(lane >= j, r[:, :128], 0.0), r[:, 128:]], axis=1)


def _shift_left(x, j):
    if j == 0:
        return x
    n = x.shape[1]
    r = pltpu.roll(x, n - j, 1)
    lane = lax.broadcasted_iota(jnp.int32, (x.shape[0], 128), 1)
    return jnp.concatenate([r[:, :n - 128], jnp.where(lane < 128 - j, r[:, n - 128:], 0.0)], axis=1)


def _causal_conv(xv, wv, bv):
    taps = wv.shape[1]
    shifted = [_shift_right(xv, taps - 1 - k) for k in range(taps - 1)]
    y = bv + wv[:, taps - 1:taps] * xv
    for k in range(taps - 1):
        y = y + wv[:, k:k + 1] * shifted[k]
    return y, shifted


def _causal_conv_bwd(dy, xv, shifted, wv):
    taps = wv.shape[1]
    lane = lax.broadcasted_iota(jnp.int32, (dy.shape[0], 128), 1)
    dwb = jnp.where(lane == taps, _rowsum(dy), 0.0)
    dwb = jnp.where(lane == taps - 1, _rowsum(dy * xv), dwb)
    dx = wv[:, taps - 1:taps] * dy
    for k in range(taps - 1):
        dx = dx + wv[:, k:k + 1] * _shift_left(dy, taps - 1 - k)
        dwb = jnp.where(lane == k, _rowsum(dy * shifted[k]), dwb)
    return dx, dwb


def _call(body, *, name, grid, in_specs, out_specs, out_shape, args, semantics, scratch_shapes=(), aliases=None,
          send=None):
    aliases = dict(aliases or {})
    if send is None:
        return pl.pallas_call(body, name=name, grid=grid, in_specs=in_specs, out_specs=out_specs, out_shape=out_shape,
                              scratch_shapes=list(scratch_shapes), input_output_aliases=aliases,
                              compiler_params=_cparams(*semantics))(*args)
    single = not isinstance(out_specs, (list, tuple))
    out_specs, out_shape = ([out_specs], [out_shape]) if single else (list(out_specs), list(out_shape))
    n_in, n_out = len(in_specs), len(out_specs)
    chips = send.shape[0] == N_DEV // 2
    n_copies = len(OTHER_CHIPS) if chips else N_DEV - 1

    def sending(*refs):
        ins, (src_ref, land_ref) = refs[:n_in], refs[n_in:n_in + 2]
        outs = refs[n_in + 2:n_in + 2 + n_out]
        send_sems, recv_sems = refs[n_in + 2 + n_out:n_in + 4 + n_out]
        scratch = refs[n_in + 6 + n_out:]
        step = 0
        for axis, size in enumerate(grid):
            step = step * size + pl.program_id(axis)

        @pl.when(step == 0)
        def _():
            for k in range(n_copies):
                if chips:
                    _chip_copy(src_ref, land_ref, send_sems, recv_sems, k, True).start()
                else:
                    _peer_copy(False, src_ref, land_ref, send_sems, recv_sems, k, True).start()

        body(*ins, *outs, *scratch)

    sem = pltpu.SemaphoreType.DMA((n_copies,))
    hbm = pltpu.HBM(send.shape, send.dtype)
    res = pl.pallas_call(
        sending, name=name, grid=grid,
        in_specs=list(in_specs) + [HBM, HBM],
        out_specs=out_specs + [SEM, SEM, HBM, HBM],
        out_shape=out_shape + [sem, sem, hbm, hbm],
        input_output_aliases={**aliases, n_in: n_out + 2, n_in + 1: n_out + 3},
        scratch_shapes=list(scratch_shapes),
        compiler_params=pltpu.CompilerParams(dimension_semantics=("arbitrary",) * len(grid), vmem_limit_bytes=VMEM_LIMIT,
                                             has_side_effects=EFFECT),
    )(*args, pltpu.with_memory_space_constraint(send, pltpu.HBM),
      pltpu.with_memory_space_constraint(lax.empty(send.shape, send.dtype), pltpu.HBM))
    return (res[0] if single else list(res[:n_out])), tuple(res[n_out:])


BLOCK_VMEM_BUDGET = VMEM_LIMIT * 3 // 4
MATMUL_MAX_TM = 768
MATMUL_MAX_TN = 3072
MATMUL_MAX_TK = 3072


def _largest_tile(n, align, cap):
    return max(d for d in range(align, min(n, cap) + 1, align) if n % d == 0)


def _matmul_tiles(m, n, k, a_bytes, b_bytes, out_bytes, has_add, m_align, k_align):
    tm = _largest_tile(m, m_align, MATMUL_MAX_TM)
    tk = _largest_tile(k, k_align, MATMUL_MAX_TK)
    for tn in sorted({d for d in range(LANES, min(n, MATMUL_MAX_TN) + 1, LANES) if n % d == 0}, reverse=True):
        need = 2 * (tm * tk * a_bytes + tk * tn * b_bytes) + tm * tn * (2 * out_bytes + (4 if k > tk else 0) + (8 if has_add else 0))
        if need <= BLOCK_VMEM_BUDGET:
            return tm, tn, tk
    return tm, LANES, tk


def _matmul(a, b, *, nt, out_dtype, name, add=None, tn_a=False, send=None):
    if tn_a:
        k, m = a.shape
    else:
        m, k = a.shape
    n = b.shape[0] if nt else b.shape[1]
    tm, tn, tk = _matmul_tiles(m, n, k, a.dtype.itemsize, b.dtype.itemsize, jnp.dtype(out_dtype).itemsize, add is not None,
                               LANES if tn_a else BF16_TILE_ROWS, BF16_TILE_ROWS if tn_a and not nt else LANES)
    nk = k // tk
    grid = (m // tm, n // tn, nk)

    def body(a_ref, b_ref, *rest):
        r_ref = None
        if add is not None:
            r_ref, rest = rest[0], rest[1:]
        o_ref = rest[0]
        av = a_ref[...].astype(BF16)
        bv = b_ref[...].astype(BF16)
        part = _dot_tn(av, bv) if tn_a else _dot_nt(av, bv) if nt else _dot(av, bv)

        def finish(r):
            if add is not None:
                r = r + r_ref[...]
            o_ref[...] = r.astype(out_dtype)

        if nk == 1:
            finish(part)
            return
        acc = rest[1]
        kk = pl.program_id(2)

        @pl.when(kk == 0)
        def _():
            acc[...] = part

        @pl.when((kk > 0) & (kk < nk - 1))
        def _():
            acc[...] += part

        @pl.when(kk == nk - 1)
        def _():
            finish(acc[...] + part)

    in_specs = [
        pl.BlockSpec((tk, tm), lambda i, j, kk: (kk, i)) if tn_a else pl.BlockSpec((tm, tk), lambda i, j, kk: (i, kk)),
        pl.BlockSpec((tn, tk), lambda i, j, kk: (j, kk)) if nt else pl.BlockSpec((tk, tn), lambda i, j, kk: (kk, j)),
    ]
    args = [a, b]
    if add is not None:
        in_specs.append(pl.BlockSpec((tm, tn), lambda i, j, kk: (i, j)))
        args.append(add)
    return _call(
        body, name=name, grid=grid, in_specs=in_specs, args=args,
        out_specs=pl.BlockSpec((tm, tn), lambda i, j, kk: (i, j)),
        out_shape=jax.ShapeDtypeStruct((m, n), out_dtype),
        scratch_shapes=[pltpu.VMEM((tm, tn), F32)] if nk > 1 else [],
        semantics=("parallel", "parallel", "arbitrary"), send=send)


def _norm_fwd(x, w_col, name):
    f, t = x.shape
    tt = _tile(t, (512, 256, 128))

    def body(x_ref, w_ref, o_ref):
        xv = x_ref[...]
        r = lax.rsqrt(jnp.mean(xv * xv, axis=0, keepdims=True) + EPS)
        o_ref[...] = (xv * r * w_ref[...]).astype(BF16)

    return pl.pallas_call(
        body,
        name=name,
        grid=(t // tt,),
        in_specs=[pl.BlockSpec((f, tt), lambda i: (0, i)), pl.BlockSpec((f, 1), lambda i: (0, 0))],
        out_specs=pl.BlockSpec((f, tt), lambda i: (0, i)),
        out_shape=jax.ShapeDtypeStruct((f, t), BF16),
        compiler_params=_cparams("parallel"),
    )(x, w_col)


def _norm_fwd_tokens(x, w_col, after, name):
    t, f = x.shape
    tt = _tile(t, (512, 256, 128))

    def body(x_ref, w_ref, after_ref, xt_ref, o_ref):
        xv = x_ref[...].T
        xt_ref[...] = xv
        r = lax.rsqrt(jnp.mean(xv * xv, axis=0, keepdims=True) + EPS)
        o_ref[...] = (xv * r * w_ref[...]).astype(BF16)

    blk = pl.BlockSpec((f, tt), lambda i: (0, i))
    return pl.pallas_call(
        body,
        name=name,
        grid=(t // tt,),
        in_specs=[pl.BlockSpec((tt, f), lambda i: (i, 0)), pl.BlockSpec((f, 1), lambda i: (0, 0)), ANY],
        out_specs=[blk, blk],
        out_shape=[jax.ShapeDtypeStruct((f, t), F32), jax.ShapeDtypeStruct((f, t), BF16)],
        compiler_params=_cparams("parallel"),
    )(x, w_col, after)


def _norm_bwd(dy, x, w_col, res, name, tokens_out=False):
    f, t = x.shape
    tt = _tile(t, (512, 256, 128))

    def body(dy_ref, x_ref, w_ref, res_ref, dx_ref, dw_ref):
        @pl.when(pl.program_id(0) == 0)
        def _():
            dw_ref[...] = jnp.zeros_like(dw_ref)

        xv = x_ref[...]
        r = lax.rsqrt(jnp.mean(xv * xv, axis=0, keepdims=True) + EPS)
        xhat = xv * r
        dyv = dy_ref[...]
        dw_ref[...] += _rowsum(dyv * xhat)
        dxhat = dyv * w_ref[...]
        dx = res_ref[...] + r * (dxhat - xhat * jnp.mean(dxhat * xhat, axis=0, keepdims=True))
        dx_ref[...] = dx.T if tokens_out else dx

    blk = pl.BlockSpec((f, tt), lambda i: (0, i))
    col = pl.BlockSpec((f, 1), lambda i: (0, 0))
    return pl.pallas_call(
        body,
        name=name,
        grid=(t // tt,),
        in_specs=[blk, blk, col, blk],
        out_specs=[pl.BlockSpec((tt, f), lambda i: (i, 0)) if tokens_out else blk, col],
        out_shape=[jax.ShapeDtypeStruct((t, f) if tokens_out else (f, t), F32), jax.ShapeDtypeStruct((f, 1), F32)],
        compiler_params=_cparams("arbitrary"),
    )(dy, x, w_col, res)


def _final_norm_loss(h, tgt, w_col):
    f, t = h.shape
    tt = _tile(t, (512, 256, 128))

    def body(h_ref, t_ref, w_ref, dh_ref, loss_ref, dw_ref):
        @pl.when(pl.program_id(0) == 0)
        def _():
            dw_ref[...] = jnp.zeros_like(dw_ref)
            loss_ref[...] = jnp.zeros_like(loss_ref)

        xv = h_ref[...]
        r = lax.rsqrt(jnp.mean(xv * xv, axis=0, keepdims=True) + EPS)
        xhat = xv * r
        wv = w_ref[...]
        err = xhat * wv - t_ref[...].T
        loss_ref[...] += 0.5 * _rowsum(jnp.mean(err * err, axis=0, keepdims=True))
        dyv = err * (1.0 / f)
        dw_ref[...] += _rowsum(dyv * xhat)
        dxhat = dyv * wv
        dh_ref[...] = r * (dxhat - xhat * jnp.mean(dxhat * xhat, axis=0, keepdims=True))

    blk = pl.BlockSpec((f, tt), lambda i: (0, i))
    col = pl.BlockSpec((f, 1), lambda i: (0, 0))
    one = pl.BlockSpec((1, 1), lambda i: (0, 0))
    return pl.pallas_call(
        body,
        name="final_norm_loss",
        grid=(t // tt,),
        in_specs=[blk, pl.BlockSpec((tt, f), lambda i: (i, 0)), col],
        out_specs=[blk, one, col],
        out_shape=[jax.ShapeDtypeStruct((f, t), F32), jax.ShapeDtypeStruct((1, 1), F32), jax.ShapeDtypeStruct((f, 1), F32)],
        compiler_params=_cparams("arbitrary"),
    )(h, tgt, w_col)


def _attn_mask(n):
    shape = (2 * WINDOW, Q_PER_KV * WINDOW)
    si = lax.broadcasted_iota(jnp.int32, shape, 0)
    qi = lax.broadcasted_iota(jnp.int32, shape, 1) & (WINDOW - 1)
    dist = WINDOW + qi - si
    return (dist >= 0) & (dist < WINDOW) & ((si >= WINDOW) | (n > 0))


def _lane_cat(ref, row0, rows):
    return jnp.concatenate([ref[row0 + i * rows:row0 + (i + 1) * rows, :] for i in range(Q_PER_KV)], axis=1)


def _attn_fwd(proj, sinks):
    t = proj.shape[1]
    nb = t // WINDOW
    scale = HEAD_DIM ** -0.5

    def body(s_ref, q_ref, kc_ref, kp_ref, vc_ref, vp_ref, o_ref, lse_ref):
        n = pl.program_id(0)
        valid = _attn_mask(n)
        for g in range(N_KV_HEADS):
            rows = slice(g * HEAD_DIM, (g + 1) * HEAD_DIM)
            kt = jnp.concatenate([kp_ref[rows, :], kc_ref[rows, :]], axis=1).astype(BF16)
            vt = jnp.concatenate([vp_ref[rows, :], vc_ref[rows, :]], axis=1).astype(BF16)
            qcat = (_lane_cat(q_ref, g * Q_PER_KV * HEAD_DIM, HEAD_DIM) * scale).astype(BF16)
            s = jnp.where(valid, _dot_tn(kt, qcat), NEG)
            sink = jnp.concatenate(
                [jnp.full((1, WINDOW), s_ref[g * Q_PER_KV + i], F32) for i in range(Q_PER_KV)], axis=1)
            m = jnp.maximum(jnp.max(s, axis=0, keepdims=True), sink)
            p = jnp.exp(s - m)
            denom = _colsum(p) + jnp.exp(sink - m)
            probs = (p / denom).astype(BF16)
            out = _dot(vt, probs)
            lse = m + jnp.log(denom)
            for i in range(Q_PER_KV):
                h = g * Q_PER_KV + i
                o_ref[h * HEAD_DIM:(h + 1) * HEAD_DIM, :] = out[:, i * WINDOW:(i + 1) * WINDOW]
                lse_ref[h:h + 1, :] = lse[:, i * WINDOW:(i + 1) * WINDOW]

    kb = OFF_K // KV_DIM
    vb = OFF_V // KV_DIM
    prev = lambda n: jnp.maximum(n - 1, 0)
    return pl.pallas_call(
        body,
        name="attn_fwd",
        grid=(nb,),
        in_specs=[
            pl.BlockSpec(memory_space=pltpu.SMEM),
            pl.BlockSpec((Q_DIM, WINDOW), lambda n: (0, n)),
            pl.BlockSpec((KV_DIM, WINDOW), lambda n: (kb, n)),
            pl.BlockSpec((KV_DIM, WINDOW), lambda n: (kb, prev(n))),
            pl.BlockSpec((KV_DIM, WINDOW), lambda n: (vb, n)),
            pl.BlockSpec((KV_DIM, WINDOW), lambda n: (vb, prev(n))),
        ],
        out_specs=[pl.BlockSpec((Q_DIM, WINDOW), lambda n: (0, n)), pl.BlockSpec((N_Q_HEADS, WINDOW), lambda n: (0, n))],
        out_shape=[jax.ShapeDtypeStruct((Q_DIM, t), F32), jax.ShapeDtypeStruct((N_Q_HEADS, t), F32)],
        compiler_params=_cparams("parallel"),
    )(sinks, proj, proj, proj, proj, proj)


def _attn_bwd(proj, sinks, out, lse, dout, dproj, send=None):
    t = proj.shape[1]
    nb = t // WINDOW
    scale = HEAD_DIM ** -0.5

    def body(s_ref, q_ref, kc_ref, kp_ref, vc_ref, vp_ref, o_ref, lse_ref, do_ref, dproj_ref,
             dqkv_ref, ds_ref, dk_carry, dv_carry):
        dq_ref = dqkv_ref.at[pl.ds(OFF_Q, Q_DIM)]
        dk_ref = dqkv_ref.at[pl.ds(OFF_K, KV_DIM)]
        dv_ref = dqkv_ref.at[pl.ds(OFF_V, KV_DIM)]
        step = pl.program_id(0)
        n = nb - 1 - step

        @pl.when(step == 0)
        def _():
            dk_carry[...] = jnp.zeros_like(dk_carry)
            dv_carry[...] = jnp.zeros_like(dv_carry)
            ds_ref[...] = jnp.zeros_like(ds_ref)

        valid = _attn_mask(n)
        for g in range(N_KV_HEADS):
            rows = slice(g * HEAD_DIM, (g + 1) * HEAD_DIM)
            q0 = g * Q_PER_KV * HEAD_DIM
            kt = jnp.concatenate([kp_ref[rows, :], kc_ref[rows, :]], axis=1).astype(BF16)
            vt = jnp.concatenate([vp_ref[rows, :], vc_ref[rows, :]], axis=1).astype(BF16)
            qf = _lane_cat(q_ref, q0, HEAD_DIM)
            qcat = qf.astype(BF16)
            ocat = _lane_cat(o_ref, q0, HEAD_DIM)
            docat = _lane_cat(do_ref, q0, HEAD_DIM)
            dob = docat.astype(BF16)
            lse_cat = jnp.concatenate(
                [lse_ref[g * Q_PER_KV + i:g * Q_PER_KV + i + 1, :] for i in range(Q_PER_KV)], axis=1)
            sink = jnp.concatenate(
                [jnp.full((1, WINDOW), s_ref[g * Q_PER_KV + i], F32) for i in range(Q_PER_KV)], axis=1)
            s = jnp.where(valid, _dot_tn(kt, (qf * scale).astype(BF16)), NEG)
            p = jnp.exp(s - lse_cat)
            dp = _dot_tn(vt, dob)
            delta = _colsum(docat * ocat)
            dsc = (p * (dp - delta)).astype(BF16)
            dsink_row = -jnp.exp(sink - lse_cat) * delta
            dq = _dot(kt, dsc) * scale
            dk = _dot_nt(qcat, dsc) * scale
            dv = _dot_nt(dob, p.astype(BF16))
            for i in range(Q_PER_KV):
                h = g * Q_PER_KV + i
                dq_ref[h * HEAD_DIM:(h + 1) * HEAD_DIM, :] = dq[:, i * WINDOW:(i + 1) * WINDOW].astype(BF16)
                ds_ref[h:h + 1, :] += _rowsum(dsink_row[:, i * WINDOW:(i + 1) * WINDOW])
            dk_ref[rows, :] = (dk[:, WINDOW:] + dk_carry[rows, :]).astype(BF16)
            dv_ref[rows, :] = (dv[:, WINDOW:] + dv_carry[rows, :]).astype(BF16)
            dk_carry[rows, :] = dk[:, :WINDOW]
            dv_carry[rows, :] = dv[:, :WINDOW]

    kb = OFF_K // KV_DIM
    vb = OFF_V // KV_DIM
    cur = lambda i: nb - 1 - i
    prev = lambda i: jnp.maximum(nb - 2 - i, 0)
    qspec = pl.BlockSpec((Q_DIM, WINDOW), lambda i: (0, cur(i)))
    return _call(
        body,
        name="attn_bwd",
        grid=(nb,),
        in_specs=[
            pl.BlockSpec(memory_space=pltpu.SMEM),
            qspec,
            pl.BlockSpec((KV_DIM, WINDOW), lambda i: (kb, cur(i))),
            pl.BlockSpec((KV_DIM, WINDOW), lambda i: (kb, prev(i))),
            pl.BlockSpec((KV_DIM, WINDOW), lambda i: (vb, cur(i))),
            pl.BlockSpec((KV_DIM, WINDOW), lambda i: (vb, prev(i))),
            qspec,
            pl.BlockSpec((N_Q_HEADS, WINDOW), lambda i: (0, cur(i))),
            qspec,
            pl.BlockSpec(memory_space=pl.ANY),
        ],
        out_specs=[pl.BlockSpec((OFF_Z, WINDOW), lambda i: (0, cur(i))), pl.BlockSpec((N_Q_HEADS, 1), lambda i: (0, 0))],
        out_shape=[jax.ShapeDtypeStruct(dproj.shape, BF16), jax.ShapeDtypeStruct((N_Q_HEADS, 1), F32)],
        scratch_shapes=[pltpu.VMEM((KV_DIM, WINDOW), F32), pltpu.VMEM((KV_DIM, WINDOW), F32)],
        aliases={9: 0},
        semantics=("arbitrary",), args=(sinks, proj, proj, proj, proj, proj, out, lse, dout, dproj), send=send)


CONV_ROWS = 256


def _conv_silu_fwd(proj, w_col, b_col):
    t = proj.shape[1]
    r0 = OFF_X // CONV_ROWS

    def body(x_ref, w_ref, b_ref, o_ref):
        y, _ = _causal_conv(x_ref[...], w_ref[...], b_ref[...])
        o_ref[...] = y * _sigmoid(y)

    return pl.pallas_call(
        body,
        name="ssd_conv_fwd",
        grid=(XBC_DIM // CONV_ROWS,),
        in_specs=[
            pl.BlockSpec((CONV_ROWS, t), lambda i: (r0 + i, 0)),
            pl.BlockSpec((CONV_ROWS, SSD_CONV), lambda i: (i, 0)),
            pl.BlockSpec((CONV_ROWS, 1), lambda i: (i, 0)),
        ],
        out_specs=pl.BlockSpec((CONV_ROWS, t), lambda i: (i, 0)),
        out_shape=jax.ShapeDtypeStruct((XBC_DIM, t), F32),
        compiler_params=_cparams("parallel"),
    )(proj, w_col, b_col)


def _conv_silu_bwd(proj, w_col, b_col, dout, dproj):
    t = proj.shape[1]
    p0 = OFF_X // CONV_ROWS

    def body(x_ref, w_ref, b_ref, do_ref, dproj_ref, dx_ref, dwb_ref):
        xv = x_ref[...]
        wv = w_ref[...]
        y, shifted = _causal_conv(xv, wv, b_ref[...])
        sg = _sigmoid(y)
        dy = do_ref[...] * (sg * (1.0 + y * (1.0 - sg)))
        dx, dwb_ref[...] = _causal_conv_bwd(dy, xv, shifted, wv)
        dx_ref[...] = dx.astype(BF16)

    return pl.pallas_call(
        body,
        name="ssd_conv_bwd",
        grid=(XBC_DIM // CONV_ROWS,),
        in_specs=[
            pl.BlockSpec((CONV_ROWS, t), lambda i: (p0 + i, 0)),
            pl.BlockSpec((CONV_ROWS, SSD_CONV), lambda i: (i, 0)),
            pl.BlockSpec((CONV_ROWS, 1), lambda i: (i, 0)),
            pl.BlockSpec((CONV_ROWS, t), lambda i: (i, 0)),
            pl.BlockSpec(memory_space=pl.ANY),
        ],
        out_specs=[pl.BlockSpec((CONV_ROWS, t), lambda i: (p0 + i, 0)), pl.BlockSpec((CONV_ROWS, 128), lambda i: (i, 0))],
        out_shape=[jax.ShapeDtypeStruct(dproj.shape, BF16), jax.ShapeDtypeStruct((XBC_DIM, 128), F32)],
        input_output_aliases={4: 0},
        compiler_params=_cparams("parallel"),
    )(proj, w_col, b_col, dout, dproj)


def _ssd_specs(order):
    xb = D_INNER // BC_DIM
    dtb = OFF_DT // N_SSD_HEADS
    col = pl.BlockSpec((N_SSD_HEADS, 1), lambda c: (0, 0))
    return [
        pl.BlockSpec((D_INNER, CHUNK), lambda c: (0, order(c))),
        pl.BlockSpec((BC_DIM, CHUNK), lambda c: (xb, order(c))),
        pl.BlockSpec((BC_DIM, CHUNK), lambda c: (xb + 1, order(c))),
        pl.BlockSpec((N_SSD_HEADS, CHUNK), lambda c: (dtb, order(c))),
        col, col, col,
    ]


def _ssd_common(dt_ref, dtb_ref, alog_ref):
    z = dt_ref[...] + dtb_ref[...]
    dt = _softplus(z)
    a_neg = -jnp.exp(alog_ref[...])
    d_a = dt * a_neg
    row = lax.broadcasted_iota(jnp.int32, (CHUNK, CHUNK), 0)
    colm = lax.broadcasted_iota(jnp.int32, (CHUNK, CHUNK), 1)
    upper = (row <= colm).astype(F32)
    a_cs = jnp.dot(d_a, upper, precision=HIGHEST, preferred_element_type=F32)
    a_last = _rowsum(d_a)
    return z, dt, a_neg, a_cs, a_last, row >= colm, row == colm


def _decay(a_row, causal):
    a_s = jnp.broadcast_to(a_row, (CHUNK, CHUNK))
    seg = a_s.T - a_s
    return jnp.where(causal, jnp.exp(jnp.where(causal, seg, 0.0)), 0.0)


def _ssd_fwd(xbc, proj, dtb_col, alog_col, dsk_col):
    t = xbc.shape[1]
    nc = t // CHUNK

    def body(xs_ref, b_ref, c_ref, dt_ref, dtb_ref, alog_ref, dsk_ref, y_ref, hst_ref, h_scr):
        @pl.when(pl.program_id(0) == 0)
        def _():
            h_scr[...] = jnp.zeros_like(h_scr)

        _, dt, _, a_cs, a_last, causal, _ = _ssd_common(dt_ref, dtb_ref, alog_ref)
        hst_ref[0] = h_scr[...]
        dsk = dsk_ref[...]
        for g in range(N_SSD_GROUPS):
            grows = slice(g * D_STATE, (g + 1) * D_STATE)
            bb = b_ref[grows, :].astype(BF16)
            cb_ = c_ref[grows, :].astype(BF16)
            cb = _dot_tn(cb_, bb)
            for j in range(g * HEADS_PER_GROUP, (g + 1) * HEADS_PER_GROUP):
                rows = slice(j * SSD_HEAD_DIM, (j + 1) * SSD_HEAD_DIM)
                a = a_cs[j:j + 1, :]
                m = (cb * _decay(a, causal)).astype(BF16)
                xs = xs_ref[rows, :]
                xc = xs * dt[j:j + 1, :]
                hj = h_scr[rows, :]
                y = _dot_nt(xc.astype(BF16), m) + _dot(hj.astype(BF16), cb_) * jnp.exp(a) + dsk[j:j + 1, :] * xs
                y_ref[rows, :] = y
                al = a_last[j:j + 1, :]
                w = jnp.exp(al - a)
                h_scr[rows, :] = jnp.exp(al) * hj + _dot_nt((xc * w).astype(BF16), bb)

    return pl.pallas_call(
        body,
        name="ssd_fwd",
        grid=(nc,),
        in_specs=_ssd_specs(lambda c: c),
        out_specs=[
            pl.BlockSpec((D_INNER, CHUNK), lambda c: (0, c)),
            pl.BlockSpec((1, D_INNER, D_STATE), lambda c: (c, 0, 0)),
        ],
        out_shape=[
            jax.ShapeDtypeStruct((D_INNER, t), F32),
            jax.ShapeDtypeStruct((nc, D_INNER, D_STATE), F32),
        ],
        scratch_shapes=[pltpu.VMEM((D_INNER, D_STATE), F32)],
        compiler_params=_cparams("arbitrary"),
    )(xbc, xbc, xbc, proj, dtb_col, alog_col, dsk_col)


def _ssd_bwd(xbc, proj, dtb_col, alog_col, dsk_col, hst, dy):
    t = xbc.shape[1]
    nc = t // CHUNK
    rev = lambda c: nc - 1 - c

    def body(xs_ref, b_ref, c_ref, dt_ref, dtb_ref, alog_ref, dsk_ref, hst_ref, dy_ref,
             dxbc_ref, ddt_ref, dalog_ref, ddsk_ref, ddtb_ref, dh_scr, da_scr, ddt_scr, dd_scr):
        dxs_ref = dxbc_ref.at[pl.ds(0, D_INNER)]
        db_ref = dxbc_ref.at[pl.ds(D_INNER, BC_DIM)]
        dc_ref = dxbc_ref.at[pl.ds(D_INNER + BC_DIM, BC_DIM)]
        @pl.when(pl.program_id(0) == 0)
        def _():
            dh_scr[...] = jnp.zeros_like(dh_scr)
            dalog_ref[...] = jnp.zeros_like(dalog_ref)
            ddsk_ref[...] = jnp.zeros_like(ddsk_ref)
            ddtb_ref[...] = jnp.zeros_like(ddtb_ref)

        z, dt, a_neg, a_cs, a_last, causal, eye = _ssd_common(dt_ref, dtb_ref, alog_ref)
        dsk = dsk_ref[...]
        last_lane = lax.broadcasted_iota(jnp.int32, (1, CHUNK), 1) == CHUNK - 1
        for g in range(N_SSD_GROUPS):
            grows = slice(g * D_STATE, (g + 1) * D_STATE)
            bb = b_ref[grows, :].astype(BF16)
            cb_ = c_ref[grows, :].astype(BF16)
            cb = _dot_tn(cb_, bb)
            dcb = jnp.zeros((CHUNK, CHUNK), F32)
            dc_acc = jnp.zeros((D_STATE, CHUNK), F32)
            db_acc = jnp.zeros((D_STATE, CHUNK), F32)
            for j in range(g * HEADS_PER_GROUP, (g + 1) * HEADS_PER_GROUP):
                rows = slice(j * SSD_HEAD_DIM, (j + 1) * SSD_HEAD_DIM)
                a = a_cs[j:j + 1, :]
                al = a_last[j:j + 1, :]
                lam = _decay(a, causal)
                mf = cb * lam
                xs = xs_ref[rows, :]
                dtj = dt[j:j + 1, :]
                xc = xs * dtj
                w = jnp.exp(al - a)
                e = jnp.exp(a)
                gam = jnp.exp(al)
                hj = hst_ref[0, rows, :]
                hjb = hj.astype(BF16)
                dyv = dy_ref[rows, :]
                dyb = dyv.astype(BF16)
                dd_scr[j:j + 1, :] = _colsum(dyv * xs)
                gb = (dyv * e).astype(BF16)
                dh_in = _dot_nt(gb, cb_)
                dc_acc = dc_acc + _dot_tn(hjb, gb)
                yoff = _dot(hjb, cb_) * e
                da = _colsum(dyv * yoff)
                dm = _dot_tn(dyb, xc.astype(BF16))
                dxc = _dot(dyb, mf.astype(BF16))
                dcb = dcb + dm * lam
                nmat = dm * mf
                rs = jnp.broadcast_to(_rowsum(nmat), (CHUNK, CHUNK))
                da = da + _colsum(jnp.where(eye, rs, 0.0)) - _colsum(nmat)
                ds = dh_scr[rows, :]
                dsb = ds.astype(BF16)
                t1 = _dot(dsb, bb)
                xcw = xc * w
                dxc = dxc + w * t1
                dww = _colsum(xcw * t1)
                da_l = _rowsum(dww) + _rowsum(_colsum(ds * hj)) * gam
                da = da - dww + jnp.where(last_lane, da_l, 0.0)
                db_acc = db_acc + _dot_tn(dsb, xcw.astype(BF16))
                dh_scr[rows, :] = gam * ds + dh_in
                dxs_ref[rows, :] = dsk[j:j + 1, :] * dyv + dxc * dtj
                da_scr[j:j + 1, :] = da
                ddt_scr[j:j + 1, :] = _colsum(dxc * xs)
            dcbb = dcb.astype(BF16)
            dc_ref[grows, :] = dc_acc + _dot_nt(bb, dcbb)
            db_ref[grows, :] = db_acc + _dot(cb_, dcbb)
        dda = jnp.dot(da_scr[...], causal.astype(F32), precision=HIGHEST, preferred_element_type=F32)
        ddt = ddt_scr[...] + dda * a_neg
        ddt_raw = ddt * _sigmoid(z)
        ddt_ref[...] = ddt_raw
        ddtb_ref[...] += _rowsum(ddt_raw)
        dalog_ref[...] += _rowsum(dda * dt) * a_neg
        ddsk_ref[...] += _rowsum(dd_scr[...])

    col = pl.BlockSpec((N_SSD_HEADS, 1), lambda c: (0, 0))
    xs_spec = pl.BlockSpec((D_INNER, CHUNK), lambda c: (0, rev(c)))
    small = pltpu.VMEM((N_SSD_HEADS, CHUNK), F32)
    return pl.pallas_call(
        body,
        name="ssd_bwd",
        grid=(nc,),
        in_specs=_ssd_specs(rev) + [pl.BlockSpec((1, D_INNER, D_STATE), lambda c: (rev(c), 0, 0)), xs_spec],
        out_specs=[pl.BlockSpec((XBC_DIM, CHUNK), lambda c: (0, rev(c))),
                   pl.BlockSpec((N_SSD_HEADS, CHUNK), lambda c: (0, rev(c))), col, col, col],
        out_shape=[
            jax.ShapeDtypeStruct((XBC_DIM, t), F32),
            jax.ShapeDtypeStruct((N_SSD_HEADS, t), F32),
            jax.ShapeDtypeStruct((N_SSD_HEADS, 1), F32),
            jax.ShapeDtypeStruct((N_SSD_HEADS, 1), F32),
            jax.ShapeDtypeStruct((N_SSD_HEADS, 1), F32),
        ],
        scratch_shapes=[pltpu.VMEM((D_INNER, D_STATE), F32), small, small, small],
        compiler_params=_cparams("arbitrary"),
    )(xbc, xbc, xbc, proj, dtb_col, alog_col, dsk_col, hst, dy)


GN_ROWS = D_INNER // N_SSD_GROUPS


def _gnorm_fwd(y, proj, w_col):
    t = y.shape[1]
    tt = _tile(t, (512, 256, 128))
    z0 = OFF_Z // GN_ROWS

    def body(y_ref, z_ref, w_ref, o_ref):
        zv = z_ref[...]
        u = y_ref[...] * (zv * _sigmoid(zv))
        r = lax.rsqrt(jnp.mean(u * u, axis=0, keepdims=True) + EPS)
        o_ref[...] = (u * r * w_ref[...]).astype(BF16)

    blk = pl.BlockSpec((GN_ROWS, tt), lambda g, i: (g, i))
    return pl.pallas_call(
        body,
        name="gnorm_fwd",
        grid=(N_SSD_GROUPS, t // tt),
        in_specs=[blk, pl.BlockSpec((GN_ROWS, tt), lambda g, i: (z0 + g, i)), pl.BlockSpec((GN_ROWS, 1), lambda g, i: (g, 0))],
        out_specs=blk,
        out_shape=jax.ShapeDtypeStruct((D_INNER, t), BF16),
        compiler_params=_cparams("parallel", "parallel"),
    )(y, proj, w_col)


def _gnorm_bwd(dout, y, proj, w_col, send=None):
    t = y.shape[1]
    tt = _tile(t, (512, 256, 128))
    z0 = OFF_Z // GN_ROWS

    def body(do_ref, y_ref, z_ref, w_ref, dy_ref, dz_ref, dw_ref):
        @pl.when(pl.program_id(1) == 0)
        def _():
            dw_ref[...] = jnp.zeros_like(dw_ref)

        zv = z_ref[...]
        yv = y_ref[...]
        sg = _sigmoid(zv)
        sz = zv * sg
        u = yv * sz
        r = lax.rsqrt(jnp.mean(u * u, axis=0, keepdims=True) + EPS)
        xhat = u * r
        dov = do_ref[...]
        dw_ref[...] += _rowsum(dov * xhat)
        dxhat = dov * w_ref[...]
        du = r * (dxhat - xhat * jnp.mean(dxhat * xhat, axis=0, keepdims=True))
        dy_ref[...] = du * sz
        dz_ref[...] = (du * yv * (sg * (1.0 + zv * (1.0 - sg)))).astype(BF16)

    blk = pl.BlockSpec((GN_ROWS, tt), lambda g, i: (g, i))
    col = pl.BlockSpec((GN_ROWS, 1), lambda g, i: (g, 0))
    return _call(
        body,
        name="gnorm_bwd",
        grid=(N_SSD_GROUPS, t // tt),
        in_specs=[blk, blk, pl.BlockSpec((GN_ROWS, tt), lambda g, i: (z0 + g, i)), col],
        out_specs=[blk, pl.BlockSpec((GN_ROWS, tt), lambda g, i: (z0 + g, i)), col],
        out_shape=[jax.ShapeDtypeStruct((D_INNER, t), F32), jax.ShapeDtypeStruct((IN_DIM, t), BF16),
                   jax.ShapeDtypeStruct((D_INNER, 1), F32)],
        semantics=("parallel", "arbitrary"), args=(dout, y, proj, w_col), send=send)


GATE_ROWS = 128


def _gate_specs(t):
    nr = D_MODEL // GATE_ROWS
    blk = pl.BlockSpec((GATE_ROWS, t), lambda r: (r, 0))
    rows_from = lambda first: pl.BlockSpec(
        (pl.Element(GATE_ROWS), pl.Element(t)), lambda r: (pl.multiple_of(first + GATE_ROWS * r, N_SSD_HEADS), 0))
    return blk, [
        rows_from(OFF_GA),
        rows_from(OFF_GS),
        pl.BlockSpec((GATE_ROWS, 1), lambda r: (r, 0)),
        pl.BlockSpec((GATE_ROWS, 1), lambda r: (nr + r, 0)),
        blk, blk,
    ]


def _gate_fwd(proj, b_col, attn, ssd):
    t = proj.shape[1]
    blk, specs = _gate_specs(t)

    def body(ga_ref, gs_ref, ba_ref, bs_ref, a_ref, s_ref, o_ref):
        o_ref[...] = (_sigmoid(ga_ref[...] + ba_ref[...]) * a_ref[...]
                      + _sigmoid(gs_ref[...] + bs_ref[...]) * s_ref[...]).astype(BF16)

    return pl.pallas_call(
        body,
        name="gate_fwd",
        grid=(D_MODEL // GATE_ROWS,),
        in_specs=specs,
        out_specs=blk,
        out_shape=jax.ShapeDtypeStruct((D_MODEL, t), BF16),
        compiler_params=_cparams("parallel"),
    )(proj, proj, b_col, b_col, attn, ssd)


def _gate_bwd(proj, b_col, attn, ssd, dmix, send=None):
    t = proj.shape[1]
    blk, specs = _gate_specs(t)

    def body(ga_ref, gs_ref, ba_ref, bs_ref, a_ref, s_ref, dm_ref, da_ref, dso_ref, dga_ref, dgs_ref, dba_ref, dbs_ref):
        dm = dm_ref[...]
        sa = _sigmoid(ga_ref[...] + ba_ref[...])
        ss = _sigmoid(gs_ref[...] + bs_ref[...])
        da_ref[...] = (dm * sa).astype(BF16)
        dso_ref[...] = (dm * ss).astype(BF16)
        dga = dm * a_ref[...] * sa * (1.0 - sa)
        dgs = dm * s_ref[...] * ss * (1.0 - ss)
        dga_ref[...] = dga.astype(BF16)
        dgs_ref[...] = dgs.astype(BF16)
        dba_ref[...] = _rowsum(dga)
        dbs_ref[...] = _rowsum(dgs)

    col = pl.BlockSpec((GATE_ROWS, 1), lambda r: (r, 0))
    act = jax.ShapeDtypeStruct((D_MODEL, t), BF16)
    bias = jax.ShapeDtypeStruct((D_MODEL, 1), F32)
    return _call(
        body,
        name="gate_bwd",
        grid=(D_MODEL // GATE_ROWS,),
        in_specs=specs + [blk],
        out_specs=[blk, blk, blk, blk, col, col],
        out_shape=[act, act, act, act, bias, bias],
        semantics=("parallel",), args=(proj, proj, b_col, b_col, attn, ssd, dmix), send=send)


FFN_ROWS = 256


def _ffn_fwd(u0, w_col, b_col):
    t = u0.shape[2]

    def body(u_ref, w_ref, b_ref, o_ref):
        val, _ = _causal_conv(u_ref[0], w_ref[0], b_ref[0])
        gt, _ = _causal_conv(u_ref[1], w_ref[1], b_ref[1])
        o_ref[...] = (gt * _sigmoid(gt) * val).astype(BF16)

    return pl.pallas_call(
        body,
        name="ffn_fwd",
        grid=(D_FF // FFN_ROWS,),
        in_specs=[
            pl.BlockSpec((2, FFN_ROWS, t), lambda i: (0, i, 0)),
            pl.BlockSpec((2, FFN_ROWS, FFN_CONV), lambda i: (0, i, 0)),
            pl.BlockSpec((2, FFN_ROWS, 1), lambda i: (0, i, 0)),
        ],
        out_specs=pl.BlockSpec((FFN_ROWS, t), lambda i: (i, 0)),
        out_shape=jax.ShapeDtypeStruct((D_FF, t), BF16),
        compiler_params=_cparams("parallel"),
    )(u0, w_col, b_col)


def _ffn_bwd(u0, w_col, b_col, dg, send=None):
    t = u0.shape[2]

    def body(u_ref, w_ref, b_ref, dg_ref, du_ref, dwb_ref):
        xval, wval = u_ref[0], w_ref[0]
        xgt, wgt = u_ref[1], w_ref[1]
        val, sh_val = _causal_conv(xval, wval, b_ref[0])
        gt, sh_gt = _causal_conv(xgt, wgt, b_ref[1])
        sg = _sigmoid(gt)
        dgv = dg_ref[...]
        dval = dgv * (gt * sg)
        dgt = dgv * val * (sg * (1.0 + gt * (1.0 - sg)))
        dx, dwb_ref[0] = _causal_conv_bwd(dval, xval, sh_val, wval)
        du_ref[0] = dx.astype(BF16)
        dx, dwb_ref[1] = _causal_conv_bwd(dgt, xgt, sh_gt, wgt)
        du_ref[1] = dx.astype(BF16)

    return _call(
        body,
        name="ffn_bwd",
        grid=(D_FF // FFN_ROWS,),
        in_specs=[
            pl.BlockSpec((2, FFN_ROWS, t), lambda i: (0, i, 0)),
            pl.BlockSpec((2, FFN_ROWS, FFN_CONV), lambda i: (0, i, 0)),
            pl.BlockSpec((2, FFN_ROWS, 1), lambda i: (0, i, 0)),
            pl.BlockSpec((FFN_ROWS, t), lambda i: (i, 0)),
        ],
        out_specs=[pl.BlockSpec((2, FFN_ROWS, t), lambda i: (0, i, 0)), pl.BlockSpec((2, FFN_ROWS, 128), lambda i: (0, i, 0))],
        out_shape=[jax.ShapeDtypeStruct((2, D_FF, t), BF16), jax.ShapeDtypeStruct((2, D_FF, 128), F32)],
        semantics=("parallel",), args=(u0, w_col, b_col, dg), send=send)


def _adamw_math(w, g, m, v):
    m = ADAM_B1 * m + (1.0 - ADAM_B1) * g
    v = ADAM_B2 * v + (1.0 - ADAM_B2) * (g * g)
    m_hat = m / (1.0 - ADAM_B1 ** ADAM_STEP)
    v_hat = v / (1.0 - ADAM_B2 ** ADAM_STEP)
    delta = -ADAM_LR * (m_hat / (jnp.sqrt(v_hat) + ADAM_EPS) + ADAM_WD * w)
    return delta, m, v


def _adamw_sharded(parts, w, m, v, name):
    r, c = w.shape[0], w.shape[-1]
    slots = parts.shape[0]
    per_lane = 2 * r * (slots * parts.dtype.itemsize + 7 * w.dtype.itemsize)
    tc = max(d for d in range(LANES, c + 1, LANES) if c % d == 0 and (d * per_lane <= BLOCK_VMEM_BUDGET or d == LANES))
    blk_shape = (r, tc) if w.ndim == 2 else (r, 1, tc)

    def body(p_ref, w_ref, m_ref, v_ref, g_ref, d_ref, nm_ref, nv_ref):
        g = p_ref[0].astype(F32)
        for s in range(1, slots):
            g = g + p_ref[s].astype(F32)
        flat = lambda ref: ref[...].reshape(r, tc)
        d, nm, nv = _adamw_math(flat(w_ref), g, flat(m_ref), flat(v_ref))
        for ref, val in ((g_ref, g), (d_ref, d), (nm_ref, nm), (nv_ref, nv)):
            ref[...] = val.reshape(blk_shape)

    blk = pl.BlockSpec(blk_shape, (lambda i: (0, i)) if w.ndim == 2 else (lambda i: (0, 0, i)))
    out = jax.ShapeDtypeStruct(w.shape, F32)
    return pl.pallas_call(
        body,
        name=name,
        grid=(c // tc,),
        in_specs=[pl.BlockSpec((slots, r, tc), lambda i: (0, 0, i)), blk, blk, blk],
        out_specs=[blk, blk, blk, blk],
        out_shape=[out, out, out, out],
        compiler_params=_cparams("parallel"),
    )(parts, w, m, v)


def _lane_offsets(sizes):
    offsets, pos = [], 0
    for n in sizes:
        offsets.append(pos)
        pos += -(-n // 128) * 128
    return offsets, pos


def _pack_row(parts):
    rows = [p.reshape(1, -1).astype(F32) for p in parts]
    return jnp.concatenate([jnp.pad(r, ((0, 0), (0, -r.shape[1] % 128))) for r in rows], axis=1)


def _small_update(parts, me, full_sizes, ws, ms, vs):
    n = len(ws)
    offsets, _ = _lane_offsets([1] + list(full_sizes))

    def body(me_ref, p_ref, *refs):
        w_refs, m_refs, v_refs = refs[:n], refs[n:2 * n], refs[2 * n:3 * n]
        scalar_ref, out_refs = refs[3 * n], refs[3 * n + 1:]
        tot = p_ref[0]
        for s in range(1, N_DEV):
            tot = tot + p_ref[s]
        scalar_ref[...] = tot[:, 0:1]
        for k in range(n):
            g_ref, d_ref, nm_ref, nv_ref = out_refs[4 * k:4 * k + 4]
            taps, cols = w_refs[k].shape
            if taps == 1:
                g_ref[...] = tot[:, offsets[k + 1]:offsets[k + 1] + cols]
            else:
                full = full_sizes[k] // taps
                for tap in range(taps):
                    mine = jnp.zeros((1, cols), F32)
                    for d in range(N_DEV):
                        lo = offsets[k + 1] + tap * full + d * cols
                        mine = jnp.where(me_ref[0] == d, tot[:, lo:lo + cols], mine)
                    g_ref[tap:tap + 1, :] = mine
            d_ref[...], nm_ref[...], nv_ref[...] = _adamw_math(w_refs[k][...], g_ref[...], m_refs[k][...], v_refs[k][...])

    vmem = pl.BlockSpec(memory_space=pltpu.VMEM)
    out_shape = [jax.ShapeDtypeStruct((1, 1), F32)]
    for wk in ws:
        out_shape += [jax.ShapeDtypeStruct(wk.shape, F32)] * 4
    res = pl.pallas_call(
        body,
        name="small_update",
        in_specs=[pl.BlockSpec(memory_space=pltpu.SMEM)] + [vmem] * (1 + 3 * n),
        out_specs=[vmem] * len(out_shape),
        out_shape=out_shape,
    )(me, parts, *ws, *ms, *vs)
    return res[0], [res[1 + 4 * k:5 + 4 * k] for k in range(n)]


ANY = pl.BlockSpec(memory_space=pl.ANY)
FLIPS = [(k >> 2 & 1, k >> 1 & 1, k & 1) for k in range(1, N_DEV)]


def _place():
    return lax.axis_index("x"), lax.axis_index("y"), lax.axis_index("c")


HBM = pl.BlockSpec(memory_space=pltpu.HBM)
SEM = pl.BlockSpec(memory_space=pltpu.SEMAPHORE)
EFFECT = pltpu.SideEffectType.DATAFLOW_SIDE_EFFECTING


def _peer_copy(gather, src_ref, land_ref, send_sems, recv_sems, k, sending):
    x, y, c = _place()
    fx, fy, fc = FLIPS[k]
    me = 4 * x + 2 * y + c
    peer = 4 * (x ^ fx) + 2 * (y ^ fy) + (c ^ fc)
    return pltpu.make_async_remote_copy(
        src_ref=src_ref if gather else src_ref.at[peer],
        dst_ref=land_ref.at[me if sending else peer],
        send_sem=send_sems.at[k], recv_sem=recv_sems.at[k],
        device_id=(x ^ fx, y ^ fy, c ^ fc), device_id_type=MESH)


SIBLING = 0
OTHER_CHIPS = (1, 3, 5)


def _gather_start(srcs, name, via_sibling):
    n = len(srcs)
    lands = [lax.empty((N_DEV,) + s.shape, s.dtype) for s in srcs]

    def body(*refs):
        src_refs, land_refs = refs[:n], refs[n:2 * n]
        send, recv = refs[2 * n:3 * n], refs[3 * n:4 * n]
        for i in range(n):
            for k in (SIBLING,) + OTHER_CHIPS if via_sibling else range(N_DEV - 1):
                _peer_copy(True, src_refs[i], land_refs[i], send[i], recv[i], k, True).start()

    sem = pltpu.SemaphoreType.DMA((N_DEV - 1,))
    hbm = lambda a: pltpu.HBM(a.shape, a.dtype)
    res = pl.pallas_call(
        body,
        name=name,
        in_specs=[HBM] * (2 * n),
        out_specs=[SEM] * (2 * n) + [HBM] * (2 * n),
        out_shape=[sem] * (2 * n) + [hbm(s) for s in srcs] + [hbm(a) for a in lands],
        input_output_aliases={i: 2 * n + i for i in range(2 * n)},
        compiler_params=pltpu.CompilerParams(has_side_effects=EFFECT),
    )(*[pltpu.with_memory_space_constraint(a, pltpu.HBM) for a in list(srcs) + lands])
    return res[:n], res[n:2 * n], res[2 * n:3 * n], res[3 * n:4 * n]


def _exchange_wait(send_sems, recv_sems, src, land, after, gather, name):
    def body(src_ref, land_ref, send_ref, recv_ref, after_ref, src_out, land_out):
        for k in range(N_DEV - 1):
            cp = _peer_copy(gather, src_ref, land_ref, send_ref, recv_ref, k, False)
            cp.wait_send()
            cp.wait_recv()

    hbm = lambda a: pltpu.HBM(a.shape, a.dtype)
    return pl.pallas_call(
        body,
        name=name,
        in_specs=[HBM, HBM, SEM, SEM, ANY],
        out_specs=[HBM, HBM],
        out_shape=[hbm(src), hbm(land)],
        input_output_aliases={0: 0, 1: 1},
        compiler_params=pltpu.CompilerParams(has_side_effects=EFFECT),
    )(src, land, send_sems, recv_sems, after)


def _own_slot(src, land, me, gather):
    own = src[None] if gather else lax.dynamic_slice_in_dim(src, me, 1, axis=0)
    return lax.dynamic_update_slice_in_dim(land, own, me, axis=0)


def _forwarded_copy(land_ref, send_sems, recv_sems, j, sending):
    x, y, c = _place()
    fx, fy, _ = FLIPS[OTHER_CHIPS[j]]
    slot = 4 * (x ^ fx) + 2 * (y ^ fy) + (c if sending else 1 - c)
    return pltpu.make_async_remote_copy(
        src_ref=land_ref.at[slot], dst_ref=land_ref.at[slot], send_sem=send_sems.at[j], recv_sem=recv_sems.at[j],
        device_id=(x, y, 1 - c), device_id_type=MESH)


def _gather_forward(send_sems, recv_sems, srcs, lands, after, name):
    n = len(srcs)

    def body(*refs):
        src_refs, land_refs = refs[:n], refs[n:2 * n]
        send, recv = refs[2 * n:3 * n], refs[3 * n:4 * n]
        fwd_send, fwd_recv = refs[4 * n + 1:5 * n + 1], refs[5 * n + 1:6 * n + 1]
        for i in range(n):
            for j, k in enumerate(OTHER_CHIPS):
                _peer_copy(True, src_refs[i], land_refs[i], send[i], recv[i], k, False).wait_recv()
                _forwarded_copy(land_refs[i], fwd_send[i], fwd_recv[i], j, True).start()

    sem = pltpu.SemaphoreType.DMA((len(OTHER_CHIPS),))
    hbm = lambda a: pltpu.HBM(a.shape, a.dtype)
    res = pl.pallas_call(
        body,
        name=name,
        in_specs=[HBM] * (2 * n) + [SEM] * (2 * n) + [ANY],
        out_specs=[SEM] * (2 * n) + [HBM] * (2 * n),
        out_shape=[sem] * (2 * n) + [hbm(a) for a in srcs] + [hbm(a) for a in lands],
        input_output_aliases={i: 2 * n + i for i in range(2 * n)},
        compiler_params=pltpu.CompilerParams(has_side_effects=EFFECT),
    )(*srcs, *lands, *send_sems, *recv_sems, after)
    return res[:n], res[n:2 * n], res[2 * n:3 * n], res[3 * n:4 * n]


def _gather_wait_forwarded(send_sems, recv_sems, fwd_send, fwd_recv, src, land, after, name):
    def body(src_ref, land_ref, send_ref, recv_ref, fwd_send_ref, fwd_recv_ref, after_ref, src_out, land_out):
        for k in (SIBLING,) + OTHER_CHIPS:
            _peer_copy(True, src_ref, land_ref, send_ref, recv_ref, k, False).wait_send()
        _peer_copy(True, src_ref, land_ref, send_ref, recv_ref, SIBLING, False).wait_recv()
        for j in range(len(OTHER_CHIPS)):
            _forwarded_copy(land_ref, fwd_send_ref, fwd_recv_ref, j, True).wait_send()
            _forwarded_copy(land_ref, fwd_send_ref, fwd_recv_ref, j, False).wait_recv()

    hbm = lambda a: pltpu.HBM(a.shape, a.dtype)
    return pl.pallas_call(
        body,
        name=name,
        in_specs=[HBM, HBM, SEM, SEM, SEM, SEM, ANY],
        out_specs=[HBM, HBM],
        out_shape=[hbm(src), hbm(land)],
        input_output_aliases={0: 0, 1: 1},
        compiler_params=pltpu.CompilerParams(has_side_effects=EFFECT),
    )(src, land, send_sems, recv_sems, fwd_send, fwd_recv, after)


N_CHIPS = N_DEV // 2


def _pair_exchange(by_core, meanwhile, name):
    def copy(src_ref, land_ref, send_sems, recv_sems, q):
        x, y, c = _place()
        return pltpu.make_async_remote_copy(
            src_ref=src_ref.at[q, 1 - c], dst_ref=land_ref.at[q], send_sem=send_sems.at[q], recv_sem=recv_sems.at[q],
            device_id=(x, y, 1 - c), device_id_type=MESH)

    def start(src_ref, land_ref, send_sems, recv_sems, src_out, land_out):
        for q in range(N_CHIPS):
            copy(src_ref, land_ref, send_sems, recv_sems, q).start()

    def wait(src_ref, land_ref, send_sems, recv_sems, after_ref, src_out, land_out):
        for q in range(N_CHIPS):
            cp = copy(src_ref, land_ref, send_sems, recv_sems, q)
            cp.wait_send()
            cp.wait_recv()

    sem = pltpu.SemaphoreType.DMA((N_CHIPS,))
    hbm_src = pltpu.HBM(by_core.shape, by_core.dtype)
    hbm_land = pltpu.HBM(by_core.shape[:1] + by_core.shape[2:], by_core.dtype)
    params = pltpu.CompilerParams(has_side_effects=EFFECT)
    send_sems, recv_sems, src, land = pl.pallas_call(
        start, name=name + "_start", in_specs=[HBM, HBM], out_specs=[SEM, SEM, HBM, HBM],
        out_shape=[sem, sem, hbm_src, hbm_land], input_output_aliases={0: 2, 1: 3}, compiler_params=params,
    )(pltpu.with_memory_space_constraint(by_core, pltpu.HBM),
      pltpu.with_memory_space_constraint(lax.empty(hbm_land.shape, by_core.dtype), pltpu.HBM))
    return pl.pallas_call(
        wait, name=name + "_wait", in_specs=[HBM, HBM, SEM, SEM, ANY], out_specs=[HBM, HBM],
        out_shape=[hbm_src, hbm_land], input_output_aliases={0: 0, 1: 1}, compiler_params=params,
    )(src, land, send_sems, recv_sems, meanwhile(src))


def _pair_add(by_core, landed, name):
    q, _, r, c = by_core.shape
    tc = _tile(c, (512, 256, 128))

    def body(a_ref, b_ref, o_ref):
        mine = a_ref[0, lax.axis_index("c")]
        o_ref[0] = (mine.astype(F32) + b_ref[0].astype(F32)).astype(BF16)

    blk = pl.BlockSpec((1, r, tc), lambda i, j: (i, 0, j))
    return pl.pallas_call(
        body, name=name, grid=(q, c // tc),
        in_specs=[pl.BlockSpec((1, 2, r, tc), lambda i, j: (i, 0, 0, j)), blk], out_specs=blk,
        out_shape=jax.ShapeDtypeStruct(landed.shape, BF16), compiler_params=_cparams("parallel", "parallel"),
    )(by_core, landed)


def _chip_copy(src_ref, land_ref, send_sems, recv_sems, j, sending):
    x, y, c = _place()
    fx, fy, _ = FLIPS[OTHER_CHIPS[j]]
    here, there = 2 * x + y, 2 * (x ^ fx) + (y ^ fy)
    return pltpu.make_async_remote_copy(
        src_ref=src_ref.at[there], dst_ref=land_ref.at[here if sending else there],
        send_sem=send_sems.at[j], recv_sem=recv_sems.at[j],
        device_id=(x ^ fx, y ^ fy, c), device_id_type=MESH)


def _chip_wait(send_sems, recv_sems, src, land, after, name):
    def body(src_ref, land_ref, send_ref, recv_ref, after_ref, src_out, land_out):
        for j in range(len(OTHER_CHIPS)):
            cp = _chip_copy(src_ref, land_ref, send_ref, recv_ref, j, False)
            cp.wait_send()
            cp.wait_recv()

    hbm = lambda a: pltpu.HBM(a.shape, a.dtype)
    return pl.pallas_call(
        body,
        name=name,
        in_specs=[HBM, HBM, SEM, SEM, ANY],
        out_specs=[HBM, HBM],
        out_shape=[hbm(src), hbm(land)],
        input_output_aliases={0: 0, 1: 1},
        compiler_params=pltpu.CompilerParams(has_side_effects=EFFECT),
    )(src, land, send_sems, recv_sems, after)


def _col(v):
    return v.reshape(-1, 1).astype(F32)


def _local_step(x, tgt, started, weight, small, pair_sums, handles):
    t = x.shape[0]
    n1 = _col(small["norm1_w"])
    n2 = _col(small["norm2_w"])
    nf = _col(small["final_norm_w"])
    bg = _col(small["b_gate"])
    sinks = small["attn_sinks"].reshape(-1).astype(F32)
    cbias = _col(small["ssd_conv_b"])
    dtb = _col(small["dt_bias"])
    alog = _col(small["a_log"])
    dsk = _col(small["d_skip"])
    gnw = _col(small["ssd_norm_w"])
    fb = small["ffn_conv_b"].reshape(2, D_FF, 1)

    xt, xn = _norm_fwd_tokens(x, n1, started, "norm1_fwd")
    cw = weight("ssd_conv_w", xn).T
    fw = weight("ffn_conv_w", xn).T.reshape(2, D_FF, FFN_CONV)
    w_in_t = weight("w_in", xn)
    proj = _matmul(w_in_t, xn, nt=False, out_dtype=F32, name="mm_in")
    ao, lse = _attn_fwd(proj, sinks)
    w_ao = weight("w_attn_o", ao)
    attn = _matmul(w_ao, ao, nt=False, out_dtype=F32, name="mm_attn_o", tn_a=True)
    xbc = _conv_silu_fwd(proj, cw, cbias)
    y, hst = _ssd_fwd(xbc, proj, dtb, alog, dsk)
    yn = _gnorm_fwd(y, proj, gnw)
    w_so = weight("w_ssd_o", yn)
    ssd = _matmul(w_so, yn, nt=False, out_dtype=F32, name="mm_ssd_o", tn_a=True)
    mix = _gate_fwd(proj, bg, attn, ssd)
    w_out = weight("w_out", mix)
    h1 = _matmul(w_out, mix, nt=False, out_dtype=F32, name="mm_out", add=xt, tn_a=True)
    hn = _norm_fwd(h1, n2, "norm2_fwd")
    w_up_t = weight("w_up", hn)
    u0 = _matmul(w_up_t, hn, nt=False, out_dtype=F32, name="mm_up").reshape(2, D_FF, t)
    gl = _ffn_fwd(u0, fw, fb)
    w_down = weight("w_down", gl)
    h2 = _matmul(w_down, gl, nt=False, out_dtype=F32, name="mm_down", add=h1, tn_a=True)
    dh2, loss, d_nf = _final_norm_loss(h2, tgt, nf)

    g = {}

    def sending(weight_name, grad, fn, *args, **kwargs):
        chunks = grad if grad.ndim == 3 else grad.reshape(N_DEV, -1, D_MODEL)
        out, handles[weight_name] = fn(*args, send=chunks, **kwargs)
        return out

    g_down = _matmul(gl, dh2, nt=True, out_dtype=BF16, name="mm_d_w_down")
    dgl = _matmul(w_down, dh2, nt=False, out_dtype=F32, name="mm_d_glu")
    du0, d_fwb = sending("w_down", g_down, _ffn_bwd, u0, fw, fb, dgl)
    du0 = du0.reshape(2 * D_FF, t)
    g_up = _matmul(du0, hn, nt=True, out_dtype=BF16, name="mm_d_w_up")
    dhn = sending("w_up", g_up, _matmul, w_up_t, du0, nt=False, out_dtype=F32, name="mm_d_hn", tn_a=True)
    dh1, d_n2 = _norm_bwd(dhn, h1, n2, dh2, "norm2_bwd")
    g_out = _matmul(mix, dh1, nt=True, out_dtype=BF16, name="mm_d_w_out")
    dmix = _matmul(w_out, dh1, nt=False, out_dtype=F32, name="mm_d_mix")
    d_attn, d_ssd, d_ga, d_gs, d_ba, d_bs = sending("w_out", g_out, _gate_bwd, proj, bg, attn, ssd, dmix)
    g_ao = _matmul(ao, d_attn, nt=True, out_dtype=BF16, name="mm_d_w_attn_o")
    dao = _matmul(w_ao, d_attn, nt=False, out_dtype=F32, name="mm_d_ao")
    g_so = _matmul(yn, d_ssd, nt=True, out_dtype=BF16, name="mm_d_w_ssd_o")
    dyn = _matmul(w_so, d_ssd, nt=False, out_dtype=F32, name="mm_d_yn")
    dy, dproj, d_gnw = sending("w_ssd_o", g_so, _gnorm_bwd, dyn, y, proj, gnw)
    dxbc, ddt, d_alog, d_dsk, d_dtb = _ssd_bwd(xbc, proj, dtb, alog, dsk, hst, dy)
    dproj, dwb_conv = _conv_silu_bwd(proj, cw, cbias, dxbc, dproj)
    dproj, d_sinks = sending("w_attn_o", g_ao, _attn_bwd, proj, sinks, ao, lse, dao, dproj)
    for rows, part in ((OFF_DT, ddt.astype(BF16)), (OFF_GA, d_ga), (OFF_GS, d_gs)):
        dproj = lax.dynamic_update_slice(dproj, part, (rows, 0))
    g_in = pair_sums(_matmul(dproj, xn, nt=True, out_dtype=BF16, name="mm_d_w_in"))
    dxn = sending("w_in", g_in, _matmul, w_in_t, dproj, nt=False, out_dtype=F32, name="mm_d_xn", tn_a=True)
    dx, d_n1 = _norm_bwd(dxn, xt, n1, dh1, "norm1_bwd", tokens_out=True)

    g["norm1_w"] = d_n1
    g["b_gate"] = jnp.concatenate([d_ba, d_bs], axis=0)
    g["attn_sinks"] = d_sinks
    g["ssd_conv_w"] = dwb_conv[:, :SSD_CONV].T
    g["ssd_conv_b"] = dwb_conv[:, SSD_CONV]
    g["dt_bias"] = d_dtb
    g["a_log"] = d_alog
    g["d_skip"] = d_dsk
    g["ssd_norm_w"] = d_gnw
    g["norm2_w"] = d_n2
    d_fwb = d_fwb.reshape(2 * D_FF, 128)
    g["ffn_conv_w"] = d_fwb[:, :FFN_CONV].T
    g["ffn_conv_b"] = d_fwb[:, FFN_CONV]
    g["final_norm_w"] = d_nf
    return loss, dx, g


SMALL = ("norm1_w", "b_gate", "attn_sinks", "ssd_conv_w", "ssd_conv_b", "dt_bias", "a_log", "d_skip", "ssd_norm_w",
         "norm2_w", "ffn_conv_w", "ffn_conv_b", "final_norm_w")
WEIGHT_ORDER = ("norm1_w", "w_in", "b_gate", "attn_sinks", "w_attn_o", "ssd_conv_w", "ssd_conv_b", "dt_bias", "a_log",
                "d_skip", "ssd_norm_w", "w_ssd_o", "w_out", "norm2_w", "w_up", "ffn_conv_w", "ffn_conv_b", "w_down",
                "final_norm_w")


def kernel(x, norm1_w, w_in, b_gate, attn_sinks, w_attn_o, ssd_conv_w, ssd_conv_b, dt_bias, a_log, d_skip, ssd_norm_w, w_ssd_o, w_out, norm2_w, w_up, ffn_conv_w, ffn_conv_b, w_down, final_norm_w, loss_target, m_norm1_w, m_w_in, m_b_gate, m_attn_sinks, m_w_attn_o, m_ssd_conv_w, m_ssd_conv_b, m_dt_bias, m_a_log, m_d_skip, m_ssd_norm_w, m_w_ssd_o, m_w_out, m_norm2_w, m_w_up, m_ffn_conv_w, m_ffn_conv_b, m_w_down, m_final_norm_w, v_norm1_w, v_w_in, v_b_gate, v_attn_sinks, v_w_attn_o, v_ssd_conv_w, v_ssd_conv_b, v_dt_bias, v_a_log, v_d_skip, v_ssd_norm_w, v_w_ssd_o, v_w_out, v_norm2_w, v_w_up, v_ffn_conv_w, v_ffn_conv_b, v_w_down, v_final_norm_w):
    w = dict(norm1_w=norm1_w, w_in=w_in, b_gate=b_gate, attn_sinks=attn_sinks, w_attn_o=w_attn_o, ssd_conv_w=ssd_conv_w, ssd_conv_b=ssd_conv_b, dt_bias=dt_bias, a_log=a_log, d_skip=d_skip, ssd_norm_w=ssd_norm_w, w_ssd_o=w_ssd_o, w_out=w_out, norm2_w=norm2_w, w_up=w_up, ffn_conv_w=ffn_conv_w, ffn_conv_b=ffn_conv_b, w_down=w_down, final_norm_w=final_norm_w)
    m = dict(norm1_w=m_norm1_w, w_in=m_w_in, b_gate=m_b_gate, attn_sinks=m_attn_sinks, w_attn_o=m_w_attn_o, ssd_conv_w=m_ssd_conv_w, ssd_conv_b=m_ssd_conv_b, dt_bias=m_dt_bias, a_log=m_a_log, d_skip=m_d_skip, ssd_norm_w=m_ssd_norm_w, w_ssd_o=m_w_ssd_o, w_out=m_w_out, norm2_w=m_norm2_w, w_up=m_w_up, ffn_conv_w=m_ffn_conv_w, ffn_conv_b=m_ffn_conv_b, w_down=m_w_down, final_norm_w=m_final_norm_w)
    v = dict(norm1_w=v_norm1_w, w_in=v_w_in, b_gate=v_b_gate, attn_sinks=v_attn_sinks, w_attn_o=v_w_attn_o, ssd_conv_w=v_ssd_conv_w, ssd_conv_b=v_ssd_conv_b, dt_bias=v_dt_bias, a_log=v_a_log, d_skip=v_d_skip, ssd_norm_w=v_ssd_norm_w, w_ssd_o=v_w_ssd_o, w_out=v_w_out, norm2_w=v_norm2_w, w_up=v_w_up, ffn_conv_w=v_ffn_conv_w, ffn_conv_b=v_ffn_conv_b, w_down=v_w_down, final_norm_w=v_final_norm_w)
    me = 4 * lax.axis_index("x") + 2 * lax.axis_index("y") + lax.axis_index("c")

    shards = {"ssd_conv_w": ssd_conv_w[0], "ffn_conv_w": ffn_conv_w[0], "w_in": w_in[0].T.astype(BF16),
              "w_attn_o": w_attn_o[0].astype(BF16), "w_ssd_o": w_ssd_o[0].astype(BF16), "w_out": w_out[0].astype(BF16),
              "w_up": w_up[0].T.astype(BF16), "w_down": w_down[0].astype(BF16)}
    order = list(shards)
    g_send, g_recv, g_src, g_land = _gather_start(list(shards.values()), "gather_start", True)
    first = ("ssd_conv_w", "ffn_conv_w", "w_in")
    forwarded = {}

    def weight(name, after):
        if name not in forwarded:
            group = [k for k in order if (k in first) == (name in first)]
            idx = [order.index(k) for k in group]
            handles = _gather_forward([g_send[i] for i in idx], [g_recv[i] for i in idx], [g_src[i] for i in idx],
                                      [g_land[i] for i in idx], after, "gather_forward_for_" + name)
            forwarded.update(zip(group, zip(*handles)))
        i = order.index(name)
        src, land = _gather_wait_forwarded(g_send[i], g_recv[i], *forwarded[name], after, "gather_wait_" + name)
        land = _own_slot(src, land, me, True)
        if name == "ssd_conv_w":
            return jnp.transpose(land, (1, 0, 2)).reshape(SSD_CONV, XBC_DIM)
        if name == "ffn_conv_w":
            return jnp.transpose(land, (1, 0, 2)).reshape(FFN_CONV, 2 * D_FF)
        return land.reshape(-1, D_MODEL)

    res, pending = {}, {}

    def update(name, after):
        if name == "w_in":
            parts = _own_slot(*_chip_wait(*pending[name], after, "grad_wait_" + name), me // 2, False)
        else:
            parts = _own_slot(*_exchange_wait(*pending[name], after, False, "grad_wait_" + name), me, False)
        view, back = {
            "w_in": (lambda a: jnp.transpose(a, (2, 0, 1)), lambda r: jnp.transpose(r, (1, 2, 0))),
            "w_up": (lambda a: a[0].T, lambda r: r.T[None]),
        }.get(name, (lambda a: a[0], lambda r: r[None]))
        done = _adamw_sharded(parts, view(w[name]), view(m[name]), view(v[name]), "adamw_" + name)
        res[name] = [back(r) for r in done]
        return done[0]

    def pair_sums(grad):
        by_core, landed = _pair_exchange(grad.reshape(N_CHIPS, 2, -1, D_MODEL),
                                         lambda started: update("w_up", update("w_down", started)), "grad_pair_w_in")
        return _pair_add(by_core, landed, "grad_pair_add_w_in")

    small = {k: w[k][0] if k != "final_norm_w" else w[k] for k in SMALL}
    loss, dx, g = _local_step(x[0], loss_target[0], g_src[0], weight, small, pair_sums, pending)

    packed = _pack_row([loss] + [g[k] for k in SMALL])
    s_send, s_recv, s_src, s_land = _gather_start([packed], "small_grads_start", False)
    after = s_src[0]
    for name in ("w_out", "w_attn_o", "w_ssd_o", "w_in"):
        after = update(name, after)

    rows = _own_slot(*_exchange_wait(s_send[0], s_recv[0], s_src[0], s_land[0], after, True, "small_grads_wait"),
                     me, True)
    flat = lambda a: a.reshape(-1, a.shape[-1])
    loss_sum, updates = _small_update(
        rows, me.reshape(1), [g[k].size for k in SMALL],
        [flat(w[k]) for k in SMALL], [flat(m[k]) for k in SMALL], [flat(v[k]) for k in SMALL])
    for k, upd in zip(SMALL, updates):
        res[k] = [u.reshape(w[k].shape) for u in upd]

    grad_x = dx[None]
    outs = [loss_sum.reshape(()), grad_x]
    for i in range(4):
        outs.extend(res[k][i] for k in WEIGHT_ORDER)
    return tuple(outs)
```

```python
import jax
import jax.numpy as jnp
from jax import lax
from jax.experimental import pallas as pl
from jax.experimental.pallas import tpu as pltpu

F32 = jnp.float32
BF16 = jnp.bfloat16
HIGHEST = lax.Precision.HIGHEST

D_MODEL = 1024
N_Q_HEADS = 16
N_KV_HEADS = 4
HEAD_DIM = 64
WINDOW = 128
Q_PER_KV = N_Q_HEADS // N_KV_HEADS
Q_DIM = N_Q_HEADS * HEAD_DIM
KV_DIM = N_KV_HEADS * HEAD_DIM
D_INNER = 2048
SSD_HEAD_DIM = 64
N_SSD_HEADS = 32
N_SSD_GROUPS = 4
HEADS_PER_GROUP = N_SSD_HEADS // N_SSD_GROUPS
D_STATE = 128
GN_ROWS = D_INNER // N_SSD_GROUPS
BC_DIM = N_SSD_GROUPS * D_STATE
XBC_DIM = D_INNER + 2 * BC_DIM
SSD_CONV = 4
CHUNK = 128
D_FF = 2816
FFN_CONV = 3
EPS = 1e-5
NEG = -1e30
IN_DIM = 8736
N_DEV = 8

OFF_Q = 0
OFF_K = OFF_Q + Q_DIM
OFF_V = OFF_K + KV_DIM
OFF_Z = OFF_V + KV_DIM
OFF_X = OFF_Z + D_INNER
OFF_DT = OFF_X + XBC_DIM
OFF_GA = OFF_DT + N_SSD_HEADS
OFF_GS = OFF_GA + D_MODEL

ADAM_LR = 0.001
ADAM_B1 = 0.9
ADAM_B2 = 0.999
ADAM_EPS = 1e-08
ADAM_WD = 0.01
ADAM_STEP = 10

LANES = 128
BF16_TILE_ROWS = 16
VMEM_BYTES = 64 * 1024 * 1024
VMEM_LIMIT = VMEM_BYTES * 3 // 4
MESH = pl.DeviceIdType.MESH


def _cparams(*sem):
    return pltpu.CompilerParams(dimension_semantics=sem, vmem_limit_bytes=VMEM_LIMIT)


def _tile(n, prefs):
    for p in prefs:
        if n % p == 0:
            return p
    return n


def _sigmoid(x):
    return 1.0 / (1.0 + jnp.exp(-x))


def _softplus(x):
    return jnp.maximum(x, 0.0) + jnp.log(1.0 + jnp.exp(-jnp.abs(x)))


def _rowsum(x):
    return jnp.sum(x, axis=1, keepdims=True)


def _colsum(x):
    return jnp.sum(x, axis=0, keepdims=True)


def _dot(a, b):
    return jnp.dot(a, b, preferred_element_type=F32)


def _dot_nt(a, b):
    return lax.dot_general(a, b, (((1,), (1,)), ((), ())), preferred_element_type=F32)


def _dot_tn(a, b):
    return lax.dot_general(a, b, (((0,), (0,)), ((), ())), preferred_element_type=F32)


def _shift_right(x, j):
    if j == 0:
        return x
    r = pltpu.roll(x, j, 1)
    lane = lax.broadcasted_iota(jnp.int32, (x.shape[0], 128), 1)
    return jnp.concatenate([jnp.where(lane >= j, r[:, :128], 0.0), r[:, 128:]], axis=1)


def _shift_left(x, j):
    if j == 0:
        return x
    n = x.shape[1]
    r = pltpu.roll(x, n - j, 1)
    lane = lax.broadcasted_iota(jnp.int32, (x.shape[0], 128), 1)
    return jnp.concatenate([r[:, :n - 128], jnp.where(lane < 128 - j, r[:, n - 128:], 0.0)], axis=1)


def _causal_conv(xv, wv, bv):
    taps = wv.shape[1]
    shifted = [_shift_right(xv, taps - 1 - k) for k in range(taps - 1)]
    y = bv + wv[:, taps - 1:taps] * xv
    for k in range(taps - 1):
        y = y + wv[:, k:k + 1] * shifted[k]
    return y, shifted


def _causal_conv_bwd(dy, xv, shifted, wv):
    taps = wv.shape[1]
    lane = lax.broadcasted_iota(jnp.int32, (dy.shape[0], 128), 1)
    dwb = jnp.where(lane == taps, _rowsum(dy), 0.0)
    dwb = jnp.where(lane == taps - 1, _rowsum(dy * xv), dwb)
    dx = wv[:, taps - 1:taps] * dy
    for k in range(taps - 1):
        dx = dx + wv[:, k:k + 1] * _shift_left(dy, taps - 1 - k)
        dwb = jnp.where(lane == k, _rowsum(dy * shifted[k]), dwb)
    return dx, dwb


def _call(body, *, name, grid, in_specs, out_specs, out_shape, args, semantics, scratch_shapes=(), aliases=None,
          send=None):
    aliases = dict(aliases or {})
    if send is None:
        return pl.pallas_call(body, name=name, grid=grid, in_specs=in_specs, out_specs=out_specs, out_shape=out_shape,
                              scratch_shapes=list(scratch_shapes), input_output_aliases=aliases,
                              compiler_params=_cparams(*semantics))(*args)
    single = not isinstance(out_specs, (list, tuple))
    out_specs, out_shape = ([out_specs], [out_shape]) if single else (list(out_specs), list(out_shape))
    n_in, n_out = len(in_specs), len(out_specs)
    chips = send.shape[0] == N_DEV // 2
    n_copies = len(OTHER_CHIPS) if chips else N_DEV - 1

    def sending(*refs):
        ins, (src_ref, land_ref) = refs[:n_in], refs[n_in:n_in + 2]
        outs = refs[n_in + 2:n_in + 2 + n_out]
        send_sems, recv_sems = refs[n_in + 2 + n_out:n_in + 4 + n_out]
        scratch = refs[n_in + 6 + n_out:]
        step = 0
        for axis, size in enumerate(grid):
            step = step * size + pl.program_id(axis)

        @pl.when(step == 0)
        def _():
            for k in range(n_copies):
                if chips:
                    _chip_copy(src_ref, land_ref, send_sems, recv_sems, k, True).start()
                else:
                    _peer_copy(False, src_ref, land_ref, send_sems, recv_sems, k, True).start()

        body(*ins, *outs, *scratch)

    sem = pltpu.SemaphoreType.DMA((n_copies,))
    hbm = pltpu.HBM(send.shape, send.dtype)
    res = pl.pallas_call(
        sending, name=name, grid=grid,
        in_specs=list(in_specs) + [HBM, HBM],
        out_specs=out_specs + [SEM, SEM, HBM, HBM],
        out_shape=out_shape + [sem, sem, hbm, hbm],
        input_output_aliases={**aliases, n_in: n_out + 2, n_in + 1: n_out + 3},
        scratch_shapes=list(scratch_shapes),
        compiler_params=pltpu.CompilerParams(dimension_semantics=("arbitrary",) * len(grid), vmem_limit_bytes=VMEM_LIMIT,
                                             has_side_effects=EFFECT),
    )(*args, pltpu.with_memory_space_constraint(send, pltpu.HBM),
      pltpu.with_memory_space_constraint(lax.empty(send.shape, send.dtype), pltpu.HBM))
    return (res[0] if single else list(res[:n_out])), tuple(res[n_out:])


BLOCK_VMEM_BUDGET = VMEM_LIMIT * 3 // 4
MATMUL_MAX_TM = 768
MATMUL_MAX_TN = 3072
MATMUL_MAX_TK = 3072


def _largest_tile(n, align, cap):
    return max(d for d in range(align, min(n, cap) + 1, align) if n % d == 0)


def _matmul_tiles(m, n, k, a_bytes, b_bytes, out_bytes, has_add, m_align, k_align):
    tm = _largest_tile(m, m_align, MATMUL_MAX_TM)
    tk = _largest_tile(k, k_align, MATMUL_MAX_TK)
    for tn in sorted({d for d in range(LANES, min(n, MATMUL_MAX_TN) + 1, LANES) if n % d == 0}, reverse=True):
        need = 2 * (tm * tk * a_bytes + tk * tn * b_bytes) + tm * tn * (2 * out_bytes + (4 if k > tk else 0) + (8 if has_add else 0))
        if need <= BLOCK_VMEM_BUDGET:
            return tm, tn, tk
    return tm, LANES, tk


def _matmul(a, b, *, nt, out_dtype, name, add=None, tn_a=False, send=None):
    if tn_a:
        k, m = a.shape
    else:
        m, k = a.shape
    n = b.shape[0] if nt else b.shape[1]
    tm, tn, tk = _matmul_tiles(m, n, k, a.dtype.itemsize, b.dtype.itemsize, jnp.dtype(out_dtype).itemsize, add is not None,
                               LANES if tn_a else BF16_TILE_ROWS, BF16_TILE_ROWS if tn_a and not nt else LANES)
    nk = k // tk
    grid = (m // tm, n // tn, nk)

    def body(a_ref, b_ref, *rest):
        r_ref = None
        if add is not None:
            r_ref, rest = rest[0], rest[1:]
        o_ref = rest[0]
        av = a_ref[...].astype(BF16)
        bv = b_ref[...].astype(BF16)
        part = _dot_tn(av, bv) if tn_a else _dot_nt(av, bv) if nt else _dot(av, bv)

        def finish(r):
            if add is not None:
                r = r + r_ref[...]
            o_ref[...] = r.astype(out_dtype)

        if nk == 1:
            finish(part)
            return
        acc = rest[1]
        kk = pl.program_id(2)

        @pl.when(kk == 0)
        def _():
            acc[...] = part

        @pl.when((kk > 0) & (kk < nk - 1))
        def _():
            acc[...] += part

        @pl.when(kk == nk - 1)
        def _():
            finish(acc[...] + part)

    in_specs = [
        pl.BlockSpec((tk, tm), lambda i, j, kk: (kk, i)) if tn_a else pl.BlockSpec((tm, tk), lambda i, j, kk: (i, kk)),
        pl.BlockSpec((tn, tk), lambda i, j, kk: (j, kk)) if nt else pl.BlockSpec((tk, tn), lambda i, j, kk: (kk, j)),
    ]
    args = [a, b]
    if add is not None:
        in_specs.append(pl.BlockSpec((tm, tn), lambda i, j, kk: (i, j)))
        args.append(add)
    return _call(
        body, name=name, grid=grid, in_specs=in_specs, args=args,
        out_specs=pl.BlockSpec((tm, tn), lambda i, j, kk: (i, j)),
        out_shape=jax.ShapeDtypeStruct((m, n), out_dtype),
        scratch_shapes=[pltpu.VMEM((tm, tn), F32)] if nk > 1 else [],
        semantics=("parallel", "parallel", "arbitrary"), send=send)


def _norm_fwd(x, w_col, name):
    f, t = x.shape
    tt = _tile(t, (512, 256, 128))

    def body(x_ref, w_ref, o_ref):
        xv = x_ref[...]
        r = lax.rsqrt(jnp.mean(xv * xv, axis=0, keepdims=True) + EPS)
        o_ref[...] = (xv * r * w_ref[...]).astype(BF16)

    return pl.pallas_call(
        body,
        name=name,
        grid=(t // tt,),
        in_specs=[pl.BlockSpec((f, tt), lambda i: (0, i)), pl.BlockSpec((f, 1), lambda i: (0, 0))],
        out_specs=pl.BlockSpec((f, tt), lambda i: (0, i)),
        out_shape=jax.ShapeDtypeStruct((f, t), BF16),
        compiler_params=_cparams("parallel"),
    )(x, w_col)


def _norm_fwd_tokens(x, w_col, after, name):
    t, f = x.shape
    tt = _tile(t, (512, 256, 128))

    def body(x_ref, w_ref, after_ref, xt_ref, o_ref):
        xv = x_ref[...].T
        xt_ref[...] = xv
        r = lax.rsqrt(jnp.mean(xv * xv, axis=0, keepdims=True) + EPS)
        o_ref[...] = (xv * r * w_ref[...]).astype(BF16)

    blk = pl.BlockSpec((f, tt), lambda i: (0, i))
    return pl.pallas_call(
        body,
        name=name,
        grid=(t // tt,),
        in_specs=[pl.BlockSpec((tt, f), lambda i: (i, 0)), pl.BlockSpec((f, 1), lambda i: (0, 0)), ANY],
        out_specs=[blk, blk],
        out_shape=[jax.ShapeDtypeStruct((f, t), F32), jax.ShapeDtypeStruct((f, t), BF16)],
        compiler_params=_cparams("parallel"),
    )(x, w_col, after)


def _norm_bwd(dy, x, w_col, res, name, tokens_out=False):
    f, t = x.shape
    tt = _tile(t, (512, 256, 128))

    def body(dy_ref, x_ref, w_ref, res_ref, dx_ref, dw_ref):
        @pl.when(pl.program_id(0) == 0)
        def _():
            dw_ref[...] = jnp.zeros_like(dw_ref)

        xv = x_ref[...]
        r = lax.rsqrt(jnp.mean(xv * xv, axis=0, keepdims=True) + EPS)
        xhat = xv * r
        dyv = dy_ref[...]
        dw_ref[...] += _rowsum(dyv * xhat)
        dxhat = dyv * w_ref[...]
        dx = res_ref[...] + r * (dxhat - xhat * jnp.mean(dxhat * xhat, axis=0, keepdims=True))
        dx_ref[...] = dx.T if tokens_out else dx

    blk = pl.BlockSpec((f, tt), lambda i: (0, i))
    col = pl.BlockSpec((f, 1), lambda i: (0, 0))
    return pl.pallas_call(
        body,
        name=name,
        grid=(t // tt,),
        in_specs=[blk, blk, col, blk],
        out_specs=[pl.BlockSpec((tt, f), lambda i: (i, 0)) if tokens_out else blk, col],
        out_shape=[jax.ShapeDtypeStruct((t, f) if tokens_out else (f, t), F32), jax.ShapeDtypeStruct((f, 1), F32)],
        compiler_params=_cparams("arbitrary"),
    )(dy, x, w_col, res)


def _final_norm_loss(h, tgt, w_col):
    f, t = h.shape
    tt = _tile(t, (512, 256, 128))

    def body(h_ref, t_ref, w_ref, dh_ref, loss_ref, dw_ref):
        @pl.when(pl.program_id(0) == 0)
        def _():
            dw_ref[...] = jnp.zeros_like(dw_ref)
            loss_ref[...] = jnp.zeros_like(loss_ref)

        xv = h_ref[...]
        r = lax.rsqrt(jnp.mean(xv * xv, axis=0, keepdims=True) + EPS)
        xhat = xv * r
        wv = w_ref[...]
        err = xhat * wv - t_ref[...].T
        loss_ref[...] += 0.5 * _rowsum(jnp.mean(err * err, axis=0, keepdims=True))
        dyv = err * (1.0 / f)
        dw_ref[...] += _rowsum(dyv * xhat)
        dxhat = dyv * wv
        dh_ref[...] = r * (dxhat - xhat * jnp.mean(dxhat * xhat, axis=0, keepdims=True))

    blk = pl.BlockSpec((f, tt), lambda i: (0, i))
    col = pl.BlockSpec((f, 1), lambda i: (0, 0))
    one = pl.BlockSpec((1, 1), lambda i: (0, 0))
    return pl.pallas_call(
        body,
        name="final_norm_loss",
        grid=(t // tt,),
        in_specs=[blk, pl.BlockSpec((tt, f), lambda i: (i, 0)), col],
        out_specs=[blk, one, col],
        out_shape=[jax.ShapeDtypeStruct((f, t), F32), jax.ShapeDtypeStruct((1, 1), F32), jax.ShapeDtypeStruct((f, 1), F32)],
        compiler_params=_cparams("arbitrary"),
    )(h, tgt, w_col)


def _attn_mask(n):
    shape = (2 * WINDOW, Q_PER_KV * WINDOW)
    si = lax.broadcasted_iota(jnp.int32, shape, 0)
    qi = lax.broadcasted_iota(jnp.int32, shape, 1) & (WINDOW - 1)
    dist = WINDOW + qi - si
    return (dist >= 0) & (dist < WINDOW) & ((si >= WINDOW) | (n > 0))


def _lane_cat(ref, row0, rows):
    return jnp.concatenate([ref[row0 + i * rows:row0 + (i + 1) * rows, :] for i in range(Q_PER_KV)], axis=1)


def _attn_fwd(proj, sinks):
    t = proj.shape[1]
    nb = t // WINDOW
    scale = HEAD_DIM ** -0.5

    def body(s_ref, q_ref, kc_ref, kp_ref, vc_ref, vp_ref, o_ref, lse_ref):
        n = pl.program_id(0)
        valid = _attn_mask(n)
        for g in range(N_KV_HEADS):
            rows = slice(g * HEAD_DIM, (g + 1) * HEAD_DIM)
            kt = jnp.concatenate([kp_ref[rows, :], kc_ref[rows, :]], axis=1).astype(BF16)
            vt = jnp.concatenate([vp_ref[rows, :], vc_ref[rows, :]], axis=1).astype(BF16)
            qcat = (_lane_cat(q_ref, g * Q_PER_KV * HEAD_DIM, HEAD_DIM) * scale).astype(BF16)
            s = jnp.where(valid, _dot_tn(kt, qcat), NEG)
            sink = jnp.concatenate(
                [jnp.full((1, WINDOW), s_ref[g * Q_PER_KV + i], F32) for i in range(Q_PER_KV)], axis=1)
            m = jnp.maximum(jnp.max(s, axis=0, keepdims=True), sink)
            p = jnp.exp(s - m)
            denom = _colsum(p) + jnp.exp(sink - m)
            probs = (p / denom).astype(BF16)
            out = _dot(vt, probs)
            lse = m + jnp.log(denom)
            for i in range(Q_PER_KV):
                h = g * Q_PER_KV + i
                o_ref[h * HEAD_DIM:(h + 1) * HEAD_DIM, :] = out[:, i * WINDOW:(i + 1) * WINDOW]
                lse_ref[h:h + 1, :] = lse[:, i * WINDOW:(i + 1) * WINDOW]

    kb = OFF_K // KV_DIM
    vb = OFF_V // KV_DIM
    prev = lambda n: jnp.maximum(n - 1, 0)
    return pl.pallas_call(
        body,
        name="attn_fwd",
        grid=(nb,),
        in_specs=[
            pl.BlockSpec(memory_space=pltpu.SMEM),
            pl.BlockSpec((Q_DIM, WINDOW), lambda n: (0, n)),
            pl.BlockSpec((KV_DIM, WINDOW), lambda n: (kb, n)),
            pl.BlockSpec((KV_DIM, WINDOW), lambda n: (kb, prev(n))),
            pl.BlockSpec((KV_DIM, WINDOW), lambda n: (vb, n)),
            pl.BlockSpec((KV_DIM, WINDOW), lambda n: (vb, prev(n))),
        ],
        out_specs=[pl.BlockSpec((Q_DIM, WINDOW), lambda n: (0, n)), pl.BlockSpec((N_Q_HEADS, WINDOW), lambda n: (0, n))],
        out_shape=[jax.ShapeDtypeStruct((Q_DIM, t), F32), jax.ShapeDtypeStruct((N_Q_HEADS, t), F32)],
        compiler_params=_cparams("parallel"),
    )(sinks, proj, proj, proj, proj, proj)


def _attn_bwd(proj, sinks, out, lse, dout, dproj, send=None):
    t = proj.shape[1]
    nb = t // WINDOW
    scale = HEAD_DIM ** -0.5

    def body(s_ref, q_ref, kc_ref, kp_ref, vc_ref, vp_ref, o_ref, lse_ref, do_ref, dproj_ref,
             dqkv_ref, ds_ref, dk_carry, dv_carry):
        dq_ref = dqkv_ref.at[pl.ds(OFF_Q, Q_DIM)]
        dk_ref = dqkv_ref.at[pl.ds(OFF_K, KV_DIM)]
        dv_ref = dqkv_ref.at[pl.ds(OFF_V, KV_DIM)]
        step = pl.program_id(0)
        n = nb - 1 - step

        @pl.when(step == 0)
        def _():
            dk_carry[...] = jnp.zeros_like(dk_carry)
            dv_carry[...] = jnp.zeros_like(dv_carry)
            ds_ref[...] = jnp.zeros_like(ds_ref)

        valid = _attn_mask(n)
        for g in range(N_KV_HEADS):
            rows = slice(g * HEAD_DIM, (g + 1) * HEAD_DIM)
            q0 = g * Q_PER_KV * HEAD_DIM
            kt = jnp.concatenate([kp_ref[rows, :], kc_ref[rows, :]], axis=1).astype(BF16)
            vt = jnp.concatenate([vp_ref[rows, :], vc_ref[rows, :]], axis=1).astype(BF16)
            qf = _lane_cat(q_ref, q0, HEAD_DIM)
            qcat = qf.astype(BF16)
            ocat = _lane_cat(o_ref, q0, HEAD_DIM)
            docat = _lane_cat(do_ref, q0, HEAD_DIM)
            dob = docat.astype(BF16)
            lse_cat = jnp.concatenate(
                [lse_ref[g * Q_PER_KV + i:g * Q_PER_KV + i + 1, :] for i in range(Q_PER_KV)], axis=1)
            sink = jnp.concatenate(
                [jnp.full((1, WINDOW), s_ref[g * Q_PER_KV + i], F32) for i in range(Q_PER_KV)], axis=1)
            s = jnp.where(valid, _dot_tn(kt, (qf * scale).astype(BF16)), NEG)
            p = jnp.exp(s - lse_cat)
            dp = _dot_tn(vt, dob)
            delta = _colsum(docat * ocat)
            dsc = (p * (dp - delta)).astype(BF16)
            dsink_row = -jnp.exp(sink - lse_cat) * delta
            dq = _dot(kt, dsc) * scale
            dk = _dot_nt(qcat, dsc) * scale
            dv = _dot_nt(dob, p.astype(BF16))
            for i in range(Q_PER_KV):
                h = g * Q_PER_KV + i
                dq_ref[h * HEAD_DIM:(h + 1) * HEAD_DIM, :] = dq[:, i * WINDOW:(i + 1) * WINDOW].astype(BF16)
                ds_ref[h:h + 1, :] += _rowsum(dsink_row[:, i * WINDOW:(i + 1) * WINDOW])
            dk_ref[rows, :] = (dk[:, WINDOW:] + dk_carry[rows, :]).astype(BF16)
            dv_ref[rows, :] = (dv[:, WINDOW:] + dv_carry[rows, :]).astype(BF16)
            dk_carry[rows, :] = dk[:, :WINDOW]
            dv_carry[rows, :] = dv[:, :WINDOW]

    kb = OFF_K // KV_DIM
    vb = OFF_V // KV_DIM
    cur = lambda i: nb - 1 - i
    prev = lambda i: jnp.maximum(nb - 2 - i, 0)
    qspec = pl.BlockSpec((Q_DIM, WINDOW), lambda i: (0, cur(i)))
    return _call(
        body,
        name="attn_bwd",
        grid=(nb,),
        in_specs=[
            pl.BlockSpec(memory_space=pltpu.SMEM),
            qspec,
            pl.BlockSpec((KV_DIM, WINDOW), lambda i: (kb, cur(i))),
            pl.BlockSpec((KV_DIM, WINDOW), lambda i: (kb, prev(i))),
            pl.BlockSpec((KV_DIM, WINDOW), lambda i: (vb, cur(i))),
            pl.BlockSpec((KV_DIM, WINDOW), lambda i: (vb, prev(i))),
            qspec,
            pl.BlockSpec((N_Q_HEADS, WINDOW), lambda i: (0, cur(i))),
            qspec,
            pl.BlockSpec(memory_space=pl.ANY),
        ],
        out_specs=[pl.BlockSpec((OFF_Z, WINDOW), lambda i: (0, cur(i))), pl.BlockSpec((N_Q_HEADS, 1), lambda i: (0, 0))],
        out_shape=[jax.ShapeDtypeStruct(dproj.shape, BF16), jax.ShapeDtypeStruct((N_Q_HEADS, 1), F32)],
        scratch_shapes=[pltpu.VMEM((KV_DIM, WINDOW), F32), pltpu.VMEM((KV_DIM, WINDOW), F32)],
        aliases={9: 0},
        semantics=("arbitrary",), args=(sinks, proj, proj, proj, proj, proj, out, lse, dout, dproj), send=send)


CONV_ROWS = 256


def _conv_silu_fwd(proj, w_col, b_col):
    t = proj.shape[1]
    r0 = OFF_X // CONV_ROWS

    def body(x_ref, w_ref, b_ref, o_ref):
        y, _ = _causal_conv(x_ref[...], w_ref[...], b_ref[...])
        o_ref[...] = y * _sigmoid(y)

    return pl.pallas_call(
        body,
        name="ssd_conv_fwd",
        grid=(XBC_DIM // CONV_ROWS,),
        in_specs=[
            pl.BlockSpec((CONV_ROWS, t), lambda i: (r0 + i, 0)),
            pl.BlockSpec((CONV_ROWS, SSD_CONV), lambda i: (i, 0)),
            pl.BlockSpec((CONV_ROWS, 1), lambda i: (i, 0)),
        ],
        out_specs=pl.BlockSpec((CONV_ROWS, t), lambda i: (i, 0)),
        out_shape=jax.ShapeDtypeStruct((XBC_DIM, t), F32),
        compiler_params=_cparams("parallel"),
    )(proj, w_col, b_col)


def _conv_silu_bwd(proj, w_col, b_col, dout, dproj):
    t = proj.shape[1]
    p0 = OFF_X // CONV_ROWS

    def body(x_ref, w_ref, b_ref, do_ref, dproj_ref, dx_ref, dwb_ref):
        xv = x_ref[...]
        wv = w_ref[...]
        y, shifted = _causal_conv(xv, wv, b_ref[...])
        sg = _sigmoid(y)
        dy = do_ref[...] * (sg * (1.0 + y * (1.0 - sg)))
        dx, dwb_ref[...] = _causal_conv_bwd(dy, xv, shifted, wv)
        dx_ref[...] = dx.astype(BF16)

    return pl.pallas_call(
        body,
        name="ssd_conv_bwd",
        grid=(XBC_DIM // CONV_ROWS,),
        in_specs=[
            pl.BlockSpec((CONV_ROWS, t), lambda i: (p0 + i, 0)),
            pl.BlockSpec((CONV_ROWS, SSD_CONV), lambda i: (i, 0)),
            pl.BlockSpec((CONV_ROWS, 1), lambda i: (i, 0)),
            pl.BlockSpec((CONV_ROWS, t), lambda i: (i, 0)),
            pl.BlockSpec(memory_space=pl.ANY),
        ],
        out_specs=[pl.BlockSpec((CONV_ROWS, t), lambda i: (p0 + i, 0)), pl.BlockSpec((CONV_ROWS, 128), lambda i: (i, 0))],
        out_shape=[jax.ShapeDtypeStruct(dproj.shape, BF16), jax.ShapeDtypeStruct((XBC_DIM, 128), F32)],
        input_output_aliases={4: 0},
        compiler_params=_cparams("parallel"),
    )(proj, w_col, b_col, dout, dproj)


def _ssd_specs(order):
    xb = D_INNER // BC_DIM
    dtb = OFF_DT // N_SSD_HEADS
    col = pl.BlockSpec((N_SSD_HEADS, 1), lambda c: (0, 0))
    return [
        pl.BlockSpec((D_INNER, CHUNK), lambda c: (0, order(c))),
        pl.BlockSpec((BC_DIM, CHUNK), lambda c: (xb, order(c))),
        pl.BlockSpec((BC_DIM, CHUNK), lambda c: (xb + 1, order(c))),
        pl.BlockSpec((N_SSD_HEADS, CHUNK), lambda c: (dtb, order(c))),
        col, col, col,
    ]


def _ssd_common(dt_ref, dtb_ref, alog_ref):
    z = dt_ref[...] + dtb_ref[...]
    dt = _softplus(z)
    a_neg = -jnp.exp(alog_ref[...])
    d_a = dt * a_neg
    row = lax.broadcasted_iota(jnp.int32, (CHUNK, CHUNK), 0)
    colm = lax.broadcasted_iota(jnp.int32, (CHUNK, CHUNK), 1)
    upper = (row <= colm).astype(F32)
    a_cs = jnp.dot(d_a, upper, precision=HIGHEST, preferred_element_type=F32)
    a_last = _rowsum(d_a)
    return z, dt, a_neg, a_cs, a_last, row >= colm, row == colm


def _decay(a_row, causal):
    a_s = jnp.broadcast_to(a_row, (CHUNK, CHUNK))
    seg = a_s.T - a_s
    return jnp.where(causal, jnp.exp(jnp.where(causal, seg, 0.0)), 0.0)


def _ssd_fwd(xbc, proj, dtb_col, alog_col, dsk_col, gnw_col):
    t = xbc.shape[1]
    nc = t // CHUNK

    def body(xs_ref, b_ref, c_ref, dt_ref, dtb_ref, alog_ref, dsk_ref, *rest):
        z_refs, (gnw_ref, y_ref, hst_ref, yn_ref, h_scr) = rest[:N_SSD_GROUPS], rest[N_SSD_GROUPS:]

        @pl.when(pl.program_id(0) == 0)
        def _():
            h_scr[...] = jnp.zeros_like(h_scr)

        _, dt, _, a_cs, a_last, causal, _ = _ssd_common(dt_ref, dtb_ref, alog_ref)
        hst_ref[0] = h_scr[...]
        dsk = dsk_ref[...]
        for g in range(N_SSD_GROUPS):
            grows = slice(g * D_STATE, (g + 1) * D_STATE)
            bb = b_ref[grows, :].astype(BF16)
            cb_ = c_ref[grows, :].astype(BF16)
            cb = _dot_tn(cb_, bb)
            for j in range(g * HEADS_PER_GROUP, (g + 1) * HEADS_PER_GROUP):
                rows = slice(j * SSD_HEAD_DIM, (j + 1) * SSD_HEAD_DIM)
                a = a_cs[j:j + 1, :]
                m = (cb * _decay(a, causal)).astype(BF16)
                xs = xs_ref[rows, :]
                xc = xs * dt[j:j + 1, :]
                hj = h_scr[rows, :]
                y = _dot_nt(xc.astype(BF16), m) + _dot(hj.astype(BF16), cb_) * jnp.exp(a) + dsk[j:j + 1, :] * xs
                y_ref[rows, :] = y
                al = a_last[j:j + 1, :]
                w = jnp.exp(al - a)
                h_scr[rows, :] = jnp.exp(al) * hj + _dot_nt((xc * w).astype(BF16), bb)
        for g in range(N_SSD_GROUPS):
            rows = slice(g * GN_ROWS, (g + 1) * GN_ROWS)
            zv = z_refs[g][...]
            u = y_ref[rows, :] * (zv * _sigmoid(zv))
            r = lax.rsqrt(jnp.mean(u * u, axis=0, keepdims=True) + EPS)
            yn_ref[rows, :] = (u * r * gnw_ref[rows, :]).astype(BF16)

    z0 = OFF_Z // GN_ROWS
    z_specs = [pl.BlockSpec((GN_ROWS, CHUNK), lambda c, g=g: (z0 + g, c)) for g in range(N_SSD_GROUPS)]
    rows_spec = pl.BlockSpec((D_INNER, CHUNK), lambda c: (0, c))
    return pl.pallas_call(
        body,
        name="ssd_fwd",
        grid=(nc,),
        in_specs=_ssd_specs(lambda c: c) + z_specs + [pl.BlockSpec((D_INNER, 1), lambda c: (0, 0))],
        out_specs=[rows_spec, pl.BlockSpec((1, D_INNER, D_STATE), lambda c: (c, 0, 0)), rows_spec],
        out_shape=[
            jax.ShapeDtypeStruct((D_INNER, t), F32),
            jax.ShapeDtypeStruct((nc, D_INNER, D_STATE), F32),
            jax.ShapeDtypeStruct((D_INNER, t), BF16),
        ],
        scratch_shapes=[pltpu.VMEM((D_INNER, D_STATE), F32)],
        compiler_params=_cparams("arbitrary"),
    )(xbc, xbc, xbc, proj, dtb_col, alog_col, dsk_col, *([proj] * N_SSD_GROUPS), gnw_col)


def _ssd_bwd(xbc, proj, dtb_col, alog_col, dsk_col, hst, dy):
    t = xbc.shape[1]
    nc = t // CHUNK
    rev = lambda c: nc - 1 - c

    def body(xs_ref, b_ref, c_ref, dt_ref, dtb_ref, alog_ref, dsk_ref, hst_ref, dy_ref,
             dxbc_ref, ddt_ref, dalog_ref, ddsk_ref, ddtb_ref, dh_scr, da_scr, ddt_scr, dd_scr):
        dxs_ref = dxbc_ref.at[pl.ds(0, D_INNER)]
        db_ref = dxbc_ref.at[pl.ds(D_INNER, BC_DIM)]
        dc_ref = dxbc_ref.at[pl.ds(D_INNER + BC_DIM, BC_DIM)]
        @pl.when(pl.program_id(0) == 0)
        def _():
            dh_scr[...] = jnp.zeros_like(dh_scr)
            dalog_ref[...] = jnp.zeros_like(dalog_ref)
            ddsk_ref[...] = jnp.zeros_like(ddsk_ref)
            ddtb_ref[...] = jnp.zeros_like(ddtb_ref)

        z, dt, a_neg, a_cs, a_last, causal, eye = _ssd_common(dt_ref, dtb_ref, alog_ref)
        dsk = dsk_ref[...]
        last_lane = lax.broadcasted_iota(jnp.int32, (1, CHUNK), 1) == CHUNK - 1
        for g in range(N_SSD_GROUPS):
            grows = slice(g * D_STATE, (g + 1) * D_STATE)
            bb = b_ref[grows, :].astype(BF16)
            cb_ = c_ref[grows, :].astype(BF16)
            cb = _dot_tn(cb_, bb)
            dcb = jnp.zeros((CHUNK, CHUNK), F32)
            dc_acc = jnp.zeros((D_STATE, CHUNK), F32)
            db_acc = jnp.zeros((D_STATE, CHUNK), F32)
            for j in range(g * HEADS_PER_GROUP, (g + 1) * HEADS_PER_GROUP):
                rows = slice(j * SSD_HEAD_DIM, (j + 1) * SSD_HEAD_DIM)
                a = a_cs[j:j + 1, :]
                al = a_last[j:j + 1, :]
                lam = _decay(a, causal)
                mf = cb * lam
                xs = xs_ref[rows, :]
                dtj = dt[j:j + 1, :]
                xc = xs * dtj
                w = jnp.exp(al - a)
                e = jnp.exp(a)
                gam = jnp.exp(al)
                hj = hst_ref[0, rows, :]
                hjb = hj.astype(BF16)
                dyv = dy_ref[rows, :]
                dyb = dyv.astype(BF16)
                dd_scr[j:j + 1, :] = _colsum(dyv * xs)
                gb = (dyv * e).astype(BF16)
                dh_in = _dot_nt(gb, cb_)
                dc_acc = dc_acc + _dot_tn(hjb, gb)
                yoff = _dot(hjb, cb_) * e
                da = _colsum(dyv * yoff)
                dm = _dot_tn(dyb, xc.astype(BF16))
                dxc = _dot(dyb, mf.astype(BF16))
                dcb = dcb + dm * lam
                nmat = dm * mf
                rs = jnp.broadcast_to(_rowsum(nmat), (CHUNK, CHUNK))
                da = da + _colsum(jnp.where(eye, rs, 0.0)) - _colsum(nmat)
                ds = dh_scr[rows, :]
                dsb = ds.astype(BF16)
                t1 = _dot(dsb, bb)
                xcw = xc * w
                dxc = dxc + w * t1
                dww = _colsum(xcw * t1)
                da_l = _rowsum(dww) + _rowsum(_colsum(ds * hj)) * gam
                da = da - dww + jnp.where(last_lane, da_l, 0.0)
                db_acc = db_acc + _dot_tn(dsb, xcw.astype(BF16))
                dh_scr[rows, :] = gam * ds + dh_in
                dxs_ref[rows, :] = dsk[j:j + 1, :] * dyv + dxc * dtj
                da_scr[j:j + 1, :] = da
                ddt_scr[j:j + 1, :] = _colsum(dxc * xs)
            dcbb = dcb.astype(BF16)
            dc_ref[grows, :] = dc_acc + _dot_nt(bb, dcbb)
            db_ref[grows, :] = db_acc + _dot(cb_, dcbb)
        dda = jnp.dot(da_scr[...], causal.astype(F32), precision=HIGHEST, preferred_element_type=F32)
        ddt = ddt_scr[...] + dda * a_neg
        ddt_raw = ddt * _sigmoid(z)
        ddt_ref[...] = ddt_raw
        ddtb_ref[...] += _rowsum(ddt_raw)
        dalog_ref[...] += _rowsum(dda * dt) * a_neg
        ddsk_ref[...] += _rowsum(dd_scr[...])

    col = pl.BlockSpec((N_SSD_HEADS, 1), lambda c: (0, 0))
    xs_spec = pl.BlockSpec((D_INNER, CHUNK), lambda c: (0, rev(c)))
    small = pltpu.VMEM((N_SSD_HEADS, CHUNK), F32)
    return pl.pallas_call(
        body,
        name="ssd_bwd",
        grid=(nc,),
        in_specs=_ssd_specs(rev) + [pl.BlockSpec((1, D_INNER, D_STATE), lambda c: (rev(c), 0, 0)), xs_spec],
        out_specs=[pl.BlockSpec((XBC_DIM, CHUNK), lambda c: (0, rev(c))),
                   pl.BlockSpec((N_SSD_HEADS, CHUNK), lambda c: (0, rev(c))), col, col, col],
        out_shape=[
            jax.ShapeDtypeStruct((XBC_DIM, t), F32),
            jax.ShapeDtypeStruct((N_SSD_HEADS, t), F32),
            jax.ShapeDtypeStruct((N_SSD_HEADS, 1), F32),
            jax.ShapeDtypeStruct((N_SSD_HEADS, 1), F32),
            jax.ShapeDtypeStruct((N_SSD_HEADS, 1), F32),
        ],
        scratch_shapes=[pltpu.VMEM((D_INNER, D_STATE), F32), small, small, small],
        compiler_params=_cparams("arbitrary"),
    )(xbc, xbc, xbc, proj, dtb_col, alog_col, dsk_col, hst, dy)


def _gnorm_bwd(dout, y, proj, w_col, send=None):
    t = y.shape[1]
    tt = _tile(t, (512, 256, 128))
    z0 = OFF_Z // GN_ROWS

    def body(do_ref, y_ref, z_ref, w_ref, dy_ref, dz_ref, dw_ref):
        @pl.when(pl.program_id(1) == 0)
        def _():
            dw_ref[...] = jnp.zeros_like(dw_ref)

        zv = z_ref[...]
        yv = y_ref[...]
        sg = _sigmoid(zv)
        sz = zv * sg
        u = yv * sz
        r = lax.rsqrt(jnp.mean(u * u, axis=0, keepdims=True) + EPS)
        xhat = u * r
        dov = do_ref[...]
        dw_ref[...] += _rowsum(dov * xhat)
        dxhat = dov * w_ref[...]
        du = r * (dxhat - xhat * jnp.mean(dxhat * xhat, axis=0, keepdims=True))
        dy_ref[...] = du * sz
        dz_ref[...] = (du * yv * (sg * (1.0 + zv * (1.0 - sg)))).astype(BF16)

    blk = pl.BlockSpec((GN_ROWS, tt), lambda g, i: (g, i))
    col = pl.BlockSpec((GN_ROWS, 1), lambda g, i: (g, 0))
    return _call(
        body,
        name="gnorm_bwd",
        grid=(N_SSD_GROUPS, t // tt),
        in_specs=[blk, blk, pl.BlockSpec((GN_ROWS, tt), lambda g, i: (z0 + g, i)), col],
        out_specs=[blk, pl.BlockSpec((GN_ROWS, tt), lambda g, i: (z0 + g, i)), col],
        out_shape=[jax.ShapeDtypeStruct((D_INNER, t), F32), jax.ShapeDtypeStruct((IN_DIM, t), BF16),
                   jax.ShapeDtypeStruct((D_INNER, 1), F32)],
        semantics=("parallel", "arbitrary"), args=(dout, y, proj, w_col), send=send)


GATE_ROWS = 128


def _gate_specs(t):
    nr = D_MODEL // GATE_ROWS
    blk = pl.BlockSpec((GATE_ROWS, t), lambda r: (r, 0))
    rows_from = lambda first: pl.BlockSpec(
        (pl.Element(GATE_ROWS), pl.Element(t)), lambda r: (pl.multiple_of(first + GATE_ROWS * r, N_SSD_HEADS), 0))
    return blk, [
        rows_from(OFF_GA),
        rows_from(OFF_GS),
        pl.BlockSpec((GATE_ROWS, 1), lambda r: (r, 0)),
        pl.BlockSpec((GATE_ROWS, 1), lambda r: (nr + r, 0)),
        blk, blk,
    ]


def _gate_fwd(proj, b_col, attn, ssd):
    t = proj.shape[1]
    blk, specs = _gate_specs(t)

    def body(ga_ref, gs_ref, ba_ref, bs_ref, a_ref, s_ref, o_ref):
        o_ref[...] = (_sigmoid(ga_ref[...] + ba_ref[...]) * a_ref[...]
                      + _sigmoid(gs_ref[...] + bs_ref[...]) * s_ref[...]).astype(BF16)

    return pl.pallas_call(
        body,
        name="gate_fwd",
        grid=(D_MODEL // GATE_ROWS,),
        in_specs=specs,
        out_specs=blk,
        out_shape=jax.ShapeDtypeStruct((D_MODEL, t), BF16),
        compiler_params=_cparams("parallel"),
    )(proj, proj, b_col, b_col, attn, ssd)


def _gate_bwd(proj, b_col, attn, ssd, dmix, send=None):
    t = proj.shape[1]
    blk, specs = _gate_specs(t)

    def body(ga_ref, gs_ref, ba_ref, bs_ref, a_ref, s_ref, dm_ref, da_ref, dso_ref, dga_ref, dgs_ref, dba_ref, dbs_ref):
        dm = dm_ref[...]
        sa = _sigmoid(ga_ref[...] + ba_ref[...])
        ss = _sigmoid(gs_ref[...] + bs_ref[...])
        da_ref[...] = (dm * sa).astype(BF16)
        dso_ref[...] = (dm * ss).astype(BF16)
        dga = dm * a_ref[...] * sa * (1.0 - sa)
        dgs = dm * s_ref[...] * ss * (1.0 - ss)
        dga_ref[...] = dga.astype(BF16)
        dgs_ref[...] = dgs.astype(BF16)
        dba_ref[...] = _rowsum(dga)
        dbs_ref[...] = _rowsum(dgs)

    col = pl.BlockSpec((GATE_ROWS, 1), lambda r: (r, 0))
    act = jax.ShapeDtypeStruct((D_MODEL, t), BF16)
    bias = jax.ShapeDtypeStruct((D_MODEL, 1), F32)
    return _call(
        body,
        name="gate_bwd",
        grid=(D_MODEL // GATE_ROWS,),
        in_specs=specs + [blk],
        out_specs=[blk, blk, blk, blk, col, col],
        out_shape=[act, act, act, act, bias, bias],
        semantics=("parallel",), args=(proj, proj, b_col, b_col, attn, ssd, dmix), send=send)


FFN_ROWS = 256


def _ffn_fwd(u0, w_col, b_col):
    t = u0.shape[2]

    def body(u_ref, w_ref, b_ref, o_ref):
        val, _ = _causal_conv(u_ref[0], w_ref[0], b_ref[0])
        gt, _ = _causal_conv(u_ref[1], w_ref[1], b_ref[1])
        o_ref[...] = (gt * _sigmoid(gt) * val).astype(BF16)

    return pl.pallas_call(
        body,
        name="ffn_fwd",
        grid=(D_FF // FFN_ROWS,),
        in_specs=[
            pl.BlockSpec((2, FFN_ROWS, t), lambda i: (0, i, 0)),
            pl.BlockSpec((2, FFN_ROWS, FFN_CONV), lambda i: (0, i, 0)),
            pl.BlockSpec((2, FFN_ROWS, 1), lambda i: (0, i, 0)),
        ],
        out_specs=pl.BlockSpec((FFN_ROWS, t), lambda i: (i, 0)),
        out_shape=jax.ShapeDtypeStruct((D_FF, t), BF16),
        compiler_params=_cparams("parallel"),
    )(u0, w_col, b_col)


def _ffn_bwd(u0, w_col, b_col, dg, send=None):
    t = u0.shape[2]

    def body(u_ref, w_ref, b_ref, dg_ref, du_ref, dwb_ref):
        xval, wval = u_ref[0], w_ref[0]
        xgt, wgt = u_ref[1], w_ref[1]
        val, sh_val = _causal_conv(xval, wval, b_ref[0])
        gt, sh_gt = _causal_conv(xgt, wgt, b_ref[1])
        sg = _sigmoid(gt)
        dgv = dg_ref[...]
        dval = dgv * (gt * sg)
        dgt = dgv * val * (sg * (1.0 + gt * (1.0 - sg)))
        dx, dwb_ref[0] = _causal_conv_bwd(dval, xval, sh_val, wval)
        du_ref[0] = dx.astype(BF16)
        dx, dwb_ref[1] = _causal_conv_bwd(dgt, xgt, sh_gt, wgt)
        du_ref[1] = dx.astype(BF16)

    return _call(
        body,
        name="ffn_bwd",
        grid=(D_FF // FFN_ROWS,),
        in_specs=[
            pl.BlockSpec((2, FFN_ROWS, t), lambda i: (0, i, 0)),
            pl.BlockSpec((2, FFN_ROWS, FFN_CONV), lambda i: (0, i, 0)),
            pl.BlockSpec((2, FFN_ROWS, 1), lambda i: (0, i, 0)),
            pl.BlockSpec((FFN_ROWS, t), lambda i: (i, 0)),
        ],
        out_specs=[pl.BlockSpec((2, FFN_ROWS, t), lambda i: (0, i, 0)), pl.BlockSpec((2, FFN_ROWS, 128), lambda i: (0, i, 0))],
        out_shape=[jax.ShapeDtypeStruct((2, D_FF, t), BF16), jax.ShapeDtypeStruct((2, D_FF, 128), F32)],
        semantics=("parallel",), args=(u0, w_col, b_col, dg), send=send)


def _adamw_math(w, g, m, v):
    m = ADAM_B1 * m + (1.0 - ADAM_B1) * g
    v = ADAM_B2 * v + (1.0 - ADAM_B2) * (g * g)
    m_hat = m / (1.0 - ADAM_B1 ** ADAM_STEP)
    v_hat = v / (1.0 - ADAM_B2 ** ADAM_STEP)
    delta = -ADAM_LR * (m_hat / (jnp.sqrt(v_hat) + ADAM_EPS) + ADAM_WD * w)
    return delta, m, v


def _adamw_sharded(parts, w, m, v, name):
    r, c = w.shape[0], w.shape[-1]
    slots = parts.shape[0]
    per_lane = 2 * r * (slots * parts.dtype.itemsize + 7 * w.dtype.itemsize)
    tc = max(d for d in range(LANES, c + 1, LANES) if c % d == 0 and (d * per_lane <= BLOCK_VMEM_BUDGET or d == LANES))
    blk_shape = (r, tc) if w.ndim == 2 else (r, 1, tc)

    def body(p_ref, w_ref, m_ref, v_ref, g_ref, d_ref, nm_ref, nv_ref):
        g = p_ref[0].astype(F32)
        for s in range(1, slots):
            g = g + p_ref[s].astype(F32)
        flat = lambda ref: ref[...].reshape(r, tc)
        d, nm, nv = _adamw_math(flat(w_ref), g, flat(m_ref), flat(v_ref))
        for ref, val in ((g_ref, g), (d_ref, d), (nm_ref, nm), (nv_ref, nv)):
            ref[...] = val.reshape(blk_shape)

    blk = pl.BlockSpec(blk_shape, (lambda i: (0, i)) if w.ndim == 2 else (lambda i: (0, 0, i)))
    out = jax.ShapeDtypeStruct(w.shape, F32)
    return pl.pallas_call(
        body,
        name=name,
        grid=(c // tc,),
        in_specs=[pl.BlockSpec((slots, r, tc), lambda i: (0, 0, i)), blk, blk, blk],
        out_specs=[blk, blk, blk, blk],
        out_shape=[out, out, out, out],
        compiler_params=_cparams("parallel"),
    )(parts, w, m, v)


def _lane_offsets(sizes):
    offsets, pos = [], 0
    for n in sizes:
        offsets.append(pos)
        pos += -(-n // 128) * 128
    return offsets, pos


def _pack_row(parts):
    rows = [p.reshape(1, -1).astype(F32) for p in parts]
    return jnp.concatenate([jnp.pad(r, ((0, 0), (0, -r.shape[1] % 128))) for r in rows], axis=1)


def _small_update(parts, me, full_sizes, ws, ms, vs):
    n = len(ws)
    offsets, _ = _lane_offsets([1] + list(full_sizes))

    def body(me_ref, p_ref, *refs):
        w_refs, m_refs, v_refs = refs[:n], refs[n:2 * n], refs[2 * n:3 * n]
        scalar_ref, out_refs = refs[3 * n], refs[3 * n + 1:]
        tot = p_ref[0]
        for s in range(1, N_DEV):
            tot = tot + p_ref[s]
        scalar_ref[...] = tot[:, 0:1]
        for k in range(n):
            g_ref, d_ref, nm_ref, nv_ref = out_refs[4 * k:4 * k + 4]
            taps, cols = w_refs[k].shape
            if taps == 1:
                g_ref[...] = tot[:, offsets[k + 1]:offsets[k + 1] + cols]
            else:
                full = full_sizes[k] // taps
                for tap in range(taps):
                    mine = jnp.zeros((1, cols), F32)
                    for d in range(N_DEV):
                        lo = offsets[k + 1] + tap * full + d * cols
                        mine = jnp.where(me_ref[0] == d, tot[:, lo:lo + cols], mine)
                    g_ref[tap:tap + 1, :] = mine
            d_ref[...], nm_ref[...], nv_ref[...] = _adamw_math(w_refs[k][...], g_ref[...], m_refs[k][...], v_refs[k][...])

    vmem = pl.BlockSpec(memory_space=pltpu.VMEM)
    out_shape = [jax.ShapeDtypeStruct((1, 1), F32)]
    for wk in ws:
        out_shape += [jax.ShapeDtypeStruct(wk.shape, F32)] * 4
    res = pl.pallas_call(
        body,
        name="small_update",
        in_specs=[pl.BlockSpec(memory_space=pltpu.SMEM)] + [vmem] * (1 + 3 * n),
        out_specs=[vmem] * len(out_shape),
        out_shape=out_shape,
    )(me, parts, *ws, *ms, *vs)
    return res[0], [res[1 + 4 * k:5 + 4 * k] for k in range(n)]


ANY = pl.BlockSpec(memory_space=pl.ANY)
FLIPS = [(k >> 2 & 1, k >> 1 & 1, k & 1) for k in range(1, N_DEV)]


def _place():
    return lax.axis_index("x"), lax.axis_index("y"), lax.axis_index("c")


HBM = pl.BlockSpec(memory_space=pltpu.HBM)
SEM = pl.BlockSpec(memory_space=pltpu.SEMAPHORE)
EFFECT = pltpu.SideEffectType.DATAFLOW_SIDE_EFFECTING


def _peer_copy(gather, src_ref, land_ref, send_sems, recv_sems, k, sending):
    x, y, c = _place()
    fx, fy, fc = FLIPS[k]
    me = 4 * x + 2 * y + c
    peer = 4 * (x ^ fx) + 2 * (y ^ fy) + (c ^ fc)
    return pltpu.make_async_remote_copy(
        src_ref=src_ref if gather else src_ref.at[peer],
        dst_ref=land_ref.at[me if sending else peer],
        send_sem=send_sems.at[k], recv_sem=recv_sems.at[k],
        device_id=(x ^ fx, y ^ fy, c ^ fc), device_id_type=MESH)


SIBLING = 0
OTHER_CHIPS = (1, 3, 5)


def _gather_start(srcs, name, via_sibling):
    n = len(srcs)
    lands = [lax.empty((N_DEV,) + s.shape, s.dtype) for s in srcs]

    def body(*refs):
        src_refs, land_refs = refs[:n], refs[n:2 * n]
        send, recv = refs[2 * n:3 * n], refs[3 * n:4 * n]
        for i in range(n):
            for k in (SIBLING,) + OTHER_CHIPS if via_sibling else range(N_DEV - 1):
                _peer_copy(True, src_refs[i], land_refs[i], send[i], recv[i], k, True).start()

    sem = pltpu.SemaphoreType.DMA((N_DEV - 1,))
    hbm = lambda a: pltpu.HBM(a.shape, a.dtype)
    res = pl.pallas_call(
        body,
        name=name,
        in_specs=[HBM] * (2 * n),
        out_specs=[SEM] * (2 * n) + [HBM] * (2 * n),
        out_shape=[sem] * (2 * n) + [hbm(s) for s in srcs] + [hbm(a) for a in lands],
        input_output_aliases={i: 2 * n + i for i in range(2 * n)},
        compiler_params=pltpu.CompilerParams(has_side_effects=EFFECT),
    )(*[pltpu.with_memory_space_constraint(a, pltpu.HBM) for a in list(srcs) + lands])
    return res[:n], res[n:2 * n], res[2 * n:3 * n], res[3 * n:4 * n]


def _exchange_wait(send_sems, recv_sems, src, land, after, gather, name):
    def body(src_ref, land_ref, send_ref, recv_ref, after_ref, src_out, land_out):
        for k in range(N_DEV - 1):
            cp = _peer_copy(gather, src_ref, land_ref, send_ref, recv_ref, k, False)
            cp.wait_send()
            cp.wait_recv()

    hbm = lambda a: pltpu.HBM(a.shape, a.dtype)
    return pl.pallas_call(
        body,
        name=name,
        in_specs=[HBM, HBM, SEM, SEM, ANY],
        out_specs=[HBM, HBM],
        out_shape=[hbm(src), hbm(land)],
        input_output_aliases={0: 0, 1: 1},
        compiler_params=pltpu.CompilerParams(has_side_effects=EFFECT),
    )(src, land, send_sems, recv_sems, after)


def _own_slot(src, land, me, gather):
    own = src[None] if gather else lax.dynamic_slice_in_dim(src, me, 1, axis=0)
    return lax.dynamic_update_slice_in_dim(land, own, me, axis=0)


def _forwarded_copy(land_ref, send_sems, recv_sems, j, sending):
    x, y, c = _place()
    fx, fy, _ = FLIPS[OTHER_CHIPS[j]]
    slot = 4 * (x ^ fx) + 2 * (y ^ fy) + (c if sending else 1 - c)
    return pltpu.make_async_remote_copy(
        src_ref=land_ref.at[slot], dst_ref=land_ref.at[slot], send_sem=send_sems.at[j], recv_sem=recv_sems.at[j],
        device_id=(x, y, 1 - c), device_id_type=MESH)


def _gather_forward(send_sems, recv_sems, srcs, lands, after, name):
    n = len(srcs)

    def body(*refs):
        src_refs, land_refs = refs[:n], refs[n:2 * n]
        send, recv = refs[2 * n:3 * n], refs[3 * n:4 * n]
        fwd_send, fwd_recv = refs[4 * n + 1:5 * n + 1], refs[5 * n + 1:6 * n + 1]
        for i in range(n):
            for j, k in enumerate(OTHER_CHIPS):
                _peer_copy(True, src_refs[i], land_refs[i], send[i], recv[i], k, False).wait_recv()
                _forwarded_copy(land_refs[i], fwd_send[i], fwd_recv[i], j, True).start()

    sem = pltpu.SemaphoreType.DMA((len(OTHER_CHIPS),))
    hbm = lambda a: pltpu.HBM(a.shape, a.dtype)
    res = pl.pallas_call(
        body,
        name=name,
        in_specs=[HBM] * (2 * n) + [SEM] * (2 * n) + [ANY],
        out_specs=[SEM] * (2 * n) + [HBM] * (2 * n),
        out_shape=[sem] * (2 * n) + [hbm(a) for a in srcs] + [hbm(a) for a in lands],
        input_output_aliases={i: 2 * n + i for i in range(2 * n)},
        compiler_params=pltpu.CompilerParams(has_side_effects=EFFECT),
    )(*srcs, *lands, *send_sems, *recv_sems, after)
    return res[:n], res[n:2 * n], res[2 * n:3 * n], res[3 * n:4 * n]


def _gather_wait_forwarded(send_sems, recv_sems, fwd_send, fwd_recv, src, land, after, name):
    def body(src_ref, land_ref, send_ref, recv_ref, fwd_send_ref, fwd_recv_ref, after_ref, src_out, land_out):
        for k in (SIBLING,) + OTHER_CHIPS:
            _peer_copy(True, src_ref, land_ref, send_ref, recv_ref, k, False).wait_send()
        _peer_copy(True, src_ref, land_ref, send_ref, recv_ref, SIBLING, False).wait_recv()
        for j in range(len(OTHER_CHIPS)):
            _forwarded_copy(land_ref, fwd_send_ref, fwd_recv_ref, j, True).wait_send()
            _forwarded_copy(land_ref, fwd_send_ref, fwd_recv_ref, j, False).wait_recv()

    hbm = lambda a: pltpu.HBM(a.shape, a.dtype)
    return pl.pallas_call(
        body,
        name=name,
        in_specs=[HBM, HBM, SEM, SEM, SEM, SEM, ANY],
        out_specs=[HBM, HBM],
        out_shape=[hbm(src), hbm(land)],
        input_output_aliases={0: 0, 1: 1},
        compiler_params=pltpu.CompilerParams(has_side_effects=EFFECT),
    )(src, land, send_sems, recv_sems, fwd_send, fwd_recv, after)


N_CHIPS = N_DEV // 2


def _pair_exchange(by_core, meanwhile, name):
    def copy(src_ref, land_ref, send_sems, recv_sems, q):
        x, y, c = _place()
        return pltpu.make_async_remote_copy(
            src_ref=src_ref.at[q, 1 - c], dst_ref=land_ref.at[q], send_sem=send_sems.at[q], recv_sem=recv_sems.at[q],
            device_id=(x, y, 1 - c), device_id_type=MESH)

    def start(src_ref, land_ref, send_sems, recv_sems, src_out, land_out):
        for q in range(N_CHIPS):
            copy(src_ref, land_ref, send_sems, recv_sems, q).start()

    def wait(src_ref, land_ref, send_sems, recv_sems, after_ref, src_out, land_out):
        for q in range(N_CHIPS):
            cp = copy(src_ref, land_ref, send_sems, recv_sems, q)
            cp.wait_send()
            cp.wait_recv()

    sem = pltpu.SemaphoreType.DMA((N_CHIPS,))
    hbm_src = pltpu.HBM(by_core.shape, by_core.dtype)
    hbm_land = pltpu.HBM(by_core.shape[:1] + by_core.shape[2:], by_core.dtype)
    params = pltpu.CompilerParams(has_side_effects=EFFECT)
    send_sems, recv_sems, src, land = pl.pallas_call(
        start, name=name + "_start", in_specs=[HBM, HBM], out_specs=[SEM, SEM, HBM, HBM],
        out_shape=[sem, sem, hbm_src, hbm_land], input_output_aliases={0: 2, 1: 3}, compiler_params=params,
    )(pltpu.with_memory_space_constraint(by_core, pltpu.HBM),
      pltpu.with_memory_space_constraint(lax.empty(hbm_land.shape, by_core.dtype), pltpu.HBM))
    return pl.pallas_call(
        wait, name=name + "_wait", in_specs=[HBM, HBM, SEM, SEM, ANY], out_specs=[HBM, HBM],
        out_shape=[hbm_src, hbm_land], input_output_aliases={0: 0, 1: 1}, compiler_params=params,
    )(src, land, send_sems, recv_sems, meanwhile(src))


def _pair_add(by_core, landed, name):
    q, _, r, c = by_core.shape
    tc = _tile(c, (512, 256, 128))

    def body(a_ref, b_ref, o_ref):
        mine = a_ref[0, lax.axis_index("c")]
        o_ref[0] = (mine.astype(F32) + b_ref[0].astype(F32)).astype(BF16)

    blk = pl.BlockSpec((1, r, tc), lambda i, j: (i, 0, j))
    return pl.pallas_call(
        body, name=name, grid=(q, c // tc),
        in_specs=[pl.BlockSpec((1, 2, r, tc), lambda i, j: (i, 0, 0, j)), blk], out_specs=blk,
        out_shape=jax.ShapeDtypeStruct(landed.shape, BF16), compiler_params=_cparams("parallel", "parallel"),
    )(by_core, landed)


def _chip_copy(src_ref, land_ref, send_sems, recv_sems, j, sending):
    x, y, c = _place()
    fx, fy, _ = FLIPS[OTHER_CHIPS[j]]
    here, there = 2 * x + y, 2 * (x ^ fx) + (y ^ fy)
    return pltpu.make_async_remote_copy(
        src_ref=src_ref.at[there], dst_ref=land_ref.at[here if sending else there],
        send_sem=send_sems.at[j], recv_sem=recv_sems.at[j],
        device_id=(x ^ fx, y ^ fy, c), device_id_type=MESH)


def _chip_wait(send_sems, recv_sems, src, land, after, name):
    def body(src_ref, land_ref, send_ref, recv_ref, after_ref, src_out, land_out):
        for j in range(len(OTHER_CHIPS)):
            cp = _chip_copy(src_ref, land_ref, send_ref, recv_ref, j, False)
            cp.wait_send()
            cp.wait_recv()

    hbm = lambda a: pltpu.HBM(a.shape, a.dtype)
    return pl.pallas_call(
        body,
        name=name,
        in_specs=[HBM, HBM, SEM, SEM, ANY],
        out_specs=[HBM, HBM],
        out_shape=[hbm(src), hbm(land)],
        input_output_aliases={0: 0, 1: 1},
        compiler_params=pltpu.CompilerParams(has_side_effects=EFFECT),
    )(src, land, send_sems, recv_sems, after)


def _col(v):
    return v.reshape(-1, 1).astype(F32)


def _local_step(x, tgt, started, weight, small, pair_sums, handles):
    t = x.shape[0]
    n1 = _col(small["norm1_w"])
    n2 = _col(small["norm2_w"])
    nf = _col(small["final_norm_w"])
    bg = _col(small["b_gate"])
    sinks = small["attn_sinks"].reshape(-1).astype(F32)
    cbias = _col(small["ssd_conv_b"])
    dtb = _col(small["dt_bias"])
    alog = _col(small["a_log"])
    dsk = _col(small["d_skip"])
    gnw = _col(small["ssd_norm_w"])
    fb = small["ffn_conv_b"].reshape(2, D_FF, 1)

    xt, xn = _norm_fwd_tokens(x, n1, started, "norm1_fwd")
    cw = weight("ssd_conv_w", xn).T
    fw = weight("ffn_conv_w", xn).T.reshape(2, D_FF, FFN_CONV)
    w_in_t = weight("w_in", xn)
    proj = _matmul(w_in_t, xn, nt=False, out_dtype=F32, name="mm_in")
    ao, lse = _attn_fwd(proj, sinks)
    w_ao = weight("w_attn_o", ao)
    attn = _matmul(w_ao, ao, nt=False, out_dtype=F32, name="mm_attn_o", tn_a=True)
    xbc = _conv_silu_fwd(proj, cw, cbias)
    y, hst, yn = _ssd_fwd(xbc, proj, dtb, alog, dsk, gnw)
    w_so = weight("w_ssd_o", yn)
    ssd = _matmul(w_so, yn, nt=False, out_dtype=F32, name="mm_ssd_o", tn_a=True)
    mix = _gate_fwd(proj, bg, attn, ssd)
    w_out = weight("w_out", mix)
    h1 = _matmul(w_out, mix, nt=False, out_dtype=F32, name="mm_out", add=xt, tn_a=True)
    hn = _norm_fwd(h1, n2, "norm2_fwd")
    w_up_t = weight("w_up", hn)
    u0 = _matmul(w_up_t, hn, nt=False, out_dtype=F32, name="mm_up").reshape(2, D_FF, t)
    gl = _ffn_fwd(u0, fw, fb)
    w_down = weight("w_down", gl)
    h2 = _matmul(w_down, gl, nt=False, out_dtype=F32, name="mm_down", add=h1, tn_a=True)
    dh2, loss, d_nf = _final_norm_loss(h2, tgt, nf)

    g = {}

    def sending(weight_name, grad, fn, *args, **kwargs):
        chunks = grad if grad.ndim == 3 else grad.reshape(N_DEV, -1, D_MODEL)
        out, handles[weight_name] = fn(*args, send=chunks, **kwargs)
        return out

    g_down = _matmul(gl, dh2, nt=True, out_dtype=BF16, name="mm_d_w_down")
    dgl = _matmul(w_down, dh2, nt=False, out_dtype=F32, name="mm_d_glu")
    du0, d_fwb = sending("w_down", g_down, _ffn_bwd, u0, fw, fb, dgl)
    du0 = du0.reshape(2 * D_FF, t)
    g_up = _matmul(du0, hn, nt=True, out_dtype=BF16, name="mm_d_w_up")
    dhn = sending("w_up", g_up, _matmul, w_up_t, du0, nt=False, out_dtype=F32, name="mm_d_hn", tn_a=True)
    dh1, d_n2 = _norm_bwd(dhn, h1, n2, dh2, "norm2_bwd")
    g_out = _matmul(mix, dh1, nt=True, out_dtype=BF16, name="mm_d_w_out")
    dmix = _matmul(w_out, dh1, nt=False, out_dtype=F32, name="mm_d_mix")
    d_attn, d_ssd, d_ga, d_gs, d_ba, d_bs = sending("w_out", g_out, _gate_bwd, proj, bg, attn, ssd, dmix)
    g_ao = _matmul(ao, d_attn, nt=True, out_dtype=BF16, name="mm_d_w_attn_o")
    dao = _matmul(w_ao, d_attn, nt=False, out_dtype=F32, name="mm_d_ao")
    g_so = _matmul(yn, d_ssd, nt=True, out_dtype=BF16, name="mm_d_w_ssd_o")
    dyn = _matmul(w_so, d_ssd, nt=False, out_dtype=F32, name="mm_d_yn")
    dy, dproj, d_gnw = sending("w_ssd_o", g_so, _gnorm_bwd, dyn, y, proj, gnw)
    dxbc, ddt, d_alog, d_dsk, d_dtb = _ssd_bwd(xbc, proj, dtb, alog, dsk, hst, dy)
    dproj, dwb_conv = _conv_silu_bwd(proj, cw, cbias, dxbc, dproj)
    dproj, d_sinks = sending("w_attn_o", g_ao, _attn_bwd, proj, sinks, ao, lse, dao, dproj)
    for rows, part in ((OFF_DT, ddt.astype(BF16)), (OFF_GA, d_ga), (OFF_GS, d_gs)):
        dproj = lax.dynamic_update_slice(dproj, part, (rows, 0))
    g_in = pair_sums(_matmul(dproj, xn, nt=True, out_dtype=BF16, name="mm_d_w_in"))
    dxn = sending("w_in", g_in, _matmul, w_in_t, dproj, nt=False, out_dtype=F32, name="mm_d_xn", tn_a=True)
    dx, d_n1 = _norm_bwd(dxn, xt, n1, dh1, "norm1_bwd", tokens_out=True)

    g["norm1_w"] = d_n1
    g["b_gate"] = jnp.concatenate([d_ba, d_bs], axis=0)
    g["attn_sinks"] = d_sinks
    g["ssd_conv_w"] = dwb_conv[:, :SSD_CONV].T
    g["ssd_conv_b"] = dwb_conv[:, SSD_CONV]
    g["dt_bias"] = d_dtb
    g["a_log"] = d_alog
    g["d_skip"] = d_dsk
    g["ssd_norm_w"] = d_gnw
    g["norm2_w"] = d_n2
    d_fwb = d_fwb.reshape(2 * D_FF, 128)
    g["ffn_conv_w"] = d_fwb[:, :FFN_CONV].T
    g["ffn_conv_b"] = d_fwb[:, FFN_CONV]
    g["final_norm_w"] = d_nf
    return loss, dx, g


SMALL = ("norm1_w", "b_gate", "attn_sinks", "ssd_conv_w", "ssd_conv_b", "dt_bias", "a_log", "d_skip", "ssd_norm_w",
         "norm2_w", "ffn_conv_w", "ffn_conv_b", "final_norm_w")
WEIGHT_ORDER = ("norm1_w", "w_in", "b_gate", "attn_sinks", "w_attn_o", "ssd_conv_w", "ssd_conv_b", "dt_bias", "a_log",
                "d_skip", "ssd_norm_w", "w_ssd_o", "w_out", "norm2_w", "w_up", "ffn_conv_w", "ffn_conv_b", "w_down",
                "final_norm_w")


def kernel(x, norm1_w, w_in, b_gate, attn_sinks, w_attn_o, ssd_conv_w, ssd_conv_b, dt_bias, a_log, d_skip, ssd_norm_w, w_ssd_o, w_out, norm2_w, w_up, ffn_conv_w, ffn_conv_b, w_down, final_norm_w, loss_target, m_norm1_w, m_w_in, m_b_gate, m_attn_sinks, m_w_attn_o, m_ssd_conv_w, m_ssd_conv_b, m_dt_bias, m_a_log, m_d_skip, m_ssd_norm_w, m_w_ssd_o, m_w_out, m_norm2_w, m_w_up, m_ffn_conv_w, m_ffn_conv_b, m_w_down, m_final_norm_w, v_norm1_w, v_w_in, v_b_gate, v_attn_sinks, v_w_attn_o, v_ssd_conv_w, v_ssd_conv_b, v_dt_bias, v_a_log, v_d_skip, v_ssd_norm_w, v_w_ssd_o, v_w_out, v_norm2_w, v_w_up, v_ffn_conv_w, v_ffn_conv_b, v_w_down, v_final_norm_w):
    w = dict(norm1_w=norm1_w, w_in=w_in, b_gate=b_gate, attn_sinks=attn_sinks, w_attn_o=w_attn_o, ssd_conv_w=ssd_conv_w, ssd_conv_b=ssd_conv_b, dt_bias=dt_bias, a_log=a_log, d_skip=d_skip, ssd_norm_w=ssd_norm_w, w_ssd_o=w_ssd_o, w_out=w_out, norm2_w=norm2_w, w_up=w_up, ffn_conv_w=ffn_conv_w, ffn_conv_b=ffn_conv_b, w_down=w_down, final_norm_w=final_norm_w)
    m = dict(norm1_w=m_norm1_w, w_in=m_w_in, b_gate=m_b_gate, attn_sinks=m_attn_sinks, w_attn_o=m_w_attn_o, ssd_conv_w=m_ssd_conv_w, ssd_conv_b=m_ssd_conv_b, dt_bias=m_dt_bias, a_log=m_a_log, d_skip=m_d_skip, ssd_norm_w=m_ssd_norm_w, w_ssd_o=m_w_ssd_o, w_out=m_w_out, norm2_w=m_norm2_w, w_up=m_w_up, ffn_conv_w=m_ffn_conv_w, ffn_conv_b=m_ffn_conv_b, w_down=m_w_down, final_norm_w=m_final_norm_w)
    v = dict(norm1_w=v_norm1_w, w_in=v_w_in, b_gate=v_b_gate, attn_sinks=v_attn_sinks, w_attn_o=v_w_attn_o, ssd_conv_w=v_ssd_conv_w, ssd_conv_b=v_ssd_conv_b, dt_bias=v_dt_bias, a_log=v_a_log, d_skip=v_d_skip, ssd_norm_w=v_ssd_norm_w, w_ssd_o=v_w_ssd_o, w_out=v_w_out, norm2_w=v_norm2_w, w_up=v_w_up, ffn_conv_w=v_ffn_conv_w, ffn_conv_b=v_ffn_conv_b, w_down=v_w_down, final_norm_w=v_final_norm_w)
    me = 4 * lax.axis_index("x") + 2 * lax.axis_index("y") + lax.axis_index("c")

    shards = {"ssd_conv_w": ssd_conv_w[0], "ffn_conv_w": ffn_conv_w[0], "w_in": w_in[0].T.astype(BF16),
              "w_attn_o": w_attn_o[0].astype(BF16), "w_ssd_o": w_ssd_o[0].astype(BF16), "w_out": w_out[0].astype(BF16),
              "w_up": w_up[0].T.astype(BF16), "w_down": w_down[0].astype(BF16)}
    order = list(shards)
    g_send, g_recv, g_src, g_land = _gather_start(list(shards.values()), "gather_start", True)
    first = ("ssd_conv_w", "ffn_conv_w", "w_in")
    forwarded = {}

    def weight(name, after):
        if name not in forwarded:
            group = [k for k in order if (k in first) == (name in first)]
            idx = [order.index(k) for k in group]
            handles = _gather_forward([g_send[i] for i in idx], [g_recv[i] for i in idx], [g_src[i] for i in idx],
                                      [g_land[i] for i in idx], after, "gather_forward_for_" + name)
            forwarded.update(zip(group, zip(*handles)))
        i = order.index(name)
        src, land = _gather_wait_forwarded(g_send[i], g_recv[i], *forwarded[name], after, "gather_wait_" + name)
        land = _own_slot(src, land, me, True)
        if name == "ssd_conv_w":
            return jnp.transpose(land, (1, 0, 2)).reshape(SSD_CONV, XBC_DIM)
        if name == "ffn_conv_w":
            return jnp.transpose(land, (1, 0, 2)).reshape(FFN_CONV, 2 * D_FF)
        return land.reshape(-1, D_MODEL)

    res, pending = {}, {}

    def update(name, after):
        if name == "w_in":
            parts = _own_slot(*_chip_wait(*pending[name], after, "grad_wait_" + name), me // 2, False)
        else:
            parts = _own_slot(*_exchange_wait(*pending[name], after, False, "grad_wait_" + name), me, False)
        view, back = {
            "w_in": (lambda a: jnp.transpose(a, (2, 0, 1)), lambda r: jnp.transpose(r, (1, 2, 0))),
            "w_up": (lambda a: a[0].T, lambda r: r.T[None]),
        }.get(name, (lambda a: a[0], lambda r: r[None]))
        done = _adamw_sharded(parts, view(w[name]), view(m[name]), view(v[name]), "adamw_" + name)
        res[name] = [back(r) for r in done]
        return done[0]

    def pair_sums(grad):
        by_core, landed = _pair_exchange(grad.reshape(N_CHIPS, 2, -1, D_MODEL),
                                         lambda started: update("w_up", update("w_down", started)), "grad_pair_w_in")
        return _pair_add(by_core, landed, "grad_pair_add_w_in")

    small = {k: w[k][0] if k != "final_norm_w" else w[k] for k in SMALL}
    loss, dx, g = _local_step(x[0], loss_target[0], g_src[0], weight, small, pair_sums, pending)

    packed = _pack_row([loss] + [g[k] for k in SMALL])
    s_send, s_recv, s_src, s_land = _gather_start([packed], "small_grads_start", False)
    after = s_src[0]
    for name in ("w_out", "w_attn_o", "w_ssd_o", "w_in"):
        after = update(name, after)

    rows = _own_slot(*_exchange_wait(s_send[0], s_recv[0], s_src[0], s_land[0], after, True, "small_grads_wait"),
                     me, True)
    flat = lambda a: a.reshape(-1, a.shape[-1])
    loss_sum, updates = _small_update(
        rows, me.reshape(1), [g[k].size for k in SMALL],
        [flat(w[k]) for k in SMALL], [flat(m[k]) for k in SMALL], [flat(v[k]) for k in SMALL])
    for k, upd in zip(SMALL, updates):
        res[k] = [u.reshape(w[k].shape) for u in upd]

    grad_x = dx[None]
    outs = [loss_sum.reshape(()), grad_x]
    for i in range(4):
        outs.extend(res[k][i] for k in WEIGHT_ORDER)
    return tuple(outs)
```

```python
import jax
import jax.numpy as jnp
from jax import lax
from jax.experimental import pallas as pl
from jax.experimental.pallas import tpu as pltpu

F32 = jnp.float32
BF16 = jnp.bfloat16
HIGHEST = lax.Precision.HIGHEST

D_MODEL = 1024
N_Q_HEADS = 16
N_KV_HEADS = 4
HEAD_DIM = 64
WINDOW = 128
Q_PER_KV = N_Q_HEADS // N_KV_HEADS
Q_DIM = N_Q_HEADS * HEAD_DIM
KV_DIM = N_KV_HEADS * HEAD_DIM
D_INNER = 2048
SSD_HEAD_DIM = 64
N_SSD_HEADS = 32
N_SSD_GROUPS = 4
HEADS_PER_GROUP = N_SSD_HEADS // N_SSD_GROUPS
D_STATE = 128
GN_ROWS = D_INNER // N_SSD_GROUPS
BC_DIM = N_SSD_GROUPS * D_STATE
XBC_DIM = D_INNER + 2 * BC_DIM
SSD_CONV = 4
CHUNK = 128
D_FF = 2816
FFN_CONV = 3
EPS = 1e-5
NEG = -1e30
IN_DIM = 8736
N_DEV = 8

OFF_Q = 0
OFF_K = OFF_Q + Q_DIM
OFF_V = OFF_K + KV_DIM
OFF_Z = OFF_V + KV_DIM
OFF_X = OFF_Z + D_INNER
OFF_DT = OFF_X + XBC_DIM
OFF_GA = OFF_DT + N_SSD_HEADS
OFF_GS = OFF_GA + D_MODEL

ADAM_LR = 0.001
ADAM_B1 = 0.9
ADAM_B2 = 0.999
ADAM_EPS = 1e-08
ADAM_WD = 0.01
ADAM_STEP = 10

LANES = 128
BF16_TILE_ROWS = 16
VMEM_BYTES = 64 * 1024 * 1024
VMEM_LIMIT = VMEM_BYTES * 3 // 4
MESH = pl.DeviceIdType.MESH


def _cparams(*sem):
    return pltpu.CompilerParams(dimension_semantics=sem, vmem_limit_bytes=VMEM_LIMIT)


def _tile(n, prefs):
    for p in prefs:
        if n % p == 0:
            return p
    return n


def _sigmoid(x):
    return 1.0 / (1.0 + jnp.exp(-x))


def _softplus(x):
    return jnp.maximum(x, 0.0) + jnp.log(1.0 + jnp.exp(-jnp.abs(x)))


def _rowsum(x):
    return jnp.sum(x, axis=1, keepdims=True)


def _colsum(x):
    return jnp.sum(x, axis=0, keepdims=True)


def _dot(a, b):
    return jnp.dot(a, b, preferred_element_type=F32)


def _dot_nt(a, b):
    return lax.dot_general(a, b, (((1,), (1,)), ((), ())), preferred_element_type=F32)


def _dot_tn(a, b):
    return lax.dot_general(a, b, (((0,), (0,)), ((), ())), preferred_element_type=F32)


def _shift_right(x, j):
    if j == 0:
        return x
    r = pltpu.roll(x, j, 1)
    lane = lax.broadcasted_iota(jnp.int32, (x.shape[0], 128), 1)
    return jnp.concatenate([jnp.where(lane >= j, r[:, :128], 0.0), r[:, 128:]], axis=1)


def _shift_left(x, j):
    if j == 0:
        return x
    n = x.shape[1]
    r = pltpu.roll(x, n - j, 1)
    lane = lax.broadcasted_iota(jnp.int32, (x.shape[0], 128), 1)
    return jnp.concatenate([r[:, :n - 128], jnp.where(lane < 128 - j, r[:, n - 128:], 0.0)], axis=1)


def _causal_conv(xv, wv, bv):
    taps = wv.shape[1]
    shifted = [_shift_right(xv, taps - 1 - k) for k in range(taps - 1)]
    y = bv + wv[:, taps - 1:taps] * xv
    for k in range(taps - 1):
        y = y + wv[:, k:k + 1] * shifted[k]
    return y, shifted


def _causal_conv_bwd(dy, xv, shifted, wv):
    taps = wv.shape[1]
    lane = lax.broadcasted_iota(jnp.int32, (dy.shape[0], 128), 1)
    dwb = jnp.where(lane == taps, _rowsum(dy), 0.0)
    dwb = jnp.where(lane == taps - 1, _rowsum(dy * xv), dwb)
    dx = wv[:, taps - 1:taps] * dy
    for k in range(taps - 1):
        dx = dx + wv[:, k:k + 1] * _shift_left(dy, taps - 1 - k)
        dwb = jnp.where(lane == k, _rowsum(dy * shifted[k]), dwb)
    return dx, dwb


def _call(body, *, name, grid, in_specs, out_specs, out_shape, args, semantics, scratch_shapes=(), aliases=None,
          send=None):
    aliases = dict(aliases or {})
    if send is None:
        return pl.pallas_call(body, name=name, grid=grid, in_specs=in_specs, out_specs=out_specs, out_shape=out_shape,
                              scratch_shapes=list(scratch_shapes), input_output_aliases=aliases,
                              compiler_params=_cparams(*semantics))(*args)
    single = not isinstance(out_specs, (list, tuple))
    out_specs, out_shape = ([out_specs], [out_shape]) if single else (list(out_specs), list(out_shape))
    n_in, n_out = len(in_specs), len(out_specs)
    chips = send.shape[0] == N_DEV // 2
    n_copies = len(OTHER_CHIPS) if chips else N_DEV - 1

    def sending(*refs):
        ins, (src_ref, land_ref) = refs[:n_in], refs[n_in:n_in + 2]
        outs = refs[n_in + 2:n_in + 2 + n_out]
        send_sems, recv_sems = refs[n_in + 2 + n_out:n_in + 4 + n_out]
        scratch = refs[n_in + 6 + n_out:]
        step = 0
        for axis, size in enumerate(grid):
            step = step * size + pl.program_id(axis)

        @pl.when(step == 0)
        def _():
            for k in range(n_copies):
                if chips:
                    _chip_copy(src_ref, land_ref, send_sems, recv_sems, k, True).start()
                else:
                    _peer_copy(False, src_ref, land_ref, send_sems, recv_sems, k, True).start()

        body(*ins, *outs, *scratch)

    sem = pltpu.SemaphoreType.DMA((n_copies,))
    hbm = pltpu.HBM(send.shape, send.dtype)
    res = pl.pallas_call(
        sending, name=name, grid=grid,
        in_specs=list(in_specs) + [HBM, HBM],
        out_specs=out_specs + [SEM, SEM, HBM, HBM],
        out_shape=out_shape + [sem, sem, hbm, hbm],
        input_output_aliases={**aliases, n_in: n_out + 2, n_in + 1: n_out + 3},
        scratch_shapes=list(scratch_shapes),
        compiler_params=pltpu.CompilerParams(dimension_semantics=("arbitrary",) * len(grid), vmem_limit_bytes=VMEM_LIMIT,
                                             has_side_effects=EFFECT),
    )(*args, pltpu.with_memory_space_constraint(send, pltpu.HBM),
      pltpu.with_memory_space_constraint(lax.empty(send.shape, send.dtype), pltpu.HBM))
    return (res[0] if single else list(res[:n_out])), tuple(res[n_out:])


BLOCK_VMEM_BUDGET = VMEM_LIMIT * 3 // 4
MATMUL_MAX_TM = 768
MATMUL_MAX_TN = 3072
MATMUL_MAX_TK = 3072


def _largest_tile(n, align, cap):
    return max(d for d in range(align, min(n, cap) + 1, align) if n % d == 0)


def _matmul_tiles(m, n, k, a_bytes, b_bytes, out_bytes, has_add, m_align, k_align):
    tm = _largest_tile(m, m_align, MATMUL_MAX_TM)
    tk = _largest_tile(k, k_align, MATMUL_MAX_TK)
    for tn in sorted({d for d in range(LANES, min(n, MATMUL_MAX_TN) + 1, LANES) if n % d == 0}, reverse=True):
        need = 2 * (tm * tk * a_bytes + tk * tn * b_bytes) + tm * tn * (2 * out_bytes + (4 if k > tk else 0) + (8 if has_add else 0))
        if need <= BLOCK_VMEM_BUDGET:
            return tm, tn, tk
    return tm, LANES, tk


def _matmul(a, b, *, nt, out_dtype, name, add=None, tn_a=False, send=None):
    if tn_a:
        k, m = a.shape
    else:
        m, k = a.shape
    n = b.shape[0] if nt else b.shape[1]
    tm, tn, tk = _matmul_tiles(m, n, k, a.dtype.itemsize, b.dtype.itemsize, jnp.dtype(out_dtype).itemsize, add is not None,
                               LANES if tn_a else BF16_TILE_ROWS, BF16_TILE_ROWS if tn_a and not nt else LANES)
    nk = k // tk
    grid = (m // tm, n // tn, nk)

    def body(a_ref, b_ref, *rest):
        r_ref = None
        if add is not None:
            r_ref, rest = rest[0], rest[1:]
        o_ref = rest[0]
        av = a_ref[...].astype(BF16)
        bv = b_ref[...].astype(BF16)
        part = _dot_tn(av, bv) if tn_a else _dot_nt(av, bv) if nt else _dot(av, bv)

        def finish(r):
            if add is not None:
                r = r + r_ref[...]
            o_ref[...] = r.astype(out_dtype)

        if nk == 1:
            finish(part)
            return
        acc = rest[1]
        kk = pl.program_id(2)

        @pl.when(kk == 0)
        def _():
            acc[...] = part

        @pl.when((kk > 0) & (kk < nk - 1))
        def _():
            acc[...] += part

        @pl.when(kk == nk - 1)
        def _():
            finish(acc[...] + part)

    in_specs = [
        pl.BlockSpec((tk, tm), lambda i, j, kk: (kk, i)) if tn_a else pl.BlockSpec((tm, tk), lambda i, j, kk: (i, kk)),
        pl.BlockSpec((tn, tk), lambda i, j, kk: (j, kk)) if nt else pl.BlockSpec((tk, tn), lambda i, j, kk: (kk, j)),
    ]
    args = [a, b]
    if add is not None:
        in_specs.append(pl.BlockSpec((tm, tn), lambda i, j, kk: (i, j)))
        args.append(add)
    return _call(
        body, name=name, grid=grid, in_specs=in_specs, args=args,
        out_specs=pl.BlockSpec((tm, tn), lambda i, j, kk: (i, j)),
        out_shape=jax.ShapeDtypeStruct((m, n), out_dtype),
        scratch_shapes=[pltpu.VMEM((tm, tn), F32)] if nk > 1 else [],
        semantics=("parallel", "parallel", "arbitrary"), send=send)


def _norm_fwd(x, w_col, name):
    f, t = x.shape
    tt = _tile(t, (512, 256, 128))

    def body(x_ref, w_ref, o_ref):
        xv = x_ref[...]
        r = lax.rsqrt(jnp.mean(xv * xv, axis=0, keepdims=True) + EPS)
        o_ref[...] = (xv * r * w_ref[...]).astype(BF16)

    return pl.pallas_call(
        body,
        name=name,
        grid=(t // tt,),
        in_specs=[pl.BlockSpec((f, tt), lambda i: (0, i)), pl.BlockSpec((f, 1), lambda i: (0, 0))],
        out_specs=pl.BlockSpec((f, tt), lambda i: (0, i)),
        out_shape=jax.ShapeDtypeStruct((f, t), BF16),
        compiler_params=_cparams("parallel"),
    )(x, w_col)


def _norm_fwd_tokens(x, w_col, after, name):
    t, f = x.shape
    tt = _tile(t, (512, 256, 128))

    def body(x_ref, w_ref, after_ref, xt_ref, o_ref):
        xv = x_ref[...].T
        xt_ref[...] = xv
        r = lax.rsqrt(jnp.mean(xv * xv, axis=0, keepdims=True) + EPS)
        o_ref[...] = (xv * r * w_ref[...]).astype(BF16)

    blk = pl.BlockSpec((f, tt), lambda i: (0, i))
    return pl.pallas_call(
        body,
        name=name,
        grid=(t // tt,),
        in_specs=[pl.BlockSpec((tt, f), lambda i: (i, 0)), pl.BlockSpec((f, 1), lambda i: (0, 0)), ANY],
        out_specs=[blk, blk],
        out_shape=[jax.ShapeDtypeStruct((f, t), F32), jax.ShapeDtypeStruct((f, t), BF16)],
        compiler_params=_cparams("parallel"),
    )(x, w_col, after)


def _norm_bwd(dy, x, w_col, res, name, tokens_out=False):
    f, t = x.shape
    tt = _tile(t, (512, 256, 128))

    def body(dy_ref, x_ref, w_ref, res_ref, dx_ref, dw_ref):
        @pl.when(pl.program_id(0) == 0)
        def _():
            dw_ref[...] = jnp.zeros_like(dw_ref)

        xv = x_ref[...]
        r = lax.rsqrt(jnp.mean(xv * xv, axis=0, keepdims=True) + EPS)
        xhat = xv * r
        dyv = dy_ref[...]
        dw_ref[...] += _rowsum(dyv * xhat)
        dxhat = dyv * w_ref[...]
        dx = res_ref[...] + r * (dxhat - xhat * jnp.mean(dxhat * xhat, axis=0, keepdims=True))
        dx_ref[...] = dx.T if tokens_out else dx

    blk = pl.BlockSpec((f, tt), lambda i: (0, i))
    col = pl.BlockSpec((f, 1), lambda i: (0, 0))
    return pl.pallas_call(
        body,
        name=name,
        grid=(t // tt,),
        in_specs=[blk, blk, col, blk],
        out_specs=[pl.BlockSpec((tt, f), lambda i: (i, 0)) if tokens_out else blk, col],
        out_shape=[jax.ShapeDtypeStruct((t, f) if tokens_out else (f, t), F32), jax.ShapeDtypeStruct((f, 1), F32)],
        compiler_params=_cparams("arbitrary"),
    )(dy, x, w_col, res)


def _final_norm_loss(h, tgt, w_col):
    f, t = h.shape
    tt = _tile(t, (512, 256, 128))

    def body(h_ref, t_ref, w_ref, dh_ref, loss_ref, dw_ref):
        @pl.when(pl.program_id(0) == 0)
        def _():
            dw_ref[...] = jnp.zeros_like(dw_ref)
            loss_ref[...] = jnp.zeros_like(loss_ref)

        xv = h_ref[...]
        r = lax.rsqrt(jnp.mean(xv * xv, axis=0, keepdims=True) + EPS)
        xhat = xv * r
        wv = w_ref[...]
        err = xhat * wv - t_ref[...].T
        loss_ref[...] += 0.5 * _rowsum(jnp.mean(err * err, axis=0, keepdims=True))
        dyv = err * (1.0 / f)
        dw_ref[...] += _rowsum(dyv * xhat)
        dxhat = dyv * wv
        dh_ref[...] = r * (dxhat - xhat * jnp.mean(dxhat * xhat, axis=0, keepdims=True))

    blk = pl.BlockSpec((f, tt), lambda i: (0, i))
    col = pl.BlockSpec((f, 1), lambda i: (0, 0))
    one = pl.BlockSpec((1, 1), lambda i: (0, 0))
    return pl.pallas_call(
        body,
        name="final_norm_loss",
        grid=(t // tt,),
        in_specs=[blk, pl.BlockSpec((tt, f), lambda i: (i, 0)), col],
        out_specs=[blk, one, col],
        out_shape=[jax.ShapeDtypeStruct((f, t), F32), jax.ShapeDtypeStruct((1, 1), F32), jax.ShapeDtypeStruct((f, 1), F32)],
        compiler_params=_cparams("arbitrary"),
    )(h, tgt, w_col)


def _attn_mask(n):
    shape = (2 * WINDOW, Q_PER_KV * WINDOW)
    si = lax.broadcasted_iota(jnp.int32, shape, 0)
    qi = lax.broadcasted_iota(jnp.int32, shape, 1) & (WINDOW - 1)
    dist = WINDOW + qi - si
    return (dist >= 0) & (dist < WINDOW) & ((si >= WINDOW) | (n > 0))


def _lane_cat(ref, row0, rows):
    return jnp.concatenate([ref[row0 + i * rows:row0 + (i + 1) * rows, :] for i in range(Q_PER_KV)], axis=1)


def _attn_fwd(proj, sinks):
    t = proj.shape[1]
    nb = t // WINDOW
    scale = HEAD_DIM ** -0.5

    def body(s_ref, q_ref, kc_ref, kp_ref, vc_ref, vp_ref, o_ref, lse_ref):
        n = pl.program_id(0)
        valid = _attn_mask(n)
        for g in range(N_KV_HEADS):
            rows = slice(g * HEAD_DIM, (g + 1) * HEAD_DIM)
            kt = jnp.concatenate([kp_ref[rows, :], kc_ref[rows, :]], axis=1).astype(BF16)
            vt = jnp.concatenate([vp_ref[rows, :], vc_ref[rows, :]], axis=1).astype(BF16)
            qcat = (_lane_cat(q_ref, g * Q_PER_KV * HEAD_DIM, HEAD_DIM) * scale).astype(BF16)
            s = jnp.where(valid, _dot_tn(kt, qcat), NEG)
            sink = jnp.concatenate(
                [jnp.full((1, WINDOW), s_ref[g * Q_PER_KV + i], F32) for i in range(Q_PER_KV)], axis=1)
            m = jnp.maximum(jnp.max(s, axis=0, keepdims=True), sink)
            p = jnp.exp(s - m)
            denom = _colsum(p) + jnp.exp(sink - m)
            probs = (p / denom).astype(BF16)
            out = _dot(vt, probs)
            lse = m + jnp.log(denom)
            for i in range(Q_PER_KV):
                h = g * Q_PER_KV + i
                o_ref[h * HEAD_DIM:(h + 1) * HEAD_DIM, :] = out[:, i * WINDOW:(i + 1) * WINDOW]
                lse_ref[h:h + 1, :] = lse[:, i * WINDOW:(i + 1) * WINDOW]

    kb = OFF_K // KV_DIM
    vb = OFF_V // KV_DIM
    prev = lambda n: jnp.maximum(n - 1, 0)
    return pl.pallas_call(
        body,
        name="attn_fwd",
        grid=(nb,),
        in_specs=[
            pl.BlockSpec(memory_space=pltpu.SMEM),
            pl.BlockSpec((Q_DIM, WINDOW), lambda n: (0, n)),
            pl.BlockSpec((KV_DIM, WINDOW), lambda n: (kb, n)),
            pl.BlockSpec((KV_DIM, WINDOW), lambda n: (kb, prev(n))),
            pl.BlockSpec((KV_DIM, WINDOW), lambda n: (vb, n)),
            pl.BlockSpec((KV_DIM, WINDOW), lambda n: (vb, prev(n))),
        ],
        out_specs=[pl.BlockSpec((Q_DIM, WINDOW), lambda n: (0, n)), pl.BlockSpec((N_Q_HEADS, WINDOW), lambda n: (0, n))],
        out_shape=[jax.ShapeDtypeStruct((Q_DIM, t), F32), jax.ShapeDtypeStruct((N_Q_HEADS, t), F32)],
        compiler_params=_cparams("parallel"),
    )(sinks, proj, proj, proj, proj, proj)


def _attn_bwd(proj, sinks, out, lse, dout, dproj, send=None):
    t = proj.shape[1]
    nb = t // WINDOW
    scale = HEAD_DIM ** -0.5

    def body(s_ref, q_ref, kc_ref, kp_ref, vc_ref, vp_ref, o_ref, lse_ref, do_ref, dproj_ref,
             dqkv_ref, ds_ref, dk_carry, dv_carry):
        dq_ref = dqkv_ref.at[pl.ds(OFF_Q, Q_DIM)]
        dk_ref = dqkv_ref.at[pl.ds(OFF_K, KV_DIM)]
        dv_ref = dqkv_ref.at[pl.ds(OFF_V, KV_DIM)]
        step = pl.program_id(0)
        n = nb - 1 - step

        @pl.when(step == 0)
        def _():
            dk_carry[...] = jnp.zeros_like(dk_carry)
            dv_carry[...] = jnp.zeros_like(dv_carry)
            ds_ref[...] = jnp.zeros_like(ds_ref)

        valid = _attn_mask(n)
        for g in range(N_KV_HEADS):
            rows = slice(g * HEAD_DIM, (g + 1) * HEAD_DIM)
            q0 = g * Q_PER_KV * HEAD_DIM
            kt = jnp.concatenate([kp_ref[rows, :], kc_ref[rows, :]], axis=1).astype(BF16)
            vt = jnp.concatenate([vp_ref[rows, :], vc_ref[rows, :]], axis=1).astype(BF16)
            qf = _lane_cat(q_ref, q0, HEAD_DIM)
            qcat = qf.astype(BF16)
            ocat = _lane_cat(o_ref, q0, HEAD_DIM)
            docat = _lane_cat(do_ref, q0, HEAD_DIM)
            dob = docat.astype(BF16)
            lse_cat = jnp.concatenate(
                [lse_ref[g * Q_PER_KV + i:g * Q_PER_KV + i + 1, :] for i in range(Q_PER_KV)], axis=1)
            sink = jnp.concatenate(
                [jnp.full((1, WINDOW), s_ref[g * Q_PER_KV + i], F32) for i in range(Q_PER_KV)], axis=1)
            s = jnp.where(valid, _dot_tn(kt, (qf * scale).astype(BF16)), NEG)
            p = jnp.exp(s - lse_cat)
            dp = _dot_tn(vt, dob)
            delta = _colsum(docat * ocat)
            dsc = (p * (dp - delta)).astype(BF16)
            dsink_row = -jnp.exp(sink - lse_cat) * delta
            dq = _dot(kt, dsc) * scale
            dk = _dot_nt(qcat, dsc) * scale
            dv = _dot_nt(dob, p.astype(BF16))
            for i in range(Q_PER_KV):
                h = g * Q_PER_KV + i
                dq_ref[h * HEAD_DIM:(h + 1) * HEAD_DIM, :] = dq[:, i * WINDOW:(i + 1) * WINDOW].astype(BF16)
                ds_ref[h:h + 1, :] += _rowsum(dsink_row[:, i * WINDOW:(i + 1) * WINDOW])
            dk_ref[rows, :] = (dk[:, WINDOW:] + dk_carry[rows, :]).astype(BF16)
            dv_ref[rows, :] = (dv[:, WINDOW:] + dv_carry[rows, :]).astype(BF16)
            dk_carry[rows, :] = dk[:, :WINDOW]
            dv_carry[rows, :] = dv[:, :WINDOW]

    kb = OFF_K // KV_DIM
    vb = OFF_V // KV_DIM
    cur = lambda i: nb - 1 - i
    prev = lambda i: jnp.maximum(nb - 2 - i, 0)
    qspec = pl.BlockSpec((Q_DIM, WINDOW), lambda i: (0, cur(i)))
    return _call(
        body,
        name="attn_bwd",
        grid=(nb,),
        in_specs=[
            pl.BlockSpec(memory_space=pltpu.SMEM),
            qspec,
            pl.BlockSpec((KV_DIM, WINDOW), lambda i: (kb, cur(i))),
            pl.BlockSpec((KV_DIM, WINDOW), lambda i: (kb, prev(i))),
            pl.BlockSpec((KV_DIM, WINDOW), lambda i: (vb, cur(i))),
            pl.BlockSpec((KV_DIM, WINDOW), lambda i: (vb, prev(i))),
            qspec,
            pl.BlockSpec((N_Q_HEADS, WINDOW), lambda i: (0, cur(i))),
            qspec,
            pl.BlockSpec(memory_space=pl.ANY),
        ],
        out_specs=[pl.BlockSpec((OFF_Z, WINDOW), lambda i: (0, cur(i))), pl.BlockSpec((N_Q_HEADS, 1), lambda i: (0, 0))],
        out_shape=[jax.ShapeDtypeStruct(dproj.shape, BF16), jax.ShapeDtypeStruct((N_Q_HEADS, 1), F32)],
        scratch_shapes=[pltpu.VMEM((KV_DIM, WINDOW), F32), pltpu.VMEM((KV_DIM, WINDOW), F32)],
        aliases={9: 0},
        semantics=("arbitrary",), args=(sinks, proj, proj, proj, proj, proj, out, lse, dout, dproj), send=send)


CONV_ROWS = 256


def _conv_silu_fwd(proj, w_col, b_col):
    t = proj.shape[1]
    r0 = OFF_X // CONV_ROWS

    def body(x_ref, w_ref, b_ref, o_ref):
        y, _ = _causal_conv(x_ref[...], w_ref[...], b_ref[...])
        o_ref[...] = y * _sigmoid(y)

    return pl.pallas_call(
        body,
        name="ssd_conv_fwd",
        grid=(XBC_DIM // CONV_ROWS,),
        in_specs=[
            pl.BlockSpec((CONV_ROWS, t), lambda i: (r0 + i, 0)),
            pl.BlockSpec((CONV_ROWS, SSD_CONV), lambda i: (i, 0)),
            pl.BlockSpec((CONV_ROWS, 1), lambda i: (i, 0)),
        ],
        out_specs=pl.BlockSpec((CONV_ROWS, t), lambda i: (i, 0)),
        out_shape=jax.ShapeDtypeStruct((XBC_DIM, t), F32),
        compiler_params=_cparams("parallel"),
    )(proj, w_col, b_col)


def _conv_silu_bwd(proj, w_col, b_col, dout, dproj):
    t = proj.shape[1]
    p0 = OFF_X // CONV_ROWS

    def body(x_ref, w_ref, b_ref, do_ref, dproj_ref, dx_ref, dwb_ref):
        xv = x_ref[...]
        wv = w_ref[...]
        y, shifted = _causal_conv(xv, wv, b_ref[...])
        sg = _sigmoid(y)
        dy = do_ref[...] * (sg * (1.0 + y * (1.0 - sg)))
        dx, dwb_ref[...] = _causal_conv_bwd(dy, xv, shifted, wv)
        dx_ref[...] = dx.astype(BF16)

    return pl.pallas_call(
        body,
        name="ssd_conv_bwd",
        grid=(XBC_DIM // CONV_ROWS,),
        in_specs=[
            pl.BlockSpec((CONV_ROWS, t), lambda i: (p0 + i, 0)),
            pl.BlockSpec((CONV_ROWS, SSD_CONV), lambda i: (i, 0)),
            pl.BlockSpec((CONV_ROWS, 1), lambda i: (i, 0)),
            pl.BlockSpec((CONV_ROWS, t), lambda i: (i, 0)),
            pl.BlockSpec(memory_space=pl.ANY),
        ],
        out_specs=[pl.BlockSpec((CONV_ROWS, t), lambda i: (p0 + i, 0)), pl.BlockSpec((CONV_ROWS, 128), lambda i: (i, 0))],
        out_shape=[jax.ShapeDtypeStruct(dproj.shape, BF16), jax.ShapeDtypeStruct((XBC_DIM, 128), F32)],
        input_output_aliases={4: 0},
        compiler_params=_cparams("parallel"),
    )(proj, w_col, b_col, dout, dproj)


def _ssd_specs(order):
    xb = D_INNER // BC_DIM
    dtb = OFF_DT // N_SSD_HEADS
    col = pl.BlockSpec((N_SSD_HEADS, 1), lambda c: (0, 0))
    return [
        pl.BlockSpec((D_INNER, CHUNK), lambda c: (0, order(c))),
        pl.BlockSpec((BC_DIM, CHUNK), lambda c: (xb, order(c))),
        pl.BlockSpec((BC_DIM, CHUNK), lambda c: (xb + 1, order(c))),
        pl.BlockSpec((N_SSD_HEADS, CHUNK), lambda c: (dtb, order(c))),
        col, col, col,
    ]


def _ssd_common(dt_ref, dtb_ref, alog_ref):
    z = dt_ref[...] + dtb_ref[...]
    dt = _softplus(z)
    a_neg = -jnp.exp(alog_ref[...])
    d_a = dt * a_neg
    row = lax.broadcasted_iota(jnp.int32, (CHUNK, CHUNK), 0)
    colm = lax.broadcasted_iota(jnp.int32, (CHUNK, CHUNK), 1)
    upper = (row <= colm).astype(F32)
    a_cs = jnp.dot(d_a, upper, precision=HIGHEST, preferred_element_type=F32)
    a_last = _rowsum(d_a)
    return z, dt, a_neg, a_cs, a_last, row >= colm, row == colm


def _decay(a_row, causal):
    a_s = jnp.broadcast_to(a_row, (CHUNK, CHUNK))
    seg = a_s.T - a_s
    return jnp.where(causal, jnp.exp(jnp.where(causal, seg, 0.0)), 0.0)


def _ssd_fwd(xbc, proj, dtb_col, alog_col, dsk_col, gnw_col):
    t = xbc.shape[1]
    nc = t // CHUNK

    def body(xs_ref, b_ref, c_ref, dt_ref, dtb_ref, alog_ref, dsk_ref, *rest):
        z_refs, (gnw_ref, y_ref, hst_ref, yn_ref, h_scr) = rest[:N_SSD_GROUPS], rest[N_SSD_GROUPS:]

        @pl.when(pl.program_id(0) == 0)
        def _():
            h_scr[...] = jnp.zeros_like(h_scr)

        _, dt, _, a_cs, a_last, causal, _ = _ssd_common(dt_ref, dtb_ref, alog_ref)
        hst_ref[0] = h_scr[...]
        dsk = dsk_ref[...]
        for g in range(N_SSD_GROUPS):
            grows = slice(g * D_STATE, (g + 1) * D_STATE)
            bb = b_ref[grows, :].astype(BF16)
            cb_ = c_ref[grows, :].astype(BF16)
            cb = _dot_tn(cb_, bb)
            for j in range(g * HEADS_PER_GROUP, (g + 1) * HEADS_PER_GROUP):
                rows = slice(j * SSD_HEAD_DIM, (j + 1) * SSD_HEAD_DIM)
                a = a_cs[j:j + 1, :]
                m = (cb * _decay(a, causal)).astype(BF16)
                xs = xs_ref[rows, :]
                xc = xs * dt[j:j + 1, :]
                hj = h_scr[rows, :]
                y = _dot_nt(xc.astype(BF16), m) + _dot(hj.astype(BF16), cb_) * jnp.exp(a) + dsk[j:j + 1, :] * xs
                y_ref[rows, :] = y
                al = a_last[j:j + 1, :]
                w = jnp.exp(al - a)
                h_scr[rows, :] = jnp.exp(al) * hj + _dot_nt((xc * w).astype(BF16), bb)
        for g in range(N_SSD_GROUPS):
            rows = slice(g * GN_ROWS, (g + 1) * GN_ROWS)
            zv = z_refs[g][...]
            u = y_ref[rows, :] * (zv * _sigmoid(zv))
            r = lax.rsqrt(jnp.mean(u * u, axis=0, keepdims=True) + EPS)
            yn_ref[rows, :] = (u * r * gnw_ref[rows, :]).astype(BF16)

    z0 = OFF_Z // GN_ROWS
    z_specs = [pl.BlockSpec((GN_ROWS, CHUNK), lambda c, g=g: (z0 + g, c)) for g in range(N_SSD_GROUPS)]
    rows_spec = pl.BlockSpec((D_INNER, CHUNK), lambda c: (0, c))
    return pl.pallas_call(
        body,
        name="ssd_fwd",
        grid=(nc,),
        in_specs=_ssd_specs(lambda c: c) + z_specs + [pl.BlockSpec((D_INNER, 1), lambda c: (0, 0))],
        out_specs=[rows_spec, pl.BlockSpec((1, D_INNER, D_STATE), lambda c: (c, 0, 0)), rows_spec],
        out_shape=[
            jax.ShapeDtypeStruct((D_INNER, t), F32),
            jax.ShapeDtypeStruct((nc, D_INNER, D_STATE), F32),
            jax.ShapeDtypeStruct((D_INNER, t), BF16),
        ],
        scratch_shapes=[pltpu.VMEM((D_INNER, D_STATE), F32)],
        compiler_params=_cparams("arbitrary"),
    )(xbc, xbc, xbc, proj, dtb_col, alog_col, dsk_col, *([proj] * N_SSD_GROUPS), gnw_col)


def _ssd_bwd(xbc, proj, dtb_col, alog_col, dsk_col, hst, y, dyn, gnw_col, send=None):
    t = xbc.shape[1]
    nc = t // CHUNK
    rev = lambda c: nc - 1 - c

    def body(xs_ref, b_ref, c_ref, dt_ref, dtb_ref, alog_ref, dsk_ref, hst_ref, y_ref, dyn_ref, *rest):
        z_refs, rest = rest[:N_SSD_GROUPS], rest[N_SSD_GROUPS:]
        (gnw_ref, dxbc_ref, ddt_ref, dalog_ref, ddsk_ref, ddtb_ref, dz_ref, dgnw_ref,
         dh_scr, da_scr, ddt_scr, dd_scr, dy_ref) = rest
        dxs_ref = dxbc_ref.at[pl.ds(0, D_INNER)]
        db_ref = dxbc_ref.at[pl.ds(D_INNER, BC_DIM)]
        dc_ref = dxbc_ref.at[pl.ds(D_INNER + BC_DIM, BC_DIM)]

        @pl.when(pl.program_id(0) == 0)
        def _():
            dh_scr[...] = jnp.zeros_like(dh_scr)
            dalog_ref[...] = jnp.zeros_like(dalog_ref)
            ddsk_ref[...] = jnp.zeros_like(ddsk_ref)
            ddtb_ref[...] = jnp.zeros_like(ddtb_ref)
            dgnw_ref[...] = jnp.zeros_like(dgnw_ref)

        for g in range(N_SSD_GROUPS):
            rows = slice(g * GN_ROWS, (g + 1) * GN_ROWS)
            zv = z_refs[g][...]
            yv = y_ref[rows, :]
            sg = _sigmoid(zv)
            sz = zv * sg
            u = yv * sz
            r = lax.rsqrt(jnp.mean(u * u, axis=0, keepdims=True) + EPS)
            xhat = u * r
            dov = dyn_ref[rows, :]
            dgnw_ref[rows, :] += _rowsum(dov * xhat)
            dxhat = dov * gnw_ref[rows, :]
            du = r * (dxhat - xhat * jnp.mean(dxhat * xhat, axis=0, keepdims=True))
            dy_ref[rows, :] = du * sz
            dz_ref[rows, :] = (du * yv * (sg * (1.0 + zv * (1.0 - sg)))).astype(BF16)

        z, dt, a_neg, a_cs, a_last, causal, eye = _ssd_common(dt_ref, dtb_ref, alog_ref)
        dsk = dsk_ref[...]
        last_lane = lax.broadcasted_iota(jnp.int32, (1, CHUNK), 1) == CHUNK - 1
        for g in range(N_SSD_GROUPS):
            grows = slice(g * D_STATE, (g + 1) * D_STATE)
            bb = b_ref[grows, :].astype(BF16)
            cb_ = c_ref[grows, :].astype(BF16)
            cb = _dot_tn(cb_, bb)
            dcb = jnp.zeros((CHUNK, CHUNK), F32)
            dc_acc = jnp.zeros((D_STATE, CHUNK), F32)
            db_acc = jnp.zeros((D_STATE, CHUNK), F32)
            for j in range(g * HEADS_PER_GROUP, (g + 1) * HEADS_PER_GROUP):
                rows = slice(j * SSD_HEAD_DIM, (j + 1) * SSD_HEAD_DIM)
                a = a_cs[j:j + 1, :]
                al = a_last[j:j + 1, :]
                lam = _decay(a, causal)
                mf = cb * lam
                xs = xs_ref[rows, :]
                dtj = dt[j:j + 1, :]
                xc = xs * dtj
                w = jnp.exp(al - a)
                e = jnp.exp(a)
                gam = jnp.exp(al)
                hj = hst_ref[0, rows, :]
                hjb = hj.astype(BF16)
                dyv = dy_ref[rows, :]
                dyb = dyv.astype(BF16)
                dd_scr[j:j + 1, :] = _colsum(dyv * xs)
                gb = (dyv * e).astype(BF16)
                dh_in = _dot_nt(gb, cb_)
                dc_acc = dc_acc + _dot_tn(hjb, gb)
                yoff = _dot(hjb, cb_) * e
                da = _colsum(dyv * yoff)
                dm = _dot_tn(dyb, xc.astype(BF16))
                dxc = _dot(dyb, mf.astype(BF16))
                dcb = dcb + dm * lam
                nmat = dm * mf
                rs = jnp.broadcast_to(_rowsum(nmat), (CHUNK, CHUNK))
                da = da + _colsum(jnp.where(eye, rs, 0.0)) - _colsum(nmat)
                ds = dh_scr[rows, :]
                dsb = ds.astype(BF16)
                t1 = _dot(dsb, bb)
                xcw = xc * w
                dxc = dxc + w * t1
                dww = _colsum(xcw * t1)
                da_l = _rowsum(dww) + _rowsum(_colsum(ds * hj)) * gam
                da = da - dww + jnp.where(last_lane, da_l, 0.0)
                db_acc = db_acc + _dot_tn(dsb, xcw.astype(BF16))
                dh_scr[rows, :] = gam * ds + dh_in
                dxs_ref[rows, :] = dsk[j:j + 1, :] * dyv + dxc * dtj
                da_scr[j:j + 1, :] = da
                ddt_scr[j:j + 1, :] = _colsum(dxc * xs)
            dcbb = dcb.astype(BF16)
            dc_ref[grows, :] = dc_acc + _dot_nt(bb, dcbb)
            db_ref[grows, :] = db_acc + _dot(cb_, dcbb)
        dda = jnp.dot(da_scr[...], causal.astype(F32), precision=HIGHEST, preferred_element_type=F32)
        ddt = ddt_scr[...] + dda * a_neg
        ddt_raw = ddt * _sigmoid(z)
        ddt_ref[...] = ddt_raw
        ddtb_ref[...] += _rowsum(ddt_raw)
        dalog_ref[...] += _rowsum(dda * dt) * a_neg
        ddsk_ref[...] += _rowsum(dd_scr[...])

    col = pl.BlockSpec((N_SSD_HEADS, 1), lambda c: (0, 0))
    xs_spec = pl.BlockSpec((D_INNER, CHUNK), lambda c: (0, rev(c)))
    gn_col = pl.BlockSpec((D_INNER, 1), lambda c: (0, 0))
    z0 = OFF_Z // GN_ROWS
    z_specs = [pl.BlockSpec((GN_ROWS, CHUNK), lambda c, g=g: (z0 + g, rev(c))) for g in range(N_SSD_GROUPS)]
    dz_spec = pl.BlockSpec((pl.Element(D_INNER), pl.Element(CHUNK)),
                           lambda c: (OFF_Z, pl.multiple_of(CHUNK * rev(c), CHUNK)))
    small = pltpu.VMEM((N_SSD_HEADS, CHUNK), F32)
    return _call(
        body,
        name="ssd_bwd",
        grid=(nc,),
        in_specs=_ssd_specs(rev) + [pl.BlockSpec((1, D_INNER, D_STATE), lambda c: (rev(c), 0, 0)), xs_spec, xs_spec]
        + z_specs + [gn_col],
        out_specs=[pl.BlockSpec((XBC_DIM, CHUNK), lambda c: (0, rev(c))),
                   pl.BlockSpec((N_SSD_HEADS, CHUNK), lambda c: (0, rev(c))), col, col, col, dz_spec, gn_col],
        out_shape=[
            jax.ShapeDtypeStruct((XBC_DIM, t), F32),
            jax.ShapeDtypeStruct((N_SSD_HEADS, t), F32),
            jax.ShapeDtypeStruct((N_SSD_HEADS, 1), F32),
            jax.ShapeDtypeStruct((N_SSD_HEADS, 1), F32),
            jax.ShapeDtypeStruct((N_SSD_HEADS, 1), F32),
            jax.ShapeDtypeStruct((IN_DIM, t), BF16),
            jax.ShapeDtypeStruct((D_INNER, 1), F32),
        ],
        scratch_shapes=[pltpu.VMEM((D_INNER, D_STATE), F32), small, small, small, pltpu.VMEM((D_INNER, CHUNK), F32)],
        semantics=("arbitrary",),
        args=(xbc, xbc, xbc, proj, dtb_col, alog_col, dsk_col, hst, y, dyn, *([proj] * N_SSD_GROUPS), gnw_col),
        send=send)


GATE_ROWS = 128


def _gate_specs(t):
    nr = D_MODEL // GATE_ROWS
    blk = pl.BlockSpec((GATE_ROWS, t), lambda r: (r, 0))
    rows_from = lambda first: pl.BlockSpec(
        (pl.Element(GATE_ROWS), pl.Element(t)), lambda r: (pl.multiple_of(first + GATE_ROWS * r, N_SSD_HEADS), 0))
    return blk, [
        rows_from(OFF_GA),
        rows_from(OFF_GS),
        pl.BlockSpec((GATE_ROWS, 1), lambda r: (r, 0)),
        pl.BlockSpec((GATE_ROWS, 1), lambda r: (nr + r, 0)),
        blk, blk,
    ]


def _gate_fwd(proj, b_col, attn, ssd):
    t = proj.shape[1]
    blk, specs = _gate_specs(t)

    def body(ga_ref, gs_ref, ba_ref, bs_ref, a_ref, s_ref, o_ref):
        o_ref[...] = (_sigmoid(ga_ref[...] + ba_ref[...]) * a_ref[...]
                      + _sigmoid(gs_ref[...] + bs_ref[...]) * s_ref[...]).astype(BF16)

    return pl.pallas_call(
        body,
        name="gate_fwd",
        grid=(D_MODEL // GATE_ROWS,),
        in_specs=specs,
        out_specs=blk,
        out_shape=jax.ShapeDtypeStruct((D_MODEL, t), BF16),
        compiler_params=_cparams("parallel"),
    )(proj, proj, b_col, b_col, attn, ssd)


def _gate_bwd(proj, b_col, attn, ssd, dmix, send=None):
    t = proj.shape[1]
    blk, specs = _gate_specs(t)

    def body(ga_ref, gs_ref, ba_ref, bs_ref, a_ref, s_ref, dm_ref, da_ref, dso_ref, dga_ref, dgs_ref, dba_ref, dbs_ref):
        dm = dm_ref[...]
        sa = _sigmoid(ga_ref[...] + ba_ref[...])
        ss = _sigmoid(gs_ref[...] + bs_ref[...])
        da_ref[...] = (dm * sa).astype(BF16)
        dso_ref[...] = (dm * ss).astype(BF16)
        dga = dm * a_ref[...] * sa * (1.0 - sa)
        dgs = dm * s_ref[...] * ss * (1.0 - ss)
        dga_ref[...] = dga.astype(BF16)
        dgs_ref[...] = dgs.astype(BF16)
        dba_ref[...] = _rowsum(dga)
        dbs_ref[...] = _rowsum(dgs)

    col = pl.BlockSpec((GATE_ROWS, 1), lambda r: (r, 0))
    act = jax.ShapeDtypeStruct((D_MODEL, t), BF16)
    bias = jax.ShapeDtypeStruct((D_MODEL, 1), F32)
    return _call(
        body,
        name="gate_bwd",
        grid=(D_MODEL // GATE_ROWS,),
        in_specs=specs + [blk],
        out_specs=[blk, blk, blk, blk, col, col],
        out_shape=[act, act, act, act, bias, bias],
        semantics=("parallel",), args=(proj, proj, b_col, b_col, attn, ssd, dmix), send=send)


FFN_ROWS = 256


def _ffn_fwd(u0, w_col, b_col):
    t = u0.shape[2]

    def body(u_ref, w_ref, b_ref, o_ref):
        val, _ = _causal_conv(u_ref[0], w_ref[0], b_ref[0])
        gt, _ = _causal_conv(u_ref[1], w_ref[1], b_ref[1])
        o_ref[...] = (gt * _sigmoid(gt) * val).astype(BF16)

    return pl.pallas_call(
        body,
        name="ffn_fwd",
        grid=(D_FF // FFN_ROWS,),
        in_specs=[
            pl.BlockSpec((2, FFN_ROWS, t), lambda i: (0, i, 0)),
            pl.BlockSpec((2, FFN_ROWS, FFN_CONV), lambda i: (0, i, 0)),
            pl.BlockSpec((2, FFN_ROWS, 1), lambda i: (0, i, 0)),
        ],
        out_specs=pl.BlockSpec((FFN_ROWS, t), lambda i: (i, 0)),
        out_shape=jax.ShapeDtypeStruct((D_FF, t), BF16),
        compiler_params=_cparams("parallel"),
    )(u0, w_col, b_col)


def _ffn_bwd(u0, w_col, b_col, dg, send=None):
    t = u0.shape[2]

    def body(u_ref, w_ref, b_ref, dg_ref, du_ref, dwb_ref):
        xval, wval = u_ref[0], w_ref[0]
        xgt, wgt = u_ref[1], w_ref[1]
        val, sh_val = _causal_conv(xval, wval, b_ref[0])
        gt, sh_gt = _causal_conv(xgt, wgt, b_ref[1])
        sg = _sigmoid(gt)
        dgv = dg_ref[...]
        dval = dgv * (gt * sg)
        dgt = dgv * val * (sg * (1.0 + gt * (1.0 - sg)))
        dx, dwb_ref[0] = _causal_conv_bwd(dval, xval, sh_val, wval)
        du_ref[0] = dx.astype(BF16)
        dx, dwb_ref[1] = _causal_conv_bwd(dgt, xgt, sh_gt, wgt)
        du_ref[1] = dx.astype(BF16)

    return _call(
        body,
        name="ffn_bwd",
        grid=(D_FF // FFN_ROWS,),
        in_specs=[
            pl.BlockSpec((2, FFN_ROWS, t), lambda i: (0, i, 0)),
            pl.BlockSpec((2, FFN_ROWS, FFN_CONV), lambda i: (0, i, 0)),
            pl.BlockSpec((2, FFN_ROWS, 1), lambda i: (0, i, 0)),
            pl.BlockSpec((FFN_ROWS, t), lambda i: (i, 0)),
        ],
        out_specs=[pl.BlockSpec((2, FFN_ROWS, t), lambda i: (0, i, 0)), pl.BlockSpec((2, FFN_ROWS, 128), lambda i: (0, i, 0))],
        out_shape=[jax.ShapeDtypeStruct((2, D_FF, t), BF16), jax.ShapeDtypeStruct((2, D_FF, 128), F32)],
        semantics=("parallel",), args=(u0, w_col, b_col, dg), send=send)


def _adamw_math(w, g, m, v):
    m = ADAM_B1 * m + (1.0 - ADAM_B1) * g
    v = ADAM_B2 * v + (1.0 - ADAM_B2) * (g * g)
    m_hat = m / (1.0 - ADAM_B1 ** ADAM_STEP)
    v_hat = v / (1.0 - ADAM_B2 ** ADAM_STEP)
    delta = -ADAM_LR * (m_hat / (jnp.sqrt(v_hat) + ADAM_EPS) + ADAM_WD * w)
    return delta, m, v


def _adamw_sharded(parts, w, m, v, name):
    r, c = w.shape[0], w.shape[-1]
    slots = parts.shape[0]
    per_lane = 2 * r * (slots * parts.dtype.itemsize + 7 * w.dtype.itemsize)
    tc = max(d for d in range(LANES, c + 1, LANES) if c % d == 0 and (d * per_lane <= BLOCK_VMEM_BUDGET or d == LANES))
    blk_shape = (r, tc) if w.ndim == 2 else (r, 1, tc)

    def body(p_ref, w_ref, m_ref, v_ref, g_ref, d_ref, nm_ref, nv_ref):
        g = p_ref[0].astype(F32)
        for s in range(1, slots):
            g = g + p_ref[s].astype(F32)
        flat = lambda ref: ref[...].reshape(r, tc)
        d, nm, nv = _adamw_math(flat(w_ref), g, flat(m_ref), flat(v_ref))
        for ref, val in ((g_ref, g), (d_ref, d), (nm_ref, nm), (nv_ref, nv)):
            ref[...] = val.reshape(blk_shape)

    blk = pl.BlockSpec(blk_shape, (lambda i: (0, i)) if w.ndim == 2 else (lambda i: (0, 0, i)))
    out = jax.ShapeDtypeStruct(w.shape, F32)
    return pl.pallas_call(
        body,
        name=name,
        grid=(c // tc,),
        in_specs=[pl.BlockSpec((slots, r, tc), lambda i: (0, 0, i)), blk, blk, blk],
        out_specs=[blk, blk, blk, blk],
        out_shape=[out, out, out, out],
        compiler_params=_cparams("parallel"),
    )(parts, w, m, v)


def _lane_offsets(sizes):
    offsets, pos = [], 0
    for n in sizes:
        offsets.append(pos)
        pos += -(-n // 128) * 128
    return offsets, pos


def _pack_row(parts):
    rows = [p.reshape(1, -1).astype(F32) for p in parts]
    return jnp.concatenate([jnp.pad(r, ((0, 0), (0, -r.shape[1] % 128))) for r in rows], axis=1)


def _small_update(parts, me, full_sizes, ws, ms, vs):
    n = len(ws)
    offsets, _ = _lane_offsets([1] + list(full_sizes))

    def body(me_ref, p_ref, *refs):
        w_refs, m_refs, v_refs = refs[:n], refs[n:2 * n], refs[2 * n:3 * n]
        scalar_ref, out_refs = refs[3 * n], refs[3 * n + 1:]
        tot = p_ref[0]
        for s in range(1, N_DEV):
            tot = tot + p_ref[s]
        scalar_ref[...] = tot[:, 0:1]
        for k in range(n):
            g_ref, d_ref, nm_ref, nv_ref = out_refs[4 * k:4 * k + 4]
            taps, cols = w_refs[k].shape
            if taps == 1:
                g_ref[...] = tot[:, offsets[k + 1]:offsets[k + 1] + cols]
            else:
                full = full_sizes[k] // taps
                for tap in range(taps):
                    mine = jnp.zeros((1, cols), F32)
                    for d in range(N_DEV):
                        lo = offsets[k + 1] + tap * full + d * cols
                        mine = jnp.where(me_ref[0] == d, tot[:, lo:lo + cols], mine)
                    g_ref[tap:tap + 1, :] = mine
            d_ref[...], nm_ref[...], nv_ref[...] = _adamw_math(w_refs[k][...], g_ref[...], m_refs[k][...], v_refs[k][...])

    vmem = pl.BlockSpec(memory_space=pltpu.VMEM)
    out_shape = [jax.ShapeDtypeStruct((1, 1), F32)]
    for wk in ws:
        out_shape += [jax.ShapeDtypeStruct(wk.shape, F32)] * 4
    res = pl.pallas_call(
        body,
        name="small_update",
        in_specs=[pl.BlockSpec(memory_space=pltpu.SMEM)] + [vmem] * (1 + 3 * n),
        out_specs=[vmem] * len(out_shape),
        out_shape=out_shape,
    )(me, parts, *ws, *ms, *vs)
    return res[0], [res[1 + 4 * k:5 + 4 * k] for k in range(n)]


ANY = pl.BlockSpec(memory_space=pl.ANY)
FLIPS = [(k >> 2 & 1, k >> 1 & 1, k & 1) for k in range(1, N_DEV)]


def _place():
    return lax.axis_index("x"), lax.axis_index("y"), lax.axis_index("c")


HBM = pl.BlockSpec(memory_space=pltpu.HBM)
SEM = pl.BlockSpec(memory_space=pltpu.SEMAPHORE)
EFFECT = pltpu.SideEffectType.DATAFLOW_SIDE_EFFECTING


def _peer_copy(gather, src_ref, land_ref, send_sems, recv_sems, k, sending):
    x, y, c = _place()
    fx, fy, fc = FLIPS[k]
    me = 4 * x + 2 * y + c
    peer = 4 * (x ^ fx) + 2 * (y ^ fy) + (c ^ fc)
    return pltpu.make_async_remote_copy(
        src_ref=src_ref if gather else src_ref.at[peer],
        dst_ref=land_ref.at[me if sending else peer],
        send_sem=send_sems.at[k], recv_sem=recv_sems.at[k],
        device_id=(x ^ fx, y ^ fy, c ^ fc), device_id_type=MESH)


SIBLING = 0
OTHER_CHIPS = (1, 3, 5)


def _gather_start(srcs, name, via_sibling):
    n = len(srcs)
    lands = [lax.empty((N_DEV,) + s.shape, s.dtype) for s in srcs]

    def body(*refs):
        src_refs, land_refs = refs[:n], refs[n:2 * n]
        send, recv = refs[2 * n:3 * n], refs[3 * n:4 * n]
        for i in range(n):
            for k in (SIBLING,) + OTHER_CHIPS if via_sibling else range(N_DEV - 1):
                _peer_copy(True, src_refs[i], land_refs[i], send[i], recv[i], k, True).start()

    sem = pltpu.SemaphoreType.DMA((N_DEV - 1,))
    hbm = lambda a: pltpu.HBM(a.shape, a.dtype)
    res = pl.pallas_call(
        body,
        name=name,
        in_specs=[HBM] * (2 * n),
        out_specs=[SEM] * (2 * n) + [HBM] * (2 * n),
        out_shape=[sem] * (2 * n) + [hbm(s) for s in srcs] + [hbm(a) for a in lands],
        input_output_aliases={i: 2 * n + i for i in range(2 * n)},
        compiler_params=pltpu.CompilerParams(has_side_effects=EFFECT),
    )(*[pltpu.with_memory_space_constraint(a, pltpu.HBM) for a in list(srcs) + lands])
    return res[:n], res[n:2 * n], res[2 * n:3 * n], res[3 * n:4 * n]


def _exchange_wait(send_sems, recv_sems, src, land, after, gather, name):
    def body(src_ref, land_ref, send_ref, recv_ref, after_ref, src_out, land_out):
        for k in range(N_DEV - 1):
            cp = _peer_copy(gather, src_ref, land_ref, send_ref, recv_ref, k, False)
            cp.wait_send()
            cp.wait_recv()

    hbm = lambda a: pltpu.HBM(a.shape, a.dtype)
    return pl.pallas_call(
        body,
        name=name,
        in_specs=[HBM, HBM, SEM, SEM, ANY],
        out_specs=[HBM, HBM],
        out_shape=[hbm(src), hbm(land)],
        input_output_aliases={0: 0, 1: 1},
        compiler_params=pltpu.CompilerParams(has_side_effects=EFFECT),
    )(src, land, send_sems, recv_sems, after)


def _own_slot(src, land, me, gather):
    own = src[None] if gather else lax.dynamic_slice_in_dim(src, me, 1, axis=0)
    return lax.dynamic_update_slice_in_dim(land, own, me, axis=0)


def _forwarded_copy(land_ref, send_sems, recv_sems, j, sending):
    x, y, c = _place()
    fx, fy, _ = FLIPS[OTHER_CHIPS[j]]
    slot = 4 * (x ^ fx) + 2 * (y ^ fy) + (c if sending else 1 - c)
    return pltpu.make_async_remote_copy(
        src_ref=land_ref.at[slot], dst_ref=land_ref.at[slot], send_sem=send_sems.at[j], recv_sem=recv_sems.at[j],
        device_id=(x, y, 1 - c), device_id_type=MESH)


def _gather_forward(send_sems, recv_sems, srcs, lands, after, name):
    n = len(srcs)

    def body(*refs):
        src_refs, land_refs = refs[:n], refs[n:2 * n]
        send, recv = refs[2 * n:3 * n], refs[3 * n:4 * n]
        fwd_send, fwd_recv = refs[4 * n + 1:5 * n + 1], refs[5 * n + 1:6 * n + 1]
        for i in range(n):
            for j, k in enumerate(OTHER_CHIPS):
                _peer_copy(True, src_refs[i], land_refs[i], send[i], recv[i], k, False).wait_recv()
                _forwarded_copy(land_refs[i], fwd_send[i], fwd_recv[i], j, True).start()

    sem = pltpu.SemaphoreType.DMA((len(OTHER_CHIPS),))
    hbm = lambda a: pltpu.HBM(a.shape, a.dtype)
    res = pl.pallas_call(
        body,
        name=name,
        in_specs=[HBM] * (2 * n) + [SEM] * (2 * n) + [ANY],
        out_specs=[SEM] * (2 * n) + [HBM] * (2 * n),
        out_shape=[sem] * (2 * n) + [hbm(a) for a in srcs] + [hbm(a) for a in lands],
        input_output_aliases={i: 2 * n + i for i in range(2 * n)},
        compiler_params=pltpu.CompilerParams(has_side_effects=EFFECT),
    )(*srcs, *lands, *send_sems, *recv_sems, after)
    return res[:n], res[n:2 * n], res[2 * n:3 * n], res[3 * n:4 * n]


def _gather_wait_forwarded(send_sems, recv_sems, fwd_send, fwd_recv, src, land, after, name):
    def body(src_ref, land_ref, send_ref, recv_ref, fwd_send_ref, fwd_recv_ref, after_ref, src_out, land_out):
        for k in (SIBLING,) + OTHER_CHIPS:
            _peer_copy(True, src_ref, land_ref, send_ref, recv_ref, k, False).wait_send()
        _peer_copy(True, src_ref, land_ref, send_ref, recv_ref, SIBLING, False).wait_recv()
        for j in range(len(OTHER_CHIPS)):
            _forwarded_copy(land_ref, fwd_send_ref, fwd_recv_ref, j, True).wait_send()
            _forwarded_copy(land_ref, fwd_send_ref, fwd_recv_ref, j, False).wait_recv()

    hbm = lambda a: pltpu.HBM(a.shape, a.dtype)
    return pl.pallas_call(
        body,
        name=name,
        in_specs=[HBM, HBM, SEM, SEM, SEM, SEM, ANY],
        out_specs=[HBM, HBM],
        out_shape=[hbm(src), hbm(land)],
        input_output_aliases={0: 0, 1: 1},
        compiler_params=pltpu.CompilerParams(has_side_effects=EFFECT),
    )(src, land, send_sems, recv_sems, fwd_send, fwd_recv, after)


N_CHIPS = N_DEV // 2


def _pair_exchange(by_core, meanwhile, name):
    def copy(src_ref, land_ref, send_sems, recv_sems, q):
        x, y, c = _place()
        return pltpu.make_async_remote_copy(
            src_ref=src_ref.at[q, 1 - c], dst_ref=land_ref.at[q], send_sem=send_sems.at[q], recv_sem=recv_sems.at[q],
            device_id=(x, y, 1 - c), device_id_type=MESH)

    def start(src_ref, land_ref, send_sems, recv_sems, src_out, land_out):
        for q in range(N_CHIPS):
            copy(src_ref, land_ref, send_sems, recv_sems, q).start()

    def wait(src_ref, land_ref, send_sems, recv_sems, after_ref, src_out, land_out):
        for q in range(N_CHIPS):
            cp = copy(src_ref, land_ref, send_sems, recv_sems, q)
            cp.wait_send()
            cp.wait_recv()

    sem = pltpu.SemaphoreType.DMA((N_CHIPS,))
    hbm_src = pltpu.HBM(by_core.shape, by_core.dtype)
    hbm_land = pltpu.HBM(by_core.shape[:1] + by_core.shape[2:], by_core.dtype)
    params = pltpu.CompilerParams(has_side_effects=EFFECT)
    send_sems, recv_sems, src, land = pl.pallas_call(
        start, name=name + "_start", in_specs=[HBM, HBM], out_specs=[SEM, SEM, HBM, HBM],
        out_shape=[sem, sem, hbm_src, hbm_land], input_output_aliases={0: 2, 1: 3}, compiler_params=params,
    )(pltpu.with_memory_space_constraint(by_core, pltpu.HBM),
      pltpu.with_memory_space_constraint(lax.empty(hbm_land.shape, by_core.dtype), pltpu.HBM))
    return pl.pallas_call(
        wait, name=name + "_wait", in_specs=[HBM, HBM, SEM, SEM, ANY], out_specs=[HBM, HBM],
        out_shape=[hbm_src, hbm_land], input_output_aliases={0: 0, 1: 1}, compiler_params=params,
    )(src, land, send_sems, recv_sems, meanwhile(src))


def _pair_add(by_core, landed, name):
    q, _, r, c = by_core.shape
    tc = _tile(c, (512, 256, 128))

    def body(a_ref, b_ref, o_ref):
        mine = a_ref[0, lax.axis_index("c")]
        o_ref[0] = (mine.astype(F32) + b_ref[0].astype(F32)).astype(BF16)

    blk = pl.BlockSpec((1, r, tc), lambda i, j: (i, 0, j))
    return pl.pallas_call(
        body, name=name, grid=(q, c // tc),
        in_specs=[pl.BlockSpec((1, 2, r, tc), lambda i, j: (i, 0, 0, j)), blk], out_specs=blk,
        out_shape=jax.ShapeDtypeStruct(landed.shape, BF16), compiler_params=_cparams("parallel", "parallel"),
    )(by_core, landed)


def _chip_copy(src_ref, land_ref, send_sems, recv_sems, j, sending):
    x, y, c = _place()
    fx, fy, _ = FLIPS[OTHER_CHIPS[j]]
    here, there = 2 * x + y, 2 * (x ^ fx) + (y ^ fy)
    return pltpu.make_async_remote_copy(
        src_ref=src_ref.at[there], dst_ref=land_ref.at[here if sending else there],
        send_sem=send_sems.at[j], recv_sem=recv_sems.at[j],
        device_id=(x ^ fx, y ^ fy, c), device_id_type=MESH)


def _chip_wait(send_sems, recv_sems, src, land, after, name):
    def body(src_ref, land_ref, send_ref, recv_ref, after_ref, src_out, land_out):
        for j in range(len(OTHER_CHIPS)):
            cp = _chip_copy(src_ref, land_ref, send_ref, recv_ref, j, False)
            cp.wait_send()
            cp.wait_recv()

    hbm = lambda a: pltpu.HBM(a.shape, a.dtype)
    return pl.pallas_call(
        body,
        name=name,
        in_specs=[HBM, HBM, SEM, SEM, ANY],
        out_specs=[HBM, HBM],
        out_shape=[hbm(src), hbm(land)],
        input_output_aliases={0: 0, 1: 1},
        compiler_params=pltpu.CompilerParams(has_side_effects=EFFECT),
    )(src, land, send_sems, recv_sems, after)


def _col(v):
    return v.reshape(-1, 1).astype(F32)


def _local_step(x, tgt, started, weight, small, pair_sums, handles):
    t = x.shape[0]
    n1 = _col(small["norm1_w"])
    n2 = _col(small["norm2_w"])
    nf = _col(small["final_norm_w"])
    bg = _col(small["b_gate"])
    sinks = small["attn_sinks"].reshape(-1).astype(F32)
    cbias = _col(small["ssd_conv_b"])
    dtb = _col(small["dt_bias"])
    alog = _col(small["a_log"])
    dsk = _col(small["d_skip"])
    gnw = _col(small["ssd_norm_w"])
    fb = small["ffn_conv_b"].reshape(2, D_FF, 1)

    xt, xn = _norm_fwd_tokens(x, n1, started, "norm1_fwd")
    cw = weight("ssd_conv_w", xn).T
    fw = weight("ffn_conv_w", xn).T.reshape(2, D_FF, FFN_CONV)
    w_in_t = weight("w_in", xn)
    proj = _matmul(w_in_t, xn, nt=False, out_dtype=F32, name="mm_in")
    ao, lse = _attn_fwd(proj, sinks)
    w_ao = weight("w_attn_o", ao)
    attn = _matmul(w_ao, ao, nt=False, out_dtype=F32, name="mm_attn_o", tn_a=True)
    xbc = _conv_silu_fwd(proj, cw, cbias)
    y, hst, yn = _ssd_fwd(xbc, proj, dtb, alog, dsk, gnw)
    w_so = weight("w_ssd_o", yn)
    ssd = _matmul(w_so, yn, nt=False, out_dtype=F32, name="mm_ssd_o", tn_a=True)
    mix = _gate_fwd(proj, bg, attn, ssd)
    w_out = weight("w_out", mix)
    h1 = _matmul(w_out, mix, nt=False, out_dtype=F32, name="mm_out", add=xt, tn_a=True)
    hn = _norm_fwd(h1, n2, "norm2_fwd")
    w_up_t = weight("w_up", hn)
    u0 = _matmul(w_up_t, hn, nt=False, out_dtype=F32, name="mm_up").reshape(2, D_FF, t)
    gl = _ffn_fwd(u0, fw, fb)
    w_down = weight("w_down", gl)
    h2 = _matmul(w_down, gl, nt=False, out_dtype=F32, name="mm_down", add=h1, tn_a=True)
    dh2, loss, d_nf = _final_norm_loss(h2, tgt, nf)

    g = {}

    def sending(weight_name, grad, fn, *args, **kwargs):
        chunks = grad if grad.ndim == 3 else grad.reshape(N_DEV, -1, D_MODEL)
        out, handles[weight_name] = fn(*args, send=chunks, **kwargs)
        return out

    g_down = _matmul(gl, dh2, nt=True, out_dtype=BF16, name="mm_d_w_down")
    dgl = _matmul(w_down, dh2, nt=False, out_dtype=F32, name="mm_d_glu")
    du0, d_fwb = sending("w_down", g_down, _ffn_bwd, u0, fw, fb, dgl)
    du0 = du0.reshape(2 * D_FF, t)
    g_up = _matmul(du0, hn, nt=True, out_dtype=BF16, name="mm_d_w_up")
    dhn = sending("w_up", g_up, _matmul, w_up_t, du0, nt=False, out_dtype=F32, name="mm_d_hn", tn_a=True)
    dh1, d_n2 = _norm_bwd(dhn, h1, n2, dh2, "norm2_bwd")
    g_out = _matmul(mix, dh1, nt=True, out_dtype=BF16, name="mm_d_w_out")
    dmix = _matmul(w_out, dh1, nt=False, out_dtype=F32, name="mm_d_mix")
    d_attn, d_ssd, d_ga, d_gs, d_ba, d_bs = sending("w_out", g_out, _gate_bwd, proj, bg, attn, ssd, dmix)
    g_ao = _matmul(ao, d_attn, nt=True, out_dtype=BF16, name="mm_d_w_attn_o")
    dao = _matmul(w_ao, d_attn, nt=False, out_dtype=F32, name="mm_d_ao")
    g_so = _matmul(yn, d_ssd, nt=True, out_dtype=BF16, name="mm_d_w_ssd_o")
    dyn = _matmul(w_so, d_ssd, nt=False, out_dtype=F32, name="mm_d_yn")
    dxbc, ddt, d_alog, d_dsk, d_dtb, dproj, d_gnw = sending(
        "w_ssd_o", g_so, _ssd_bwd, xbc, proj, dtb, alog, dsk, hst, y, dyn, gnw)
    dproj, dwb_conv = _conv_silu_bwd(proj, cw, cbias, dxbc, dproj)
    dproj, d_sinks = sending("w_attn_o", g_ao, _attn_bwd, proj, sinks, ao, lse, dao, dproj)
    for rows, part in ((OFF_DT, ddt.astype(BF16)), (OFF_GA, d_ga), (OFF_GS, d_gs)):
        dproj = lax.dynamic_update_slice(dproj, part, (rows, 0))
    g_in = pair_sums(_matmul(dproj, xn, nt=True, out_dtype=BF16, name="mm_d_w_in"))
    dxn = sending("w_in", g_in, _matmul, w_in_t, dproj, nt=False, out_dtype=F32, name="mm_d_xn", tn_a=True)
    dx, d_n1 = _norm_bwd(dxn, xt, n1, dh1, "norm1_bwd", tokens_out=True)

    g["norm1_w"] = d_n1
    g["b_gate"] = jnp.concatenate([d_ba, d_bs], axis=0)
    g["attn_sinks"] = d_sinks
    g["ssd_conv_w"] = dwb_conv[:, :SSD_CONV].T
    g["ssd_conv_b"] = dwb_conv[:, SSD_CONV]
    g["dt_bias"] = d_dtb
    g["a_log"] = d_alog
    g["d_skip"] = d_dsk
    g["ssd_norm_w"] = d_gnw
    g["norm2_w"] = d_n2
    d_fwb = d_fwb.reshape(2 * D_FF, 128)
    g["ffn_conv_w"] = d_fwb[:, :FFN_CONV].T
    g["ffn_conv_b"] = d_fwb[:, FFN_CONV]
    g["final_norm_w"] = d_nf
    return loss, dx, g


SMALL = ("norm1_w", "b_gate", "attn_sinks", "ssd_conv_w", "ssd_conv_b", "dt_bias", "a_log", "d_skip", "ssd_norm_w",
         "norm2_w", "ffn_conv_w", "ffn_conv_b", "final_norm_w")
WEIGHT_ORDER = ("norm1_w", "w_in", "b_gate", "attn_sinks", "w_attn_o", "ssd_conv_w", "ssd_conv_b", "dt_bias", "a_log",
                "d_skip", "ssd_norm_w", "w_ssd_o", "w_out", "norm2_w", "w_up", "ffn_conv_w", "ffn_conv_b", "w_down",
                "final_norm_w")


def kernel(x, norm1_w, w_in, b_gate, attn_sinks, w_attn_o, ssd_conv_w, ssd_conv_b, dt_bias, a_log, d_skip, ssd_norm_w, w_ssd_o, w_out, norm2_w, w_up, ffn_conv_w, ffn_conv_b, w_down, final_norm_w, loss_target, m_norm1_w, m_w_in, m_b_gate, m_attn_sinks, m_w_attn_o, m_ssd_conv_w, m_ssd_conv_b, m_dt_bias, m_a_log, m_d_skip, m_ssd_norm_w, m_w_ssd_o, m_w_out, m_norm2_w, m_w_up, m_ffn_conv_w, m_ffn_conv_b, m_w_down, m_final_norm_w, v_norm1_w, v_w_in, v_b_gate, v_attn_sinks, v_w_attn_o, v_ssd_conv_w, v_ssd_conv_b, v_dt_bias, v_a_log, v_d_skip, v_ssd_norm_w, v_w_ssd_o, v_w_out, v_norm2_w, v_w_up, v_ffn_conv_w, v_ffn_conv_b, v_w_down, v_final_norm_w):
    w = dict(norm1_w=norm1_w, w_in=w_in, b_gate=b_gate, attn_sinks=attn_sinks, w_attn_o=w_attn_o, ssd_conv_w=ssd_conv_w, ssd_conv_b=ssd_conv_b, dt_bias=dt_bias, a_log=a_log, d_skip=d_skip, ssd_norm_w=ssd_norm_w, w_ssd_o=w_ssd_o, w_out=w_out, norm2_w=norm2_w, w_up=w_up, ffn_conv_w=ffn_conv_w, ffn_conv_b=ffn_conv_b, w_down=w_down, final_norm_w=final_norm_w)
    m = dict(norm1_w=m_norm1_w, w_in=m_w_in, b_gate=m_b_gate, attn_sinks=m_attn_sinks, w_attn_o=m_w_attn_o, ssd_conv_w=m_ssd_conv_w, ssd_conv_b=m_ssd_conv_b, dt_bias=m_dt_bias, a_log=m_a_log, d_skip=m_d_skip, ssd_norm_w=m_ssd_norm_w, w_ssd_o=m_w_ssd_o, w_out=m_w_out, norm2_w=m_norm2_w, w_up=m_w_up, ffn_conv_w=m_ffn_conv_w, ffn_conv_b=m_ffn_conv_b, w_down=m_w_down, final_norm_w=m_final_norm_w)
    v = dict(norm1_w=v_norm1_w, w_in=v_w_in, b_gate=v_b_gate, attn_sinks=v_attn_sinks, w_attn_o=v_w_attn_o, ssd_conv_w=v_ssd_conv_w, ssd_conv_b=v_ssd_conv_b, dt_bias=v_dt_bias, a_log=v_a_log, d_skip=v_d_skip, ssd_norm_w=v_ssd_norm_w, w_ssd_o=v_w_ssd_o, w_out=v_w_out, norm2_w=v_norm2_w, w_up=v_w_up, ffn_conv_w=v_ffn_conv_w, ffn_conv_b=v_ffn_conv_b, w_down=v_w_down, final_norm_w=v_final_norm_w)
    me = 4 * lax.axis_index("x") + 2 * lax.axis_index("y") + lax.axis_index("c")

    shards = {"ssd_conv_w": ssd_conv_w[0], "ffn_conv_w": ffn_conv_w[0], "w_in": w_in[0].T.astype(BF16),
              "w_attn_o": w_attn_o[0].astype(BF16), "w_ssd_o": w_ssd_o[0].astype(BF16), "w_out": w_out[0].astype(BF16),
              "w_up": w_up[0].T.astype(BF16), "w_down": w_down[0].astype(BF16)}
    order = list(shards)
    g_send, g_recv, g_src, g_land = _gather_start(list(shards.values()), "gather_start", True)
    first = ("ssd_conv_w", "ffn_conv_w", "w_in")
    forwarded = {}

    def weight(name, after):
        if name not in forwarded:
            group = [k for k in order if (k in first) == (name in first)]
            idx = [order.index(k) for k in group]
            handles = _gather_forward([g_send[i] for i in idx], [g_recv[i] for i in idx], [g_src[i] for i in idx],
                                      [g_land[i] for i in idx], after, "gather_forward_for_" + name)
            forwarded.update(zip(group, zip(*handles)))
        i = order.index(name)
        src, land = _gather_wait_forwarded(g_send[i], g_recv[i], *forwarded[name], after, "gather_wait_" + name)
        land = _own_slot(src, land, me, True)
        if name == "ssd_conv_w":
            return jnp.transpose(land, (1, 0, 2)).reshape(SSD_CONV, XBC_DIM)
        if name == "ffn_conv_w":
            return jnp.transpose(land, (1, 0, 2)).reshape(FFN_CONV, 2 * D_FF)
        return land.reshape(-1, D_MODEL)

    res, pending = {}, {}

    def update(name, after):
        if name == "w_in":
            parts = _own_slot(*_chip_wait(*pending[name], after, "grad_wait_" + name), me // 2, False)
        else:
            parts = _own_slot(*_exchange_wait(*pending[name], after, False, "grad_wait_" + name), me, False)
        view, back = {
            "w_in": (lambda a: jnp.transpose(a, (2, 0, 1)), lambda r: jnp.transpose(r, (1, 2, 0))),
            "w_up": (lambda a: a[0].T, lambda r: r.T[None]),
        }.get(name, (lambda a: a[0], lambda r: r[None]))
        done = _adamw_sharded(parts, view(w[name]), view(m[name]), view(v[name]), "adamw_" + name)
        res[name] = [back(r) for r in done]
        return done[0]

    def pair_sums(grad):
        by_core, landed = _pair_exchange(grad.reshape(N_CHIPS, 2, -1, D_MODEL),
                                         lambda started: update("w_up", update("w_down", started)), "grad_pair_w_in")
        return _pair_add(by_core, landed, "grad_pair_add_w_in")

    small = {k: w[k][0] if k != "final_norm_w" else w[k] for k in SMALL}
    loss, dx, g = _local_step(x[0], loss_target[0], g_src[0], weight, small, pair_sums, pending)

    packed = _pack_row([loss] + [g[k] for k in SMALL])
    s_send, s_recv, s_src, s_land = _gather_start([packed], "small_grads_start", False)
    after = s_src[0]
    for name in ("w_out", "w_attn_o", "w_ssd_o", "w_in"):
        after = update(name, after)

    rows = _own_slot(*_exchange_wait(s_send[0], s_recv[0], s_src[0], s_land[0], after, True, "small_grads_wait"),
                     me, True)
    flat = lambda a: a.reshape(-1, a.shape[-1])
    loss_sum, updates = _small_update(
        rows, me.reshape(1), [g[k].size for k in SMALL],
        [flat(w[k]) for k in SMALL], [flat(m[k]) for k in SMALL], [flat(v[k]) for k in SMALL])
    for k, upd in zip(SMALL, updates):
        res[k] = [u.reshape(w[k].shape) for u in upd]

    grad_x = dx[None]
    outs = [loss_sum.reshape(()), grad_x]
    for i in range(4):
        outs.extend(res[k][i] for k in WEIGHT_ORDER)
    return tuple(outs)
```

```python
import jax
import jax.numpy as jnp
from jax import lax
from jax.experimental import pallas as pl
from jax.experimental.pallas import tpu as pltpu

F32 = jnp.float32
BF16 = jnp.bfloat16
HIGHEST = lax.Precision.HIGHEST

D_MODEL = 1024
N_Q_HEADS = 16
N_KV_HEADS = 4
HEAD_DIM = 64
WINDOW = 128
Q_PER_KV = N_Q_HEADS // N_KV_HEADS
Q_DIM = N_Q_HEADS * HEAD_DIM
KV_DIM = N_KV_HEADS * HEAD_DIM
D_INNER = 2048
SSD_HEAD_DIM = 64
N_SSD_HEADS = 32
N_SSD_GROUPS = 4
HEADS_PER_GROUP = N_SSD_HEADS // N_SSD_GROUPS
D_STATE = 128
GN_ROWS = D_INNER // N_SSD_GROUPS
BC_DIM = N_SSD_GROUPS * D_STATE
XBC_DIM = D_INNER + 2 * BC_DIM
SSD_CONV = 4
CHUNK = 128
D_FF = 2816
FFN_CONV = 3
EPS = 1e-5
NEG = -1e30
IN_DIM = 8736
N_DEV = 8

OFF_Q = 0
OFF_K = OFF_Q + Q_DIM
OFF_V = OFF_K + KV_DIM
OFF_Z = OFF_V + KV_DIM
OFF_X = OFF_Z + D_INNER
OFF_DT = OFF_X + XBC_DIM
OFF_GA = OFF_DT + N_SSD_HEADS
OFF_GS = OFF_GA + D_MODEL

ADAM_LR = 0.001
ADAM_B1 = 0.9
ADAM_B2 = 0.999
ADAM_EPS = 1e-08
ADAM_WD = 0.01
ADAM_STEP = 10

LANES = 128
BF16_TILE_ROWS = 16
VMEM_BYTES = 64 * 1024 * 1024
VMEM_LIMIT = VMEM_BYTES * 3 // 4
MESH = pl.DeviceIdType.MESH


def _cparams(*sem):
    return pltpu.CompilerParams(dimension_semantics=sem, vmem_limit_bytes=VMEM_LIMIT)


def _tile(n, prefs):
    for p in prefs:
        if n % p == 0:
            return p
    return n


def _sigmoid(x):
    return 1.0 / (1.0 + jnp.exp(-x))


def _softplus(x):
    return jnp.maximum(x, 0.0) + jnp.log(1.0 + jnp.exp(-jnp.abs(x)))


def _rowsum(x):
    return jnp.sum(x, axis=1, keepdims=True)


def _colsum(x):
    return jnp.sum(x, axis=0, keepdims=True)


def _dot(a, b):
    return jnp.dot(a, b, preferred_element_type=F32)


def _dot_nt(a, b):
    return lax.dot_general(a, b, (((1,), (1,)), ((), ())), preferred_element_type=F32)


def _dot_tn(a, b):
    return lax.dot_general(a, b, (((0,), (0,)), ((), ())), preferred_element_type=F32)


def _shift_right(x, j):
    if j == 0:
        return x
    r = pltpu.roll(x, j, 1)
    lane = lax.broadcasted_iota(jnp.int32, (x.shape[0], 128), 1)
    return jnp.concatenate([jnp.where(lane >= j, r[:, :128], 0.0), r[:, 128:]], axis=1)


def _shift_left(x, j):
    if j == 0:
        return x
    n = x.shape[1]
    r = pltpu.roll(x, n - j, 1)
    lane = lax.broadcasted_iota(jnp.int32, (x.shape[0], 128), 1)
    return jnp.concatenate([r[:, :n - 128], jnp.where(lane < 128 - j, r[:, n - 128:], 0.0)], axis=1)


def _causal_conv(xv, wv, bv):
    taps = wv.shape[1]
    shifted = [_shift_right(xv, taps - 1 - k) for k in range(taps - 1)]
    y = bv + wv[:, taps - 1:taps] * xv
    for k in range(taps - 1):
        y = y + wv[:, k:k + 1] * shifted[k]
    return y, shifted


def _causal_conv_bwd(dy, xv, shifted, wv):
    taps = wv.shape[1]
    lane = lax.broadcasted_iota(jnp.int32, (dy.shape[0], 128), 1)
    dwb = jnp.where(lane == taps, _rowsum(dy), 0.0)
    dwb = jnp.where(lane == taps - 1, _rowsum(dy * xv), dwb)
    dx = wv[:, taps - 1:taps] * dy
    for k in range(taps - 1):
        dx = dx + wv[:, k:k + 1] * _shift_left(dy, taps - 1 - k)
        dwb = jnp.where(lane == k, _rowsum(dy * shifted[k]), dwb)
    return dx, dwb


def _call(body, *, name, grid, in_specs, out_specs, out_shape, args, semantics, scratch_shapes=(), aliases=None,
          send=None):
    aliases = dict(aliases or {})
    if send is None:
        return pl.pallas_call(body, name=name, grid=grid, in_specs=in_specs, out_specs=out_specs, out_shape=out_shape,
                              scratch_shapes=list(scratch_shapes), input_output_aliases=aliases,
                              compiler_params=_cparams(*semantics))(*args)
    single = not isinstance(out_specs, (list, tuple))
    out_specs, out_shape = ([out_specs], [out_shape]) if single else (list(out_specs), list(out_shape))
    n_in, n_out = len(in_specs), len(out_specs)
    chips = send.shape[0] == N_DEV // 2
    n_copies = len(OTHER_CHIPS) if chips else N_DEV - 1

    def sending(*refs):
        ins, (src_ref, land_ref) = refs[:n_in], refs[n_in:n_in + 2]
        outs = refs[n_in + 2:n_in + 2 + n_out]
        send_sems, recv_sems = refs[n_in + 2 + n_out:n_in + 4 + n_out]
        scratch = refs[n_in + 6 + n_out:]
        step = 0
        for axis, size in enumerate(grid):
            step = step * size + pl.program_id(axis)

        @pl.when(step == 0)
        def _():
            for k in range(n_copies):
                if chips:
                    _chip_copy(src_ref, land_ref, send_sems, recv_sems, k, True).start()
                else:
                    _peer_copy(False, src_ref, land_ref, send_sems, recv_sems, k, True).start()

        body(*ins, *outs, *scratch)

    sem = pltpu.SemaphoreType.DMA((n_copies,))
    hbm = pltpu.HBM(send.shape, send.dtype)
    res = pl.pallas_call(
        sending, name=name, grid=grid,
        in_specs=list(in_specs) + [HBM, HBM],
        out_specs=out_specs + [SEM, SEM, HBM, HBM],
        out_shape=out_shape + [sem, sem, hbm, hbm],
        input_output_aliases={**aliases, n_in: n_out + 2, n_in + 1: n_out + 3},
        scratch_shapes=list(scratch_shapes),
        compiler_params=pltpu.CompilerParams(dimension_semantics=("arbitrary",) * len(grid), vmem_limit_bytes=VMEM_LIMIT,
                                             has_side_effects=EFFECT),
    )(*args, pltpu.with_memory_space_constraint(send, pltpu.HBM),
      pltpu.with_memory_space_constraint(lax.empty(send.shape, send.dtype), pltpu.HBM))
    return (res[0] if single else list(res[:n_out])), tuple(res[n_out:])


BLOCK_VMEM_BUDGET = VMEM_LIMIT * 3 // 4
MATMUL_MAX_TM = 768
MATMUL_MAX_TN = 3072
MATMUL_MAX_TK = 3072


def _largest_tile(n, align, cap):
    return max(d for d in range(align, min(n, cap) + 1, align) if n % d == 0)


def _matmul_tiles(m, n, k, a_bytes, b_bytes, out_bytes, f32_blocks, m_align, k_align, whole_m):
    tm = m if whole_m else _largest_tile(m, m_align, MATMUL_MAX_TM)
    tk = _largest_tile(k, k_align, MATMUL_MAX_TK)
    for tn in sorted({d for d in range(LANES, min(n, MATMUL_MAX_TN) + 1, LANES) if n % d == 0}, reverse=True):
        need = 2 * (tm * tk * a_bytes + tk * tn * b_bytes) + tm * tn * (2 * out_bytes + (4 if k > tk else 0) + 8 * f32_blocks)
        if need <= BLOCK_VMEM_BUDGET:
            return tm, tn, tk
    return tm, LANES, tk


def _norm_bwd_math(dy, x, w, res):
    r = lax.rsqrt(jnp.mean(x * x, axis=0, keepdims=True) + EPS)
    xhat = x * r
    dxhat = dy * w
    return res + r * (dxhat - xhat * jnp.mean(dxhat * xhat, axis=0, keepdims=True)), _rowsum(dy * xhat)


def _matmul(a, b, *, nt, out_dtype, name, add=None, tn_a=False, send=None, norm_bwd=None):
    if tn_a:
        k, m = a.shape
    else:
        m, k = a.shape
    n = b.shape[0] if nt else b.shape[1]
    tokens_out = norm_bwd is not None and norm_bwd[3]
    tm, tn, tk = _matmul_tiles(m, n, k, a.dtype.itemsize, b.dtype.itemsize, jnp.dtype(out_dtype).itemsize,
                               (add is not None) + 2 * (norm_bwd is not None),
                               LANES if tn_a or tokens_out else BF16_TILE_ROWS,
                               BF16_TILE_ROWS if tn_a and not nt else LANES, norm_bwd is not None)
    nk = k // tk
    grid = (m // tm, n // tn, nk)
    n_extra = (add is not None) + 3 * (norm_bwd is not None)
    n_out = 1 + (norm_bwd is not None)

    def body(a_ref, b_ref, *rest):
        extra, outs, scratch = rest[:n_extra], rest[n_extra:n_extra + n_out], rest[n_extra + n_out:]
        av = a_ref[...].astype(BF16)
        bv = b_ref[...].astype(BF16)
        part = _dot_tn(av, bv) if tn_a else _dot_nt(av, bv) if nt else _dot(av, bv)

        def finish(r):
            if add is not None:
                r = r + extra[0][...]
            if norm_bwd is not None:
                x_ref, w_ref, res_ref = extra[-3:]
                dx, dw = _norm_bwd_math(r, x_ref[...], w_ref[...], res_ref[...])
                outs[1][...] += dw
                r = dx.T if tokens_out else dx
            outs[0][...] = r.astype(out_dtype)

        if norm_bwd is not None:
            @pl.when((pl.program_id(1) == 0) & (pl.program_id(2) == 0))
            def _():
                outs[1][...] = jnp.zeros_like(outs[1])

        if nk == 1:
            finish(part)
            return
        acc = scratch[0]
        kk = pl.program_id(2)

        @pl.when(kk == 0)
        def _():
            acc[...] = part

        @pl.when((kk > 0) & (kk < nk - 1))
        def _():
            acc[...] += part

        @pl.when(kk == nk - 1)
        def _():
            finish(acc[...] + part)

    tile = pl.BlockSpec((tm, tn), lambda i, j, kk: (i, j))
    in_specs = [
        pl.BlockSpec((tk, tm), lambda i, j, kk: (kk, i)) if tn_a else pl.BlockSpec((tm, tk), lambda i, j, kk: (i, kk)),
        pl.BlockSpec((tn, tk), lambda i, j, kk: (j, kk)) if nt else pl.BlockSpec((tk, tn), lambda i, j, kk: (kk, j)),
    ]
    args = [a, b]
    out_specs, out_shape = tile, jax.ShapeDtypeStruct((m, n), out_dtype)
    if add is not None:
        in_specs.append(tile)
        args.append(add)
    if norm_bwd is not None:
        x, w_col, res, _ = norm_bwd
        col = pl.BlockSpec((m, 1), lambda i, j, kk: (0, 0))
        in_specs += [tile, col, tile]
        args += [x, w_col, res]
        if tokens_out:
            out_specs, out_shape = pl.BlockSpec((tn, tm), lambda i, j, kk: (j, i)), jax.ShapeDtypeStruct((n, m), out_dtype)
        out_specs, out_shape = [out_specs, col], [out_shape, jax.ShapeDtypeStruct((m, 1), F32)]
    return _call(
        body, name=name, grid=grid, in_specs=in_specs, args=args, out_specs=out_specs, out_shape=out_shape,
        scratch_shapes=[pltpu.VMEM((tm, tn), F32)] if nk > 1 else [],
        semantics=("parallel", "parallel" if norm_bwd is None else "arbitrary", "arbitrary"), send=send)


def _norm_fwd(x, w_col, name):
    f, t = x.shape
    tt = _tile(t, (512, 256, 128))

    def body(x_ref, w_ref, o_ref):
        xv = x_ref[...]
        r = lax.rsqrt(jnp.mean(xv * xv, axis=0, keepdims=True) + EPS)
        o_ref[...] = (xv * r * w_ref[...]).astype(BF16)

    return pl.pallas_call(
        body,
        name=name,
        grid=(t // tt,),
        in_specs=[pl.BlockSpec((f, tt), lambda i: (0, i)), pl.BlockSpec((f, 1), lambda i: (0, 0))],
        out_specs=pl.BlockSpec((f, tt), lambda i: (0, i)),
        out_shape=jax.ShapeDtypeStruct((f, t), BF16),
        compiler_params=_cparams("parallel"),
    )(x, w_col)


def _norm_fwd_tokens(x, w_col, after, name):
    t, f = x.shape
    tt = _tile(t, (512, 256, 128))

    def body(x_ref, w_ref, after_ref, xt_ref, o_ref):
        xv = x_ref[...].T
        xt_ref[...] = xv
        r = lax.rsqrt(jnp.mean(xv * xv, axis=0, keepdims=True) + EPS)
        o_ref[...] = (xv * r * w_ref[...]).astype(BF16)

    blk = pl.BlockSpec((f, tt), lambda i: (0, i))
    return pl.pallas_call(
        body,
        name=name,
        grid=(t // tt,),
        in_specs=[pl.BlockSpec((tt, f), lambda i: (i, 0)), pl.BlockSpec((f, 1), lambda i: (0, 0)), ANY],
        out_specs=[blk, blk],
        out_shape=[jax.ShapeDtypeStruct((f, t), F32), jax.ShapeDtypeStruct((f, t), BF16)],
        compiler_params=_cparams("parallel"),
    )(x, w_col, after)


def _final_norm_loss(h, tgt, w_col):
    f, t = h.shape
    tt = _tile(t, (512, 256, 128))

    def body(h_ref, t_ref, w_ref, dh_ref, loss_ref, dw_ref):
        @pl.when(pl.program_id(0) == 0)
        def _():
            dw_ref[...] = jnp.zeros_like(dw_ref)
            loss_ref[...] = jnp.zeros_like(loss_ref)

        xv = h_ref[...]
        r = lax.rsqrt(jnp.mean(xv * xv, axis=0, keepdims=True) + EPS)
        xhat = xv * r
        wv = w_ref[...]
        err = xhat * wv - t_ref[...].T
        loss_ref[...] += 0.5 * _rowsum(jnp.mean(err * err, axis=0, keepdims=True))
        dyv = err * (1.0 / f)
        dw_ref[...] += _rowsum(dyv * xhat)
        dxhat = dyv * wv
        dh_ref[...] = r * (dxhat - xhat * jnp.mean(dxhat * xhat, axis=0, keepdims=True))

    blk = pl.BlockSpec((f, tt), lambda i: (0, i))
    col = pl.BlockSpec((f, 1), lambda i: (0, 0))
    one = pl.BlockSpec((1, 1), lambda i: (0, 0))
    return pl.pallas_call(
        body,
        name="final_norm_loss",
        grid=(t // tt,),
        in_specs=[blk, pl.BlockSpec((tt, f), lambda i: (i, 0)), col],
        out_specs=[blk, one, col],
        out_shape=[jax.ShapeDtypeStruct((f, t), F32), jax.ShapeDtypeStruct((1, 1), F32), jax.ShapeDtypeStruct((f, 1), F32)],
        compiler_params=_cparams("arbitrary"),
    )(h, tgt, w_col)


def _attn_mask(n):
    shape = (2 * WINDOW, Q_PER_KV * WINDOW)
    si = lax.broadcasted_iota(jnp.int32, shape, 0)
    qi = lax.broadcasted_iota(jnp.int32, shape, 1) & (WINDOW - 1)
    dist = WINDOW + qi - si
    return (dist >= 0) & (dist < WINDOW) & ((si >= WINDOW) | (n > 0))


def _lane_cat(ref, row0, rows):
    return jnp.concatenate([ref[row0 + i * rows:row0 + (i + 1) * rows, :] for i in range(Q_PER_KV)], axis=1)


def _attn_fwd(proj, sinks):
    t = proj.shape[1]
    nb = t // WINDOW
    scale = HEAD_DIM ** -0.5

    def body(s_ref, q_ref, kc_ref, kp_ref, vc_ref, vp_ref, o_ref, lse_ref):
        n = pl.program_id(0)
        valid = _attn_mask(n)
        for g in range(N_KV_HEADS):
            rows = slice(g * HEAD_DIM, (g + 1) * HEAD_DIM)
            kt = jnp.concatenate([kp_ref[rows, :], kc_ref[rows, :]], axis=1).astype(BF16)
            vt = jnp.concatenate([vp_ref[rows, :], vc_ref[rows, :]], axis=1).astype(BF16)
            qcat = (_lane_cat(q_ref, g * Q_PER_KV * HEAD_DIM, HEAD_DIM) * scale).astype(BF16)
            s = jnp.where(valid, _dot_tn(kt, qcat), NEG)
            sink = jnp.concatenate(
                [jnp.full((1, WINDOW), s_ref[g * Q_PER_KV + i], F32) for i in range(Q_PER_KV)], axis=1)
            m = jnp.maximum(jnp.max(s, axis=0, keepdims=True), sink)
            p = jnp.exp(s - m)
            denom = _colsum(p) + jnp.exp(sink - m)
            probs = (p / denom).astype(BF16)
            out = _dot(vt, probs)
            lse = m + jnp.log(denom)
            for i in range(Q_PER_KV):
                h = g * Q_PER_KV + i
                o_ref[h * HEAD_DIM:(h + 1) * HEAD_DIM, :] = out[:, i * WINDOW:(i + 1) * WINDOW]
                lse_ref[h:h + 1, :] = lse[:, i * WINDOW:(i + 1) * WINDOW]

    kb = OFF_K // KV_DIM
    vb = OFF_V // KV_DIM
    prev = lambda n: jnp.maximum(n - 1, 0)
    return pl.pallas_call(
        body,
        name="attn_fwd",
        grid=(nb,),
        in_specs=[
            pl.BlockSpec(memory_space=pltpu.SMEM),
            pl.BlockSpec((Q_DIM, WINDOW), lambda n: (0, n)),
            pl.BlockSpec((KV_DIM, WINDOW), lambda n: (kb, n)),
            pl.BlockSpec((KV_DIM, WINDOW), lambda n: (kb, prev(n))),
            pl.BlockSpec((KV_DIM, WINDOW), lambda n: (vb, n)),
            pl.BlockSpec((KV_DIM, WINDOW), lambda n: (vb, prev(n))),
        ],
        out_specs=[pl.BlockSpec((Q_DIM, WINDOW), lambda n: (0, n)), pl.BlockSpec((N_Q_HEADS, WINDOW), lambda n: (0, n))],
        out_shape=[jax.ShapeDtypeStruct((Q_DIM, t), F32), jax.ShapeDtypeStruct((N_Q_HEADS, t), F32)],
        compiler_params=_cparams("parallel"),
    )(sinks, proj, proj, proj, proj, proj)


def _attn_bwd(proj, sinks, out, lse, dout, dproj, send=None):
    t = proj.shape[1]
    nb = t // WINDOW
    scale = HEAD_DIM ** -0.5

    def body(s_ref, q_ref, kc_ref, kp_ref, vc_ref, vp_ref, o_ref, lse_ref, do_ref, dproj_ref,
             dqkv_ref, ds_ref, dk_carry, dv_carry):
        dq_ref = dqkv_ref.at[pl.ds(OFF_Q, Q_DIM)]
        dk_ref = dqkv_ref.at[pl.ds(OFF_K, KV_DIM)]
        dv_ref = dqkv_ref.at[pl.ds(OFF_V, KV_DIM)]
        step = pl.program_id(0)
        n = nb - 1 - step

        @pl.when(step == 0)
        def _():
            dk_carry[...] = jnp.zeros_like(dk_carry)
            dv_carry[...] = jnp.zeros_like(dv_carry)
            ds_ref[...] = jnp.zeros_like(ds_ref)

        valid = _attn_mask(n)
        for g in range(N_KV_HEADS):
            rows = slice(g * HEAD_DIM, (g + 1) * HEAD_DIM)
            q0 = g * Q_PER_KV * HEAD_DIM
            kt = jnp.concatenate([kp_ref[rows, :], kc_ref[rows, :]], axis=1).astype(BF16)
            vt = jnp.concatenate([vp_ref[rows, :], vc_ref[rows, :]], axis=1).astype(BF16)
            qf = _lane_cat(q_ref, q0, HEAD_DIM)
            qcat = qf.astype(BF16)
            ocat = _lane_cat(o_ref, q0, HEAD_DIM)
            docat = _lane_cat(do_ref, q0, HEAD_DIM)
            dob = docat.astype(BF16)
            lse_cat = jnp.concatenate(
                [lse_ref[g * Q_PER_KV + i:g * Q_PER_KV + i + 1, :] for i in range(Q_PER_KV)], axis=1)
            sink = jnp.concatenate(
                [jnp.full((1, WINDOW), s_ref[g * Q_PER_KV + i], F32) for i in range(Q_PER_KV)], axis=1)
            s = jnp.where(valid, _dot_tn(kt, (qf * scale).astype(BF16)), NEG)
            p = jnp.exp(s - lse_cat)
            dp = _dot_tn(vt, dob)
            delta = _colsum(docat * ocat)
            dsc = (p * (dp - delta)).astype(BF16)
            dsink_row = -jnp.exp(sink - lse_cat) * delta
            dq = _dot(kt, dsc) * scale
            dk = _dot_nt(qcat, dsc) * scale
            dv = _dot_nt(dob, p.astype(BF16))
            for i in range(Q_PER_KV):
                h = g * Q_PER_KV + i
                dq_ref[h * HEAD_DIM:(h + 1) * HEAD_DIM, :] = dq[:, i * WINDOW:(i + 1) * WINDOW].astype(BF16)
                ds_ref[h:h + 1, :] += _rowsum(dsink_row[:, i * WINDOW:(i + 1) * WINDOW])
            dk_ref[rows, :] = (dk[:, WINDOW:] + dk_carry[rows, :]).astype(BF16)
            dv_ref[rows, :] = (dv[:, WINDOW:] + dv_carry[rows, :]).astype(BF16)
            dk_carry[rows, :] = dk[:, :WINDOW]
            dv_carry[rows, :] = dv[:, :WINDOW]

    kb = OFF_K // KV_DIM
    vb = OFF_V // KV_DIM
    cur = lambda i: nb - 1 - i
    prev = lambda i: jnp.maximum(nb - 2 - i, 0)
    qspec = pl.BlockSpec((Q_DIM, WINDOW), lambda i: (0, cur(i)))
    return _call(
        body,
        name="attn_bwd",
        grid=(nb,),
        in_specs=[
            pl.BlockSpec(memory_space=pltpu.SMEM),
            qspec,
            pl.BlockSpec((KV_DIM, WINDOW), lambda i: (kb, cur(i))),
            pl.BlockSpec((KV_DIM, WINDOW), lambda i: (kb, prev(i))),
            pl.BlockSpec((KV_DIM, WINDOW), lambda i: (vb, cur(i))),
            pl.BlockSpec((KV_DIM, WINDOW), lambda i: (vb, prev(i))),
            qspec,
            pl.BlockSpec((N_Q_HEADS, WINDOW), lambda i: (0, cur(i))),
            qspec,
            pl.BlockSpec(memory_space=pl.ANY),
        ],
        out_specs=[pl.BlockSpec((OFF_Z, WINDOW), lambda i: (0, cur(i))), pl.BlockSpec((N_Q_HEADS, 1), lambda i: (0, 0))],
        out_shape=[jax.ShapeDtypeStruct(dproj.shape, BF16), jax.ShapeDtypeStruct((N_Q_HEADS, 1), F32)],
        scratch_shapes=[pltpu.VMEM((KV_DIM, WINDOW), F32), pltpu.VMEM((KV_DIM, WINDOW), F32)],
        aliases={9: 0},
        semantics=("arbitrary",), args=(sinks, proj, proj, proj, proj, proj, out, lse, dout, dproj), send=send)


CONV_ROWS = 256


def _conv_silu_fwd(proj, w_col, b_col):
    t = proj.shape[1]
    r0 = OFF_X // CONV_ROWS

    def body(x_ref, w_ref, b_ref, o_ref):
        y, _ = _causal_conv(x_ref[...], w_ref[...], b_ref[...])
        o_ref[...] = y * _sigmoid(y)

    return pl.pallas_call(
        body,
        name="ssd_conv_fwd",
        grid=(XBC_DIM // CONV_ROWS,),
        in_specs=[
            pl.BlockSpec((CONV_ROWS, t), lambda i: (r0 + i, 0)),
            pl.BlockSpec((CONV_ROWS, SSD_CONV), lambda i: (i, 0)),
            pl.BlockSpec((CONV_ROWS, 1), lambda i: (i, 0)),
        ],
        out_specs=pl.BlockSpec((CONV_ROWS, t), lambda i: (i, 0)),
        out_shape=jax.ShapeDtypeStruct((XBC_DIM, t), F32),
        compiler_params=_cparams("parallel"),
    )(proj, w_col, b_col)


def _conv_silu_bwd(proj, w_col, b_col, dout, dproj):
    t = proj.shape[1]
    p0 = OFF_X // CONV_ROWS

    def body(x_ref, w_ref, b_ref, do_ref, dproj_ref, dx_ref, dwb_ref):
        xv = x_ref[...]
        wv = w_ref[...]
        y, shifted = _causal_conv(xv, wv, b_ref[...])
        sg = _sigmoid(y)
        dy = do_ref[...] * (sg * (1.0 + y * (1.0 - sg)))
        dx, dwb_ref[...] = _causal_conv_bwd(dy, xv, shifted, wv)
        dx_ref[...] = dx.astype(BF16)

    return pl.pallas_call(
        body,
        name="ssd_conv_bwd",
        grid=(XBC_DIM // CONV_ROWS,),
        in_specs=[
            pl.BlockSpec((CONV_ROWS, t), lambda i: (p0 + i, 0)),
            pl.BlockSpec((CONV_ROWS, SSD_CONV), lambda i: (i, 0)),
            pl.BlockSpec((CONV_ROWS, 1), lambda i: (i, 0)),
            pl.BlockSpec((CONV_ROWS, t), lambda i: (i, 0)),
            pl.BlockSpec(memory_space=pl.ANY),
        ],
        out_specs=[pl.BlockSpec((CONV_ROWS, t), lambda i: (p0 + i, 0)), pl.BlockSpec((CONV_ROWS, 128), lambda i: (i, 0))],
        out_shape=[jax.ShapeDtypeStruct(dproj.shape, BF16), jax.ShapeDtypeStruct((XBC_DIM, 128), F32)],
        input_output_aliases={4: 0},
        compiler_params=_cparams("parallel"),
    )(proj, w_col, b_col, dout, dproj)


def _ssd_specs(order):
    xb = D_INNER // BC_DIM
    dtb = OFF_DT // N_SSD_HEADS
    col = pl.BlockSpec((N_SSD_HEADS, 1), lambda c: (0, 0))
    return [
        pl.BlockSpec((D_INNER, CHUNK), lambda c: (0, order(c))),
        pl.BlockSpec((BC_DIM, CHUNK), lambda c: (xb, order(c))),
        pl.BlockSpec((BC_DIM, CHUNK), lambda c: (xb + 1, order(c))),
        pl.BlockSpec((N_SSD_HEADS, CHUNK), lambda c: (dtb, order(c))),
        col, col, col,
    ]


def _ssd_common(dt_ref, dtb_ref, alog_ref):
    z = dt_ref[...] + dtb_ref[...]
    dt = _softplus(z)
    a_neg = -jnp.exp(alog_ref[...])
    d_a = dt * a_neg
    row = lax.broadcasted_iota(jnp.int32, (CHUNK, CHUNK), 0)
    colm = lax.broadcasted_iota(jnp.int32, (CHUNK, CHUNK), 1)
    upper = (row <= colm).astype(F32)
    a_cs = jnp.dot(d_a, upper, precision=HIGHEST, preferred_element_type=F32)
    a_last = _rowsum(d_a)
    return z, dt, a_neg, a_cs, a_last, row >= colm, row == colm


def _decay(a_row, causal):
    a_s = jnp.broadcast_to(a_row, (CHUNK, CHUNK))
    seg = a_s.T - a_s
    return jnp.where(causal, jnp.exp(jnp.where(causal, seg, 0.0)), 0.0)


def _ssd_fwd(xbc, proj, dtb_col, alog_col, dsk_col, gnw_col):
    t = xbc.shape[1]
    nc = t // CHUNK

    def body(xs_ref, b_ref, c_ref, dt_ref, dtb_ref, alog_ref, dsk_ref, *rest):
        z_refs, (gnw_ref, y_ref, hst_ref, yn_ref, h_scr) = rest[:N_SSD_GROUPS], rest[N_SSD_GROUPS:]

        @pl.when(pl.program_id(0) == 0)
        def _():
            h_scr[...] = jnp.zeros_like(h_scr)

        _, dt, _, a_cs, a_last, causal, _ = _ssd_common(dt_ref, dtb_ref, alog_ref)
        hst_ref[0] = h_scr[...]
        dsk = dsk_ref[...]
        for g in range(N_SSD_GROUPS):
            grows = slice(g * D_STATE, (g + 1) * D_STATE)
            bb = b_ref[grows, :].astype(BF16)
            cb_ = c_ref[grows, :].astype(BF16)
            cb = _dot_tn(cb_, bb)
            for j in range(g * HEADS_PER_GROUP, (g + 1) * HEADS_PER_GROUP):
                rows = slice(j * SSD_HEAD_DIM, (j + 1) * SSD_HEAD_DIM)
                a = a_cs[j:j + 1, :]
                m = (cb * _decay(a, causal)).astype(BF16)
                xs = xs_ref[rows, :]
                xc = xs * dt[j:j + 1, :]
                hj = h_scr[rows, :]
                y = _dot_nt(xc.astype(BF16), m) + _dot(hj.astype(BF16), cb_) * jnp.exp(a) + dsk[j:j + 1, :] * xs
                y_ref[rows, :] = y
                al = a_last[j:j + 1, :]
                w = jnp.exp(al - a)
                h_scr[rows, :] = jnp.exp(al) * hj + _dot_nt((xc * w).astype(BF16), bb)
        for g in range(N_SSD_GROUPS):
            rows = slice(g * GN_ROWS, (g + 1) * GN_ROWS)
            zv = z_refs[g][...]
            u = y_ref[rows, :] * (zv * _sigmoid(zv))
            r = lax.rsqrt(jnp.mean(u * u, axis=0, keepdims=True) + EPS)
            yn_ref[rows, :] = (u * r * gnw_ref[rows, :]).astype(BF16)

    z0 = OFF_Z // GN_ROWS
    z_specs = [pl.BlockSpec((GN_ROWS, CHUNK), lambda c, g=g: (z0 + g, c)) for g in range(N_SSD_GROUPS)]
    rows_spec = pl.BlockSpec((D_INNER, CHUNK), lambda c: (0, c))
    return pl.pallas_call(
        body,
        name="ssd_fwd",
        grid=(nc,),
        in_specs=_ssd_specs(lambda c: c) + z_specs + [pl.BlockSpec((D_INNER, 1), lambda c: (0, 0))],
        out_specs=[rows_spec, pl.BlockSpec((1, D_INNER, D_STATE), lambda c: (c, 0, 0)), rows_spec],
        out_shape=[
            jax.ShapeDtypeStruct((D_INNER, t), F32),
            jax.ShapeDtypeStruct((nc, D_INNER, D_STATE), F32),
            jax.ShapeDtypeStruct((D_INNER, t), BF16),
        ],
        scratch_shapes=[pltpu.VMEM((D_INNER, D_STATE), F32)],
        compiler_params=_cparams("arbitrary"),
    )(xbc, xbc, xbc, proj, dtb_col, alog_col, dsk_col, *([proj] * N_SSD_GROUPS), gnw_col)


def _ssd_bwd(xbc, proj, dtb_col, alog_col, dsk_col, hst, y, dyn, gnw_col, send=None):
    t = xbc.shape[1]
    nc = t // CHUNK
    rev = lambda c: nc - 1 - c

    def body(xs_ref, b_ref, c_ref, dt_ref, dtb_ref, alog_ref, dsk_ref, hst_ref, y_ref, dyn_ref, *rest):
        z_refs, rest = rest[:N_SSD_GROUPS], rest[N_SSD_GROUPS:]
        (gnw_ref, dxbc_ref, ddt_ref, dalog_ref, ddsk_ref, ddtb_ref, dz_ref, dgnw_ref,
         dh_scr, da_scr, ddt_scr, dd_scr, dy_ref) = rest
        dxs_ref = dxbc_ref.at[pl.ds(0, D_INNER)]
        db_ref = dxbc_ref.at[pl.ds(D_INNER, BC_DIM)]
        dc_ref = dxbc_ref.at[pl.ds(D_INNER + BC_DIM, BC_DIM)]

        @pl.when(pl.program_id(0) == 0)
        def _():
            dh_scr[...] = jnp.zeros_like(dh_scr)
            dalog_ref[...] = jnp.zeros_like(dalog_ref)
            ddsk_ref[...] = jnp.zeros_like(ddsk_ref)
            ddtb_ref[...] = jnp.zeros_like(ddtb_ref)
            dgnw_ref[...] = jnp.zeros_like(dgnw_ref)

        for g in range(N_SSD_GROUPS):
            rows = slice(g * GN_ROWS, (g + 1) * GN_ROWS)
            zv = z_refs[g][...]
            yv = y_ref[rows, :]
            sg = _sigmoid(zv)
            sz = zv * sg
            u = yv * sz
            r = lax.rsqrt(jnp.mean(u * u, axis=0, keepdims=True) + EPS)
            xhat = u * r
            dov = dyn_ref[rows, :]
            dgnw_ref[rows, :] += _rowsum(dov * xhat)
            dxhat = dov * gnw_ref[rows, :]
            du = r * (dxhat - xhat * jnp.mean(dxhat * xhat, axis=0, keepdims=True))
            dy_ref[rows, :] = du * sz
            dz_ref[rows, :] = (du * yv * (sg * (1.0 + zv * (1.0 - sg)))).astype(BF16)

        z, dt, a_neg, a_cs, a_last, causal, eye = _ssd_common(dt_ref, dtb_ref, alog_ref)
        dsk = dsk_ref[...]
        last_lane = lax.broadcasted_iota(jnp.int32, (1, CHUNK), 1) == CHUNK - 1
        for g in range(N_SSD_GROUPS):
            grows = slice(g * D_STATE, (g + 1) * D_STATE)
            bb = b_ref[grows, :].astype(BF16)
            cb_ = c_ref[grows, :].astype(BF16)
            cb = _dot_tn(cb_, bb)
            dcb = jnp.zeros((CHUNK, CHUNK), F32)
            dc_acc = jnp.zeros((D_STATE, CHUNK), F32)
            db_acc = jnp.zeros((D_STATE, CHUNK), F32)
            for j in range(g * HEADS_PER_GROUP, (g + 1) * HEADS_PER_GROUP):
                rows = slice(j * SSD_HEAD_DIM, (j + 1) * SSD_HEAD_DIM)
                a = a_cs[j:j + 1, :]
                al = a_last[j:j + 1, :]
                lam = _decay(a, causal)
                mf = cb * lam
                xs = xs_ref[rows, :]
                dtj = dt[j:j + 1, :]
                xc = xs * dtj
                w = jnp.exp(al - a)
                e = jnp.exp(a)
                gam = jnp.exp(al)
                hj = hst_ref[0, rows, :]
                hjb = hj.astype(BF16)
                dyv = dy_ref[rows, :]
                dyb = dyv.astype(BF16)
                dd_scr[j:j + 1, :] = _colsum(dyv * xs)
                gb = (dyv * e).astype(BF16)
                dh_in = _dot_nt(gb, cb_)
                dc_acc = dc_acc + _dot_tn(hjb, gb)
                yoff = _dot(hjb, cb_) * e
                da = _colsum(dyv * yoff)
                dm = _dot_tn(dyb, xc.astype(BF16))
                dxc = _dot(dyb, mf.astype(BF16))
                dcb = dcb + dm * lam
                nmat = dm * mf
                rs = jnp.broadcast_to(_rowsum(nmat), (CHUNK, CHUNK))
                da = da + _colsum(jnp.where(eye, rs, 0.0)) - _colsum(nmat)
                ds = dh_scr[rows, :]
                dsb = ds.astype(BF16)
                t1 = _dot(dsb, bb)
                xcw = xc * w
                dxc = dxc + w * t1
                dww = _colsum(xcw * t1)
                da_l = _rowsum(dww) + _rowsum(_colsum(ds * hj)) * gam
                da = da - dww + jnp.where(last_lane, da_l, 0.0)
                db_acc = db_acc + _dot_tn(dsb, xcw.astype(BF16))
                dh_scr[rows, :] = gam * ds + dh_in
                dxs_ref[rows, :] = dsk[j:j + 1, :] * dyv + dxc * dtj
                da_scr[j:j + 1, :] = da
                ddt_scr[j:j + 1, :] = _colsum(dxc * xs)
            dcbb = dcb.astype(BF16)
            dc_ref[grows, :] = dc_acc + _dot_nt(bb, dcbb)
            db_ref[grows, :] = db_acc + _dot(cb_, dcbb)
        dda = jnp.dot(da_scr[...], causal.astype(F32), precision=HIGHEST, preferred_element_type=F32)
        ddt = ddt_scr[...] + dda * a_neg
        ddt_raw = ddt * _sigmoid(z)
        ddt_ref[...] = ddt_raw
        ddtb_ref[...] += _rowsum(ddt_raw)
        dalog_ref[...] += _rowsum(dda * dt) * a_neg
        ddsk_ref[...] += _rowsum(dd_scr[...])

    col = pl.BlockSpec((N_SSD_HEADS, 1), lambda c: (0, 0))
    xs_spec = pl.BlockSpec((D_INNER, CHUNK), lambda c: (0, rev(c)))
    gn_col = pl.BlockSpec((D_INNER, 1), lambda c: (0, 0))
    z0 = OFF_Z // GN_ROWS
    z_specs = [pl.BlockSpec((GN_ROWS, CHUNK), lambda c, g=g: (z0 + g, rev(c))) for g in range(N_SSD_GROUPS)]
    dz_spec = pl.BlockSpec((pl.Element(D_INNER), pl.Element(CHUNK)),
                           lambda c: (OFF_Z, pl.multiple_of(CHUNK * rev(c), CHUNK)))
    small = pltpu.VMEM((N_SSD_HEADS, CHUNK), F32)
    return _call(
        body,
        name="ssd_bwd",
        grid=(nc,),
        in_specs=_ssd_specs(rev) + [pl.BlockSpec((1, D_INNER, D_STATE), lambda c: (rev(c), 0, 0)), xs_spec, xs_spec]
        + z_specs + [gn_col],
        out_specs=[pl.BlockSpec((XBC_DIM, CHUNK), lambda c: (0, rev(c))),
                   pl.BlockSpec((N_SSD_HEADS, CHUNK), lambda c: (0, rev(c))), col, col, col, dz_spec, gn_col],
        out_shape=[
            jax.ShapeDtypeStruct((XBC_DIM, t), F32),
            jax.ShapeDtypeStruct((N_SSD_HEADS, t), F32),
            jax.ShapeDtypeStruct((N_SSD_HEADS, 1), F32),
            jax.ShapeDtypeStruct((N_SSD_HEADS, 1), F32),
            jax.ShapeDtypeStruct((N_SSD_HEADS, 1), F32),
            jax.ShapeDtypeStruct((IN_DIM, t), BF16),
            jax.ShapeDtypeStruct((D_INNER, 1), F32),
        ],
        scratch_shapes=[pltpu.VMEM((D_INNER, D_STATE), F32), small, small, small, pltpu.VMEM((D_INNER, CHUNK), F32)],
        semantics=("arbitrary",),
        args=(xbc, xbc, xbc, proj, dtb_col, alog_col, dsk_col, hst, y, dyn, *([proj] * N_SSD_GROUPS), gnw_col),
        send=send)


GATE_ROWS = 128


def _gate_specs(t):
    nr = D_MODEL // GATE_ROWS
    blk = pl.BlockSpec((GATE_ROWS, t), lambda r: (r, 0))
    rows_from = lambda first: pl.BlockSpec(
        (pl.Element(GATE_ROWS), pl.Element(t)), lambda r: (pl.multiple_of(first + GATE_ROWS * r, N_SSD_HEADS), 0))
    return blk, [
        rows_from(OFF_GA),
        rows_from(OFF_GS),
        pl.BlockSpec((GATE_ROWS, 1), lambda r: (r, 0)),
        pl.BlockSpec((GATE_ROWS, 1), lambda r: (nr + r, 0)),
        blk, blk,
    ]


def _gate_fwd(proj, b_col, attn, ssd):
    t = proj.shape[1]
    blk, specs = _gate_specs(t)

    def body(ga_ref, gs_ref, ba_ref, bs_ref, a_ref, s_ref, o_ref):
        o_ref[...] = (_sigmoid(ga_ref[...] + ba_ref[...]) * a_ref[...]
                      + _sigmoid(gs_ref[...] + bs_ref[...]) * s_ref[...]).astype(BF16)

    return pl.pallas_call(
        body,
        name="gate_fwd",
        grid=(D_MODEL // GATE_ROWS,),
        in_specs=specs,
        out_specs=blk,
        out_shape=jax.ShapeDtypeStruct((D_MODEL, t), BF16),
        compiler_params=_cparams("parallel"),
    )(proj, proj, b_col, b_col, attn, ssd)


def _gate_bwd(proj, b_col, attn, ssd, dmix, send=None):
    t = proj.shape[1]
    blk, specs = _gate_specs(t)

    def body(ga_ref, gs_ref, ba_ref, bs_ref, a_ref, s_ref, dm_ref, da_ref, dso_ref, dga_ref, dgs_ref, dba_ref, dbs_ref):
        dm = dm_ref[...]
        sa = _sigmoid(ga_ref[...] + ba_ref[...])
        ss = _sigmoid(gs_ref[...] + bs_ref[...])
        da_ref[...] = (dm * sa).astype(BF16)
        dso_ref[...] = (dm * ss).astype(BF16)
        dga = dm * a_ref[...] * sa * (1.0 - sa)
        dgs = dm * s_ref[...] * ss * (1.0 - ss)
        dga_ref[...] = dga.astype(BF16)
        dgs_ref[...] = dgs.astype(BF16)
        dba_ref[...] = _rowsum(dga)
        dbs_ref[...] = _rowsum(dgs)

    col = pl.BlockSpec((GATE_ROWS, 1), lambda r: (r, 0))
    act = jax.ShapeDtypeStruct((D_MODEL, t), BF16)
    bias = jax.ShapeDtypeStruct((D_MODEL, 1), F32)
    return _call(
        body,
        name="gate_bwd",
        grid=(D_MODEL // GATE_ROWS,),
        in_specs=specs + [blk],
        out_specs=[blk, blk, blk, blk, col, col],
        out_shape=[act, act, act, act, bias, bias],
        semantics=("parallel",), args=(proj, proj, b_col, b_col, attn, ssd, dmix), send=send)


FFN_ROWS = 256


def _ffn_fwd(u0, w_col, b_col):
    t = u0.shape[2]

    def body(u_ref, w_ref, b_ref, o_ref):
        val, _ = _causal_conv(u_ref[0], w_ref[0], b_ref[0])
        gt, _ = _causal_conv(u_ref[1], w_ref[1], b_ref[1])
        o_ref[...] = (gt * _sigmoid(gt) * val).astype(BF16)

    return pl.pallas_call(
        body,
        name="ffn_fwd",
        grid=(D_FF // FFN_ROWS,),
        in_specs=[
            pl.BlockSpec((2, FFN_ROWS, t), lambda i: (0, i, 0)),
            pl.BlockSpec((2, FFN_ROWS, FFN_CONV), lambda i: (0, i, 0)),
            pl.BlockSpec((2, FFN_ROWS, 1), lambda i: (0, i, 0)),
        ],
        out_specs=pl.BlockSpec((FFN_ROWS, t), lambda i: (i, 0)),
        out_shape=jax.ShapeDtypeStruct((D_FF, t), BF16),
        compiler_params=_cparams("parallel"),
    )(u0, w_col, b_col)


def _ffn_bwd(u0, w_col, b_col, dg, send=None):
    t = u0.shape[2]

    def body(u_ref, w_ref, b_ref, dg_ref, du_ref, dwb_ref):
        xval, wval = u_ref[0], w_ref[0]
        xgt, wgt = u_ref[1], w_ref[1]
        val, sh_val = _causal_conv(xval, wval, b_ref[0])
        gt, sh_gt = _causal_conv(xgt, wgt, b_ref[1])
        sg = _sigmoid(gt)
        dgv = dg_ref[...]
        dval = dgv * (gt * sg)
        dgt = dgv * val * (sg * (1.0 + gt * (1.0 - sg)))
        dx, dwb_ref[0] = _causal_conv_bwd(dval, xval, sh_val, wval)
        du_ref[0] = dx.astype(BF16)
        dx, dwb_ref[1] = _causal_conv_bwd(dgt, xgt, sh_gt, wgt)
        du_ref[1] = dx.astype(BF16)

    return _call(
        body,
        name="ffn_bwd",
        grid=(D_FF // FFN_ROWS,),
        in_specs=[
            pl.BlockSpec((2, FFN_ROWS, t), lambda i: (0, i, 0)),
            pl.BlockSpec((2, FFN_ROWS, FFN_CONV), lambda i: (0, i, 0)),
            pl.BlockSpec((2, FFN_ROWS, 1), lambda i: (0, i, 0)),
            pl.BlockSpec((FFN_ROWS, t), lambda i: (i, 0)),
        ],
        out_specs=[pl.BlockSpec((2, FFN_ROWS, t), lambda i: (0, i, 0)), pl.BlockSpec((2, FFN_ROWS, 128), lambda i: (0, i, 0))],
        out_shape=[jax.ShapeDtypeStruct((2, D_FF, t), BF16), jax.ShapeDtypeStruct((2, D_FF, 128), F32)],
        semantics=("parallel",), args=(u0, w_col, b_col, dg), send=send)


def _adamw_math(w, g, m, v):
    m = ADAM_B1 * m + (1.0 - ADAM_B1) * g
    v = ADAM_B2 * v + (1.0 - ADAM_B2) * (g * g)
    m_hat = m / (1.0 - ADAM_B1 ** ADAM_STEP)
    v_hat = v / (1.0 - ADAM_B2 ** ADAM_STEP)
    delta = -ADAM_LR * (m_hat / (jnp.sqrt(v_hat) + ADAM_EPS) + ADAM_WD * w)
    return delta, m, v


def _adamw_sharded(parts, w, m, v, name):
    r, c = w.shape[0], w.shape[-1]
    slots = parts.shape[0]
    per_lane = 2 * r * (slots * parts.dtype.itemsize + 7 * w.dtype.itemsize)
    tc = max(d for d in range(LANES, c + 1, LANES) if c % d == 0 and (d * per_lane <= BLOCK_VMEM_BUDGET or d == LANES))
    blk_shape = (r, tc) if w.ndim == 2 else (r, 1, tc)

    def body(p_ref, w_ref, m_ref, v_ref, g_ref, d_ref, nm_ref, nv_ref):
        g = p_ref[0].astype(F32)
        for s in range(1, slots):
            g = g + p_ref[s].astype(F32)
        flat = lambda ref: ref[...].reshape(r, tc)
        d, nm, nv = _adamw_math(flat(w_ref), g, flat(m_ref), flat(v_ref))
        for ref, val in ((g_ref, g), (d_ref, d), (nm_ref, nm), (nv_ref, nv)):
            ref[...] = val.reshape(blk_shape)

    blk = pl.BlockSpec(blk_shape, (lambda i: (0, i)) if w.ndim == 2 else (lambda i: (0, 0, i)))
    out = jax.ShapeDtypeStruct(w.shape, F32)
    return pl.pallas_call(
        body,
        name=name,
        grid=(c // tc,),
        in_specs=[pl.BlockSpec((slots, r, tc), lambda i: (0, 0, i)), blk, blk, blk],
        out_specs=[blk, blk, blk, blk],
        out_shape=[out, out, out, out],
        compiler_params=_cparams("parallel"),
    )(parts, w, m, v)


def _lane_offsets(sizes):
    offsets, pos = [], 0
    for n in sizes:
        offsets.append(pos)
        pos += -(-n // 128) * 128
    return offsets, pos


def _pack_row(parts):
    rows = [p.reshape(1, -1).astype(F32) for p in parts]
    return jnp.concatenate([jnp.pad(r, ((0, 0), (0, -r.shape[1] % 128))) for r in rows], axis=1)


def _small_update(parts, me, full_sizes, ws, ms, vs):
    n = len(ws)
    offsets, _ = _lane_offsets([1] + list(full_sizes))

    def body(me_ref, p_ref, *refs):
        w_refs, m_refs, v_refs = refs[:n], refs[n:2 * n], refs[2 * n:3 * n]
        scalar_ref, out_refs = refs[3 * n], refs[3 * n + 1:]
        tot = p_ref[0]
        for s in range(1, N_DEV):
            tot = tot + p_ref[s]
        scalar_ref[...] = tot[:, 0:1]
        for k in range(n):
            g_ref, d_ref, nm_ref, nv_ref = out_refs[4 * k:4 * k + 4]
            taps, cols = w_refs[k].shape
            if taps == 1:
                g_ref[...] = tot[:, offsets[k + 1]:offsets[k + 1] + cols]
            else:
                full = full_sizes[k] // taps
                for tap in range(taps):
                    mine = jnp.zeros((1, cols), F32)
                    for d in range(N_DEV):
                        lo = offsets[k + 1] + tap * full + d * cols
                        mine = jnp.where(me_ref[0] == d, tot[:, lo:lo + cols], mine)
                    g_ref[tap:tap + 1, :] = mine
            d_ref[...], nm_ref[...], nv_ref[...] = _adamw_math(w_refs[k][...], g_ref[...], m_refs[k][...], v_refs[k][...])

    vmem = pl.BlockSpec(memory_space=pltpu.VMEM)
    out_shape = [jax.ShapeDtypeStruct((1, 1), F32)]
    for wk in ws:
        out_shape += [jax.ShapeDtypeStruct(wk.shape, F32)] * 4
    res = pl.pallas_call(
        body,
        name="small_update",
        in_specs=[pl.BlockSpec(memory_space=pltpu.SMEM)] + [vmem] * (1 + 3 * n),
        out_specs=[vmem] * len(out_shape),
        out_shape=out_shape,
    )(me, parts, *ws, *ms, *vs)
    return res[0], [res[1 + 4 * k:5 + 4 * k] for k in range(n)]


ANY = pl.BlockSpec(memory_space=pl.ANY)
FLIPS = [(k >> 2 & 1, k >> 1 & 1, k & 1) for k in range(1, N_DEV)]


def _place():
    return lax.axis_index("x"), lax.axis_index("y"), lax.axis_index("c")


HBM = pl.BlockSpec(memory_space=pltpu.HBM)
SEM = pl.BlockSpec(memory_space=pltpu.SEMAPHORE)
EFFECT = pltpu.SideEffectType.DATAFLOW_SIDE_EFFECTING


def _peer_copy(gather, src_ref, land_ref, send_sems, recv_sems, k, sending):
    x, y, c = _place()
    fx, fy, fc = FLIPS[k]
    me = 4 * x + 2 * y + c
    peer = 4 * (x ^ fx) + 2 * (y ^ fy) + (c ^ fc)
    return pltpu.make_async_remote_copy(
        src_ref=src_ref if gather else src_ref.at[peer],
        dst_ref=land_ref.at[me if sending else peer],
        send_sem=send_sems.at[k], recv_sem=recv_sems.at[k],
        device_id=(x ^ fx, y ^ fy, c ^ fc), device_id_type=MESH)


SIBLING = 0
OTHER_CHIPS = (1, 3, 5)


def _gather_start(srcs, name, via_sibling):
    n = len(srcs)
    lands = [lax.empty((N_DEV,) + s.shape, s.dtype) for s in srcs]

    def body(*refs):
        src_refs, land_refs = refs[:n], refs[n:2 * n]
        send, recv = refs[2 * n:3 * n], refs[3 * n:4 * n]
        for i in range(n):
            for k in (SIBLING,) + OTHER_CHIPS if via_sibling else range(N_DEV - 1):
                _peer_copy(True, src_refs[i], land_refs[i], send[i], recv[i], k, True).start()

    sem = pltpu.SemaphoreType.DMA((N_DEV - 1,))
    hbm = lambda a: pltpu.HBM(a.shape, a.dtype)
    res = pl.pallas_call(
        body,
        name=name,
        in_specs=[HBM] * (2 * n),
        out_specs=[SEM] * (2 * n) + [HBM] * (2 * n),
        out_shape=[sem] * (2 * n) + [hbm(s) for s in srcs] + [hbm(a) for a in lands],
        input_output_aliases={i: 2 * n + i for i in range(2 * n)},
        compiler_params=pltpu.CompilerParams(has_side_effects=EFFECT),
    )(*[pltpu.with_memory_space_constraint(a, pltpu.HBM) for a in list(srcs) + lands])
    return res[:n], res[n:2 * n], res[2 * n:3 * n], res[3 * n:4 * n]


def _exchange_wait(send_sems, recv_sems, src, land, after, gather, name):
    def body(src_ref, land_ref, send_ref, recv_ref, after_ref, src_out, land_out):
        for k in range(N_DEV - 1):
            cp = _peer_copy(gather, src_ref, land_ref, send_ref, recv_ref, k, False)
            cp.wait_send()
            cp.wait_recv()

    hbm = lambda a: pltpu.HBM(a.shape, a.dtype)
    return pl.pallas_call(
        body,
        name=name,
        in_specs=[HBM, HBM, SEM, SEM, ANY],
        out_specs=[HBM, HBM],
        out_shape=[hbm(src), hbm(land)],
        input_output_aliases={0: 0, 1: 1},
        compiler_params=pltpu.CompilerParams(has_side_effects=EFFECT),
    )(src, land, send_sems, recv_sems, after)


def _own_slot(src, land, me, gather):
    own = src[None] if gather else lax.dynamic_slice_in_dim(src, me, 1, axis=0)
    return lax.dynamic_update_slice_in_dim(land, own, me, axis=0)


def _forwarded_copy(land_ref, send_sems, recv_sems, j, sending):
    x, y, c = _place()
    fx, fy, _ = FLIPS[OTHER_CHIPS[j]]
    slot = 4 * (x ^ fx) + 2 * (y ^ fy) + (c if sending else 1 - c)
    return pltpu.make_async_remote_copy(
        src_ref=land_ref.at[slot], dst_ref=land_ref.at[slot], send_sem=send_sems.at[j], recv_sem=recv_sems.at[j],
        device_id=(x, y, 1 - c), device_id_type=MESH)


def _gather_forward(send_sems, recv_sems, srcs, lands, after, name):
    n = len(srcs)

    def body(*refs):
        src_refs, land_refs = refs[:n], refs[n:2 * n]
        send, recv = refs[2 * n:3 * n], refs[3 * n:4 * n]
        fwd_send, fwd_recv = refs[4 * n + 1:5 * n + 1], refs[5 * n + 1:6 * n + 1]
        for i in range(n):
            for j, k in enumerate(OTHER_CHIPS):
                _peer_copy(True, src_refs[i], land_refs[i], send[i], recv[i], k, False).wait_recv()
                _forwarded_copy(land_refs[i], fwd_send[i], fwd_recv[i], j, True).start()

    sem = pltpu.SemaphoreType.DMA((len(OTHER_CHIPS),))
    hbm = lambda a: pltpu.HBM(a.shape, a.dtype)
    res = pl.pallas_call(
        body,
        name=name,
        in_specs=[HBM] * (2 * n) + [SEM] * (2 * n) + [ANY],
        out_specs=[SEM] * (2 * n) + [HBM] * (2 * n),
        out_shape=[sem] * (2 * n) + [hbm(a) for a in srcs] + [hbm(a) for a in lands],
        input_output_aliases={i: 2 * n + i for i in range(2 * n)},
        compiler_params=pltpu.CompilerParams(has_side_effects=EFFECT),
    )(*srcs, *lands, *send_sems, *recv_sems, after)
    return res[:n], res[n:2 * n], res[2 * n:3 * n], res[3 * n:4 * n]


def _gather_wait_forwarded(send_sems, recv_sems, fwd_send, fwd_recv, src, land, after, name):
    def body(src_ref, land_ref, send_ref, recv_ref, fwd_send_ref, fwd_recv_ref, after_ref, src_out, land_out):
        for k in (SIBLING,) + OTHER_CHIPS:
            _peer_copy(True, src_ref, land_ref, send_ref, recv_ref, k, False).wait_send()
        _peer_copy(True, src_ref, land_ref, send_ref, recv_ref, SIBLING, False).wait_recv()
        for j in range(len(OTHER_CHIPS)):
            _forwarded_copy(land_ref, fwd_send_ref, fwd_recv_ref, j, True).wait_send()
            _forwarded_copy(land_ref, fwd_send_ref, fwd_recv_ref, j, False).wait_recv()

    hbm = lambda a: pltpu.HBM(a.shape, a.dtype)
    return pl.pallas_call(
        body,
        name=name,
        in_specs=[HBM, HBM, SEM, SEM, SEM, SEM, ANY],
        out_specs=[HBM, HBM],
        out_shape=[hbm(src), hbm(land)],
        input_output_aliases={0: 0, 1: 1},
        compiler_params=pltpu.CompilerParams(has_side_effects=EFFECT),
    )(src, land, send_sems, recv_sems, fwd_send, fwd_recv, after)


N_CHIPS = N_DEV // 2


def _pair_exchange(by_core, meanwhile, name):
    def copy(src_ref, land_ref, send_sems, recv_sems, q):
        x, y, c = _place()
        return pltpu.make_async_remote_copy(
            src_ref=src_ref.at[q, 1 - c], dst_ref=land_ref.at[q], send_sem=send_sems.at[q], recv_sem=recv_sems.at[q],
            device_id=(x, y, 1 - c), device_id_type=MESH)

    def start(src_ref, land_ref, send_sems, recv_sems, src_out, land_out):
        for q in range(N_CHIPS):
            copy(src_ref, land_ref, send_sems, recv_sems, q).start()

    def wait(src_ref, land_ref, send_sems, recv_sems, after_ref, src_out, land_out):
        for q in range(N_CHIPS):
            cp = copy(src_ref, land_ref, send_sems, recv_sems, q)
            cp.wait_send()
            cp.wait_recv()

    sem = pltpu.SemaphoreType.DMA((N_CHIPS,))
    hbm_src = pltpu.HBM(by_core.shape, by_core.dtype)
    hbm_land = pltpu.HBM(by_core.shape[:1] + by_core.shape[2:], by_core.dtype)
    params = pltpu.CompilerParams(has_side_effects=EFFECT)
    send_sems, recv_sems, src, land = pl.pallas_call(
        start, name=name + "_start", in_specs=[HBM, HBM], out_specs=[SEM, SEM, HBM, HBM],
        out_shape=[sem, sem, hbm_src, hbm_land], input_output_aliases={0: 2, 1: 3}, compiler_params=params,
    )(pltpu.with_memory_space_constraint(by_core, pltpu.HBM),
      pltpu.with_memory_space_constraint(lax.empty(hbm_land.shape, by_core.dtype), pltpu.HBM))
    return pl.pallas_call(
        wait, name=name + "_wait", in_specs=[HBM, HBM, SEM, SEM, ANY], out_specs=[HBM, HBM],
        out_shape=[hbm_src, hbm_land], input_output_aliases={0: 0, 1: 1}, compiler_params=params,
    )(src, land, send_sems, recv_sems, meanwhile(src))


def _pair_add(by_core, landed, name):
    q, _, r, c = by_core.shape
    tc = _tile(c, (512, 256, 128))

    def body(a_ref, b_ref, o_ref):
        mine = a_ref[0, lax.axis_index("c")]
        o_ref[0] = (mine.astype(F32) + b_ref[0].astype(F32)).astype(BF16)

    blk = pl.BlockSpec((1, r, tc), lambda i, j: (i, 0, j))
    return pl.pallas_call(
        body, name=name, grid=(q, c // tc),
        in_specs=[pl.BlockSpec((1, 2, r, tc), lambda i, j: (i, 0, 0, j)), blk], out_specs=blk,
        out_shape=jax.ShapeDtypeStruct(landed.shape, BF16), compiler_params=_cparams("parallel", "parallel"),
    )(by_core, landed)


def _chip_copy(src_ref, land_ref, send_sems, recv_sems, j, sending):
    x, y, c = _place()
    fx, fy, _ = FLIPS[OTHER_CHIPS[j]]
    here, there = 2 * x + y, 2 * (x ^ fx) + (y ^ fy)
    return pltpu.make_async_remote_copy(
        src_ref=src_ref.at[there], dst_ref=land_ref.at[here if sending else there],
        send_sem=send_sems.at[j], recv_sem=recv_sems.at[j],
        device_id=(x ^ fx, y ^ fy, c), device_id_type=MESH)


def _chip_wait(send_sems, recv_sems, src, land, after, name):
    def body(src_ref, land_ref, send_ref, recv_ref, after_ref, src_out, land_out):
        for j in range(len(OTHER_CHIPS)):
            cp = _chip_copy(src_ref, land_ref, send_ref, recv_ref, j, False)
            cp.wait_send()
            cp.wait_recv()

    hbm = lambda a: pltpu.HBM(a.shape, a.dtype)
    return pl.pallas_call(
        body,
        name=name,
        in_specs=[HBM, HBM, SEM, SEM, ANY],
        out_specs=[HBM, HBM],
        out_shape=[hbm(src), hbm(land)],
        input_output_aliases={0: 0, 1: 1},
        compiler_params=pltpu.CompilerParams(has_side_effects=EFFECT),
    )(src, land, send_sems, recv_sems, after)


def _col(v):
    return v.reshape(-1, 1).astype(F32)


def _local_step(x, tgt, started, weight, small, pair_sums, handles):
    t = x.shape[0]
    n1 = _col(small["norm1_w"])
    n2 = _col(small["norm2_w"])
    nf = _col(small["final_norm_w"])
    bg = _col(small["b_gate"])
    sinks = small["attn_sinks"].reshape(-1).astype(F32)
    cbias = _col(small["ssd_conv_b"])
    dtb = _col(small["dt_bias"])
    alog = _col(small["a_log"])
    dsk = _col(small["d_skip"])
    gnw = _col(small["ssd_norm_w"])
    fb = small["ffn_conv_b"].reshape(2, D_FF, 1)

    xt, xn = _norm_fwd_tokens(x, n1, started, "norm1_fwd")
    cw = weight("ssd_conv_w", xn).T
    fw = weight("ffn_conv_w", xn).T.reshape(2, D_FF, FFN_CONV)
    w_in_t = weight("w_in", xn)
    proj = _matmul(w_in_t, xn, nt=False, out_dtype=F32, name="mm_in")
    ao, lse = _attn_fwd(proj, sinks)
    w_ao = weight("w_attn_o", ao)
    attn = _matmul(w_ao, ao, nt=False, out_dtype=F32, name="mm_attn_o", tn_a=True)
    xbc = _conv_silu_fwd(proj, cw, cbias)
    y, hst, yn = _ssd_fwd(xbc, proj, dtb, alog, dsk, gnw)
    w_so = weight("w_ssd_o", yn)
    ssd = _matmul(w_so, yn, nt=False, out_dtype=F32, name="mm_ssd_o", tn_a=True)
    mix = _gate_fwd(proj, bg, attn, ssd)
    w_out = weight("w_out", mix)
    h1 = _matmul(w_out, mix, nt=False, out_dtype=F32, name="mm_out", add=xt, tn_a=True)
    hn = _norm_fwd(h1, n2, "norm2_fwd")
    w_up_t = weight("w_up", hn)
    u0 = _matmul(w_up_t, hn, nt=False, out_dtype=F32, name="mm_up").reshape(2, D_FF, t)
    gl = _ffn_fwd(u0, fw, fb)
    w_down = weight("w_down", gl)
    h2 = _matmul(w_down, gl, nt=False, out_dtype=F32, name="mm_down", add=h1, tn_a=True)
    dh2, loss, d_nf = _final_norm_loss(h2, tgt, nf)

    g = {}

    def sending(weight_name, grad, fn, *args, **kwargs):
        chunks = grad if grad.ndim == 3 else grad.reshape(N_DEV, -1, D_MODEL)
        out, handles[weight_name] = fn(*args, send=chunks, **kwargs)
        return out

    g_down = _matmul(gl, dh2, nt=True, out_dtype=BF16, name="mm_d_w_down")
    dgl = _matmul(w_down, dh2, nt=False, out_dtype=F32, name="mm_d_glu")
    du0, d_fwb = sending("w_down", g_down, _ffn_bwd, u0, fw, fb, dgl)
    du0 = du0.reshape(2 * D_FF, t)
    g_up = _matmul(du0, hn, nt=True, out_dtype=BF16, name="mm_d_w_up")
    dh1, d_n2 = sending("w_up", g_up, _matmul, w_up_t, du0, nt=False, out_dtype=F32, name="mm_d_hn_norm2_bwd", tn_a=True,
                        norm_bwd=(h1, n2, dh2, False))
    g_out = _matmul(mix, dh1, nt=True, out_dtype=BF16, name="mm_d_w_out")
    dmix = _matmul(w_out, dh1, nt=False, out_dtype=F32, name="mm_d_mix")
    d_attn, d_ssd, d_ga, d_gs, d_ba, d_bs = sending("w_out", g_out, _gate_bwd, proj, bg, attn, ssd, dmix)
    g_ao = _matmul(ao, d_attn, nt=True, out_dtype=BF16, name="mm_d_w_attn_o")
    dao = _matmul(w_ao, d_attn, nt=False, out_dtype=F32, name="mm_d_ao")
    g_so = _matmul(yn, d_ssd, nt=True, out_dtype=BF16, name="mm_d_w_ssd_o")
    dyn = _matmul(w_so, d_ssd, nt=False, out_dtype=F32, name="mm_d_yn")
    dxbc, ddt, d_alog, d_dsk, d_dtb, dproj, d_gnw = sending(
        "w_ssd_o", g_so, _ssd_bwd, xbc, proj, dtb, alog, dsk, hst, y, dyn, gnw)
    dproj, dwb_conv = _conv_silu_bwd(proj, cw, cbias, dxbc, dproj)
    dproj, d_sinks = sending("w_attn_o", g_ao, _attn_bwd, proj, sinks, ao, lse, dao, dproj)
    for rows, part in ((OFF_DT, ddt.astype(BF16)), (OFF_GA, d_ga), (OFF_GS, d_gs)):
        dproj = lax.dynamic_update_slice(dproj, part, (rows, 0))
    g_in = pair_sums(_matmul(dproj, xn, nt=True, out_dtype=BF16, name="mm_d_w_in"))
    dx, d_n1 = sending("w_in", g_in, _matmul, w_in_t, dproj, nt=False, out_dtype=F32, name="mm_d_xn_norm1_bwd", tn_a=True,
                       norm_bwd=(xt, n1, dh1, True))

    g["norm1_w"] = d_n1
    g["b_gate"] = jnp.concatenate([d_ba, d_bs], axis=0)
    g["attn_sinks"] = d_sinks
    g["ssd_conv_w"] = dwb_conv[:, :SSD_CONV].T
    g["ssd_conv_b"] = dwb_conv[:, SSD_CONV]
    g["dt_bias"] = d_dtb
    g["a_log"] = d_alog
    g["d_skip"] = d_dsk
    g["ssd_norm_w"] = d_gnw
    g["norm2_w"] = d_n2
    d_fwb = d_fwb.reshape(2 * D_FF, 128)
    g["ffn_conv_w"] = d_fwb[:, :FFN_CONV].T
    g["ffn_conv_b"] = d_fwb[:, FFN_CONV]
    g["final_norm_w"] = d_nf
    return loss, dx, g


SMALL = ("norm1_w", "b_gate", "attn_sinks", "ssd_conv_w", "ssd_conv_b", "dt_bias", "a_log", "d_skip", "ssd_norm_w",
         "norm2_w", "ffn_conv_w", "ffn_conv_b", "final_norm_w")
WEIGHT_ORDER = ("norm1_w", "w_in", "b_gate", "attn_sinks", "w_attn_o", "ssd_conv_w", "ssd_conv_b", "dt_bias", "a_log",
                "d_skip", "ssd_norm_w", "w_ssd_o", "w_out", "norm2_w", "w_up", "ffn_conv_w", "ffn_conv_b", "w_down",
                "final_norm_w")


def kernel(x, norm1_w, w_in, b_gate, attn_sinks, w_attn_o, ssd_conv_w, ssd_conv_b, dt_bias, a_log, d_skip, ssd_norm_w, w_ssd_o, w_out, norm2_w, w_up, ffn_conv_w, ffn_conv_b, w_down, final_norm_w, loss_target, m_norm1_w, m_w_in, m_b_gate, m_attn_sinks, m_w_attn_o, m_ssd_conv_w, m_ssd_conv_b, m_dt_bias, m_a_log, m_d_skip, m_ssd_norm_w, m_w_ssd_o, m_w_out, m_norm2_w, m_w_up, m_ffn_conv_w, m_ffn_conv_b, m_w_down, m_final_norm_w, v_norm1_w, v_w_in, v_b_gate, v_attn_sinks, v_w_attn_o, v_ssd_conv_w, v_ssd_conv_b, v_dt_bias, v_a_log, v_d_skip, v_ssd_norm_w, v_w_ssd_o, v_w_out, v_norm2_w, v_w_up, v_ffn_conv_w, v_ffn_conv_b, v_w_down, v_final_norm_w):
    w = dict(norm1_w=norm1_w, w_in=w_in, b_gate=b_gate, attn_sinks=attn_sinks, w_attn_o=w_attn_o, ssd_conv_w=ssd_conv_w, ssd_conv_b=ssd_conv_b, dt_bias=dt_bias, a_log=a_log, d_skip=d_skip, ssd_norm_w=ssd_norm_w, w_ssd_o=w_ssd_o, w_out=w_out, norm2_w=norm2_w, w_up=w_up, ffn_conv_w=ffn_conv_w, ffn_conv_b=ffn_conv_b, w_down=w_down, final_norm_w=final_norm_w)
    m = dict(norm1_w=m_norm1_w, w_in=m_w_in, b_gate=m_b_gate, attn_sinks=m_attn_sinks, w_attn_o=m_w_attn_o, ssd_conv_w=m_ssd_conv_w, ssd_conv_b=m_ssd_conv_b, dt_bias=m_dt_bias, a_log=m_a_log, d_skip=m_d_skip, ssd_norm_w=m_ssd_norm_w, w_ssd_o=m_w_ssd_o, w_out=m_w_out, norm2_w=m_norm2_w, w_up=m_w_up, ffn_conv_w=m_ffn_conv_w, ffn_conv_b=m_ffn_conv_b, w_down=m_w_down, final_norm_w=m_final_norm_w)
    v = dict(norm1_w=v_norm1_w, w_in=v_w_in, b_gate=v_b_gate, attn_sinks=v_attn_sinks, w_attn_o=v_w_attn_o, ssd_conv_w=v_ssd_conv_w, ssd_conv_b=v_ssd_conv_b, dt_bias=v_dt_bias, a_log=v_a_log, d_skip=v_d_skip, ssd_norm_w=v_ssd_norm_w, w_ssd_o=v_w_ssd_o, w_out=v_w_out, norm2_w=v_norm2_w, w_up=v_w_up, ffn_conv_w=v_ffn_conv_w, ffn_conv_b=v_ffn_conv_b, w_down=v_w_down, final_norm_w=v_final_norm_w)
    me = 4 * lax.axis_index("x") + 2 * lax.axis_index("y") + lax.axis_index("c")

    shards = {"ssd_conv_w": ssd_conv_w[0], "ffn_conv_w": ffn_conv_w[0], "w_in": w_in[0].T.astype(BF16),
              "w_attn_o": w_attn_o[0].astype(BF16), "w_ssd_o": w_ssd_o[0].astype(BF16), "w_out": w_out[0].astype(BF16),
              "w_up": w_up[0].T.astype(BF16), "w_down": w_down[0].astype(BF16)}
    order = list(shards)
    g_send, g_recv, g_src, g_land = _gather_start(list(shards.values()), "gather_start", True)
    first = ("ssd_conv_w", "ffn_conv_w", "w_in")
    forwarded = {}

    def weight(name, after):
        if name not in forwarded:
            group = [k for k in order if (k in first) == (name in first)]
            idx = [order.index(k) for k in group]
            handles = _gather_forward([g_send[i] for i in idx], [g_recv[i] for i in idx], [g_src[i] for i in idx],
                                      [g_land[i] for i in idx], after, "gather_forward_for_" + name)
            forwarded.update(zip(group, zip(*handles)))
        i = order.index(name)
        src, land = _gather_wait_forwarded(g_send[i], g_recv[i], *forwarded[name], after, "gather_wait_" + name)
        land = _own_slot(src, land, me, True)
        if name == "ssd_conv_w":
            return jnp.transpose(land, (1, 0, 2)).reshape(SSD_CONV, XBC_DIM)
        if name == "ffn_conv_w":
            return jnp.transpose(land, (1, 0, 2)).reshape(FFN_CONV, 2 * D_FF)
        return land.reshape(-1, D_MODEL)

    res, pending = {}, {}

    def update(name, after):
        if name == "w_in":
            parts = _own_slot(*_chip_wait(*pending[name], after, "grad_wait_" + name), me // 2, False)
        else:
            parts = _own_slot(*_exchange_wait(*pending[name], after, False, "grad_wait_" + name), me, False)
        view, back = {
            "w_in": (lambda a: jnp.transpose(a, (2, 0, 1)), lambda r: jnp.transpose(r, (1, 2, 0))),
            "w_up": (lambda a: a[0].T, lambda r: r.T[None]),
        }.get(name, (lambda a: a[0], lambda r: r[None]))
        done = _adamw_sharded(parts, view(w[name]), view(m[name]), view(v[name]), "adamw_" + name)
        res[name] = [back(r) for r in done]
        return done[0]

    def pair_sums(grad):
        by_core, landed = _pair_exchange(grad.reshape(N_CHIPS, 2, -1, D_MODEL),
                                         lambda started: update("w_up", update("w_down", started)), "grad_pair_w_in")
        return _pair_add(by_core, landed, "grad_pair_add_w_in")

    small = {k: w[k][0] if k != "final_norm_w" else w[k] for k in SMALL}
    loss, dx, g = _local_step(x[0], loss_target[0], g_src[0], weight, small, pair_sums, pending)

    packed = _pack_row([loss] + [g[k] for k in SMALL])
    s_send, s_recv, s_src, s_land = _gather_start([packed], "small_grads_start", False)
    after = s_src[0]
    for name in ("w_out", "w_attn_o", "w_ssd_o", "w_in"):
        after = update(name, after)

    rows = _own_slot(*_exchange_wait(s_send[0], s_recv[0], s_src[0], s_land[0], after, True, "small_grads_wait"),
                     me, True)
    flat = lambda a: a.reshape(-1, a.shape[-1])
    loss_sum, updates = _small_update(
        rows, me.reshape(1), [g[k].size for k in SMALL],
        [flat(w[k]) for k in SMALL], [flat(m[k]) for k in SMALL], [flat(v[k]) for k in SMALL])
    for k, upd in zip(SMALL, updates):
        res[k] = [u.reshape(w[k].shape) for u in upd]

    grad_x = dx[None]
    outs = [loss_sum.reshape(()), grad_x]
    for i in range(4):
        outs.extend(res[k][i] for k in WEIGHT_ORDER)
    return tuple(outs)
```

```python
import jax
import jax.numpy as jnp
from jax import lax
from jax.experimental import pallas as pl
from jax.experimental.pallas import tpu as pltpu

F32 = jnp.float32
BF16 = jnp.bfloat16
HIGHEST = lax.Precision.HIGHEST

D_MODEL = 1024
N_Q_HEADS = 16
N_KV_HEADS = 4
HEAD_DIM = 64
WINDOW = 128
Q_PER_KV = N_Q_HEADS // N_KV_HEADS
Q_DIM = N_Q_HEADS * HEAD_DIM
KV_DIM = N_KV_HEADS * HEAD_DIM
D_INNER = 2048
SSD_HEAD_DIM = 64
N_SSD_HEADS = 32
N_SSD_GROUPS = 4
HEADS_PER_GROUP = N_SSD_HEADS // N_SSD_GROUPS
D_STATE = 128
GN_ROWS = D_INNER // N_SSD_GROUPS
BC_DIM = N_SSD_GROUPS * D_STATE
XBC_DIM = D_INNER + 2 * BC_DIM
SSD_CONV = 4
CHUNK = 128
D_FF = 2816
FFN_CONV = 3
EPS = 1e-5
NEG = -1e30
IN_DIM = 8736
N_DEV = 8

OFF_Q = 0
OFF_K = OFF_Q + Q_DIM
OFF_V = OFF_K + KV_DIM
OFF_Z = OFF_V + KV_DIM
OFF_X = OFF_Z + D_INNER
OFF_DT = OFF_X + XBC_DIM
OFF_GA = OFF_DT + N_SSD_HEADS
OFF_GS = OFF_GA + D_MODEL

ADAM_LR = 0.001
ADAM_B1 = 0.9
ADAM_B2 = 0.999
ADAM_EPS = 1e-08
ADAM_WD = 0.01
ADAM_STEP = 10

LANES = 128
BF16_TILE_ROWS = 16
VMEM_BYTES = 64 * 1024 * 1024
VMEM_LIMIT = VMEM_BYTES * 3 // 4
MESH = pl.DeviceIdType.MESH


def _cparams(*sem):
    return pltpu.CompilerParams(dimension_semantics=sem, vmem_limit_bytes=VMEM_LIMIT)


def _tile(n, prefs):
    for p in prefs:
        if n % p == 0:
            return p
    return n


def _sigmoid(x):
    return 1.0 / (1.0 + jnp.exp(-x))


def _softplus(x):
    return jnp.maximum(x, 0.0) + jnp.log(1.0 + jnp.exp(-jnp.abs(x)))


def _rowsum(x):
    return jnp.sum(x, axis=1, keepdims=True)


def _colsum(x):
    return jnp.sum(x, axis=0, keepdims=True)


def _dot(a, b):
    return jnp.dot(a, b, preferred_element_type=F32)


def _dot_nt(a, b):
    return lax.dot_general(a, b, (((1,), (1,)), ((), ())), preferred_element_type=F32)


def _dot_tn(a, b):
    return lax.dot_general(a, b, (((0,), (0,)), ((), ())), preferred_element_type=F32)


def _shift_right(x, j):
    if j == 0:
        return x
    r = pltpu.roll(x, j, 1)
    lane = lax.broadcasted_iota(jnp.int32, (x.shape[0], 128), 1)
    return jnp.concatenate([jnp.where(lane >= j, r[:, :128], 0.0), r[:, 128:]], axis=1)


def _shift_left(x, j):
    if j == 0:
        return x
    n = x.shape[1]
    r = pltpu.roll(x, n - j, 1)
    lane = lax.broadcasted_iota(jnp.int32, (x.shape[0], 128), 1)
    return jnp.concatenate([r[:, :n - 128], jnp.where(lane < 128 - j, r[:, n - 128:], 0.0)], axis=1)


def _causal_conv(xv, wv, bv):
    taps = wv.shape[1]
    shifted = [_shift_right(xv, taps - 1 - k) for k in range(taps - 1)]
    y = bv + wv[:, taps - 1:taps] * xv
    for k in range(taps - 1):
        y = y + wv[:, k:k + 1] * shifted[k]
    return y, shifted


def _causal_conv_bwd(dy, xv, shifted, wv):
    taps = wv.shape[1]
    lane = lax.broadcasted_iota(jnp.int32, (dy.shape[0], 128), 1)
    dwb = jnp.where(lane == taps, _rowsum(dy), 0.0)
    dwb = jnp.where(lane == taps - 1, _rowsum(dy * xv), dwb)
    dx = wv[:, taps - 1:taps] * dy
    for k in range(taps - 1):
        dx = dx + wv[:, k:k + 1] * _shift_left(dy, taps - 1 - k)
        dwb = jnp.where(lane == k, _rowsum(dy * shifted[k]), dwb)
    return dx, dwb


def _call(body, *, name, grid, in_specs, out_specs, out_shape, args, semantics, scratch_shapes=(), aliases=None,
          send=None):
    aliases = dict(aliases or {})
    if send is None:
        return pl.pallas_call(body, name=name, grid=grid, in_specs=in_specs, out_specs=out_specs, out_shape=out_shape,
                              scratch_shapes=list(scratch_shapes), input_output_aliases=aliases,
                              compiler_params=_cparams(*semantics))(*args)
    single = not isinstance(out_specs, (list, tuple))
    out_specs, out_shape = ([out_specs], [out_shape]) if single else (list(out_specs), list(out_shape))
    n_in, n_out = len(in_specs), len(out_specs)
    n_copies = N_DEV - 1

    def sending(*refs):
        ins, (src_ref, land_ref) = refs[:n_in], refs[n_in:n_in + 2]
        outs = refs[n_in + 2:n_in + 2 + n_out]
        send_sems, recv_sems = refs[n_in + 2 + n_out:n_in + 4 + n_out]
        scratch = refs[n_in + 6 + n_out:]
        step = 0
        for axis, size in enumerate(grid):
            step = step * size + pl.program_id(axis)

        @pl.when(step == 0)
        def _():
            for k in range(n_copies):
                _peer_copy(False, src_ref, land_ref, send_sems, recv_sems, k, True).start()

        body(*ins, *outs, *scratch)

    sem = pltpu.SemaphoreType.DMA((n_copies,))
    hbm = pltpu.HBM(send.shape, send.dtype)
    res = pl.pallas_call(
        sending, name=name, grid=grid,
        in_specs=list(in_specs) + [HBM, HBM],
        out_specs=out_specs + [SEM, SEM, HBM, HBM],
        out_shape=out_shape + [sem, sem, hbm, hbm],
        input_output_aliases={**aliases, n_in: n_out + 2, n_in + 1: n_out + 3},
        scratch_shapes=list(scratch_shapes),
        compiler_params=pltpu.CompilerParams(dimension_semantics=("arbitrary",) * len(grid), vmem_limit_bytes=VMEM_LIMIT,
                                             has_side_effects=EFFECT),
    )(*args, pltpu.with_memory_space_constraint(send, pltpu.HBM),
      pltpu.with_memory_space_constraint(lax.empty(send.shape, send.dtype), pltpu.HBM))
    return (res[0] if single else list(res[:n_out])), tuple(res[n_out:])


BLOCK_VMEM_BUDGET = VMEM_LIMIT * 3 // 4
MATMUL_MAX_TM = 768
MATMUL_MAX_TN = 3072
MATMUL_MAX_TK = 3072


def _largest_tile(n, align, cap):
    return max(d for d in range(align, min(n, cap) + 1, align) if n % d == 0)


def _matmul_tiles(m, n, k, a_bytes, b_bytes, out_bytes, f32_blocks, m_align, k_align, whole_m):
    tm = m if whole_m else _largest_tile(m, m_align, MATMUL_MAX_TM)
    tk = _largest_tile(k, k_align, MATMUL_MAX_TK)
    for tn in sorted({d for d in range(LANES, min(n, MATMUL_MAX_TN) + 1, LANES) if n % d == 0}, reverse=True):
        need = 2 * (tm * tk * a_bytes + tk * tn * b_bytes) + tm * tn * (2 * out_bytes + (4 if k > tk else 0) + 8 * f32_blocks)
        if need <= BLOCK_VMEM_BUDGET:
            return tm, tn, tk
    return tm, LANES, tk


def _norm_bwd_math(dy, x, w, res):
    r = lax.rsqrt(jnp.mean(x * x, axis=0, keepdims=True) + EPS)
    xhat = x * r
    dxhat = dy * w
    return res + r * (dxhat - xhat * jnp.mean(dxhat * xhat, axis=0, keepdims=True)), _rowsum(dy * xhat)


def _matmul(a, b, *, nt, out_dtype, name, add=None, tn_a=False, send=None, norm_bwd=None, after=None):
    if tn_a:
        k, m = a.shape
    else:
        m, k = a.shape
    n = b.shape[0] if nt else b.shape[1]
    tokens_out = norm_bwd is not None and norm_bwd[3]
    tm, tn, tk = _matmul_tiles(m, n, k, a.dtype.itemsize, b.dtype.itemsize, jnp.dtype(out_dtype).itemsize,
                               (add is not None) + 2 * (norm_bwd is not None),
                               LANES if tn_a or tokens_out else BF16_TILE_ROWS,
                               BF16_TILE_ROWS if tn_a and not nt else LANES, norm_bwd is not None)
    nk = k // tk
    grid = (m // tm, n // tn, nk)
    n_extra = (add is not None) + (after is not None) + 3 * (norm_bwd is not None)
    n_out = 1 + (norm_bwd is not None)

    def body(a_ref, b_ref, *rest):
        extra, outs, scratch = rest[:n_extra], rest[n_extra:n_extra + n_out], rest[n_extra + n_out:]
        av = a_ref[...].astype(BF16)
        bv = b_ref[...].astype(BF16)
        part = _dot_tn(av, bv) if tn_a else _dot_nt(av, bv) if nt else _dot(av, bv)

        def finish(r):
            if add is not None:
                r = r + extra[0][...]
            if norm_bwd is not None:
                x_ref, w_ref, res_ref = extra[-3:]
                dx, dw = _norm_bwd_math(r, x_ref[...], w_ref[...], res_ref[...])
                outs[1][...] += dw
                r = dx.T if tokens_out else dx
            outs[0][...] = r.astype(out_dtype)

        if norm_bwd is not None:
            @pl.when((pl.program_id(1) == 0) & (pl.program_id(2) == 0))
            def _():
                outs[1][...] = jnp.zeros_like(outs[1])

        if nk == 1:
            finish(part)
            return
        acc = scratch[0]
        kk = pl.program_id(2)

        @pl.when(kk == 0)
        def _():
            acc[...] = part

        @pl.when((kk > 0) & (kk < nk - 1))
        def _():
            acc[...] += part

        @pl.when(kk == nk - 1)
        def _():
            finish(acc[...] + part)

    tile = pl.BlockSpec((tm, tn), lambda i, j, kk: (i, j))
    in_specs = [
        pl.BlockSpec((tk, tm), lambda i, j, kk: (kk, i)) if tn_a else pl.BlockSpec((tm, tk), lambda i, j, kk: (i, kk)),
        pl.BlockSpec((tn, tk), lambda i, j, kk: (j, kk)) if nt else pl.BlockSpec((tk, tn), lambda i, j, kk: (kk, j)),
    ]
    args = [a, b]
    out_specs, out_shape = tile, jax.ShapeDtypeStruct((m, n), out_dtype)
    if add is not None:
        in_specs.append(tile)
        args.append(add)
    if after is not None:
        in_specs.append(ANY)
        args.append(after)
    if norm_bwd is not None:
        x, w_col, res, _ = norm_bwd
        col = pl.BlockSpec((m, 1), lambda i, j, kk: (0, 0))
        in_specs += [tile, col, tile]
        args += [x, w_col, res]
        if tokens_out:
            out_specs, out_shape = pl.BlockSpec((tn, tm), lambda i, j, kk: (j, i)), jax.ShapeDtypeStruct((n, m), out_dtype)
        out_specs, out_shape = [out_specs, col], [out_shape, jax.ShapeDtypeStruct((m, 1), F32)]
    return _call(
        body, name=name, grid=grid, in_specs=in_specs, args=args, out_specs=out_specs, out_shape=out_shape,
        scratch_shapes=[pltpu.VMEM((tm, tn), F32)] if nk > 1 else [],
        semantics=("parallel", "parallel" if norm_bwd is None else "arbitrary", "arbitrary"), send=send)


def _norm_fwd(x, w_col, name):
    f, t = x.shape
    tt = _tile(t, (512, 256, 128))

    def body(x_ref, w_ref, o_ref):
        xv = x_ref[...]
        r = lax.rsqrt(jnp.mean(xv * xv, axis=0, keepdims=True) + EPS)
        o_ref[...] = (xv * r * w_ref[...]).astype(BF16)

    return pl.pallas_call(
        body,
        name=name,
        grid=(t // tt,),
        in_specs=[pl.BlockSpec((f, tt), lambda i: (0, i)), pl.BlockSpec((f, 1), lambda i: (0, 0))],
        out_specs=pl.BlockSpec((f, tt), lambda i: (0, i)),
        out_shape=jax.ShapeDtypeStruct((f, t), BF16),
        compiler_params=_cparams("parallel"),
    )(x, w_col)


def _norm_fwd_tokens(x, w_col, after, name):
    t, f = x.shape
    tt = _tile(t, (512, 256, 128))

    def body(x_ref, w_ref, after_ref, xt_ref, o_ref):
        xv = x_ref[...].T
        xt_ref[...] = xv
        r = lax.rsqrt(jnp.mean(xv * xv, axis=0, keepdims=True) + EPS)
        o_ref[...] = (xv * r * w_ref[...]).astype(BF16)

    blk = pl.BlockSpec((f, tt), lambda i: (0, i))
    return pl.pallas_call(
        body,
        name=name,
        grid=(t // tt,),
        in_specs=[pl.BlockSpec((tt, f), lambda i: (i, 0)), pl.BlockSpec((f, 1), lambda i: (0, 0)), ANY],
        out_specs=[blk, blk],
        out_shape=[jax.ShapeDtypeStruct((f, t), F32), jax.ShapeDtypeStruct((f, t), BF16)],
        compiler_params=_cparams("parallel"),
    )(x, w_col, after)


def _final_norm_loss(h, tgt, w_col):
    f, t = h.shape
    tt = _tile(t, (512, 256, 128))

    def body(h_ref, t_ref, w_ref, dh_ref, loss_ref, dw_ref):
        @pl.when(pl.program_id(0) == 0)
        def _():
            dw_ref[...] = jnp.zeros_like(dw_ref)
            loss_ref[...] = jnp.zeros_like(loss_ref)

        xv = h_ref[...]
        r = lax.rsqrt(jnp.mean(xv * xv, axis=0, keepdims=True) + EPS)
        xhat = xv * r
        wv = w_ref[...]
        err = xhat * wv - t_ref[...].T
        loss_ref[...] += 0.5 * _rowsum(jnp.mean(err * err, axis=0, keepdims=True))
        dyv = err * (1.0 / f)
        dw_ref[...] += _rowsum(dyv * xhat)
        dxhat = dyv * wv
        dh_ref[...] = r * (dxhat - xhat * jnp.mean(dxhat * xhat, axis=0, keepdims=True))

    blk = pl.BlockSpec((f, tt), lambda i: (0, i))
    col = pl.BlockSpec((f, 1), lambda i: (0, 0))
    one = pl.BlockSpec((1, 1), lambda i: (0, 0))
    return pl.pallas_call(
        body,
        name="final_norm_loss",
        grid=(t // tt,),
        in_specs=[blk, pl.BlockSpec((tt, f), lambda i: (i, 0)), col],
        out_specs=[blk, one, col],
        out_shape=[jax.ShapeDtypeStruct((f, t), F32), jax.ShapeDtypeStruct((1, 1), F32), jax.ShapeDtypeStruct((f, 1), F32)],
        compiler_params=_cparams("arbitrary"),
    )(h, tgt, w_col)


def _attn_mask(n):
    shape = (2 * WINDOW, Q_PER_KV * WINDOW)
    si = lax.broadcasted_iota(jnp.int32, shape, 0)
    qi = lax.broadcasted_iota(jnp.int32, shape, 1) & (WINDOW - 1)
    dist = WINDOW + qi - si
    return (dist >= 0) & (dist < WINDOW) & ((si >= WINDOW) | (n > 0))


def _lane_cat(ref, row0, rows):
    return jnp.concatenate([ref[row0 + i * rows:row0 + (i + 1) * rows, :] for i in range(Q_PER_KV)], axis=1)


def _attn_fwd(proj, sinks):
    t = proj.shape[1]
    nb = t // WINDOW
    scale = HEAD_DIM ** -0.5

    def body(s_ref, q_ref, kc_ref, kp_ref, vc_ref, vp_ref, o_ref, lse_ref):
        n = pl.program_id(0)
        valid = _attn_mask(n)
        for g in range(N_KV_HEADS):
            rows = slice(g * HEAD_DIM, (g + 1) * HEAD_DIM)
            kt = jnp.concatenate([kp_ref[rows, :], kc_ref[rows, :]], axis=1).astype(BF16)
            vt = jnp.concatenate([vp_ref[rows, :], vc_ref[rows, :]], axis=1).astype(BF16)
            qcat = (_lane_cat(q_ref, g * Q_PER_KV * HEAD_DIM, HEAD_DIM) * scale).astype(BF16)
            s = jnp.where(valid, _dot_tn(kt, qcat), NEG)
            sink = jnp.concatenate(
                [jnp.full((1, WINDOW), s_ref[g * Q_PER_KV + i], F32) for i in range(Q_PER_KV)], axis=1)
            m = jnp.maximum(jnp.max(s, axis=0, keepdims=True), sink)
            p = jnp.exp(s - m)
            denom = _colsum(p) + jnp.exp(sink - m)
            probs = (p / denom).astype(BF16)
            out = _dot(vt, probs)
            lse = m + jnp.log(denom)
            for i in range(Q_PER_KV):
                h = g * Q_PER_KV + i
                o_ref[h * HEAD_DIM:(h + 1) * HEAD_DIM, :] = out[:, i * WINDOW:(i + 1) * WINDOW]
                lse_ref[h:h + 1, :] = lse[:, i * WINDOW:(i + 1) * WINDOW]

    kb = OFF_K // KV_DIM
    vb = OFF_V // KV_DIM
    prev = lambda n: jnp.maximum(n - 1, 0)
    return pl.pallas_call(
        body,
        name="attn_fwd",
        grid=(nb,),
        in_specs=[
            pl.BlockSpec(memory_space=pltpu.SMEM),
            pl.BlockSpec((Q_DIM, WINDOW), lambda n: (0, n)),
            pl.BlockSpec((KV_DIM, WINDOW), lambda n: (kb, n)),
            pl.BlockSpec((KV_DIM, WINDOW), lambda n: (kb, prev(n))),
            pl.BlockSpec((KV_DIM, WINDOW), lambda n: (vb, n)),
            pl.BlockSpec((KV_DIM, WINDOW), lambda n: (vb, prev(n))),
        ],
        out_specs=[pl.BlockSpec((Q_DIM, WINDOW), lambda n: (0, n)), pl.BlockSpec((N_Q_HEADS, WINDOW), lambda n: (0, n))],
        out_shape=[jax.ShapeDtypeStruct((Q_DIM, t), F32), jax.ShapeDtypeStruct((N_Q_HEADS, t), F32)],
        compiler_params=_cparams("parallel"),
    )(sinks, proj, proj, proj, proj, proj)


def _attn_bwd(proj, sinks, out, lse, dout, dproj, send=None):
    t = proj.shape[1]
    nb = t // WINDOW
    scale = HEAD_DIM ** -0.5

    def body(s_ref, q_ref, kc_ref, kp_ref, vc_ref, vp_ref, o_ref, lse_ref, do_ref, dproj_ref,
             dqkv_ref, ds_ref, dk_carry, dv_carry):
        dq_ref = dqkv_ref.at[pl.ds(OFF_Q, Q_DIM)]
        dk_ref = dqkv_ref.at[pl.ds(OFF_K, KV_DIM)]
        dv_ref = dqkv_ref.at[pl.ds(OFF_V, KV_DIM)]
        step = pl.program_id(0)
        n = nb - 1 - step

        @pl.when(step == 0)
        def _():
            dk_carry[...] = jnp.zeros_like(dk_carry)
            dv_carry[...] = jnp.zeros_like(dv_carry)
            ds_ref[...] = jnp.zeros_like(ds_ref)

        valid = _attn_mask(n)
        for g in range(N_KV_HEADS):
            rows = slice(g * HEAD_DIM, (g + 1) * HEAD_DIM)
            q0 = g * Q_PER_KV * HEAD_DIM
            kt = jnp.concatenate([kp_ref[rows, :], kc_ref[rows, :]], axis=1).astype(BF16)
            vt = jnp.concatenate([vp_ref[rows, :], vc_ref[rows, :]], axis=1).astype(BF16)
            qf = _lane_cat(q_ref, q0, HEAD_DIM)
            qcat = qf.astype(BF16)
            ocat = _lane_cat(o_ref, q0, HEAD_DIM)
            docat = _lane_cat(do_ref, q0, HEAD_DIM)
            dob = docat.astype(BF16)
            lse_cat = jnp.concatenate(
                [lse_ref[g * Q_PER_KV + i:g * Q_PER_KV + i + 1, :] for i in range(Q_PER_KV)], axis=1)
            sink = jnp.concatenate(
                [jnp.full((1, WINDOW), s_ref[g * Q_PER_KV + i], F32) for i in range(Q_PER_KV)], axis=1)
            s = jnp.where(valid, _dot_tn(kt, (qf * scale).astype(BF16)), NEG)
            p = jnp.exp(s - lse_cat)
            dp = _dot_tn(vt, dob)
            delta = _colsum(docat * ocat)
            dsc = (p * (dp - delta)).astype(BF16)
            dsink_row = -jnp.exp(sink - lse_cat) * delta
            dq = _dot(kt, dsc) * scale
            dk = _dot_nt(qcat, dsc) * scale
            dv = _dot_nt(dob, p.astype(BF16))
            for i in range(Q_PER_KV):
                h = g * Q_PER_KV + i
                dq_ref[h * HEAD_DIM:(h + 1) * HEAD_DIM, :] = dq[:, i * WINDOW:(i + 1) * WINDOW].astype(BF16)
                ds_ref[h:h + 1, :] += _rowsum(dsink_row[:, i * WINDOW:(i + 1) * WINDOW])
            dk_ref[rows, :] = (dk[:, WINDOW:] + dk_carry[rows, :]).astype(BF16)
            dv_ref[rows, :] = (dv[:, WINDOW:] + dv_carry[rows, :]).astype(BF16)
            dk_carry[rows, :] = dk[:, :WINDOW]
            dv_carry[rows, :] = dv[:, :WINDOW]

    kb = OFF_K // KV_DIM
    vb = OFF_V // KV_DIM
    cur = lambda i: nb - 1 - i
    prev = lambda i: jnp.maximum(nb - 2 - i, 0)
    qspec = pl.BlockSpec((Q_DIM, WINDOW), lambda i: (0, cur(i)))
    return _call(
        body,
        name="attn_bwd",
        grid=(nb,),
        in_specs=[
            pl.BlockSpec(memory_space=pltpu.SMEM),
            qspec,
            pl.BlockSpec((KV_DIM, WINDOW), lambda i: (kb, cur(i))),
            pl.BlockSpec((KV_DIM, WINDOW), lambda i: (kb, prev(i))),
            pl.BlockSpec((KV_DIM, WINDOW), lambda i: (vb, cur(i))),
            pl.BlockSpec((KV_DIM, WINDOW), lambda i: (vb, prev(i))),
            qspec,
            pl.BlockSpec((N_Q_HEADS, WINDOW), lambda i: (0, cur(i))),
            qspec,
            pl.BlockSpec(memory_space=pl.ANY),
        ],
        out_specs=[pl.BlockSpec((OFF_Z, WINDOW), lambda i: (0, cur(i))), pl.BlockSpec((N_Q_HEADS, 1), lambda i: (0, 0))],
        out_shape=[jax.ShapeDtypeStruct(dproj.shape, BF16), jax.ShapeDtypeStruct((N_Q_HEADS, 1), F32)],
        scratch_shapes=[pltpu.VMEM((KV_DIM, WINDOW), F32), pltpu.VMEM((KV_DIM, WINDOW), F32)],
        aliases={9: 0},
        semantics=("arbitrary",), args=(sinks, proj, proj, proj, proj, proj, out, lse, dout, dproj), send=send)


CONV_ROWS = 256


def _conv_silu_fwd(proj, w_col, b_col):
    t = proj.shape[1]
    r0 = OFF_X // CONV_ROWS

    def body(x_ref, w_ref, b_ref, o_ref):
        y, _ = _causal_conv(x_ref[...], w_ref[...], b_ref[...])
        o_ref[...] = y * _sigmoid(y)

    return pl.pallas_call(
        body,
        name="ssd_conv_fwd",
        grid=(XBC_DIM // CONV_ROWS,),
        in_specs=[
            pl.BlockSpec((CONV_ROWS, t), lambda i: (r0 + i, 0)),
            pl.BlockSpec((CONV_ROWS, SSD_CONV), lambda i: (i, 0)),
            pl.BlockSpec((CONV_ROWS, 1), lambda i: (i, 0)),
        ],
        out_specs=pl.BlockSpec((CONV_ROWS, t), lambda i: (i, 0)),
        out_shape=jax.ShapeDtypeStruct((XBC_DIM, t), F32),
        compiler_params=_cparams("parallel"),
    )(proj, w_col, b_col)


def _conv_silu_bwd(proj, w_col, b_col, dout, dproj):
    t = proj.shape[1]
    p0 = OFF_X // CONV_ROWS

    def body(x_ref, w_ref, b_ref, do_ref, dproj_ref, dx_ref, dwb_ref):
        xv = x_ref[...]
        wv = w_ref[...]
        y, shifted = _causal_conv(xv, wv, b_ref[...])
        sg = _sigmoid(y)
        dy = do_ref[...] * (sg * (1.0 + y * (1.0 - sg)))
        dx, dwb_ref[...] = _causal_conv_bwd(dy, xv, shifted, wv)
        dx_ref[...] = dx.astype(BF16)

    return pl.pallas_call(
        body,
        name="ssd_conv_bwd",
        grid=(XBC_DIM // CONV_ROWS,),
        in_specs=[
            pl.BlockSpec((CONV_ROWS, t), lambda i: (p0 + i, 0)),
            pl.BlockSpec((CONV_ROWS, SSD_CONV), lambda i: (i, 0)),
            pl.BlockSpec((CONV_ROWS, 1), lambda i: (i, 0)),
            pl.BlockSpec((CONV_ROWS, t), lambda i: (i, 0)),
            pl.BlockSpec(memory_space=pl.ANY),
        ],
        out_specs=[pl.BlockSpec((CONV_ROWS, t), lambda i: (p0 + i, 0)), pl.BlockSpec((CONV_ROWS, 128), lambda i: (i, 0))],
        out_shape=[jax.ShapeDtypeStruct(dproj.shape, BF16), jax.ShapeDtypeStruct((XBC_DIM, 128), F32)],
        input_output_aliases={4: 0},
        compiler_params=_cparams("parallel"),
    )(proj, w_col, b_col, dout, dproj)


def _ssd_specs(order):
    xb = D_INNER // BC_DIM
    dtb = OFF_DT // N_SSD_HEADS
    col = pl.BlockSpec((N_SSD_HEADS, 1), lambda c: (0, 0))
    return [
        pl.BlockSpec((D_INNER, CHUNK), lambda c: (0, order(c))),
        pl.BlockSpec((BC_DIM, CHUNK), lambda c: (xb, order(c))),
        pl.BlockSpec((BC_DIM, CHUNK), lambda c: (xb + 1, order(c))),
        pl.BlockSpec((N_SSD_HEADS, CHUNK), lambda c: (dtb, order(c))),
        col, col, col,
    ]


def _ssd_common(dt_ref, dtb_ref, alog_ref):
    z = dt_ref[...] + dtb_ref[...]
    dt = _softplus(z)
    a_neg = -jnp.exp(alog_ref[...])
    d_a = dt * a_neg
    row = lax.broadcasted_iota(jnp.int32, (CHUNK, CHUNK), 0)
    colm = lax.broadcasted_iota(jnp.int32, (CHUNK, CHUNK), 1)
    upper = (row <= colm).astype(F32)
    a_cs = jnp.dot(d_a, upper, precision=HIGHEST, preferred_element_type=F32)
    a_last = _rowsum(d_a)
    return z, dt, a_neg, a_cs, a_last, row >= colm, row == colm


def _decay(a_row, causal):
    a_s = jnp.broadcast_to(a_row, (CHUNK, CHUNK))
    seg = a_s.T - a_s
    return jnp.where(causal, jnp.exp(jnp.where(causal, seg, 0.0)), 0.0)


def _ssd_fwd(xbc, proj, dtb_col, alog_col, dsk_col, gnw_col):
    t = xbc.shape[1]
    nc = t // CHUNK

    def body(xs_ref, b_ref, c_ref, dt_ref, dtb_ref, alog_ref, dsk_ref, *rest):
        z_refs, (gnw_ref, y_ref, hst_ref, yn_ref, h_scr) = rest[:N_SSD_GROUPS], rest[N_SSD_GROUPS:]

        @pl.when(pl.program_id(0) == 0)
        def _():
            h_scr[...] = jnp.zeros_like(h_scr)

        _, dt, _, a_cs, a_last, causal, _ = _ssd_common(dt_ref, dtb_ref, alog_ref)
        hst_ref[0] = h_scr[...]
        dsk = dsk_ref[...]
        for g in range(N_SSD_GROUPS):
            grows = slice(g * D_STATE, (g + 1) * D_STATE)
            bb = b_ref[grows, :].astype(BF16)
            cb_ = c_ref[grows, :].astype(BF16)
            cb = _dot_tn(cb_, bb)
            for j in range(g * HEADS_PER_GROUP, (g + 1) * HEADS_PER_GROUP):
                rows = slice(j * SSD_HEAD_DIM, (j + 1) * SSD_HEAD_DIM)
                a = a_cs[j:j + 1, :]
                m = (cb * _decay(a, causal)).astype(BF16)
                xs = xs_ref[rows, :]
                xc = xs * dt[j:j + 1, :]
                hj = h_scr[rows, :]
                y = _dot_nt(xc.astype(BF16), m) + _dot(hj.astype(BF16), cb_) * jnp.exp(a) + dsk[j:j + 1, :] * xs
                y_ref[rows, :] = y
                al = a_last[j:j + 1, :]
                w = jnp.exp(al - a)
                h_scr[rows, :] = jnp.exp(al) * hj + _dot_nt((xc * w).astype(BF16), bb)
        for g in range(N_SSD_GROUPS):
            rows = slice(g * GN_ROWS, (g + 1) * GN_ROWS)
            zv = z_refs[g][...]
            u = y_ref[rows, :] * (zv * _sigmoid(zv))
            r = lax.rsqrt(jnp.mean(u * u, axis=0, keepdims=True) + EPS)
            yn_ref[rows, :] = (u * r * gnw_ref[rows, :]).astype(BF16)

    z0 = OFF_Z // GN_ROWS
    z_specs = [pl.BlockSpec((GN_ROWS, CHUNK), lambda c, g=g: (z0 + g, c)) for g in range(N_SSD_GROUPS)]
    rows_spec = pl.BlockSpec((D_INNER, CHUNK), lambda c: (0, c))
    return pl.pallas_call(
        body,
        name="ssd_fwd",
        grid=(nc,),
        in_specs=_ssd_specs(lambda c: c) + z_specs + [pl.BlockSpec((D_INNER, 1), lambda c: (0, 0))],
        out_specs=[rows_spec, pl.BlockSpec((1, D_INNER, D_STATE), lambda c: (c, 0, 0)), rows_spec],
        out_shape=[
            jax.ShapeDtypeStruct((D_INNER, t), F32),
            jax.ShapeDtypeStruct((nc, D_INNER, D_STATE), F32),
            jax.ShapeDtypeStruct((D_INNER, t), BF16),
        ],
        scratch_shapes=[pltpu.VMEM((D_INNER, D_STATE), F32)],
        compiler_params=_cparams("arbitrary"),
    )(xbc, xbc, xbc, proj, dtb_col, alog_col, dsk_col, *([proj] * N_SSD_GROUPS), gnw_col)


def _ssd_bwd(xbc, proj, dtb_col, alog_col, dsk_col, hst, y, dyn, gnw_col, send=None):
    t = xbc.shape[1]
    nc = t // CHUNK
    rev = lambda c: nc - 1 - c

    def body(xs_ref, b_ref, c_ref, dt_ref, dtb_ref, alog_ref, dsk_ref, hst_ref, y_ref, dyn_ref, *rest):
        z_refs, rest = rest[:N_SSD_GROUPS], rest[N_SSD_GROUPS:]
        (gnw_ref, dxbc_ref, ddt_ref, dalog_ref, ddsk_ref, ddtb_ref, dz_ref, dgnw_ref,
         dh_scr, da_scr, ddt_scr, dd_scr, dy_ref) = rest
        dxs_ref = dxbc_ref.at[pl.ds(0, D_INNER)]
        db_ref = dxbc_ref.at[pl.ds(D_INNER, BC_DIM)]
        dc_ref = dxbc_ref.at[pl.ds(D_INNER + BC_DIM, BC_DIM)]

        @pl.when(pl.program_id(0) == 0)
        def _():
            dh_scr[...] = jnp.zeros_like(dh_scr)
            dalog_ref[...] = jnp.zeros_like(dalog_ref)
            ddsk_ref[...] = jnp.zeros_like(ddsk_ref)
            ddtb_ref[...] = jnp.zeros_like(ddtb_ref)
            dgnw_ref[...] = jnp.zeros_like(dgnw_ref)

        for g in range(N_SSD_GROUPS):
            rows = slice(g * GN_ROWS, (g + 1) * GN_ROWS)
            zv = z_refs[g][...]
            yv = y_ref[rows, :]
            sg = _sigmoid(zv)
            sz = zv * sg
            u = yv * sz
            r = lax.rsqrt(jnp.mean(u * u, axis=0, keepdims=True) + EPS)
            xhat = u * r
            dov = dyn_ref[rows, :]
            dgnw_ref[rows, :] += _rowsum(dov * xhat)
            dxhat = dov * gnw_ref[rows, :]
            du = r * (dxhat - xhat * jnp.mean(dxhat * xhat, axis=0, keepdims=True))
            dy_ref[rows, :] = du * sz
            dz_ref[rows, :] = (du * yv * (sg * (1.0 + zv * (1.0 - sg)))).astype(BF16)

        z, dt, a_neg, a_cs, a_last, causal, eye = _ssd_common(dt_ref, dtb_ref, alog_ref)
        dsk = dsk_ref[...]
        last_lane = lax.broadcasted_iota(jnp.int32, (1, CHUNK), 1) == CHUNK - 1
        for g in range(N_SSD_GROUPS):
            grows = slice(g * D_STATE, (g + 1) * D_STATE)
            bb = b_ref[grows, :].astype(BF16)
            cb_ = c_ref[grows, :].astype(BF16)
            cb = _dot_tn(cb_, bb)
            dcb = jnp.zeros((CHUNK, CHUNK), F32)
            dc_acc = jnp.zeros((D_STATE, CHUNK), F32)
            db_acc = jnp.zeros((D_STATE, CHUNK), F32)
            for j in range(g * HEADS_PER_GROUP, (g + 1) * HEADS_PER_GROUP):
                rows = slice(j * SSD_HEAD_DIM, (j + 1) * SSD_HEAD_DIM)
                a = a_cs[j:j + 1, :]
                al = a_last[j:j + 1, :]
                lam = _decay(a, causal)
                mf = cb * lam
                xs = xs_ref[rows, :]
                dtj = dt[j:j + 1, :]
                xc = xs * dtj
                w = jnp.exp(al - a)
                e = jnp.exp(a)
                gam = jnp.exp(al)
                hj = hst_ref[0, rows, :]
                hjb = hj.astype(BF16)
                dyv = dy_ref[rows, :]
                dyb = dyv.astype(BF16)
                dd_scr[j:j + 1, :] = _colsum(dyv * xs)
                gb = (dyv * e).astype(BF16)
                dh_in = _dot_nt(gb, cb_)
                dc_acc = dc_acc + _dot_tn(hjb, gb)
                yoff = _dot(hjb, cb_) * e
                da = _colsum(dyv * yoff)
                dm = _dot_tn(dyb, xc.astype(BF16))
                dxc = _dot(dyb, mf.astype(BF16))
                dcb = dcb + dm * lam
                nmat = dm * mf
                rs = jnp.broadcast_to(_rowsum(nmat), (CHUNK, CHUNK))
                da = da + _colsum(jnp.where(eye, rs, 0.0)) - _colsum(nmat)
                ds = dh_scr[rows, :]
                dsb = ds.astype(BF16)
                t1 = _dot(dsb, bb)
                xcw = xc * w
                dxc = dxc + w * t1
                dww = _colsum(xcw * t1)
                da_l = _rowsum(dww) + _rowsum(_colsum(ds * hj)) * gam
                da = da - dww + jnp.where(last_lane, da_l, 0.0)
                db_acc = db_acc + _dot_tn(dsb, xcw.astype(BF16))
                dh_scr[rows, :] = gam * ds + dh_in
                dxs_ref[rows, :] = dsk[j:j + 1, :] * dyv + dxc * dtj
                da_scr[j:j + 1, :] = da
                ddt_scr[j:j + 1, :] = _colsum(dxc * xs)
            dcbb = dcb.astype(BF16)
            dc_ref[grows, :] = dc_acc + _dot_nt(bb, dcbb)
            db_ref[grows, :] = db_acc + _dot(cb_, dcbb)
        dda = jnp.dot(da_scr[...], causal.astype(F32), precision=HIGHEST, preferred_element_type=F32)
        ddt = ddt_scr[...] + dda * a_neg
        ddt_raw = ddt * _sigmoid(z)
        ddt_ref[...] = ddt_raw
        ddtb_ref[...] += _rowsum(ddt_raw)
        dalog_ref[...] += _rowsum(dda * dt) * a_neg
        ddsk_ref[...] += _rowsum(dd_scr[...])

    col = pl.BlockSpec((N_SSD_HEADS, 1), lambda c: (0, 0))
    xs_spec = pl.BlockSpec((D_INNER, CHUNK), lambda c: (0, rev(c)))
    gn_col = pl.BlockSpec((D_INNER, 1), lambda c: (0, 0))
    z0 = OFF_Z // GN_ROWS
    z_specs = [pl.BlockSpec((GN_ROWS, CHUNK), lambda c, g=g: (z0 + g, rev(c))) for g in range(N_SSD_GROUPS)]
    dz_spec = pl.BlockSpec((pl.Element(D_INNER), pl.Element(CHUNK)),
                           lambda c: (OFF_Z, pl.multiple_of(CHUNK * rev(c), CHUNK)))
    small = pltpu.VMEM((N_SSD_HEADS, CHUNK), F32)
    return _call(
        body,
        name="ssd_bwd",
        grid=(nc,),
        in_specs=_ssd_specs(rev) + [pl.BlockSpec((1, D_INNER, D_STATE), lambda c: (rev(c), 0, 0)), xs_spec, xs_spec]
        + z_specs + [gn_col],
        out_specs=[pl.BlockSpec((XBC_DIM, CHUNK), lambda c: (0, rev(c))),
                   pl.BlockSpec((N_SSD_HEADS, CHUNK), lambda c: (0, rev(c))), col, col, col, dz_spec, gn_col],
        out_shape=[
            jax.ShapeDtypeStruct((XBC_DIM, t), F32),
            jax.ShapeDtypeStruct((N_SSD_HEADS, t), F32),
            jax.ShapeDtypeStruct((N_SSD_HEADS, 1), F32),
            jax.ShapeDtypeStruct((N_SSD_HEADS, 1), F32),
            jax.ShapeDtypeStruct((N_SSD_HEADS, 1), F32),
            jax.ShapeDtypeStruct((IN_DIM, t), BF16),
            jax.ShapeDtypeStruct((D_INNER, 1), F32),
        ],
        scratch_shapes=[pltpu.VMEM((D_INNER, D_STATE), F32), small, small, small, pltpu.VMEM((D_INNER, CHUNK), F32)],
        semantics=("arbitrary",),
        args=(xbc, xbc, xbc, proj, dtb_col, alog_col, dsk_col, hst, y, dyn, *([proj] * N_SSD_GROUPS), gnw_col),
        send=send)


GATE_ROWS = 128


def _gate_specs(t):
    nr = D_MODEL // GATE_ROWS
    blk = pl.BlockSpec((GATE_ROWS, t), lambda r: (r, 0))
    rows_from = lambda first: pl.BlockSpec(
        (pl.Element(GATE_ROWS), pl.Element(t)), lambda r: (pl.multiple_of(first + GATE_ROWS * r, N_SSD_HEADS), 0))
    return blk, [
        rows_from(OFF_GA),
        rows_from(OFF_GS),
        pl.BlockSpec((GATE_ROWS, 1), lambda r: (r, 0)),
        pl.BlockSpec((GATE_ROWS, 1), lambda r: (nr + r, 0)),
        blk, blk,
    ]


def _gate_fwd(proj, b_col, attn, ssd):
    t = proj.shape[1]
    blk, specs = _gate_specs(t)

    def body(ga_ref, gs_ref, ba_ref, bs_ref, a_ref, s_ref, o_ref):
        o_ref[...] = (_sigmoid(ga_ref[...] + ba_ref[...]) * a_ref[...]
                      + _sigmoid(gs_ref[...] + bs_ref[...]) * s_ref[...]).astype(BF16)

    return pl.pallas_call(
        body,
        name="gate_fwd",
        grid=(D_MODEL // GATE_ROWS,),
        in_specs=specs,
        out_specs=blk,
        out_shape=jax.ShapeDtypeStruct((D_MODEL, t), BF16),
        compiler_params=_cparams("parallel"),
    )(proj, proj, b_col, b_col, attn, ssd)


def _gate_bwd(proj, b_col, attn, ssd, dmix, send=None):
    t = proj.shape[1]
    blk, specs = _gate_specs(t)

    def body(ga_ref, gs_ref, ba_ref, bs_ref, a_ref, s_ref, dm_ref, da_ref, dso_ref, dga_ref, dgs_ref, dba_ref, dbs_ref):
        dm = dm_ref[...]
        sa = _sigmoid(ga_ref[...] + ba_ref[...])
        ss = _sigmoid(gs_ref[...] + bs_ref[...])
        da_ref[...] = (dm * sa).astype(BF16)
        dso_ref[...] = (dm * ss).astype(BF16)
        dga = dm * a_ref[...] * sa * (1.0 - sa)
        dgs = dm * s_ref[...] * ss * (1.0 - ss)
        dga_ref[...] = dga.astype(BF16)
        dgs_ref[...] = dgs.astype(BF16)
        dba_ref[...] = _rowsum(dga)
        dbs_ref[...] = _rowsum(dgs)

    col = pl.BlockSpec((GATE_ROWS, 1), lambda r: (r, 0))
    act = jax.ShapeDtypeStruct((D_MODEL, t), BF16)
    bias = jax.ShapeDtypeStruct((D_MODEL, 1), F32)
    return _call(
        body,
        name="gate_bwd",
        grid=(D_MODEL // GATE_ROWS,),
        in_specs=specs + [blk],
        out_specs=[blk, blk, blk, blk, col, col],
        out_shape=[act, act, act, act, bias, bias],
        semantics=("parallel",), args=(proj, proj, b_col, b_col, attn, ssd, dmix), send=send)


FFN_ROWS = 256


def _ffn_fwd(u0, w_col, b_col):
    t = u0.shape[2]

    def body(u_ref, w_ref, b_ref, o_ref):
        val, _ = _causal_conv(u_ref[0], w_ref[0], b_ref[0])
        gt, _ = _causal_conv(u_ref[1], w_ref[1], b_ref[1])
        o_ref[...] = (gt * _sigmoid(gt) * val).astype(BF16)

    return pl.pallas_call(
        body,
        name="ffn_fwd",
        grid=(D_FF // FFN_ROWS,),
        in_specs=[
            pl.BlockSpec((2, FFN_ROWS, t), lambda i: (0, i, 0)),
            pl.BlockSpec((2, FFN_ROWS, FFN_CONV), lambda i: (0, i, 0)),
            pl.BlockSpec((2, FFN_ROWS, 1), lambda i: (0, i, 0)),
        ],
        out_specs=pl.BlockSpec((FFN_ROWS, t), lambda i: (i, 0)),
        out_shape=jax.ShapeDtypeStruct((D_FF, t), BF16),
        compiler_params=_cparams("parallel"),
    )(u0, w_col, b_col)


def _ffn_bwd(u0, w_col, b_col, dg, send=None):
    t = u0.shape[2]

    def body(u_ref, w_ref, b_ref, dg_ref, du_ref, dwb_ref):
        xval, wval = u_ref[0], w_ref[0]
        xgt, wgt = u_ref[1], w_ref[1]
        val, sh_val = _causal_conv(xval, wval, b_ref[0])
        gt, sh_gt = _causal_conv(xgt, wgt, b_ref[1])
        sg = _sigmoid(gt)
        dgv = dg_ref[...]
        dval = dgv * (gt * sg)
        dgt = dgv * val * (sg * (1.0 + gt * (1.0 - sg)))
        dx, dwb_ref[0] = _causal_conv_bwd(dval, xval, sh_val, wval)
        du_ref[0] = dx.astype(BF16)
        dx, dwb_ref[1] = _causal_conv_bwd(dgt, xgt, sh_gt, wgt)
        du_ref[1] = dx.astype(BF16)

    return _call(
        body,
        name="ffn_bwd",
        grid=(D_FF // FFN_ROWS,),
        in_specs=[
            pl.BlockSpec((2, FFN_ROWS, t), lambda i: (0, i, 0)),
            pl.BlockSpec((2, FFN_ROWS, FFN_CONV), lambda i: (0, i, 0)),
            pl.BlockSpec((2, FFN_ROWS, 1), lambda i: (0, i, 0)),
            pl.BlockSpec((FFN_ROWS, t), lambda i: (i, 0)),
        ],
        out_specs=[pl.BlockSpec((2, FFN_ROWS, t), lambda i: (0, i, 0)), pl.BlockSpec((2, FFN_ROWS, 128), lambda i: (0, i, 0))],
        out_shape=[jax.ShapeDtypeStruct((2, D_FF, t), BF16), jax.ShapeDtypeStruct((2, D_FF, 128), F32)],
        semantics=("parallel",), args=(u0, w_col, b_col, dg), send=send)


def _adamw_math(w, g, m, v):
    m = ADAM_B1 * m + (1.0 - ADAM_B1) * g
    v = ADAM_B2 * v + (1.0 - ADAM_B2) * (g * g)
    m_hat = m / (1.0 - ADAM_B1 ** ADAM_STEP)
    v_hat = v / (1.0 - ADAM_B2 ** ADAM_STEP)
    delta = -ADAM_LR * (m_hat / (jnp.sqrt(v_hat) + ADAM_EPS) + ADAM_WD * w)
    return delta, m, v


def _adamw_sharded(parts, w, m, v, name):
    r, c = w.shape[0], w.shape[-1]
    slots = parts.shape[0]
    per_lane = 2 * r * (slots * parts.dtype.itemsize + 7 * w.dtype.itemsize)
    tc = max(d for d in range(LANES, c + 1, LANES) if c % d == 0 and (d * per_lane <= BLOCK_VMEM_BUDGET or d == LANES))
    blk_shape = (r, tc) if w.ndim == 2 else (r, 1, tc)

    def body(p_ref, w_ref, m_ref, v_ref, g_ref, d_ref, nm_ref, nv_ref):
        g = p_ref[0].astype(F32)
        for s in range(1, slots):
            g = g + p_ref[s].astype(F32)
        flat = lambda ref: ref[...].reshape(r, tc)
        d, nm, nv = _adamw_math(flat(w_ref), g, flat(m_ref), flat(v_ref))
        for ref, val in ((g_ref, g), (d_ref, d), (nm_ref, nm), (nv_ref, nv)):
            ref[...] = val.reshape(blk_shape)

    blk = pl.BlockSpec(blk_shape, (lambda i: (0, i)) if w.ndim == 2 else (lambda i: (0, 0, i)))
    out = jax.ShapeDtypeStruct(w.shape, F32)
    return pl.pallas_call(
        body,
        name=name,
        grid=(c // tc,),
        in_specs=[pl.BlockSpec((slots, r, tc), lambda i: (0, 0, i)), blk, blk, blk],
        out_specs=[blk, blk, blk, blk],
        out_shape=[out, out, out, out],
        compiler_params=_cparams("parallel"),
    )(parts, w, m, v)


def _lane_offsets(sizes):
    offsets, pos = [], 0
    for n in sizes:
        offsets.append(pos)
        pos += -(-n // 128) * 128
    return offsets, pos


def _pack_row(parts):
    rows = [p.reshape(1, -1).astype(F32) for p in parts]
    return jnp.concatenate([jnp.pad(r, ((0, 0), (0, -r.shape[1] % 128))) for r in rows], axis=1)


def _small_update(parts, me, full_sizes, ws, ms, vs):
    n = len(ws)
    offsets, _ = _lane_offsets([1] + list(full_sizes))

    def body(me_ref, p_ref, *refs):
        w_refs, m_refs, v_refs = refs[:n], refs[n:2 * n], refs[2 * n:3 * n]
        scalar_ref, out_refs = refs[3 * n], refs[3 * n + 1:]
        tot = p_ref[0]
        for s in range(1, N_DEV):
            tot = tot + p_ref[s]
        scalar_ref[...] = tot[:, 0:1]
        for k in range(n):
            g_ref, d_ref, nm_ref, nv_ref = out_refs[4 * k:4 * k + 4]
            taps, cols = w_refs[k].shape
            if taps == 1:
                g_ref[...] = tot[:, offsets[k + 1]:offsets[k + 1] + cols]
            else:
                full = full_sizes[k] // taps
                for tap in range(taps):
                    mine = jnp.zeros((1, cols), F32)
                    for d in range(N_DEV):
                        lo = offsets[k + 1] + tap * full + d * cols
                        mine = jnp.where(me_ref[0] == d, tot[:, lo:lo + cols], mine)
                    g_ref[tap:tap + 1, :] = mine
            d_ref[...], nm_ref[...], nv_ref[...] = _adamw_math(w_refs[k][...], g_ref[...], m_refs[k][...], v_refs[k][...])

    vmem = pl.BlockSpec(memory_space=pltpu.VMEM)
    out_shape = [jax.ShapeDtypeStruct((1, 1), F32)]
    for wk in ws:
        out_shape += [jax.ShapeDtypeStruct(wk.shape, F32)] * 4
    res = pl.pallas_call(
        body,
        name="small_update",
        in_specs=[pl.BlockSpec(memory_space=pltpu.SMEM)] + [vmem] * (1 + 3 * n),
        out_specs=[vmem] * len(out_shape),
        out_shape=out_shape,
    )(me, parts, *ws, *ms, *vs)
    return res[0], [res[1 + 4 * k:5 + 4 * k] for k in range(n)]


ANY = pl.BlockSpec(memory_space=pl.ANY)
FLIPS = [(k >> 2 & 1, k >> 1 & 1, k & 1) for k in range(1, N_DEV)]


def _place():
    return lax.axis_index("x"), lax.axis_index("y"), lax.axis_index("c")


HBM = pl.BlockSpec(memory_space=pltpu.HBM)
SEM = pl.BlockSpec(memory_space=pltpu.SEMAPHORE)
EFFECT = pltpu.SideEffectType.DATAFLOW_SIDE_EFFECTING


def _peer_copy(gather, src_ref, land_ref, send_sems, recv_sems, k, sending):
    x, y, c = _place()
    fx, fy, fc = FLIPS[k]
    me = 4 * x + 2 * y + c
    peer = 4 * (x ^ fx) + 2 * (y ^ fy) + (c ^ fc)
    return pltpu.make_async_remote_copy(
        src_ref=src_ref if gather else src_ref.at[peer],
        dst_ref=land_ref.at[me if sending else peer],
        send_sem=send_sems.at[k], recv_sem=recv_sems.at[k],
        device_id=(x ^ fx, y ^ fy, c ^ fc), device_id_type=MESH)


SIBLING = 0
OTHER_CHIPS = (1, 3, 5)


def _gather_start(srcs, name, via_sibling):
    n = len(srcs)
    lands = [lax.empty((N_DEV,) + s.shape, s.dtype) for s in srcs]

    def body(*refs):
        src_refs, land_refs = refs[:n], refs[n:2 * n]
        send, recv = refs[2 * n:3 * n], refs[3 * n:4 * n]
        for i in range(n):
            for k in (SIBLING,) + OTHER_CHIPS if via_sibling else range(N_DEV - 1):
                _peer_copy(True, src_refs[i], land_refs[i], send[i], recv[i], k, True).start()

    sem = pltpu.SemaphoreType.DMA((N_DEV - 1,))
    hbm = lambda a: pltpu.HBM(a.shape, a.dtype)
    res = pl.pallas_call(
        body,
        name=name,
        in_specs=[HBM] * (2 * n),
        out_specs=[SEM] * (2 * n) + [HBM] * (2 * n),
        out_shape=[sem] * (2 * n) + [hbm(s) for s in srcs] + [hbm(a) for a in lands],
        input_output_aliases={i: 2 * n + i for i in range(2 * n)},
        compiler_params=pltpu.CompilerParams(has_side_effects=EFFECT),
    )(*[pltpu.with_memory_space_constraint(a, pltpu.HBM) for a in list(srcs) + lands])
    return res[:n], res[n:2 * n], res[2 * n:3 * n], res[3 * n:4 * n]


def _exchange_wait(send_sems, recv_sems, src, land, after, gather, name):
    def body(src_ref, land_ref, send_ref, recv_ref, after_ref, src_out, land_out):
        for k in range(N_DEV - 1):
            cp = _peer_copy(gather, src_ref, land_ref, send_ref, recv_ref, k, False)
            cp.wait_send()
            cp.wait_recv()

    hbm = lambda a: pltpu.HBM(a.shape, a.dtype)
    return pl.pallas_call(
        body,
        name=name,
        in_specs=[HBM, HBM, SEM, SEM, ANY],
        out_specs=[HBM, HBM],
        out_shape=[hbm(src), hbm(land)],
        input_output_aliases={0: 0, 1: 1},
        compiler_params=pltpu.CompilerParams(has_side_effects=EFFECT),
    )(src, land, send_sems, recv_sems, after)


def _own_slot(src, land, me, gather):
    own = src[None] if gather else lax.dynamic_slice_in_dim(src, me, 1, axis=0)
    return lax.dynamic_update_slice_in_dim(land, own, me, axis=0)


def _forwarded_copy(land_ref, send_sems, recv_sems, j, sending):
    x, y, c = _place()
    fx, fy, _ = FLIPS[OTHER_CHIPS[j]]
    slot = 4 * (x ^ fx) + 2 * (y ^ fy) + (c if sending else 1 - c)
    return pltpu.make_async_remote_copy(
        src_ref=land_ref.at[slot], dst_ref=land_ref.at[slot], send_sem=send_sems.at[j], recv_sem=recv_sems.at[j],
        device_id=(x, y, 1 - c), device_id_type=MESH)


def _gather_forward(send_sems, recv_sems, srcs, lands, after, name):
    n = len(srcs)

    def body(*refs):
        src_refs, land_refs = refs[:n], refs[n:2 * n]
        send, recv = refs[2 * n:3 * n], refs[3 * n:4 * n]
        fwd_send, fwd_recv = refs[4 * n + 1:5 * n + 1], refs[5 * n + 1:6 * n + 1]
        for i in range(n):
            for j, k in enumerate(OTHER_CHIPS):
                _peer_copy(True, src_refs[i], land_refs[i], send[i], recv[i], k, False).wait_recv()
                _forwarded_copy(land_refs[i], fwd_send[i], fwd_recv[i], j, True).start()

    sem = pltpu.SemaphoreType.DMA((len(OTHER_CHIPS),))
    hbm = lambda a: pltpu.HBM(a.shape, a.dtype)
    res = pl.pallas_call(
        body,
        name=name,
        in_specs=[HBM] * (2 * n) + [SEM] * (2 * n) + [ANY],
        out_specs=[SEM] * (2 * n) + [HBM] * (2 * n),
        out_shape=[sem] * (2 * n) + [hbm(a) for a in srcs] + [hbm(a) for a in lands],
        input_output_aliases={i: 2 * n + i for i in range(2 * n)},
        compiler_params=pltpu.CompilerParams(has_side_effects=EFFECT),
    )(*srcs, *lands, *send_sems, *recv_sems, after)
    return res[:n], res[n:2 * n], res[2 * n:3 * n], res[3 * n:4 * n]


def _gather_wait_forwarded(send_sems, recv_sems, fwd_send, fwd_recv, src, land, after, name):
    def body(src_ref, land_ref, send_ref, recv_ref, fwd_send_ref, fwd_recv_ref, after_ref, src_out, land_out):
        for k in (SIBLING,) + OTHER_CHIPS:
            _peer_copy(True, src_ref, land_ref, send_ref, recv_ref, k, False).wait_send()
        _peer_copy(True, src_ref, land_ref, send_ref, recv_ref, SIBLING, False).wait_recv()
        for j in range(len(OTHER_CHIPS)):
            _forwarded_copy(land_ref, fwd_send_ref, fwd_recv_ref, j, True).wait_send()
            _forwarded_copy(land_ref, fwd_send_ref, fwd_recv_ref, j, False).wait_recv()

    hbm = lambda a: pltpu.HBM(a.shape, a.dtype)
    return pl.pallas_call(
        body,
        name=name,
        in_specs=[HBM, HBM, SEM, SEM, SEM, SEM, ANY],
        out_specs=[HBM, HBM],
        out_shape=[hbm(src), hbm(land)],
        input_output_aliases={0: 0, 1: 1},
        compiler_params=pltpu.CompilerParams(has_side_effects=EFFECT),
    )(src, land, send_sems, recv_sems, fwd_send, fwd_recv, after)


N_CHIPS = N_DEV // 2


def _pair_exchange(by_core, meanwhile, name):
    def copy(src_ref, land_ref, send_sems, recv_sems, q):
        x, y, c = _place()
        return pltpu.make_async_remote_copy(
            src_ref=src_ref.at[q, 1 - c], dst_ref=land_ref.at[q], send_sem=send_sems.at[q], recv_sem=recv_sems.at[q],
            device_id=(x, y, 1 - c), device_id_type=MESH)

    def start(src_ref, land_ref, send_sems, recv_sems, src_out, land_out):
        for q in range(N_CHIPS):
            copy(src_ref, land_ref, send_sems, recv_sems, q).start()

    def wait(src_ref, land_ref, send_sems, recv_sems, after_ref, src_out, land_out):
        for q in range(N_CHIPS):
            cp = copy(src_ref, land_ref, send_sems, recv_sems, q)
            cp.wait_send()
            cp.wait_recv()

    sem = pltpu.SemaphoreType.DMA((N_CHIPS,))
    hbm_src = pltpu.HBM(by_core.shape, by_core.dtype)
    hbm_land = pltpu.HBM(by_core.shape[:1] + by_core.shape[2:], by_core.dtype)
    params = pltpu.CompilerParams(has_side_effects=EFFECT)
    send_sems, recv_sems, src, land = pl.pallas_call(
        start, name=name + "_start", in_specs=[HBM, HBM], out_specs=[SEM, SEM, HBM, HBM],
        out_shape=[sem, sem, hbm_src, hbm_land], input_output_aliases={0: 2, 1: 3}, compiler_params=params,
    )(pltpu.with_memory_space_constraint(by_core, pltpu.HBM),
      pltpu.with_memory_space_constraint(lax.empty(hbm_land.shape, by_core.dtype), pltpu.HBM))
    return pl.pallas_call(
        wait, name=name + "_wait", in_specs=[HBM, HBM, SEM, SEM, ANY], out_specs=[HBM, HBM],
        out_shape=[hbm_src, hbm_land], input_output_aliases={0: 0, 1: 1}, compiler_params=params,
    )(src, land, send_sems, recv_sems, meanwhile(src))


def _pair_add(by_core, landed, name):
    q, _, r, c = by_core.shape
    tc = _tile(c, (512, 256, 128))

    def body(a_ref, b_ref, o_ref):
        mine = a_ref[0, lax.axis_index("c")]
        o_ref[0] = (mine.astype(F32) + b_ref[0].astype(F32)).astype(BF16)

    blk = pl.BlockSpec((1, r, tc), lambda i, j: (i, 0, j))
    return pl.pallas_call(
        body, name=name, grid=(q, c // tc),
        in_specs=[pl.BlockSpec((1, 2, r, tc), lambda i, j: (i, 0, 0, j)), blk], out_specs=blk,
        out_shape=jax.ShapeDtypeStruct(landed.shape, BF16), compiler_params=_cparams("parallel", "parallel"),
    )(by_core, landed)


def _chip_copy(src_ref, land_ref, send_sems, recv_sems, j, sending):
    x, y, c = _place()
    fx, fy, _ = FLIPS[OTHER_CHIPS[j]]
    here, there = 2 * x + y, 2 * (x ^ fx) + (y ^ fy)
    return pltpu.make_async_remote_copy(
        src_ref=src_ref.at[there], dst_ref=land_ref.at[here if sending else there],
        send_sem=send_sems.at[j], recv_sem=recv_sems.at[j],
        device_id=(x ^ fx, y ^ fy, c), device_id_type=MESH)


def _chip_start(sums, name):
    def start(src_ref, land_ref, send_sems, recv_sems, src_out, land_out):
        for j in range(len(OTHER_CHIPS)):
            _chip_copy(src_ref, land_ref, send_sems, recv_sems, j, True).start()

    sem = pltpu.SemaphoreType.DMA((len(OTHER_CHIPS),))
    hbm = pltpu.HBM(sums.shape, sums.dtype)
    handles = pl.pallas_call(
        start, name=name, in_specs=[HBM, HBM], out_specs=[SEM, SEM, HBM, HBM], out_shape=[sem, sem, hbm, hbm],
        input_output_aliases={0: 2, 1: 3}, compiler_params=pltpu.CompilerParams(has_side_effects=EFFECT),
    )(pltpu.with_memory_space_constraint(sums, pltpu.HBM),
      pltpu.with_memory_space_constraint(lax.empty(sums.shape, sums.dtype), pltpu.HBM))
    return handles[2], tuple(handles)


def _chip_wait(send_sems, recv_sems, src, land, after, name):
    def body(src_ref, land_ref, send_ref, recv_ref, after_ref, src_out, land_out):
        for j in range(len(OTHER_CHIPS)):
            cp = _chip_copy(src_ref, land_ref, send_ref, recv_ref, j, False)
            cp.wait_send()
            cp.wait_recv()

    hbm = lambda a: pltpu.HBM(a.shape, a.dtype)
    return pl.pallas_call(
        body,
        name=name,
        in_specs=[HBM, HBM, SEM, SEM, ANY],
        out_specs=[HBM, HBM],
        out_shape=[hbm(src), hbm(land)],
        input_output_aliases={0: 0, 1: 1},
        compiler_params=pltpu.CompilerParams(has_side_effects=EFFECT),
    )(src, land, send_sems, recv_sems, after)


def _col(v):
    return v.reshape(-1, 1).astype(F32)


def _local_step(x, tgt, started, weight, small, pair_sums, handles):
    t = x.shape[0]
    n1 = _col(small["norm1_w"])
    n2 = _col(small["norm2_w"])
    nf = _col(small["final_norm_w"])
    bg = _col(small["b_gate"])
    sinks = small["attn_sinks"].reshape(-1).astype(F32)
    cbias = _col(small["ssd_conv_b"])
    dtb = _col(small["dt_bias"])
    alog = _col(small["a_log"])
    dsk = _col(small["d_skip"])
    gnw = _col(small["ssd_norm_w"])
    fb = small["ffn_conv_b"].reshape(2, D_FF, 1)

    xt, xn = _norm_fwd_tokens(x, n1, started, "norm1_fwd")
    cw = weight("ssd_conv_w", xn).T
    fw = weight("ffn_conv_w", xn).T.reshape(2, D_FF, FFN_CONV)
    w_in_t = weight("w_in", xn)
    proj = _matmul(w_in_t, xn, nt=False, out_dtype=F32, name="mm_in")
    ao, lse = _attn_fwd(proj, sinks)
    w_ao = weight("w_attn_o", ao)
    attn = _matmul(w_ao, ao, nt=False, out_dtype=F32, name="mm_attn_o", tn_a=True)
    xbc = _conv_silu_fwd(proj, cw, cbias)
    y, hst, yn = _ssd_fwd(xbc, proj, dtb, alog, dsk, gnw)
    w_so = weight("w_ssd_o", yn)
    ssd = _matmul(w_so, yn, nt=False, out_dtype=F32, name="mm_ssd_o", tn_a=True)
    mix = _gate_fwd(proj, bg, attn, ssd)
    w_out = weight("w_out", mix)
    h1 = _matmul(w_out, mix, nt=False, out_dtype=F32, name="mm_out", add=xt, tn_a=True)
    hn = _norm_fwd(h1, n2, "norm2_fwd")
    w_up_t = weight("w_up", hn)
    u0 = _matmul(w_up_t, hn, nt=False, out_dtype=F32, name="mm_up").reshape(2, D_FF, t)
    gl = _ffn_fwd(u0, fw, fb)
    w_down = weight("w_down", gl)
    h2 = _matmul(w_down, gl, nt=False, out_dtype=F32, name="mm_down", add=h1, tn_a=True)
    dh2, loss, d_nf = _final_norm_loss(h2, tgt, nf)

    g = {}

    def sending(weight_name, grad, fn, *args, **kwargs):
        chunks = grad if grad.ndim == 3 else grad.reshape(N_DEV, -1, D_MODEL)
        out, handles[weight_name] = fn(*args, send=chunks, **kwargs)
        return out

    g_down = _matmul(gl, dh2, nt=True, out_dtype=BF16, name="mm_d_w_down")
    dgl = _matmul(w_down, dh2, nt=False, out_dtype=F32, name="mm_d_glu")
    du0, d_fwb = sending("w_down", g_down, _ffn_bwd, u0, fw, fb, dgl)
    du0 = du0.reshape(2 * D_FF, t)
    g_up = _matmul(du0, hn, nt=True, out_dtype=BF16, name="mm_d_w_up")
    dh1, d_n2 = sending("w_up", g_up, _matmul, w_up_t, du0, nt=False, out_dtype=F32, name="mm_d_hn_norm2_bwd", tn_a=True,
                        norm_bwd=(h1, n2, dh2, False))
    g_out = _matmul(mix, dh1, nt=True, out_dtype=BF16, name="mm_d_w_out")
    dmix = _matmul(w_out, dh1, nt=False, out_dtype=F32, name="mm_d_mix")
    d_attn, d_ssd, d_ga, d_gs, d_ba, d_bs = sending("w_out", g_out, _gate_bwd, proj, bg, attn, ssd, dmix)
    g_ao = _matmul(ao, d_attn, nt=True, out_dtype=BF16, name="mm_d_w_attn_o")
    dao = _matmul(w_ao, d_attn, nt=False, out_dtype=F32, name="mm_d_ao")
    g_so = _matmul(yn, d_ssd, nt=True, out_dtype=BF16, name="mm_d_w_ssd_o")
    dyn = _matmul(w_so, d_ssd, nt=False, out_dtype=F32, name="mm_d_yn")
    dxbc, ddt, d_alog, d_dsk, d_dtb, dproj, d_gnw = sending(
        "w_ssd_o", g_so, _ssd_bwd, xbc, proj, dtb, alog, dsk, hst, y, dyn, gnw)
    dproj, dwb_conv = _conv_silu_bwd(proj, cw, cbias, dxbc, dproj)
    dproj, d_sinks = sending("w_attn_o", g_ao, _attn_bwd, proj, sinks, ao, lse, dao, dproj)
    for rows, part in ((OFF_DT, ddt.astype(BF16)), (OFF_GA, d_ga), (OFF_GS, d_gs)):
        dproj = lax.dynamic_update_slice(dproj, part, (rows, 0))
    leaving = pair_sums(_matmul(dproj, xn, nt=True, out_dtype=BF16, name="mm_d_w_in"))
    dx, d_n1 = _matmul(w_in_t, dproj, nt=False, out_dtype=F32, name="mm_d_xn_norm1_bwd", tn_a=True,
                       norm_bwd=(xt, n1, dh1, True), after=leaving)

    g["norm1_w"] = d_n1
    g["b_gate"] = jnp.concatenate([d_ba, d_bs], axis=0)
    g["attn_sinks"] = d_sinks
    g["ssd_conv_w"] = dwb_conv[:, :SSD_CONV].T
    g["ssd_conv_b"] = dwb_conv[:, SSD_CONV]
    g["dt_bias"] = d_dtb
    g["a_log"] = d_alog
    g["d_skip"] = d_dsk
    g["ssd_norm_w"] = d_gnw
    g["norm2_w"] = d_n2
    d_fwb = d_fwb.reshape(2 * D_FF, 128)
    g["ffn_conv_w"] = d_fwb[:, :FFN_CONV].T
    g["ffn_conv_b"] = d_fwb[:, FFN_CONV]
    g["final_norm_w"] = d_nf
    return loss, dx, g


SMALL = ("norm1_w", "b_gate", "attn_sinks", "ssd_conv_w", "ssd_conv_b", "dt_bias", "a_log", "d_skip", "ssd_norm_w",
         "norm2_w", "ffn_conv_w", "ffn_conv_b", "final_norm_w")
WEIGHT_ORDER = ("norm1_w", "w_in", "b_gate", "attn_sinks", "w_attn_o", "ssd_conv_w", "ssd_conv_b", "dt_bias", "a_log",
                "d_skip", "ssd_norm_w", "w_ssd_o", "w_out", "norm2_w", "w_up", "ffn_conv_w", "ffn_conv_b", "w_down",
                "final_norm_w")


def kernel(x, norm1_w, w_in, b_gate, attn_sinks, w_attn_o, ssd_conv_w, ssd_conv_b, dt_bias, a_log, d_skip, ssd_norm_w, w_ssd_o, w_out, norm2_w, w_up, ffn_conv_w, ffn_conv_b, w_down, final_norm_w, loss_target, m_norm1_w, m_w_in, m_b_gate, m_attn_sinks, m_w_attn_o, m_ssd_conv_w, m_ssd_conv_b, m_dt_bias, m_a_log, m_d_skip, m_ssd_norm_w, m_w_ssd_o, m_w_out, m_norm2_w, m_w_up, m_ffn_conv_w, m_ffn_conv_b, m_w_down, m_final_norm_w, v_norm1_w, v_w_in, v_b_gate, v_attn_sinks, v_w_attn_o, v_ssd_conv_w, v_ssd_conv_b, v_dt_bias, v_a_log, v_d_skip, v_ssd_norm_w, v_w_ssd_o, v_w_out, v_norm2_w, v_w_up, v_ffn_conv_w, v_ffn_conv_b, v_w_down, v_final_norm_w):
    w = dict(norm1_w=norm1_w, w_in=w_in, b_gate=b_gate, attn_sinks=attn_sinks, w_attn_o=w_attn_o, ssd_conv_w=ssd_conv_w, ssd_conv_b=ssd_conv_b, dt_bias=dt_bias, a_log=a_log, d_skip=d_skip, ssd_norm_w=ssd_norm_w, w_ssd_o=w_ssd_o, w_out=w_out, norm2_w=norm2_w, w_up=w_up, ffn_conv_w=ffn_conv_w, ffn_conv_b=ffn_conv_b, w_down=w_down, final_norm_w=final_norm_w)
    m = dict(norm1_w=m_norm1_w, w_in=m_w_in, b_gate=m_b_gate, attn_sinks=m_attn_sinks, w_attn_o=m_w_attn_o, ssd_conv_w=m_ssd_conv_w, ssd_conv_b=m_ssd_conv_b, dt_bias=m_dt_bias, a_log=m_a_log, d_skip=m_d_skip, ssd_norm_w=m_ssd_norm_w, w_ssd_o=m_w_ssd_o, w_out=m_w_out, norm2_w=m_norm2_w, w_up=m_w_up, ffn_conv_w=m_ffn_conv_w, ffn_conv_b=m_ffn_conv_b, w_down=m_w_down, final_norm_w=m_final_norm_w)
    v = dict(norm1_w=v_norm1_w, w_in=v_w_in, b_gate=v_b_gate, attn_sinks=v_attn_sinks, w_attn_o=v_w_attn_o, ssd_conv_w=v_ssd_conv_w, ssd_conv_b=v_ssd_conv_b, dt_bias=v_dt_bias, a_log=v_a_log, d_skip=v_d_skip, ssd_norm_w=v_ssd_norm_w, w_ssd_o=v_w_ssd_o, w_out=v_w_out, norm2_w=v_norm2_w, w_up=v_w_up, ffn_conv_w=v_ffn_conv_w, ffn_conv_b=v_ffn_conv_b, w_down=v_w_down, final_norm_w=v_final_norm_w)
    me = 4 * lax.axis_index("x") + 2 * lax.axis_index("y") + lax.axis_index("c")

    shards = {"ssd_conv_w": ssd_conv_w[0], "ffn_conv_w": ffn_conv_w[0], "w_in": w_in[0].T.astype(BF16),
              "w_attn_o": w_attn_o[0].astype(BF16), "w_ssd_o": w_ssd_o[0].astype(BF16), "w_out": w_out[0].astype(BF16),
              "w_up": w_up[0].T.astype(BF16), "w_down": w_down[0].astype(BF16)}
    order = list(shards)
    g_send, g_recv, g_src, g_land = _gather_start(list(shards.values()), "gather_start", True)
    first = ("ssd_conv_w", "ffn_conv_w", "w_in")
    forwarded = {}

    def weight(name, after):
        if name not in forwarded:
            group = [k for k in order if (k in first) == (name in first)]
            idx = [order.index(k) for k in group]
            handles = _gather_forward([g_send[i] for i in idx], [g_recv[i] for i in idx], [g_src[i] for i in idx],
                                      [g_land[i] for i in idx], after, "gather_forward_for_" + name)
            forwarded.update(zip(group, zip(*handles)))
        i = order.index(name)
        src, land = _gather_wait_forwarded(g_send[i], g_recv[i], *forwarded[name], after, "gather_wait_" + name)
        land = _own_slot(src, land, me, True)
        if name == "ssd_conv_w":
            return jnp.transpose(land, (1, 0, 2)).reshape(SSD_CONV, XBC_DIM)
        if name == "ffn_conv_w":
            return jnp.transpose(land, (1, 0, 2)).reshape(FFN_CONV, 2 * D_FF)
        return land.reshape(-1, D_MODEL)

    res, pending = {}, {}

    def update(name, after):
        if name == "w_in":
            parts = _own_slot(*_chip_wait(*pending[name], after, "grad_wait_" + name), me // 2, False)
        else:
            parts = _own_slot(*_exchange_wait(*pending[name], after, False, "grad_wait_" + name), me, False)
        view, back = {
            "w_in": (lambda a: jnp.transpose(a, (2, 0, 1)), lambda r: jnp.transpose(r, (1, 2, 0))),
            "w_up": (lambda a: a[0].T, lambda r: r.T[None]),
        }.get(name, (lambda a: a[0], lambda r: r[None]))
        done = _adamw_sharded(parts, view(w[name]), view(m[name]), view(v[name]), "adamw_" + name)
        res[name] = [back(r) for r in done]
        return done[0]

    def pair_sums(grad):
        by_core, landed = _pair_exchange(grad.reshape(N_CHIPS, 2, -1, D_MODEL),
                                         lambda started: update("w_up", update("w_down", started)), "grad_pair_w_in")
        leaving, pending["w_in"] = _chip_start(_pair_add(by_core, landed, "grad_pair_add_w_in"), "grad_chips_w_in_start")
        return leaving

    small = {k: w[k][0] if k != "final_norm_w" else w[k] for k in SMALL}
    loss, dx, g = _local_step(x[0], loss_target[0], g_src[0], weight, small, pair_sums, pending)

    packed = _pack_row([loss] + [g[k] for k in SMALL])
    s_send, s_recv, s_src, s_land = _gather_start([packed], "small_grads_start", False)
    after = s_src[0]
    for name in ("w_out", "w_attn_o", "w_ssd_o", "w_in"):
        after = update(name, after)

    rows = _own_slot(*_exchange_wait(s_send[0], s_recv[0], s_src[0], s_land[0], after, True, "small_grads_wait"),
                     me, True)
    flat = lambda a: a.reshape(-1, a.shape[-1])
    loss_sum, updates = _small_update(
        rows, me.reshape(1), [g[k].size for k in SMALL],
        [flat(w[k]) for k in SMALL], [flat(m[k]) for k in SMALL], [flat(v[k]) for k in SMALL])
    for k, upd in zip(SMALL, updates):
        res[k] = [u.reshape(w[k].shape) for u in upd]

    grad_x = dx[None]
    outs = [loss_sum.reshape(()), grad_x]
    for i in range(4):
        outs.extend(res[k][i] for k in WEIGHT_ORDER)
    return tuple(outs)
```

```python
import jax
import jax.numpy as jnp
from jax import lax
from jax.experimental import pallas as pl
from jax.experimental.pallas import tpu as pltpu

F32 = jnp.float32
BF16 = jnp.bfloat16
HIGHEST = lax.Precision.HIGHEST

D_MODEL = 1024
N_Q_HEADS = 16
N_KV_HEADS = 4
HEAD_DIM = 64
WINDOW = 128
Q_PER_KV = N_Q_HEADS // N_KV_HEADS
Q_DIM = N_Q_HEADS * HEAD_DIM
KV_DIM = N_KV_HEADS * HEAD_DIM
D_INNER = 2048
SSD_HEAD_DIM = 64
N_SSD_HEADS = 32
N_SSD_GROUPS = 4
HEADS_PER_GROUP = N_SSD_HEADS // N_SSD_GROUPS
D_STATE = 128
GN_ROWS = D_INNER // N_SSD_GROUPS
BC_DIM = N_SSD_GROUPS * D_STATE
XBC_DIM = D_INNER + 2 * BC_DIM
SSD_CONV = 4
CHUNK = 128
D_FF = 2816
FFN_CONV = 3
EPS = 1e-5
NEG = -1e30
IN_DIM = 8736
N_DEV = 8

OFF_Q = 0
OFF_K = OFF_Q + Q_DIM
OFF_V = OFF_K + KV_DIM
OFF_Z = OFF_V + KV_DIM
OFF_X = OFF_Z + D_INNER
OFF_DT = OFF_X + XBC_DIM
OFF_GA = OFF_DT + N_SSD_HEADS
OFF_GS = OFF_GA + D_MODEL

ADAM_LR = 0.001
ADAM_B1 = 0.9
ADAM_B2 = 0.999
ADAM_EPS = 1e-08
ADAM_WD = 0.01
ADAM_STEP = 10

LANES = 128
BF16_TILE_ROWS = 16
VMEM_BYTES = 64 * 1024 * 1024
VMEM_LIMIT = VMEM_BYTES * 3 // 4
MESH = pl.DeviceIdType.MESH


def _cparams(*sem):
    return pltpu.CompilerParams(dimension_semantics=sem, vmem_limit_bytes=VMEM_LIMIT)


def _tile(n, prefs):
    for p in prefs:
        if n % p == 0:
            return p
    return n


def _sigmoid(x):
    return 1.0 / (1.0 + jnp.exp(-x))


def _softplus(x):
    return jnp.maximum(x, 0.0) + jnp.log(1.0 + jnp.exp(-jnp.abs(x)))


def _rowsum(x):
    return jnp.sum(x, axis=1, keepdims=True)


def _colsum(x):
    return jnp.sum(x, axis=0, keepdims=True)


def _dot(a, b):
    return jnp.dot(a, b, preferred_element_type=F32)


def _dot_nt(a, b):
    return lax.dot_general(a, b, (((1,), (1,)), ((), ())), preferred_element_type=F32)


def _dot_tn(a, b):
    return lax.dot_general(a, b, (((0,), (0,)), ((), ())), preferred_element_type=F32)


def _shift_right(x, j):
    if j == 0:
        return x
    r = pltpu.roll(x, j, 1)
    lane = lax.broadcasted_iota(jnp.int32, (x.shape[0], 128), 1)
    return jnp.concatenate([jnp.where(lane >= j, r[:, :128], 0.0), r[:, 128:]], axis=1)


def _shift_left(x, j):
    if j == 0:
        return x
    n = x.shape[1]
    r = pltpu.roll(x, n - j, 1)
    lane = lax.broadcasted_iota(jnp.int32, (x.shape[0], 128), 1)
    return jnp.concatenate([r[:, :n - 128], jnp.where(lane < 128 - j, r[:, n - 128:], 0.0)], axis=1)


def _causal_conv(xv, wv, bv):
    taps = wv.shape[1]
    shifted = [_shift_right(xv, taps - 1 - k) for k in range(taps - 1)]
    y = bv + wv[:, taps - 1:taps] * xv
    for k in range(taps - 1):
        y = y + wv[:, k:k + 1] * shifted[k]
    return y, shifted


def _causal_conv_bwd(dy, xv, shifted, wv):
    taps = wv.shape[1]
    lane = lax.broadcasted_iota(jnp.int32, (dy.shape[0], 128), 1)
    dwb = jnp.where(lane == taps, _rowsum(dy), 0.0)
    dwb = jnp.where(lane == taps - 1, _rowsum(dy * xv), dwb)
    dx = wv[:, taps - 1:taps] * dy
    for k in range(taps - 1):
        dx = dx + wv[:, k:k + 1] * _shift_left(dy, taps - 1 - k)
        dwb = jnp.where(lane == k, _rowsum(dy * shifted[k]), dwb)
    return dx, dwb


def _call(body, *, name, grid, in_specs, out_specs, out_shape, args, semantics, scratch_shapes=(), aliases=None,
          send=None):
    aliases = dict(aliases or {})
    if send is None:
        return pl.pallas_call(body, name=name, grid=grid, in_specs=in_specs, out_specs=out_specs, out_shape=out_shape,
                              scratch_shapes=list(scratch_shapes), input_output_aliases=aliases,
                              compiler_params=_cparams(*semantics))(*args)
    single = not isinstance(out_specs, (list, tuple))
    out_specs, out_shape = ([out_specs], [out_shape]) if single else (list(out_specs), list(out_shape))
    n_in, n_out = len(in_specs), len(out_specs)
    n_copies = N_DEV - 1

    def sending(*refs):
        ins, (src_ref, land_ref) = refs[:n_in], refs[n_in:n_in + 2]
        outs = refs[n_in + 2:n_in + 2 + n_out]
        send_sems, recv_sems = refs[n_in + 2 + n_out:n_in + 4 + n_out]
        scratch = refs[n_in + 6 + n_out:]
        step = 0
        for axis, size in enumerate(grid):
            step = step * size + pl.program_id(axis)

        @pl.when(step == 0)
        def _():
            for k in range(n_copies):
                _peer_copy(False, src_ref, land_ref, send_sems, recv_sems, k, True).start()

        body(*ins, *outs, *scratch)

    sem = pltpu.SemaphoreType.DMA((n_copies,))
    hbm = pltpu.HBM(send.shape, send.dtype)
    res = pl.pallas_call(
        sending, name=name, grid=grid,
        in_specs=list(in_specs) + [HBM, HBM],
        out_specs=out_specs + [SEM, SEM, HBM, HBM],
        out_shape=out_shape + [sem, sem, hbm, hbm],
        input_output_aliases={**aliases, n_in: n_out + 2, n_in + 1: n_out + 3},
        scratch_shapes=list(scratch_shapes),
        compiler_params=pltpu.CompilerParams(dimension_semantics=("arbitrary",) * len(grid), vmem_limit_bytes=VMEM_LIMIT,
                                             has_side_effects=EFFECT),
    )(*args, pltpu.with_memory_space_constraint(send, pltpu.HBM),
      pltpu.with_memory_space_constraint(lax.empty(send.shape, send.dtype), pltpu.HBM))
    return (res[0] if single else list(res[:n_out])), tuple(res[n_out:])


BLOCK_VMEM_BUDGET = VMEM_LIMIT * 3 // 4
MATMUL_MAX_TM = 768
MATMUL_MAX_TN = 3072
MATMUL_MAX_TK = 3072


def _largest_tile(n, align, cap):
    return max(d for d in range(align, min(n, cap) + 1, align) if n % d == 0)


def _matmul_tiles(m, n, k, a_bytes, b_bytes, out_bytes, f32_blocks, m_align, k_align, whole_m):
    tm = m if whole_m else _largest_tile(m, m_align, MATMUL_MAX_TM)
    tk = _largest_tile(k, k_align, MATMUL_MAX_TK)
    for tn in sorted({d for d in range(LANES, min(n, MATMUL_MAX_TN) + 1, LANES) if n % d == 0}, reverse=True):
        need = 2 * (tm * tk * a_bytes + tk * tn * b_bytes) + tm * tn * (2 * out_bytes + (4 if k > tk else 0) + 8 * f32_blocks)
        if need <= BLOCK_VMEM_BUDGET:
            return tm, tn, tk
    return tm, LANES, tk


def _norm_bwd_math(dy, x, w, res):
    r = lax.rsqrt(jnp.mean(x * x, axis=0, keepdims=True) + EPS)
    xhat = x * r
    dxhat = dy * w
    return res + r * (dxhat - xhat * jnp.mean(dxhat * xhat, axis=0, keepdims=True)), _rowsum(dy * xhat)


def _final_norm_loss_math(h, tgt, w):
    r = lax.rsqrt(jnp.mean(h * h, axis=0, keepdims=True) + EPS)
    xhat = h * r
    err = xhat * w - tgt
    dy = err * (1.0 / h.shape[0])
    dxhat = dy * w
    return (r * (dxhat - xhat * jnp.mean(dxhat * xhat, axis=0, keepdims=True)),
            0.5 * _rowsum(jnp.mean(err * err, axis=0, keepdims=True)), _rowsum(dy * xhat))


def _matmul(a, b, *, nt, out_dtype, name, add=None, tn_a=False, send=None, norm_bwd=None, loss=None, after=None):
    if tn_a:
        k, m = a.shape
    else:
        m, k = a.shape
    n = b.shape[0] if nt else b.shape[1]
    tokens_out = norm_bwd is not None and norm_bwd[3]
    tm, tn, tk = _matmul_tiles(m, n, k, a.dtype.itemsize, b.dtype.itemsize, jnp.dtype(out_dtype).itemsize,
                               (add is not None) + 2 * (norm_bwd is not None) + (loss is not None),
                               LANES if tn_a or tokens_out else BF16_TILE_ROWS,
                               BF16_TILE_ROWS if tn_a and not nt else LANES, norm_bwd is not None or loss is not None)
    nk = k // tk
    grid = (m // tm, n // tn, nk)
    n_extra = (add is not None) + (after is not None) + 3 * (norm_bwd is not None) + 2 * (loss is not None)
    n_out = 1 + (norm_bwd is not None) + 2 * (loss is not None)

    def body(a_ref, b_ref, *rest):
        extra, outs, scratch = rest[:n_extra], rest[n_extra:n_extra + n_out], rest[n_extra + n_out:]
        av = a_ref[...].astype(BF16)
        bv = b_ref[...].astype(BF16)
        part = _dot_tn(av, bv) if tn_a else _dot_nt(av, bv) if nt else _dot(av, bv)

        def finish(r):
            if add is not None:
                r = r + extra[0][...]
            if norm_bwd is not None:
                x_ref, w_ref, res_ref = extra[-3:]
                dx, dw = _norm_bwd_math(r, x_ref[...], w_ref[...], res_ref[...])
                outs[1][...] += dw
                r = dx.T if tokens_out else dx
            if loss is not None:
                t_ref, w_ref = extra[-2:]
                r, term, dw = _final_norm_loss_math(r, t_ref[...].T, w_ref[...])
                outs[1][...] += term
                outs[2][...] += dw
            outs[0][...] = r.astype(out_dtype)

        if n_out > 1:
            @pl.when((pl.program_id(1) == 0) & (pl.program_id(2) == 0))
            def _():
                for o_ref in outs[1:]:
                    o_ref[...] = jnp.zeros_like(o_ref)

        if nk == 1:
            finish(part)
            return
        acc = scratch[0]
        kk = pl.program_id(2)

        @pl.when(kk == 0)
        def _():
            acc[...] = part

        @pl.when((kk > 0) & (kk < nk - 1))
        def _():
            acc[...] += part

        @pl.when(kk == nk - 1)
        def _():
            finish(acc[...] + part)

    tile = pl.BlockSpec((tm, tn), lambda i, j, kk: (i, j))
    in_specs = [
        pl.BlockSpec((tk, tm), lambda i, j, kk: (kk, i)) if tn_a else pl.BlockSpec((tm, tk), lambda i, j, kk: (i, kk)),
        pl.BlockSpec((tn, tk), lambda i, j, kk: (j, kk)) if nt else pl.BlockSpec((tk, tn), lambda i, j, kk: (kk, j)),
    ]
    args = [a, b]
    out_specs, out_shape = tile, jax.ShapeDtypeStruct((m, n), out_dtype)
    if add is not None:
        in_specs.append(tile)
        args.append(add)
    if after is not None:
        in_specs.append(ANY)
        args.append(after)
    assert norm_bwd is None or loss is None
    col = pl.BlockSpec((m, 1), lambda i, j, kk: (0, 0))
    tokens_tile = pl.BlockSpec((tn, tm), lambda i, j, kk: (j, i))
    if norm_bwd is not None:
        x, w_col, res, _ = norm_bwd
        in_specs += [tile, col, tile]
        args += [x, w_col, res]
        if tokens_out:
            out_specs, out_shape = tokens_tile, jax.ShapeDtypeStruct((n, m), out_dtype)
        out_specs, out_shape = [out_specs, col], [out_shape, jax.ShapeDtypeStruct((m, 1), F32)]
    if loss is not None:
        in_specs += [tokens_tile, col]
        args += list(loss)
        out_specs = [out_specs, pl.BlockSpec((1, 1), lambda i, j, kk: (0, 0)), col]
        out_shape = [out_shape, jax.ShapeDtypeStruct((1, 1), F32), jax.ShapeDtypeStruct((m, 1), F32)]
    return _call(
        body, name=name, grid=grid, in_specs=in_specs, args=args, out_specs=out_specs, out_shape=out_shape,
        scratch_shapes=[pltpu.VMEM((tm, tn), F32)] if nk > 1 else [],
        semantics=("parallel", "parallel" if n_out == 1 else "arbitrary", "arbitrary"), send=send)


def _norm_fwd(x, w_col, name):
    f, t = x.shape
    tt = _tile(t, (512, 256, 128))

    def body(x_ref, w_ref, o_ref):
        xv = x_ref[...]
        r = lax.rsqrt(jnp.mean(xv * xv, axis=0, keepdims=True) + EPS)
        o_ref[...] = (xv * r * w_ref[...]).astype(BF16)

    return pl.pallas_call(
        body,
        name=name,
        grid=(t // tt,),
        in_specs=[pl.BlockSpec((f, tt), lambda i: (0, i)), pl.BlockSpec((f, 1), lambda i: (0, 0))],
        out_specs=pl.BlockSpec((f, tt), lambda i: (0, i)),
        out_shape=jax.ShapeDtypeStruct((f, t), BF16),
        compiler_params=_cparams("parallel"),
    )(x, w_col)


def _norm_fwd_tokens(x, w_col, after, name):
    t, f = x.shape
    tt = _tile(t, (512, 256, 128))

    def body(x_ref, w_ref, after_ref, xt_ref, o_ref):
        xv = x_ref[...].T
        xt_ref[...] = xv
        r = lax.rsqrt(jnp.mean(xv * xv, axis=0, keepdims=True) + EPS)
        o_ref[...] = (xv * r * w_ref[...]).astype(BF16)

    blk = pl.BlockSpec((f, tt), lambda i: (0, i))
    return pl.pallas_call(
        body,
        name=name,
        grid=(t // tt,),
        in_specs=[pl.BlockSpec((tt, f), lambda i: (i, 0)), pl.BlockSpec((f, 1), lambda i: (0, 0)), ANY],
        out_specs=[blk, blk],
        out_shape=[jax.ShapeDtypeStruct((f, t), F32), jax.ShapeDtypeStruct((f, t), BF16)],
        compiler_params=_cparams("parallel"),
    )(x, w_col, after)


def _attn_mask(n):
    shape = (2 * WINDOW, Q_PER_KV * WINDOW)
    si = lax.broadcasted_iota(jnp.int32, shape, 0)
    qi = lax.broadcasted_iota(jnp.int32, shape, 1) & (WINDOW - 1)
    dist = WINDOW + qi - si
    return (dist >= 0) & (dist < WINDOW) & ((si >= WINDOW) | (n > 0))


def _lane_cat(ref, row0, rows):
    return jnp.concatenate([ref[row0 + i * rows:row0 + (i + 1) * rows, :] for i in range(Q_PER_KV)], axis=1)


def _attn_fwd(proj, sinks):
    t = proj.shape[1]
    nb = t // WINDOW
    scale = HEAD_DIM ** -0.5

    def body(s_ref, q_ref, kc_ref, kp_ref, vc_ref, vp_ref, o_ref, lse_ref):
        n = pl.program_id(0)
        valid = _attn_mask(n)
        for g in range(N_KV_HEADS):
            rows = slice(g * HEAD_DIM, (g + 1) * HEAD_DIM)
            kt = jnp.concatenate([kp_ref[rows, :], kc_ref[rows, :]], axis=1).astype(BF16)
            vt = jnp.concatenate([vp_ref[rows, :], vc_ref[rows, :]], axis=1).astype(BF16)
            qcat = (_lane_cat(q_ref, g * Q_PER_KV * HEAD_DIM, HEAD_DIM) * scale).astype(BF16)
            s = jnp.where(valid, _dot_tn(kt, qcat), NEG)
            sink = jnp.concatenate(
                [jnp.full((1, WINDOW), s_ref[g * Q_PER_KV + i], F32) for i in range(Q_PER_KV)], axis=1)
            m = jnp.maximum(jnp.max(s, axis=0, keepdims=True), sink)
            p = jnp.exp(s - m)
            denom = _colsum(p) + jnp.exp(sink - m)
            probs = (p / denom).astype(BF16)
            out = _dot(vt, probs)
            lse = m + jnp.log(denom)
            for i in range(Q_PER_KV):
                h = g * Q_PER_KV + i
                o_ref[h * HEAD_DIM:(h + 1) * HEAD_DIM, :] = out[:, i * WINDOW:(i + 1) * WINDOW]
                lse_ref[h:h + 1, :] = lse[:, i * WINDOW:(i + 1) * WINDOW]

    kb = OFF_K // KV_DIM
    vb = OFF_V // KV_DIM
    prev = lambda n: jnp.maximum(n - 1, 0)
    return pl.pallas_call(
        body,
        name="attn_fwd",
        grid=(nb,),
        in_specs=[
            pl.BlockSpec(memory_space=pltpu.SMEM),
            pl.BlockSpec((Q_DIM, WINDOW), lambda n: (0, n)),
            pl.BlockSpec((KV_DIM, WINDOW), lambda n: (kb, n)),
            pl.BlockSpec((KV_DIM, WINDOW), lambda n: (kb, prev(n))),
            pl.BlockSpec((KV_DIM, WINDOW), lambda n: (vb, n)),
            pl.BlockSpec((KV_DIM, WINDOW), lambda n: (vb, prev(n))),
        ],
        out_specs=[pl.BlockSpec((Q_DIM, WINDOW), lambda n: (0, n)), pl.BlockSpec((N_Q_HEADS, WINDOW), lambda n: (0, n))],
        out_shape=[jax.ShapeDtypeStruct((Q_DIM, t), F32), jax.ShapeDtypeStruct((N_Q_HEADS, t), F32)],
        compiler_params=_cparams("parallel"),
    )(sinks, proj, proj, proj, proj, proj)


def _attn_bwd(proj, sinks, out, lse, dout, dproj, send=None):
    t = proj.shape[1]
    nb = t // WINDOW
    scale = HEAD_DIM ** -0.5

    def body(s_ref, q_ref, kc_ref, kp_ref, vc_ref, vp_ref, o_ref, lse_ref, do_ref, dproj_ref,
             dqkv_ref, ds_ref, dk_carry, dv_carry):
        dq_ref = dqkv_ref.at[pl.ds(OFF_Q, Q_DIM)]
        dk_ref = dqkv_ref.at[pl.ds(OFF_K, KV_DIM)]
        dv_ref = dqkv_ref.at[pl.ds(OFF_V, KV_DIM)]
        step = pl.program_id(0)
        n = nb - 1 - step

        @pl.when(step == 0)
        def _():
            dk_carry[...] = jnp.zeros_like(dk_carry)
            dv_carry[...] = jnp.zeros_like(dv_carry)
            ds_ref[...] = jnp.zeros_like(ds_ref)

        valid = _attn_mask(n)
        for g in range(N_KV_HEADS):
            rows = slice(g * HEAD_DIM, (g + 1) * HEAD_DIM)
            q0 = g * Q_PER_KV * HEAD_DIM
            kt = jnp.concatenate([kp_ref[rows, :], kc_ref[rows, :]], axis=1).astype(BF16)
            vt = jnp.concatenate([vp_ref[rows, :], vc_ref[rows, :]], axis=1).astype(BF16)
            qf = _lane_cat(q_ref, q0, HEAD_DIM)
            qcat = qf.astype(BF16)
            ocat = _lane_cat(o_ref, q0, HEAD_DIM)
            docat = _lane_cat(do_ref, q0, HEAD_DIM)
            dob = docat.astype(BF16)
            lse_cat = jnp.concatenate(
                [lse_ref[g * Q_PER_KV + i:g * Q_PER_KV + i + 1, :] for i in range(Q_PER_KV)], axis=1)
            sink = jnp.concatenate(
                [jnp.full((1, WINDOW), s_ref[g * Q_PER_KV + i], F32) for i in range(Q_PER_KV)], axis=1)
            s = jnp.where(valid, _dot_tn(kt, (qf * scale).astype(BF16)), NEG)
            p = jnp.exp(s - lse_cat)
            dp = _dot_tn(vt, dob)
            delta = _colsum(docat * ocat)
            dsc = (p * (dp - delta)).astype(BF16)
            dsink_row = -jnp.exp(sink - lse_cat) * delta
            dq = _dot(kt, dsc) * scale
            dk = _dot_nt(qcat, dsc) * scale
            dv = _dot_nt(dob, p.astype(BF16))
            for i in range(Q_PER_KV):
                h = g * Q_PER_KV + i
                dq_ref[h * HEAD_DIM:(h + 1) * HEAD_DIM, :] = dq[:, i * WINDOW:(i + 1) * WINDOW].astype(BF16)
                ds_ref[h:h + 1, :] += _rowsum(dsink_row[:, i * WINDOW:(i + 1) * WINDOW])
            dk_ref[rows, :] = (dk[:, WINDOW:] + dk_carry[rows, :]).astype(BF16)
            dv_ref[rows, :] = (dv[:, WINDOW:] + dv_carry[rows, :]).astype(BF16)
            dk_carry[rows, :] = dk[:, :WINDOW]
            dv_carry[rows, :] = dv[:, :WINDOW]

    kb = OFF_K // KV_DIM
    vb = OFF_V // KV_DIM
    cur = lambda i: nb - 1 - i
    prev = lambda i: jnp.maximum(nb - 2 - i, 0)
    qspec = pl.BlockSpec((Q_DIM, WINDOW), lambda i: (0, cur(i)))
    return _call(
        body,
        name="attn_bwd",
        grid=(nb,),
        in_specs=[
            pl.BlockSpec(memory_space=pltpu.SMEM),
            qspec,
            pl.BlockSpec((KV_DIM, WINDOW), lambda i: (kb, cur(i))),
            pl.BlockSpec((KV_DIM, WINDOW), lambda i: (kb, prev(i))),
            pl.BlockSpec((KV_DIM, WINDOW), lambda i: (vb, cur(i))),
            pl.BlockSpec((KV_DIM, WINDOW), lambda i: (vb, prev(i))),
            qspec,
            pl.BlockSpec((N_Q_HEADS, WINDOW), lambda i: (0, cur(i))),
            qspec,
            pl.BlockSpec(memory_space=pl.ANY),
        ],
        out_specs=[pl.BlockSpec((OFF_Z, WINDOW), lambda i: (0, cur(i))), pl.BlockSpec((N_Q_HEADS, 1), lambda i: (0, 0))],
        out_shape=[jax.ShapeDtypeStruct(dproj.shape, BF16), jax.ShapeDtypeStruct((N_Q_HEADS, 1), F32)],
        scratch_shapes=[pltpu.VMEM((KV_DIM, WINDOW), F32), pltpu.VMEM((KV_DIM, WINDOW), F32)],
        aliases={9: 0},
        semantics=("arbitrary",), args=(sinks, proj, proj, proj, proj, proj, out, lse, dout, dproj), send=send)


CONV_ROWS = 256


def _conv_silu_fwd(proj, w_col, b_col):
    t = proj.shape[1]
    r0 = OFF_X // CONV_ROWS

    def body(x_ref, w_ref, b_ref, o_ref):
        y, _ = _causal_conv(x_ref[...], w_ref[...], b_ref[...])
        o_ref[...] = y * _sigmoid(y)

    return pl.pallas_call(
        body,
        name="ssd_conv_fwd",
        grid=(XBC_DIM // CONV_ROWS,),
        in_specs=[
            pl.BlockSpec((CONV_ROWS, t), lambda i: (r0 + i, 0)),
            pl.BlockSpec((CONV_ROWS, SSD_CONV), lambda i: (i, 0)),
            pl.BlockSpec((CONV_ROWS, 1), lambda i: (i, 0)),
        ],
        out_specs=pl.BlockSpec((CONV_ROWS, t), lambda i: (i, 0)),
        out_shape=jax.ShapeDtypeStruct((XBC_DIM, t), F32),
        compiler_params=_cparams("parallel"),
    )(proj, w_col, b_col)


def _conv_silu_bwd(proj, w_col, b_col, dout, dproj):
    t = proj.shape[1]
    p0 = OFF_X // CONV_ROWS

    def body(x_ref, w_ref, b_ref, do_ref, dproj_ref, dx_ref, dwb_ref):
        xv = x_ref[...]
        wv = w_ref[...]
        y, shifted = _causal_conv(xv, wv, b_ref[...])
        sg = _sigmoid(y)
        dy = do_ref[...] * (sg * (1.0 + y * (1.0 - sg)))
        dx, dwb_ref[...] = _causal_conv_bwd(dy, xv, shifted, wv)
        dx_ref[...] = dx.astype(BF16)

    return pl.pallas_call(
        body,
        name="ssd_conv_bwd",
        grid=(XBC_DIM // CONV_ROWS,),
        in_specs=[
            pl.BlockSpec((CONV_ROWS, t), lambda i: (p0 + i, 0)),
            pl.BlockSpec((CONV_ROWS, SSD_CONV), lambda i: (i, 0)),
            pl.BlockSpec((CONV_ROWS, 1), lambda i: (i, 0)),
            pl.BlockSpec((CONV_ROWS, t), lambda i: (i, 0)),
            pl.BlockSpec(memory_space=pl.ANY),
        ],
        out_specs=[pl.BlockSpec((CONV_ROWS, t), lambda i: (p0 + i, 0)), pl.BlockSpec((CONV_ROWS, 128), lambda i: (i, 0))],
        out_shape=[jax.ShapeDtypeStruct(dproj.shape, BF16), jax.ShapeDtypeStruct((XBC_DIM, 128), F32)],
        input_output_aliases={4: 0},
        compiler_params=_cparams("parallel"),
    )(proj, w_col, b_col, dout, dproj)


def _ssd_specs(order):
    xb = D_INNER // BC_DIM
    dtb = OFF_DT // N_SSD_HEADS
    col = pl.BlockSpec((N_SSD_HEADS, 1), lambda c: (0, 0))
    return [
        pl.BlockSpec((D_INNER, CHUNK), lambda c: (0, order(c))),
        pl.BlockSpec((BC_DIM, CHUNK), lambda c: (xb, order(c))),
        pl.BlockSpec((BC_DIM, CHUNK), lambda c: (xb + 1, order(c))),
        pl.BlockSpec((N_SSD_HEADS, CHUNK), lambda c: (dtb, order(c))),
        col, col, col,
    ]


def _ssd_common(dt_ref, dtb_ref, alog_ref):
    z = dt_ref[...] + dtb_ref[...]
    dt = _softplus(z)
    a_neg = -jnp.exp(alog_ref[...])
    d_a = dt * a_neg
    row = lax.broadcasted_iota(jnp.int32, (CHUNK, CHUNK), 0)
    colm = lax.broadcasted_iota(jnp.int32, (CHUNK, CHUNK), 1)
    upper = (row <= colm).astype(F32)
    a_cs = jnp.dot(d_a, upper, precision=HIGHEST, preferred_element_type=F32)
    a_last = _rowsum(d_a)
    return z, dt, a_neg, a_cs, a_last, row >= colm, row == colm


def _decay(a_row, causal):
    a_s = jnp.broadcast_to(a_row, (CHUNK, CHUNK))
    seg = a_s.T - a_s
    return jnp.where(causal, jnp.exp(jnp.where(causal, seg, 0.0)), 0.0)


def _ssd_fwd(xbc, proj, dtb_col, alog_col, dsk_col, gnw_col):
    t = xbc.shape[1]
    nc = t // CHUNK

    def body(xs_ref, b_ref, c_ref, dt_ref, dtb_ref, alog_ref, dsk_ref, *rest):
        z_refs, (gnw_ref, y_ref, hst_ref, yn_ref, h_scr) = rest[:N_SSD_GROUPS], rest[N_SSD_GROUPS:]

        @pl.when(pl.program_id(0) == 0)
        def _():
            h_scr[...] = jnp.zeros_like(h_scr)

        _, dt, _, a_cs, a_last, causal, _ = _ssd_common(dt_ref, dtb_ref, alog_ref)
        hst_ref[0] = h_scr[...]
        dsk = dsk_ref[...]
        for g in range(N_SSD_GROUPS):
            grows = slice(g * D_STATE, (g + 1) * D_STATE)
            bb = b_ref[grows, :].astype(BF16)
            cb_ = c_ref[grows, :].astype(BF16)
            cb = _dot_tn(cb_, bb)
            for j in range(g * HEADS_PER_GROUP, (g + 1) * HEADS_PER_GROUP):
                rows = slice(j * SSD_HEAD_DIM, (j + 1) * SSD_HEAD_DIM)
                a = a_cs[j:j + 1, :]
                m = (cb * _decay(a, causal)).astype(BF16)
                xs = xs_ref[rows, :]
                xc = xs * dt[j:j + 1, :]
                hj = h_scr[rows, :]
                y = _dot_nt(xc.astype(BF16), m) + _dot(hj.astype(BF16), cb_) * jnp.exp(a) + dsk[j:j + 1, :] * xs
                y_ref[rows, :] = y
                al = a_last[j:j + 1, :]
                w = jnp.exp(al - a)
                h_scr[rows, :] = jnp.exp(al) * hj + _dot_nt((xc * w).astype(BF16), bb)
        for g in range(N_SSD_GROUPS):
            rows = slice(g * GN_ROWS, (g + 1) * GN_ROWS)
            zv = z_refs[g][...]
            u = y_ref[rows, :] * (zv * _sigmoid(zv))
            r = lax.rsqrt(jnp.mean(u * u, axis=0, keepdims=True) + EPS)
            yn_ref[rows, :] = (u * r * gnw_ref[rows, :]).astype(BF16)

    z0 = OFF_Z // GN_ROWS
    z_specs = [pl.BlockSpec((GN_ROWS, CHUNK), lambda c, g=g: (z0 + g, c)) for g in range(N_SSD_GROUPS)]
    rows_spec = pl.BlockSpec((D_INNER, CHUNK), lambda c: (0, c))
    return pl.pallas_call(
        body,
        name="ssd_fwd",
        grid=(nc,),
        in_specs=_ssd_specs(lambda c: c) + z_specs + [pl.BlockSpec((D_INNER, 1), lambda c: (0, 0))],
        out_specs=[rows_spec, pl.BlockSpec((1, D_INNER, D_STATE), lambda c: (c, 0, 0)), rows_spec],
        out_shape=[
            jax.ShapeDtypeStruct((D_INNER, t), F32),
            jax.ShapeDtypeStruct((nc, D_INNER, D_STATE), F32),
            jax.ShapeDtypeStruct((D_INNER, t), BF16),
        ],
        scratch_shapes=[pltpu.VMEM((D_INNER, D_STATE), F32)],
        compiler_params=_cparams("arbitrary"),
    )(xbc, xbc, xbc, proj, dtb_col, alog_col, dsk_col, *([proj] * N_SSD_GROUPS), gnw_col)


def _ssd_bwd(xbc, proj, dtb_col, alog_col, dsk_col, hst, y, dyn, gnw_col, send=None):
    t = xbc.shape[1]
    nc = t // CHUNK
    rev = lambda c: nc - 1 - c

    def body(xs_ref, b_ref, c_ref, dt_ref, dtb_ref, alog_ref, dsk_ref, hst_ref, y_ref, dyn_ref, *rest):
        z_refs, rest = rest[:N_SSD_GROUPS], rest[N_SSD_GROUPS:]
        (gnw_ref, dxbc_ref, ddt_ref, dalog_ref, ddsk_ref, ddtb_ref, dz_ref, dgnw_ref,
         dh_scr, da_scr, ddt_scr, dd_scr, dy_ref) = rest
        dxs_ref = dxbc_ref.at[pl.ds(0, D_INNER)]
        db_ref = dxbc_ref.at[pl.ds(D_INNER, BC_DIM)]
        dc_ref = dxbc_ref.at[pl.ds(D_INNER + BC_DIM, BC_DIM)]

        @pl.when(pl.program_id(0) == 0)
        def _():
            dh_scr[...] = jnp.zeros_like(dh_scr)
            dalog_ref[...] = jnp.zeros_like(dalog_ref)
            ddsk_ref[...] = jnp.zeros_like(ddsk_ref)
            ddtb_ref[...] = jnp.zeros_like(ddtb_ref)
            dgnw_ref[...] = jnp.zeros_like(dgnw_ref)

        for g in range(N_SSD_GROUPS):
            rows = slice(g * GN_ROWS, (g + 1) * GN_ROWS)
            zv = z_refs[g][...]
            yv = y_ref[rows, :]
            sg = _sigmoid(zv)
            sz = zv * sg
            u = yv * sz
            r = lax.rsqrt(jnp.mean(u * u, axis=0, keepdims=True) + EPS)
            xhat = u * r
            dov = dyn_ref[rows, :]
            dgnw_ref[rows, :] += _rowsum(dov * xhat)
            dxhat = dov * gnw_ref[rows, :]
            du = r * (dxhat - xhat * jnp.mean(dxhat * xhat, axis=0, keepdims=True))
            dy_ref[rows, :] = du * sz
            dz_ref[rows, :] = (du * yv * (sg * (1.0 + zv * (1.0 - sg)))).astype(BF16)

        z, dt, a_neg, a_cs, a_last, causal, eye = _ssd_common(dt_ref, dtb_ref, alog_ref)
        dsk = dsk_ref[...]
        last_lane = lax.broadcasted_iota(jnp.int32, (1, CHUNK), 1) == CHUNK - 1
        for g in range(N_SSD_GROUPS):
            grows = slice(g * D_STATE, (g + 1) * D_STATE)
            bb = b_ref[grows, :].astype(BF16)
            cb_ = c_ref[grows, :].astype(BF16)
            cb = _dot_tn(cb_, bb)
            dcb = jnp.zeros((CHUNK, CHUNK), F32)
            dc_acc = jnp.zeros((D_STATE, CHUNK), F32)
            db_acc = jnp.zeros((D_STATE, CHUNK), F32)
            for j in range(g * HEADS_PER_GROUP, (g + 1) * HEADS_PER_GROUP):
                rows = slice(j * SSD_HEAD_DIM, (j + 1) * SSD_HEAD_DIM)
                a = a_cs[j:j + 1, :]
                al = a_last[j:j + 1, :]
                lam = _decay(a, causal)
                mf = cb * lam
                xs = xs_ref[rows, :]
                dtj = dt[j:j + 1, :]
                xc = xs * dtj
                w = jnp.exp(al - a)
                e = jnp.exp(a)
                gam = jnp.exp(al)
                hj = hst_ref[0, rows, :]
                hjb = hj.astype(BF16)
                dyv = dy_ref[rows, :]
                dyb = dyv.astype(BF16)
                dd_scr[j:j + 1, :] = _colsum(dyv * xs)
                gb = (dyv * e).astype(BF16)
                dh_in = _dot_nt(gb, cb_)
                dc_acc = dc_acc + _dot_tn(hjb, gb)
                yoff = _dot(hjb, cb_) * e
                da = _colsum(dyv * yoff)
                dm = _dot_tn(dyb, xc.astype(BF16))
                dxc = _dot(dyb, mf.astype(BF16))
                dcb = dcb + dm * lam
                nmat = dm * mf
                rs = jnp.broadcast_to(_rowsum(nmat), (CHUNK, CHUNK))
                da = da + _colsum(jnp.where(eye, rs, 0.0)) - _colsum(nmat)
                ds = dh_scr[rows, :]
                dsb = ds.astype(BF16)
                t1 = _dot(dsb, bb)
                xcw = xc * w
                dxc = dxc + w * t1
                dww = _colsum(xcw * t1)
                da_l = _rowsum(dww) + _rowsum(_colsum(ds * hj)) * gam
                da = da - dww + jnp.where(last_lane, da_l, 0.0)
                db_acc = db_acc + _dot_tn(dsb, xcw.astype(BF16))
                dh_scr[rows, :] = gam * ds + dh_in
                dxs_ref[rows, :] = dsk[j:j + 1, :] * dyv + dxc * dtj
                da_scr[j:j + 1, :] = da
                ddt_scr[j:j + 1, :] = _colsum(dxc * xs)
            dcbb = dcb.astype(BF16)
            dc_ref[grows, :] = dc_acc + _dot_nt(bb, dcbb)
            db_ref[grows, :] = db_acc + _dot(cb_, dcbb)
        dda = jnp.dot(da_scr[...], causal.astype(F32), precision=HIGHEST, preferred_element_type=F32)
        ddt = ddt_scr[...] + dda * a_neg
        ddt_raw = ddt * _sigmoid(z)
        ddt_ref[...] = ddt_raw
        ddtb_ref[...] += _rowsum(ddt_raw)
        dalog_ref[...] += _rowsum(dda * dt) * a_neg
        ddsk_ref[...] += _rowsum(dd_scr[...])

    col = pl.BlockSpec((N_SSD_HEADS, 1), lambda c: (0, 0))
    xs_spec = pl.BlockSpec((D_INNER, CHUNK), lambda c: (0, rev(c)))
    gn_col = pl.BlockSpec((D_INNER, 1), lambda c: (0, 0))
    z0 = OFF_Z // GN_ROWS
    z_specs = [pl.BlockSpec((GN_ROWS, CHUNK), lambda c, g=g: (z0 + g, rev(c))) for g in range(N_SSD_GROUPS)]
    dz_spec = pl.BlockSpec((pl.Element(D_INNER), pl.Element(CHUNK)),
                           lambda c: (OFF_Z, pl.multiple_of(CHUNK * rev(c), CHUNK)))
    small = pltpu.VMEM((N_SSD_HEADS, CHUNK), F32)
    return _call(
        body,
        name="ssd_bwd",
        grid=(nc,),
        in_specs=_ssd_specs(rev) + [pl.BlockSpec((1, D_INNER, D_STATE), lambda c: (rev(c), 0, 0)), xs_spec, xs_spec]
        + z_specs + [gn_col],
        out_specs=[pl.BlockSpec((XBC_DIM, CHUNK), lambda c: (0, rev(c))),
                   pl.BlockSpec((N_SSD_HEADS, CHUNK), lambda c: (0, rev(c))), col, col, col, dz_spec, gn_col],
        out_shape=[
            jax.ShapeDtypeStruct((XBC_DIM, t), F32),
            jax.ShapeDtypeStruct((N_SSD_HEADS, t), F32),
            jax.ShapeDtypeStruct((N_SSD_HEADS, 1), F32),
            jax.ShapeDtypeStruct((N_SSD_HEADS, 1), F32),
            jax.ShapeDtypeStruct((N_SSD_HEADS, 1), F32),
            jax.ShapeDtypeStruct((IN_DIM, t), BF16),
            jax.ShapeDtypeStruct((D_INNER, 1), F32),
        ],
        scratch_shapes=[pltpu.VMEM((D_INNER, D_STATE), F32), small, small, small, pltpu.VMEM((D_INNER, CHUNK), F32)],
        semantics=("arbitrary",),
        args=(xbc, xbc, xbc, proj, dtb_col, alog_col, dsk_col, hst, y, dyn, *([proj] * N_SSD_GROUPS), gnw_col),
        send=send)


GATE_ROWS = 128


def _gate_specs(t):
    nr = D_MODEL // GATE_ROWS
    blk = pl.BlockSpec((GATE_ROWS, t), lambda r: (r, 0))
    rows_from = lambda first: pl.BlockSpec(
        (pl.Element(GATE_ROWS), pl.Element(t)), lambda r: (pl.multiple_of(first + GATE_ROWS * r, N_SSD_HEADS), 0))
    return blk, [
        rows_from(OFF_GA),
        rows_from(OFF_GS),
        pl.BlockSpec((GATE_ROWS, 1), lambda r: (r, 0)),
        pl.BlockSpec((GATE_ROWS, 1), lambda r: (nr + r, 0)),
        blk, blk,
    ]


def _gate_fwd(proj, b_col, attn, ssd):
    t = proj.shape[1]
    blk, specs = _gate_specs(t)

    def body(ga_ref, gs_ref, ba_ref, bs_ref, a_ref, s_ref, o_ref):
        o_ref[...] = (_sigmoid(ga_ref[...] + ba_ref[...]) * a_ref[...]
                      + _sigmoid(gs_ref[...] + bs_ref[...]) * s_ref[...]).astype(BF16)

    return pl.pallas_call(
        body,
        name="gate_fwd",
        grid=(D_MODEL // GATE_ROWS,),
        in_specs=specs,
        out_specs=blk,
        out_shape=jax.ShapeDtypeStruct((D_MODEL, t), BF16),
        compiler_params=_cparams("parallel"),
    )(proj, proj, b_col, b_col, attn, ssd)


def _gate_bwd(proj, b_col, attn, ssd, dmix, send=None):
    t = proj.shape[1]
    blk, specs = _gate_specs(t)

    def body(ga_ref, gs_ref, ba_ref, bs_ref, a_ref, s_ref, dm_ref, da_ref, dso_ref, dga_ref, dgs_ref, dba_ref, dbs_ref):
        dm = dm_ref[...]
        sa = _sigmoid(ga_ref[...] + ba_ref[...])
        ss = _sigmoid(gs_ref[...] + bs_ref[...])
        da_ref[...] = (dm * sa).astype(BF16)
        dso_ref[...] = (dm * ss).astype(BF16)
        dga = dm * a_ref[...] * sa * (1.0 - sa)
        dgs = dm * s_ref[...] * ss * (1.0 - ss)
        dga_ref[...] = dga.astype(BF16)
        dgs_ref[...] = dgs.astype(BF16)
        dba_ref[...] = _rowsum(dga)
        dbs_ref[...] = _rowsum(dgs)

    col = pl.BlockSpec((GATE_ROWS, 1), lambda r: (r, 0))
    act = jax.ShapeDtypeStruct((D_MODEL, t), BF16)
    bias = jax.ShapeDtypeStruct((D_MODEL, 1), F32)
    return _call(
        body,
        name="gate_bwd",
        grid=(D_MODEL // GATE_ROWS,),
        in_specs=specs + [blk],
        out_specs=[blk, blk, blk, blk, col, col],
        out_shape=[act, act, act, act, bias, bias],
        semantics=("parallel",), args=(proj, proj, b_col, b_col, attn, ssd, dmix), send=send)


FFN_ROWS = 256


def _ffn_fwd(u0, w_col, b_col):
    t = u0.shape[2]

    def body(u_ref, w_ref, b_ref, o_ref):
        val, _ = _causal_conv(u_ref[0], w_ref[0], b_ref[0])
        gt, _ = _causal_conv(u_ref[1], w_ref[1], b_ref[1])
        o_ref[...] = (gt * _sigmoid(gt) * val).astype(BF16)

    return pl.pallas_call(
        body,
        name="ffn_fwd",
        grid=(D_FF // FFN_ROWS,),
        in_specs=[
            pl.BlockSpec((2, FFN_ROWS, t), lambda i: (0, i, 0)),
            pl.BlockSpec((2, FFN_ROWS, FFN_CONV), lambda i: (0, i, 0)),
            pl.BlockSpec((2, FFN_ROWS, 1), lambda i: (0, i, 0)),
        ],
        out_specs=pl.BlockSpec((FFN_ROWS, t), lambda i: (i, 0)),
        out_shape=jax.ShapeDtypeStruct((D_FF, t), BF16),
        compiler_params=_cparams("parallel"),
    )(u0, w_col, b_col)


def _ffn_bwd(u0, w_col, b_col, dg, send=None):
    t = u0.shape[2]

    def body(u_ref, w_ref, b_ref, dg_ref, du_ref, dwb_ref):
        xval, wval = u_ref[0], w_ref[0]
        xgt, wgt = u_ref[1], w_ref[1]
        val, sh_val = _causal_conv(xval, wval, b_ref[0])
        gt, sh_gt = _causal_conv(xgt, wgt, b_ref[1])
        sg = _sigmoid(gt)
        dgv = dg_ref[...]
        dval = dgv * (gt * sg)
        dgt = dgv * val * (sg * (1.0 + gt * (1.0 - sg)))
        dx, dwb_ref[0] = _causal_conv_bwd(dval, xval, sh_val, wval)
        du_ref[0] = dx.astype(BF16)
        dx, dwb_ref[1] = _causal_conv_bwd(dgt, xgt, sh_gt, wgt)
        du_ref[1] = dx.astype(BF16)

    return _call(
        body,
        name="ffn_bwd",
        grid=(D_FF // FFN_ROWS,),
        in_specs=[
            pl.BlockSpec((2, FFN_ROWS, t), lambda i: (0, i, 0)),
            pl.BlockSpec((2, FFN_ROWS, FFN_CONV), lambda i: (0, i, 0)),
            pl.BlockSpec((2, FFN_ROWS, 1), lambda i: (0, i, 0)),
            pl.BlockSpec((FFN_ROWS, t), lambda i: (i, 0)),
        ],
        out_specs=[pl.BlockSpec((2, FFN_ROWS, t), lambda i: (0, i, 0)), pl.BlockSpec((2, FFN_ROWS, 128), lambda i: (0, i, 0))],
        out_shape=[jax.ShapeDtypeStruct((2, D_FF, t), BF16), jax.ShapeDtypeStruct((2, D_FF, 128), F32)],
        semantics=("parallel",), args=(u0, w_col, b_col, dg), send=send)


def _adamw_math(w, g, m, v):
    m = ADAM_B1 * m + (1.0 - ADAM_B1) * g
    v = ADAM_B2 * v + (1.0 - ADAM_B2) * (g * g)
    m_hat = m / (1.0 - ADAM_B1 ** ADAM_STEP)
    v_hat = v / (1.0 - ADAM_B2 ** ADAM_STEP)
    delta = -ADAM_LR * (m_hat / (jnp.sqrt(v_hat) + ADAM_EPS) + ADAM_WD * w)
    return delta, m, v


def _adamw_sharded(parts, w, m, v, name):
    r, c = w.shape[0], w.shape[-1]
    slots = parts.shape[0]
    per_lane = 2 * r * (slots * parts.dtype.itemsize + 7 * w.dtype.itemsize)
    tc = max(d for d in range(LANES, c + 1, LANES) if c % d == 0 and (d * per_lane <= BLOCK_VMEM_BUDGET or d == LANES))
    blk_shape = (r, tc) if w.ndim == 2 else (r, 1, tc)

    def body(p_ref, w_ref, m_ref, v_ref, g_ref, d_ref, nm_ref, nv_ref):
        g = p_ref[0].astype(F32)
        for s in range(1, slots):
            g = g + p_ref[s].astype(F32)
        flat = lambda ref: ref[...].reshape(r, tc)
        d, nm, nv = _adamw_math(flat(w_ref), g, flat(m_ref), flat(v_ref))
        for ref, val in ((g_ref, g), (d_ref, d), (nm_ref, nm), (nv_ref, nv)):
            ref[...] = val.reshape(blk_shape)

    blk = pl.BlockSpec(blk_shape, (lambda i: (0, i)) if w.ndim == 2 else (lambda i: (0, 0, i)))
    out = jax.ShapeDtypeStruct(w.shape, F32)
    return pl.pallas_call(
        body,
        name=name,
        grid=(c // tc,),
        in_specs=[pl.BlockSpec((slots, r, tc), lambda i: (0, 0, i)), blk, blk, blk],
        out_specs=[blk, blk, blk, blk],
        out_shape=[out, out, out, out],
        compiler_params=_cparams("parallel"),
    )(parts, w, m, v)


def _lane_offsets(sizes):
    offsets, pos = [], 0
    for n in sizes:
        offsets.append(pos)
        pos += -(-n // 128) * 128
    return offsets, pos


def _pack_row(parts):
    rows = [p.reshape(1, -1).astype(F32) for p in parts]
    return jnp.concatenate([jnp.pad(r, ((0, 0), (0, -r.shape[1] % 128))) for r in rows], axis=1)


def _small_update(parts, me, full_sizes, ws, ms, vs):
    n = len(ws)
    offsets, _ = _lane_offsets([1] + list(full_sizes))

    def body(me_ref, p_ref, *refs):
        w_refs, m_refs, v_refs = refs[:n], refs[n:2 * n], refs[2 * n:3 * n]
        scalar_ref, out_refs = refs[3 * n], refs[3 * n + 1:]
        tot = p_ref[0]
        for s in range(1, N_DEV):
            tot = tot + p_ref[s]
        scalar_ref[...] = tot[:, 0:1]
        for k in range(n):
            g_ref, d_ref, nm_ref, nv_ref = out_refs[4 * k:4 * k + 4]
            taps, cols = w_refs[k].shape
            if taps == 1:
                g_ref[...] = tot[:, offsets[k + 1]:offsets[k + 1] + cols]
            else:
                full = full_sizes[k] // taps
                for tap in range(taps):
                    mine = jnp.zeros((1, cols), F32)
                    for d in range(N_DEV):
                        lo = offsets[k + 1] + tap * full + d * cols
                        mine = jnp.where(me_ref[0] == d, tot[:, lo:lo + cols], mine)
                    g_ref[tap:tap + 1, :] = mine
            d_ref[...], nm_ref[...], nv_ref[...] = _adamw_math(w_refs[k][...], g_ref[...], m_refs[k][...], v_refs[k][...])

    vmem = pl.BlockSpec(memory_space=pltpu.VMEM)
    out_shape = [jax.ShapeDtypeStruct((1, 1), F32)]
    for wk in ws:
        out_shape += [jax.ShapeDtypeStruct(wk.shape, F32)] * 4
    res = pl.pallas_call(
        body,
        name="small_update",
        in_specs=[pl.BlockSpec(memory_space=pltpu.SMEM)] + [vmem] * (1 + 3 * n),
        out_specs=[vmem] * len(out_shape),
        out_shape=out_shape,
    )(me, parts, *ws, *ms, *vs)
    return res[0], [res[1 + 4 * k:5 + 4 * k] for k in range(n)]


ANY = pl.BlockSpec(memory_space=pl.ANY)
FLIPS = [(k >> 2 & 1, k >> 1 & 1, k & 1) for k in range(1, N_DEV)]


def _place():
    return lax.axis_index("x"), lax.axis_index("y"), lax.axis_index("c")


HBM = pl.BlockSpec(memory_space=pltpu.HBM)
SEM = pl.BlockSpec(memory_space=pltpu.SEMAPHORE)
EFFECT = pltpu.SideEffectType.DATAFLOW_SIDE_EFFECTING


def _peer_copy(gather, src_ref, land_ref, send_sems, recv_sems, k, sending):
    x, y, c = _place()
    fx, fy, fc = FLIPS[k]
    me = 4 * x + 2 * y + c
    peer = 4 * (x ^ fx) + 2 * (y ^ fy) + (c ^ fc)
    return pltpu.make_async_remote_copy(
        src_ref=src_ref if gather else src_ref.at[peer],
        dst_ref=land_ref.at[me if sending else peer],
        send_sem=send_sems.at[k], recv_sem=recv_sems.at[k],
        device_id=(x ^ fx, y ^ fy, c ^ fc), device_id_type=MESH)


SIBLING = 0
OTHER_CHIPS = (1, 3, 5)


def _gather_start(srcs, name, via_sibling):
    n = len(srcs)
    lands = [lax.empty((N_DEV,) + s.shape, s.dtype) for s in srcs]

    def body(*refs):
        src_refs, land_refs = refs[:n], refs[n:2 * n]
        send, recv = refs[2 * n:3 * n], refs[3 * n:4 * n]
        for i in range(n):
            for k in (SIBLING,) + OTHER_CHIPS if via_sibling else range(N_DEV - 1):
                _peer_copy(True, src_refs[i], land_refs[i], send[i], recv[i], k, True).start()

    sem = pltpu.SemaphoreType.DMA((N_DEV - 1,))
    hbm = lambda a: pltpu.HBM(a.shape, a.dtype)
    res = pl.pallas_call(
        body,
        name=name,
        in_specs=[HBM] * (2 * n),
        out_specs=[SEM] * (2 * n) + [HBM] * (2 * n),
        out_shape=[sem] * (2 * n) + [hbm(s) for s in srcs] + [hbm(a) for a in lands],
        input_output_aliases={i: 2 * n + i for i in range(2 * n)},
        compiler_params=pltpu.CompilerParams(has_side_effects=EFFECT),
    )(*[pltpu.with_memory_space_constraint(a, pltpu.HBM) for a in list(srcs) + lands])
    return res[:n], res[n:2 * n], res[2 * n:3 * n], res[3 * n:4 * n]


def _exchange_wait(send_sems, recv_sems, src, land, after, gather, name):
    def body(src_ref, land_ref, send_ref, recv_ref, after_ref, src_out, land_out):
        for k in range(N_DEV - 1):
            cp = _peer_copy(gather, src_ref, land_ref, send_ref, recv_ref, k, False)
            cp.wait_send()
            cp.wait_recv()

    hbm = lambda a: pltpu.HBM(a.shape, a.dtype)
    return pl.pallas_call(
        body,
        name=name,
        in_specs=[HBM, HBM, SEM, SEM, ANY],
        out_specs=[HBM, HBM],
        out_shape=[hbm(src), hbm(land)],
        input_output_aliases={0: 0, 1: 1},
        compiler_params=pltpu.CompilerParams(has_side_effects=EFFECT),
    )(src, land, send_sems, recv_sems, after)


def _own_slot(src, land, me, gather):
    own = src[None] if gather else lax.dynamic_slice_in_dim(src, me, 1, axis=0)
    return lax.dynamic_update_slice_in_dim(land, own, me, axis=0)


def _forwarded_copy(land_ref, send_sems, recv_sems, j, sending):
    x, y, c = _place()
    fx, fy, _ = FLIPS[OTHER_CHIPS[j]]
    slot = 4 * (x ^ fx) + 2 * (y ^ fy) + (c if sending else 1 - c)
    return pltpu.make_async_remote_copy(
        src_ref=land_ref.at[slot], dst_ref=land_ref.at[slot], send_sem=send_sems.at[j], recv_sem=recv_sems.at[j],
        device_id=(x, y, 1 - c), device_id_type=MESH)


def _gather_forward(send_sems, recv_sems, srcs, lands, after, name):
    n = len(srcs)

    def body(*refs):
        src_refs, land_refs = refs[:n], refs[n:2 * n]
        send, recv = refs[2 * n:3 * n], refs[3 * n:4 * n]
        fwd_send, fwd_recv = refs[4 * n + 1:5 * n + 1], refs[5 * n + 1:6 * n + 1]
        for i in range(n):
            for j, k in enumerate(OTHER_CHIPS):
                _peer_copy(True, src_refs[i], land_refs[i], send[i], recv[i], k, False).wait_recv()
                _forwarded_copy(land_refs[i], fwd_send[i], fwd_recv[i], j, True).start()

    sem = pltpu.SemaphoreType.DMA((len(OTHER_CHIPS),))
    hbm = lambda a: pltpu.HBM(a.shape, a.dtype)
    res = pl.pallas_call(
        body,
        name=name,
        in_specs=[HBM] * (2 * n) + [SEM] * (2 * n) + [ANY],
        out_specs=[SEM] * (2 * n) + [HBM] * (2 * n),
        out_shape=[sem] * (2 * n) + [hbm(a) for a in srcs] + [hbm(a) for a in lands],
        input_output_aliases={i: 2 * n + i for i in range(2 * n)},
        compiler_params=pltpu.CompilerParams(has_side_effects=EFFECT),
    )(*srcs, *lands, *send_sems, *recv_sems, after)
    return res[:n], res[n:2 * n], res[2 * n:3 * n], res[3 * n:4 * n]


def _gather_wait_forwarded(send_sems, recv_sems, fwd_send, fwd_recv, src, land, after, name):
    def body(src_ref, land_ref, send_ref, recv_ref, fwd_send_ref, fwd_recv_ref, after_ref, src_out, land_out):
        for k in (SIBLING,) + OTHER_CHIPS:
            _peer_copy(True, src_ref, land_ref, send_ref, recv_ref, k, False).wait_send()
        _peer_copy(True, src_ref, land_ref, send_ref, recv_ref, SIBLING, False).wait_recv()
        for j in range(len(OTHER_CHIPS)):
            _forwarded_copy(land_ref, fwd_send_ref, fwd_recv_ref, j, True).wait_send()
            _forwarded_copy(land_ref, fwd_send_ref, fwd_recv_ref, j, False).wait_recv()

    hbm = lambda a: pltpu.HBM(a.shape, a.dtype)
    return pl.pallas_call(
        body,
        name=name,
        in_specs=[HBM, HBM, SEM, SEM, SEM, SEM, ANY],
        out_specs=[HBM, HBM],
        out_shape=[hbm(src), hbm(land)],
        input_output_aliases={0: 0, 1: 1},
        compiler_params=pltpu.CompilerParams(has_side_effects=EFFECT),
    )(src, land, send_sems, recv_sems, fwd_send, fwd_recv, after)


N_CHIPS = N_DEV // 2


def _pair_exchange(by_core, meanwhile, name):
    def copy(src_ref, land_ref, send_sems, recv_sems, q):
        x, y, c = _place()
        return pltpu.make_async_remote_copy(
            src_ref=src_ref.at[q, 1 - c], dst_ref=land_ref.at[q], send_sem=send_sems.at[q], recv_sem=recv_sems.at[q],
            device_id=(x, y, 1 - c), device_id_type=MESH)

    def start(src_ref, land_ref, send_sems, recv_sems, src_out, land_out):
        for q in range(N_CHIPS):
            copy(src_ref, land_ref, send_sems, recv_sems, q).start()

    def wait(src_ref, land_ref, send_sems, recv_sems, after_ref, src_out, land_out):
        for q in range(N_CHIPS):
            cp = copy(src_ref, land_ref, send_sems, recv_sems, q)
            cp.wait_send()
            cp.wait_recv()

    sem = pltpu.SemaphoreType.DMA((N_CHIPS,))
    hbm_src = pltpu.HBM(by_core.shape, by_core.dtype)
    hbm_land = pltpu.HBM(by_core.shape[:1] + by_core.shape[2:], by_core.dtype)
    params = pltpu.CompilerParams(has_side_effects=EFFECT)
    send_sems, recv_sems, src, land = pl.pallas_call(
        start, name=name + "_start", in_specs=[HBM, HBM], out_specs=[SEM, SEM, HBM, HBM],
        out_shape=[sem, sem, hbm_src, hbm_land], input_output_aliases={0: 2, 1: 3}, compiler_params=params,
    )(pltpu.with_memory_space_constraint(by_core, pltpu.HBM),
      pltpu.with_memory_space_constraint(lax.empty(hbm_land.shape, by_core.dtype), pltpu.HBM))
    return pl.pallas_call(
        wait, name=name + "_wait", in_specs=[HBM, HBM, SEM, SEM, ANY], out_specs=[HBM, HBM],
        out_shape=[hbm_src, hbm_land], input_output_aliases={0: 0, 1: 1}, compiler_params=params,
    )(src, land, send_sems, recv_sems, meanwhile(src))


def _pair_add(by_core, landed, name):
    q, _, r, c = by_core.shape
    tc = _tile(c, (512, 256, 128))

    def body(a_ref, b_ref, o_ref):
        mine = a_ref[0, lax.axis_index("c")]
        o_ref[0] = (mine.astype(F32) + b_ref[0].astype(F32)).astype(BF16)

    blk = pl.BlockSpec((1, r, tc), lambda i, j: (i, 0, j))
    return pl.pallas_call(
        body, name=name, grid=(q, c // tc),
        in_specs=[pl.BlockSpec((1, 2, r, tc), lambda i, j: (i, 0, 0, j)), blk], out_specs=blk,
        out_shape=jax.ShapeDtypeStruct(landed.shape, BF16), compiler_params=_cparams("parallel", "parallel"),
    )(by_core, landed)


def _chip_copy(src_ref, land_ref, send_sems, recv_sems, j, sending):
    x, y, c = _place()
    fx, fy, _ = FLIPS[OTHER_CHIPS[j]]
    here, there = 2 * x + y, 2 * (x ^ fx) + (y ^ fy)
    return pltpu.make_async_remote_copy(
        src_ref=src_ref.at[there], dst_ref=land_ref.at[here if sending else there],
        send_sem=send_sems.at[j], recv_sem=recv_sems.at[j],
        device_id=(x ^ fx, y ^ fy, c), device_id_type=MESH)


def _chip_start(sums, name):
    def start(src_ref, land_ref, send_sems, recv_sems, src_out, land_out):
        for j in range(len(OTHER_CHIPS)):
            _chip_copy(src_ref, land_ref, send_sems, recv_sems, j, True).start()

    sem = pltpu.SemaphoreType.DMA((len(OTHER_CHIPS),))
    hbm = pltpu.HBM(sums.shape, sums.dtype)
    handles = pl.pallas_call(
        start, name=name, in_specs=[HBM, HBM], out_specs=[SEM, SEM, HBM, HBM], out_shape=[sem, sem, hbm, hbm],
        input_output_aliases={0: 2, 1: 3}, compiler_params=pltpu.CompilerParams(has_side_effects=EFFECT),
    )(pltpu.with_memory_space_constraint(sums, pltpu.HBM),
      pltpu.with_memory_space_constraint(lax.empty(sums.shape, sums.dtype), pltpu.HBM))
    return handles[2], tuple(handles)


def _chip_wait(send_sems, recv_sems, src, land, after, name):
    def body(src_ref, land_ref, send_ref, recv_ref, after_ref, src_out, land_out):
        for j in range(len(OTHER_CHIPS)):
            cp = _chip_copy(src_ref, land_ref, send_ref, recv_ref, j, False)
            cp.wait_send()
            cp.wait_recv()

    hbm = lambda a: pltpu.HBM(a.shape, a.dtype)
    return pl.pallas_call(
        body,
        name=name,
        in_specs=[HBM, HBM, SEM, SEM, ANY],
        out_specs=[HBM, HBM],
        out_shape=[hbm(src), hbm(land)],
        input_output_aliases={0: 0, 1: 1},
        compiler_params=pltpu.CompilerParams(has_side_effects=EFFECT),
    )(src, land, send_sems, recv_sems, after)


def _col(v):
    return v.reshape(-1, 1).astype(F32)


def _local_step(x, tgt, started, weight, small, pair_sums, handles):
    t = x.shape[0]
    n1 = _col(small["norm1_w"])
    n2 = _col(small["norm2_w"])
    nf = _col(small["final_norm_w"])
    bg = _col(small["b_gate"])
    sinks = small["attn_sinks"].reshape(-1).astype(F32)
    cbias = _col(small["ssd_conv_b"])
    dtb = _col(small["dt_bias"])
    alog = _col(small["a_log"])
    dsk = _col(small["d_skip"])
    gnw = _col(small["ssd_norm_w"])
    fb = small["ffn_conv_b"].reshape(2, D_FF, 1)

    xt, xn = _norm_fwd_tokens(x, n1, started, "norm1_fwd")
    cw = weight("ssd_conv_w", xn).T
    fw = weight("ffn_conv_w", xn).T.reshape(2, D_FF, FFN_CONV)
    w_in_t = weight("w_in", xn)
    proj = _matmul(w_in_t, xn, nt=False, out_dtype=F32, name="mm_in")
    ao, lse = _attn_fwd(proj, sinks)
    w_ao = weight("w_attn_o", ao)
    attn = _matmul(w_ao, ao, nt=False, out_dtype=F32, name="mm_attn_o", tn_a=True)
    xbc = _conv_silu_fwd(proj, cw, cbias)
    y, hst, yn = _ssd_fwd(xbc, proj, dtb, alog, dsk, gnw)
    w_so = weight("w_ssd_o", yn)
    ssd = _matmul(w_so, yn, nt=False, out_dtype=F32, name="mm_ssd_o", tn_a=True)
    mix = _gate_fwd(proj, bg, attn, ssd)
    w_out = weight("w_out", mix)
    h1 = _matmul(w_out, mix, nt=False, out_dtype=F32, name="mm_out", add=xt, tn_a=True)
    hn = _norm_fwd(h1, n2, "norm2_fwd")
    w_up_t = weight("w_up", hn)
    u0 = _matmul(w_up_t, hn, nt=False, out_dtype=F32, name="mm_up").reshape(2, D_FF, t)
    gl = _ffn_fwd(u0, fw, fb)
    w_down = weight("w_down", gl)
    dh2, loss, d_nf = _matmul(w_down, gl, nt=False, out_dtype=F32, name="mm_down_final_norm_loss", add=h1, tn_a=True,
                              loss=(tgt, nf))

    g = {}

    def sending(weight_name, grad, fn, *args, **kwargs):
        chunks = grad if grad.ndim == 3 else grad.reshape(N_DEV, -1, D_MODEL)
        out, handles[weight_name] = fn(*args, send=chunks, **kwargs)
        return out

    g_down = _matmul(gl, dh2, nt=True, out_dtype=BF16, name="mm_d_w_down")
    dgl = _matmul(w_down, dh2, nt=False, out_dtype=F32, name="mm_d_glu")
    du0, d_fwb = sending("w_down", g_down, _ffn_bwd, u0, fw, fb, dgl)
    du0 = du0.reshape(2 * D_FF, t)
    g_up = _matmul(du0, hn, nt=True, out_dtype=BF16, name="mm_d_w_up")
    dh1, d_n2 = sending("w_up", g_up, _matmul, w_up_t, du0, nt=False, out_dtype=F32, name="mm_d_hn_norm2_bwd", tn_a=True,
                        norm_bwd=(h1, n2, dh2, False))
    g_out = _matmul(mix, dh1, nt=True, out_dtype=BF16, name="mm_d_w_out")
    dmix = _matmul(w_out, dh1, nt=False, out_dtype=F32, name="mm_d_mix")
    d_attn, d_ssd, d_ga, d_gs, d_ba, d_bs = sending("w_out", g_out, _gate_bwd, proj, bg, attn, ssd, dmix)
    g_ao = _matmul(ao, d_attn, nt=True, out_dtype=BF16, name="mm_d_w_attn_o")
    dao = _matmul(w_ao, d_attn, nt=False, out_dtype=F32, name="mm_d_ao")
    g_so = _matmul(yn, d_ssd, nt=True, out_dtype=BF16, name="mm_d_w_ssd_o")
    dyn = _matmul(w_so, d_ssd, nt=False, out_dtype=F32, name="mm_d_yn")
    dxbc, ddt, d_alog, d_dsk, d_dtb, dproj, d_gnw = sending(
        "w_ssd_o", g_so, _ssd_bwd, xbc, proj, dtb, alog, dsk, hst, y, dyn, gnw)
    dproj, dwb_conv = _conv_silu_bwd(proj, cw, cbias, dxbc, dproj)
    dproj, d_sinks = sending("w_attn_o", g_ao, _attn_bwd, proj, sinks, ao, lse, dao, dproj)
    for rows, part in ((OFF_DT, ddt.astype(BF16)), (OFF_GA, d_ga), (OFF_GS, d_gs)):
        dproj = lax.dynamic_update_slice(dproj, part, (rows, 0))
    leaving = pair_sums(_matmul(dproj, xn, nt=True, out_dtype=BF16, name="mm_d_w_in"))
    dx, d_n1 = _matmul(w_in_t, dproj, nt=False, out_dtype=F32, name="mm_d_xn_norm1_bwd", tn_a=True,
                       norm_bwd=(xt, n1, dh1, True), after=leaving)

    g["norm1_w"] = d_n1
    g["b_gate"] = jnp.concatenate([d_ba, d_bs], axis=0)
    g["attn_sinks"] = d_sinks
    g["ssd_conv_w"] = dwb_conv[:, :SSD_CONV].T
    g["ssd_conv_b"] = dwb_conv[:, SSD_CONV]
    g["dt_bias"] = d_dtb
    g["a_log"] = d_alog
    g["d_skip"] = d_dsk
    g["ssd_norm_w"] = d_gnw
    g["norm2_w"] = d_n2
    d_fwb = d_fwb.reshape(2 * D_FF, 128)
    g["ffn_conv_w"] = d_fwb[:, :FFN_CONV].T
    g["ffn_conv_b"] = d_fwb[:, FFN_CONV]
    g["final_norm_w"] = d_nf
    return loss, dx, g


SMALL = ("norm1_w", "b_gate", "attn_sinks", "ssd_conv_w", "ssd_conv_b", "dt_bias", "a_log", "d_skip", "ssd_norm_w",
         "norm2_w", "ffn_conv_w", "ffn_conv_b", "final_norm_w")
WEIGHT_ORDER = ("norm1_w", "w_in", "b_gate", "attn_sinks", "w_attn_o", "ssd_conv_w", "ssd_conv_b", "dt_bias", "a_log",
                "d_skip", "ssd_norm_w", "w_ssd_o", "w_out", "norm2_w", "w_up", "ffn_conv_w", "ffn_conv_b", "w_down",
                "final_norm_w")


def kernel(x, norm1_w, w_in, b_gate, attn_sinks, w_attn_o, ssd_conv_w, ssd_conv_b, dt_bias, a_log, d_skip, ssd_norm_w, w_ssd_o, w_out, norm2_w, w_up, ffn_conv_w, ffn_conv_b, w_down, final_norm_w, loss_target, m_norm1_w, m_w_in, m_b_gate, m_attn_sinks, m_w_attn_o, m_ssd_conv_w, m_ssd_conv_b, m_dt_bias, m_a_log, m_d_skip, m_ssd_norm_w, m_w_ssd_o, m_w_out, m_norm2_w, m_w_up, m_ffn_conv_w, m_ffn_conv_b, m_w_down, m_final_norm_w, v_norm1_w, v_w_in, v_b_gate, v_attn_sinks, v_w_attn_o, v_ssd_conv_w, v_ssd_conv_b, v_dt_bias, v_a_log, v_d_skip, v_ssd_norm_w, v_w_ssd_o, v_w_out, v_norm2_w, v_w_up, v_ffn_conv_w, v_ffn_conv_b, v_w_down, v_final_norm_w):
    w = dict(norm1_w=norm1_w, w_in=w_in, b_gate=b_gate, attn_sinks=attn_sinks, w_attn_o=w_attn_o, ssd_conv_w=ssd_conv_w, ssd_conv_b=ssd_conv_b, dt_bias=dt_bias, a_log=a_log, d_skip=d_skip, ssd_norm_w=ssd_norm_w, w_ssd_o=w_ssd_o, w_out=w_out, norm2_w=norm2_w, w_up=w_up, ffn_conv_w=ffn_conv_w, ffn_conv_b=ffn_conv_b, w_down=w_down, final_norm_w=final_norm_w)
    m = dict(norm1_w=m_norm1_w, w_in=m_w_in, b_gate=m_b_gate, attn_sinks=m_attn_sinks, w_attn_o=m_w_attn_o, ssd_conv_w=m_ssd_conv_w, ssd_conv_b=m_ssd_conv_b, dt_bias=m_dt_bias, a_log=m_a_log, d_skip=m_d_skip, ssd_norm_w=m_ssd_norm_w, w_ssd_o=m_w_ssd_o, w_out=m_w_out, norm2_w=m_norm2_w, w_up=m_w_up, ffn_conv_w=m_ffn_conv_w, ffn_conv_b=m_ffn_conv_b, w_down=m_w_down, final_norm_w=m_final_norm_w)
    v = dict(norm1_w=v_norm1_w, w_in=v_w_in, b_gate=v_b_gate, attn_sinks=v_attn_sinks, w_attn_o=v_w_attn_o, ssd_conv_w=v_ssd_conv_w, ssd_conv_b=v_ssd_conv_b, dt_bias=v_dt_bias, a_log=v_a_log, d_skip=v_d_skip, ssd_norm_w=v_ssd_norm_w, w_ssd_o=v_w_ssd_o, w_out=v_w_out, norm2_w=v_norm2_w, w_up=v_w_up, ffn_conv_w=v_ffn_conv_w, ffn_conv_b=v_ffn_conv_b, w_down=v_w_down, final_norm_w=v_final_norm_w)
    me = 4 * lax.axis_index("x") + 2 * lax.axis_index("y") + lax.axis_index("c")

    shards = {"ssd_conv_w": ssd_conv_w[0], "ffn_conv_w": ffn_conv_w[0], "w_in": w_in[0].T.astype(BF16),
              "w_attn_o": w_attn_o[0].astype(BF16), "w_ssd_o": w_ssd_o[0].astype(BF16), "w_out": w_out[0].astype(BF16),
              "w_up": w_up[0].T.astype(BF16), "w_down": w_down[0].astype(BF16)}
    order = list(shards)
    g_send, g_recv, g_src, g_land = _gather_start(list(shards.values()), "gather_start", True)
    first = ("ssd_conv_w", "ffn_conv_w", "w_in")
    forwarded = {}

    def weight(name, after):
        if name not in forwarded:
            group = [k for k in order if (k in first) == (name in first)]
            idx = [order.index(k) for k in group]
            handles = _gather_forward([g_send[i] for i in idx], [g_recv[i] for i in idx], [g_src[i] for i in idx],
                                      [g_land[i] for i in idx], after, "gather_forward_for_" + name)
            forwarded.update(zip(group, zip(*handles)))
        i = order.index(name)
        src, land = _gather_wait_forwarded(g_send[i], g_recv[i], *forwarded[name], after, "gather_wait_" + name)
        land = _own_slot(src, land, me, True)
        if name == "ssd_conv_w":
            return jnp.transpose(land, (1, 0, 2)).reshape(SSD_CONV, XBC_DIM)
        if name == "ffn_conv_w":
            return jnp.transpose(land, (1, 0, 2)).reshape(FFN_CONV, 2 * D_FF)
        return land.reshape(-1, D_MODEL)

    res, pending = {}, {}

    def update(name, after):
        if name == "w_in":
            parts = _own_slot(*_chip_wait(*pending[name], after, "grad_wait_" + name), me // 2, False)
        else:
            parts = _own_slot(*_exchange_wait(*pending[name], after, False, "grad_wait_" + name), me, False)
        view, back = {
            "w_in": (lambda a: jnp.transpose(a, (2, 0, 1)), lambda r: jnp.transpose(r, (1, 2, 0))),
            "w_up": (lambda a: a[0].T, lambda r: r.T[None]),
        }.get(name, (lambda a: a[0], lambda r: r[None]))
        done = _adamw_sharded(parts, view(w[name]), view(m[name]), view(v[name]), "adamw_" + name)
        res[name] = [back(r) for r in done]
        return done[0]

    def pair_sums(grad):
        by_core, landed = _pair_exchange(grad.reshape(N_CHIPS, 2, -1, D_MODEL),
                                         lambda started: update("w_up", update("w_down", started)), "grad_pair_w_in")
        leaving, pending["w_in"] = _chip_start(_pair_add(by_core, landed, "grad_pair_add_w_in"), "grad_chips_w_in_start")
        return leaving

    small = {k: w[k][0] if k != "final_norm_w" else w[k] for k in SMALL}
    loss, dx, g = _local_step(x[0], loss_target[0], g_src[0], weight, small, pair_sums, pending)

    packed = _pack_row([loss] + [g[k] for k in SMALL])
    s_send, s_recv, s_src, s_land = _gather_start([packed], "small_grads_start", False)
    after = s_src[0]
    for name in ("w_out", "w_attn_o", "w_ssd_o", "w_in"):
        after = update(name, after)

    rows = _own_slot(*_exchange_wait(s_send[0], s_recv[0], s_src[0], s_land[0], after, True, "small_grads_wait"),
                     me, True)
    flat = lambda a: a.reshape(-1, a.shape[-1])
    loss_sum, updates = _small_update(
        rows, me.reshape(1), [g[k].size for k in SMALL],
        [flat(w[k]) for k in SMALL], [flat(m[k]) for k in SMALL], [flat(v[k]) for k in SMALL])
    for k, upd in zip(SMALL, updates):
        res[k] = [u.reshape(w[k].shape) for u in upd]

    grad_x = dx[None]
    outs = [loss_sum.reshape(()), grad_x]
    for i in range(4):
        outs.extend(res[k][i] for k in WEIGHT_ORDER)
    return tuple(outs)
```

```python
import jax
import jax.numpy as jnp
from jax import lax
from jax.experimental import pallas as pl
from jax.experimental.pallas import tpu as pltpu

F32 = jnp.float32
BF16 = jnp.bfloat16
HIGHEST = lax.Precision.HIGHEST

D_MODEL = 1024
N_Q_HEADS = 16
N_KV_HEADS = 4
HEAD_DIM = 64
WINDOW = 128
Q_PER_KV = N_Q_HEADS // N_KV_HEADS
Q_DIM = N_Q_HEADS * HEAD_DIM
KV_DIM = N_KV_HEADS * HEAD_DIM
D_INNER = 2048
SSD_HEAD_DIM = 64
N_SSD_HEADS = 32
N_SSD_GROUPS = 4
HEADS_PER_GROUP = N_SSD_HEADS // N_SSD_GROUPS
D_STATE = 128
GN_ROWS = D_INNER // N_SSD_GROUPS
BC_DIM = N_SSD_GROUPS * D_STATE
XBC_DIM = D_INNER + 2 * BC_DIM
SSD_CONV = 4
CHUNK = 128
D_FF = 2816
FFN_CONV = 3
EPS = 1e-5
NEG = -1e30
IN_DIM = 8736
N_DEV = 8

OFF_Q = 0
OFF_K = OFF_Q + Q_DIM
OFF_V = OFF_K + KV_DIM
OFF_Z = OFF_V + KV_DIM
OFF_X = OFF_Z + D_INNER
OFF_DT = OFF_X + XBC_DIM
OFF_GA = OFF_DT + N_SSD_HEADS
OFF_GS = OFF_GA + D_MODEL

ADAM_LR = 0.001
ADAM_B1 = 0.9
ADAM_B2 = 0.999
ADAM_EPS = 1e-08
ADAM_WD = 0.01
ADAM_STEP = 10

LANES = 128
BF16_TILE_ROWS = 16
VMEM_BYTES = 64 * 1024 * 1024
VMEM_LIMIT = VMEM_BYTES * 3 // 4
MESH = pl.DeviceIdType.MESH


def _cparams(*sem):
    return pltpu.CompilerParams(dimension_semantics=sem, vmem_limit_bytes=VMEM_LIMIT)


def _tile(n, prefs):
    for p in prefs:
        if n % p == 0:
            return p
    return n


def _sigmoid(x):
    return 1.0 / (1.0 + jnp.exp(-x))


def _softplus(x):
    return jnp.maximum(x, 0.0) + jnp.log(1.0 + jnp.exp(-jnp.abs(x)))


def _rowsum(x):
    return jnp.sum(x, axis=1, keepdims=True)


def _colsum(x):
    return jnp.sum(x, axis=0, keepdims=True)


def _dot(a, b):
    return jnp.dot(a, b, preferred_element_type=F32)


def _dot_nt(a, b):
    return lax.dot_general(a, b, (((1,), (1,)), ((), ())), preferred_element_type=F32)


def _dot_tn(a, b):
    return lax.dot_general(a, b, (((0,), (0,)), ((), ())), preferred_element_type=F32)


def _shift_right(x, j):
    if j == 0:
        return x
    r = pltpu.roll(x, j, 1)
    lane = lax.broadcasted_iota(jnp.int32, (x.shape[0], 128), 1)
    return jnp.concatenate([jnp.where(lane >= j, r[:, :128], 0.0), r[:, 128:]], axis=1)


def _shift_left(x, j):
    if j == 0:
        return x
    n = x.shape[1]
    r = pltpu.roll(x, n - j, 1)
    lane = lax.broadcasted_iota(jnp.int32, (x.shape[0], 128), 1)
    return jnp.concatenate([r[:, :n - 128], jnp.where(lane < 128 - j, r[:, n - 128:], 0.0)], axis=1)


def _causal_conv(xv, wv, bv):
    taps = wv.shape[1]
    shifted = [_shift_right(xv, taps - 1 - k) for k in range(taps - 1)]
    y = bv + wv[:, taps - 1:taps] * xv
    for k in range(taps - 1):
        y = y + wv[:, k:k + 1] * shifted[k]
    return y, shifted


def _causal_conv_bwd(dy, xv, shifted, wv):
    taps = wv.shape[1]
    lane = lax.broadcasted_iota(jnp.int32, (dy.shape[0], 128), 1)
    dwb = jnp.where(lane == taps, _rowsum(dy), 0.0)
    dwb = jnp.where(lane == taps - 1, _rowsum(dy * xv), dwb)
    dx = wv[:, taps - 1:taps] * dy
    for k in range(taps - 1):
        dx = dx + wv[:, k:k + 1] * _shift_left(dy, taps - 1 - k)
        dwb = jnp.where(lane == k, _rowsum(dy * shifted[k]), dwb)
    return dx, dwb


def _call(body, *, name, grid, in_specs, out_specs, out_shape, args, semantics, scratch_shapes=(), aliases=None,
          send=None):
    aliases = dict(aliases or {})
    if send is None:
        return pl.pallas_call(body, name=name, grid=grid, in_specs=in_specs, out_specs=out_specs, out_shape=out_shape,
                              scratch_shapes=list(scratch_shapes), input_output_aliases=aliases,
                              compiler_params=_cparams(*semantics))(*args)
    single = not isinstance(out_specs, (list, tuple))
    out_specs, out_shape = ([out_specs], [out_shape]) if single else (list(out_specs), list(out_shape))
    n_in, n_out = len(in_specs), len(out_specs)
    n_copies = N_DEV - 1

    def sending(*refs):
        ins, (src_ref, land_ref) = refs[:n_in], refs[n_in:n_in + 2]
        outs = refs[n_in + 2:n_in + 2 + n_out]
        send_sems, recv_sems = refs[n_in + 2 + n_out:n_in + 4 + n_out]
        scratch = refs[n_in + 6 + n_out:]
        step = 0
        for axis, size in enumerate(grid):
            step = step * size + pl.program_id(axis)

        @pl.when(step == 0)
        def _():
            for k in range(n_copies):
                _peer_copy(False, src_ref, land_ref, send_sems, recv_sems, k, True).start()

        body(*ins, *outs, *scratch)

    sem = pltpu.SemaphoreType.DMA((n_copies,))
    hbm = pltpu.HBM(send.shape, send.dtype)
    res = pl.pallas_call(
        sending, name=name, grid=grid,
        in_specs=list(in_specs) + [HBM, HBM],
        out_specs=out_specs + [SEM, SEM, HBM, HBM],
        out_shape=out_shape + [sem, sem, hbm, hbm],
        input_output_aliases={**aliases, n_in: n_out + 2, n_in + 1: n_out + 3},
        scratch_shapes=list(scratch_shapes),
        compiler_params=pltpu.CompilerParams(dimension_semantics=("arbitrary",) * len(grid), vmem_limit_bytes=VMEM_LIMIT,
                                             has_side_effects=EFFECT),
    )(*args, pltpu.with_memory_space_constraint(send, pltpu.HBM),
      pltpu.with_memory_space_constraint(lax.empty(send.shape, send.dtype), pltpu.HBM))
    return (res[0] if single else list(res[:n_out])), tuple(res[n_out:])


BLOCK_VMEM_BUDGET = VMEM_LIMIT * 3 // 4
MATMUL_MAX_TM = 768
MATMUL_MAX_TN = 3072
MATMUL_MAX_TK = 3072


def _largest_tile(n, align, cap):
    return max(d for d in range(align, min(n, cap) + 1, align) if n % d == 0)


def _matmul_tiles(m, n, k, a_bytes, b_bytes, out_bytes, f32_blocks, m_align, k_align, whole_m):
    tm = m if whole_m else _largest_tile(m, m_align, MATMUL_MAX_TM)
    tk = _largest_tile(k, k_align, MATMUL_MAX_TK)
    for tn in sorted({d for d in range(LANES, min(n, MATMUL_MAX_TN) + 1, LANES) if n % d == 0}, reverse=True):
        need = 2 * (tm * tk * a_bytes + tk * tn * b_bytes) + tm * tn * (2 * out_bytes + (4 if k > tk else 0) + 8 * f32_blocks)
        if need <= BLOCK_VMEM_BUDGET:
            return tm, tn, tk
    return tm, LANES, tk


def _norm_bwd_math(dy, x, w, res):
    r = lax.rsqrt(jnp.mean(x * x, axis=0, keepdims=True) + EPS)
    xhat = x * r
    dxhat = dy * w
    return res + r * (dxhat - xhat * jnp.mean(dxhat * xhat, axis=0, keepdims=True)), _rowsum(dy * xhat)


def _matmul(a, b, *, nt, out_dtype, name, add=None, tn_a=False, send=None, norm_bwd=None, after=None):
    if tn_a:
        k, m = a.shape
    else:
        m, k = a.shape
    n = b.shape[0] if nt else b.shape[1]
    tokens_out = norm_bwd is not None and norm_bwd[3]
    tm, tn, tk = _matmul_tiles(m, n, k, a.dtype.itemsize, b.dtype.itemsize, jnp.dtype(out_dtype).itemsize,
                               (add is not None) + 2 * (norm_bwd is not None),
                               LANES if tn_a or tokens_out else BF16_TILE_ROWS,
                               BF16_TILE_ROWS if tn_a and not nt else LANES, norm_bwd is not None)
    nk = k // tk
    grid = (m // tm, n // tn, nk)
    n_extra = (add is not None) + (after is not None) + 3 * (norm_bwd is not None)
    n_out = 1 + (norm_bwd is not None)

    def body(a_ref, b_ref, *rest):
        extra, outs, scratch = rest[:n_extra], rest[n_extra:n_extra + n_out], rest[n_extra + n_out:]
        av = a_ref[...].astype(BF16)
        bv = b_ref[...].astype(BF16)
        part = _dot_tn(av, bv) if tn_a else _dot_nt(av, bv) if nt else _dot(av, bv)

        def finish(r):
            if add is not None:
                r = r + extra[0][...]
            if norm_bwd is not None:
                x_ref, w_ref, res_ref = extra[-3:]
                dx, dw = _norm_bwd_math(r, x_ref[...], w_ref[...], res_ref[...])
                outs[1][...] += dw
                r = dx.T if tokens_out else dx
            outs[0][...] = r.astype(out_dtype)

        if norm_bwd is not None:
            @pl.when((pl.program_id(1) == 0) & (pl.program_id(2) == 0))
            def _():
                outs[1][...] = jnp.zeros_like(outs[1])

        if nk == 1:
            finish(part)
            return
        acc = scratch[0]
        kk = pl.program_id(2)

        @pl.when(kk == 0)
        def _():
            acc[...] = part

        @pl.when((kk > 0) & (kk < nk - 1))
        def _():
            acc[...] += part

        @pl.when(kk == nk - 1)
        def _():
            finish(acc[...] + part)

    tile = pl.BlockSpec((tm, tn), lambda i, j, kk: (i, j))
    in_specs = [
        pl.BlockSpec((tk, tm), lambda i, j, kk: (kk, i)) if tn_a else pl.BlockSpec((tm, tk), lambda i, j, kk: (i, kk)),
        pl.BlockSpec((tn, tk), lambda i, j, kk: (j, kk)) if nt else pl.BlockSpec((tk, tn), lambda i, j, kk: (kk, j)),
    ]
    args = [a, b]
    out_specs, out_shape = tile, jax.ShapeDtypeStruct((m, n), out_dtype)
    if add is not None:
        in_specs.append(tile)
        args.append(add)
    if after is not None:
        in_specs.append(ANY)
        args.append(after)
    if norm_bwd is not None:
        x, w_col, res, _ = norm_bwd
        col = pl.BlockSpec((m, 1), lambda i, j, kk: (0, 0))
        in_specs += [tile, col, tile]
        args += [x, w_col, res]
        if tokens_out:
            out_specs, out_shape = pl.BlockSpec((tn, tm), lambda i, j, kk: (j, i)), jax.ShapeDtypeStruct((n, m), out_dtype)
        out_specs, out_shape = [out_specs, col], [out_shape, jax.ShapeDtypeStruct((m, 1), F32)]
    return _call(
        body, name=name, grid=grid, in_specs=in_specs, args=args, out_specs=out_specs, out_shape=out_shape,
        scratch_shapes=[pltpu.VMEM((tm, tn), F32)] if nk > 1 else [],
        semantics=("parallel", "parallel" if norm_bwd is None else "arbitrary", "arbitrary"), send=send)


def _norm_fwd(x, w_col, name):
    f, t = x.shape
    tt = _tile(t, (512, 256, 128))

    def body(x_ref, w_ref, o_ref):
        xv = x_ref[...]
        r = lax.rsqrt(jnp.mean(xv * xv, axis=0, keepdims=True) + EPS)
        o_ref[...] = (xv * r * w_ref[...]).astype(BF16)

    return pl.pallas_call(
        body,
        name=name,
        grid=(t // tt,),
        in_specs=[pl.BlockSpec((f, tt), lambda i: (0, i)), pl.BlockSpec((f, 1), lambda i: (0, 0))],
        out_specs=pl.BlockSpec((f, tt), lambda i: (0, i)),
        out_shape=jax.ShapeDtypeStruct((f, t), BF16),
        compiler_params=_cparams("parallel"),
    )(x, w_col)


def _norm_fwd_tokens(x, w_col, after, name):
    t, f = x.shape
    tt = _tile(t, (512, 256, 128))

    def body(x_ref, w_ref, after_ref, xt_ref, o_ref):
        xv = x_ref[...].T
        xt_ref[...] = xv
        r = lax.rsqrt(jnp.mean(xv * xv, axis=0, keepdims=True) + EPS)
        o_ref[...] = (xv * r * w_ref[...]).astype(BF16)

    blk = pl.BlockSpec((f, tt), lambda i: (0, i))
    return pl.pallas_call(
        body,
        name=name,
        grid=(t // tt,),
        in_specs=[pl.BlockSpec((tt, f), lambda i: (i, 0)), pl.BlockSpec((f, 1), lambda i: (0, 0)), ANY],
        out_specs=[blk, blk],
        out_shape=[jax.ShapeDtypeStruct((f, t), F32), jax.ShapeDtypeStruct((f, t), BF16)],
        compiler_params=_cparams("parallel"),
    )(x, w_col, after)


def _final_norm_loss(h, tgt, w_col):
    f, t = h.shape
    tt = _tile(t, (512, 256, 128))

    def body(h_ref, t_ref, w_ref, dh_ref, loss_ref, dw_ref):
        @pl.when(pl.program_id(0) == 0)
        def _():
            dw_ref[...] = jnp.zeros_like(dw_ref)
            loss_ref[...] = jnp.zeros_like(loss_ref)

        xv = h_ref[...]
        r = lax.rsqrt(jnp.mean(xv * xv, axis=0, keepdims=True) + EPS)
        xhat = xv * r
        wv = w_ref[...]
        err = xhat * wv - t_ref[...].T
        loss_ref[...] += 0.5 * _rowsum(jnp.mean(err * err, axis=0, keepdims=True))
        dyv = err * (1.0 / f)
        dw_ref[...] += _rowsum(dyv * xhat)
        dxhat = dyv * wv
        dh_ref[...] = r * (dxhat - xhat * jnp.mean(dxhat * xhat, axis=0, keepdims=True))

    blk = pl.BlockSpec((f, tt), lambda i: (0, i))
    col = pl.BlockSpec((f, 1), lambda i: (0, 0))
    one = pl.BlockSpec((1, 1), lambda i: (0, 0))
    return pl.pallas_call(
        body,
        name="final_norm_loss",
        grid=(t // tt,),
        in_specs=[blk, pl.BlockSpec((tt, f), lambda i: (i, 0)), col],
        out_specs=[blk, one, col],
        out_shape=[jax.ShapeDtypeStruct((f, t), F32), jax.ShapeDtypeStruct((1, 1), F32), jax.ShapeDtypeStruct((f, 1), F32)],
        compiler_params=_cparams("arbitrary"),
    )(h, tgt, w_col)


def _attn_mask(n):
    shape = (2 * WINDOW, Q_PER_KV * WINDOW)
    si = lax.broadcasted_iota(jnp.int32, shape, 0)
    qi = lax.broadcasted_iota(jnp.int32, shape, 1) & (WINDOW - 1)
    dist = WINDOW + qi - si
    return (dist >= 0) & (dist < WINDOW) & ((si >= WINDOW) | (n > 0))


def _lane_cat(ref, row0, rows):
    return jnp.concatenate([ref[row0 + i * rows:row0 + (i + 1) * rows, :] for i in range(Q_PER_KV)], axis=1)


def _attn_fwd(proj, sinks):
    t = proj.shape[1]
    nb = t // WINDOW
    scale = HEAD_DIM ** -0.5

    def body(s_ref, q_ref, kc_ref, kp_ref, vc_ref, vp_ref, o_ref, lse_ref):
        n = pl.program_id(0)
        valid = _attn_mask(n)
        for g in range(N_KV_HEADS):
            rows = slice(g * HEAD_DIM, (g + 1) * HEAD_DIM)
            kt = jnp.concatenate([kp_ref[rows, :], kc_ref[rows, :]], axis=1).astype(BF16)
            vt = jnp.concatenate([vp_ref[rows, :], vc_ref[rows, :]], axis=1).astype(BF16)
            qcat = (_lane_cat(q_ref, g * Q_PER_KV * HEAD_DIM, HEAD_DIM) * scale).astype(BF16)
            s = jnp.where(valid, _dot_tn(kt, qcat), NEG)
            sink = jnp.concatenate(
                [jnp.full((1, WINDOW), s_ref[g * Q_PER_KV + i], F32) for i in range(Q_PER_KV)], axis=1)
            m = jnp.maximum(jnp.max(s, axis=0, keepdims=True), sink)
            p = jnp.exp(s - m)
            denom = _colsum(p) + jnp.exp(sink - m)
            probs = (p / denom).astype(BF16)
            out = _dot(vt, probs)
            lse = m + jnp.log(denom)
            for i in range(Q_PER_KV):
                h = g * Q_PER_KV + i
                o_ref[h * HEAD_DIM:(h + 1) * HEAD_DIM, :] = out[:, i * WINDOW:(i + 1) * WINDOW]
                lse_ref[h:h + 1, :] = lse[:, i * WINDOW:(i + 1) * WINDOW]

    kb = OFF_K // KV_DIM
    vb = OFF_V // KV_DIM
    prev = lambda n: jnp.maximum(n - 1, 0)
    return pl.pallas_call(
        body,
        name="attn_fwd",
        grid=(nb,),
        in_specs=[
            pl.BlockSpec(memory_space=pltpu.SMEM),
            pl.BlockSpec((Q_DIM, WINDOW), lambda n: (0, n)),
            pl.BlockSpec((KV_DIM, WINDOW), lambda n: (kb, n)),
            pl.BlockSpec((KV_DIM, WINDOW), lambda n: (kb, prev(n))),
            pl.BlockSpec((KV_DIM, WINDOW), lambda n: (vb, n)),
            pl.BlockSpec((KV_DIM, WINDOW), lambda n: (vb, prev(n))),
        ],
        out_specs=[pl.BlockSpec((Q_DIM, WINDOW), lambda n: (0, n)), pl.BlockSpec((N_Q_HEADS, WINDOW), lambda n: (0, n))],
        out_shape=[jax.ShapeDtypeStruct((Q_DIM, t), F32), jax.ShapeDtypeStruct((N_Q_HEADS, t), F32)],
        compiler_params=_cparams("parallel"),
    )(sinks, proj, proj, proj, proj, proj)


def _attn_bwd(proj, sinks, out, lse, dout, dproj, send=None):
    t = proj.shape[1]
    nb = t // WINDOW
    scale = HEAD_DIM ** -0.5

    def body(s_ref, q_ref, kc_ref, kp_ref, vc_ref, vp_ref, o_ref, lse_ref, do_ref, dproj_ref,
             dqkv_ref, ds_ref, dk_carry, dv_carry):
        dq_ref = dqkv_ref.at[pl.ds(OFF_Q, Q_DIM)]
        dk_ref = dqkv_ref.at[pl.ds(OFF_K, KV_DIM)]
        dv_ref = dqkv_ref.at[pl.ds(OFF_V, KV_DIM)]
        step = pl.program_id(0)
        n = nb - 1 - step

        @pl.when(step == 0)
        def _():
            dk_carry[...] = jnp.zeros_like(dk_carry)
            dv_carry[...] = jnp.zeros_like(dv_carry)
            ds_ref[...] = jnp.zeros_like(ds_ref)

        valid = _attn_mask(n)
        for g in range(N_KV_HEADS):
            rows = slice(g * HEAD_DIM, (g + 1) * HEAD_DIM)
            q0 = g * Q_PER_KV * HEAD_DIM
            kt = jnp.concatenate([kp_ref[rows, :], kc_ref[rows, :]], axis=1).astype(BF16)
            vt = jnp.concatenate([vp_ref[rows, :], vc_ref[rows, :]], axis=1).astype(BF16)
            qf = _lane_cat(q_ref, q0, HEAD_DIM)
            qcat = qf.astype(BF16)
            ocat = _lane_cat(o_ref, q0, HEAD_DIM)
            docat = _lane_cat(do_ref, q0, HEAD_DIM)
            dob = docat.astype(BF16)
            lse_cat = jnp.concatenate(
                [lse_ref[g * Q_PER_KV + i:g * Q_PER_KV + i + 1, :] for i in range(Q_PER_KV)], axis=1)
            sink = jnp.concatenate(
                [jnp.full((1, WINDOW), s_ref[g * Q_PER_KV + i], F32) for i in range(Q_PER_KV)], axis=1)
            s = jnp.where(valid, _dot_tn(kt, (qf * scale).astype(BF16)), NEG)
            p = jnp.exp(s - lse_cat)
            dp = _dot_tn(vt, dob)
            delta = _colsum(docat * ocat)
            dsc = (p * (dp - delta)).astype(BF16)
            dsink_row = -jnp.exp(sink - lse_cat) * delta
            dq = _dot(kt, dsc) * scale
            dk = _dot_nt(qcat, dsc) * scale
            dv = _dot_nt(dob, p.astype(BF16))
            for i in range(Q_PER_KV):
                h = g * Q_PER_KV + i
                dq_ref[h * HEAD_DIM:(h + 1) * HEAD_DIM, :] = dq[:, i * WINDOW:(i + 1) * WINDOW].astype(BF16)
                ds_ref[h:h + 1, :] += _rowsum(dsink_row[:, i * WINDOW:(i + 1) * WINDOW])
            dk_ref[rows, :] = (dk[:, WINDOW:] + dk_carry[rows, :]).astype(BF16)
            dv_ref[rows, :] = (dv[:, WINDOW:] + dv_carry[rows, :]).astype(BF16)
            dk_carry[rows, :] = dk[:, :WINDOW]
            dv_carry[rows, :] = dv[:, :WINDOW]

    kb = OFF_K // KV_DIM
    vb = OFF_V // KV_DIM
    cur = lambda i: nb - 1 - i
    prev = lambda i: jnp.maximum(nb - 2 - i, 0)
    qspec = pl.BlockSpec((Q_DIM, WINDOW), lambda i: (0, cur(i)))
    return _call(
        body,
        name="attn_bwd",
        grid=(nb,),
        in_specs=[
            pl.BlockSpec(memory_space=pltpu.SMEM),
            qspec,
            pl.BlockSpec((KV_DIM, WINDOW), lambda i: (kb, cur(i))),
            pl.BlockSpec((KV_DIM, WINDOW), lambda i: (kb, prev(i))),
            pl.BlockSpec((KV_DIM, WINDOW), lambda i: (vb, cur(i))),
            pl.BlockSpec((KV_DIM, WINDOW), lambda i: (vb, prev(i))),
            qspec,
            pl.BlockSpec((N_Q_HEADS, WINDOW), lambda i: (0, cur(i))),
            qspec,
            pl.BlockSpec(memory_space=pl.ANY),
        ],
        out_specs=[pl.BlockSpec((OFF_Z, WINDOW), lambda i: (0, cur(i))), pl.BlockSpec((N_Q_HEADS, 1), lambda i: (0, 0))],
        out_shape=[jax.ShapeDtypeStruct(dproj.shape, BF16), jax.ShapeDtypeStruct((N_Q_HEADS, 1), F32)],
        scratch_shapes=[pltpu.VMEM((KV_DIM, WINDOW), F32), pltpu.VMEM((KV_DIM, WINDOW), F32)],
        aliases={9: 0},
        semantics=("arbitrary",), args=(sinks, proj, proj, proj, proj, proj, out, lse, dout, dproj), send=send)


CONV_ROWS = 256


def _conv_silu_fwd(proj, w_col, b_col):
    t = proj.shape[1]
    r0 = OFF_X // CONV_ROWS

    def body(x_ref, w_ref, b_ref, o_ref):
        y, _ = _causal_conv(x_ref[...], w_ref[...], b_ref[...])
        o_ref[...] = y * _sigmoid(y)

    return pl.pallas_call(
        body,
        name="ssd_conv_fwd",
        grid=(XBC_DIM // CONV_ROWS,),
        in_specs=[
            pl.BlockSpec((CONV_ROWS, t), lambda i: (r0 + i, 0)),
            pl.BlockSpec((CONV_ROWS, SSD_CONV), lambda i: (i, 0)),
            pl.BlockSpec((CONV_ROWS, 1), lambda i: (i, 0)),
        ],
        out_specs=pl.BlockSpec((CONV_ROWS, t), lambda i: (i, 0)),
        out_shape=jax.ShapeDtypeStruct((XBC_DIM, t), F32),
        compiler_params=_cparams("parallel"),
    )(proj, w_col, b_col)


def _conv_silu_bwd(proj, w_col, b_col, dout, dproj):
    t = proj.shape[1]
    p0 = OFF_X // CONV_ROWS

    def body(x_ref, w_ref, b_ref, do_ref, dproj_ref, dx_ref, dwb_ref):
        xv = x_ref[...]
        wv = w_ref[...]
        y, shifted = _causal_conv(xv, wv, b_ref[...])
        sg = _sigmoid(y)
        dy = do_ref[...] * (sg * (1.0 + y * (1.0 - sg)))
        dx, dwb_ref[...] = _causal_conv_bwd(dy, xv, shifted, wv)
        dx_ref[...] = dx.astype(BF16)

    return pl.pallas_call(
        body,
        name="ssd_conv_bwd",
        grid=(XBC_DIM // CONV_ROWS,),
        in_specs=[
            pl.BlockSpec((CONV_ROWS, t), lambda i: (p0 + i, 0)),
            pl.BlockSpec((CONV_ROWS, SSD_CONV), lambda i: (i, 0)),
            pl.BlockSpec((CONV_ROWS, 1), lambda i: (i, 0)),
            pl.BlockSpec((CONV_ROWS, t), lambda i: (i, 0)),
            pl.BlockSpec(memory_space=pl.ANY),
        ],
        out_specs=[pl.BlockSpec((CONV_ROWS, t), lambda i: (p0 + i, 0)), pl.BlockSpec((CONV_ROWS, 128), lambda i: (i, 0))],
        out_shape=[jax.ShapeDtypeStruct(dproj.shape, BF16), jax.ShapeDtypeStruct((XBC_DIM, 128), F32)],
        input_output_aliases={4: 0},
        compiler_params=_cparams("parallel"),
    )(proj, w_col, b_col, dout, dproj)


def _ssd_specs(order):
    xb = D_INNER // BC_DIM
    dtb = OFF_DT // N_SSD_HEADS
    col = pl.BlockSpec((N_SSD_HEADS, 1), lambda c: (0, 0))
    return [
        pl.BlockSpec((D_INNER, CHUNK), lambda c: (0, order(c))),
        pl.BlockSpec((BC_DIM, CHUNK), lambda c: (xb, order(c))),
        pl.BlockSpec((BC_DIM, CHUNK), lambda c: (xb + 1, order(c))),
        pl.BlockSpec((N_SSD_HEADS, CHUNK), lambda c: (dtb, order(c))),
        col, col, col,
    ]


def _ssd_common(dt_ref, dtb_ref, alog_ref):
    z = dt_ref[...] + dtb_ref[...]
    dt = _softplus(z)
    a_neg = -jnp.exp(alog_ref[...])
    d_a = dt * a_neg
    row = lax.broadcasted_iota(jnp.int32, (CHUNK, CHUNK), 0)
    colm = lax.broadcasted_iota(jnp.int32, (CHUNK, CHUNK), 1)
    upper = (row <= colm).astype(F32)
    a_cs = jnp.dot(d_a, upper, precision=HIGHEST, preferred_element_type=F32)
    a_last = _rowsum(d_a)
    return z, dt, a_neg, a_cs, a_last, row >= colm, row == colm


def _decay(a_row, causal):
    a_s = jnp.broadcast_to(a_row, (CHUNK, CHUNK))
    seg = a_s.T - a_s
    return jnp.exp(jnp.where(causal, seg, NEG))


def _ssd_fwd(xbc, proj, dtb_col, alog_col, dsk_col, gnw_col):
    t = xbc.shape[1]
    nc = t // CHUNK

    def body(xs_ref, b_ref, c_ref, dt_ref, dtb_ref, alog_ref, dsk_ref, *rest):
        z_refs, (gnw_ref, y_ref, hst_ref, yn_ref, h_scr) = rest[:N_SSD_GROUPS], rest[N_SSD_GROUPS:]

        @pl.when(pl.program_id(0) == 0)
        def _():
            h_scr[...] = jnp.zeros_like(h_scr)

        _, dt, _, a_cs, a_last, causal, _ = _ssd_common(dt_ref, dtb_ref, alog_ref)
        hst_ref[0] = h_scr[...]
        dsk = dsk_ref[...]
        for g in range(N_SSD_GROUPS):
            grows = slice(g * D_STATE, (g + 1) * D_STATE)
            bb = b_ref[grows, :].astype(BF16)
            cb_ = c_ref[grows, :].astype(BF16)
            cb = _dot_tn(cb_, bb)
            for j in range(g * HEADS_PER_GROUP, (g + 1) * HEADS_PER_GROUP):
                rows = slice(j * SSD_HEAD_DIM, (j + 1) * SSD_HEAD_DIM)
                a = a_cs[j:j + 1, :]
                m = (cb * _decay(a, causal)).astype(BF16)
                xs = xs_ref[rows, :]
                xc = xs * dt[j:j + 1, :]
                hj = h_scr[rows, :]
                y = _dot_nt(xc.astype(BF16), m) + _dot(hj.astype(BF16), cb_) * jnp.exp(a) + dsk[j:j + 1, :] * xs
                y_ref[rows, :] = y
                al = a_last[j:j + 1, :]
                w = jnp.exp(al - a)
                h_scr[rows, :] = jnp.exp(al) * hj + _dot_nt((xc * w).astype(BF16), bb)
        for g in range(N_SSD_GROUPS):
            rows = slice(g * GN_ROWS, (g + 1) * GN_ROWS)
            zv = z_refs[g][...]
            u = y_ref[rows, :] * (zv * _sigmoid(zv))
            r = lax.rsqrt(jnp.mean(u * u, axis=0, keepdims=True) + EPS)
            yn_ref[rows, :] = (u * r * gnw_ref[rows, :]).astype(BF16)

    z0 = OFF_Z // GN_ROWS
    z_specs = [pl.BlockSpec((GN_ROWS, CHUNK), lambda c, g=g: (z0 + g, c)) for g in range(N_SSD_GROUPS)]
    rows_spec = pl.BlockSpec((D_INNER, CHUNK), lambda c: (0, c))
    return pl.pallas_call(
        body,
        name="ssd_fwd",
        grid=(nc,),
        in_specs=_ssd_specs(lambda c: c) + z_specs + [pl.BlockSpec((D_INNER, 1), lambda c: (0, 0))],
        out_specs=[rows_spec, pl.BlockSpec((1, D_INNER, D_STATE), lambda c: (c, 0, 0)), rows_spec],
        out_shape=[
            jax.ShapeDtypeStruct((D_INNER, t), F32),
            jax.ShapeDtypeStruct((nc, D_INNER, D_STATE), F32),
            jax.ShapeDtypeStruct((D_INNER, t), BF16),
        ],
        scratch_shapes=[pltpu.VMEM((D_INNER, D_STATE), F32)],
        compiler_params=_cparams("arbitrary"),
    )(xbc, xbc, xbc, proj, dtb_col, alog_col, dsk_col, *([proj] * N_SSD_GROUPS), gnw_col)


def _ssd_bwd(xbc, proj, dtb_col, alog_col, dsk_col, hst, y, dyn, gnw_col, send=None):
    t = xbc.shape[1]
    nc = t // CHUNK
    rev = lambda c: nc - 1 - c

    def body(xs_ref, b_ref, c_ref, dt_ref, dtb_ref, alog_ref, dsk_ref, hst_ref, y_ref, dyn_ref, *rest):
        z_refs, rest = rest[:N_SSD_GROUPS], rest[N_SSD_GROUPS:]
        (gnw_ref, dxbc_ref, ddt_ref, dalog_ref, ddsk_ref, ddtb_ref, dz_ref, dgnw_ref,
         dh_scr, da_scr, ddt_scr, dd_scr, dy_ref) = rest
        dxs_ref = dxbc_ref.at[pl.ds(0, D_INNER)]
        db_ref = dxbc_ref.at[pl.ds(D_INNER, BC_DIM)]
        dc_ref = dxbc_ref.at[pl.ds(D_INNER + BC_DIM, BC_DIM)]

        @pl.when(pl.program_id(0) == 0)
        def _():
            dh_scr[...] = jnp.zeros_like(dh_scr)
            dalog_ref[...] = jnp.zeros_like(dalog_ref)
            ddsk_ref[...] = jnp.zeros_like(ddsk_ref)
            ddtb_ref[...] = jnp.zeros_like(ddtb_ref)
            dgnw_ref[...] = jnp.zeros_like(dgnw_ref)

        for g in range(N_SSD_GROUPS):
            rows = slice(g * GN_ROWS, (g + 1) * GN_ROWS)
            zv = z_refs[g][...]
            yv = y_ref[rows, :]
            sg = _sigmoid(zv)
            sz = zv * sg
            u = yv * sz
            r = lax.rsqrt(jnp.mean(u * u, axis=0, keepdims=True) + EPS)
            xhat = u * r
            dov = dyn_ref[rows, :]
            dgnw_ref[rows, :] += _rowsum(dov * xhat)
            dxhat = dov * gnw_ref[rows, :]
            du = r * (dxhat - xhat * jnp.mean(dxhat * xhat, axis=0, keepdims=True))
            dy_ref[rows, :] = du * sz
            dz_ref[rows, :] = (du * yv * (sg * (1.0 + zv * (1.0 - sg)))).astype(BF16)

        z, dt, a_neg, a_cs, a_last, causal, eye = _ssd_common(dt_ref, dtb_ref, alog_ref)
        dsk = dsk_ref[...]
        last_lane = lax.broadcasted_iota(jnp.int32, (1, CHUNK), 1) == CHUNK - 1
        for g in range(N_SSD_GROUPS):
            grows = slice(g * D_STATE, (g + 1) * D_STATE)
            bb = b_ref[grows, :].astype(BF16)
            cb_ = c_ref[grows, :].astype(BF16)
            cb = _dot_tn(cb_, bb)
            dcb = jnp.zeros((CHUNK, CHUNK), F32)
            dc_acc = jnp.zeros((D_STATE, CHUNK), F32)
            db_acc = jnp.zeros((D_STATE, CHUNK), F32)
            for j in range(g * HEADS_PER_GROUP, (g + 1) * HEADS_PER_GROUP):
                rows = slice(j * SSD_HEAD_DIM, (j + 1) * SSD_HEAD_DIM)
                a = a_cs[j:j + 1, :]
                al = a_last[j:j + 1, :]
                lam = _decay(a, causal)
                mf = cb * lam
                xs = xs_ref[rows, :]
                dtj = dt[j:j + 1, :]
                xc = xs * dtj
                w = jnp.exp(al - a)
                e = jnp.exp(a)
                gam = jnp.exp(al)
                hj = hst_ref[0, rows, :]
                hjb = hj.astype(BF16)
                dyv = dy_ref[rows, :]
                dyb = dyv.astype(BF16)
                dd_scr[j:j + 1, :] = _colsum(dyv * xs)
                gb = (dyv * e).astype(BF16)
                dh_in = _dot_nt(gb, cb_)
                dc_acc = dc_acc + _dot_tn(hjb, gb)
                yoff = _dot(hjb, cb_) * e
                da = _colsum(dyv * yoff)
                dm = _dot_tn(dyb, xc.astype(BF16))
                dxc = _dot(dyb, mf.astype(BF16))
                dcb = dcb + dm * lam
                nmat = dm * mf
                rs = jnp.broadcast_to(_rowsum(nmat), (CHUNK, CHUNK))
                da = da + _colsum(jnp.where(eye, rs, 0.0)) - _colsum(nmat)
                ds = dh_scr[rows, :]
                dsb = ds.astype(BF16)
                t1 = _dot(dsb, bb)
                xcw = xc * w
                dxc = dxc + w * t1
                dww = _colsum(xcw * t1)
                da_l = _rowsum(dww) + _rowsum(_colsum(ds * hj)) * gam
                da = da - dww + jnp.where(last_lane, da_l, 0.0)
                db_acc = db_acc + _dot_tn(dsb, xcw.astype(BF16))
                dh_scr[rows, :] = gam * ds + dh_in
                dxs_ref[rows, :] = dsk[j:j + 1, :] * dyv + dxc * dtj
                da_scr[j:j + 1, :] = da
                ddt_scr[j:j + 1, :] = _colsum(dxc * xs)
            dcbb = dcb.astype(BF16)
            dc_ref[grows, :] = dc_acc + _dot_nt(bb, dcbb)
            db_ref[grows, :] = db_acc + _dot(cb_, dcbb)
        dda = jnp.dot(da_scr[...], causal.astype(F32), precision=HIGHEST, preferred_element_type=F32)
        ddt = ddt_scr[...] + dda * a_neg
        ddt_raw = ddt * _sigmoid(z)
        ddt_ref[...] = ddt_raw
        ddtb_ref[...] += _rowsum(ddt_raw)
        dalog_ref[...] += _rowsum(dda * dt) * a_neg
        ddsk_ref[...] += _rowsum(dd_scr[...])

    col = pl.BlockSpec((N_SSD_HEADS, 1), lambda c: (0, 0))
    xs_spec = pl.BlockSpec((D_INNER, CHUNK), lambda c: (0, rev(c)))
    gn_col = pl.BlockSpec((D_INNER, 1), lambda c: (0, 0))
    z0 = OFF_Z // GN_ROWS
    z_specs = [pl.BlockSpec((GN_ROWS, CHUNK), lambda c, g=g: (z0 + g, rev(c))) for g in range(N_SSD_GROUPS)]
    dz_spec = pl.BlockSpec((pl.Element(D_INNER), pl.Element(CHUNK)),
                           lambda c: (OFF_Z, pl.multiple_of(CHUNK * rev(c), CHUNK)))
    small = pltpu.VMEM((N_SSD_HEADS, CHUNK), F32)
    return _call(
        body,
        name="ssd_bwd",
        grid=(nc,),
        in_specs=_ssd_specs(rev) + [pl.BlockSpec((1, D_INNER, D_STATE), lambda c: (rev(c), 0, 0)), xs_spec, xs_spec]
        + z_specs + [gn_col],
        out_specs=[pl.BlockSpec((XBC_DIM, CHUNK), lambda c: (0, rev(c))),
                   pl.BlockSpec((N_SSD_HEADS, CHUNK), lambda c: (0, rev(c))), col, col, col, dz_spec, gn_col],
        out_shape=[
            jax.ShapeDtypeStruct((XBC_DIM, t), F32),
            jax.ShapeDtypeStruct((N_SSD_HEADS, t), F32),
            jax.ShapeDtypeStruct((N_SSD_HEADS, 1), F32),
            jax.ShapeDtypeStruct((N_SSD_HEADS, 1), F32),
            jax.ShapeDtypeStruct((N_SSD_HEADS, 1), F32),
            jax.ShapeDtypeStruct((IN_DIM, t), BF16),
            jax.ShapeDtypeStruct((D_INNER, 1), F32),
        ],
        scratch_shapes=[pltpu.VMEM((D_INNER, D_STATE), F32), small, small, small, pltpu.VMEM((D_INNER, CHUNK), F32)],
        semantics=("arbitrary",),
        args=(xbc, xbc, xbc, proj, dtb_col, alog_col, dsk_col, hst, y, dyn, *([proj] * N_SSD_GROUPS), gnw_col),
        send=send)


GATE_ROWS = 128


def _gate_specs(t):
    nr = D_MODEL // GATE_ROWS
    blk = pl.BlockSpec((GATE_ROWS, t), lambda r: (r, 0))
    rows_from = lambda first: pl.BlockSpec(
        (pl.Element(GATE_ROWS), pl.Element(t)), lambda r: (pl.multiple_of(first + GATE_ROWS * r, N_SSD_HEADS), 0))
    return blk, [
        rows_from(OFF_GA),
        rows_from(OFF_GS),
        pl.BlockSpec((GATE_ROWS, 1), lambda r: (r, 0)),
        pl.BlockSpec((GATE_ROWS, 1), lambda r: (nr + r, 0)),
        blk, blk,
    ]


def _gate_fwd(proj, b_col, attn, ssd):
    t = proj.shape[1]
    blk, specs = _gate_specs(t)

    def body(ga_ref, gs_ref, ba_ref, bs_ref, a_ref, s_ref, o_ref):
        o_ref[...] = (_sigmoid(ga_ref[...] + ba_ref[...]) * a_ref[...]
                      + _sigmoid(gs_ref[...] + bs_ref[...]) * s_ref[...]).astype(BF16)

    return pl.pallas_call(
        body,
        name="gate_fwd",
        grid=(D_MODEL // GATE_ROWS,),
        in_specs=specs,
        out_specs=blk,
        out_shape=jax.ShapeDtypeStruct((D_MODEL, t), BF16),
        compiler_params=_cparams("parallel"),
    )(proj, proj, b_col, b_col, attn, ssd)


def _gate_bwd(proj, b_col, attn, ssd, dmix, send=None):
    t = proj.shape[1]
    blk, specs = _gate_specs(t)

    def body(ga_ref, gs_ref, ba_ref, bs_ref, a_ref, s_ref, dm_ref, da_ref, dso_ref, dga_ref, dgs_ref, dba_ref, dbs_ref):
        dm = dm_ref[...]
        sa = _sigmoid(ga_ref[...] + ba_ref[...])
        ss = _sigmoid(gs_ref[...] + bs_ref[...])
        da_ref[...] = (dm * sa).astype(BF16)
        dso_ref[...] = (dm * ss).astype(BF16)
        dga = dm * a_ref[...] * sa * (1.0 - sa)
        dgs = dm * s_ref[...] * ss * (1.0 - ss)
        dga_ref[...] = dga.astype(BF16)
        dgs_ref[...] = dgs.astype(BF16)
        dba_ref[...] = _rowsum(dga)
        dbs_ref[...] = _rowsum(dgs)

    col = pl.BlockSpec((GATE_ROWS, 1), lambda r: (r, 0))
    act = jax.ShapeDtypeStruct((D_MODEL, t), BF16)
    bias = jax.ShapeDtypeStruct((D_MODEL, 1), F32)
    return _call(
        body,
        name="gate_bwd",
        grid=(D_MODEL // GATE_ROWS,),
        in_specs=specs + [blk],
        out_specs=[blk, blk, blk, blk, col, col],
        out_shape=[act, act, act, act, bias, bias],
        semantics=("parallel",), args=(proj, proj, b_col, b_col, attn, ssd, dmix), send=send)


FFN_ROWS = 256


def _ffn_fwd(u0, w_col, b_col):
    t = u0.shape[2]

    def body(u_ref, w_ref, b_ref, o_ref):
        val, _ = _causal_conv(u_ref[0], w_ref[0], b_ref[0])
        gt, _ = _causal_conv(u_ref[1], w_ref[1], b_ref[1])
        o_ref[...] = (gt * _sigmoid(gt) * val).astype(BF16)

    return pl.pallas_call(
        body,
        name="ffn_fwd",
        grid=(D_FF // FFN_ROWS,),
        in_specs=[
            pl.BlockSpec((2, FFN_ROWS, t), lambda i: (0, i, 0)),
            pl.BlockSpec((2, FFN_ROWS, FFN_CONV), lambda i: (0, i, 0)),
            pl.BlockSpec((2, FFN_ROWS, 1), lambda i: (0, i, 0)),
        ],
        out_specs=pl.BlockSpec((FFN_ROWS, t), lambda i: (i, 0)),
        out_shape=jax.ShapeDtypeStruct((D_FF, t), BF16),
        compiler_params=_cparams("parallel"),
    )(u0, w_col, b_col)


def _ffn_bwd(u0, w_col, b_col, dg, send=None):
    t = u0.shape[2]

    def body(u_ref, w_ref, b_ref, dg_ref, du_ref, dwb_ref):
        xval, wval = u_ref[0], w_ref[0]
        xgt, wgt = u_ref[1], w_ref[1]
        val, sh_val = _causal_conv(xval, wval, b_ref[0])
        gt, sh_gt = _causal_conv(xgt, wgt, b_ref[1])
        sg = _sigmoid(gt)
        dgv = dg_ref[...]
        dval = dgv * (gt * sg)
        dgt = dgv * val * (sg * (1.0 + gt * (1.0 - sg)))
        dx, dwb_ref[0] = _causal_conv_bwd(dval, xval, sh_val, wval)
        du_ref[0] = dx.astype(BF16)
        dx, dwb_ref[1] = _causal_conv_bwd(dgt, xgt, sh_gt, wgt)
        du_ref[1] = dx.astype(BF16)

    return _call(
        body,
        name="ffn_bwd",
        grid=(D_FF // FFN_ROWS,),
        in_specs=[
            pl.BlockSpec((2, FFN_ROWS, t), lambda i: (0, i, 0)),
            pl.BlockSpec((2, FFN_ROWS, FFN_CONV), lambda i: (0, i, 0)),
            pl.BlockSpec((2, FFN_ROWS, 1), lambda i: (0, i, 0)),
            pl.BlockSpec((FFN_ROWS, t), lambda i: (i, 0)),
        ],
        out_specs=[pl.BlockSpec((2, FFN_ROWS, t), lambda i: (0, i, 0)), pl.BlockSpec((2, FFN_ROWS, 128), lambda i: (0, i, 0))],
        out_shape=[jax.ShapeDtypeStruct((2, D_FF, t), BF16), jax.ShapeDtypeStruct((2, D_FF, 128), F32)],
        semantics=("parallel",), args=(u0, w_col, b_col, dg), send=send)


def _adamw_math(w, g, m, v):
    m = ADAM_B1 * m + (1.0 - ADAM_B1) * g
    v = ADAM_B2 * v + (1.0 - ADAM_B2) * (g * g)
    m_hat = m / (1.0 - ADAM_B1 ** ADAM_STEP)
    v_hat = v / (1.0 - ADAM_B2 ** ADAM_STEP)
    delta = -ADAM_LR * (m_hat / (jnp.sqrt(v_hat) + ADAM_EPS) + ADAM_WD * w)
    return delta, m, v


def _adamw_sharded(parts, w, m, v, name):
    r, c = w.shape[0], w.shape[-1]
    slots = parts.shape[0]
    per_lane = 2 * r * (slots * parts.dtype.itemsize + 7 * w.dtype.itemsize)
    tc = max(d for d in range(LANES, c + 1, LANES) if c % d == 0 and (d * per_lane <= BLOCK_VMEM_BUDGET or d == LANES))
    blk_shape = (r, tc) if w.ndim == 2 else (r, 1, tc)

    def body(p_ref, w_ref, m_ref, v_ref, g_ref, d_ref, nm_ref, nv_ref):
        g = p_ref[0].astype(F32)
        for s in range(1, slots):
            g = g + p_ref[s].astype(F32)
        flat = lambda ref: ref[...].reshape(r, tc)
        d, nm, nv = _adamw_math(flat(w_ref), g, flat(m_ref), flat(v_ref))
        for ref, val in ((g_ref, g), (d_ref, d), (nm_ref, nm), (nv_ref, nv)):
            ref[...] = val.reshape(blk_shape)

    blk = pl.BlockSpec(blk_shape, (lambda i: (0, i)) if w.ndim == 2 else (lambda i: (0, 0, i)))
    out = jax.ShapeDtypeStruct(w.shape, F32)
    return pl.pallas_call(
        body,
        name=name,
        grid=(c // tc,),
        in_specs=[pl.BlockSpec((slots, r, tc), lambda i: (0, 0, i)), blk, blk, blk],
        out_specs=[blk, blk, blk, blk],
        out_shape=[out, out, out, out],
        compiler_params=_cparams("parallel"),
    )(parts, w, m, v)


def _lane_offsets(sizes):
    offsets, pos = [], 0
    for n in sizes:
        offsets.append(pos)
        pos += -(-n // 128) * 128
    return offsets, pos


def _pack_row(parts):
    rows = [p.reshape(1, -1).astype(F32) for p in parts]
    return jnp.concatenate([jnp.pad(r, ((0, 0), (0, -r.shape[1] % 128))) for r in rows], axis=1)


def _small_update(parts, me, full_sizes, ws, ms, vs):
    n = len(ws)
    offsets, _ = _lane_offsets([1] + list(full_sizes))

    def body(me_ref, p_ref, *refs):
        w_refs, m_refs, v_refs = refs[:n], refs[n:2 * n], refs[2 * n:3 * n]
        scalar_ref, out_refs = refs[3 * n], refs[3 * n + 1:]
        tot = p_ref[0]
        for s in range(1, N_DEV):
            tot = tot + p_ref[s]
        scalar_ref[...] = tot[:, 0:1]
        for k in range(n):
            g_ref, d_ref, nm_ref, nv_ref = out_refs[4 * k:4 * k + 4]
            taps, cols = w_refs[k].shape
            if taps == 1:
                g_ref[...] = tot[:, offsets[k + 1]:offsets[k + 1] + cols]
            else:
                full = full_sizes[k] // taps
                for tap in range(taps):
                    mine = jnp.zeros((1, cols), F32)
                    for d in range(N_DEV):
                        lo = offsets[k + 1] + tap * full + d * cols
                        mine = jnp.where(me_ref[0] == d, tot[:, lo:lo + cols], mine)
                    g_ref[tap:tap + 1, :] = mine
            d_ref[...], nm_ref[...], nv_ref[...] = _adamw_math(w_refs[k][...], g_ref[...], m_refs[k][...], v_refs[k][...])

    vmem = pl.BlockSpec(memory_space=pltpu.VMEM)
    out_shape = [jax.ShapeDtypeStruct((1, 1), F32)]
    for wk in ws:
        out_shape += [jax.ShapeDtypeStruct(wk.shape, F32)] * 4
    res = pl.pallas_call(
        body,
        name="small_update",
        in_specs=[pl.BlockSpec(memory_space=pltpu.SMEM)] + [vmem] * (1 + 3 * n),
        out_specs=[vmem] * len(out_shape),
        out_shape=out_shape,
    )(me, parts, *ws, *ms, *vs)
    return res[0], [res[1 + 4 * k:5 + 4 * k] for k in range(n)]


ANY = pl.BlockSpec(memory_space=pl.ANY)
FLIPS = [(k >> 2 & 1, k >> 1 & 1, k & 1) for k in range(1, N_DEV)]


def _place():
    return lax.axis_index("x"), lax.axis_index("y"), lax.axis_index("c")


HBM = pl.BlockSpec(memory_space=pltpu.HBM)
SEM = pl.BlockSpec(memory_space=pltpu.SEMAPHORE)
EFFECT = pltpu.SideEffectType.DATAFLOW_SIDE_EFFECTING


def _peer_copy(gather, src_ref, land_ref, send_sems, recv_sems, k, sending):
    x, y, c = _place()
    fx, fy, fc = FLIPS[k]
    me = 4 * x + 2 * y + c
    peer = 4 * (x ^ fx) + 2 * (y ^ fy) + (c ^ fc)
    return pltpu.make_async_remote_copy(
        src_ref=src_ref if gather else src_ref.at[peer],
        dst_ref=land_ref.at[me if sending else peer],
        send_sem=send_sems.at[k], recv_sem=recv_sems.at[k],
        device_id=(x ^ fx, y ^ fy, c ^ fc), device_id_type=MESH)


SIBLING = 0
OTHER_CHIPS = (1, 3, 5)


def _gather_start(srcs, name, via_sibling):
    n = len(srcs)
    lands = [lax.empty((N_DEV,) + s.shape, s.dtype) for s in srcs]

    def body(*refs):
        src_refs, land_refs = refs[:n], refs[n:2 * n]
        send, recv = refs[2 * n:3 * n], refs[3 * n:4 * n]
        for i in range(n):
            for k in (SIBLING,) + OTHER_CHIPS if via_sibling else range(N_DEV - 1):
                _peer_copy(True, src_refs[i], land_refs[i], send[i], recv[i], k, True).start()

    sem = pltpu.SemaphoreType.DMA((N_DEV - 1,))
    hbm = lambda a: pltpu.HBM(a.shape, a.dtype)
    res = pl.pallas_call(
        body,
        name=name,
        in_specs=[HBM] * (2 * n),
        out_specs=[SEM] * (2 * n) + [HBM] * (2 * n),
        out_shape=[sem] * (2 * n) + [hbm(s) for s in srcs] + [hbm(a) for a in lands],
        input_output_aliases={i: 2 * n + i for i in range(2 * n)},
        compiler_params=pltpu.CompilerParams(has_side_effects=EFFECT),
    )(*[pltpu.with_memory_space_constraint(a, pltpu.HBM) for a in list(srcs) + lands])
    return res[:n], res[n:2 * n], res[2 * n:3 * n], res[3 * n:4 * n]


def _exchange_wait(send_sems, recv_sems, src, land, after, gather, name):
    def body(src_ref, land_ref, send_ref, recv_ref, after_ref, src_out, land_out):
        for k in range(N_DEV - 1):
            cp = _peer_copy(gather, src_ref, land_ref, send_ref, recv_ref, k, False)
            cp.wait_send()
            cp.wait_recv()

    hbm = lambda a: pltpu.HBM(a.shape, a.dtype)
    return pl.pallas_call(
        body,
        name=name,
        in_specs=[HBM, HBM, SEM, SEM, ANY],
        out_specs=[HBM, HBM],
        out_shape=[hbm(src), hbm(land)],
        input_output_aliases={0: 0, 1: 1},
        compiler_params=pltpu.CompilerParams(has_side_effects=EFFECT),
    )(src, land, send_sems, recv_sems, after)


def _own_slot(src, land, me, gather):
    own = src[None] if gather else lax.dynamic_slice_in_dim(src, me, 1, axis=0)
    return lax.dynamic_update_slice_in_dim(land, own, me, axis=0)


def _forwarded_copy(land_ref, send_sems, recv_sems, j, sending):
    x, y, c = _place()
    fx, fy, _ = FLIPS[OTHER_CHIPS[j]]
    slot = 4 * (x ^ fx) + 2 * (y ^ fy) + (c if sending else 1 - c)
    return pltpu.make_async_remote_copy(
        src_ref=land_ref.at[slot], dst_ref=land_ref.at[slot], send_sem=send_sems.at[j], recv_sem=recv_sems.at[j],
        device_id=(x, y, 1 - c), device_id_type=MESH)


def _gather_forward(send_sems, recv_sems, srcs, lands, after, name):
    n = len(srcs)

    def body(*refs):
        src_refs, land_refs = refs[:n], refs[n:2 * n]
        send, recv = refs[2 * n:3 * n], refs[3 * n:4 * n]
        fwd_send, fwd_recv = refs[4 * n + 1:5 * n + 1], refs[5 * n + 1:6 * n + 1]
        for i in range(n):
            for j, k in enumerate(OTHER_CHIPS):
                _peer_copy(True, src_refs[i], land_refs[i], send[i], recv[i], k, False).wait_recv()
                _forwarded_copy(land_refs[i], fwd_send[i], fwd_recv[i], j, True).start()

    sem = pltpu.SemaphoreType.DMA((len(OTHER_CHIPS),))
    hbm = lambda a: pltpu.HBM(a.shape, a.dtype)
    res = pl.pallas_call(
        body,
        name=name,
        in_specs=[HBM] * (2 * n) + [SEM] * (2 * n) + [ANY],
        out_specs=[SEM] * (2 * n) + [HBM] * (2 * n),
        out_shape=[sem] * (2 * n) + [hbm(a) for a in srcs] + [hbm(a) for a in lands],
        input_output_aliases={i: 2 * n + i for i in range(2 * n)},
        compiler_params=pltpu.CompilerParams(has_side_effects=EFFECT),
    )(*srcs, *lands, *send_sems, *recv_sems, after)
    return res[:n], res[n:2 * n], res[2 * n:3 * n], res[3 * n:4 * n]


def _gather_wait_forwarded(send_sems, recv_sems, fwd_send, fwd_recv, src, land, after, name):
    def body(src_ref, land_ref, send_ref, recv_ref, fwd_send_ref, fwd_recv_ref, after_ref, src_out, land_out):
        for k in (SIBLING,) + OTHER_CHIPS:
            _peer_copy(True, src_ref, land_ref, send_ref, recv_ref, k, False).wait_send()
        _peer_copy(True, src_ref, land_ref, send_ref, recv_ref, SIBLING, False).wait_recv()
        for j in range(len(OTHER_CHIPS)):
            _forwarded_copy(land_ref, fwd_send_ref, fwd_recv_ref, j, True).wait_send()
            _forwarded_copy(land_ref, fwd_send_ref, fwd_recv_ref, j, False).wait_recv()

    hbm = lambda a: pltpu.HBM(a.shape, a.dtype)
    return pl.pallas_call(
        body,
        name=name,
        in_specs=[HBM, HBM, SEM, SEM, SEM, SEM, ANY],
        out_specs=[HBM, HBM],
        out_shape=[hbm(src), hbm(land)],
        input_output_aliases={0: 0, 1: 1},
        compiler_params=pltpu.CompilerParams(has_side_effects=EFFECT),
    )(src, land, send_sems, recv_sems, fwd_send, fwd_recv, after)


N_CHIPS = N_DEV // 2


def _pair_exchange(by_core, meanwhile, name):
    def copy(src_ref, land_ref, send_sems, recv_sems, q):
        x, y, c = _place()
        return pltpu.make_async_remote_copy(
            src_ref=src_ref.at[q, 1 - c], dst_ref=land_ref.at[q], send_sem=send_sems.at[q], recv_sem=recv_sems.at[q],
            device_id=(x, y, 1 - c), device_id_type=MESH)

    def start(src_ref, land_ref, send_sems, recv_sems, src_out, land_out):
        for q in range(N_CHIPS):
            copy(src_ref, land_ref, send_sems, recv_sems, q).start()

    def wait(src_ref, land_ref, send_sems, recv_sems, after_ref, src_out, land_out):
        for q in range(N_CHIPS):
            cp = copy(src_ref, land_ref, send_sems, recv_sems, q)
            cp.wait_send()
            cp.wait_recv()

    sem = pltpu.SemaphoreType.DMA((N_CHIPS,))
    hbm_src = pltpu.HBM(by_core.shape, by_core.dtype)
    hbm_land = pltpu.HBM(by_core.shape[:1] + by_core.shape[2:], by_core.dtype)
    params = pltpu.CompilerParams(has_side_effects=EFFECT)
    send_sems, recv_sems, src, land = pl.pallas_call(
        start, name=name + "_start", in_specs=[HBM, HBM], out_specs=[SEM, SEM, HBM, HBM],
        out_shape=[sem, sem, hbm_src, hbm_land], input_output_aliases={0: 2, 1: 3}, compiler_params=params,
    )(pltpu.with_memory_space_constraint(by_core, pltpu.HBM),
      pltpu.with_memory_space_constraint(lax.empty(hbm_land.shape, by_core.dtype), pltpu.HBM))
    return pl.pallas_call(
        wait, name=name + "_wait", in_specs=[HBM, HBM, SEM, SEM, ANY], out_specs=[HBM, HBM],
        out_shape=[hbm_src, hbm_land], input_output_aliases={0: 0, 1: 1}, compiler_params=params,
    )(src, land, send_sems, recv_sems, meanwhile(src))


def _pair_add(by_core, landed, name):
    q, _, r, c = by_core.shape
    tc = _tile(c, (512, 256, 128))

    def body(a_ref, b_ref, o_ref):
        mine = a_ref[0, lax.axis_index("c")]
        o_ref[0] = (mine.astype(F32) + b_ref[0].astype(F32)).astype(BF16)

    blk = pl.BlockSpec((1, r, tc), lambda i, j: (i, 0, j))
    return pl.pallas_call(
        body, name=name, grid=(q, c // tc),
        in_specs=[pl.BlockSpec((1, 2, r, tc), lambda i, j: (i, 0, 0, j)), blk], out_specs=blk,
        out_shape=jax.ShapeDtypeStruct(landed.shape, BF16), compiler_params=_cparams("parallel", "parallel"),
    )(by_core, landed)


def _chip_copy(src_ref, land_ref, send_sems, recv_sems, j, sending):
    x, y, c = _place()
    fx, fy, _ = FLIPS[OTHER_CHIPS[j]]
    here, there = 2 * x + y, 2 * (x ^ fx) + (y ^ fy)
    return pltpu.make_async_remote_copy(
        src_ref=src_ref.at[there], dst_ref=land_ref.at[here if sending else there],
        send_sem=send_sems.at[j], recv_sem=recv_sems.at[j],
        device_id=(x ^ fx, y ^ fy, c), device_id_type=MESH)


def _chip_start(sums, name):
    def start(src_ref, land_ref, send_sems, recv_sems, src_out, land_out):
        for j in range(len(OTHER_CHIPS)):
            _chip_copy(src_ref, land_ref, send_sems, recv_sems, j, True).start()

    sem = pltpu.SemaphoreType.DMA((len(OTHER_CHIPS),))
    hbm = pltpu.HBM(sums.shape, sums.dtype)
    handles = pl.pallas_call(
        start, name=name, in_specs=[HBM, HBM], out_specs=[SEM, SEM, HBM, HBM], out_shape=[sem, sem, hbm, hbm],
        input_output_aliases={0: 2, 1: 3}, compiler_params=pltpu.CompilerParams(has_side_effects=EFFECT),
    )(pltpu.with_memory_space_constraint(sums, pltpu.HBM),
      pltpu.with_memory_space_constraint(lax.empty(sums.shape, sums.dtype), pltpu.HBM))
    return handles[2], tuple(handles)


def _chip_wait(send_sems, recv_sems, src, land, after, name):
    def body(src_ref, land_ref, send_ref, recv_ref, after_ref, src_out, land_out):
        for j in range(len(OTHER_CHIPS)):
            cp = _chip_copy(src_ref, land_ref, send_ref, recv_ref, j, False)
            cp.wait_send()
            cp.wait_recv()

    hbm = lambda a: pltpu.HBM(a.shape, a.dtype)
    return pl.pallas_call(
        body,
        name=name,
        in_specs=[HBM, HBM, SEM, SEM, ANY],
        out_specs=[HBM, HBM],
        out_shape=[hbm(src), hbm(land)],
        input_output_aliases={0: 0, 1: 1},
        compiler_params=pltpu.CompilerParams(has_side_effects=EFFECT),
    )(src, land, send_sems, recv_sems, after)


def _col(v):
    return v.reshape(-1, 1).astype(F32)


def _local_step(x, tgt, started, weight, small, pair_sums, handles):
    t = x.shape[0]
    n1 = _col(small["norm1_w"])
    n2 = _col(small["norm2_w"])
    nf = _col(small["final_norm_w"])
    bg = _col(small["b_gate"])
    sinks = small["attn_sinks"].reshape(-1).astype(F32)
    cbias = _col(small["ssd_conv_b"])
    dtb = _col(small["dt_bias"])
    alog = _col(small["a_log"])
    dsk = _col(small["d_skip"])
    gnw = _col(small["ssd_norm_w"])
    fb = small["ffn_conv_b"].reshape(2, D_FF, 1)

    xt, xn = _norm_fwd_tokens(x, n1, started, "norm1_fwd")
    cw = weight("ssd_conv_w", xn).T
    fw = weight("ffn_conv_w", xn).T.reshape(2, D_FF, FFN_CONV)
    w_in_t = weight("w_in", xn)
    proj = _matmul(w_in_t, xn, nt=False, out_dtype=F32, name="mm_in")
    ao, lse = _attn_fwd(proj, sinks)
    w_ao = weight("w_attn_o", ao)
    attn = _matmul(w_ao, ao, nt=False, out_dtype=F32, name="mm_attn_o", tn_a=True)
    xbc = _conv_silu_fwd(proj, cw, cbias)
    y, hst, yn = _ssd_fwd(xbc, proj, dtb, alog, dsk, gnw)
    w_so = weight("w_ssd_o", yn)
    ssd = _matmul(w_so, yn, nt=False, out_dtype=F32, name="mm_ssd_o", tn_a=True)
    mix = _gate_fwd(proj, bg, attn, ssd)
    w_out = weight("w_out", mix)
    h1 = _matmul(w_out, mix, nt=False, out_dtype=F32, name="mm_out", add=xt, tn_a=True)
    hn = _norm_fwd(h1, n2, "norm2_fwd")
    w_up_t = weight("w_up", hn)
    u0 = _matmul(w_up_t, hn, nt=False, out_dtype=F32, name="mm_up").reshape(2, D_FF, t)
    gl = _ffn_fwd(u0, fw, fb)
    w_down = weight("w_down", gl)
    h2 = _matmul(w_down, gl, nt=False, out_dtype=F32, name="mm_down", add=h1, tn_a=True)
    dh2, loss, d_nf = _final_norm_loss(h2, tgt, nf)

    g = {}

    def sending(weight_name, grad, fn, *args, **kwargs):
        chunks = grad if grad.ndim == 3 else grad.reshape(N_DEV, -1, D_MODEL)
        out, handles[weight_name] = fn(*args, send=chunks, **kwargs)
        return out

    g_down = _matmul(gl, dh2, nt=True, out_dtype=BF16, name="mm_d_w_down")
    dgl = _matmul(w_down, dh2, nt=False, out_dtype=F32, name="mm_d_glu")
    du0, d_fwb = sending("w_down", g_down, _ffn_bwd, u0, fw, fb, dgl)
    du0 = du0.reshape(2 * D_FF, t)
    g_up = _matmul(du0, hn, nt=True, out_dtype=BF16, name="mm_d_w_up")
    dh1, d_n2 = sending("w_up", g_up, _matmul, w_up_t, du0, nt=False, out_dtype=F32, name="mm_d_hn_norm2_bwd", tn_a=True,
                        norm_bwd=(h1, n2, dh2, False))
    g_out = _matmul(mix, dh1, nt=True, out_dtype=BF16, name="mm_d_w_out")
    dmix = _matmul(w_out, dh1, nt=False, out_dtype=F32, name="mm_d_mix")
    d_attn, d_ssd, d_ga, d_gs, d_ba, d_bs = sending("w_out", g_out, _gate_bwd, proj, bg, attn, ssd, dmix)
    g_ao = _matmul(ao, d_attn, nt=True, out_dtype=BF16, name="mm_d_w_attn_o")
    dao = _matmul(w_ao, d_attn, nt=False, out_dtype=F32, name="mm_d_ao")
    g_so = _matmul(yn, d_ssd, nt=True, out_dtype=BF16, name="mm_d_w_ssd_o")
    dyn = _matmul(w_so, d_ssd, nt=False, out_dtype=F32, name="mm_d_yn")
    dxbc, ddt, d_alog, d_dsk, d_dtb, dproj, d_gnw = sending(
        "w_ssd_o", g_so, _ssd_bwd, xbc, proj, dtb, alog, dsk, hst, y, dyn, gnw)
    dproj, dwb_conv = _conv_silu_bwd(proj, cw, cbias, dxbc, dproj)
    dproj, d_sinks = sending("w_attn_o", g_ao, _attn_bwd, proj, sinks, ao, lse, dao, dproj)
    for rows, part in ((OFF_DT, ddt.astype(BF16)), (OFF_GA, d_ga), (OFF_GS, d_gs)):
        dproj = lax.dynamic_update_slice(dproj, part, (rows, 0))
    leaving = pair_sums(_matmul(dproj, xn, nt=True, out_dtype=BF16, name="mm_d_w_in"))
    dx, d_n1 = _matmul(w_in_t, dproj, nt=False, out_dtype=F32, name="mm_d_xn_norm1_bwd", tn_a=True,
                       norm_bwd=(xt, n1, dh1, True), after=leaving)

    g["norm1_w"] = d_n1
    g["b_gate"] = jnp.concatenate([d_ba, d_bs], axis=0)
    g["attn_sinks"] = d_sinks
    g["ssd_conv_w"] = dwb_conv[:, :SSD_CONV].T
    g["ssd_conv_b"] = dwb_conv[:, SSD_CONV]
    g["dt_bias"] = d_dtb
    g["a_log"] = d_alog
    g["d_skip"] = d_dsk
    g["ssd_norm_w"] = d_gnw
    g["norm2_w"] = d_n2
    d_fwb = d_fwb.reshape(2 * D_FF, 128)
    g["ffn_conv_w"] = d_fwb[:, :FFN_CONV].T
    g["ffn_conv_b"] = d_fwb[:, FFN_CONV]
    g["final_norm_w"] = d_nf
    return loss, dx, g


SMALL = ("norm1_w", "b_gate", "attn_sinks", "ssd_conv_w", "ssd_conv_b", "dt_bias", "a_log", "d_skip", "ssd_norm_w",
         "norm2_w", "ffn_conv_w", "ffn_conv_b", "final_norm_w")
WEIGHT_ORDER = ("norm1_w", "w_in", "b_gate", "attn_sinks", "w_attn_o", "ssd_conv_w", "ssd_conv_b", "dt_bias", "a_log",
                "d_skip", "ssd_norm_w", "w_ssd_o", "w_out", "norm2_w", "w_up", "ffn_conv_w", "ffn_conv_b", "w_down",
                "final_norm_w")


def kernel(x, norm1_w, w_in, b_gate, attn_sinks, w_attn_o, ssd_conv_w, ssd_conv_b, dt_bias, a_log, d_skip, ssd_norm_w, w_ssd_o, w_out, norm2_w, w_up, ffn_conv_w, ffn_conv_b, w_down, final_norm_w, loss_target, m_norm1_w, m_w_in, m_b_gate, m_attn_sinks, m_w_attn_o, m_ssd_conv_w, m_ssd_conv_b, m_dt_bias, m_a_log, m_d_skip, m_ssd_norm_w, m_w_ssd_o, m_w_out, m_norm2_w, m_w_up, m_ffn_conv_w, m_ffn_conv_b, m_w_down, m_final_norm_w, v_norm1_w, v_w_in, v_b_gate, v_attn_sinks, v_w_attn_o, v_ssd_conv_w, v_ssd_conv_b, v_dt_bias, v_a_log, v_d_skip, v_ssd_norm_w, v_w_ssd_o, v_w_out, v_norm2_w, v_w_up, v_ffn_conv_w, v_ffn_conv_b, v_w_down, v_final_norm_w):
    w = dict(norm1_w=norm1_w, w_in=w_in, b_gate=b_gate, attn_sinks=attn_sinks, w_attn_o=w_attn_o, ssd_conv_w=ssd_conv_w, ssd_conv_b=ssd_conv_b, dt_bias=dt_bias, a_log=a_log, d_skip=d_skip, ssd_norm_w=ssd_norm_w, w_ssd_o=w_ssd_o, w_out=w_out, norm2_w=norm2_w, w_up=w_up, ffn_conv_w=ffn_conv_w, ffn_conv_b=ffn_conv_b, w_down=w_down, final_norm_w=final_norm_w)
    m = dict(norm1_w=m_norm1_w, w_in=m_w_in, b_gate=m_b_gate, attn_sinks=m_attn_sinks, w_attn_o=m_w_attn_o, ssd_conv_w=m_ssd_conv_w, ssd_conv_b=m_ssd_conv_b, dt_bias=m_dt_bias, a_log=m_a_log, d_skip=m_d_skip, ssd_norm_w=m_ssd_norm_w, w_ssd_o=m_w_ssd_o, w_out=m_w_out, norm2_w=m_norm2_w, w_up=m_w_up, ffn_conv_w=m_ffn_conv_w, ffn_conv_b=m_ffn_conv_b, w_down=m_w_down, final_norm_w=m_final_norm_w)
    v = dict(norm1_w=v_norm1_w, w_in=v_w_in, b_gate=v_b_gate, attn_sinks=v_attn_sinks, w_attn_o=v_w_attn_o, ssd_conv_w=v_ssd_conv_w, ssd_conv_b=v_ssd_conv_b, dt_bias=v_dt_bias, a_log=v_a_log, d_skip=v_d_skip, ssd_norm_w=v_ssd_norm_w, w_ssd_o=v_w_ssd_o, w_out=v_w_out, norm2_w=v_norm2_w, w_up=v_w_up, ffn_conv_w=v_ffn_conv_w, ffn_conv_b=v_ffn_conv_b, w_down=v_w_down, final_norm_w=v_final_norm_w)
    me = 4 * lax.axis_index("x") + 2 * lax.axis_index("y") + lax.axis_index("c")

    shards = {"ssd_conv_w": ssd_conv_w[0], "ffn_conv_w": ffn_conv_w[0], "w_in": w_in[0].T.astype(BF16),
              "w_attn_o": w_attn_o[0].astype(BF16), "w_ssd_o": w_ssd_o[0].astype(BF16), "w_out": w_out[0].astype(BF16),
              "w_up": w_up[0].T.astype(BF16), "w_down": w_down[0].astype(BF16)}
    order = list(shards)
    g_send, g_recv, g_src, g_land = _gather_start(list(shards.values()), "gather_start", True)
    first = ("ssd_conv_w", "ffn_conv_w", "w_in")
    forwarded = {}

    def weight(name, after):
        if name not in forwarded:
            group = [k for k in order if (k in first) == (name in first)]
            idx = [order.index(k) for k in group]
            handles = _gather_forward([g_send[i] for i in idx], [g_recv[i] for i in idx], [g_src[i] for i in idx],
                                      [g_land[i] for i in idx], after, "gather_forward_for_" + name)
            forwarded.update(zip(group, zip(*handles)))
        i = order.index(name)
        src, land = _gather_wait_forwarded(g_send[i], g_recv[i], *forwarded[name], after, "gather_wait_" + name)
        land = _own_slot(src, land, me, True)
        if name == "ssd_conv_w":
            return jnp.transpose(land, (1, 0, 2)).reshape(SSD_CONV, XBC_DIM)
        if name == "ffn_conv_w":
            return jnp.transpose(land, (1, 0, 2)).reshape(FFN_CONV, 2 * D_FF)
        return land.reshape(-1, D_MODEL)

    res, pending = {}, {}

    def update(name, after):
        if name == "w_in":
            parts = _own_slot(*_chip_wait(*pending[name], after, "grad_wait_" + name), me // 2, False)
        else:
            parts = _own_slot(*_exchange_wait(*pending[name], after, False, "grad_wait_" + name), me, False)
        view, back = {
            "w_in": (lambda a: jnp.transpose(a, (2, 0, 1)), lambda r: jnp.transpose(r, (1, 2, 0))),
            "w_up": (lambda a: a[0].T, lambda r: r.T[None]),
        }.get(name, (lambda a: a[0], lambda r: r[None]))
        done = _adamw_sharded(parts, view(w[name]), view(m[name]), view(v[name]), "adamw_" + name)
        res[name] = [back(r) for r in done]
        return done[0]

    def pair_sums(grad):
        by_core, landed = _pair_exchange(grad.reshape(N_CHIPS, 2, -1, D_MODEL),
                                         lambda started: update("w_up", update("w_down", started)), "grad_pair_w_in")
        leaving, pending["w_in"] = _chip_start(_pair_add(by_core, landed, "grad_pair_add_w_in"), "grad_chips_w_in_start")
        return leaving

    small = {k: w[k][0] if k != "final_norm_w" else w[k] for k in SMALL}
    loss, dx, g = _local_step(x[0], loss_target[0], g_src[0], weight, small, pair_sums, pending)

    packed = _pack_row([loss] + [g[k] for k in SMALL])
    s_send, s_recv, s_src, s_land = _gather_start([packed], "small_grads_start", False)
    after = s_src[0]
    for name in ("w_out", "w_attn_o", "w_ssd_o", "w_in"):
        after = update(name, after)

    rows = _own_slot(*_exchange_wait(s_send[0], s_recv[0], s_src[0], s_land[0], after, True, "small_grads_wait"),
                     me, True)
    flat = lambda a: a.reshape(-1, a.shape[-1])
    loss_sum, updates = _small_update(
        rows, me.reshape(1), [g[k].size for k in SMALL],
        [flat(w[k]) for k in SMALL], [flat(m[k]) for k in SMALL], [flat(v[k]) for k in SMALL])
    for k, upd in zip(SMALL, updates):
        res[k] = [u.reshape(w[k].shape) for u in upd]

    grad_x = dx[None]
    outs = [loss_sum.reshape(()), grad_x]
    for i in range(4):
        outs.extend(res[k][i] for k in WEIGHT_ORDER)
    return tuple(outs)
```
